```python
import math
import jax, jax.numpy as jnp
from jax import lax
import numpy as np

D_MODEL = 1024
BATCH = 16
SEQ = 2048
DEPTH = 2

EPS = 1e-6
POOL_WINDOWS = (2, 4, 8, 16)
POOL_GROUPS = 4
POOL_DH = D_MODEL // 8
POOL_WIDTH = POOL_GROUPS * POOL_DH
CONV_WIDTH = D_MODEL // 2
CONV_K = 3
AB_IN = POOL_WIDTH + 3 * CONV_WIDTH
AB_OUT = POOL_WIDTH + CONV_WIDTH
SGU_CHUNK = 128
SGU_GROUPS = 4
SGU_DH = D_MODEL // 8
SGU_WIDTH = SGU_GROUPS * SGU_DH
SB_HEADS = 8
SB_DH = 64
SB_WIDTH = SB_HEADS * SB_DH
SB_BLOCK = 128
CD_IN = 2 * SGU_WIDTH + 3 * SB_WIDTH
CD_OUT = SGU_WIDTH + SB_WIDTH
D_FF = 4 * D_MODEL
N_AB = (DEPTH + 1) // 2
N_CD = DEPTH // 2

kernel_name = 'hybrid_pool_conv_sgu_stickbreak_trunk'


def rmsnorm(x, g):
    xf = x.astype(jnp.float32)
    y = xf * lax.rsqrt(jnp.mean(xf * xf, axis=-1, keepdims=True) + EPS)
    return (y * g.astype(jnp.float32)).astype(x.dtype)


def layernorm(x, g, b):
    xf = x.astype(jnp.float32)
    mu = jnp.mean(xf, axis=-1, keepdims=True)
    xc = xf - mu
    y = xc * lax.rsqrt(jnp.mean(xc * xc, axis=-1, keepdims=True) + EPS)
    return (y * g.astype(jnp.float32) + b.astype(jnp.float32)).astype(x.dtype)


def pool_mixer(a, w, scale):
    T = a.shape[1]
    af = a.astype(jnp.float32)
    cs = jnp.pad(jnp.cumsum(af, axis=1), ((0, 0), (1, 0), (0, 0), (0, 0)))
    win = jnp.array(POOL_WINDOWS, dtype=jnp.int32)
    pos = jnp.arange(T, dtype=jnp.int32)
    start = jnp.maximum(pos[:, None] + 1 - win[None, :], 0)
    count = jnp.minimum(pos[:, None] + 1, win[None, :]).astype(jnp.float32)
    grp = jnp.arange(POOL_GROUPS, dtype=jnp.int32)
    window_sum = cs[:, 1:] - cs[:, start, grp[None, :], :]
    pooled = (window_sum / count[None, :, :, None] - af).astype(a.dtype)
    mixed = jnp.einsum('btgc,gcd->btgd', pooled, w)
    return mixed * scale


def short_conv(h, w, b):
    T = h.shape[1]
    hp = jnp.pad(h, ((0, 0), (CONV_K - 1, 0), (0, 0)))
    y = hp[:, 0:T] * w[0]
    for k in range(1, CONV_K):
        y = y + hp[:, k:k + T] * w[k]
    return y + b


def spatial_gating(u, v, g, beta, w_s, b_s):
    B, T, _ = v.shape
    v = layernorm(v, g, beta)
    vc = v.reshape(B, T // SGU_CHUNK, SGU_CHUNK, SGU_GROUPS, SGU_DH)
    causal = jnp.tril(jnp.ones((SGU_CHUNK, SGU_CHUNK), dtype=bool))
    w = jnp.where(causal[None], w_s, 0)
    s = jnp.einsum('gts,bnsgc->bntgc', w, vc) + b_s.T[:, :, None]
    return u * s.reshape(B, T, SGU_WIDTH)


def stick_breaking_attention(q, k, v):
    B, T, H, dh = q.shape
    q = q.transpose(0, 2, 1, 3)
    k = k.transpose(0, 2, 1, 3)
    v = v.transpose(0, 2, 1, 3)
    scale = 1.0 / math.sqrt(dh)
    outs = []
    for i in range(T // SB_BLOCK):
        q0 = i * SB_BLOCK
        kend = q0 + SB_BLOCK
        qb = q[:, :, q0:kend]
        kb = k[:, :, :kend]
        vb = v[:, :, :kend]
        z = jnp.einsum('bhqd,bhkd->bhqk', qb, kb,
                       preferred_element_type=jnp.float32) * scale
        qpos = q0 + jnp.arange(SB_BLOCK, dtype=jnp.int32)
        kpos = jnp.arange(kend, dtype=jnp.int32)
        mask = kpos[None, :] < qpos[:, None]
        log_keep = jnp.where(mask, jax.nn.log_sigmoid(-z), 0.0)
        suffix = lax.cumsum(log_keep, axis=3, reverse=True) - log_keep
        weights = jnp.where(mask, jnp.exp(jax.nn.log_sigmoid(z) + suffix), 0.0)
        outs.append(jnp.einsum('bhqk,bhkd->bhqd', weights.astype(vb.dtype), vb))
    o = jnp.concatenate(outs, axis=2)
    return o.transpose(0, 2, 1, 3).reshape(B, T, H * dh)


def _fwd_setup_inputs(seed: int = 0) -> dict:
    key = jax.random.key(seed)
    ks = jax.random.split(key, 20)
    nrm = jax.random.normal
    f32 = jnp.float32
    res_scale = 1.0 / math.sqrt(2 * DEPTH)
    return {
        'x': nrm(ks[0], (BATCH, SEQ, D_MODEL), f32),
        'mix_norm_g': 1.0 + 0.01 * nrm(ks[1], (DEPTH, D_MODEL), f32),
        'mlp_norm_g': 1.0 + 0.01 * nrm(ks[2], (DEPTH, D_MODEL), f32),
        'ab_w_in': nrm(ks[3], (N_AB, D_MODEL, AB_IN), f32) * D_MODEL ** -0.5,
        'pool_w': nrm(ks[4], (N_AB, POOL_GROUPS, POOL_DH, POOL_DH), f32) * POOL_DH ** -0.5,
        'pool_scale': 1.0 + 0.02 * nrm(ks[5], (N_AB, POOL_GROUPS, POOL_DH), f32),
        'conv_w': nrm(ks[6], (N_AB, CONV_K, CONV_WIDTH), f32) * CONV_K ** -0.5,
        'conv_b': 0.01 * nrm(ks[7], (N_AB, CONV_WIDTH), f32),
        'ab_w_out': nrm(ks[8], (N_AB, AB_OUT, D_MODEL), f32) * AB_OUT ** -0.5 * res_scale,
        'cd_w_in': nrm(ks[9], (N_CD, D_MODEL, CD_IN), f32) * D_MODEL ** -0.5,
        'sgu_norm_g': 1.0 + 0.01 * nrm(ks[10], (N_CD, SGU_WIDTH), f32),
        'sgu_norm_b': 0.01 * nrm(ks[11], (N_CD, SGU_WIDTH), f32),
        'sgu_w': nrm(ks[12], (N_CD, SGU_GROUPS, SGU_CHUNK, SGU_CHUNK), f32) * SGU_CHUNK ** -0.5,
        'sgu_b': 1.0 + 0.01 * nrm(ks[13], (N_CD, SGU_GROUPS, SGU_CHUNK), f32),
        'cd_w_out': nrm(ks[14], (N_CD, CD_OUT, D_MODEL), f32) * CD_OUT ** -0.5 * res_scale,
        'mlp_w1': nrm(ks[15], (DEPTH, D_MODEL, D_FF), f32) * D_MODEL ** -0.5,
        'mlp_w2': nrm(ks[16], (DEPTH, D_FF, D_MODEL), f32) * D_FF ** -0.5 * res_scale,
        'final_norm_g': 1.0 + 0.01 * nrm(ks[17], (D_MODEL,), f32),
    }


def _fwd_reference(x, mix_norm_g, mlp_norm_g, ab_w_in, pool_w, pool_scale, conv_w, conv_b,
              ab_w_out, cd_w_in, sgu_norm_g, sgu_norm_b, sgu_w, sgu_b, cd_w_out,
              mlp_w1, mlp_w2, final_norm_g):
    B, T, _ = x.shape
    h = x
    for layer in range(DEPTH):
        xn = rmsnorm(h, mix_norm_g[layer])
        if layer % 2 == 0:
            i = layer // 2
            p = xn @ ab_w_in[i]
            a, xb, gate_b, gate_c = jnp.split(
                p, [POOL_WIDTH, POOL_WIDTH + CONV_WIDTH, POOL_WIDTH + 2 * CONV_WIDTH], axis=-1)
            a_out = pool_mixer(a.reshape(B, T, POOL_GROUPS, POOL_DH),
                               pool_w[i], pool_scale[i]).reshape(B, T, POOL_WIDTH)
            b_out = gate_b * short_conv(gate_c * xb, conv_w[i], conv_b[i])
            mix = jnp.concatenate([a_out, b_out], axis=-1) @ ab_w_out[i]
        else:
            i = layer // 2
            p = xn @ cd_w_in[i]
            uv = jax.nn.gelu(p[..., :2 * SGU_WIDTH], approximate=False)
            u, v = jnp.split(uv, 2, axis=-1)
            c_out = spatial_gating(u, v, sgu_norm_g[i], sgu_norm_b[i], sgu_w[i], sgu_b[i])
            qkv = p[..., 2 * SGU_WIDTH:].reshape(B, T, 3, SB_HEADS, SB_DH)
            d_out = stick_breaking_attention(qkv[:, :, 0], qkv[:, :, 1], qkv[:, :, 2])
            mix = jnp.concatenate([c_out, d_out], axis=-1) @ cd_w_out[i]
        h = h + mix
        hn = rmsnorm(h, mlp_norm_g[layer])
        h = h + jnp.square(jax.nn.relu(hn @ mlp_w1[layer])) @ mlp_w2[layer]
    return rmsnorm(h, final_norm_g)


import jax as _jax
import jax.numpy as _jnp

TWIN_FORMAT = 'train_step'
FWD_PARAMS = ['x', 'mix_norm_g', 'mlp_norm_g', 'ab_w_in', 'pool_w', 'pool_scale', 'conv_w', 'conv_b', 'ab_w_out', 'cd_w_in', 'sgu_norm_g', 'sgu_norm_b', 'sgu_w', 'sgu_b', 'cd_w_out', 'mlp_w1', 'mlp_w2', 'final_norm_g']
TWIN_WEIGHTS = ['mix_norm_g', 'mlp_norm_g', 'ab_w_in', 'pool_w', 'pool_scale', 'conv_w', 'conv_b', 'ab_w_out', 'cd_w_in', 'sgu_norm_g', 'sgu_norm_b', 'sgu_w', 'sgu_b', 'cd_w_out', 'mlp_w1', 'mlp_w2', 'final_norm_g']
TWIN_DIFF_INPUT = 'x'
TWIN_INPUTS = ['x', 'mix_norm_g', 'mlp_norm_g', 'ab_w_in', 'pool_w', 'pool_scale', 'conv_w', 'conv_b', 'ab_w_out', 'cd_w_in', 'sgu_norm_g', 'sgu_norm_b', 'sgu_w', 'sgu_b', 'cd_w_out', 'mlp_w1', 'mlp_w2', 'final_norm_g', 'loss_target', 'm_mix_norm_g', 'm_mlp_norm_g', 'm_ab_w_in', 'm_pool_w', 'm_pool_scale', 'm_conv_w', 'm_conv_b', 'm_ab_w_out', 'm_cd_w_in', 'm_sgu_norm_g', 'm_sgu_norm_b', 'm_sgu_w', 'm_sgu_b', 'm_cd_w_out', 'm_mlp_w1', 'm_mlp_w2', 'm_final_norm_g', 'v_mix_norm_g', 'v_mlp_norm_g', 'v_ab_w_in', 'v_pool_w', 'v_pool_scale', 'v_conv_w', 'v_conv_b', 'v_ab_w_out', 'v_cd_w_in', 'v_sgu_norm_g', 'v_sgu_norm_b', 'v_sgu_w', 'v_sgu_b', 'v_cd_w_out', 'v_mlp_w1', 'v_mlp_w2', 'v_final_norm_g']
TWIN_OUTPUTS = ['loss', 'grad_x', 'grad_mix_norm_g', 'grad_mlp_norm_g', 'grad_ab_w_in', 'grad_pool_w', 'grad_pool_scale', 'grad_conv_w', 'grad_conv_b', 'grad_ab_w_out', 'grad_cd_w_in', 'grad_sgu_norm_g', 'grad_sgu_norm_b', 'grad_sgu_w', 'grad_sgu_b', 'grad_cd_w_out', 'grad_mlp_w1', 'grad_mlp_w2', 'grad_final_norm_g', 'delta_mix_norm_g', 'delta_mlp_norm_g', 'delta_ab_w_in', 'delta_pool_w', 'delta_pool_scale', 'delta_conv_w', 'delta_conv_b', 'delta_ab_w_out', 'delta_cd_w_in', 'delta_sgu_norm_g', 'delta_sgu_norm_b', 'delta_sgu_w', 'delta_sgu_b', 'delta_cd_w_out', 'delta_mlp_w1', 'delta_mlp_w2', 'delta_final_norm_g', 'new_m_mix_norm_g', 'new_m_mlp_norm_g', 'new_m_ab_w_in', 'new_m_pool_w', 'new_m_pool_scale', 'new_m_conv_w', 'new_m_conv_b', 'new_m_ab_w_out', 'new_m_cd_w_in', 'new_m_sgu_norm_g', 'new_m_sgu_norm_b', 'new_m_sgu_w', 'new_m_sgu_b', 'new_m_cd_w_out', 'new_m_mlp_w1', 'new_m_mlp_w2', 'new_m_final_norm_g', 'new_v_mix_norm_g', 'new_v_mlp_norm_g', 'new_v_ab_w_in', 'new_v_pool_w', 'new_v_pool_scale', 'new_v_conv_w', 'new_v_conv_b', 'new_v_ab_w_out', 'new_v_cd_w_in', 'new_v_sgu_norm_g', 'new_v_sgu_norm_b', 'new_v_sgu_w', 'new_v_sgu_b', 'new_v_cd_w_out', 'new_v_mlp_w1', 'new_v_mlp_w2', 'new_v_final_norm_g']
TWIN_LEAF_KINDS = {'loss': 'loss', 'grad_x': 'grad_x', 'grad_mix_norm_g': 'grad_w', 'grad_mlp_norm_g': 'grad_w', 'grad_ab_w_in': 'grad_w', 'grad_pool_w': 'grad_w', 'grad_pool_scale': 'grad_w', 'grad_conv_w': 'grad_w', 'grad_conv_b': 'grad_w', 'grad_ab_w_out': 'grad_w', 'grad_cd_w_in': 'grad_w', 'grad_sgu_norm_g': 'grad_w', 'grad_sgu_norm_b': 'grad_w', 'grad_sgu_w': 'grad_w', 'grad_sgu_b': 'grad_w', 'grad_cd_w_out': 'grad_w', 'grad_mlp_w1': 'grad_w', 'grad_mlp_w2': 'grad_w', 'grad_final_norm_g': 'grad_w', 'delta_mix_norm_g': 'delta_w', 'delta_mlp_norm_g': 'delta_w', 'delta_ab_w_in': 'delta_w', 'delta_pool_w': 'delta_w', 'delta_pool_scale': 'delta_w', 'delta_conv_w': 'delta_w', 'delta_conv_b': 'delta_w', 'delta_ab_w_out': 'delta_w', 'delta_cd_w_in': 'delta_w', 'delta_sgu_norm_g': 'delta_w', 'delta_sgu_norm_b': 'delta_w', 'delta_sgu_w': 'delta_w', 'delta_sgu_b': 'delta_w', 'delta_cd_w_out': 'delta_w', 'delta_mlp_w1': 'delta_w', 'delta_mlp_w2': 'delta_w', 'delta_final_norm_g': 'delta_w', 'new_m_mix_norm_g': 'new_m', 'new_m_mlp_norm_g': 'new_m', 'new_m_ab_w_in': 'new_m', 'new_m_pool_w': 'new_m', 'new_m_pool_scale': 'new_m', 'new_m_conv_w': 'new_m', 'new_m_conv_b': 'new_m', 'new_m_ab_w_out': 'new_m', 'new_m_cd_w_in': 'new_m', 'new_m_sgu_norm_g': 'new_m', 'new_m_sgu_norm_b': 'new_m', 'new_m_sgu_w': 'new_m', 'new_m_sgu_b': 'new_m', 'new_m_cd_w_out': 'new_m', 'new_m_mlp_w1': 'new_m', 'new_m_mlp_w2': 'new_m', 'new_m_final_norm_g': 'new_m', 'new_v_mix_norm_g': 'new_v', 'new_v_mlp_norm_g': 'new_v', 'new_v_ab_w_in': 'new_v', 'new_v_pool_w': 'new_v', 'new_v_pool_scale': 'new_v', 'new_v_conv_w': 'new_v', 'new_v_conv_b': 'new_v', 'new_v_ab_w_out': 'new_v', 'new_v_cd_w_in': 'new_v', 'new_v_sgu_norm_g': 'new_v', 'new_v_sgu_norm_b': 'new_v', 'new_v_sgu_w': 'new_v', 'new_v_sgu_b': 'new_v', 'new_v_cd_w_out': 'new_v', 'new_v_mlp_w1': 'new_v', 'new_v_mlp_w2': 'new_v', 'new_v_final_norm_g': 'new_v'}


def _forward(args):
    return _fwd_reference(*[args[k] for k in FWD_PARAMS])


def _output_shape():
    out = _jax.eval_shape(lambda: _forward(_fwd_setup_inputs(0)))
    return out.shape, out.dtype

N_MICROBATCH = 1
ADAM_LR = 0.001
ADAM_B1 = 0.9
ADAM_B2 = 0.999
ADAM_EPS = 1e-08
ADAM_WD = 0.01
ADAM_STEP = 10
PER_EXAMPLE_BATCH_AXIS = {'x': 0, 'loss_target': 0}
SHARED_INPUTS = []
_WEIGHT_DTYPES = {'mix_norm_g': _jnp.float32, 'mlp_norm_g': _jnp.float32, 'ab_w_in': _jnp.float32, 'pool_w': _jnp.float32, 'pool_scale': _jnp.float32, 'conv_w': _jnp.float32, 'conv_b': _jnp.float32, 'ab_w_out': _jnp.float32, 'cd_w_in': _jnp.float32, 'sgu_norm_g': _jnp.float32, 'sgu_norm_b': _jnp.float32, 'sgu_w': _jnp.float32, 'sgu_b': _jnp.float32, 'cd_w_out': _jnp.float32, 'mlp_w1': _jnp.float32, 'mlp_w2': _jnp.float32, 'final_norm_g': _jnp.float32}
MOMENT_SCALE = {'mix_norm_g': 9.650477e-02, 'mlp_norm_g': 9.897052e-02, 'ab_w_in': 8.509606e-02, 'pool_w': 7.724093e-02, 'pool_scale': 7.831200e-02, 'conv_w': 8.876470e-02, 'conv_b': 8.987864e-02, 'ab_w_out': 1.646309e-01, 'cd_w_in': 3.865198e-02, 'sgu_norm_g': 3.159107e-02, 'sgu_norm_b': 3.161694e-02, 'sgu_w': 3.176317e-02, 'sgu_b': 4.583648e-02, 'cd_w_out': 1.023255e-01, 'mlp_w1': 4.780133e-02, 'mlp_w2': 1.684281e-01, 'final_norm_g': 3.214066e+01}


def _to_microbatches(a, axis):
    t = _jnp.moveaxis(a, axis, 0)
    t = t.reshape((N_MICROBATCH, t.shape[0] // N_MICROBATCH) + t.shape[1:])
    return _jnp.moveaxis(t, 1, axis + 1)


def setup_inputs(seed: int = 0) -> dict:
    inp = _fwd_setup_inputs(seed)
    key = _jax.random.fold_in(_jax.random.key(seed), 7919)
    shape, _ = _output_shape()
    out = dict(inp)
    out["loss_target"] = _jax.random.normal(_jax.random.fold_in(key, 0), shape, _jnp.float32)
    for i, name in enumerate(TWIN_WEIGHTS):
        w = inp[name].astype(_jnp.float32)
        if MOMENT_SCALE is None:
            s = _jnp.sqrt(_jnp.mean(_jnp.square(w)) + 1e-30)
        else:
            s = MOMENT_SCALE[name]
        km, kv = _jax.random.split(_jax.random.fold_in(key, i + 1))
        out[name] = w
        out["m_" + name] = s * _jax.random.normal(km, w.shape, _jnp.float32)
        out["v_" + name] = (s * s) * _jax.random.uniform(kv, w.shape, _jnp.float32, 0.5, 1.5)
    if N_MICROBATCH > 1:
        for name, axis in PER_EXAMPLE_BATCH_AXIS.items():
            out[name] = _to_microbatches(out[name], axis)
    return {'x': out['x'], 'mix_norm_g': out['mix_norm_g'], 'mlp_norm_g': out['mlp_norm_g'], 'ab_w_in': out['ab_w_in'], 'pool_w': out['pool_w'], 'pool_scale': out['pool_scale'], 'conv_w': out['conv_w'], 'conv_b': out['conv_b'], 'ab_w_out': out['ab_w_out'], 'cd_w_in': out['cd_w_in'], 'sgu_norm_g': out['sgu_norm_g'], 'sgu_norm_b': out['sgu_norm_b'], 'sgu_w': out['sgu_w'], 'sgu_b': out['sgu_b'], 'cd_w_out': out['cd_w_out'], 'mlp_w1': out['mlp_w1'], 'mlp_w2': out['mlp_w2'], 'final_norm_g': out['final_norm_g'], 'loss_target': out['loss_target'], 'm_mix_norm_g': out['m_mix_norm_g'], 'm_mlp_norm_g': out['m_mlp_norm_g'], 'm_ab_w_in': out['m_ab_w_in'], 'm_pool_w': out['m_pool_w'], 'm_pool_scale': out['m_pool_scale'], 'm_conv_w': out['m_conv_w'], 'm_conv_b': out['m_conv_b'], 'm_ab_w_out': out['m_ab_w_out'], 'm_cd_w_in': out['m_cd_w_in'], 'm_sgu_norm_g': out['m_sgu_norm_g'], 'm_sgu_norm_b': out['m_sgu_norm_b'], 'm_sgu_w': out['m_sgu_w'], 'm_sgu_b': out['m_sgu_b'], 'm_cd_w_out': out['m_cd_w_out'], 'm_mlp_w1': out['m_mlp_w1'], 'm_mlp_w2': out['m_mlp_w2'], 'm_final_norm_g': out['m_final_norm_g'], 'v_mix_norm_g': out['v_mix_norm_g'], 'v_mlp_norm_g': out['v_mlp_norm_g'], 'v_ab_w_in': out['v_ab_w_in'], 'v_pool_w': out['v_pool_w'], 'v_pool_scale': out['v_pool_scale'], 'v_conv_w': out['v_conv_w'], 'v_conv_b': out['v_conv_b'], 'v_ab_w_out': out['v_ab_w_out'], 'v_cd_w_in': out['v_cd_w_in'], 'v_sgu_norm_g': out['v_sgu_norm_g'], 'v_sgu_norm_b': out['v_sgu_norm_b'], 'v_sgu_w': out['v_sgu_w'], 'v_sgu_b': out['v_sgu_b'], 'v_cd_w_out': out['v_cd_w_out'], 'v_mlp_w1': out['v_mlp_w1'], 'v_mlp_w2': out['v_mlp_w2'], 'v_final_norm_g': out['v_final_norm_g']}


def _loss(weights, diff, rest, loss_target):
    with _jax.named_scope("forward"):
        args = {**rest, TWIN_DIFF_INPUT: diff, **{k: w.astype(_WEIGHT_DTYPES[k]) for k, w in weights.items()}}
        y = _forward(args)
    with _jax.named_scope("loss_head"):
        err = _jnp.square(y.astype(_jnp.float32) - loss_target)
        return 0.5 * _jnp.sum(_jnp.mean(err, axis=-1)) if err.ndim else 0.5 * err


def _adamw(w, g, m, v):
    m = ADAM_B1 * m + (1.0 - ADAM_B1) * g
    v = ADAM_B2 * v + (1.0 - ADAM_B2) * _jnp.square(g)
    m_hat = m / (1.0 - ADAM_B1 ** ADAM_STEP)
    v_hat = v / (1.0 - ADAM_B2 ** ADAM_STEP)
    delta = -ADAM_LR * (m_hat / (_jnp.sqrt(v_hat) + ADAM_EPS) + ADAM_WD * w)
    return delta, m, v


def reference(x, mix_norm_g, mlp_norm_g, ab_w_in, pool_w, pool_scale, conv_w, conv_b, ab_w_out, cd_w_in, sgu_norm_g, sgu_norm_b, sgu_w, sgu_b, cd_w_out, mlp_w1, mlp_w2, final_norm_g, loss_target, m_mix_norm_g, m_mlp_norm_g, m_ab_w_in, m_pool_w, m_pool_scale, m_conv_w, m_conv_b, m_ab_w_out, m_cd_w_in, m_sgu_norm_g, m_sgu_norm_b, m_sgu_w, m_sgu_b, m_cd_w_out, m_mlp_w1, m_mlp_w2, m_final_norm_g, v_mix_norm_g, v_mlp_norm_g, v_ab_w_in, v_pool_w, v_pool_scale, v_conv_w, v_conv_b, v_ab_w_out, v_cd_w_in, v_sgu_norm_g, v_sgu_norm_b, v_sgu_w, v_sgu_b, v_cd_w_out, v_mlp_w1, v_mlp_w2, v_final_norm_g):
    given = dict(x=x, mix_norm_g=mix_norm_g, mlp_norm_g=mlp_norm_g, ab_w_in=ab_w_in, pool_w=pool_w, pool_scale=pool_scale, conv_w=conv_w, conv_b=conv_b, ab_w_out=ab_w_out, cd_w_in=cd_w_in, sgu_norm_g=sgu_norm_g, sgu_norm_b=sgu_norm_b, sgu_w=sgu_w, sgu_b=sgu_b, cd_w_out=cd_w_out, mlp_w1=mlp_w1, mlp_w2=mlp_w2, final_norm_g=final_norm_g, loss_target=loss_target, m_mix_norm_g=m_mix_norm_g, m_mlp_norm_g=m_mlp_norm_g, m_ab_w_in=m_ab_w_in, m_pool_w=m_pool_w, m_pool_scale=m_pool_scale, m_conv_w=m_conv_w, m_conv_b=m_conv_b, m_ab_w_out=m_ab_w_out, m_cd_w_in=m_cd_w_in, m_sgu_norm_g=m_sgu_norm_g, m_sgu_norm_b=m_sgu_norm_b, m_sgu_w=m_sgu_w, m_sgu_b=m_sgu_b, m_cd_w_out=m_cd_w_out, m_mlp_w1=m_mlp_w1, m_mlp_w2=m_mlp_w2, m_final_norm_g=m_final_norm_g, v_mix_norm_g=v_mix_norm_g, v_mlp_norm_g=v_mlp_norm_g, v_ab_w_in=v_ab_w_in, v_pool_w=v_pool_w, v_pool_scale=v_pool_scale, v_conv_w=v_conv_w, v_conv_b=v_conv_b, v_ab_w_out=v_ab_w_out, v_cd_w_in=v_cd_w_in, v_sgu_norm_g=v_sgu_norm_g, v_sgu_norm_b=v_sgu_norm_b, v_sgu_w=v_sgu_w, v_sgu_b=v_sgu_b, v_cd_w_out=v_cd_w_out, v_mlp_w1=v_mlp_w1, v_mlp_w2=v_mlp_w2, v_final_norm_g=v_final_norm_g)
    weights = {n: given[n] for n in TWIN_WEIGHTS}
    shared = {n: given[n] for n in SHARED_INPUTS}
    per_example = {n: given[n] for n in ['x']}
    grad_fn = _jax.value_and_grad(_loss, argnums=(0, 1))

    def one_microbatch(ex, loss_target):
        ex = dict(ex)
        diff = ex.pop(TWIN_DIFF_INPUT)
        return grad_fn(weights, diff, {**shared, **ex}, loss_target)

    if N_MICROBATCH == 1:
        loss, (grad_w, grad_x) = one_microbatch(per_example, given["loss_target"])
    else:
        def body(carry, xs):
            loss_sum, grad_sum = carry
            l_k, (gw_k, gx_k) = one_microbatch(xs[0], xs[1])
            with _jax.named_scope("update"):
                return (loss_sum + l_k, _jax.tree.map(_jnp.add, grad_sum, gw_k)), gx_k

        init = (_jnp.zeros((), _jnp.float32), _jax.tree.map(_jnp.zeros_like, weights))
        (loss, grad_w), grad_x = _jax.lax.scan(body, init, (per_example, given["loss_target"]))
    with _jax.named_scope("update"):
        delta_w, new_m, new_v = {}, {}, {}
        for n in TWIN_WEIGHTS:
            delta_w[n], new_m[n], new_v[n] = _adamw(weights[n], grad_w[n], given["m_" + n], given["v_" + n])
    return (loss, grad_x, *[grad_w[n] for n in TWIN_WEIGHTS], *[delta_w[n] for n in TWIN_WEIGHTS],
            *[new_m[n] for n in TWIN_WEIGHTS], *[new_v[n] for n in TWIN_WEIGHTS])
```

```python
import math

import jax
import jax.numpy as jnp
from jax import lax
from jax.experimental import pallas as pl
from jax.experimental.pallas import tpu as pltpu

F32 = jnp.float32
BF16 = jnp.bfloat16
MESH = pl.DeviceIdType.MESH

D_MODEL = 1024
EPS = 1e-6
TILE = 128
N_CHIP = 4
N_DEV = 8
VMEM_LIMIT_BYTES = 56 * 1024 * 1024

ADAM_LR = 0.001
ADAM_B1 = 0.9
ADAM_B2 = 0.999
ADAM_EPS = 1e-08
ADAM_WD = 0.01
ADAM_STEP = 10

NT_DIMS = (((1,), (1,)), ((), ()))
TN_DIMS = (((0,), (0,)), ((), ()))


def _params(sem=None):
    return pltpu.CompilerParams(dimension_semantics=sem, vmem_limit_bytes=VMEM_LIMIT_BYTES)


def _mm_nn(a, b4, layer, *, out_dtype, name, epilogue=None, extra=None, tm=1024, tk=1024):
    m, k_dim = a.shape
    _, s_dim, kb, n = b4.shape
    assert kb == k_dim
    tn = min(n, 1024)
    assert m % tm == 0 and k_dim % tk == 0 and n % tn == 0
    nk, npb = k_dim // tk, n // tn
    grid = (m // tm, s_dim * npb, nk)

    def body(*refs):
        if extra is None:
            a_ref, b_ref, o_ref, *scr = refs
        else:
            a_ref, b_ref, e_ref, o_ref, *scr = refs

        def finish(acc):
            if epilogue == "relu2":
                r = jnp.maximum(acc, 0.0)
                acc = r * r
            elif epilogue == "residual":
                acc = acc + e_ref[...]
            o_ref[...] = acc.astype(out_dtype)

        part = jnp.dot(a_ref[...], b_ref[...], preferred_element_type=F32)
        if nk == 1:
            finish(part)
        else:
            acc_ref, = scr
            kk = pl.program_id(2)

            @pl.when(kk == 0)
            def _():
                acc_ref[...] = part

            @pl.when(kk > 0)
            def _():
                acc_ref[...] += part

            @pl.when(kk == nk - 1)
            def _():
                finish(acc_ref[...])

    in_specs = [
        pl.BlockSpec((tm, tk), lambda i, j, kk: (i, kk)),
        pl.BlockSpec((None, None, tk, tn), lambda i, j, kk: (layer, j // npb, kk, j % npb)),
    ]
    args = [a, b4]
    if extra is not None:
        in_specs.append(pl.BlockSpec((tm, tn), lambda i, j, kk: (i, j)))
        args.append(extra)
    return pl.pallas_call(
        body, name=name, grid=grid, in_specs=in_specs,
        out_specs=pl.BlockSpec((tm, tn), lambda i, j, kk: (i, j)),
        out_shape=jax.ShapeDtypeStruct((m, s_dim * n), out_dtype),
        scratch_shapes=[] if nk == 1 else [pltpu.VMEM((tm, tn), F32)],
        compiler_params=_params(("parallel", "parallel", "arbitrary")),
    )(*args)


def _mm_nt(a, b4, layer, *, out_dtype, name, epilogue=None, extra=None, tm=1024, tn=1024):
    m, k_dim = a.shape
    _, s_dim, n_out, n = b4.shape
    assert k_dim == s_dim * n
    tk = min(n, 1024)
    tn = min(tn, n_out)
    assert m % tm == 0 and n_out % tn == 0 and n % tk == 0
    kpb = n // tk
    nk = s_dim * kpb
    grid = (m // tm, n_out // tn, nk)

    def body(*refs):
        if extra is None:
            a_ref, b_ref, o_ref, *scr = refs
        else:
            a_ref, b_ref, e_ref, o_ref, *scr = refs

        def finish(acc):
            if epilogue == "relu2_bwd":
                acc = acc * (2.0 * jnp.sqrt(e_ref[...].astype(F32)))
            o_ref[...] = acc.astype(out_dtype)

        part = lax.dot_general(a_ref[...], b_ref[...], NT_DIMS, preferred_element_type=F32)
        if nk == 1:
            finish(part)
        else:
            acc_ref, = scr
            kk = pl.program_id(2)

            @pl.when(kk == 0)
            def _():
                acc_ref[...] = part

            @pl.when(kk > 0)
            def _():
                acc_ref[...] += part

            @pl.when(kk == nk - 1)
            def _():
                finish(acc_ref[...])

    in_specs = [
        pl.BlockSpec((tm, tk), lambda i, j, kk: (i, kk)),
        pl.BlockSpec((None, None, tn, tk), lambda i, j, kk: (layer, kk // kpb, j, kk % kpb)),
    ]
    args = [a, b4]
    if extra is not None:
        in_specs.append(pl.BlockSpec((tm, tn), lambda i, j, kk: (i, j)))
        args.append(extra)
    return pl.pallas_call(
        body, name=name, grid=grid, in_specs=in_specs,
        out_specs=pl.BlockSpec((tm, tn), lambda i, j, kk: (i, j)),
        out_shape=jax.ShapeDtypeStruct((m, n_out), out_dtype),
        scratch_shapes=[] if nk == 1 else [pltpu.VMEM((tm, tn), F32)],
        compiler_params=_params(("parallel", "parallel", "arbitrary")),
    )(*args)


def _mm_tn(a, b, s_dim, *, name, tm=1024, t1=1024):
    m, k1 = a.shape
    mb, n_all = b.shape
    assert mb == m and n_all % s_dim == 0
    n = n_all // s_dim
    tn = min(n, 1024)
    t1 = min(t1, k1)
    assert m % tm == 0 and k1 % t1 == 0 and n % tn == 0
    npb = n // tn
    nk = m // tm
    grid = (k1 // t1, s_dim * npb, nk)

    def body(a_ref, b_ref, o_ref, acc_ref):
        kk = pl.program_id(2)
        part = lax.dot_general(a_ref[...], b_ref[...], TN_DIMS, preferred_element_type=F32)

        @pl.when(kk == 0)
        def _():
            acc_ref[...] = part

        @pl.when(kk > 0)
        def _():
            acc_ref[...] += part

        @pl.when(kk == nk - 1)
        def _():
            o_ref[...] = acc_ref[...].astype(BF16)

    return pl.pallas_call(
        body, name=name, grid=grid,
        in_specs=[pl.BlockSpec((tm, t1), lambda i, j, kk: (kk, i)),
                  pl.BlockSpec((tm, tn), lambda i, j, kk: (kk, j))],
        out_specs=pl.BlockSpec((None, None, t1, tn), lambda i, j, kk: (0, j // npb, i, j % npb)),
        out_shape=jax.ShapeDtypeStruct((1, s_dim, k1, n), BF16),
        scratch_shapes=[pltpu.VMEM((t1, tn), F32)],
        compiler_params=_params(("parallel", "parallel", "arbitrary")),
    )(a, b)


ROW_TILE = 512


def _rms_fwd(h, g, *, name):
    m, d = h.shape

    def body(h_ref, g_ref, o_ref):
        hv = h_ref[...]
        rstd = lax.rsqrt(jnp.mean(hv * hv, axis=-1, keepdims=True) + EPS)
        o_ref[...] = (hv * rstd * g_ref[...]).astype(BF16)

    return pl.pallas_call(
        body, name=name, grid=(m // ROW_TILE,),
        in_specs=[pl.BlockSpec((ROW_TILE, d), lambda i: (i, 0)), pl.BlockSpec((1, d), lambda i: (0, 0))],
        out_specs=pl.BlockSpec((ROW_TILE, d), lambda i: (i, 0)),
        out_shape=jax.ShapeDtypeStruct((m, d), BF16),
        compiler_params=_params(("parallel",)),
    )(h, g)


def _rms_bwd(h, g, dxn, dres, *, name):
    m, d = h.shape

    def body(h_ref, g_ref, dxn_ref, dres_ref, dh_ref, dg_ref):
        hv = h_ref[...]
        rstd = lax.rsqrt(jnp.mean(hv * hv, axis=-1, keepdims=True) + EPS)
        xhat = hv * rstd
        dy = dxn_ref[...]
        dxhat = dy * g_ref[...]
        dh_ref[...] = dres_ref[...] + rstd * (dxhat - xhat * jnp.mean(dxhat * xhat, axis=-1, keepdims=True))
        part = jnp.sum(dy * xhat, axis=0, keepdims=True)

        @pl.when(pl.program_id(0) == 0)
        def _():
            dg_ref[...] = part

        @pl.when(pl.program_id(0) > 0)
        def _():
            dg_ref[...] += part

    row = pl.BlockSpec((ROW_TILE, d), lambda i: (i, 0))
    vec = pl.BlockSpec((1, d), lambda i: (0, 0))
    return pl.pallas_call(
        body, name=name, grid=(m // ROW_TILE,),
        in_specs=[row, vec, row, row], out_specs=[row, vec],
        out_shape=[jax.ShapeDtypeStruct((m, d), F32), jax.ShapeDtypeStruct((1, d), F32)],
        compiler_params=_params(("arbitrary",)),
    )(h, g, dxn, dres)


def _final_loss(h, g, target):
    m, d = h.shape

    def body(h_ref, g_ref, t_ref, dh_ref, dg_ref, loss_ref):
        hv = h_ref[...]
        gv = g_ref[...]
        rstd = lax.rsqrt(jnp.mean(hv * hv, axis=-1, keepdims=True) + EPS)
        xhat = hv * rstd
        err = xhat * gv - t_ref[...]
        dy = err * (1.0 / d)
        dxhat = dy * gv
        dh_ref[...] = rstd * (dxhat - xhat * jnp.mean(dxhat * xhat, axis=-1, keepdims=True))
        dg_part = jnp.sum(dy * xhat, axis=0, keepdims=True)
        sq = jnp.sum(jnp.sum(err * err, axis=1, keepdims=True), axis=0, keepdims=True) * (0.5 / d)
        loss_part = jnp.broadcast_to(sq, (8, TILE))

        @pl.when(pl.program_id(0) == 0)
        def _():
            dg_ref[...] = dg_part
            loss_ref[...] = loss_part

        @pl.when(pl.program_id(0) > 0)
        def _():
            dg_ref[...] += dg_part
            loss_ref[...] += loss_part

    row = pl.BlockSpec((ROW_TILE, d), lambda i: (i, 0))
    vec = pl.BlockSpec((1, d), lambda i: (0, 0))
    return pl.pallas_call(
        body, name="final_loss", grid=(m // ROW_TILE,),
        in_specs=[row, vec, row],
        out_specs=[row, vec, pl.BlockSpec((8, TILE), lambda i: (0, 0))],
        out_shape=[jax.ShapeDtypeStruct((m, d), F32), jax.ShapeDtypeStruct((1, d), F32),
                   jax.ShapeDtypeStruct((8, TILE), F32)],
        compiler_params=_params(("arbitrary",)),
    )(h, g, target)


def _shift_down(x, s, t_idx):
    return jnp.where(t_idx >= s, pltpu.roll(x, s, 0), 0.0)


def _shift_up(x, s, t_idx, t_len):
    return jnp.where(t_idx < t_len - s, pltpu.roll(x, t_len - s, 0), 0.0)


def _pool_select(group, s2, s4, s8, s16):
    return jnp.where(group == 0, s2, jnp.where(group == 1, s4, jnp.where(group == 2, s8, s16)))


def _pool_count(group, t_idx):
    win = jnp.left_shift(2, group)
    return jnp.minimum(t_idx + 1, win).astype(F32)


def _pool_fwd_math(a, group, t_idx):
    s2 = a + _shift_down(a, 1, t_idx)
    s4 = s2 + _shift_down(s2, 2, t_idx)
    s8 = s4 + _shift_down(s4, 4, t_idx)
    s16 = s8 + _shift_down(s8, 8, t_idx)
    return _pool_select(group, s2, s4, s8, s16) / _pool_count(group, t_idx) - a


def _pool_bwd_math(dpooled, group, t_idx, t_len):
    e = dpooled / _pool_count(group, t_idx)
    s2 = e + _shift_up(e, 1, t_idx, t_len)
    s4 = s2 + _shift_up(s2, 2, t_idx, t_len)
    s8 = s4 + _shift_up(s4, 4, t_idx, t_len)
    s16 = s8 + _shift_up(s8, 8, t_idx, t_len)
    return _pool_select(group, s2, s4, s8, s16) - dpooled


def _conv_fwd_math(c, w_ref, b_ref, t_idx):
    return (w_ref[0:1, :] * _shift_down(c, 2, t_idx) + w_ref[1:2, :] * _shift_down(c, 1, t_idx)
            + w_ref[2:3, :] * c + b_ref[...])


def _ab_fwd(p, pool_w, pool_scale, conv_w, conv_b, nseq, t_len):
    m = p.shape[0]
    ng = 4

    def body(a_ref, xb_ref, gb_ref, gc_ref, pw_ref, ps_ref, cw_ref, cb_ref, o_ref):
        j = pl.program_id(1)
        t_idx = lax.broadcasted_iota(jnp.int32, (t_len, TILE), 0)

        @pl.when(j < ng)
        def _():
            pooled = _pool_fwd_math(a_ref[...].astype(F32), j, t_idx)
            mixed = jnp.dot(pooled.astype(BF16), pw_ref[...].astype(BF16), preferred_element_type=F32)
            o_ref[...] = (mixed * ps_ref[...]).astype(BF16)

        @pl.when(j >= ng)
        def _():
            c = gc_ref[...].astype(F32) * xb_ref[...].astype(F32)
            y = _conv_fwd_math(c, cw_ref, cb_ref, t_idx)
            o_ref[...] = (gb_ref[...].astype(F32) * y).astype(BF16)

    def pool_j(j):
        return jnp.minimum(j, ng - 1)

    def conv_j(j):
        return jnp.maximum(j - ng, 0)

    in_specs = [
        pl.BlockSpec((t_len, TILE), lambda s, j: (s, pool_j(j))),
        pl.BlockSpec((t_len, TILE), lambda s, j: (s, ng + conv_j(j))),
        pl.BlockSpec((t_len, TILE), lambda s, j: (s, 2 * ng + conv_j(j))),
        pl.BlockSpec((t_len, TILE), lambda s, j: (s, 3 * ng + conv_j(j))),
        pl.BlockSpec((None, TILE, TILE), lambda s, j: (pool_j(j), 0, 0)),
        pl.BlockSpec((None, 1, TILE), lambda s, j: (pool_j(j), 0, 0)),
        pl.BlockSpec((3, TILE), lambda s, j: (0, conv_j(j))),
        pl.BlockSpec((1, TILE), lambda s, j: (0, conv_j(j))),
    ]
    return pl.pallas_call(
        body, name="ab_mixer_fwd", grid=(nseq, 2 * ng), in_specs=in_specs,
        out_specs=pl.BlockSpec((t_len, TILE), lambda s, j: (s, j)),
        out_shape=jax.ShapeDtypeStruct((m, 2 * ng * TILE), BF16),
        compiler_params=_params(("parallel", "arbitrary")),
    )(p, p, p, p, pool_w, pool_scale, conv_w, conv_b)


def _ab_bwd(p, dmix, pool_w, pool_scale, conv_w, conv_b, nseq, t_len):
    m = p.shape[0]
    ng = 4

    def body(a_ref, xb_ref, gb_ref, gc_ref, dma_ref, dmb_ref, pw_ref, ps_ref, cw_ref, cb_ref,
             da_ref, dxb_ref, dgb_ref, dgc_ref, dpw_ref, dps_ref, dcw_ref, dcb_ref):
        j = pl.program_id(0)
        first = pl.program_id(1) == 0
        t_idx = lax.broadcasted_iota(jnp.int32, (t_len, TILE), 0)

        pooled = _pool_fwd_math(a_ref[...].astype(F32), j, t_idx).astype(BF16)
        w_bf = pw_ref[...].astype(BF16)
        mixed = jnp.dot(pooled, w_bf, preferred_element_type=F32)
        dm = dma_ref[...].astype(F32)
        dps = jnp.sum(dm * mixed, axis=0, keepdims=True)
        dmixed = (dm * ps_ref[...]).astype(BF16)
        dpw = lax.dot_general(pooled, dmixed, TN_DIMS, preferred_element_type=F32)
        dpooled = lax.dot_general(dmixed, w_bf, NT_DIMS, preferred_element_type=F32)
        da_ref[...] = _pool_bwd_math(dpooled, j, t_idx, t_len).astype(BF16)

        xb = xb_ref[...].astype(F32)
        gb = gb_ref[...].astype(F32)
        gc = gc_ref[...].astype(F32)
        d = dmb_ref[...].astype(F32)
        c = gc * xb
        c1 = _shift_down(c, 1, t_idx)
        c2 = _shift_down(c, 2, t_idx)
        y = cw_ref[0:1, :] * c2 + cw_ref[1:2, :] * c1 + cw_ref[2:3, :] * c + cb_ref[...]
        dgb_ref[...] = (d * y).astype(BF16)
        dy = d * gb
        dc = (cw_ref[2:3, :] * dy + cw_ref[1:2, :] * _shift_up(dy, 1, t_idx, t_len)
              + cw_ref[0:1, :] * _shift_up(dy, 2, t_idx, t_len))
        dgc_ref[...] = (dc * xb).astype(BF16)
        dxb_ref[...] = (dc * gc).astype(BF16)
        dcw = jnp.concatenate([jnp.sum(dy * c2, axis=0, keepdims=True),
                               jnp.sum(dy * c1, axis=0, keepdims=True),
                               jnp.sum(dy * c, axis=0, keepdims=True)], axis=0)
        dcb = jnp.sum(dy, axis=0, keepdims=True)

        @pl.when(first)
        def _():
            dpw_ref[...] = dpw
            dps_ref[...] = dps
            dcw_ref[...] = dcw
            dcb_ref[...] = dcb

        @pl.when(jnp.logical_not(first))
        def _():
            dpw_ref[...] += dpw
            dps_ref[...] += dps
            dcw_ref[...] += dcw
            dcb_ref[...] += dcb

    def col(k):
        return pl.BlockSpec((t_len, TILE), lambda j, s: (s, k * ng + j))

    in_specs = [
        col(0), col(1), col(2), col(3), col(0), col(1),
        pl.BlockSpec((None, TILE, TILE), lambda j, s: (j, 0, 0)),
        pl.BlockSpec((None, 1, TILE), lambda j, s: (j, 0, 0)),
        pl.BlockSpec((3, TILE), lambda j, s: (0, j)),
        pl.BlockSpec((1, TILE), lambda j, s: (0, j)),
    ]
    piece = pl.BlockSpec((t_len, TILE), lambda j, s: (s, j))
    out_specs = [
        piece, piece, piece, piece,
        pl.BlockSpec((None, TILE, TILE), lambda j, s: (j, 0, 0)),
        pl.BlockSpec((None, 1, TILE), lambda j, s: (j, 0, 0)),
        pl.BlockSpec((3, TILE), lambda j, s: (0, j)),
        pl.BlockSpec((1, TILE), lambda j, s: (0, j)),
    ]
    w = ng * TILE
    out_shape = [jax.ShapeDtypeStruct((m, w), BF16)] * 4 + [
        jax.ShapeDtypeStruct((ng, TILE, TILE), F32), jax.ShapeDtypeStruct((ng, 1, TILE), F32),
        jax.ShapeDtypeStruct((3, w), F32), jax.ShapeDtypeStruct((1, w), F32)]
    return pl.pallas_call(
        body, name="ab_mixer_bwd", grid=(ng, nseq), in_specs=in_specs, out_specs=out_specs,
        out_shape=out_shape, compiler_params=_params(("parallel", "arbitrary")),
    )(p, p, p, p, dmix, dmix, pool_w, pool_scale, conv_w, conv_b)


SGU_ROWS = 512
INV_SQRT2 = 1.0 / math.sqrt(2.0)
INV_SQRT_2PI = 1.0 / math.sqrt(2.0 * math.pi)


def _gelu(x):
    return 0.5 * x * (1.0 + lax.erf(x * INV_SQRT2))


def _gelu_grad(x):
    return 0.5 * (1.0 + lax.erf(x * INV_SQRT2)) + x * (INV_SQRT_2PI * jnp.exp(-0.5 * x * x))


def _causal_tile(transposed=False):
    r = lax.broadcasted_iota(jnp.int32, (TILE, TILE), 0)
    c = lax.broadcasted_iota(jnp.int32, (TILE, TILE), 1)
    return r <= c if transposed else c <= r


def _sgu_norm(v, g_ref, b_ref):
    mu = jnp.mean(v, axis=-1, keepdims=True)
    xc = v - mu
    rstd = lax.rsqrt(jnp.mean(xc * xc, axis=-1, keepdims=True) + EPS)
    xhat = xc * rstd
    return xhat, rstd, xhat * g_ref[...] + b_ref[...]


def _sgu_fwd(p, norm_g, norm_b, w_s, bias_tile):
    m = p.shape[0]
    ng = 4
    width = ng * TILE

    def body(u_ref, v_ref, g_ref, b_ref, w_ref, bias_ref, o_ref):
        u = _gelu(u_ref[...].astype(F32))
        _, _, vln = _sgu_norm(_gelu(v_ref[...].astype(F32)), g_ref, b_ref)
        vln = vln.astype(BF16)
        causal = _causal_tile()
        for g in range(ng):
            cols = slice(g * TILE, (g + 1) * TILE)
            wg = jnp.where(causal, w_ref[g], 0.0).astype(BF16)
            for n in range(SGU_ROWS // TILE):
                rows = slice(n * TILE, (n + 1) * TILE)
                s = jnp.dot(wg, vln[rows, cols], preferred_element_type=F32) + bias_ref[g]
                o_ref[rows, cols] = (u[rows, cols] * s).astype(BF16)

    vec = pl.BlockSpec((1, width), lambda i: (0, 0))
    tiles = pl.BlockSpec((ng, TILE, TILE), lambda i: (0, 0, 0))
    return pl.pallas_call(
        body, name="sgu_fwd", grid=(m // SGU_ROWS,),
        in_specs=[pl.BlockSpec((SGU_ROWS, width), lambda i: (i, 0)),
                  pl.BlockSpec((SGU_ROWS, width), lambda i: (i, 1)), vec, vec, tiles, tiles],
        out_specs=pl.BlockSpec((SGU_ROWS, width), lambda i: (i, 0)),
        out_shape=jax.ShapeDtypeStruct((m, width), BF16),
        compiler_params=_params(("parallel",)),
    )(p, p, norm_g, norm_b, w_s, bias_tile)


def _sgu_bwd(p, dmix, norm_g, norm_b, w_s, w_s_t, bias_tile):
    m = p.shape[0]
    ng = 4
    width = ng * TILE

    def body(u_ref, v_ref, dc_ref, g_ref, b_ref, w_ref, wt_ref, bias_ref,
             du_ref, dv_ref, dw_ref, dbs_ref, dg_ref, db_ref, ds_scr, dvln_scr):
        u_pre = u_ref[...].astype(F32)
        v_pre = v_ref[...].astype(F32)
        u = _gelu(u_pre)
        xhat, rstd, vln = _sgu_norm(_gelu(v_pre), g_ref, b_ref)
        vln = vln.astype(BF16)
        dc = dc_ref[...].astype(F32)
        causal = _causal_tile()
        ones = jnp.ones((TILE, TILE), BF16)
        first = pl.program_id(0) == 0
        for g in range(ng):
            cols = slice(g * TILE, (g + 1) * TILE)
            wg = jnp.where(causal, w_ref[g], 0.0).astype(BF16)
            wgt = jnp.where(_causal_tile(transposed=True), wt_ref[g], 0.0).astype(BF16)
            dw_acc = jnp.zeros((TILE, TILE), F32)
            dbs_acc = jnp.zeros((TILE, TILE), F32)
            for n in range(SGU_ROWS // TILE):
                rows = slice(n * TILE, (n + 1) * TILE)
                vt = vln[rows, cols]
                s = jnp.dot(wg, vt, preferred_element_type=F32) + bias_ref[g]
                ds_scr[rows, cols] = dc[rows, cols] * s
                ds = (dc[rows, cols] * u[rows, cols]).astype(BF16)
                dw_acc += lax.dot_general(ds, vt, NT_DIMS, preferred_element_type=F32)
                dbs_acc += jnp.dot(ds, ones, preferred_element_type=F32)
                dvln_scr[rows, cols] = jnp.dot(wgt, ds, preferred_element_type=F32)
            dw_g = jnp.where(causal, dw_acc, 0.0)

            @pl.when(first)
            def _():
                dw_ref[g] = dw_g
                dbs_ref[g] = dbs_acc

            @pl.when(jnp.logical_not(first))
            def _():
                dw_ref[g] += dw_g
                dbs_ref[g] += dbs_acc

        du_ref[...] = (ds_scr[...] * _gelu_grad(u_pre)).astype(BF16)
        dvln = dvln_scr[...]
        dxhat = dvln * g_ref[...]
        dv = rstd * (dxhat - jnp.mean(dxhat, axis=-1, keepdims=True)
                     - xhat * jnp.mean(dxhat * xhat, axis=-1, keepdims=True))
        dv_ref[...] = (dv * _gelu_grad(v_pre)).astype(BF16)
        dg_part = jnp.sum(dvln * xhat, axis=0, keepdims=True)
        db_part = jnp.sum(dvln, axis=0, keepdims=True)

        @pl.when(first)
        def _():
            dg_ref[...] = dg_part
            db_ref[...] = db_part

        @pl.when(jnp.logical_not(first))
        def _():
            dg_ref[...] += dg_part
            db_ref[...] += db_part

    vec = pl.BlockSpec((1, width), lambda i: (0, 0))
    tiles = pl.BlockSpec((ng, TILE, TILE), lambda i: (0, 0, 0))
    rows0 = pl.BlockSpec((SGU_ROWS, width), lambda i: (i, 0))
    rows1 = pl.BlockSpec((SGU_ROWS, width), lambda i: (i, 1))
    return pl.pallas_call(
        body, name="sgu_bwd", grid=(m // SGU_ROWS,),
        in_specs=[rows0, rows1, rows0, vec, vec, tiles, tiles, tiles],
        out_specs=[rows0, rows0, tiles, tiles, vec, vec],
        out_shape=[jax.ShapeDtypeStruct((m, width), BF16), jax.ShapeDtypeStruct((m, width), BF16),
                   jax.ShapeDtypeStruct((ng, TILE, TILE), F32), jax.ShapeDtypeStruct((ng, TILE, TILE), F32),
                   jax.ShapeDtypeStruct((1, width), F32), jax.ShapeDtypeStruct((1, width), F32)],
        scratch_shapes=[pltpu.VMEM((SGU_ROWS, width), F32), pltpu.VMEM((SGU_ROWS, width), F32)],
        compiler_params=_params(("arbitrary",)),
    )(p, p, dmix, norm_g, norm_b, w_s, w_s_t, bias_tile)


SB_DH = 64
SB_SCALE = 1.0 / math.sqrt(SB_DH)


def _split_bf16(x):
    hi = x.astype(BF16)
    lo = (x - hi.astype(F32)).astype(BF16)
    return jnp.concatenate([hi, lo], axis=1)


def _sum_matrix(kind):
    j = lax.broadcasted_iota(jnp.int32, (2 * TILE, 2 * TILE), 0) % TILE
    s = lax.broadcasted_iota(jnp.int32, (2 * TILE, 2 * TILE), 1)
    tri = {"after": j > s, "upto": j <= s, "before": j < s}[kind]
    return jnp.where(jnp.logical_or(s >= TILE, tri), 1.0, 0.0).astype(BF16)


def _head_lanes(h):
    lane = lax.broadcasted_iota(jnp.int32, (1, TILE), 1)
    return (lane >= h * SB_DH) & (lane < (h + 1) * SB_DH)


def _softplus(z):
    return jnp.maximum(z, 0.0) + jnp.log(1.0 + jnp.exp(-jnp.abs(z)))


def _sb_fwd(p, nseq, t_len):
    m = p.shape[0]
    nb = t_len // TILE
    npair = 4

    def body(q_ref, k_ref, v_ref, o_ref, lt_ref, kh_ref, vh_ref):
        for h in range(2):
            keep = _head_lanes(h)
            kh_ref[h] = jnp.where(keep, k_ref[...], 0).astype(BF16)
            vh_ref[h] = jnp.where(keep, v_ref[...], 0).astype(BF16)
        summat = _sum_matrix("after")
        r = lax.broadcasted_iota(jnp.int32, (TILE, TILE), 0)
        c = lax.broadcasted_iota(jnp.int32, (TILE, TILE), 1)
        strict = c < r

        def q_block(i, _):
            r0 = pl.multiple_of(i * TILE, TILE)
            q = q_ref[pl.ds(r0, TILE), :]
            acc_out = jnp.zeros((TILE, TILE), F32)
            lt_out = jnp.zeros((TILE, TILE), F32)
            for h in range(2):
                def tile(j, carry, acc, diag):
                    c0 = pl.multiple_of(j * TILE, TILE)
                    kj = kh_ref[h, pl.ds(c0, TILE), :]
                    vj = vh_ref[h, pl.ds(c0, TILE), :]
                    z = lax.dot_general(q, kj, NT_DIMS, preferred_element_type=F32) * SB_SCALE
                    sp = _softplus(z)
                    logkeep = jnp.where(strict, -sp, 0.0) if diag else -sp
                    sums = jnp.dot(_split_bf16(logkeep), summat, preferred_element_type=F32)
                    logw = z - sp + sums[:, :TILE] + carry
                    w = jnp.exp(logw)
                    if diag:
                        w = jnp.where(strict, w, 0.0)
                    acc = acc + jnp.dot(w.astype(BF16), vj, preferred_element_type=F32)
                    return carry + sums[:, TILE:], acc

                zero = jnp.zeros((TILE, TILE), F32)
                carry, acc = tile(i, zero, zero, True)
                carry, acc = lax.fori_loop(
                    0, i, lambda jj, ca: tile(i - 1 - jj, ca[0], ca[1], False), (carry, acc))
                acc_out = acc_out + acc
                lt_out = jnp.where(_head_lanes(h), carry, lt_out)
            o_ref[pl.ds(r0, TILE), :] = acc_out.astype(BF16)
            lt_ref[pl.ds(r0, TILE), :] = lt_out
            return 0

        lax.fori_loop(0, nb, q_block, 0)

    def col(k):
        return pl.BlockSpec((t_len, TILE), lambda s, hp: (s, k * npair + hp))

    out = pl.BlockSpec((t_len, TILE), lambda s, hp: (s, hp))
    return pl.pallas_call(
        body, name="stickbreak_fwd", grid=(nseq, npair), in_specs=[col(2), col(3), col(4)],
        out_specs=[out, out],
        out_shape=[jax.ShapeDtypeStruct((m, npair * TILE), BF16), jax.ShapeDtypeStruct((m, npair * TILE), F32)],
        scratch_shapes=[pltpu.VMEM((2, t_len, TILE), BF16), pltpu.VMEM((2, t_len, TILE), BF16)],
        compiler_params=_params(("parallel", "parallel")),
    )(p, p, p)


def _sb_bwd(p, dmix, ltot, nseq, t_len):
    m = p.shape[0]
    nb = t_len // TILE
    npair = 4

    def body(q_ref, k_ref, v_ref, do_ref, lt_ref, dq_ref, dk_ref, dv_ref, kh_ref, vh_ref, dk_acc, dv_acc):
        for h in range(2):
            keep = _head_lanes(h)
            kh_ref[h] = jnp.where(keep, k_ref[...], 0).astype(BF16)
            vh_ref[h] = jnp.where(keep, v_ref[...], 0).astype(BF16)
        dk_acc[...] = jnp.zeros_like(dk_acc)
        dv_acc[...] = jnp.zeros_like(dv_acc)
        sum_upto = _sum_matrix("upto")
        sum_before = _sum_matrix("before")
        r = lax.broadcasted_iota(jnp.int32, (TILE, TILE), 0)
        c = lax.broadcasted_iota(jnp.int32, (TILE, TILE), 1)
        strict = c < r
        lane = lax.broadcasted_iota(jnp.int32, (TILE, TILE), 1)

        def q_block(i, _):
            r0 = pl.multiple_of(i * TILE, TILE)
            q = q_ref[pl.ds(r0, TILE), :]
            do = do_ref[pl.ds(r0, TILE), :]
            lt = lt_ref[pl.ds(r0, TILE), :]
            dq_out = jnp.zeros((TILE, TILE), F32)
            for h in range(2):
                keep = _head_lanes(h)
                qh = jnp.where(keep, q, 0).astype(BF16)
                doh = jnp.where(keep, do, 0).astype(BF16)
                ltot_h = jnp.sum(jnp.where(lane == h * SB_DH, lt, 0.0), axis=1, keepdims=True)

                def tile(j, sum_l, sum_g, dq, diag):
                    c0 = pl.multiple_of(j * TILE, TILE)
                    kj = kh_ref[h, pl.ds(c0, TILE), :]
                    vj = vh_ref[h, pl.ds(c0, TILE), :]
                    z = lax.dot_general(q, kj, NT_DIMS, preferred_element_type=F32) * SB_SCALE
                    sp = _softplus(z)
                    logkeep = jnp.where(strict, -sp, 0.0) if diag else -sp
                    sums_l = jnp.dot(_split_bf16(logkeep), sum_upto, preferred_element_type=F32)
                    suffix = ltot_h - sum_l - sums_l[:, :TILE]
                    w = jnp.exp(z - sp + suffix)
                    if diag:
                        w = jnp.where(strict, w, 0.0)
                    dw = lax.dot_general(do, vj, NT_DIMS, preferred_element_type=F32)
                    g = w * dw
                    sums_g = jnp.dot(_split_bf16(g), sum_before, preferred_element_type=F32)
                    sig = jnp.exp(z - sp)
                    dz = (g - sig * (g + sum_g + sums_g[:, :TILE])) * SB_SCALE
                    if diag:
                        dz = jnp.where(strict, dz, 0.0)
                    dzb = dz.astype(BF16)
                    dq = dq + jnp.dot(dzb, kj, preferred_element_type=F32)
                    dk_acc[pl.ds(c0, TILE), :] += lax.dot_general(dzb, qh, TN_DIMS, preferred_element_type=F32)
                    dv_acc[pl.ds(c0, TILE), :] += lax.dot_general(
                        w.astype(BF16), doh, TN_DIMS, preferred_element_type=F32)
                    return sum_l + sums_l[:, TILE:], sum_g + sums_g[:, TILE:], dq

                zero = jnp.zeros((TILE, TILE), F32)
                sum_l, sum_g, dq = lax.fori_loop(
                    0, i, lambda j, st: tile(j, st[0], st[1], st[2], False), (zero, zero, zero))
                _, _, dq = tile(i, sum_l, sum_g, dq, True)
                dq_out = dq_out + dq
            dq_ref[pl.ds(r0, TILE), :] = dq_out.astype(BF16)
            return 0

        lax.fori_loop(0, nb, q_block, 0)
        dk_ref[...] = dk_acc[...].astype(BF16)
        dv_ref[...] = dv_acc[...].astype(BF16)

    def col(k):
        return pl.BlockSpec((t_len, TILE), lambda s, hp: (s, k * npair + hp))

    out = pl.BlockSpec((t_len, TILE), lambda s, hp: (s, hp))
    width = npair * TILE
    return pl.pallas_call(
        body, name="stickbreak_bwd", grid=(nseq, npair),
        in_specs=[col(2), col(3), col(4), col(1), out], out_specs=[out, out, out],
        out_shape=[jax.ShapeDtypeStruct((m, width), BF16)] * 3,
        scratch_shapes=[pltpu.VMEM((2, t_len, TILE), BF16), pltpu.VMEM((2, t_len, TILE), BF16),
                        pltpu.VMEM((t_len, TILE), F32), pltpu.VMEM((t_len, TILE), F32)],
        compiler_params=_params(("parallel", "parallel")),
    )(p, p, p, dmix, ltot)


def _adam_math(w, g, m, v):
    m = ADAM_B1 * m + (1.0 - ADAM_B1) * g
    v = ADAM_B2 * v + (1.0 - ADAM_B2) * (g * g)
    m_hat = m / (1.0 - ADAM_B1 ** ADAM_STEP)
    v_hat = v / (1.0 - ADAM_B2 ** ADAM_STEP)
    delta = -ADAM_LR * (m_hat / (jnp.sqrt(v_hat) + ADAM_EPS) + ADAM_WD * w)
    return delta, m, v


def _pair_sum(a, b, *, name):
    l_dim, s_dim, h, c = a.shape
    th = min(h, 512)

    def body(a_ref, b_ref, o_ref):
        o_ref[...] = (a_ref[...].astype(F32) + b_ref[...].astype(F32)).astype(BF16)

    spec = pl.BlockSpec((None, None, th, c), lambda l, s, i: (l, s, i, 0))
    return pl.pallas_call(
        body, name=name, grid=(l_dim, s_dim, h // th), in_specs=[spec, spec], out_specs=spec,
        out_shape=jax.ShapeDtypeStruct(a.shape, BF16), compiler_params=_params(("parallel",) * 3),
    )(a, b)


def _chip_sum(r2, *, name):
    l_dim, s_dim, h, c = r2.shape
    th = min(h, 512)

    def body(r0, r1, r2_, r3, o_ref):
        o_ref[...] = ((r0[...].astype(F32) + r1[...].astype(F32)) + r2_[...].astype(F32)) + r3[...].astype(F32)

    def piece(s):
        return pl.BlockSpec((None, None, th, c), lambda l, i: (l, s, i, 0))

    return pl.pallas_call(
        body, name=name, grid=(l_dim, h // th), in_specs=[piece(s) for s in range(s_dim)],
        out_specs=pl.BlockSpec((None, th, c), lambda l, i: (l, i, 0)),
        out_shape=jax.ShapeDtypeStruct((l_dim, h, c), F32), compiler_params=_params(("parallel",) * 2),
    )(r2, r2, r2, r2)


def _adam_big(w, m, v, grads, *, name):
    l_dim, r, c = w.shape
    assert len(grads) == l_dim
    tr = min(r, 256)

    def body(*refs):
        w_ref, m_ref, v_ref = refs[:3]
        g_refs = refs[3:3 + l_dim]
        go_ref, d_ref, mo_ref, vo_ref = refs[3 + l_dim:]
        g = g_refs[0][...]
        for l in range(1, l_dim):
            g = jnp.where(pl.program_id(0) == l, g_refs[l][...], g)
        delta, m_new, v_new = _adam_math(w_ref[...], g, m_ref[...], v_ref[...])
        go_ref[...] = g
        d_ref[...] = delta
        mo_ref[...] = m_new
        vo_ref[...] = v_new

    spec = pl.BlockSpec((None, tr, c), lambda l, i: (l, i, 0))
    gspec = pl.BlockSpec((None, tr, c), lambda l, i: (0, i, 0))
    return pl.pallas_call(
        body, name=name, grid=(l_dim, r // tr), in_specs=[spec] * 3 + [gspec] * l_dim, out_specs=[spec] * 4,
        out_shape=[jax.ShapeDtypeStruct(w.shape, F32)] * 4, compiler_params=_params(("parallel",) * 2),
    )(w, m, v, *grads)


def _position():
    return lax.axis_index("x"), lax.axis_index("y"), lax.axis_index("c")


def _other_chips(x, y):
    return [(1 - x, y), (x, 1 - y), (1 - x, 1 - y)]


def _remote(src, dst, send_sem, recv_sem, device):
    return pltpu.make_async_remote_copy(src_ref=src, dst_ref=dst, send_sem=send_sem, recv_sem=recv_sem,
                                        device_id=device, device_id_type=MESH)


ANY = pl.BlockSpec(memory_space=pl.ANY)


def _gather_weights(shards):
    n = len(shards)

    def body(*refs):
        ins, outs = refs[:n], refs[n:2 * n]
        send_sems, recv_sems, fwd_send, fwd_recv, local_sems = refs[2 * n:]
        x, y, c = _position()
        q = 2 * x + y
        chips = _other_chips(x, y)
        sibling = (x, y, 1 - c)

        def half(a, chip, core):
            h = shards[a].shape[1] // 2
            return outs[a].at[:, 2 * chip[0] + chip[1], pl.ds(core * h, h), :]

        local = [pltpu.make_async_copy(ins[a], outs[a].at[:, q], local_sems.at[a]) for a in range(n)]
        for cp in local:
            cp.start()
        sends = []
        for a in range(n):
            h = shards[a].shape[1] // 2
            for k, chip in enumerate(chips):
                cp = _remote(ins[a].at[:, pl.ds(c * h, h), :], half(a, (x, y), c),
                             send_sems.at[3 * a + k], recv_sems.at[3 * a + k], (*chip, c))
                cp.start()
                sends.append(cp)
        for k, chip in enumerate(chips):
            for a in range(n):
                landed = half(a, chip, c)
                _remote(landed, landed, send_sems.at[3 * a + k], recv_sems.at[3 * a + k], (*chip, c)).wait_recv()
                cp = _remote(landed, landed, fwd_send.at[3 * a + k], fwd_recv.at[3 * a + k], sibling)
                cp.start()
                sends.append(cp)
        for k, chip in enumerate(chips):
            for a in range(n):
                got = half(a, chip, 1 - c)
                _remote(got, got, fwd_send.at[3 * a + k], fwd_recv.at[3 * a + k], sibling).wait_recv()
        for cp in sends:
            cp.wait_send()
        for cp in local:
            cp.wait()

    sem = pltpu.SemaphoreType.DMA((3 * n,))
    return pl.pallas_call(
        body, name="gather_weights", in_specs=[ANY] * n, out_specs=[ANY] * n,
        out_shape=[jax.ShapeDtypeStruct((s.shape[0], N_CHIP) + s.shape[1:], s.dtype) for s in shards],
        scratch_shapes=[sem, sem, sem, sem, pltpu.SemaphoreType.DMA((n,))],
        compiler_params=pltpu.CompilerParams(has_side_effects=True),
    )(*shards)


def _swap_halves(grads):
    n = len(grads)

    def body(*refs):
        ins, own, got = refs[:n], refs[n:2 * n], refs[2 * n:3 * n]
        send_sems, recv_sems, local_sems = refs[3 * n:]
        x, y, c = _position()
        sibling = (x, y, 1 - c)
        copies = []
        for a in range(n):
            h = grads[a].shape[2] // 2
            mine = pltpu.make_async_copy(ins[a].at[:, :, pl.ds(c * h, h), :], own[a], local_sems.at[a])
            mine.start()
            cp = _remote(ins[a].at[:, :, pl.ds((1 - c) * h, h), :], got[a], send_sems.at[a], recv_sems.at[a], sibling)
            cp.start()
            copies.append((mine, cp))
        for mine, cp in copies:
            cp.wait()
            mine.wait()

    halves = [jax.ShapeDtypeStruct(g.shape[:2] + (g.shape[2] // 2, g.shape[3]), g.dtype) for g in grads]
    sem = pltpu.SemaphoreType.DMA((n,))
    res = pl.pallas_call(
        body, name="swap_halves", in_specs=[ANY] * n, out_specs=[ANY] * (2 * n), out_shape=halves + halves,
        scratch_shapes=[sem, sem, sem], compiler_params=pltpu.CompilerParams(has_side_effects=True),
    )(*grads)
    return res[:n], res[n:]


def _exchange_chips(sums):
    n = len(sums)

    def body(*refs):
        ins, outs = refs[:n], refs[n:2 * n]
        send_sems, recv_sems, local_sems = refs[2 * n:]
        x, y, c = _position()
        q = 2 * x + y
        chips = _other_chips(x, y)
        local, sends = [], []
        for a in range(n):
            mine = pltpu.make_async_copy(ins[a].at[:, q], outs[a].at[:, q], local_sems.at[a])
            mine.start()
            local.append(mine)
            for k, chip in enumerate(chips):
                cp = _remote(ins[a].at[:, 2 * chip[0] + chip[1]], outs[a].at[:, q],
                             send_sems.at[3 * a + k], recv_sems.at[3 * a + k], (*chip, c))
                cp.start()
                sends.append(cp)
        for a in range(n):
            for k, chip in enumerate(chips):
                got = outs[a].at[:, 2 * chip[0] + chip[1]]
                _remote(got, got, send_sems.at[3 * a + k], recv_sems.at[3 * a + k], (*chip, c)).wait_recv()
        for cp in sends:
            cp.wait_send()
        for cp in local:
            cp.wait()

    sem = pltpu.SemaphoreType.DMA((3 * n,))
    return pl.pallas_call(
        body, name="exchange_chips", in_specs=[ANY] * n, out_specs=[ANY] * n,
        out_shape=[jax.ShapeDtypeStruct(s.shape, s.dtype) for s in sums],
        scratch_shapes=[sem, sem, pltpu.SemaphoreType.DMA((n,))],
        compiler_params=pltpu.CompilerParams(has_side_effects=True),
    )(*sums)


def _join_halves(halves):
    n = len(halves)

    def body(*refs):
        ins, outs = refs[:n], refs[n:2 * n]
        send_sems, recv_sems, local_sems = refs[2 * n:]
        x, y, c = _position()
        sibling = (x, y, 1 - c)
        copies = []
        for a in range(n):
            h = halves[a].shape[1]
            dst = outs[a].at[:, pl.ds(c * h, h), :]
            mine = pltpu.make_async_copy(ins[a], dst, local_sems.at[a])
            mine.start()
            cp = _remote(ins[a], dst, send_sems.at[a], recv_sems.at[a], sibling)
            cp.start()
            copies.append((mine, cp, a, h))
        for mine, cp, a, h in copies:
            cp.wait_send()
            got = outs[a].at[:, pl.ds((1 - c) * h, h), :]
            _remote(got, got, send_sems.at[a], recv_sems.at[a], sibling).wait_recv()
            mine.wait()

    sem = pltpu.SemaphoreType.DMA((n,))
    return pl.pallas_call(
        body, name="join_halves", in_specs=[ANY] * n, out_specs=[ANY] * n,
        out_shape=[jax.ShapeDtypeStruct((s.shape[0], 2 * s.shape[1], s.shape[2]), s.dtype) for s in halves],
        scratch_shapes=[sem, sem, sem], compiler_params=pltpu.CompilerParams(has_side_effects=True),
    )(*halves)


def _allreduce_small(packs):
    n = len(packs)

    def body(*refs):
        ins, outs, gath = refs[:n], refs[n:2 * n], refs[2 * n:3 * n]
        send_sems, recv_sems = refs[3 * n:]
        x, y, c = _position()
        me, sibling = (x, y, c), (x, y, 1 - c)
        chips = _other_chips(x, y)

        def slot(a, dev):
            return gath[a].at[4 * dev[0] + 2 * dev[1] + dev[2]]

        def copy(a, k, block, to, src=None):
            return _remote(slot(a, block) if src is None else src, slot(a, block),
                           send_sems.at[7 * a + k], recv_sems.at[7 * a + k], to)

        started = []
        for a in range(n):
            slot(a, me)[...] = ins[a][...]
            first = [copy(a, 0, me, sibling, src=ins[a])]
            first += [copy(a, 1 + k, me, (*chip, c), src=ins[a]) for k, chip in enumerate(chips)]
            for cp in first:
                cp.start()
            started += first
        for a in range(n):
            for k, chip in enumerate(chips):
                copy(a, 1 + k, (*chip, c), me).wait_recv()
                cp = copy(a, 4 + k, (*chip, c), sibling)
                cp.start()
                started.append(cp)
        for a in range(n):
            copy(a, 0, sibling, me).wait_recv()
            for k, chip in enumerate(chips):
                copy(a, 4 + k, (*chip, 1 - c), me).wait_recv()
        for cp in started:
            cp.wait_send()
        for a in range(n):
            total = gath[a][0]
            for d in range(1, N_DEV):
                total = total + gath[a][d]
            outs[a][...] = total

    vmem = pl.BlockSpec(memory_space=pltpu.VMEM)
    sem = pltpu.SemaphoreType.DMA((7 * n,))
    return pl.pallas_call(
        body, name="allreduce_small", in_specs=[vmem] * n, out_specs=[vmem] * n,
        out_shape=[jax.ShapeDtypeStruct(p.shape, p.dtype) for p in packs],
        scratch_shapes=[pltpu.VMEM((N_DEV,) + p.shape, p.dtype) for p in packs] + [sem, sem],
        compiler_params=pltpu.CompilerParams(has_side_effects=True, vmem_limit_bytes=VMEM_LIMIT_BYTES),
    )(*packs)


LOSS_ROW = 1040


def _pad_rows(a, rows=8):
    return jnp.concatenate([a, jnp.zeros((rows - a.shape[0], a.shape[1]), a.dtype)], axis=0)

def _adam_small(wide, mid, narrow, params):
    names = ["mix_norm_g", "mlp_norm_g", "final_norm_g", "conv_b", "conv_w", "sgu_norm_g", "sgu_norm_b",
             "pool_w", "pool_scale", "sgu_w", "sgu_b"]
    n = len(names)

    def body(*refs):
        wide_ref, mid_ref, narrow_ref = refs[:3]
        wmv = refs[3:3 + 3 * n]
        outs = refs[3 + 3 * n:]
        x, y, _ = _position()
        q = 2 * x + y

        def my_quarter(rows):
            parts = [rows[:, s * TILE:(s + 1) * TILE] for s in range(N_CHIP)]
            return jnp.where(q == 0, parts[0], jnp.where(q == 1, parts[1], jnp.where(q == 2, parts[2], parts[3])))

        def tiles(first_row):
            return [((0, g), narrow_ref[first_row + g * TILE:first_row + (g + 1) * TILE, :]) for g in range(4)]

        grads = {
            "mix_norm_g": [((), wide_ref[0:2, :])],
            "mlp_norm_g": [((), wide_ref[8:10, :])],
            "final_norm_g": [((), wide_ref[16:17, :])],
            "conv_b": [((), mid_ref[0:1, :])],
            "conv_w": [((0,), my_quarter(mid_ref[8:11, :]))],
            "sgu_norm_g": [((), my_quarter(mid_ref[16:17, :]))],
            "sgu_norm_b": [((), my_quarter(mid_ref[24:25, :]))],
            "pool_w": tiles(0),
            "sgu_w": tiles(512),
            "pool_scale": [((0,), narrow_ref[1024:1028, :])],
            "sgu_b": [((0,), narrow_ref[1032:1036, :])],
        }
        for i, name in enumerate(names):
            w_ref, m_ref, v_ref = wmv[3 * i:3 * i + 3]
            for lead, g in grads[name]:
                idx = lead + (slice(None), slice(None))
                delta, m_new, v_new = _adam_math(w_ref[idx], g, m_ref[idx], v_ref[idx])
                outs[4 * i][idx] = g
                outs[4 * i + 1][idx] = delta
                outs[4 * i + 2][idx] = m_new
                outs[4 * i + 3][idx] = v_new

    vmem = pl.BlockSpec(memory_space=pltpu.VMEM)
    args, out_shape = [wide, mid, narrow], []
    for name in names:
        w, m, v = params[name]
        args += [w, m, v]
        out_shape += [jax.ShapeDtypeStruct(w.shape, F32)] * 4
    res = pl.pallas_call(
        body, name="adam_small", in_specs=[vmem] * len(args), out_specs=[vmem] * len(out_shape),
        out_shape=out_shape, compiler_params=pltpu.CompilerParams(vmem_limit_bytes=VMEM_LIMIT_BYTES),
    )(*args)
    return {name: res[4 * i:4 * i + 4] for i, name in enumerate(names)}


def _reduce_big(grads):
    own, got = _swap_halves(grads)
    sums = [_pair_sum(a, b, name=f"pair_sum_{i}") for i, (a, b) in enumerate(zip(own, got))]
    landed = _exchange_chips(sums)
    halves = [_chip_sum(r2, name=f"chip_sum_{i}") for i, r2 in enumerate(landed)]
    return _join_halves(halves)


def kernel(x, mix_norm_g, mlp_norm_g, ab_w_in, pool_w, pool_scale, conv_w, conv_b, ab_w_out, cd_w_in, sgu_norm_g, sgu_norm_b, sgu_w, sgu_b, cd_w_out, mlp_w1, mlp_w2, final_norm_g, loss_target, m_mix_norm_g, m_mlp_norm_g, m_ab_w_in, m_pool_w, m_pool_scale, m_conv_w, m_conv_b, m_ab_w_out, m_cd_w_in, m_sgu_norm_g, m_sgu_norm_b, m_sgu_w, m_sgu_b, m_cd_w_out, m_mlp_w1, m_mlp_w2, m_final_norm_g, v_mix_norm_g, v_mlp_norm_g, v_ab_w_in, v_pool_w, v_pool_scale, v_conv_w, v_conv_b, v_ab_w_out, v_cd_w_in, v_sgu_norm_g, v_sgu_norm_b, v_sgu_w, v_sgu_b, v_cd_w_out, v_mlp_w1, v_mlp_w2, v_final_norm_g):
    nseq, t_len, d = x.shape
    m_tok = nseq * t_len
    h0 = x.reshape(m_tok, d)
    target = loss_target.reshape(m_tok, d)

    w_ab_in, w_ab_out, w_cd_in, w_cd_out, w_1, w_2 = _gather_weights(
        [w.astype(BF16) for w in (ab_w_in, ab_w_out, cd_w_in, cd_w_out, mlp_w1, mlp_w2)])
    w_ab_out = w_ab_out.reshape(1, 1, -1, d)
    w_cd_out = w_cd_out.reshape(1, 1, -1, d)
    w_2 = w_2.reshape(2, 1, -1, d)

    pool_w3, pool_scale3 = pool_w[0], pool_scale[0].reshape(4, 1, TILE)
    sgu_w3 = sgu_w[0]
    sgu_w3_t = jnp.swapaxes(sgu_w3, 1, 2)
    sgu_bias_tile = jnp.broadcast_to(sgu_b[0][:, :, None], (4, TILE, TILE))
    conv_w2, conv_b2 = conv_w[0], conv_b
    x_idx, y_idx = lax.axis_index("x"), lax.axis_index("y")
    q_idx = 2 * x_idx + y_idx

    def place_quarter(v):
        return lax.dynamic_update_slice(jnp.zeros((v.shape[0], 4 * TILE), F32), v, (0, q_idx * TILE))

    sharded_small = jnp.concatenate(
        [place_quarter(conv_w[0]), place_quarter(sgu_norm_g), place_quarter(sgu_norm_b),
         jnp.zeros((3, 4 * TILE), F32)], axis=0)
    sharded_small, = _allreduce_small([sharded_small])
    sharded_small = sharded_small * 0.5
    conv_w_full = sharded_small[0:3]
    sgu_g_full = sharded_small[3:4]
    sgu_b_full = sharded_small[4:5]

    xn0 = _rms_fwd(h0, mix_norm_g[0:1], name="rms_fwd_mix0")
    p_ab = _mm_nn(xn0, w_ab_in, 0, out_dtype=BF16, name="ab_in_proj")
    mix0 = _ab_fwd(p_ab, pool_w3, pool_scale3, conv_w_full, conv_b2, nseq, t_len)
    h1 = _mm_nn(mix0, w_ab_out, 0, out_dtype=F32, name="ab_out_proj", epilogue="residual", extra=h0)
    hn0 = _rms_fwd(h1, mlp_norm_g[0:1], name="rms_fwd_mlp0")
    act0 = _mm_nn(hn0, w_1, 0, out_dtype=BF16, name="mlp0_up", epilogue="relu2")
    h2 = _mm_nn(act0, w_2, 0, out_dtype=F32, name="mlp0_down", epilogue="residual", extra=h1)

    xn1 = _rms_fwd(h2, mix_norm_g[1:2], name="rms_fwd_mix1")
    p_cd = _mm_nn(xn1, w_cd_in, 0, out_dtype=BF16, name="cd_in_proj")
    c_out = _sgu_fwd(p_cd, sgu_g_full, sgu_b_full, sgu_w3, sgu_bias_tile)
    d_out, ltot = _sb_fwd(p_cd, nseq, t_len)
    mix1 = jnp.concatenate([c_out, d_out], axis=1)
    h3 = _mm_nn(mix1, w_cd_out, 0, out_dtype=F32, name="cd_out_proj", epilogue="residual", extra=h2)
    hn1 = _rms_fwd(h3, mlp_norm_g[1:2], name="rms_fwd_mlp1")
    act1 = _mm_nn(hn1, w_1, 1, out_dtype=BF16, name="mlp1_up", epilogue="relu2")
    h4 = _mm_nn(act1, w_2, 1, out_dtype=F32, name="mlp1_down", epilogue="residual", extra=h3)

    dh4, dg_final, loss_tile = _final_loss(h4, final_norm_g.reshape(1, d), target)

    def mlp_bwd(dh_out, h_in, hn, act, layer, tag):
        dz = _mm_nt(dh_out.astype(BF16), w_2, layer, out_dtype=BF16, name=f"mlp{tag}_down_bwd",
                    epilogue="relu2_bwd", extra=act)
        g_w2 = _mm_tn(act, dh_out.astype(BF16), 1, name=f"mlp{tag}_down_wgrad")
        g_w1 = _mm_tn(hn, dz, N_CHIP, name=f"mlp{tag}_up_wgrad")
        dhn = _mm_nt(dz, w_1, layer, out_dtype=F32, name=f"mlp{tag}_up_bwd")
        dh_in, dg = _rms_bwd(h_in, mlp_norm_g[layer:layer + 1], dhn, dh_out, name=f"rms_bwd_mlp{tag}")
        return dh_in, dg, g_w1, g_w2

    dh3, dg_mlp1, g_w1_1, g_w2_1 = mlp_bwd(dh4, h3, hn1, act1, 1, "1")

    dh3_bf = dh3.astype(BF16)
    dmix1 = _mm_nt(dh3_bf, w_cd_out, 0, out_dtype=BF16, name="cd_out_bwd")
    g_cd_out = _mm_tn(mix1, dh3_bf, 1, name="cd_out_wgrad")
    du, dv, dsgu_w, dsgu_bs, dsgu_g, dsgu_b = _sgu_bwd(p_cd, dmix1, sgu_g_full, sgu_b_full, sgu_w3, sgu_w3_t,
                                                      sgu_bias_tile)
    dq, dk, dvv = _sb_bwd(p_cd, dmix1, ltot, nseq, t_len)
    dp_cd = jnp.concatenate([du, dv, dq, dk, dvv], axis=1)
    g_cd_in = _mm_tn(xn1, dp_cd, N_CHIP, name="cd_in_wgrad")
    dxn1 = _mm_nt(dp_cd, w_cd_in, 0, out_dtype=F32, name="cd_in_bwd")
    dh2, dg_mix1 = _rms_bwd(h2, mix_norm_g[1:2], dxn1, dh3, name="rms_bwd_mix1")

    dh1, dg_mlp0, g_w1_0, g_w2_0 = mlp_bwd(dh2, h1, hn0, act0, 0, "0")

    dh1_bf = dh1.astype(BF16)
    dmix0 = _mm_nt(dh1_bf, w_ab_out, 0, out_dtype=BF16, name="ab_out_bwd")
    g_ab_out = _mm_tn(mix0, dh1_bf, 1, name="ab_out_wgrad")
    da, dxb, dgb, dgc, dpool_w, dpool_scale, dconv_w, dconv_b = _ab_bwd(
        p_ab, dmix0, pool_w3, pool_scale3, conv_w_full, conv_b2, nseq, t_len)
    dp_ab = jnp.concatenate([da, dxb, dgb, dgc], axis=1)
    g_ab_in = _mm_tn(xn0, dp_ab, N_CHIP, name="ab_in_wgrad")
    dxn0 = _mm_nt(dp_ab, w_ab_in, 0, out_dtype=F32, name="ab_in_bwd")
    grad_x, dg_mix0 = _rms_bwd(h0, mix_norm_g[0:1], dxn0, dh1, name="rms_bwd_mix0")

    def as_pieces(g):
        return g.reshape(1, N_CHIP, -1, g.shape[-1]) if g.shape[1] == 1 else g

    big = [as_pieces(g) for g in (g_ab_in, g_ab_out, g_cd_in, g_cd_out, g_w1_0, g_w1_1, g_w2_0, g_w2_1)]
    r_ab_in, r_ab_out, r_cd_in, r_cd_out, r_w1_0, r_w1_1, r_w2_0, r_w2_1 = _reduce_big(big)

    big_out = {
        "ab_w_in": _adam_big(ab_w_in, m_ab_w_in, v_ab_w_in, [r_ab_in], name="adam_ab_w_in"),
        "ab_w_out": _adam_big(ab_w_out, m_ab_w_out, v_ab_w_out, [r_ab_out], name="adam_ab_w_out"),
        "cd_w_in": _adam_big(cd_w_in, m_cd_w_in, v_cd_w_in, [r_cd_in], name="adam_cd_w_in"),
        "cd_w_out": _adam_big(cd_w_out, m_cd_w_out, v_cd_w_out, [r_cd_out], name="adam_cd_w_out"),
        "mlp_w1": _adam_big(mlp_w1, m_mlp_w1, v_mlp_w1, [r_w1_0, r_w1_1], name="adam_mlp_w1"),
        "mlp_w2": _adam_big(mlp_w2, m_mlp_w2, v_mlp_w2, [r_w2_0, r_w2_1], name="adam_mlp_w2"),
    }

    wide = jnp.concatenate([_pad_rows(jnp.concatenate([dg_mix0, dg_mix1], axis=0)),
                            _pad_rows(jnp.concatenate([dg_mlp0, dg_mlp1], axis=0)), _pad_rows(dg_final)], axis=0)
    mid = jnp.concatenate([_pad_rows(dconv_b), _pad_rows(dconv_w), _pad_rows(dsgu_g), _pad_rows(dsgu_b)], axis=0)
    narrow = jnp.concatenate(
        [dpool_w.reshape(4 * TILE, TILE), dsgu_w.reshape(4 * TILE, TILE), _pad_rows(dpool_scale.reshape(4, TILE)),
         _pad_rows(dsgu_bs[:, :, 0]), loss_tile], axis=0)
    wide, mid, narrow = _allreduce_small([wide, mid, narrow])
    small_out = _adam_small(wide, mid, narrow, {
        "mix_norm_g": (mix_norm_g, m_mix_norm_g, v_mix_norm_g),
        "mlp_norm_g": (mlp_norm_g, m_mlp_norm_g, v_mlp_norm_g),
        "final_norm_g": tuple(a.reshape(1, d) for a in (final_norm_g, m_final_norm_g, v_final_norm_g)),
        "conv_b": (conv_b, m_conv_b, v_conv_b),
        "conv_w": (conv_w, m_conv_w, v_conv_w),
        "sgu_norm_g": (sgu_norm_g, m_sgu_norm_g, v_sgu_norm_g),
        "sgu_norm_b": (sgu_norm_b, m_sgu_norm_b, v_sgu_norm_b),
        "pool_w": (pool_w, m_pool_w, v_pool_w),
        "pool_scale": (pool_scale, m_pool_scale, v_pool_scale),
        "sgu_w": (sgu_w, m_sgu_w, v_sgu_w),
        "sgu_b": (sgu_b, m_sgu_b, v_sgu_b),
    })
    small_out["final_norm_g"] = [a.reshape(d) for a in small_out["final_norm_g"]]

    order = ["mix_norm_g", "mlp_norm_g", "ab_w_in", "pool_w", "pool_scale", "conv_w", "conv_b", "ab_w_out",
             "cd_w_in", "sgu_norm_g", "sgu_norm_b", "sgu_w", "sgu_b", "cd_w_out", "mlp_w1", "mlp_w2",
             "final_norm_g"]
    both = {**big_out, **small_out}
    loss = narrow[LOSS_ROW, 0]
    outs = [loss, grad_x.reshape(nseq, t_len, d)]
    for kind in range(4):
        outs += [both[name][kind] for name in order]
    return tuple(outs)
```

```python
import math

import jax
import jax.numpy as jnp
from jax import lax
from jax.experimental import pallas as pl
from jax.experimental.pallas import tpu as pltpu

F32 = jnp.float32
BF16 = jnp.bfloat16
MESH = pl.DeviceIdType.MESH

D_MODEL = 1024
EPS = 1e-6
TILE = 128
N_CHIP = 4
N_DEV = 8
VMEM_LIMIT_BYTES = 56 * 1024 * 1024

ADAM_LR = 0.001
ADAM_B1 = 0.9
ADAM_B2 = 0.999
ADAM_EPS = 1e-08
ADAM_WD = 0.01
ADAM_STEP = 10

NT_DIMS = (((1,), (1,)), ((), ()))
TN_DIMS = (((0,), (0,)), ((), ()))


def _params(sem=None):
    return pltpu.CompilerParams(dimension_semantics=sem, vmem_limit_bytes=VMEM_LIMIT_BYTES)


def _mm_nn(a, b4, layer, *, out_dtype, name, epilogue=None, extra=None, tm=1024, tk=1024):
    m, k_dim = a.shape
    _, s_dim, kb, n = b4.shape
    assert kb == k_dim
    tn = min(n, 1024)
    assert m % tm == 0 and k_dim % tk == 0 and n % tn == 0
    nk, npb = k_dim // tk, n // tn
    grid = (m // tm, s_dim * npb, nk)

    def body(*refs):
        if extra is None:
            a_ref, b_ref, o_ref, *scr = refs
        else:
            a_ref, b_ref, e_ref, o_ref, *scr = refs

        def finish(acc):
            if epilogue == "relu2":
                r = jnp.maximum(acc, 0.0)
                acc = r * r
            elif epilogue == "residual":
                acc = acc + e_ref[...]
            o_ref[...] = acc.astype(out_dtype)

        part = jnp.dot(a_ref[...], b_ref[...], preferred_element_type=F32)
        if nk == 1:
            finish(part)
        else:
            acc_ref, = scr
            kk = pl.program_id(2)

            @pl.when(kk == 0)
            def _():
                acc_ref[...] = part

            @pl.when(kk > 0)
            def _():
                acc_ref[...] += part

            @pl.when(kk == nk - 1)
            def _():
                finish(acc_ref[...])

    in_specs = [
        pl.BlockSpec((tm, tk), lambda i, j, kk: (i, kk)),
        pl.BlockSpec((None, None, tk, tn), lambda i, j, kk: (layer, j // npb, kk, j % npb)),
    ]
    args = [a, b4]
    if extra is not None:
        in_specs.append(pl.BlockSpec((tm, tn), lambda i, j, kk: (i, j)))
        args.append(extra)
    return pl.pallas_call(
        body, name=name, grid=grid, in_specs=in_specs,
        out_specs=pl.BlockSpec((tm, tn), lambda i, j, kk: (i, j)),
        out_shape=jax.ShapeDtypeStruct((m, s_dim * n), out_dtype),
        scratch_shapes=[] if nk == 1 else [pltpu.VMEM((tm, tn), F32)],
        compiler_params=_params(("parallel", "parallel", "arbitrary")),
    )(*args)


def _mm_nt(a, b4, layer, *, out_dtype, name, epilogue=None, extra=None, tm=1024, tn=1024):
    m, k_dim = a.shape
    _, s_dim, n_out, n = b4.shape
    assert k_dim == s_dim * n
    tk = min(n, 1024)
    tn = min(tn, n_out)
    assert m % tm == 0 and n_out % tn == 0 and n % tk == 0
    kpb = n // tk
    nk = s_dim * kpb
    grid = (m // tm, n_out // tn, nk)

    def body(*refs):
        if extra is None:
            a_ref, b_ref, o_ref, *scr = refs
        else:
            a_ref, b_ref, e_ref, o_ref, *scr = refs

        def finish(acc):
            if epilogue == "relu2_bwd":
                acc = acc * (2.0 * jnp.sqrt(e_ref[...].astype(F32)))
            o_ref[...] = acc.astype(out_dtype)

        part = lax.dot_general(a_ref[...], b_ref[...], NT_DIMS, preferred_element_type=F32)
        if nk == 1:
            finish(part)
        else:
            acc_ref, = scr
            kk = pl.program_id(2)

            @pl.when(kk == 0)
            def _():
                acc_ref[...] = part

            @pl.when(kk > 0)
            def _():
                acc_ref[...] += part

            @pl.when(kk == nk - 1)
            def _():
                finish(acc_ref[...])

    in_specs = [
        pl.BlockSpec((tm, tk), lambda i, j, kk: (i, kk)),
        pl.BlockSpec((None, None, tn, tk), lambda i, j, kk: (layer, kk // kpb, j, kk % kpb)),
    ]
    args = [a, b4]
    if extra is not None:
        in_specs.append(pl.BlockSpec((tm, tn), lambda i, j, kk: (i, j)))
        args.append(extra)
    return pl.pallas_call(
        body, name=name, grid=grid, in_specs=in_specs,
        out_specs=pl.BlockSpec((tm, tn), lambda i, j, kk: (i, j)),
        out_shape=jax.ShapeDtypeStruct((m, n_out), out_dtype),
        scratch_shapes=[] if nk == 1 else [pltpu.VMEM((tm, tn), F32)],
        compiler_params=_params(("parallel", "parallel", "arbitrary")),
    )(*args)


def _mm_tn(a, b, s_dim, *, name, tm=1024, t1=1024):
    m, k1 = a.shape
    mb, n_all = b.shape
    assert mb == m and n_all % s_dim == 0
    n = n_all // s_dim
    tn = min(n, 1024)
    t1 = min(t1, k1)
    assert m % tm == 0 and k1 % t1 == 0 and n % tn == 0
    npb = n // tn
    nk = m // tm
    grid = (k1 // t1, s_dim * npb, nk)

    def body(a_ref, b_ref, o_ref, acc_ref):
        kk = pl.program_id(2)
        part = lax.dot_general(a_ref[...], b_ref[...], TN_DIMS, preferred_element_type=F32)

        @pl.when(kk == 0)
        def _():
            acc_ref[...] = part

        @pl.when(kk > 0)
        def _():
            acc_ref[...] += part

        @pl.when(kk == nk - 1)
        def _():
            o_ref[...] = acc_ref[...].astype(BF16)

    return pl.pallas_call(
        body, name=name, grid=grid,
        in_specs=[pl.BlockSpec((tm, t1), lambda i, j, kk: (kk, i)),
                  pl.BlockSpec((tm, tn), lambda i, j, kk: (kk, j))],
        out_specs=pl.BlockSpec((None, None, t1, tn), lambda i, j, kk: (0, j // npb, i, j % npb)),
        out_shape=jax.ShapeDtypeStruct((1, s_dim, k1, n), BF16),
        scratch_shapes=[pltpu.VMEM((t1, tn), F32)],
        compiler_params=_params(("parallel", "parallel", "arbitrary")),
    )(a, b)


ROW_TILE = 512


def _rms_fwd(h, g, *, name):
    m, d = h.shape

    def body(h_ref, g_ref, o_ref):
        hv = h_ref[...]
        rstd = lax.rsqrt(jnp.mean(hv * hv, axis=-1, keepdims=True) + EPS)
        o_ref[...] = (hv * rstd * g_ref[...]).astype(BF16)

    return pl.pallas_call(
        body, name=name, grid=(m // ROW_TILE,),
        in_specs=[pl.BlockSpec((ROW_TILE, d), lambda i: (i, 0)), pl.BlockSpec((1, d), lambda i: (0, 0))],
        out_specs=pl.BlockSpec((ROW_TILE, d), lambda i: (i, 0)),
        out_shape=jax.ShapeDtypeStruct((m, d), BF16),
        compiler_params=_params(("parallel",)),
    )(h, g)


def _rms_bwd(h, g, dxn, dres, *, name):
    m, d = h.shape

    def body(h_ref, g_ref, dxn_ref, dres_ref, dh_ref, dg_ref):
        hv = h_ref[...]
        rstd = lax.rsqrt(jnp.mean(hv * hv, axis=-1, keepdims=True) + EPS)
        xhat = hv * rstd
        dy = dxn_ref[...]
        dxhat = dy * g_ref[...]
        dh_ref[...] = dres_ref[...] + rstd * (dxhat - xhat * jnp.mean(dxhat * xhat, axis=-1, keepdims=True))
        part = jnp.sum(dy * xhat, axis=0, keepdims=True)

        @pl.when(pl.program_id(0) == 0)
        def _():
            dg_ref[...] = part

        @pl.when(pl.program_id(0) > 0)
        def _():
            dg_ref[...] += part

    row = pl.BlockSpec((ROW_TILE, d), lambda i: (i, 0))
    vec = pl.BlockSpec((1, d), lambda i: (0, 0))
    return pl.pallas_call(
        body, name=name, grid=(m // ROW_TILE,),
        in_specs=[row, vec, row, row], out_specs=[row, vec],
        out_shape=[jax.ShapeDtypeStruct((m, d), F32), jax.ShapeDtypeStruct((1, d), F32)],
        compiler_params=_params(("arbitrary",)),
    )(h, g, dxn, dres)


def _final_loss(h, g, target):
    m, d = h.shape

    def body(h_ref, g_ref, t_ref, dh_ref, dg_ref, loss_ref):
        hv = h_ref[...]
        gv = g_ref[...]
        rstd = lax.rsqrt(jnp.mean(hv * hv, axis=-1, keepdims=True) + EPS)
        xhat = hv * rstd
        err = xhat * gv - t_ref[...]
        dy = err * (1.0 / d)
        dxhat = dy * gv
        dh_ref[...] = rstd * (dxhat - xhat * jnp.mean(dxhat * xhat, axis=-1, keepdims=True))
        dg_part = jnp.sum(dy * xhat, axis=0, keepdims=True)
        sq = jnp.sum(jnp.sum(err * err, axis=1, keepdims=True), axis=0, keepdims=True) * (0.5 / d)
        loss_part = jnp.broadcast_to(sq, (8, TILE))

        @pl.when(pl.program_id(0) == 0)
        def _():
            dg_ref[...] = dg_part
            loss_ref[...] = loss_part

        @pl.when(pl.program_id(0) > 0)
        def _():
            dg_ref[...] += dg_part
            loss_ref[...] += loss_part

    row = pl.BlockSpec((ROW_TILE, d), lambda i: (i, 0))
    vec = pl.BlockSpec((1, d), lambda i: (0, 0))
    return pl.pallas_call(
        body, name="final_loss", grid=(m // ROW_TILE,),
        in_specs=[row, vec, row],
        out_specs=[row, vec, pl.BlockSpec((8, TILE), lambda i: (0, 0))],
        out_shape=[jax.ShapeDtypeStruct((m, d), F32), jax.ShapeDtypeStruct((1, d), F32),
                   jax.ShapeDtypeStruct((8, TILE), F32)],
        compiler_params=_params(("arbitrary",)),
    )(h, g, target)


def _shift_down(x, s, t_idx):
    return jnp.where(t_idx >= s, pltpu.roll(x, s, 0), 0.0)


def _shift_up(x, s, t_idx, t_len):
    return jnp.where(t_idx < t_len - s, pltpu.roll(x, t_len - s, 0), 0.0)


def _pool_select(group, s2, s4, s8, s16):
    return jnp.where(group == 0, s2, jnp.where(group == 1, s4, jnp.where(group == 2, s8, s16)))


def _pool_count(group, t_idx):
    win = jnp.left_shift(2, group)
    return jnp.minimum(t_idx + 1, win).astype(F32)


def _pool_fwd_math(a, group, t_idx):
    s2 = a + _shift_down(a, 1, t_idx)
    s4 = s2 + _shift_down(s2, 2, t_idx)
    s8 = s4 + _shift_down(s4, 4, t_idx)
    s16 = s8 + _shift_down(s8, 8, t_idx)
    return _pool_select(group, s2, s4, s8, s16) / _pool_count(group, t_idx) - a


def _pool_bwd_math(dpooled, group, t_idx, t_len):
    e = dpooled / _pool_count(group, t_idx)
    s2 = e + _shift_up(e, 1, t_idx, t_len)
    s4 = s2 + _shift_up(s2, 2, t_idx, t_len)
    s8 = s4 + _shift_up(s4, 4, t_idx, t_len)
    s16 = s8 + _shift_up(s8, 8, t_idx, t_len)
    return _pool_select(group, s2, s4, s8, s16) - dpooled


def _conv_fwd_math(c, w_ref, b_ref, t_idx):
    return (w_ref[0:1, :] * _shift_down(c, 2, t_idx) + w_ref[1:2, :] * _shift_down(c, 1, t_idx)
            + w_ref[2:3, :] * c + b_ref[...])


def _ab_fwd(p, pool_w, pool_scale, conv_w, conv_b, nseq, t_len):
    m = p.shape[0]
    ng = 4

    def body(a_ref, xb_ref, gb_ref, gc_ref, pw_ref, ps_ref, cw_ref, cb_ref, o_ref):
        j = pl.program_id(1)
        t_idx = lax.broadcasted_iota(jnp.int32, (t_len, TILE), 0)

        @pl.when(j < ng)
        def _():
            pooled = _pool_fwd_math(a_ref[...].astype(F32), j, t_idx)
            mixed = jnp.dot(pooled.astype(BF16), pw_ref[...].astype(BF16), preferred_element_type=F32)
            o_ref[...] = (mixed * ps_ref[...]).astype(BF16)

        @pl.when(j >= ng)
        def _():
            c = gc_ref[...].astype(F32) * xb_ref[...].astype(F32)
            y = _conv_fwd_math(c, cw_ref, cb_ref, t_idx)
            o_ref[...] = (gb_ref[...].astype(F32) * y).astype(BF16)

    def pool_j(j):
        return jnp.minimum(j, ng - 1)

    def conv_j(j):
        return jnp.maximum(j - ng, 0)

    in_specs = [
        pl.BlockSpec((t_len, TILE), lambda s, j: (s, pool_j(j))),
        pl.BlockSpec((t_len, TILE), lambda s, j: (s, ng + conv_j(j))),
        pl.BlockSpec((t_len, TILE), lambda s, j: (s, 2 * ng + conv_j(j))),
        pl.BlockSpec((t_len, TILE), lambda s, j: (s, 3 * ng + conv_j(j))),
        pl.BlockSpec((None, TILE, TILE), lambda s, j: (pool_j(j), 0, 0)),
        pl.BlockSpec((None, 1, TILE), lambda s, j: (pool_j(j), 0, 0)),
        pl.BlockSpec((3, TILE), lambda s, j: (0, conv_j(j))),
        pl.BlockSpec((1, TILE), lambda s, j: (0, conv_j(j))),
    ]
    return pl.pallas_call(
        body, name="ab_mixer_fwd", grid=(nseq, 2 * ng), in_specs=in_specs,
        out_specs=pl.BlockSpec((t_len, TILE), lambda s, j: (s, j)),
        out_shape=jax.ShapeDtypeStruct((m, 2 * ng * TILE), BF16),
        compiler_params=_params(("parallel", "arbitrary")),
    )(p, p, p, p, pool_w, pool_scale, conv_w, conv_b)


def _ab_bwd(p, dmix, pool_w, pool_scale, conv_w, conv_b, nseq, t_len):
    m = p.shape[0]
    ng = 4

    def body(a_ref, xb_ref, gb_ref, gc_ref, dma_ref, dmb_ref, pw_ref, ps_ref, cw_ref, cb_ref,
             da_ref, dxb_ref, dgb_ref, dgc_ref, dpw_ref, dps_ref, dcw_ref, dcb_ref):
        j = pl.program_id(0)
        first = pl.program_id(1) == 0
        t_idx = lax.broadcasted_iota(jnp.int32, (t_len, TILE), 0)

        pooled = _pool_fwd_math(a_ref[...].astype(F32), j, t_idx).astype(BF16)
        w_bf = pw_ref[...].astype(BF16)
        mixed = jnp.dot(pooled, w_bf, preferred_element_type=F32)
        dm = dma_ref[...].astype(F32)
        dps = jnp.sum(dm * mixed, axis=0, keepdims=True)
        dmixed = (dm * ps_ref[...]).astype(BF16)
        dpw = lax.dot_general(pooled, dmixed, TN_DIMS, preferred_element_type=F32)
        dpooled = lax.dot_general(dmixed, w_bf, NT_DIMS, preferred_element_type=F32)
        da_ref[...] = _pool_bwd_math(dpooled, j, t_idx, t_len).astype(BF16)

        xb = xb_ref[...].astype(F32)
        gb = gb_ref[...].astype(F32)
        gc = gc_ref[...].astype(F32)
        d = dmb_ref[...].astype(F32)
        c = gc * xb
        c1 = _shift_down(c, 1, t_idx)
        c2 = _shift_down(c, 2, t_idx)
        y = cw_ref[0:1, :] * c2 + cw_ref[1:2, :] * c1 + cw_ref[2:3, :] * c + cb_ref[...]
        dgb_ref[...] = (d * y).astype(BF16)
        dy = d * gb
        dc = (cw_ref[2:3, :] * dy + cw_ref[1:2, :] * _shift_up(dy, 1, t_idx, t_len)
              + cw_ref[0:1, :] * _shift_up(dy, 2, t_idx, t_len))
        dgc_ref[...] = (dc * xb).astype(BF16)
        dxb_ref[...] = (dc * gc).astype(BF16)
        dcw = jnp.concatenate([jnp.sum(dy * c2, axis=0, keepdims=True),
                               jnp.sum(dy * c1, axis=0, keepdims=True),
                               jnp.sum(dy * c, axis=0, keepdims=True)], axis=0)
        dcb = jnp.sum(dy, axis=0, keepdims=True)

        @pl.when(first)
        def _():
            dpw_ref[...] = dpw
            dps_ref[...] = dps
            dcw_ref[...] = dcw
            dcb_ref[...] = dcb

        @pl.when(jnp.logical_not(first))
        def _():
            dpw_ref[...] += dpw
            dps_ref[...] += dps
            dcw_ref[...] += dcw
            dcb_ref[...] += dcb

    def col(k):
        return pl.BlockSpec((t_len, TILE), lambda j, s: (s, k * ng + j))

    in_specs = [
        col(0), col(1), col(2), col(3), col(0), col(1),
        pl.BlockSpec((None, TILE, TILE), lambda j, s: (j, 0, 0)),
        pl.BlockSpec((None, 1, TILE), lambda j, s: (j, 0, 0)),
        pl.BlockSpec((3, TILE), lambda j, s: (0, j)),
        pl.BlockSpec((1, TILE), lambda j, s: (0, j)),
    ]
    piece = pl.BlockSpec((t_len, TILE), lambda j, s: (s, j))
    out_specs = [
        piece, piece, piece, piece,
        pl.BlockSpec((None, TILE, TILE), lambda j, s: (j, 0, 0)),
        pl.BlockSpec((None, 1, TILE), lambda j, s: (j, 0, 0)),
        pl.BlockSpec((3, TILE), lambda j, s: (0, j)),
        pl.BlockSpec((1, TILE), lambda j, s: (0, j)),
    ]
    w = ng * TILE
    out_shape = [jax.ShapeDtypeStruct((m, w), BF16)] * 4 + [
        jax.ShapeDtypeStruct((ng, TILE, TILE), F32), jax.ShapeDtypeStruct((ng, 1, TILE), F32),
        jax.ShapeDtypeStruct((3, w), F32), jax.ShapeDtypeStruct((1, w), F32)]
    return pl.pallas_call(
        body, name="ab_mixer_bwd", grid=(ng, nseq), in_specs=in_specs, out_specs=out_specs,
        out_shape=out_shape, compiler_params=_params(("parallel", "arbitrary")),
    )(p, p, p, p, dmix, dmix, pool_w, pool_scale, conv_w, conv_b)


SGU_ROWS = 512
INV_SQRT2 = 1.0 / math.sqrt(2.0)
INV_SQRT_2PI = 1.0 / math.sqrt(2.0 * math.pi)


def _gelu(x):
    return 0.5 * x * (1.0 + lax.erf(x * INV_SQRT2))


def _gelu_grad(x):
    return 0.5 * (1.0 + lax.erf(x * INV_SQRT2)) + x * (INV_SQRT_2PI * jnp.exp(-0.5 * x * x))


def _causal_tile(transposed=False):
    r = lax.broadcasted_iota(jnp.int32, (TILE, TILE), 0)
    c = lax.broadcasted_iota(jnp.int32, (TILE, TILE), 1)
    return r <= c if transposed else c <= r


def _sgu_norm(v, g_ref, b_ref):
    mu = jnp.mean(v, axis=-1, keepdims=True)
    xc = v - mu
    rstd = lax.rsqrt(jnp.mean(xc * xc, axis=-1, keepdims=True) + EPS)
    xhat = xc * rstd
    return xhat, rstd, xhat * g_ref[...] + b_ref[...]


def _sgu_fwd(p, norm_g, norm_b, w_s, bias_tile):
    m = p.shape[0]
    ng = 4
    width = ng * TILE

    def body(u_ref, v_ref, g_ref, b_ref, w_ref, bias_ref, o_ref):
        u = _gelu(u_ref[...].astype(F32))
        _, _, vln = _sgu_norm(_gelu(v_ref[...].astype(F32)), g_ref, b_ref)
        vln = vln.astype(BF16)
        causal = _causal_tile()
        for g in range(ng):
            cols = slice(g * TILE, (g + 1) * TILE)
            wg = jnp.where(causal, w_ref[g], 0.0).astype(BF16)
            for n in range(SGU_ROWS // TILE):
                rows = slice(n * TILE, (n + 1) * TILE)
                s = jnp.dot(wg, vln[rows, cols], preferred_element_type=F32) + bias_ref[g]
                o_ref[rows, cols] = (u[rows, cols] * s).astype(BF16)

    vec = pl.BlockSpec((1, width), lambda i: (0, 0))
    tiles = pl.BlockSpec((ng, TILE, TILE), lambda i: (0, 0, 0))
    return pl.pallas_call(
        body, name="sgu_fwd", grid=(m // SGU_ROWS,),
        in_specs=[pl.BlockSpec((SGU_ROWS, width), lambda i: (i, 0)),
                  pl.BlockSpec((SGU_ROWS, width), lambda i: (i, 1)), vec, vec, tiles, tiles],
        out_specs=pl.BlockSpec((SGU_ROWS, width), lambda i: (i, 0)),
        out_shape=jax.ShapeDtypeStruct((m, width), BF16),
        compiler_params=_params(("parallel",)),
    )(p, p, norm_g, norm_b, w_s, bias_tile)


def _sgu_bwd(p, dmix, norm_g, norm_b, w_s, w_s_t, bias_tile):
    m = p.shape[0]
    ng = 4
    width = ng * TILE

    def body(u_ref, v_ref, dc_ref, g_ref, b_ref, w_ref, wt_ref, bias_ref,
             du_ref, dv_ref, dw_ref, dbs_ref, dg_ref, db_ref, ds_scr, dvln_scr):
        u_pre = u_ref[...].astype(F32)
        v_pre = v_ref[...].astype(F32)
        u = _gelu(u_pre)
        xhat, rstd, vln = _sgu_norm(_gelu(v_pre), g_ref, b_ref)
        vln = vln.astype(BF16)
        dc = dc_ref[...].astype(F32)
        causal = _causal_tile()
        ones = jnp.ones((TILE, TILE), BF16)
        first = pl.program_id(0) == 0
        for g in range(ng):
            cols = slice(g * TILE, (g + 1) * TILE)
            wg = jnp.where(causal, w_ref[g], 0.0).astype(BF16)
            wgt = jnp.where(_causal_tile(transposed=True), wt_ref[g], 0.0).astype(BF16)
            dw_acc = jnp.zeros((TILE, TILE), F32)
            dbs_acc = jnp.zeros((TILE, TILE), F32)
            for n in range(SGU_ROWS // TILE):
                rows = slice(n * TILE, (n + 1) * TILE)
                vt = vln[rows, cols]
                s = jnp.dot(wg, vt, preferred_element_type=F32) + bias_ref[g]
                ds_scr[rows, cols] = dc[rows, cols] * s
                ds = (dc[rows, cols] * u[rows, cols]).astype(BF16)
                dw_acc += lax.dot_general(ds, vt, NT_DIMS, preferred_element_type=F32)
                dbs_acc += jnp.dot(ds, ones, preferred_element_type=F32)
                dvln_scr[rows, cols] = jnp.dot(wgt, ds, preferred_element_type=F32)
            dw_g = jnp.where(causal, dw_acc, 0.0)

            @pl.when(first)
            def _():
                dw_ref[g] = dw_g
                dbs_ref[g] = dbs_acc

            @pl.when(jnp.logical_not(first))
            def _():
                dw_ref[g] += dw_g
                dbs_ref[g] += dbs_acc

        du_ref[...] = (ds_scr[...] * _gelu_grad(u_pre)).astype(BF16)
        dvln = dvln_scr[...]
        dxhat = dvln * g_ref[...]
        dv = rstd * (dxhat - jnp.mean(dxhat, axis=-1, keepdims=True)
                     - xhat * jnp.mean(dxhat * xhat, axis=-1, keepdims=True))
        dv_ref[...] = (dv * _gelu_grad(v_pre)).astype(BF16)
        dg_part = jnp.sum(dvln * xhat, axis=0, keepdims=True)
        db_part = jnp.sum(dvln, axis=0, keepdims=True)

        @pl.when(first)
        def _():
            dg_ref[...] = dg_part
            db_ref[...] = db_part

        @pl.when(jnp.logical_not(first))
        def _():
            dg_ref[...] += dg_part
            db_ref[...] += db_part

    vec = pl.BlockSpec((1, width), lambda i: (0, 0))
    tiles = pl.BlockSpec((ng, TILE, TILE), lambda i: (0, 0, 0))
    rows0 = pl.BlockSpec((SGU_ROWS, width), lambda i: (i, 0))
    rows1 = pl.BlockSpec((SGU_ROWS, width), lambda i: (i, 1))
    return pl.pallas_call(
        body, name="sgu_bwd", grid=(m // SGU_ROWS,),
        in_specs=[rows0, rows1, rows0, vec, vec, tiles, tiles, tiles],
        out_specs=[rows0, rows0, tiles, tiles, vec, vec],
        out_shape=[jax.ShapeDtypeStruct((m, width), BF16), jax.ShapeDtypeStruct((m, width), BF16),
                   jax.ShapeDtypeStruct((ng, TILE, TILE), F32), jax.ShapeDtypeStruct((ng, TILE, TILE), F32),
                   jax.ShapeDtypeStruct((1, width), F32), jax.ShapeDtypeStruct((1, width), F32)],
        scratch_shapes=[pltpu.VMEM((SGU_ROWS, width), F32), pltpu.VMEM((SGU_ROWS, width), F32)],
        compiler_params=_params(("arbitrary",)),
    )(p, p, dmix, norm_g, norm_b, w_s, w_s_t, bias_tile)


SB_DH = 64
SB_SCALE = 1.0 / math.sqrt(SB_DH)


def _split_bf16(x):
    hi = x.astype(BF16)
    lo = (x - hi.astype(F32)).astype(BF16)
    return jnp.concatenate([hi, lo], axis=1)


def _sum_matrix(kind):
    j = lax.broadcasted_iota(jnp.int32, (2 * TILE, 2 * TILE), 0) % TILE
    s = lax.broadcasted_iota(jnp.int32, (2 * TILE, 2 * TILE), 1)
    tri = {"after": j > s, "upto": j <= s, "before": j < s}[kind]
    return jnp.where(jnp.logical_or(s >= TILE, tri), 1.0, 0.0).astype(BF16)


def _head_lanes(h):
    lane = lax.broadcasted_iota(jnp.int32, (1, TILE), 1)
    return (lane >= h * SB_DH) & (lane < (h + 1) * SB_DH)


def _softplus(z):
    return jnp.maximum(z, 0.0) + jnp.log(1.0 + jnp.exp(-jnp.abs(z)))


def _sb_fwd(p, nseq, t_len):
    m = p.shape[0]
    nb = t_len // TILE
    npair = 4

    def body(q_ref, k_ref, v_ref, o_ref, lt_ref, kh_ref, vh_ref):
        for h in range(2):
            keep = _head_lanes(h)
            kh_ref[h] = jnp.where(keep, k_ref[...], 0).astype(BF16)
            vh_ref[h] = jnp.where(keep, v_ref[...], 0).astype(BF16)
        summat = _sum_matrix("after")
        r = lax.broadcasted_iota(jnp.int32, (TILE, TILE), 0)
        c = lax.broadcasted_iota(jnp.int32, (TILE, TILE), 1)
        strict = c < r

        def tile(h, q, row0, nblk, diag, carry, acc):
            rows = pl.ds(row0, nblk * TILE)
            kj = kh_ref[h, rows, :]
            vj = vh_ref[h, rows, :]
            z = lax.dot_general(q, kj, NT_DIMS, preferred_element_type=F32) * SB_SCALE
            sp = _softplus(z)
            logkeep = jnp.where(strict, -sp, 0.0) if diag else -sp
            after = [None] * nblk
            for b in reversed(range(nblk)):
                sums = jnp.dot(_split_bf16(logkeep[:, b * TILE:(b + 1) * TILE]), summat,
                               preferred_element_type=F32)
                after[b] = sums[:, :TILE] + carry
                carry = carry + sums[:, TILE:]
            w = jnp.exp(z - sp + (after[0] if nblk == 1 else jnp.concatenate(after, axis=1)))
            if diag:
                w = jnp.where(strict, w, 0.0)
            return carry, acc + jnp.dot(w.astype(BF16), vj, preferred_element_type=F32)

        def both_heads(q, row0, nblk, diag, state):
            out = []
            for h in range(2):
                out += list(tile(h, q, row0, nblk, diag, state[2 * h], state[2 * h + 1]))
            return tuple(out)

        def q_block(i, _):
            r0 = pl.multiple_of(i * TILE, TILE)
            q = q_ref[pl.ds(r0, TILE), :]
            zero = jnp.zeros((TILE, TILE), F32)
            state = both_heads(q, r0, 1, True, (zero,) * 4)
            state = lax.fori_loop(
                0, i // 2,
                lambda jj, st: both_heads(q, pl.multiple_of((i - 2 - 2 * jj) * TILE, TILE), 2, False, st), state)
            state = lax.cond(i % 2 == 1, lambda st: both_heads(q, 0, 1, False, st), lambda st: st, state)
            o_ref[pl.ds(r0, TILE), :] = (state[1] + state[3]).astype(BF16)
            lt_ref[pl.ds(r0, TILE), :] = jnp.where(_head_lanes(0), state[0], state[2])
            return 0

        lax.fori_loop(0, nb, q_block, 0)

    def col(k):
        return pl.BlockSpec((t_len, TILE), lambda s, hp: (s, k * npair + hp))

    out = pl.BlockSpec((t_len, TILE), lambda s, hp: (s, hp))
    return pl.pallas_call(
        body, name="stickbreak_fwd", grid=(nseq, npair), in_specs=[col(2), col(3), col(4)],
        out_specs=[out, out],
        out_shape=[jax.ShapeDtypeStruct((m, npair * TILE), BF16), jax.ShapeDtypeStruct((m, npair * TILE), F32)],
        scratch_shapes=[pltpu.VMEM((2, t_len, TILE), BF16), pltpu.VMEM((2, t_len, TILE), BF16)],
        compiler_params=_params(("parallel", "parallel")),
    )(p, p, p)


def _sb_bwd(p, dmix, ltot, nseq, t_len):
    m = p.shape[0]
    nb = t_len // TILE
    npair = 4

    def body(q_ref, k_ref, v_ref, do_ref, lt_ref, dq_ref, dk_ref, dv_ref, kh_ref, vh_ref, dk_acc, dv_acc):
        for h in range(2):
            keep = _head_lanes(h)
            kh_ref[h] = jnp.where(keep, k_ref[...], 0).astype(BF16)
            vh_ref[h] = jnp.where(keep, v_ref[...], 0).astype(BF16)
        dk_acc[...] = jnp.zeros_like(dk_acc)
        dv_acc[...] = jnp.zeros_like(dv_acc)
        sum_upto = _sum_matrix("upto")
        sum_before = _sum_matrix("before")
        r = lax.broadcasted_iota(jnp.int32, (TILE, TILE), 0)
        c = lax.broadcasted_iota(jnp.int32, (TILE, TILE), 1)
        strict = c < r
        lane = lax.broadcasted_iota(jnp.int32, (TILE, TILE), 1)

        def tile(h, q, do, qh, doh, ltot_h, row0, nblk, diag, sum_l, sum_g, dq):
            rows = pl.ds(row0, nblk * TILE)
            kj = kh_ref[h, rows, :]
            vj = vh_ref[h, rows, :]
            z = lax.dot_general(q, kj, NT_DIMS, preferred_element_type=F32) * SB_SCALE
            sp = _softplus(z)
            logkeep = jnp.where(strict, -sp, 0.0) if diag else -sp
            upto = []
            for b in range(nblk):
                sums = jnp.dot(_split_bf16(logkeep[:, b * TILE:(b + 1) * TILE]), sum_upto,
                               preferred_element_type=F32)
                upto.append(sums[:, :TILE] + sum_l)
                sum_l = sum_l + sums[:, TILE:]
            suffix = ltot_h - (upto[0] if nblk == 1 else jnp.concatenate(upto, axis=1))
            w = jnp.exp(z - sp + suffix)
            if diag:
                w = jnp.where(strict, w, 0.0)
            g = w * lax.dot_general(do, vj, NT_DIMS, preferred_element_type=F32)
            before = []
            for b in range(nblk):
                sums = jnp.dot(_split_bf16(g[:, b * TILE:(b + 1) * TILE]), sum_before, preferred_element_type=F32)
                before.append(sums[:, :TILE] + sum_g)
                sum_g = sum_g + sums[:, TILE:]
            g_before = before[0] if nblk == 1 else jnp.concatenate(before, axis=1)
            dz = (g - jnp.exp(z - sp) * (g + g_before)) * SB_SCALE
            if diag:
                dz = jnp.where(strict, dz, 0.0)
            dzb = dz.astype(BF16)
            dq = dq + jnp.dot(dzb, kj, preferred_element_type=F32)
            dk_part = lax.dot_general(dzb, qh, TN_DIMS, preferred_element_type=F32)
            dv_part = lax.dot_general(w.astype(BF16), doh, TN_DIMS, preferred_element_type=F32)
            return sum_l, sum_g, dq, dk_part, dv_part

        def both_heads(per_head, row0, nblk, diag, state):
            out, dk_new, dv_new = [], 0.0, 0.0
            for h in range(2):
                sum_l, sum_g, dq, dk_part, dv_part = tile(h, *per_head[h], row0, nblk, diag, *state[3 * h:3 * h + 3])
                out += [sum_l, sum_g, dq]
                dk_new = dk_new + dk_part
                dv_new = dv_new + dv_part
            rows = pl.ds(row0, nblk * TILE)
            dk_acc[rows, :] += dk_new
            dv_acc[rows, :] += dv_new
            return tuple(out)

        def q_block(i, _):
            r0 = pl.multiple_of(i * TILE, TILE)
            q = q_ref[pl.ds(r0, TILE), :]
            do = do_ref[pl.ds(r0, TILE), :]
            lt = lt_ref[pl.ds(r0, TILE), :]
            per_head = []
            for h in range(2):
                keep = _head_lanes(h)
                ltot_h = jnp.sum(jnp.where(lane == h * SB_DH, lt, 0.0), axis=1, keepdims=True)
                per_head.append((q, do, jnp.where(keep, q, 0).astype(BF16), jnp.where(keep, do, 0).astype(BF16),
                                 ltot_h))
            zero = jnp.zeros((TILE, TILE), F32)
            state = lax.fori_loop(
                0, i // 2,
                lambda jj, st: both_heads(per_head, pl.multiple_of(2 * jj * TILE, TILE), 2, False, st), (zero,) * 6)
            state = lax.cond(
                i % 2 == 1,
                lambda st: both_heads(per_head, pl.multiple_of((i - 1) * TILE, TILE), 1, False, st),
                lambda st: st, state)
            state = both_heads(per_head, r0, 1, True, state)
            dq_ref[pl.ds(r0, TILE), :] = (state[2] + state[5]).astype(BF16)
            return 0

        lax.fori_loop(0, nb, q_block, 0)
        dk_ref[...] = dk_acc[...].astype(BF16)
        dv_ref[...] = dv_acc[...].astype(BF16)

    def col(k):
        return pl.BlockSpec((t_len, TILE), lambda s, hp: (s, k * npair + hp))

    out = pl.BlockSpec((t_len, TILE), lambda s, hp: (s, hp))
    width = npair * TILE
    return pl.pallas_call(
        body, name="stickbreak_bwd", grid=(nseq, npair),
        in_specs=[col(2), col(3), col(4), col(1), out], out_specs=[out, out, out],
        out_shape=[jax.ShapeDtypeStruct((m, width), BF16)] * 3,
        scratch_shapes=[pltpu.VMEM((2, t_len, TILE), BF16), pltpu.VMEM((2, t_len, TILE), BF16),
                        pltpu.VMEM((t_len, TILE), F32), pltpu.VMEM((t_len, TILE), F32)],
        compiler_params=_params(("parallel", "parallel")),
    )(p, p, p, dmix, ltot)


def _adam_math(w, g, m, v):
    m = ADAM_B1 * m + (1.0 - ADAM_B1) * g
    v = ADAM_B2 * v + (1.0 - ADAM_B2) * (g * g)
    m_hat = m / (1.0 - ADAM_B1 ** ADAM_STEP)
    v_hat = v / (1.0 - ADAM_B2 ** ADAM_STEP)
    delta = -ADAM_LR * (m_hat / (jnp.sqrt(v_hat) + ADAM_EPS) + ADAM_WD * w)
    return delta, m, v


def _cast_place(w, pos, *, name):
    l_dim, r, c = w.shape
    tr = min(r, 256)

    def body(pos_ref, w_ref, o_ref):
        o_ref[...] = w_ref[...].astype(BF16)

    grid_spec = pltpu.PrefetchScalarGridSpec(
        num_scalar_prefetch=1, grid=(l_dim, r // tr),
        in_specs=[pl.BlockSpec((None, tr, c), lambda l, i, pos_ref: (l, i, 0))],
        out_specs=pl.BlockSpec((None, None, tr, c), lambda l, i, pos_ref: (l, pos_ref[0], i, 0)))
    return pl.pallas_call(
        body, name=name, grid_spec=grid_spec, out_shape=jax.ShapeDtypeStruct((l_dim, N_CHIP, r, c), BF16),
        compiler_params=_params(("parallel",) * 2),
    )(pos, w)


def _pair_sum(mine, got, pos, *, name):
    l_dim, s_dim, h, c = got.shape
    th = min(h, 512)
    nt = h // th

    def body(pos_ref, a_ref, b_ref, o_ref):
        o_ref[...] = (a_ref[...].astype(F32) + b_ref[...].astype(F32)).astype(BF16)

    spec = pl.BlockSpec((None, None, th, c), lambda l, s, i, pos_ref: (l, s, i, 0))
    grid_spec = pltpu.PrefetchScalarGridSpec(
        num_scalar_prefetch=1, grid=(l_dim, s_dim, nt),
        in_specs=[pl.BlockSpec((None, None, th, c), lambda l, s, i, pos_ref: (l, s, pos_ref[1] * nt + i, 0)), spec],
        out_specs=spec)
    return pl.pallas_call(
        body, name=name, grid_spec=grid_spec, out_shape=jax.ShapeDtypeStruct(got.shape, BF16),
        compiler_params=_params(("parallel",) * 3),
    )(pos, mine, got)


def _chip_sum(sums, landed, pos, *, name):
    l_dim, _, h, c = sums.shape
    th = min(h, 512)
    nt = h // th

    def body(pos_ref, own, r0, r1, r2, o_ref):
        o_ref[...] = ((own[...].astype(F32) + r0[...].astype(F32)) + r1[...].astype(F32)) + r2[...].astype(F32)

    def piece(k):
        return pl.BlockSpec((None, None, th, c), lambda l, i, pos_ref: (l, k, i, 0))

    grid_spec = pltpu.PrefetchScalarGridSpec(
        num_scalar_prefetch=1, grid=(l_dim, nt),
        in_specs=[pl.BlockSpec((None, None, th, c), lambda l, i, pos_ref: (l, pos_ref[0], i, 0)),
                  piece(0), piece(1), piece(2)],
        out_specs=pl.BlockSpec((None, th, c), lambda l, i, pos_ref: (l, pos_ref[1] * nt + i, 0)))
    return pl.pallas_call(
        body, name=name, grid_spec=grid_spec, out_shape=jax.ShapeDtypeStruct((l_dim, 2 * h, c), F32),
        compiler_params=_params(("parallel",) * 2),
    )(pos, sums, landed, landed, landed)


def _adam_big(w, m, v, grads, *, name):
    l_dim, r, c = w.shape
    assert len(grads) == l_dim
    tr = min(r, 256)

    def body(*refs):
        w_ref, m_ref, v_ref = refs[:3]
        g_refs = refs[3:3 + l_dim]
        go_ref, d_ref, mo_ref, vo_ref = refs[3 + l_dim:]
        g = g_refs[0][...]
        for l in range(1, l_dim):
            g = jnp.where(pl.program_id(0) == l, g_refs[l][...], g)
        delta, m_new, v_new = _adam_math(w_ref[...], g, m_ref[...], v_ref[...])
        go_ref[...] = g
        d_ref[...] = delta
        mo_ref[...] = m_new
        vo_ref[...] = v_new

    spec = pl.BlockSpec((None, tr, c), lambda l, i: (l, i, 0))
    gspec = pl.BlockSpec((None, tr, c), lambda l, i: (0, i, 0))
    return pl.pallas_call(
        body, name=name, grid=(l_dim, r // tr), in_specs=[spec] * 3 + [gspec] * l_dim, out_specs=[spec] * 4,
        out_shape=[jax.ShapeDtypeStruct(w.shape, F32)] * 4, compiler_params=_params(("parallel",) * 2),
    )(w, m, v, *grads)


def _position():
    return lax.axis_index("x"), lax.axis_index("y"), lax.axis_index("c")


def _other_chips(x, y):
    return [(1 - x, y), (x, 1 - y), (1 - x, 1 - y)]


def _remote(src, dst, send_sem, recv_sem, device):
    return pltpu.make_async_remote_copy(src_ref=src, dst_ref=dst, send_sem=send_sem, recv_sem=recv_sem,
                                        device_id=device, device_id_type=MESH)


ANY = pl.BlockSpec(memory_space=pl.ANY)


def _gather_weights(bufs):
    n = len(bufs)

    def body(*refs):
        outs = refs[n:2 * n]
        send_sems, recv_sems, fwd_send, fwd_recv = refs[2 * n:]
        x, y, c = _position()
        chips = _other_chips(x, y)
        sibling = (x, y, 1 - c)

        def half(a, chip, core):
            h = bufs[a].shape[2] // 2
            return outs[a].at[:, 2 * chip[0] + chip[1], pl.ds(core * h, h), :]

        sends = []
        for a in range(n):
            mine = half(a, (x, y), c)
            for k, chip in enumerate(chips):
                cp = _remote(mine, mine, send_sems.at[3 * a + k], recv_sems.at[3 * a + k], (*chip, c))
                cp.start()
                sends.append(cp)
        for k, chip in enumerate(chips):
            for a in range(n):
                landed = half(a, chip, c)
                _remote(landed, landed, send_sems.at[3 * a + k], recv_sems.at[3 * a + k], (*chip, c)).wait_recv()
                cp = _remote(landed, landed, fwd_send.at[3 * a + k], fwd_recv.at[3 * a + k], sibling)
                cp.start()
                sends.append(cp)
        for k, chip in enumerate(chips):
            for a in range(n):
                got = half(a, chip, 1 - c)
                _remote(got, got, fwd_send.at[3 * a + k], fwd_recv.at[3 * a + k], sibling).wait_recv()
        for cp in sends:
            cp.wait_send()

    sem = pltpu.SemaphoreType.DMA((3 * n,))
    return pl.pallas_call(
        body, name="gather_weights", in_specs=[ANY] * n, out_specs=[ANY] * n,
        out_shape=[jax.ShapeDtypeStruct(b.shape, b.dtype) for b in bufs],
        input_output_aliases={a: a for a in range(n)},
        scratch_shapes=[sem, sem, sem, sem],
        compiler_params=pltpu.CompilerParams(has_side_effects=True),
    )(*bufs)


def _swap_halves(grads):
    n = len(grads)

    def body(*refs):
        ins, got = refs[:n], refs[n:2 * n]
        send_sems, recv_sems = refs[2 * n:]
        x, y, c = _position()
        sibling = (x, y, 1 - c)
        copies = []
        for a in range(n):
            h = grads[a].shape[2] // 2
            cp = _remote(ins[a].at[:, :, pl.ds((1 - c) * h, h), :], got[a], send_sems.at[a], recv_sems.at[a], sibling)
            cp.start()
            copies.append(cp)
        for cp in copies:
            cp.wait()

    sem = pltpu.SemaphoreType.DMA((n,))
    return pl.pallas_call(
        body, name="swap_halves", in_specs=[ANY] * n, out_specs=[ANY] * n,
        out_shape=[jax.ShapeDtypeStruct(g.shape[:2] + (g.shape[2] // 2, g.shape[3]), g.dtype) for g in grads],
        scratch_shapes=[sem, sem], compiler_params=pltpu.CompilerParams(has_side_effects=True),
    )(*grads)


def _exchange_chips(sums):
    n = len(sums)

    def body(*refs):
        ins, outs = refs[:n], refs[n:2 * n]
        send_sems, recv_sems = refs[2 * n:]
        x, y, c = _position()
        chips = _other_chips(x, y)
        sends = []
        for a in range(n):
            for k, chip in enumerate(chips):
                cp = _remote(ins[a].at[:, 2 * chip[0] + chip[1]], outs[a].at[:, k],
                             send_sems.at[3 * a + k], recv_sems.at[3 * a + k], (*chip, c))
                cp.start()
                sends.append(cp)
        for cp in sends:
            cp.wait()

    sem = pltpu.SemaphoreType.DMA((3 * n,))
    return pl.pallas_call(
        body, name="exchange_chips", in_specs=[ANY] * n, out_specs=[ANY] * n,
        out_shape=[jax.ShapeDtypeStruct((s.shape[0], 3) + s.shape[2:], s.dtype) for s in sums],
        scratch_shapes=[sem, sem], compiler_params=pltpu.CompilerParams(has_side_effects=True),
    )(*sums)


def _join_halves(bufs):
    n = len(bufs)

    def body(*refs):
        outs = refs[n:2 * n]
        send_sems, recv_sems = refs[2 * n:]
        x, y, c = _position()
        sibling = (x, y, 1 - c)
        copies = []
        for a in range(n):
            h = bufs[a].shape[1] // 2
            mine = outs[a].at[:, pl.ds(c * h, h), :]
            cp = _remote(mine, mine, send_sems.at[a], recv_sems.at[a], sibling)
            cp.start()
            copies.append((cp, a, h))
        for cp, a, h in copies:
            cp.wait_send()
            got = outs[a].at[:, pl.ds((1 - c) * h, h), :]
            _remote(got, got, send_sems.at[a], recv_sems.at[a], sibling).wait_recv()

    sem = pltpu.SemaphoreType.DMA((n,))
    return pl.pallas_call(
        body, name="join_halves", in_specs=[ANY] * n, out_specs=[ANY] * n,
        out_shape=[jax.ShapeDtypeStruct(b.shape, b.dtype) for b in bufs],
        input_output_aliases={a: a for a in range(n)},
        scratch_shapes=[sem, sem], compiler_params=pltpu.CompilerParams(has_side_effects=True),
    )(*bufs)


def _allreduce_small(packs):
    n = len(packs)

    def body(*refs):
        ins, outs, gath = refs[:n], refs[n:2 * n], refs[2 * n:3 * n]
        send_sems, recv_sems = refs[3 * n:]
        x, y, c = _position()
        me, sibling = (x, y, c), (x, y, 1 - c)
        chips = _other_chips(x, y)

        def slot(a, dev):
            return gath[a].at[4 * dev[0] + 2 * dev[1] + dev[2]]

        def copy(a, k, block, to, src=None):
            return _remote(slot(a, block) if src is None else src, slot(a, block),
                           send_sems.at[7 * a + k], recv_sems.at[7 * a + k], to)

        started = []
        for a in range(n):
            slot(a, me)[...] = ins[a][...]
            first = [copy(a, 0, me, sibling, src=ins[a])]
            first += [copy(a, 1 + k, me, (*chip, c), src=ins[a]) for k, chip in enumerate(chips)]
            for cp in first:
                cp.start()
            started += first
        for a in range(n):
            for k, chip in enumerate(chips):
                copy(a, 1 + k, (*chip, c), me).wait_recv()
                cp = copy(a, 4 + k, (*chip, c), sibling)
                cp.start()
                started.append(cp)
        for a in range(n):
            copy(a, 0, sibling, me).wait_recv()
            for k, chip in enumerate(chips):
                copy(a, 4 + k, (*chip, 1 - c), me).wait_recv()
        for cp in started:
            cp.wait_send()
        for a in range(n):
            total = gath[a][0]
            for d in range(1, N_DEV):
                total = total + gath[a][d]
            outs[a][...] = total

    vmem = pl.BlockSpec(memory_space=pltpu.VMEM)
    sem = pltpu.SemaphoreType.DMA((7 * n,))
    return pl.pallas_call(
        body, name="allreduce_small", in_specs=[vmem] * n, out_specs=[vmem] * n,
        out_shape=[jax.ShapeDtypeStruct(p.shape, p.dtype) for p in packs],
        scratch_shapes=[pltpu.VMEM((N_DEV,) + p.shape, p.dtype) for p in packs] + [sem, sem],
        compiler_params=pltpu.CompilerParams(has_side_effects=True, vmem_limit_bytes=VMEM_LIMIT_BYTES),
    )(*packs)


LOSS_ROW = 1040


def _pad_rows(a, rows=8):
    return jnp.concatenate([a, jnp.zeros((rows - a.shape[0], a.shape[1]), a.dtype)], axis=0)

def _adam_small(wide, mid, narrow, params):
    names = ["mix_norm_g", "mlp_norm_g", "final_norm_g", "conv_b", "conv_w", "sgu_norm_g", "sgu_norm_b",
             "pool_w", "pool_scale", "sgu_w", "sgu_b"]
    n = len(names)

    def body(*refs):
        wide_ref, mid_ref, narrow_ref = refs[:3]
        wmv = refs[3:3 + 3 * n]
        outs = refs[3 + 3 * n:]
        x, y, _ = _position()
        q = 2 * x + y

        def my_quarter(rows):
            parts = [rows[:, s * TILE:(s + 1) * TILE] for s in range(N_CHIP)]
            return jnp.where(q == 0, parts[0], jnp.where(q == 1, parts[1], jnp.where(q == 2, parts[2], parts[3])))

        def tiles(first_row):
            return [((0, g), narrow_ref[first_row + g * TILE:first_row + (g + 1) * TILE, :]) for g in range(4)]

        grads = {
            "mix_norm_g": [((), wide_ref[0:2, :])],
            "mlp_norm_g": [((), wide_ref[8:10, :])],
            "final_norm_g": [((), wide_ref[16:17, :])],
            "conv_b": [((), mid_ref[0:1, :])],
            "conv_w": [((0,), my_quarter(mid_ref[8:11, :]))],
            "sgu_norm_g": [((), my_quarter(mid_ref[16:17, :]))],
            "sgu_norm_b": [((), my_quarter(mid_ref[24:25, :]))],
            "pool_w": tiles(0),
            "sgu_w": tiles(512),
            "pool_scale": [((0,), narrow_ref[1024:1028, :])],
            "sgu_b": [((0,), narrow_ref[1032:1036, :])],
        }
        for i, name in enumerate(names):
            w_ref, m_ref, v_ref = wmv[3 * i:3 * i + 3]
            for lead, g in grads[name]:
                idx = lead + (slice(None), slice(None))
                delta, m_new, v_new = _adam_math(w_ref[idx], g, m_ref[idx], v_ref[idx])
                outs[4 * i][idx] = g
                outs[4 * i + 1][idx] = delta
                outs[4 * i + 2][idx] = m_new
                outs[4 * i + 3][idx] = v_new

    vmem = pl.BlockSpec(memory_space=pltpu.VMEM)
    args, out_shape = [wide, mid, narrow], []
    for name in names:
        w, m, v = params[name]
        args += [w, m, v]
        out_shape += [jax.ShapeDtypeStruct(w.shape, F32)] * 4
    res = pl.pallas_call(
        body, name="adam_small", in_specs=[vmem] * len(args), out_specs=[vmem] * len(out_shape),
        out_shape=out_shape, compiler_params=pltpu.CompilerParams(vmem_limit_bytes=VMEM_LIMIT_BYTES),
    )(*args)
    return {name: res[4 * i:4 * i + 4] for i, name in enumerate(names)}


def _reduce_big(grads, pos):
    got = _swap_halves(grads)
    sums = [_pair_sum(a, b, pos, name=f"pair_sum_{i}") for i, (a, b) in enumerate(zip(grads, got))]
    landed = _exchange_chips(sums)
    halves = [_chip_sum(s, r, pos, name=f"chip_sum_{i}") for i, (s, r) in enumerate(zip(sums, landed))]
    return _join_halves(halves)


def kernel(x, mix_norm_g, mlp_norm_g, ab_w_in, pool_w, pool_scale, conv_w, conv_b, ab_w_out, cd_w_in, sgu_norm_g, sgu_norm_b, sgu_w, sgu_b, cd_w_out, mlp_w1, mlp_w2, final_norm_g, loss_target, m_mix_norm_g, m_mlp_norm_g, m_ab_w_in, m_pool_w, m_pool_scale, m_conv_w, m_conv_b, m_ab_w_out, m_cd_w_in, m_sgu_norm_g, m_sgu_norm_b, m_sgu_w, m_sgu_b, m_cd_w_out, m_mlp_w1, m_mlp_w2, m_final_norm_g, v_mix_norm_g, v_mlp_norm_g, v_ab_w_in, v_pool_w, v_pool_scale, v_conv_w, v_conv_b, v_ab_w_out, v_cd_w_in, v_sgu_norm_g, v_sgu_norm_b, v_sgu_w, v_sgu_b, v_cd_w_out, v_mlp_w1, v_mlp_w2, v_final_norm_g):
    nseq, t_len, d = x.shape
    m_tok = nseq * t_len
    h0 = x.reshape(m_tok, d)
    target = loss_target.reshape(m_tok, d)

    x_idx, y_idx = lax.axis_index("x"), lax.axis_index("y")
    q_idx = 2 * x_idx + y_idx
    pos = jnp.stack([q_idx, lax.axis_index("c")]).astype(jnp.int32)
    w_ab_in, w_ab_out, w_cd_in, w_cd_out, w_1, w_2 = _gather_weights(
        [_cast_place(w, pos, name=f"cast_place_{i}")
         for i, w in enumerate((ab_w_in, ab_w_out, cd_w_in, cd_w_out, mlp_w1, mlp_w2))])
    w_ab_out = w_ab_out.reshape(1, 1, -1, d)
    w_cd_out = w_cd_out.reshape(1, 1, -1, d)
    w_2 = w_2.reshape(2, 1, -1, d)

    pool_w3, pool_scale3 = pool_w[0], pool_scale[0].reshape(4, 1, TILE)
    sgu_w3 = sgu_w[0]
    sgu_w3_t = jnp.swapaxes(sgu_w3, 1, 2)
    sgu_bias_tile = jnp.broadcast_to(sgu_b[0][:, :, None], (4, TILE, TILE))
    conv_w2, conv_b2 = conv_w[0], conv_b
    def place_quarter(v):
        return lax.dynamic_update_slice(jnp.zeros((v.shape[0], 4 * TILE), F32), v, (0, q_idx * TILE))

    sharded_small = jnp.concatenate(
        [place_quarter(conv_w[0]), place_quarter(sgu_norm_g), place_quarter(sgu_norm_b),
         jnp.zeros((3, 4 * TILE), F32)], axis=0)
    sharded_small, = _allreduce_small([sharded_small])
    sharded_small = sharded_small * 0.5
    conv_w_full = sharded_small[0:3]
    sgu_g_full = sharded_small[3:4]
    sgu_b_full = sharded_small[4:5]

    xn0 = _rms_fwd(h0, mix_norm_g[0:1], name="rms_fwd_mix0")
    p_ab = _mm_nn(xn0, w_ab_in, 0, out_dtype=BF16, name="ab_in_proj")
    mix0 = _ab_fwd(p_ab, pool_w3, pool_scale3, conv_w_full, conv_b2, nseq, t_len)
    h1 = _mm_nn(mix0, w_ab_out, 0, out_dtype=F32, name="ab_out_proj", epilogue="residual", extra=h0)
    hn0 = _rms_fwd(h1, mlp_norm_g[0:1], name="rms_fwd_mlp0")
    act0 = _mm_nn(hn0, w_1, 0, out_dtype=BF16, name="mlp0_up", epilogue="relu2")
    h2 = _mm_nn(act0, w_2, 0, out_dtype=F32, name="mlp0_down", epilogue="residual", extra=h1)

    xn1 = _rms_fwd(h2, mix_norm_g[1:2], name="rms_fwd_mix1")
    p_cd = _mm_nn(xn1, w_cd_in, 0, out_dtype=BF16, name="cd_in_proj")
    c_out = _sgu_fwd(p_cd, sgu_g_full, sgu_b_full, sgu_w3, sgu_bias_tile)
    d_out, ltot = _sb_fwd(p_cd, nseq, t_len)
    mix1 = jnp.concatenate([c_out, d_out], axis=1)
    h3 = _mm_nn(mix1, w_cd_out, 0, out_dtype=F32, name="cd_out_proj", epilogue="residual", extra=h2)
    hn1 = _rms_fwd(h3, mlp_norm_g[1:2], name="rms_fwd_mlp1")
    act1 = _mm_nn(hn1, w_1, 1, out_dtype=BF16, name="mlp1_up", epilogue="relu2")
    h4 = _mm_nn(act1, w_2, 1, out_dtype=F32, name="mlp1_down", epilogue="residual", extra=h3)

    dh4, dg_final, loss_tile = _final_loss(h4, final_norm_g.reshape(1, d), target)

    def mlp_bwd(dh_out, h_in, hn, act, layer, tag):
        dz = _mm_nt(dh_out.astype(BF16), w_2, layer, out_dtype=BF16, name=f"mlp{tag}_down_bwd",
                    epilogue="relu2_bwd", extra=act)
        g_w2 = _mm_tn(act, dh_out.astype(BF16), 1, name=f"mlp{tag}_down_wgrad")
        g_w1 = _mm_tn(hn, dz, N_CHIP, name=f"mlp{tag}_up_wgrad")
        dhn = _mm_nt(dz, w_1, layer, out_dtype=F32, name=f"mlp{tag}_up_bwd")
        dh_in, dg = _rms_bwd(h_in, mlp_norm_g[layer:layer + 1], dhn, dh_out, name=f"rms_bwd_mlp{tag}")
        return dh_in, dg, g_w1, g_w2

    dh3, dg_mlp1, g_w1_1, g_w2_1 = mlp_bwd(dh4, h3, hn1, act1, 1, "1")

    dh3_bf = dh3.astype(BF16)
    dmix1 = _mm_nt(dh3_bf, w_cd_out, 0, out_dtype=BF16, name="cd_out_bwd")
    g_cd_out = _mm_tn(mix1, dh3_bf, 1, name="cd_out_wgrad")
    du, dv, dsgu_w, dsgu_bs, dsgu_g, dsgu_b = _sgu_bwd(p_cd, dmix1, sgu_g_full, sgu_b_full, sgu_w3, sgu_w3_t,
                                                      sgu_bias_tile)
    dq, dk, dvv = _sb_bwd(p_cd, dmix1, ltot, nseq, t_len)
    dp_cd = jnp.concatenate([du, dv, dq, dk, dvv], axis=1)
    g_cd_in = _mm_tn(xn1, dp_cd, N_CHIP, name="cd_in_wgrad")
    dxn1 = _mm_nt(dp_cd, w_cd_in, 0, out_dtype=F32, name="cd_in_bwd")
    dh2, dg_mix1 = _rms_bwd(h2, mix_norm_g[1:2], dxn1, dh3, name="rms_bwd_mix1")

    dh1, dg_mlp0, g_w1_0, g_w2_0 = mlp_bwd(dh2, h1, hn0, act0, 0, "0")

    dh1_bf = dh1.astype(BF16)
    dmix0 = _mm_nt(dh1_bf, w_ab_out, 0, out_dtype=BF16, name="ab_out_bwd")
    g_ab_out = _mm_tn(mix0, dh1_bf, 1, name="ab_out_wgrad")
    da, dxb, dgb, dgc, dpool_w, dpool_scale, dconv_w, dconv_b = _ab_bwd(
        p_ab, dmix0, pool_w3, pool_scale3, conv_w_full, conv_b2, nseq, t_len)
    dp_ab = jnp.concatenate([da, dxb, dgb, dgc], axis=1)
    g_ab_in = _mm_tn(xn0, dp_ab, N_CHIP, name="ab_in_wgrad")
    dxn0 = _mm_nt(dp_ab, w_ab_in, 0, out_dtype=F32, name="ab_in_bwd")
    grad_x, dg_mix0 = _rms_bwd(h0, mix_norm_g[0:1], dxn0, dh1, name="rms_bwd_mix0")

    def as_pieces(g):
        return g.reshape(1, N_CHIP, -1, g.shape[-1]) if g.shape[1] == 1 else g

    big = [as_pieces(g) for g in (g_ab_in, g_ab_out, g_cd_in, g_cd_out, g_w1_0, g_w1_1, g_w2_0, g_w2_1)]
    r_ab_in, r_ab_out, r_cd_in, r_cd_out, r_w1_0, r_w1_1, r_w2_0, r_w2_1 = _reduce_big(big, pos)

    big_out = {
        "ab_w_in": _adam_big(ab_w_in, m_ab_w_in, v_ab_w_in, [r_ab_in], name="adam_ab_w_in"),
        "ab_w_out": _adam_big(ab_w_out, m_ab_w_out, v_ab_w_out, [r_ab_out], name="adam_ab_w_out"),
        "cd_w_in": _adam_big(cd_w_in, m_cd_w_in, v_cd_w_in, [r_cd_in], name="adam_cd_w_in"),
        "cd_w_out": _adam_big(cd_w_out, m_cd_w_out, v_cd_w_out, [r_cd_out], name="adam_cd_w_out"),
        "mlp_w1": _adam_big(mlp_w1, m_mlp_w1, v_mlp_w1, [r_w1_0, r_w1_1], name="adam_mlp_w1"),
        "mlp_w2": _adam_big(mlp_w2, m_mlp_w2, v_mlp_w2, [r_w2_0, r_w2_1], name="adam_mlp_w2"),
    }

    wide = jnp.concatenate([_pad_rows(jnp.concatenate([dg_mix0, dg_mix1], axis=0)),
                            _pad_rows(jnp.concatenate([dg_mlp0, dg_mlp1], axis=0)), _pad_rows(dg_final)], axis=0)
    mid = jnp.concatenate([_pad_rows(dconv_b), _pad_rows(dconv_w), _pad_rows(dsgu_g), _pad_rows(dsgu_b)], axis=0)
    narrow = jnp.concatenate(
        [dpool_w.reshape(4 * TILE, TILE), dsgu_w.reshape(4 * TILE, TILE), _pad_rows(dpool_scale.reshape(4, TILE)),
         _pad_rows(dsgu_bs[:, :, 0]), loss_tile], axis=0)
    wide, mid, narrow = _allreduce_small([wide, mid, narrow])
    small_out = _adam_small(wide, mid, narrow, {
        "mix_norm_g": (mix_norm_g, m_mix_norm_g, v_mix_norm_g),
        "mlp_norm_g": (mlp_norm_g, m_mlp_norm_g, v_mlp_norm_g),
        "final_norm_g": tuple(a.reshape(1, d) for a in (final_norm_g, m_final_norm_g, v_final_norm_g)),
        "conv_b": (conv_b, m_conv_b, v_conv_b),
        "conv_w": (conv_w, m_conv_w, v_conv_w),
        "sgu_norm_g": (sgu_norm_g, m_sgu_norm_g, v_sgu_norm_g),
        "sgu_norm_b": (sgu_norm_b, m_sgu_norm_b, v_sgu_norm_b),
        "pool_w": (pool_w, m_pool_w, v_pool_w),
        "pool_scale": (pool_scale, m_pool_scale, v_pool_scale),
        "sgu_w": (sgu_w, m_sgu_w, v_sgu_w),
        "sgu_b": (sgu_b, m_sgu_b, v_sgu_b),
    })
    small_out["final_norm_g"] = [a.reshape(d) for a in small_out["final_norm_g"]]

    order = ["mix_norm_g", "mlp_norm_g", "ab_w_in", "pool_w", "pool_scale", "conv_w", "conv_b", "ab_w_out",
             "cd_w_in", "sgu_norm_g", "sgu_norm_b", "sgu_w", "sgu_b", "cd_w_out", "mlp_w1", "mlp_w2",
             "final_norm_g"]
    both = {**big_out, **small_out}
    loss = narrow[LOSS_ROW, 0]
    outs = [loss, grad_x.reshape(nseq, t_len, d)]
    for kind in range(4):
        outs += [both[name][kind] for name in order]
    return tuple(outs)
```

```python
import math

import jax
import jax.numpy as jnp
from jax import lax
from jax.experimental import pallas as pl
from jax.experimental.pallas import tpu as pltpu

F32 = jnp.float32
BF16 = jnp.bfloat16
MESH = pl.DeviceIdType.MESH

D_MODEL = 1024
EPS = 1e-6
TILE = 128
N_CHIP = 4
N_DEV = 8
VMEM_LIMIT_BYTES = 56 * 1024 * 1024

ADAM_LR = 0.001
ADAM_B1 = 0.9
ADAM_B2 = 0.999
ADAM_EPS = 1e-08
ADAM_WD = 0.01
ADAM_STEP = 10

NT_DIMS = (((1,), (1,)), ((), ()))
TN_DIMS = (((0,), (0,)), ((), ()))


def _params(sem=None):
    return pltpu.CompilerParams(dimension_semantics=sem, vmem_limit_bytes=VMEM_LIMIT_BYTES)


def _mm_nn(a, b4, layer, *, out_dtype, name, epilogue=None, extra=None, tm=1024, tk=1024):
    m, k_dim = a.shape
    _, s_dim, kb, n = b4.shape
    assert kb == k_dim
    tn = min(n, 1024)
    assert m % tm == 0 and k_dim % tk == 0 and n % tn == 0
    nk, npb = k_dim // tk, n // tn
    grid = (m // tm, s_dim * npb, nk)

    def body(*refs):
        if extra is None:
            a_ref, b_ref, o_ref, *scr = refs
        else:
            a_ref, b_ref, e_ref, o_ref, *scr = refs

        def finish(acc):
            if epilogue == "relu2":
                r = jnp.maximum(acc, 0.0)
                acc = r * r
            elif epilogue == "residual":
                acc = acc + e_ref[...]
            o_ref[...] = acc.astype(out_dtype)

        part = jnp.dot(a_ref[...], b_ref[...], preferred_element_type=F32)
        if nk == 1:
            finish(part)
        else:
            acc_ref, = scr
            kk = pl.program_id(2)

            @pl.when(kk == 0)
            def _():
                acc_ref[...] = part

            @pl.when(kk > 0)
            def _():
                acc_ref[...] += part

            @pl.when(kk == nk - 1)
            def _():
                finish(acc_ref[...])

    in_specs = [
        pl.BlockSpec((tm, tk), lambda i, j, kk: (i, kk)),
        pl.BlockSpec((None, None, tk, tn), lambda i, j, kk: (layer, j // npb, kk, j % npb)),
    ]
    args = [a, b4]
    if extra is not None:
        in_specs.append(pl.BlockSpec((tm, tn), lambda i, j, kk: (i, j)))
        args.append(extra)
    return pl.pallas_call(
        body, name=name, grid=grid, in_specs=in_specs,
        out_specs=pl.BlockSpec((tm, tn), lambda i, j, kk: (i, j)),
        out_shape=jax.ShapeDtypeStruct((m, s_dim * n), out_dtype),
        scratch_shapes=[] if nk == 1 else [pltpu.VMEM((tm, tn), F32)],
        compiler_params=_params(("parallel", "parallel", "arbitrary")),
    )(*args)


def _mm_nt(a, b4, layer, *, out_dtype, name, epilogue=None, extra=None, tm=1024, tn=1024):
    m, k_dim = a.shape
    _, s_dim, n_out, n = b4.shape
    assert k_dim == s_dim * n
    tk = min(n, 1024)
    tn = min(tn, n_out)
    assert m % tm == 0 and n_out % tn == 0 and n % tk == 0
    kpb = n // tk
    nk = s_dim * kpb
    grid = (m // tm, n_out // tn, nk)

    def body(*refs):
        if extra is None:
            a_ref, b_ref, o_ref, *scr = refs
        else:
            a_ref, b_ref, e_ref, o_ref, *scr = refs

        def finish(acc):
            if epilogue == "relu2_bwd":
                acc = acc * (2.0 * jnp.sqrt(e_ref[...].astype(F32)))
            o_ref[...] = acc.astype(out_dtype)

        part = lax.dot_general(a_ref[...], b_ref[...], NT_DIMS, preferred_element_type=F32)
        if nk == 1:
            finish(part)
        else:
            acc_ref, = scr
            kk = pl.program_id(2)

            @pl.when(kk == 0)
            def _():
                acc_ref[...] = part

            @pl.when(kk > 0)
            def _():
                acc_ref[...] += part

            @pl.when(kk == nk - 1)
            def _():
                finish(acc_ref[...])

    in_specs = [
        pl.BlockSpec((tm, tk), lambda i, j, kk: (i, kk)),
        pl.BlockSpec((None, None, tn, tk), lambda i, j, kk: (layer, kk // kpb, j, kk % kpb)),
    ]
    args = [a, b4]
    if extra is not None:
        in_specs.append(pl.BlockSpec((tm, tn), lambda i, j, kk: (i, j)))
        args.append(extra)
    return pl.pallas_call(
        body, name=name, grid=grid, in_specs=in_specs,
        out_specs=pl.BlockSpec((tm, tn), lambda i, j, kk: (i, j)),
        out_shape=jax.ShapeDtypeStruct((m, n_out), out_dtype),
        scratch_shapes=[] if nk == 1 else [pltpu.VMEM((tm, tn), F32)],
        compiler_params=_params(("parallel", "parallel", "arbitrary")),
    )(*args)


def _mm_tn(a, b, s_dim, *, name, tm=1024, t1=1024):
    m, k1 = a.shape
    mb, n_all = b.shape
    assert mb == m and n_all % s_dim == 0
    n = n_all // s_dim
    tn = min(n, 1024)
    t1 = min(t1, k1)
    assert m % tm == 0 and k1 % t1 == 0 and n % tn == 0
    npb = n // tn
    nk = m // tm
    grid = (k1 // t1, s_dim * npb, nk)

    def body(a_ref, b_ref, o_ref, acc_ref):
        kk = pl.program_id(2)
        part = lax.dot_general(a_ref[...], b_ref[...], TN_DIMS, preferred_element_type=F32)

        @pl.when(kk == 0)
        def _():
            acc_ref[...] = part

        @pl.when(kk > 0)
        def _():
            acc_ref[...] += part

        @pl.when(kk == nk - 1)
        def _():
            o_ref[...] = acc_ref[...].astype(BF16)

    return pl.pallas_call(
        body, name=name, grid=grid,
        in_specs=[pl.BlockSpec((tm, t1), lambda i, j, kk: (kk, i)),
                  pl.BlockSpec((tm, tn), lambda i, j, kk: (kk, j))],
        out_specs=pl.BlockSpec((None, None, t1, tn), lambda i, j, kk: (0, j // npb, i, j % npb)),
        out_shape=jax.ShapeDtypeStruct((1, s_dim, k1, n), BF16),
        scratch_shapes=[pltpu.VMEM((t1, tn), F32)],
        compiler_params=_params(("parallel", "parallel", "arbitrary")),
    )(a, b)


ROW_TILE = 512


def _rms_fwd(h, g, *, name):
    m, d = h.shape

    def body(h_ref, g_ref, o_ref):
        hv = h_ref[...]
        rstd = lax.rsqrt(jnp.mean(hv * hv, axis=-1, keepdims=True) + EPS)
        o_ref[...] = (hv * rstd * g_ref[...]).astype(BF16)

    return pl.pallas_call(
        body, name=name, grid=(m // ROW_TILE,),
        in_specs=[pl.BlockSpec((ROW_TILE, d), lambda i: (i, 0)), pl.BlockSpec((1, d), lambda i: (0, 0))],
        out_specs=pl.BlockSpec((ROW_TILE, d), lambda i: (i, 0)),
        out_shape=jax.ShapeDtypeStruct((m, d), BF16),
        compiler_params=_params(("parallel",)),
    )(h, g)


def _rms_bwd(h, g, dxn, dres, *, name):
    m, d = h.shape

    def body(h_ref, g_ref, dxn_ref, dres_ref, dh_ref, dhb_ref, dg_ref):
        hv = h_ref[...]
        rstd = lax.rsqrt(jnp.mean(hv * hv, axis=-1, keepdims=True) + EPS)
        xhat = hv * rstd
        dy = dxn_ref[...]
        dxhat = dy * g_ref[...]
        dh = dres_ref[...] + rstd * (dxhat - xhat * jnp.mean(dxhat * xhat, axis=-1, keepdims=True))
        dh_ref[...] = dh
        dhb_ref[...] = dh.astype(BF16)
        part = jnp.sum(dy * xhat, axis=0, keepdims=True)

        @pl.when(pl.program_id(0) == 0)
        def _():
            dg_ref[...] = part

        @pl.when(pl.program_id(0) > 0)
        def _():
            dg_ref[...] += part

    row = pl.BlockSpec((ROW_TILE, d), lambda i: (i, 0))
    vec = pl.BlockSpec((1, d), lambda i: (0, 0))
    return pl.pallas_call(
        body, name=name, grid=(m // ROW_TILE,),
        in_specs=[row, vec, row, row], out_specs=[row, row, vec],
        out_shape=[jax.ShapeDtypeStruct((m, d), F32), jax.ShapeDtypeStruct((m, d), BF16),
                   jax.ShapeDtypeStruct((1, d), F32)],
        compiler_params=_params(("arbitrary",)),
    )(h, g, dxn, dres)


def _final_loss(h, g, target):
    m, d = h.shape

    def body(h_ref, g_ref, t_ref, dh_ref, dhb_ref, dg_ref, loss_ref):
        hv = h_ref[...]
        gv = g_ref[...]
        rstd = lax.rsqrt(jnp.mean(hv * hv, axis=-1, keepdims=True) + EPS)
        xhat = hv * rstd
        err = xhat * gv - t_ref[...]
        dy = err * (1.0 / d)
        dxhat = dy * gv
        dh = rstd * (dxhat - xhat * jnp.mean(dxhat * xhat, axis=-1, keepdims=True))
        dh_ref[...] = dh
        dhb_ref[...] = dh.astype(BF16)
        dg_part = jnp.sum(dy * xhat, axis=0, keepdims=True)
        sq = jnp.sum(jnp.sum(err * err, axis=1, keepdims=True), axis=0, keepdims=True) * (0.5 / d)
        loss_part = jnp.broadcast_to(sq, (8, TILE))

        @pl.when(pl.program_id(0) == 0)
        def _():
            dg_ref[...] = dg_part
            loss_ref[...] = loss_part

        @pl.when(pl.program_id(0) > 0)
        def _():
            dg_ref[...] += dg_part
            loss_ref[...] += loss_part

    row = pl.BlockSpec((ROW_TILE, d), lambda i: (i, 0))
    vec = pl.BlockSpec((1, d), lambda i: (0, 0))
    return pl.pallas_call(
        body, name="final_loss", grid=(m // ROW_TILE,),
        in_specs=[row, vec, row],
        out_specs=[row, row, vec, pl.BlockSpec((8, TILE), lambda i: (0, 0))],
        out_shape=[jax.ShapeDtypeStruct((m, d), F32), jax.ShapeDtypeStruct((m, d), BF16),
                   jax.ShapeDtypeStruct((1, d), F32), jax.ShapeDtypeStruct((8, TILE), F32)],
        compiler_params=_params(("arbitrary",)),
    )(h, g, target)


def _shift_down(x, s, t_idx):
    return jnp.where(t_idx >= s, pltpu.roll(x, s, 0), 0.0)


def _shift_up(x, s, t_idx, t_len):
    return jnp.where(t_idx < t_len - s, pltpu.roll(x, t_len - s, 0), 0.0)


def _pool_select(group, s2, s4, s8, s16):
    return jnp.where(group == 0, s2, jnp.where(group == 1, s4, jnp.where(group == 2, s8, s16)))


def _pool_count(group, t_idx):
    win = jnp.left_shift(2, group)
    return jnp.minimum(t_idx + 1, win).astype(F32)


def _pool_fwd_math(a, group, t_idx):
    s2 = a + _shift_down(a, 1, t_idx)
    s4 = s2 + _shift_down(s2, 2, t_idx)
    s8 = s4 + _shift_down(s4, 4, t_idx)
    s16 = s8 + _shift_down(s8, 8, t_idx)
    return _pool_select(group, s2, s4, s8, s16) / _pool_count(group, t_idx) - a


def _pool_bwd_math(dpooled, group, t_idx, t_len):
    e = dpooled / _pool_count(group, t_idx)
    s2 = e + _shift_up(e, 1, t_idx, t_len)
    s4 = s2 + _shift_up(s2, 2, t_idx, t_len)
    s8 = s4 + _shift_up(s4, 4, t_idx, t_len)
    s16 = s8 + _shift_up(s8, 8, t_idx, t_len)
    return _pool_select(group, s2, s4, s8, s16) - dpooled


def _conv_fwd_math(c, w_ref, b_ref, t_idx):
    return (w_ref[0:1, :] * _shift_down(c, 2, t_idx) + w_ref[1:2, :] * _shift_down(c, 1, t_idx)
            + w_ref[2:3, :] * c + b_ref[...])


def _ab_fwd(p, pool_w, pool_scale, conv_w, conv_b, nseq, t_len):
    m = p.shape[0]
    ng = 4

    def body(a_ref, xb_ref, gb_ref, gc_ref, pw_ref, ps_ref, cw_ref, cb_ref, o_ref):
        j = pl.program_id(1)
        t_idx = lax.broadcasted_iota(jnp.int32, (t_len, TILE), 0)

        @pl.when(j < ng)
        def _():
            pooled = _pool_fwd_math(a_ref[...].astype(F32), j, t_idx)
            mixed = jnp.dot(pooled.astype(BF16), pw_ref[...].astype(BF16), preferred_element_type=F32)
            o_ref[...] = (mixed * ps_ref[...]).astype(BF16)

        @pl.when(j >= ng)
        def _():
            c = gc_ref[...].astype(F32) * xb_ref[...].astype(F32)
            y = _conv_fwd_math(c, cw_ref, cb_ref, t_idx)
            o_ref[...] = (gb_ref[...].astype(F32) * y).astype(BF16)

    def pool_j(j):
        return jnp.minimum(j, ng - 1)

    def conv_j(j):
        return jnp.maximum(j - ng, 0)

    in_specs = [
        pl.BlockSpec((t_len, TILE), lambda s, j: (s, pool_j(j))),
        pl.BlockSpec((t_len, TILE), lambda s, j: (s, ng + conv_j(j))),
        pl.BlockSpec((t_len, TILE), lambda s, j: (s, 2 * ng + conv_j(j))),
        pl.BlockSpec((t_len, TILE), lambda s, j: (s, 3 * ng + conv_j(j))),
        pl.BlockSpec((None, TILE, TILE), lambda s, j: (pool_j(j), 0, 0)),
        pl.BlockSpec((None, 1, TILE), lambda s, j: (pool_j(j), 0, 0)),
        pl.BlockSpec((3, TILE), lambda s, j: (0, conv_j(j))),
        pl.BlockSpec((1, TILE), lambda s, j: (0, conv_j(j))),
    ]
    return pl.pallas_call(
        body, name="ab_mixer_fwd", grid=(nseq, 2 * ng), in_specs=in_specs,
        out_specs=pl.BlockSpec((t_len, TILE), lambda s, j: (s, j)),
        out_shape=jax.ShapeDtypeStruct((m, 2 * ng * TILE), BF16),
        compiler_params=_params(("parallel", "arbitrary")),
    )(p, p, p, p, pool_w, pool_scale, conv_w, conv_b)


def _ab_bwd(p, dmix, pool_w, pool_scale, conv_w, conv_b, nseq, t_len):
    m = p.shape[0]
    ng = 4

    def body(a_ref, xb_ref, gb_ref, gc_ref, dma_ref, dmb_ref, pw_ref, ps_ref, cw_ref, cb_ref,
             da_ref, dxb_ref, dgb_ref, dgc_ref, dpw_ref, dps_ref, dcw_ref, dcb_ref):
        j = pl.program_id(0)
        first = pl.program_id(1) == 0
        t_idx = lax.broadcasted_iota(jnp.int32, (t_len, TILE), 0)

        pooled = _pool_fwd_math(a_ref[...].astype(F32), j, t_idx).astype(BF16)
        w_bf = pw_ref[...].astype(BF16)
        mixed = jnp.dot(pooled, w_bf, preferred_element_type=F32)
        dm = dma_ref[...].astype(F32)
        dps = jnp.sum(dm * mixed, axis=0, keepdims=True)
        dmixed = (dm * ps_ref[...]).astype(BF16)
        dpw = lax.dot_general(pooled, dmixed, TN_DIMS, preferred_element_type=F32)
        dpooled = lax.dot_general(dmixed, w_bf, NT_DIMS, preferred_element_type=F32)
        da_ref[...] = _pool_bwd_math(dpooled, j, t_idx, t_len).astype(BF16)

        xb = xb_ref[...].astype(F32)
        gb = gb_ref[...].astype(F32)
        gc = gc_ref[...].astype(F32)
        d = dmb_ref[...].astype(F32)
        c = gc * xb
        c1 = _shift_down(c, 1, t_idx)
        c2 = _shift_down(c, 2, t_idx)
        y = cw_ref[0:1, :] * c2 + cw_ref[1:2, :] * c1 + cw_ref[2:3, :] * c + cb_ref[...]
        dgb_ref[...] = (d * y).astype(BF16)
        dy = d * gb
        dc = (cw_ref[2:3, :] * dy + cw_ref[1:2, :] * _shift_up(dy, 1, t_idx, t_len)
              + cw_ref[0:1, :] * _shift_up(dy, 2, t_idx, t_len))
        dgc_ref[...] = (dc * xb).astype(BF16)
        dxb_ref[...] = (dc * gc).astype(BF16)
        dcw = jnp.concatenate([jnp.sum(dy * c2, axis=0, keepdims=True),
                               jnp.sum(dy * c1, axis=0, keepdims=True),
                               jnp.sum(dy * c, axis=0, keepdims=True)], axis=0)
        dcb = jnp.sum(dy, axis=0, keepdims=True)

        @pl.when(first)
        def _():
            dpw_ref[...] = dpw
            dps_ref[...] = dps
            dcw_ref[...] = dcw
            dcb_ref[...] = dcb

        @pl.when(jnp.logical_not(first))
        def _():
            dpw_ref[...] += dpw
            dps_ref[...] += dps
            dcw_ref[...] += dcw
            dcb_ref[...] += dcb

    def col(k):
        return pl.BlockSpec((t_len, TILE), lambda j, s: (s, k * ng + j))

    in_specs = [
        col(0), col(1), col(2), col(3), col(0), col(1),
        pl.BlockSpec((None, TILE, TILE), lambda j, s: (j, 0, 0)),
        pl.BlockSpec((None, 1, TILE), lambda j, s: (j, 0, 0)),
        pl.BlockSpec((3, TILE), lambda j, s: (0, j)),
        pl.BlockSpec((1, TILE), lambda j, s: (0, j)),
    ]
    piece = pl.BlockSpec((t_len, TILE), lambda j, s: (s, j))
    out_specs = [
        piece, piece, piece, piece,
        pl.BlockSpec((None, TILE, TILE), lambda j, s: (j, 0, 0)),
        pl.BlockSpec((None, 1, TILE), lambda j, s: (j, 0, 0)),
        pl.BlockSpec((3, TILE), lambda j, s: (0, j)),
        pl.BlockSpec((1, TILE), lambda j, s: (0, j)),
    ]
    w = ng * TILE
    out_shape = [jax.ShapeDtypeStruct((m, w), BF16)] * 4 + [
        jax.ShapeDtypeStruct((ng, TILE, TILE), F32), jax.ShapeDtypeStruct((ng, 1, TILE), F32),
        jax.ShapeDtypeStruct((3, w), F32), jax.ShapeDtypeStruct((1, w), F32)]
    return pl.pallas_call(
        body, name="ab_mixer_bwd", grid=(ng, nseq), in_specs=in_specs, out_specs=out_specs,
        out_shape=out_shape, compiler_params=_params(("parallel", "arbitrary")),
    )(p, p, p, p, dmix, dmix, pool_w, pool_scale, conv_w, conv_b)


SGU_ROWS = 512
INV_SQRT2 = 1.0 / math.sqrt(2.0)
INV_SQRT_2PI = 1.0 / math.sqrt(2.0 * math.pi)


def _gelu(x):
    return 0.5 * x * (1.0 + lax.erf(x * INV_SQRT2))


def _gelu_grad(x):
    return 0.5 * (1.0 + lax.erf(x * INV_SQRT2)) + x * (INV_SQRT_2PI * jnp.exp(-0.5 * x * x))


def _causal_tile(transposed=False):
    r = lax.broadcasted_iota(jnp.int32, (TILE, TILE), 0)
    c = lax.broadcasted_iota(jnp.int32, (TILE, TILE), 1)
    return r <= c if transposed else c <= r


def _sgu_norm(v, g_ref, b_ref):
    mu = jnp.mean(v, axis=-1, keepdims=True)
    xc = v - mu
    rstd = lax.rsqrt(jnp.mean(xc * xc, axis=-1, keepdims=True) + EPS)
    xhat = xc * rstd
    return xhat, rstd, xhat * g_ref[...] + b_ref[...]


def _sgu_fwd(p, norm_g, norm_b, w_s, bias_tile):
    m = p.shape[0]
    ng = 4
    width = ng * TILE

    def body(u_ref, v_ref, g_ref, b_ref, w_ref, bias_ref, o_ref):
        u = _gelu(u_ref[...].astype(F32))
        _, _, vln = _sgu_norm(_gelu(v_ref[...].astype(F32)), g_ref, b_ref)
        vln = vln.astype(BF16)
        causal = _causal_tile()
        for g in range(ng):
            cols = slice(g * TILE, (g + 1) * TILE)
            wg = jnp.where(causal, w_ref[g], 0.0).astype(BF16)
            for n in range(SGU_ROWS // TILE):
                rows = slice(n * TILE, (n + 1) * TILE)
                s = jnp.dot(wg, vln[rows, cols], preferred_element_type=F32) + bias_ref[g]
                o_ref[rows, cols] = (u[rows, cols] * s).astype(BF16)

    vec = pl.BlockSpec((1, width), lambda i: (0, 0))
    tiles = pl.BlockSpec((ng, TILE, TILE), lambda i: (0, 0, 0))
    return pl.pallas_call(
        body, name="sgu_fwd", grid=(m // SGU_ROWS,),
        in_specs=[pl.BlockSpec((SGU_ROWS, width), lambda i: (i, 0)),
                  pl.BlockSpec((SGU_ROWS, width), lambda i: (i, 1)), vec, vec, tiles, tiles],
        out_specs=pl.BlockSpec((SGU_ROWS, width), lambda i: (i, 0)),
        out_shape=jax.ShapeDtypeStruct((m, width), BF16),
        compiler_params=_params(("parallel",)),
    )(p, p, norm_g, norm_b, w_s, bias_tile)


def _sgu_bwd(p, dmix, norm_g, norm_b, w_s, w_s_t, bias_tile):
    m = p.shape[0]
    ng = 4
    width = ng * TILE

    def body(u_ref, v_ref, dc_ref, g_ref, b_ref, w_ref, wt_ref, bias_ref,
             du_ref, dv_ref, dw_ref, dbs_ref, dg_ref, db_ref, ds_scr, dvln_scr):
        u_pre = u_ref[...].astype(F32)
        v_pre = v_ref[...].astype(F32)
        u = _gelu(u_pre)
        xhat, rstd, vln = _sgu_norm(_gelu(v_pre), g_ref, b_ref)
        vln = vln.astype(BF16)
        dc = dc_ref[...].astype(F32)
        causal = _causal_tile()
        ones = jnp.ones((TILE, TILE), BF16)
        first = pl.program_id(0) == 0
        for g in range(ng):
            cols = slice(g * TILE, (g + 1) * TILE)
            wg = jnp.where(causal, w_ref[g], 0.0).astype(BF16)
            wgt = jnp.where(_causal_tile(transposed=True), wt_ref[g], 0.0).astype(BF16)
            dw_acc = jnp.zeros((TILE, TILE), F32)
            dbs_acc = jnp.zeros((TILE, TILE), F32)
            for n in range(SGU_ROWS // TILE):
                rows = slice(n * TILE, (n + 1) * TILE)
                vt = vln[rows, cols]
                s = jnp.dot(wg, vt, preferred_element_type=F32) + bias_ref[g]
                ds_scr[rows, cols] = dc[rows, cols] * s
                ds = (dc[rows, cols] * u[rows, cols]).astype(BF16)
                dw_acc += lax.dot_general(ds, vt, NT_DIMS, preferred_element_type=F32)
                dbs_acc += jnp.dot(ds, ones, preferred_element_type=F32)
                dvln_scr[rows, cols] = jnp.dot(wgt, ds, preferred_element_type=F32)
            dw_g = jnp.where(causal, dw_acc, 0.0)

            @pl.when(first)
            def _():
                dw_ref[g] = dw_g
                dbs_ref[g] = dbs_acc

            @pl.when(jnp.logical_not(first))
            def _():
                dw_ref[g] += dw_g
                dbs_ref[g] += dbs_acc

        du_ref[...] = (ds_scr[...] * _gelu_grad(u_pre)).astype(BF16)
        dvln = dvln_scr[...]
        dxhat = dvln * g_ref[...]
        dv = rstd * (dxhat - jnp.mean(dxhat, axis=-1, keepdims=True)
                     - xhat * jnp.mean(dxhat * xhat, axis=-1, keepdims=True))
        dv_ref[...] = (dv * _gelu_grad(v_pre)).astype(BF16)
        dg_part = jnp.sum(dvln * xhat, axis=0, keepdims=True)
        db_part = jnp.sum(dvln, axis=0, keepdims=True)

        @pl.when(first)
        def _():
            dg_ref[...] = dg_part
            db_ref[...] = db_part

        @pl.when(jnp.logical_not(first))
        def _():
            dg_ref[...] += dg_part
            db_ref[...] += db_part

    vec = pl.BlockSpec((1, width), lambda i: (0, 0))
    tiles = pl.BlockSpec((ng, TILE, TILE), lambda i: (0, 0, 0))
    rows0 = pl.BlockSpec((SGU_ROWS, width), lambda i: (i, 0))
    rows1 = pl.BlockSpec((SGU_ROWS, width), lambda i: (i, 1))
    return pl.pallas_call(
        body, name="sgu_bwd", grid=(m // SGU_ROWS,),
        in_specs=[rows0, rows1, rows0, vec, vec, tiles, tiles, tiles],
        out_specs=[rows0, rows0, tiles, tiles, vec, vec],
        out_shape=[jax.ShapeDtypeStruct((m, width), BF16), jax.ShapeDtypeStruct((m, width), BF16),
                   jax.ShapeDtypeStruct((ng, TILE, TILE), F32), jax.ShapeDtypeStruct((ng, TILE, TILE), F32),
                   jax.ShapeDtypeStruct((1, width), F32), jax.ShapeDtypeStruct((1, width), F32)],
        scratch_shapes=[pltpu.VMEM((SGU_ROWS, width), F32), pltpu.VMEM((SGU_ROWS, width), F32)],
        compiler_params=_params(("arbitrary",)),
    )(p, p, dmix, norm_g, norm_b, w_s, w_s_t, bias_tile)


SB_DH = 64
SB_SCALE = 1.0 / math.sqrt(SB_DH)


SB_PASS_BLOCKS = 4


def _sum_matrix(kind):
    j = lax.broadcasted_iota(jnp.int32, (TILE, 2 * TILE), 0)
    s = lax.broadcasted_iota(jnp.int32, (TILE, 2 * TILE), 1)
    tri = {"after": j > s, "upto": j <= s, "before": j < s}[kind]
    return jnp.where(jnp.logical_or(s >= TILE, tri), 1.0, 0.0).astype(BF16)


def _passes(i):
    rem = i % SB_PASS_BLOCKS
    return i // SB_PASS_BLOCKS, rem >= 2, rem % 2 == 1


def _head_lanes(h):
    lane = lax.broadcasted_iota(jnp.int32, (1, TILE), 1)
    return (lane >= h * SB_DH) & (lane < (h + 1) * SB_DH)


def _softplus(z):
    return jnp.maximum(z, 0.0) + jnp.log(1.0 + jnp.exp(-jnp.abs(z)))


def _sb_fwd(p, nseq, t_len):
    m = p.shape[0]
    nb = t_len // TILE
    npair = 4

    def body(q_ref, k_ref, v_ref, o_ref, lt_ref, kh_ref, vh_ref):
        for h in range(2):
            keep = _head_lanes(h)
            kh_ref[h] = jnp.where(keep, k_ref[...], 0).astype(BF16)
            vh_ref[h] = jnp.where(keep, v_ref[...], 0).astype(BF16)
        summat = _sum_matrix("after")
        r = lax.broadcasted_iota(jnp.int32, (TILE, TILE), 0)
        c = lax.broadcasted_iota(jnp.int32, (TILE, TILE), 1)
        strict = c < r

        def tile(h, q, row0, nblk, diag, carry, acc):
            rows = pl.ds(row0, nblk * TILE)
            kj = kh_ref[h, rows, :]
            vj = vh_ref[h, rows, :]
            z = lax.dot_general(q, kj, NT_DIMS, preferred_element_type=F32) * SB_SCALE
            sp = _softplus(z)
            logkeep = jnp.where(strict, -sp, 0.0) if diag else -sp
            after = [None] * nblk
            for b in reversed(range(nblk)):
                sums = jnp.dot(logkeep[:, b * TILE:(b + 1) * TILE].astype(BF16), summat,
                               preferred_element_type=F32)
                after[b] = sums[:, :TILE] + carry
                carry = carry + sums[:, TILE:]
            w = jnp.exp(z - sp + (after[0] if nblk == 1 else jnp.concatenate(after, axis=1)))
            if diag:
                w = jnp.where(strict, w, 0.0)
            return carry, acc + jnp.dot(w.astype(BF16), vj, preferred_element_type=F32)

        def both_heads(q, row0, nblk, diag, state):
            out = []
            for h in range(2):
                out += list(tile(h, q, row0, nblk, diag, state[2 * h], state[2 * h + 1]))
            return tuple(out)

        def q_block(i, _):
            r0 = pl.multiple_of(i * TILE, TILE)
            q = q_ref[pl.ds(r0, TILE), :]
            zero = jnp.zeros((TILE, TILE), F32)
            state = both_heads(q, r0, 1, True, (zero,) * 4)
            full, two, one = _passes(i)
            nbp = SB_PASS_BLOCKS
            state = lax.fori_loop(
                0, full,
                lambda jj, st: both_heads(q, pl.multiple_of((i - nbp * (jj + 1)) * TILE, TILE), nbp, False, st),
                state)
            state = lax.cond(
                two, lambda st: both_heads(q, pl.multiple_of((i % 2) * TILE, TILE), 2, False, st), lambda st: st, state)
            state = lax.cond(one, lambda st: both_heads(q, 0, 1, False, st), lambda st: st, state)
            o_ref[pl.ds(r0, TILE), :] = (state[1] + state[3]).astype(BF16)
            lt_ref[pl.ds(r0, TILE), :] = jnp.where(_head_lanes(0), state[0], state[2])
            return 0

        lax.fori_loop(0, nb, q_block, 0)

    def col(k):
        return pl.BlockSpec((t_len, TILE), lambda s, hp: (s, k * npair + hp))

    out = pl.BlockSpec((t_len, TILE), lambda s, hp: (s, hp))
    return pl.pallas_call(
        body, name="stickbreak_fwd", grid=(nseq, npair), in_specs=[col(2), col(3), col(4)],
        out_specs=[out, out],
        out_shape=[jax.ShapeDtypeStruct((m, npair * TILE), BF16), jax.ShapeDtypeStruct((m, npair * TILE), F32)],
        scratch_shapes=[pltpu.VMEM((2, t_len, TILE), BF16), pltpu.VMEM((2, t_len, TILE), BF16)],
        compiler_params=_params(("parallel", "parallel")),
    )(p, p, p)


def _sb_bwd(p, dmix, ltot, nseq, t_len):
    m = p.shape[0]
    nb = t_len // TILE
    npair = 4

    def body(q_ref, k_ref, v_ref, do_ref, lt_ref, dq_ref, dk_ref, dv_ref, kh_ref, vh_ref, dk_acc, dv_acc):
        for h in range(2):
            keep = _head_lanes(h)
            kh_ref[h] = jnp.where(keep, k_ref[...], 0).astype(BF16)
            vh_ref[h] = jnp.where(keep, v_ref[...], 0).astype(BF16)
        dk_acc[...] = jnp.zeros_like(dk_acc)
        dv_acc[...] = jnp.zeros_like(dv_acc)
        sum_upto = _sum_matrix("upto")
        sum_before = _sum_matrix("before")
        r = lax.broadcasted_iota(jnp.int32, (TILE, TILE), 0)
        c = lax.broadcasted_iota(jnp.int32, (TILE, TILE), 1)
        strict = c < r
        lane = lax.broadcasted_iota(jnp.int32, (TILE, TILE), 1)

        def tile(h, q, do, qh, doh, ltot_h, row0, nblk, diag, sum_l, sum_g, dq):
            rows = pl.ds(row0, nblk * TILE)
            kj = kh_ref[h, rows, :]
            vj = vh_ref[h, rows, :]
            z = lax.dot_general(q, kj, NT_DIMS, preferred_element_type=F32) * SB_SCALE
            sp = _softplus(z)
            logkeep = jnp.where(strict, -sp, 0.0) if diag else -sp
            upto = []
            for b in range(nblk):
                sums = jnp.dot(logkeep[:, b * TILE:(b + 1) * TILE].astype(BF16), sum_upto,
                               preferred_element_type=F32)
                upto.append(sums[:, :TILE] + sum_l)
                sum_l = sum_l + sums[:, TILE:]
            suffix = ltot_h - (upto[0] if nblk == 1 else jnp.concatenate(upto, axis=1))
            w = jnp.exp(z - sp + suffix)
            if diag:
                w = jnp.where(strict, w, 0.0)
            g = w * lax.dot_general(do, vj, NT_DIMS, preferred_element_type=F32)
            before = []
            for b in range(nblk):
                sums = jnp.dot(g[:, b * TILE:(b + 1) * TILE].astype(BF16), sum_before, preferred_element_type=F32)
                before.append(sums[:, :TILE] + sum_g)
                sum_g = sum_g + sums[:, TILE:]
            g_before = before[0] if nblk == 1 else jnp.concatenate(before, axis=1)
            dz = (g - jnp.exp(z - sp) * (g + g_before)) * SB_SCALE
            if diag:
                dz = jnp.where(strict, dz, 0.0)
            dzb = dz.astype(BF16)
            dq = dq + jnp.dot(dzb, kj, preferred_element_type=F32)
            dk_part = lax.dot_general(dzb, qh, TN_DIMS, preferred_element_type=F32)
            dv_part = lax.dot_general(w.astype(BF16), doh, TN_DIMS, preferred_element_type=F32)
            return sum_l, sum_g, dq, dk_part, dv_part

        def both_heads(per_head, row0, nblk, diag, state):
            out, dk_new, dv_new = [], 0.0, 0.0
            for h in range(2):
                sum_l, sum_g, dq, dk_part, dv_part = tile(h, *per_head[h], row0, nblk, diag, *state[3 * h:3 * h + 3])
                out += [sum_l, sum_g, dq]
                dk_new = dk_new + dk_part
                dv_new = dv_new + dv_part
            rows = pl.ds(row0, nblk * TILE)
            dk_acc[rows, :] += dk_new
            dv_acc[rows, :] += dv_new
            return tuple(out)

        def q_block(i, _):
            r0 = pl.multiple_of(i * TILE, TILE)
            q = q_ref[pl.ds(r0, TILE), :]
            do = do_ref[pl.ds(r0, TILE), :]
            lt = lt_ref[pl.ds(r0, TILE), :]
            per_head = []
            for h in range(2):
                keep = _head_lanes(h)
                ltot_h = jnp.sum(jnp.where(lane == h * SB_DH, lt, 0.0), axis=1, keepdims=True)
                per_head.append((q, do, jnp.where(keep, q, 0).astype(BF16), jnp.where(keep, do, 0).astype(BF16),
                                 ltot_h))
            zero = jnp.zeros((TILE, TILE), F32)
            full, two, one = _passes(i)
            nbp = SB_PASS_BLOCKS
            state = lax.fori_loop(
                0, full,
                lambda jj, st: both_heads(per_head, pl.multiple_of(nbp * jj * TILE, TILE), nbp, False, st),
                (zero,) * 6)
            state = lax.cond(
                two, lambda st: both_heads(per_head, pl.multiple_of(nbp * full * TILE, TILE), 2, False, st),
                lambda st: st, state)
            state = lax.cond(
                one, lambda st: both_heads(per_head, pl.multiple_of((i - 1) * TILE, TILE), 1, False, st),
                lambda st: st, state)
            state = both_heads(per_head, r0, 1, True, state)
            dq_ref[pl.ds(r0, TILE), :] = (state[2] + state[5]).astype(BF16)
            return 0

        lax.fori_loop(0, nb, q_block, 0)
        dk_ref[...] = dk_acc[...].astype(BF16)
        dv_ref[...] = dv_acc[...].astype(BF16)

    def col(k):
        return pl.BlockSpec((t_len, TILE), lambda s, hp: (s, k * npair + hp))

    out = pl.BlockSpec((t_len, TILE), lambda s, hp: (s, hp))
    width = npair * TILE
    return pl.pallas_call(
        body, name="stickbreak_bwd", grid=(nseq, npair),
        in_specs=[col(2), col(3), col(4), col(1), out], out_specs=[out, out, out],
        out_shape=[jax.ShapeDtypeStruct((m, width), BF16)] * 3,
        scratch_shapes=[pltpu.VMEM((2, t_len, TILE), BF16), pltpu.VMEM((2, t_len, TILE), BF16),
                        pltpu.VMEM((t_len, TILE), F32), pltpu.VMEM((t_len, TILE), F32)],
        compiler_params=_params(("parallel", "parallel")),
    )(p, p, p, dmix, ltot)


def _adam_math(w, g, m, v):
    m = ADAM_B1 * m + (1.0 - ADAM_B1) * g
    v = ADAM_B2 * v + (1.0 - ADAM_B2) * (g * g)
    m_hat = m / (1.0 - ADAM_B1 ** ADAM_STEP)
    v_hat = v / (1.0 - ADAM_B2 ** ADAM_STEP)
    delta = -ADAM_LR * (m_hat / (jnp.sqrt(v_hat) + ADAM_EPS) + ADAM_WD * w)
    return delta, m, v


def _cast_place(w, pos, *, name):
    l_dim, r, c = w.shape
    tr = min(r, 256)

    def body(pos_ref, w_ref, o_ref):
        o_ref[...] = w_ref[...].astype(BF16)

    grid_spec = pltpu.PrefetchScalarGridSpec(
        num_scalar_prefetch=1, grid=(l_dim, r // tr),
        in_specs=[pl.BlockSpec((None, tr, c), lambda l, i, pos_ref: (l, i, 0))],
        out_specs=pl.BlockSpec((None, None, tr, c), lambda l, i, pos_ref: (l, pos_ref[0], i, 0)))
    return pl.pallas_call(
        body, name=name, grid_spec=grid_spec, out_shape=jax.ShapeDtypeStruct((l_dim, N_CHIP, r, c), BF16),
        compiler_params=_params(("parallel",) * 2),
    )(pos, w)


def _pair_sum(mine, got, pos, *, name):
    l_dim, s_dim, h, c = got.shape
    th = min(h, 512)
    nt = h // th

    def body(pos_ref, a_ref, b_ref, o_ref):
        o_ref[...] = (a_ref[...].astype(F32) + b_ref[...].astype(F32)).astype(BF16)

    spec = pl.BlockSpec((None, None, th, c), lambda l, s, i, pos_ref: (l, s, i, 0))
    grid_spec = pltpu.PrefetchScalarGridSpec(
        num_scalar_prefetch=1, grid=(l_dim, s_dim, nt),
        in_specs=[pl.BlockSpec((None, None, th, c), lambda l, s, i, pos_ref: (l, s, pos_ref[1] * nt + i, 0)), spec],
        out_specs=spec)
    return pl.pallas_call(
        body, name=name, grid_spec=grid_spec, out_shape=jax.ShapeDtypeStruct(got.shape, BF16),
        compiler_params=_params(("parallel",) * 3),
    )(pos, mine, got)


def _chip_sum(sums, landed, pos, *, name):
    l_dim, _, h, c = sums.shape
    th = min(h, 512)
    nt = h // th

    def body(pos_ref, own, r0, r1, r2, o_ref):
        o_ref[...] = ((own[...].astype(F32) + r0[...].astype(F32)) + r1[...].astype(F32)) + r2[...].astype(F32)

    def piece(k):
        return pl.BlockSpec((None, None, th, c), lambda l, i, pos_ref: (l, k, i, 0))

    grid_spec = pltpu.PrefetchScalarGridSpec(
        num_scalar_prefetch=1, grid=(l_dim, nt),
        in_specs=[pl.BlockSpec((None, None, th, c), lambda l, i, pos_ref: (l, pos_ref[0], i, 0)),
                  piece(0), piece(1), piece(2)],
        out_specs=pl.BlockSpec((None, th, c), lambda l, i, pos_ref: (l, pos_ref[1] * nt + i, 0)))
    return pl.pallas_call(
        body, name=name, grid_spec=grid_spec, out_shape=jax.ShapeDtypeStruct((l_dim, 2 * h, c), F32),
        compiler_params=_params(("parallel",) * 2),
    )(pos, sums, landed, landed, landed)


def _adam_big(w, m, v, grads, *, name):
    l_dim, r, c = w.shape
    assert len(grads) == l_dim
    tr = min(r, 256)

    def body(*refs):
        w_ref, m_ref, v_ref = refs[:3]
        g_refs = refs[3:3 + l_dim]
        go_ref, d_ref, mo_ref, vo_ref = refs[3 + l_dim:]
        g = g_refs[0][...]
        for l in range(1, l_dim):
            g = jnp.where(pl.program_id(0) == l, g_refs[l][...], g)
        delta, m_new, v_new = _adam_math(w_ref[...], g, m_ref[...], v_ref[...])
        go_ref[...] = g
        d_ref[...] = delta
        mo_ref[...] = m_new
        vo_ref[...] = v_new

    spec = pl.BlockSpec((None, tr, c), lambda l, i: (l, i, 0))
    gspec = pl.BlockSpec((None, tr, c), lambda l, i: (0, i, 0))
    return pl.pallas_call(
        body, name=name, grid=(l_dim, r // tr), in_specs=[spec] * 3 + [gspec] * l_dim, out_specs=[spec] * 4,
        out_shape=[jax.ShapeDtypeStruct(w.shape, F32)] * 4, compiler_params=_params(("parallel",) * 2),
    )(w, m, v, *grads)


def _position():
    return lax.axis_index("x"), lax.axis_index("y"), lax.axis_index("c")


def _other_chips(x, y):
    return [(1 - x, y), (x, 1 - y), (1 - x, 1 - y)]


def _remote(src, dst, send_sem, recv_sem, device):
    return pltpu.make_async_remote_copy(src_ref=src, dst_ref=dst, send_sem=send_sem, recv_sem=recv_sem,
                                        device_id=device, device_id_type=MESH)


ANY = pl.BlockSpec(memory_space=pl.ANY)


def _gather_weights(bufs):
    n = len(bufs)

    def body(*refs):
        outs = refs[n:2 * n]
        send_sems, recv_sems, fwd_send, fwd_recv = refs[2 * n:]
        x, y, c = _position()
        chips = _other_chips(x, y)
        sibling = (x, y, 1 - c)

        def half(a, chip, core):
            h = bufs[a].shape[2] // 2
            return outs[a].at[:, 2 * chip[0] + chip[1], pl.ds(core * h, h), :]

        sends = []
        for a in range(n):
            mine = half(a, (x, y), c)
            for k, chip in enumerate(chips):
                cp = _remote(mine, mine, send_sems.at[3 * a + k], recv_sems.at[3 * a + k], (*chip, c))
                cp.start()
                sends.append(cp)
        for k, chip in enumerate(chips):
            for a in range(n):
                landed = half(a, chip, c)
                _remote(landed, landed, send_sems.at[3 * a + k], recv_sems.at[3 * a + k], (*chip, c)).wait_recv()
                cp = _remote(landed, landed, fwd_send.at[3 * a + k], fwd_recv.at[3 * a + k], sibling)
                cp.start()
                sends.append(cp)
        for k, chip in enumerate(chips):
            for a in range(n):
                got = half(a, chip, 1 - c)
                _remote(got, got, fwd_send.at[3 * a + k], fwd_recv.at[3 * a + k], sibling).wait_recv()
        for cp in sends:
            cp.wait_send()

    sem = pltpu.SemaphoreType.DMA((3 * n,))
    return pl.pallas_call(
        body, name="gather_weights", in_specs=[ANY] * n, out_specs=[ANY] * n,
        out_shape=[jax.ShapeDtypeStruct(b.shape, b.dtype) for b in bufs],
        input_output_aliases={a: a for a in range(n)},
        scratch_shapes=[sem, sem, sem, sem],
        compiler_params=pltpu.CompilerParams(has_side_effects=True),
    )(*bufs)


def _swap_halves(grads):
    n = len(grads)

    def body(*refs):
        ins, got = refs[:n], refs[n:2 * n]
        send_sems, recv_sems = refs[2 * n:]
        x, y, c = _position()
        sibling = (x, y, 1 - c)
        copies = []
        for a in range(n):
            h = grads[a].shape[2] // 2
            cp = _remote(ins[a].at[:, :, pl.ds((1 - c) * h, h), :], got[a], send_sems.at[a], recv_sems.at[a], sibling)
            cp.start()
            copies.append(cp)
        for cp in copies:
            cp.wait()

    sem = pltpu.SemaphoreType.DMA((n,))
    return pl.pallas_call(
        body, name="swap_halves", in_specs=[ANY] * n, out_specs=[ANY] * n,
        out_shape=[jax.ShapeDtypeStruct(g.shape[:2] + (g.shape[2] // 2, g.shape[3]), g.dtype) for g in grads],
        scratch_shapes=[sem, sem], compiler_params=pltpu.CompilerParams(has_side_effects=True),
    )(*grads)


def _exchange_chips(sums):
    n = len(sums)

    def body(*refs):
        ins, outs = refs[:n], refs[n:2 * n]
        send_sems, recv_sems = refs[2 * n:]
        x, y, c = _position()
        chips = _other_chips(x, y)
        sends = []
        for a in range(n):
            for k, chip in enumerate(chips):
                cp = _remote(ins[a].at[:, 2 * chip[0] + chip[1]], outs[a].at[:, k],
                             send_sems.at[3 * a + k], recv_sems.at[3 * a + k], (*chip, c))
                cp.start()
                sends.append(cp)
        for cp in sends:
            cp.wait()

    sem = pltpu.SemaphoreType.DMA((3 * n,))
    return pl.pallas_call(
        body, name="exchange_chips", in_specs=[ANY] * n, out_specs=[ANY] * n,
        out_shape=[jax.ShapeDtypeStruct((s.shape[0], 3) + s.shape[2:], s.dtype) for s in sums],
        scratch_shapes=[sem, sem], compiler_params=pltpu.CompilerParams(has_side_effects=True),
    )(*sums)


def _join_halves(bufs):
    n = len(bufs)

    def body(*refs):
        outs = refs[n:2 * n]
        send_sems, recv_sems = refs[2 * n:]
        x, y, c = _position()
        sibling = (x, y, 1 - c)
        copies = []
        for a in range(n):
            h = bufs[a].shape[1] // 2
            mine = outs[a].at[:, pl.ds(c * h, h), :]
            cp = _remote(mine, mine, send_sems.at[a], recv_sems.at[a], sibling)
            cp.start()
            copies.append((cp, a, h))
        for cp, a, h in copies:
            cp.wait_send()
            got = outs[a].at[:, pl.ds((1 - c) * h, h), :]
            _remote(got, got, send_sems.at[a], recv_sems.at[a], sibling).wait_recv()

    sem = pltpu.SemaphoreType.DMA((n,))
    return pl.pallas_call(
        body, name="join_halves", in_specs=[ANY] * n, out_specs=[ANY] * n,
        out_shape=[jax.ShapeDtypeStruct(b.shape, b.dtype) for b in bufs],
        input_output_aliases={a: a for a in range(n)},
        scratch_shapes=[sem, sem], compiler_params=pltpu.CompilerParams(has_side_effects=True),
    )(*bufs)


def _allreduce_small(packs):
    n = len(packs)

    def body(*refs):
        ins, outs, gath = refs[:n], refs[n:2 * n], refs[2 * n:3 * n]
        send_sems, recv_sems = refs[3 * n:]
        x, y, c = _position()
        me, sibling = (x, y, c), (x, y, 1 - c)
        chips = _other_chips(x, y)

        def slot(a, dev):
            return gath[a].at[4 * dev[0] + 2 * dev[1] + dev[2]]

        def copy(a, k, block, to, src=None):
            return _remote(slot(a, block) if src is None else src, slot(a, block),
                           send_sems.at[7 * a + k], recv_sems.at[7 * a + k], to)

        started = []
        for a in range(n):
            slot(a, me)[...] = ins[a][...]
            first = [copy(a, 0, me, sibling, src=ins[a])]
            first += [copy(a, 1 + k, me, (*chip, c), src=ins[a]) for k, chip in enumerate(chips)]
            for cp in first:
                cp.start()
            started += first
        for a in range(n):
            for k, chip in enumerate(chips):
                copy(a, 1 + k, (*chip, c), me).wait_recv()
                cp = copy(a, 4 + k, (*chip, c), sibling)
                cp.start()
                started.append(cp)
        for a in range(n):
            copy(a, 0, sibling, me).wait_recv()
            for k, chip in enumerate(chips):
                copy(a, 4 + k, (*chip, 1 - c), me).wait_recv()
        for cp in started:
            cp.wait_send()
        for a in range(n):
            total = gath[a][0]
            for d in range(1, N_DEV):
                total = total + gath[a][d]
            outs[a][...] = total

    vmem = pl.BlockSpec(memory_space=pltpu.VMEM)
    sem = pltpu.SemaphoreType.DMA((7 * n,))
    return pl.pallas_call(
        body, name="allreduce_small", in_specs=[vmem] * n, out_specs=[vmem] * n,
        out_shape=[jax.ShapeDtypeStruct(p.shape, p.dtype) for p in packs],
        scratch_shapes=[pltpu.VMEM((N_DEV,) + p.shape, p.dtype) for p in packs] + [sem, sem],
        compiler_params=pltpu.CompilerParams(has_side_effects=True, vmem_limit_bytes=VMEM_LIMIT_BYTES),
    )(*packs)


LOSS_ROW = 1040


def _pad_rows(a, rows=8):
    return jnp.concatenate([a, jnp.zeros((rows - a.shape[0], a.shape[1]), a.dtype)], axis=0)

def _adam_small(wide, mid, narrow, params):
    names = ["mix_norm_g", "mlp_norm_g", "final_norm_g", "conv_b", "conv_w", "sgu_norm_g", "sgu_norm_b",
             "pool_w", "pool_scale", "sgu_w", "sgu_b"]
    n = len(names)

    def body(*refs):
        wide_ref, mid_ref, narrow_ref = refs[:3]
        wmv = refs[3:3 + 3 * n]
        outs = refs[3 + 3 * n:]
        x, y, _ = _position()
        q = 2 * x + y

        def my_quarter(rows):
            parts = [rows[:, s * TILE:(s + 1) * TILE] for s in range(N_CHIP)]
            return jnp.where(q == 0, parts[0], jnp.where(q == 1, parts[1], jnp.where(q == 2, parts[2], parts[3])))

        def tiles(first_row):
            return [((0, g), narrow_ref[first_row + g * TILE:first_row + (g + 1) * TILE, :]) for g in range(4)]

        grads = {
            "mix_norm_g": [((), wide_ref[0:2, :])],
            "mlp_norm_g": [((), wide_ref[8:10, :])],
            "final_norm_g": [((), wide_ref[16:17, :])],
            "conv_b": [((), mid_ref[0:1, :])],
            "conv_w": [((0,), my_quarter(mid_ref[8:11, :]))],
            "sgu_norm_g": [((), my_quarter(mid_ref[16:17, :]))],
            "sgu_norm_b": [((), my_quarter(mid_ref[24:25, :]))],
            "pool_w": tiles(0),
            "sgu_w": tiles(512),
            "pool_scale": [((0,), narrow_ref[1024:1028, :])],
            "sgu_b": [((0,), narrow_ref[1032:1036, :])],
        }
        for i, name in enumerate(names):
            w_ref, m_ref, v_ref = wmv[3 * i:3 * i + 3]
            for lead, g in grads[name]:
                idx = lead + (slice(None), slice(None))
                delta, m_new, v_new = _adam_math(w_ref[idx], g, m_ref[idx], v_ref[idx])
                outs[4 * i][idx] = g
                outs[4 * i + 1][idx] = delta
                outs[4 * i + 2][idx] = m_new
                outs[4 * i + 3][idx] = v_new

    vmem = pl.BlockSpec(memory_space=pltpu.VMEM)
    args, out_shape = [wide, mid, narrow], []
    for name in names:
        w, m, v = params[name]
        args += [w, m, v]
        out_shape += [jax.ShapeDtypeStruct(w.shape, F32)] * 4
    res = pl.pallas_call(
        body, name="adam_small", in_specs=[vmem] * len(args), out_specs=[vmem] * len(out_shape),
        out_shape=out_shape, compiler_params=pltpu.CompilerParams(vmem_limit_bytes=VMEM_LIMIT_BYTES),
    )(*args)
    return {name: res[4 * i:4 * i + 4] for i, name in enumerate(names)}


def _reduce_big(grads, pos):
    got = _swap_halves(grads)
    sums = [_pair_sum(a, b, pos, name=f"pair_sum_{i}") for i, (a, b) in enumerate(zip(grads, got))]
    landed = _exchange_chips(sums)
    halves = [_chip_sum(s, r, pos, name=f"chip_sum_{i}") for i, (s, r) in enumerate(zip(sums, landed))]
    return _join_halves(halves)


def kernel(x, mix_norm_g, mlp_norm_g, ab_w_in, pool_w, pool_scale, conv_w, conv_b, ab_w_out, cd_w_in, sgu_norm_g, sgu_norm_b, sgu_w, sgu_b, cd_w_out, mlp_w1, mlp_w2, final_norm_g, loss_target, m_mix_norm_g, m_mlp_norm_g, m_ab_w_in, m_pool_w, m_pool_scale, m_conv_w, m_conv_b, m_ab_w_out, m_cd_w_in, m_sgu_norm_g, m_sgu_norm_b, m_sgu_w, m_sgu_b, m_cd_w_out, m_mlp_w1, m_mlp_w2, m_final_norm_g, v_mix_norm_g, v_mlp_norm_g, v_ab_w_in, v_pool_w, v_pool_scale, v_conv_w, v_conv_b, v_ab_w_out, v_cd_w_in, v_sgu_norm_g, v_sgu_norm_b, v_sgu_w, v_sgu_b, v_cd_w_out, v_mlp_w1, v_mlp_w2, v_final_norm_g):
    nseq, t_len, d = x.shape
    m_tok = nseq * t_len
    h0 = x.reshape(m_tok, d)
    target = loss_target.reshape(m_tok, d)

    x_idx, y_idx = lax.axis_index("x"), lax.axis_index("y")
    q_idx = 2 * x_idx + y_idx
    pos = jnp.stack([q_idx, lax.axis_index("c")]).astype(jnp.int32)
    w_ab_in, w_ab_out, w_cd_in, w_cd_out, w_1, w_2 = _gather_weights(
        [_cast_place(w, pos, name=f"cast_place_{i}")
         for i, w in enumerate((ab_w_in, ab_w_out, cd_w_in, cd_w_out, mlp_w1, mlp_w2))])
    w_ab_out = w_ab_out.reshape(1, 1, -1, d)
    w_cd_out = w_cd_out.reshape(1, 1, -1, d)
    w_2 = w_2.reshape(2, 1, -1, d)

    pool_w3, pool_scale3 = pool_w[0], pool_scale[0].reshape(4, 1, TILE)
    sgu_w3 = sgu_w[0]
    sgu_w3_t = jnp.swapaxes(sgu_w3, 1, 2)
    sgu_bias_tile = jnp.broadcast_to(sgu_b[0][:, :, None], (4, TILE, TILE))
    conv_w2, conv_b2 = conv_w[0], conv_b
    def place_quarter(v):
        return lax.dynamic_update_slice(jnp.zeros((v.shape[0], 4 * TILE), F32), v, (0, q_idx * TILE))

    sharded_small = jnp.concatenate(
        [place_quarter(conv_w[0]), place_quarter(sgu_norm_g), place_quarter(sgu_norm_b),
         jnp.zeros((3, 4 * TILE), F32)], axis=0)
    sharded_small, = _allreduce_small([sharded_small])
    sharded_small = sharded_small * 0.5
    conv_w_full = sharded_small[0:3]
    sgu_g_full = sharded_small[3:4]
    sgu_b_full = sharded_small[4:5]

    xn0 = _rms_fwd(h0, mix_norm_g[0:1], name="rms_fwd_mix0")
    p_ab = _mm_nn(xn0, w_ab_in, 0, out_dtype=BF16, name="ab_in_proj")
    mix0 = _ab_fwd(p_ab, pool_w3, pool_scale3, conv_w_full, conv_b2, nseq, t_len)
    h1 = _mm_nn(mix0, w_ab_out, 0, out_dtype=F32, name="ab_out_proj", epilogue="residual", extra=h0)
    hn0 = _rms_fwd(h1, mlp_norm_g[0:1], name="rms_fwd_mlp0")
    act0 = _mm_nn(hn0, w_1, 0, out_dtype=BF16, name="mlp0_up", epilogue="relu2")
    h2 = _mm_nn(act0, w_2, 0, out_dtype=F32, name="mlp0_down", epilogue="residual", extra=h1)

    xn1 = _rms_fwd(h2, mix_norm_g[1:2], name="rms_fwd_mix1")
    p_cd = _mm_nn(xn1, w_cd_in, 0, out_dtype=BF16, name="cd_in_proj")
    c_out = _sgu_fwd(p_cd, sgu_g_full, sgu_b_full, sgu_w3, sgu_bias_tile)
    d_out, ltot = _sb_fwd(p_cd, nseq, t_len)
    mix1 = jnp.concatenate([c_out, d_out], axis=1)
    h3 = _mm_nn(mix1, w_cd_out, 0, out_dtype=F32, name="cd_out_proj", epilogue="residual", extra=h2)
    hn1 = _rms_fwd(h3, mlp_norm_g[1:2], name="rms_fwd_mlp1")
    act1 = _mm_nn(hn1, w_1, 1, out_dtype=BF16, name="mlp1_up", epilogue="relu2")
    h4 = _mm_nn(act1, w_2, 1, out_dtype=F32, name="mlp1_down", epilogue="residual", extra=h3)

    dh4, dh4_bf, dg_final, loss_tile = _final_loss(h4, final_norm_g.reshape(1, d), target)

    def mlp_bwd(dh_out, dh_out_bf, h_in, hn, act, layer, tag):
        dz = _mm_nt(dh_out_bf, w_2, layer, out_dtype=BF16, name=f"mlp{tag}_down_bwd",
                    epilogue="relu2_bwd", extra=act)
        g_w2 = _mm_tn(act, dh_out_bf, 1, name=f"mlp{tag}_down_wgrad")
        g_w1 = _mm_tn(hn, dz, N_CHIP, name=f"mlp{tag}_up_wgrad")
        dhn = _mm_nt(dz, w_1, layer, out_dtype=F32, name=f"mlp{tag}_up_bwd")
        dh_in, dh_in_bf, dg = _rms_bwd(h_in, mlp_norm_g[layer:layer + 1], dhn, dh_out, name=f"rms_bwd_mlp{tag}")
        return dh_in, dh_in_bf, dg, g_w1, g_w2

    dh3, dh3_bf, dg_mlp1, g_w1_1, g_w2_1 = mlp_bwd(dh4, dh4_bf, h3, hn1, act1, 1, "1")

    dmix1 = _mm_nt(dh3_bf, w_cd_out, 0, out_dtype=BF16, name="cd_out_bwd")
    g_cd_out = _mm_tn(mix1, dh3_bf, 1, name="cd_out_wgrad")
    du, dv, dsgu_w, dsgu_bs, dsgu_g, dsgu_b = _sgu_bwd(p_cd, dmix1, sgu_g_full, sgu_b_full, sgu_w3, sgu_w3_t,
                                                      sgu_bias_tile)
    dq, dk, dvv = _sb_bwd(p_cd, dmix1, ltot, nseq, t_len)
    dp_cd = jnp.concatenate([du, dv, dq, dk, dvv], axis=1)
    g_cd_in = _mm_tn(xn1, dp_cd, N_CHIP, name="cd_in_wgrad")
    dxn1 = _mm_nt(dp_cd, w_cd_in, 0, out_dtype=F32, name="cd_in_bwd")
    dh2, dh2_bf, dg_mix1 = _rms_bwd(h2, mix_norm_g[1:2], dxn1, dh3, name="rms_bwd_mix1")

    dh1, dh1_bf, dg_mlp0, g_w1_0, g_w2_0 = mlp_bwd(dh2, dh2_bf, h1, hn0, act0, 0, "0")

    dmix0 = _mm_nt(dh1_bf, w_ab_out, 0, out_dtype=BF16, name="ab_out_bwd")
    g_ab_out = _mm_tn(mix0, dh1_bf, 1, name="ab_out_wgrad")
    da, dxb, dgb, dgc, dpool_w, dpool_scale, dconv_w, dconv_b = _ab_bwd(
        p_ab, dmix0, pool_w3, pool_scale3, conv_w_full, conv_b2, nseq, t_len)
    dp_ab = jnp.concatenate([da, dxb, dgb, dgc], axis=1)
    g_ab_in = _mm_tn(xn0, dp_ab, N_CHIP, name="ab_in_wgrad")
    dxn0 = _mm_nt(dp_ab, w_ab_in, 0, out_dtype=F32, name="ab_in_bwd")
    grad_x, _, dg_mix0 = _rms_bwd(h0, mix_norm_g[0:1], dxn0, dh1, name="rms_bwd_mix0")

    def as_pieces(g):
        return g.reshape(1, N_CHIP, -1, g.shape[-1]) if g.shape[1] == 1 else g

    big = [as_pieces(g) for g in (g_ab_in, g_ab_out, g_cd_in, g_cd_out, g_w1_0, g_w1_1, g_w2_0, g_w2_1)]
    r_ab_in, r_ab_out, r_cd_in, r_cd_out, r_w1_0, r_w1_1, r_w2_0, r_w2_1 = _reduce_big(big, pos)

    big_out = {
        "ab_w_in": _adam_big(ab_w_in, m_ab_w_in, v_ab_w_in, [r_ab_in], name="adam_ab_w_in"),
        "ab_w_out": _adam_big(ab_w_out, m_ab_w_out, v_ab_w_out, [r_ab_out], name="adam_ab_w_out"),
        "cd_w_in": _adam_big(cd_w_in, m_cd_w_in, v_cd_w_in, [r_cd_in], name="adam_cd_w_in"),
        "cd_w_out": _adam_big(cd_w_out, m_cd_w_out, v_cd_w_out, [r_cd_out], name="adam_cd_w_out"),
        "mlp_w1": _adam_big(mlp_w1, m_mlp_w1, v_mlp_w1, [r_w1_0, r_w1_1], name="adam_mlp_w1"),
        "mlp_w2": _adam_big(mlp_w2, m_mlp_w2, v_mlp_w2, [r_w2_0, r_w2_1], name="adam_mlp_w2"),
    }

    wide = jnp.concatenate([_pad_rows(jnp.concatenate([dg_mix0, dg_mix1], axis=0)),
                            _pad_rows(jnp.concatenate([dg_mlp0, dg_mlp1], axis=0)), _pad_rows(dg_final)], axis=0)
    mid = jnp.concatenate([_pad_rows(dconv_b), _pad_rows(dconv_w), _pad_rows(dsgu_g), _pad_rows(dsgu_b)], axis=0)
    narrow = jnp.concatenate(
        [dpool_w.reshape(4 * TILE, TILE), dsgu_w.reshape(4 * TILE, TILE), _pad_rows(dpool_scale.reshape(4, TILE)),
         _pad_rows(dsgu_bs[:, :, 0]), loss_tile], axis=0)
    wide, mid, narrow = _allreduce_small([wide, mid, narrow])
    small_out = _adam_small(wide, mid, narrow, {
        "mix_norm_g": (mix_norm_g, m_mix_norm_g, v_mix_norm_g),
        "mlp_norm_g": (mlp_norm_g, m_mlp_norm_g, v_mlp_norm_g),
        "final_norm_g": tuple(a.reshape(1, d) for a in (final_norm_g, m_final_norm_g, v_final_norm_g)),
        "conv_b": (conv_b, m_conv_b, v_conv_b),
        "conv_w": (conv_w, m_conv_w, v_conv_w),
        "sgu_norm_g": (sgu_norm_g, m_sgu_norm_g, v_sgu_norm_g),
        "sgu_norm_b": (sgu_norm_b, m_sgu_norm_b, v_sgu_norm_b),
        "pool_w": (pool_w, m_pool_w, v_pool_w),
        "pool_scale": (pool_scale, m_pool_scale, v_pool_scale),
        "sgu_w": (sgu_w, m_sgu_w, v_sgu_w),
        "sgu_b": (sgu_b, m_sgu_b, v_sgu_b),
    })
    small_out["final_norm_g"] = [a.reshape(d) for a in small_out["final_norm_g"]]

    order = ["mix_norm_g", "mlp_norm_g", "ab_w_in", "pool_w", "pool_scale", "conv_w", "conv_b", "ab_w_out",
             "cd_w_in", "sgu_norm_g", "sgu_norm_b", "sgu_w", "sgu_b", "cd_w_out", "mlp_w1", "mlp_w2",
             "final_norm_g"]
    both = {**big_out, **small_out}
    loss = narrow[LOSS_ROW, 0]
    outs = [loss, grad_x.reshape(nseq, t_len, d)]
    for kind in range(4):
        outs += [both[name][kind] for name in order]
    return tuple(outs)
```

```python
import math

import jax
import jax.numpy as jnp
from jax import lax
from jax.experimental import pallas as pl
from jax.experimental.pallas import tpu as pltpu

F32 = jnp.float32
BF16 = jnp.bfloat16
MESH = pl.DeviceIdType.MESH

D_MODEL = 1024
EPS = 1e-6
TILE = 128
N_CHIP = 4
N_DEV = 8
VMEM_LIMIT_BYTES = 56 * 1024 * 1024

ADAM_LR = 0.001
ADAM_B1 = 0.9
ADAM_B2 = 0.999
ADAM_EPS = 1e-08
ADAM_WD = 0.01
ADAM_STEP = 10

NT_DIMS = (((1,), (1,)), ((), ()))
TN_DIMS = (((0,), (0,)), ((), ()))


def _params(sem=None):
    return pltpu.CompilerParams(dimension_semantics=sem, vmem_limit_bytes=VMEM_LIMIT_BYTES)


def _mm_nn(a, b4, layer, *, out_dtype, name, epilogue=None, extra=None, tm=1024, tk=1024):
    m, k_dim = a.shape
    _, s_dim, kb, n = b4.shape
    assert kb == k_dim
    tn = min(n, 1024)
    assert m % tm == 0 and k_dim % tk == 0 and n % tn == 0
    nk, npb = k_dim // tk, n // tn
    grid = (m // tm, s_dim * npb, nk)

    def body(*refs):
        if extra is None:
            a_ref, b_ref, o_ref, *scr = refs
        else:
            a_ref, b_ref, e_ref, o_ref, *scr = refs

        def finish(acc):
            if epilogue == "relu2":
                r = jnp.maximum(acc, 0.0)
                acc = r * r
            elif epilogue == "residual":
                acc = acc + e_ref[...]
            o_ref[...] = acc.astype(out_dtype)

        part = jnp.dot(a_ref[...], b_ref[...], preferred_element_type=F32)
        if nk == 1:
            finish(part)
        else:
            acc_ref, = scr
            kk = pl.program_id(2)

            @pl.when(kk == 0)
            def _():
                acc_ref[...] = part

            @pl.when(kk > 0)
            def _():
                acc_ref[...] += part

            @pl.when(kk == nk - 1)
            def _():
                finish(acc_ref[...])

    in_specs = [
        pl.BlockSpec((tm, tk), lambda i, j, kk: (i, kk)),
        pl.BlockSpec((None, None, tk, tn), lambda i, j, kk: (layer, j // npb, kk, j % npb)),
    ]
    args = [a, b4]
    if extra is not None:
        in_specs.append(pl.BlockSpec((tm, tn), lambda i, j, kk: (i, j)))
        args.append(extra)
    return pl.pallas_call(
        body, name=name, grid=grid, in_specs=in_specs,
        out_specs=pl.BlockSpec((tm, tn), lambda i, j, kk: (i, j)),
        out_shape=jax.ShapeDtypeStruct((m, s_dim * n), out_dtype),
        scratch_shapes=[] if nk == 1 else [pltpu.VMEM((tm, tn), F32)],
        compiler_params=_params(("parallel", "parallel", "arbitrary")),
    )(*args)


def _mm_nt(a, b4, layer, *, out_dtype, name, epilogue=None, extra=None, tm=1024, tn=1024):
    m, k_dim = a.shape
    _, s_dim, n_out, n = b4.shape
    assert k_dim == s_dim * n
    tk = min(n, 1024)
    tn = min(tn, n_out)
    assert m % tm == 0 and n_out % tn == 0 and n % tk == 0
    kpb = n // tk
    nk = s_dim * kpb
    grid = (m // tm, n_out // tn, nk)

    def body(*refs):
        if extra is None:
            a_ref, b_ref, o_ref, *scr = refs
        else:
            a_ref, b_ref, e_ref, o_ref, *scr = refs

        def finish(acc):
            if epilogue == "relu2_bwd":
                acc = acc * (2.0 * jnp.sqrt(e_ref[...].astype(F32)))
            o_ref[...] = acc.astype(out_dtype)

        part = lax.dot_general(a_ref[...], b_ref[...], NT_DIMS, preferred_element_type=F32)
        if nk == 1:
            finish(part)
        else:
            acc_ref, = scr
            kk = pl.program_id(2)

            @pl.when(kk == 0)
            def _():
                acc_ref[...] = part

            @pl.when(kk > 0)
            def _():
                acc_ref[...] += part

            @pl.when(kk == nk - 1)
            def _():
                finish(acc_ref[...])

    in_specs = [
        pl.BlockSpec((tm, tk), lambda i, j, kk: (i, kk)),
        pl.BlockSpec((None, None, tn, tk), lambda i, j, kk: (layer, kk // kpb, j, kk % kpb)),
    ]
    args = [a, b4]
    if extra is not None:
        in_specs.append(pl.BlockSpec((tm, tn), lambda i, j, kk: (i, j)))
        args.append(extra)
    return pl.pallas_call(
        body, name=name, grid=grid, in_specs=in_specs,
        out_specs=pl.BlockSpec((tm, tn), lambda i, j, kk: (i, j)),
        out_shape=jax.ShapeDtypeStruct((m, n_out), out_dtype),
        scratch_shapes=[] if nk == 1 else [pltpu.VMEM((tm, tn), F32)],
        compiler_params=_params(("parallel", "parallel", "arbitrary")),
    )(*args)


def _mm_tn(a, b, s_dim, *, name, tm=1024, t1=1024):
    m, k1 = a.shape
    mb, n_all = b.shape
    assert mb == m and n_all % s_dim == 0
    n = n_all // s_dim
    tn = min(n, 1024)
    t1 = min(t1, k1)
    assert m % tm == 0 and k1 % t1 == 0 and n % tn == 0
    npb = n // tn
    nk = m // tm
    grid = (k1 // t1, s_dim * npb, nk)

    def body(a_ref, b_ref, o_ref, acc_ref):
        kk = pl.program_id(2)
        part = lax.dot_general(a_ref[...], b_ref[...], TN_DIMS, preferred_element_type=F32)

        @pl.when(kk == 0)
        def _():
            acc_ref[...] = part

        @pl.when(kk > 0)
        def _():
            acc_ref[...] += part

        @pl.when(kk == nk - 1)
        def _():
            o_ref[...] = acc_ref[...].astype(BF16)

    return pl.pallas_call(
        body, name=name, grid=grid,
        in_specs=[pl.BlockSpec((tm, t1), lambda i, j, kk: (kk, i)),
                  pl.BlockSpec((tm, tn), lambda i, j, kk: (kk, j))],
        out_specs=pl.BlockSpec((None, None, t1, tn), lambda i, j, kk: (0, j // npb, i, j % npb)),
        out_shape=jax.ShapeDtypeStruct((1, s_dim, k1, n), BF16),
        scratch_shapes=[pltpu.VMEM((t1, tn), F32)],
        compiler_params=_params(("parallel", "parallel", "arbitrary")),
    )(a, b)


ROW_TILE = 512


def _rms_fwd(h, g, *, name):
    m, d = h.shape

    def body(h_ref, g_ref, o_ref):
        hv = h_ref[...]
        rstd = lax.rsqrt(jnp.mean(hv * hv, axis=-1, keepdims=True) + EPS)
        o_ref[...] = (hv * rstd * g_ref[...]).astype(BF16)

    return pl.pallas_call(
        body, name=name, grid=(m // ROW_TILE,),
        in_specs=[pl.BlockSpec((ROW_TILE, d), lambda i: (i, 0)), pl.BlockSpec((1, d), lambda i: (0, 0))],
        out_specs=pl.BlockSpec((ROW_TILE, d), lambda i: (i, 0)),
        out_shape=jax.ShapeDtypeStruct((m, d), BF16),
        compiler_params=_params(("parallel",)),
    )(h, g)


def _rms_bwd(h, g, dxn, dres, *, name):
    m, d = h.shape

    def body(h_ref, g_ref, dxn_ref, dres_ref, dh_ref, dhb_ref, dg_ref):
        hv = h_ref[...]
        rstd = lax.rsqrt(jnp.mean(hv * hv, axis=-1, keepdims=True) + EPS)
        xhat = hv * rstd
        dy = dxn_ref[...]
        dxhat = dy * g_ref[...]
        dh = dres_ref[...] + rstd * (dxhat - xhat * jnp.mean(dxhat * xhat, axis=-1, keepdims=True))
        dh_ref[...] = dh
        dhb_ref[...] = dh.astype(BF16)
        part = jnp.sum(dy * xhat, axis=0, keepdims=True)

        @pl.when(pl.program_id(0) == 0)
        def _():
            dg_ref[...] = part

        @pl.when(pl.program_id(0) > 0)
        def _():
            dg_ref[...] += part

    row = pl.BlockSpec((ROW_TILE, d), lambda i: (i, 0))
    vec = pl.BlockSpec((1, d), lambda i: (0, 0))
    return pl.pallas_call(
        body, name=name, grid=(m // ROW_TILE,),
        in_specs=[row, vec, row, row], out_specs=[row, row, vec],
        out_shape=[jax.ShapeDtypeStruct((m, d), F32), jax.ShapeDtypeStruct((m, d), BF16),
                   jax.ShapeDtypeStruct((1, d), F32)],
        compiler_params=_params(("arbitrary",)),
    )(h, g, dxn, dres)


def _final_loss(h, g, target):
    m, d = h.shape

    def body(h_ref, g_ref, t_ref, dh_ref, dhb_ref, dg_ref, loss_ref):
        hv = h_ref[...]
        gv = g_ref[...]
        rstd = lax.rsqrt(jnp.mean(hv * hv, axis=-1, keepdims=True) + EPS)
        xhat = hv * rstd
        err = xhat * gv - t_ref[...]
        dy = err * (1.0 / d)
        dxhat = dy * gv
        dh = rstd * (dxhat - xhat * jnp.mean(dxhat * xhat, axis=-1, keepdims=True))
        dh_ref[...] = dh
        dhb_ref[...] = dh.astype(BF16)
        dg_part = jnp.sum(dy * xhat, axis=0, keepdims=True)
        sq = jnp.sum(jnp.sum(err * err, axis=1, keepdims=True), axis=0, keepdims=True) * (0.5 / d)
        loss_part = jnp.broadcast_to(sq, (8, TILE))

        @pl.when(pl.program_id(0) == 0)
        def _():
            dg_ref[...] = dg_part
            loss_ref[...] = loss_part

        @pl.when(pl.program_id(0) > 0)
        def _():
            dg_ref[...] += dg_part
            loss_ref[...] += loss_part

    row = pl.BlockSpec((ROW_TILE, d), lambda i: (i, 0))
    vec = pl.BlockSpec((1, d), lambda i: (0, 0))
    return pl.pallas_call(
        body, name="final_loss", grid=(m // ROW_TILE,),
        in_specs=[row, vec, row],
        out_specs=[row, row, vec, pl.BlockSpec((8, TILE), lambda i: (0, 0))],
        out_shape=[jax.ShapeDtypeStruct((m, d), F32), jax.ShapeDtypeStruct((m, d), BF16),
                   jax.ShapeDtypeStruct((1, d), F32), jax.ShapeDtypeStruct((8, TILE), F32)],
        compiler_params=_params(("arbitrary",)),
    )(h, g, target)


def _shift_down(x, s, t_idx):
    return jnp.where(t_idx >= s, pltpu.roll(x, s, 0), 0.0)


def _shift_up(x, s, t_idx, t_len):
    return jnp.where(t_idx < t_len - s, pltpu.roll(x, t_len - s, 0), 0.0)


def _pool_select(group, s2, s4, s8, s16):
    return jnp.where(group == 0, s2, jnp.where(group == 1, s4, jnp.where(group == 2, s8, s16)))


def _pool_count(group, t_idx):
    win = jnp.left_shift(2, group)
    return jnp.minimum(t_idx + 1, win).astype(F32)


def _pool_fwd_math(a, group, t_idx):
    s2 = a + _shift_down(a, 1, t_idx)
    s4 = s2 + _shift_down(s2, 2, t_idx)
    s8 = s4 + _shift_down(s4, 4, t_idx)
    s16 = s8 + _shift_down(s8, 8, t_idx)
    return _pool_select(group, s2, s4, s8, s16) / _pool_count(group, t_idx) - a


def _pool_bwd_math(dpooled, group, t_idx, t_len):
    e = dpooled / _pool_count(group, t_idx)
    s2 = e + _shift_up(e, 1, t_idx, t_len)
    s4 = s2 + _shift_up(s2, 2, t_idx, t_len)
    s8 = s4 + _shift_up(s4, 4, t_idx, t_len)
    s16 = s8 + _shift_up(s8, 8, t_idx, t_len)
    return _pool_select(group, s2, s4, s8, s16) - dpooled


def _conv_fwd_math(c, w_ref, b_ref, t_idx):
    return (w_ref[0:1, :] * _shift_down(c, 2, t_idx) + w_ref[1:2, :] * _shift_down(c, 1, t_idx)
            + w_ref[2:3, :] * c + b_ref[...])


def _ab_fwd(p, pool_w, pool_scale, conv_w, conv_b, nseq, t_len):
    m = p.shape[0]
    ng = 4

    def body(a_ref, xb_ref, gb_ref, gc_ref, pw_ref, ps_ref, cw_ref, cb_ref, o_ref):
        j = pl.program_id(1)
        t_idx = lax.broadcasted_iota(jnp.int32, (t_len, TILE), 0)

        @pl.when(j < ng)
        def _():
            pooled = _pool_fwd_math(a_ref[...].astype(F32), j, t_idx)
            mixed = jnp.dot(pooled.astype(BF16), pw_ref[...].astype(BF16), preferred_element_type=F32)
            o_ref[...] = (mixed * ps_ref[...]).astype(BF16)

        @pl.when(j >= ng)
        def _():
            c = gc_ref[...].astype(F32) * xb_ref[...].astype(F32)
            y = _conv_fwd_math(c, cw_ref, cb_ref, t_idx)
            o_ref[...] = (gb_ref[...].astype(F32) * y).astype(BF16)

    def pool_j(j):
        return jnp.minimum(j, ng - 1)

    def conv_j(j):
        return jnp.maximum(j - ng, 0)

    in_specs = [
        pl.BlockSpec((t_len, TILE), lambda s, j: (s, pool_j(j))),
        pl.BlockSpec((t_len, TILE), lambda s, j: (s, ng + conv_j(j))),
        pl.BlockSpec((t_len, TILE), lambda s, j: (s, 2 * ng + conv_j(j))),
        pl.BlockSpec((t_len, TILE), lambda s, j: (s, 3 * ng + conv_j(j))),
        pl.BlockSpec((None, TILE, TILE), lambda s, j: (pool_j(j), 0, 0)),
        pl.BlockSpec((None, 1, TILE), lambda s, j: (pool_j(j), 0, 0)),
        pl.BlockSpec((3, TILE), lambda s, j: (0, conv_j(j))),
        pl.BlockSpec((1, TILE), lambda s, j: (0, conv_j(j))),
    ]
    return pl.pallas_call(
        body, name="ab_mixer_fwd", grid=(nseq, 2 * ng), in_specs=in_specs,
        out_specs=pl.BlockSpec((t_len, TILE), lambda s, j: (s, j)),
        out_shape=jax.ShapeDtypeStruct((m, 2 * ng * TILE), BF16),
        compiler_params=_params(("parallel", "arbitrary")),
    )(p, p, p, p, pool_w, pool_scale, conv_w, conv_b)


def _ab_bwd(p, dmix, pool_w, pool_scale, conv_w, conv_b, nseq, t_len):
    m = p.shape[0]
    ng = 4

    def body(a_ref, xb_ref, gb_ref, gc_ref, dma_ref, dmb_ref, pw_ref, ps_ref, cw_ref, cb_ref,
             da_ref, dxb_ref, dgb_ref, dgc_ref, dpw_ref, dps_ref, dcw_ref, dcb_ref):
        j = pl.program_id(0)
        first = pl.program_id(1) == 0
        t_idx = lax.broadcasted_iota(jnp.int32, (t_len, TILE), 0)

        pooled = _pool_fwd_math(a_ref[...].astype(F32), j, t_idx).astype(BF16)
        w_bf = pw_ref[...].astype(BF16)
        mixed = jnp.dot(pooled, w_bf, preferred_element_type=F32)
        dm = dma_ref[...].astype(F32)
        dps = jnp.sum(dm * mixed, axis=0, keepdims=True)
        dmixed = (dm * ps_ref[...]).astype(BF16)
        dpw = lax.dot_general(pooled, dmixed, TN_DIMS, preferred_element_type=F32)
        dpooled = lax.dot_general(dmixed, w_bf, NT_DIMS, preferred_element_type=F32)
        da_ref[...] = _pool_bwd_math(dpooled, j, t_idx, t_len).astype(BF16)

        xb = xb_ref[...].astype(F32)
        gb = gb_ref[...].astype(F32)
        gc = gc_ref[...].astype(F32)
        d = dmb_ref[...].astype(F32)
        c = gc * xb
        c1 = _shift_down(c, 1, t_idx)
        c2 = _shift_down(c, 2, t_idx)
        y = cw_ref[0:1, :] * c2 + cw_ref[1:2, :] * c1 + cw_ref[2:3, :] * c + cb_ref[...]
        dgb_ref[...] = (d * y).astype(BF16)
        dy = d * gb
        dc = (cw_ref[2:3, :] * dy + cw_ref[1:2, :] * _shift_up(dy, 1, t_idx, t_len)
              + cw_ref[0:1, :] * _shift_up(dy, 2, t_idx, t_len))
        dgc_ref[...] = (dc * xb).astype(BF16)
        dxb_ref[...] = (dc * gc).astype(BF16)
        dcw = jnp.concatenate([jnp.sum(dy * c2, axis=0, keepdims=True),
                               jnp.sum(dy * c1, axis=0, keepdims=True),
                               jnp.sum(dy * c, axis=0, keepdims=True)], axis=0)
        dcb = jnp.sum(dy, axis=0, keepdims=True)

        @pl.when(first)
        def _():
            dpw_ref[...] = dpw
            dps_ref[...] = dps
            dcw_ref[...] = dcw
            dcb_ref[...] = dcb

        @pl.when(jnp.logical_not(first))
        def _():
            dpw_ref[...] += dpw
            dps_ref[...] += dps
            dcw_ref[...] += dcw
            dcb_ref[...] += dcb

    def col(k):
        return pl.BlockSpec((t_len, TILE), lambda j, s: (s, k * ng + j))

    in_specs = [
        col(0), col(1), col(2), col(3), col(0), col(1),
        pl.BlockSpec((None, TILE, TILE), lambda j, s: (j, 0, 0)),
        pl.BlockSpec((None, 1, TILE), lambda j, s: (j, 0, 0)),
        pl.BlockSpec((3, TILE), lambda j, s: (0, j)),
        pl.BlockSpec((1, TILE), lambda j, s: (0, j)),
    ]
    piece = pl.BlockSpec((t_len, TILE), lambda j, s: (s, j))
    out_specs = [
        piece, piece, piece, piece,
        pl.BlockSpec((None, TILE, TILE), lambda j, s: (j, 0, 0)),
        pl.BlockSpec((None, 1, TILE), lambda j, s: (j, 0, 0)),
        pl.BlockSpec((3, TILE), lambda j, s: (0, j)),
        pl.BlockSpec((1, TILE), lambda j, s: (0, j)),
    ]
    w = ng * TILE
    out_shape = [jax.ShapeDtypeStruct((m, w), BF16)] * 4 + [
        jax.ShapeDtypeStruct((ng, TILE, TILE), F32), jax.ShapeDtypeStruct((ng, 1, TILE), F32),
        jax.ShapeDtypeStruct((3, w), F32), jax.ShapeDtypeStruct((1, w), F32)]
    return pl.pallas_call(
        body, name="ab_mixer_bwd", grid=(ng, nseq), in_specs=in_specs, out_specs=out_specs,
        out_shape=out_shape, compiler_params=_params(("parallel", "arbitrary")),
    )(p, p, p, p, dmix, dmix, pool_w, pool_scale, conv_w, conv_b)


SGU_ROWS = 512
INV_SQRT2 = 1.0 / math.sqrt(2.0)
INV_SQRT_2PI = 1.0 / math.sqrt(2.0 * math.pi)


def _gelu(x):
    return 0.5 * x * (1.0 + lax.erf(x * INV_SQRT2))


def _gelu_grad(x):
    return 0.5 * (1.0 + lax.erf(x * INV_SQRT2)) + x * (INV_SQRT_2PI * jnp.exp(-0.5 * x * x))


def _causal_tile(transposed=False):
    r = lax.broadcasted_iota(jnp.int32, (TILE, TILE), 0)
    c = lax.broadcasted_iota(jnp.int32, (TILE, TILE), 1)
    return r <= c if transposed else c <= r


def _sgu_norm(v, g_ref, b_ref):
    mu = jnp.mean(v, axis=-1, keepdims=True)
    xc = v - mu
    rstd = lax.rsqrt(jnp.mean(xc * xc, axis=-1, keepdims=True) + EPS)
    xhat = xc * rstd
    return xhat, rstd, xhat * g_ref[...] + b_ref[...]


def _sgu_fwd(p, norm_g, norm_b, w_s, bias_tile):
    m = p.shape[0]
    ng = 4
    width = ng * TILE

    def body(u_ref, v_ref, g_ref, b_ref, w_ref, bias_ref, o_ref):
        u = _gelu(u_ref[...].astype(F32))
        _, _, vln = _sgu_norm(_gelu(v_ref[...].astype(F32)), g_ref, b_ref)
        vln = vln.astype(BF16)
        causal = _causal_tile()
        for g in range(ng):
            cols = slice(g * TILE, (g + 1) * TILE)
            wg = jnp.where(causal, w_ref[g], 0.0).astype(BF16)
            for n in range(SGU_ROWS // TILE):
                rows = slice(n * TILE, (n + 1) * TILE)
                s = jnp.dot(wg, vln[rows, cols], preferred_element_type=F32) + bias_ref[g]
                o_ref[rows, cols] = (u[rows, cols] * s).astype(BF16)

    vec = pl.BlockSpec((1, width), lambda i: (0, 0))
    tiles = pl.BlockSpec((ng, TILE, TILE), lambda i: (0, 0, 0))
    return pl.pallas_call(
        body, name="sgu_fwd", grid=(m // SGU_ROWS,),
        in_specs=[pl.BlockSpec((SGU_ROWS, width), lambda i: (i, 0)),
                  pl.BlockSpec((SGU_ROWS, width), lambda i: (i, 1)), vec, vec, tiles, tiles],
        out_specs=pl.BlockSpec((SGU_ROWS, width), lambda i: (i, 0)),
        out_shape=jax.ShapeDtypeStruct((m, width), BF16),
        compiler_params=_params(("parallel",)),
    )(p, p, norm_g, norm_b, w_s, bias_tile)


def _sgu_bwd(p, dmix, norm_g, norm_b, w_s, w_s_t, bias_tile):
    m = p.shape[0]
    ng = 4
    width = ng * TILE

    def body(u_ref, v_ref, dc_ref, g_ref, b_ref, w_ref, wt_ref, bias_ref,
             du_ref, dv_ref, dw_ref, dbs_ref, dg_ref, db_ref, ds_scr, dvln_scr):
        u_pre = u_ref[...].astype(F32)
        v_pre = v_ref[...].astype(F32)
        u = _gelu(u_pre)
        xhat, rstd, vln = _sgu_norm(_gelu(v_pre), g_ref, b_ref)
        vln = vln.astype(BF16)
        dc = dc_ref[...].astype(F32)
        causal = _causal_tile()
        ones = jnp.ones((TILE, TILE), BF16)
        first = pl.program_id(0) == 0
        for g in range(ng):
            cols = slice(g * TILE, (g + 1) * TILE)
            wg = jnp.where(causal, w_ref[g], 0.0).astype(BF16)
            wgt = jnp.where(_causal_tile(transposed=True), wt_ref[g], 0.0).astype(BF16)
            dw_acc = jnp.zeros((TILE, TILE), F32)
            dbs_acc = jnp.zeros((TILE, TILE), F32)
            for n in range(SGU_ROWS // TILE):
                rows = slice(n * TILE, (n + 1) * TILE)
                vt = vln[rows, cols]
                s = jnp.dot(wg, vt, preferred_element_type=F32) + bias_ref[g]
                ds_scr[rows, cols] = dc[rows, cols] * s
                ds = (dc[rows, cols] * u[rows, cols]).astype(BF16)
                dw_acc += lax.dot_general(ds, vt, NT_DIMS, preferred_element_type=F32)
                dbs_acc += jnp.dot(ds, ones, preferred_element_type=F32)
                dvln_scr[rows, cols] = jnp.dot(wgt, ds, preferred_element_type=F32)
            dw_g = jnp.where(causal, dw_acc, 0.0)

            @pl.when(first)
            def _():
                dw_ref[g] = dw_g
                dbs_ref[g] = dbs_acc

            @pl.when(jnp.logical_not(first))
            def _():
                dw_ref[g] += dw_g
                dbs_ref[g] += dbs_acc

        du_ref[...] = (ds_scr[...] * _gelu_grad(u_pre)).astype(BF16)
        dvln = dvln_scr[...]
        dxhat = dvln * g_ref[...]
        dv = rstd * (dxhat - jnp.mean(dxhat, axis=-1, keepdims=True)
                     - xhat * jnp.mean(dxhat * xhat, axis=-1, keepdims=True))
        dv_ref[...] = (dv * _gelu_grad(v_pre)).astype(BF16)
        dg_part = jnp.sum(dvln * xhat, axis=0, keepdims=True)
        db_part = jnp.sum(dvln, axis=0, keepdims=True)

        @pl.when(first)
        def _():
            dg_ref[...] = dg_part
            db_ref[...] = db_part

        @pl.when(jnp.logical_not(first))
        def _():
            dg_ref[...] += dg_part
            db_ref[...] += db_part

    vec = pl.BlockSpec((1, width), lambda i: (0, 0))
    tiles = pl.BlockSpec((ng, TILE, TILE), lambda i: (0, 0, 0))
    rows0 = pl.BlockSpec((SGU_ROWS, width), lambda i: (i, 0))
    rows1 = pl.BlockSpec((SGU_ROWS, width), lambda i: (i, 1))
    return pl.pallas_call(
        body, name="sgu_bwd", grid=(m // SGU_ROWS,),
        in_specs=[rows0, rows1, rows0, vec, vec, tiles, tiles, tiles],
        out_specs=[rows0, rows0, tiles, tiles, vec, vec],
        out_shape=[jax.ShapeDtypeStruct((m, width), BF16), jax.ShapeDtypeStruct((m, width), BF16),
                   jax.ShapeDtypeStruct((ng, TILE, TILE), F32), jax.ShapeDtypeStruct((ng, TILE, TILE), F32),
                   jax.ShapeDtypeStruct((1, width), F32), jax.ShapeDtypeStruct((1, width), F32)],
        scratch_shapes=[pltpu.VMEM((SGU_ROWS, width), F32), pltpu.VMEM((SGU_ROWS, width), F32)],
        compiler_params=_params(("arbitrary",)),
    )(p, p, dmix, norm_g, norm_b, w_s, w_s_t, bias_tile)


SB_DH = 64
SB_SCALE = 1.0 / math.sqrt(SB_DH)


SB_BLOCK = 256
SB_SUB = SB_BLOCK // TILE


def _sum_matrix(kind):
    j = lax.broadcasted_iota(jnp.int32, (TILE, 2 * TILE), 0)
    s = lax.broadcasted_iota(jnp.int32, (TILE, 2 * TILE), 1)
    tri = {"after": j > s, "upto": j <= s, "before": j < s}[kind]
    return jnp.where(jnp.logical_or(s >= TILE, tri), 1.0, 0.0).astype(BF16)


def _strict_mask():
    r = lax.broadcasted_iota(jnp.int32, (SB_BLOCK, SB_BLOCK), 0)
    c = lax.broadcasted_iota(jnp.int32, (SB_BLOCK, SB_BLOCK), 1)
    return c < r


def _head_lanes(h):
    lane = lax.broadcasted_iota(jnp.int32, (1, TILE), 1)
    return (lane >= h * SB_DH) & (lane < (h + 1) * SB_DH)


def _softplus(z):
    return jnp.maximum(z, 0.0) + jnp.log(1.0 + jnp.exp(-jnp.abs(z)))


def _sb_fwd(p, nseq, t_len):
    m = p.shape[0]
    nb = t_len // TILE
    npair = 4

    def body(q_ref, k_ref, v_ref, o_ref, lt_ref, kh_ref, vh_ref):
        for h in range(2):
            keep = _head_lanes(h)
            kh_ref[h] = jnp.where(keep, k_ref[...], 0).astype(BF16)
            vh_ref[h] = jnp.where(keep, v_ref[...], 0).astype(BF16)
        summat = _sum_matrix("after")
        strict = _strict_mask()

        def one_pass(q, row0, diag, state):
            rows = pl.ds(row0, SB_BLOCK)
            z, sp, pieces = [], [], []
            for h in range(2):
                zh = lax.dot_general(q, kh_ref[h, rows, :], NT_DIMS, preferred_element_type=F32) * SB_SCALE
                sph = _softplus(zh)
                logkeep = jnp.where(strict, -sph, 0.0) if diag else -sph
                z.append(zh)
                sp.append(sph)
                pieces += [logkeep[:, b * TILE:(b + 1) * TILE] for b in range(SB_SUB)]
            sums = jnp.dot(jnp.concatenate(pieces, axis=0).astype(BF16), summat, preferred_element_type=F32)
            out = []
            for h in range(2):
                carry, acc = state[2 * h], state[2 * h + 1]
                after = [None] * SB_SUB
                for b in reversed(range(SB_SUB)):
                    part = sums[(h * SB_SUB + b) * SB_BLOCK:(h * SB_SUB + b + 1) * SB_BLOCK]
                    after[b] = part[:, :TILE] + carry
                    carry = carry + part[:, TILE:]
                w = jnp.exp(z[h] - sp[h] + jnp.concatenate(after, axis=1))
                if diag:
                    w = jnp.where(strict, w, 0.0)
                out += [carry, acc + jnp.dot(w.astype(BF16), vh_ref[h, rows, :], preferred_element_type=F32)]
            return tuple(out)

        def q_block(i, _):
            r0 = pl.multiple_of(i * SB_BLOCK, SB_BLOCK)
            q = q_ref[pl.ds(r0, SB_BLOCK), :]
            zero = jnp.zeros((SB_BLOCK, TILE), F32)
            state = one_pass(q, r0, True, (zero,) * 4)
            state = lax.fori_loop(
                0, i, lambda jj, st: one_pass(q, pl.multiple_of((i - 1 - jj) * SB_BLOCK, SB_BLOCK), False, st), state)
            o_ref[pl.ds(r0, SB_BLOCK), :] = (state[1] + state[3]).astype(BF16)
            lt_ref[pl.ds(r0, SB_BLOCK), :] = jnp.where(_head_lanes(0), state[0], state[2])
            return 0

        lax.fori_loop(0, t_len // SB_BLOCK, q_block, 0)

    def col(k):
        return pl.BlockSpec((t_len, TILE), lambda s, hp: (s, k * npair + hp))

    out = pl.BlockSpec((t_len, TILE), lambda s, hp: (s, hp))
    return pl.pallas_call(
        body, name="stickbreak_fwd", grid=(nseq, npair), in_specs=[col(2), col(3), col(4)],
        out_specs=[out, out],
        out_shape=[jax.ShapeDtypeStruct((m, npair * TILE), BF16), jax.ShapeDtypeStruct((m, npair * TILE), F32)],
        scratch_shapes=[pltpu.VMEM((2, t_len, TILE), BF16), pltpu.VMEM((2, t_len, TILE), BF16)],
        compiler_params=_params(("parallel", "parallel")),
    )(p, p, p)


def _sb_bwd(p, dmix, ltot, nseq, t_len):
    m = p.shape[0]
    nb = t_len // TILE
    npair = 4

    def body(q_ref, k_ref, v_ref, do_ref, lt_ref, dq_ref, dk_ref, dv_ref, kh_ref, vh_ref, dk_acc, dv_acc):
        for h in range(2):
            keep = _head_lanes(h)
            kh_ref[h] = jnp.where(keep, k_ref[...], 0).astype(BF16)
            vh_ref[h] = jnp.where(keep, v_ref[...], 0).astype(BF16)
        dk_acc[...] = jnp.zeros_like(dk_acc)
        dv_acc[...] = jnp.zeros_like(dv_acc)
        sum_upto = _sum_matrix("upto")
        sum_before = _sum_matrix("before")
        strict = _strict_mask()
        lane = lax.broadcasted_iota(jnp.int32, (SB_BLOCK, TILE), 1)

        def running(x, matrix, start):
            pieces = [x[h][:, b * TILE:(b + 1) * TILE] for h in range(2) for b in range(SB_SUB)]
            sums = jnp.dot(jnp.concatenate(pieces, axis=0).astype(BF16), matrix, preferred_element_type=F32)
            wide, ends = [], []
            for h in range(2):
                total, cols = start[h], []
                for b in range(SB_SUB):
                    part = sums[(h * SB_SUB + b) * SB_BLOCK:(h * SB_SUB + b + 1) * SB_BLOCK]
                    cols.append(part[:, :TILE] + total)
                    total = total + part[:, TILE:]
                wide.append(jnp.concatenate(cols, axis=1))
                ends.append(total)
            return wide, ends

        def one_pass(q, do, qh, doh, ltot, row0, diag, state):
            rows = pl.ds(row0, SB_BLOCK)
            z, sp, logkeep = [], [], []
            for h in range(2):
                zh = lax.dot_general(q, kh_ref[h, rows, :], NT_DIMS, preferred_element_type=F32) * SB_SCALE
                sph = _softplus(zh)
                z.append(zh)
                sp.append(sph)
                logkeep.append(jnp.where(strict, -sph, 0.0) if diag else -sph)
            upto, sum_l = running(logkeep, sum_upto, [state[0], state[3]])
            w, g = [], []
            for h in range(2):
                wh = jnp.exp(z[h] - sp[h] + (ltot[h] - upto[h]))
                if diag:
                    wh = jnp.where(strict, wh, 0.0)
                w.append(wh)
                g.append(wh * lax.dot_general(do, vh_ref[h, rows, :], NT_DIMS, preferred_element_type=F32))
            g_before, sum_g = running(g, sum_before, [state[1], state[4]])
            out, dk_new, dv_new = [], 0.0, 0.0
            for h in range(2):
                dz = (g[h] - jnp.exp(z[h] - sp[h]) * (g[h] + g_before[h])) * SB_SCALE
                if diag:
                    dz = jnp.where(strict, dz, 0.0)
                dzb = dz.astype(BF16)
                dq = state[3 * h + 2] + jnp.dot(dzb, kh_ref[h, rows, :], preferred_element_type=F32)
                dk_new = dk_new + lax.dot_general(dzb, qh[h], TN_DIMS, preferred_element_type=F32)
                dv_new = dv_new + lax.dot_general(w[h].astype(BF16), doh[h], TN_DIMS, preferred_element_type=F32)
                out += [sum_l[h], sum_g[h], dq]
            dk_acc[rows, :] += dk_new
            dv_acc[rows, :] += dv_new
            return tuple(out)

        def q_block(i, _):
            r0 = pl.multiple_of(i * SB_BLOCK, SB_BLOCK)
            q = q_ref[pl.ds(r0, SB_BLOCK), :]
            do = do_ref[pl.ds(r0, SB_BLOCK), :]
            lt = lt_ref[pl.ds(r0, SB_BLOCK), :]
            qh, doh, ltot = [], [], []
            for h in range(2):
                keep = _head_lanes(h)
                qh.append(jnp.where(keep, q, 0).astype(BF16))
                doh.append(jnp.where(keep, do, 0).astype(BF16))
                ltot.append(jnp.sum(jnp.where(lane == h * SB_DH, lt, 0.0), axis=1, keepdims=True))
            zero = jnp.zeros((SB_BLOCK, TILE), F32)
            state = lax.fori_loop(
                0, i,
                lambda jj, st: one_pass(q, do, qh, doh, ltot, pl.multiple_of(jj * SB_BLOCK, SB_BLOCK), False, st),
                (zero,) * 6)
            state = one_pass(q, do, qh, doh, ltot, r0, True, state)
            dq_ref[pl.ds(r0, SB_BLOCK), :] = (state[2] + state[5]).astype(BF16)
            return 0

        lax.fori_loop(0, t_len // SB_BLOCK, q_block, 0)
        dk_ref[...] = dk_acc[...].astype(BF16)
        dv_ref[...] = dv_acc[...].astype(BF16)

    def col(k):
        return pl.BlockSpec((t_len, TILE), lambda s, hp: (s, k * npair + hp))

    out = pl.BlockSpec((t_len, TILE), lambda s, hp: (s, hp))
    width = npair * TILE
    return pl.pallas_call(
        body, name="stickbreak_bwd", grid=(nseq, npair),
        in_specs=[col(2), col(3), col(4), col(1), out], out_specs=[out, out, out],
        out_shape=[jax.ShapeDtypeStruct((m, width), BF16)] * 3,
        scratch_shapes=[pltpu.VMEM((2, t_len, TILE), BF16), pltpu.VMEM((2, t_len, TILE), BF16),
                        pltpu.VMEM((t_len, TILE), F32), pltpu.VMEM((t_len, TILE), F32)],
        compiler_params=_params(("parallel", "parallel")),
    )(p, p, p, dmix, ltot)


def _adam_math(w, g, m, v):
    m = ADAM_B1 * m + (1.0 - ADAM_B1) * g
    v = ADAM_B2 * v + (1.0 - ADAM_B2) * (g * g)
    m_hat = m / (1.0 - ADAM_B1 ** ADAM_STEP)
    v_hat = v / (1.0 - ADAM_B2 ** ADAM_STEP)
    delta = -ADAM_LR * (m_hat / (jnp.sqrt(v_hat) + ADAM_EPS) + ADAM_WD * w)
    return delta, m, v


def _cast_place(w, pos, *, name):
    l_dim, r, c = w.shape
    tr = min(r, 256)

    def body(pos_ref, w_ref, o_ref):
        o_ref[...] = w_ref[...].astype(BF16)

    grid_spec = pltpu.PrefetchScalarGridSpec(
        num_scalar_prefetch=1, grid=(l_dim, r // tr),
        in_specs=[pl.BlockSpec((None, tr, c), lambda l, i, pos_ref: (l, i, 0))],
        out_specs=pl.BlockSpec((None, None, tr, c), lambda l, i, pos_ref: (l, pos_ref[0], i, 0)))
    return pl.pallas_call(
        body, name=name, grid_spec=grid_spec, out_shape=jax.ShapeDtypeStruct((l_dim, N_CHIP, r, c), BF16),
        compiler_params=_params(("parallel",) * 2),
    )(pos, w)


def _pair_sum(mine, got, pos, *, name):
    l_dim, s_dim, h, c = got.shape
    th = min(h, 512)
    nt = h // th

    def body(pos_ref, a_ref, b_ref, o_ref):
        o_ref[...] = (a_ref[...].astype(F32) + b_ref[...].astype(F32)).astype(BF16)

    spec = pl.BlockSpec((None, None, th, c), lambda l, s, i, pos_ref: (l, s, i, 0))
    grid_spec = pltpu.PrefetchScalarGridSpec(
        num_scalar_prefetch=1, grid=(l_dim, s_dim, nt),
        in_specs=[pl.BlockSpec((None, None, th, c), lambda l, s, i, pos_ref: (l, s, pos_ref[1] * nt + i, 0)), spec],
        out_specs=spec)
    return pl.pallas_call(
        body, name=name, grid_spec=grid_spec, out_shape=jax.ShapeDtypeStruct(got.shape, BF16),
        compiler_params=_params(("parallel",) * 3),
    )(pos, mine, got)


def _chip_sum(sums, landed, pos, *, name):
    l_dim, _, h, c = sums.shape
    th = min(h, 512)
    nt = h // th

    def body(pos_ref, own, r0, r1, r2, o_ref):
        o_ref[...] = ((own[...].astype(F32) + r0[...].astype(F32)) + r1[...].astype(F32)) + r2[...].astype(F32)

    def piece(k):
        return pl.BlockSpec((None, None, th, c), lambda l, i, pos_ref: (l, k, i, 0))

    grid_spec = pltpu.PrefetchScalarGridSpec(
        num_scalar_prefetch=1, grid=(l_dim, nt),
        in_specs=[pl.BlockSpec((None, None, th, c), lambda l, i, pos_ref: (l, pos_ref[0], i, 0)),
                  piece(0), piece(1), piece(2)],
        out_specs=pl.BlockSpec((None, th, c), lambda l, i, pos_ref: (l, pos_ref[1] * nt + i, 0)))
    return pl.pallas_call(
        body, name=name, grid_spec=grid_spec, out_shape=jax.ShapeDtypeStruct((l_dim, 2 * h, c), F32),
        compiler_params=_params(("parallel",) * 2),
    )(pos, sums, landed, landed, landed)


def _adam_big(w, m, v, grads, *, name):
    l_dim, r, c = w.shape
    assert len(grads) == l_dim
    tr = min(r, 256)

    def body(*refs):
        w_ref, m_ref, v_ref = refs[:3]
        g_refs = refs[3:3 + l_dim]
        go_ref, d_ref, mo_ref, vo_ref = refs[3 + l_dim:]
        g = g_refs[0][...]
        for l in range(1, l_dim):
            g = jnp.where(pl.program_id(0) == l, g_refs[l][...], g)
        delta, m_new, v_new = _adam_math(w_ref[...], g, m_ref[...], v_ref[...])
        go_ref[...] = g
        d_ref[...] = delta
        mo_ref[...] = m_new
        vo_ref[...] = v_new

    spec = pl.BlockSpec((None, tr, c), lambda l, i: (l, i, 0))
    gspec = pl.BlockSpec((None, tr, c), lambda l, i: (0, i, 0))
    return pl.pallas_call(
        body, name=name, grid=(l_dim, r // tr), in_specs=[spec] * 3 + [gspec] * l_dim, out_specs=[spec] * 4,
        out_shape=[jax.ShapeDtypeStruct(w.shape, F32)] * 4, compiler_params=_params(("parallel",) * 2),
    )(w, m, v, *grads)


def _position():
    return lax.axis_index("x"), lax.axis_index("y"), lax.axis_index("c")


def _other_chips(x, y):
    return [(1 - x, y), (x, 1 - y), (1 - x, 1 - y)]


def _remote(src, dst, send_sem, recv_sem, device):
    return pltpu.make_async_remote_copy(src_ref=src, dst_ref=dst, send_sem=send_sem, recv_sem=recv_sem,
                                        device_id=device, device_id_type=MESH)


ANY = pl.BlockSpec(memory_space=pl.ANY)


def _gather_weights(bufs):
    n = len(bufs)

    def body(*refs):
        outs = refs[n:2 * n]
        send_sems, recv_sems, fwd_send, fwd_recv = refs[2 * n:]
        x, y, c = _position()
        chips = _other_chips(x, y)
        sibling = (x, y, 1 - c)

        def half(a, chip, core):
            h = bufs[a].shape[2] // 2
            return outs[a].at[:, 2 * chip[0] + chip[1], pl.ds(core * h, h), :]

        sends = []
        for a in range(n):
            mine = half(a, (x, y), c)
            for k, chip in enumerate(chips):
                cp = _remote(mine, mine, send_sems.at[3 * a + k], recv_sems.at[3 * a + k], (*chip, c))
                cp.start()
                sends.append(cp)
        for k, chip in enumerate(chips):
            for a in range(n):
                landed = half(a, chip, c)
                _remote(landed, landed, send_sems.at[3 * a + k], recv_sems.at[3 * a + k], (*chip, c)).wait_recv()
                cp = _remote(landed, landed, fwd_send.at[3 * a + k], fwd_recv.at[3 * a + k], sibling)
                cp.start()
                sends.append(cp)
        for k, chip in enumerate(chips):
            for a in range(n):
                got = half(a, chip, 1 - c)
                _remote(got, got, fwd_send.at[3 * a + k], fwd_recv.at[3 * a + k], sibling).wait_recv()
        for cp in sends:
            cp.wait_send()

    sem = pltpu.SemaphoreType.DMA((3 * n,))
    return pl.pallas_call(
        body, name="gather_weights", in_specs=[ANY] * n, out_specs=[ANY] * n,
        out_shape=[jax.ShapeDtypeStruct(b.shape, b.dtype) for b in bufs],
        input_output_aliases={a: a for a in range(n)},
        scratch_shapes=[sem, sem, sem, sem],
        compiler_params=pltpu.CompilerParams(has_side_effects=True),
    )(*bufs)


def _swap_halves(grads):
    n = len(grads)

    def body(*refs):
        ins, got = refs[:n], refs[n:2 * n]
        send_sems, recv_sems = refs[2 * n:]
        x, y, c = _position()
        sibling = (x, y, 1 - c)
        copies = []
        for a in range(n):
            h = grads[a].shape[2] // 2
            cp = _remote(ins[a].at[:, :, pl.ds((1 - c) * h, h), :], got[a], send_sems.at[a], recv_sems.at[a], sibling)
            cp.start()
            copies.append(cp)
        for cp in copies:
            cp.wait()

    sem = pltpu.SemaphoreType.DMA((n,))
    return pl.pallas_call(
        body, name="swap_halves", in_specs=[ANY] * n, out_specs=[ANY] * n,
        out_shape=[jax.ShapeDtypeStruct(g.shape[:2] + (g.shape[2] // 2, g.shape[3]), g.dtype) for g in grads],
        scratch_shapes=[sem, sem], compiler_params=pltpu.CompilerParams(has_side_effects=True),
    )(*grads)


def _exchange_chips(sums):
    n = len(sums)

    def body(*refs):
        ins, outs = refs[:n], refs[n:2 * n]
        send_sems, recv_sems = refs[2 * n:]
        x, y, c = _position()
        chips = _other_chips(x, y)
        sends = []
        for a in range(n):
            for k, chip in enumerate(chips):
                cp = _remote(ins[a].at[:, 2 * chip[0] + chip[1]], outs[a].at[:, k],
                             send_sems.at[3 * a + k], recv_sems.at[3 * a + k], (*chip, c))
                cp.start()
                sends.append(cp)
        for cp in sends:
            cp.wait()

    sem = pltpu.SemaphoreType.DMA((3 * n,))
    return pl.pallas_call(
        body, name="exchange_chips", in_specs=[ANY] * n, out_specs=[ANY] * n,
        out_shape=[jax.ShapeDtypeStruct((s.shape[0], 3) + s.shape[2:], s.dtype) for s in sums],
        scratch_shapes=[sem, sem], compiler_params=pltpu.CompilerParams(has_side_effects=True),
    )(*sums)


def _join_halves(bufs):
    n = len(bufs)

    def body(*refs):
        outs = refs[n:2 * n]
        send_sems, recv_sems = refs[2 * n:]
        x, y, c = _position()
        sibling = (x, y, 1 - c)
        copies = []
        for a in range(n):
            h = bufs[a].shape[1] // 2
            mine = outs[a].at[:, pl.ds(c * h, h), :]
            cp = _remote(mine, mine, send_sems.at[a], recv_sems.at[a], sibling)
            cp.start()
            copies.append((cp, a, h))
        for cp, a, h in copies:
            cp.wait_send()
            got = outs[a].at[:, pl.ds((1 - c) * h, h), :]
            _remote(got, got, send_sems.at[a], recv_sems.at[a], sibling).wait_recv()

    sem = pltpu.SemaphoreType.DMA((n,))
    return pl.pallas_call(
        body, name="join_halves", in_specs=[ANY] * n, out_specs=[ANY] * n,
        out_shape=[jax.ShapeDtypeStruct(b.shape, b.dtype) for b in bufs],
        input_output_aliases={a: a for a in range(n)},
        scratch_shapes=[sem, sem], compiler_params=pltpu.CompilerParams(has_side_effects=True),
    )(*bufs)


def _allreduce_small(packs):
    n = len(packs)

    def body(*refs):
        ins, outs, gath = refs[:n], refs[n:2 * n], refs[2 * n:3 * n]
        send_sems, recv_sems = refs[3 * n:]
        x, y, c = _position()
        me, sibling = (x, y, c), (x, y, 1 - c)
        chips = _other_chips(x, y)

        def slot(a, dev):
            return gath[a].at[4 * dev[0] + 2 * dev[1] + dev[2]]

        def copy(a, k, block, to, src=None):
            return _remote(slot(a, block) if src is None else src, slot(a, block),
                           send_sems.at[7 * a + k], recv_sems.at[7 * a + k], to)

        started = []
        for a in range(n):
            slot(a, me)[...] = ins[a][...]
            first = [copy(a, 0, me, sibling, src=ins[a])]
            first += [copy(a, 1 + k, me, (*chip, c), src=ins[a]) for k, chip in enumerate(chips)]
            for cp in first:
                cp.start()
            started += first
        for a in range(n):
            for k, chip in enumerate(chips):
                copy(a, 1 + k, (*chip, c), me).wait_recv()
                cp = copy(a, 4 + k, (*chip, c), sibling)
                cp.start()
                started.append(cp)
        for a in range(n):
            copy(a, 0, sibling, me).wait_recv()
            for k, chip in enumerate(chips):
                copy(a, 4 + k, (*chip, 1 - c), me).wait_recv()
        for cp in started:
            cp.wait_send()
        for a in range(n):
            total = gath[a][0]
            for d in range(1, N_DEV):
                total = total + gath[a][d]
            outs[a][...] = total

    vmem = pl.BlockSpec(memory_space=pltpu.VMEM)
    sem = pltpu.SemaphoreType.DMA((7 * n,))
    return pl.pallas_call(
        body, name="allreduce_small", in_specs=[vmem] * n, out_specs=[vmem] * n,
        out_shape=[jax.ShapeDtypeStruct(p.shape, p.dtype) for p in packs],
        scratch_shapes=[pltpu.VMEM((N_DEV,) + p.shape, p.dtype) for p in packs] + [sem, sem],
        compiler_params=pltpu.CompilerParams(has_side_effects=True, vmem_limit_bytes=VMEM_LIMIT_BYTES),
    )(*packs)


LOSS_ROW = 1040


def _pad_rows(a, rows=8):
    return jnp.concatenate([a, jnp.zeros((rows - a.shape[0], a.shape[1]), a.dtype)], axis=0)

def _adam_small(wide, mid, narrow, params):
    names = ["mix_norm_g", "mlp_norm_g", "final_norm_g", "conv_b", "conv_w", "sgu_norm_g", "sgu_norm_b",
             "pool_w", "pool_scale", "sgu_w", "sgu_b"]
    n = len(names)

    def body(*refs):
        wide_ref, mid_ref, narrow_ref = refs[:3]
        wmv = refs[3:3 + 3 * n]
        outs = refs[3 + 3 * n:]
        x, y, _ = _position()
        q = 2 * x + y

        def my_quarter(rows):
            parts = [rows[:, s * TILE:(s + 1) * TILE] for s in range(N_CHIP)]
            return jnp.where(q == 0, parts[0], jnp.where(q == 1, parts[1], jnp.where(q == 2, parts[2], parts[3])))

        def tiles(first_row):
            return [((0, g), narrow_ref[first_row + g * TILE:first_row + (g + 1) * TILE, :]) for g in range(4)]

        grads = {
            "mix_norm_g": [((), wide_ref[0:2, :])],
            "mlp_norm_g": [((), wide_ref[8:10, :])],
            "final_norm_g": [((), wide_ref[16:17, :])],
            "conv_b": [((), mid_ref[0:1, :])],
            "conv_w": [((0,), my_quarter(mid_ref[8:11, :]))],
            "sgu_norm_g": [((), my_quarter(mid_ref[16:17, :]))],
            "sgu_norm_b": [((), my_quarter(mid_ref[24:25, :]))],
            "pool_w": tiles(0),
            "sgu_w": tiles(512),
            "pool_scale": [((0,), narrow_ref[1024:1028, :])],
            "sgu_b": [((0,), narrow_ref[1032:1036, :])],
        }
        for i, name in enumerate(names):
            w_ref, m_ref, v_ref = wmv[3 * i:3 * i + 3]
            for lead, g in grads[name]:
                idx = lead + (slice(None), slice(None))
                delta, m_new, v_new = _adam_math(w_ref[idx], g, m_ref[idx], v_ref[idx])
                outs[4 * i][idx] = g
                outs[4 * i + 1][idx] = delta
                outs[4 * i + 2][idx] = m_new
                outs[4 * i + 3][idx] = v_new

    vmem = pl.BlockSpec(memory_space=pltpu.VMEM)
    args, out_shape = [wide, mid, narrow], []
    for name in names:
        w, m, v = params[name]
        args += [w, m, v]
        out_shape += [jax.ShapeDtypeStruct(w.shape, F32)] * 4
    res = pl.pallas_call(
        body, name="adam_small", in_specs=[vmem] * len(args), out_specs=[vmem] * len(out_shape),
        out_shape=out_shape, compiler_params=pltpu.CompilerParams(vmem_limit_bytes=VMEM_LIMIT_BYTES),
    )(*args)
    return {name: res[4 * i:4 * i + 4] for i, name in enumerate(names)}


def _reduce_big(grads, pos):
    got = _swap_halves(grads)
    sums = [_pair_sum(a, b, pos, name=f"pair_sum_{i}") for i, (a, b) in enumerate(zip(grads, got))]
    landed = _exchange_chips(sums)
    halves = [_chip_sum(s, r, pos, name=f"chip_sum_{i}") for i, (s, r) in enumerate(zip(sums, landed))]
    return _join_halves(halves)


def kernel(x, mix_norm_g, mlp_norm_g, ab_w_in, pool_w, pool_scale, conv_w, conv_b, ab_w_out, cd_w_in, sgu_norm_g, sgu_norm_b, sgu_w, sgu_b, cd_w_out, mlp_w1, mlp_w2, final_norm_g, loss_target, m_mix_norm_g, m_mlp_norm_g, m_ab_w_in, m_pool_w, m_pool_scale, m_conv_w, m_conv_b, m_ab_w_out, m_cd_w_in, m_sgu_norm_g, m_sgu_norm_b, m_sgu_w, m_sgu_b, m_cd_w_out, m_mlp_w1, m_mlp_w2, m_final_norm_g, v_mix_norm_g, v_mlp_norm_g, v_ab_w_in, v_pool_w, v_pool_scale, v_conv_w, v_conv_b, v_ab_w_out, v_cd_w_in, v_sgu_norm_g, v_sgu_norm_b, v_sgu_w, v_sgu_b, v_cd_w_out, v_mlp_w1, v_mlp_w2, v_final_norm_g):
    nseq, t_len, d = x.shape
    m_tok = nseq * t_len
    h0 = x.reshape(m_tok, d)
    target = loss_target.reshape(m_tok, d)

    x_idx, y_idx = lax.axis_index("x"), lax.axis_index("y")
    q_idx = 2 * x_idx + y_idx
    pos = jnp.stack([q_idx, lax.axis_index("c")]).astype(jnp.int32)
    w_ab_in, w_ab_out, w_cd_in, w_cd_out, w_1, w_2 = _gather_weights(
        [_cast_place(w, pos, name=f"cast_place_{i}")
         for i, w in enumerate((ab_w_in, ab_w_out, cd_w_in, cd_w_out, mlp_w1, mlp_w2))])
    w_ab_out = w_ab_out.reshape(1, 1, -1, d)
    w_cd_out = w_cd_out.reshape(1, 1, -1, d)
    w_2 = w_2.reshape(2, 1, -1, d)

    pool_w3, pool_scale3 = pool_w[0], pool_scale[0].reshape(4, 1, TILE)
    sgu_w3 = sgu_w[0]
    sgu_w3_t = jnp.swapaxes(sgu_w3, 1, 2)
    sgu_bias_tile = jnp.broadcast_to(sgu_b[0][:, :, None], (4, TILE, TILE))
    conv_w2, conv_b2 = conv_w[0], conv_b
    def place_quarter(v):
        return lax.dynamic_update_slice(jnp.zeros((v.shape[0], 4 * TILE), F32), v, (0, q_idx * TILE))

    sharded_small = jnp.concatenate(
        [place_quarter(conv_w[0]), place_quarter(sgu_norm_g), place_quarter(sgu_norm_b),
         jnp.zeros((3, 4 * TILE), F32)], axis=0)
    sharded_small, = _allreduce_small([sharded_small])
    sharded_small = sharded_small * 0.5
    conv_w_full = sharded_small[0:3]
    sgu_g_full = sharded_small[3:4]
    sgu_b_full = sharded_small[4:5]

    xn0 = _rms_fwd(h0, mix_norm_g[0:1], name="rms_fwd_mix0")
    p_ab = _mm_nn(xn0, w_ab_in, 0, out_dtype=BF16, name="ab_in_proj")
    mix0 = _ab_fwd(p_ab, pool_w3, pool_scale3, conv_w_full, conv_b2, nseq, t_len)
    h1 = _mm_nn(mix0, w_ab_out, 0, out_dtype=F32, name="ab_out_proj", epilogue="residual", extra=h0)
    hn0 = _rms_fwd(h1, mlp_norm_g[0:1], name="rms_fwd_mlp0")
    act0 = _mm_nn(hn0, w_1, 0, out_dtype=BF16, name="mlp0_up", epilogue="relu2")
    h2 = _mm_nn(act0, w_2, 0, out_dtype=F32, name="mlp0_down", epilogue="residual", extra=h1)

    xn1 = _rms_fwd(h2, mix_norm_g[1:2], name="rms_fwd_mix1")
    p_cd = _mm_nn(xn1, w_cd_in, 0, out_dtype=BF16, name="cd_in_proj")
    c_out = _sgu_fwd(p_cd, sgu_g_full, sgu_b_full, sgu_w3, sgu_bias_tile)
    d_out, ltot = _sb_fwd(p_cd, nseq, t_len)
    mix1 = jnp.concatenate([c_out, d_out], axis=1)
    h3 = _mm_nn(mix1, w_cd_out, 0, out_dtype=F32, name="cd_out_proj", epilogue="residual", extra=h2)
    hn1 = _rms_fwd(h3, mlp_norm_g[1:2], name="rms_fwd_mlp1")
    act1 = _mm_nn(hn1, w_1, 1, out_dtype=BF16, name="mlp1_up", epilogue="relu2")
    h4 = _mm_nn(act1, w_2, 1, out_dtype=F32, name="mlp1_down", epilogue="residual", extra=h3)

    dh4, dh4_bf, dg_final, loss_tile = _final_loss(h4, final_norm_g.reshape(1, d), target)

    def mlp_bwd(dh_out, dh_out_bf, h_in, hn, act, layer, tag):
        dz = _mm_nt(dh_out_bf, w_2, layer, out_dtype=BF16, name=f"mlp{tag}_down_bwd",
                    epilogue="relu2_bwd", extra=act)
        g_w2 = _mm_tn(act, dh_out_bf, 1, name=f"mlp{tag}_down_wgrad")
        g_w1 = _mm_tn(hn, dz, N_CHIP, name=f"mlp{tag}_up_wgrad")
        dhn = _mm_nt(dz, w_1, layer, out_dtype=F32, name=f"mlp{tag}_up_bwd")
        dh_in, dh_in_bf, dg = _rms_bwd(h_in, mlp_norm_g[layer:layer + 1], dhn, dh_out, name=f"rms_bwd_mlp{tag}")
        return dh_in, dh_in_bf, dg, g_w1, g_w2

    dh3, dh3_bf, dg_mlp1, g_w1_1, g_w2_1 = mlp_bwd(dh4, dh4_bf, h3, hn1, act1, 1, "1")

    dmix1 = _mm_nt(dh3_bf, w_cd_out, 0, out_dtype=BF16, name="cd_out_bwd")
    g_cd_out = _mm_tn(mix1, dh3_bf, 1, name="cd_out_wgrad")
    du, dv, dsgu_w, dsgu_bs, dsgu_g, dsgu_b = _sgu_bwd(p_cd, dmix1, sgu_g_full, sgu_b_full, sgu_w3, sgu_w3_t,
                                                      sgu_bias_tile)
    dq, dk, dvv = _sb_bwd(p_cd, dmix1, ltot, nseq, t_len)
    dp_cd = jnp.concatenate([du, dv, dq, dk, dvv], axis=1)
    g_cd_in = _mm_tn(xn1, dp_cd, N_CHIP, name="cd_in_wgrad")
    dxn1 = _mm_nt(dp_cd, w_cd_in, 0, out_dtype=F32, name="cd_in_bwd")
    dh2, dh2_bf, dg_mix1 = _rms_bwd(h2, mix_norm_g[1:2], dxn1, dh3, name="rms_bwd_mix1")

    dh1, dh1_bf, dg_mlp0, g_w1_0, g_w2_0 = mlp_bwd(dh2, dh2_bf, h1, hn0, act0, 0, "0")

    dmix0 = _mm_nt(dh1_bf, w_ab_out, 0, out_dtype=BF16, name="ab_out_bwd")
    g_ab_out = _mm_tn(mix0, dh1_bf, 1, name="ab_out_wgrad")
    da, dxb, dgb, dgc, dpool_w, dpool_scale, dconv_w, dconv_b = _ab_bwd(
        p_ab, dmix0, pool_w3, pool_scale3, conv_w_full, conv_b2, nseq, t_len)
    dp_ab = jnp.concatenate([da, dxb, dgb, dgc], axis=1)
    g_ab_in = _mm_tn(xn0, dp_ab, N_CHIP, name="ab_in_wgrad")
    dxn0 = _mm_nt(dp_ab, w_ab_in, 0, out_dtype=F32, name="ab_in_bwd")
    grad_x, _, dg_mix0 = _rms_bwd(h0, mix_norm_g[0:1], dxn0, dh1, name="rms_bwd_mix0")

    def as_pieces(g):
        return g.reshape(1, N_CHIP, -1, g.shape[-1]) if g.shape[1] == 1 else g

    big = [as_pieces(g) for g in (g_ab_in, g_ab_out, g_cd_in, g_cd_out, g_w1_0, g_w1_1, g_w2_0, g_w2_1)]
    r_ab_in, r_ab_out, r_cd_in, r_cd_out, r_w1_0, r_w1_1, r_w2_0, r_w2_1 = _reduce_big(big, pos)

    big_out = {
        "ab_w_in": _adam_big(ab_w_in, m_ab_w_in, v_ab_w_in, [r_ab_in], name="adam_ab_w_in"),
        "ab_w_out": _adam_big(ab_w_out, m_ab_w_out, v_ab_w_out, [r_ab_out], name="adam_ab_w_out"),
        "cd_w_in": _adam_big(cd_w_in, m_cd_w_in, v_cd_w_in, [r_cd_in], name="adam_cd_w_in"),
        "cd_w_out": _adam_big(cd_w_out, m_cd_w_out, v_cd_w_out, [r_cd_out], name="adam_cd_w_out"),
        "mlp_w1": _adam_big(mlp_w1, m_mlp_w1, v_mlp_w1, [r_w1_0, r_w1_1], name="adam_mlp_w1"),
        "mlp_w2": _adam_big(mlp_w2, m_mlp_w2, v_mlp_w2, [r_w2_0, r_w2_1], name="adam_mlp_w2"),
    }

    wide = jnp.concatenate([_pad_rows(jnp.concatenate([dg_mix0, dg_mix1], axis=0)),
                            _pad_rows(jnp.concatenate([dg_mlp0, dg_mlp1], axis=0)), _pad_rows(dg_final)], axis=0)
    mid = jnp.concatenate([_pad_rows(dconv_b), _pad_rows(dconv_w), _pad_rows(dsgu_g), _pad_rows(dsgu_b)], axis=0)
    narrow = jnp.concatenate(
        [dpool_w.reshape(4 * TILE, TILE), dsgu_w.reshape(4 * TILE, TILE), _pad_rows(dpool_scale.reshape(4, TILE)),
         _pad_rows(dsgu_bs[:, :, 0]), loss_tile], axis=0)
    wide, mid, narrow = _allreduce_small([wide, mid, narrow])
    small_out = _adam_small(wide, mid, narrow, {
        "mix_norm_g": (mix_norm_g, m_mix_norm_g, v_mix_norm_g),
        "mlp_norm_g": (mlp_norm_g, m_mlp_norm_g, v_mlp_norm_g),
        "final_norm_g": tuple(a.reshape(1, d) for a in (final_norm_g, m_final_norm_g, v_final_norm_g)),
        "conv_b": (conv_b, m_conv_b, v_conv_b),
        "conv_w": (conv_w, m_conv_w, v_conv_w),
        "sgu_norm_g": (sgu_norm_g, m_sgu_norm_g, v_sgu_norm_g),
        "sgu_norm_b": (sgu_norm_b, m_sgu_norm_b, v_sgu_norm_b),
        "pool_w": (pool_w, m_pool_w, v_pool_w),
        "pool_scale": (pool_scale, m_pool_scale, v_pool_scale),
        "sgu_w": (sgu_w, m_sgu_w, v_sgu_w),
        "sgu_b": (sgu_b, m_sgu_b, v_sgu_b),
    })
    small_out["final_norm_g"] = [a.reshape(d) for a in small_out["final_norm_g"]]

    order = ["mix_norm_g", "mlp_norm_g", "ab_w_in", "pool_w", "pool_scale", "conv_w", "conv_b", "ab_w_out",
             "cd_w_in", "sgu_norm_g", "sgu_norm_b", "sgu_w", "sgu_b", "cd_w_out", "mlp_w1", "mlp_w2",
             "final_norm_g"]
    both = {**big_out, **small_out}
    loss = narrow[LOSS_ROW, 0]
    outs = [loss, grad_x.reshape(nseq, t_len, d)]
    for kind in range(4):
        outs += [both[name][kind] for name in order]
    return tuple(outs)
```

```python
import math

import jax
import jax.numpy as jnp
from jax import lax
from jax.experimental import pallas as pl
from jax.experimental.pallas import tpu as pltpu

F32 = jnp.float32
BF16 = jnp.bfloat16
MESH = pl.DeviceIdType.MESH

D_MODEL = 1024
EPS = 1e-6
TILE = 128
N_CHIP = 4
N_DEV = 8
VMEM_LIMIT_BYTES = 56 * 1024 * 1024

ADAM_LR = 0.001
ADAM_B1 = 0.9
ADAM_B2 = 0.999
ADAM_EPS = 1e-08
ADAM_WD = 0.01
ADAM_STEP = 10

NT_DIMS = (((1,), (1,)), ((), ()))
TN_DIMS = (((0,), (0,)), ((), ()))


def _params(sem=None):
    return pltpu.CompilerParams(dimension_semantics=sem, vmem_limit_bytes=VMEM_LIMIT_BYTES)


def _mm_nn(a, b4, layer, *, out_dtype, name, epilogue=None, extra=None, tm=1024, tk=1024):
    m, k_dim = a.shape
    _, s_dim, kb, n = b4.shape
    assert kb == k_dim
    tn = min(n, 1024)
    assert m % tm == 0 and k_dim % tk == 0 and n % tn == 0
    nk, npb = k_dim // tk, n // tn
    grid = (m // tm, s_dim * npb, nk)

    def body(*refs):
        if extra is None:
            a_ref, b_ref, o_ref, *scr = refs
        else:
            a_ref, b_ref, e_ref, o_ref, *scr = refs

        def finish(acc):
            if epilogue == "relu2":
                r = jnp.maximum(acc, 0.0)
                acc = r * r
            elif epilogue == "residual":
                acc = acc + e_ref[...]
            o_ref[...] = acc.astype(out_dtype)

        part = jnp.dot(a_ref[...], b_ref[...], preferred_element_type=F32)
        if nk == 1:
            finish(part)
        else:
            acc_ref, = scr
            kk = pl.program_id(2)

            @pl.when(kk == 0)
            def _():
                acc_ref[...] = part

            @pl.when(kk > 0)
            def _():
                acc_ref[...] += part

            @pl.when(kk == nk - 1)
            def _():
                finish(acc_ref[...])

    in_specs = [
        pl.BlockSpec((tm, tk), lambda i, j, kk: (i, kk)),
        pl.BlockSpec((None, None, tk, tn), lambda i, j, kk: (layer, j // npb, kk, j % npb)),
    ]
    args = [a, b4]
    if extra is not None:
        in_specs.append(pl.BlockSpec((tm, tn), lambda i, j, kk: (i, j)))
        args.append(extra)
    return pl.pallas_call(
        body, name=name, grid=grid, in_specs=in_specs,
        out_specs=pl.BlockSpec((tm, tn), lambda i, j, kk: (i, j)),
        out_shape=jax.ShapeDtypeStruct((m, s_dim * n), out_dtype),
        scratch_shapes=[] if nk == 1 else [pltpu.VMEM((tm, tn), F32)],
        compiler_params=_params(("parallel", "parallel", "arbitrary")),
    )(*args)


def _mm_nt(a, b4, layer, *, out_dtype, name, epilogue=None, extra=None, tm=1024, tn=1024):
    m, k_dim = a.shape
    _, s_dim, n_out, n = b4.shape
    assert k_dim == s_dim * n
    tk = min(n, 1024)
    tn = min(tn, n_out)
    assert m % tm == 0 and n_out % tn == 0 and n % tk == 0
    kpb = n // tk
    nk = s_dim * kpb
    grid = (m // tm, n_out // tn, nk)

    def body(*refs):
        if extra is None:
            a_ref, b_ref, o_ref, *scr = refs
        else:
            a_ref, b_ref, e_ref, o_ref, *scr = refs

        def finish(acc):
            if epilogue == "relu2_bwd":
                acc = acc * (2.0 * jnp.sqrt(e_ref[...].astype(F32)))
            o_ref[...] = acc.astype(out_dtype)

        part = lax.dot_general(a_ref[...], b_ref[...], NT_DIMS, preferred_element_type=F32)
        if nk == 1:
            finish(part)
        else:
            acc_ref, = scr
            kk = pl.program_id(2)

            @pl.when(kk == 0)
            def _():
                acc_ref[...] = part

            @pl.when(kk > 0)
            def _():
                acc_ref[...] += part

            @pl.when(kk == nk - 1)
            def _():
                finish(acc_ref[...])

    in_specs = [
        pl.BlockSpec((tm, tk), lambda i, j, kk: (i, kk)),
        pl.BlockSpec((None, None, tn, tk), lambda i, j, kk: (layer, kk // kpb, j, kk % kpb)),
    ]
    args = [a, b4]
    if extra is not None:
        in_specs.append(pl.BlockSpec((tm, tn), lambda i, j, kk: (i, j)))
        args.append(extra)
    return pl.pallas_call(
        body, name=name, grid=grid, in_specs=in_specs,
        out_specs=pl.BlockSpec((tm, tn), lambda i, j, kk: (i, j)),
        out_shape=jax.ShapeDtypeStruct((m, n_out), out_dtype),
        scratch_shapes=[] if nk == 1 else [pltpu.VMEM((tm, tn), F32)],
        compiler_params=_params(("parallel", "parallel", "arbitrary")),
    )(*args)


def _mm_tn(a, b, s_dim, *, name, tm=1024, t1=1024):
    m, k1 = a.shape
    mb, n_all = b.shape
    assert mb == m and n_all % s_dim == 0
    n = n_all // s_dim
    tn = min(n, 1024)
    t1 = min(t1, k1)
    assert m % tm == 0 and k1 % t1 == 0 and n % tn == 0
    npb = n // tn
    nk = m // tm
    grid = (k1 // t1, s_dim * npb, nk)

    def body(a_ref, b_ref, o_ref, acc_ref):
        kk = pl.program_id(2)
        part = lax.dot_general(a_ref[...], b_ref[...], TN_DIMS, preferred_element_type=F32)

        @pl.when(kk == 0)
        def _():
            acc_ref[...] = part

        @pl.when(kk > 0)
        def _():
            acc_ref[...] += part

        @pl.when(kk == nk - 1)
        def _():
            o_ref[...] = acc_ref[...].astype(BF16)

    return pl.pallas_call(
        body, name=name, grid=grid,
        in_specs=[pl.BlockSpec((tm, t1), lambda i, j, kk: (kk, i)),
                  pl.BlockSpec((tm, tn), lambda i, j, kk: (kk, j))],
        out_specs=pl.BlockSpec((None, None, t1, tn), lambda i, j, kk: (0, j // npb, i, j % npb)),
        out_shape=jax.ShapeDtypeStruct((1, s_dim, k1, n), BF16),
        scratch_shapes=[pltpu.VMEM((t1, tn), F32)],
        compiler_params=_params(("parallel", "parallel", "arbitrary")),
    )(a, b)


ROW_TILE = 512


def _rms_fwd(h, g, *, name):
    m, d = h.shape

    def body(h_ref, g_ref, o_ref):
        hv = h_ref[...]
        rstd = lax.rsqrt(jnp.mean(hv * hv, axis=-1, keepdims=True) + EPS)
        o_ref[...] = (hv * rstd * g_ref[...]).astype(BF16)

    return pl.pallas_call(
        body, name=name, grid=(m // ROW_TILE,),
        in_specs=[pl.BlockSpec((ROW_TILE, d), lambda i: (i, 0)), pl.BlockSpec((1, d), lambda i: (0, 0))],
        out_specs=pl.BlockSpec((ROW_TILE, d), lambda i: (i, 0)),
        out_shape=jax.ShapeDtypeStruct((m, d), BF16),
        compiler_params=_params(("parallel",)),
    )(h, g)


def _rms_bwd(h, g, dxn, dres, *, name):
    m, d = h.shape

    def body(h_ref, g_ref, dxn_ref, dres_ref, dh_ref, dhb_ref, dg_ref):
        hv = h_ref[...]
        rstd = lax.rsqrt(jnp.mean(hv * hv, axis=-1, keepdims=True) + EPS)
        xhat = hv * rstd
        dy = dxn_ref[...]
        dxhat = dy * g_ref[...]
        dh = dres_ref[...] + rstd * (dxhat - xhat * jnp.mean(dxhat * xhat, axis=-1, keepdims=True))
        dh_ref[...] = dh
        dhb_ref[...] = dh.astype(BF16)
        part = jnp.sum(dy * xhat, axis=0, keepdims=True)

        @pl.when(pl.program_id(0) == 0)
        def _():
            dg_ref[...] = part

        @pl.when(pl.program_id(0) > 0)
        def _():
            dg_ref[...] += part

    row = pl.BlockSpec((ROW_TILE, d), lambda i: (i, 0))
    vec = pl.BlockSpec((1, d), lambda i: (0, 0))
    return pl.pallas_call(
        body, name=name, grid=(m // ROW_TILE,),
        in_specs=[row, vec, row, row], out_specs=[row, row, vec],
        out_shape=[jax.ShapeDtypeStruct((m, d), F32), jax.ShapeDtypeStruct((m, d), BF16),
                   jax.ShapeDtypeStruct((1, d), F32)],
        compiler_params=_params(("arbitrary",)),
    )(h, g, dxn, dres)


def _final_loss(h, g, target):
    m, d = h.shape

    def body(h_ref, g_ref, t_ref, dh_ref, dhb_ref, dg_ref, loss_ref):
        hv = h_ref[...]
        gv = g_ref[...]
        rstd = lax.rsqrt(jnp.mean(hv * hv, axis=-1, keepdims=True) + EPS)
        xhat = hv * rstd
        err = xhat * gv - t_ref[...]
        dy = err * (1.0 / d)
        dxhat = dy * gv
        dh = rstd * (dxhat - xhat * jnp.mean(dxhat * xhat, axis=-1, keepdims=True))
        dh_ref[...] = dh
        dhb_ref[...] = dh.astype(BF16)
        dg_part = jnp.sum(dy * xhat, axis=0, keepdims=True)
        sq = jnp.sum(jnp.sum(err * err, axis=1, keepdims=True), axis=0, keepdims=True) * (0.5 / d)
        loss_part = jnp.broadcast_to(sq, (8, TILE))

        @pl.when(pl.program_id(0) == 0)
        def _():
            dg_ref[...] = dg_part
            loss_ref[...] = loss_part

        @pl.when(pl.program_id(0) > 0)
        def _():
            dg_ref[...] += dg_part
            loss_ref[...] += loss_part

    row = pl.BlockSpec((ROW_TILE, d), lambda i: (i, 0))
    vec = pl.BlockSpec((1, d), lambda i: (0, 0))
    return pl.pallas_call(
        body, name="final_loss", grid=(m // ROW_TILE,),
        in_specs=[row, vec, row],
        out_specs=[row, row, vec, pl.BlockSpec((8, TILE), lambda i: (0, 0))],
        out_shape=[jax.ShapeDtypeStruct((m, d), F32), jax.ShapeDtypeStruct((m, d), BF16),
                   jax.ShapeDtypeStruct((1, d), F32), jax.ShapeDtypeStruct((8, TILE), F32)],
        compiler_params=_params(("arbitrary",)),
    )(h, g, target)


def _shift_down(x, s, t_idx):
    return jnp.where(t_idx >= s, pltpu.roll(x, s, 0), 0.0)


def _shift_up(x, s, t_idx, t_len):
    return jnp.where(t_idx < t_len - s, pltpu.roll(x, t_len - s, 0), 0.0)


def _pool_select(group, s2, s4, s8, s16):
    return jnp.where(group == 0, s2, jnp.where(group == 1, s4, jnp.where(group == 2, s8, s16)))


def _pool_count(group, t_idx):
    win = jnp.left_shift(2, group)
    return jnp.minimum(t_idx + 1, win).astype(F32)


def _pool_fwd_math(a, group, t_idx):
    s2 = a + _shift_down(a, 1, t_idx)
    s4 = s2 + _shift_down(s2, 2, t_idx)
    s8 = s4 + _shift_down(s4, 4, t_idx)
    s16 = s8 + _shift_down(s8, 8, t_idx)
    return _pool_select(group, s2, s4, s8, s16) / _pool_count(group, t_idx) - a


def _pool_bwd_math(dpooled, group, t_idx, t_len):
    e = dpooled / _pool_count(group, t_idx)
    s2 = e + _shift_up(e, 1, t_idx, t_len)
    s4 = s2 + _shift_up(s2, 2, t_idx, t_len)
    s8 = s4 + _shift_up(s4, 4, t_idx, t_len)
    s16 = s8 + _shift_up(s8, 8, t_idx, t_len)
    return _pool_select(group, s2, s4, s8, s16) - dpooled


def _conv_fwd_math(c, w_ref, b_ref, t_idx):
    return (w_ref[0:1, :] * _shift_down(c, 2, t_idx) + w_ref[1:2, :] * _shift_down(c, 1, t_idx)
            + w_ref[2:3, :] * c + b_ref[...])


def _ab_fwd(p, pool_w, pool_scale, conv_w, conv_b, nseq, t_len):
    m = p.shape[0]
    ng = 4

    def body(a_ref, xb_ref, gb_ref, gc_ref, pw_ref, ps_ref, cw_ref, cb_ref, o_ref):
        j = pl.program_id(1)
        t_idx = lax.broadcasted_iota(jnp.int32, (t_len, TILE), 0)

        @pl.when(j < ng)
        def _():
            pooled = _pool_fwd_math(a_ref[...].astype(F32), j, t_idx)
            mixed = jnp.dot(pooled.astype(BF16), pw_ref[...].astype(BF16), preferred_element_type=F32)
            o_ref[...] = (mixed * ps_ref[...]).astype(BF16)

        @pl.when(j >= ng)
        def _():
            c = gc_ref[...].astype(F32) * xb_ref[...].astype(F32)
            y = _conv_fwd_math(c, cw_ref, cb_ref, t_idx)
            o_ref[...] = (gb_ref[...].astype(F32) * y).astype(BF16)

    def pool_j(j):
        return jnp.minimum(j, ng - 1)

    def conv_j(j):
        return jnp.maximum(j - ng, 0)

    in_specs = [
        pl.BlockSpec((t_len, TILE), lambda s, j: (s, pool_j(j))),
        pl.BlockSpec((t_len, TILE), lambda s, j: (s, ng + conv_j(j))),
        pl.BlockSpec((t_len, TILE), lambda s, j: (s, 2 * ng + conv_j(j))),
        pl.BlockSpec((t_len, TILE), lambda s, j: (s, 3 * ng + conv_j(j))),
        pl.BlockSpec((None, TILE, TILE), lambda s, j: (pool_j(j), 0, 0)),
        pl.BlockSpec((None, 1, TILE), lambda s, j: (pool_j(j), 0, 0)),
        pl.BlockSpec((3, TILE), lambda s, j: (0, conv_j(j))),
        pl.BlockSpec((1, TILE), lambda s, j: (0, conv_j(j))),
    ]
    return pl.pallas_call(
        body, name="ab_mixer_fwd", grid=(nseq, 2 * ng), in_specs=in_specs,
        out_specs=pl.BlockSpec((t_len, TILE), lambda s, j: (s, j)),
        out_shape=jax.ShapeDtypeStruct((m, 2 * ng * TILE), BF16),
        compiler_params=_params(("parallel", "arbitrary")),
    )(p, p, p, p, pool_w, pool_scale, conv_w, conv_b)


def _ab_bwd(p, dmix, pool_w, pool_scale, conv_w, conv_b, nseq, t_len):
    m = p.shape[0]
    ng = 4

    def body(a_ref, xb_ref, gb_ref, gc_ref, dma_ref, dmb_ref, pw_ref, ps_ref, cw_ref, cb_ref,
             da_ref, dxb_ref, dgb_ref, dgc_ref, dpw_ref, dps_ref, dcw_ref, dcb_ref):
        j = pl.program_id(0)
        first = pl.program_id(1) == 0
        t_idx = lax.broadcasted_iota(jnp.int32, (t_len, TILE), 0)

        pooled = _pool_fwd_math(a_ref[...].astype(F32), j, t_idx).astype(BF16)
        w_bf = pw_ref[...].astype(BF16)
        mixed = jnp.dot(pooled, w_bf, preferred_element_type=F32)
        dm = dma_ref[...].astype(F32)
        dps = jnp.sum(dm * mixed, axis=0, keepdims=True)
        dmixed = (dm * ps_ref[...]).astype(BF16)
        dpw = lax.dot_general(pooled, dmixed, TN_DIMS, preferred_element_type=F32)
        dpooled = lax.dot_general(dmixed, w_bf, NT_DIMS, preferred_element_type=F32)
        da_ref[...] = _pool_bwd_math(dpooled, j, t_idx, t_len).astype(BF16)

        xb = xb_ref[...].astype(F32)
        gb = gb_ref[...].astype(F32)
        gc = gc_ref[...].astype(F32)
        d = dmb_ref[...].astype(F32)
        c = gc * xb
        c1 = _shift_down(c, 1, t_idx)
        c2 = _shift_down(c, 2, t_idx)
        y = cw_ref[0:1, :] * c2 + cw_ref[1:2, :] * c1 + cw_ref[2:3, :] * c + cb_ref[...]
        dgb_ref[...] = (d * y).astype(BF16)
        dy = d * gb
        dc = (cw_ref[2:3, :] * dy + cw_ref[1:2, :] * _shift_up(dy, 1, t_idx, t_len)
              + cw_ref[0:1, :] * _shift_up(dy, 2, t_idx, t_len))
        dgc_ref[...] = (dc * xb).astype(BF16)
        dxb_ref[...] = (dc * gc).astype(BF16)
        dcw = jnp.concatenate([jnp.sum(dy * c2, axis=0, keepdims=True),
                               jnp.sum(dy * c1, axis=0, keepdims=True),
                               jnp.sum(dy * c, axis=0, keepdims=True)], axis=0)
        dcb = jnp.sum(dy, axis=0, keepdims=True)

        @pl.when(first)
        def _():
            dpw_ref[...] = dpw
            dps_ref[...] = dps
            dcw_ref[...] = dcw
            dcb_ref[...] = dcb

        @pl.when(jnp.logical_not(first))
        def _():
            dpw_ref[...] += dpw
            dps_ref[...] += dps
            dcw_ref[...] += dcw
            dcb_ref[...] += dcb

    def col(k):
        return pl.BlockSpec((t_len, TILE), lambda j, s: (s, k * ng + j))

    in_specs = [
        col(0), col(1), col(2), col(3), col(0), col(1),
        pl.BlockSpec((None, TILE, TILE), lambda j, s: (j, 0, 0)),
        pl.BlockSpec((None, 1, TILE), lambda j, s: (j, 0, 0)),
        pl.BlockSpec((3, TILE), lambda j, s: (0, j)),
        pl.BlockSpec((1, TILE), lambda j, s: (0, j)),
    ]
    piece = pl.BlockSpec((t_len, TILE), lambda j, s: (s, j))
    out_specs = [
        piece, piece, piece, piece,
        pl.BlockSpec((None, TILE, TILE), lambda j, s: (j, 0, 0)),
        pl.BlockSpec((None, 1, TILE), lambda j, s: (j, 0, 0)),
        pl.BlockSpec((3, TILE), lambda j, s: (0, j)),
        pl.BlockSpec((1, TILE), lambda j, s: (0, j)),
    ]
    w = ng * TILE
    out_shape = [jax.ShapeDtypeStruct((m, w), BF16)] * 4 + [
        jax.ShapeDtypeStruct((ng, TILE, TILE), F32), jax.ShapeDtypeStruct((ng, 1, TILE), F32),
        jax.ShapeDtypeStruct((3, w), F32), jax.ShapeDtypeStruct((1, w), F32)]
    return pl.pallas_call(
        body, name="ab_mixer_bwd", grid=(ng, nseq), in_specs=in_specs, out_specs=out_specs,
        out_shape=out_shape, compiler_params=_params(("parallel", "arbitrary")),
    )(p, p, p, p, dmix, dmix, pool_w, pool_scale, conv_w, conv_b)


SGU_ROWS = 512
INV_SQRT2 = 1.0 / math.sqrt(2.0)
INV_SQRT_2PI = 1.0 / math.sqrt(2.0 * math.pi)


def _gelu(x):
    return 0.5 * x * (1.0 + lax.erf(x * INV_SQRT2))


def _gelu_grad(x):
    return 0.5 * (1.0 + lax.erf(x * INV_SQRT2)) + x * (INV_SQRT_2PI * jnp.exp(-0.5 * x * x))


def _causal_tile(transposed=False):
    r = lax.broadcasted_iota(jnp.int32, (TILE, TILE), 0)
    c = lax.broadcasted_iota(jnp.int32, (TILE, TILE), 1)
    return r <= c if transposed else c <= r


def _sgu_norm(v, g_ref, b_ref):
    mu = jnp.mean(v, axis=-1, keepdims=True)
    xc = v - mu
    rstd = lax.rsqrt(jnp.mean(xc * xc, axis=-1, keepdims=True) + EPS)
    xhat = xc * rstd
    return xhat, rstd, xhat * g_ref[...] + b_ref[...]


def _sgu_fwd(p, norm_g, norm_b, w_s, bias_tile):
    m = p.shape[0]
    ng = 4
    width = ng * TILE

    def body(u_ref, v_ref, g_ref, b_ref, w_ref, bias_ref, o_ref):
        u = _gelu(u_ref[...].astype(F32))
        _, _, vln = _sgu_norm(_gelu(v_ref[...].astype(F32)), g_ref, b_ref)
        vln = vln.astype(BF16)
        causal = _causal_tile()
        for g in range(ng):
            cols = slice(g * TILE, (g + 1) * TILE)
            wg = jnp.where(causal, w_ref[g], 0.0).astype(BF16)
            for n in range(SGU_ROWS // TILE):
                rows = slice(n * TILE, (n + 1) * TILE)
                s = jnp.dot(wg, vln[rows, cols], preferred_element_type=F32) + bias_ref[g]
                o_ref[rows, cols] = (u[rows, cols] * s).astype(BF16)

    vec = pl.BlockSpec((1, width), lambda i: (0, 0))
    tiles = pl.BlockSpec((ng, TILE, TILE), lambda i: (0, 0, 0))
    return pl.pallas_call(
        body, name="sgu_fwd", grid=(m // SGU_ROWS,),
        in_specs=[pl.BlockSpec((SGU_ROWS, width), lambda i: (i, 0)),
                  pl.BlockSpec((SGU_ROWS, width), lambda i: (i, 1)), vec, vec, tiles, tiles],
        out_specs=pl.BlockSpec((SGU_ROWS, width), lambda i: (i, 0)),
        out_shape=jax.ShapeDtypeStruct((m, width), BF16),
        compiler_params=_params(("parallel",)),
    )(p, p, norm_g, norm_b, w_s, bias_tile)


def _sgu_bwd(p, dmix, norm_g, norm_b, w_s, w_s_t, bias_tile):
    m = p.shape[0]
    ng = 4
    width = ng * TILE

    def body(u_ref, v_ref, dc_ref, g_ref, b_ref, w_ref, wt_ref, bias_ref,
             du_ref, dv_ref, dw_ref, dbs_ref, dg_ref, db_ref, ds_scr, dvln_scr):
        u_pre = u_ref[...].astype(F32)
        v_pre = v_ref[...].astype(F32)
        u = _gelu(u_pre)
        xhat, rstd, vln = _sgu_norm(_gelu(v_pre), g_ref, b_ref)
        vln = vln.astype(BF16)
        dc = dc_ref[...].astype(F32)
        causal = _causal_tile()
        ones = jnp.ones((TILE, TILE), BF16)
        first = pl.program_id(0) == 0
        for g in range(ng):
            cols = slice(g * TILE, (g + 1) * TILE)
            wg = jnp.where(causal, w_ref[g], 0.0).astype(BF16)
            wgt = jnp.where(_causal_tile(transposed=True), wt_ref[g], 0.0).astype(BF16)
            dw_acc = jnp.zeros((TILE, TILE), F32)
            dbs_acc = jnp.zeros((TILE, TILE), F32)
            for n in range(SGU_ROWS // TILE):
                rows = slice(n * TILE, (n + 1) * TILE)
                vt = vln[rows, cols]
                s = jnp.dot(wg, vt, preferred_element_type=F32) + bias_ref[g]
                ds_scr[rows, cols] = dc[rows, cols] * s
                ds = (dc[rows, cols] * u[rows, cols]).astype(BF16)
                dw_acc += lax.dot_general(ds, vt, NT_DIMS, preferred_element_type=F32)
                dbs_acc += jnp.dot(ds, ones, preferred_element_type=F32)
                dvln_scr[rows, cols] = jnp.dot(wgt, ds, preferred_element_type=F32)
            dw_g = jnp.where(causal, dw_acc, 0.0)

            @pl.when(first)
            def _():
                dw_ref[g] = dw_g
                dbs_ref[g] = dbs_acc

            @pl.when(jnp.logical_not(first))
            def _():
                dw_ref[g] += dw_g
                dbs_ref[g] += dbs_acc

        du_ref[...] = (ds_scr[...] * _gelu_grad(u_pre)).astype(BF16)
        dvln = dvln_scr[...]
        dxhat = dvln * g_ref[...]
        dv = rstd * (dxhat - jnp.mean(dxhat, axis=-1, keepdims=True)
                     - xhat * jnp.mean(dxhat * xhat, axis=-1, keepdims=True))
        dv_ref[...] = (dv * _gelu_grad(v_pre)).astype(BF16)
        dg_part = jnp.sum(dvln * xhat, axis=0, keepdims=True)
        db_part = jnp.sum(dvln, axis=0, keepdims=True)

        @pl.when(first)
        def _():
            dg_ref[...] = dg_part
            db_ref[...] = db_part

        @pl.when(jnp.logical_not(first))
        def _():
            dg_ref[...] += dg_part
            db_ref[...] += db_part

    vec = pl.BlockSpec((1, width), lambda i: (0, 0))
    tiles = pl.BlockSpec((ng, TILE, TILE), lambda i: (0, 0, 0))
    rows0 = pl.BlockSpec((SGU_ROWS, width), lambda i: (i, 0))
    rows1 = pl.BlockSpec((SGU_ROWS, width), lambda i: (i, 1))
    return pl.pallas_call(
        body, name="sgu_bwd", grid=(m // SGU_ROWS,),
        in_specs=[rows0, rows1, rows0, vec, vec, tiles, tiles, tiles],
        out_specs=[rows0, rows0, tiles, tiles, vec, vec],
        out_shape=[jax.ShapeDtypeStruct((m, width), BF16), jax.ShapeDtypeStruct((m, width), BF16),
                   jax.ShapeDtypeStruct((ng, TILE, TILE), F32), jax.ShapeDtypeStruct((ng, TILE, TILE), F32),
                   jax.ShapeDtypeStruct((1, width), F32), jax.ShapeDtypeStruct((1, width), F32)],
        scratch_shapes=[pltpu.VMEM((SGU_ROWS, width), F32), pltpu.VMEM((SGU_ROWS, width), F32)],
        compiler_params=_params(("arbitrary",)),
    )(p, p, dmix, norm_g, norm_b, w_s, w_s_t, bias_tile)


SB_DH = 64
SB_SCALE = 1.0 / math.sqrt(SB_DH)


SB_BLOCK = 256
SB_SUB = SB_BLOCK // TILE


def _sum_matrix(kind):
    j = lax.broadcasted_iota(jnp.int32, (TILE, 2 * TILE), 0)
    s = lax.broadcasted_iota(jnp.int32, (TILE, 2 * TILE), 1)
    tri = {"after": j > s, "upto": j <= s, "before": j < s}[kind]
    return jnp.where(jnp.logical_or(s >= TILE, tri), 1.0, 0.0).astype(BF16)


def _strict_mask():
    r = lax.broadcasted_iota(jnp.int32, (SB_BLOCK, SB_BLOCK), 0)
    c = lax.broadcasted_iota(jnp.int32, (SB_BLOCK, SB_BLOCK), 1)
    return c < r


def _head_lanes(h):
    lane = lax.broadcasted_iota(jnp.int32, (1, TILE), 1)
    return (lane >= h * SB_DH) & (lane < (h + 1) * SB_DH)


def _softplus(z):
    return jnp.maximum(z, 0.0) + jnp.log(1.0 + jnp.exp(-jnp.abs(z)))


def _sb_fwd(p, nseq, t_len, gather):
    m = p.shape[0]
    npair = 4
    ng = len(gather)
    last_step = nseq * npair - 1

    def body(q_ref, k_ref, v_ref, *rest):
        o_ref, lt_ref = rest[ng:ng + 2]
        kh_ref, vh_ref = rest[2 * ng + 2:2 * ng + 4]
        step = pl.program_id(0) * npair + pl.program_id(1)
        send, forward, finish = _gather_steps(rest[ng + 2:2 * ng + 2], *rest[2 * ng + 4:])
        pl.when(step == 0)(send)
        pl.when(step == (last_step + 1) // 2)(forward)
        for h in range(2):
            keep = _head_lanes(h)
            kh_ref[h] = jnp.where(keep, k_ref[...], 0).astype(BF16)
            vh_ref[h] = jnp.where(keep, v_ref[...], 0).astype(BF16)
        summat = _sum_matrix("after")
        strict = _strict_mask()

        def one_pass(q, row0, diag, state):
            rows = pl.ds(row0, SB_BLOCK)
            z, sp, pieces = [], [], []
            for h in range(2):
                zh = lax.dot_general(q, kh_ref[h, rows, :], NT_DIMS, preferred_element_type=F32) * SB_SCALE
                sph = _softplus(zh)
                logkeep = jnp.where(strict, -sph, 0.0) if diag else -sph
                z.append(zh)
                sp.append(sph)
                pieces += [logkeep[:, b * TILE:(b + 1) * TILE] for b in range(SB_SUB)]
            sums = jnp.dot(jnp.concatenate(pieces, axis=0).astype(BF16), summat, preferred_element_type=F32)
            out = []
            for h in range(2):
                carry, acc = state[2 * h], state[2 * h + 1]
                after = [None] * SB_SUB
                for b in reversed(range(SB_SUB)):
                    part = sums[(h * SB_SUB + b) * SB_BLOCK:(h * SB_SUB + b + 1) * SB_BLOCK]
                    after[b] = part[:, :TILE] + carry
                    carry = carry + part[:, TILE:]
                w = jnp.exp(z[h] - sp[h] + jnp.concatenate(after, axis=1))
                if diag:
                    w = jnp.where(strict, w, 0.0)
                out += [carry, acc + jnp.dot(w.astype(BF16), vh_ref[h, rows, :], preferred_element_type=F32)]
            return tuple(out)

        def q_block(i, _):
            r0 = pl.multiple_of(i * SB_BLOCK, SB_BLOCK)
            q = q_ref[pl.ds(r0, SB_BLOCK), :]
            zero = jnp.zeros((SB_BLOCK, TILE), F32)
            state = one_pass(q, r0, True, (zero,) * 4)
            state = lax.fori_loop(
                0, i, lambda jj, st: one_pass(q, pl.multiple_of((i - 1 - jj) * SB_BLOCK, SB_BLOCK), False, st), state)
            o_ref[pl.ds(r0, SB_BLOCK), :] = (state[1] + state[3]).astype(BF16)
            lt_ref[pl.ds(r0, SB_BLOCK), :] = jnp.where(_head_lanes(0), state[0], state[2])
            return 0

        lax.fori_loop(0, t_len // SB_BLOCK, q_block, 0)
        pl.when(step == last_step)(finish)

    def col(k):
        return pl.BlockSpec((t_len, TILE), lambda s, hp: (s, k * npair + hp))

    out = pl.BlockSpec((t_len, TILE), lambda s, hp: (s, hp))
    res = pl.pallas_call(
        body, name="stickbreak_fwd", grid=(nseq, npair), in_specs=[col(2), col(3), col(4)] + [ANY] * ng,
        out_specs=[out, out] + [ANY] * ng,
        out_shape=[jax.ShapeDtypeStruct((m, npair * TILE), BF16), jax.ShapeDtypeStruct((m, npair * TILE), F32)]
        + [jax.ShapeDtypeStruct(b.shape, b.dtype) for b in gather],
        input_output_aliases={3 + a: 2 + a for a in range(ng)},
        scratch_shapes=[pltpu.VMEM((2, t_len, TILE), BF16), pltpu.VMEM((2, t_len, TILE), BF16)] + _gather_sems(ng),
        compiler_params=pltpu.CompilerParams(dimension_semantics=("arbitrary", "arbitrary"),
                                             vmem_limit_bytes=VMEM_LIMIT_BYTES, has_side_effects=True),
    )(p, p, p, *gather)
    return res[0], res[1], res[2:]


def _sb_bwd(p, dmix, ltot, nseq, t_len, exchange):
    m = p.shape[0]
    npair = 4
    ne = len(exchange)
    last_step = nseq * npair - 1

    def body(q_ref, k_ref, v_ref, do_ref, lt_ref, *rest):
        dq_ref, dk_ref, dv_ref = rest[ne:ne + 3]
        kh_ref, vh_ref, dk_acc, dv_acc = rest[2 * ne + 3:2 * ne + 7]
        step = pl.program_id(0) * npair + pl.program_id(1)
        send, finish = _exchange_steps(rest[:ne], rest[ne + 3:2 * ne + 3], *rest[2 * ne + 7:])
        pl.when(step == 0)(send)
        for h in range(2):
            keep = _head_lanes(h)
            kh_ref[h] = jnp.where(keep, k_ref[...], 0).astype(BF16)
            vh_ref[h] = jnp.where(keep, v_ref[...], 0).astype(BF16)
        dk_acc[...] = jnp.zeros_like(dk_acc)
        dv_acc[...] = jnp.zeros_like(dv_acc)
        sum_upto = _sum_matrix("upto")
        sum_before = _sum_matrix("before")
        strict = _strict_mask()
        lane = lax.broadcasted_iota(jnp.int32, (SB_BLOCK, TILE), 1)

        def running(x, matrix, start):
            pieces = [x[h][:, b * TILE:(b + 1) * TILE] for h in range(2) for b in range(SB_SUB)]
            sums = jnp.dot(jnp.concatenate(pieces, axis=0).astype(BF16), matrix, preferred_element_type=F32)
            wide, ends = [], []
            for h in range(2):
                total, cols = start[h], []
                for b in range(SB_SUB):
                    part = sums[(h * SB_SUB + b) * SB_BLOCK:(h * SB_SUB + b + 1) * SB_BLOCK]
                    cols.append(part[:, :TILE] + total)
                    total = total + part[:, TILE:]
                wide.append(jnp.concatenate(cols, axis=1))
                ends.append(total)
            return wide, ends

        def one_pass(q, do, qh, doh, ltot, row0, diag, state):
            rows = pl.ds(row0, SB_BLOCK)
            z, sp, logkeep = [], [], []
            for h in range(2):
                zh = lax.dot_general(q, kh_ref[h, rows, :], NT_DIMS, preferred_element_type=F32) * SB_SCALE
                sph = _softplus(zh)
                z.append(zh)
                sp.append(sph)
                logkeep.append(jnp.where(strict, -sph, 0.0) if diag else -sph)
            upto, sum_l = running(logkeep, sum_upto, [state[0], state[3]])
            w, g = [], []
            for h in range(2):
                wh = jnp.exp(z[h] - sp[h] + (ltot[h] - upto[h]))
                if diag:
                    wh = jnp.where(strict, wh, 0.0)
                w.append(wh)
                g.append(wh * lax.dot_general(do, vh_ref[h, rows, :], NT_DIMS, preferred_element_type=F32))
            g_before, sum_g = running(g, sum_before, [state[1], state[4]])
            out, dk_new, dv_new = [], 0.0, 0.0
            for h in range(2):
                dz = (g[h] - jnp.exp(z[h] - sp[h]) * (g[h] + g_before[h])) * SB_SCALE
                if diag:
                    dz = jnp.where(strict, dz, 0.0)
                dzb = dz.astype(BF16)
                dq = state[3 * h + 2] + jnp.dot(dzb, kh_ref[h, rows, :], preferred_element_type=F32)
                dk_new = dk_new + lax.dot_general(dzb, qh[h], TN_DIMS, preferred_element_type=F32)
                dv_new = dv_new + lax.dot_general(w[h].astype(BF16), doh[h], TN_DIMS, preferred_element_type=F32)
                out += [sum_l[h], sum_g[h], dq]
            dk_acc[rows, :] += dk_new
            dv_acc[rows, :] += dv_new
            return tuple(out)

        def q_block(i, _):
            r0 = pl.multiple_of(i * SB_BLOCK, SB_BLOCK)
            q = q_ref[pl.ds(r0, SB_BLOCK), :]
            do = do_ref[pl.ds(r0, SB_BLOCK), :]
            lt = lt_ref[pl.ds(r0, SB_BLOCK), :]
            qh, doh, ltot = [], [], []
            for h in range(2):
                keep = _head_lanes(h)
                qh.append(jnp.where(keep, q, 0).astype(BF16))
                doh.append(jnp.where(keep, do, 0).astype(BF16))
                ltot.append(jnp.sum(jnp.where(lane == h * SB_DH, lt, 0.0), axis=1, keepdims=True))
            zero = jnp.zeros((SB_BLOCK, TILE), F32)
            state = lax.fori_loop(
                0, i,
                lambda jj, st: one_pass(q, do, qh, doh, ltot, pl.multiple_of(jj * SB_BLOCK, SB_BLOCK), False, st),
                (zero,) * 6)
            state = one_pass(q, do, qh, doh, ltot, r0, True, state)
            dq_ref[pl.ds(r0, SB_BLOCK), :] = (state[2] + state[5]).astype(BF16)
            return 0

        lax.fori_loop(0, t_len // SB_BLOCK, q_block, 0)
        dk_ref[...] = dk_acc[...].astype(BF16)
        dv_ref[...] = dv_acc[...].astype(BF16)
        pl.when(step == last_step)(finish)

    def col(k):
        return pl.BlockSpec((t_len, TILE), lambda s, hp: (s, k * npair + hp))

    out = pl.BlockSpec((t_len, TILE), lambda s, hp: (s, hp))
    width = npair * TILE
    res = pl.pallas_call(
        body, name="stickbreak_bwd", grid=(nseq, npair),
        in_specs=[col(2), col(3), col(4), col(1), out] + [ANY] * ne, out_specs=[out, out, out] + [ANY] * ne,
        out_shape=[jax.ShapeDtypeStruct((m, width), BF16)] * 3 + _exchange_shapes(exchange),
        scratch_shapes=[pltpu.VMEM((2, t_len, TILE), BF16), pltpu.VMEM((2, t_len, TILE), BF16),
                        pltpu.VMEM((t_len, TILE), F32), pltpu.VMEM((t_len, TILE), F32)] + _exchange_sems(ne),
        compiler_params=pltpu.CompilerParams(dimension_semantics=("arbitrary", "arbitrary"),
                                             vmem_limit_bytes=VMEM_LIMIT_BYTES, has_side_effects=True),
    )(p, p, p, dmix, ltot, *exchange)
    return res[0], res[1], res[2], res[3:]


def _adam_math(w, g, m, v):
    m = ADAM_B1 * m + (1.0 - ADAM_B1) * g
    v = ADAM_B2 * v + (1.0 - ADAM_B2) * (g * g)
    m_hat = m / (1.0 - ADAM_B1 ** ADAM_STEP)
    v_hat = v / (1.0 - ADAM_B2 ** ADAM_STEP)
    delta = -ADAM_LR * (m_hat / (jnp.sqrt(v_hat) + ADAM_EPS) + ADAM_WD * w)
    return delta, m, v


def _cast_place(w, layer, pos, *, name):
    _, r, c = w.shape
    tr = min(r, 256)

    def body(pos_ref, w_ref, o_ref):
        o_ref[...] = w_ref[...].astype(BF16)

    grid_spec = pltpu.PrefetchScalarGridSpec(
        num_scalar_prefetch=1, grid=(r // tr,),
        in_specs=[pl.BlockSpec((None, tr, c), lambda i, pos_ref: (layer, i, 0))],
        out_specs=pl.BlockSpec((None, None, tr, c), lambda i, pos_ref: (0, pos_ref[0], i, 0)))
    return pl.pallas_call(
        body, name=name, grid_spec=grid_spec, out_shape=jax.ShapeDtypeStruct((1, N_CHIP, r, c), BF16),
        compiler_params=_params(("parallel",)),
    )(pos, w)


def _pair_sum(mine, got, pos, *, name):
    l_dim, s_dim, h, c = got.shape
    th = min(h, 512)
    nt = h // th

    def body(pos_ref, a_ref, b_ref, o_ref):
        o_ref[...] = (a_ref[...].astype(F32) + b_ref[...].astype(F32)).astype(BF16)

    spec = pl.BlockSpec((None, None, th, c), lambda l, s, i, pos_ref: (l, s, i, 0))
    grid_spec = pltpu.PrefetchScalarGridSpec(
        num_scalar_prefetch=1, grid=(l_dim, s_dim, nt),
        in_specs=[pl.BlockSpec((None, None, th, c), lambda l, s, i, pos_ref: (l, s, pos_ref[1] * nt + i, 0)), spec],
        out_specs=spec)
    return pl.pallas_call(
        body, name=name, grid_spec=grid_spec, out_shape=jax.ShapeDtypeStruct(got.shape, BF16),
        compiler_params=_params(("parallel",) * 3),
    )(pos, mine, got)


def _chip_sum(sums, landed, pos, *, name):
    l_dim, _, h, c = sums.shape
    th = min(h, 512)
    nt = h // th

    def body(pos_ref, own, r0, r1, r2, o_ref):
        o_ref[...] = ((own[...].astype(F32) + r0[...].astype(F32)) + r1[...].astype(F32)) + r2[...].astype(F32)

    def piece(k):
        return pl.BlockSpec((None, None, th, c), lambda l, i, pos_ref: (l, k, i, 0))

    grid_spec = pltpu.PrefetchScalarGridSpec(
        num_scalar_prefetch=1, grid=(l_dim, nt),
        in_specs=[pl.BlockSpec((None, None, th, c), lambda l, i, pos_ref: (l, pos_ref[0], i, 0)),
                  piece(0), piece(1), piece(2)],
        out_specs=pl.BlockSpec((None, th, c), lambda l, i, pos_ref: (l, pos_ref[1] * nt + i, 0)))
    return pl.pallas_call(
        body, name=name, grid_spec=grid_spec, out_shape=jax.ShapeDtypeStruct((l_dim, 2 * h, c), F32),
        compiler_params=_params(("parallel",) * 2),
    )(pos, sums, landed, landed, landed)


def _adam_big(w, m, v, grads, *, name):
    l_dim, r, c = w.shape
    assert len(grads) == l_dim
    tr = min(r, 256)

    def body(*refs):
        w_ref, m_ref, v_ref = refs[:3]
        g_refs = refs[3:3 + l_dim]
        go_ref, d_ref, mo_ref, vo_ref = refs[3 + l_dim:]
        g = g_refs[0][...]
        for l in range(1, l_dim):
            g = jnp.where(pl.program_id(0) == l, g_refs[l][...], g)
        delta, m_new, v_new = _adam_math(w_ref[...], g, m_ref[...], v_ref[...])
        go_ref[...] = g
        d_ref[...] = delta
        mo_ref[...] = m_new
        vo_ref[...] = v_new

    spec = pl.BlockSpec((None, tr, c), lambda l, i: (l, i, 0))
    gspec = pl.BlockSpec((None, tr, c), lambda l, i: (0, i, 0))
    return pl.pallas_call(
        body, name=name, grid=(l_dim, r // tr), in_specs=[spec] * 3 + [gspec] * l_dim, out_specs=[spec] * 4,
        out_shape=[jax.ShapeDtypeStruct(w.shape, F32)] * 4, compiler_params=_params(("parallel",) * 2),
    )(w, m, v, *grads)


def _position():
    return lax.axis_index("x"), lax.axis_index("y"), lax.axis_index("c")


def _other_chips(x, y):
    return [(1 - x, y), (x, 1 - y), (1 - x, 1 - y)]


def _remote(src, dst, send_sem, recv_sem, device):
    return pltpu.make_async_remote_copy(src_ref=src, dst_ref=dst, send_sem=send_sem, recv_sem=recv_sem,
                                        device_id=device, device_id_type=MESH)


ANY = pl.BlockSpec(memory_space=pl.ANY)


def _gather_weights(bufs):
    n = len(bufs)

    def body(*refs):
        send, forward, finish = _gather_steps(refs[n:2 * n], *refs[2 * n:])
        send()
        forward()
        finish()

    return pl.pallas_call(
        body, name="gather_weights", in_specs=[ANY] * n, out_specs=[ANY] * n,
        out_shape=[jax.ShapeDtypeStruct(b.shape, b.dtype) for b in bufs],
        input_output_aliases={a: a for a in range(n)},
        scratch_shapes=_gather_sems(n),
        compiler_params=pltpu.CompilerParams(has_side_effects=True),
    )(*bufs)


def _gather_sems(n):
    return [pltpu.SemaphoreType.DMA((3 * n,))] * 4


def _gather_steps(outs, send_sems, recv_sems, fwd_send, fwd_recv):
    n = len(outs)
    x, y, c = _position()
    chips = _other_chips(x, y)
    sibling = (x, y, 1 - c)

    def half(a, chip, core):
        h = outs[a].shape[2] // 2
        return outs[a].at[:, 2 * chip[0] + chip[1], pl.ds(core * h, h), :]

    def over_ici(a, k, chip):
        block = half(a, chip, c)
        return _remote(block, block, send_sems.at[3 * a + k], recv_sems.at[3 * a + k], (*chips[k], c))

    def over_d2d(a, k, core):
        block = half(a, chips[k], core)
        return _remote(block, block, fwd_send.at[3 * a + k], fwd_recv.at[3 * a + k], sibling)

    def send():
        for a in range(n):
            for k in range(3):
                over_ici(a, k, (x, y)).start()

    def forward():
        for k in range(3):
            for a in range(n):
                over_ici(a, k, chips[k]).wait_recv()
                over_d2d(a, k, c).start()

    def finish():
        for k in range(3):
            for a in range(n):
                over_d2d(a, k, 1 - c).wait_recv()
        for a in range(n):
            for k in range(3):
                over_ici(a, k, (x, y)).wait_send()
                over_d2d(a, k, c).wait_send()

    return send, forward, finish


def _swap_halves(grads, *, name):
    n = len(grads)

    def body(*refs):
        ins, got = refs[:n], refs[n:2 * n]
        send_sems, recv_sems = refs[2 * n:]
        x, y, c = _position()
        sibling = (x, y, 1 - c)
        copies = []
        for a in range(n):
            h = grads[a].shape[2] // 2
            cp = _remote(ins[a].at[:, :, pl.ds((1 - c) * h, h), :], got[a], send_sems.at[a], recv_sems.at[a], sibling)
            cp.start()
            copies.append(cp)
        for cp in copies:
            cp.wait()

    sem = pltpu.SemaphoreType.DMA((n,))
    return pl.pallas_call(
        body, name=name, in_specs=[ANY] * n, out_specs=[ANY] * n,
        out_shape=[jax.ShapeDtypeStruct(g.shape[:2] + (g.shape[2] // 2, g.shape[3]), g.dtype) for g in grads],
        scratch_shapes=[sem, sem], compiler_params=pltpu.CompilerParams(has_side_effects=True),
    )(*grads)


def _exchange_chips(sums):
    n = len(sums)

    def body(*refs):
        send, finish = _exchange_steps(refs[:n], refs[n:2 * n], *refs[2 * n:])
        send()
        finish()

    return pl.pallas_call(
        body, name="exchange_chips", in_specs=[ANY] * n, out_specs=[ANY] * n,
        out_shape=_exchange_shapes(sums), scratch_shapes=_exchange_sems(n),
        compiler_params=pltpu.CompilerParams(has_side_effects=True),
    )(*sums)


def _exchange_shapes(sums):
    return [jax.ShapeDtypeStruct((s.shape[0], 3) + s.shape[2:], s.dtype) for s in sums]


def _exchange_sems(n):
    return [pltpu.SemaphoreType.DMA((3 * n,))] * 2


def _exchange_steps(ins, outs, send_sems, recv_sems):
    n = len(ins)
    x, y, c = _position()
    chips = _other_chips(x, y)

    def copy(a, k):
        chip = chips[k]
        return _remote(ins[a].at[:, 2 * chip[0] + chip[1]], outs[a].at[:, k],
                       send_sems.at[3 * a + k], recv_sems.at[3 * a + k], (*chip, c))

    def send():
        for a in range(n):
            for k in range(3):
                copy(a, k).start()

    def finish():
        for a in range(n):
            for k in range(3):
                copy(a, k).wait()

    return send, finish


def _join_halves(bufs, *, name):
    n = len(bufs)

    def body(*refs):
        outs = refs[n:2 * n]
        send_sems, recv_sems = refs[2 * n:]
        x, y, c = _position()
        sibling = (x, y, 1 - c)
        copies = []
        for a in range(n):
            h = bufs[a].shape[1] // 2
            mine = outs[a].at[:, pl.ds(c * h, h), :]
            cp = _remote(mine, mine, send_sems.at[a], recv_sems.at[a], sibling)
            cp.start()
            copies.append((cp, a, h))
        for cp, a, h in copies:
            cp.wait_send()
            got = outs[a].at[:, pl.ds((1 - c) * h, h), :]
            _remote(got, got, send_sems.at[a], recv_sems.at[a], sibling).wait_recv()

    sem = pltpu.SemaphoreType.DMA((n,))
    return pl.pallas_call(
        body, name=name, in_specs=[ANY] * n, out_specs=[ANY] * n,
        out_shape=[jax.ShapeDtypeStruct(b.shape, b.dtype) for b in bufs],
        input_output_aliases={a: a for a in range(n)},
        scratch_shapes=[sem, sem], compiler_params=pltpu.CompilerParams(has_side_effects=True),
    )(*bufs)


def _allreduce_small(packs):
    n = len(packs)

    def body(*refs):
        ins, outs, gath = refs[:n], refs[n:2 * n], refs[2 * n:3 * n]
        send_sems, recv_sems = refs[3 * n:]
        x, y, c = _position()
        me, sibling = (x, y, c), (x, y, 1 - c)
        chips = _other_chips(x, y)

        def slot(a, dev):
            return gath[a].at[4 * dev[0] + 2 * dev[1] + dev[2]]

        def copy(a, k, block, to, src=None):
            return _remote(slot(a, block) if src is None else src, slot(a, block),
                           send_sems.at[7 * a + k], recv_sems.at[7 * a + k], to)

        started = []
        for a in range(n):
            slot(a, me)[...] = ins[a][...]
            first = [copy(a, 0, me, sibling, src=ins[a])]
            first += [copy(a, 1 + k, me, (*chip, c), src=ins[a]) for k, chip in enumerate(chips)]
            for cp in first:
                cp.start()
            started += first
        for a in range(n):
            for k, chip in enumerate(chips):
                copy(a, 1 + k, (*chip, c), me).wait_recv()
                cp = copy(a, 4 + k, (*chip, c), sibling)
                cp.start()
                started.append(cp)
        for a in range(n):
            copy(a, 0, sibling, me).wait_recv()
            for k, chip in enumerate(chips):
                copy(a, 4 + k, (*chip, 1 - c), me).wait_recv()
        for cp in started:
            cp.wait_send()
        for a in range(n):
            total = gath[a][0]
            for d in range(1, N_DEV):
                total = total + gath[a][d]
            outs[a][...] = total

    vmem = pl.BlockSpec(memory_space=pltpu.VMEM)
    sem = pltpu.SemaphoreType.DMA((7 * n,))
    return pl.pallas_call(
        body, name="allreduce_small", in_specs=[vmem] * n, out_specs=[vmem] * n,
        out_shape=[jax.ShapeDtypeStruct(p.shape, p.dtype) for p in packs],
        scratch_shapes=[pltpu.VMEM((N_DEV,) + p.shape, p.dtype) for p in packs] + [sem, sem],
        compiler_params=pltpu.CompilerParams(has_side_effects=True, vmem_limit_bytes=VMEM_LIMIT_BYTES),
    )(*packs)


LOSS_ROW = 1040


def _pad_rows(a, rows=8):
    return jnp.concatenate([a, jnp.zeros((rows - a.shape[0], a.shape[1]), a.dtype)], axis=0)

def _adam_small(wide, mid, narrow, params):
    names = ["mix_norm_g", "mlp_norm_g", "final_norm_g", "conv_b", "conv_w", "sgu_norm_g", "sgu_norm_b",
             "pool_w", "pool_scale", "sgu_w", "sgu_b"]
    n = len(names)

    def body(*refs):
        wide_ref, mid_ref, narrow_ref = refs[:3]
        wmv = refs[3:3 + 3 * n]
        outs = refs[3 + 3 * n:]
        x, y, _ = _position()
        q = 2 * x + y

        def my_quarter(rows):
            parts = [rows[:, s * TILE:(s + 1) * TILE] for s in range(N_CHIP)]
            return jnp.where(q == 0, parts[0], jnp.where(q == 1, parts[1], jnp.where(q == 2, parts[2], parts[3])))

        def tiles(first_row):
            return [((0, g), narrow_ref[first_row + g * TILE:first_row + (g + 1) * TILE, :]) for g in range(4)]

        grads = {
            "mix_norm_g": [((), wide_ref[0:2, :])],
            "mlp_norm_g": [((), wide_ref[8:10, :])],
            "final_norm_g": [((), wide_ref[16:17, :])],
            "conv_b": [((), mid_ref[0:1, :])],
            "conv_w": [((0,), my_quarter(mid_ref[8:11, :]))],
            "sgu_norm_g": [((), my_quarter(mid_ref[16:17, :]))],
            "sgu_norm_b": [((), my_quarter(mid_ref[24:25, :]))],
            "pool_w": tiles(0),
            "sgu_w": tiles(512),
            "pool_scale": [((0,), narrow_ref[1024:1028, :])],
            "sgu_b": [((0,), narrow_ref[1032:1036, :])],
        }
        for i, name in enumerate(names):
            w_ref, m_ref, v_ref = wmv[3 * i:3 * i + 3]
            for lead, g in grads[name]:
                idx = lead + (slice(None), slice(None))
                delta, m_new, v_new = _adam_math(w_ref[idx], g, m_ref[idx], v_ref[idx])
                outs[4 * i][idx] = g
                outs[4 * i + 1][idx] = delta
                outs[4 * i + 2][idx] = m_new
                outs[4 * i + 3][idx] = v_new

    vmem = pl.BlockSpec(memory_space=pltpu.VMEM)
    args, out_shape = [wide, mid, narrow], []
    for name in names:
        w, m, v = params[name]
        args += [w, m, v]
        out_shape += [jax.ShapeDtypeStruct(w.shape, F32)] * 4
    res = pl.pallas_call(
        body, name="adam_small", in_specs=[vmem] * len(args), out_specs=[vmem] * len(out_shape),
        out_shape=out_shape, compiler_params=pltpu.CompilerParams(vmem_limit_bytes=VMEM_LIMIT_BYTES),
    )(*args)
    return {name: res[4 * i:4 * i + 4] for i, name in enumerate(names)}


def _pair_sums(grads, pos, tag):
    got = _swap_halves(grads, name=f"swap_halves_{tag}")
    return [_pair_sum(a, b, pos, name=f"pair_sum_{tag}{i}") for i, (a, b) in enumerate(zip(grads, got))]


def _finish_reduce(sums, landed, pos, tag):
    halves = [_chip_sum(s, r, pos, name=f"chip_sum_{tag}{i}") for i, (s, r) in enumerate(zip(sums, landed))]
    return _join_halves(halves, name=f"join_halves_{tag}")


def kernel(x, mix_norm_g, mlp_norm_g, ab_w_in, pool_w, pool_scale, conv_w, conv_b, ab_w_out, cd_w_in, sgu_norm_g, sgu_norm_b, sgu_w, sgu_b, cd_w_out, mlp_w1, mlp_w2, final_norm_g, loss_target, m_mix_norm_g, m_mlp_norm_g, m_ab_w_in, m_pool_w, m_pool_scale, m_conv_w, m_conv_b, m_ab_w_out, m_cd_w_in, m_sgu_norm_g, m_sgu_norm_b, m_sgu_w, m_sgu_b, m_cd_w_out, m_mlp_w1, m_mlp_w2, m_final_norm_g, v_mix_norm_g, v_mlp_norm_g, v_ab_w_in, v_pool_w, v_pool_scale, v_conv_w, v_conv_b, v_ab_w_out, v_cd_w_in, v_sgu_norm_g, v_sgu_norm_b, v_sgu_w, v_sgu_b, v_cd_w_out, v_mlp_w1, v_mlp_w2, v_final_norm_g):
    nseq, t_len, d = x.shape
    m_tok = nseq * t_len
    h0 = x.reshape(m_tok, d)
    target = loss_target.reshape(m_tok, d)

    x_idx, y_idx = lax.axis_index("x"), lax.axis_index("y")
    q_idx = 2 * x_idx + y_idx
    pos = jnp.stack([q_idx, lax.axis_index("c")]).astype(jnp.int32)
    def shard_buffer(w, layer, tag):
        return _cast_place(w, layer, pos, name=f"cast_place_{tag}")

    def row_block(w):
        return w.reshape(1, 1, -1, w.shape[-1])

    w_ab_in, w_ab_out, w_1_0, w_2_0, w_cd_in = _gather_weights(
        [shard_buffer(ab_w_in, 0, "ab_in"), shard_buffer(ab_w_out, 0, "ab_out"), shard_buffer(mlp_w1, 0, "w1_0"),
         shard_buffer(mlp_w2, 0, "w2_0"), shard_buffer(cd_w_in, 0, "cd_in")])
    w_ab_out, w_2_0 = row_block(w_ab_out), row_block(w_2_0)
    later_weights = [shard_buffer(cd_w_out, 0, "cd_out"), shard_buffer(mlp_w1, 1, "w1_1"),
                     shard_buffer(mlp_w2, 1, "w2_1")]

    pool_w3, pool_scale3 = pool_w[0], pool_scale[0].reshape(4, 1, TILE)
    sgu_w3 = sgu_w[0]
    sgu_w3_t = jnp.swapaxes(sgu_w3, 1, 2)
    sgu_bias_tile = jnp.broadcast_to(sgu_b[0][:, :, None], (4, TILE, TILE))
    conv_w2, conv_b2 = conv_w[0], conv_b
    def place_quarter(v):
        return lax.dynamic_update_slice(jnp.zeros((v.shape[0], 4 * TILE), F32), v, (0, q_idx * TILE))

    sharded_small = jnp.concatenate(
        [place_quarter(conv_w[0]), place_quarter(sgu_norm_g), place_quarter(sgu_norm_b),
         jnp.zeros((3, 4 * TILE), F32)], axis=0)
    sharded_small, = _allreduce_small([sharded_small])
    sharded_small = sharded_small * 0.5
    conv_w_full = sharded_small[0:3]
    sgu_g_full = sharded_small[3:4]
    sgu_b_full = sharded_small[4:5]

    xn0 = _rms_fwd(h0, mix_norm_g[0:1], name="rms_fwd_mix0")
    p_ab = _mm_nn(xn0, w_ab_in, 0, out_dtype=BF16, name="ab_in_proj")
    mix0 = _ab_fwd(p_ab, pool_w3, pool_scale3, conv_w_full, conv_b2, nseq, t_len)
    h1 = _mm_nn(mix0, w_ab_out, 0, out_dtype=F32, name="ab_out_proj", epilogue="residual", extra=h0)
    hn0 = _rms_fwd(h1, mlp_norm_g[0:1], name="rms_fwd_mlp0")
    act0 = _mm_nn(hn0, w_1_0, 0, out_dtype=BF16, name="mlp0_up", epilogue="relu2")
    h2 = _mm_nn(act0, w_2_0, 0, out_dtype=F32, name="mlp0_down", epilogue="residual", extra=h1)

    xn1 = _rms_fwd(h2, mix_norm_g[1:2], name="rms_fwd_mix1")
    p_cd = _mm_nn(xn1, w_cd_in, 0, out_dtype=BF16, name="cd_in_proj")
    c_out = _sgu_fwd(p_cd, sgu_g_full, sgu_b_full, sgu_w3, sgu_bias_tile)
    d_out, ltot, (w_cd_out, w_1_1, w_2_1) = _sb_fwd(p_cd, nseq, t_len, later_weights)
    w_cd_out, w_2_1 = row_block(w_cd_out), row_block(w_2_1)
    mix1 = jnp.concatenate([c_out, d_out], axis=1)
    h3 = _mm_nn(mix1, w_cd_out, 0, out_dtype=F32, name="cd_out_proj", epilogue="residual", extra=h2)
    hn1 = _rms_fwd(h3, mlp_norm_g[1:2], name="rms_fwd_mlp1")
    act1 = _mm_nn(hn1, w_1_1, 0, out_dtype=BF16, name="mlp1_up", epilogue="relu2")
    h4 = _mm_nn(act1, w_2_1, 0, out_dtype=F32, name="mlp1_down", epilogue="residual", extra=h3)

    dh4, dh4_bf, dg_final, loss_tile = _final_loss(h4, final_norm_g.reshape(1, d), target)

    def mlp_bwd(dh_out, dh_out_bf, h_in, hn, act, w_1, w_2, layer, tag):
        dz = _mm_nt(dh_out_bf, w_2, 0, out_dtype=BF16, name=f"mlp{tag}_down_bwd",
                    epilogue="relu2_bwd", extra=act)
        g_w2 = _mm_tn(act, dh_out_bf, 1, name=f"mlp{tag}_down_wgrad")
        g_w1 = _mm_tn(hn, dz, N_CHIP, name=f"mlp{tag}_up_wgrad")
        dhn = _mm_nt(dz, w_1, 0, out_dtype=F32, name=f"mlp{tag}_up_bwd")
        dh_in, dh_in_bf, dg = _rms_bwd(h_in, mlp_norm_g[layer:layer + 1], dhn, dh_out, name=f"rms_bwd_mlp{tag}")
        return dh_in, dh_in_bf, dg, g_w1, g_w2

    def as_pieces(g):
        return g.reshape(1, N_CHIP, -1, g.shape[-1]) if g.shape[1] == 1 else g

    dh3, dh3_bf, dg_mlp1, g_w1_1, g_w2_1 = mlp_bwd(dh4, dh4_bf, h3, hn1, act1, w_1_1, w_2_1, 1, "1")
    sums_a = _pair_sums([g_w1_1, as_pieces(g_w2_1)], pos, "a")

    dmix1 = _mm_nt(dh3_bf, w_cd_out, 0, out_dtype=BF16, name="cd_out_bwd")
    g_cd_out = _mm_tn(mix1, dh3_bf, 1, name="cd_out_wgrad")
    du, dv, dsgu_w, dsgu_bs, dsgu_g, dsgu_b = _sgu_bwd(p_cd, dmix1, sgu_g_full, sgu_b_full, sgu_w3, sgu_w3_t,
                                                      sgu_bias_tile)
    dq, dk, dvv, landed_a = _sb_bwd(p_cd, dmix1, ltot, nseq, t_len, sums_a)
    r_w1_1, r_w2_1 = _finish_reduce(sums_a, landed_a, pos, "a")
    dp_cd = jnp.concatenate([du, dv, dq, dk, dvv], axis=1)
    g_cd_in = _mm_tn(xn1, dp_cd, N_CHIP, name="cd_in_wgrad")
    dxn1 = _mm_nt(dp_cd, w_cd_in, 0, out_dtype=F32, name="cd_in_bwd")
    dh2, dh2_bf, dg_mix1 = _rms_bwd(h2, mix_norm_g[1:2], dxn1, dh3, name="rms_bwd_mix1")

    dh1, dh1_bf, dg_mlp0, g_w1_0, g_w2_0 = mlp_bwd(dh2, dh2_bf, h1, hn0, act0, w_1_0, w_2_0, 0, "0")

    dmix0 = _mm_nt(dh1_bf, w_ab_out, 0, out_dtype=BF16, name="ab_out_bwd")
    g_ab_out = _mm_tn(mix0, dh1_bf, 1, name="ab_out_wgrad")
    da, dxb, dgb, dgc, dpool_w, dpool_scale, dconv_w, dconv_b = _ab_bwd(
        p_ab, dmix0, pool_w3, pool_scale3, conv_w_full, conv_b2, nseq, t_len)
    dp_ab = jnp.concatenate([da, dxb, dgb, dgc], axis=1)
    g_ab_in = _mm_tn(xn0, dp_ab, N_CHIP, name="ab_in_wgrad")
    dxn0 = _mm_nt(dp_ab, w_ab_in, 0, out_dtype=F32, name="ab_in_bwd")
    grad_x, _, dg_mix0 = _rms_bwd(h0, mix_norm_g[0:1], dxn0, dh1, name="rms_bwd_mix0")

    sums_b = _pair_sums([as_pieces(g) for g in (g_ab_in, g_ab_out, g_cd_in, g_cd_out, g_w1_0, g_w2_0)], pos, "b")
    r_ab_in, r_ab_out, r_cd_in, r_cd_out, r_w1_0, r_w2_0 = _finish_reduce(
        sums_b, _exchange_chips(sums_b), pos, "b")

    big_out = {
        "ab_w_in": _adam_big(ab_w_in, m_ab_w_in, v_ab_w_in, [r_ab_in], name="adam_ab_w_in"),
        "ab_w_out": _adam_big(ab_w_out, m_ab_w_out, v_ab_w_out, [r_ab_out], name="adam_ab_w_out"),
        "cd_w_in": _adam_big(cd_w_in, m_cd_w_in, v_cd_w_in, [r_cd_in], name="adam_cd_w_in"),
        "cd_w_out": _adam_big(cd_w_out, m_cd_w_out, v_cd_w_out, [r_cd_out], name="adam_cd_w_out"),
        "mlp_w1": _adam_big(mlp_w1, m_mlp_w1, v_mlp_w1, [r_w1_0, r_w1_1], name="adam_mlp_w1"),
        "mlp_w2": _adam_big(mlp_w2, m_mlp_w2, v_mlp_w2, [r_w2_0, r_w2_1], name="adam_mlp_w2"),
    }

    wide = jnp.concatenate([_pad_rows(jnp.concatenate([dg_mix0, dg_mix1], axis=0)),
                            _pad_rows(jnp.concatenate([dg_mlp0, dg_mlp1], axis=0)), _pad_rows(dg_final)], axis=0)
    mid = jnp.concatenate([_pad_rows(dconv_b), _pad_rows(dconv_w), _pad_rows(dsgu_g), _pad_rows(dsgu_b)], axis=0)
    narrow = jnp.concatenate(
        [dpool_w.reshape(4 * TILE, TILE), dsgu_w.reshape(4 * TILE, TILE), _pad_rows(dpool_scale.reshape(4, TILE)),
         _pad_rows(dsgu_bs[:, :, 0]), loss_tile], axis=0)
    wide, mid, narrow = _allreduce_small([wide, mid, narrow])
    small_out = _adam_small(wide, mid, narrow, {
        "mix_norm_g": (mix_norm_g, m_mix_norm_g, v_mix_norm_g),
        "mlp_norm_g": (mlp_norm_g, m_mlp_norm_g, v_mlp_norm_g),
        "final_norm_g": tuple(a.reshape(1, d) for a in (final_norm_g, m_final_norm_g, v_final_norm_g)),
        "conv_b": (conv_b, m_conv_b, v_conv_b),
        "conv_w": (conv_w, m_conv_w, v_conv_w),
        "sgu_norm_g": (sgu_norm_g, m_sgu_norm_g, v_sgu_norm_g),
        "sgu_norm_b": (sgu_norm_b, m_sgu_norm_b, v_sgu_norm_b),
        "pool_w": (pool_w, m_pool_w, v_pool_w),
        "pool_scale": (pool_scale, m_pool_scale, v_pool_scale),
        "sgu_w": (sgu_w, m_sgu_w, v_sgu_w),
        "sgu_b": (sgu_b, m_sgu_b, v_sgu_b),
    })
    small_out["final_norm_g"] = [a.reshape(d) for a in small_out["final_norm_g"]]

    order = ["mix_norm_g", "mlp_norm_g", "ab_w_in", "pool_w", "pool_scale", "conv_w", "conv_b", "ab_w_out",
             "cd_w_in", "sgu_norm_g", "sgu_norm_b", "sgu_w", "sgu_b", "cd_w_out", "mlp_w1", "mlp_w2",
             "final_norm_g"]
    both = {**big_out, **small_out}
    loss = narrow[LOSS_ROW, 0]
    outs = [loss, grad_x.reshape(nseq, t_len, d)]
    for kind in range(4):
        outs += [both[name][kind] for name in order]
    return tuple(outs)
```

```python
import math

import jax
import jax.numpy as jnp
from jax import lax
from jax.experimental import pallas as pl
from jax.experimental.pallas import tpu as pltpu

F32 = jnp.float32
BF16 = jnp.bfloat16
MESH = pl.DeviceIdType.MESH

D_MODEL = 1024
EPS = 1e-6
TILE = 128
N_CHIP = 4
N_DEV = 8
VMEM_LIMIT_BYTES = 56 * 1024 * 1024

ADAM_LR = 0.001
ADAM_B1 = 0.9
ADAM_B2 = 0.999
ADAM_EPS = 1e-08
ADAM_WD = 0.01
ADAM_STEP = 10

NT_DIMS = (((1,), (1,)), ((), ()))
TN_DIMS = (((0,), (0,)), ((), ()))


def _params(sem=None):
    return pltpu.CompilerParams(dimension_semantics=sem, vmem_limit_bytes=VMEM_LIMIT_BYTES)


def _mm_nn(a, b4, layer, *, out_dtype, name, epilogue=None, extra=None, norm_g=None, tm=1024, tk=1024):
    m, k_dim = a.shape
    _, s_dim, kb, n = b4.shape
    assert kb == k_dim
    tn = min(n, 1024)
    assert m % tm == 0 and k_dim % tk == 0 and n % tn == 0
    nk, npb = k_dim // tk, n // tn
    grid = (m // tm, s_dim * npb, nk)
    n_in = 2 + (extra is not None) + (norm_g is not None)
    n_out = 1 + (norm_g is not None)
    assert norm_g is None or tn == s_dim * n

    def body(*refs):
        a_ref, b_ref = refs[:2]
        e_ref = refs[2] if extra is not None else None
        g_ref = refs[n_in - 1] if norm_g is not None else None
        o_ref = refs[n_in]
        scr = refs[n_in + n_out:]

        def finish(acc):
            if epilogue == "relu2":
                r = jnp.maximum(acc, 0.0)
                acc = r * r
            elif epilogue == "residual":
                acc = acc + e_ref[...]
            o_ref[...] = acc.astype(out_dtype)
            if norm_g is not None:
                rstd = lax.rsqrt(jnp.mean(acc * acc, axis=-1, keepdims=True) + EPS)
                refs[n_in + 1][...] = (acc * rstd * g_ref[...]).astype(BF16)

        part = jnp.dot(a_ref[...], b_ref[...], preferred_element_type=F32)
        if nk == 1:
            finish(part)
        else:
            acc_ref, = scr
            kk = pl.program_id(2)

            @pl.when(kk == 0)
            def _():
                acc_ref[...] = part

            @pl.when(kk > 0)
            def _():
                acc_ref[...] += part

            @pl.when(kk == nk - 1)
            def _():
                finish(acc_ref[...])

    in_specs = [
        pl.BlockSpec((tm, tk), lambda i, j, kk: (i, kk)),
        pl.BlockSpec((None, None, tk, tn), lambda i, j, kk: (layer, j // npb, kk, j % npb)),
    ]
    args = [a, b4]
    if extra is not None:
        in_specs.append(pl.BlockSpec((tm, tn), lambda i, j, kk: (i, j)))
        args.append(extra)
    out_block = pl.BlockSpec((tm, tn), lambda i, j, kk: (i, j))
    out_specs, out_shape = [out_block], [jax.ShapeDtypeStruct((m, s_dim * n), out_dtype)]
    if norm_g is not None:
        in_specs.append(pl.BlockSpec((1, tn), lambda i, j, kk: (0, j)))
        args.append(norm_g)
        out_specs.append(out_block)
        out_shape.append(jax.ShapeDtypeStruct((m, s_dim * n), BF16))
    res = pl.pallas_call(
        body, name=name, grid=grid, in_specs=in_specs, out_specs=out_specs, out_shape=out_shape,
        scratch_shapes=[] if nk == 1 else [pltpu.VMEM((tm, tn), F32)],
        compiler_params=_params(("parallel", "parallel", "arbitrary")),
    )(*args)
    return res[0] if norm_g is None else res


def _mm_nt(a, b4, layer, *, out_dtype, name, epilogue=None, extra=None, tm=1024, tn=1024):
    m, k_dim = a.shape
    _, s_dim, n_out, n = b4.shape
    assert k_dim == s_dim * n
    tk = min(n, 1024)
    tn = min(tn, n_out)
    assert m % tm == 0 and n_out % tn == 0 and n % tk == 0
    kpb = n // tk
    nk = s_dim * kpb
    grid = (m // tm, n_out // tn, nk)
    rms = epilogue == "rms_bwd"
    assert not rms or tn == n_out
    extras = [] if extra is None else (list(extra) if rms else [extra])
    n_in = 2 + len(extras)
    n_res = 3 if rms else 1

    def body(*refs):
        a_ref, b_ref = refs[:2]
        e_refs = refs[2:n_in]
        o_ref = refs[n_in]
        scr = refs[n_in + n_res:]

        def finish(acc):
            if epilogue == "relu2_bwd":
                acc = acc * (2.0 * jnp.sqrt(e_refs[0][...].astype(F32)))
            if not rms:
                o_ref[...] = acc.astype(out_dtype)
                return
            h_ref, g_ref, dres_ref = e_refs
            dhb_ref, dg_ref = refs[n_in + 1:n_in + 3]
            hv = h_ref[...]
            rstd = lax.rsqrt(jnp.mean(hv * hv, axis=-1, keepdims=True) + EPS)
            xhat = hv * rstd
            dxhat = acc * g_ref[...]
            dh = dres_ref[...] + rstd * (dxhat - xhat * jnp.mean(dxhat * xhat, axis=-1, keepdims=True))
            o_ref[...] = dh
            dhb_ref[...] = dh.astype(BF16)
            dg_part = jnp.sum(acc * xhat, axis=0, keepdims=True)
            first = pl.program_id(0) == 0

            @pl.when(first)
            def _():
                dg_ref[...] = dg_part

            @pl.when(jnp.logical_not(first))
            def _():
                dg_ref[...] += dg_part

        part = lax.dot_general(a_ref[...], b_ref[...], NT_DIMS, preferred_element_type=F32)
        if nk == 1:
            finish(part)
        else:
            acc_ref, = scr
            kk = pl.program_id(2)

            @pl.when(kk == 0)
            def _():
                acc_ref[...] = part

            @pl.when(kk > 0)
            def _():
                acc_ref[...] += part

            @pl.when(kk == nk - 1)
            def _():
                finish(acc_ref[...])

    in_specs = [
        pl.BlockSpec((tm, tk), lambda i, j, kk: (i, kk)),
        pl.BlockSpec((None, None, tn, tk), lambda i, j, kk: (layer, kk // kpb, j, kk % kpb)),
    ]
    args = [a, b4] + extras
    block = pl.BlockSpec((tm, tn), lambda i, j, kk: (i, j))
    vec = pl.BlockSpec((1, tn), lambda i, j, kk: (0, j))
    if rms:
        in_specs += [block, vec, block]
        out_specs = [block, block, vec]
        out_shape = [jax.ShapeDtypeStruct((m, n_out), F32), jax.ShapeDtypeStruct((m, n_out), BF16),
                     jax.ShapeDtypeStruct((1, n_out), F32)]
    else:
        in_specs += [block] * len(extras)
        out_specs, out_shape = [block], [jax.ShapeDtypeStruct((m, n_out), out_dtype)]
    res = pl.pallas_call(
        body, name=name, grid=grid, in_specs=in_specs, out_specs=out_specs, out_shape=out_shape,
        scratch_shapes=[] if nk == 1 else [pltpu.VMEM((tm, tn), F32)],
        compiler_params=_params(("arbitrary",) * 3 if rms else ("parallel", "parallel", "arbitrary")),
    )(*args)
    return res if rms else res[0]


def _mm_tn(a, b, s_dim, *, name, tm=1024, t1=1024):
    m, k1 = a.shape
    mb, n_all = b.shape
    assert mb == m and n_all % s_dim == 0
    n = n_all // s_dim
    tn = min(n, 1024)
    t1 = min(t1, k1)
    assert m % tm == 0 and k1 % t1 == 0 and n % tn == 0
    npb = n // tn
    nk = m // tm
    grid = (k1 // t1, s_dim * npb, nk)

    def body(a_ref, b_ref, o_ref, acc_ref):
        kk = pl.program_id(2)
        part = lax.dot_general(a_ref[...], b_ref[...], TN_DIMS, preferred_element_type=F32)

        @pl.when(kk == 0)
        def _():
            acc_ref[...] = part

        @pl.when(kk > 0)
        def _():
            acc_ref[...] += part

        @pl.when(kk == nk - 1)
        def _():
            o_ref[...] = acc_ref[...].astype(BF16)

    return pl.pallas_call(
        body, name=name, grid=grid,
        in_specs=[pl.BlockSpec((tm, t1), lambda i, j, kk: (kk, i)),
                  pl.BlockSpec((tm, tn), lambda i, j, kk: (kk, j))],
        out_specs=pl.BlockSpec((None, None, t1, tn), lambda i, j, kk: (0, j // npb, i, j % npb)),
        out_shape=jax.ShapeDtypeStruct((1, s_dim, k1, n), BF16),
        scratch_shapes=[pltpu.VMEM((t1, tn), F32)],
        compiler_params=_params(("parallel", "parallel", "arbitrary")),
    )(a, b)


ROW_TILE = 512


def _rms_fwd(h, g, *, name):
    m, d = h.shape

    def body(h_ref, g_ref, o_ref):
        hv = h_ref[...]
        rstd = lax.rsqrt(jnp.mean(hv * hv, axis=-1, keepdims=True) + EPS)
        o_ref[...] = (hv * rstd * g_ref[...]).astype(BF16)

    return pl.pallas_call(
        body, name=name, grid=(m // ROW_TILE,),
        in_specs=[pl.BlockSpec((ROW_TILE, d), lambda i: (i, 0)), pl.BlockSpec((1, d), lambda i: (0, 0))],
        out_specs=pl.BlockSpec((ROW_TILE, d), lambda i: (i, 0)),
        out_shape=jax.ShapeDtypeStruct((m, d), BF16),
        compiler_params=_params(("parallel",)),
    )(h, g)


def _final_loss(h, g, target):
    m, d = h.shape

    def body(h_ref, g_ref, t_ref, dh_ref, dhb_ref, dg_ref, loss_ref):
        hv = h_ref[...]
        gv = g_ref[...]
        rstd = lax.rsqrt(jnp.mean(hv * hv, axis=-1, keepdims=True) + EPS)
        xhat = hv * rstd
        err = xhat * gv - t_ref[...]
        dy = err * (1.0 / d)
        dxhat = dy * gv
        dh = rstd * (dxhat - xhat * jnp.mean(dxhat * xhat, axis=-1, keepdims=True))
        dh_ref[...] = dh
        dhb_ref[...] = dh.astype(BF16)
        dg_part = jnp.sum(dy * xhat, axis=0, keepdims=True)
        sq = jnp.sum(jnp.sum(err * err, axis=1, keepdims=True), axis=0, keepdims=True) * (0.5 / d)
        loss_part = jnp.broadcast_to(sq, (8, TILE))

        @pl.when(pl.program_id(0) == 0)
        def _():
            dg_ref[...] = dg_part
            loss_ref[...] = loss_part

        @pl.when(pl.program_id(0) > 0)
        def _():
            dg_ref[...] += dg_part
            loss_ref[...] += loss_part

    row = pl.BlockSpec((ROW_TILE, d), lambda i: (i, 0))
    vec = pl.BlockSpec((1, d), lambda i: (0, 0))
    return pl.pallas_call(
        body, name="final_loss", grid=(m // ROW_TILE,),
        in_specs=[row, vec, row],
        out_specs=[row, row, vec, pl.BlockSpec((8, TILE), lambda i: (0, 0))],
        out_shape=[jax.ShapeDtypeStruct((m, d), F32), jax.ShapeDtypeStruct((m, d), BF16),
                   jax.ShapeDtypeStruct((1, d), F32), jax.ShapeDtypeStruct((8, TILE), F32)],
        compiler_params=_params(("arbitrary",)),
    )(h, g, target)


def _shift_down(x, s, t_idx):
    return jnp.where(t_idx >= s, pltpu.roll(x, s, 0), 0.0)


def _shift_up(x, s, t_idx, t_len):
    return jnp.where(t_idx < t_len - s, pltpu.roll(x, t_len - s, 0), 0.0)


def _pool_select(group, s2, s4, s8, s16):
    return jnp.where(group == 0, s2, jnp.where(group == 1, s4, jnp.where(group == 2, s8, s16)))


def _pool_count(group, t_idx):
    win = jnp.left_shift(2, group)
    return jnp.minimum(t_idx + 1, win).astype(F32)


def _pool_fwd_math(a, group, t_idx):
    s2 = a + _shift_down(a, 1, t_idx)
    s4 = s2 + _shift_down(s2, 2, t_idx)
    s8 = s4 + _shift_down(s4, 4, t_idx)
    s16 = s8 + _shift_down(s8, 8, t_idx)
    return _pool_select(group, s2, s4, s8, s16) / _pool_count(group, t_idx) - a


def _pool_bwd_math(dpooled, group, t_idx, t_len):
    e = dpooled / _pool_count(group, t_idx)
    s2 = e + _shift_up(e, 1, t_idx, t_len)
    s4 = s2 + _shift_up(s2, 2, t_idx, t_len)
    s8 = s4 + _shift_up(s4, 4, t_idx, t_len)
    s16 = s8 + _shift_up(s8, 8, t_idx, t_len)
    return _pool_select(group, s2, s4, s8, s16) - dpooled


def _conv_fwd_math(c, w_ref, b_ref, t_idx):
    return (w_ref[0:1, :] * _shift_down(c, 2, t_idx) + w_ref[1:2, :] * _shift_down(c, 1, t_idx)
            + w_ref[2:3, :] * c + b_ref[...])


def _ab_fwd(p, pool_w, pool_scale, conv_w, conv_b, nseq, t_len):
    m = p.shape[0]
    ng = 4

    def body(a_ref, xb_ref, gb_ref, gc_ref, pw_ref, ps_ref, cw_ref, cb_ref, o_ref):
        j = pl.program_id(1)
        t_idx = lax.broadcasted_iota(jnp.int32, (t_len, TILE), 0)

        @pl.when(j < ng)
        def _():
            pooled = _pool_fwd_math(a_ref[...].astype(F32), j, t_idx)
            mixed = jnp.dot(pooled.astype(BF16), pw_ref[...].astype(BF16), preferred_element_type=F32)
            o_ref[...] = (mixed * ps_ref[...]).astype(BF16)

        @pl.when(j >= ng)
        def _():
            c = gc_ref[...].astype(F32) * xb_ref[...].astype(F32)
            y = _conv_fwd_math(c, cw_ref, cb_ref, t_idx)
            o_ref[...] = (gb_ref[...].astype(F32) * y).astype(BF16)

    def pool_j(j):
        return jnp.minimum(j, ng - 1)

    def conv_j(j):
        return jnp.maximum(j - ng, 0)

    in_specs = [
        pl.BlockSpec((t_len, TILE), lambda s, j: (s, pool_j(j))),
        pl.BlockSpec((t_len, TILE), lambda s, j: (s, ng + conv_j(j))),
        pl.BlockSpec((t_len, TILE), lambda s, j: (s, 2 * ng + conv_j(j))),
        pl.BlockSpec((t_len, TILE), lambda s, j: (s, 3 * ng + conv_j(j))),
        pl.BlockSpec((None, TILE, TILE), lambda s, j: (pool_j(j), 0, 0)),
        pl.BlockSpec((None, 1, TILE), lambda s, j: (pool_j(j), 0, 0)),
        pl.BlockSpec((3, TILE), lambda s, j: (0, conv_j(j))),
        pl.BlockSpec((1, TILE), lambda s, j: (0, conv_j(j))),
    ]
    return pl.pallas_call(
        body, name="ab_mixer_fwd", grid=(nseq, 2 * ng), in_specs=in_specs,
        out_specs=pl.BlockSpec((t_len, TILE), lambda s, j: (s, j)),
        out_shape=jax.ShapeDtypeStruct((m, 2 * ng * TILE), BF16),
        compiler_params=_params(("parallel", "arbitrary")),
    )(p, p, p, p, pool_w, pool_scale, conv_w, conv_b)


def _ab_bwd(p, dmix, pool_w, pool_scale, conv_w, conv_b, nseq, t_len):
    m = p.shape[0]
    ng = 4

    def body(a_ref, xb_ref, gb_ref, gc_ref, dma_ref, dmb_ref, pw_ref, ps_ref, cw_ref, cb_ref,
             da_ref, dxb_ref, dgb_ref, dgc_ref, dpw_ref, dps_ref, dcw_ref, dcb_ref):
        j = pl.program_id(0)
        first = pl.program_id(1) == 0
        t_idx = lax.broadcasted_iota(jnp.int32, (t_len, TILE), 0)

        pooled = _pool_fwd_math(a_ref[...].astype(F32), j, t_idx).astype(BF16)
        w_bf = pw_ref[...].astype(BF16)
        mixed = jnp.dot(pooled, w_bf, preferred_element_type=F32)
        dm = dma_ref[...].astype(F32)
        dps = jnp.sum(dm * mixed, axis=0, keepdims=True)
        dmixed = (dm * ps_ref[...]).astype(BF16)
        dpw = lax.dot_general(pooled, dmixed, TN_DIMS, preferred_element_type=F32)
        dpooled = lax.dot_general(dmixed, w_bf, NT_DIMS, preferred_element_type=F32)
        da_ref[...] = _pool_bwd_math(dpooled, j, t_idx, t_len).astype(BF16)

        xb = xb_ref[...].astype(F32)
        gb = gb_ref[...].astype(F32)
        gc = gc_ref[...].astype(F32)
        d = dmb_ref[...].astype(F32)
        c = gc * xb
        c1 = _shift_down(c, 1, t_idx)
        c2 = _shift_down(c, 2, t_idx)
        y = cw_ref[0:1, :] * c2 + cw_ref[1:2, :] * c1 + cw_ref[2:3, :] * c + cb_ref[...]
        dgb_ref[...] = (d * y).astype(BF16)
        dy = d * gb
        dc = (cw_ref[2:3, :] * dy + cw_ref[1:2, :] * _shift_up(dy, 1, t_idx, t_len)
              + cw_ref[0:1, :] * _shift_up(dy, 2, t_idx, t_len))
        dgc_ref[...] = (dc * xb).astype(BF16)
        dxb_ref[...] = (dc * gc).astype(BF16)
        dcw = jnp.concatenate([jnp.sum(dy * c2, axis=0, keepdims=True),
                               jnp.sum(dy * c1, axis=0, keepdims=True),
                               jnp.sum(dy * c, axis=0, keepdims=True)], axis=0)
        dcb = jnp.sum(dy, axis=0, keepdims=True)

        @pl.when(first)
        def _():
            dpw_ref[...] = dpw
            dps_ref[...] = dps
            dcw_ref[...] = dcw
            dcb_ref[...] = dcb

        @pl.when(jnp.logical_not(first))
        def _():
            dpw_ref[...] += dpw
            dps_ref[...] += dps
            dcw_ref[...] += dcw
            dcb_ref[...] += dcb

    def col(k):
        return pl.BlockSpec((t_len, TILE), lambda j, s: (s, k * ng + j))

    in_specs = [
        col(0), col(1), col(2), col(3), col(0), col(1),
        pl.BlockSpec((None, TILE, TILE), lambda j, s: (j, 0, 0)),
        pl.BlockSpec((None, 1, TILE), lambda j, s: (j, 0, 0)),
        pl.BlockSpec((3, TILE), lambda j, s: (0, j)),
        pl.BlockSpec((1, TILE), lambda j, s: (0, j)),
    ]
    piece = pl.BlockSpec((t_len, TILE), lambda j, s: (s, j))
    out_specs = [
        piece, piece, piece, piece,
        pl.BlockSpec((None, TILE, TILE), lambda j, s: (j, 0, 0)),
        pl.BlockSpec((None, 1, TILE), lambda j, s: (j, 0, 0)),
        pl.BlockSpec((3, TILE), lambda j, s: (0, j)),
        pl.BlockSpec((1, TILE), lambda j, s: (0, j)),
    ]
    w = ng * TILE
    out_shape = [jax.ShapeDtypeStruct((m, w), BF16)] * 4 + [
        jax.ShapeDtypeStruct((ng, TILE, TILE), F32), jax.ShapeDtypeStruct((ng, 1, TILE), F32),
        jax.ShapeDtypeStruct((3, w), F32), jax.ShapeDtypeStruct((1, w), F32)]
    return pl.pallas_call(
        body, name="ab_mixer_bwd", grid=(ng, nseq), in_specs=in_specs, out_specs=out_specs,
        out_shape=out_shape, compiler_params=_params(("parallel", "arbitrary")),
    )(p, p, p, p, dmix, dmix, pool_w, pool_scale, conv_w, conv_b)


SGU_ROWS = 512
INV_SQRT2 = 1.0 / math.sqrt(2.0)
INV_SQRT_2PI = 1.0 / math.sqrt(2.0 * math.pi)


def _gelu(x):
    return 0.5 * x * (1.0 + lax.erf(x * INV_SQRT2))


def _gelu_grad(x):
    return 0.5 * (1.0 + lax.erf(x * INV_SQRT2)) + x * (INV_SQRT_2PI * jnp.exp(-0.5 * x * x))


def _causal_tile(transposed=False):
    r = lax.broadcasted_iota(jnp.int32, (TILE, TILE), 0)
    c = lax.broadcasted_iota(jnp.int32, (TILE, TILE), 1)
    return r <= c if transposed else c <= r


def _sgu_norm(v, g_ref, b_ref):
    mu = jnp.mean(v, axis=-1, keepdims=True)
    xc = v - mu
    rstd = lax.rsqrt(jnp.mean(xc * xc, axis=-1, keepdims=True) + EPS)
    xhat = xc * rstd
    return xhat, rstd, xhat * g_ref[...] + b_ref[...]


def _sgu_fwd(p, norm_g, norm_b, w_s, bias_tile):
    m = p.shape[0]
    ng = 4
    width = ng * TILE

    def body(u_ref, v_ref, g_ref, b_ref, w_ref, bias_ref, o_ref):
        u = _gelu(u_ref[...].astype(F32))
        _, _, vln = _sgu_norm(_gelu(v_ref[...].astype(F32)), g_ref, b_ref)
        vln = vln.astype(BF16)
        causal = _causal_tile()
        for g in range(ng):
            cols = slice(g * TILE, (g + 1) * TILE)
            wg = jnp.where(causal, w_ref[g], 0.0).astype(BF16)
            for n in range(SGU_ROWS // TILE):
                rows = slice(n * TILE, (n + 1) * TILE)
                s = jnp.dot(wg, vln[rows, cols], preferred_element_type=F32) + bias_ref[g]
                o_ref[rows, cols] = (u[rows, cols] * s).astype(BF16)

    vec = pl.BlockSpec((1, width), lambda i: (0, 0))
    tiles = pl.BlockSpec((ng, TILE, TILE), lambda i: (0, 0, 0))
    return pl.pallas_call(
        body, name="sgu_fwd", grid=(m // SGU_ROWS,),
        in_specs=[pl.BlockSpec((SGU_ROWS, width), lambda i: (i, 0)),
                  pl.BlockSpec((SGU_ROWS, width), lambda i: (i, 1)), vec, vec, tiles, tiles],
        out_specs=pl.BlockSpec((SGU_ROWS, width), lambda i: (i, 0)),
        out_shape=jax.ShapeDtypeStruct((m, width), BF16),
        compiler_params=_params(("parallel",)),
    )(p, p, norm_g, norm_b, w_s, bias_tile)


def _sgu_bwd(p, dmix, norm_g, norm_b, w_s, w_s_t, bias_tile):
    m = p.shape[0]
    ng = 4
    width = ng * TILE

    def body(u_ref, v_ref, dc_ref, g_ref, b_ref, w_ref, wt_ref, bias_ref,
             du_ref, dv_ref, dw_ref, dbs_ref, dg_ref, db_ref, ds_scr, dvln_scr):
        u_pre = u_ref[...].astype(F32)
        v_pre = v_ref[...].astype(F32)
        u = _gelu(u_pre)
        xhat, rstd, vln = _sgu_norm(_gelu(v_pre), g_ref, b_ref)
        vln = vln.astype(BF16)
        dc = dc_ref[...].astype(F32)
        causal = _causal_tile()
        ones = jnp.ones((TILE, TILE), BF16)
        first = pl.program_id(0) == 0
        for g in range(ng):
            cols = slice(g * TILE, (g + 1) * TILE)
            wg = jnp.where(causal, w_ref[g], 0.0).astype(BF16)
            wgt = jnp.where(_causal_tile(transposed=True), wt_ref[g], 0.0).astype(BF16)
            dw_acc = jnp.zeros((TILE, TILE), F32)
            dbs_acc = jnp.zeros((TILE, TILE), F32)
            for n in range(SGU_ROWS // TILE):
                rows = slice(n * TILE, (n + 1) * TILE)
                vt = vln[rows, cols]
                s = jnp.dot(wg, vt, preferred_element_type=F32) + bias_ref[g]
                ds_scr[rows, cols] = dc[rows, cols] * s
                ds = (dc[rows, cols] * u[rows, cols]).astype(BF16)
                dw_acc += lax.dot_general(ds, vt, NT_DIMS, preferred_element_type=F32)
                dbs_acc += jnp.dot(ds, ones, preferred_element_type=F32)
                dvln_scr[rows, cols] = jnp.dot(wgt, ds, preferred_element_type=F32)
            dw_g = jnp.where(causal, dw_acc, 0.0)

            @pl.when(first)
            def _():
                dw_ref[g] = dw_g
                dbs_ref[g] = dbs_acc

            @pl.when(jnp.logical_not(first))
            def _():
                dw_ref[g] += dw_g
                dbs_ref[g] += dbs_acc

        du_ref[...] = (ds_scr[...] * _gelu_grad(u_pre)).astype(BF16)
        dvln = dvln_scr[...]
        dxhat = dvln * g_ref[...]
        dv = rstd * (dxhat - jnp.mean(dxhat, axis=-1, keepdims=True)
                     - xhat * jnp.mean(dxhat * xhat, axis=-1, keepdims=True))
        dv_ref[...] = (dv * _gelu_grad(v_pre)).astype(BF16)
        dg_part = jnp.sum(dvln * xhat, axis=0, keepdims=True)
        db_part = jnp.sum(dvln, axis=0, keepdims=True)

        @pl.when(first)
        def _():
            dg_ref[...] = dg_part
            db_ref[...] = db_part

        @pl.when(jnp.logical_not(first))
        def _():
            dg_ref[...] += dg_part
            db_ref[...] += db_part

    vec = pl.BlockSpec((1, width), lambda i: (0, 0))
    tiles = pl.BlockSpec((ng, TILE, TILE), lambda i: (0, 0, 0))
    rows0 = pl.BlockSpec((SGU_ROWS, width), lambda i: (i, 0))
    rows1 = pl.BlockSpec((SGU_ROWS, width), lambda i: (i, 1))
    return pl.pallas_call(
        body, name="sgu_bwd", grid=(m // SGU_ROWS,),
        in_specs=[rows0, rows1, rows0, vec, vec, tiles, tiles, tiles],
        out_specs=[rows0, rows0, tiles, tiles, vec, vec],
        out_shape=[jax.ShapeDtypeStruct((m, width), BF16), jax.ShapeDtypeStruct((m, width), BF16),
                   jax.ShapeDtypeStruct((ng, TILE, TILE), F32), jax.ShapeDtypeStruct((ng, TILE, TILE), F32),
                   jax.ShapeDtypeStruct((1, width), F32), jax.ShapeDtypeStruct((1, width), F32)],
        scratch_shapes=[pltpu.VMEM((SGU_ROWS, width), F32), pltpu.VMEM((SGU_ROWS, width), F32)],
        compiler_params=_params(("arbitrary",)),
    )(p, p, dmix, norm_g, norm_b, w_s, w_s_t, bias_tile)


SB_DH = 64
SB_SCALE = 1.0 / math.sqrt(SB_DH)


SB_BLOCK = 256
SB_SUB = SB_BLOCK // TILE


def _sum_matrix(kind):
    j = lax.broadcasted_iota(jnp.int32, (TILE, 2 * TILE), 0)
    s = lax.broadcasted_iota(jnp.int32, (TILE, 2 * TILE), 1)
    tri = {"after": j > s, "upto": j <= s, "before": j < s}[kind]
    return jnp.where(jnp.logical_or(s >= TILE, tri), 1.0, 0.0).astype(BF16)


def _strict_mask():
    r = lax.broadcasted_iota(jnp.int32, (SB_BLOCK, SB_BLOCK), 0)
    c = lax.broadcasted_iota(jnp.int32, (SB_BLOCK, SB_BLOCK), 1)
    return c < r


def _head_lanes(h):
    lane = lax.broadcasted_iota(jnp.int32, (1, TILE), 1)
    return (lane >= h * SB_DH) & (lane < (h + 1) * SB_DH)


def _softplus(z):
    return jnp.maximum(z, 0.0) + jnp.log(1.0 + jnp.exp(-jnp.abs(z)))


def _sb_fwd(p, nseq, t_len, gather):
    m = p.shape[0]
    npair = 4
    ng = len(gather)
    last_step = nseq * npair - 1

    def body(q_ref, k_ref, v_ref, *rest):
        o_ref, lt_ref = rest[ng:ng + 2]
        kh_ref, vh_ref = rest[2 * ng + 2:2 * ng + 4]
        step = pl.program_id(0) * npair + pl.program_id(1)
        send, forward, finish = _gather_steps(rest[ng + 2:2 * ng + 2], *rest[2 * ng + 4:])
        pl.when(step == 0)(send)
        pl.when(step == (last_step + 1) // 2)(forward)
        for h in range(2):
            keep = _head_lanes(h)
            kh_ref[h] = jnp.where(keep, k_ref[...], 0).astype(BF16)
            vh_ref[h] = jnp.where(keep, v_ref[...], 0).astype(BF16)
        summat = _sum_matrix("after")
        strict = _strict_mask()

        def one_pass(q, row0, diag, state):
            rows = pl.ds(row0, SB_BLOCK)
            z, sp, pieces = [], [], []
            for h in range(2):
                zh = lax.dot_general(q, kh_ref[h, rows, :], NT_DIMS, preferred_element_type=F32) * SB_SCALE
                sph = _softplus(zh)
                logkeep = jnp.where(strict, -sph, 0.0) if diag else -sph
                z.append(zh)
                sp.append(sph)
                pieces += [logkeep[:, b * TILE:(b + 1) * TILE] for b in range(SB_SUB)]
            sums = jnp.dot(jnp.concatenate(pieces, axis=0).astype(BF16), summat, preferred_element_type=F32)
            out = []
            for h in range(2):
                carry, acc = state[2 * h], state[2 * h + 1]
                after = [None] * SB_SUB
                for b in reversed(range(SB_SUB)):
                    part = sums[(h * SB_SUB + b) * SB_BLOCK:(h * SB_SUB + b + 1) * SB_BLOCK]
                    after[b] = part[:, :TILE] + carry
                    carry = carry + part[:, TILE:]
                w = jnp.exp(z[h] - sp[h] + jnp.concatenate(after, axis=1))
                if diag:
                    w = jnp.where(strict, w, 0.0)
                out += [carry, acc + jnp.dot(w.astype(BF16), vh_ref[h, rows, :], preferred_element_type=F32)]
            return tuple(out)

        def q_block(i, _):
            r0 = pl.multiple_of(i * SB_BLOCK, SB_BLOCK)
            q = q_ref[pl.ds(r0, SB_BLOCK), :]
            zero = jnp.zeros((SB_BLOCK, TILE), F32)
            state = one_pass(q, r0, True, (zero,) * 4)
            state = lax.fori_loop(
                0, i, lambda jj, st: one_pass(q, pl.multiple_of((i - 1 - jj) * SB_BLOCK, SB_BLOCK), False, st), state)
            o_ref[pl.ds(r0, SB_BLOCK), :] = (state[1] + state[3]).astype(BF16)
            lt_ref[pl.ds(r0, SB_BLOCK), :] = jnp.where(_head_lanes(0), state[0], state[2])
            return 0

        lax.fori_loop(0, t_len // SB_BLOCK, q_block, 0)
        pl.when(step == last_step)(finish)

    def col(k):
        return pl.BlockSpec((t_len, TILE), lambda s, hp: (s, k * npair + hp))

    out = pl.BlockSpec((t_len, TILE), lambda s, hp: (s, hp))
    res = pl.pallas_call(
        body, name="stickbreak_fwd", grid=(nseq, npair), in_specs=[col(2), col(3), col(4)] + [ANY] * ng,
        out_specs=[out, out] + [ANY] * ng,
        out_shape=[jax.ShapeDtypeStruct((m, npair * TILE), BF16), jax.ShapeDtypeStruct((m, npair * TILE), F32)]
        + [jax.ShapeDtypeStruct(b.shape, b.dtype) for b in gather],
        input_output_aliases={3 + a: 2 + a for a in range(ng)},
        scratch_shapes=[pltpu.VMEM((2, t_len, TILE), BF16), pltpu.VMEM((2, t_len, TILE), BF16)] + _gather_sems(ng),
        compiler_params=pltpu.CompilerParams(dimension_semantics=("arbitrary", "arbitrary"),
                                             vmem_limit_bytes=VMEM_LIMIT_BYTES, has_side_effects=True),
    )(p, p, p, *gather)
    return res[0], res[1], res[2:]


def _sb_bwd(p, dmix, ltot, nseq, t_len, exchange):
    m = p.shape[0]
    npair = 4
    ne = len(exchange)
    last_step = nseq * npair - 1

    def body(q_ref, k_ref, v_ref, do_ref, lt_ref, *rest):
        dq_ref, dk_ref, dv_ref = rest[ne:ne + 3]
        kh_ref, vh_ref, dk_acc, dv_acc = rest[2 * ne + 3:2 * ne + 7]
        step = pl.program_id(0) * npair + pl.program_id(1)
        send, finish = _exchange_steps(rest[:ne], rest[ne + 3:2 * ne + 3], *rest[2 * ne + 7:])
        pl.when(step == 0)(send)
        for h in range(2):
            keep = _head_lanes(h)
            kh_ref[h] = jnp.where(keep, k_ref[...], 0).astype(BF16)
            vh_ref[h] = jnp.where(keep, v_ref[...], 0).astype(BF16)
        dk_acc[...] = jnp.zeros_like(dk_acc)
        dv_acc[...] = jnp.zeros_like(dv_acc)
        sum_upto = _sum_matrix("upto")
        sum_before = _sum_matrix("before")
        strict = _strict_mask()
        lane = lax.broadcasted_iota(jnp.int32, (SB_BLOCK, TILE), 1)

        def running(x, matrix, start):
            pieces = [x[h][:, b * TILE:(b + 1) * TILE] for h in range(2) for b in range(SB_SUB)]
            sums = jnp.dot(jnp.concatenate(pieces, axis=0).astype(BF16), matrix, preferred_element_type=F32)
            wide, ends = [], []
            for h in range(2):
                total, cols = start[h], []
                for b in range(SB_SUB):
                    part = sums[(h * SB_SUB + b) * SB_BLOCK:(h * SB_SUB + b + 1) * SB_BLOCK]
                    cols.append(part[:, :TILE] + total)
                    total = total + part[:, TILE:]
                wide.append(jnp.concatenate(cols, axis=1))
                ends.append(total)
            return wide, ends

        def one_pass(q, do, qh, doh, ltot, row0, diag, state):
            rows = pl.ds(row0, SB_BLOCK)
            z, sp, logkeep = [], [], []
            for h in range(2):
                zh = lax.dot_general(q, kh_ref[h, rows, :], NT_DIMS, preferred_element_type=F32) * SB_SCALE
                sph = _softplus(zh)
                z.append(zh)
                sp.append(sph)
                logkeep.append(jnp.where(strict, -sph, 0.0) if diag else -sph)
            upto, sum_l = running(logkeep, sum_upto, [state[0], state[3]])
            w, g = [], []
            for h in range(2):
                wh = jnp.exp(z[h] - sp[h] + (ltot[h] - upto[h]))
                if diag:
                    wh = jnp.where(strict, wh, 0.0)
                w.append(wh)
                g.append(wh * lax.dot_general(do, vh_ref[h, rows, :], NT_DIMS, preferred_element_type=F32))
            g_before, sum_g = running(g, sum_before, [state[1], state[4]])
            out, dk_new, dv_new = [], 0.0, 0.0
            for h in range(2):
                dz = (g[h] - jnp.exp(z[h] - sp[h]) * (g[h] + g_before[h])) * SB_SCALE
                if diag:
                    dz = jnp.where(strict, dz, 0.0)
                dzb = dz.astype(BF16)
                dq = state[3 * h + 2] + jnp.dot(dzb, kh_ref[h, rows, :], preferred_element_type=F32)
                dk_new = dk_new + lax.dot_general(dzb, qh[h], TN_DIMS, preferred_element_type=F32)
                dv_new = dv_new + lax.dot_general(w[h].astype(BF16), doh[h], TN_DIMS, preferred_element_type=F32)
                out += [sum_l[h], sum_g[h], dq]
            dk_acc[rows, :] += dk_new
            dv_acc[rows, :] += dv_new
            return tuple(out)

        def q_block(i, _):
            r0 = pl.multiple_of(i * SB_BLOCK, SB_BLOCK)
            q = q_ref[pl.ds(r0, SB_BLOCK), :]
            do = do_ref[pl.ds(r0, SB_BLOCK), :]
            lt = lt_ref[pl.ds(r0, SB_BLOCK), :]
            qh, doh, ltot = [], [], []
            for h in range(2):
                keep = _head_lanes(h)
                qh.append(jnp.where(keep, q, 0).astype(BF16))
                doh.append(jnp.where(keep, do, 0).astype(BF16))
                ltot.append(jnp.sum(jnp.where(lane == h * SB_DH, lt, 0.0), axis=1, keepdims=True))
            zero = jnp.zeros((SB_BLOCK, TILE), F32)
            state = lax.fori_loop(
                0, i,
                lambda jj, st: one_pass(q, do, qh, doh, ltot, pl.multiple_of(jj * SB_BLOCK, SB_BLOCK), False, st),
                (zero,) * 6)
            state = one_pass(q, do, qh, doh, ltot, r0, True, state)
            dq_ref[pl.ds(r0, SB_BLOCK), :] = (state[2] + state[5]).astype(BF16)
            return 0

        lax.fori_loop(0, t_len // SB_BLOCK, q_block, 0)
        dk_ref[...] = dk_acc[...].astype(BF16)
        dv_ref[...] = dv_acc[...].astype(BF16)
        pl.when(step == last_step)(finish)

    def col(k):
        return pl.BlockSpec((t_len, TILE), lambda s, hp: (s, k * npair + hp))

    out = pl.BlockSpec((t_len, TILE), lambda s, hp: (s, hp))
    width = npair * TILE
    res = pl.pallas_call(
        body, name="stickbreak_bwd", grid=(nseq, npair),
        in_specs=[col(2), col(3), col(4), col(1), out] + [ANY] * ne, out_specs=[out, out, out] + [ANY] * ne,
        out_shape=[jax.ShapeDtypeStruct((m, width), BF16)] * 3 + _exchange_shapes(exchange),
        scratch_shapes=[pltpu.VMEM((2, t_len, TILE), BF16), pltpu.VMEM((2, t_len, TILE), BF16),
                        pltpu.VMEM((t_len, TILE), F32), pltpu.VMEM((t_len, TILE), F32)] + _exchange_sems(ne),
        compiler_params=pltpu.CompilerParams(dimension_semantics=("arbitrary", "arbitrary"),
                                             vmem_limit_bytes=VMEM_LIMIT_BYTES, has_side_effects=True),
    )(p, p, p, dmix, ltot, *exchange)
    return res[0], res[1], res[2], res[3:]


def _adam_math(w, g, m, v):
    m = ADAM_B1 * m + (1.0 - ADAM_B1) * g
    v = ADAM_B2 * v + (1.0 - ADAM_B2) * (g * g)
    m_hat = m / (1.0 - ADAM_B1 ** ADAM_STEP)
    v_hat = v / (1.0 - ADAM_B2 ** ADAM_STEP)
    delta = -ADAM_LR * (m_hat / (jnp.sqrt(v_hat) + ADAM_EPS) + ADAM_WD * w)
    return delta, m, v


def _cast_place(w, layer, pos, *, name):
    _, r, c = w.shape
    tr = min(r, 256)

    def body(pos_ref, w_ref, o_ref):
        o_ref[...] = w_ref[...].astype(BF16)

    grid_spec = pltpu.PrefetchScalarGridSpec(
        num_scalar_prefetch=1, grid=(r // tr,),
        in_specs=[pl.BlockSpec((None, tr, c), lambda i, pos_ref: (layer, i, 0))],
        out_specs=pl.BlockSpec((None, None, tr, c), lambda i, pos_ref: (0, pos_ref[0], i, 0)))
    return pl.pallas_call(
        body, name=name, grid_spec=grid_spec, out_shape=jax.ShapeDtypeStruct((1, N_CHIP, r, c), BF16),
        compiler_params=_params(("parallel",)),
    )(pos, w)


def _pair_sum(mine, got, pos, *, name):
    l_dim, s_dim, h, c = got.shape
    th = min(h, 512)
    nt = h // th

    def body(pos_ref, a_ref, b_ref, o_ref):
        o_ref[...] = (a_ref[...].astype(F32) + b_ref[...].astype(F32)).astype(BF16)

    spec = pl.BlockSpec((None, None, th, c), lambda l, s, i, pos_ref: (l, s, i, 0))
    grid_spec = pltpu.PrefetchScalarGridSpec(
        num_scalar_prefetch=1, grid=(l_dim, s_dim, nt),
        in_specs=[pl.BlockSpec((None, None, th, c), lambda l, s, i, pos_ref: (l, s, pos_ref[1] * nt + i, 0)), spec],
        out_specs=spec)
    return pl.pallas_call(
        body, name=name, grid_spec=grid_spec, out_shape=jax.ShapeDtypeStruct(got.shape, BF16),
        compiler_params=_params(("parallel",) * 3),
    )(pos, mine, got)


def _chip_sum(sums, landed, pos, *, name):
    l_dim, _, h, c = sums.shape
    th = min(h, 512)
    nt = h // th

    def body(pos_ref, own, r0, r1, r2, o_ref):
        o_ref[...] = ((own[...].astype(F32) + r0[...].astype(F32)) + r1[...].astype(F32)) + r2[...].astype(F32)

    def piece(k):
        return pl.BlockSpec((None, None, th, c), lambda l, i, pos_ref: (l, k, i, 0))

    grid_spec = pltpu.PrefetchScalarGridSpec(
        num_scalar_prefetch=1, grid=(l_dim, nt),
        in_specs=[pl.BlockSpec((None, None, th, c), lambda l, i, pos_ref: (l, pos_ref[0], i, 0)),
                  piece(0), piece(1), piece(2)],
        out_specs=pl.BlockSpec((None, th, c), lambda l, i, pos_ref: (l, pos_ref[1] * nt + i, 0)))
    return pl.pallas_call(
        body, name=name, grid_spec=grid_spec, out_shape=jax.ShapeDtypeStruct((l_dim, 2 * h, c), F32),
        compiler_params=_params(("parallel",) * 2),
    )(pos, sums, landed, landed, landed)


def _adam_big(w, m, v, grads, *, name):
    l_dim, r, c = w.shape
    assert len(grads) == l_dim
    tr = min(r, 256)

    def body(*refs):
        w_ref, m_ref, v_ref = refs[:3]
        g_refs = refs[3:3 + l_dim]
        go_ref, d_ref, mo_ref, vo_ref = refs[3 + l_dim:]
        g = g_refs[0][...]
        for l in range(1, l_dim):
            g = jnp.where(pl.program_id(0) == l, g_refs[l][...], g)
        delta, m_new, v_new = _adam_math(w_ref[...], g, m_ref[...], v_ref[...])
        go_ref[...] = g
        d_ref[...] = delta
        mo_ref[...] = m_new
        vo_ref[...] = v_new

    spec = pl.BlockSpec((None, tr, c), lambda l, i: (l, i, 0))
    gspec = pl.BlockSpec((None, tr, c), lambda l, i: (0, i, 0))
    return pl.pallas_call(
        body, name=name, grid=(l_dim, r // tr), in_specs=[spec] * 3 + [gspec] * l_dim, out_specs=[spec] * 4,
        out_shape=[jax.ShapeDtypeStruct(w.shape, F32)] * 4, compiler_params=_params(("parallel",) * 2),
    )(w, m, v, *grads)


def _position():
    return lax.axis_index("x"), lax.axis_index("y"), lax.axis_index("c")


def _other_chips(x, y):
    return [(1 - x, y), (x, 1 - y), (1 - x, 1 - y)]


def _remote(src, dst, send_sem, recv_sem, device):
    return pltpu.make_async_remote_copy(src_ref=src, dst_ref=dst, send_sem=send_sem, recv_sem=recv_sem,
                                        device_id=device, device_id_type=MESH)


ANY = pl.BlockSpec(memory_space=pl.ANY)


def _gather_weights(bufs):
    n = len(bufs)

    def body(*refs):
        send, forward, finish = _gather_steps(refs[n:2 * n], *refs[2 * n:])
        send()
        forward()
        finish()

    return pl.pallas_call(
        body, name="gather_weights", in_specs=[ANY] * n, out_specs=[ANY] * n,
        out_shape=[jax.ShapeDtypeStruct(b.shape, b.dtype) for b in bufs],
        input_output_aliases={a: a for a in range(n)},
        scratch_shapes=_gather_sems(n),
        compiler_params=pltpu.CompilerParams(has_side_effects=True),
    )(*bufs)


def _gather_sems(n):
    return [pltpu.SemaphoreType.DMA((3 * n,))] * 4


def _gather_steps(outs, send_sems, recv_sems, fwd_send, fwd_recv):
    n = len(outs)
    x, y, c = _position()
    chips = _other_chips(x, y)
    sibling = (x, y, 1 - c)

    def half(a, chip, core):
        h = outs[a].shape[2] // 2
        return outs[a].at[:, 2 * chip[0] + chip[1], pl.ds(core * h, h), :]

    def over_ici(a, k, chip):
        block = half(a, chip, c)
        return _remote(block, block, send_sems.at[3 * a + k], recv_sems.at[3 * a + k], (*chips[k], c))

    def over_d2d(a, k, core):
        block = half(a, chips[k], core)
        return _remote(block, block, fwd_send.at[3 * a + k], fwd_recv.at[3 * a + k], sibling)

    def send():
        for a in range(n):
            for k in range(3):
                over_ici(a, k, (x, y)).start()

    def forward():
        for k in range(3):
            for a in range(n):
                over_ici(a, k, chips[k]).wait_recv()
                over_d2d(a, k, c).start()

    def finish():
        for k in range(3):
            for a in range(n):
                over_d2d(a, k, 1 - c).wait_recv()
        for a in range(n):
            for k in range(3):
                over_ici(a, k, (x, y)).wait_send()
                over_d2d(a, k, c).wait_send()

    return send, forward, finish


def _swap_halves(grads, *, name):
    n = len(grads)

    def body(*refs):
        ins, got = refs[:n], refs[n:2 * n]
        send_sems, recv_sems = refs[2 * n:]
        x, y, c = _position()
        sibling = (x, y, 1 - c)
        copies = []
        for a in range(n):
            h = grads[a].shape[2] // 2
            cp = _remote(ins[a].at[:, :, pl.ds((1 - c) * h, h), :], got[a], send_sems.at[a], recv_sems.at[a], sibling)
            cp.start()
            copies.append(cp)
        for cp in copies:
            cp.wait()

    sem = pltpu.SemaphoreType.DMA((n,))
    return pl.pallas_call(
        body, name=name, in_specs=[ANY] * n, out_specs=[ANY] * n,
        out_shape=[jax.ShapeDtypeStruct(g.shape[:2] + (g.shape[2] // 2, g.shape[3]), g.dtype) for g in grads],
        scratch_shapes=[sem, sem], compiler_params=pltpu.CompilerParams(has_side_effects=True),
    )(*grads)


def _exchange_chips(sums):
    n = len(sums)

    def body(*refs):
        send, finish = _exchange_steps(refs[:n], refs[n:2 * n], *refs[2 * n:])
        send()
        finish()

    return pl.pallas_call(
        body, name="exchange_chips", in_specs=[ANY] * n, out_specs=[ANY] * n,
        out_shape=_exchange_shapes(sums), scratch_shapes=_exchange_sems(n),
        compiler_params=pltpu.CompilerParams(has_side_effects=True),
    )(*sums)


def _exchange_shapes(sums):
    return [jax.ShapeDtypeStruct((s.shape[0], 3) + s.shape[2:], s.dtype) for s in sums]


def _exchange_sems(n):
    return [pltpu.SemaphoreType.DMA((3 * n,))] * 2


def _exchange_steps(ins, outs, send_sems, recv_sems):
    n = len(ins)
    x, y, c = _position()
    chips = _other_chips(x, y)

    def copy(a, k):
        chip = chips[k]
        return _remote(ins[a].at[:, 2 * chip[0] + chip[1]], outs[a].at[:, k],
                       send_sems.at[3 * a + k], recv_sems.at[3 * a + k], (*chip, c))

    def send():
        for a in range(n):
            for k in range(3):
                copy(a, k).start()

    def finish():
        for a in range(n):
            for k in range(3):
                copy(a, k).wait()

    return send, finish


def _join_halves(bufs, *, name):
    n = len(bufs)

    def body(*refs):
        outs = refs[n:2 * n]
        send_sems, recv_sems = refs[2 * n:]
        x, y, c = _position()
        sibling = (x, y, 1 - c)
        copies = []
        for a in range(n):
            h = bufs[a].shape[1] // 2
            mine = outs[a].at[:, pl.ds(c * h, h), :]
            cp = _remote(mine, mine, send_sems.at[a], recv_sems.at[a], sibling)
            cp.start()
            copies.append((cp, a, h))
        for cp, a, h in copies:
            cp.wait_send()
            got = outs[a].at[:, pl.ds((1 - c) * h, h), :]
            _remote(got, got, send_sems.at[a], recv_sems.at[a], sibling).wait_recv()

    sem = pltpu.SemaphoreType.DMA((n,))
    return pl.pallas_call(
        body, name=name, in_specs=[ANY] * n, out_specs=[ANY] * n,
        out_shape=[jax.ShapeDtypeStruct(b.shape, b.dtype) for b in bufs],
        input_output_aliases={a: a for a in range(n)},
        scratch_shapes=[sem, sem], compiler_params=pltpu.CompilerParams(has_side_effects=True),
    )(*bufs)


def _allreduce_small(packs):
    n = len(packs)

    def body(*refs):
        ins, outs, gath = refs[:n], refs[n:2 * n], refs[2 * n:3 * n]
        send_sems, recv_sems = refs[3 * n:]
        x, y, c = _position()
        me, sibling = (x, y, c), (x, y, 1 - c)
        chips = _other_chips(x, y)

        def slot(a, dev):
            return gath[a].at[4 * dev[0] + 2 * dev[1] + dev[2]]

        def copy(a, k, block, to, src=None):
            return _remote(slot(a, block) if src is None else src, slot(a, block),
                           send_sems.at[7 * a + k], recv_sems.at[7 * a + k], to)

        started = []
        for a in range(n):
            slot(a, me)[...] = ins[a][...]
            first = [copy(a, 0, me, sibling, src=ins[a])]
            first += [copy(a, 1 + k, me, (*chip, c), src=ins[a]) for k, chip in enumerate(chips)]
            for cp in first:
                cp.start()
            started += first
        for a in range(n):
            for k, chip in enumerate(chips):
                copy(a, 1 + k, (*chip, c), me).wait_recv()
                cp = copy(a, 4 + k, (*chip, c), sibling)
                cp.start()
                started.append(cp)
        for a in range(n):
            copy(a, 0, sibling, me).wait_recv()
            for k, chip in enumerate(chips):
                copy(a, 4 + k, (*chip, 1 - c), me).wait_recv()
        for cp in started:
            cp.wait_send()
        for a in range(n):
            total = gath[a][0]
            for d in range(1, N_DEV):
                total = total + gath[a][d]
            outs[a][...] = total

    vmem = pl.BlockSpec(memory_space=pltpu.VMEM)
    sem = pltpu.SemaphoreType.DMA((7 * n,))
    return pl.pallas_call(
        body, name="allreduce_small", in_specs=[vmem] * n, out_specs=[vmem] * n,
        out_shape=[jax.ShapeDtypeStruct(p.shape, p.dtype) for p in packs],
        scratch_shapes=[pltpu.VMEM((N_DEV,) + p.shape, p.dtype) for p in packs] + [sem, sem],
        compiler_params=pltpu.CompilerParams(has_side_effects=True, vmem_limit_bytes=VMEM_LIMIT_BYTES),
    )(*packs)


LOSS_ROW = 1040


def _pad_rows(a, rows=8):
    return jnp.concatenate([a, jnp.zeros((rows - a.shape[0], a.shape[1]), a.dtype)], axis=0)

def _adam_small(wide, mid, narrow, params):
    names = ["mix_norm_g", "mlp_norm_g", "final_norm_g", "conv_b", "conv_w", "sgu_norm_g", "sgu_norm_b",
             "pool_w", "pool_scale", "sgu_w", "sgu_b"]
    n = len(names)

    def body(*refs):
        wide_ref, mid_ref, narrow_ref = refs[:3]
        wmv = refs[3:3 + 3 * n]
        outs = refs[3 + 3 * n:]
        x, y, _ = _position()
        q = 2 * x + y

        def my_quarter(rows):
            parts = [rows[:, s * TILE:(s + 1) * TILE] for s in range(N_CHIP)]
            return jnp.where(q == 0, parts[0], jnp.where(q == 1, parts[1], jnp.where(q == 2, parts[2], parts[3])))

        def tiles(first_row):
            return [((0, g), narrow_ref[first_row + g * TILE:first_row + (g + 1) * TILE, :]) for g in range(4)]

        grads = {
            "mix_norm_g": [((), wide_ref[0:2, :])],
            "mlp_norm_g": [((), wide_ref[8:10, :])],
            "final_norm_g": [((), wide_ref[16:17, :])],
            "conv_b": [((), mid_ref[0:1, :])],
            "conv_w": [((0,), my_quarter(mid_ref[8:11, :]))],
            "sgu_norm_g": [((), my_quarter(mid_ref[16:17, :]))],
            "sgu_norm_b": [((), my_quarter(mid_ref[24:25, :]))],
            "pool_w": tiles(0),
            "sgu_w": tiles(512),
            "pool_scale": [((0,), narrow_ref[1024:1028, :])],
            "sgu_b": [((0,), narrow_ref[1032:1036, :])],
        }
        for i, name in enumerate(names):
            w_ref, m_ref, v_ref = wmv[3 * i:3 * i + 3]
            for lead, g in grads[name]:
                idx = lead + (slice(None), slice(None))
                delta, m_new, v_new = _adam_math(w_ref[idx], g, m_ref[idx], v_ref[idx])
                outs[4 * i][idx] = g
                outs[4 * i + 1][idx] = delta
                outs[4 * i + 2][idx] = m_new
                outs[4 * i + 3][idx] = v_new

    vmem = pl.BlockSpec(memory_space=pltpu.VMEM)
    args, out_shape = [wide, mid, narrow], []
    for name in names:
        w, m, v = params[name]
        args += [w, m, v]
        out_shape += [jax.ShapeDtypeStruct(w.shape, F32)] * 4
    res = pl.pallas_call(
        body, name="adam_small", in_specs=[vmem] * len(args), out_specs=[vmem] * len(out_shape),
        out_shape=out_shape, compiler_params=pltpu.CompilerParams(vmem_limit_bytes=VMEM_LIMIT_BYTES),
    )(*args)
    return {name: res[4 * i:4 * i + 4] for i, name in enumerate(names)}


def _pair_sums(grads, pos, tag):
    got = _swap_halves(grads, name=f"swap_halves_{tag}")
    return [_pair_sum(a, b, pos, name=f"pair_sum_{tag}{i}") for i, (a, b) in enumerate(zip(grads, got))]


def _finish_reduce(sums, landed, pos, tag):
    halves = [_chip_sum(s, r, pos, name=f"chip_sum_{tag}{i}") for i, (s, r) in enumerate(zip(sums, landed))]
    return _join_halves(halves, name=f"join_halves_{tag}")


def kernel(x, mix_norm_g, mlp_norm_g, ab_w_in, pool_w, pool_scale, conv_w, conv_b, ab_w_out, cd_w_in, sgu_norm_g, sgu_norm_b, sgu_w, sgu_b, cd_w_out, mlp_w1, mlp_w2, final_norm_g, loss_target, m_mix_norm_g, m_mlp_norm_g, m_ab_w_in, m_pool_w, m_pool_scale, m_conv_w, m_conv_b, m_ab_w_out, m_cd_w_in, m_sgu_norm_g, m_sgu_norm_b, m_sgu_w, m_sgu_b, m_cd_w_out, m_mlp_w1, m_mlp_w2, m_final_norm_g, v_mix_norm_g, v_mlp_norm_g, v_ab_w_in, v_pool_w, v_pool_scale, v_conv_w, v_conv_b, v_ab_w_out, v_cd_w_in, v_sgu_norm_g, v_sgu_norm_b, v_sgu_w, v_sgu_b, v_cd_w_out, v_mlp_w1, v_mlp_w2, v_final_norm_g):
    nseq, t_len, d = x.shape
    m_tok = nseq * t_len
    h0 = x.reshape(m_tok, d)
    target = loss_target.reshape(m_tok, d)

    x_idx, y_idx = lax.axis_index("x"), lax.axis_index("y")
    q_idx = 2 * x_idx + y_idx
    pos = jnp.stack([q_idx, lax.axis_index("c")]).astype(jnp.int32)
    def shard_buffer(w, layer, tag):
        return _cast_place(w, layer, pos, name=f"cast_place_{tag}")

    def row_block(w):
        return w.reshape(1, 1, -1, w.shape[-1])

    w_ab_in, w_ab_out, w_1_0, w_2_0, w_cd_in = _gather_weights(
        [shard_buffer(ab_w_in, 0, "ab_in"), shard_buffer(ab_w_out, 0, "ab_out"), shard_buffer(mlp_w1, 0, "w1_0"),
         shard_buffer(mlp_w2, 0, "w2_0"), shard_buffer(cd_w_in, 0, "cd_in")])
    w_ab_out, w_2_0 = row_block(w_ab_out), row_block(w_2_0)
    later_weights = [shard_buffer(cd_w_out, 0, "cd_out"), shard_buffer(mlp_w1, 1, "w1_1"),
                     shard_buffer(mlp_w2, 1, "w2_1")]

    pool_w3, pool_scale3 = pool_w[0], pool_scale[0].reshape(4, 1, TILE)
    sgu_w3 = sgu_w[0]
    sgu_w3_t = jnp.swapaxes(sgu_w3, 1, 2)
    sgu_bias_tile = jnp.broadcast_to(sgu_b[0][:, :, None], (4, TILE, TILE))
    conv_w2, conv_b2 = conv_w[0], conv_b
    def place_quarter(v):
        return lax.dynamic_update_slice(jnp.zeros((v.shape[0], 4 * TILE), F32), v, (0, q_idx * TILE))

    sharded_small = jnp.concatenate(
        [place_quarter(conv_w[0]), place_quarter(sgu_norm_g), place_quarter(sgu_norm_b),
         jnp.zeros((3, 4 * TILE), F32)], axis=0)
    sharded_small, = _allreduce_small([sharded_small])
    sharded_small = sharded_small * 0.5
    conv_w_full = sharded_small[0:3]
    sgu_g_full = sharded_small[3:4]
    sgu_b_full = sharded_small[4:5]

    xn0 = _rms_fwd(h0, mix_norm_g[0:1], name="rms_fwd_mix0")
    p_ab = _mm_nn(xn0, w_ab_in, 0, out_dtype=BF16, name="ab_in_proj")
    mix0 = _ab_fwd(p_ab, pool_w3, pool_scale3, conv_w_full, conv_b2, nseq, t_len)
    h1, hn0 = _mm_nn(mix0, w_ab_out, 0, out_dtype=F32, name="ab_out_proj", epilogue="residual", extra=h0,
                     norm_g=mlp_norm_g[0:1])
    act0 = _mm_nn(hn0, w_1_0, 0, out_dtype=BF16, name="mlp0_up", epilogue="relu2")
    h2, xn1 = _mm_nn(act0, w_2_0, 0, out_dtype=F32, name="mlp0_down", epilogue="residual", extra=h1,
                     norm_g=mix_norm_g[1:2])

    p_cd = _mm_nn(xn1, w_cd_in, 0, out_dtype=BF16, name="cd_in_proj")
    c_out = _sgu_fwd(p_cd, sgu_g_full, sgu_b_full, sgu_w3, sgu_bias_tile)
    d_out, ltot, (w_cd_out, w_1_1, w_2_1) = _sb_fwd(p_cd, nseq, t_len, later_weights)
    w_cd_out, w_2_1 = row_block(w_cd_out), row_block(w_2_1)
    mix1 = jnp.concatenate([c_out, d_out], axis=1)
    h3, hn1 = _mm_nn(mix1, w_cd_out, 0, out_dtype=F32, name="cd_out_proj", epilogue="residual", extra=h2,
                     norm_g=mlp_norm_g[1:2])
    act1 = _mm_nn(hn1, w_1_1, 0, out_dtype=BF16, name="mlp1_up", epilogue="relu2")
    h4 = _mm_nn(act1, w_2_1, 0, out_dtype=F32, name="mlp1_down", epilogue="residual", extra=h3)

    dh4, dh4_bf, dg_final, loss_tile = _final_loss(h4, final_norm_g.reshape(1, d), target)

    def mlp_bwd(dh_out, dh_out_bf, h_in, hn, act, w_1, w_2, layer, tag):
        dz = _mm_nt(dh_out_bf, w_2, 0, out_dtype=BF16, name=f"mlp{tag}_down_bwd",
                    epilogue="relu2_bwd", extra=act)
        g_w2 = _mm_tn(act, dh_out_bf, 1, name=f"mlp{tag}_down_wgrad")
        g_w1 = _mm_tn(hn, dz, N_CHIP, name=f"mlp{tag}_up_wgrad")
        dh_in, dh_in_bf, dg = _mm_nt(dz, w_1, 0, out_dtype=F32, name=f"mlp{tag}_up_bwd", epilogue="rms_bwd",
                                     extra=(h_in, mlp_norm_g[layer:layer + 1], dh_out))
        return dh_in, dh_in_bf, dg, g_w1, g_w2

    def as_pieces(g):
        return g.reshape(1, N_CHIP, -1, g.shape[-1]) if g.shape[1] == 1 else g

    dh3, dh3_bf, dg_mlp1, g_w1_1, g_w2_1 = mlp_bwd(dh4, dh4_bf, h3, hn1, act1, w_1_1, w_2_1, 1, "1")

    dmix1 = _mm_nt(dh3_bf, w_cd_out, 0, out_dtype=BF16, name="cd_out_bwd")
    g_cd_out = _mm_tn(mix1, dh3_bf, 1, name="cd_out_wgrad")
    sums_a = _pair_sums([g_w1_1, as_pieces(g_w2_1), as_pieces(g_cd_out)], pos, "a")
    du, dv, dsgu_w, dsgu_bs, dsgu_g, dsgu_b = _sgu_bwd(p_cd, dmix1, sgu_g_full, sgu_b_full, sgu_w3, sgu_w3_t,
                                                      sgu_bias_tile)
    dq, dk, dvv, landed_a = _sb_bwd(p_cd, dmix1, ltot, nseq, t_len, sums_a)
    r_w1_1, r_w2_1, r_cd_out = _finish_reduce(sums_a, landed_a, pos, "a")
    dp_cd = jnp.concatenate([du, dv, dq, dk, dvv], axis=1)
    g_cd_in = _mm_tn(xn1, dp_cd, N_CHIP, name="cd_in_wgrad")
    dh2, dh2_bf, dg_mix1 = _mm_nt(dp_cd, w_cd_in, 0, out_dtype=F32, name="cd_in_bwd", epilogue="rms_bwd",
                                  extra=(h2, mix_norm_g[1:2], dh3))

    dh1, dh1_bf, dg_mlp0, g_w1_0, g_w2_0 = mlp_bwd(dh2, dh2_bf, h1, hn0, act0, w_1_0, w_2_0, 0, "0")

    dmix0 = _mm_nt(dh1_bf, w_ab_out, 0, out_dtype=BF16, name="ab_out_bwd")
    g_ab_out = _mm_tn(mix0, dh1_bf, 1, name="ab_out_wgrad")
    da, dxb, dgb, dgc, dpool_w, dpool_scale, dconv_w, dconv_b = _ab_bwd(
        p_ab, dmix0, pool_w3, pool_scale3, conv_w_full, conv_b2, nseq, t_len)
    dp_ab = jnp.concatenate([da, dxb, dgb, dgc], axis=1)
    g_ab_in = _mm_tn(xn0, dp_ab, N_CHIP, name="ab_in_wgrad")
    grad_x, _, dg_mix0 = _mm_nt(dp_ab, w_ab_in, 0, out_dtype=F32, name="ab_in_bwd", epilogue="rms_bwd",
                                extra=(h0, mix_norm_g[0:1], dh1))

    sums_b = _pair_sums([as_pieces(g) for g in (g_ab_in, g_ab_out, g_cd_in, g_w1_0, g_w2_0)], pos, "b")
    r_ab_in, r_ab_out, r_cd_in, r_w1_0, r_w2_0 = _finish_reduce(
        sums_b, _exchange_chips(sums_b), pos, "b")

    big_out = {
        "ab_w_in": _adam_big(ab_w_in, m_ab_w_in, v_ab_w_in, [r_ab_in], name="adam_ab_w_in"),
        "ab_w_out": _adam_big(ab_w_out, m_ab_w_out, v_ab_w_out, [r_ab_out], name="adam_ab_w_out"),
        "cd_w_in": _adam_big(cd_w_in, m_cd_w_in, v_cd_w_in, [r_cd_in], name="adam_cd_w_in"),
        "cd_w_out": _adam_big(cd_w_out, m_cd_w_out, v_cd_w_out, [r_cd_out], name="adam_cd_w_out"),
        "mlp_w1": _adam_big(mlp_w1, m_mlp_w1, v_mlp_w1, [r_w1_0, r_w1_1], name="adam_mlp_w1"),
        "mlp_w2": _adam_big(mlp_w2, m_mlp_w2, v_mlp_w2, [r_w2_0, r_w2_1], name="adam_mlp_w2"),
    }

    wide = jnp.concatenate([_pad_rows(jnp.concatenate([dg_mix0, dg_mix1], axis=0)),
                            _pad_rows(jnp.concatenate([dg_mlp0, dg_mlp1], axis=0)), _pad_rows(dg_final)], axis=0)
    mid = jnp.concatenate([_pad_rows(dconv_b), _pad_rows(dconv_w), _pad_rows(dsgu_g), _pad_rows(dsgu_b)], axis=0)
    narrow = jnp.concatenate(
        [dpool_w.reshape(4 * TILE, TILE), dsgu_w.reshape(4 * TILE, TILE), _pad_rows(dpool_scale.reshape(4, TILE)),
         _pad_rows(dsgu_bs[:, :, 0]), loss_tile], axis=0)
    wide, mid, narrow = _allreduce_small([wide, mid, narrow])
    small_out = _adam_small(wide, mid, narrow, {
        "mix_norm_g": (mix_norm_g, m_mix_norm_g, v_mix_norm_g),
        "mlp_norm_g": (mlp_norm_g, m_mlp_norm_g, v_mlp_norm_g),
        "final_norm_g": tuple(a.reshape(1, d) for a in (final_norm_g, m_final_norm_g, v_final_norm_g)),
        "conv_b": (conv_b, m_conv_b, v_conv_b),
        "conv_w": (conv_w, m_conv_w, v_conv_w),
        "sgu_norm_g": (sgu_norm_g, m_sgu_norm_g, v_sgu_norm_g),
        "sgu_norm_b": (sgu_norm_b, m_sgu_norm_b, v_sgu_norm_b),
        "pool_w": (pool_w, m_pool_w, v_pool_w),
        "pool_scale": (pool_scale, m_pool_scale, v_pool_scale),
        "sgu_w": (sgu_w, m_sgu_w, v_sgu_w),
        "sgu_b": (sgu_b, m_sgu_b, v_sgu_b),
    })
    small_out["final_norm_g"] = [a.reshape(d) for a in small_out["final_norm_g"]]

    order = ["mix_norm_g", "mlp_norm_g", "ab_w_in", "pool_w", "pool_scale", "conv_w", "conv_b", "ab_w_out",
             "cd_w_in", "sgu_norm_g", "sgu_norm_b", "sgu_w", "sgu_b", "cd_w_out", "mlp_w1", "mlp_w2",
             "final_norm_g"]
    both = {**big_out, **small_out}
    loss = narrow[LOSS_ROW, 0]
    outs = [loss, grad_x.reshape(nseq, t_len, d)]
    for kind in range(4):
        outs += [both[name][kind] for name in order]
    return tuple(outs)
```

```python
import math

import jax
import jax.numpy as jnp
from jax import lax
from jax.experimental import pallas as pl
from jax.experimental.pallas import tpu as pltpu

F32 = jnp.float32
BF16 = jnp.bfloat16
MESH = pl.DeviceIdType.MESH

D_MODEL = 1024
EPS = 1e-6
TILE = 128
N_CHIP = 4
N_DEV = 8
VMEM_LIMIT_BYTES = 56 * 1024 * 1024

ADAM_LR = 0.001
ADAM_B1 = 0.9
ADAM_B2 = 0.999
ADAM_EPS = 1e-08
ADAM_WD = 0.01
ADAM_STEP = 10

NT_DIMS = (((1,), (1,)), ((), ()))
TN_DIMS = (((0,), (0,)), ((), ()))


def _params(sem=None):
    return pltpu.CompilerParams(dimension_semantics=sem, vmem_limit_bytes=VMEM_LIMIT_BYTES)


def _call(body, *, name, grid, in_specs, out_specs, out_shape, scratch_shapes, semantics, args, rider=None):
    if rider is None:
        res = pl.pallas_call(body, name=name, grid=grid, in_specs=in_specs, out_specs=out_specs, out_shape=out_shape,
                             scratch_shapes=scratch_shapes, compiler_params=_params(semantics))(*args)
        return list(res), []
    kind, arrays = rider
    gather = kind == "gather"
    nr, n_in, n_out, n_scr = len(arrays), len(in_specs), len(out_specs), len(scratch_shapes)
    first_out, first_scr = n_in + nr, n_in + nr + n_out + nr
    last_step = math.prod(grid) - 1

    def riding(*refs):
        r_in = refs[n_in:first_out]
        r_out = refs[first_out + n_out:first_scr]
        sems = refs[first_scr + n_scr:]
        step = 0
        for axis, size in enumerate(grid):
            step = step * size + pl.program_id(axis)
        if gather:
            send, forward, finish = _gather_steps(r_out, *sems)
            pl.when(step == 0)(send)
            pl.when(step == (last_step + 1) // 2)(forward)
        else:
            send, finish = _exchange_steps(r_in, r_out, *sems)
            pl.when(step == 0)(send)
        body(*refs[:n_in], *refs[first_out:first_out + n_out], *refs[first_scr:first_scr + n_scr])
        pl.when(step == last_step)(finish)

    res = pl.pallas_call(
        riding, name=name, grid=grid, in_specs=list(in_specs) + [ANY] * nr, out_specs=list(out_specs) + [ANY] * nr,
        out_shape=list(out_shape) + ([jax.ShapeDtypeStruct(a.shape, a.dtype) for a in arrays] if gather
                                     else _exchange_shapes(arrays)),
        scratch_shapes=list(scratch_shapes) + (_gather_sems(nr) if gather else _exchange_sems(nr)),
        input_output_aliases={n_in + a: n_out + a for a in range(nr)} if gather else {},
        compiler_params=pltpu.CompilerParams(dimension_semantics=("arbitrary",) * len(grid),
                                             vmem_limit_bytes=VMEM_LIMIT_BYTES, has_side_effects=True),
    )(*args, *arrays)
    return list(res[:n_out]), list(res[n_out:])


def _mm_nn(a, b4, layer, *, out_dtype, name, epilogue=None, extra=None, norm_g=None, rider=None, tm=1024, tk=1024):
    m, k_dim = a.shape
    _, s_dim, kb, n = b4.shape
    assert kb == k_dim
    tn = min(n, 1024)
    assert m % tm == 0 and k_dim % tk == 0 and n % tn == 0
    nk, npb = k_dim // tk, n // tn
    grid = (m // tm, s_dim * npb, nk)
    n_in = 2 + (extra is not None) + (norm_g is not None)
    n_out = 1 + (norm_g is not None)
    assert norm_g is None or tn == s_dim * n

    def body(*refs):
        a_ref, b_ref = refs[:2]
        e_ref = refs[2] if extra is not None else None
        g_ref = refs[n_in - 1] if norm_g is not None else None
        o_ref = refs[n_in]
        scr = refs[n_in + n_out:]

        def finish(acc):
            if epilogue == "relu2":
                r = jnp.maximum(acc, 0.0)
                acc = r * r
            elif epilogue == "residual":
                acc = acc + e_ref[...]
            o_ref[...] = acc.astype(out_dtype)
            if norm_g is not None:
                rstd = lax.rsqrt(jnp.mean(acc * acc, axis=-1, keepdims=True) + EPS)
                refs[n_in + 1][...] = (acc * rstd * g_ref[...]).astype(BF16)

        part = jnp.dot(a_ref[...], b_ref[...], preferred_element_type=F32)
        if nk == 1:
            finish(part)
        else:
            acc_ref, = scr
            kk = pl.program_id(2)

            @pl.when(kk == 0)
            def _():
                acc_ref[...] = part

            @pl.when(kk > 0)
            def _():
                acc_ref[...] += part

            @pl.when(kk == nk - 1)
            def _():
                finish(acc_ref[...])

    in_specs = [
        pl.BlockSpec((tm, tk), lambda i, j, kk: (i, kk)),
        pl.BlockSpec((None, None, tk, tn), lambda i, j, kk: (layer, j // npb, kk, j % npb)),
    ]
    args = [a, b4]
    if extra is not None:
        in_specs.append(pl.BlockSpec((tm, tn), lambda i, j, kk: (i, j)))
        args.append(extra)
    out_block = pl.BlockSpec((tm, tn), lambda i, j, kk: (i, j))
    out_specs, out_shape = [out_block], [jax.ShapeDtypeStruct((m, s_dim * n), out_dtype)]
    if norm_g is not None:
        in_specs.append(pl.BlockSpec((1, tn), lambda i, j, kk: (0, j)))
        args.append(norm_g)
        out_specs.append(out_block)
        out_shape.append(jax.ShapeDtypeStruct((m, s_dim * n), BF16))
    res, rode = _call(
        body, name=name, grid=grid, in_specs=in_specs, out_specs=out_specs, out_shape=out_shape,
        scratch_shapes=[] if nk == 1 else [pltpu.VMEM((tm, tn), F32)],
        semantics=("parallel", "parallel", "arbitrary"), args=args, rider=rider)
    res = res[0] if norm_g is None else res
    return res if rider is None else (res, rode)


def _mm_nt(a, b4, layer, *, out_dtype, name, epilogue=None, extra=None, rider=None, tm=1024, tn=1024):
    m, k_dim = a.shape
    _, s_dim, n_out, n = b4.shape
    assert k_dim == s_dim * n
    tk = min(n, 1024)
    tn = min(tn, n_out)
    assert m % tm == 0 and n_out % tn == 0 and n % tk == 0
    kpb = n // tk
    nk = s_dim * kpb
    grid = (m // tm, n_out // tn, nk)
    rms = epilogue == "rms_bwd"
    assert not rms or tn == n_out
    extras = [] if extra is None else (list(extra) if rms else [extra])
    n_in = 2 + len(extras)
    n_res = 3 if rms else 1

    def body(*refs):
        a_ref, b_ref = refs[:2]
        e_refs = refs[2:n_in]
        o_ref = refs[n_in]
        scr = refs[n_in + n_res:]

        def finish(acc):
            if epilogue == "relu2_bwd":
                acc = acc * (2.0 * jnp.sqrt(e_refs[0][...].astype(F32)))
            if not rms:
                o_ref[...] = acc.astype(out_dtype)
                return
            h_ref, g_ref, dres_ref = e_refs
            dhb_ref, dg_ref = refs[n_in + 1:n_in + 3]
            hv = h_ref[...]
            rstd = lax.rsqrt(jnp.mean(hv * hv, axis=-1, keepdims=True) + EPS)
            xhat = hv * rstd
            dxhat = acc * g_ref[...]
            dh = dres_ref[...] + rstd * (dxhat - xhat * jnp.mean(dxhat * xhat, axis=-1, keepdims=True))
            o_ref[...] = dh
            dhb_ref[...] = dh.astype(BF16)
            dg_part = jnp.sum(acc * xhat, axis=0, keepdims=True)
            first = pl.program_id(0) == 0

            @pl.when(first)
            def _():
                dg_ref[...] = dg_part

            @pl.when(jnp.logical_not(first))
            def _():
                dg_ref[...] += dg_part

        part = lax.dot_general(a_ref[...], b_ref[...], NT_DIMS, preferred_element_type=F32)
        if nk == 1:
            finish(part)
        else:
            acc_ref, = scr
            kk = pl.program_id(2)

            @pl.when(kk == 0)
            def _():
                acc_ref[...] = part

            @pl.when(kk > 0)
            def _():
                acc_ref[...] += part

            @pl.when(kk == nk - 1)
            def _():
                finish(acc_ref[...])

    in_specs = [
        pl.BlockSpec((tm, tk), lambda i, j, kk: (i, kk)),
        pl.BlockSpec((None, None, tn, tk), lambda i, j, kk: (layer, kk // kpb, j, kk % kpb)),
    ]
    args = [a, b4] + extras
    block = pl.BlockSpec((tm, tn), lambda i, j, kk: (i, j))
    vec = pl.BlockSpec((1, tn), lambda i, j, kk: (0, j))
    if rms:
        in_specs += [block, vec, block]
        out_specs = [block, block, vec]
        out_shape = [jax.ShapeDtypeStruct((m, n_out), F32), jax.ShapeDtypeStruct((m, n_out), BF16),
                     jax.ShapeDtypeStruct((1, n_out), F32)]
    else:
        in_specs += [block] * len(extras)
        out_specs, out_shape = [block], [jax.ShapeDtypeStruct((m, n_out), out_dtype)]
    res, rode = _call(
        body, name=name, grid=grid, in_specs=in_specs, out_specs=out_specs, out_shape=out_shape,
        scratch_shapes=[] if nk == 1 else [pltpu.VMEM((tm, tn), F32)],
        semantics=("arbitrary",) * 3 if rms else ("parallel", "parallel", "arbitrary"), args=args, rider=rider)
    res = res if rms else res[0]
    return res if rider is None else (res, rode)


def _mm_tn(a, b, s_dim, *, name, rider=None, tm=1024, t1=1024):
    m, k1 = a.shape
    mb, n_all = b.shape
    assert mb == m and n_all % s_dim == 0
    n = n_all // s_dim
    tn = min(n, 1024)
    t1 = min(t1, k1)
    assert m % tm == 0 and k1 % t1 == 0 and n % tn == 0
    npb = n // tn
    nk = m // tm
    grid = (k1 // t1, s_dim * npb, nk)

    def body(a_ref, b_ref, o_ref, acc_ref):
        kk = pl.program_id(2)
        part = lax.dot_general(a_ref[...], b_ref[...], TN_DIMS, preferred_element_type=F32)

        @pl.when(kk == 0)
        def _():
            acc_ref[...] = part

        @pl.when(kk > 0)
        def _():
            acc_ref[...] += part

        @pl.when(kk == nk - 1)
        def _():
            o_ref[...] = acc_ref[...].astype(BF16)

    res, rode = _call(
        body, name=name, grid=grid,
        in_specs=[pl.BlockSpec((tm, t1), lambda i, j, kk: (kk, i)),
                  pl.BlockSpec((tm, tn), lambda i, j, kk: (kk, j))],
        out_specs=[pl.BlockSpec((None, None, t1, tn), lambda i, j, kk: (0, j // npb, i, j % npb))],
        out_shape=[jax.ShapeDtypeStruct((1, s_dim, k1, n), BF16)],
        scratch_shapes=[pltpu.VMEM((t1, tn), F32)],
        semantics=("parallel", "parallel", "arbitrary"), args=[a, b], rider=rider)
    return res[0] if rider is None else (res[0], rode)


ROW_TILE = 512


def _rms_fwd(h, g, *, name):
    m, d = h.shape

    def body(h_ref, g_ref, o_ref):
        hv = h_ref[...]
        rstd = lax.rsqrt(jnp.mean(hv * hv, axis=-1, keepdims=True) + EPS)
        o_ref[...] = (hv * rstd * g_ref[...]).astype(BF16)

    return pl.pallas_call(
        body, name=name, grid=(m // ROW_TILE,),
        in_specs=[pl.BlockSpec((ROW_TILE, d), lambda i: (i, 0)), pl.BlockSpec((1, d), lambda i: (0, 0))],
        out_specs=pl.BlockSpec((ROW_TILE, d), lambda i: (i, 0)),
        out_shape=jax.ShapeDtypeStruct((m, d), BF16),
        compiler_params=_params(("parallel",)),
    )(h, g)


def _final_loss(h, g, target):
    m, d = h.shape

    def body(h_ref, g_ref, t_ref, dh_ref, dhb_ref, dg_ref, loss_ref):
        hv = h_ref[...]
        gv = g_ref[...]
        rstd = lax.rsqrt(jnp.mean(hv * hv, axis=-1, keepdims=True) + EPS)
        xhat = hv * rstd
        err = xhat * gv - t_ref[...]
        dy = err * (1.0 / d)
        dxhat = dy * gv
        dh = rstd * (dxhat - xhat * jnp.mean(dxhat * xhat, axis=-1, keepdims=True))
        dh_ref[...] = dh
        dhb_ref[...] = dh.astype(BF16)
        dg_part = jnp.sum(dy * xhat, axis=0, keepdims=True)
        sq = jnp.sum(jnp.sum(err * err, axis=1, keepdims=True), axis=0, keepdims=True) * (0.5 / d)
        loss_part = jnp.broadcast_to(sq, (8, TILE))

        @pl.when(pl.program_id(0) == 0)
        def _():
            dg_ref[...] = dg_part
            loss_ref[...] = loss_part

        @pl.when(pl.program_id(0) > 0)
        def _():
            dg_ref[...] += dg_part
            loss_ref[...] += loss_part

    row = pl.BlockSpec((ROW_TILE, d), lambda i: (i, 0))
    vec = pl.BlockSpec((1, d), lambda i: (0, 0))
    return pl.pallas_call(
        body, name="final_loss", grid=(m // ROW_TILE,),
        in_specs=[row, vec, row],
        out_specs=[row, row, vec, pl.BlockSpec((8, TILE), lambda i: (0, 0))],
        out_shape=[jax.ShapeDtypeStruct((m, d), F32), jax.ShapeDtypeStruct((m, d), BF16),
                   jax.ShapeDtypeStruct((1, d), F32), jax.ShapeDtypeStruct((8, TILE), F32)],
        compiler_params=_params(("arbitrary",)),
    )(h, g, target)


def _shift_down(x, s, t_idx):
    return jnp.where(t_idx >= s, pltpu.roll(x, s, 0), 0.0)


def _shift_up(x, s, t_idx, t_len):
    return jnp.where(t_idx < t_len - s, pltpu.roll(x, t_len - s, 0), 0.0)


def _pool_select(group, s2, s4, s8, s16):
    return jnp.where(group == 0, s2, jnp.where(group == 1, s4, jnp.where(group == 2, s8, s16)))


def _pool_count(group, t_idx):
    win = jnp.left_shift(2, group)
    return jnp.minimum(t_idx + 1, win).astype(F32)


def _pool_fwd_math(a, group, t_idx):
    s2 = a + _shift_down(a, 1, t_idx)
    s4 = s2 + _shift_down(s2, 2, t_idx)
    s8 = s4 + _shift_down(s4, 4, t_idx)
    s16 = s8 + _shift_down(s8, 8, t_idx)
    return _pool_select(group, s2, s4, s8, s16) / _pool_count(group, t_idx) - a


def _pool_bwd_math(dpooled, group, t_idx, t_len):
    e = dpooled / _pool_count(group, t_idx)
    s2 = e + _shift_up(e, 1, t_idx, t_len)
    s4 = s2 + _shift_up(s2, 2, t_idx, t_len)
    s8 = s4 + _shift_up(s4, 4, t_idx, t_len)
    s16 = s8 + _shift_up(s8, 8, t_idx, t_len)
    return _pool_select(group, s2, s4, s8, s16) - dpooled


def _conv_fwd_math(c, w_ref, b_ref, t_idx):
    return (w_ref[0:1, :] * _shift_down(c, 2, t_idx) + w_ref[1:2, :] * _shift_down(c, 1, t_idx)
            + w_ref[2:3, :] * c + b_ref[...])


def _ab_fwd(p, pool_w, pool_scale, conv_w, conv_b, nseq, t_len):
    m = p.shape[0]
    ng = 4

    def body(a_ref, xb_ref, gb_ref, gc_ref, pw_ref, ps_ref, cw_ref, cb_ref, o_ref):
        j = pl.program_id(1)
        t_idx = lax.broadcasted_iota(jnp.int32, (t_len, TILE), 0)

        @pl.when(j < ng)
        def _():
            pooled = _pool_fwd_math(a_ref[...].astype(F32), j, t_idx)
            mixed = jnp.dot(pooled.astype(BF16), pw_ref[...].astype(BF16), preferred_element_type=F32)
            o_ref[...] = (mixed * ps_ref[...]).astype(BF16)

        @pl.when(j >= ng)
        def _():
            c = gc_ref[...].astype(F32) * xb_ref[...].astype(F32)
            y = _conv_fwd_math(c, cw_ref, cb_ref, t_idx)
            o_ref[...] = (gb_ref[...].astype(F32) * y).astype(BF16)

    def pool_j(j):
        return jnp.minimum(j, ng - 1)

    def conv_j(j):
        return jnp.maximum(j - ng, 0)

    in_specs = [
        pl.BlockSpec((t_len, TILE), lambda s, j: (s, pool_j(j))),
        pl.BlockSpec((t_len, TILE), lambda s, j: (s, ng + conv_j(j))),
        pl.BlockSpec((t_len, TILE), lambda s, j: (s, 2 * ng + conv_j(j))),
        pl.BlockSpec((t_len, TILE), lambda s, j: (s, 3 * ng + conv_j(j))),
        pl.BlockSpec((None, TILE, TILE), lambda s, j: (pool_j(j), 0, 0)),
        pl.BlockSpec((None, 1, TILE), lambda s, j: (pool_j(j), 0, 0)),
        pl.BlockSpec((3, TILE), lambda s, j: (0, conv_j(j))),
        pl.BlockSpec((1, TILE), lambda s, j: (0, conv_j(j))),
    ]
    return pl.pallas_call(
        body, name="ab_mixer_fwd", grid=(nseq, 2 * ng), in_specs=in_specs,
        out_specs=pl.BlockSpec((t_len, TILE), lambda s, j: (s, j)),
        out_shape=jax.ShapeDtypeStruct((m, 2 * ng * TILE), BF16),
        compiler_params=_params(("parallel", "arbitrary")),
    )(p, p, p, p, pool_w, pool_scale, conv_w, conv_b)


def _ab_bwd(p, dmix, pool_w, pool_scale, conv_w, conv_b, nseq, t_len, rider=None):
    m = p.shape[0]
    ng = 4

    def body(a_ref, xb_ref, gb_ref, gc_ref, dma_ref, dmb_ref, pw_ref, ps_ref, cw_ref, cb_ref,
             da_ref, dxb_ref, dgb_ref, dgc_ref, dpw_ref, dps_ref, dcw_ref, dcb_ref):
        j = pl.program_id(0)
        first = pl.program_id(1) == 0
        t_idx = lax.broadcasted_iota(jnp.int32, (t_len, TILE), 0)

        pooled = _pool_fwd_math(a_ref[...].astype(F32), j, t_idx).astype(BF16)
        w_bf = pw_ref[...].astype(BF16)
        mixed = jnp.dot(pooled, w_bf, preferred_element_type=F32)
        dm = dma_ref[...].astype(F32)
        dps = jnp.sum(dm * mixed, axis=0, keepdims=True)
        dmixed = (dm * ps_ref[...]).astype(BF16)
        dpw = lax.dot_general(pooled, dmixed, TN_DIMS, preferred_element_type=F32)
        dpooled = lax.dot_general(dmixed, w_bf, NT_DIMS, preferred_element_type=F32)
        da_ref[...] = _pool_bwd_math(dpooled, j, t_idx, t_len).astype(BF16)

        xb = xb_ref[...].astype(F32)
        gb = gb_ref[...].astype(F32)
        gc = gc_ref[...].astype(F32)
        d = dmb_ref[...].astype(F32)
        c = gc * xb
        c1 = _shift_down(c, 1, t_idx)
        c2 = _shift_down(c, 2, t_idx)
        y = cw_ref[0:1, :] * c2 + cw_ref[1:2, :] * c1 + cw_ref[2:3, :] * c + cb_ref[...]
        dgb_ref[...] = (d * y).astype(BF16)
        dy = d * gb
        dc = (cw_ref[2:3, :] * dy + cw_ref[1:2, :] * _shift_up(dy, 1, t_idx, t_len)
              + cw_ref[0:1, :] * _shift_up(dy, 2, t_idx, t_len))
        dgc_ref[...] = (dc * xb).astype(BF16)
        dxb_ref[...] = (dc * gc).astype(BF16)
        dcw = jnp.concatenate([jnp.sum(dy * c2, axis=0, keepdims=True),
                               jnp.sum(dy * c1, axis=0, keepdims=True),
                               jnp.sum(dy * c, axis=0, keepdims=True)], axis=0)
        dcb = jnp.sum(dy, axis=0, keepdims=True)

        @pl.when(first)
        def _():
            dpw_ref[...] = dpw
            dps_ref[...] = dps
            dcw_ref[...] = dcw
            dcb_ref[...] = dcb

        @pl.when(jnp.logical_not(first))
        def _():
            dpw_ref[...] += dpw
            dps_ref[...] += dps
            dcw_ref[...] += dcw
            dcb_ref[...] += dcb

    def col(k):
        return pl.BlockSpec((t_len, TILE), lambda j, s: (s, k * ng + j))

    in_specs = [
        col(0), col(1), col(2), col(3), col(0), col(1),
        pl.BlockSpec((None, TILE, TILE), lambda j, s: (j, 0, 0)),
        pl.BlockSpec((None, 1, TILE), lambda j, s: (j, 0, 0)),
        pl.BlockSpec((3, TILE), lambda j, s: (0, j)),
        pl.BlockSpec((1, TILE), lambda j, s: (0, j)),
    ]
    piece = pl.BlockSpec((t_len, TILE), lambda j, s: (s, j))
    out_specs = [
        piece, piece, piece, piece,
        pl.BlockSpec((None, TILE, TILE), lambda j, s: (j, 0, 0)),
        pl.BlockSpec((None, 1, TILE), lambda j, s: (j, 0, 0)),
        pl.BlockSpec((3, TILE), lambda j, s: (0, j)),
        pl.BlockSpec((1, TILE), lambda j, s: (0, j)),
    ]
    w = ng * TILE
    out_shape = [jax.ShapeDtypeStruct((m, w), BF16)] * 4 + [
        jax.ShapeDtypeStruct((ng, TILE, TILE), F32), jax.ShapeDtypeStruct((ng, 1, TILE), F32),
        jax.ShapeDtypeStruct((3, w), F32), jax.ShapeDtypeStruct((1, w), F32)]
    res, rode = _call(
        body, name="ab_mixer_bwd", grid=(ng, nseq), in_specs=in_specs, out_specs=out_specs, out_shape=out_shape,
        scratch_shapes=[], semantics=("parallel", "arbitrary"),
        args=[p, p, p, p, dmix, dmix, pool_w, pool_scale, conv_w, conv_b], rider=rider)
    return res if rider is None else (res, rode)


SGU_ROWS = 512
INV_SQRT2 = 1.0 / math.sqrt(2.0)
INV_SQRT_2PI = 1.0 / math.sqrt(2.0 * math.pi)


def _gelu(x):
    return 0.5 * x * (1.0 + lax.erf(x * INV_SQRT2))


def _gelu_grad(x):
    return 0.5 * (1.0 + lax.erf(x * INV_SQRT2)) + x * (INV_SQRT_2PI * jnp.exp(-0.5 * x * x))


def _causal_tile(transposed=False):
    r = lax.broadcasted_iota(jnp.int32, (TILE, TILE), 0)
    c = lax.broadcasted_iota(jnp.int32, (TILE, TILE), 1)
    return r <= c if transposed else c <= r


def _sgu_norm(v, g_ref, b_ref):
    mu = jnp.mean(v, axis=-1, keepdims=True)
    xc = v - mu
    rstd = lax.rsqrt(jnp.mean(xc * xc, axis=-1, keepdims=True) + EPS)
    xhat = xc * rstd
    return xhat, rstd, xhat * g_ref[...] + b_ref[...]


def _sgu_fwd(p, norm_g, norm_b, w_s, bias_tile):
    m = p.shape[0]
    ng = 4
    width = ng * TILE

    def body(u_ref, v_ref, g_ref, b_ref, w_ref, bias_ref, o_ref):
        u = _gelu(u_ref[...].astype(F32))
        _, _, vln = _sgu_norm(_gelu(v_ref[...].astype(F32)), g_ref, b_ref)
        vln = vln.astype(BF16)
        causal = _causal_tile()
        for g in range(ng):
            cols = slice(g * TILE, (g + 1) * TILE)
            wg = jnp.where(causal, w_ref[g], 0.0).astype(BF16)
            for n in range(SGU_ROWS // TILE):
                rows = slice(n * TILE, (n + 1) * TILE)
                s = jnp.dot(wg, vln[rows, cols], preferred_element_type=F32) + bias_ref[g]
                o_ref[rows, cols] = (u[rows, cols] * s).astype(BF16)

    vec = pl.BlockSpec((1, width), lambda i: (0, 0))
    tiles = pl.BlockSpec((ng, TILE, TILE), lambda i: (0, 0, 0))
    return pl.pallas_call(
        body, name="sgu_fwd", grid=(m // SGU_ROWS,),
        in_specs=[pl.BlockSpec((SGU_ROWS, width), lambda i: (i, 0)),
                  pl.BlockSpec((SGU_ROWS, width), lambda i: (i, 1)), vec, vec, tiles, tiles],
        out_specs=pl.BlockSpec((SGU_ROWS, width), lambda i: (i, 0)),
        out_shape=jax.ShapeDtypeStruct((m, width), BF16),
        compiler_params=_params(("parallel",)),
    )(p, p, norm_g, norm_b, w_s, bias_tile)


def _sgu_bwd(p, dmix, norm_g, norm_b, w_s, w_s_t, bias_tile):
    m = p.shape[0]
    ng = 4
    width = ng * TILE

    def body(u_ref, v_ref, dc_ref, g_ref, b_ref, w_ref, wt_ref, bias_ref,
             du_ref, dv_ref, dw_ref, dbs_ref, dg_ref, db_ref, ds_scr, dvln_scr):
        u_pre = u_ref[...].astype(F32)
        v_pre = v_ref[...].astype(F32)
        u = _gelu(u_pre)
        xhat, rstd, vln = _sgu_norm(_gelu(v_pre), g_ref, b_ref)
        vln = vln.astype(BF16)
        dc = dc_ref[...].astype(F32)
        causal = _causal_tile()
        ones = jnp.ones((TILE, TILE), BF16)
        first = pl.program_id(0) == 0
        for g in range(ng):
            cols = slice(g * TILE, (g + 1) * TILE)
            wg = jnp.where(causal, w_ref[g], 0.0).astype(BF16)
            wgt = jnp.where(_causal_tile(transposed=True), wt_ref[g], 0.0).astype(BF16)
            dw_acc = jnp.zeros((TILE, TILE), F32)
            dbs_acc = jnp.zeros((TILE, TILE), F32)
            for n in range(SGU_ROWS // TILE):
                rows = slice(n * TILE, (n + 1) * TILE)
                vt = vln[rows, cols]
                s = jnp.dot(wg, vt, preferred_element_type=F32) + bias_ref[g]
                ds_scr[rows, cols] = dc[rows, cols] * s
                ds = (dc[rows, cols] * u[rows, cols]).astype(BF16)
                dw_acc += lax.dot_general(ds, vt, NT_DIMS, preferred_element_type=F32)
                dbs_acc += jnp.dot(ds, ones, preferred_element_type=F32)
                dvln_scr[rows, cols] = jnp.dot(wgt, ds, preferred_element_type=F32)
            dw_g = jnp.where(causal, dw_acc, 0.0)

            @pl.when(first)
            def _():
                dw_ref[g] = dw_g
                dbs_ref[g] = dbs_acc

            @pl.when(jnp.logical_not(first))
            def _():
                dw_ref[g] += dw_g
                dbs_ref[g] += dbs_acc

        du_ref[...] = (ds_scr[...] * _gelu_grad(u_pre)).astype(BF16)
        dvln = dvln_scr[...]
        dxhat = dvln * g_ref[...]
        dv = rstd * (dxhat - jnp.mean(dxhat, axis=-1, keepdims=True)
                     - xhat * jnp.mean(dxhat * xhat, axis=-1, keepdims=True))
        dv_ref[...] = (dv * _gelu_grad(v_pre)).astype(BF16)
        dg_part = jnp.sum(dvln * xhat, axis=0, keepdims=True)
        db_part = jnp.sum(dvln, axis=0, keepdims=True)

        @pl.when(first)
        def _():
            dg_ref[...] = dg_part
            db_ref[...] = db_part

        @pl.when(jnp.logical_not(first))
        def _():
            dg_ref[...] += dg_part
            db_ref[...] += db_part

    vec = pl.BlockSpec((1, width), lambda i: (0, 0))
    tiles = pl.BlockSpec((ng, TILE, TILE), lambda i: (0, 0, 0))
    rows0 = pl.BlockSpec((SGU_ROWS, width), lambda i: (i, 0))
    rows1 = pl.BlockSpec((SGU_ROWS, width), lambda i: (i, 1))
    return pl.pallas_call(
        body, name="sgu_bwd", grid=(m // SGU_ROWS,),
        in_specs=[rows0, rows1, rows0, vec, vec, tiles, tiles, tiles],
        out_specs=[rows0, rows0, tiles, tiles, vec, vec],
        out_shape=[jax.ShapeDtypeStruct((m, width), BF16), jax.ShapeDtypeStruct((m, width), BF16),
                   jax.ShapeDtypeStruct((ng, TILE, TILE), F32), jax.ShapeDtypeStruct((ng, TILE, TILE), F32),
                   jax.ShapeDtypeStruct((1, width), F32), jax.ShapeDtypeStruct((1, width), F32)],
        scratch_shapes=[pltpu.VMEM((SGU_ROWS, width), F32), pltpu.VMEM((SGU_ROWS, width), F32)],
        compiler_params=_params(("arbitrary",)),
    )(p, p, dmix, norm_g, norm_b, w_s, w_s_t, bias_tile)


SB_DH = 64
SB_SCALE = 1.0 / math.sqrt(SB_DH)


SB_BLOCK = 256
SB_SUB = SB_BLOCK // TILE


def _sum_matrix(kind):
    j = lax.broadcasted_iota(jnp.int32, (TILE, 2 * TILE), 0)
    s = lax.broadcasted_iota(jnp.int32, (TILE, 2 * TILE), 1)
    tri = {"after": j > s, "upto": j <= s, "before": j < s}[kind]
    return jnp.where(jnp.logical_or(s >= TILE, tri), 1.0, 0.0).astype(BF16)


def _strict_mask():
    r = lax.broadcasted_iota(jnp.int32, (SB_BLOCK, SB_BLOCK), 0)
    c = lax.broadcasted_iota(jnp.int32, (SB_BLOCK, SB_BLOCK), 1)
    return c < r


def _head_lanes(h):
    lane = lax.broadcasted_iota(jnp.int32, (1, TILE), 1)
    return (lane >= h * SB_DH) & (lane < (h + 1) * SB_DH)


def _softplus(z):
    return jnp.maximum(z, 0.0) + jnp.log(1.0 + jnp.exp(-jnp.abs(z)))


def _sb_fwd(p, nseq, t_len, gather):
    m = p.shape[0]
    npair = 4
    ng = len(gather)
    last_step = nseq * npair - 1

    def body(q_ref, k_ref, v_ref, *rest):
        o_ref, lt_ref = rest[ng:ng + 2]
        kh_ref, vh_ref = rest[2 * ng + 2:2 * ng + 4]
        step = pl.program_id(0) * npair + pl.program_id(1)
        send, forward, finish = _gather_steps(rest[ng + 2:2 * ng + 2], *rest[2 * ng + 4:])
        pl.when(step == 0)(send)
        pl.when(step == (last_step + 1) // 2)(forward)
        for h in range(2):
            keep = _head_lanes(h)
            kh_ref[h] = jnp.where(keep, k_ref[...], 0).astype(BF16)
            vh_ref[h] = jnp.where(keep, v_ref[...], 0).astype(BF16)
        summat = _sum_matrix("after")
        strict = _strict_mask()

        def one_pass(q, row0, diag, state):
            rows = pl.ds(row0, SB_BLOCK)
            z, sp, pieces = [], [], []
            for h in range(2):
                zh = lax.dot_general(q, kh_ref[h, rows, :], NT_DIMS, preferred_element_type=F32) * SB_SCALE
                sph = _softplus(zh)
                logkeep = jnp.where(strict, -sph, 0.0) if diag else -sph
                z.append(zh)
                sp.append(sph)
                pieces += [logkeep[:, b * TILE:(b + 1) * TILE] for b in range(SB_SUB)]
            sums = jnp.dot(jnp.concatenate(pieces, axis=0).astype(BF16), summat, preferred_element_type=F32)
            out = []
            for h in range(2):
                carry, acc = state[2 * h], state[2 * h + 1]
                after = [None] * SB_SUB
                for b in reversed(range(SB_SUB)):
                    part = sums[(h * SB_SUB + b) * SB_BLOCK:(h * SB_SUB + b + 1) * SB_BLOCK]
                    after[b] = part[:, :TILE] + carry
                    carry = carry + part[:, TILE:]
                w = jnp.exp(z[h] - sp[h] + jnp.concatenate(after, axis=1))
                if diag:
                    w = jnp.where(strict, w, 0.0)
                out += [carry, acc + jnp.dot(w.astype(BF16), vh_ref[h, rows, :], preferred_element_type=F32)]
            return tuple(out)

        def q_block(i, _):
            r0 = pl.multiple_of(i * SB_BLOCK, SB_BLOCK)
            q = q_ref[pl.ds(r0, SB_BLOCK), :]
            zero = jnp.zeros((SB_BLOCK, TILE), F32)
            state = one_pass(q, r0, True, (zero,) * 4)
            state = lax.fori_loop(
                0, i, lambda jj, st: one_pass(q, pl.multiple_of((i - 1 - jj) * SB_BLOCK, SB_BLOCK), False, st), state)
            o_ref[pl.ds(r0, SB_BLOCK), :] = (state[1] + state[3]).astype(BF16)
            lt_ref[pl.ds(r0, SB_BLOCK), :] = jnp.where(_head_lanes(0), state[0], state[2])
            return 0

        lax.fori_loop(0, t_len // SB_BLOCK, q_block, 0)
        pl.when(step == last_step)(finish)

    def col(k):
        return pl.BlockSpec((t_len, TILE), lambda s, hp: (s, k * npair + hp))

    out = pl.BlockSpec((t_len, TILE), lambda s, hp: (s, hp))
    res = pl.pallas_call(
        body, name="stickbreak_fwd", grid=(nseq, npair), in_specs=[col(2), col(3), col(4)] + [ANY] * ng,
        out_specs=[out, out] + [ANY] * ng,
        out_shape=[jax.ShapeDtypeStruct((m, npair * TILE), BF16), jax.ShapeDtypeStruct((m, npair * TILE), F32)]
        + [jax.ShapeDtypeStruct(b.shape, b.dtype) for b in gather],
        input_output_aliases={3 + a: 2 + a for a in range(ng)},
        scratch_shapes=[pltpu.VMEM((2, t_len, TILE), BF16), pltpu.VMEM((2, t_len, TILE), BF16)] + _gather_sems(ng),
        compiler_params=pltpu.CompilerParams(dimension_semantics=("arbitrary", "arbitrary"),
                                             vmem_limit_bytes=VMEM_LIMIT_BYTES, has_side_effects=True),
    )(p, p, p, *gather)
    return res[0], res[1], res[2:]


def _sb_bwd(p, dmix, ltot, nseq, t_len, exchange):
    m = p.shape[0]
    npair = 4
    ne = len(exchange)
    last_step = nseq * npair - 1

    def body(q_ref, k_ref, v_ref, do_ref, lt_ref, *rest):
        dq_ref, dk_ref, dv_ref = rest[ne:ne + 3]
        kh_ref, vh_ref, dk_acc, dv_acc = rest[2 * ne + 3:2 * ne + 7]
        step = pl.program_id(0) * npair + pl.program_id(1)
        send, finish = _exchange_steps(rest[:ne], rest[ne + 3:2 * ne + 3], *rest[2 * ne + 7:])
        pl.when(step == 0)(send)
        for h in range(2):
            keep = _head_lanes(h)
            kh_ref[h] = jnp.where(keep, k_ref[...], 0).astype(BF16)
            vh_ref[h] = jnp.where(keep, v_ref[...], 0).astype(BF16)
        dk_acc[...] = jnp.zeros_like(dk_acc)
        dv_acc[...] = jnp.zeros_like(dv_acc)
        sum_upto = _sum_matrix("upto")
        sum_before = _sum_matrix("before")
        strict = _strict_mask()
        lane = lax.broadcasted_iota(jnp.int32, (SB_BLOCK, TILE), 1)

        def running(x, matrix, start):
            pieces = [x[h][:, b * TILE:(b + 1) * TILE] for h in range(2) for b in range(SB_SUB)]
            sums = jnp.dot(jnp.concatenate(pieces, axis=0).astype(BF16), matrix, preferred_element_type=F32)
            wide, ends = [], []
            for h in range(2):
                total, cols = start[h], []
                for b in range(SB_SUB):
                    part = sums[(h * SB_SUB + b) * SB_BLOCK:(h * SB_SUB + b + 1) * SB_BLOCK]
                    cols.append(part[:, :TILE] + total)
                    total = total + part[:, TILE:]
                wide.append(jnp.concatenate(cols, axis=1))
                ends.append(total)
            return wide, ends

        def one_pass(q, do, qh, doh, ltot, row0, diag, state):
            rows = pl.ds(row0, SB_BLOCK)
            z, sp, logkeep = [], [], []
            for h in range(2):
                zh = lax.dot_general(q, kh_ref[h, rows, :], NT_DIMS, preferred_element_type=F32) * SB_SCALE
                sph = _softplus(zh)
                z.append(zh)
                sp.append(sph)
                logkeep.append(jnp.where(strict, -sph, 0.0) if diag else -sph)
            upto, sum_l = running(logkeep, sum_upto, [state[0], state[3]])
            w, g = [], []
            for h in range(2):
                wh = jnp.exp(z[h] - sp[h] + (ltot[h] - upto[h]))
                if diag:
                    wh = jnp.where(strict, wh, 0.0)
                w.append(wh)
                g.append(wh * lax.dot_general(do, vh_ref[h, rows, :], NT_DIMS, preferred_element_type=F32))
            g_before, sum_g = running(g, sum_before, [state[1], state[4]])
            out, dk_new, dv_new = [], 0.0, 0.0
            for h in range(2):
                dz = (g[h] - jnp.exp(z[h] - sp[h]) * (g[h] + g_before[h])) * SB_SCALE
                if diag:
                    dz = jnp.where(strict, dz, 0.0)
                dzb = dz.astype(BF16)
                dq = state[3 * h + 2] + jnp.dot(dzb, kh_ref[h, rows, :], preferred_element_type=F32)
                dk_new = dk_new + lax.dot_general(dzb, qh[h], TN_DIMS, preferred_element_type=F32)
                dv_new = dv_new + lax.dot_general(w[h].astype(BF16), doh[h], TN_DIMS, preferred_element_type=F32)
                out += [sum_l[h], sum_g[h], dq]
            dk_acc[rows, :] += dk_new
            dv_acc[rows, :] += dv_new
            return tuple(out)

        def q_block(i, _):
            r0 = pl.multiple_of(i * SB_BLOCK, SB_BLOCK)
            q = q_ref[pl.ds(r0, SB_BLOCK), :]
            do = do_ref[pl.ds(r0, SB_BLOCK), :]
            lt = lt_ref[pl.ds(r0, SB_BLOCK), :]
            qh, doh, ltot = [], [], []
            for h in range(2):
                keep = _head_lanes(h)
                qh.append(jnp.where(keep, q, 0).astype(BF16))
                doh.append(jnp.where(keep, do, 0).astype(BF16))
                ltot.append(jnp.sum(jnp.where(lane == h * SB_DH, lt, 0.0), axis=1, keepdims=True))
            zero = jnp.zeros((SB_BLOCK, TILE), F32)
            state = lax.fori_loop(
                0, i,
                lambda jj, st: one_pass(q, do, qh, doh, ltot, pl.multiple_of(jj * SB_BLOCK, SB_BLOCK), False, st),
                (zero,) * 6)
            state = one_pass(q, do, qh, doh, ltot, r0, True, state)
            dq_ref[pl.ds(r0, SB_BLOCK), :] = (state[2] + state[5]).astype(BF16)
            return 0

        lax.fori_loop(0, t_len // SB_BLOCK, q_block, 0)
        dk_ref[...] = dk_acc[...].astype(BF16)
        dv_ref[...] = dv_acc[...].astype(BF16)
        pl.when(step == last_step)(finish)

    def col(k):
        return pl.BlockSpec((t_len, TILE), lambda s, hp: (s, k * npair + hp))

    out = pl.BlockSpec((t_len, TILE), lambda s, hp: (s, hp))
    width = npair * TILE
    res = pl.pallas_call(
        body, name="stickbreak_bwd", grid=(nseq, npair),
        in_specs=[col(2), col(3), col(4), col(1), out] + [ANY] * ne, out_specs=[out, out, out] + [ANY] * ne,
        out_shape=[jax.ShapeDtypeStruct((m, width), BF16)] * 3 + _exchange_shapes(exchange),
        scratch_shapes=[pltpu.VMEM((2, t_len, TILE), BF16), pltpu.VMEM((2, t_len, TILE), BF16),
                        pltpu.VMEM((t_len, TILE), F32), pltpu.VMEM((t_len, TILE), F32)] + _exchange_sems(ne),
        compiler_params=pltpu.CompilerParams(dimension_semantics=("arbitrary", "arbitrary"),
                                             vmem_limit_bytes=VMEM_LIMIT_BYTES, has_side_effects=True),
    )(p, p, p, dmix, ltot, *exchange)
    return res[0], res[1], res[2], res[3:]


def _adam_math(w, g, m, v):
    m = ADAM_B1 * m + (1.0 - ADAM_B1) * g
    v = ADAM_B2 * v + (1.0 - ADAM_B2) * (g * g)
    m_hat = m / (1.0 - ADAM_B1 ** ADAM_STEP)
    v_hat = v / (1.0 - ADAM_B2 ** ADAM_STEP)
    delta = -ADAM_LR * (m_hat / (jnp.sqrt(v_hat) + ADAM_EPS) + ADAM_WD * w)
    return delta, m, v


def _cast_place(w, layer, pos, *, name):
    _, r, c = w.shape
    tr = min(r, 256)

    def body(pos_ref, w_ref, o_ref):
        o_ref[...] = w_ref[...].astype(BF16)

    grid_spec = pltpu.PrefetchScalarGridSpec(
        num_scalar_prefetch=1, grid=(r // tr,),
        in_specs=[pl.BlockSpec((None, tr, c), lambda i, pos_ref: (layer, i, 0))],
        out_specs=pl.BlockSpec((None, None, tr, c), lambda i, pos_ref: (0, pos_ref[0], i, 0)))
    return pl.pallas_call(
        body, name=name, grid_spec=grid_spec, out_shape=jax.ShapeDtypeStruct((1, N_CHIP, r, c), BF16),
        compiler_params=_params(("parallel",)),
    )(pos, w)


def _pair_sum(mine, got, pos, *, name):
    l_dim, s_dim, h, c = got.shape
    th = min(h, 512)
    nt = h // th

    def body(pos_ref, a_ref, b_ref, o_ref):
        o_ref[...] = (a_ref[...].astype(F32) + b_ref[...].astype(F32)).astype(BF16)

    spec = pl.BlockSpec((None, None, th, c), lambda l, s, i, pos_ref: (l, s, i, 0))
    grid_spec = pltpu.PrefetchScalarGridSpec(
        num_scalar_prefetch=1, grid=(l_dim, s_dim, nt),
        in_specs=[pl.BlockSpec((None, None, th, c), lambda l, s, i, pos_ref: (l, s, pos_ref[1] * nt + i, 0)), spec],
        out_specs=spec)
    return pl.pallas_call(
        body, name=name, grid_spec=grid_spec, out_shape=jax.ShapeDtypeStruct(got.shape, BF16),
        compiler_params=_params(("parallel",) * 3),
    )(pos, mine, got)


def _chip_sum(sums, landed, pos, *, name):
    l_dim, _, h, c = sums.shape
    th = min(h, 512)
    nt = h // th

    def body(pos_ref, own, r0, r1, r2, o_ref):
        o_ref[...] = ((own[...].astype(F32) + r0[...].astype(F32)) + r1[...].astype(F32)) + r2[...].astype(F32)

    def piece(k):
        return pl.BlockSpec((None, None, th, c), lambda l, i, pos_ref: (l, k, i, 0))

    grid_spec = pltpu.PrefetchScalarGridSpec(
        num_scalar_prefetch=1, grid=(l_dim, nt),
        in_specs=[pl.BlockSpec((None, None, th, c), lambda l, i, pos_ref: (l, pos_ref[0], i, 0)),
                  piece(0), piece(1), piece(2)],
        out_specs=pl.BlockSpec((None, th, c), lambda l, i, pos_ref: (l, pos_ref[1] * nt + i, 0)))
    return pl.pallas_call(
        body, name=name, grid_spec=grid_spec, out_shape=jax.ShapeDtypeStruct((l_dim, 2 * h, c), F32),
        compiler_params=_params(("parallel",) * 2),
    )(pos, sums, landed, landed, landed)


def _adam_big(w, m, v, grads, *, name):
    l_dim, r, c = w.shape
    assert len(grads) == l_dim
    tr = min(r, 256)

    def body(*refs):
        w_ref, m_ref, v_ref = refs[:3]
        g_refs = refs[3:3 + l_dim]
        go_ref, d_ref, mo_ref, vo_ref = refs[3 + l_dim:]
        g = g_refs[0][...]
        for l in range(1, l_dim):
            g = jnp.where(pl.program_id(0) == l, g_refs[l][...], g)
        delta, m_new, v_new = _adam_math(w_ref[...], g, m_ref[...], v_ref[...])
        go_ref[...] = g
        d_ref[...] = delta
        mo_ref[...] = m_new
        vo_ref[...] = v_new

    spec = pl.BlockSpec((None, tr, c), lambda l, i: (l, i, 0))
    gspec = pl.BlockSpec((None, tr, c), lambda l, i: (0, i, 0))
    return pl.pallas_call(
        body, name=name, grid=(l_dim, r // tr), in_specs=[spec] * 3 + [gspec] * l_dim, out_specs=[spec] * 4,
        out_shape=[jax.ShapeDtypeStruct(w.shape, F32)] * 4, compiler_params=_params(("parallel",) * 2),
    )(w, m, v, *grads)


def _position():
    return lax.axis_index("x"), lax.axis_index("y"), lax.axis_index("c")


def _other_chips(x, y):
    return [(1 - x, y), (x, 1 - y), (1 - x, 1 - y)]


def _remote(src, dst, send_sem, recv_sem, device):
    return pltpu.make_async_remote_copy(src_ref=src, dst_ref=dst, send_sem=send_sem, recv_sem=recv_sem,
                                        device_id=device, device_id_type=MESH)


ANY = pl.BlockSpec(memory_space=pl.ANY)


def _gather_weights(bufs):
    n = len(bufs)

    def body(*refs):
        send, forward, finish = _gather_steps(refs[n:2 * n], *refs[2 * n:])
        send()
        forward()
        finish()

    return pl.pallas_call(
        body, name="gather_weights", in_specs=[ANY] * n, out_specs=[ANY] * n,
        out_shape=[jax.ShapeDtypeStruct(b.shape, b.dtype) for b in bufs],
        input_output_aliases={a: a for a in range(n)},
        scratch_shapes=_gather_sems(n),
        compiler_params=pltpu.CompilerParams(has_side_effects=True),
    )(*bufs)


def _gather_sems(n):
    return [pltpu.SemaphoreType.DMA((3 * n,))] * 4


def _gather_steps(outs, send_sems, recv_sems, fwd_send, fwd_recv):
    n = len(outs)
    x, y, c = _position()
    chips = _other_chips(x, y)
    sibling = (x, y, 1 - c)

    def half(a, chip, core):
        h = outs[a].shape[2] // 2
        return outs[a].at[:, 2 * chip[0] + chip[1], pl.ds(core * h, h), :]

    def over_ici(a, k, chip):
        block = half(a, chip, c)
        return _remote(block, block, send_sems.at[3 * a + k], recv_sems.at[3 * a + k], (*chips[k], c))

    def over_d2d(a, k, core):
        block = half(a, chips[k], core)
        return _remote(block, block, fwd_send.at[3 * a + k], fwd_recv.at[3 * a + k], sibling)

    def send():
        for a in range(n):
            for k in range(3):
                over_ici(a, k, (x, y)).start()

    def forward():
        for k in range(3):
            for a in range(n):
                over_ici(a, k, chips[k]).wait_recv()
                over_d2d(a, k, c).start()

    def finish():
        for k in range(3):
            for a in range(n):
                over_d2d(a, k, 1 - c).wait_recv()
        for a in range(n):
            for k in range(3):
                over_ici(a, k, (x, y)).wait_send()
                over_d2d(a, k, c).wait_send()

    return send, forward, finish


def _swap_halves(grads, *, name):
    n = len(grads)

    def body(*refs):
        ins, got = refs[:n], refs[n:2 * n]
        send_sems, recv_sems = refs[2 * n:]
        x, y, c = _position()
        sibling = (x, y, 1 - c)
        copies = []
        for a in range(n):
            h = grads[a].shape[2] // 2
            cp = _remote(ins[a].at[:, :, pl.ds((1 - c) * h, h), :], got[a], send_sems.at[a], recv_sems.at[a], sibling)
            cp.start()
            copies.append(cp)
        for cp in copies:
            cp.wait()

    sem = pltpu.SemaphoreType.DMA((n,))
    return pl.pallas_call(
        body, name=name, in_specs=[ANY] * n, out_specs=[ANY] * n,
        out_shape=[jax.ShapeDtypeStruct(g.shape[:2] + (g.shape[2] // 2, g.shape[3]), g.dtype) for g in grads],
        scratch_shapes=[sem, sem], compiler_params=pltpu.CompilerParams(has_side_effects=True),
    )(*grads)


def _exchange_chips(sums):
    n = len(sums)

    def body(*refs):
        send, finish = _exchange_steps(refs[:n], refs[n:2 * n], *refs[2 * n:])
        send()
        finish()

    return pl.pallas_call(
        body, name="exchange_chips", in_specs=[ANY] * n, out_specs=[ANY] * n,
        out_shape=_exchange_shapes(sums), scratch_shapes=_exchange_sems(n),
        compiler_params=pltpu.CompilerParams(has_side_effects=True),
    )(*sums)


def _exchange_shapes(sums):
    return [jax.ShapeDtypeStruct((s.shape[0], 3) + s.shape[2:], s.dtype) for s in sums]


def _exchange_sems(n):
    return [pltpu.SemaphoreType.DMA((3 * n,))] * 2


def _exchange_steps(ins, outs, send_sems, recv_sems):
    n = len(ins)
    x, y, c = _position()
    chips = _other_chips(x, y)

    def copy(a, k):
        chip = chips[k]
        return _remote(ins[a].at[:, 2 * chip[0] + chip[1]], outs[a].at[:, k],
                       send_sems.at[3 * a + k], recv_sems.at[3 * a + k], (*chip, c))

    def send():
        for a in range(n):
            for k in range(3):
                copy(a, k).start()

    def finish():
        for a in range(n):
            for k in range(3):
                copy(a, k).wait()

    return send, finish


def _join_halves(bufs, *, name):
    n = len(bufs)

    def body(*refs):
        outs = refs[n:2 * n]
        send_sems, recv_sems = refs[2 * n:]
        x, y, c = _position()
        sibling = (x, y, 1 - c)
        copies = []
        for a in range(n):
            h = bufs[a].shape[1] // 2
            mine = outs[a].at[:, pl.ds(c * h, h), :]
            cp = _remote(mine, mine, send_sems.at[a], recv_sems.at[a], sibling)
            cp.start()
            copies.append((cp, a, h))
        for cp, a, h in copies:
            cp.wait_send()
            got = outs[a].at[:, pl.ds((1 - c) * h, h), :]
            _remote(got, got, send_sems.at[a], recv_sems.at[a], sibling).wait_recv()

    sem = pltpu.SemaphoreType.DMA((n,))
    return pl.pallas_call(
        body, name=name, in_specs=[ANY] * n, out_specs=[ANY] * n,
        out_shape=[jax.ShapeDtypeStruct(b.shape, b.dtype) for b in bufs],
        input_output_aliases={a: a for a in range(n)},
        scratch_shapes=[sem, sem], compiler_params=pltpu.CompilerParams(has_side_effects=True),
    )(*bufs)


def _allreduce_small(packs):
    n = len(packs)

    def body(*refs):
        ins, outs, gath = refs[:n], refs[n:2 * n], refs[2 * n:3 * n]
        send_sems, recv_sems = refs[3 * n:]
        x, y, c = _position()
        me, sibling = (x, y, c), (x, y, 1 - c)
        chips = _other_chips(x, y)

        def slot(a, dev):
            return gath[a].at[4 * dev[0] + 2 * dev[1] + dev[2]]

        def copy(a, k, block, to, src=None):
            return _remote(slot(a, block) if src is None else src, slot(a, block),
                           send_sems.at[7 * a + k], recv_sems.at[7 * a + k], to)

        started = []
        for a in range(n):
            slot(a, me)[...] = ins[a][...]
            first = [copy(a, 0, me, sibling, src=ins[a])]
            first += [copy(a, 1 + k, me, (*chip, c), src=ins[a]) for k, chip in enumerate(chips)]
            for cp in first:
                cp.start()
            started += first
        for a in range(n):
            for k, chip in enumerate(chips):
                copy(a, 1 + k, (*chip, c), me).wait_recv()
                cp = copy(a, 4 + k, (*chip, c), sibling)
                cp.start()
                started.append(cp)
        for a in range(n):
            copy(a, 0, sibling, me).wait_recv()
            for k, chip in enumerate(chips):
                copy(a, 4 + k, (*chip, 1 - c), me).wait_recv()
        for cp in started:
            cp.wait_send()
        for a in range(n):
            total = gath[a][0]
            for d in range(1, N_DEV):
                total = total + gath[a][d]
            outs[a][...] = total

    vmem = pl.BlockSpec(memory_space=pltpu.VMEM)
    sem = pltpu.SemaphoreType.DMA((7 * n,))
    return pl.pallas_call(
        body, name="allreduce_small", in_specs=[vmem] * n, out_specs=[vmem] * n,
        out_shape=[jax.ShapeDtypeStruct(p.shape, p.dtype) for p in packs],
        scratch_shapes=[pltpu.VMEM((N_DEV,) + p.shape, p.dtype) for p in packs] + [sem, sem],
        compiler_params=pltpu.CompilerParams(has_side_effects=True, vmem_limit_bytes=VMEM_LIMIT_BYTES),
    )(*packs)


LOSS_ROW = 1040


def _pad_rows(a, rows=8):
    return jnp.concatenate([a, jnp.zeros((rows - a.shape[0], a.shape[1]), a.dtype)], axis=0)

def _adam_small(wide, mid, narrow, params):
    names = ["mix_norm_g", "mlp_norm_g", "final_norm_g", "conv_b", "conv_w", "sgu_norm_g", "sgu_norm_b",
             "pool_w", "pool_scale", "sgu_w", "sgu_b"]
    n = len(names)

    def body(*refs):
        wide_ref, mid_ref, narrow_ref = refs[:3]
        wmv = refs[3:3 + 3 * n]
        outs = refs[3 + 3 * n:]
        x, y, _ = _position()
        q = 2 * x + y

        def my_quarter(rows):
            parts = [rows[:, s * TILE:(s + 1) * TILE] for s in range(N_CHIP)]
            return jnp.where(q == 0, parts[0], jnp.where(q == 1, parts[1], jnp.where(q == 2, parts[2], parts[3])))

        def tiles(first_row):
            return [((0, g), narrow_ref[first_row + g * TILE:first_row + (g + 1) * TILE, :]) for g in range(4)]

        grads = {
            "mix_norm_g": [((), wide_ref[0:2, :])],
            "mlp_norm_g": [((), wide_ref[8:10, :])],
            "final_norm_g": [((), wide_ref[16:17, :])],
            "conv_b": [((), mid_ref[0:1, :])],
            "conv_w": [((0,), my_quarter(mid_ref[8:11, :]))],
            "sgu_norm_g": [((), my_quarter(mid_ref[16:17, :]))],
            "sgu_norm_b": [((), my_quarter(mid_ref[24:25, :]))],
            "pool_w": tiles(0),
            "sgu_w": tiles(512),
            "pool_scale": [((0,), narrow_ref[1024:1028, :])],
            "sgu_b": [((0,), narrow_ref[1032:1036, :])],
        }
        for i, name in enumerate(names):
            w_ref, m_ref, v_ref = wmv[3 * i:3 * i + 3]
            for lead, g in grads[name]:
                idx = lead + (slice(None), slice(None))
                delta, m_new, v_new = _adam_math(w_ref[idx], g, m_ref[idx], v_ref[idx])
                outs[4 * i][idx] = g
                outs[4 * i + 1][idx] = delta
                outs[4 * i + 2][idx] = m_new
                outs[4 * i + 3][idx] = v_new

    vmem = pl.BlockSpec(memory_space=pltpu.VMEM)
    args, out_shape = [wide, mid, narrow], []
    for name in names:
        w, m, v = params[name]
        args += [w, m, v]
        out_shape += [jax.ShapeDtypeStruct(w.shape, F32)] * 4
    res = pl.pallas_call(
        body, name="adam_small", in_specs=[vmem] * len(args), out_specs=[vmem] * len(out_shape),
        out_shape=out_shape, compiler_params=pltpu.CompilerParams(vmem_limit_bytes=VMEM_LIMIT_BYTES),
    )(*args)
    return {name: res[4 * i:4 * i + 4] for i, name in enumerate(names)}


def _pair_sums(grads, pos, tag):
    got = _swap_halves(grads, name=f"swap_halves_{tag}")
    return [_pair_sum(a, b, pos, name=f"pair_sum_{tag}{i}") for i, (a, b) in enumerate(zip(grads, got))]


def _finish_reduce(sums, landed, pos, tag):
    halves = [_chip_sum(s, r, pos, name=f"chip_sum_{tag}{i}") for i, (s, r) in enumerate(zip(sums, landed))]
    return _join_halves(halves, name=f"join_halves_{tag}")


def kernel(x, mix_norm_g, mlp_norm_g, ab_w_in, pool_w, pool_scale, conv_w, conv_b, ab_w_out, cd_w_in, sgu_norm_g, sgu_norm_b, sgu_w, sgu_b, cd_w_out, mlp_w1, mlp_w2, final_norm_g, loss_target, m_mix_norm_g, m_mlp_norm_g, m_ab_w_in, m_pool_w, m_pool_scale, m_conv_w, m_conv_b, m_ab_w_out, m_cd_w_in, m_sgu_norm_g, m_sgu_norm_b, m_sgu_w, m_sgu_b, m_cd_w_out, m_mlp_w1, m_mlp_w2, m_final_norm_g, v_mix_norm_g, v_mlp_norm_g, v_ab_w_in, v_pool_w, v_pool_scale, v_conv_w, v_conv_b, v_ab_w_out, v_cd_w_in, v_sgu_norm_g, v_sgu_norm_b, v_sgu_w, v_sgu_b, v_cd_w_out, v_mlp_w1, v_mlp_w2, v_final_norm_g):
    nseq, t_len, d = x.shape
    m_tok = nseq * t_len
    h0 = x.reshape(m_tok, d)
    target = loss_target.reshape(m_tok, d)

    x_idx, y_idx = lax.axis_index("x"), lax.axis_index("y")
    q_idx = 2 * x_idx + y_idx
    pos = jnp.stack([q_idx, lax.axis_index("c")]).astype(jnp.int32)
    def shard_buffer(w, layer, tag):
        return _cast_place(w, layer, pos, name=f"cast_place_{tag}")

    def row_block(w):
        return w.reshape(1, 1, -1, w.shape[-1])

    w_ab_in, w_ab_out = _gather_weights([shard_buffer(ab_w_in, 0, "ab_in"), shard_buffer(ab_w_out, 0, "ab_out")])
    w_ab_out = row_block(w_ab_out)
    later_weights = [shard_buffer(cd_w_out, 0, "cd_out"), shard_buffer(mlp_w1, 1, "w1_1"),
                     shard_buffer(mlp_w2, 1, "w2_1")]

    pool_w3, pool_scale3 = pool_w[0], pool_scale[0].reshape(4, 1, TILE)
    sgu_w3 = sgu_w[0]
    sgu_w3_t = jnp.swapaxes(sgu_w3, 1, 2)
    sgu_bias_tile = jnp.broadcast_to(sgu_b[0][:, :, None], (4, TILE, TILE))
    conv_w2, conv_b2 = conv_w[0], conv_b
    def place_quarter(v):
        return lax.dynamic_update_slice(jnp.zeros((v.shape[0], 4 * TILE), F32), v, (0, q_idx * TILE))

    sharded_small = jnp.concatenate(
        [place_quarter(conv_w[0]), place_quarter(sgu_norm_g), place_quarter(sgu_norm_b),
         jnp.zeros((3, 4 * TILE), F32)], axis=0)
    sharded_small, = _allreduce_small([sharded_small])
    sharded_small = sharded_small * 0.5
    conv_w_full = sharded_small[0:3]
    sgu_g_full = sharded_small[3:4]
    sgu_b_full = sharded_small[4:5]

    xn0 = _rms_fwd(h0, mix_norm_g[0:1], name="rms_fwd_mix0")
    p_ab, (w_1_0,) = _mm_nn(xn0, w_ab_in, 0, out_dtype=BF16, name="ab_in_proj",
                            rider=("gather", [shard_buffer(mlp_w1, 0, "w1_0")]))
    mix0 = _ab_fwd(p_ab, pool_w3, pool_scale3, conv_w_full, conv_b2, nseq, t_len)
    h1, hn0 = _mm_nn(mix0, w_ab_out, 0, out_dtype=F32, name="ab_out_proj", epilogue="residual", extra=h0,
                     norm_g=mlp_norm_g[0:1])
    act0, (w_2_0,) = _mm_nn(hn0, w_1_0, 0, out_dtype=BF16, name="mlp0_up", epilogue="relu2",
                            rider=("gather", [shard_buffer(mlp_w2, 0, "w2_0")]))
    w_2_0 = row_block(w_2_0)
    (h2, xn1), (w_cd_in,) = _mm_nn(act0, w_2_0, 0, out_dtype=F32, name="mlp0_down", epilogue="residual", extra=h1,
                                   norm_g=mix_norm_g[1:2], rider=("gather", [shard_buffer(cd_w_in, 0, "cd_in")]))

    p_cd = _mm_nn(xn1, w_cd_in, 0, out_dtype=BF16, name="cd_in_proj")
    c_out = _sgu_fwd(p_cd, sgu_g_full, sgu_b_full, sgu_w3, sgu_bias_tile)
    d_out, ltot, (w_cd_out, w_1_1, w_2_1) = _sb_fwd(p_cd, nseq, t_len, later_weights)
    w_cd_out, w_2_1 = row_block(w_cd_out), row_block(w_2_1)
    mix1 = jnp.concatenate([c_out, d_out], axis=1)
    h3, hn1 = _mm_nn(mix1, w_cd_out, 0, out_dtype=F32, name="cd_out_proj", epilogue="residual", extra=h2,
                     norm_g=mlp_norm_g[1:2])
    act1 = _mm_nn(hn1, w_1_1, 0, out_dtype=BF16, name="mlp1_up", epilogue="relu2")
    h4 = _mm_nn(act1, w_2_1, 0, out_dtype=F32, name="mlp1_down", epilogue="residual", extra=h3)

    dh4, dh4_bf, dg_final, loss_tile = _final_loss(h4, final_norm_g.reshape(1, d), target)

    def mlp_bwd(dh_out, dh_out_bf, h_in, hn, act, w_1, w_2, layer, tag):
        dz = _mm_nt(dh_out_bf, w_2, 0, out_dtype=BF16, name=f"mlp{tag}_down_bwd",
                    epilogue="relu2_bwd", extra=act)
        g_w2 = _mm_tn(act, dh_out_bf, 1, name=f"mlp{tag}_down_wgrad")
        g_w1 = _mm_tn(hn, dz, N_CHIP, name=f"mlp{tag}_up_wgrad")
        dh_in, dh_in_bf, dg = _mm_nt(dz, w_1, 0, out_dtype=F32, name=f"mlp{tag}_up_bwd", epilogue="rms_bwd",
                                     extra=(h_in, mlp_norm_g[layer:layer + 1], dh_out))
        return dh_in, dh_in_bf, dg, g_w1, g_w2

    def as_pieces(g):
        return g.reshape(1, N_CHIP, -1, g.shape[-1]) if g.shape[1] == 1 else g

    dh3, dh3_bf, dg_mlp1, g_w1_1, g_w2_1 = mlp_bwd(dh4, dh4_bf, h3, hn1, act1, w_1_1, w_2_1, 1, "1")

    dmix1 = _mm_nt(dh3_bf, w_cd_out, 0, out_dtype=BF16, name="cd_out_bwd")
    g_cd_out = _mm_tn(mix1, dh3_bf, 1, name="cd_out_wgrad")
    sums_a = _pair_sums([g_w1_1, as_pieces(g_w2_1), as_pieces(g_cd_out)], pos, "a")
    du, dv, dsgu_w, dsgu_bs, dsgu_g, dsgu_b = _sgu_bwd(p_cd, dmix1, sgu_g_full, sgu_b_full, sgu_w3, sgu_w3_t,
                                                      sgu_bias_tile)
    dq, dk, dvv, landed_a = _sb_bwd(p_cd, dmix1, ltot, nseq, t_len, sums_a)
    r_w1_1, r_w2_1, r_cd_out = _finish_reduce(sums_a, landed_a, pos, "a")
    dp_cd = jnp.concatenate([du, dv, dq, dk, dvv], axis=1)
    g_cd_in = _mm_tn(xn1, dp_cd, N_CHIP, name="cd_in_wgrad")
    dh2, dh2_bf, dg_mix1 = _mm_nt(dp_cd, w_cd_in, 0, out_dtype=F32, name="cd_in_bwd", epilogue="rms_bwd",
                                  extra=(h2, mix_norm_g[1:2], dh3))

    sums_c = _pair_sums([g_cd_in], pos, "c")
    dz0, landed_c = _mm_nt(dh2_bf, w_2_0, 0, out_dtype=BF16, name="mlp0_down_bwd", epilogue="relu2_bwd", extra=act0,
                           rider=("exchange", sums_c))
    r_cd_in, = _finish_reduce(sums_c, landed_c, pos, "c")
    g_w2_0 = _mm_tn(act0, dh2_bf, 1, name="mlp0_down_wgrad")
    g_w1_0 = _mm_tn(hn0, dz0, N_CHIP, name="mlp0_up_wgrad")
    sums_d = _pair_sums([as_pieces(g_w2_0)], pos, "d")
    (dh1, dh1_bf, dg_mlp0), landed_d = _mm_nt(dz0, w_1_0, 0, out_dtype=F32, name="mlp0_up_bwd", epilogue="rms_bwd",
                                              extra=(h1, mlp_norm_g[0:1], dh2), rider=("exchange", sums_d))
    r_w2_0, = _finish_reduce(sums_d, landed_d, pos, "d")

    dmix0 = _mm_nt(dh1_bf, w_ab_out, 0, out_dtype=BF16, name="ab_out_bwd")
    g_ab_out = _mm_tn(mix0, dh1_bf, 1, name="ab_out_wgrad")
    sums_e = _pair_sums([g_w1_0], pos, "e")
    (da, dxb, dgb, dgc, dpool_w, dpool_scale, dconv_w, dconv_b), landed_e = _ab_bwd(
        p_ab, dmix0, pool_w3, pool_scale3, conv_w_full, conv_b2, nseq, t_len, rider=("exchange", sums_e))
    r_w1_0, = _finish_reduce(sums_e, landed_e, pos, "e")
    dp_ab = jnp.concatenate([da, dxb, dgb, dgc], axis=1)
    sums_f = _pair_sums([as_pieces(g_ab_out)], pos, "f")
    g_ab_in, landed_f = _mm_tn(xn0, dp_ab, N_CHIP, name="ab_in_wgrad", rider=("exchange", sums_f))
    r_ab_out, = _finish_reduce(sums_f, landed_f, pos, "f")
    grad_x, _, dg_mix0 = _mm_nt(dp_ab, w_ab_in, 0, out_dtype=F32, name="ab_in_bwd", epilogue="rms_bwd",
                                extra=(h0, mix_norm_g[0:1], dh1))
    sums_g = _pair_sums([g_ab_in], pos, "g")
    r_ab_in, = _finish_reduce(sums_g, _exchange_chips(sums_g), pos, "g")

    big_out = {
        "ab_w_in": _adam_big(ab_w_in, m_ab_w_in, v_ab_w_in, [r_ab_in], name="adam_ab_w_in"),
        "ab_w_out": _adam_big(ab_w_out, m_ab_w_out, v_ab_w_out, [r_ab_out], name="adam_ab_w_out"),
        "cd_w_in": _adam_big(cd_w_in, m_cd_w_in, v_cd_w_in, [r_cd_in], name="adam_cd_w_in"),
        "cd_w_out": _adam_big(cd_w_out, m_cd_w_out, v_cd_w_out, [r_cd_out], name="adam_cd_w_out"),
        "mlp_w1": _adam_big(mlp_w1, m_mlp_w1, v_mlp_w1, [r_w1_0, r_w1_1], name="adam_mlp_w1"),
        "mlp_w2": _adam_big(mlp_w2, m_mlp_w2, v_mlp_w2, [r_w2_0, r_w2_1], name="adam_mlp_w2"),
    }

    wide = jnp.concatenate([_pad_rows(jnp.concatenate([dg_mix0, dg_mix1], axis=0)),
                            _pad_rows(jnp.concatenate([dg_mlp0, dg_mlp1], axis=0)), _pad_rows(dg_final)], axis=0)
    mid = jnp.concatenate([_pad_rows(dconv_b), _pad_rows(dconv_w), _pad_rows(dsgu_g), _pad_rows(dsgu_b)], axis=0)
    narrow = jnp.concatenate(
        [dpool_w.reshape(4 * TILE, TILE), dsgu_w.reshape(4 * TILE, TILE), _pad_rows(dpool_scale.reshape(4, TILE)),
         _pad_rows(dsgu_bs[:, :, 0]), loss_tile], axis=0)
    wide, mid, narrow = _allreduce_small([wide, mid, narrow])
    small_out = _adam_small(wide, mid, narrow, {
        "mix_norm_g": (mix_norm_g, m_mix_norm_g, v_mix_norm_g),
        "mlp_norm_g": (mlp_norm_g, m_mlp_norm_g, v_mlp_norm_g),
        "final_norm_g": tuple(a.reshape(1, d) for a in (final_norm_g, m_final_norm_g, v_final_norm_g)),
        "conv_b": (conv_b, m_conv_b, v_conv_b),
        "conv_w": (conv_w, m_conv_w, v_conv_w),
        "sgu_norm_g": (sgu_norm_g, m_sgu_norm_g, v_sgu_norm_g),
        "sgu_norm_b": (sgu_norm_b, m_sgu_norm_b, v_sgu_norm_b),
        "pool_w": (pool_w, m_pool_w, v_pool_w),
        "pool_scale": (pool_scale, m_pool_scale, v_pool_scale),
        "sgu_w": (sgu_w, m_sgu_w, v_sgu_w),
        "sgu_b": (sgu_b, m_sgu_b, v_sgu_b),
    })
    small_out["final_norm_g"] = [a.reshape(d) for a in small_out["final_norm_g"]]

    order = ["mix_norm_g", "mlp_norm_g", "ab_w_in", "pool_w", "pool_scale", "conv_w", "conv_b", "ab_w_out",
             "cd_w_in", "sgu_norm_g", "sgu_norm_b", "sgu_w", "sgu_b", "cd_w_out", "mlp_w1", "mlp_w2",
             "final_norm_g"]
    both = {**big_out, **small_out}
    loss = narrow[LOSS_ROW, 0]
    outs = [loss, grad_x.reshape(nseq, t_len, d)]
    for kind in range(4):
        outs += [both[name][kind] for name in order]
    return tuple(outs)
```

```python
import math

import jax
import jax.numpy as jnp
from jax import lax
from jax.experimental import pallas as pl
from jax.experimental.pallas import tpu as pltpu

F32 = jnp.float32
BF16 = jnp.bfloat16
MESH = pl.DeviceIdType.MESH

D_MODEL = 1024
EPS = 1e-6
TILE = 128
N_CHIP = 4
N_DEV = 8
VMEM_LIMIT_BYTES = 56 * 1024 * 1024

ADAM_LR = 0.001
ADAM_B1 = 0.9
ADAM_B2 = 0.999
ADAM_EPS = 1e-08
ADAM_WD = 0.01
ADAM_STEP = 10

NT_DIMS = (((1,), (1,)), ((), ()))
TN_DIMS = (((0,), (0,)), ((), ()))


def _params(sem=None):
    return pltpu.CompilerParams(dimension_semantics=sem, vmem_limit_bytes=VMEM_LIMIT_BYTES)


def _call(body, *, name, grid, in_specs, out_specs, out_shape, scratch_shapes, semantics, args, rider=None):
    if rider is None:
        res = pl.pallas_call(body, name=name, grid=grid, in_specs=in_specs, out_specs=out_specs, out_shape=out_shape,
                             scratch_shapes=scratch_shapes, compiler_params=_params(semantics))(*args)
        return list(res), []
    kind, arrays = rider
    gather = kind == "gather"
    nr, n_in, n_out, n_scr = len(arrays), len(in_specs), len(out_specs), len(scratch_shapes)
    first_out, first_scr = n_in + nr, n_in + nr + n_out + nr
    last_step = math.prod(grid) - 1

    def riding(*refs):
        r_in = refs[n_in:first_out]
        r_out = refs[first_out + n_out:first_scr]
        sems = refs[first_scr + n_scr:]
        step = 0
        for axis, size in enumerate(grid):
            step = step * size + pl.program_id(axis)
        if gather:
            send, forward, finish = _gather_steps(r_out, *sems)
            pl.when(step == 0)(send)
            pl.when(step == last_step)(forward)
        else:
            send, finish = _exchange_steps(r_in, r_out, *sems)
            pl.when(step == 0)(send)
        body(*refs[:n_in], *refs[first_out:first_out + n_out], *refs[first_scr:first_scr + n_scr])
        pl.when(step == last_step)(finish)

    res = pl.pallas_call(
        riding, name=name, grid=grid, in_specs=list(in_specs) + [ANY] * nr, out_specs=list(out_specs) + [ANY] * nr,
        out_shape=list(out_shape) + ([jax.ShapeDtypeStruct(a.shape, a.dtype) for a in arrays] if gather
                                     else _exchange_shapes(arrays)),
        scratch_shapes=list(scratch_shapes) + (_gather_sems(nr) if gather else _exchange_sems(nr)),
        input_output_aliases={n_in + a: n_out + a for a in range(nr)} if gather else {},
        compiler_params=pltpu.CompilerParams(dimension_semantics=("arbitrary",) * len(grid),
                                             vmem_limit_bytes=VMEM_LIMIT_BYTES, has_side_effects=True),
    )(*args, *arrays)
    return list(res[:n_out]), list(res[n_out:])


def _row_tile(k_dim):
    return 1024 if k_dim <= 1024 else 512


def _mm_nn(a, b4, layer, *, out_dtype, name, epilogue=None, extra=None, norm_g=None, rider=None):
    m, k_dim = a.shape
    _, s_dim, kb, n = b4.shape
    assert kb == k_dim
    tm, tn = _row_tile(k_dim), min(n, 1024)
    assert m % tm == 0 and n % tn == 0
    npb = n // tn
    grid = (m // tm, s_dim * npb)
    n_in = 2 + (extra is not None) + (norm_g is not None)
    n_out = 1 + (norm_g is not None)
    assert norm_g is None or tn == s_dim * n

    def body(*refs):
        a_ref, b_ref = refs[:2]
        e_ref = refs[2] if extra is not None else None
        g_ref = refs[n_in - 1] if norm_g is not None else None
        o_ref = refs[n_in]
        acc = jnp.dot(a_ref[...], b_ref[...], preferred_element_type=F32)
        if epilogue == "relu2":
            r = jnp.maximum(acc, 0.0)
            acc = r * r
        elif epilogue == "residual":
            acc = acc + e_ref[...]
        o_ref[...] = acc.astype(out_dtype)
        if norm_g is not None:
            rstd = lax.rsqrt(jnp.mean(acc * acc, axis=-1, keepdims=True) + EPS)
            refs[n_in + 1][...] = (acc * rstd * g_ref[...]).astype(BF16)

    in_specs = [
        pl.BlockSpec((tm, k_dim), lambda i, j: (i, 0)),
        pl.BlockSpec((None, None, k_dim, tn), lambda i, j: (layer, j // npb, 0, j % npb)),
    ]
    args = [a, b4]
    if extra is not None:
        in_specs.append(pl.BlockSpec((tm, tn), lambda i, j: (i, j)))
        args.append(extra)
    out_block = pl.BlockSpec((tm, tn), lambda i, j: (i, j))
    out_specs, out_shape = [out_block], [jax.ShapeDtypeStruct((m, s_dim * n), out_dtype)]
    if norm_g is not None:
        in_specs.append(pl.BlockSpec((1, tn), lambda i, j: (0, j)))
        args.append(norm_g)
        out_specs.append(out_block)
        out_shape.append(jax.ShapeDtypeStruct((m, s_dim * n), BF16))
    res, rode = _call(
        body, name=name, grid=grid, in_specs=in_specs, out_specs=out_specs, out_shape=out_shape,
        scratch_shapes=[], semantics=("parallel", "parallel"), args=args, rider=rider)
    res = res[0] if norm_g is None else res
    return res if rider is None else (res, rode)


def _mm_nt(a, b4, layer, *, out_dtype, name, epilogue=None, extra=None, rider=None):
    m, k_dim = a.shape
    _, s_dim, n_out, n = b4.shape
    assert k_dim == s_dim * n
    tm, tn = _row_tile(k_dim), min(n_out, 1024)
    assert m % tm == 0 and n_out % tn == 0
    grid = (m // tm, n_out // tn)
    rms = epilogue == "rms_bwd"
    assert not rms or tn == n_out
    extras = [] if extra is None else (list(extra) if rms else [extra])
    n_in = 2 + len(extras)
    n_res = 3 if rms else 1

    def body(*refs):
        a_ref, b_ref = refs[:2]
        e_refs = refs[2:n_in]
        o_ref = refs[n_in]
        acc = lax.dot_general(a_ref[:, 0:n], b_ref[0], NT_DIMS, preferred_element_type=F32)
        for s in range(1, s_dim):
            acc = acc + lax.dot_general(a_ref[:, s * n:(s + 1) * n], b_ref[s], NT_DIMS, preferred_element_type=F32)
        if epilogue == "relu2_bwd":
            acc = acc * (2.0 * jnp.sqrt(e_refs[0][...].astype(F32)))
        if not rms:
            o_ref[...] = acc.astype(out_dtype)
        else:
            h_ref, g_ref, dres_ref = e_refs
            dhb_ref, dg_ref = refs[n_in + 1:n_in + 3]
            hv = h_ref[...]
            rstd = lax.rsqrt(jnp.mean(hv * hv, axis=-1, keepdims=True) + EPS)
            xhat = hv * rstd
            dxhat = acc * g_ref[...]
            dh = dres_ref[...] + rstd * (dxhat - xhat * jnp.mean(dxhat * xhat, axis=-1, keepdims=True))
            o_ref[...] = dh
            dhb_ref[...] = dh.astype(BF16)
            dg_part = jnp.sum(acc * xhat, axis=0, keepdims=True)
            first = pl.program_id(0) == 0

            @pl.when(first)
            def _():
                dg_ref[...] = dg_part

            @pl.when(jnp.logical_not(first))
            def _():
                dg_ref[...] += dg_part

    in_specs = [
        pl.BlockSpec((tm, k_dim), lambda i, j: (i, 0)),
        pl.BlockSpec((None, s_dim, tn, n), lambda i, j: (layer, 0, j, 0)),
    ]
    args = [a, b4] + extras
    block = pl.BlockSpec((tm, tn), lambda i, j: (i, j))
    vec = pl.BlockSpec((1, tn), lambda i, j: (0, j))
    if rms:
        in_specs += [block, vec, block]
        out_specs = [block, block, vec]
        out_shape = [jax.ShapeDtypeStruct((m, n_out), F32), jax.ShapeDtypeStruct((m, n_out), BF16),
                     jax.ShapeDtypeStruct((1, n_out), F32)]
    else:
        in_specs += [block] * len(extras)
        out_specs, out_shape = [block], [jax.ShapeDtypeStruct((m, n_out), out_dtype)]
    res, rode = _call(
        body, name=name, grid=grid, in_specs=in_specs, out_specs=out_specs, out_shape=out_shape,
        scratch_shapes=[], semantics=("arbitrary",) * 2 if rms else ("parallel", "parallel"), args=args, rider=rider)
    res = res if rms else res[0]
    return res if rider is None else (res, rode)


def _mm_tn(a, b, s_dim, *, name, rider=None):
    m, k1 = a.shape
    mb, n_all = b.shape
    assert mb == m and n_all % s_dim == 0
    n = n_all // s_dim
    tn, t1 = min(n, 1024), _row_tile(m)
    assert k1 % t1 == 0 and n % tn == 0
    npb = n // tn
    grid = (k1 // t1, s_dim * npb)

    def body(a_ref, b_ref, o_ref):
        o_ref[...] = lax.dot_general(a_ref[...], b_ref[...], TN_DIMS, preferred_element_type=F32).astype(BF16)

    res, rode = _call(
        body, name=name, grid=grid,
        in_specs=[pl.BlockSpec((m, t1), lambda i, j: (0, i)), pl.BlockSpec((m, tn), lambda i, j: (0, j))],
        out_specs=[pl.BlockSpec((None, None, t1, tn), lambda i, j: (0, j // npb, i, j % npb))],
        out_shape=[jax.ShapeDtypeStruct((1, s_dim, k1, n), BF16)],
        scratch_shapes=[], semantics=("parallel", "parallel"), args=[a, b], rider=rider)
    return res[0] if rider is None else (res[0], rode)


ROW_TILE = 512


def _rms_fwd(h, g, *, name):
    m, d = h.shape

    def body(h_ref, g_ref, o_ref):
        hv = h_ref[...]
        rstd = lax.rsqrt(jnp.mean(hv * hv, axis=-1, keepdims=True) + EPS)
        o_ref[...] = (hv * rstd * g_ref[...]).astype(BF16)

    return pl.pallas_call(
        body, name=name, grid=(m // ROW_TILE,),
        in_specs=[pl.BlockSpec((ROW_TILE, d), lambda i: (i, 0)), pl.BlockSpec((1, d), lambda i: (0, 0))],
        out_specs=pl.BlockSpec((ROW_TILE, d), lambda i: (i, 0)),
        out_shape=jax.ShapeDtypeStruct((m, d), BF16),
        compiler_params=_params(("parallel",)),
    )(h, g)


def _final_loss(h, g, target):
    m, d = h.shape

    def body(h_ref, g_ref, t_ref, dh_ref, dhb_ref, dg_ref, loss_ref):
        hv = h_ref[...]
        gv = g_ref[...]
        rstd = lax.rsqrt(jnp.mean(hv * hv, axis=-1, keepdims=True) + EPS)
        xhat = hv * rstd
        err = xhat * gv - t_ref[...]
        dy = err * (1.0 / d)
        dxhat = dy * gv
        dh = rstd * (dxhat - xhat * jnp.mean(dxhat * xhat, axis=-1, keepdims=True))
        dh_ref[...] = dh
        dhb_ref[...] = dh.astype(BF16)
        dg_part = jnp.sum(dy * xhat, axis=0, keepdims=True)
        sq = jnp.sum(jnp.sum(err * err, axis=1, keepdims=True), axis=0, keepdims=True) * (0.5 / d)
        loss_part = jnp.broadcast_to(sq, (8, TILE))

        @pl.when(pl.program_id(0) == 0)
        def _():
            dg_ref[...] = dg_part
            loss_ref[...] = loss_part

        @pl.when(pl.program_id(0) > 0)
        def _():
            dg_ref[...] += dg_part
            loss_ref[...] += loss_part

    row = pl.BlockSpec((ROW_TILE, d), lambda i: (i, 0))
    vec = pl.BlockSpec((1, d), lambda i: (0, 0))
    return pl.pallas_call(
        body, name="final_loss", grid=(m // ROW_TILE,),
        in_specs=[row, vec, row],
        out_specs=[row, row, vec, pl.BlockSpec((8, TILE), lambda i: (0, 0))],
        out_shape=[jax.ShapeDtypeStruct((m, d), F32), jax.ShapeDtypeStruct((m, d), BF16),
                   jax.ShapeDtypeStruct((1, d), F32), jax.ShapeDtypeStruct((8, TILE), F32)],
        compiler_params=_params(("arbitrary",)),
    )(h, g, target)


def _shift_down(x, s, t_idx):
    return jnp.where(t_idx >= s, pltpu.roll(x, s, 0), 0.0)


def _shift_up(x, s, t_idx, t_len):
    return jnp.where(t_idx < t_len - s, pltpu.roll(x, t_len - s, 0), 0.0)


def _pool_select(group, s2, s4, s8, s16):
    return jnp.where(group == 0, s2, jnp.where(group == 1, s4, jnp.where(group == 2, s8, s16)))


def _pool_count(group, t_idx):
    win = jnp.left_shift(2, group)
    return jnp.minimum(t_idx + 1, win).astype(F32)


def _pool_fwd_math(a, group, t_idx):
    s2 = a + _shift_down(a, 1, t_idx)
    s4 = s2 + _shift_down(s2, 2, t_idx)
    s8 = s4 + _shift_down(s4, 4, t_idx)
    s16 = s8 + _shift_down(s8, 8, t_idx)
    return _pool_select(group, s2, s4, s8, s16) / _pool_count(group, t_idx) - a


def _pool_bwd_math(dpooled, group, t_idx, t_len):
    e = dpooled / _pool_count(group, t_idx)
    s2 = e + _shift_up(e, 1, t_idx, t_len)
    s4 = s2 + _shift_up(s2, 2, t_idx, t_len)
    s8 = s4 + _shift_up(s4, 4, t_idx, t_len)
    s16 = s8 + _shift_up(s8, 8, t_idx, t_len)
    return _pool_select(group, s2, s4, s8, s16) - dpooled


def _conv_fwd_math(c, w_ref, b_ref, t_idx):
    return (w_ref[0:1, :] * _shift_down(c, 2, t_idx) + w_ref[1:2, :] * _shift_down(c, 1, t_idx)
            + w_ref[2:3, :] * c + b_ref[...])


def _ab_fwd(p, pool_w, pool_scale, conv_w, conv_b, nseq, t_len):
    m = p.shape[0]
    ng = 4

    def body(a_ref, xb_ref, gb_ref, gc_ref, pw_ref, ps_ref, cw_ref, cb_ref, o_ref):
        j = pl.program_id(1)
        t_idx = lax.broadcasted_iota(jnp.int32, (t_len, TILE), 0)

        @pl.when(j < ng)
        def _():
            pooled = _pool_fwd_math(a_ref[...].astype(F32), j, t_idx)
            mixed = jnp.dot(pooled.astype(BF16), pw_ref[...].astype(BF16), preferred_element_type=F32)
            o_ref[...] = (mixed * ps_ref[...]).astype(BF16)

        @pl.when(j >= ng)
        def _():
            c = gc_ref[...].astype(F32) * xb_ref[...].astype(F32)
            y = _conv_fwd_math(c, cw_ref, cb_ref, t_idx)
            o_ref[...] = (gb_ref[...].astype(F32) * y).astype(BF16)

    def pool_j(j):
        return jnp.minimum(j, ng - 1)

    def conv_j(j):
        return jnp.maximum(j - ng, 0)

    in_specs = [
        pl.BlockSpec((t_len, TILE), lambda s, j: (s, pool_j(j))),
        pl.BlockSpec((t_len, TILE), lambda s, j: (s, ng + conv_j(j))),
        pl.BlockSpec((t_len, TILE), lambda s, j: (s, 2 * ng + conv_j(j))),
        pl.BlockSpec((t_len, TILE), lambda s, j: (s, 3 * ng + conv_j(j))),
        pl.BlockSpec((None, TILE, TILE), lambda s, j: (pool_j(j), 0, 0)),
        pl.BlockSpec((None, 1, TILE), lambda s, j: (pool_j(j), 0, 0)),
        pl.BlockSpec((3, TILE), lambda s, j: (0, conv_j(j))),
        pl.BlockSpec((1, TILE), lambda s, j: (0, conv_j(j))),
    ]
    return pl.pallas_call(
        body, name="ab_mixer_fwd", grid=(nseq, 2 * ng), in_specs=in_specs,
        out_specs=pl.BlockSpec((t_len, TILE), lambda s, j: (s, j)),
        out_shape=jax.ShapeDtypeStruct((m, 2 * ng * TILE), BF16),
        compiler_params=_params(("parallel", "arbitrary")),
    )(p, p, p, p, pool_w, pool_scale, conv_w, conv_b)


def _ab_bwd(p, dmix, pool_w, pool_scale, conv_w, conv_b, nseq, t_len, rider=None):
    m = p.shape[0]
    ng = 4

    def body(a_ref, xb_ref, gb_ref, gc_ref, dma_ref, dmb_ref, pw_ref, ps_ref, cw_ref, cb_ref,
             da_ref, dxb_ref, dgb_ref, dgc_ref, dpw_ref, dps_ref, dcw_ref, dcb_ref):
        j = pl.program_id(0)
        first = pl.program_id(1) == 0
        t_idx = lax.broadcasted_iota(jnp.int32, (t_len, TILE), 0)

        pooled = _pool_fwd_math(a_ref[...].astype(F32), j, t_idx).astype(BF16)
        w_bf = pw_ref[...].astype(BF16)
        mixed = jnp.dot(pooled, w_bf, preferred_element_type=F32)
        dm = dma_ref[...].astype(F32)
        dps = jnp.sum(dm * mixed, axis=0, keepdims=True)
        dmixed = (dm * ps_ref[...]).astype(BF16)
        dpw = lax.dot_general(pooled, dmixed, TN_DIMS, preferred_element_type=F32)
        dpooled = lax.dot_general(dmixed, w_bf, NT_DIMS, preferred_element_type=F32)
        da_ref[...] = _pool_bwd_math(dpooled, j, t_idx, t_len).astype(BF16)

        xb = xb_ref[...].astype(F32)
        gb = gb_ref[...].astype(F32)
        gc = gc_ref[...].astype(F32)
        d = dmb_ref[...].astype(F32)
        c = gc * xb
        c1 = _shift_down(c, 1, t_idx)
        c2 = _shift_down(c, 2, t_idx)
        y = cw_ref[0:1, :] * c2 + cw_ref[1:2, :] * c1 + cw_ref[2:3, :] * c + cb_ref[...]
        dgb_ref[...] = (d * y).astype(BF16)
        dy = d * gb
        dc = (cw_ref[2:3, :] * dy + cw_ref[1:2, :] * _shift_up(dy, 1, t_idx, t_len)
              + cw_ref[0:1, :] * _shift_up(dy, 2, t_idx, t_len))
        dgc_ref[...] = (dc * xb).astype(BF16)
        dxb_ref[...] = (dc * gc).astype(BF16)
        dcw = jnp.concatenate([jnp.sum(dy * c2, axis=0, keepdims=True),
                               jnp.sum(dy * c1, axis=0, keepdims=True),
                               jnp.sum(dy * c, axis=0, keepdims=True)], axis=0)
        dcb = jnp.sum(dy, axis=0, keepdims=True)

        @pl.when(first)
        def _():
            dpw_ref[...] = dpw
            dps_ref[...] = dps
            dcw_ref[...] = dcw
            dcb_ref[...] = dcb

        @pl.when(jnp.logical_not(first))
        def _():
            dpw_ref[...] += dpw
            dps_ref[...] += dps
            dcw_ref[...] += dcw
            dcb_ref[...] += dcb

    def col(k):
        return pl.BlockSpec((t_len, TILE), lambda j, s: (s, k * ng + j))

    in_specs = [
        col(0), col(1), col(2), col(3), col(0), col(1),
        pl.BlockSpec((None, TILE, TILE), lambda j, s: (j, 0, 0)),
        pl.BlockSpec((None, 1, TILE), lambda j, s: (j, 0, 0)),
        pl.BlockSpec((3, TILE), lambda j, s: (0, j)),
        pl.BlockSpec((1, TILE), lambda j, s: (0, j)),
    ]
    piece = pl.BlockSpec((t_len, TILE), lambda j, s: (s, j))
    out_specs = [
        piece, piece, piece, piece,
        pl.BlockSpec((None, TILE, TILE), lambda j, s: (j, 0, 0)),
        pl.BlockSpec((None, 1, TILE), lambda j, s: (j, 0, 0)),
        pl.BlockSpec((3, TILE), lambda j, s: (0, j)),
        pl.BlockSpec((1, TILE), lambda j, s: (0, j)),
    ]
    w = ng * TILE
    out_shape = [jax.ShapeDtypeStruct((m, w), BF16)] * 4 + [
        jax.ShapeDtypeStruct((ng, TILE, TILE), F32), jax.ShapeDtypeStruct((ng, 1, TILE), F32),
        jax.ShapeDtypeStruct((3, w), F32), jax.ShapeDtypeStruct((1, w), F32)]
    res, rode = _call(
        body, name="ab_mixer_bwd", grid=(ng, nseq), in_specs=in_specs, out_specs=out_specs, out_shape=out_shape,
        scratch_shapes=[], semantics=("parallel", "arbitrary"),
        args=[p, p, p, p, dmix, dmix, pool_w, pool_scale, conv_w, conv_b], rider=rider)
    return res if rider is None else (res, rode)


SGU_ROWS = 512
INV_SQRT2 = 1.0 / math.sqrt(2.0)
INV_SQRT_2PI = 1.0 / math.sqrt(2.0 * math.pi)


def _gelu(x):
    return 0.5 * x * (1.0 + lax.erf(x * INV_SQRT2))


def _gelu_grad(x):
    return 0.5 * (1.0 + lax.erf(x * INV_SQRT2)) + x * (INV_SQRT_2PI * jnp.exp(-0.5 * x * x))


def _causal_tile(transposed=False):
    r = lax.broadcasted_iota(jnp.int32, (TILE, TILE), 0)
    c = lax.broadcasted_iota(jnp.int32, (TILE, TILE), 1)
    return r <= c if transposed else c <= r


def _sgu_norm(v, g_ref, b_ref):
    mu = jnp.mean(v, axis=-1, keepdims=True)
    xc = v - mu
    rstd = lax.rsqrt(jnp.mean(xc * xc, axis=-1, keepdims=True) + EPS)
    xhat = xc * rstd
    return xhat, rstd, xhat * g_ref[...] + b_ref[...]


def _sgu_fwd(p, norm_g, norm_b, w_s, bias_tile):
    m = p.shape[0]
    ng = 4
    width = ng * TILE

    def body(u_ref, v_ref, g_ref, b_ref, w_ref, bias_ref, o_ref):
        u = _gelu(u_ref[...].astype(F32))
        _, _, vln = _sgu_norm(_gelu(v_ref[...].astype(F32)), g_ref, b_ref)
        vln = vln.astype(BF16)
        causal = _causal_tile()
        for g in range(ng):
            cols = slice(g * TILE, (g + 1) * TILE)
            wg = jnp.where(causal, w_ref[g], 0.0).astype(BF16)
            for n in range(SGU_ROWS // TILE):
                rows = slice(n * TILE, (n + 1) * TILE)
                s = jnp.dot(wg, vln[rows, cols], preferred_element_type=F32) + bias_ref[g]
                o_ref[rows, cols] = (u[rows, cols] * s).astype(BF16)

    vec = pl.BlockSpec((1, width), lambda i: (0, 0))
    tiles = pl.BlockSpec((ng, TILE, TILE), lambda i: (0, 0, 0))
    return pl.pallas_call(
        body, name="sgu_fwd", grid=(m // SGU_ROWS,),
        in_specs=[pl.BlockSpec((SGU_ROWS, width), lambda i: (i, 0)),
                  pl.BlockSpec((SGU_ROWS, width), lambda i: (i, 1)), vec, vec, tiles, tiles],
        out_specs=pl.BlockSpec((SGU_ROWS, width), lambda i: (i, 0)),
        out_shape=jax.ShapeDtypeStruct((m, width), BF16),
        compiler_params=_params(("parallel",)),
    )(p, p, norm_g, norm_b, w_s, bias_tile)


def _sgu_bwd(p, dmix, norm_g, norm_b, w_s, w_s_t, bias_tile):
    m = p.shape[0]
    ng = 4
    width = ng * TILE

    def body(u_ref, v_ref, dc_ref, g_ref, b_ref, w_ref, wt_ref, bias_ref,
             du_ref, dv_ref, dw_ref, dbs_ref, dg_ref, db_ref, ds_scr, dvln_scr):
        u_pre = u_ref[...].astype(F32)
        v_pre = v_ref[...].astype(F32)
        u = _gelu(u_pre)
        xhat, rstd, vln = _sgu_norm(_gelu(v_pre), g_ref, b_ref)
        vln = vln.astype(BF16)
        dc = dc_ref[...].astype(F32)
        causal = _causal_tile()
        ones = jnp.ones((TILE, TILE), BF16)
        first = pl.program_id(0) == 0
        for g in range(ng):
            cols = slice(g * TILE, (g + 1) * TILE)
            wg = jnp.where(causal, w_ref[g], 0.0).astype(BF16)
            wgt = jnp.where(_causal_tile(transposed=True), wt_ref[g], 0.0).astype(BF16)
            dw_acc = jnp.zeros((TILE, TILE), F32)
            dbs_acc = jnp.zeros((TILE, TILE), F32)
            for n in range(SGU_ROWS // TILE):
                rows = slice(n * TILE, (n + 1) * TILE)
                vt = vln[rows, cols]
                s = jnp.dot(wg, vt, preferred_element_type=F32) + bias_ref[g]
                ds_scr[rows, cols] = dc[rows, cols] * s
                ds = (dc[rows, cols] * u[rows, cols]).astype(BF16)
                dw_acc += lax.dot_general(ds, vt, NT_DIMS, preferred_element_type=F32)
                dbs_acc += jnp.dot(ds, ones, preferred_element_type=F32)
                dvln_scr[rows, cols] = jnp.dot(wgt, ds, preferred_element_type=F32)
            dw_g = jnp.where(causal, dw_acc, 0.0)

            @pl.when(first)
            def _():
                dw_ref[g] = dw_g
                dbs_ref[g] = dbs_acc

            @pl.when(jnp.logical_not(first))
            def _():
                dw_ref[g] += dw_g
                dbs_ref[g] += dbs_acc

        du_ref[...] = (ds_scr[...] * _gelu_grad(u_pre)).astype(BF16)
        dvln = dvln_scr[...]
        dxhat = dvln * g_ref[...]
        dv = rstd * (dxhat - jnp.mean(dxhat, axis=-1, keepdims=True)
                     - xhat * jnp.mean(dxhat * xhat, axis=-1, keepdims=True))
        dv_ref[...] = (dv * _gelu_grad(v_pre)).astype(BF16)
        dg_part = jnp.sum(dvln * xhat, axis=0, keepdims=True)
        db_part = jnp.sum(dvln, axis=0, keepdims=True)

        @pl.when(first)
        def _():
            dg_ref[...] = dg_part
            db_ref[...] = db_part

        @pl.when(jnp.logical_not(first))
        def _():
            dg_ref[...] += dg_part
            db_ref[...] += db_part

    vec = pl.BlockSpec((1, width), lambda i: (0, 0))
    tiles = pl.BlockSpec((ng, TILE, TILE), lambda i: (0, 0, 0))
    rows0 = pl.BlockSpec((SGU_ROWS, width), lambda i: (i, 0))
    rows1 = pl.BlockSpec((SGU_ROWS, width), lambda i: (i, 1))
    return pl.pallas_call(
        body, name="sgu_bwd", grid=(m // SGU_ROWS,),
        in_specs=[rows0, rows1, rows0, vec, vec, tiles, tiles, tiles],
        out_specs=[rows0, rows0, tiles, tiles, vec, vec],
        out_shape=[jax.ShapeDtypeStruct((m, width), BF16), jax.ShapeDtypeStruct((m, width), BF16),
                   jax.ShapeDtypeStruct((ng, TILE, TILE), F32), jax.ShapeDtypeStruct((ng, TILE, TILE), F32),
                   jax.ShapeDtypeStruct((1, width), F32), jax.ShapeDtypeStruct((1, width), F32)],
        scratch_shapes=[pltpu.VMEM((SGU_ROWS, width), F32), pltpu.VMEM((SGU_ROWS, width), F32)],
        compiler_params=_params(("arbitrary",)),
    )(p, p, dmix, norm_g, norm_b, w_s, w_s_t, bias_tile)


SB_DH = 64
SB_SCALE = 1.0 / math.sqrt(SB_DH)


SB_BLOCK = 256
SB_SUB = SB_BLOCK // TILE


def _sum_matrix(kind):
    j = lax.broadcasted_iota(jnp.int32, (TILE, 2 * TILE), 0)
    s = lax.broadcasted_iota(jnp.int32, (TILE, 2 * TILE), 1)
    tri = {"after": j > s, "upto": j <= s, "before": j < s}[kind]
    return jnp.where(jnp.logical_or(s >= TILE, tri), 1.0, 0.0).astype(BF16)


def _strict_mask():
    r = lax.broadcasted_iota(jnp.int32, (SB_BLOCK, SB_BLOCK), 0)
    c = lax.broadcasted_iota(jnp.int32, (SB_BLOCK, SB_BLOCK), 1)
    return c < r


def _head_lanes(h):
    lane = lax.broadcasted_iota(jnp.int32, (1, TILE), 1)
    return (lane >= h * SB_DH) & (lane < (h + 1) * SB_DH)


def _softplus(z):
    return jnp.maximum(z, 0.0) + jnp.log(1.0 + jnp.exp(-jnp.abs(z)))


def _sb_fwd(p, nseq, t_len, gather):
    m = p.shape[0]
    npair = 4
    ng = len(gather)
    last_step = nseq * npair - 1

    def body(q_ref, k_ref, v_ref, *rest):
        o_ref, lt_ref = rest[ng:ng + 2]
        kh_ref, vh_ref = rest[2 * ng + 2:2 * ng + 4]
        step = pl.program_id(0) * npair + pl.program_id(1)
        send, forward, finish = _gather_steps(rest[ng + 2:2 * ng + 2], *rest[2 * ng + 4:])
        pl.when(step == 0)(send)
        pl.when(step == (last_step + 1) // 2)(forward)
        for h in range(2):
            keep = _head_lanes(h)
            kh_ref[h] = jnp.where(keep, k_ref[...], 0).astype(BF16)
            vh_ref[h] = jnp.where(keep, v_ref[...], 0).astype(BF16)
        summat = _sum_matrix("after")
        strict = _strict_mask()

        def one_pass(q, row0, diag, state):
            rows = pl.ds(row0, SB_BLOCK)
            z, sp, pieces = [], [], []
            for h in range(2):
                zh = lax.dot_general(q, kh_ref[h, rows, :], NT_DIMS, preferred_element_type=F32) * SB_SCALE
                sph = _softplus(zh)
                logkeep = jnp.where(strict, -sph, 0.0) if diag else -sph
                z.append(zh)
                sp.append(sph)
                pieces += [logkeep[:, b * TILE:(b + 1) * TILE] for b in range(SB_SUB)]
            sums = jnp.dot(jnp.concatenate(pieces, axis=0).astype(BF16), summat, preferred_element_type=F32)
            out = []
            for h in range(2):
                carry, acc = state[2 * h], state[2 * h + 1]
                after = [None] * SB_SUB
                for b in reversed(range(SB_SUB)):
                    part = sums[(h * SB_SUB + b) * SB_BLOCK:(h * SB_SUB + b + 1) * SB_BLOCK]
                    after[b] = part[:, :TILE] + carry
                    carry = carry + part[:, TILE:]
                w = jnp.exp(z[h] - sp[h] + jnp.concatenate(after, axis=1))
                if diag:
                    w = jnp.where(strict, w, 0.0)
                out += [carry, acc + jnp.dot(w.astype(BF16), vh_ref[h, rows, :], preferred_element_type=F32)]
            return tuple(out)

        def q_block(i, _):
            r0 = pl.multiple_of(i * SB_BLOCK, SB_BLOCK)
            q = q_ref[pl.ds(r0, SB_BLOCK), :]
            zero = jnp.zeros((SB_BLOCK, TILE), F32)
            state = one_pass(q, r0, True, (zero,) * 4)
            state = lax.fori_loop(
                0, i, lambda jj, st: one_pass(q, pl.multiple_of((i - 1 - jj) * SB_BLOCK, SB_BLOCK), False, st), state)
            o_ref[pl.ds(r0, SB_BLOCK), :] = (state[1] + state[3]).astype(BF16)
            lt_ref[pl.ds(r0, SB_BLOCK), :] = jnp.where(_head_lanes(0), state[0], state[2])
            return 0

        lax.fori_loop(0, t_len // SB_BLOCK, q_block, 0)
        pl.when(step == last_step)(finish)

    def col(k):
        return pl.BlockSpec((t_len, TILE), lambda s, hp: (s, k * npair + hp))

    out = pl.BlockSpec((t_len, TILE), lambda s, hp: (s, hp))
    res = pl.pallas_call(
        body, name="stickbreak_fwd", grid=(nseq, npair), in_specs=[col(2), col(3), col(4)] + [ANY] * ng,
        out_specs=[out, out] + [ANY] * ng,
        out_shape=[jax.ShapeDtypeStruct((m, npair * TILE), BF16), jax.ShapeDtypeStruct((m, npair * TILE), F32)]
        + [jax.ShapeDtypeStruct(b.shape, b.dtype) for b in gather],
        input_output_aliases={3 + a: 2 + a for a in range(ng)},
        scratch_shapes=[pltpu.VMEM((2, t_len, TILE), BF16), pltpu.VMEM((2, t_len, TILE), BF16)] + _gather_sems(ng),
        compiler_params=pltpu.CompilerParams(dimension_semantics=("arbitrary", "arbitrary"),
                                             vmem_limit_bytes=VMEM_LIMIT_BYTES, has_side_effects=True),
    )(p, p, p, *gather)
    return res[0], res[1], res[2:]


def _sb_bwd(p, dmix, ltot, nseq, t_len, exchange):
    m = p.shape[0]
    npair = 4
    ne = len(exchange)
    last_step = nseq * npair - 1

    def body(q_ref, k_ref, v_ref, do_ref, lt_ref, *rest):
        dq_ref, dk_ref, dv_ref = rest[ne:ne + 3]
        kh_ref, vh_ref, dk_acc, dv_acc = rest[2 * ne + 3:2 * ne + 7]
        step = pl.program_id(0) * npair + pl.program_id(1)
        send, finish = _exchange_steps(rest[:ne], rest[ne + 3:2 * ne + 3], *rest[2 * ne + 7:])
        pl.when(step == 0)(send)
        for h in range(2):
            keep = _head_lanes(h)
            kh_ref[h] = jnp.where(keep, k_ref[...], 0).astype(BF16)
            vh_ref[h] = jnp.where(keep, v_ref[...], 0).astype(BF16)
        dk_acc[...] = jnp.zeros_like(dk_acc)
        dv_acc[...] = jnp.zeros_like(dv_acc)
        sum_upto = _sum_matrix("upto")
        sum_before = _sum_matrix("before")
        strict = _strict_mask()
        lane = lax.broadcasted_iota(jnp.int32, (SB_BLOCK, TILE), 1)

        def running(x, matrix, start):
            pieces = [x[h][:, b * TILE:(b + 1) * TILE] for h in range(2) for b in range(SB_SUB)]
            sums = jnp.dot(jnp.concatenate(pieces, axis=0).astype(BF16), matrix, preferred_element_type=F32)
            wide, ends = [], []
            for h in range(2):
                total, cols = start[h], []
                for b in range(SB_SUB):
                    part = sums[(h * SB_SUB + b) * SB_BLOCK:(h * SB_SUB + b + 1) * SB_BLOCK]
                    cols.append(part[:, :TILE] + total)
                    total = total + part[:, TILE:]
                wide.append(jnp.concatenate(cols, axis=1))
                ends.append(total)
            return wide, ends

        def one_pass(q, do, qh, doh, ltot, row0, diag, state):
            rows = pl.ds(row0, SB_BLOCK)
            z, sp, logkeep = [], [], []
            for h in range(2):
                zh = lax.dot_general(q, kh_ref[h, rows, :], NT_DIMS, preferred_element_type=F32) * SB_SCALE
                sph = _softplus(zh)
                z.append(zh)
                sp.append(sph)
                logkeep.append(jnp.where(strict, -sph, 0.0) if diag else -sph)
            upto, sum_l = running(logkeep, sum_upto, [state[0], state[3]])
            w, g = [], []
            for h in range(2):
                wh = jnp.exp(z[h] - sp[h] + (ltot[h] - upto[h]))
                if diag:
                    wh = jnp.where(strict, wh, 0.0)
                w.append(wh)
                g.append(wh * lax.dot_general(do, vh_ref[h, rows, :], NT_DIMS, preferred_element_type=F32))
            g_before, sum_g = running(g, sum_before, [state[1], state[4]])
            out, dk_new, dv_new = [], 0.0, 0.0
            for h in range(2):
                dz = (g[h] - jnp.exp(z[h] - sp[h]) * (g[h] + g_before[h])) * SB_SCALE
                if diag:
                    dz = jnp.where(strict, dz, 0.0)
                dzb = dz.astype(BF16)
                dq = state[3 * h + 2] + jnp.dot(dzb, kh_ref[h, rows, :], preferred_element_type=F32)
                dk_new = dk_new + lax.dot_general(dzb, qh[h], TN_DIMS, preferred_element_type=F32)
                dv_new = dv_new + lax.dot_general(w[h].astype(BF16), doh[h], TN_DIMS, preferred_element_type=F32)
                out += [sum_l[h], sum_g[h], dq]
            dk_acc[rows, :] += dk_new
            dv_acc[rows, :] += dv_new
            return tuple(out)

        def q_block(i, _):
            r0 = pl.multiple_of(i * SB_BLOCK, SB_BLOCK)
            q = q_ref[pl.ds(r0, SB_BLOCK), :]
            do = do_ref[pl.ds(r0, SB_BLOCK), :]
            lt = lt_ref[pl.ds(r0, SB_BLOCK), :]
            qh, doh, ltot = [], [], []
            for h in range(2):
                keep = _head_lanes(h)
                qh.append(jnp.where(keep, q, 0).astype(BF16))
                doh.append(jnp.where(keep, do, 0).astype(BF16))
                ltot.append(jnp.sum(jnp.where(lane == h * SB_DH, lt, 0.0), axis=1, keepdims=True))
            zero = jnp.zeros((SB_BLOCK, TILE), F32)
            state = lax.fori_loop(
                0, i,
                lambda jj, st: one_pass(q, do, qh, doh, ltot, pl.multiple_of(jj * SB_BLOCK, SB_BLOCK), False, st),
                (zero,) * 6)
            state = one_pass(q, do, qh, doh, ltot, r0, True, state)
            dq_ref[pl.ds(r0, SB_BLOCK), :] = (state[2] + state[5]).astype(BF16)
            return 0

        lax.fori_loop(0, t_len // SB_BLOCK, q_block, 0)
        dk_ref[...] = dk_acc[...].astype(BF16)
        dv_ref[...] = dv_acc[...].astype(BF16)
        pl.when(step == last_step)(finish)

    def col(k):
        return pl.BlockSpec((t_len, TILE), lambda s, hp: (s, k * npair + hp))

    out = pl.BlockSpec((t_len, TILE), lambda s, hp: (s, hp))
    width = npair * TILE
    res = pl.pallas_call(
        body, name="stickbreak_bwd", grid=(nseq, npair),
        in_specs=[col(2), col(3), col(4), col(1), out] + [ANY] * ne, out_specs=[out, out, out] + [ANY] * ne,
        out_shape=[jax.ShapeDtypeStruct((m, width), BF16)] * 3 + _exchange_shapes(exchange),
        scratch_shapes=[pltpu.VMEM((2, t_len, TILE), BF16), pltpu.VMEM((2, t_len, TILE), BF16),
                        pltpu.VMEM((t_len, TILE), F32), pltpu.VMEM((t_len, TILE), F32)] + _exchange_sems(ne),
        compiler_params=pltpu.CompilerParams(dimension_semantics=("arbitrary", "arbitrary"),
                                             vmem_limit_bytes=VMEM_LIMIT_BYTES, has_side_effects=True),
    )(p, p, p, dmix, ltot, *exchange)
    return res[0], res[1], res[2], res[3:]


def _adam_math(w, g, m, v):
    m = ADAM_B1 * m + (1.0 - ADAM_B1) * g
    v = ADAM_B2 * v + (1.0 - ADAM_B2) * (g * g)
    m_hat = m / (1.0 - ADAM_B1 ** ADAM_STEP)
    v_hat = v / (1.0 - ADAM_B2 ** ADAM_STEP)
    delta = -ADAM_LR * (m_hat / (jnp.sqrt(v_hat) + ADAM_EPS) + ADAM_WD * w)
    return delta, m, v


def _cast_place(w, layer, pos, *, name):
    _, r, c = w.shape
    tr = min(r, 256)

    def body(pos_ref, w_ref, o_ref):
        o_ref[...] = w_ref[...].astype(BF16)

    grid_spec = pltpu.PrefetchScalarGridSpec(
        num_scalar_prefetch=1, grid=(r // tr,),
        in_specs=[pl.BlockSpec((None, tr, c), lambda i, pos_ref: (layer, i, 0))],
        out_specs=pl.BlockSpec((None, None, tr, c), lambda i, pos_ref: (0, pos_ref[0], i, 0)))
    return pl.pallas_call(
        body, name=name, grid_spec=grid_spec, out_shape=jax.ShapeDtypeStruct((1, N_CHIP, r, c), BF16),
        compiler_params=_params(("parallel",)),
    )(pos, w)


def _pair_sum(mine, got, pos, *, name):
    l_dim, s_dim, h, c = got.shape
    th = min(h, 512)
    nt = h // th

    def body(pos_ref, a_ref, b_ref, o_ref):
        o_ref[...] = (a_ref[...].astype(F32) + b_ref[...].astype(F32)).astype(BF16)

    spec = pl.BlockSpec((None, None, th, c), lambda l, s, i, pos_ref: (l, s, i, 0))
    grid_spec = pltpu.PrefetchScalarGridSpec(
        num_scalar_prefetch=1, grid=(l_dim, s_dim, nt),
        in_specs=[pl.BlockSpec((None, None, th, c), lambda l, s, i, pos_ref: (l, s, pos_ref[1] * nt + i, 0)), spec],
        out_specs=spec)
    return pl.pallas_call(
        body, name=name, grid_spec=grid_spec, out_shape=jax.ShapeDtypeStruct(got.shape, BF16),
        compiler_params=_params(("parallel",) * 3),
    )(pos, mine, got)


def _chip_sum(sums, landed, pos, *, name):
    l_dim, _, h, c = sums.shape
    th = min(h, 512)
    nt = h // th

    def body(pos_ref, own, r0, r1, r2, o_ref):
        o_ref[...] = ((own[...].astype(F32) + r0[...].astype(F32)) + r1[...].astype(F32)) + r2[...].astype(F32)

    def piece(k):
        return pl.BlockSpec((None, None, th, c), lambda l, i, pos_ref: (l, k, i, 0))

    grid_spec = pltpu.PrefetchScalarGridSpec(
        num_scalar_prefetch=1, grid=(l_dim, nt),
        in_specs=[pl.BlockSpec((None, None, th, c), lambda l, i, pos_ref: (l, pos_ref[0], i, 0)),
                  piece(0), piece(1), piece(2)],
        out_specs=pl.BlockSpec((None, th, c), lambda l, i, pos_ref: (l, pos_ref[1] * nt + i, 0)))
    return pl.pallas_call(
        body, name=name, grid_spec=grid_spec, out_shape=jax.ShapeDtypeStruct((l_dim, 2 * h, c), F32),
        compiler_params=_params(("parallel",) * 2),
    )(pos, sums, landed, landed, landed)


def _adam_big(w, m, v, grads, *, name):
    l_dim, r, c = w.shape
    assert len(grads) == l_dim
    tr = min(r, 256)

    def body(*refs):
        w_ref, m_ref, v_ref = refs[:3]
        g_refs = refs[3:3 + l_dim]
        go_ref, d_ref, mo_ref, vo_ref = refs[3 + l_dim:]
        g = g_refs[0][...]
        for l in range(1, l_dim):
            g = jnp.where(pl.program_id(0) == l, g_refs[l][...], g)
        delta, m_new, v_new = _adam_math(w_ref[...], g, m_ref[...], v_ref[...])
        go_ref[...] = g
        d_ref[...] = delta
        mo_ref[...] = m_new
        vo_ref[...] = v_new

    spec = pl.BlockSpec((None, tr, c), lambda l, i: (l, i, 0))
    gspec = pl.BlockSpec((None, tr, c), lambda l, i: (0, i, 0))
    return pl.pallas_call(
        body, name=name, grid=(l_dim, r // tr), in_specs=[spec] * 3 + [gspec] * l_dim, out_specs=[spec] * 4,
        out_shape=[jax.ShapeDtypeStruct(w.shape, F32)] * 4, compiler_params=_params(("parallel",) * 2),
    )(w, m, v, *grads)


def _position():
    return lax.axis_index("x"), lax.axis_index("y"), lax.axis_index("c")


def _other_chips(x, y):
    return [(1 - x, y), (x, 1 - y), (1 - x, 1 - y)]


def _remote(src, dst, send_sem, recv_sem, device):
    return pltpu.make_async_remote_copy(src_ref=src, dst_ref=dst, send_sem=send_sem, recv_sem=recv_sem,
                                        device_id=device, device_id_type=MESH)


ANY = pl.BlockSpec(memory_space=pl.ANY)


def _gather_weights(bufs):
    n = len(bufs)

    def body(*refs):
        send, forward, finish = _gather_steps(refs[n:2 * n], *refs[2 * n:])
        send()
        forward()
        finish()

    return pl.pallas_call(
        body, name="gather_weights", in_specs=[ANY] * n, out_specs=[ANY] * n,
        out_shape=[jax.ShapeDtypeStruct(b.shape, b.dtype) for b in bufs],
        input_output_aliases={a: a for a in range(n)},
        scratch_shapes=_gather_sems(n),
        compiler_params=pltpu.CompilerParams(has_side_effects=True),
    )(*bufs)


def _gather_sems(n):
    return [pltpu.SemaphoreType.DMA((3 * n,))] * 4


def _gather_steps(outs, send_sems, recv_sems, fwd_send, fwd_recv):
    n = len(outs)
    x, y, c = _position()
    chips = _other_chips(x, y)
    sibling = (x, y, 1 - c)

    def half(a, chip, core):
        h = outs[a].shape[2] // 2
        return outs[a].at[:, 2 * chip[0] + chip[1], pl.ds(core * h, h), :]

    def over_ici(a, k, chip):
        block = half(a, chip, c)
        return _remote(block, block, send_sems.at[3 * a + k], recv_sems.at[3 * a + k], (*chips[k], c))

    def over_d2d(a, k, core):
        block = half(a, chips[k], core)
        return _remote(block, block, fwd_send.at[3 * a + k], fwd_recv.at[3 * a + k], sibling)

    def send():
        for a in range(n):
            for k in range(3):
                over_ici(a, k, (x, y)).start()

    def forward():
        for k in range(3):
            for a in range(n):
                over_ici(a, k, chips[k]).wait_recv()
                over_d2d(a, k, c).start()

    def finish():
        for k in range(3):
            for a in range(n):
                over_d2d(a, k, 1 - c).wait_recv()
        for a in range(n):
            for k in range(3):
                over_ici(a, k, (x, y)).wait_send()
                over_d2d(a, k, c).wait_send()

    return send, forward, finish


def _swap_halves(grads, *, name):
    n = len(grads)

    def body(*refs):
        ins, got = refs[:n], refs[n:2 * n]
        send_sems, recv_sems = refs[2 * n:]
        x, y, c = _position()
        sibling = (x, y, 1 - c)
        copies = []
        for a in range(n):
            h = grads[a].shape[2] // 2
            cp = _remote(ins[a].at[:, :, pl.ds((1 - c) * h, h), :], got[a], send_sems.at[a], recv_sems.at[a], sibling)
            cp.start()
            copies.append(cp)
        for cp in copies:
            cp.wait()

    sem = pltpu.SemaphoreType.DMA((n,))
    return pl.pallas_call(
        body, name=name, in_specs=[ANY] * n, out_specs=[ANY] * n,
        out_shape=[jax.ShapeDtypeStruct(g.shape[:2] + (g.shape[2] // 2, g.shape[3]), g.dtype) for g in grads],
        scratch_shapes=[sem, sem], compiler_params=pltpu.CompilerParams(has_side_effects=True),
    )(*grads)


def _exchange_chips(sums):
    n = len(sums)

    def body(*refs):
        send, finish = _exchange_steps(refs[:n], refs[n:2 * n], *refs[2 * n:])
        send()
        finish()

    return pl.pallas_call(
        body, name="exchange_chips", in_specs=[ANY] * n, out_specs=[ANY] * n,
        out_shape=_exchange_shapes(sums), scratch_shapes=_exchange_sems(n),
        compiler_params=pltpu.CompilerParams(has_side_effects=True),
    )(*sums)


def _exchange_shapes(sums):
    return [jax.ShapeDtypeStruct((s.shape[0], 3) + s.shape[2:], s.dtype) for s in sums]


def _exchange_sems(n):
    return [pltpu.SemaphoreType.DMA((3 * n,))] * 2


def _exchange_steps(ins, outs, send_sems, recv_sems):
    n = len(ins)
    x, y, c = _position()
    chips = _other_chips(x, y)

    def copy(a, k):
        chip = chips[k]
        return _remote(ins[a].at[:, 2 * chip[0] + chip[1]], outs[a].at[:, k],
                       send_sems.at[3 * a + k], recv_sems.at[3 * a + k], (*chip, c))

    def send():
        for a in range(n):
            for k in range(3):
                copy(a, k).start()

    def finish():
        for a in range(n):
            for k in range(3):
                copy(a, k).wait()

    return send, finish


def _join_halves(bufs, *, name):
    n = len(bufs)

    def body(*refs):
        outs = refs[n:2 * n]
        send_sems, recv_sems = refs[2 * n:]
        x, y, c = _position()
        sibling = (x, y, 1 - c)
        copies = []
        for a in range(n):
            h = bufs[a].shape[1] // 2
            mine = outs[a].at[:, pl.ds(c * h, h), :]
            cp = _remote(mine, mine, send_sems.at[a], recv_sems.at[a], sibling)
            cp.start()
            copies.append((cp, a, h))
        for cp, a, h in copies:
            cp.wait_send()
            got = outs[a].at[:, pl.ds((1 - c) * h, h), :]
            _remote(got, got, send_sems.at[a], recv_sems.at[a], sibling).wait_recv()

    sem = pltpu.SemaphoreType.DMA((n,))
    return pl.pallas_call(
        body, name=name, in_specs=[ANY] * n, out_specs=[ANY] * n,
        out_shape=[jax.ShapeDtypeStruct(b.shape, b.dtype) for b in bufs],
        input_output_aliases={a: a for a in range(n)},
        scratch_shapes=[sem, sem], compiler_params=pltpu.CompilerParams(has_side_effects=True),
    )(*bufs)


def _allreduce_small(packs):
    n = len(packs)

    def body(*refs):
        ins, outs, gath = refs[:n], refs[n:2 * n], refs[2 * n:3 * n]
        send_sems, recv_sems = refs[3 * n:]
        x, y, c = _position()
        me, sibling = (x, y, c), (x, y, 1 - c)
        chips = _other_chips(x, y)

        def slot(a, dev):
            return gath[a].at[4 * dev[0] + 2 * dev[1] + dev[2]]

        def copy(a, k, block, to, src=None):
            return _remote(slot(a, block) if src is None else src, slot(a, block),
                           send_sems.at[7 * a + k], recv_sems.at[7 * a + k], to)

        started = []
        for a in range(n):
            slot(a, me)[...] = ins[a][...]
            first = [copy(a, 0, me, sibling, src=ins[a])]
            first += [copy(a, 1 + k, me, (*chip, c), src=ins[a]) for k, chip in enumerate(chips)]
            for cp in first:
                cp.start()
            started += first
        for a in range(n):
            for k, chip in enumerate(chips):
                copy(a, 1 + k, (*chip, c), me).wait_recv()
                cp = copy(a, 4 + k, (*chip, c), sibling)
                cp.start()
                started.append(cp)
        for a in range(n):
            copy(a, 0, sibling, me).wait_recv()
            for k, chip in enumerate(chips):
                copy(a, 4 + k, (*chip, 1 - c), me).wait_recv()
        for cp in started:
            cp.wait_send()
        for a in range(n):
            total = gath[a][0]
            for d in range(1, N_DEV):
                total = total + gath[a][d]
            outs[a][...] = total

    vmem = pl.BlockSpec(memory_space=pltpu.VMEM)
    sem = pltpu.SemaphoreType.DMA((7 * n,))
    return pl.pallas_call(
        body, name="allreduce_small", in_specs=[vmem] * n, out_specs=[vmem] * n,
        out_shape=[jax.ShapeDtypeStruct(p.shape, p.dtype) for p in packs],
        scratch_shapes=[pltpu.VMEM((N_DEV,) + p.shape, p.dtype) for p in packs] + [sem, sem],
        compiler_params=pltpu.CompilerParams(has_side_effects=True, vmem_limit_bytes=VMEM_LIMIT_BYTES),
    )(*packs)


LOSS_ROW = 1040


def _pad_rows(a, rows=8):
    return jnp.concatenate([a, jnp.zeros((rows - a.shape[0], a.shape[1]), a.dtype)], axis=0)

def _adam_small(wide, mid, narrow, params):
    names = ["mix_norm_g", "mlp_norm_g", "final_norm_g", "conv_b", "conv_w", "sgu_norm_g", "sgu_norm_b",
             "pool_w", "pool_scale", "sgu_w", "sgu_b"]
    n = len(names)

    def body(*refs):
        wide_ref, mid_ref, narrow_ref = refs[:3]
        wmv = refs[3:3 + 3 * n]
        outs = refs[3 + 3 * n:]
        x, y, _ = _position()
        q = 2 * x + y

        def my_quarter(rows):
            parts = [rows[:, s * TILE:(s + 1) * TILE] for s in range(N_CHIP)]
            return jnp.where(q == 0, parts[0], jnp.where(q == 1, parts[1], jnp.where(q == 2, parts[2], parts[3])))

        def tiles(first_row):
            return [((0, g), narrow_ref[first_row + g * TILE:first_row + (g + 1) * TILE, :]) for g in range(4)]

        grads = {
            "mix_norm_g": [((), wide_ref[0:2, :])],
            "mlp_norm_g": [((), wide_ref[8:10, :])],
            "final_norm_g": [((), wide_ref[16:17, :])],
            "conv_b": [((), mid_ref[0:1, :])],
            "conv_w": [((0,), my_quarter(mid_ref[8:11, :]))],
            "sgu_norm_g": [((), my_quarter(mid_ref[16:17, :]))],
            "sgu_norm_b": [((), my_quarter(mid_ref[24:25, :]))],
            "pool_w": tiles(0),
            "sgu_w": tiles(512),
            "pool_scale": [((0,), narrow_ref[1024:1028, :])],
            "sgu_b": [((0,), narrow_ref[1032:1036, :])],
        }
        for i, name in enumerate(names):
            w_ref, m_ref, v_ref = wmv[3 * i:3 * i + 3]
            for lead, g in grads[name]:
                idx = lead + (slice(None), slice(None))
                delta, m_new, v_new = _adam_math(w_ref[idx], g, m_ref[idx], v_ref[idx])
                outs[4 * i][idx] = g
                outs[4 * i + 1][idx] = delta
                outs[4 * i + 2][idx] = m_new
                outs[4 * i + 3][idx] = v_new

    vmem = pl.BlockSpec(memory_space=pltpu.VMEM)
    args, out_shape = [wide, mid, narrow], []
    for name in names:
        w, m, v = params[name]
        args += [w, m, v]
        out_shape += [jax.ShapeDtypeStruct(w.shape, F32)] * 4
    res = pl.pallas_call(
        body, name="adam_small", in_specs=[vmem] * len(args), out_specs=[vmem] * len(out_shape),
        out_shape=out_shape, compiler_params=pltpu.CompilerParams(vmem_limit_bytes=VMEM_LIMIT_BYTES),
    )(*args)
    return {name: res[4 * i:4 * i + 4] for i, name in enumerate(names)}


def _pair_sums(grads, pos, tag):
    got = _swap_halves(grads, name=f"swap_halves_{tag}")
    return [_pair_sum(a, b, pos, name=f"pair_sum_{tag}{i}") for i, (a, b) in enumerate(zip(grads, got))]


def _finish_reduce(sums, landed, pos, tag):
    halves = [_chip_sum(s, r, pos, name=f"chip_sum_{tag}{i}") for i, (s, r) in enumerate(zip(sums, landed))]
    return _join_halves(halves, name=f"join_halves_{tag}")


def kernel(x, mix_norm_g, mlp_norm_g, ab_w_in, pool_w, pool_scale, conv_w, conv_b, ab_w_out, cd_w_in, sgu_norm_g, sgu_norm_b, sgu_w, sgu_b, cd_w_out, mlp_w1, mlp_w2, final_norm_g, loss_target, m_mix_norm_g, m_mlp_norm_g, m_ab_w_in, m_pool_w, m_pool_scale, m_conv_w, m_conv_b, m_ab_w_out, m_cd_w_in, m_sgu_norm_g, m_sgu_norm_b, m_sgu_w, m_sgu_b, m_cd_w_out, m_mlp_w1, m_mlp_w2, m_final_norm_g, v_mix_norm_g, v_mlp_norm_g, v_ab_w_in, v_pool_w, v_pool_scale, v_conv_w, v_conv_b, v_ab_w_out, v_cd_w_in, v_sgu_norm_g, v_sgu_norm_b, v_sgu_w, v_sgu_b, v_cd_w_out, v_mlp_w1, v_mlp_w2, v_final_norm_g):
    nseq, t_len, d = x.shape
    m_tok = nseq * t_len
    h0 = x.reshape(m_tok, d)
    target = loss_target.reshape(m_tok, d)

    x_idx, y_idx = lax.axis_index("x"), lax.axis_index("y")
    q_idx = 2 * x_idx + y_idx
    pos = jnp.stack([q_idx, lax.axis_index("c")]).astype(jnp.int32)
    def shard_buffer(w, layer, tag):
        return _cast_place(w, layer, pos, name=f"cast_place_{tag}")

    def row_block(w):
        return w.reshape(1, 1, -1, w.shape[-1])

    w_ab_in, w_ab_out = _gather_weights([shard_buffer(ab_w_in, 0, "ab_in"), shard_buffer(ab_w_out, 0, "ab_out")])
    w_ab_out = row_block(w_ab_out)
    later_weights = [shard_buffer(cd_w_out, 0, "cd_out"), shard_buffer(mlp_w1, 1, "w1_1"),
                     shard_buffer(mlp_w2, 1, "w2_1")]

    pool_w3, pool_scale3 = pool_w[0], pool_scale[0].reshape(4, 1, TILE)
    sgu_w3 = sgu_w[0]
    sgu_w3_t = jnp.swapaxes(sgu_w3, 1, 2)
    sgu_bias_tile = jnp.broadcast_to(sgu_b[0][:, :, None], (4, TILE, TILE))
    conv_w2, conv_b2 = conv_w[0], conv_b
    def place_quarter(v):
        return lax.dynamic_update_slice(jnp.zeros((v.shape[0], 4 * TILE), F32), v, (0, q_idx * TILE))

    sharded_small = jnp.concatenate(
        [place_quarter(conv_w[0]), place_quarter(sgu_norm_g), place_quarter(sgu_norm_b),
         jnp.zeros((3, 4 * TILE), F32)], axis=0)
    sharded_small, = _allreduce_small([sharded_small])
    sharded_small = sharded_small * 0.5
    conv_w_full = sharded_small[0:3]
    sgu_g_full = sharded_small[3:4]
    sgu_b_full = sharded_small[4:5]

    xn0 = _rms_fwd(h0, mix_norm_g[0:1], name="rms_fwd_mix0")
    p_ab, (w_1_0,) = _mm_nn(xn0, w_ab_in, 0, out_dtype=BF16, name="ab_in_proj",
                            rider=("gather", [shard_buffer(mlp_w1, 0, "w1_0")]))
    mix0 = _ab_fwd(p_ab, pool_w3, pool_scale3, conv_w_full, conv_b2, nseq, t_len)
    h1, hn0 = _mm_nn(mix0, w_ab_out, 0, out_dtype=F32, name="ab_out_proj", epilogue="residual", extra=h0,
                     norm_g=mlp_norm_g[0:1])
    act0, (w_2_0,) = _mm_nn(hn0, w_1_0, 0, out_dtype=BF16, name="mlp0_up", epilogue="relu2",
                            rider=("gather", [shard_buffer(mlp_w2, 0, "w2_0")]))
    w_2_0 = row_block(w_2_0)
    (h2, xn1), (w_cd_in,) = _mm_nn(act0, w_2_0, 0, out_dtype=F32, name="mlp0_down", epilogue="residual", extra=h1,
                                   norm_g=mix_norm_g[1:2], rider=("gather", [shard_buffer(cd_w_in, 0, "cd_in")]))

    p_cd = _mm_nn(xn1, w_cd_in, 0, out_dtype=BF16, name="cd_in_proj")
    c_out = _sgu_fwd(p_cd, sgu_g_full, sgu_b_full, sgu_w3, sgu_bias_tile)
    d_out, ltot, (w_cd_out, w_1_1, w_2_1) = _sb_fwd(p_cd, nseq, t_len, later_weights)
    w_cd_out, w_2_1 = row_block(w_cd_out), row_block(w_2_1)
    mix1 = jnp.concatenate([c_out, d_out], axis=1)
    h3, hn1 = _mm_nn(mix1, w_cd_out, 0, out_dtype=F32, name="cd_out_proj", epilogue="residual", extra=h2,
                     norm_g=mlp_norm_g[1:2])
    act1 = _mm_nn(hn1, w_1_1, 0, out_dtype=BF16, name="mlp1_up", epilogue="relu2")
    h4 = _mm_nn(act1, w_2_1, 0, out_dtype=F32, name="mlp1_down", epilogue="residual", extra=h3)

    dh4, dh4_bf, dg_final, loss_tile = _final_loss(h4, final_norm_g.reshape(1, d), target)

    def mlp_bwd(dh_out, dh_out_bf, h_in, hn, act, w_1, w_2, layer, tag):
        dz = _mm_nt(dh_out_bf, w_2, 0, out_dtype=BF16, name=f"mlp{tag}_down_bwd",
                    epilogue="relu2_bwd", extra=act)
        g_w2 = _mm_tn(act, dh_out_bf, 1, name=f"mlp{tag}_down_wgrad")
        g_w1 = _mm_tn(hn, dz, N_CHIP, name=f"mlp{tag}_up_wgrad")
        dh_in, dh_in_bf, dg = _mm_nt(dz, w_1, 0, out_dtype=F32, name=f"mlp{tag}_up_bwd", epilogue="rms_bwd",
                                     extra=(h_in, mlp_norm_g[layer:layer + 1], dh_out))
        return dh_in, dh_in_bf, dg, g_w1, g_w2

    def as_pieces(g):
        return g.reshape(1, N_CHIP, -1, g.shape[-1]) if g.shape[1] == 1 else g

    dh3, dh3_bf, dg_mlp1, g_w1_1, g_w2_1 = mlp_bwd(dh4, dh4_bf, h3, hn1, act1, w_1_1, w_2_1, 1, "1")

    dmix1 = _mm_nt(dh3_bf, w_cd_out, 0, out_dtype=BF16, name="cd_out_bwd")
    g_cd_out = _mm_tn(mix1, dh3_bf, 1, name="cd_out_wgrad")
    sums_a = _pair_sums([g_w1_1, as_pieces(g_w2_1), as_pieces(g_cd_out)], pos, "a")
    du, dv, dsgu_w, dsgu_bs, dsgu_g, dsgu_b = _sgu_bwd(p_cd, dmix1, sgu_g_full, sgu_b_full, sgu_w3, sgu_w3_t,
                                                      sgu_bias_tile)
    dq, dk, dvv, landed_a = _sb_bwd(p_cd, dmix1, ltot, nseq, t_len, sums_a)
    r_w1_1, r_w2_1, r_cd_out = _finish_reduce(sums_a, landed_a, pos, "a")
    dp_cd = jnp.concatenate([du, dv, dq, dk, dvv], axis=1)
    g_cd_in = _mm_tn(xn1, dp_cd, N_CHIP, name="cd_in_wgrad")
    dh2, dh2_bf, dg_mix1 = _mm_nt(dp_cd, w_cd_in, 0, out_dtype=F32, name="cd_in_bwd", epilogue="rms_bwd",
                                  extra=(h2, mix_norm_g[1:2], dh3))

    sums_c = _pair_sums([g_cd_in], pos, "c")
    dz0, landed_c = _mm_nt(dh2_bf, w_2_0, 0, out_dtype=BF16, name="mlp0_down_bwd", epilogue="relu2_bwd", extra=act0,
                           rider=("exchange", sums_c))
    r_cd_in, = _finish_reduce(sums_c, landed_c, pos, "c")
    g_w2_0 = _mm_tn(act0, dh2_bf, 1, name="mlp0_down_wgrad")
    g_w1_0 = _mm_tn(hn0, dz0, N_CHIP, name="mlp0_up_wgrad")
    sums_d = _pair_sums([as_pieces(g_w2_0)], pos, "d")
    (dh1, dh1_bf, dg_mlp0), landed_d = _mm_nt(dz0, w_1_0, 0, out_dtype=F32, name="mlp0_up_bwd", epilogue="rms_bwd",
                                              extra=(h1, mlp_norm_g[0:1], dh2), rider=("exchange", sums_d))
    r_w2_0, = _finish_reduce(sums_d, landed_d, pos, "d")

    dmix0 = _mm_nt(dh1_bf, w_ab_out, 0, out_dtype=BF16, name="ab_out_bwd")
    g_ab_out = _mm_tn(mix0, dh1_bf, 1, name="ab_out_wgrad")
    sums_e = _pair_sums([g_w1_0], pos, "e")
    (da, dxb, dgb, dgc, dpool_w, dpool_scale, dconv_w, dconv_b), landed_e = _ab_bwd(
        p_ab, dmix0, pool_w3, pool_scale3, conv_w_full, conv_b2, nseq, t_len, rider=("exchange", sums_e))
    r_w1_0, = _finish_reduce(sums_e, landed_e, pos, "e")
    dp_ab = jnp.concatenate([da, dxb, dgb, dgc], axis=1)
    sums_f = _pair_sums([as_pieces(g_ab_out)], pos, "f")
    g_ab_in, landed_f = _mm_tn(xn0, dp_ab, N_CHIP, name="ab_in_wgrad", rider=("exchange", sums_f))
    r_ab_out, = _finish_reduce(sums_f, landed_f, pos, "f")
    grad_x, _, dg_mix0 = _mm_nt(dp_ab, w_ab_in, 0, out_dtype=F32, name="ab_in_bwd", epilogue="rms_bwd",
                                extra=(h0, mix_norm_g[0:1], dh1))
    sums_g = _pair_sums([g_ab_in], pos, "g")
    r_ab_in, = _finish_reduce(sums_g, _exchange_chips(sums_g), pos, "g")

    big_out = {
        "ab_w_in": _adam_big(ab_w_in, m_ab_w_in, v_ab_w_in, [r_ab_in], name="adam_ab_w_in"),
        "ab_w_out": _adam_big(ab_w_out, m_ab_w_out, v_ab_w_out, [r_ab_out], name="adam_ab_w_out"),
        "cd_w_in": _adam_big(cd_w_in, m_cd_w_in, v_cd_w_in, [r_cd_in], name="adam_cd_w_in"),
        "cd_w_out": _adam_big(cd_w_out, m_cd_w_out, v_cd_w_out, [r_cd_out], name="adam_cd_w_out"),
        "mlp_w1": _adam_big(mlp_w1, m_mlp_w1, v_mlp_w1, [r_w1_0, r_w1_1], name="adam_mlp_w1"),
        "mlp_w2": _adam_big(mlp_w2, m_mlp_w2, v_mlp_w2, [r_w2_0, r_w2_1], name="adam_mlp_w2"),
    }

    wide = jnp.concatenate([_pad_rows(jnp.concatenate([dg_mix0, dg_mix1], axis=0)),
                            _pad_rows(jnp.concatenate([dg_mlp0, dg_mlp1], axis=0)), _pad_rows(dg_final)], axis=0)
    mid = jnp.concatenate([_pad_rows(dconv_b), _pad_rows(dconv_w), _pad_rows(dsgu_g), _pad_rows(dsgu_b)], axis=0)
    narrow = jnp.concatenate(
        [dpool_w.reshape(4 * TILE, TILE), dsgu_w.reshape(4 * TILE, TILE), _pad_rows(dpool_scale.reshape(4, TILE)),
         _pad_rows(dsgu_bs[:, :, 0]), loss_tile], axis=0)
    wide, mid, narrow = _allreduce_small([wide, mid, narrow])
    small_out = _adam_small(wide, mid, narrow, {
        "mix_norm_g": (mix_norm_g, m_mix_norm_g, v_mix_norm_g),
        "mlp_norm_g": (mlp_norm_g, m_mlp_norm_g, v_mlp_norm_g),
        "final_norm_g": tuple(a.reshape(1, d) for a in (final_norm_g, m_final_norm_g, v_final_norm_g)),
        "conv_b": (conv_b, m_conv_b, v_conv_b),
        "conv_w": (conv_w, m_conv_w, v_conv_w),
        "sgu_norm_g": (sgu_norm_g, m_sgu_norm_g, v_sgu_norm_g),
        "sgu_norm_b": (sgu_norm_b, m_sgu_norm_b, v_sgu_norm_b),
        "pool_w": (pool_w, m_pool_w, v_pool_w),
        "pool_scale": (pool_scale, m_pool_scale, v_pool_scale),
        "sgu_w": (sgu_w, m_sgu_w, v_sgu_w),
        "sgu_b": (sgu_b, m_sgu_b, v_sgu_b),
    })
    small_out["final_norm_g"] = [a.reshape(d) for a in small_out["final_norm_g"]]

    order = ["mix_norm_g", "mlp_norm_g", "ab_w_in", "pool_w", "pool_scale", "conv_w", "conv_b", "ab_w_out",
             "cd_w_in", "sgu_norm_g", "sgu_norm_b", "sgu_w", "sgu_b", "cd_w_out", "mlp_w1", "mlp_w2",
             "final_norm_g"]
    both = {**big_out, **small_out}
    loss = narrow[LOSS_ROW, 0]
    outs = [loss, grad_x.reshape(nseq, t_len, d)]
    for kind in range(4):
        outs += [both[name][kind] for name in order]
    return tuple(outs)
```

```python
import math

import jax
import jax.numpy as jnp
from jax import lax
from jax.experimental import pallas as pl
from jax.experimental.pallas import tpu as pltpu

F32 = jnp.float32
BF16 = jnp.bfloat16
MESH = pl.DeviceIdType.MESH

D_MODEL = 1024
EPS = 1e-6
TILE = 128
N_CHIP = 4
N_DEV = 8
VMEM_LIMIT_BYTES = 56 * 1024 * 1024

ADAM_LR = 0.001
ADAM_B1 = 0.9
ADAM_B2 = 0.999
ADAM_EPS = 1e-08
ADAM_WD = 0.01
ADAM_STEP = 10

NT_DIMS = (((1,), (1,)), ((), ()))
TN_DIMS = (((0,), (0,)), ((), ()))


def _params(sem=None):
    return pltpu.CompilerParams(dimension_semantics=sem, vmem_limit_bytes=VMEM_LIMIT_BYTES)


def _call(body, *, name, grid, in_specs, out_specs, out_shape, scratch_shapes, semantics, args, rider=None):
    if rider is None:
        res = pl.pallas_call(body, name=name, grid=grid, in_specs=in_specs, out_specs=out_specs, out_shape=out_shape,
                             scratch_shapes=scratch_shapes, compiler_params=_params(semantics))(*args)
        return list(res), []
    kind, arrays = rider
    gather = kind == "gather"
    nr, n_in, n_out, n_scr = len(arrays), len(in_specs), len(out_specs), len(scratch_shapes)
    first_out, first_scr = n_in + nr, n_in + nr + n_out + nr
    last_step = math.prod(grid) - 1

    def riding(*refs):
        r_in = refs[n_in:first_out]
        r_out = refs[first_out + n_out:first_scr]
        sems = refs[first_scr + n_scr:]
        step = 0
        for axis, size in enumerate(grid):
            step = step * size + pl.program_id(axis)
        if gather:
            send, forward, finish = _gather_steps(r_out, *sems)
            pl.when(step == 0)(send)
            pl.when(step == last_step)(forward)
        else:
            send, finish = _exchange_steps(r_in, r_out, *sems)
            pl.when(step == 0)(send)
        body(*refs[:n_in], *refs[first_out:first_out + n_out], *refs[first_scr:first_scr + n_scr])
        pl.when(step == last_step)(finish)

    res = pl.pallas_call(
        riding, name=name, grid=grid, in_specs=list(in_specs) + [ANY] * nr, out_specs=list(out_specs) + [ANY] * nr,
        out_shape=list(out_shape) + ([jax.ShapeDtypeStruct(a.shape, a.dtype) for a in arrays] if gather
                                     else _exchange_shapes(arrays)),
        scratch_shapes=list(scratch_shapes) + (_gather_sems(nr) if gather else _exchange_sems(nr)),
        input_output_aliases={n_in + a: n_out + a for a in range(nr)} if gather else {},
        compiler_params=pltpu.CompilerParams(dimension_semantics=("arbitrary",) * len(grid),
                                             vmem_limit_bytes=VMEM_LIMIT_BYTES, has_side_effects=True),
    )(*args, *arrays)
    return list(res[:n_out]), list(res[n_out:])


def _row_tile(k_dim):
    return 1024 if k_dim <= 1024 else 512


def _mm_nn(a, b4, layer, *, out_dtype, name, epilogue=None, extra=None, norm_g=None, rider=None):
    m, k_dim = a.shape
    _, s_dim, kb, n = b4.shape
    assert kb == k_dim
    tm, tn = _row_tile(k_dim), min(n, 1024)
    assert m % tm == 0 and n % tn == 0
    npb = n // tn
    grid = (m // tm, s_dim * npb)
    n_in = 2 + (extra is not None) + (norm_g is not None)
    two_outputs = norm_g is not None or epilogue == "relu2"
    assert norm_g is None or (tn == s_dim * n and epilogue != "relu2")

    def body(*refs):
        a_ref, b_ref = refs[:2]
        e_ref = refs[2] if extra is not None else None
        g_ref = refs[n_in - 1] if norm_g is not None else None
        o_ref = refs[n_in]
        acc = jnp.dot(a_ref[...], b_ref[...], preferred_element_type=F32)
        if epilogue == "relu2":
            r = jnp.maximum(acc, 0.0)
            refs[n_in + 1][...] = r.astype(BF16)
            acc = r * r
        elif epilogue == "residual":
            acc = acc + e_ref[...]
        o_ref[...] = acc.astype(out_dtype)
        if norm_g is not None:
            rstd = lax.rsqrt(jnp.mean(acc * acc, axis=-1, keepdims=True) + EPS)
            refs[n_in + 1][...] = (acc * rstd * g_ref[...]).astype(BF16)

    in_specs = [
        pl.BlockSpec((tm, k_dim), lambda i, j: (i, 0)),
        pl.BlockSpec((None, None, k_dim, tn), lambda i, j: (layer, j // npb, 0, j % npb)),
    ]
    args = [a, b4]
    if extra is not None:
        in_specs.append(pl.BlockSpec((tm, tn), lambda i, j: (i, j)))
        args.append(extra)
    out_block = pl.BlockSpec((tm, tn), lambda i, j: (i, j))
    out_specs, out_shape = [out_block], [jax.ShapeDtypeStruct((m, s_dim * n), out_dtype)]
    if norm_g is not None:
        in_specs.append(pl.BlockSpec((1, tn), lambda i, j: (0, j)))
        args.append(norm_g)
    if two_outputs:
        out_specs.append(out_block)
        out_shape.append(jax.ShapeDtypeStruct((m, s_dim * n), BF16))
    res, rode = _call(
        body, name=name, grid=grid, in_specs=in_specs, out_specs=out_specs, out_shape=out_shape,
        scratch_shapes=[], semantics=("parallel", "parallel"), args=args, rider=rider)
    res = res if two_outputs else res[0]
    return res if rider is None else (res, rode)


def _mm_nt(a, b4, layer, *, out_dtype, name, epilogue=None, extra=None, rider=None):
    m, k_dim = a.shape
    _, s_dim, n_out, n = b4.shape
    assert k_dim == s_dim * n
    tm, tn = _row_tile(k_dim), min(n_out, 1024)
    assert m % tm == 0 and n_out % tn == 0
    grid = (m // tm, n_out // tn)
    rms = epilogue == "rms_bwd"
    assert not rms or tn == n_out
    extras = [] if extra is None else (list(extra) if rms else [extra])
    n_in = 2 + len(extras)
    n_res = 3 if rms else 1

    def body(*refs):
        a_ref, b_ref = refs[:2]
        e_refs = refs[2:n_in]
        o_ref = refs[n_in]
        acc = lax.dot_general(a_ref[:, 0:n], b_ref[0], NT_DIMS, preferred_element_type=F32)
        for s in range(1, s_dim):
            acc = acc + lax.dot_general(a_ref[:, s * n:(s + 1) * n], b_ref[s], NT_DIMS, preferred_element_type=F32)
        if epilogue == "relu2_bwd":
            acc = acc * (2.0 * e_refs[0][...].astype(F32))
        if not rms:
            o_ref[...] = acc.astype(out_dtype)
        else:
            h_ref, g_ref, dres_ref = e_refs
            dhb_ref, dg_ref = refs[n_in + 1:n_in + 3]
            hv = h_ref[...]
            rstd = lax.rsqrt(jnp.mean(hv * hv, axis=-1, keepdims=True) + EPS)
            xhat = hv * rstd
            dxhat = acc * g_ref[...]
            dh = dres_ref[...] + rstd * (dxhat - xhat * jnp.mean(dxhat * xhat, axis=-1, keepdims=True))
            o_ref[...] = dh
            dhb_ref[...] = dh.astype(BF16)
            dg_part = jnp.sum(acc * xhat, axis=0, keepdims=True)
            first = pl.program_id(0) == 0

            @pl.when(first)
            def _():
                dg_ref[...] = dg_part

            @pl.when(jnp.logical_not(first))
            def _():
                dg_ref[...] += dg_part

    in_specs = [
        pl.BlockSpec((tm, k_dim), lambda i, j: (i, 0)),
        pl.BlockSpec((None, s_dim, tn, n), lambda i, j: (layer, 0, j, 0)),
    ]
    args = [a, b4] + extras
    block = pl.BlockSpec((tm, tn), lambda i, j: (i, j))
    vec = pl.BlockSpec((1, tn), lambda i, j: (0, j))
    if rms:
        in_specs += [block, vec, block]
        out_specs = [block, block, vec]
        out_shape = [jax.ShapeDtypeStruct((m, n_out), F32), jax.ShapeDtypeStruct((m, n_out), BF16),
                     jax.ShapeDtypeStruct((1, n_out), F32)]
    else:
        in_specs += [block] * len(extras)
        out_specs, out_shape = [block], [jax.ShapeDtypeStruct((m, n_out), out_dtype)]
    res, rode = _call(
        body, name=name, grid=grid, in_specs=in_specs, out_specs=out_specs, out_shape=out_shape,
        scratch_shapes=[], semantics=("arbitrary",) * 2 if rms else ("parallel", "parallel"), args=args, rider=rider)
    res = res if rms else res[0]
    return res if rider is None else (res, rode)


def _mm_tn(a, b, s_dim, *, name, rider=None):
    m, k1 = a.shape
    mb, n_all = b.shape
    assert mb == m and n_all % s_dim == 0
    n = n_all // s_dim
    tn, t1 = min(n, 1024), _row_tile(m)
    assert k1 % t1 == 0 and n % tn == 0
    npb = n // tn
    grid = (k1 // t1, s_dim * npb)

    def body(a_ref, b_ref, o_ref):
        o_ref[...] = lax.dot_general(a_ref[...], b_ref[...], TN_DIMS, preferred_element_type=F32).astype(BF16)

    res, rode = _call(
        body, name=name, grid=grid,
        in_specs=[pl.BlockSpec((m, t1), lambda i, j: (0, i)), pl.BlockSpec((m, tn), lambda i, j: (0, j))],
        out_specs=[pl.BlockSpec((None, None, t1, tn), lambda i, j: (0, j // npb, i, j % npb))],
        out_shape=[jax.ShapeDtypeStruct((1, s_dim, k1, n), BF16)],
        scratch_shapes=[], semantics=("parallel", "parallel"), args=[a, b], rider=rider)
    return res[0] if rider is None else (res[0], rode)


ROW_TILE = 512


def _rms_fwd(h, g, *, name, rider=None):
    m, d = h.shape

    def body(h_ref, g_ref, o_ref):
        hv = h_ref[...]
        rstd = lax.rsqrt(jnp.mean(hv * hv, axis=-1, keepdims=True) + EPS)
        o_ref[...] = (hv * rstd * g_ref[...]).astype(BF16)

    res, rode = _call(
        body, name=name, grid=(m // ROW_TILE,),
        in_specs=[pl.BlockSpec((ROW_TILE, d), lambda i: (i, 0)), pl.BlockSpec((1, d), lambda i: (0, 0))],
        out_specs=[pl.BlockSpec((ROW_TILE, d), lambda i: (i, 0))], out_shape=[jax.ShapeDtypeStruct((m, d), BF16)],
        scratch_shapes=[], semantics=("parallel",), args=[h, g], rider=rider)
    return res[0] if rider is None else (res[0], rode)


def _final_loss(h, g, target):
    m, d = h.shape

    def body(h_ref, g_ref, t_ref, dh_ref, dhb_ref, dg_ref, loss_ref):
        hv = h_ref[...]
        gv = g_ref[...]
        rstd = lax.rsqrt(jnp.mean(hv * hv, axis=-1, keepdims=True) + EPS)
        xhat = hv * rstd
        err = xhat * gv - t_ref[...]
        dy = err * (1.0 / d)
        dxhat = dy * gv
        dh = rstd * (dxhat - xhat * jnp.mean(dxhat * xhat, axis=-1, keepdims=True))
        dh_ref[...] = dh
        dhb_ref[...] = dh.astype(BF16)
        dg_part = jnp.sum(dy * xhat, axis=0, keepdims=True)
        sq = jnp.sum(jnp.sum(err * err, axis=1, keepdims=True), axis=0, keepdims=True) * (0.5 / d)
        loss_part = jnp.broadcast_to(sq, (8, TILE))

        @pl.when(pl.program_id(0) == 0)
        def _():
            dg_ref[...] = dg_part
            loss_ref[...] = loss_part

        @pl.when(pl.program_id(0) > 0)
        def _():
            dg_ref[...] += dg_part
            loss_ref[...] += loss_part

    row = pl.BlockSpec((ROW_TILE, d), lambda i: (i, 0))
    vec = pl.BlockSpec((1, d), lambda i: (0, 0))
    return pl.pallas_call(
        body, name="final_loss", grid=(m // ROW_TILE,),
        in_specs=[row, vec, row],
        out_specs=[row, row, vec, pl.BlockSpec((8, TILE), lambda i: (0, 0))],
        out_shape=[jax.ShapeDtypeStruct((m, d), F32), jax.ShapeDtypeStruct((m, d), BF16),
                   jax.ShapeDtypeStruct((1, d), F32), jax.ShapeDtypeStruct((8, TILE), F32)],
        compiler_params=_params(("arbitrary",)),
    )(h, g, target)


def _shift_down(x, s, t_idx):
    return jnp.where(t_idx >= s, pltpu.roll(x, s, 0), 0.0)


def _shift_up(x, s, t_idx, t_len):
    return jnp.where(t_idx < t_len - s, pltpu.roll(x, t_len - s, 0), 0.0)


def _pool_select(group, s2, s4, s8, s16):
    return jnp.where(group == 0, s2, jnp.where(group == 1, s4, jnp.where(group == 2, s8, s16)))


def _pool_count(group, t_idx):
    win = jnp.left_shift(2, group)
    return jnp.minimum(t_idx + 1, win).astype(F32)


def _pool_fwd_math(a, group, t_idx):
    s2 = a + _shift_down(a, 1, t_idx)
    s4 = s2 + _shift_down(s2, 2, t_idx)
    s8 = s4 + _shift_down(s4, 4, t_idx)
    s16 = s8 + _shift_down(s8, 8, t_idx)
    return _pool_select(group, s2, s4, s8, s16) / _pool_count(group, t_idx) - a


def _pool_bwd_math(dpooled, group, t_idx, t_len):
    e = dpooled / _pool_count(group, t_idx)
    s2 = e + _shift_up(e, 1, t_idx, t_len)
    s4 = s2 + _shift_up(s2, 2, t_idx, t_len)
    s8 = s4 + _shift_up(s4, 4, t_idx, t_len)
    s16 = s8 + _shift_up(s8, 8, t_idx, t_len)
    return _pool_select(group, s2, s4, s8, s16) - dpooled


def _conv_fwd_math(c, w_ref, b_ref, t_idx):
    return (w_ref[0:1, :] * _shift_down(c, 2, t_idx) + w_ref[1:2, :] * _shift_down(c, 1, t_idx)
            + w_ref[2:3, :] * c + b_ref[...])


def _ab_fwd(p, pool_w, pool_scale, conv_w, conv_b, nseq, t_len):
    m = p.shape[0]
    ng = 4

    def body(a_ref, xb_ref, gb_ref, gc_ref, pw_ref, ps_ref, cw_ref, cb_ref, o_ref):
        j = pl.program_id(1)
        t_idx = lax.broadcasted_iota(jnp.int32, (t_len, TILE), 0)

        @pl.when(j < ng)
        def _():
            pooled = _pool_fwd_math(a_ref[...].astype(F32), j, t_idx)
            mixed = jnp.dot(pooled.astype(BF16), pw_ref[...].astype(BF16), preferred_element_type=F32)
            o_ref[...] = (mixed * ps_ref[...]).astype(BF16)

        @pl.when(j >= ng)
        def _():
            c = gc_ref[...].astype(F32) * xb_ref[...].astype(F32)
            y = _conv_fwd_math(c, cw_ref, cb_ref, t_idx)
            o_ref[...] = (gb_ref[...].astype(F32) * y).astype(BF16)

    def pool_j(j):
        return jnp.minimum(j, ng - 1)

    def conv_j(j):
        return jnp.maximum(j - ng, 0)

    in_specs = [
        pl.BlockSpec((t_len, TILE), lambda s, j: (s, pool_j(j))),
        pl.BlockSpec((t_len, TILE), lambda s, j: (s, ng + conv_j(j))),
        pl.BlockSpec((t_len, TILE), lambda s, j: (s, 2 * ng + conv_j(j))),
        pl.BlockSpec((t_len, TILE), lambda s, j: (s, 3 * ng + conv_j(j))),
        pl.BlockSpec((None, TILE, TILE), lambda s, j: (pool_j(j), 0, 0)),
        pl.BlockSpec((None, 1, TILE), lambda s, j: (pool_j(j), 0, 0)),
        pl.BlockSpec((3, TILE), lambda s, j: (0, conv_j(j))),
        pl.BlockSpec((1, TILE), lambda s, j: (0, conv_j(j))),
    ]
    return pl.pallas_call(
        body, name="ab_mixer_fwd", grid=(nseq, 2 * ng), in_specs=in_specs,
        out_specs=pl.BlockSpec((t_len, TILE), lambda s, j: (s, j)),
        out_shape=jax.ShapeDtypeStruct((m, 2 * ng * TILE), BF16),
        compiler_params=_params(("parallel", "arbitrary")),
    )(p, p, p, p, pool_w, pool_scale, conv_w, conv_b)


def _ab_bwd(p, dmix, pool_w, pool_scale, conv_w, conv_b, nseq, t_len, rider=None):
    m = p.shape[0]
    ng = 4

    def body(a_ref, xb_ref, gb_ref, gc_ref, dma_ref, dmb_ref, pw_ref, ps_ref, cw_ref, cb_ref,
             da_ref, dxb_ref, dgb_ref, dgc_ref, dpw_ref, dps_ref, dcw_ref, dcb_ref):
        j = pl.program_id(0)
        first = pl.program_id(1) == 0
        t_idx = lax.broadcasted_iota(jnp.int32, (t_len, TILE), 0)

        pooled = _pool_fwd_math(a_ref[...].astype(F32), j, t_idx).astype(BF16)
        w_bf = pw_ref[...].astype(BF16)
        mixed = jnp.dot(pooled, w_bf, preferred_element_type=F32)
        dm = dma_ref[...].astype(F32)
        dps = jnp.sum(dm * mixed, axis=0, keepdims=True)
        dmixed = (dm * ps_ref[...]).astype(BF16)
        dpw = lax.dot_general(pooled, dmixed, TN_DIMS, preferred_element_type=F32)
        dpooled = lax.dot_general(dmixed, w_bf, NT_DIMS, preferred_element_type=F32)
        da_ref[...] = _pool_bwd_math(dpooled, j, t_idx, t_len).astype(BF16)

        xb = xb_ref[...].astype(F32)
        gb = gb_ref[...].astype(F32)
        gc = gc_ref[...].astype(F32)
        d = dmb_ref[...].astype(F32)
        c = gc * xb
        c1 = _shift_down(c, 1, t_idx)
        c2 = _shift_down(c, 2, t_idx)
        y = cw_ref[0:1, :] * c2 + cw_ref[1:2, :] * c1 + cw_ref[2:3, :] * c + cb_ref[...]
        dgb_ref[...] = (d * y).astype(BF16)
        dy = d * gb
        dc = (cw_ref[2:3, :] * dy + cw_ref[1:2, :] * _shift_up(dy, 1, t_idx, t_len)
              + cw_ref[0:1, :] * _shift_up(dy, 2, t_idx, t_len))
        dgc_ref[...] = (dc * xb).astype(BF16)
        dxb_ref[...] = (dc * gc).astype(BF16)
        dcw = jnp.concatenate([jnp.sum(dy * c2, axis=0, keepdims=True),
                               jnp.sum(dy * c1, axis=0, keepdims=True),
                               jnp.sum(dy * c, axis=0, keepdims=True)], axis=0)
        dcb = jnp.sum(dy, axis=0, keepdims=True)

        @pl.when(first)
        def _():
            dpw_ref[...] = dpw
            dps_ref[...] = dps
            dcw_ref[...] = dcw
            dcb_ref[...] = dcb

        @pl.when(jnp.logical_not(first))
        def _():
            dpw_ref[...] += dpw
            dps_ref[...] += dps
            dcw_ref[...] += dcw
            dcb_ref[...] += dcb

    def col(k):
        return pl.BlockSpec((t_len, TILE), lambda j, s: (s, k * ng + j))

    in_specs = [
        col(0), col(1), col(2), col(3), col(0), col(1),
        pl.BlockSpec((None, TILE, TILE), lambda j, s: (j, 0, 0)),
        pl.BlockSpec((None, 1, TILE), lambda j, s: (j, 0, 0)),
        pl.BlockSpec((3, TILE), lambda j, s: (0, j)),
        pl.BlockSpec((1, TILE), lambda j, s: (0, j)),
    ]
    piece = pl.BlockSpec((t_len, TILE), lambda j, s: (s, j))
    out_specs = [
        piece, piece, piece, piece,
        pl.BlockSpec((None, TILE, TILE), lambda j, s: (j, 0, 0)),
        pl.BlockSpec((None, 1, TILE), lambda j, s: (j, 0, 0)),
        pl.BlockSpec((3, TILE), lambda j, s: (0, j)),
        pl.BlockSpec((1, TILE), lambda j, s: (0, j)),
    ]
    w = ng * TILE
    out_shape = [jax.ShapeDtypeStruct((m, w), BF16)] * 4 + [
        jax.ShapeDtypeStruct((ng, TILE, TILE), F32), jax.ShapeDtypeStruct((ng, 1, TILE), F32),
        jax.ShapeDtypeStruct((3, w), F32), jax.ShapeDtypeStruct((1, w), F32)]
    res, rode = _call(
        body, name="ab_mixer_bwd", grid=(ng, nseq), in_specs=in_specs, out_specs=out_specs, out_shape=out_shape,
        scratch_shapes=[], semantics=("parallel", "arbitrary"),
        args=[p, p, p, p, dmix, dmix, pool_w, pool_scale, conv_w, conv_b], rider=rider)
    return res if rider is None else (res, rode)


SGU_ROWS = 512
INV_SQRT2 = 1.0 / math.sqrt(2.0)
INV_SQRT_2PI = 1.0 / math.sqrt(2.0 * math.pi)


def _gelu(x):
    return 0.5 * x * (1.0 + lax.erf(x * INV_SQRT2))


def _gelu_grad(x):
    return 0.5 * (1.0 + lax.erf(x * INV_SQRT2)) + x * (INV_SQRT_2PI * jnp.exp(-0.5 * x * x))


def _causal_tile(transposed=False):
    r = lax.broadcasted_iota(jnp.int32, (TILE, TILE), 0)
    c = lax.broadcasted_iota(jnp.int32, (TILE, TILE), 1)
    return r <= c if transposed else c <= r


def _sgu_norm(v, g_ref, b_ref):
    mu = jnp.mean(v, axis=-1, keepdims=True)
    xc = v - mu
    rstd = lax.rsqrt(jnp.mean(xc * xc, axis=-1, keepdims=True) + EPS)
    xhat = xc * rstd
    return xhat, rstd, xhat * g_ref[...] + b_ref[...]


def _sgu_fwd(p, norm_g, norm_b, w_s, bias_tile):
    m = p.shape[0]
    ng = 4
    width = ng * TILE

    def body(u_ref, v_ref, g_ref, b_ref, w_ref, bias_ref, o_ref):
        u = _gelu(u_ref[...].astype(F32))
        _, _, vln = _sgu_norm(_gelu(v_ref[...].astype(F32)), g_ref, b_ref)
        vln = vln.astype(BF16)
        causal = _causal_tile()
        for g in range(ng):
            cols = slice(g * TILE, (g + 1) * TILE)
            wg = jnp.where(causal, w_ref[g], 0.0).astype(BF16)
            for n in range(SGU_ROWS // TILE):
                rows = slice(n * TILE, (n + 1) * TILE)
                s = jnp.dot(wg, vln[rows, cols], preferred_element_type=F32) + bias_ref[g]
                o_ref[rows, cols] = (u[rows, cols] * s).astype(BF16)

    vec = pl.BlockSpec((1, width), lambda i: (0, 0))
    tiles = pl.BlockSpec((ng, TILE, TILE), lambda i: (0, 0, 0))
    return pl.pallas_call(
        body, name="sgu_fwd", grid=(m // SGU_ROWS,),
        in_specs=[pl.BlockSpec((SGU_ROWS, width), lambda i: (i, 0)),
                  pl.BlockSpec((SGU_ROWS, width), lambda i: (i, 1)), vec, vec, tiles, tiles],
        out_specs=pl.BlockSpec((SGU_ROWS, width), lambda i: (i, 0)),
        out_shape=jax.ShapeDtypeStruct((m, width), BF16),
        compiler_params=_params(("parallel",)),
    )(p, p, norm_g, norm_b, w_s, bias_tile)


def _sgu_bwd(p, dmix, norm_g, norm_b, w_s, w_s_t, bias_tile):
    m = p.shape[0]
    ng = 4
    width = ng * TILE

    def body(u_ref, v_ref, dc_ref, g_ref, b_ref, w_ref, wt_ref, bias_ref,
             du_ref, dv_ref, dw_ref, dbs_ref, dg_ref, db_ref, ds_scr, dvln_scr):
        u_pre = u_ref[...].astype(F32)
        v_pre = v_ref[...].astype(F32)
        u = _gelu(u_pre)
        xhat, rstd, vln = _sgu_norm(_gelu(v_pre), g_ref, b_ref)
        vln = vln.astype(BF16)
        dc = dc_ref[...].astype(F32)
        causal = _causal_tile()
        ones = jnp.ones((TILE, TILE), BF16)
        first = pl.program_id(0) == 0
        for g in range(ng):
            cols = slice(g * TILE, (g + 1) * TILE)
            wg = jnp.where(causal, w_ref[g], 0.0).astype(BF16)
            wgt = jnp.where(_causal_tile(transposed=True), wt_ref[g], 0.0).astype(BF16)
            dw_acc = jnp.zeros((TILE, TILE), F32)
            dbs_acc = jnp.zeros((TILE, TILE), F32)
            for n in range(SGU_ROWS // TILE):
                rows = slice(n * TILE, (n + 1) * TILE)
                vt = vln[rows, cols]
                s = jnp.dot(wg, vt, preferred_element_type=F32) + bias_ref[g]
                ds_scr[rows, cols] = dc[rows, cols] * s
                ds = (dc[rows, cols] * u[rows, cols]).astype(BF16)
                dw_acc += lax.dot_general(ds, vt, NT_DIMS, preferred_element_type=F32)
                dbs_acc += jnp.dot(ds, ones, preferred_element_type=F32)
                dvln_scr[rows, cols] = jnp.dot(wgt, ds, preferred_element_type=F32)
            dw_g = jnp.where(causal, dw_acc, 0.0)

            @pl.when(first)
            def _():
                dw_ref[g] = dw_g
                dbs_ref[g] = dbs_acc

            @pl.when(jnp.logical_not(first))
            def _():
                dw_ref[g] += dw_g
                dbs_ref[g] += dbs_acc

        du_ref[...] = (ds_scr[...] * _gelu_grad(u_pre)).astype(BF16)
        dvln = dvln_scr[...]
        dxhat = dvln * g_ref[...]
        dv = rstd * (dxhat - jnp.mean(dxhat, axis=-1, keepdims=True)
                     - xhat * jnp.mean(dxhat * xhat, axis=-1, keepdims=True))
        dv_ref[...] = (dv * _gelu_grad(v_pre)).astype(BF16)
        dg_part = jnp.sum(dvln * xhat, axis=0, keepdims=True)
        db_part = jnp.sum(dvln, axis=0, keepdims=True)

        @pl.when(first)
        def _():
            dg_ref[...] = dg_part
            db_ref[...] = db_part

        @pl.when(jnp.logical_not(first))
        def _():
            dg_ref[...] += dg_part
            db_ref[...] += db_part

    vec = pl.BlockSpec((1, width), lambda i: (0, 0))
    tiles = pl.BlockSpec((ng, TILE, TILE), lambda i: (0, 0, 0))
    rows0 = pl.BlockSpec((SGU_ROWS, width), lambda i: (i, 0))
    rows1 = pl.BlockSpec((SGU_ROWS, width), lambda i: (i, 1))
    return pl.pallas_call(
        body, name="sgu_bwd", grid=(m // SGU_ROWS,),
        in_specs=[rows0, rows1, rows0, vec, vec, tiles, tiles, tiles],
        out_specs=[rows0, rows0, tiles, tiles, vec, vec],
        out_shape=[jax.ShapeDtypeStruct((m, width), BF16), jax.ShapeDtypeStruct((m, width), BF16),
                   jax.ShapeDtypeStruct((ng, TILE, TILE), F32), jax.ShapeDtypeStruct((ng, TILE, TILE), F32),
                   jax.ShapeDtypeStruct((1, width), F32), jax.ShapeDtypeStruct((1, width), F32)],
        scratch_shapes=[pltpu.VMEM((SGU_ROWS, width), F32), pltpu.VMEM((SGU_ROWS, width), F32)],
        compiler_params=_params(("arbitrary",)),
    )(p, p, dmix, norm_g, norm_b, w_s, w_s_t, bias_tile)


SB_DH = 64
SB_SCALE = 1.0 / math.sqrt(SB_DH)


SB_BLOCK = 256
SB_SUB = SB_BLOCK // TILE


def _sum_matrix(kind):
    j = lax.broadcasted_iota(jnp.int32, (TILE, 2 * TILE), 0)
    s = lax.broadcasted_iota(jnp.int32, (TILE, 2 * TILE), 1)
    tri = {"after": j > s, "upto": j <= s, "before": j < s}[kind]
    return jnp.where(jnp.logical_or(s >= TILE, tri), 1.0, 0.0).astype(BF16)


def _strict_mask():
    r = lax.broadcasted_iota(jnp.int32, (SB_BLOCK, SB_BLOCK), 0)
    c = lax.broadcasted_iota(jnp.int32, (SB_BLOCK, SB_BLOCK), 1)
    return c < r


def _head_lanes(h):
    lane = lax.broadcasted_iota(jnp.int32, (1, TILE), 1)
    return (lane >= h * SB_DH) & (lane < (h + 1) * SB_DH)


def _softplus(z):
    return jnp.maximum(z, 0.0) + jnp.log(1.0 + jnp.exp(-jnp.abs(z)))


def _sb_fwd(p, nseq, t_len, gather):
    m = p.shape[0]
    npair = 4
    ng = len(gather)
    last_step = nseq * npair - 1

    def body(q_ref, k_ref, v_ref, *rest):
        o_ref, lt_ref = rest[ng:ng + 2]
        kh_ref, vh_ref = rest[2 * ng + 2:2 * ng + 4]
        step = pl.program_id(0) * npair + pl.program_id(1)
        send, forward, finish = _gather_steps(rest[ng + 2:2 * ng + 2], *rest[2 * ng + 4:])
        pl.when(step == 0)(send)
        pl.when(step == (last_step + 1) // 2)(forward)
        for h in range(2):
            keep = _head_lanes(h)
            kh_ref[h] = jnp.where(keep, k_ref[...], 0).astype(BF16)
            vh_ref[h] = jnp.where(keep, v_ref[...], 0).astype(BF16)
        summat = _sum_matrix("after")
        strict = _strict_mask()

        def one_pass(q, row0, diag, state):
            rows = pl.ds(row0, SB_BLOCK)
            z, sp, pieces = [], [], []
            for h in range(2):
                zh = lax.dot_general(q, kh_ref[h, rows, :], NT_DIMS, preferred_element_type=F32) * SB_SCALE
                sph = _softplus(zh)
                logkeep = jnp.where(strict, -sph, 0.0) if diag else -sph
                z.append(zh)
                sp.append(sph)
                pieces += [logkeep[:, b * TILE:(b + 1) * TILE] for b in range(SB_SUB)]
            sums = jnp.dot(jnp.concatenate(pieces, axis=0).astype(BF16), summat, preferred_element_type=F32)
            out = []
            for h in range(2):
                carry, acc = state[2 * h], state[2 * h + 1]
                after = [None] * SB_SUB
                for b in reversed(range(SB_SUB)):
                    part = sums[(h * SB_SUB + b) * SB_BLOCK:(h * SB_SUB + b + 1) * SB_BLOCK]
                    after[b] = part[:, :TILE] + carry
                    carry = carry + part[:, TILE:]
                w = jnp.exp(z[h] - sp[h] + jnp.concatenate(after, axis=1))
                if diag:
                    w = jnp.where(strict, w, 0.0)
                out += [carry, acc + jnp.dot(w.astype(BF16), vh_ref[h, rows, :], preferred_element_type=F32)]
            return tuple(out)

        def q_block(i, _):
            r0 = pl.multiple_of(i * SB_BLOCK, SB_BLOCK)
            q = q_ref[pl.ds(r0, SB_BLOCK), :]
            zero = jnp.zeros((SB_BLOCK, TILE), F32)
            state = one_pass(q, r0, True, (zero,) * 4)
            state = lax.fori_loop(
                0, i, lambda jj, st: one_pass(q, pl.multiple_of((i - 1 - jj) * SB_BLOCK, SB_BLOCK), False, st), state)
            o_ref[pl.ds(r0, SB_BLOCK), :] = (state[1] + state[3]).astype(BF16)
            lt_ref[pl.ds(r0, SB_BLOCK), :] = jnp.where(_head_lanes(0), state[0], state[2])
            return 0

        lax.fori_loop(0, t_len // SB_BLOCK, q_block, 0)
        pl.when(step == last_step)(finish)

    def col(k):
        return pl.BlockSpec((t_len, TILE), lambda s, hp: (s, k * npair + hp))

    out = pl.BlockSpec((t_len, TILE), lambda s, hp: (s, hp))
    res = pl.pallas_call(
        body, name="stickbreak_fwd", grid=(nseq, npair), in_specs=[col(2), col(3), col(4)] + [ANY] * ng,
        out_specs=[out, out] + [ANY] * ng,
        out_shape=[jax.ShapeDtypeStruct((m, npair * TILE), BF16), jax.ShapeDtypeStruct((m, npair * TILE), F32)]
        + [jax.ShapeDtypeStruct(b.shape, b.dtype) for b in gather],
        input_output_aliases={3 + a: 2 + a for a in range(ng)},
        scratch_shapes=[pltpu.VMEM((2, t_len, TILE), BF16), pltpu.VMEM((2, t_len, TILE), BF16)] + _gather_sems(ng),
        compiler_params=pltpu.CompilerParams(dimension_semantics=("arbitrary", "arbitrary"),
                                             vmem_limit_bytes=VMEM_LIMIT_BYTES, has_side_effects=True),
    )(p, p, p, *gather)
    return res[0], res[1], res[2:]


def _sb_bwd(p, dmix, ltot, nseq, t_len, exchange):
    m = p.shape[0]
    npair = 4
    ne = len(exchange)
    last_step = nseq * npair - 1

    def body(q_ref, k_ref, v_ref, do_ref, lt_ref, *rest):
        dq_ref, dk_ref, dv_ref = rest[ne:ne + 3]
        kh_ref, vh_ref, dk_acc, dv_acc = rest[2 * ne + 3:2 * ne + 7]
        step = pl.program_id(0) * npair + pl.program_id(1)
        send, finish = _exchange_steps(rest[:ne], rest[ne + 3:2 * ne + 3], *rest[2 * ne + 7:])
        pl.when(step == 0)(send)
        for h in range(2):
            keep = _head_lanes(h)
            kh_ref[h] = jnp.where(keep, k_ref[...], 0).astype(BF16)
            vh_ref[h] = jnp.where(keep, v_ref[...], 0).astype(BF16)
        dk_acc[...] = jnp.zeros_like(dk_acc)
        dv_acc[...] = jnp.zeros_like(dv_acc)
        sum_upto = _sum_matrix("upto")
        sum_before = _sum_matrix("before")
        strict = _strict_mask()
        lane = lax.broadcasted_iota(jnp.int32, (SB_BLOCK, TILE), 1)

        def running(x, matrix, start):
            pieces = [x[h][:, b * TILE:(b + 1) * TILE] for h in range(2) for b in range(SB_SUB)]
            sums = jnp.dot(jnp.concatenate(pieces, axis=0).astype(BF16), matrix, preferred_element_type=F32)
            wide, ends = [], []
            for h in range(2):
                total, cols = start[h], []
                for b in range(SB_SUB):
                    part = sums[(h * SB_SUB + b) * SB_BLOCK:(h * SB_SUB + b + 1) * SB_BLOCK]
                    cols.append(part[:, :TILE] + total)
                    total = total + part[:, TILE:]
                wide.append(jnp.concatenate(cols, axis=1))
                ends.append(total)
            return wide, ends

        def one_pass(q, do, qh, doh, ltot, row0, diag, state):
            rows = pl.ds(row0, SB_BLOCK)
            z, sp, logkeep = [], [], []
            for h in range(2):
                zh = lax.dot_general(q, kh_ref[h, rows, :], NT_DIMS, preferred_element_type=F32) * SB_SCALE
                sph = _softplus(zh)
                z.append(zh)
                sp.append(sph)
                logkeep.append(jnp.where(strict, -sph, 0.0) if diag else -sph)
            upto, sum_l = running(logkeep, sum_upto, [state[0], state[3]])
            w, g = [], []
            for h in range(2):
                wh = jnp.exp(z[h] - sp[h] + (ltot[h] - upto[h]))
                if diag:
                    wh = jnp.where(strict, wh, 0.0)
                w.append(wh)
                g.append(wh * lax.dot_general(do, vh_ref[h, rows, :], NT_DIMS, preferred_element_type=F32))
            g_before, sum_g = running(g, sum_before, [state[1], state[4]])
            out, dk_new, dv_new = [], 0.0, 0.0
            for h in range(2):
                dz = (g[h] - jnp.exp(z[h] - sp[h]) * (g[h] + g_before[h])) * SB_SCALE
                if diag:
                    dz = jnp.where(strict, dz, 0.0)
                dzb = dz.astype(BF16)
                dq = state[3 * h + 2] + jnp.dot(dzb, kh_ref[h, rows, :], preferred_element_type=F32)
                dk_new = dk_new + lax.dot_general(dzb, qh[h], TN_DIMS, preferred_element_type=F32)
                dv_new = dv_new + lax.dot_general(w[h].astype(BF16), doh[h], TN_DIMS, preferred_element_type=F32)
                out += [sum_l[h], sum_g[h], dq]
            dk_acc[rows, :] += dk_new
            dv_acc[rows, :] += dv_new
            return tuple(out)

        def q_block(i, _):
            r0 = pl.multiple_of(i * SB_BLOCK, SB_BLOCK)
            q = q_ref[pl.ds(r0, SB_BLOCK), :]
            do = do_ref[pl.ds(r0, SB_BLOCK), :]
            lt = lt_ref[pl.ds(r0, SB_BLOCK), :]
            qh, doh, ltot = [], [], []
            for h in range(2):
                keep = _head_lanes(h)
                qh.append(jnp.where(keep, q, 0).astype(BF16))
                doh.append(jnp.where(keep, do, 0).astype(BF16))
                ltot.append(jnp.sum(jnp.where(lane == h * SB_DH, lt, 0.0), axis=1, keepdims=True))
            zero = jnp.zeros((SB_BLOCK, TILE), F32)
            state = lax.fori_loop(
                0, i,
                lambda jj, st: one_pass(q, do, qh, doh, ltot, pl.multiple_of(jj * SB_BLOCK, SB_BLOCK), False, st),
                (zero,) * 6)
            state = one_pass(q, do, qh, doh, ltot, r0, True, state)
            dq_ref[pl.ds(r0, SB_BLOCK), :] = (state[2] + state[5]).astype(BF16)
            return 0

        lax.fori_loop(0, t_len // SB_BLOCK, q_block, 0)
        dk_ref[...] = dk_acc[...].astype(BF16)
        dv_ref[...] = dv_acc[...].astype(BF16)
        pl.when(step == last_step)(finish)

    def col(k):
        return pl.BlockSpec((t_len, TILE), lambda s, hp: (s, k * npair + hp))

    out = pl.BlockSpec((t_len, TILE), lambda s, hp: (s, hp))
    width = npair * TILE
    res = pl.pallas_call(
        body, name="stickbreak_bwd", grid=(nseq, npair),
        in_specs=[col(2), col(3), col(4), col(1), out] + [ANY] * ne, out_specs=[out, out, out] + [ANY] * ne,
        out_shape=[jax.ShapeDtypeStruct((m, width), BF16)] * 3 + _exchange_shapes(exchange),
        scratch_shapes=[pltpu.VMEM((2, t_len, TILE), BF16), pltpu.VMEM((2, t_len, TILE), BF16),
                        pltpu.VMEM((t_len, TILE), F32), pltpu.VMEM((t_len, TILE), F32)] + _exchange_sems(ne),
        compiler_params=pltpu.CompilerParams(dimension_semantics=("arbitrary", "arbitrary"),
                                             vmem_limit_bytes=VMEM_LIMIT_BYTES, has_side_effects=True),
    )(p, p, p, dmix, ltot, *exchange)
    return res[0], res[1], res[2], res[3:]


def _adam_math(w, g, m, v):
    m = ADAM_B1 * m + (1.0 - ADAM_B1) * g
    v = ADAM_B2 * v + (1.0 - ADAM_B2) * (g * g)
    m_hat = m / (1.0 - ADAM_B1 ** ADAM_STEP)
    v_hat = v / (1.0 - ADAM_B2 ** ADAM_STEP)
    delta = -ADAM_LR * (m_hat / (jnp.sqrt(v_hat) + ADAM_EPS) + ADAM_WD * w)
    return delta, m, v


def _cast_place(w, layer, pos, *, name):
    _, r, c = w.shape
    tr = min(r, 256)

    def body(pos_ref, w_ref, o_ref):
        o_ref[...] = w_ref[...].astype(BF16)

    grid_spec = pltpu.PrefetchScalarGridSpec(
        num_scalar_prefetch=1, grid=(r // tr,),
        in_specs=[pl.BlockSpec((None, tr, c), lambda i, pos_ref: (layer, i, 0))],
        out_specs=pl.BlockSpec((None, None, tr, c), lambda i, pos_ref: (0, pos_ref[0], i, 0)))
    return pl.pallas_call(
        body, name=name, grid_spec=grid_spec, out_shape=jax.ShapeDtypeStruct((1, N_CHIP, r, c), BF16),
        compiler_params=_params(("parallel",)),
    )(pos, w)


def _pair_sum(mine, got, pos, *, name):
    l_dim, s_dim, h, c = got.shape
    th = min(h, 512)
    nt = h // th

    def body(pos_ref, a_ref, b_ref, o_ref):
        o_ref[...] = (a_ref[...].astype(F32) + b_ref[...].astype(F32)).astype(BF16)

    spec = pl.BlockSpec((None, None, th, c), lambda l, s, i, pos_ref: (l, s, i, 0))
    grid_spec = pltpu.PrefetchScalarGridSpec(
        num_scalar_prefetch=1, grid=(l_dim, s_dim, nt),
        in_specs=[pl.BlockSpec((None, None, th, c), lambda l, s, i, pos_ref: (l, s, pos_ref[1] * nt + i, 0)), spec],
        out_specs=spec)
    return pl.pallas_call(
        body, name=name, grid_spec=grid_spec, out_shape=jax.ShapeDtypeStruct(got.shape, BF16),
        compiler_params=_params(("parallel",) * 3),
    )(pos, mine, got)


def _chip_sum(sums, landed, pos, *, name):
    l_dim, _, h, c = sums.shape
    th = min(h, 512)
    nt = h // th

    def body(pos_ref, own, r0, r1, r2, o_ref):
        o_ref[...] = ((own[...].astype(F32) + r0[...].astype(F32)) + r1[...].astype(F32)) + r2[...].astype(F32)

    def piece(k):
        return pl.BlockSpec((None, None, th, c), lambda l, i, pos_ref: (l, k, i, 0))

    grid_spec = pltpu.PrefetchScalarGridSpec(
        num_scalar_prefetch=1, grid=(l_dim, nt),
        in_specs=[pl.BlockSpec((None, None, th, c), lambda l, i, pos_ref: (l, pos_ref[0], i, 0)),
                  piece(0), piece(1), piece(2)],
        out_specs=pl.BlockSpec((None, th, c), lambda l, i, pos_ref: (l, pos_ref[1] * nt + i, 0)))
    return pl.pallas_call(
        body, name=name, grid_spec=grid_spec, out_shape=jax.ShapeDtypeStruct((l_dim, 2 * h, c), F32),
        compiler_params=_params(("parallel",) * 2),
    )(pos, sums, landed, landed, landed)


def _adam_big(w, m, v, grads, *, name):
    l_dim, r, c = w.shape
    assert len(grads) == l_dim
    tr = min(r, 256)

    def body(*refs):
        w_ref, m_ref, v_ref = refs[:3]
        g_refs = refs[3:3 + l_dim]
        go_ref, d_ref, mo_ref, vo_ref = refs[3 + l_dim:]
        g = g_refs[0][...]
        for l in range(1, l_dim):
            g = jnp.where(pl.program_id(0) == l, g_refs[l][...], g)
        delta, m_new, v_new = _adam_math(w_ref[...], g, m_ref[...], v_ref[...])
        go_ref[...] = g
        d_ref[...] = delta
        mo_ref[...] = m_new
        vo_ref[...] = v_new

    spec = pl.BlockSpec((None, tr, c), lambda l, i: (l, i, 0))
    gspec = pl.BlockSpec((None, tr, c), lambda l, i: (0, i, 0))
    return pl.pallas_call(
        body, name=name, grid=(l_dim, r // tr), in_specs=[spec] * 3 + [gspec] * l_dim, out_specs=[spec] * 4,
        out_shape=[jax.ShapeDtypeStruct(w.shape, F32)] * 4, compiler_params=_params(("parallel",) * 2),
    )(w, m, v, *grads)


def _position():
    return lax.axis_index("x"), lax.axis_index("y"), lax.axis_index("c")


def _other_chips(x, y):
    return [(1 - x, y), (x, 1 - y), (1 - x, 1 - y)]


def _remote(src, dst, send_sem, recv_sem, device):
    return pltpu.make_async_remote_copy(src_ref=src, dst_ref=dst, send_sem=send_sem, recv_sem=recv_sem,
                                        device_id=device, device_id_type=MESH)


ANY = pl.BlockSpec(memory_space=pl.ANY)


def _gather_sems(n):
    return [pltpu.SemaphoreType.DMA((3 * n,))] * 4


def _gather_steps(outs, send_sems, recv_sems, fwd_send, fwd_recv):
    n = len(outs)
    x, y, c = _position()
    chips = _other_chips(x, y)
    sibling = (x, y, 1 - c)

    def half(a, chip, core):
        h = outs[a].shape[2] // 2
        return outs[a].at[:, 2 * chip[0] + chip[1], pl.ds(core * h, h), :]

    def over_ici(a, k, chip):
        block = half(a, chip, c)
        return _remote(block, block, send_sems.at[3 * a + k], recv_sems.at[3 * a + k], (*chips[k], c))

    def over_d2d(a, k, core):
        block = half(a, chips[k], core)
        return _remote(block, block, fwd_send.at[3 * a + k], fwd_recv.at[3 * a + k], sibling)

    def send():
        for a in range(n):
            for k in range(3):
                over_ici(a, k, (x, y)).start()

    def forward():
        for k in range(3):
            for a in range(n):
                over_ici(a, k, chips[k]).wait_recv()
                over_d2d(a, k, c).start()

    def finish():
        for k in range(3):
            for a in range(n):
                over_d2d(a, k, 1 - c).wait_recv()
        for a in range(n):
            for k in range(3):
                over_ici(a, k, (x, y)).wait_send()
                over_d2d(a, k, c).wait_send()

    return send, forward, finish


def _swap_halves(grads, *, name):
    n = len(grads)

    def body(*refs):
        ins, got = refs[:n], refs[n:2 * n]
        send_sems, recv_sems = refs[2 * n:]
        x, y, c = _position()
        sibling = (x, y, 1 - c)
        copies = []
        for a in range(n):
            h = grads[a].shape[2] // 2
            cp = _remote(ins[a].at[:, :, pl.ds((1 - c) * h, h), :], got[a], send_sems.at[a], recv_sems.at[a], sibling)
            cp.start()
            copies.append(cp)
        for cp in copies:
            cp.wait()

    sem = pltpu.SemaphoreType.DMA((n,))
    return pl.pallas_call(
        body, name=name, in_specs=[ANY] * n, out_specs=[ANY] * n,
        out_shape=[jax.ShapeDtypeStruct(g.shape[:2] + (g.shape[2] // 2, g.shape[3]), g.dtype) for g in grads],
        scratch_shapes=[sem, sem], compiler_params=pltpu.CompilerParams(has_side_effects=True),
    )(*grads)


def _exchange_shapes(sums):
    return [jax.ShapeDtypeStruct((s.shape[0], 3) + s.shape[2:], s.dtype) for s in sums]


def _exchange_sems(n):
    return [pltpu.SemaphoreType.DMA((3 * n,))] * 2


def _exchange_steps(ins, outs, send_sems, recv_sems):
    n = len(ins)
    x, y, c = _position()
    chips = _other_chips(x, y)

    def copy(a, k):
        chip = chips[k]
        return _remote(ins[a].at[:, 2 * chip[0] + chip[1]], outs[a].at[:, k],
                       send_sems.at[3 * a + k], recv_sems.at[3 * a + k], (*chip, c))

    def send():
        for a in range(n):
            for k in range(3):
                copy(a, k).start()

    def finish():
        for a in range(n):
            for k in range(3):
                copy(a, k).wait()

    return send, finish


def _join_halves(bufs, *, name):
    n = len(bufs)

    def body(*refs):
        outs = refs[n:2 * n]
        send_sems, recv_sems = refs[2 * n:]
        x, y, c = _position()
        sibling = (x, y, 1 - c)
        copies = []
        for a in range(n):
            h = bufs[a].shape[1] // 2
            mine = outs[a].at[:, pl.ds(c * h, h), :]
            cp = _remote(mine, mine, send_sems.at[a], recv_sems.at[a], sibling)
            cp.start()
            copies.append((cp, a, h))
        for cp, a, h in copies:
            cp.wait_send()
            got = outs[a].at[:, pl.ds((1 - c) * h, h), :]
            _remote(got, got, send_sems.at[a], recv_sems.at[a], sibling).wait_recv()

    sem = pltpu.SemaphoreType.DMA((n,))
    return pl.pallas_call(
        body, name=name, in_specs=[ANY] * n, out_specs=[ANY] * n,
        out_shape=[jax.ShapeDtypeStruct(b.shape, b.dtype) for b in bufs],
        input_output_aliases={a: a for a in range(n)},
        scratch_shapes=[sem, sem], compiler_params=pltpu.CompilerParams(has_side_effects=True),
    )(*bufs)


def _allreduce_small(packs):
    n = len(packs)

    def body(*refs):
        ins, outs, gath = refs[:n], refs[n:2 * n], refs[2 * n:3 * n]
        send_sems, recv_sems = refs[3 * n:]
        x, y, c = _position()
        me, sibling = (x, y, c), (x, y, 1 - c)
        chips = _other_chips(x, y)

        def slot(a, dev):
            return gath[a].at[4 * dev[0] + 2 * dev[1] + dev[2]]

        def copy(a, k, block, to, src=None):
            return _remote(slot(a, block) if src is None else src, slot(a, block),
                           send_sems.at[7 * a + k], recv_sems.at[7 * a + k], to)

        started = []
        for a in range(n):
            slot(a, me)[...] = ins[a][...]
            first = [copy(a, 0, me, sibling, src=ins[a])]
            first += [copy(a, 1 + k, me, (*chip, c), src=ins[a]) for k, chip in enumerate(chips)]
            for cp in first:
                cp.start()
            started += first
        for a in range(n):
            for k, chip in enumerate(chips):
                copy(a, 1 + k, (*chip, c), me).wait_recv()
                cp = copy(a, 4 + k, (*chip, c), sibling)
                cp.start()
                started.append(cp)
        for a in range(n):
            copy(a, 0, sibling, me).wait_recv()
            for k, chip in enumerate(chips):
                copy(a, 4 + k, (*chip, 1 - c), me).wait_recv()
        for cp in started:
            cp.wait_send()
        for a in range(n):
            total = gath[a][0]
            for d in range(1, N_DEV):
                total = total + gath[a][d]
            outs[a][...] = total

    vmem = pl.BlockSpec(memory_space=pltpu.VMEM)
    sem = pltpu.SemaphoreType.DMA((7 * n,))
    return pl.pallas_call(
        body, name="allreduce_small", in_specs=[vmem] * n, out_specs=[vmem] * n,
        out_shape=[jax.ShapeDtypeStruct(p.shape, p.dtype) for p in packs],
        scratch_shapes=[pltpu.VMEM((N_DEV,) + p.shape, p.dtype) for p in packs] + [sem, sem],
        compiler_params=pltpu.CompilerParams(has_side_effects=True, vmem_limit_bytes=VMEM_LIMIT_BYTES),
    )(*packs)


LOSS_ROW = 1040


def _pad_rows(a, rows=8):
    return jnp.concatenate([a, jnp.zeros((rows - a.shape[0], a.shape[1]), a.dtype)], axis=0)

def _adam_small(wide, mid, narrow, params):
    names = ["mix_norm_g", "mlp_norm_g", "final_norm_g", "conv_b", "conv_w", "sgu_norm_g", "sgu_norm_b",
             "pool_w", "pool_scale", "sgu_w", "sgu_b"]
    n = len(names)

    def body(*refs):
        wide_ref, mid_ref, narrow_ref = refs[:3]
        wmv = refs[3:3 + 3 * n]
        outs = refs[3 + 3 * n:]
        x, y, _ = _position()
        q = 2 * x + y

        def my_quarter(rows):
            parts = [rows[:, s * TILE:(s + 1) * TILE] for s in range(N_CHIP)]
            return jnp.where(q == 0, parts[0], jnp.where(q == 1, parts[1], jnp.where(q == 2, parts[2], parts[3])))

        def tiles(first_row):
            return [((0, g), narrow_ref[first_row + g * TILE:first_row + (g + 1) * TILE, :]) for g in range(4)]

        grads = {
            "mix_norm_g": [((), wide_ref[0:2, :])],
            "mlp_norm_g": [((), wide_ref[8:10, :])],
            "final_norm_g": [((), wide_ref[16:17, :])],
            "conv_b": [((), mid_ref[0:1, :])],
            "conv_w": [((0,), my_quarter(mid_ref[8:11, :]))],
            "sgu_norm_g": [((), my_quarter(mid_ref[16:17, :]))],
            "sgu_norm_b": [((), my_quarter(mid_ref[24:25, :]))],
            "pool_w": tiles(0),
            "sgu_w": tiles(512),
            "pool_scale": [((0,), narrow_ref[1024:1028, :])],
            "sgu_b": [((0,), narrow_ref[1032:1036, :])],
        }
        for i, name in enumerate(names):
            w_ref, m_ref, v_ref = wmv[3 * i:3 * i + 3]
            for lead, g in grads[name]:
                idx = lead + (slice(None), slice(None))
                delta, m_new, v_new = _adam_math(w_ref[idx], g, m_ref[idx], v_ref[idx])
                outs[4 * i][idx] = g
                outs[4 * i + 1][idx] = delta
                outs[4 * i + 2][idx] = m_new
                outs[4 * i + 3][idx] = v_new

    vmem = pl.BlockSpec(memory_space=pltpu.VMEM)
    args, out_shape = [wide, mid, narrow], []
    for name in names:
        w, m, v = params[name]
        args += [w, m, v]
        out_shape += [jax.ShapeDtypeStruct(w.shape, F32)] * 4
    res = pl.pallas_call(
        body, name="adam_small", in_specs=[vmem] * len(args), out_specs=[vmem] * len(out_shape),
        out_shape=out_shape, compiler_params=pltpu.CompilerParams(vmem_limit_bytes=VMEM_LIMIT_BYTES),
    )(*args)
    return {name: res[4 * i:4 * i + 4] for i, name in enumerate(names)}


def _pair_sums(grads, pos, tag):
    got = _swap_halves(grads, name=f"swap_halves_{tag}")
    return [_pair_sum(a, b, pos, name=f"pair_sum_{tag}{i}") for i, (a, b) in enumerate(zip(grads, got))]


def _finish_reduce(sums, landed, pos, tag):
    halves = [_chip_sum(s, r, pos, name=f"chip_sum_{tag}{i}") for i, (s, r) in enumerate(zip(sums, landed))]
    return _join_halves(halves, name=f"join_halves_{tag}")


def kernel(x, mix_norm_g, mlp_norm_g, ab_w_in, pool_w, pool_scale, conv_w, conv_b, ab_w_out, cd_w_in, sgu_norm_g, sgu_norm_b, sgu_w, sgu_b, cd_w_out, mlp_w1, mlp_w2, final_norm_g, loss_target, m_mix_norm_g, m_mlp_norm_g, m_ab_w_in, m_pool_w, m_pool_scale, m_conv_w, m_conv_b, m_ab_w_out, m_cd_w_in, m_sgu_norm_g, m_sgu_norm_b, m_sgu_w, m_sgu_b, m_cd_w_out, m_mlp_w1, m_mlp_w2, m_final_norm_g, v_mix_norm_g, v_mlp_norm_g, v_ab_w_in, v_pool_w, v_pool_scale, v_conv_w, v_conv_b, v_ab_w_out, v_cd_w_in, v_sgu_norm_g, v_sgu_norm_b, v_sgu_w, v_sgu_b, v_cd_w_out, v_mlp_w1, v_mlp_w2, v_final_norm_g):
    nseq, t_len, d = x.shape
    m_tok = nseq * t_len
    h0 = x.reshape(m_tok, d)
    target = loss_target.reshape(m_tok, d)

    x_idx, y_idx = lax.axis_index("x"), lax.axis_index("y")
    q_idx = 2 * x_idx + y_idx
    pos = jnp.stack([q_idx, lax.axis_index("c")]).astype(jnp.int32)
    def shard_buffer(w, layer, tag):
        return _cast_place(w, layer, pos, name=f"cast_place_{tag}")

    def row_block(w):
        return w.reshape(1, 1, -1, w.shape[-1])

    later_weights = [shard_buffer(cd_w_out, 0, "cd_out"), shard_buffer(mlp_w1, 1, "w1_1"),
                     shard_buffer(mlp_w2, 1, "w2_1")]

    pool_w3, pool_scale3 = pool_w[0], pool_scale[0].reshape(4, 1, TILE)
    sgu_w3 = sgu_w[0]
    sgu_w3_t = jnp.swapaxes(sgu_w3, 1, 2)
    sgu_bias_tile = jnp.broadcast_to(sgu_b[0][:, :, None], (4, TILE, TILE))
    conv_w2, conv_b2 = conv_w[0], conv_b
    def place_quarter(v):
        return lax.dynamic_update_slice(jnp.zeros((v.shape[0], 4 * TILE), F32), v, (0, q_idx * TILE))

    sharded_small = jnp.concatenate(
        [place_quarter(conv_w[0]), place_quarter(sgu_norm_g), place_quarter(sgu_norm_b),
         jnp.zeros((3, 4 * TILE), F32)], axis=0)
    sharded_small, = _allreduce_small([sharded_small])
    sharded_small = sharded_small * 0.5
    conv_w_full = sharded_small[0:3]
    sgu_g_full = sharded_small[3:4]
    sgu_b_full = sharded_small[4:5]

    xn0, (w_ab_in,) = _rms_fwd(h0, mix_norm_g[0:1], name="rms_fwd_mix0",
                               rider=("gather", [shard_buffer(ab_w_in, 0, "ab_in")]))
    p_ab, (w_ab_out, w_1_0) = _mm_nn(
        xn0, w_ab_in, 0, out_dtype=BF16, name="ab_in_proj",
        rider=("gather", [shard_buffer(ab_w_out, 0, "ab_out"), shard_buffer(mlp_w1, 0, "w1_0")]))
    w_ab_out = row_block(w_ab_out)
    mix0 = _ab_fwd(p_ab, pool_w3, pool_scale3, conv_w_full, conv_b2, nseq, t_len)
    h1, hn0 = _mm_nn(mix0, w_ab_out, 0, out_dtype=F32, name="ab_out_proj", epilogue="residual", extra=h0,
                     norm_g=mlp_norm_g[0:1])
    (act0, relu0), (w_2_0,) = _mm_nn(hn0, w_1_0, 0, out_dtype=BF16, name="mlp0_up", epilogue="relu2",
                                     rider=("gather", [shard_buffer(mlp_w2, 0, "w2_0")]))
    w_2_0 = row_block(w_2_0)
    (h2, xn1), (w_cd_in,) = _mm_nn(act0, w_2_0, 0, out_dtype=F32, name="mlp0_down", epilogue="residual", extra=h1,
                                   norm_g=mix_norm_g[1:2], rider=("gather", [shard_buffer(cd_w_in, 0, "cd_in")]))

    p_cd = _mm_nn(xn1, w_cd_in, 0, out_dtype=BF16, name="cd_in_proj")
    c_out = _sgu_fwd(p_cd, sgu_g_full, sgu_b_full, sgu_w3, sgu_bias_tile)
    d_out, ltot, (w_cd_out, w_1_1, w_2_1) = _sb_fwd(p_cd, nseq, t_len, later_weights)
    w_cd_out, w_2_1 = row_block(w_cd_out), row_block(w_2_1)
    mix1 = jnp.concatenate([c_out, d_out], axis=1)
    h3, hn1 = _mm_nn(mix1, w_cd_out, 0, out_dtype=F32, name="cd_out_proj", epilogue="residual", extra=h2,
                     norm_g=mlp_norm_g[1:2])
    act1, relu1 = _mm_nn(hn1, w_1_1, 0, out_dtype=BF16, name="mlp1_up", epilogue="relu2")
    h4 = _mm_nn(act1, w_2_1, 0, out_dtype=F32, name="mlp1_down", epilogue="residual", extra=h3)

    dh4, dh4_bf, dg_final, loss_tile = _final_loss(h4, final_norm_g.reshape(1, d), target)

    def mlp_bwd(dh_out, dh_out_bf, h_in, hn, act, relu, w_1, w_2, layer, tag):
        dz = _mm_nt(dh_out_bf, w_2, 0, out_dtype=BF16, name=f"mlp{tag}_down_bwd",
                    epilogue="relu2_bwd", extra=relu)
        g_w2 = _mm_tn(act, dh_out_bf, 1, name=f"mlp{tag}_down_wgrad")
        g_w1 = _mm_tn(hn, dz, N_CHIP, name=f"mlp{tag}_up_wgrad")
        dh_in, dh_in_bf, dg = _mm_nt(dz, w_1, 0, out_dtype=F32, name=f"mlp{tag}_up_bwd", epilogue="rms_bwd",
                                     extra=(h_in, mlp_norm_g[layer:layer + 1], dh_out))
        return dh_in, dh_in_bf, dg, g_w1, g_w2

    def as_pieces(g):
        return g.reshape(1, N_CHIP, -1, g.shape[-1]) if g.shape[1] == 1 else g

    dh3, dh3_bf, dg_mlp1, g_w1_1, g_w2_1 = mlp_bwd(dh4, dh4_bf, h3, hn1, act1, relu1, w_1_1, w_2_1, 1, "1")

    dmix1 = _mm_nt(dh3_bf, w_cd_out, 0, out_dtype=BF16, name="cd_out_bwd")
    g_cd_out = _mm_tn(mix1, dh3_bf, 1, name="cd_out_wgrad")
    sums_a = _pair_sums([g_w1_1, as_pieces(g_w2_1), as_pieces(g_cd_out)], pos, "a")
    du, dv, dsgu_w, dsgu_bs, dsgu_g, dsgu_b = _sgu_bwd(p_cd, dmix1, sgu_g_full, sgu_b_full, sgu_w3, sgu_w3_t,
                                                      sgu_bias_tile)
    dq, dk, dvv, landed_a = _sb_bwd(p_cd, dmix1, ltot, nseq, t_len, sums_a)
    r_w1_1, r_w2_1, r_cd_out = _finish_reduce(sums_a, landed_a, pos, "a")
    dp_cd = jnp.concatenate([du, dv, dq, dk, dvv], axis=1)
    g_cd_in = _mm_tn(xn1, dp_cd, N_CHIP, name="cd_in_wgrad")
    dh2, dh2_bf, dg_mix1 = _mm_nt(dp_cd, w_cd_in, 0, out_dtype=F32, name="cd_in_bwd", epilogue="rms_bwd",
                                  extra=(h2, mix_norm_g[1:2], dh3))

    sums_c = _pair_sums([g_cd_in], pos, "c")
    dz0, landed_c = _mm_nt(dh2_bf, w_2_0, 0, out_dtype=BF16, name="mlp0_down_bwd", epilogue="relu2_bwd", extra=relu0,
                           rider=("exchange", sums_c))
    r_cd_in, = _finish_reduce(sums_c, landed_c, pos, "c")
    g_w2_0 = _mm_tn(act0, dh2_bf, 1, name="mlp0_down_wgrad")
    g_w1_0 = _mm_tn(hn0, dz0, N_CHIP, name="mlp0_up_wgrad")
    sums_d = _pair_sums([as_pieces(g_w2_0)], pos, "d")
    (dh1, dh1_bf, dg_mlp0), landed_d = _mm_nt(dz0, w_1_0, 0, out_dtype=F32, name="mlp0_up_bwd", epilogue="rms_bwd",
                                              extra=(h1, mlp_norm_g[0:1], dh2), rider=("exchange", sums_d))
    r_w2_0, = _finish_reduce(sums_d, landed_d, pos, "d")

    dmix0 = _mm_nt(dh1_bf, w_ab_out, 0, out_dtype=BF16, name="ab_out_bwd")
    g_ab_out = _mm_tn(mix0, dh1_bf, 1, name="ab_out_wgrad")
    sums_e = _pair_sums([g_w1_0], pos, "e")
    (da, dxb, dgb, dgc, dpool_w, dpool_scale, dconv_w, dconv_b), landed_e = _ab_bwd(
        p_ab, dmix0, pool_w3, pool_scale3, conv_w_full, conv_b2, nseq, t_len, rider=("exchange", sums_e))
    r_w1_0, = _finish_reduce(sums_e, landed_e, pos, "e")
    dp_ab = jnp.concatenate([da, dxb, dgb, dgc], axis=1)
    sums_f = _pair_sums([as_pieces(g_ab_out)], pos, "f")
    g_ab_in, landed_f = _mm_tn(xn0, dp_ab, N_CHIP, name="ab_in_wgrad", rider=("exchange", sums_f))
    r_ab_out, = _finish_reduce(sums_f, landed_f, pos, "f")
    sums_g = _pair_sums([g_ab_in], pos, "g")
    (grad_x, _, dg_mix0), landed_g = _mm_nt(dp_ab, w_ab_in, 0, out_dtype=F32, name="ab_in_bwd", epilogue="rms_bwd",
                                            extra=(h0, mix_norm_g[0:1], dh1), rider=("exchange", sums_g))
    r_ab_in, = _finish_reduce(sums_g, landed_g, pos, "g")

    big_out = {
        "ab_w_in": _adam_big(ab_w_in, m_ab_w_in, v_ab_w_in, [r_ab_in], name="adam_ab_w_in"),
        "ab_w_out": _adam_big(ab_w_out, m_ab_w_out, v_ab_w_out, [r_ab_out], name="adam_ab_w_out"),
        "cd_w_in": _adam_big(cd_w_in, m_cd_w_in, v_cd_w_in, [r_cd_in], name="adam_cd_w_in"),
        "cd_w_out": _adam_big(cd_w_out, m_cd_w_out, v_cd_w_out, [r_cd_out], name="adam_cd_w_out"),
        "mlp_w1": _adam_big(mlp_w1, m_mlp_w1, v_mlp_w1, [r_w1_0, r_w1_1], name="adam_mlp_w1"),
        "mlp_w2": _adam_big(mlp_w2, m_mlp_w2, v_mlp_w2, [r_w2_0, r_w2_1], name="adam_mlp_w2"),
    }

    wide = jnp.concatenate([_pad_rows(jnp.concatenate([dg_mix0, dg_mix1], axis=0)),
                            _pad_rows(jnp.concatenate([dg_mlp0, dg_mlp1], axis=0)), _pad_rows(dg_final)], axis=0)
    mid = jnp.concatenate([_pad_rows(dconv_b), _pad_rows(dconv_w), _pad_rows(dsgu_g), _pad_rows(dsgu_b)], axis=0)
    narrow = jnp.concatenate(
        [dpool_w.reshape(4 * TILE, TILE), dsgu_w.reshape(4 * TILE, TILE), _pad_rows(dpool_scale.reshape(4, TILE)),
         _pad_rows(dsgu_bs[:, :, 0]), loss_tile], axis=0)
    wide, mid, narrow = _allreduce_small([wide, mid, narrow])
    small_out = _adam_small(wide, mid, narrow, {
        "mix_norm_g": (mix_norm_g, m_mix_norm_g, v_mix_norm_g),
        "mlp_norm_g": (mlp_norm_g, m_mlp_norm_g, v_mlp_norm_g),
        "final_norm_g": tuple(a.reshape(1, d) for a in (final_norm_g, m_final_norm_g, v_final_norm_g)),
        "conv_b": (conv_b, m_conv_b, v_conv_b),
        "conv_w": (conv_w, m_conv_w, v_conv_w),
        "sgu_norm_g": (sgu_norm_g, m_sgu_norm_g, v_sgu_norm_g),
        "sgu_norm_b": (sgu_norm_b, m_sgu_norm_b, v_sgu_norm_b),
        "pool_w": (pool_w, m_pool_w, v_pool_w),
        "pool_scale": (pool_scale, m_pool_scale, v_pool_scale),
        "sgu_w": (sgu_w, m_sgu_w, v_sgu_w),
        "sgu_b": (sgu_b, m_sgu_b, v_sgu_b),
    })
    small_out["final_norm_g"] = [a.reshape(d) for a in small_out["final_norm_g"]]

    order = ["mix_norm_g", "mlp_norm_g", "ab_w_in", "pool_w", "pool_scale", "conv_w", "conv_b", "ab_w_out",
             "cd_w_in", "sgu_norm_g", "sgu_norm_b", "sgu_w", "sgu_b", "cd_w_out", "mlp_w1", "mlp_w2",
             "final_norm_g"]
    both = {**big_out, **small_out}
    loss = narrow[LOSS_ROW, 0]
    outs = [loss, grad_x.reshape(nseq, t_len, d)]
    for kind in range(4):
        outs += [both[name][kind] for name in order]
    return tuple(outs)
```

```python
import math

import jax
import jax.numpy as jnp
from jax import lax
from jax.experimental import pallas as pl
from jax.experimental.pallas import tpu as pltpu

F32 = jnp.float32
BF16 = jnp.bfloat16
MESH = pl.DeviceIdType.MESH

D_MODEL = 1024
EPS = 1e-6
TILE = 128
N_CHIP = 4
N_DEV = 8
VMEM_LIMIT_BYTES = 56 * 1024 * 1024

ADAM_LR = 0.001
ADAM_B1 = 0.9
ADAM_B2 = 0.999
ADAM_EPS = 1e-08
ADAM_WD = 0.01
ADAM_STEP = 10

NT_DIMS = (((1,), (1,)), ((), ()))
TN_DIMS = (((0,), (0,)), ((), ()))


def _params(sem=None):
    return pltpu.CompilerParams(dimension_semantics=sem, vmem_limit_bytes=VMEM_LIMIT_BYTES)


def _call(body, *, name, grid, in_specs, out_specs, out_shape, scratch_shapes, semantics, args, rider=None):
    if not rider:
        res = pl.pallas_call(body, name=name, grid=grid, in_specs=in_specs, out_specs=out_specs, out_shape=out_shape,
                             scratch_shapes=scratch_shapes, compiler_params=_params(semantics))(*args)
        return list(res), []
    plans = [_rider_plan(kind, arrays) for kind, arrays in rider]
    arrays = [a for _, group in rider for a in group]
    nr, n_in, n_out, n_scr = len(arrays), len(in_specs), len(out_specs), len(scratch_shapes)
    first_out, first_scr = n_in + nr, n_in + nr + n_out + nr
    last_step = math.prod(grid) - 1

    def riding(*refs):
        step = 0
        for axis, size in enumerate(grid):
            step = step * size + pl.program_id(axis)
        steps, at, sem_at = [], 0, first_scr + n_scr
        for (kind, group), (_, sems, _) in zip(rider, plans):
            k = len(group)
            steps.append(_rider_steps(kind, refs[n_in + at:n_in + at + k],
                                      refs[first_out + n_out + at:first_out + n_out + at + k],
                                      refs[sem_at:sem_at + len(sems)]))
            at, sem_at = at + k, sem_at + len(sems)
        for send, _, _ in steps:
            pl.when(step == 0)(send)
        for _, forward, _ in steps:
            if forward is not None:
                pl.when(step == last_step)(forward)
        body(*refs[:n_in], *refs[first_out:first_out + n_out], *refs[first_scr:first_scr + n_scr])
        for _, _, finish in steps:
            pl.when(step == last_step)(finish)

    aliases, at = {}, 0
    for (_, group), (_, _, aliased) in zip(rider, plans):
        if aliased:
            aliases.update({n_in + at + a: n_out + at + a for a in range(len(group))})
        at += len(group)
    res = pl.pallas_call(
        riding, name=name, grid=grid, in_specs=list(in_specs) + [ANY] * nr, out_specs=list(out_specs) + [ANY] * nr,
        out_shape=list(out_shape) + [s for shapes, _, _ in plans for s in shapes],
        scratch_shapes=list(scratch_shapes) + [s for _, sems, _ in plans for s in sems],
        input_output_aliases=aliases,
        compiler_params=pltpu.CompilerParams(dimension_semantics=("arbitrary",) * len(grid),
                                             vmem_limit_bytes=VMEM_LIMIT_BYTES, has_side_effects=True),
    )(*args, *arrays)
    rode, at = [], n_out
    for _, group in rider:
        rode.append(list(res[at:at + len(group)]))
        at += len(group)
    return list(res[:n_out]), rode


def _rider_plan(kind, arrays):
    n = len(arrays)
    same = [jax.ShapeDtypeStruct(a.shape, a.dtype) for a in arrays]
    pair = [pltpu.SemaphoreType.DMA((n,))] * 2
    if kind == "gather":
        return same, _gather_sems(n), True
    if kind == "exchange":
        return _exchange_shapes(arrays), _exchange_sems(n), False
    if kind == "swap":
        return _swap_shapes(arrays), pair, False
    assert kind == "join"
    return same, pair, True


def _rider_steps(kind, ins, outs, sems):
    if kind == "gather":
        return _gather_steps(outs, *sems)
    if kind == "exchange":
        send, finish = _exchange_steps(ins, outs, *sems)
    elif kind == "swap":
        send, finish = _swap_steps(ins, outs, *sems)
    else:
        send, finish = _join_steps(outs, *sems)
    return send, None, finish


def _row_tile(k_dim):
    return 1024 if k_dim <= 1024 else 512


def _mm_nn(a, b4, layer, *, out_dtype, name, epilogue=None, extra=None, norm_g=None, rider=None):
    m, k_dim = a.shape
    _, s_dim, kb, n = b4.shape
    assert kb == k_dim
    tm, tn = _row_tile(k_dim), min(n, 1024)
    assert m % tm == 0 and n % tn == 0
    npb = n // tn
    grid = (m // tm, s_dim * npb)
    n_in = 2 + (extra is not None) + (norm_g is not None)
    two_outputs = norm_g is not None or epilogue == "relu2"
    assert norm_g is None or (tn == s_dim * n and epilogue != "relu2")

    def body(*refs):
        a_ref, b_ref = refs[:2]
        e_ref = refs[2] if extra is not None else None
        g_ref = refs[n_in - 1] if norm_g is not None else None
        o_ref = refs[n_in]
        acc = jnp.dot(a_ref[...], b_ref[...], preferred_element_type=F32)
        if epilogue == "relu2":
            r = jnp.maximum(acc, 0.0)
            refs[n_in + 1][...] = r.astype(BF16)
            acc = r * r
        elif epilogue == "residual":
            acc = acc + e_ref[...]
        o_ref[...] = acc.astype(out_dtype)
        if norm_g is not None:
            rstd = lax.rsqrt(jnp.mean(acc * acc, axis=-1, keepdims=True) + EPS)
            refs[n_in + 1][...] = (acc * rstd * g_ref[...]).astype(BF16)

    in_specs = [
        pl.BlockSpec((tm, k_dim), lambda i, j: (i, 0)),
        pl.BlockSpec((None, None, k_dim, tn), lambda i, j: (layer, j // npb, 0, j % npb)),
    ]
    args = [a, b4]
    if extra is not None:
        in_specs.append(pl.BlockSpec((tm, tn), lambda i, j: (i, j)))
        args.append(extra)
    out_block = pl.BlockSpec((tm, tn), lambda i, j: (i, j))
    out_specs, out_shape = [out_block], [jax.ShapeDtypeStruct((m, s_dim * n), out_dtype)]
    if norm_g is not None:
        in_specs.append(pl.BlockSpec((1, tn), lambda i, j: (0, j)))
        args.append(norm_g)
    if two_outputs:
        out_specs.append(out_block)
        out_shape.append(jax.ShapeDtypeStruct((m, s_dim * n), BF16))
    res, rode = _call(
        body, name=name, grid=grid, in_specs=in_specs, out_specs=out_specs, out_shape=out_shape,
        scratch_shapes=[], semantics=("parallel", "parallel"), args=args, rider=rider)
    res = res if two_outputs else res[0]
    return res if rider is None else (res, rode)


def _mm_nt(a, b4, layer, *, out_dtype, name, epilogue=None, extra=None, rider=None):
    m, k_dim = a.shape
    _, s_dim, n_out, n = b4.shape
    assert k_dim == s_dim * n
    tm, tn = _row_tile(k_dim), min(n_out, 1024)
    assert m % tm == 0 and n_out % tn == 0
    grid = (m // tm, n_out // tn)
    rms = epilogue == "rms_bwd"
    assert not rms or tn == n_out
    extras = [] if extra is None else (list(extra) if rms else [extra])
    n_in = 2 + len(extras)
    n_res = 3 if rms else 1

    def body(*refs):
        a_ref, b_ref = refs[:2]
        e_refs = refs[2:n_in]
        o_ref = refs[n_in]
        acc = lax.dot_general(a_ref[:, 0:n], b_ref[0], NT_DIMS, preferred_element_type=F32)
        for s in range(1, s_dim):
            acc = acc + lax.dot_general(a_ref[:, s * n:(s + 1) * n], b_ref[s], NT_DIMS, preferred_element_type=F32)
        if epilogue == "relu2_bwd":
            acc = acc * (2.0 * e_refs[0][...].astype(F32))
        if not rms:
            o_ref[...] = acc.astype(out_dtype)
        else:
            h_ref, g_ref, dres_ref = e_refs
            dhb_ref, dg_ref = refs[n_in + 1:n_in + 3]
            hv = h_ref[...]
            rstd = lax.rsqrt(jnp.mean(hv * hv, axis=-1, keepdims=True) + EPS)
            xhat = hv * rstd
            dxhat = acc * g_ref[...]
            dh = dres_ref[...] + rstd * (dxhat - xhat * jnp.mean(dxhat * xhat, axis=-1, keepdims=True))
            o_ref[...] = dh
            dhb_ref[...] = dh.astype(BF16)
            dg_part = jnp.sum(acc * xhat, axis=0, keepdims=True)
            first = pl.program_id(0) == 0

            @pl.when(first)
            def _():
                dg_ref[...] = dg_part

            @pl.when(jnp.logical_not(first))
            def _():
                dg_ref[...] += dg_part

    in_specs = [
        pl.BlockSpec((tm, k_dim), lambda i, j: (i, 0)),
        pl.BlockSpec((None, s_dim, tn, n), lambda i, j: (layer, 0, j, 0)),
    ]
    args = [a, b4] + extras
    block = pl.BlockSpec((tm, tn), lambda i, j: (i, j))
    vec = pl.BlockSpec((1, tn), lambda i, j: (0, j))
    if rms:
        in_specs += [block, vec, block]
        out_specs = [block, block, vec]
        out_shape = [jax.ShapeDtypeStruct((m, n_out), F32), jax.ShapeDtypeStruct((m, n_out), BF16),
                     jax.ShapeDtypeStruct((1, n_out), F32)]
    else:
        in_specs += [block] * len(extras)
        out_specs, out_shape = [block], [jax.ShapeDtypeStruct((m, n_out), out_dtype)]
    res, rode = _call(
        body, name=name, grid=grid, in_specs=in_specs, out_specs=out_specs, out_shape=out_shape,
        scratch_shapes=[], semantics=("arbitrary",) * 2 if rms else ("parallel", "parallel"), args=args, rider=rider)
    res = res if rms else res[0]
    return res if rider is None else (res, rode)


def _mm_tn(a, b, s_dim, *, name, rider=None):
    m, k1 = a.shape
    mb, n_all = b.shape
    assert mb == m and n_all % s_dim == 0
    n = n_all // s_dim
    tn, t1 = min(n, 1024), _row_tile(m)
    assert k1 % t1 == 0 and n % tn == 0
    npb = n // tn
    grid = (k1 // t1, s_dim * npb)

    def body(a_ref, b_ref, o_ref):
        o_ref[...] = lax.dot_general(a_ref[...], b_ref[...], TN_DIMS, preferred_element_type=F32).astype(BF16)

    res, rode = _call(
        body, name=name, grid=grid,
        in_specs=[pl.BlockSpec((m, t1), lambda i, j: (0, i)), pl.BlockSpec((m, tn), lambda i, j: (0, j))],
        out_specs=[pl.BlockSpec((None, None, t1, tn), lambda i, j: (0, j // npb, i, j % npb))],
        out_shape=[jax.ShapeDtypeStruct((1, s_dim, k1, n), BF16)],
        scratch_shapes=[], semantics=("parallel", "parallel"), args=[a, b], rider=rider)
    return res[0] if rider is None else (res[0], rode)


ROW_TILE = 512


def _rms_fwd(h, g, *, name, rider=None):
    m, d = h.shape

    def body(h_ref, g_ref, o_ref):
        hv = h_ref[...]
        rstd = lax.rsqrt(jnp.mean(hv * hv, axis=-1, keepdims=True) + EPS)
        o_ref[...] = (hv * rstd * g_ref[...]).astype(BF16)

    res, rode = _call(
        body, name=name, grid=(m // ROW_TILE,),
        in_specs=[pl.BlockSpec((ROW_TILE, d), lambda i: (i, 0)), pl.BlockSpec((1, d), lambda i: (0, 0))],
        out_specs=[pl.BlockSpec((ROW_TILE, d), lambda i: (i, 0))], out_shape=[jax.ShapeDtypeStruct((m, d), BF16)],
        scratch_shapes=[], semantics=("parallel",), args=[h, g], rider=rider)
    return res[0] if rider is None else (res[0], rode)


def _final_loss(h, g, target):
    m, d = h.shape

    def body(h_ref, g_ref, t_ref, dh_ref, dhb_ref, dg_ref, loss_ref):
        hv = h_ref[...]
        gv = g_ref[...]
        rstd = lax.rsqrt(jnp.mean(hv * hv, axis=-1, keepdims=True) + EPS)
        xhat = hv * rstd
        err = xhat * gv - t_ref[...]
        dy = err * (1.0 / d)
        dxhat = dy * gv
        dh = rstd * (dxhat - xhat * jnp.mean(dxhat * xhat, axis=-1, keepdims=True))
        dh_ref[...] = dh
        dhb_ref[...] = dh.astype(BF16)
        dg_part = jnp.sum(dy * xhat, axis=0, keepdims=True)
        sq = jnp.sum(jnp.sum(err * err, axis=1, keepdims=True), axis=0, keepdims=True) * (0.5 / d)
        loss_part = jnp.broadcast_to(sq, (8, TILE))

        @pl.when(pl.program_id(0) == 0)
        def _():
            dg_ref[...] = dg_part
            loss_ref[...] = loss_part

        @pl.when(pl.program_id(0) > 0)
        def _():
            dg_ref[...] += dg_part
            loss_ref[...] += loss_part

    row = pl.BlockSpec((ROW_TILE, d), lambda i: (i, 0))
    vec = pl.BlockSpec((1, d), lambda i: (0, 0))
    return pl.pallas_call(
        body, name="final_loss", grid=(m // ROW_TILE,),
        in_specs=[row, vec, row],
        out_specs=[row, row, vec, pl.BlockSpec((8, TILE), lambda i: (0, 0))],
        out_shape=[jax.ShapeDtypeStruct((m, d), F32), jax.ShapeDtypeStruct((m, d), BF16),
                   jax.ShapeDtypeStruct((1, d), F32), jax.ShapeDtypeStruct((8, TILE), F32)],
        compiler_params=_params(("arbitrary",)),
    )(h, g, target)


def _shift_down(x, s, t_idx):
    return jnp.where(t_idx >= s, pltpu.roll(x, s, 0), 0.0)


def _shift_up(x, s, t_idx, t_len):
    return jnp.where(t_idx < t_len - s, pltpu.roll(x, t_len - s, 0), 0.0)


def _pool_select(group, s2, s4, s8, s16):
    return jnp.where(group == 0, s2, jnp.where(group == 1, s4, jnp.where(group == 2, s8, s16)))


def _pool_count(group, t_idx):
    win = jnp.left_shift(2, group)
    return jnp.minimum(t_idx + 1, win).astype(F32)


def _pool_fwd_math(a, group, t_idx):
    s2 = a + _shift_down(a, 1, t_idx)
    s4 = s2 + _shift_down(s2, 2, t_idx)
    s8 = s4 + _shift_down(s4, 4, t_idx)
    s16 = s8 + _shift_down(s8, 8, t_idx)
    return _pool_select(group, s2, s4, s8, s16) / _pool_count(group, t_idx) - a


def _pool_bwd_math(dpooled, group, t_idx, t_len):
    e = dpooled / _pool_count(group, t_idx)
    s2 = e + _shift_up(e, 1, t_idx, t_len)
    s4 = s2 + _shift_up(s2, 2, t_idx, t_len)
    s8 = s4 + _shift_up(s4, 4, t_idx, t_len)
    s16 = s8 + _shift_up(s8, 8, t_idx, t_len)
    return _pool_select(group, s2, s4, s8, s16) - dpooled


def _conv_fwd_math(c, w_ref, b_ref, t_idx):
    return (w_ref[0:1, :] * _shift_down(c, 2, t_idx) + w_ref[1:2, :] * _shift_down(c, 1, t_idx)
            + w_ref[2:3, :] * c + b_ref[...])


def _ab_fwd(p, pool_w, pool_scale, conv_w, conv_b, nseq, t_len):
    m = p.shape[0]
    ng = 4

    def body(a_ref, xb_ref, gb_ref, gc_ref, pw_ref, ps_ref, cw_ref, cb_ref, o_ref):
        j = pl.program_id(1)
        t_idx = lax.broadcasted_iota(jnp.int32, (t_len, TILE), 0)

        @pl.when(j < ng)
        def _():
            pooled = _pool_fwd_math(a_ref[...].astype(F32), j, t_idx)
            mixed = jnp.dot(pooled.astype(BF16), pw_ref[...].astype(BF16), preferred_element_type=F32)
            o_ref[...] = (mixed * ps_ref[...]).astype(BF16)

        @pl.when(j >= ng)
        def _():
            c = gc_ref[...].astype(F32) * xb_ref[...].astype(F32)
            y = _conv_fwd_math(c, cw_ref, cb_ref, t_idx)
            o_ref[...] = (gb_ref[...].astype(F32) * y).astype(BF16)

    def pool_j(j):
        return jnp.minimum(j, ng - 1)

    def conv_j(j):
        return jnp.maximum(j - ng, 0)

    in_specs = [
        pl.BlockSpec((t_len, TILE), lambda s, j: (s, pool_j(j))),
        pl.BlockSpec((t_len, TILE), lambda s, j: (s, ng + conv_j(j))),
        pl.BlockSpec((t_len, TILE), lambda s, j: (s, 2 * ng + conv_j(j))),
        pl.BlockSpec((t_len, TILE), lambda s, j: (s, 3 * ng + conv_j(j))),
        pl.BlockSpec((None, TILE, TILE), lambda s, j: (pool_j(j), 0, 0)),
        pl.BlockSpec((None, 1, TILE), lambda s, j: (pool_j(j), 0, 0)),
        pl.BlockSpec((3, TILE), lambda s, j: (0, conv_j(j))),
        pl.BlockSpec((1, TILE), lambda s, j: (0, conv_j(j))),
    ]
    return pl.pallas_call(
        body, name="ab_mixer_fwd", grid=(nseq, 2 * ng), in_specs=in_specs,
        out_specs=pl.BlockSpec((t_len, TILE), lambda s, j: (s, j)),
        out_shape=jax.ShapeDtypeStruct((m, 2 * ng * TILE), BF16),
        compiler_params=_params(("parallel", "arbitrary")),
    )(p, p, p, p, pool_w, pool_scale, conv_w, conv_b)


def _ab_bwd(p, dmix, pool_w, pool_scale, conv_w, conv_b, nseq, t_len, rider=None):
    m = p.shape[0]
    ng = 4

    def body(a_ref, xb_ref, gb_ref, gc_ref, dma_ref, dmb_ref, pw_ref, ps_ref, cw_ref, cb_ref,
             da_ref, dxb_ref, dgb_ref, dgc_ref, dpw_ref, dps_ref, dcw_ref, dcb_ref):
        j = pl.program_id(0)
        first = pl.program_id(1) == 0
        t_idx = lax.broadcasted_iota(jnp.int32, (t_len, TILE), 0)

        pooled = _pool_fwd_math(a_ref[...].astype(F32), j, t_idx).astype(BF16)
        w_bf = pw_ref[...].astype(BF16)
        mixed = jnp.dot(pooled, w_bf, preferred_element_type=F32)
        dm = dma_ref[...].astype(F32)
        dps = jnp.sum(dm * mixed, axis=0, keepdims=True)
        dmixed = (dm * ps_ref[...]).astype(BF16)
        dpw = lax.dot_general(pooled, dmixed, TN_DIMS, preferred_element_type=F32)
        dpooled = lax.dot_general(dmixed, w_bf, NT_DIMS, preferred_element_type=F32)
        da_ref[...] = _pool_bwd_math(dpooled, j, t_idx, t_len).astype(BF16)

        xb = xb_ref[...].astype(F32)
        gb = gb_ref[...].astype(F32)
        gc = gc_ref[...].astype(F32)
        d = dmb_ref[...].astype(F32)
        c = gc * xb
        c1 = _shift_down(c, 1, t_idx)
        c2 = _shift_down(c, 2, t_idx)
        y = cw_ref[0:1, :] * c2 + cw_ref[1:2, :] * c1 + cw_ref[2:3, :] * c + cb_ref[...]
        dgb_ref[...] = (d * y).astype(BF16)
        dy = d * gb
        dc = (cw_ref[2:3, :] * dy + cw_ref[1:2, :] * _shift_up(dy, 1, t_idx, t_len)
              + cw_ref[0:1, :] * _shift_up(dy, 2, t_idx, t_len))
        dgc_ref[...] = (dc * xb).astype(BF16)
        dxb_ref[...] = (dc * gc).astype(BF16)
        dcw = jnp.concatenate([jnp.sum(dy * c2, axis=0, keepdims=True),
                               jnp.sum(dy * c1, axis=0, keepdims=True),
                               jnp.sum(dy * c, axis=0, keepdims=True)], axis=0)
        dcb = jnp.sum(dy, axis=0, keepdims=True)

        @pl.when(first)
        def _():
            dpw_ref[...] = dpw
            dps_ref[...] = dps
            dcw_ref[...] = dcw
            dcb_ref[...] = dcb

        @pl.when(jnp.logical_not(first))
        def _():
            dpw_ref[...] += dpw
            dps_ref[...] += dps
            dcw_ref[...] += dcw
            dcb_ref[...] += dcb

    def col(k):
        return pl.BlockSpec((t_len, TILE), lambda j, s: (s, k * ng + j))

    in_specs = [
        col(0), col(1), col(2), col(3), col(0), col(1),
        pl.BlockSpec((None, TILE, TILE), lambda j, s: (j, 0, 0)),
        pl.BlockSpec((None, 1, TILE), lambda j, s: (j, 0, 0)),
        pl.BlockSpec((3, TILE), lambda j, s: (0, j)),
        pl.BlockSpec((1, TILE), lambda j, s: (0, j)),
    ]
    piece = pl.BlockSpec((t_len, TILE), lambda j, s: (s, j))
    out_specs = [
        piece, piece, piece, piece,
        pl.BlockSpec((None, TILE, TILE), lambda j, s: (j, 0, 0)),
        pl.BlockSpec((None, 1, TILE), lambda j, s: (j, 0, 0)),
        pl.BlockSpec((3, TILE), lambda j, s: (0, j)),
        pl.BlockSpec((1, TILE), lambda j, s: (0, j)),
    ]
    w = ng * TILE
    out_shape = [jax.ShapeDtypeStruct((m, w), BF16)] * 4 + [
        jax.ShapeDtypeStruct((ng, TILE, TILE), F32), jax.ShapeDtypeStruct((ng, 1, TILE), F32),
        jax.ShapeDtypeStruct((3, w), F32), jax.ShapeDtypeStruct((1, w), F32)]
    res, rode = _call(
        body, name="ab_mixer_bwd", grid=(ng, nseq), in_specs=in_specs, out_specs=out_specs, out_shape=out_shape,
        scratch_shapes=[], semantics=("parallel", "arbitrary"),
        args=[p, p, p, p, dmix, dmix, pool_w, pool_scale, conv_w, conv_b], rider=rider)
    return res if rider is None else (res, rode)


SGU_ROWS = 512
INV_SQRT2 = 1.0 / math.sqrt(2.0)
INV_SQRT_2PI = 1.0 / math.sqrt(2.0 * math.pi)


def _gelu(x):
    return 0.5 * x * (1.0 + lax.erf(x * INV_SQRT2))


def _gelu_grad(x):
    return 0.5 * (1.0 + lax.erf(x * INV_SQRT2)) + x * (INV_SQRT_2PI * jnp.exp(-0.5 * x * x))


def _causal_tile(transposed=False):
    r = lax.broadcasted_iota(jnp.int32, (TILE, TILE), 0)
    c = lax.broadcasted_iota(jnp.int32, (TILE, TILE), 1)
    return r <= c if transposed else c <= r


def _sgu_norm(v, g_ref, b_ref):
    mu = jnp.mean(v, axis=-1, keepdims=True)
    xc = v - mu
    rstd = lax.rsqrt(jnp.mean(xc * xc, axis=-1, keepdims=True) + EPS)
    xhat = xc * rstd
    return xhat, rstd, xhat * g_ref[...] + b_ref[...]


def _sgu_fwd(p, norm_g, norm_b, w_s, bias_tile):
    m = p.shape[0]
    ng = 4
    width = ng * TILE

    def body(u_ref, v_ref, g_ref, b_ref, w_ref, bias_ref, o_ref):
        u = _gelu(u_ref[...].astype(F32))
        _, _, vln = _sgu_norm(_gelu(v_ref[...].astype(F32)), g_ref, b_ref)
        vln = vln.astype(BF16)
        causal = _causal_tile()
        for g in range(ng):
            cols = slice(g * TILE, (g + 1) * TILE)
            wg = jnp.where(causal, w_ref[g], 0.0).astype(BF16)
            for n in range(SGU_ROWS // TILE):
                rows = slice(n * TILE, (n + 1) * TILE)
                s = jnp.dot(wg, vln[rows, cols], preferred_element_type=F32) + bias_ref[g]
                o_ref[rows, cols] = (u[rows, cols] * s).astype(BF16)

    vec = pl.BlockSpec((1, width), lambda i: (0, 0))
    tiles = pl.BlockSpec((ng, TILE, TILE), lambda i: (0, 0, 0))
    return pl.pallas_call(
        body, name="sgu_fwd", grid=(m // SGU_ROWS,),
        in_specs=[pl.BlockSpec((SGU_ROWS, width), lambda i: (i, 0)),
                  pl.BlockSpec((SGU_ROWS, width), lambda i: (i, 1)), vec, vec, tiles, tiles],
        out_specs=pl.BlockSpec((SGU_ROWS, width), lambda i: (i, 0)),
        out_shape=jax.ShapeDtypeStruct((m, width), BF16),
        compiler_params=_params(("parallel",)),
    )(p, p, norm_g, norm_b, w_s, bias_tile)


def _sgu_bwd(p, dmix, norm_g, norm_b, w_s, w_s_t, bias_tile):
    m = p.shape[0]
    ng = 4
    width = ng * TILE

    def body(u_ref, v_ref, dc_ref, g_ref, b_ref, w_ref, wt_ref, bias_ref,
             du_ref, dv_ref, dw_ref, dbs_ref, dg_ref, db_ref, ds_scr, dvln_scr):
        u_pre = u_ref[...].astype(F32)
        v_pre = v_ref[...].astype(F32)
        u = _gelu(u_pre)
        xhat, rstd, vln = _sgu_norm(_gelu(v_pre), g_ref, b_ref)
        vln = vln.astype(BF16)
        dc = dc_ref[...].astype(F32)
        causal = _causal_tile()
        ones = jnp.ones((TILE, TILE), BF16)
        first = pl.program_id(0) == 0
        for g in range(ng):
            cols = slice(g * TILE, (g + 1) * TILE)
            wg = jnp.where(causal, w_ref[g], 0.0).astype(BF16)
            wgt = jnp.where(_causal_tile(transposed=True), wt_ref[g], 0.0).astype(BF16)
            dw_acc = jnp.zeros((TILE, TILE), F32)
            dbs_acc = jnp.zeros((TILE, TILE), F32)
            for n in range(SGU_ROWS // TILE):
                rows = slice(n * TILE, (n + 1) * TILE)
                vt = vln[rows, cols]
                s = jnp.dot(wg, vt, preferred_element_type=F32) + bias_ref[g]
                ds_scr[rows, cols] = dc[rows, cols] * s
                ds = (dc[rows, cols] * u[rows, cols]).astype(BF16)
                dw_acc += lax.dot_general(ds, vt, NT_DIMS, preferred_element_type=F32)
                dbs_acc += jnp.dot(ds, ones, preferred_element_type=F32)
                dvln_scr[rows, cols] = jnp.dot(wgt, ds, preferred_element_type=F32)
            dw_g = jnp.where(causal, dw_acc, 0.0)

            @pl.when(first)
            def _():
                dw_ref[g] = dw_g
                dbs_ref[g] = dbs_acc

            @pl.when(jnp.logical_not(first))
            def _():
                dw_ref[g] += dw_g
                dbs_ref[g] += dbs_acc

        du_ref[...] = (ds_scr[...] * _gelu_grad(u_pre)).astype(BF16)
        dvln = dvln_scr[...]
        dxhat = dvln * g_ref[...]
        dv = rstd * (dxhat - jnp.mean(dxhat, axis=-1, keepdims=True)
                     - xhat * jnp.mean(dxhat * xhat, axis=-1, keepdims=True))
        dv_ref[...] = (dv * _gelu_grad(v_pre)).astype(BF16)
        dg_part = jnp.sum(dvln * xhat, axis=0, keepdims=True)
        db_part = jnp.sum(dvln, axis=0, keepdims=True)

        @pl.when(first)
        def _():
            dg_ref[...] = dg_part
            db_ref[...] = db_part

        @pl.when(jnp.logical_not(first))
        def _():
            dg_ref[...] += dg_part
            db_ref[...] += db_part

    vec = pl.BlockSpec((1, width), lambda i: (0, 0))
    tiles = pl.BlockSpec((ng, TILE, TILE), lambda i: (0, 0, 0))
    rows0 = pl.BlockSpec((SGU_ROWS, width), lambda i: (i, 0))
    rows1 = pl.BlockSpec((SGU_ROWS, width), lambda i: (i, 1))
    return pl.pallas_call(
        body, name="sgu_bwd", grid=(m // SGU_ROWS,),
        in_specs=[rows0, rows1, rows0, vec, vec, tiles, tiles, tiles],
        out_specs=[rows0, rows0, tiles, tiles, vec, vec],
        out_shape=[jax.ShapeDtypeStruct((m, width), BF16), jax.ShapeDtypeStruct((m, width), BF16),
                   jax.ShapeDtypeStruct((ng, TILE, TILE), F32), jax.ShapeDtypeStruct((ng, TILE, TILE), F32),
                   jax.ShapeDtypeStruct((1, width), F32), jax.ShapeDtypeStruct((1, width), F32)],
        scratch_shapes=[pltpu.VMEM((SGU_ROWS, width), F32), pltpu.VMEM((SGU_ROWS, width), F32)],
        compiler_params=_params(("arbitrary",)),
    )(p, p, dmix, norm_g, norm_b, w_s, w_s_t, bias_tile)


SB_DH = 64
SB_SCALE = 1.0 / math.sqrt(SB_DH)


SB_BLOCK = 256
SB_SUB = SB_BLOCK // TILE


def _sum_matrix(kind):
    j = lax.broadcasted_iota(jnp.int32, (TILE, 2 * TILE), 0)
    s = lax.broadcasted_iota(jnp.int32, (TILE, 2 * TILE), 1)
    tri = {"after": j > s, "upto": j <= s, "before": j < s}[kind]
    return jnp.where(jnp.logical_or(s >= TILE, tri), 1.0, 0.0).astype(BF16)


def _strict_mask():
    r = lax.broadcasted_iota(jnp.int32, (SB_BLOCK, SB_BLOCK), 0)
    c = lax.broadcasted_iota(jnp.int32, (SB_BLOCK, SB_BLOCK), 1)
    return c < r


def _head_lanes(h):
    lane = lax.broadcasted_iota(jnp.int32, (1, TILE), 1)
    return (lane >= h * SB_DH) & (lane < (h + 1) * SB_DH)


def _softplus(z):
    return jnp.maximum(z, 0.0) + jnp.log(1.0 + jnp.exp(-jnp.abs(z)))


def _sb_fwd(p, nseq, t_len, gather):
    m = p.shape[0]
    npair = 4
    ng = len(gather)
    last_step = nseq * npair - 1

    def body(q_ref, k_ref, v_ref, *rest):
        o_ref, lt_ref = rest[ng:ng + 2]
        kh_ref, vh_ref = rest[2 * ng + 2:2 * ng + 4]
        step = pl.program_id(0) * npair + pl.program_id(1)
        send, forward, finish = _gather_steps(rest[ng + 2:2 * ng + 2], *rest[2 * ng + 4:])
        pl.when(step == 0)(send)
        pl.when(step == (last_step + 1) // 2)(forward)
        for h in range(2):
            keep = _head_lanes(h)
            kh_ref[h] = jnp.where(keep, k_ref[...], 0).astype(BF16)
            vh_ref[h] = jnp.where(keep, v_ref[...], 0).astype(BF16)
        summat = _sum_matrix("after")
        strict = _strict_mask()

        def one_pass(q, row0, diag, state):
            rows = pl.ds(row0, SB_BLOCK)
            z, sp, pieces = [], [], []
            for h in range(2):
                zh = lax.dot_general(q, kh_ref[h, rows, :], NT_DIMS, preferred_element_type=F32) * SB_SCALE
                sph = _softplus(zh)
                logkeep = jnp.where(strict, -sph, 0.0) if diag else -sph
                z.append(zh)
                sp.append(sph)
                pieces += [logkeep[:, b * TILE:(b + 1) * TILE] for b in range(SB_SUB)]
            sums = jnp.dot(jnp.concatenate(pieces, axis=0).astype(BF16), summat, preferred_element_type=F32)
            out = []
            for h in range(2):
                carry, acc = state[2 * h], state[2 * h + 1]
                after = [None] * SB_SUB
                for b in reversed(range(SB_SUB)):
                    part = sums[(h * SB_SUB + b) * SB_BLOCK:(h * SB_SUB + b + 1) * SB_BLOCK]
                    after[b] = part[:, :TILE] + carry
                    carry = carry + part[:, TILE:]
                w = jnp.exp(z[h] - sp[h] + jnp.concatenate(after, axis=1))
                if diag:
                    w = jnp.where(strict, w, 0.0)
                out += [carry, acc + jnp.dot(w.astype(BF16), vh_ref[h, rows, :], preferred_element_type=F32)]
            return tuple(out)

        def q_block(i, _):
            r0 = pl.multiple_of(i * SB_BLOCK, SB_BLOCK)
            q = q_ref[pl.ds(r0, SB_BLOCK), :]
            zero = jnp.zeros((SB_BLOCK, TILE), F32)
            state = one_pass(q, r0, True, (zero,) * 4)
            state = lax.fori_loop(
                0, i, lambda jj, st: one_pass(q, pl.multiple_of((i - 1 - jj) * SB_BLOCK, SB_BLOCK), False, st), state)
            o_ref[pl.ds(r0, SB_BLOCK), :] = (state[1] + state[3]).astype(BF16)
            lt_ref[pl.ds(r0, SB_BLOCK), :] = jnp.where(_head_lanes(0), state[0], state[2])
            return 0

        lax.fori_loop(0, t_len // SB_BLOCK, q_block, 0)
        pl.when(step == last_step)(finish)

    def col(k):
        return pl.BlockSpec((t_len, TILE), lambda s, hp: (s, k * npair + hp))

    out = pl.BlockSpec((t_len, TILE), lambda s, hp: (s, hp))
    res = pl.pallas_call(
        body, name="stickbreak_fwd", grid=(nseq, npair), in_specs=[col(2), col(3), col(4)] + [ANY] * ng,
        out_specs=[out, out] + [ANY] * ng,
        out_shape=[jax.ShapeDtypeStruct((m, npair * TILE), BF16), jax.ShapeDtypeStruct((m, npair * TILE), F32)]
        + [jax.ShapeDtypeStruct(b.shape, b.dtype) for b in gather],
        input_output_aliases={3 + a: 2 + a for a in range(ng)},
        scratch_shapes=[pltpu.VMEM((2, t_len, TILE), BF16), pltpu.VMEM((2, t_len, TILE), BF16)] + _gather_sems(ng),
        compiler_params=pltpu.CompilerParams(dimension_semantics=("arbitrary", "arbitrary"),
                                             vmem_limit_bytes=VMEM_LIMIT_BYTES, has_side_effects=True),
    )(p, p, p, *gather)
    return res[0], res[1], res[2:]


def _sb_bwd(p, dmix, ltot, nseq, t_len, exchange):
    m = p.shape[0]
    npair = 4
    ne = len(exchange)
    last_step = nseq * npair - 1

    def body(q_ref, k_ref, v_ref, do_ref, lt_ref, *rest):
        dq_ref, dk_ref, dv_ref = rest[ne:ne + 3]
        kh_ref, vh_ref, dk_acc, dv_acc = rest[2 * ne + 3:2 * ne + 7]
        step = pl.program_id(0) * npair + pl.program_id(1)
        send, finish = _exchange_steps(rest[:ne], rest[ne + 3:2 * ne + 3], *rest[2 * ne + 7:])
        pl.when(step == 0)(send)
        for h in range(2):
            keep = _head_lanes(h)
            kh_ref[h] = jnp.where(keep, k_ref[...], 0).astype(BF16)
            vh_ref[h] = jnp.where(keep, v_ref[...], 0).astype(BF16)
        dk_acc[...] = jnp.zeros_like(dk_acc)
        dv_acc[...] = jnp.zeros_like(dv_acc)
        sum_upto = _sum_matrix("upto")
        sum_before = _sum_matrix("before")
        strict = _strict_mask()
        lane = lax.broadcasted_iota(jnp.int32, (SB_BLOCK, TILE), 1)

        def running(x, matrix, start):
            pieces = [x[h][:, b * TILE:(b + 1) * TILE] for h in range(2) for b in range(SB_SUB)]
            sums = jnp.dot(jnp.concatenate(pieces, axis=0).astype(BF16), matrix, preferred_element_type=F32)
            wide, ends = [], []
            for h in range(2):
                total, cols = start[h], []
                for b in range(SB_SUB):
                    part = sums[(h * SB_SUB + b) * SB_BLOCK:(h * SB_SUB + b + 1) * SB_BLOCK]
                    cols.append(part[:, :TILE] + total)
                    total = total + part[:, TILE:]
                wide.append(jnp.concatenate(cols, axis=1))
                ends.append(total)
            return wide, ends

        def one_pass(q, do, qh, doh, ltot, row0, diag, state):
            rows = pl.ds(row0, SB_BLOCK)
            z, sp, logkeep = [], [], []
            for h in range(2):
                zh = lax.dot_general(q, kh_ref[h, rows, :], NT_DIMS, preferred_element_type=F32) * SB_SCALE
                sph = _softplus(zh)
                z.append(zh)
                sp.append(sph)
                logkeep.append(jnp.where(strict, -sph, 0.0) if diag else -sph)
            upto, sum_l = running(logkeep, sum_upto, [state[0], state[3]])
            w, g = [], []
            for h in range(2):
                wh = jnp.exp(z[h] - sp[h] + (ltot[h] - upto[h]))
                if diag:
                    wh = jnp.where(strict, wh, 0.0)
                w.append(wh)
                g.append(wh * lax.dot_general(do, vh_ref[h, rows, :], NT_DIMS, preferred_element_type=F32))
            g_before, sum_g = running(g, sum_before, [state[1], state[4]])
            out, dk_new, dv_new = [], 0.0, 0.0
            for h in range(2):
                dz = (g[h] - jnp.exp(z[h] - sp[h]) * (g[h] + g_before[h])) * SB_SCALE
                if diag:
                    dz = jnp.where(strict, dz, 0.0)
                dzb = dz.astype(BF16)
                dq = state[3 * h + 2] + jnp.dot(dzb, kh_ref[h, rows, :], preferred_element_type=F32)
                dk_new = dk_new + lax.dot_general(dzb, qh[h], TN_DIMS, preferred_element_type=F32)
                dv_new = dv_new + lax.dot_general(w[h].astype(BF16), doh[h], TN_DIMS, preferred_element_type=F32)
                out += [sum_l[h], sum_g[h], dq]
            dk_acc[rows, :] += dk_new
            dv_acc[rows, :] += dv_new
            return tuple(out)

        def q_block(i, _):
            r0 = pl.multiple_of(i * SB_BLOCK, SB_BLOCK)
            q = q_ref[pl.ds(r0, SB_BLOCK), :]
            do = do_ref[pl.ds(r0, SB_BLOCK), :]
            lt = lt_ref[pl.ds(r0, SB_BLOCK), :]
            qh, doh, ltot = [], [], []
            for h in range(2):
                keep = _head_lanes(h)
                qh.append(jnp.where(keep, q, 0).astype(BF16))
                doh.append(jnp.where(keep, do, 0).astype(BF16))
                ltot.append(jnp.sum(jnp.where(lane == h * SB_DH, lt, 0.0), axis=1, keepdims=True))
            zero = jnp.zeros((SB_BLOCK, TILE), F32)
            state = lax.fori_loop(
                0, i,
                lambda jj, st: one_pass(q, do, qh, doh, ltot, pl.multiple_of(jj * SB_BLOCK, SB_BLOCK), False, st),
                (zero,) * 6)
            state = one_pass(q, do, qh, doh, ltot, r0, True, state)
            dq_ref[pl.ds(r0, SB_BLOCK), :] = (state[2] + state[5]).astype(BF16)
            return 0

        lax.fori_loop(0, t_len // SB_BLOCK, q_block, 0)
        dk_ref[...] = dk_acc[...].astype(BF16)
        dv_ref[...] = dv_acc[...].astype(BF16)
        pl.when(step == last_step)(finish)

    def col(k):
        return pl.BlockSpec((t_len, TILE), lambda s, hp: (s, k * npair + hp))

    out = pl.BlockSpec((t_len, TILE), lambda s, hp: (s, hp))
    width = npair * TILE
    res = pl.pallas_call(
        body, name="stickbreak_bwd", grid=(nseq, npair),
        in_specs=[col(2), col(3), col(4), col(1), out] + [ANY] * ne, out_specs=[out, out, out] + [ANY] * ne,
        out_shape=[jax.ShapeDtypeStruct((m, width), BF16)] * 3 + _exchange_shapes(exchange),
        scratch_shapes=[pltpu.VMEM((2, t_len, TILE), BF16), pltpu.VMEM((2, t_len, TILE), BF16),
                        pltpu.VMEM((t_len, TILE), F32), pltpu.VMEM((t_len, TILE), F32)] + _exchange_sems(ne),
        compiler_params=pltpu.CompilerParams(dimension_semantics=("arbitrary", "arbitrary"),
                                             vmem_limit_bytes=VMEM_LIMIT_BYTES, has_side_effects=True),
    )(p, p, p, dmix, ltot, *exchange)
    return res[0], res[1], res[2], res[3:]


def _adam_math(w, g, m, v):
    m = ADAM_B1 * m + (1.0 - ADAM_B1) * g
    v = ADAM_B2 * v + (1.0 - ADAM_B2) * (g * g)
    m_hat = m / (1.0 - ADAM_B1 ** ADAM_STEP)
    v_hat = v / (1.0 - ADAM_B2 ** ADAM_STEP)
    delta = -ADAM_LR * (m_hat / (jnp.sqrt(v_hat) + ADAM_EPS) + ADAM_WD * w)
    return delta, m, v


def _cast_place(w, layer, pos, *, name):
    _, r, c = w.shape
    tr = min(r, 256)

    def body(pos_ref, w_ref, o_ref):
        o_ref[...] = w_ref[...].astype(BF16)

    grid_spec = pltpu.PrefetchScalarGridSpec(
        num_scalar_prefetch=1, grid=(r // tr,),
        in_specs=[pl.BlockSpec((None, tr, c), lambda i, pos_ref: (layer, i, 0))],
        out_specs=pl.BlockSpec((None, None, tr, c), lambda i, pos_ref: (0, pos_ref[0], i, 0)))
    return pl.pallas_call(
        body, name=name, grid_spec=grid_spec, out_shape=jax.ShapeDtypeStruct((1, N_CHIP, r, c), BF16),
        compiler_params=_params(("parallel",)),
    )(pos, w)


def _pair_sum(mine, got, pos, *, name):
    l_dim, s_dim, h, c = got.shape
    th = min(h, 512)
    nt = h // th

    def body(pos_ref, a_ref, b_ref, o_ref):
        o_ref[...] = (a_ref[...].astype(F32) + b_ref[...].astype(F32)).astype(BF16)

    spec = pl.BlockSpec((None, None, th, c), lambda l, s, i, pos_ref: (l, s, i, 0))
    grid_spec = pltpu.PrefetchScalarGridSpec(
        num_scalar_prefetch=1, grid=(l_dim, s_dim, nt),
        in_specs=[pl.BlockSpec((None, None, th, c), lambda l, s, i, pos_ref: (l, s, pos_ref[1] * nt + i, 0)), spec],
        out_specs=spec)
    return pl.pallas_call(
        body, name=name, grid_spec=grid_spec, out_shape=jax.ShapeDtypeStruct(got.shape, BF16),
        compiler_params=_params(("parallel",) * 3),
    )(pos, mine, got)


def _chip_sum(sums, landed, pos, *, name):
    l_dim, _, h, c = sums.shape
    th = min(h, 512)
    nt = h // th

    def body(pos_ref, own, r0, r1, r2, o_ref):
        o_ref[...] = ((own[...].astype(F32) + r0[...].astype(F32)) + r1[...].astype(F32)) + r2[...].astype(F32)

    def piece(k):
        return pl.BlockSpec((None, None, th, c), lambda l, i, pos_ref: (l, k, i, 0))

    grid_spec = pltpu.PrefetchScalarGridSpec(
        num_scalar_prefetch=1, grid=(l_dim, nt),
        in_specs=[pl.BlockSpec((None, None, th, c), lambda l, i, pos_ref: (l, pos_ref[0], i, 0)),
                  piece(0), piece(1), piece(2)],
        out_specs=pl.BlockSpec((None, th, c), lambda l, i, pos_ref: (l, pos_ref[1] * nt + i, 0)))
    return pl.pallas_call(
        body, name=name, grid_spec=grid_spec, out_shape=jax.ShapeDtypeStruct((l_dim, 2 * h, c), F32),
        compiler_params=_params(("parallel",) * 2),
    )(pos, sums, landed, landed, landed)


def _adam_big(w, m, v, grads, *, name):
    l_dim, r, c = w.shape
    assert len(grads) == l_dim
    tr = min(r, 256)

    def body(*refs):
        w_ref, m_ref, v_ref = refs[:3]
        g_refs = refs[3:3 + l_dim]
        go_ref, d_ref, mo_ref, vo_ref = refs[3 + l_dim:]
        g = g_refs[0][...]
        for l in range(1, l_dim):
            g = jnp.where(pl.program_id(0) == l, g_refs[l][...], g)
        delta, m_new, v_new = _adam_math(w_ref[...], g, m_ref[...], v_ref[...])
        go_ref[...] = g
        d_ref[...] = delta
        mo_ref[...] = m_new
        vo_ref[...] = v_new

    spec = pl.BlockSpec((None, tr, c), lambda l, i: (l, i, 0))
    gspec = pl.BlockSpec((None, tr, c), lambda l, i: (0, i, 0))
    return pl.pallas_call(
        body, name=name, grid=(l_dim, r // tr), in_specs=[spec] * 3 + [gspec] * l_dim, out_specs=[spec] * 4,
        out_shape=[jax.ShapeDtypeStruct(w.shape, F32)] * 4, compiler_params=_params(("parallel",) * 2),
    )(w, m, v, *grads)


def _position():
    return lax.axis_index("x"), lax.axis_index("y"), lax.axis_index("c")


def _other_chips(x, y):
    return [(1 - x, y), (x, 1 - y), (1 - x, 1 - y)]


def _remote(src, dst, send_sem, recv_sem, device):
    return pltpu.make_async_remote_copy(src_ref=src, dst_ref=dst, send_sem=send_sem, recv_sem=recv_sem,
                                        device_id=device, device_id_type=MESH)


ANY = pl.BlockSpec(memory_space=pl.ANY)


def _gather_sems(n):
    return [pltpu.SemaphoreType.DMA((3 * n,))] * 4


def _gather_steps(outs, send_sems, recv_sems, fwd_send, fwd_recv):
    n = len(outs)
    x, y, c = _position()
    chips = _other_chips(x, y)
    sibling = (x, y, 1 - c)

    def half(a, chip, core):
        h = outs[a].shape[2] // 2
        return outs[a].at[:, 2 * chip[0] + chip[1], pl.ds(core * h, h), :]

    def over_ici(a, k, chip):
        block = half(a, chip, c)
        return _remote(block, block, send_sems.at[3 * a + k], recv_sems.at[3 * a + k], (*chips[k], c))

    def over_d2d(a, k, core):
        block = half(a, chips[k], core)
        return _remote(block, block, fwd_send.at[3 * a + k], fwd_recv.at[3 * a + k], sibling)

    def send():
        for a in range(n):
            for k in range(3):
                over_ici(a, k, (x, y)).start()

    def forward():
        for k in range(3):
            for a in range(n):
                over_ici(a, k, chips[k]).wait_recv()
                over_d2d(a, k, c).start()

    def finish():
        for k in range(3):
            for a in range(n):
                over_d2d(a, k, 1 - c).wait_recv()
        for a in range(n):
            for k in range(3):
                over_ici(a, k, (x, y)).wait_send()
                over_d2d(a, k, c).wait_send()

    return send, forward, finish


def _swap_halves(grads, *, name):
    n = len(grads)

    def body(*refs):
        send, finish = _swap_steps(refs[:n], refs[n:2 * n], *refs[2 * n:])
        send()
        finish()

    sem = pltpu.SemaphoreType.DMA((n,))
    return pl.pallas_call(
        body, name=name, in_specs=[ANY] * n, out_specs=[ANY] * n, out_shape=_swap_shapes(grads),
        scratch_shapes=[sem, sem], compiler_params=pltpu.CompilerParams(has_side_effects=True),
    )(*grads)


def _swap_shapes(grads):
    return [jax.ShapeDtypeStruct(g.shape[:2] + (g.shape[2] // 2, g.shape[3]), g.dtype) for g in grads]


def _swap_steps(ins, outs, send_sems, recv_sems):
    x, y, c = _position()

    def copy(a):
        h = ins[a].shape[2] // 2
        return _remote(ins[a].at[:, :, pl.ds((1 - c) * h, h), :], outs[a], send_sems.at[a], recv_sems.at[a],
                       (x, y, 1 - c))

    def send():
        for a in range(len(ins)):
            copy(a).start()

    def finish():
        for a in range(len(ins)):
            copy(a).wait()

    return send, finish


def _exchange_shapes(sums):
    return [jax.ShapeDtypeStruct((s.shape[0], 3) + s.shape[2:], s.dtype) for s in sums]


def _exchange_sems(n):
    return [pltpu.SemaphoreType.DMA((3 * n,))] * 2


def _exchange_steps(ins, outs, send_sems, recv_sems):
    n = len(ins)
    x, y, c = _position()
    chips = _other_chips(x, y)

    def copy(a, k):
        chip = chips[k]
        return _remote(ins[a].at[:, 2 * chip[0] + chip[1]], outs[a].at[:, k],
                       send_sems.at[3 * a + k], recv_sems.at[3 * a + k], (*chip, c))

    def send():
        for a in range(n):
            for k in range(3):
                copy(a, k).start()

    def finish():
        for a in range(n):
            for k in range(3):
                copy(a, k).wait()

    return send, finish


def _join_halves(bufs, *, name):
    n = len(bufs)

    def body(*refs):
        send, finish = _join_steps(refs[n:2 * n], *refs[2 * n:])
        send()
        finish()

    sem = pltpu.SemaphoreType.DMA((n,))
    return pl.pallas_call(
        body, name=name, in_specs=[ANY] * n, out_specs=[ANY] * n,
        out_shape=[jax.ShapeDtypeStruct(b.shape, b.dtype) for b in bufs],
        input_output_aliases={a: a for a in range(n)},
        scratch_shapes=[sem, sem], compiler_params=pltpu.CompilerParams(has_side_effects=True),
    )(*bufs)


def _join_steps(outs, send_sems, recv_sems):
    x, y, c = _position()

    def copy(a, core):
        h = outs[a].shape[1] // 2
        half = outs[a].at[:, pl.ds(core * h, h), :]
        return _remote(half, half, send_sems.at[a], recv_sems.at[a], (x, y, 1 - c))

    def send():
        for a in range(len(outs)):
            copy(a, c).start()

    def finish():
        for a in range(len(outs)):
            copy(a, c).wait_send()
            copy(a, 1 - c).wait_recv()

    return send, finish


def _allreduce_small(packs):
    n = len(packs)

    def body(*refs):
        ins, outs, gath = refs[:n], refs[n:2 * n], refs[2 * n:3 * n]
        send_sems, recv_sems = refs[3 * n:]
        x, y, c = _position()
        me, sibling = (x, y, c), (x, y, 1 - c)
        chips = _other_chips(x, y)

        def slot(a, dev):
            return gath[a].at[4 * dev[0] + 2 * dev[1] + dev[2]]

        def copy(a, k, block, to, src=None):
            return _remote(slot(a, block) if src is None else src, slot(a, block),
                           send_sems.at[7 * a + k], recv_sems.at[7 * a + k], to)

        started = []
        for a in range(n):
            slot(a, me)[...] = ins[a][...]
            first = [copy(a, 0, me, sibling, src=ins[a])]
            first += [copy(a, 1 + k, me, (*chip, c), src=ins[a]) for k, chip in enumerate(chips)]
            for cp in first:
                cp.start()
            started += first
        for a in range(n):
            for k, chip in enumerate(chips):
                copy(a, 1 + k, (*chip, c), me).wait_recv()
                cp = copy(a, 4 + k, (*chip, c), sibling)
                cp.start()
                started.append(cp)
        for a in range(n):
            copy(a, 0, sibling, me).wait_recv()
            for k, chip in enumerate(chips):
                copy(a, 4 + k, (*chip, 1 - c), me).wait_recv()
        for cp in started:
            cp.wait_send()
        for a in range(n):
            total = gath[a][0]
            for d in range(1, N_DEV):
                total = total + gath[a][d]
            outs[a][...] = total

    vmem = pl.BlockSpec(memory_space=pltpu.VMEM)
    sem = pltpu.SemaphoreType.DMA((7 * n,))
    return pl.pallas_call(
        body, name="allreduce_small", in_specs=[vmem] * n, out_specs=[vmem] * n,
        out_shape=[jax.ShapeDtypeStruct(p.shape, p.dtype) for p in packs],
        scratch_shapes=[pltpu.VMEM((N_DEV,) + p.shape, p.dtype) for p in packs] + [sem, sem],
        compiler_params=pltpu.CompilerParams(has_side_effects=True, vmem_limit_bytes=VMEM_LIMIT_BYTES),
    )(*packs)


LOSS_ROW = 1040


def _pad_rows(a, rows=8):
    return jnp.concatenate([a, jnp.zeros((rows - a.shape[0], a.shape[1]), a.dtype)], axis=0)

def _adam_small(wide, mid, narrow, params):
    names = ["mix_norm_g", "mlp_norm_g", "final_norm_g", "conv_b", "conv_w", "sgu_norm_g", "sgu_norm_b",
             "pool_w", "pool_scale", "sgu_w", "sgu_b"]
    n = len(names)

    def body(*refs):
        wide_ref, mid_ref, narrow_ref = refs[:3]
        wmv = refs[3:3 + 3 * n]
        outs = refs[3 + 3 * n:]
        x, y, _ = _position()
        q = 2 * x + y

        def my_quarter(rows):
            parts = [rows[:, s * TILE:(s + 1) * TILE] for s in range(N_CHIP)]
            return jnp.where(q == 0, parts[0], jnp.where(q == 1, parts[1], jnp.where(q == 2, parts[2], parts[3])))

        def tiles(first_row):
            return [((0, g), narrow_ref[first_row + g * TILE:first_row + (g + 1) * TILE, :]) for g in range(4)]

        grads = {
            "mix_norm_g": [((), wide_ref[0:2, :])],
            "mlp_norm_g": [((), wide_ref[8:10, :])],
            "final_norm_g": [((), wide_ref[16:17, :])],
            "conv_b": [((), mid_ref[0:1, :])],
            "conv_w": [((0,), my_quarter(mid_ref[8:11, :]))],
            "sgu_norm_g": [((), my_quarter(mid_ref[16:17, :]))],
            "sgu_norm_b": [((), my_quarter(mid_ref[24:25, :]))],
            "pool_w": tiles(0),
            "sgu_w": tiles(512),
            "pool_scale": [((0,), narrow_ref[1024:1028, :])],
            "sgu_b": [((0,), narrow_ref[1032:1036, :])],
        }
        for i, name in enumerate(names):
            w_ref, m_ref, v_ref = wmv[3 * i:3 * i + 3]
            for lead, g in grads[name]:
                idx = lead + (slice(None), slice(None))
                delta, m_new, v_new = _adam_math(w_ref[idx], g, m_ref[idx], v_ref[idx])
                outs[4 * i][idx] = g
                outs[4 * i + 1][idx] = delta
                outs[4 * i + 2][idx] = m_new
                outs[4 * i + 3][idx] = v_new

    vmem = pl.BlockSpec(memory_space=pltpu.VMEM)
    args, out_shape = [wide, mid, narrow], []
    for name in names:
        w, m, v = params[name]
        args += [w, m, v]
        out_shape += [jax.ShapeDtypeStruct(w.shape, F32)] * 4
    res = pl.pallas_call(
        body, name="adam_small", in_specs=[vmem] * len(args), out_specs=[vmem] * len(out_shape),
        out_shape=out_shape, compiler_params=pltpu.CompilerParams(vmem_limit_bytes=VMEM_LIMIT_BYTES),
    )(*args)
    return {name: res[4 * i:4 * i + 4] for i, name in enumerate(names)}


def _pair_sums(grads, got, pos, tag):
    return [_pair_sum(a, b, pos, name=f"pair_sum_{tag}{i}") for i, (a, b) in enumerate(zip(grads, got))]


def _chip_sums(sums, landed, pos, tag):
    return [_chip_sum(s, r, pos, name=f"chip_sum_{tag}{i}") for i, (s, r) in enumerate(zip(sums, landed))]


def kernel(x, mix_norm_g, mlp_norm_g, ab_w_in, pool_w, pool_scale, conv_w, conv_b, ab_w_out, cd_w_in, sgu_norm_g, sgu_norm_b, sgu_w, sgu_b, cd_w_out, mlp_w1, mlp_w2, final_norm_g, loss_target, m_mix_norm_g, m_mlp_norm_g, m_ab_w_in, m_pool_w, m_pool_scale, m_conv_w, m_conv_b, m_ab_w_out, m_cd_w_in, m_sgu_norm_g, m_sgu_norm_b, m_sgu_w, m_sgu_b, m_cd_w_out, m_mlp_w1, m_mlp_w2, m_final_norm_g, v_mix_norm_g, v_mlp_norm_g, v_ab_w_in, v_pool_w, v_pool_scale, v_conv_w, v_conv_b, v_ab_w_out, v_cd_w_in, v_sgu_norm_g, v_sgu_norm_b, v_sgu_w, v_sgu_b, v_cd_w_out, v_mlp_w1, v_mlp_w2, v_final_norm_g):
    nseq, t_len, d = x.shape
    m_tok = nseq * t_len
    h0 = x.reshape(m_tok, d)
    target = loss_target.reshape(m_tok, d)

    x_idx, y_idx = lax.axis_index("x"), lax.axis_index("y")
    q_idx = 2 * x_idx + y_idx
    pos = jnp.stack([q_idx, lax.axis_index("c")]).astype(jnp.int32)
    def shard_buffer(w, layer, tag):
        return _cast_place(w, layer, pos, name=f"cast_place_{tag}")

    def row_block(w):
        return w.reshape(1, 1, -1, w.shape[-1])

    later_weights = [shard_buffer(cd_w_out, 0, "cd_out"), shard_buffer(mlp_w1, 1, "w1_1"),
                     shard_buffer(mlp_w2, 1, "w2_1")]

    pool_w3, pool_scale3 = pool_w[0], pool_scale[0].reshape(4, 1, TILE)
    sgu_w3 = sgu_w[0]
    sgu_w3_t = jnp.swapaxes(sgu_w3, 1, 2)
    sgu_bias_tile = jnp.broadcast_to(sgu_b[0][:, :, None], (4, TILE, TILE))
    conv_w2, conv_b2 = conv_w[0], conv_b
    def place_quarter(v):
        return lax.dynamic_update_slice(jnp.zeros((v.shape[0], 4 * TILE), F32), v, (0, q_idx * TILE))

    sharded_small = jnp.concatenate(
        [place_quarter(conv_w[0]), place_quarter(sgu_norm_g), place_quarter(sgu_norm_b),
         jnp.zeros((3, 4 * TILE), F32)], axis=0)
    sharded_small, = _allreduce_small([sharded_small])
    sharded_small = sharded_small * 0.5
    conv_w_full = sharded_small[0:3]
    sgu_g_full = sharded_small[3:4]
    sgu_b_full = sharded_small[4:5]

    xn0, ((w_ab_in, w_ab_out),) = _rms_fwd(
        h0, mix_norm_g[0:1], name="rms_fwd_mix0",
        rider=[("gather", [shard_buffer(ab_w_in, 0, "ab_in"), shard_buffer(ab_w_out, 0, "ab_out")])])
    w_ab_out = row_block(w_ab_out)
    p_ab, ((w_1_0,),) = _mm_nn(xn0, w_ab_in, 0, out_dtype=BF16, name="ab_in_proj",
                               rider=[("gather", [shard_buffer(mlp_w1, 0, "w1_0")])])
    mix0 = _ab_fwd(p_ab, pool_w3, pool_scale3, conv_w_full, conv_b2, nseq, t_len)
    h1, hn0 = _mm_nn(mix0, w_ab_out, 0, out_dtype=F32, name="ab_out_proj", epilogue="residual", extra=h0,
                     norm_g=mlp_norm_g[0:1])
    (act0, relu0), ((w_2_0,),) = _mm_nn(hn0, w_1_0, 0, out_dtype=BF16, name="mlp0_up", epilogue="relu2",
                                        rider=[("gather", [shard_buffer(mlp_w2, 0, "w2_0")])])
    w_2_0 = row_block(w_2_0)
    (h2, xn1), ((w_cd_in,),) = _mm_nn(act0, w_2_0, 0, out_dtype=F32, name="mlp0_down", epilogue="residual", extra=h1,
                                      norm_g=mix_norm_g[1:2],
                                      rider=[("gather", [shard_buffer(cd_w_in, 0, "cd_in")])])

    p_cd = _mm_nn(xn1, w_cd_in, 0, out_dtype=BF16, name="cd_in_proj")
    c_out = _sgu_fwd(p_cd, sgu_g_full, sgu_b_full, sgu_w3, sgu_bias_tile)
    d_out, ltot, (w_cd_out, w_1_1, w_2_1) = _sb_fwd(p_cd, nseq, t_len, later_weights)
    w_cd_out, w_2_1 = row_block(w_cd_out), row_block(w_2_1)
    mix1 = jnp.concatenate([c_out, d_out], axis=1)
    h3, hn1 = _mm_nn(mix1, w_cd_out, 0, out_dtype=F32, name="cd_out_proj", epilogue="residual", extra=h2,
                     norm_g=mlp_norm_g[1:2])
    act1, relu1 = _mm_nn(hn1, w_1_1, 0, out_dtype=BF16, name="mlp1_up", epilogue="relu2")
    h4 = _mm_nn(act1, w_2_1, 0, out_dtype=F32, name="mlp1_down", epilogue="residual", extra=h3)

    dh4, dh4_bf, dg_final, loss_tile = _final_loss(h4, final_norm_g.reshape(1, d), target)

    def as_pieces(g):
        return g.reshape(1, N_CHIP, -1, g.shape[-1]) if g.shape[1] == 1 else g

    dz1 = _mm_nt(dh4_bf, w_2_1, 0, out_dtype=BF16, name="mlp1_down_bwd", epilogue="relu2_bwd", extra=relu1)
    g_w2_1 = as_pieces(_mm_tn(act1, dh4_bf, 1, name="mlp1_down_wgrad"))
    g_w1_1 = _mm_tn(hn1, dz1, N_CHIP, name="mlp1_up_wgrad")
    (dh3, dh3_bf, dg_mlp1), (got_a,) = _mm_nt(
        dz1, w_1_1, 0, out_dtype=F32, name="mlp1_up_bwd", epilogue="rms_bwd",
        extra=(h3, mlp_norm_g[1:2], dh4), rider=[("swap", [g_w1_1, g_w2_1])])

    g_cd_out = as_pieces(_mm_tn(mix1, dh3_bf, 1, name="cd_out_wgrad"))
    dmix1, (got_cd_out,) = _mm_nt(dh3_bf, w_cd_out, 0, out_dtype=BF16, name="cd_out_bwd",
                                  rider=[("swap", [g_cd_out])])
    sums_a = _pair_sums([g_w1_1, g_w2_1, g_cd_out], got_a + got_cd_out, pos, "a")
    du, dv, dsgu_w, dsgu_bs, dsgu_g, dsgu_b = _sgu_bwd(p_cd, dmix1, sgu_g_full, sgu_b_full, sgu_w3, sgu_w3_t,
                                                      sgu_bias_tile)
    dq, dk, dvv, landed_a = _sb_bwd(p_cd, dmix1, ltot, nseq, t_len, sums_a)
    halves_a = _chip_sums(sums_a, landed_a, pos, "a")
    dp_cd = jnp.concatenate([du, dv, dq, dk, dvv], axis=1)
    g_cd_in, ((r_w1_1, r_w2_1, r_cd_out),) = _mm_tn(xn1, dp_cd, N_CHIP, name="cd_in_wgrad",
                                                    rider=[("join", halves_a)])
    (dh2, dh2_bf, dg_mix1), (got_c,) = _mm_nt(
        dp_cd, w_cd_in, 0, out_dtype=F32, name="cd_in_bwd", epilogue="rms_bwd",
        extra=(h2, mix_norm_g[1:2], dh3), rider=[("swap", [g_cd_in])])

    sums_c = _pair_sums([g_cd_in], got_c, pos, "c")
    dz0, (landed_c,) = _mm_nt(dh2_bf, w_2_0, 0, out_dtype=BF16, name="mlp0_down_bwd", epilogue="relu2_bwd",
                              extra=relu0, rider=[("exchange", sums_c)])
    halves_c = _chip_sums(sums_c, landed_c, pos, "c")
    g_w2_0, ((r_cd_in,),) = _mm_tn(act0, dh2_bf, 1, name="mlp0_down_wgrad", rider=[("join", halves_c)])
    g_w2_0 = as_pieces(g_w2_0)
    g_w1_0, (got_d,) = _mm_tn(hn0, dz0, N_CHIP, name="mlp0_up_wgrad", rider=[("swap", [g_w2_0])])
    sums_d = _pair_sums([g_w2_0], got_d, pos, "d")
    (dh1, dh1_bf, dg_mlp0), (landed_d, got_e) = _mm_nt(
        dz0, w_1_0, 0, out_dtype=F32, name="mlp0_up_bwd", epilogue="rms_bwd",
        extra=(h1, mlp_norm_g[0:1], dh2), rider=[("exchange", sums_d), ("swap", [g_w1_0])])
    halves_d = _chip_sums(sums_d, landed_d, pos, "d")
    sums_e = _pair_sums([g_w1_0], got_e, pos, "e")

    dmix0, ((r_w2_0,),) = _mm_nt(dh1_bf, w_ab_out, 0, out_dtype=BF16, name="ab_out_bwd", rider=[("join", halves_d)])
    g_ab_out = as_pieces(_mm_tn(mix0, dh1_bf, 1, name="ab_out_wgrad"))
    (da, dxb, dgb, dgc, dpool_w, dpool_scale, dconv_w, dconv_b), (landed_e, got_f) = _ab_bwd(
        p_ab, dmix0, pool_w3, pool_scale3, conv_w_full, conv_b2, nseq, t_len,
        rider=[("exchange", sums_e), ("swap", [g_ab_out])])
    halves_e = _chip_sums(sums_e, landed_e, pos, "e")
    sums_f = _pair_sums([g_ab_out], got_f, pos, "f")
    dp_ab = jnp.concatenate([da, dxb, dgb, dgc], axis=1)
    g_ab_in, (landed_f, (r_w1_0,)) = _mm_tn(xn0, dp_ab, N_CHIP, name="ab_in_wgrad",
                                            rider=[("exchange", sums_f), ("join", halves_e)])
    halves_f = _chip_sums(sums_f, landed_f, pos, "f")
    sums_g = _pair_sums([g_ab_in], _swap_halves([g_ab_in], name="swap_halves_g"), pos, "g")
    (grad_x, _, dg_mix0), (landed_g, (r_ab_out,)) = _mm_nt(
        dp_ab, w_ab_in, 0, out_dtype=F32, name="ab_in_bwd", epilogue="rms_bwd",
        extra=(h0, mix_norm_g[0:1], dh1), rider=[("exchange", sums_g), ("join", halves_f)])
    r_ab_in, = _join_halves(_chip_sums(sums_g, landed_g, pos, "g"), name="join_halves_g")

    big_out = {
        "ab_w_in": _adam_big(ab_w_in, m_ab_w_in, v_ab_w_in, [r_ab_in], name="adam_ab_w_in"),
        "ab_w_out": _adam_big(ab_w_out, m_ab_w_out, v_ab_w_out, [r_ab_out], name="adam_ab_w_out"),
        "cd_w_in": _adam_big(cd_w_in, m_cd_w_in, v_cd_w_in, [r_cd_in], name="adam_cd_w_in"),
        "cd_w_out": _adam_big(cd_w_out, m_cd_w_out, v_cd_w_out, [r_cd_out], name="adam_cd_w_out"),
        "mlp_w1": _adam_big(mlp_w1, m_mlp_w1, v_mlp_w1, [r_w1_0, r_w1_1], name="adam_mlp_w1"),
        "mlp_w2": _adam_big(mlp_w2, m_mlp_w2, v_mlp_w2, [r_w2_0, r_w2_1], name="adam_mlp_w2"),
    }

    wide = jnp.concatenate([_pad_rows(jnp.concatenate([dg_mix0, dg_mix1], axis=0)),
                            _pad_rows(jnp.concatenate([dg_mlp0, dg_mlp1], axis=0)), _pad_rows(dg_final)], axis=0)
    mid = jnp.concatenate([_pad_rows(dconv_b), _pad_rows(dconv_w), _pad_rows(dsgu_g), _pad_rows(dsgu_b)], axis=0)
    narrow = jnp.concatenate(
        [dpool_w.reshape(4 * TILE, TILE), dsgu_w.reshape(4 * TILE, TILE), _pad_rows(dpool_scale.reshape(4, TILE)),
         _pad_rows(dsgu_bs[:, :, 0]), loss_tile], axis=0)
    wide, mid, narrow = _allreduce_small([wide, mid, narrow])
    small_out = _adam_small(wide, mid, narrow, {
        "mix_norm_g": (mix_norm_g, m_mix_norm_g, v_mix_norm_g),
        "mlp_norm_g": (mlp_norm_g, m_mlp_norm_g, v_mlp_norm_g),
        "final_norm_g": tuple(a.reshape(1, d) for a in (final_norm_g, m_final_norm_g, v_final_norm_g)),
        "conv_b": (conv_b, m_conv_b, v_conv_b),
        "conv_w": (conv_w, m_conv_w, v_conv_w),
        "sgu_norm_g": (sgu_norm_g, m_sgu_norm_g, v_sgu_norm_g),
        "sgu_norm_b": (sgu_norm_b, m_sgu_norm_b, v_sgu_norm_b),
        "pool_w": (pool_w, m_pool_w, v_pool_w),
        "pool_scale": (pool_scale, m_pool_scale, v_pool_scale),
        "sgu_w": (sgu_w, m_sgu_w, v_sgu_w),
        "sgu_b": (sgu_b, m_sgu_b, v_sgu_b),
    })
    small_out["final_norm_g"] = [a.reshape(d) for a in small_out["final_norm_g"]]

    order = ["mix_norm_g", "mlp_norm_g", "ab_w_in", "pool_w", "pool_scale", "conv_w", "conv_b", "ab_w_out",
             "cd_w_in", "sgu_norm_g", "sgu_norm_b", "sgu_w", "sgu_b", "cd_w_out", "mlp_w1", "mlp_w2",
             "final_norm_g"]
    both = {**big_out, **small_out}
    loss = narrow[LOSS_ROW, 0]
    outs = [loss, grad_x.reshape(nseq, t_len, d)]
    for kind in range(4):
        outs += [both[name][kind] for name in order]
    return tuple(outs)
```

```python
import math

import jax
import jax.numpy as jnp
from jax import lax
from jax.experimental import pallas as pl
from jax.experimental.pallas import tpu as pltpu

F32 = jnp.float32
BF16 = jnp.bfloat16
MESH = pl.DeviceIdType.MESH

D_MODEL = 1024
EPS = 1e-6
TILE = 128
N_CHIP = 4
N_DEV = 8
VMEM_LIMIT_BYTES = 56 * 1024 * 1024

ADAM_LR = 0.001
ADAM_B1 = 0.9
ADAM_B2 = 0.999
ADAM_EPS = 1e-08
ADAM_WD = 0.01
ADAM_STEP = 10

NT_DIMS = (((1,), (1,)), ((), ()))
TN_DIMS = (((0,), (0,)), ((), ()))


def _params(sem=None):
    return pltpu.CompilerParams(dimension_semantics=sem, vmem_limit_bytes=VMEM_LIMIT_BYTES)


def _call(body, *, name, grid, in_specs, out_specs, out_shape, scratch_shapes, semantics, args, rider=None):
    if not rider:
        res = pl.pallas_call(body, name=name, grid=grid, in_specs=in_specs, out_specs=out_specs, out_shape=out_shape,
                             scratch_shapes=scratch_shapes, compiler_params=_params(semantics))(*args)
        return list(res), []
    plans = [_rider_plan(kind, arrays) for kind, arrays in rider]
    arrays = [a for _, group in rider for a in group]
    nr, n_in, n_out, n_scr = len(arrays), len(in_specs), len(out_specs), len(scratch_shapes)
    first_out, first_scr = n_in + nr, n_in + nr + n_out + nr
    last_step = math.prod(grid) - 1

    def riding(*refs):
        step = 0
        for axis, size in enumerate(grid):
            step = step * size + pl.program_id(axis)
        steps, at, sem_at = [], 0, first_scr + n_scr
        for (kind, group), (_, sems, _) in zip(rider, plans):
            k = len(group)
            steps.append(_rider_steps(kind, refs[n_in + at:n_in + at + k],
                                      refs[first_out + n_out + at:first_out + n_out + at + k],
                                      refs[sem_at:sem_at + len(sems)]))
            at, sem_at = at + k, sem_at + len(sems)
        for send, _, _ in steps:
            pl.when(step == 0)(send)
        for _, forward, _ in steps:
            if forward is not None:
                pl.when(step == last_step)(forward)
        body(*refs[:n_in], *refs[first_out:first_out + n_out], *refs[first_scr:first_scr + n_scr])
        for _, _, finish in steps:
            pl.when(step == last_step)(finish)

    aliases, at = {}, 0
    for (_, group), (_, _, aliased) in zip(rider, plans):
        if aliased:
            aliases.update({n_in + at + a: n_out + at + a for a in range(len(group))})
        at += len(group)
    res = pl.pallas_call(
        riding, name=name, grid=grid, in_specs=list(in_specs) + [ANY] * nr, out_specs=list(out_specs) + [ANY] * nr,
        out_shape=list(out_shape) + [s for shapes, _, _ in plans for s in shapes],
        scratch_shapes=list(scratch_shapes) + [s for _, sems, _ in plans for s in sems],
        input_output_aliases=aliases,
        compiler_params=pltpu.CompilerParams(dimension_semantics=("arbitrary",) * len(grid),
                                             vmem_limit_bytes=VMEM_LIMIT_BYTES, has_side_effects=True),
    )(*args, *arrays)
    rode, at = [], n_out
    for _, group in rider:
        rode.append(list(res[at:at + len(group)]))
        at += len(group)
    return list(res[:n_out]), rode


def _rider_plan(kind, arrays):
    n = len(arrays)
    same = [jax.ShapeDtypeStruct(a.shape, a.dtype) for a in arrays]
    pair = [pltpu.SemaphoreType.DMA((n,))] * 2
    if kind == "gather":
        return same, _gather_sems(n), True
    if kind == "exchange":
        return _exchange_shapes(arrays), _exchange_sems(n), False
    if kind == "swap":
        return _swap_shapes(arrays), pair, False
    if kind == "allgather":
        return ([jax.ShapeDtypeStruct((N_DEV,) + a.shape, a.dtype) for a in arrays],
                [pltpu.SemaphoreType.DMA((7 * n,))] * 2 + [pltpu.SemaphoreType.DMA((n,))], False)
    assert kind == "join"
    return same, pair, True


def _rider_steps(kind, ins, outs, sems):
    if kind == "gather":
        return _gather_steps(outs, *sems)
    if kind == "allgather":
        return _allgather_steps(ins, outs, *sems)
    if kind == "exchange":
        send, finish = _exchange_steps(ins, outs, *sems)
    elif kind == "swap":
        send, finish = _swap_steps(ins, outs, *sems)
    else:
        send, finish = _join_steps(outs, *sems)
    return send, None, finish


def _row_tile(k_dim):
    return 1024 if k_dim <= 1024 else 512


def _mm_nn(a, b4, layer, *, out_dtype, name, epilogue=None, extra=None, norm_g=None, rider=None):
    m, k_dim = a.shape
    _, s_dim, kb, n = b4.shape
    assert kb == k_dim
    tm, tn = _row_tile(k_dim), min(n, 1024)
    assert m % tm == 0 and n % tn == 0
    npb = n // tn
    grid = (m // tm, s_dim * npb)
    n_in = 2 + (extra is not None) + (norm_g is not None)
    two_outputs = norm_g is not None or epilogue == "relu2"
    assert norm_g is None or (tn == s_dim * n and epilogue != "relu2")

    def body(*refs):
        a_ref, b_ref = refs[:2]
        e_ref = refs[2] if extra is not None else None
        g_ref = refs[n_in - 1] if norm_g is not None else None
        o_ref = refs[n_in]
        acc = jnp.dot(a_ref[...], b_ref[...], preferred_element_type=F32)
        if epilogue == "relu2":
            r = jnp.maximum(acc, 0.0)
            refs[n_in + 1][...] = r.astype(BF16)
            acc = r * r
        elif epilogue == "residual":
            acc = acc + e_ref[...]
        o_ref[...] = acc.astype(out_dtype)
        if norm_g is not None:
            rstd = lax.rsqrt(jnp.mean(acc * acc, axis=-1, keepdims=True) + EPS)
            refs[n_in + 1][...] = (acc * rstd * g_ref[...]).astype(BF16)

    in_specs = [
        pl.BlockSpec((tm, k_dim), lambda i, j: (i, 0)),
        pl.BlockSpec((None, None, k_dim, tn), lambda i, j: (layer, j // npb, 0, j % npb)),
    ]
    args = [a, b4]
    if extra is not None:
        in_specs.append(pl.BlockSpec((tm, tn), lambda i, j: (i, j)))
        args.append(extra)
    out_block = pl.BlockSpec((tm, tn), lambda i, j: (i, j))
    out_specs, out_shape = [out_block], [jax.ShapeDtypeStruct((m, s_dim * n), out_dtype)]
    if norm_g is not None:
        in_specs.append(pl.BlockSpec((1, tn), lambda i, j: (0, j)))
        args.append(norm_g)
    if two_outputs:
        out_specs.append(out_block)
        out_shape.append(jax.ShapeDtypeStruct((m, s_dim * n), BF16))
    res, rode = _call(
        body, name=name, grid=grid, in_specs=in_specs, out_specs=out_specs, out_shape=out_shape,
        scratch_shapes=[], semantics=("parallel", "parallel"), args=args, rider=rider)
    res = res if two_outputs else res[0]
    return res if rider is None else (res, rode)


def _mm_nt(a, b4, layer, *, out_dtype, name, epilogue=None, extra=None, rider=None):
    m, k_dim = a.shape
    _, s_dim, n_out, n = b4.shape
    assert k_dim == s_dim * n
    tm, tn = _row_tile(k_dim), min(n_out, 1024)
    assert m % tm == 0 and n_out % tn == 0
    grid = (m // tm, n_out // tn)
    rms = epilogue == "rms_bwd"
    assert not rms or tn == n_out
    extras = [] if extra is None else (list(extra) if rms else [extra])
    n_in = 2 + len(extras)
    n_res = 3 if rms else 1

    def body(*refs):
        a_ref, b_ref = refs[:2]
        e_refs = refs[2:n_in]
        o_ref = refs[n_in]
        acc = lax.dot_general(a_ref[:, 0:n], b_ref[0], NT_DIMS, preferred_element_type=F32)
        for s in range(1, s_dim):
            acc = acc + lax.dot_general(a_ref[:, s * n:(s + 1) * n], b_ref[s], NT_DIMS, preferred_element_type=F32)
        if epilogue == "relu2_bwd":
            acc = acc * (2.0 * e_refs[0][...].astype(F32))
        if not rms:
            o_ref[...] = acc.astype(out_dtype)
        else:
            h_ref, g_ref, dres_ref = e_refs
            dhb_ref, dg_ref = refs[n_in + 1:n_in + 3]
            hv = h_ref[...]
            rstd = lax.rsqrt(jnp.mean(hv * hv, axis=-1, keepdims=True) + EPS)
            xhat = hv * rstd
            dxhat = acc * g_ref[...]
            dh = dres_ref[...] + rstd * (dxhat - xhat * jnp.mean(dxhat * xhat, axis=-1, keepdims=True))
            o_ref[...] = dh
            dhb_ref[...] = dh.astype(BF16)
            dg_part = jnp.sum(acc * xhat, axis=0, keepdims=True)
            first = pl.program_id(0) == 0

            @pl.when(first)
            def _():
                dg_ref[...] = dg_part

            @pl.when(jnp.logical_not(first))
            def _():
                dg_ref[...] += dg_part

    in_specs = [
        pl.BlockSpec((tm, k_dim), lambda i, j: (i, 0)),
        pl.BlockSpec((None, s_dim, tn, n), lambda i, j: (layer, 0, j, 0)),
    ]
    args = [a, b4] + extras
    block = pl.BlockSpec((tm, tn), lambda i, j: (i, j))
    vec = pl.BlockSpec((1, tn), lambda i, j: (0, j))
    if rms:
        in_specs += [block, vec, block]
        out_specs = [block, block, vec]
        out_shape = [jax.ShapeDtypeStruct((m, n_out), F32), jax.ShapeDtypeStruct((m, n_out), BF16),
                     jax.ShapeDtypeStruct((1, n_out), F32)]
    else:
        in_specs += [block] * len(extras)
        out_specs, out_shape = [block], [jax.ShapeDtypeStruct((m, n_out), out_dtype)]
    res, rode = _call(
        body, name=name, grid=grid, in_specs=in_specs, out_specs=out_specs, out_shape=out_shape,
        scratch_shapes=[], semantics=("arbitrary",) * 2 if rms else ("parallel", "parallel"), args=args, rider=rider)
    res = res if rms else res[0]
    return res if rider is None else (res, rode)


def _mm_tn(a, b, s_dim, *, name, rider=None):
    m, k1 = a.shape
    mb, n_all = b.shape
    assert mb == m and n_all % s_dim == 0
    n = n_all // s_dim
    tn, t1 = min(n, 1024), _row_tile(m)
    assert k1 % t1 == 0 and n % tn == 0
    npb = n // tn
    grid = (k1 // t1, s_dim * npb)

    def body(a_ref, b_ref, o_ref):
        o_ref[...] = lax.dot_general(a_ref[...], b_ref[...], TN_DIMS, preferred_element_type=F32).astype(BF16)

    res, rode = _call(
        body, name=name, grid=grid,
        in_specs=[pl.BlockSpec((m, t1), lambda i, j: (0, i)), pl.BlockSpec((m, tn), lambda i, j: (0, j))],
        out_specs=[pl.BlockSpec((None, None, t1, tn), lambda i, j: (0, j // npb, i, j % npb))],
        out_shape=[jax.ShapeDtypeStruct((1, s_dim, k1, n), BF16)],
        scratch_shapes=[], semantics=("parallel", "parallel"), args=[a, b], rider=rider)
    return res[0] if rider is None else (res[0], rode)


ROW_TILE = 512


def _rms_fwd(h, g, *, name, rider=None):
    m, d = h.shape

    def body(h_ref, g_ref, o_ref):
        hv = h_ref[...]
        rstd = lax.rsqrt(jnp.mean(hv * hv, axis=-1, keepdims=True) + EPS)
        o_ref[...] = (hv * rstd * g_ref[...]).astype(BF16)

    res, rode = _call(
        body, name=name, grid=(m // ROW_TILE,),
        in_specs=[pl.BlockSpec((ROW_TILE, d), lambda i: (i, 0)), pl.BlockSpec((1, d), lambda i: (0, 0))],
        out_specs=[pl.BlockSpec((ROW_TILE, d), lambda i: (i, 0))], out_shape=[jax.ShapeDtypeStruct((m, d), BF16)],
        scratch_shapes=[], semantics=("parallel",), args=[h, g], rider=rider)
    return res[0] if rider is None else (res[0], rode)


def _final_loss(h, g, target):
    m, d = h.shape

    def body(h_ref, g_ref, t_ref, dh_ref, dhb_ref, dg_ref, loss_ref):
        hv = h_ref[...]
        gv = g_ref[...]
        rstd = lax.rsqrt(jnp.mean(hv * hv, axis=-1, keepdims=True) + EPS)
        xhat = hv * rstd
        err = xhat * gv - t_ref[...]
        dy = err * (1.0 / d)
        dxhat = dy * gv
        dh = rstd * (dxhat - xhat * jnp.mean(dxhat * xhat, axis=-1, keepdims=True))
        dh_ref[...] = dh
        dhb_ref[...] = dh.astype(BF16)
        dg_part = jnp.sum(dy * xhat, axis=0, keepdims=True)
        sq = jnp.sum(jnp.sum(err * err, axis=1, keepdims=True), axis=0, keepdims=True) * (0.5 / d)
        loss_part = jnp.broadcast_to(sq, (8, TILE))

        @pl.when(pl.program_id(0) == 0)
        def _():
            dg_ref[...] = dg_part
            loss_ref[...] = loss_part

        @pl.when(pl.program_id(0) > 0)
        def _():
            dg_ref[...] += dg_part
            loss_ref[...] += loss_part

    row = pl.BlockSpec((ROW_TILE, d), lambda i: (i, 0))
    vec = pl.BlockSpec((1, d), lambda i: (0, 0))
    return pl.pallas_call(
        body, name="final_loss", grid=(m // ROW_TILE,),
        in_specs=[row, vec, row],
        out_specs=[row, row, vec, pl.BlockSpec((8, TILE), lambda i: (0, 0))],
        out_shape=[jax.ShapeDtypeStruct((m, d), F32), jax.ShapeDtypeStruct((m, d), BF16),
                   jax.ShapeDtypeStruct((1, d), F32), jax.ShapeDtypeStruct((8, TILE), F32)],
        compiler_params=_params(("arbitrary",)),
    )(h, g, target)


def _shift_down(x, s, t_idx):
    return jnp.where(t_idx >= s, pltpu.roll(x, s, 0), 0.0)


def _shift_up(x, s, t_idx, t_len):
    return jnp.where(t_idx < t_len - s, pltpu.roll(x, t_len - s, 0), 0.0)


def _pool_select(group, s2, s4, s8, s16):
    return jnp.where(group == 0, s2, jnp.where(group == 1, s4, jnp.where(group == 2, s8, s16)))


def _pool_count(group, t_idx):
    win = jnp.left_shift(2, group)
    return jnp.minimum(t_idx + 1, win).astype(F32)


def _pool_fwd_math(a, group, t_idx):
    s2 = a + _shift_down(a, 1, t_idx)
    s4 = s2 + _shift_down(s2, 2, t_idx)
    s8 = s4 + _shift_down(s4, 4, t_idx)
    s16 = s8 + _shift_down(s8, 8, t_idx)
    return _pool_select(group, s2, s4, s8, s16) / _pool_count(group, t_idx) - a


def _pool_bwd_math(dpooled, group, t_idx, t_len):
    e = dpooled / _pool_count(group, t_idx)
    s2 = e + _shift_up(e, 1, t_idx, t_len)
    s4 = s2 + _shift_up(s2, 2, t_idx, t_len)
    s8 = s4 + _shift_up(s4, 4, t_idx, t_len)
    s16 = s8 + _shift_up(s8, 8, t_idx, t_len)
    return _pool_select(group, s2, s4, s8, s16) - dpooled


def _conv_fwd_math(c, w_ref, b_ref, t_idx):
    return (w_ref[0:1, :] * _shift_down(c, 2, t_idx) + w_ref[1:2, :] * _shift_down(c, 1, t_idx)
            + w_ref[2:3, :] * c + b_ref[...])


def _ab_fwd(p, pool_w, pool_scale, conv_w, conv_b, nseq, t_len, rider=None):
    m = p.shape[0]
    ng = 4

    def body(a_ref, xb_ref, gb_ref, gc_ref, pw_ref, ps_ref, cw_ref, cb_ref, o_ref):
        j = pl.program_id(1)
        t_idx = lax.broadcasted_iota(jnp.int32, (t_len, TILE), 0)

        @pl.when(j < ng)
        def _():
            pooled = _pool_fwd_math(a_ref[...].astype(F32), j, t_idx)
            mixed = jnp.dot(pooled.astype(BF16), pw_ref[...].astype(BF16), preferred_element_type=F32)
            o_ref[...] = (mixed * ps_ref[...]).astype(BF16)

        @pl.when(j >= ng)
        def _():
            c = gc_ref[...].astype(F32) * xb_ref[...].astype(F32)
            y = _conv_fwd_math(c, cw_ref, cb_ref, t_idx)
            o_ref[...] = (gb_ref[...].astype(F32) * y).astype(BF16)

    def pool_j(j):
        return jnp.minimum(j, ng - 1)

    def conv_j(j):
        return jnp.maximum(j - ng, 0)

    in_specs = [
        pl.BlockSpec((t_len, TILE), lambda s, j: (s, pool_j(j))),
        pl.BlockSpec((t_len, TILE), lambda s, j: (s, ng + conv_j(j))),
        pl.BlockSpec((t_len, TILE), lambda s, j: (s, 2 * ng + conv_j(j))),
        pl.BlockSpec((t_len, TILE), lambda s, j: (s, 3 * ng + conv_j(j))),
        pl.BlockSpec((None, TILE, TILE), lambda s, j: (pool_j(j), 0, 0)),
        pl.BlockSpec((None, 1, TILE), lambda s, j: (pool_j(j), 0, 0)),
        pl.BlockSpec((3, TILE), lambda s, j: (0, conv_j(j))),
        pl.BlockSpec((1, TILE), lambda s, j: (0, conv_j(j))),
    ]
    res, rode = _call(
        body, name="ab_mixer_fwd", grid=(nseq, 2 * ng), in_specs=in_specs,
        out_specs=[pl.BlockSpec((t_len, TILE), lambda s, j: (s, j))],
        out_shape=[jax.ShapeDtypeStruct((m, 2 * ng * TILE), BF16)], scratch_shapes=[],
        semantics=("parallel", "arbitrary"), args=[p, p, p, p, pool_w, pool_scale, conv_w, conv_b], rider=rider)
    return res[0] if rider is None else (res[0], rode)


def _ab_bwd(p, dmix, pool_w, pool_scale, conv_w, conv_b, nseq, t_len, rider=None):
    m = p.shape[0]
    ng = 4

    def body(a_ref, xb_ref, gb_ref, gc_ref, dma_ref, dmb_ref, pw_ref, ps_ref, cw_ref, cb_ref,
             da_ref, dxb_ref, dgb_ref, dgc_ref, dpw_ref, dps_ref, dcw_ref, dcb_ref):
        j = pl.program_id(0)
        first = pl.program_id(1) == 0
        t_idx = lax.broadcasted_iota(jnp.int32, (t_len, TILE), 0)

        pooled = _pool_fwd_math(a_ref[...].astype(F32), j, t_idx).astype(BF16)
        w_bf = pw_ref[...].astype(BF16)
        mixed = jnp.dot(pooled, w_bf, preferred_element_type=F32)
        dm = dma_ref[...].astype(F32)
        dps = jnp.sum(dm * mixed, axis=0, keepdims=True)
        dmixed = (dm * ps_ref[...]).astype(BF16)
        dpw = lax.dot_general(pooled, dmixed, TN_DIMS, preferred_element_type=F32)
        dpooled = lax.dot_general(dmixed, w_bf, NT_DIMS, preferred_element_type=F32)
        da_ref[...] = _pool_bwd_math(dpooled, j, t_idx, t_len).astype(BF16)

        xb = xb_ref[...].astype(F32)
        gb = gb_ref[...].astype(F32)
        gc = gc_ref[...].astype(F32)
        d = dmb_ref[...].astype(F32)
        c = gc * xb
        c1 = _shift_down(c, 1, t_idx)
        c2 = _shift_down(c, 2, t_idx)
        y = cw_ref[0:1, :] * c2 + cw_ref[1:2, :] * c1 + cw_ref[2:3, :] * c + cb_ref[...]
        dgb_ref[...] = (d * y).astype(BF16)
        dy = d * gb
        dc = (cw_ref[2:3, :] * dy + cw_ref[1:2, :] * _shift_up(dy, 1, t_idx, t_len)
              + cw_ref[0:1, :] * _shift_up(dy, 2, t_idx, t_len))
        dgc_ref[...] = (dc * xb).astype(BF16)
        dxb_ref[...] = (dc * gc).astype(BF16)
        dcw = jnp.concatenate([jnp.sum(dy * c2, axis=0, keepdims=True),
                               jnp.sum(dy * c1, axis=0, keepdims=True),
                               jnp.sum(dy * c, axis=0, keepdims=True)], axis=0)
        dcb = jnp.sum(dy, axis=0, keepdims=True)

        @pl.when(first)
        def _():
            dpw_ref[...] = dpw
            dps_ref[...] = dps
            dcw_ref[...] = dcw
            dcb_ref[...] = dcb

        @pl.when(jnp.logical_not(first))
        def _():
            dpw_ref[...] += dpw
            dps_ref[...] += dps
            dcw_ref[...] += dcw
            dcb_ref[...] += dcb

    def col(k):
        return pl.BlockSpec((t_len, TILE), lambda j, s: (s, k * ng + j))

    in_specs = [
        col(0), col(1), col(2), col(3), col(0), col(1),
        pl.BlockSpec((None, TILE, TILE), lambda j, s: (j, 0, 0)),
        pl.BlockSpec((None, 1, TILE), lambda j, s: (j, 0, 0)),
        pl.BlockSpec((3, TILE), lambda j, s: (0, j)),
        pl.BlockSpec((1, TILE), lambda j, s: (0, j)),
    ]
    piece = pl.BlockSpec((t_len, TILE), lambda j, s: (s, j))
    out_specs = [
        piece, piece, piece, piece,
        pl.BlockSpec((None, TILE, TILE), lambda j, s: (j, 0, 0)),
        pl.BlockSpec((None, 1, TILE), lambda j, s: (j, 0, 0)),
        pl.BlockSpec((3, TILE), lambda j, s: (0, j)),
        pl.BlockSpec((1, TILE), lambda j, s: (0, j)),
    ]
    w = ng * TILE
    out_shape = [jax.ShapeDtypeStruct((m, w), BF16)] * 4 + [
        jax.ShapeDtypeStruct((ng, TILE, TILE), F32), jax.ShapeDtypeStruct((ng, 1, TILE), F32),
        jax.ShapeDtypeStruct((3, w), F32), jax.ShapeDtypeStruct((1, w), F32)]
    res, rode = _call(
        body, name="ab_mixer_bwd", grid=(ng, nseq), in_specs=in_specs, out_specs=out_specs, out_shape=out_shape,
        scratch_shapes=[], semantics=("parallel", "arbitrary"),
        args=[p, p, p, p, dmix, dmix, pool_w, pool_scale, conv_w, conv_b], rider=rider)
    return res if rider is None else (res, rode)


SGU_ROWS = 512
INV_SQRT2 = 1.0 / math.sqrt(2.0)
INV_SQRT_2PI = 1.0 / math.sqrt(2.0 * math.pi)


def _gelu(x):
    return 0.5 * x * (1.0 + lax.erf(x * INV_SQRT2))


def _gelu_grad(x):
    return 0.5 * (1.0 + lax.erf(x * INV_SQRT2)) + x * (INV_SQRT_2PI * jnp.exp(-0.5 * x * x))


def _causal_tile(transposed=False):
    r = lax.broadcasted_iota(jnp.int32, (TILE, TILE), 0)
    c = lax.broadcasted_iota(jnp.int32, (TILE, TILE), 1)
    return r <= c if transposed else c <= r


def _sgu_norm(v, g_ref, b_ref):
    mu = jnp.mean(v, axis=-1, keepdims=True)
    xc = v - mu
    rstd = lax.rsqrt(jnp.mean(xc * xc, axis=-1, keepdims=True) + EPS)
    xhat = xc * rstd
    return xhat, rstd, xhat * g_ref[...] + b_ref[...]


def _sgu_fwd(p, norm_g, norm_b, w_s, bias_tile):
    m = p.shape[0]
    ng = 4
    width = ng * TILE

    def body(u_ref, v_ref, g_ref, b_ref, w_ref, bias_ref, o_ref):
        u = _gelu(u_ref[...].astype(F32))
        _, _, vln = _sgu_norm(_gelu(v_ref[...].astype(F32)), g_ref, b_ref)
        vln = vln.astype(BF16)
        causal = _causal_tile()
        for g in range(ng):
            cols = slice(g * TILE, (g + 1) * TILE)
            wg = jnp.where(causal, w_ref[g], 0.0).astype(BF16)
            for n in range(SGU_ROWS // TILE):
                rows = slice(n * TILE, (n + 1) * TILE)
                s = jnp.dot(wg, vln[rows, cols], preferred_element_type=F32) + bias_ref[g]
                o_ref[rows, cols] = (u[rows, cols] * s).astype(BF16)

    vec = pl.BlockSpec((1, width), lambda i: (0, 0))
    tiles = pl.BlockSpec((ng, TILE, TILE), lambda i: (0, 0, 0))
    return pl.pallas_call(
        body, name="sgu_fwd", grid=(m // SGU_ROWS,),
        in_specs=[pl.BlockSpec((SGU_ROWS, width), lambda i: (i, 0)),
                  pl.BlockSpec((SGU_ROWS, width), lambda i: (i, 1)), vec, vec, tiles, tiles],
        out_specs=pl.BlockSpec((SGU_ROWS, width), lambda i: (i, 0)),
        out_shape=jax.ShapeDtypeStruct((m, width), BF16),
        compiler_params=_params(("parallel",)),
    )(p, p, norm_g, norm_b, w_s, bias_tile)


def _sgu_bwd(p, dmix, norm_g, norm_b, w_s, w_s_t, bias_tile):
    m = p.shape[0]
    ng = 4
    width = ng * TILE

    def body(u_ref, v_ref, dc_ref, g_ref, b_ref, w_ref, wt_ref, bias_ref,
             du_ref, dv_ref, dw_ref, dbs_ref, dg_ref, db_ref, ds_scr, dvln_scr):
        u_pre = u_ref[...].astype(F32)
        v_pre = v_ref[...].astype(F32)
        u = _gelu(u_pre)
        xhat, rstd, vln = _sgu_norm(_gelu(v_pre), g_ref, b_ref)
        vln = vln.astype(BF16)
        dc = dc_ref[...].astype(F32)
        causal = _causal_tile()
        ones = jnp.ones((TILE, TILE), BF16)
        first = pl.program_id(0) == 0
        for g in range(ng):
            cols = slice(g * TILE, (g + 1) * TILE)
            wg = jnp.where(causal, w_ref[g], 0.0).astype(BF16)
            wgt = jnp.where(_causal_tile(transposed=True), wt_ref[g], 0.0).astype(BF16)
            dw_acc = jnp.zeros((TILE, TILE), F32)
            dbs_acc = jnp.zeros((TILE, TILE), F32)
            for n in range(SGU_ROWS // TILE):
                rows = slice(n * TILE, (n + 1) * TILE)
                vt = vln[rows, cols]
                s = jnp.dot(wg, vt, preferred_element_type=F32) + bias_ref[g]
                ds_scr[rows, cols] = dc[rows, cols] * s
                ds = (dc[rows, cols] * u[rows, cols]).astype(BF16)
                dw_acc += lax.dot_general(ds, vt, NT_DIMS, preferred_element_type=F32)
                dbs_acc += jnp.dot(ds, ones, preferred_element_type=F32)
                dvln_scr[rows, cols] = jnp.dot(wgt, ds, preferred_element_type=F32)
            dw_g = jnp.where(causal, dw_acc, 0.0)

            @pl.when(first)
            def _():
                dw_ref[g] = dw_g
                dbs_ref[g] = dbs_acc

            @pl.when(jnp.logical_not(first))
            def _():
                dw_ref[g] += dw_g
                dbs_ref[g] += dbs_acc

        du_ref[...] = (ds_scr[...] * _gelu_grad(u_pre)).astype(BF16)
        dvln = dvln_scr[...]
        dxhat = dvln * g_ref[...]
        dv = rstd * (dxhat - jnp.mean(dxhat, axis=-1, keepdims=True)
                     - xhat * jnp.mean(dxhat * xhat, axis=-1, keepdims=True))
        dv_ref[...] = (dv * _gelu_grad(v_pre)).astype(BF16)
        dg_part = jnp.sum(dvln * xhat, axis=0, keepdims=True)
        db_part = jnp.sum(dvln, axis=0, keepdims=True)

        @pl.when(first)
        def _():
            dg_ref[...] = dg_part
            db_ref[...] = db_part

        @pl.when(jnp.logical_not(first))
        def _():
            dg_ref[...] += dg_part
            db_ref[...] += db_part

    vec = pl.BlockSpec((1, width), lambda i: (0, 0))
    tiles = pl.BlockSpec((ng, TILE, TILE), lambda i: (0, 0, 0))
    rows0 = pl.BlockSpec((SGU_ROWS, width), lambda i: (i, 0))
    rows1 = pl.BlockSpec((SGU_ROWS, width), lambda i: (i, 1))
    return pl.pallas_call(
        body, name="sgu_bwd", grid=(m // SGU_ROWS,),
        in_specs=[rows0, rows1, rows0, vec, vec, tiles, tiles, tiles],
        out_specs=[rows0, rows0, tiles, tiles, vec, vec],
        out_shape=[jax.ShapeDtypeStruct((m, width), BF16), jax.ShapeDtypeStruct((m, width), BF16),
                   jax.ShapeDtypeStruct((ng, TILE, TILE), F32), jax.ShapeDtypeStruct((ng, TILE, TILE), F32),
                   jax.ShapeDtypeStruct((1, width), F32), jax.ShapeDtypeStruct((1, width), F32)],
        scratch_shapes=[pltpu.VMEM((SGU_ROWS, width), F32), pltpu.VMEM((SGU_ROWS, width), F32)],
        compiler_params=_params(("arbitrary",)),
    )(p, p, dmix, norm_g, norm_b, w_s, w_s_t, bias_tile)


SB_DH = 64
SB_SCALE = 1.0 / math.sqrt(SB_DH)


SB_BLOCK = 256
SB_SUB = SB_BLOCK // TILE


def _sum_matrix(kind):
    j = lax.broadcasted_iota(jnp.int32, (TILE, 2 * TILE), 0)
    s = lax.broadcasted_iota(jnp.int32, (TILE, 2 * TILE), 1)
    tri = {"after": j > s, "upto": j <= s, "before": j < s}[kind]
    return jnp.where(jnp.logical_or(s >= TILE, tri), 1.0, 0.0).astype(BF16)


def _strict_mask():
    r = lax.broadcasted_iota(jnp.int32, (SB_BLOCK, SB_BLOCK), 0)
    c = lax.broadcasted_iota(jnp.int32, (SB_BLOCK, SB_BLOCK), 1)
    return c < r


def _head_lanes(h):
    lane = lax.broadcasted_iota(jnp.int32, (1, TILE), 1)
    return (lane >= h * SB_DH) & (lane < (h + 1) * SB_DH)


def _softplus(z):
    return jnp.maximum(z, 0.0) + jnp.log(1.0 + jnp.exp(-jnp.abs(z)))


def _sb_fwd(p, nseq, t_len, gather):
    m = p.shape[0]
    npair = 4
    ng = len(gather)
    last_step = nseq * npair - 1

    def body(q_ref, k_ref, v_ref, *rest):
        o_ref, lt_ref = rest[ng:ng + 2]
        kh_ref, vh_ref = rest[2 * ng + 2:2 * ng + 4]
        step = pl.program_id(0) * npair + pl.program_id(1)
        send, forward, finish = _gather_steps(rest[ng + 2:2 * ng + 2], *rest[2 * ng + 4:])
        pl.when(step == 0)(send)
        pl.when(step == (last_step + 1) // 2)(forward)
        for h in range(2):
            keep = _head_lanes(h)
            kh_ref[h] = jnp.where(keep, k_ref[...], 0).astype(BF16)
            vh_ref[h] = jnp.where(keep, v_ref[...], 0).astype(BF16)
        summat = _sum_matrix("after")
        strict = _strict_mask()

        def one_pass(q, row0, diag, state):
            rows = pl.ds(row0, SB_BLOCK)
            z, sp, pieces = [], [], []
            for h in range(2):
                zh = lax.dot_general(q, kh_ref[h, rows, :], NT_DIMS, preferred_element_type=F32)
                sph = _softplus(zh)
                logkeep = jnp.where(strict, -sph, 0.0) if diag else -sph
                z.append(zh)
                sp.append(sph)
                pieces += [logkeep[:, b * TILE:(b + 1) * TILE] for b in range(SB_SUB)]
            sums = jnp.dot(jnp.concatenate(pieces, axis=0).astype(BF16), summat, preferred_element_type=F32)
            out = []
            for h in range(2):
                carry, acc = state[2 * h], state[2 * h + 1]
                after = [None] * SB_SUB
                for b in reversed(range(SB_SUB)):
                    part = sums[(h * SB_SUB + b) * SB_BLOCK:(h * SB_SUB + b + 1) * SB_BLOCK]
                    after[b] = part[:, :TILE] + carry
                    carry = carry + part[:, TILE:]
                w = jnp.exp(z[h] - sp[h] + jnp.concatenate(after, axis=1))
                if diag:
                    w = jnp.where(strict, w, 0.0)
                out += [carry, acc + jnp.dot(w.astype(BF16), vh_ref[h, rows, :], preferred_element_type=F32)]
            return tuple(out)

        def q_block(i, _):
            r0 = pl.multiple_of(i * SB_BLOCK, SB_BLOCK)
            q = q_ref[pl.ds(r0, SB_BLOCK), :] * SB_SCALE
            zero = jnp.zeros((SB_BLOCK, TILE), F32)
            state = one_pass(q, r0, True, (zero,) * 4)
            state = lax.fori_loop(
                0, i, lambda jj, st: one_pass(q, pl.multiple_of((i - 1 - jj) * SB_BLOCK, SB_BLOCK), False, st), state)
            o_ref[pl.ds(r0, SB_BLOCK), :] = (state[1] + state[3]).astype(BF16)
            lt_ref[pl.ds(r0, SB_BLOCK), :] = jnp.where(_head_lanes(0), state[0], state[2])
            return 0

        lax.fori_loop(0, t_len // SB_BLOCK, q_block, 0)
        pl.when(step == last_step)(finish)

    def col(k):
        return pl.BlockSpec((t_len, TILE), lambda s, hp: (s, k * npair + hp))

    out = pl.BlockSpec((t_len, TILE), lambda s, hp: (s, hp))
    res = pl.pallas_call(
        body, name="stickbreak_fwd", grid=(nseq, npair), in_specs=[col(2), col(3), col(4)] + [ANY] * ng,
        out_specs=[out, out] + [ANY] * ng,
        out_shape=[jax.ShapeDtypeStruct((m, npair * TILE), BF16), jax.ShapeDtypeStruct((m, npair * TILE), F32)]
        + [jax.ShapeDtypeStruct(b.shape, b.dtype) for b in gather],
        input_output_aliases={3 + a: 2 + a for a in range(ng)},
        scratch_shapes=[pltpu.VMEM((2, t_len, TILE), BF16), pltpu.VMEM((2, t_len, TILE), BF16)] + _gather_sems(ng),
        compiler_params=pltpu.CompilerParams(dimension_semantics=("arbitrary", "arbitrary"),
                                             vmem_limit_bytes=VMEM_LIMIT_BYTES, has_side_effects=True),
    )(p, p, p, *gather)
    return res[0], res[1], res[2:]


def _sb_bwd(p, dmix, ltot, nseq, t_len, exchange):
    m = p.shape[0]
    npair = 4
    ne = len(exchange)
    last_step = nseq * npair - 1

    def body(q_ref, k_ref, v_ref, do_ref, lt_ref, *rest):
        dq_ref, dk_ref, dv_ref = rest[ne:ne + 3]
        kh_ref, vh_ref, dk_acc, dv_acc = rest[2 * ne + 3:2 * ne + 7]
        step = pl.program_id(0) * npair + pl.program_id(1)
        send, finish = _exchange_steps(rest[:ne], rest[ne + 3:2 * ne + 3], *rest[2 * ne + 7:])
        pl.when(step == 0)(send)
        for h in range(2):
            keep = _head_lanes(h)
            kh_ref[h] = jnp.where(keep, k_ref[...], 0).astype(BF16)
            vh_ref[h] = jnp.where(keep, v_ref[...], 0).astype(BF16)
        dk_acc[...] = jnp.zeros_like(dk_acc)
        dv_acc[...] = jnp.zeros_like(dv_acc)
        sum_upto = _sum_matrix("upto")
        sum_before = _sum_matrix("before")
        strict = _strict_mask()
        lane = lax.broadcasted_iota(jnp.int32, (SB_BLOCK, TILE), 1)

        def running(x, matrix, start):
            pieces = [x[h][:, b * TILE:(b + 1) * TILE] for h in range(2) for b in range(SB_SUB)]
            sums = jnp.dot(jnp.concatenate(pieces, axis=0).astype(BF16), matrix, preferred_element_type=F32)
            wide, ends = [], []
            for h in range(2):
                total, cols = start[h], []
                for b in range(SB_SUB):
                    part = sums[(h * SB_SUB + b) * SB_BLOCK:(h * SB_SUB + b + 1) * SB_BLOCK]
                    cols.append(part[:, :TILE] + total)
                    total = total + part[:, TILE:]
                wide.append(jnp.concatenate(cols, axis=1))
                ends.append(total)
            return wide, ends

        def one_pass(q, do, qh, doh, ltot, row0, diag, state):
            rows = pl.ds(row0, SB_BLOCK)
            z, sp, logkeep = [], [], []
            for h in range(2):
                zh = lax.dot_general(q, kh_ref[h, rows, :], NT_DIMS, preferred_element_type=F32)
                sph = _softplus(zh)
                z.append(zh)
                sp.append(sph)
                logkeep.append(jnp.where(strict, -sph, 0.0) if diag else -sph)
            upto, sum_l = running(logkeep, sum_upto, [state[0], state[3]])
            w, g = [], []
            for h in range(2):
                wh = jnp.exp(z[h] - sp[h] + (ltot[h] - upto[h]))
                if diag:
                    wh = jnp.where(strict, wh, 0.0)
                w.append(wh)
                g.append(wh * lax.dot_general(do, vh_ref[h, rows, :], NT_DIMS, preferred_element_type=F32))
            g_before, sum_g = running(g, sum_before, [state[1], state[4]])
            out, dk_new, dv_new = [], 0.0, 0.0
            for h in range(2):
                dz = g[h] - jnp.exp(z[h] - sp[h]) * (g[h] + g_before[h])
                if diag:
                    dz = jnp.where(strict, dz, 0.0)
                dzb = dz.astype(BF16)
                dq = state[3 * h + 2] + jnp.dot(dzb, kh_ref[h, rows, :], preferred_element_type=F32)
                dk_new = dk_new + lax.dot_general(dzb, qh[h], TN_DIMS, preferred_element_type=F32)
                dv_new = dv_new + lax.dot_general(w[h].astype(BF16), doh[h], TN_DIMS, preferred_element_type=F32)
                out += [sum_l[h], sum_g[h], dq]
            dk_acc[rows, :] += dk_new
            dv_acc[rows, :] += dv_new
            return tuple(out)

        def q_block(i, _):
            r0 = pl.multiple_of(i * SB_BLOCK, SB_BLOCK)
            q = q_ref[pl.ds(r0, SB_BLOCK), :] * SB_SCALE
            do = do_ref[pl.ds(r0, SB_BLOCK), :]
            lt = lt_ref[pl.ds(r0, SB_BLOCK), :]
            qh, doh, ltot = [], [], []
            for h in range(2):
                keep = _head_lanes(h)
                qh.append(jnp.where(keep, q, 0).astype(BF16))
                doh.append(jnp.where(keep, do, 0).astype(BF16))
                ltot.append(jnp.sum(jnp.where(lane == h * SB_DH, lt, 0.0), axis=1, keepdims=True))
            zero = jnp.zeros((SB_BLOCK, TILE), F32)
            state = lax.fori_loop(
                0, i,
                lambda jj, st: one_pass(q, do, qh, doh, ltot, pl.multiple_of(jj * SB_BLOCK, SB_BLOCK), False, st),
                (zero,) * 6)
            state = one_pass(q, do, qh, doh, ltot, r0, True, state)
            dq_ref[pl.ds(r0, SB_BLOCK), :] = ((state[2] + state[5]) * SB_SCALE).astype(BF16)
            return 0

        lax.fori_loop(0, t_len // SB_BLOCK, q_block, 0)
        dk_ref[...] = dk_acc[...].astype(BF16)
        dv_ref[...] = dv_acc[...].astype(BF16)
        pl.when(step == last_step)(finish)

    def col(k):
        return pl.BlockSpec((t_len, TILE), lambda s, hp: (s, k * npair + hp))

    out = pl.BlockSpec((t_len, TILE), lambda s, hp: (s, hp))
    width = npair * TILE
    res = pl.pallas_call(
        body, name="stickbreak_bwd", grid=(nseq, npair),
        in_specs=[col(2), col(3), col(4), col(1), out] + [ANY] * ne, out_specs=[out, out, out] + [ANY] * ne,
        out_shape=[jax.ShapeDtypeStruct((m, width), BF16)] * 3 + _exchange_shapes(exchange),
        scratch_shapes=[pltpu.VMEM((2, t_len, TILE), BF16), pltpu.VMEM((2, t_len, TILE), BF16),
                        pltpu.VMEM((t_len, TILE), F32), pltpu.VMEM((t_len, TILE), F32)] + _exchange_sems(ne),
        compiler_params=pltpu.CompilerParams(dimension_semantics=("arbitrary", "arbitrary"),
                                             vmem_limit_bytes=VMEM_LIMIT_BYTES, has_side_effects=True),
    )(p, p, p, dmix, ltot, *exchange)
    return res[0], res[1], res[2], res[3:]


def _adam_math(w, g, m, v):
    m = ADAM_B1 * m + (1.0 - ADAM_B1) * g
    v = ADAM_B2 * v + (1.0 - ADAM_B2) * (g * g)
    m_hat = m / (1.0 - ADAM_B1 ** ADAM_STEP)
    v_hat = v / (1.0 - ADAM_B2 ** ADAM_STEP)
    delta = -ADAM_LR * (m_hat / (jnp.sqrt(v_hat) + ADAM_EPS) + ADAM_WD * w)
    return delta, m, v


def _cast_place(w, layer, pos, *, name):
    _, r, c = w.shape
    tr = min(r, 256)

    def body(pos_ref, w_ref, o_ref):
        o_ref[...] = w_ref[...].astype(BF16)

    grid_spec = pltpu.PrefetchScalarGridSpec(
        num_scalar_prefetch=1, grid=(r // tr,),
        in_specs=[pl.BlockSpec((None, tr, c), lambda i, pos_ref: (layer, i, 0))],
        out_specs=pl.BlockSpec((None, None, tr, c), lambda i, pos_ref: (0, pos_ref[0], i, 0)))
    return pl.pallas_call(
        body, name=name, grid_spec=grid_spec, out_shape=jax.ShapeDtypeStruct((1, N_CHIP, r, c), BF16),
        compiler_params=_params(("parallel",)),
    )(pos, w)


def _pair_sum(mine, got, pos, *, name):
    l_dim, s_dim, h, c = got.shape
    th = min(h, 512)
    nt = h // th

    def body(pos_ref, a_ref, b_ref, o_ref):
        o_ref[...] = (a_ref[...].astype(F32) + b_ref[...].astype(F32)).astype(BF16)

    spec = pl.BlockSpec((None, None, th, c), lambda l, s, i, pos_ref: (l, s, i, 0))
    grid_spec = pltpu.PrefetchScalarGridSpec(
        num_scalar_prefetch=1, grid=(l_dim, s_dim, nt),
        in_specs=[pl.BlockSpec((None, None, th, c), lambda l, s, i, pos_ref: (l, s, pos_ref[1] * nt + i, 0)), spec],
        out_specs=spec)
    return pl.pallas_call(
        body, name=name, grid_spec=grid_spec, out_shape=jax.ShapeDtypeStruct(got.shape, BF16),
        compiler_params=_params(("parallel",) * 3),
    )(pos, mine, got)


def _chip_sum(sums, landed, pos, *, name):
    l_dim, _, h, c = sums.shape
    th = min(h, 512)
    nt = h // th

    def body(pos_ref, own, r0, r1, r2, o_ref):
        o_ref[...] = ((own[...].astype(F32) + r0[...].astype(F32)) + r1[...].astype(F32)) + r2[...].astype(F32)

    def piece(k):
        return pl.BlockSpec((None, None, th, c), lambda l, i, pos_ref: (l, k, i, 0))

    grid_spec = pltpu.PrefetchScalarGridSpec(
        num_scalar_prefetch=1, grid=(l_dim, nt),
        in_specs=[pl.BlockSpec((None, None, th, c), lambda l, i, pos_ref: (l, pos_ref[0], i, 0)),
                  piece(0), piece(1), piece(2)],
        out_specs=pl.BlockSpec((None, th, c), lambda l, i, pos_ref: (l, pos_ref[1] * nt + i, 0)))
    return pl.pallas_call(
        body, name=name, grid_spec=grid_spec, out_shape=jax.ShapeDtypeStruct((l_dim, 2 * h, c), F32),
        compiler_params=_params(("parallel",) * 2),
    )(pos, sums, landed, landed, landed)


def _adam_big(w, m, v, grads, *, name):
    l_dim, r, c = w.shape
    assert len(grads) == l_dim
    tr = min(r, 256)

    def body(*refs):
        w_ref, m_ref, v_ref = refs[:3]
        g_refs = refs[3:3 + l_dim]
        go_ref, d_ref, mo_ref, vo_ref = refs[3 + l_dim:]
        g = g_refs[0][...]
        for l in range(1, l_dim):
            g = jnp.where(pl.program_id(0) == l, g_refs[l][...], g)
        delta, m_new, v_new = _adam_math(w_ref[...], g, m_ref[...], v_ref[...])
        go_ref[...] = g
        d_ref[...] = delta
        mo_ref[...] = m_new
        vo_ref[...] = v_new

    spec = pl.BlockSpec((None, tr, c), lambda l, i: (l, i, 0))
    gspec = pl.BlockSpec((None, tr, c), lambda l, i: (0, i, 0))
    return pl.pallas_call(
        body, name=name, grid=(l_dim, r // tr), in_specs=[spec] * 3 + [gspec] * l_dim, out_specs=[spec] * 4,
        out_shape=[jax.ShapeDtypeStruct(w.shape, F32)] * 4, compiler_params=_params(("parallel",) * 2),
    )(w, m, v, *grads)


def _position():
    return lax.axis_index("x"), lax.axis_index("y"), lax.axis_index("c")


def _other_chips(x, y):
    return [(1 - x, y), (x, 1 - y), (1 - x, 1 - y)]


def _remote(src, dst, send_sem, recv_sem, device):
    return pltpu.make_async_remote_copy(src_ref=src, dst_ref=dst, send_sem=send_sem, recv_sem=recv_sem,
                                        device_id=device, device_id_type=MESH)


ANY = pl.BlockSpec(memory_space=pl.ANY)


def _gather_sems(n):
    return [pltpu.SemaphoreType.DMA((3 * n,))] * 4


def _gather_steps(outs, send_sems, recv_sems, fwd_send, fwd_recv):
    n = len(outs)
    x, y, c = _position()
    chips = _other_chips(x, y)
    sibling = (x, y, 1 - c)

    def half(a, chip, core):
        h = outs[a].shape[2] // 2
        return outs[a].at[:, 2 * chip[0] + chip[1], pl.ds(core * h, h), :]

    def over_ici(a, k, chip):
        block = half(a, chip, c)
        return _remote(block, block, send_sems.at[3 * a + k], recv_sems.at[3 * a + k], (*chips[k], c))

    def over_d2d(a, k, core):
        block = half(a, chips[k], core)
        return _remote(block, block, fwd_send.at[3 * a + k], fwd_recv.at[3 * a + k], sibling)

    def send():
        for a in range(n):
            for k in range(3):
                over_ici(a, k, (x, y)).start()

    def forward():
        for k in range(3):
            for a in range(n):
                over_ici(a, k, chips[k]).wait_recv()
                over_d2d(a, k, c).start()

    def finish():
        for k in range(3):
            for a in range(n):
                over_d2d(a, k, 1 - c).wait_recv()
        for a in range(n):
            for k in range(3):
                over_ici(a, k, (x, y)).wait_send()
                over_d2d(a, k, c).wait_send()

    return send, forward, finish


def _swap_halves(grads, *, name):
    n = len(grads)

    def body(*refs):
        send, finish = _swap_steps(refs[:n], refs[n:2 * n], *refs[2 * n:])
        send()
        finish()

    sem = pltpu.SemaphoreType.DMA((n,))
    return pl.pallas_call(
        body, name=name, in_specs=[ANY] * n, out_specs=[ANY] * n, out_shape=_swap_shapes(grads),
        scratch_shapes=[sem, sem], compiler_params=pltpu.CompilerParams(has_side_effects=True),
    )(*grads)


def _swap_shapes(grads):
    return [jax.ShapeDtypeStruct(g.shape[:2] + (g.shape[2] // 2, g.shape[3]), g.dtype) for g in grads]


def _swap_steps(ins, outs, send_sems, recv_sems):
    x, y, c = _position()

    def copy(a):
        h = ins[a].shape[2] // 2
        return _remote(ins[a].at[:, :, pl.ds((1 - c) * h, h), :], outs[a], send_sems.at[a], recv_sems.at[a],
                       (x, y, 1 - c))

    def send():
        for a in range(len(ins)):
            copy(a).start()

    def finish():
        for a in range(len(ins)):
            copy(a).wait()

    return send, finish


def _exchange_shapes(sums):
    return [jax.ShapeDtypeStruct((s.shape[0], 3) + s.shape[2:], s.dtype) for s in sums]


def _exchange_sems(n):
    return [pltpu.SemaphoreType.DMA((3 * n,))] * 2


def _exchange_steps(ins, outs, send_sems, recv_sems):
    n = len(ins)
    x, y, c = _position()
    chips = _other_chips(x, y)

    def copy(a, k):
        chip = chips[k]
        return _remote(ins[a].at[:, 2 * chip[0] + chip[1]], outs[a].at[:, k],
                       send_sems.at[3 * a + k], recv_sems.at[3 * a + k], (*chip, c))

    def send():
        for a in range(n):
            for k in range(3):
                copy(a, k).start()

    def finish():
        for a in range(n):
            for k in range(3):
                copy(a, k).wait()

    return send, finish


def _join_halves(bufs, *, name):
    n = len(bufs)

    def body(*refs):
        send, finish = _join_steps(refs[n:2 * n], *refs[2 * n:])
        send()
        finish()

    sem = pltpu.SemaphoreType.DMA((n,))
    return pl.pallas_call(
        body, name=name, in_specs=[ANY] * n, out_specs=[ANY] * n,
        out_shape=[jax.ShapeDtypeStruct(b.shape, b.dtype) for b in bufs],
        input_output_aliases={a: a for a in range(n)},
        scratch_shapes=[sem, sem], compiler_params=pltpu.CompilerParams(has_side_effects=True),
    )(*bufs)


def _join_steps(outs, send_sems, recv_sems):
    x, y, c = _position()

    def copy(a, core):
        h = outs[a].shape[1] // 2
        half = outs[a].at[:, pl.ds(core * h, h), :]
        return _remote(half, half, send_sems.at[a], recv_sems.at[a], (x, y, 1 - c))

    def send():
        for a in range(len(outs)):
            copy(a, c).start()

    def finish():
        for a in range(len(outs)):
            copy(a, c).wait_send()
            copy(a, 1 - c).wait_recv()

    return send, finish


def _allgather_steps(ins, outs, send_sems, recv_sems, local_sems):
    n = len(ins)
    x, y, c = _position()
    me, sibling = (x, y, c), (x, y, 1 - c)
    chips = _other_chips(x, y)

    def slot(a, dev):
        return outs[a].at[4 * dev[0] + 2 * dev[1] + dev[2]]

    def copy(a, k, block, to, own=False):
        return _remote(ins[a] if own else slot(a, block), slot(a, block),
                       send_sems.at[7 * a + k], recv_sems.at[7 * a + k], to)

    def first(a):
        return [copy(a, 0, me, sibling, own=True)] + [copy(a, 1 + k, me, (*chips[k], c), own=True) for k in range(3)]

    def local(a):
        return pltpu.make_async_copy(ins[a], slot(a, me), local_sems.at[a])

    def send():
        for a in range(n):
            local(a).start()
            for cp in first(a):
                cp.start()

    def forward():
        for a in range(n):
            for k in range(3):
                copy(a, 1 + k, (*chips[k], c), me).wait_recv()
                copy(a, 4 + k, (*chips[k], c), sibling).start()

    def finish():
        for a in range(n):
            copy(a, 0, sibling, me).wait_recv()
            for k in range(3):
                copy(a, 4 + k, (*chips[k], 1 - c), me).wait_recv()
        for a in range(n):
            for cp in first(a) + [copy(a, 4 + k, (*chips[k], c), sibling) for k in range(3)]:
                cp.wait_send()
            local(a).wait()

    return send, forward, finish


def _allreduce_small(packs):
    n = len(packs)

    def body(*refs):
        ins, outs, gath = refs[:n], refs[n:2 * n], refs[2 * n:3 * n]
        send_sems, recv_sems = refs[3 * n:]
        x, y, c = _position()
        me, sibling = (x, y, c), (x, y, 1 - c)
        chips = _other_chips(x, y)

        def slot(a, dev):
            return gath[a].at[4 * dev[0] + 2 * dev[1] + dev[2]]

        def copy(a, k, block, to, src=None):
            return _remote(slot(a, block) if src is None else src, slot(a, block),
                           send_sems.at[7 * a + k], recv_sems.at[7 * a + k], to)

        started = []
        for a in range(n):
            slot(a, me)[...] = ins[a][...]
            first = [copy(a, 0, me, sibling, src=ins[a])]
            first += [copy(a, 1 + k, me, (*chip, c), src=ins[a]) for k, chip in enumerate(chips)]
            for cp in first:
                cp.start()
            started += first
        for a in range(n):
            for k, chip in enumerate(chips):
                copy(a, 1 + k, (*chip, c), me).wait_recv()
                cp = copy(a, 4 + k, (*chip, c), sibling)
                cp.start()
                started.append(cp)
        for a in range(n):
            copy(a, 0, sibling, me).wait_recv()
            for k, chip in enumerate(chips):
                copy(a, 4 + k, (*chip, 1 - c), me).wait_recv()
        for cp in started:
            cp.wait_send()
        for a in range(n):
            total = gath[a][0]
            for d in range(1, N_DEV):
                total = total + gath[a][d]
            outs[a][...] = total

    vmem = pl.BlockSpec(memory_space=pltpu.VMEM)
    sem = pltpu.SemaphoreType.DMA((7 * n,))
    return pl.pallas_call(
        body, name="allreduce_small", in_specs=[vmem] * n, out_specs=[vmem] * n,
        out_shape=[jax.ShapeDtypeStruct(p.shape, p.dtype) for p in packs],
        scratch_shapes=[pltpu.VMEM((N_DEV,) + p.shape, p.dtype) for p in packs] + [sem, sem],
        compiler_params=pltpu.CompilerParams(has_side_effects=True, vmem_limit_bytes=VMEM_LIMIT_BYTES),
    )(*packs)


LOSS_ROW = 1040


def _pad_rows(a, rows=8):
    return jnp.concatenate([a, jnp.zeros((rows - a.shape[0], a.shape[1]), a.dtype)], axis=0)

def _adam_small(wide, mid, narrow, late, params):
    names = ["mix_norm_g", "mlp_norm_g", "final_norm_g", "conv_b", "conv_w", "sgu_norm_g", "sgu_norm_b",
             "pool_w", "pool_scale", "sgu_w", "sgu_b"]
    n = len(names)

    def body(*refs):
        wmv = refs[4:4 + 3 * n]
        outs = refs[4 + 3 * n:]
        x, y, _ = _position()
        q = 2 * x + y

        def total(ref):
            t = ref[0]
            for dev in range(1, N_DEV):
                t = t + ref[dev]
            return t

        wide_sum, mid_sum, narrow_sum = total(refs[0]), total(refs[1]), total(refs[2])
        late_ref = refs[3]

        def my_quarter(rows):
            parts = [rows[:, s * TILE:(s + 1) * TILE] for s in range(N_CHIP)]
            return jnp.where(q == 0, parts[0], jnp.where(q == 1, parts[1], jnp.where(q == 2, parts[2], parts[3])))

        def tiles(first_row):
            return [((0, g), narrow_sum[first_row + g * TILE:first_row + (g + 1) * TILE, :]) for g in range(4)]

        grads = {
            "mix_norm_g": [((), wide_sum[0:2, :] + late_ref[0:2, :])],
            "mlp_norm_g": [((), wide_sum[8:10, :])],
            "final_norm_g": [((), wide_sum[16:17, :])],
            "conv_b": [((), mid_sum[0:1, :])],
            "conv_w": [((0,), my_quarter(mid_sum[8:11, :]))],
            "sgu_norm_g": [((), my_quarter(mid_sum[16:17, :]))],
            "sgu_norm_b": [((), my_quarter(mid_sum[24:25, :]))],
            "pool_w": tiles(0),
            "sgu_w": tiles(512),
            "pool_scale": [((0,), narrow_sum[1024:1028, :])],
            "sgu_b": [((0,), narrow_sum[1032:1036, :])],
        }
        outs[4 * n][...] = narrow_sum[LOSS_ROW:LOSS_ROW + 8, :]
        for i, name in enumerate(names):
            w_ref, m_ref, v_ref = wmv[3 * i:3 * i + 3]
            for lead, g in grads[name]:
                idx = lead + (slice(None), slice(None))
                delta, m_new, v_new = _adam_math(w_ref[idx], g, m_ref[idx], v_ref[idx])
                outs[4 * i][idx] = g
                outs[4 * i + 1][idx] = delta
                outs[4 * i + 2][idx] = m_new
                outs[4 * i + 3][idx] = v_new

    vmem = pl.BlockSpec(memory_space=pltpu.VMEM)
    args, out_shape = [wide, mid, narrow, late], []
    for name in names:
        w, m, v = params[name]
        args += [w, m, v]
        out_shape += [jax.ShapeDtypeStruct(w.shape, F32)] * 4
    out_shape.append(jax.ShapeDtypeStruct((8, TILE), F32))
    res = pl.pallas_call(
        body, name="adam_small", in_specs=[vmem] * len(args), out_specs=[vmem] * len(out_shape),
        out_shape=out_shape, compiler_params=pltpu.CompilerParams(vmem_limit_bytes=VMEM_LIMIT_BYTES),
    )(*args)
    return {name: res[4 * i:4 * i + 4] for i, name in enumerate(names)}, res[4 * n]


def _pair_sums(grads, got, pos, tag):
    return [_pair_sum(a, b, pos, name=f"pair_sum_{tag}{i}") for i, (a, b) in enumerate(zip(grads, got))]


def _chip_sums(sums, landed, pos, tag):
    return [_chip_sum(s, r, pos, name=f"chip_sum_{tag}{i}") for i, (s, r) in enumerate(zip(sums, landed))]


def kernel(x, mix_norm_g, mlp_norm_g, ab_w_in, pool_w, pool_scale, conv_w, conv_b, ab_w_out, cd_w_in, sgu_norm_g, sgu_norm_b, sgu_w, sgu_b, cd_w_out, mlp_w1, mlp_w2, final_norm_g, loss_target, m_mix_norm_g, m_mlp_norm_g, m_ab_w_in, m_pool_w, m_pool_scale, m_conv_w, m_conv_b, m_ab_w_out, m_cd_w_in, m_sgu_norm_g, m_sgu_norm_b, m_sgu_w, m_sgu_b, m_cd_w_out, m_mlp_w1, m_mlp_w2, m_final_norm_g, v_mix_norm_g, v_mlp_norm_g, v_ab_w_in, v_pool_w, v_pool_scale, v_conv_w, v_conv_b, v_ab_w_out, v_cd_w_in, v_sgu_norm_g, v_sgu_norm_b, v_sgu_w, v_sgu_b, v_cd_w_out, v_mlp_w1, v_mlp_w2, v_final_norm_g):
    nseq, t_len, d = x.shape
    m_tok = nseq * t_len
    h0 = x.reshape(m_tok, d)
    target = loss_target.reshape(m_tok, d)

    x_idx, y_idx = lax.axis_index("x"), lax.axis_index("y")
    q_idx = 2 * x_idx + y_idx
    pos = jnp.stack([q_idx, lax.axis_index("c")]).astype(jnp.int32)
    def shard_buffer(w, layer, tag):
        return _cast_place(w, layer, pos, name=f"cast_place_{tag}")

    def row_block(w):
        return w.reshape(1, 1, -1, w.shape[-1])

    later_weights = [shard_buffer(cd_w_out, 0, "cd_out"), shard_buffer(mlp_w1, 1, "w1_1"),
                     shard_buffer(mlp_w2, 1, "w2_1")]

    pool_w3, pool_scale3 = pool_w[0], pool_scale[0].reshape(4, 1, TILE)
    sgu_w3 = sgu_w[0]
    sgu_w3_t = jnp.swapaxes(sgu_w3, 1, 2)
    sgu_bias_tile = jnp.broadcast_to(sgu_b[0][:, :, None], (4, TILE, TILE))
    conv_w2, conv_b2 = conv_w[0], conv_b
    def place_quarter(v):
        return lax.dynamic_update_slice(jnp.zeros((v.shape[0], 4 * TILE), F32), v, (0, q_idx * TILE))

    sharded_small = jnp.concatenate(
        [place_quarter(conv_w[0]), place_quarter(sgu_norm_g), place_quarter(sgu_norm_b),
         jnp.zeros((3, 4 * TILE), F32)], axis=0)
    sharded_small, = _allreduce_small([sharded_small])
    sharded_small = sharded_small * 0.5
    conv_w_full = sharded_small[0:3]
    sgu_g_full = sharded_small[3:4]
    sgu_b_full = sharded_small[4:5]

    xn0, ((w_ab_in,),) = _rms_fwd(h0, mix_norm_g[0:1], name="rms_fwd_mix0",
                                  rider=[("gather", [shard_buffer(ab_w_in, 0, "ab_in")])])
    p_ab, ((w_1_0,),) = _mm_nn(xn0, w_ab_in, 0, out_dtype=BF16, name="ab_in_proj",
                               rider=[("gather", [shard_buffer(mlp_w1, 0, "w1_0")])])
    mix0, ((w_ab_out,),) = _ab_fwd(p_ab, pool_w3, pool_scale3, conv_w_full, conv_b2, nseq, t_len,
                                   rider=[("gather", [shard_buffer(ab_w_out, 0, "ab_out")])])
    w_ab_out = row_block(w_ab_out)
    h1, hn0 = _mm_nn(mix0, w_ab_out, 0, out_dtype=F32, name="ab_out_proj", epilogue="residual", extra=h0,
                     norm_g=mlp_norm_g[0:1])
    (act0, relu0), ((w_2_0,),) = _mm_nn(hn0, w_1_0, 0, out_dtype=BF16, name="mlp0_up", epilogue="relu2",
                                        rider=[("gather", [shard_buffer(mlp_w2, 0, "w2_0")])])
    w_2_0 = row_block(w_2_0)
    (h2, xn1), ((w_cd_in,),) = _mm_nn(act0, w_2_0, 0, out_dtype=F32, name="mlp0_down", epilogue="residual", extra=h1,
                                      norm_g=mix_norm_g[1:2],
                                      rider=[("gather", [shard_buffer(cd_w_in, 0, "cd_in")])])

    p_cd = _mm_nn(xn1, w_cd_in, 0, out_dtype=BF16, name="cd_in_proj")
    c_out = _sgu_fwd(p_cd, sgu_g_full, sgu_b_full, sgu_w3, sgu_bias_tile)
    d_out, ltot, (w_cd_out, w_1_1, w_2_1) = _sb_fwd(p_cd, nseq, t_len, later_weights)
    w_cd_out, w_2_1 = row_block(w_cd_out), row_block(w_2_1)
    mix1 = jnp.concatenate([c_out, d_out], axis=1)
    h3, hn1 = _mm_nn(mix1, w_cd_out, 0, out_dtype=F32, name="cd_out_proj", epilogue="residual", extra=h2,
                     norm_g=mlp_norm_g[1:2])
    act1, relu1 = _mm_nn(hn1, w_1_1, 0, out_dtype=BF16, name="mlp1_up", epilogue="relu2")
    h4 = _mm_nn(act1, w_2_1, 0, out_dtype=F32, name="mlp1_down", epilogue="residual", extra=h3)

    dh4, dh4_bf, dg_final, loss_tile = _final_loss(h4, final_norm_g.reshape(1, d), target)

    def as_pieces(g):
        return g.reshape(1, N_CHIP, -1, g.shape[-1]) if g.shape[1] == 1 else g

    dz1 = _mm_nt(dh4_bf, w_2_1, 0, out_dtype=BF16, name="mlp1_down_bwd", epilogue="relu2_bwd", extra=relu1)
    g_w2_1 = as_pieces(_mm_tn(act1, dh4_bf, 1, name="mlp1_down_wgrad"))
    g_w1_1 = _mm_tn(hn1, dz1, N_CHIP, name="mlp1_up_wgrad")
    (dh3, dh3_bf, dg_mlp1), (got_a,) = _mm_nt(
        dz1, w_1_1, 0, out_dtype=F32, name="mlp1_up_bwd", epilogue="rms_bwd",
        extra=(h3, mlp_norm_g[1:2], dh4), rider=[("swap", [g_w1_1, g_w2_1])])

    g_cd_out = as_pieces(_mm_tn(mix1, dh3_bf, 1, name="cd_out_wgrad"))
    dmix1, (got_cd_out,) = _mm_nt(dh3_bf, w_cd_out, 0, out_dtype=BF16, name="cd_out_bwd",
                                  rider=[("swap", [g_cd_out])])
    sums_a = _pair_sums([g_w1_1, g_w2_1, g_cd_out], got_a + got_cd_out, pos, "a")
    du, dv, dsgu_w, dsgu_bs, dsgu_g, dsgu_b = _sgu_bwd(p_cd, dmix1, sgu_g_full, sgu_b_full, sgu_w3, sgu_w3_t,
                                                      sgu_bias_tile)
    dq, dk, dvv, landed_a = _sb_bwd(p_cd, dmix1, ltot, nseq, t_len, sums_a)
    halves_a = _chip_sums(sums_a, landed_a, pos, "a")
    dp_cd = jnp.concatenate([du, dv, dq, dk, dvv], axis=1)
    g_cd_in, ((r_w1_1, r_w2_1, r_cd_out),) = _mm_tn(xn1, dp_cd, N_CHIP, name="cd_in_wgrad",
                                                    rider=[("join", halves_a)])
    (dh2, dh2_bf, dg_mix1), (got_c,) = _mm_nt(
        dp_cd, w_cd_in, 0, out_dtype=F32, name="cd_in_bwd", epilogue="rms_bwd",
        extra=(h2, mix_norm_g[1:2], dh3), rider=[("swap", [g_cd_in])])

    sums_c = _pair_sums([g_cd_in], got_c, pos, "c")
    dz0, (landed_c,) = _mm_nt(dh2_bf, w_2_0, 0, out_dtype=BF16, name="mlp0_down_bwd", epilogue="relu2_bwd",
                              extra=relu0, rider=[("exchange", sums_c)])
    halves_c = _chip_sums(sums_c, landed_c, pos, "c")
    g_w2_0, ((r_cd_in,),) = _mm_tn(act0, dh2_bf, 1, name="mlp0_down_wgrad", rider=[("join", halves_c)])
    g_w2_0 = as_pieces(g_w2_0)
    g_w1_0, (got_d,) = _mm_tn(hn0, dz0, N_CHIP, name="mlp0_up_wgrad", rider=[("swap", [g_w2_0])])
    sums_d = _pair_sums([g_w2_0], got_d, pos, "d")
    (dh1, dh1_bf, dg_mlp0), (landed_d, got_e) = _mm_nt(
        dz0, w_1_0, 0, out_dtype=F32, name="mlp0_up_bwd", epilogue="rms_bwd",
        extra=(h1, mlp_norm_g[0:1], dh2), rider=[("exchange", sums_d), ("swap", [g_w1_0])])
    halves_d = _chip_sums(sums_d, landed_d, pos, "d")
    sums_e = _pair_sums([g_w1_0], got_e, pos, "e")

    dmix0, ((r_w2_0,),) = _mm_nt(dh1_bf, w_ab_out, 0, out_dtype=BF16, name="ab_out_bwd", rider=[("join", halves_d)])
    g_ab_out = as_pieces(_mm_tn(mix0, dh1_bf, 1, name="ab_out_wgrad"))
    (da, dxb, dgb, dgc, dpool_w, dpool_scale, dconv_w, dconv_b), (landed_e, got_f) = _ab_bwd(
        p_ab, dmix0, pool_w3, pool_scale3, conv_w_full, conv_b2, nseq, t_len,
        rider=[("exchange", sums_e), ("swap", [g_ab_out])])
    halves_e = _chip_sums(sums_e, landed_e, pos, "e")
    sums_f = _pair_sums([g_ab_out], got_f, pos, "f")
    dp_ab = jnp.concatenate([da, dxb, dgb, dgc], axis=1)
    g_ab_in, (landed_f, (r_w1_0,)) = _mm_tn(xn0, dp_ab, N_CHIP, name="ab_in_wgrad",
                                            rider=[("exchange", sums_f), ("join", halves_e)])
    halves_f = _chip_sums(sums_f, landed_f, pos, "f")
    sums_g = _pair_sums([g_ab_in], _swap_halves([g_ab_in], name="swap_halves_g"), pos, "g")
    wide = jnp.concatenate([_pad_rows(jnp.concatenate([jnp.zeros_like(dg_mix1), dg_mix1], axis=0)),
                            _pad_rows(jnp.concatenate([dg_mlp0, dg_mlp1], axis=0)), _pad_rows(dg_final)], axis=0)
    mid = jnp.concatenate([_pad_rows(dconv_b), _pad_rows(dconv_w), _pad_rows(dsgu_g), _pad_rows(dsgu_b)], axis=0)
    narrow = jnp.concatenate(
        [dpool_w.reshape(4 * TILE, TILE), dsgu_w.reshape(4 * TILE, TILE), _pad_rows(dpool_scale.reshape(4, TILE)),
         _pad_rows(dsgu_bs[:, :, 0]), loss_tile], axis=0)
    (grad_x, _, dg_mix0), (landed_g, (r_ab_out,), (wide, mid, narrow)) = _mm_nt(
        dp_ab, w_ab_in, 0, out_dtype=F32, name="ab_in_bwd", epilogue="rms_bwd",
        extra=(h0, mix_norm_g[0:1], dh1),
        rider=[("exchange", sums_g), ("join", halves_f), ("allgather", [wide, mid, narrow])])
    r_ab_in, = _join_halves(_chip_sums(sums_g, landed_g, pos, "g"), name="join_halves_g")

    big_out = {
        "ab_w_in": _adam_big(ab_w_in, m_ab_w_in, v_ab_w_in, [r_ab_in], name="adam_ab_w_in"),
        "ab_w_out": _adam_big(ab_w_out, m_ab_w_out, v_ab_w_out, [r_ab_out], name="adam_ab_w_out"),
        "cd_w_in": _adam_big(cd_w_in, m_cd_w_in, v_cd_w_in, [r_cd_in], name="adam_cd_w_in"),
        "cd_w_out": _adam_big(cd_w_out, m_cd_w_out, v_cd_w_out, [r_cd_out], name="adam_cd_w_out"),
        "mlp_w1": _adam_big(mlp_w1, m_mlp_w1, v_mlp_w1, [r_w1_0, r_w1_1], name="adam_mlp_w1"),
        "mlp_w2": _adam_big(mlp_w2, m_mlp_w2, v_mlp_w2, [r_w2_0, r_w2_1], name="adam_mlp_w2"),
    }

    late, = _allreduce_small([_pad_rows(dg_mix0)])
    small_out, loss_sum = _adam_small(wide, mid, narrow, late, {
        "mix_norm_g": (mix_norm_g, m_mix_norm_g, v_mix_norm_g),
        "mlp_norm_g": (mlp_norm_g, m_mlp_norm_g, v_mlp_norm_g),
        "final_norm_g": tuple(a.reshape(1, d) for a in (final_norm_g, m_final_norm_g, v_final_norm_g)),
        "conv_b": (conv_b, m_conv_b, v_conv_b),
        "conv_w": (conv_w, m_conv_w, v_conv_w),
        "sgu_norm_g": (sgu_norm_g, m_sgu_norm_g, v_sgu_norm_g),
        "sgu_norm_b": (sgu_norm_b, m_sgu_norm_b, v_sgu_norm_b),
        "pool_w": (pool_w, m_pool_w, v_pool_w),
        "pool_scale": (pool_scale, m_pool_scale, v_pool_scale),
        "sgu_w": (sgu_w, m_sgu_w, v_sgu_w),
        "sgu_b": (sgu_b, m_sgu_b, v_sgu_b),
    })
    small_out["final_norm_g"] = [a.reshape(d) for a in small_out["final_norm_g"]]

    order = ["mix_norm_g", "mlp_norm_g", "ab_w_in", "pool_w", "pool_scale", "conv_w", "conv_b", "ab_w_out",
             "cd_w_in", "sgu_norm_g", "sgu_norm_b", "sgu_w", "sgu_b", "cd_w_out", "mlp_w1", "mlp_w2",
             "final_norm_g"]
    both = {**big_out, **small_out}
    loss = loss_sum[0, 0]
    outs = [loss, grad_x.reshape(nseq, t_len, d)]
    for kind in range(4):
        outs += [both[name][kind] for name in order]
    return tuple(outs)
```

```python
import math

import jax
import jax.numpy as jnp
from jax import lax
from jax.experimental import pallas as pl
from jax.experimental.pallas import tpu as pltpu

F32 = jnp.float32
BF16 = jnp.bfloat16
MESH = pl.DeviceIdType.MESH

D_MODEL = 1024
EPS = 1e-6
TILE = 128
N_CHIP = 4
N_DEV = 8
VMEM_LIMIT_BYTES = 56 * 1024 * 1024

ADAM_LR = 0.001
ADAM_B1 = 0.9
ADAM_B2 = 0.999
ADAM_EPS = 1e-08
ADAM_WD = 0.01
ADAM_STEP = 10

NT_DIMS = (((1,), (1,)), ((), ()))
TN_DIMS = (((0,), (0,)), ((), ()))


def _params(sem=None):
    return pltpu.CompilerParams(dimension_semantics=sem, vmem_limit_bytes=VMEM_LIMIT_BYTES)


def _call(body, *, name, grid, in_specs, out_specs, out_shape, scratch_shapes, semantics, args, rider=None):
    if not rider:
        res = pl.pallas_call(body, name=name, grid=grid, in_specs=in_specs, out_specs=out_specs, out_shape=out_shape,
                             scratch_shapes=scratch_shapes, compiler_params=_params(semantics))(*args)
        return list(res), []
    plans = [_rider_plan(kind, arrays) for kind, arrays in rider]
    arrays = [a for _, group in rider for a in group]
    nr, n_in, n_out, n_scr = len(arrays), len(in_specs), len(out_specs), len(scratch_shapes)
    first_out, first_scr = n_in + nr, n_in + nr + n_out + nr
    last_step = math.prod(grid) - 1

    def riding(*refs):
        step = 0
        for axis, size in enumerate(grid):
            step = step * size + pl.program_id(axis)
        steps, at, sem_at = [], 0, first_scr + n_scr
        for (kind, group), (_, sems, _) in zip(rider, plans):
            k = len(group)
            steps.append(_rider_steps(kind, refs[n_in + at:n_in + at + k],
                                      refs[first_out + n_out + at:first_out + n_out + at + k],
                                      refs[sem_at:sem_at + len(sems)]))
            at, sem_at = at + k, sem_at + len(sems)
        for send, _, _ in steps:
            pl.when(step == 0)(send)
        for _, forward, _ in steps:
            if forward is not None:
                pl.when(step == last_step)(forward)
        body(*refs[:n_in], *refs[first_out:first_out + n_out], *refs[first_scr:first_scr + n_scr])
        for _, _, finish in steps:
            pl.when(step == last_step)(finish)

    aliases, at = {}, 0
    for (_, group), (_, _, aliased) in zip(rider, plans):
        if aliased:
            aliases.update({n_in + at + a: n_out + at + a for a in range(len(group))})
        at += len(group)
    res = pl.pallas_call(
        riding, name=name, grid=grid, in_specs=list(in_specs) + [ANY] * nr, out_specs=list(out_specs) + [ANY] * nr,
        out_shape=list(out_shape) + [s for shapes, _, _ in plans for s in shapes],
        scratch_shapes=list(scratch_shapes) + [s for _, sems, _ in plans for s in sems],
        input_output_aliases=aliases,
        compiler_params=pltpu.CompilerParams(dimension_semantics=("arbitrary",) * len(grid),
                                             vmem_limit_bytes=VMEM_LIMIT_BYTES, has_side_effects=True),
    )(*args, *arrays)
    rode, at = [], n_out
    for _, group in rider:
        rode.append(list(res[at:at + len(group)]))
        at += len(group)
    return list(res[:n_out]), rode


def _rider_plan(kind, arrays):
    n = len(arrays)
    same = [jax.ShapeDtypeStruct(a.shape, a.dtype) for a in arrays]
    pair = [pltpu.SemaphoreType.DMA((n,))] * 2
    if kind == "gather":
        return same, _gather_sems(n), True
    if kind == "exchange":
        return _exchange_shapes(arrays), _exchange_sems(n), False
    if kind == "swap":
        return _swap_shapes(arrays), pair, False
    if kind == "allgather":
        return ([jax.ShapeDtypeStruct((N_DEV,) + a.shape, a.dtype) for a in arrays],
                [pltpu.SemaphoreType.DMA((7 * n,))] * 2 + [pltpu.SemaphoreType.DMA((n,))], False)
    assert kind == "join"
    return same, pair, True


def _rider_steps(kind, ins, outs, sems):
    if kind == "gather":
        return _gather_steps(outs, *sems)
    if kind == "allgather":
        return _allgather_steps(ins, outs, *sems)
    if kind == "exchange":
        send, finish = _exchange_steps(ins, outs, *sems)
    elif kind == "swap":
        send, finish = _swap_steps(ins, outs, *sems)
    else:
        send, finish = _join_steps(outs, *sems)
    return send, None, finish


def _row_tile(k_dim):
    return 1024 if k_dim <= 1024 else 512


def _mm_nn(a, b4, layer, *, out_dtype, name, epilogue=None, extra=None, norm_g=None, rider=None):
    m, k_dim = a.shape
    _, s_dim, kb, n = b4.shape
    assert kb == k_dim
    tm, tn = _row_tile(k_dim), min(n, 1024)
    assert m % tm == 0 and n % tn == 0
    npb = n // tn
    grid = (m // tm, s_dim * npb)
    n_in = 2 + (extra is not None) + (norm_g is not None)
    two_outputs = norm_g is not None or epilogue == "relu2"
    assert norm_g is None or (tn == s_dim * n and epilogue != "relu2")

    def body(*refs):
        a_ref, b_ref = refs[:2]
        e_ref = refs[2] if extra is not None else None
        g_ref = refs[n_in - 1] if norm_g is not None else None
        o_ref = refs[n_in]
        acc = jnp.dot(a_ref[...], b_ref[...], preferred_element_type=F32)
        if epilogue == "relu2":
            r = jnp.maximum(acc, 0.0)
            refs[n_in + 1][...] = r.astype(BF16)
            acc = r * r
        elif epilogue == "residual":
            acc = acc + e_ref[...]
        o_ref[...] = acc.astype(out_dtype)
        if norm_g is not None:
            rstd = lax.rsqrt(jnp.mean(acc * acc, axis=-1, keepdims=True) + EPS)
            refs[n_in + 1][...] = (acc * rstd * g_ref[...]).astype(BF16)

    in_specs = [
        pl.BlockSpec((tm, k_dim), lambda i, j: (i, 0)),
        pl.BlockSpec((None, None, k_dim, tn), lambda i, j: (layer, j // npb, 0, j % npb)),
    ]
    args = [a, b4]
    if extra is not None:
        in_specs.append(pl.BlockSpec((tm, tn), lambda i, j: (i, j)))
        args.append(extra)
    out_block = pl.BlockSpec((tm, tn), lambda i, j: (i, j))
    out_specs, out_shape = [out_block], [jax.ShapeDtypeStruct((m, s_dim * n), out_dtype)]
    if norm_g is not None:
        in_specs.append(pl.BlockSpec((1, tn), lambda i, j: (0, j)))
        args.append(norm_g)
    if two_outputs:
        out_specs.append(out_block)
        out_shape.append(jax.ShapeDtypeStruct((m, s_dim * n), BF16))
    res, rode = _call(
        body, name=name, grid=grid, in_specs=in_specs, out_specs=out_specs, out_shape=out_shape,
        scratch_shapes=[], semantics=("parallel", "parallel"), args=args, rider=rider)
    res = res if two_outputs else res[0]
    return res if rider is None else (res, rode)


def _mm_nt(a, b4, layer, *, out_dtype, name, epilogue=None, extra=None, rider=None):
    m, k_dim = a.shape
    _, s_dim, n_out, n = b4.shape
    assert k_dim == s_dim * n
    tm, tn = _row_tile(k_dim), min(n_out, 1024)
    assert m % tm == 0 and n_out % tn == 0
    grid = (m // tm, n_out // tn)
    rms = epilogue == "rms_bwd"
    assert not rms or tn == n_out
    extras = [] if extra is None else (list(extra) if rms else [extra])
    n_in = 2 + len(extras)
    n_res = 3 if rms else 1

    def body(*refs):
        a_ref, b_ref = refs[:2]
        e_refs = refs[2:n_in]
        o_ref = refs[n_in]
        acc = lax.dot_general(a_ref[:, 0:n], b_ref[0], NT_DIMS, preferred_element_type=F32)
        for s in range(1, s_dim):
            acc = acc + lax.dot_general(a_ref[:, s * n:(s + 1) * n], b_ref[s], NT_DIMS, preferred_element_type=F32)
        if epilogue == "relu2_bwd":
            acc = acc * (2.0 * e_refs[0][...].astype(F32))
        if not rms:
            o_ref[...] = acc.astype(out_dtype)
        else:
            h_ref, g_ref, dres_ref = e_refs
            dhb_ref, dg_ref = refs[n_in + 1:n_in + 3]
            hv = h_ref[...]
            rstd = lax.rsqrt(jnp.mean(hv * hv, axis=-1, keepdims=True) + EPS)
            xhat = hv * rstd
            dxhat = acc * g_ref[...]
            dh = dres_ref[...] + rstd * (dxhat - xhat * jnp.mean(dxhat * xhat, axis=-1, keepdims=True))
            o_ref[...] = dh
            dhb_ref[...] = dh.astype(BF16)
            dg_part = jnp.sum(acc * xhat, axis=0, keepdims=True)
            first = pl.program_id(0) == 0

            @pl.when(first)
            def _():
                dg_ref[...] = dg_part

            @pl.when(jnp.logical_not(first))
            def _():
                dg_ref[...] += dg_part

    in_specs = [
        pl.BlockSpec((tm, k_dim), lambda i, j: (i, 0)),
        pl.BlockSpec((None, s_dim, tn, n), lambda i, j: (layer, 0, j, 0)),
    ]
    args = [a, b4] + extras
    block = pl.BlockSpec((tm, tn), lambda i, j: (i, j))
    vec = pl.BlockSpec((1, tn), lambda i, j: (0, j))
    if rms:
        in_specs += [block, vec, block]
        out_specs = [block, block, vec]
        out_shape = [jax.ShapeDtypeStruct((m, n_out), F32), jax.ShapeDtypeStruct((m, n_out), BF16),
                     jax.ShapeDtypeStruct((1, n_out), F32)]
    else:
        in_specs += [block] * len(extras)
        out_specs, out_shape = [block], [jax.ShapeDtypeStruct((m, n_out), out_dtype)]
    res, rode = _call(
        body, name=name, grid=grid, in_specs=in_specs, out_specs=out_specs, out_shape=out_shape,
        scratch_shapes=[], semantics=("arbitrary",) * 2 if rms else ("parallel", "parallel"), args=args, rider=rider)
    res = res if rms else res[0]
    return res if rider is None else (res, rode)


def _mm_tn(a, b, s_dim, *, name, rider=None):
    m, k1 = a.shape
    mb, n_all = b.shape
    assert mb == m and n_all % s_dim == 0
    n = n_all // s_dim
    tn, t1 = min(n, 1024), _row_tile(m)
    assert k1 % t1 == 0 and n % tn == 0
    npb = n // tn
    grid = (k1 // t1, s_dim * npb)

    def body(a_ref, b_ref, o_ref):
        o_ref[...] = lax.dot_general(a_ref[...], b_ref[...], TN_DIMS, preferred_element_type=F32).astype(BF16)

    res, rode = _call(
        body, name=name, grid=grid,
        in_specs=[pl.BlockSpec((m, t1), lambda i, j: (0, i)), pl.BlockSpec((m, tn), lambda i, j: (0, j))],
        out_specs=[pl.BlockSpec((None, None, t1, tn), lambda i, j: (0, j // npb, i, j % npb))],
        out_shape=[jax.ShapeDtypeStruct((1, s_dim, k1, n), BF16)],
        scratch_shapes=[], semantics=("parallel", "parallel"), args=[a, b], rider=rider)
    return res[0] if rider is None else (res[0], rode)


ROW_TILE = 512


def _rms_fwd(h, g, *, name, rider=None):
    m, d = h.shape

    def body(h_ref, g_ref, o_ref):
        hv = h_ref[...]
        rstd = lax.rsqrt(jnp.mean(hv * hv, axis=-1, keepdims=True) + EPS)
        o_ref[...] = (hv * rstd * g_ref[...]).astype(BF16)

    res, rode = _call(
        body, name=name, grid=(m // ROW_TILE,),
        in_specs=[pl.BlockSpec((ROW_TILE, d), lambda i: (i, 0)), pl.BlockSpec((1, d), lambda i: (0, 0))],
        out_specs=[pl.BlockSpec((ROW_TILE, d), lambda i: (i, 0))], out_shape=[jax.ShapeDtypeStruct((m, d), BF16)],
        scratch_shapes=[], semantics=("parallel",), args=[h, g], rider=rider)
    return res[0] if rider is None else (res[0], rode)


def _mlp_down_loss(act, w_2, h_res, g, target):
    m, k_dim = act.shape
    d = w_2.shape[-1]
    tm = _row_tile(k_dim)

    def body(a_ref, b_ref, r_ref, g_ref, t_ref, dh_ref, dhb_ref, dg_ref, loss_ref):
        hv = jnp.dot(a_ref[...], b_ref[...], preferred_element_type=F32) + r_ref[...]
        gv = g_ref[...]
        rstd = lax.rsqrt(jnp.mean(hv * hv, axis=-1, keepdims=True) + EPS)
        xhat = hv * rstd
        err = xhat * gv - t_ref[...]
        dy = err * (1.0 / d)
        dxhat = dy * gv
        dh = rstd * (dxhat - xhat * jnp.mean(dxhat * xhat, axis=-1, keepdims=True))
        dh_ref[...] = dh
        dhb_ref[...] = dh.astype(BF16)
        dg_part = jnp.sum(dy * xhat, axis=0, keepdims=True)
        sq = jnp.sum(jnp.sum(err * err, axis=1, keepdims=True), axis=0, keepdims=True) * (0.5 / d)
        loss_part = jnp.broadcast_to(sq, (8, TILE))

        @pl.when(pl.program_id(0) == 0)
        def _():
            dg_ref[...] = dg_part
            loss_ref[...] = loss_part

        @pl.when(pl.program_id(0) > 0)
        def _():
            dg_ref[...] += dg_part
            loss_ref[...] += loss_part

    row = pl.BlockSpec((tm, d), lambda i: (i, 0))
    vec = pl.BlockSpec((1, d), lambda i: (0, 0))
    return pl.pallas_call(
        body, name="mlp1_down_loss", grid=(m // tm,),
        in_specs=[pl.BlockSpec((tm, k_dim), lambda i: (i, 0)),
                  pl.BlockSpec((None, None, k_dim, d), lambda i: (0, 0, 0, 0)), row, vec, row],
        out_specs=[row, row, vec, pl.BlockSpec((8, TILE), lambda i: (0, 0))],
        out_shape=[jax.ShapeDtypeStruct((m, d), F32), jax.ShapeDtypeStruct((m, d), BF16),
                   jax.ShapeDtypeStruct((1, d), F32), jax.ShapeDtypeStruct((8, TILE), F32)],
        compiler_params=_params(("arbitrary",)),
    )(act, w_2, h_res, g, target)


def _shift_down(x, s, t_idx):
    return jnp.where(t_idx >= s, pltpu.roll(x, s, 0), 0.0)


def _shift_up(x, s, t_idx, t_len):
    return jnp.where(t_idx < t_len - s, pltpu.roll(x, t_len - s, 0), 0.0)


def _pool_select(group, s2, s4, s8, s16):
    return jnp.where(group == 0, s2, jnp.where(group == 1, s4, jnp.where(group == 2, s8, s16)))


def _pool_count(group, t_idx):
    win = jnp.left_shift(2, group)
    return jnp.minimum(t_idx + 1, win).astype(F32)


def _pool_fwd_math(a, group, t_idx):
    s2 = a + _shift_down(a, 1, t_idx)
    s4 = s2 + _shift_down(s2, 2, t_idx)
    s8 = s4 + _shift_down(s4, 4, t_idx)
    s16 = s8 + _shift_down(s8, 8, t_idx)
    return _pool_select(group, s2, s4, s8, s16) / _pool_count(group, t_idx) - a


def _pool_bwd_math(dpooled, group, t_idx, t_len):
    e = dpooled / _pool_count(group, t_idx)
    s2 = e + _shift_up(e, 1, t_idx, t_len)
    s4 = s2 + _shift_up(s2, 2, t_idx, t_len)
    s8 = s4 + _shift_up(s4, 4, t_idx, t_len)
    s16 = s8 + _shift_up(s8, 8, t_idx, t_len)
    return _pool_select(group, s2, s4, s8, s16) - dpooled


def _conv_fwd_math(c, w_ref, b_ref, t_idx):
    return (w_ref[0:1, :] * _shift_down(c, 2, t_idx) + w_ref[1:2, :] * _shift_down(c, 1, t_idx)
            + w_ref[2:3, :] * c + b_ref[...])


def _ab_fwd(p, pool_w, pool_scale, conv_w, conv_b, nseq, t_len, rider=None):
    m = p.shape[0]
    ng = 4

    def body(a_ref, xb_ref, gb_ref, gc_ref, pw_ref, ps_ref, cw_ref, cb_ref, o_ref):
        j = pl.program_id(1)
        t_idx = lax.broadcasted_iota(jnp.int32, (t_len, TILE), 0)

        @pl.when(j < ng)
        def _():
            pooled = _pool_fwd_math(a_ref[...].astype(F32), j, t_idx)
            mixed = jnp.dot(pooled.astype(BF16), pw_ref[...].astype(BF16), preferred_element_type=F32)
            o_ref[...] = (mixed * ps_ref[...]).astype(BF16)

        @pl.when(j >= ng)
        def _():
            c = gc_ref[...].astype(F32) * xb_ref[...].astype(F32)
            y = _conv_fwd_math(c, cw_ref, cb_ref, t_idx)
            o_ref[...] = (gb_ref[...].astype(F32) * y).astype(BF16)

    def pool_j(j):
        return jnp.minimum(j, ng - 1)

    def conv_j(j):
        return jnp.maximum(j - ng, 0)

    in_specs = [
        pl.BlockSpec((t_len, TILE), lambda s, j: (s, pool_j(j))),
        pl.BlockSpec((t_len, TILE), lambda s, j: (s, ng + conv_j(j))),
        pl.BlockSpec((t_len, TILE), lambda s, j: (s, 2 * ng + conv_j(j))),
        pl.BlockSpec((t_len, TILE), lambda s, j: (s, 3 * ng + conv_j(j))),
        pl.BlockSpec((None, TILE, TILE), lambda s, j: (pool_j(j), 0, 0)),
        pl.BlockSpec((None, 1, TILE), lambda s, j: (pool_j(j), 0, 0)),
        pl.BlockSpec((3, TILE), lambda s, j: (0, conv_j(j))),
        pl.BlockSpec((1, TILE), lambda s, j: (0, conv_j(j))),
    ]
    res, rode = _call(
        body, name="ab_mixer_fwd", grid=(nseq, 2 * ng), in_specs=in_specs,
        out_specs=[pl.BlockSpec((t_len, TILE), lambda s, j: (s, j))],
        out_shape=[jax.ShapeDtypeStruct((m, 2 * ng * TILE), BF16)], scratch_shapes=[],
        semantics=("parallel", "arbitrary"), args=[p, p, p, p, pool_w, pool_scale, conv_w, conv_b], rider=rider)
    return res[0] if rider is None else (res[0], rode)


def _ab_bwd(p, dmix, pool_w, pool_scale, conv_w, conv_b, nseq, t_len, rider=None):
    m = p.shape[0]
    ng = 4

    def body(a_ref, xb_ref, gb_ref, gc_ref, dma_ref, dmb_ref, pw_ref, ps_ref, cw_ref, cb_ref,
             da_ref, dxb_ref, dgb_ref, dgc_ref, dpw_ref, dps_ref, dcw_ref, dcb_ref):
        j = pl.program_id(0)
        first = pl.program_id(1) == 0
        t_idx = lax.broadcasted_iota(jnp.int32, (t_len, TILE), 0)

        pooled = _pool_fwd_math(a_ref[...].astype(F32), j, t_idx).astype(BF16)
        w_bf = pw_ref[...].astype(BF16)
        mixed = jnp.dot(pooled, w_bf, preferred_element_type=F32)
        dm = dma_ref[...].astype(F32)
        dps = jnp.sum(dm * mixed, axis=0, keepdims=True)
        dmixed = (dm * ps_ref[...]).astype(BF16)
        dpw = lax.dot_general(pooled, dmixed, TN_DIMS, preferred_element_type=F32)
        dpooled = lax.dot_general(dmixed, w_bf, NT_DIMS, preferred_element_type=F32)
        da_ref[...] = _pool_bwd_math(dpooled, j, t_idx, t_len).astype(BF16)

        xb = xb_ref[...].astype(F32)
        gb = gb_ref[...].astype(F32)
        gc = gc_ref[...].astype(F32)
        d = dmb_ref[...].astype(F32)
        c = gc * xb
        c1 = _shift_down(c, 1, t_idx)
        c2 = _shift_down(c, 2, t_idx)
        y = cw_ref[0:1, :] * c2 + cw_ref[1:2, :] * c1 + cw_ref[2:3, :] * c + cb_ref[...]
        dgb_ref[...] = (d * y).astype(BF16)
        dy = d * gb
        dc = (cw_ref[2:3, :] * dy + cw_ref[1:2, :] * _shift_up(dy, 1, t_idx, t_len)
              + cw_ref[0:1, :] * _shift_up(dy, 2, t_idx, t_len))
        dgc_ref[...] = (dc * xb).astype(BF16)
        dxb_ref[...] = (dc * gc).astype(BF16)
        dcw = jnp.concatenate([jnp.sum(dy * c2, axis=0, keepdims=True),
                               jnp.sum(dy * c1, axis=0, keepdims=True),
                               jnp.sum(dy * c, axis=0, keepdims=True)], axis=0)
        dcb = jnp.sum(dy, axis=0, keepdims=True)

        @pl.when(first)
        def _():
            dpw_ref[...] = dpw
            dps_ref[...] = dps
            dcw_ref[...] = dcw
            dcb_ref[...] = dcb

        @pl.when(jnp.logical_not(first))
        def _():
            dpw_ref[...] += dpw
            dps_ref[...] += dps
            dcw_ref[...] += dcw
            dcb_ref[...] += dcb

    def col(k):
        return pl.BlockSpec((t_len, TILE), lambda j, s: (s, k * ng + j))

    in_specs = [
        col(0), col(1), col(2), col(3), col(0), col(1),
        pl.BlockSpec((None, TILE, TILE), lambda j, s: (j, 0, 0)),
        pl.BlockSpec((None, 1, TILE), lambda j, s: (j, 0, 0)),
        pl.BlockSpec((3, TILE), lambda j, s: (0, j)),
        pl.BlockSpec((1, TILE), lambda j, s: (0, j)),
    ]
    piece = pl.BlockSpec((t_len, TILE), lambda j, s: (s, j))
    out_specs = [
        piece, piece, piece, piece,
        pl.BlockSpec((None, TILE, TILE), lambda j, s: (j, 0, 0)),
        pl.BlockSpec((None, 1, TILE), lambda j, s: (j, 0, 0)),
        pl.BlockSpec((3, TILE), lambda j, s: (0, j)),
        pl.BlockSpec((1, TILE), lambda j, s: (0, j)),
    ]
    w = ng * TILE
    out_shape = [jax.ShapeDtypeStruct((m, w), BF16)] * 4 + [
        jax.ShapeDtypeStruct((ng, TILE, TILE), F32), jax.ShapeDtypeStruct((ng, 1, TILE), F32),
        jax.ShapeDtypeStruct((3, w), F32), jax.ShapeDtypeStruct((1, w), F32)]
    res, rode = _call(
        body, name="ab_mixer_bwd", grid=(ng, nseq), in_specs=in_specs, out_specs=out_specs, out_shape=out_shape,
        scratch_shapes=[], semantics=("parallel", "arbitrary"),
        args=[p, p, p, p, dmix, dmix, pool_w, pool_scale, conv_w, conv_b], rider=rider)
    return res if rider is None else (res, rode)


SGU_ROWS = 512
INV_SQRT2 = 1.0 / math.sqrt(2.0)
INV_SQRT_2PI = 1.0 / math.sqrt(2.0 * math.pi)


def _gelu(x):
    return 0.5 * x * (1.0 + lax.erf(x * INV_SQRT2))


def _gelu_grad(x):
    return 0.5 * (1.0 + lax.erf(x * INV_SQRT2)) + x * (INV_SQRT_2PI * jnp.exp(-0.5 * x * x))


def _causal_tile(transposed=False):
    r = lax.broadcasted_iota(jnp.int32, (TILE, TILE), 0)
    c = lax.broadcasted_iota(jnp.int32, (TILE, TILE), 1)
    return r <= c if transposed else c <= r


def _sgu_norm(v, g_ref, b_ref):
    mu = jnp.mean(v, axis=-1, keepdims=True)
    xc = v - mu
    rstd = lax.rsqrt(jnp.mean(xc * xc, axis=-1, keepdims=True) + EPS)
    xhat = xc * rstd
    return xhat, rstd, xhat * g_ref[...] + b_ref[...]


def _sgu_fwd(p, norm_g, norm_b, w_s, bias_tile):
    m = p.shape[0]
    ng = 4
    width = ng * TILE

    def body(u_ref, v_ref, g_ref, b_ref, w_ref, bias_ref, o_ref):
        u = _gelu(u_ref[...].astype(F32))
        _, _, vln = _sgu_norm(_gelu(v_ref[...].astype(F32)), g_ref, b_ref)
        vln = vln.astype(BF16)
        causal = _causal_tile()
        for g in range(ng):
            cols = slice(g * TILE, (g + 1) * TILE)
            wg = jnp.where(causal, w_ref[g], 0.0).astype(BF16)
            for n in range(SGU_ROWS // TILE):
                rows = slice(n * TILE, (n + 1) * TILE)
                s = jnp.dot(wg, vln[rows, cols], preferred_element_type=F32) + bias_ref[g]
                o_ref[rows, cols] = (u[rows, cols] * s).astype(BF16)

    vec = pl.BlockSpec((1, width), lambda i: (0, 0))
    tiles = pl.BlockSpec((ng, TILE, TILE), lambda i: (0, 0, 0))
    return pl.pallas_call(
        body, name="sgu_fwd", grid=(m // SGU_ROWS,),
        in_specs=[pl.BlockSpec((SGU_ROWS, width), lambda i: (i, 0)),
                  pl.BlockSpec((SGU_ROWS, width), lambda i: (i, 1)), vec, vec, tiles, tiles],
        out_specs=pl.BlockSpec((SGU_ROWS, width), lambda i: (i, 0)),
        out_shape=jax.ShapeDtypeStruct((m, width), BF16),
        compiler_params=_params(("parallel",)),
    )(p, p, norm_g, norm_b, w_s, bias_tile)


def _sgu_bwd(p, dmix, norm_g, norm_b, w_s, w_s_t, bias_tile):
    m = p.shape[0]
    ng = 4
    width = ng * TILE

    def body(u_ref, v_ref, dc_ref, g_ref, b_ref, w_ref, wt_ref, bias_ref,
             du_ref, dv_ref, dw_ref, dbs_ref, dg_ref, db_ref, ds_scr, dvln_scr):
        u_pre = u_ref[...].astype(F32)
        v_pre = v_ref[...].astype(F32)
        u = _gelu(u_pre)
        xhat, rstd, vln = _sgu_norm(_gelu(v_pre), g_ref, b_ref)
        vln = vln.astype(BF16)
        dc = dc_ref[...].astype(F32)
        causal = _causal_tile()
        ones = jnp.ones((TILE, TILE), BF16)
        first = pl.program_id(0) == 0
        for g in range(ng):
            cols = slice(g * TILE, (g + 1) * TILE)
            wg = jnp.where(causal, w_ref[g], 0.0).astype(BF16)
            wgt = jnp.where(_causal_tile(transposed=True), wt_ref[g], 0.0).astype(BF16)
            dw_acc = jnp.zeros((TILE, TILE), F32)
            dbs_acc = jnp.zeros((TILE, TILE), F32)
            for n in range(SGU_ROWS // TILE):
                rows = slice(n * TILE, (n + 1) * TILE)
                vt = vln[rows, cols]
                s = jnp.dot(wg, vt, preferred_element_type=F32) + bias_ref[g]
                ds_scr[rows, cols] = dc[rows, cols] * s
                ds = (dc[rows, cols] * u[rows, cols]).astype(BF16)
                dw_acc += lax.dot_general(ds, vt, NT_DIMS, preferred_element_type=F32)
                dbs_acc += jnp.dot(ds, ones, preferred_element_type=F32)
                dvln_scr[rows, cols] = jnp.dot(wgt, ds, preferred_element_type=F32)
            dw_g = jnp.where(causal, dw_acc, 0.0)

            @pl.when(first)
            def _():
                dw_ref[g] = dw_g
                dbs_ref[g] = dbs_acc

            @pl.when(jnp.logical_not(first))
            def _():
                dw_ref[g] += dw_g
                dbs_ref[g] += dbs_acc

        du_ref[...] = (ds_scr[...] * _gelu_grad(u_pre)).astype(BF16)
        dvln = dvln_scr[...]
        dxhat = dvln * g_ref[...]
        dv = rstd * (dxhat - jnp.mean(dxhat, axis=-1, keepdims=True)
                     - xhat * jnp.mean(dxhat * xhat, axis=-1, keepdims=True))
        dv_ref[...] = (dv * _gelu_grad(v_pre)).astype(BF16)
        dg_part = jnp.sum(dvln * xhat, axis=0, keepdims=True)
        db_part = jnp.sum(dvln, axis=0, keepdims=True)

        @pl.when(first)
        def _():
            dg_ref[...] = dg_part
            db_ref[...] = db_part

        @pl.when(jnp.logical_not(first))
        def _():
            dg_ref[...] += dg_part
            db_ref[...] += db_part

    vec = pl.BlockSpec((1, width), lambda i: (0, 0))
    tiles = pl.BlockSpec((ng, TILE, TILE), lambda i: (0, 0, 0))
    rows0 = pl.BlockSpec((SGU_ROWS, width), lambda i: (i, 0))
    rows1 = pl.BlockSpec((SGU_ROWS, width), lambda i: (i, 1))
    return pl.pallas_call(
        body, name="sgu_bwd", grid=(m // SGU_ROWS,),
        in_specs=[rows0, rows1, rows0, vec, vec, tiles, tiles, tiles],
        out_specs=[rows0, rows0, tiles, tiles, vec, vec],
        out_shape=[jax.ShapeDtypeStruct((m, width), BF16), jax.ShapeDtypeStruct((m, width), BF16),
                   jax.ShapeDtypeStruct((ng, TILE, TILE), F32), jax.ShapeDtypeStruct((ng, TILE, TILE), F32),
                   jax.ShapeDtypeStruct((1, width), F32), jax.ShapeDtypeStruct((1, width), F32)],
        scratch_shapes=[pltpu.VMEM((SGU_ROWS, width), F32), pltpu.VMEM((SGU_ROWS, width), F32)],
        compiler_params=_params(("arbitrary",)),
    )(p, p, dmix, norm_g, norm_b, w_s, w_s_t, bias_tile)


SB_DH = 64
SB_SCALE = 1.0 / math.sqrt(SB_DH)


SB_BLOCK = 256
SB_SUB = SB_BLOCK // TILE


def _sum_matrix(kind):
    j = lax.broadcasted_iota(jnp.int32, (TILE, 2 * TILE), 0)
    s = lax.broadcasted_iota(jnp.int32, (TILE, 2 * TILE), 1)
    tri = {"after": j > s, "upto": j <= s, "before": j < s}[kind]
    return jnp.where(jnp.logical_or(s >= TILE, tri), 1.0, 0.0).astype(BF16)


def _strict_mask():
    r = lax.broadcasted_iota(jnp.int32, (SB_BLOCK, SB_BLOCK), 0)
    c = lax.broadcasted_iota(jnp.int32, (SB_BLOCK, SB_BLOCK), 1)
    return c < r


def _head_lanes(h):
    lane = lax.broadcasted_iota(jnp.int32, (1, TILE), 1)
    return (lane >= h * SB_DH) & (lane < (h + 1) * SB_DH)


def _softplus(z):
    return jnp.maximum(z, 0.0) + jnp.log(1.0 + jnp.exp(-jnp.abs(z)))


def _sb_fwd(p, nseq, t_len, gather):
    m = p.shape[0]
    npair = 4
    ng = len(gather)
    last_step = nseq * npair - 1

    def body(q_ref, k_ref, v_ref, *rest):
        o_ref, lt_ref = rest[ng:ng + 2]
        kh_ref, vh_ref = rest[2 * ng + 2:2 * ng + 4]
        step = pl.program_id(0) * npair + pl.program_id(1)
        send, forward, finish = _gather_steps(rest[ng + 2:2 * ng + 2], *rest[2 * ng + 4:])
        pl.when(step == 0)(send)
        pl.when(step == (last_step + 1) // 2)(forward)
        for h in range(2):
            keep = _head_lanes(h)
            kh_ref[h] = jnp.where(keep, k_ref[...], 0).astype(BF16)
            vh_ref[h] = jnp.where(keep, v_ref[...], 0).astype(BF16)
        summat = _sum_matrix("after")
        strict = _strict_mask()

        def one_pass(q, row0, diag, state):
            rows = pl.ds(row0, SB_BLOCK)
            z, sp, pieces = [], [], []
            for h in range(2):
                zh = lax.dot_general(q, kh_ref[h, rows, :], NT_DIMS, preferred_element_type=F32)
                sph = _softplus(zh)
                logkeep = jnp.where(strict, -sph, 0.0) if diag else -sph
                z.append(zh)
                sp.append(sph)
                pieces += [logkeep[:, b * TILE:(b + 1) * TILE] for b in range(SB_SUB)]
            sums = jnp.dot(jnp.concatenate(pieces, axis=0).astype(BF16), summat, preferred_element_type=F32)
            out = []
            for h in range(2):
                carry, acc = state[2 * h], state[2 * h + 1]
                after = [None] * SB_SUB
                for b in reversed(range(SB_SUB)):
                    part = sums[(h * SB_SUB + b) * SB_BLOCK:(h * SB_SUB + b + 1) * SB_BLOCK]
                    after[b] = part[:, :TILE] + carry
                    carry = carry + part[:, TILE:]
                w = jnp.exp(z[h] - sp[h] + jnp.concatenate(after, axis=1))
                if diag:
                    w = jnp.where(strict, w, 0.0)
                out += [carry, acc + jnp.dot(w.astype(BF16), vh_ref[h, rows, :], preferred_element_type=F32)]
            return tuple(out)

        def q_block(i, _):
            r0 = pl.multiple_of(i * SB_BLOCK, SB_BLOCK)
            q = q_ref[pl.ds(r0, SB_BLOCK), :] * SB_SCALE
            zero = jnp.zeros((SB_BLOCK, TILE), F32)
            state = one_pass(q, r0, True, (zero,) * 4)
            state = lax.fori_loop(
                0, i, lambda jj, st: one_pass(q, pl.multiple_of((i - 1 - jj) * SB_BLOCK, SB_BLOCK), False, st), state)
            o_ref[pl.ds(r0, SB_BLOCK), :] = (state[1] + state[3]).astype(BF16)
            lt_ref[pl.ds(r0, SB_BLOCK), :] = jnp.where(_head_lanes(0), state[0], state[2])
            return 0

        lax.fori_loop(0, t_len // SB_BLOCK, q_block, 0)
        pl.when(step == last_step)(finish)

    def col(k):
        return pl.BlockSpec((t_len, TILE), lambda s, hp: (s, k * npair + hp))

    out = pl.BlockSpec((t_len, TILE), lambda s, hp: (s, hp))
    res = pl.pallas_call(
        body, name="stickbreak_fwd", grid=(nseq, npair), in_specs=[col(2), col(3), col(4)] + [ANY] * ng,
        out_specs=[out, out] + [ANY] * ng,
        out_shape=[jax.ShapeDtypeStruct((m, npair * TILE), BF16), jax.ShapeDtypeStruct((m, npair * TILE), F32)]
        + [jax.ShapeDtypeStruct(b.shape, b.dtype) for b in gather],
        input_output_aliases={3 + a: 2 + a for a in range(ng)},
        scratch_shapes=[pltpu.VMEM((2, t_len, TILE), BF16), pltpu.VMEM((2, t_len, TILE), BF16)] + _gather_sems(ng),
        compiler_params=pltpu.CompilerParams(dimension_semantics=("arbitrary", "arbitrary"),
                                             vmem_limit_bytes=VMEM_LIMIT_BYTES, has_side_effects=True),
    )(p, p, p, *gather)
    return res[0], res[1], res[2:]


def _sb_bwd(p, dmix, ltot, nseq, t_len, exchange):
    m = p.shape[0]
    npair = 4
    ne = len(exchange)
    last_step = nseq * npair - 1

    def body(q_ref, k_ref, v_ref, do_ref, lt_ref, *rest):
        dq_ref, dk_ref, dv_ref = rest[ne:ne + 3]
        kh_ref, vh_ref, dk_acc, dv_acc = rest[2 * ne + 3:2 * ne + 7]
        step = pl.program_id(0) * npair + pl.program_id(1)
        send, finish = _exchange_steps(rest[:ne], rest[ne + 3:2 * ne + 3], *rest[2 * ne + 7:])
        pl.when(step == 0)(send)
        for h in range(2):
            keep = _head_lanes(h)
            kh_ref[h] = jnp.where(keep, k_ref[...], 0).astype(BF16)
            vh_ref[h] = jnp.where(keep, v_ref[...], 0).astype(BF16)
        dk_acc[...] = jnp.zeros_like(dk_acc)
        dv_acc[...] = jnp.zeros_like(dv_acc)
        sum_upto = _sum_matrix("upto")
        sum_before = _sum_matrix("before")
        strict = _strict_mask()
        lane = lax.broadcasted_iota(jnp.int32, (SB_BLOCK, TILE), 1)

        def running(x, matrix, start):
            pieces = [x[h][:, b * TILE:(b + 1) * TILE] for h in range(2) for b in range(SB_SUB)]
            sums = jnp.dot(jnp.concatenate(pieces, axis=0).astype(BF16), matrix, preferred_element_type=F32)
            wide, ends = [], []
            for h in range(2):
                total, cols = start[h], []
                for b in range(SB_SUB):
                    part = sums[(h * SB_SUB + b) * SB_BLOCK:(h * SB_SUB + b + 1) * SB_BLOCK]
                    cols.append(part[:, :TILE] + total)
                    total = total + part[:, TILE:]
                wide.append(jnp.concatenate(cols, axis=1))
                ends.append(total)
            return wide, ends

        def one_pass(q, do, qh, doh, ltot, row0, diag, state):
            rows = pl.ds(row0, SB_BLOCK)
            z, sp, logkeep = [], [], []
            for h in range(2):
                zh = lax.dot_general(q, kh_ref[h, rows, :], NT_DIMS, preferred_element_type=F32)
                sph = _softplus(zh)
                z.append(zh)
                sp.append(sph)
                logkeep.append(jnp.where(strict, -sph, 0.0) if diag else -sph)
            upto, sum_l = running(logkeep, sum_upto, [state[0], state[3]])
            w, g = [], []
            for h in range(2):
                wh = jnp.exp(z[h] - sp[h] + (ltot[h] - upto[h]))
                if diag:
                    wh = jnp.where(strict, wh, 0.0)
                w.append(wh)
                g.append(wh * lax.dot_general(do, vh_ref[h, rows, :], NT_DIMS, preferred_element_type=F32))
            g_before, sum_g = running(g, sum_before, [state[1], state[4]])
            out, dk_new, dv_new = [], 0.0, 0.0
            for h in range(2):
                dz = g[h] - jnp.exp(z[h] - sp[h]) * (g[h] + g_before[h])
                if diag:
                    dz = jnp.where(strict, dz, 0.0)
                dzb = dz.astype(BF16)
                dq = state[3 * h + 2] + jnp.dot(dzb, kh_ref[h, rows, :], preferred_element_type=F32)
                dk_new = dk_new + lax.dot_general(dzb, qh[h], TN_DIMS, preferred_element_type=F32)
                dv_new = dv_new + lax.dot_general(w[h].astype(BF16), doh[h], TN_DIMS, preferred_element_type=F32)
                out += [sum_l[h], sum_g[h], dq]
            dk_acc[rows, :] += dk_new
            dv_acc[rows, :] += dv_new
            return tuple(out)

        def q_block(i, _):
            r0 = pl.multiple_of(i * SB_BLOCK, SB_BLOCK)
            q = q_ref[pl.ds(r0, SB_BLOCK), :] * SB_SCALE
            do = do_ref[pl.ds(r0, SB_BLOCK), :]
            lt = lt_ref[pl.ds(r0, SB_BLOCK), :]
            qh, doh, ltot = [], [], []
            for h in range(2):
                keep = _head_lanes(h)
                qh.append(jnp.where(keep, q, 0).astype(BF16))
                doh.append(jnp.where(keep, do, 0).astype(BF16))
                ltot.append(jnp.sum(jnp.where(lane == h * SB_DH, lt, 0.0), axis=1, keepdims=True))
            zero = jnp.zeros((SB_BLOCK, TILE), F32)
            state = lax.fori_loop(
                0, i,
                lambda jj, st: one_pass(q, do, qh, doh, ltot, pl.multiple_of(jj * SB_BLOCK, SB_BLOCK), False, st),
                (zero,) * 6)
            state = one_pass(q, do, qh, doh, ltot, r0, True, state)
            dq_ref[pl.ds(r0, SB_BLOCK), :] = ((state[2] + state[5]) * SB_SCALE).astype(BF16)
            return 0

        lax.fori_loop(0, t_len // SB_BLOCK, q_block, 0)
        dk_ref[...] = dk_acc[...].astype(BF16)
        dv_ref[...] = dv_acc[...].astype(BF16)
        pl.when(step == last_step)(finish)

    def col(k):
        return pl.BlockSpec((t_len, TILE), lambda s, hp: (s, k * npair + hp))

    out = pl.BlockSpec((t_len, TILE), lambda s, hp: (s, hp))
    width = npair * TILE
    res = pl.pallas_call(
        body, name="stickbreak_bwd", grid=(nseq, npair),
        in_specs=[col(2), col(3), col(4), col(1), out] + [ANY] * ne, out_specs=[out, out, out] + [ANY] * ne,
        out_shape=[jax.ShapeDtypeStruct((m, width), BF16)] * 3 + _exchange_shapes(exchange),
        scratch_shapes=[pltpu.VMEM((2, t_len, TILE), BF16), pltpu.VMEM((2, t_len, TILE), BF16),
                        pltpu.VMEM((t_len, TILE), F32), pltpu.VMEM((t_len, TILE), F32)] + _exchange_sems(ne),
        compiler_params=pltpu.CompilerParams(dimension_semantics=("arbitrary", "arbitrary"),
                                             vmem_limit_bytes=VMEM_LIMIT_BYTES, has_side_effects=True),
    )(p, p, p, dmix, ltot, *exchange)
    return res[0], res[1], res[2], res[3:]


def _adam_math(w, g, m, v):
    m = ADAM_B1 * m + (1.0 - ADAM_B1) * g
    v = ADAM_B2 * v + (1.0 - ADAM_B2) * (g * g)
    m_hat = m / (1.0 - ADAM_B1 ** ADAM_STEP)
    v_hat = v / (1.0 - ADAM_B2 ** ADAM_STEP)
    delta = -ADAM_LR * (m_hat / (jnp.sqrt(v_hat) + ADAM_EPS) + ADAM_WD * w)
    return delta, m, v


def _cast_place(w, layer, pos, *, name):
    _, r, c = w.shape
    tr = min(r, 256)

    def body(pos_ref, w_ref, o_ref):
        o_ref[...] = w_ref[...].astype(BF16)

    grid_spec = pltpu.PrefetchScalarGridSpec(
        num_scalar_prefetch=1, grid=(r // tr,),
        in_specs=[pl.BlockSpec((None, tr, c), lambda i, pos_ref: (layer, i, 0))],
        out_specs=pl.BlockSpec((None, None, tr, c), lambda i, pos_ref: (0, pos_ref[0], i, 0)))
    return pl.pallas_call(
        body, name=name, grid_spec=grid_spec, out_shape=jax.ShapeDtypeStruct((1, N_CHIP, r, c), BF16),
        compiler_params=_params(("parallel",)),
    )(pos, w)


def _pair_sum(mine, got, pos, *, name):
    l_dim, s_dim, h, c = got.shape
    th = min(h, 512)
    nt = h // th

    def body(pos_ref, a_ref, b_ref, o_ref):
        o_ref[...] = (a_ref[...].astype(F32) + b_ref[...].astype(F32)).astype(BF16)

    spec = pl.BlockSpec((None, None, th, c), lambda l, s, i, pos_ref: (l, s, i, 0))
    grid_spec = pltpu.PrefetchScalarGridSpec(
        num_scalar_prefetch=1, grid=(l_dim, s_dim, nt),
        in_specs=[pl.BlockSpec((None, None, th, c), lambda l, s, i, pos_ref: (l, s, pos_ref[1] * nt + i, 0)), spec],
        out_specs=spec)
    return pl.pallas_call(
        body, name=name, grid_spec=grid_spec, out_shape=jax.ShapeDtypeStruct(got.shape, BF16),
        compiler_params=_params(("parallel",) * 3),
    )(pos, mine, got)


def _chip_sum(sums, landed, pos, *, name):
    l_dim, _, h, c = sums.shape
    th = min(h, 512)
    nt = h // th

    def body(pos_ref, own, r0, r1, r2, o_ref):
        o_ref[...] = ((own[...].astype(F32) + r0[...].astype(F32)) + r1[...].astype(F32)) + r2[...].astype(F32)

    def piece(k):
        return pl.BlockSpec((None, None, th, c), lambda l, i, pos_ref: (l, k, i, 0))

    grid_spec = pltpu.PrefetchScalarGridSpec(
        num_scalar_prefetch=1, grid=(l_dim, nt),
        in_specs=[pl.BlockSpec((None, None, th, c), lambda l, i, pos_ref: (l, pos_ref[0], i, 0)),
                  piece(0), piece(1), piece(2)],
        out_specs=pl.BlockSpec((None, th, c), lambda l, i, pos_ref: (l, pos_ref[1] * nt + i, 0)))
    return pl.pallas_call(
        body, name=name, grid_spec=grid_spec, out_shape=jax.ShapeDtypeStruct((l_dim, 2 * h, c), F32),
        compiler_params=_params(("parallel",) * 2),
    )(pos, sums, landed, landed, landed)


def _adam_big(w, m, v, grads, *, name):
    l_dim, r, c = w.shape
    assert len(grads) == l_dim
    tr = min(r, 256)

    def body(*refs):
        w_ref, m_ref, v_ref = refs[:3]
        g_refs = refs[3:3 + l_dim]
        go_ref, d_ref, mo_ref, vo_ref = refs[3 + l_dim:]
        g = g_refs[0][...]
        for l in range(1, l_dim):
            g = jnp.where(pl.program_id(0) == l, g_refs[l][...], g)
        delta, m_new, v_new = _adam_math(w_ref[...], g, m_ref[...], v_ref[...])
        go_ref[...] = g
        d_ref[...] = delta
        mo_ref[...] = m_new
        vo_ref[...] = v_new

    spec = pl.BlockSpec((None, tr, c), lambda l, i: (l, i, 0))
    gspec = pl.BlockSpec((None, tr, c), lambda l, i: (0, i, 0))
    return pl.pallas_call(
        body, name=name, grid=(l_dim, r // tr), in_specs=[spec] * 3 + [gspec] * l_dim, out_specs=[spec] * 4,
        out_shape=[jax.ShapeDtypeStruct(w.shape, F32)] * 4, compiler_params=_params(("parallel",) * 2),
    )(w, m, v, *grads)


def _position():
    return lax.axis_index("x"), lax.axis_index("y"), lax.axis_index("c")


def _other_chips(x, y):
    return [(1 - x, y), (x, 1 - y), (1 - x, 1 - y)]


def _remote(src, dst, send_sem, recv_sem, device):
    return pltpu.make_async_remote_copy(src_ref=src, dst_ref=dst, send_sem=send_sem, recv_sem=recv_sem,
                                        device_id=device, device_id_type=MESH)


ANY = pl.BlockSpec(memory_space=pl.ANY)


def _gather_sems(n):
    return [pltpu.SemaphoreType.DMA((3 * n,))] * 4


def _gather_steps(outs, send_sems, recv_sems, fwd_send, fwd_recv):
    n = len(outs)
    x, y, c = _position()
    chips = _other_chips(x, y)
    sibling = (x, y, 1 - c)

    def half(a, chip, core):
        h = outs[a].shape[2] // 2
        return outs[a].at[:, 2 * chip[0] + chip[1], pl.ds(core * h, h), :]

    def over_ici(a, k, chip):
        block = half(a, chip, c)
        return _remote(block, block, send_sems.at[3 * a + k], recv_sems.at[3 * a + k], (*chips[k], c))

    def over_d2d(a, k, core):
        block = half(a, chips[k], core)
        return _remote(block, block, fwd_send.at[3 * a + k], fwd_recv.at[3 * a + k], sibling)

    def send():
        for a in range(n):
            for k in range(3):
                over_ici(a, k, (x, y)).start()

    def forward():
        for k in range(3):
            for a in range(n):
                over_ici(a, k, chips[k]).wait_recv()
                over_d2d(a, k, c).start()

    def finish():
        for k in range(3):
            for a in range(n):
                over_d2d(a, k, 1 - c).wait_recv()
        for a in range(n):
            for k in range(3):
                over_ici(a, k, (x, y)).wait_send()
                over_d2d(a, k, c).wait_send()

    return send, forward, finish


def _swap_halves(grads, *, name):
    n = len(grads)

    def body(*refs):
        send, finish = _swap_steps(refs[:n], refs[n:2 * n], *refs[2 * n:])
        send()
        finish()

    sem = pltpu.SemaphoreType.DMA((n,))
    return pl.pallas_call(
        body, name=name, in_specs=[ANY] * n, out_specs=[ANY] * n, out_shape=_swap_shapes(grads),
        scratch_shapes=[sem, sem], compiler_params=pltpu.CompilerParams(has_side_effects=True),
    )(*grads)


def _swap_shapes(grads):
    return [jax.ShapeDtypeStruct(g.shape[:2] + (g.shape[2] // 2, g.shape[3]), g.dtype) for g in grads]


def _swap_steps(ins, outs, send_sems, recv_sems):
    x, y, c = _position()

    def copy(a):
        h = ins[a].shape[2] // 2
        return _remote(ins[a].at[:, :, pl.ds((1 - c) * h, h), :], outs[a], send_sems.at[a], recv_sems.at[a],
                       (x, y, 1 - c))

    def send():
        for a in range(len(ins)):
            copy(a).start()

    def finish():
        for a in range(len(ins)):
            copy(a).wait()

    return send, finish


def _exchange_shapes(sums):
    return [jax.ShapeDtypeStruct((s.shape[0], 3) + s.shape[2:], s.dtype) for s in sums]


def _exchange_sems(n):
    return [pltpu.SemaphoreType.DMA((3 * n,))] * 2


def _exchange_steps(ins, outs, send_sems, recv_sems):
    n = len(ins)
    x, y, c = _position()
    chips = _other_chips(x, y)

    def copy(a, k):
        chip = chips[k]
        return _remote(ins[a].at[:, 2 * chip[0] + chip[1]], outs[a].at[:, k],
                       send_sems.at[3 * a + k], recv_sems.at[3 * a + k], (*chip, c))

    def send():
        for a in range(n):
            for k in range(3):
                copy(a, k).start()

    def finish():
        for a in range(n):
            for k in range(3):
                copy(a, k).wait()

    return send, finish


def _join_halves(bufs, *, name):
    n = len(bufs)

    def body(*refs):
        send, finish = _join_steps(refs[n:2 * n], *refs[2 * n:])
        send()
        finish()

    sem = pltpu.SemaphoreType.DMA((n,))
    return pl.pallas_call(
        body, name=name, in_specs=[ANY] * n, out_specs=[ANY] * n,
        out_shape=[jax.ShapeDtypeStruct(b.shape, b.dtype) for b in bufs],
        input_output_aliases={a: a for a in range(n)},
        scratch_shapes=[sem, sem], compiler_params=pltpu.CompilerParams(has_side_effects=True),
    )(*bufs)


def _join_steps(outs, send_sems, recv_sems):
    x, y, c = _position()

    def copy(a, core):
        h = outs[a].shape[1] // 2
        half = outs[a].at[:, pl.ds(core * h, h), :]
        return _remote(half, half, send_sems.at[a], recv_sems.at[a], (x, y, 1 - c))

    def send():
        for a in range(len(outs)):
            copy(a, c).start()

    def finish():
        for a in range(len(outs)):
            copy(a, c).wait_send()
            copy(a, 1 - c).wait_recv()

    return send, finish


def _allgather_steps(ins, outs, send_sems, recv_sems, local_sems):
    n = len(ins)
    x, y, c = _position()
    me, sibling = (x, y, c), (x, y, 1 - c)
    chips = _other_chips(x, y)

    def slot(a, dev):
        return outs[a].at[4 * dev[0] + 2 * dev[1] + dev[2]]

    def copy(a, k, block, to, own=False):
        return _remote(ins[a] if own else slot(a, block), slot(a, block),
                       send_sems.at[7 * a + k], recv_sems.at[7 * a + k], to)

    def first(a):
        return [copy(a, 0, me, sibling, own=True)] + [copy(a, 1 + k, me, (*chips[k], c), own=True) for k in range(3)]

    def local(a):
        return pltpu.make_async_copy(ins[a], slot(a, me), local_sems.at[a])

    def send():
        for a in range(n):
            local(a).start()
            for cp in first(a):
                cp.start()

    def forward():
        for a in range(n):
            for k in range(3):
                copy(a, 1 + k, (*chips[k], c), me).wait_recv()
                copy(a, 4 + k, (*chips[k], c), sibling).start()

    def finish():
        for a in range(n):
            copy(a, 0, sibling, me).wait_recv()
            for k in range(3):
                copy(a, 4 + k, (*chips[k], 1 - c), me).wait_recv()
        for a in range(n):
            for cp in first(a) + [copy(a, 4 + k, (*chips[k], c), sibling) for k in range(3)]:
                cp.wait_send()
            local(a).wait()

    return send, forward, finish


def _allreduce_small(packs):
    n = len(packs)

    def body(*refs):
        ins, outs, gath = refs[:n], refs[n:2 * n], refs[2 * n:3 * n]
        send_sems, recv_sems = refs[3 * n:]
        x, y, c = _position()
        me, sibling = (x, y, c), (x, y, 1 - c)
        chips = _other_chips(x, y)

        def slot(a, dev):
            return gath[a].at[4 * dev[0] + 2 * dev[1] + dev[2]]

        def copy(a, k, block, to, src=None):
            return _remote(slot(a, block) if src is None else src, slot(a, block),
                           send_sems.at[7 * a + k], recv_sems.at[7 * a + k], to)

        started = []
        for a in range(n):
            slot(a, me)[...] = ins[a][...]
            first = [copy(a, 0, me, sibling, src=ins[a])]
            first += [copy(a, 1 + k, me, (*chip, c), src=ins[a]) for k, chip in enumerate(chips)]
            for cp in first:
                cp.start()
            started += first
        for a in range(n):
            for k, chip in enumerate(chips):
                copy(a, 1 + k, (*chip, c), me).wait_recv()
                cp = copy(a, 4 + k, (*chip, c), sibling)
                cp.start()
                started.append(cp)
        for a in range(n):
            copy(a, 0, sibling, me).wait_recv()
            for k, chip in enumerate(chips):
                copy(a, 4 + k, (*chip, 1 - c), me).wait_recv()
        for cp in started:
            cp.wait_send()
        for a in range(n):
            total = gath[a][0]
            for d in range(1, N_DEV):
                total = total + gath[a][d]
            outs[a][...] = total

    vmem = pl.BlockSpec(memory_space=pltpu.VMEM)
    sem = pltpu.SemaphoreType.DMA((7 * n,))
    return pl.pallas_call(
        body, name="allreduce_small", in_specs=[vmem] * n, out_specs=[vmem] * n,
        out_shape=[jax.ShapeDtypeStruct(p.shape, p.dtype) for p in packs],
        scratch_shapes=[pltpu.VMEM((N_DEV,) + p.shape, p.dtype) for p in packs] + [sem, sem],
        compiler_params=pltpu.CompilerParams(has_side_effects=True, vmem_limit_bytes=VMEM_LIMIT_BYTES),
    )(*packs)


LOSS_ROW = 1040


def _pad_rows(a, rows=8):
    return jnp.concatenate([a, jnp.zeros((rows - a.shape[0], a.shape[1]), a.dtype)], axis=0)

def _adam_small(wide, mid, narrow, late, params):
    names = ["mix_norm_g", "mlp_norm_g", "final_norm_g", "conv_b", "conv_w", "sgu_norm_g", "sgu_norm_b",
             "pool_w", "pool_scale", "sgu_w", "sgu_b"]
    n = len(names)

    def body(*refs):
        wmv = refs[4:4 + 3 * n]
        outs = refs[4 + 3 * n:]
        x, y, _ = _position()
        q = 2 * x + y

        def total(ref):
            t = ref[0]
            for dev in range(1, N_DEV):
                t = t + ref[dev]
            return t

        wide_sum, mid_sum, narrow_sum = total(refs[0]), total(refs[1]), total(refs[2])
        late_ref = refs[3]

        def my_quarter(rows):
            parts = [rows[:, s * TILE:(s + 1) * TILE] for s in range(N_CHIP)]
            return jnp.where(q == 0, parts[0], jnp.where(q == 1, parts[1], jnp.where(q == 2, parts[2], parts[3])))

        def tiles(first_row):
            return [((0, g), narrow_sum[first_row + g * TILE:first_row + (g + 1) * TILE, :]) for g in range(4)]

        grads = {
            "mix_norm_g": [((), wide_sum[0:2, :] + late_ref[0:2, :])],
            "mlp_norm_g": [((), wide_sum[8:10, :])],
            "final_norm_g": [((), wide_sum[16:17, :])],
            "conv_b": [((), mid_sum[0:1, :])],
            "conv_w": [((0,), my_quarter(mid_sum[8:11, :]))],
            "sgu_norm_g": [((), my_quarter(mid_sum[16:17, :]))],
            "sgu_norm_b": [((), my_quarter(mid_sum[24:25, :]))],
            "pool_w": tiles(0),
            "sgu_w": tiles(512),
            "pool_scale": [((0,), narrow_sum[1024:1028, :])],
            "sgu_b": [((0,), narrow_sum[1032:1036, :])],
        }
        outs[4 * n][...] = narrow_sum[LOSS_ROW:LOSS_ROW + 8, :]
        for i, name in enumerate(names):
            w_ref, m_ref, v_ref = wmv[3 * i:3 * i + 3]
            for lead, g in grads[name]:
                idx = lead + (slice(None), slice(None))
                delta, m_new, v_new = _adam_math(w_ref[idx], g, m_ref[idx], v_ref[idx])
                outs[4 * i][idx] = g
                outs[4 * i + 1][idx] = delta
                outs[4 * i + 2][idx] = m_new
                outs[4 * i + 3][idx] = v_new

    vmem = pl.BlockSpec(memory_space=pltpu.VMEM)
    args, out_shape = [wide, mid, narrow, late], []
    for name in names:
        w, m, v = params[name]
        args += [w, m, v]
        out_shape += [jax.ShapeDtypeStruct(w.shape, F32)] * 4
    out_shape.append(jax.ShapeDtypeStruct((8, TILE), F32))
    res = pl.pallas_call(
        body, name="adam_small", in_specs=[vmem] * len(args), out_specs=[vmem] * len(out_shape),
        out_shape=out_shape, compiler_params=pltpu.CompilerParams(vmem_limit_bytes=VMEM_LIMIT_BYTES),
    )(*args)
    return {name: res[4 * i:4 * i + 4] for i, name in enumerate(names)}, res[4 * n]


def _pair_sums(grads, got, pos, tag):
    return [_pair_sum(a, b, pos, name=f"pair_sum_{tag}{i}") for i, (a, b) in enumerate(zip(grads, got))]


def _chip_sums(sums, landed, pos, tag):
    return [_chip_sum(s, r, pos, name=f"chip_sum_{tag}{i}") for i, (s, r) in enumerate(zip(sums, landed))]


def kernel(x, mix_norm_g, mlp_norm_g, ab_w_in, pool_w, pool_scale, conv_w, conv_b, ab_w_out, cd_w_in, sgu_norm_g, sgu_norm_b, sgu_w, sgu_b, cd_w_out, mlp_w1, mlp_w2, final_norm_g, loss_target, m_mix_norm_g, m_mlp_norm_g, m_ab_w_in, m_pool_w, m_pool_scale, m_conv_w, m_conv_b, m_ab_w_out, m_cd_w_in, m_sgu_norm_g, m_sgu_norm_b, m_sgu_w, m_sgu_b, m_cd_w_out, m_mlp_w1, m_mlp_w2, m_final_norm_g, v_mix_norm_g, v_mlp_norm_g, v_ab_w_in, v_pool_w, v_pool_scale, v_conv_w, v_conv_b, v_ab_w_out, v_cd_w_in, v_sgu_norm_g, v_sgu_norm_b, v_sgu_w, v_sgu_b, v_cd_w_out, v_mlp_w1, v_mlp_w2, v_final_norm_g):
    nseq, t_len, d = x.shape
    m_tok = nseq * t_len
    h0 = x.reshape(m_tok, d)
    target = loss_target.reshape(m_tok, d)

    x_idx, y_idx = lax.axis_index("x"), lax.axis_index("y")
    q_idx = 2 * x_idx + y_idx
    pos = jnp.stack([q_idx, lax.axis_index("c")]).astype(jnp.int32)
    def shard_buffer(w, layer, tag):
        return _cast_place(w, layer, pos, name=f"cast_place_{tag}")

    def row_block(w):
        return w.reshape(1, 1, -1, w.shape[-1])

    later_weights = [shard_buffer(cd_w_out, 0, "cd_out"), shard_buffer(mlp_w1, 1, "w1_1"),
                     shard_buffer(mlp_w2, 1, "w2_1")]

    pool_w3, pool_scale3 = pool_w[0], pool_scale[0].reshape(4, 1, TILE)
    sgu_w3 = sgu_w[0]
    sgu_w3_t = jnp.swapaxes(sgu_w3, 1, 2)
    sgu_bias_tile = jnp.broadcast_to(sgu_b[0][:, :, None], (4, TILE, TILE))
    conv_w2, conv_b2 = conv_w[0], conv_b
    def place_quarter(v):
        return lax.dynamic_update_slice(jnp.zeros((v.shape[0], 4 * TILE), F32), v, (0, q_idx * TILE))

    sharded_small = jnp.concatenate(
        [place_quarter(conv_w[0]), place_quarter(sgu_norm_g), place_quarter(sgu_norm_b),
         jnp.zeros((3, 4 * TILE), F32)], axis=0)
    sharded_small, = _allreduce_small([sharded_small])
    sharded_small = sharded_small * 0.5
    conv_w_full = sharded_small[0:3]
    sgu_g_full = sharded_small[3:4]
    sgu_b_full = sharded_small[4:5]

    xn0, ((w_ab_in,),) = _rms_fwd(h0, mix_norm_g[0:1], name="rms_fwd_mix0",
                                  rider=[("gather", [shard_buffer(ab_w_in, 0, "ab_in")])])
    p_ab, ((w_1_0,),) = _mm_nn(xn0, w_ab_in, 0, out_dtype=BF16, name="ab_in_proj",
                               rider=[("gather", [shard_buffer(mlp_w1, 0, "w1_0")])])
    mix0, ((w_ab_out,),) = _ab_fwd(p_ab, pool_w3, pool_scale3, conv_w_full, conv_b2, nseq, t_len,
                                   rider=[("gather", [shard_buffer(ab_w_out, 0, "ab_out")])])
    w_ab_out = row_block(w_ab_out)
    h1, hn0 = _mm_nn(mix0, w_ab_out, 0, out_dtype=F32, name="ab_out_proj", epilogue="residual", extra=h0,
                     norm_g=mlp_norm_g[0:1])
    (act0, relu0), ((w_2_0,),) = _mm_nn(hn0, w_1_0, 0, out_dtype=BF16, name="mlp0_up", epilogue="relu2",
                                        rider=[("gather", [shard_buffer(mlp_w2, 0, "w2_0")])])
    w_2_0 = row_block(w_2_0)
    (h2, xn1), ((w_cd_in,),) = _mm_nn(act0, w_2_0, 0, out_dtype=F32, name="mlp0_down", epilogue="residual", extra=h1,
                                      norm_g=mix_norm_g[1:2],
                                      rider=[("gather", [shard_buffer(cd_w_in, 0, "cd_in")])])

    p_cd = _mm_nn(xn1, w_cd_in, 0, out_dtype=BF16, name="cd_in_proj")
    c_out = _sgu_fwd(p_cd, sgu_g_full, sgu_b_full, sgu_w3, sgu_bias_tile)
    d_out, ltot, (w_cd_out, w_1_1, w_2_1) = _sb_fwd(p_cd, nseq, t_len, later_weights)
    w_cd_out, w_2_1 = row_block(w_cd_out), row_block(w_2_1)
    mix1 = jnp.concatenate([c_out, d_out], axis=1)
    h3, hn1 = _mm_nn(mix1, w_cd_out, 0, out_dtype=F32, name="cd_out_proj", epilogue="residual", extra=h2,
                     norm_g=mlp_norm_g[1:2])
    act1, relu1 = _mm_nn(hn1, w_1_1, 0, out_dtype=BF16, name="mlp1_up", epilogue="relu2")

    dh4, dh4_bf, dg_final, loss_tile = _mlp_down_loss(act1, w_2_1, h3, final_norm_g.reshape(1, d), target)

    def as_pieces(g):
        return g.reshape(1, N_CHIP, -1, g.shape[-1]) if g.shape[1] == 1 else g

    dz1 = _mm_nt(dh4_bf, w_2_1, 0, out_dtype=BF16, name="mlp1_down_bwd", epilogue="relu2_bwd", extra=relu1)
    g_w2_1 = as_pieces(_mm_tn(act1, dh4_bf, 1, name="mlp1_down_wgrad"))
    g_w1_1 = _mm_tn(hn1, dz1, N_CHIP, name="mlp1_up_wgrad")
    (dh3, dh3_bf, dg_mlp1), (got_a,) = _mm_nt(
        dz1, w_1_1, 0, out_dtype=F32, name="mlp1_up_bwd", epilogue="rms_bwd",
        extra=(h3, mlp_norm_g[1:2], dh4), rider=[("swap", [g_w1_1, g_w2_1])])

    g_cd_out = as_pieces(_mm_tn(mix1, dh3_bf, 1, name="cd_out_wgrad"))
    dmix1, (got_cd_out,) = _mm_nt(dh3_bf, w_cd_out, 0, out_dtype=BF16, name="cd_out_bwd",
                                  rider=[("swap", [g_cd_out])])
    sums_a = _pair_sums([g_w1_1, g_w2_1, g_cd_out], got_a + got_cd_out, pos, "a")
    du, dv, dsgu_w, dsgu_bs, dsgu_g, dsgu_b = _sgu_bwd(p_cd, dmix1, sgu_g_full, sgu_b_full, sgu_w3, sgu_w3_t,
                                                      sgu_bias_tile)
    dq, dk, dvv, landed_a = _sb_bwd(p_cd, dmix1, ltot, nseq, t_len, sums_a)
    halves_a = _chip_sums(sums_a, landed_a, pos, "a")
    dp_cd = jnp.concatenate([du, dv, dq, dk, dvv], axis=1)
    g_cd_in, ((r_w1_1, r_w2_1, r_cd_out),) = _mm_tn(xn1, dp_cd, N_CHIP, name="cd_in_wgrad",
                                                    rider=[("join", halves_a)])
    (dh2, dh2_bf, dg_mix1), (got_c,) = _mm_nt(
        dp_cd, w_cd_in, 0, out_dtype=F32, name="cd_in_bwd", epilogue="rms_bwd",
        extra=(h2, mix_norm_g[1:2], dh3), rider=[("swap", [g_cd_in])])

    sums_c = _pair_sums([g_cd_in], got_c, pos, "c")
    dz0, (landed_c,) = _mm_nt(dh2_bf, w_2_0, 0, out_dtype=BF16, name="mlp0_down_bwd", epilogue="relu2_bwd",
                              extra=relu0, rider=[("exchange", sums_c)])
    halves_c = _chip_sums(sums_c, landed_c, pos, "c")
    g_w2_0, ((r_cd_in,),) = _mm_tn(act0, dh2_bf, 1, name="mlp0_down_wgrad", rider=[("join", halves_c)])
    g_w2_0 = as_pieces(g_w2_0)
    g_w1_0, (got_d,) = _mm_tn(hn0, dz0, N_CHIP, name="mlp0_up_wgrad", rider=[("swap", [g_w2_0])])
    sums_d = _pair_sums([g_w2_0], got_d, pos, "d")
    (dh1, dh1_bf, dg_mlp0), (landed_d, got_e) = _mm_nt(
        dz0, w_1_0, 0, out_dtype=F32, name="mlp0_up_bwd", epilogue="rms_bwd",
        extra=(h1, mlp_norm_g[0:1], dh2), rider=[("exchange", sums_d), ("swap", [g_w1_0])])
    halves_d = _chip_sums(sums_d, landed_d, pos, "d")
    sums_e = _pair_sums([g_w1_0], got_e, pos, "e")

    dmix0, ((r_w2_0,),) = _mm_nt(dh1_bf, w_ab_out, 0, out_dtype=BF16, name="ab_out_bwd", rider=[("join", halves_d)])
    g_ab_out = as_pieces(_mm_tn(mix0, dh1_bf, 1, name="ab_out_wgrad"))
    (da, dxb, dgb, dgc, dpool_w, dpool_scale, dconv_w, dconv_b), (landed_e, got_f) = _ab_bwd(
        p_ab, dmix0, pool_w3, pool_scale3, conv_w_full, conv_b2, nseq, t_len,
        rider=[("exchange", sums_e), ("swap", [g_ab_out])])
    halves_e = _chip_sums(sums_e, landed_e, pos, "e")
    sums_f = _pair_sums([g_ab_out], got_f, pos, "f")
    dp_ab = jnp.concatenate([da, dxb, dgb, dgc], axis=1)
    wide = jnp.concatenate([_pad_rows(jnp.concatenate([jnp.zeros_like(dg_mix1), dg_mix1], axis=0)),
                            _pad_rows(jnp.concatenate([dg_mlp0, dg_mlp1], axis=0)), _pad_rows(dg_final)], axis=0)
    mid = jnp.concatenate([_pad_rows(dconv_b), _pad_rows(dconv_w), _pad_rows(dsgu_g), _pad_rows(dsgu_b)], axis=0)
    narrow = jnp.concatenate(
        [dpool_w.reshape(4 * TILE, TILE), dsgu_w.reshape(4 * TILE, TILE), _pad_rows(dpool_scale.reshape(4, TILE)),
         _pad_rows(dsgu_bs[:, :, 0]), loss_tile], axis=0)
    g_ab_in, (landed_f, (r_w1_0,), (wide, mid, narrow)) = _mm_tn(
        xn0, dp_ab, N_CHIP, name="ab_in_wgrad",
        rider=[("exchange", sums_f), ("join", halves_e), ("allgather", [wide, mid, narrow])])
    halves_f = _chip_sums(sums_f, landed_f, pos, "f")
    sums_g = _pair_sums([g_ab_in], _swap_halves([g_ab_in], name="swap_halves_g"), pos, "g")
    (grad_x, _, dg_mix0), (landed_g, (r_ab_out,)) = _mm_nt(
        dp_ab, w_ab_in, 0, out_dtype=F32, name="ab_in_bwd", epilogue="rms_bwd",
        extra=(h0, mix_norm_g[0:1], dh1), rider=[("exchange", sums_g), ("join", halves_f)])
    r_ab_in, = _join_halves(_chip_sums(sums_g, landed_g, pos, "g"), name="join_halves_g")

    big_out = {
        "ab_w_in": _adam_big(ab_w_in, m_ab_w_in, v_ab_w_in, [r_ab_in], name="adam_ab_w_in"),
        "ab_w_out": _adam_big(ab_w_out, m_ab_w_out, v_ab_w_out, [r_ab_out], name="adam_ab_w_out"),
        "cd_w_in": _adam_big(cd_w_in, m_cd_w_in, v_cd_w_in, [r_cd_in], name="adam_cd_w_in"),
        "cd_w_out": _adam_big(cd_w_out, m_cd_w_out, v_cd_w_out, [r_cd_out], name="adam_cd_w_out"),
        "mlp_w1": _adam_big(mlp_w1, m_mlp_w1, v_mlp_w1, [r_w1_0, r_w1_1], name="adam_mlp_w1"),
        "mlp_w2": _adam_big(mlp_w2, m_mlp_w2, v_mlp_w2, [r_w2_0, r_w2_1], name="adam_mlp_w2"),
    }

    late, = _allreduce_small([_pad_rows(dg_mix0)])
    small_out, loss_sum = _adam_small(wide, mid, narrow, late, {
        "mix_norm_g": (mix_norm_g, m_mix_norm_g, v_mix_norm_g),
        "mlp_norm_g": (mlp_norm_g, m_mlp_norm_g, v_mlp_norm_g),
        "final_norm_g": tuple(a.reshape(1, d) for a in (final_norm_g, m_final_norm_g, v_final_norm_g)),
        "conv_b": (conv_b, m_conv_b, v_conv_b),
        "conv_w": (conv_w, m_conv_w, v_conv_w),
        "sgu_norm_g": (sgu_norm_g, m_sgu_norm_g, v_sgu_norm_g),
        "sgu_norm_b": (sgu_norm_b, m_sgu_norm_b, v_sgu_norm_b),
        "pool_w": (pool_w, m_pool_w, v_pool_w),
        "pool_scale": (pool_scale, m_pool_scale, v_pool_scale),
        "sgu_w": (sgu_w, m_sgu_w, v_sgu_w),
        "sgu_b": (sgu_b, m_sgu_b, v_sgu_b),
    })
    small_out["final_norm_g"] = [a.reshape(d) for a in small_out["final_norm_g"]]

    order = ["mix_norm_g", "mlp_norm_g", "ab_w_in", "pool_w", "pool_scale", "conv_w", "conv_b", "ab_w_out",
             "cd_w_in", "sgu_norm_g", "sgu_norm_b", "sgu_w", "sgu_b", "cd_w_out", "mlp_w1", "mlp_w2",
             "final_norm_g"]
    both = {**big_out, **small_out}
    loss = loss_sum[0, 0]
    outs = [loss, grad_x.reshape(nseq, t_len, d)]
    for kind in range(4):
        outs += [both[name][kind] for name in order]
    return tuple(outs)
```

```python
import math

import jax
import jax.numpy as jnp
from jax import lax
from jax.experimental import pallas as pl
from jax.experimental.pallas import tpu as pltpu

F32 = jnp.float32
BF16 = jnp.bfloat16
MESH = pl.DeviceIdType.MESH

D_MODEL = 1024
EPS = 1e-6
TILE = 128
N_CHIP = 4
N_DEV = 8
VMEM_LIMIT_BYTES = 56 * 1024 * 1024

ADAM_LR = 0.001
ADAM_B1 = 0.9
ADAM_B2 = 0.999
ADAM_EPS = 1e-08
ADAM_WD = 0.01
ADAM_STEP = 10

NT_DIMS = (((1,), (1,)), ((), ()))
TN_DIMS = (((0,), (0,)), ((), ()))


def _params(sem=None):
    return pltpu.CompilerParams(dimension_semantics=sem, vmem_limit_bytes=VMEM_LIMIT_BYTES)


def _call(body, *, name, grid, in_specs, out_specs, out_shape, scratch_shapes, semantics, args, rider=None):
    if not rider:
        res = pl.pallas_call(body, name=name, grid=grid, in_specs=in_specs, out_specs=out_specs, out_shape=out_shape,
                             scratch_shapes=scratch_shapes, compiler_params=_params(semantics))(*args)
        return list(res), []
    plans = [_rider_plan(kind, arrays) for kind, arrays in rider]
    arrays = [a for _, group in rider for a in group]
    nr, n_in, n_out, n_scr = len(arrays), len(in_specs), len(out_specs), len(scratch_shapes)
    first_out, first_scr = n_in + nr, n_in + nr + n_out + nr
    last_step = math.prod(grid) - 1

    def riding(*refs):
        step = 0
        for axis, size in enumerate(grid):
            step = step * size + pl.program_id(axis)
        steps, at, sem_at = [], 0, first_scr + n_scr
        for (kind, group), (_, sems, _) in zip(rider, plans):
            k = len(group)
            steps.append(_rider_steps(kind, refs[n_in + at:n_in + at + k],
                                      refs[first_out + n_out + at:first_out + n_out + at + k],
                                      refs[sem_at:sem_at + len(sems)]))
            at, sem_at = at + k, sem_at + len(sems)
        for send, _, _ in steps:
            pl.when(step == 0)(send)
        for _, forward, _ in steps:
            if forward is not None:
                pl.when(step == last_step)(forward)
        body(*refs[:n_in], *refs[first_out:first_out + n_out], *refs[first_scr:first_scr + n_scr])
        for _, _, finish in steps:
            pl.when(step == last_step)(finish)

    aliases, at = {}, 0
    for (_, group), (_, _, aliased) in zip(rider, plans):
        if aliased:
            aliases.update({n_in + at + a: n_out + at + a for a in range(len(group))})
        at += len(group)
    res = pl.pallas_call(
        riding, name=name, grid=grid, in_specs=list(in_specs) + [ANY] * nr, out_specs=list(out_specs) + [ANY] * nr,
        out_shape=list(out_shape) + [s for shapes, _, _ in plans for s in shapes],
        scratch_shapes=list(scratch_shapes) + [s for _, sems, _ in plans for s in sems],
        input_output_aliases=aliases,
        compiler_params=pltpu.CompilerParams(dimension_semantics=("arbitrary",) * len(grid),
                                             vmem_limit_bytes=VMEM_LIMIT_BYTES, has_side_effects=True),
    )(*args, *arrays)
    rode, at = [], n_out
    for _, group in rider:
        rode.append(list(res[at:at + len(group)]))
        at += len(group)
    return list(res[:n_out]), rode


def _rider_plan(kind, arrays):
    n = len(arrays)
    same = [jax.ShapeDtypeStruct(a.shape, a.dtype) for a in arrays]
    pair = [pltpu.SemaphoreType.DMA((n,))] * 2
    if kind == "gather":
        return same, _gather_sems(n), True
    if kind == "exchange":
        return _exchange_shapes(arrays), _exchange_sems(n), False
    if kind == "swap":
        return _swap_shapes(arrays), pair, False
    if kind == "allgather":
        return ([jax.ShapeDtypeStruct((N_DEV,) + a.shape, a.dtype) for a in arrays],
                [pltpu.SemaphoreType.DMA((7 * n,))] * 2 + [pltpu.SemaphoreType.DMA((n,))], False)
    assert kind == "join"
    return same, pair, True


def _rider_steps(kind, ins, outs, sems):
    if kind == "gather":
        return _gather_steps(outs, *sems)
    if kind == "allgather":
        return _allgather_steps(ins, outs, *sems)
    if kind == "exchange":
        send, finish = _exchange_steps(ins, outs, *sems)
    elif kind == "swap":
        send, finish = _swap_steps(ins, outs, *sems)
    else:
        send, finish = _join_steps(outs, *sems)
    return send, None, finish


def _row_tile(k_dim):
    return 1024 if k_dim <= 1024 else 512


def _mm_nn(a, b4, layer, *, out_dtype, name, epilogue=None, extra=None, norm_g=None, rider=None):
    m, k_dim = a.shape
    _, s_dim, kb, n = b4.shape
    assert kb == k_dim
    tm, tn = _row_tile(k_dim), min(n, 1024)
    assert m % tm == 0 and n % tn == 0
    npb = n // tn
    grid = (m // tm, s_dim * npb)
    n_in = 2 + (extra is not None) + (norm_g is not None)
    two_outputs = norm_g is not None or epilogue == "relu2"
    assert norm_g is None or (tn == s_dim * n and epilogue != "relu2")

    def body(*refs):
        a_ref, b_ref = refs[:2]
        e_ref = refs[2] if extra is not None else None
        g_ref = refs[n_in - 1] if norm_g is not None else None
        o_ref = refs[n_in]
        acc = jnp.dot(a_ref[...], b_ref[...], preferred_element_type=F32)
        if epilogue == "relu2":
            r = jnp.maximum(acc, 0.0)
            refs[n_in + 1][...] = r.astype(BF16)
            acc = r * r
        elif epilogue == "residual":
            acc = acc + e_ref[...]
        o_ref[...] = acc.astype(out_dtype)
        if norm_g is not None:
            rstd = lax.rsqrt(jnp.mean(acc * acc, axis=-1, keepdims=True) + EPS)
            refs[n_in + 1][...] = (acc * rstd * g_ref[...]).astype(BF16)

    in_specs = [
        pl.BlockSpec((tm, k_dim), lambda i, j: (i, 0)),
        pl.BlockSpec((None, None, k_dim, tn), lambda i, j: (layer, j // npb, 0, j % npb)),
    ]
    args = [a, b4]
    if extra is not None:
        in_specs.append(pl.BlockSpec((tm, tn), lambda i, j: (i, j)))
        args.append(extra)
    out_block = pl.BlockSpec((tm, tn), lambda i, j: (i, j))
    out_specs, out_shape = [out_block], [jax.ShapeDtypeStruct((m, s_dim * n), out_dtype)]
    if norm_g is not None:
        in_specs.append(pl.BlockSpec((1, tn), lambda i, j: (0, j)))
        args.append(norm_g)
    if two_outputs:
        out_specs.append(out_block)
        out_shape.append(jax.ShapeDtypeStruct((m, s_dim * n), BF16))
    res, rode = _call(
        body, name=name, grid=grid, in_specs=in_specs, out_specs=out_specs, out_shape=out_shape,
        scratch_shapes=[], semantics=("parallel", "parallel"), args=args, rider=rider)
    res = res if two_outputs else res[0]
    return res if rider is None else (res, rode)


def _mm_nt(a, b4, layer, *, out_dtype, name, epilogue=None, extra=None, rider=None):
    m, k_dim = a.shape
    _, s_dim, n_out, n = b4.shape
    assert k_dim == s_dim * n
    tm, tn = _row_tile(k_dim), min(n_out, 1024)
    assert m % tm == 0 and n_out % tn == 0
    grid = (m // tm, n_out // tn)
    rms = epilogue == "rms_bwd"
    assert not rms or tn == n_out
    extras = [] if extra is None else (list(extra) if rms else [extra])
    n_in = 2 + len(extras)
    n_res = 3 if rms else 1

    def body(*refs):
        a_ref, b_ref = refs[:2]
        e_refs = refs[2:n_in]
        o_ref = refs[n_in]
        acc = lax.dot_general(a_ref[:, 0:n], b_ref[0], NT_DIMS, preferred_element_type=F32)
        for s in range(1, s_dim):
            acc = acc + lax.dot_general(a_ref[:, s * n:(s + 1) * n], b_ref[s], NT_DIMS, preferred_element_type=F32)
        if epilogue == "relu2_bwd":
            acc = acc * (2.0 * e_refs[0][...].astype(F32))
        if not rms:
            o_ref[...] = acc.astype(out_dtype)
        else:
            h_ref, g_ref, dres_ref = e_refs
            dhb_ref, dg_ref = refs[n_in + 1:n_in + 3]
            hv = h_ref[...]
            rstd = lax.rsqrt(jnp.mean(hv * hv, axis=-1, keepdims=True) + EPS)
            xhat = hv * rstd
            dxhat = acc * g_ref[...]
            dh = dres_ref[...] + rstd * (dxhat - xhat * jnp.mean(dxhat * xhat, axis=-1, keepdims=True))
            o_ref[...] = dh
            dhb_ref[...] = dh.astype(BF16)
            dg_part = jnp.sum(acc * xhat, axis=0, keepdims=True)
            first = pl.program_id(0) == 0

            @pl.when(first)
            def _():
                dg_ref[...] = dg_part

            @pl.when(jnp.logical_not(first))
            def _():
                dg_ref[...] += dg_part

    in_specs = [
        pl.BlockSpec((tm, k_dim), lambda i, j: (i, 0)),
        pl.BlockSpec((None, s_dim, tn, n), lambda i, j: (layer, 0, j, 0)),
    ]
    args = [a, b4] + extras
    block = pl.BlockSpec((tm, tn), lambda i, j: (i, j))
    vec = pl.BlockSpec((1, tn), lambda i, j: (0, j))
    if rms:
        in_specs += [block, vec, block]
        out_specs = [block, block, vec]
        out_shape = [jax.ShapeDtypeStruct((m, n_out), F32), jax.ShapeDtypeStruct((m, n_out), BF16),
                     jax.ShapeDtypeStruct((1, n_out), F32)]
    else:
        in_specs += [block] * len(extras)
        out_specs, out_shape = [block], [jax.ShapeDtypeStruct((m, n_out), out_dtype)]
    res, rode = _call(
        body, name=name, grid=grid, in_specs=in_specs, out_specs=out_specs, out_shape=out_shape,
        scratch_shapes=[], semantics=("arbitrary",) * 2 if rms else ("parallel", "parallel"), args=args, rider=rider)
    res = res if rms else res[0]
    return res if rider is None else (res, rode)


def _mm_tn(a, b, s_dim, *, name, rider=None):
    m, k1 = a.shape
    mb, n_all = b.shape
    assert mb == m and n_all % s_dim == 0
    n = n_all // s_dim
    tn, t1 = min(n, 1024), _row_tile(m)
    assert k1 % t1 == 0 and n % tn == 0
    npb = n // tn
    grid = (k1 // t1, s_dim * npb)

    def body(a_ref, b_ref, o_ref):
        o_ref[...] = lax.dot_general(a_ref[...], b_ref[...], TN_DIMS, preferred_element_type=F32).astype(BF16)

    res, rode = _call(
        body, name=name, grid=grid,
        in_specs=[pl.BlockSpec((m, t1), lambda i, j: (0, i)), pl.BlockSpec((m, tn), lambda i, j: (0, j))],
        out_specs=[pl.BlockSpec((None, None, t1, tn), lambda i, j: (0, j // npb, i, j % npb))],
        out_shape=[jax.ShapeDtypeStruct((1, s_dim, k1, n), BF16)],
        scratch_shapes=[], semantics=("parallel", "parallel"), args=[a, b], rider=rider)
    return res[0] if rider is None else (res[0], rode)


ROW_TILE = 512


def _rms_fwd(h, g, *, name, rider=None):
    m, d = h.shape

    def body(h_ref, g_ref, o_ref):
        hv = h_ref[...]
        rstd = lax.rsqrt(jnp.mean(hv * hv, axis=-1, keepdims=True) + EPS)
        o_ref[...] = (hv * rstd * g_ref[...]).astype(BF16)

    res, rode = _call(
        body, name=name, grid=(m // ROW_TILE,),
        in_specs=[pl.BlockSpec((ROW_TILE, d), lambda i: (i, 0)), pl.BlockSpec((1, d), lambda i: (0, 0))],
        out_specs=[pl.BlockSpec((ROW_TILE, d), lambda i: (i, 0))], out_shape=[jax.ShapeDtypeStruct((m, d), BF16)],
        scratch_shapes=[], semantics=("parallel",), args=[h, g], rider=rider)
    return res[0] if rider is None else (res[0], rode)


def _mlp_down_loss(act, w_2, h_res, g, target):
    m, k_dim = act.shape
    d = w_2.shape[-1]
    tm = _row_tile(k_dim)

    def body(a_ref, b_ref, r_ref, g_ref, t_ref, dh_ref, dhb_ref, dg_ref, loss_ref):
        hv = jnp.dot(a_ref[...], b_ref[...], preferred_element_type=F32) + r_ref[...]
        gv = g_ref[...]
        rstd = lax.rsqrt(jnp.mean(hv * hv, axis=-1, keepdims=True) + EPS)
        xhat = hv * rstd
        err = xhat * gv - t_ref[...]
        dy = err * (1.0 / d)
        dxhat = dy * gv
        dh = rstd * (dxhat - xhat * jnp.mean(dxhat * xhat, axis=-1, keepdims=True))
        dh_ref[...] = dh
        dhb_ref[...] = dh.astype(BF16)
        dg_part = jnp.sum(dy * xhat, axis=0, keepdims=True)
        sq = jnp.sum(jnp.sum(err * err, axis=1, keepdims=True), axis=0, keepdims=True) * (0.5 / d)
        loss_part = jnp.broadcast_to(sq, (8, TILE))

        @pl.when(pl.program_id(0) == 0)
        def _():
            dg_ref[...] = dg_part
            loss_ref[...] = loss_part

        @pl.when(pl.program_id(0) > 0)
        def _():
            dg_ref[...] += dg_part
            loss_ref[...] += loss_part

    row = pl.BlockSpec((tm, d), lambda i: (i, 0))
    vec = pl.BlockSpec((1, d), lambda i: (0, 0))
    return pl.pallas_call(
        body, name="mlp1_down_loss", grid=(m // tm,),
        in_specs=[pl.BlockSpec((tm, k_dim), lambda i: (i, 0)),
                  pl.BlockSpec((None, None, k_dim, d), lambda i: (0, 0, 0, 0)), row, vec, row],
        out_specs=[row, row, vec, pl.BlockSpec((8, TILE), lambda i: (0, 0))],
        out_shape=[jax.ShapeDtypeStruct((m, d), F32), jax.ShapeDtypeStruct((m, d), BF16),
                   jax.ShapeDtypeStruct((1, d), F32), jax.ShapeDtypeStruct((8, TILE), F32)],
        compiler_params=_params(("arbitrary",)),
    )(act, w_2, h_res, g, target)


def _shift_down(x, s, t_idx):
    return jnp.where(t_idx >= s, pltpu.roll(x, s, 0), 0.0)


def _shift_up(x, s, t_idx, t_len):
    return jnp.where(t_idx < t_len - s, pltpu.roll(x, t_len - s, 0), 0.0)


def _pool_select(group, s2, s4, s8, s16):
    return jnp.where(group == 0, s2, jnp.where(group == 1, s4, jnp.where(group == 2, s8, s16)))


def _pool_count(group, t_idx):
    win = jnp.left_shift(2, group)
    return jnp.minimum(t_idx + 1, win).astype(F32)


def _pool_fwd_math(a, group, t_idx):
    s2 = a + _shift_down(a, 1, t_idx)
    s4 = s2 + _shift_down(s2, 2, t_idx)
    s8 = s4 + _shift_down(s4, 4, t_idx)
    s16 = s8 + _shift_down(s8, 8, t_idx)
    return _pool_select(group, s2, s4, s8, s16) / _pool_count(group, t_idx) - a


def _pool_bwd_math(dpooled, group, t_idx, t_len):
    e = dpooled / _pool_count(group, t_idx)
    s2 = e + _shift_up(e, 1, t_idx, t_len)
    s4 = s2 + _shift_up(s2, 2, t_idx, t_len)
    s8 = s4 + _shift_up(s4, 4, t_idx, t_len)
    s16 = s8 + _shift_up(s8, 8, t_idx, t_len)
    return _pool_select(group, s2, s4, s8, s16) - dpooled


def _conv_fwd_math(c, w_ref, b_ref, t_idx):
    return (w_ref[0:1, :] * _shift_down(c, 2, t_idx) + w_ref[1:2, :] * _shift_down(c, 1, t_idx)
            + w_ref[2:3, :] * c + b_ref[...])


def _ab_fwd(p, pool_w, pool_scale, conv_w, conv_b, nseq, t_len, rider=None):
    m = p.shape[0]
    ng = 4

    def body(a_ref, xb_ref, gb_ref, gc_ref, pw_ref, ps_ref, cw_ref, cb_ref, o_ref):
        j = pl.program_id(1)
        t_idx = lax.broadcasted_iota(jnp.int32, (t_len, TILE), 0)

        @pl.when(j < ng)
        def _():
            pooled = _pool_fwd_math(a_ref[...].astype(F32), j, t_idx)
            mixed = jnp.dot(pooled.astype(BF16), pw_ref[...].astype(BF16), preferred_element_type=F32)
            o_ref[...] = (mixed * ps_ref[...]).astype(BF16)

        @pl.when(j >= ng)
        def _():
            c = gc_ref[...].astype(F32) * xb_ref[...].astype(F32)
            y = _conv_fwd_math(c, cw_ref, cb_ref, t_idx)
            o_ref[...] = (gb_ref[...].astype(F32) * y).astype(BF16)

    def pool_j(j):
        return jnp.minimum(j, ng - 1)

    def conv_j(j):
        return jnp.maximum(j - ng, 0)

    in_specs = [
        pl.BlockSpec((t_len, TILE), lambda s, j: (s, pool_j(j))),
        pl.BlockSpec((t_len, TILE), lambda s, j: (s, ng + conv_j(j))),
        pl.BlockSpec((t_len, TILE), lambda s, j: (s, 2 * ng + conv_j(j))),
        pl.BlockSpec((t_len, TILE), lambda s, j: (s, 3 * ng + conv_j(j))),
        pl.BlockSpec((None, TILE, TILE), lambda s, j: (pool_j(j), 0, 0)),
        pl.BlockSpec((None, 1, TILE), lambda s, j: (pool_j(j), 0, 0)),
        pl.BlockSpec((3, TILE), lambda s, j: (0, conv_j(j))),
        pl.BlockSpec((1, TILE), lambda s, j: (0, conv_j(j))),
    ]
    res, rode = _call(
        body, name="ab_mixer_fwd", grid=(nseq, 2 * ng), in_specs=in_specs,
        out_specs=[pl.BlockSpec((t_len, TILE), lambda s, j: (s, j))],
        out_shape=[jax.ShapeDtypeStruct((m, 2 * ng * TILE), BF16)], scratch_shapes=[],
        semantics=("parallel", "arbitrary"), args=[p, p, p, p, pool_w, pool_scale, conv_w, conv_b], rider=rider)
    return res[0] if rider is None else (res[0], rode)


def _ab_bwd(p, dmix, pool_w, pool_scale, conv_w, conv_b, nseq, t_len, rider=None):
    m = p.shape[0]
    ng = 4

    def body(a_ref, xb_ref, gb_ref, gc_ref, dma_ref, dmb_ref, pw_ref, ps_ref, cw_ref, cb_ref,
             da_ref, dxb_ref, dgb_ref, dgc_ref, dpw_ref, dps_ref, dcw_ref, dcb_ref):
        j = pl.program_id(0)
        first = pl.program_id(1) == 0
        t_idx = lax.broadcasted_iota(jnp.int32, (t_len, TILE), 0)

        pooled = _pool_fwd_math(a_ref[...].astype(F32), j, t_idx).astype(BF16)
        w_bf = pw_ref[...].astype(BF16)
        mixed = jnp.dot(pooled, w_bf, preferred_element_type=F32)
        dm = dma_ref[...].astype(F32)
        dps = jnp.sum(dm * mixed, axis=0, keepdims=True)
        dmixed = (dm * ps_ref[...]).astype(BF16)
        dpw = lax.dot_general(pooled, dmixed, TN_DIMS, preferred_element_type=F32)
        dpooled = lax.dot_general(dmixed, w_bf, NT_DIMS, preferred_element_type=F32)
        da_ref[...] = _pool_bwd_math(dpooled, j, t_idx, t_len).astype(BF16)

        xb = xb_ref[...].astype(F32)
        gb = gb_ref[...].astype(F32)
        gc = gc_ref[...].astype(F32)
        d = dmb_ref[...].astype(F32)
        c = gc * xb
        c1 = _shift_down(c, 1, t_idx)
        c2 = _shift_down(c, 2, t_idx)
        y = cw_ref[0:1, :] * c2 + cw_ref[1:2, :] * c1 + cw_ref[2:3, :] * c + cb_ref[...]
        dgb_ref[...] = (d * y).astype(BF16)
        dy = d * gb
        dc = (cw_ref[2:3, :] * dy + cw_ref[1:2, :] * _shift_up(dy, 1, t_idx, t_len)
              + cw_ref[0:1, :] * _shift_up(dy, 2, t_idx, t_len))
        dgc_ref[...] = (dc * xb).astype(BF16)
        dxb_ref[...] = (dc * gc).astype(BF16)
        dcw = jnp.concatenate([jnp.sum(dy * c2, axis=0, keepdims=True),
                               jnp.sum(dy * c1, axis=0, keepdims=True),
                               jnp.sum(dy * c, axis=0, keepdims=True)], axis=0)
        dcb = jnp.sum(dy, axis=0, keepdims=True)

        @pl.when(first)
        def _():
            dpw_ref[...] = dpw
            dps_ref[...] = dps
            dcw_ref[...] = dcw
            dcb_ref[...] = dcb

        @pl.when(jnp.logical_not(first))
        def _():
            dpw_ref[...] += dpw
            dps_ref[...] += dps
            dcw_ref[...] += dcw
            dcb_ref[...] += dcb

    def col(k):
        return pl.BlockSpec((t_len, TILE), lambda j, s: (s, k * ng + j))

    in_specs = [
        col(0), col(1), col(2), col(3), col(0), col(1),
        pl.BlockSpec((None, TILE, TILE), lambda j, s: (j, 0, 0)),
        pl.BlockSpec((None, 1, TILE), lambda j, s: (j, 0, 0)),
        pl.BlockSpec((3, TILE), lambda j, s: (0, j)),
        pl.BlockSpec((1, TILE), lambda j, s: (0, j)),
    ]
    piece = pl.BlockSpec((t_len, TILE), lambda j, s: (s, j))
    out_specs = [
        piece, piece, piece, piece,
        pl.BlockSpec((None, TILE, TILE), lambda j, s: (j, 0, 0)),
        pl.BlockSpec((None, 1, TILE), lambda j, s: (j, 0, 0)),
        pl.BlockSpec((3, TILE), lambda j, s: (0, j)),
        pl.BlockSpec((1, TILE), lambda j, s: (0, j)),
    ]
    w = ng * TILE
    out_shape = [jax.ShapeDtypeStruct((m, w), BF16)] * 4 + [
        jax.ShapeDtypeStruct((ng, TILE, TILE), F32), jax.ShapeDtypeStruct((ng, 1, TILE), F32),
        jax.ShapeDtypeStruct((3, w), F32), jax.ShapeDtypeStruct((1, w), F32)]
    res, rode = _call(
        body, name="ab_mixer_bwd", grid=(ng, nseq), in_specs=in_specs, out_specs=out_specs, out_shape=out_shape,
        scratch_shapes=[], semantics=("parallel", "arbitrary"),
        args=[p, p, p, p, dmix, dmix, pool_w, pool_scale, conv_w, conv_b], rider=rider)
    return res if rider is None else (res, rode)


SGU_ROWS = 512
INV_SQRT2 = 1.0 / math.sqrt(2.0)
INV_SQRT_2PI = 1.0 / math.sqrt(2.0 * math.pi)


def _gelu(x):
    return 0.5 * x * (1.0 + lax.erf(x * INV_SQRT2))


def _gelu_grad(x):
    return 0.5 * (1.0 + lax.erf(x * INV_SQRT2)) + x * (INV_SQRT_2PI * jnp.exp(-0.5 * x * x))


def _causal_tile(transposed=False):
    r = lax.broadcasted_iota(jnp.int32, (TILE, TILE), 0)
    c = lax.broadcasted_iota(jnp.int32, (TILE, TILE), 1)
    return r <= c if transposed else c <= r


def _sgu_norm(v, g_ref, b_ref):
    mu = jnp.mean(v, axis=-1, keepdims=True)
    xc = v - mu
    rstd = lax.rsqrt(jnp.mean(xc * xc, axis=-1, keepdims=True) + EPS)
    xhat = xc * rstd
    return xhat, rstd, xhat * g_ref[...] + b_ref[...]


def _sgu_fwd(p, norm_g, norm_b, w_s, bias_tile):
    m = p.shape[0]
    ng = 4
    width = ng * TILE

    def body(u_ref, v_ref, g_ref, b_ref, w_ref, bias_ref, o_ref):
        u = _gelu(u_ref[...].astype(F32))
        _, _, vln = _sgu_norm(_gelu(v_ref[...].astype(F32)), g_ref, b_ref)
        vln = vln.astype(BF16)
        causal = _causal_tile()
        for g in range(ng):
            cols = slice(g * TILE, (g + 1) * TILE)
            wg = jnp.where(causal, w_ref[g], 0.0).astype(BF16)
            for n in range(SGU_ROWS // TILE):
                rows = slice(n * TILE, (n + 1) * TILE)
                s = jnp.dot(wg, vln[rows, cols], preferred_element_type=F32) + bias_ref[g]
                o_ref[rows, cols] = (u[rows, cols] * s).astype(BF16)

    vec = pl.BlockSpec((1, width), lambda i: (0, 0))
    tiles = pl.BlockSpec((ng, TILE, TILE), lambda i: (0, 0, 0))
    return pl.pallas_call(
        body, name="sgu_fwd", grid=(m // SGU_ROWS,),
        in_specs=[pl.BlockSpec((SGU_ROWS, width), lambda i: (i, 0)),
                  pl.BlockSpec((SGU_ROWS, width), lambda i: (i, 1)), vec, vec, tiles, tiles],
        out_specs=pl.BlockSpec((SGU_ROWS, width), lambda i: (i, 0)),
        out_shape=jax.ShapeDtypeStruct((m, width), BF16),
        compiler_params=_params(("parallel",)),
    )(p, p, norm_g, norm_b, w_s, bias_tile)


def _sgu_bwd(p, dmix, norm_g, norm_b, w_s, w_s_t, bias_tile):
    m = p.shape[0]
    ng = 4
    width = ng * TILE

    def body(u_ref, v_ref, dc_ref, g_ref, b_ref, w_ref, wt_ref, bias_ref,
             du_ref, dv_ref, dw_ref, dbs_ref, dg_ref, db_ref, ds_scr, dvln_scr):
        u_pre = u_ref[...].astype(F32)
        v_pre = v_ref[...].astype(F32)
        u = _gelu(u_pre)
        xhat, rstd, vln = _sgu_norm(_gelu(v_pre), g_ref, b_ref)
        vln = vln.astype(BF16)
        dc = dc_ref[...].astype(F32)
        causal = _causal_tile()
        ones = jnp.ones((TILE, TILE), BF16)
        first = pl.program_id(0) == 0
        for g in range(ng):
            cols = slice(g * TILE, (g + 1) * TILE)
            wg = jnp.where(causal, w_ref[g], 0.0).astype(BF16)
            wgt = jnp.where(_causal_tile(transposed=True), wt_ref[g], 0.0).astype(BF16)
            dw_acc = jnp.zeros((TILE, TILE), F32)
            dbs_acc = jnp.zeros((TILE, TILE), F32)
            for n in range(SGU_ROWS // TILE):
                rows = slice(n * TILE, (n + 1) * TILE)
                vt = vln[rows, cols]
                s = jnp.dot(wg, vt, preferred_element_type=F32) + bias_ref[g]
                ds_scr[rows, cols] = dc[rows, cols] * s
                ds = (dc[rows, cols] * u[rows, cols]).astype(BF16)
                dw_acc += lax.dot_general(ds, vt, NT_DIMS, preferred_element_type=F32)
                dbs_acc += jnp.dot(ds, ones, preferred_element_type=F32)
                dvln_scr[rows, cols] = jnp.dot(wgt, ds, preferred_element_type=F32)
            dw_g = jnp.where(causal, dw_acc, 0.0)

            @pl.when(first)
            def _():
                dw_ref[g] = dw_g
                dbs_ref[g] = dbs_acc

            @pl.when(jnp.logical_not(first))
            def _():
                dw_ref[g] += dw_g
                dbs_ref[g] += dbs_acc

        du_ref[...] = (ds_scr[...] * _gelu_grad(u_pre)).astype(BF16)
        dvln = dvln_scr[...]
        dxhat = dvln * g_ref[...]
        dv = rstd * (dxhat - jnp.mean(dxhat, axis=-1, keepdims=True)
                     - xhat * jnp.mean(dxhat * xhat, axis=-1, keepdims=True))
        dv_ref[...] = (dv * _gelu_grad(v_pre)).astype(BF16)
        dg_part = jnp.sum(dvln * xhat, axis=0, keepdims=True)
        db_part = jnp.sum(dvln, axis=0, keepdims=True)

        @pl.when(first)
        def _():
            dg_ref[...] = dg_part
            db_ref[...] = db_part

        @pl.when(jnp.logical_not(first))
        def _():
            dg_ref[...] += dg_part
            db_ref[...] += db_part

    vec = pl.BlockSpec((1, width), lambda i: (0, 0))
    tiles = pl.BlockSpec((ng, TILE, TILE), lambda i: (0, 0, 0))
    rows0 = pl.BlockSpec((SGU_ROWS, width), lambda i: (i, 0))
    rows1 = pl.BlockSpec((SGU_ROWS, width), lambda i: (i, 1))
    return pl.pallas_call(
        body, name="sgu_bwd", grid=(m // SGU_ROWS,),
        in_specs=[rows0, rows1, rows0, vec, vec, tiles, tiles, tiles],
        out_specs=[rows0, rows0, tiles, tiles, vec, vec],
        out_shape=[jax.ShapeDtypeStruct((m, width), BF16), jax.ShapeDtypeStruct((m, width), BF16),
                   jax.ShapeDtypeStruct((ng, TILE, TILE), F32), jax.ShapeDtypeStruct((ng, TILE, TILE), F32),
                   jax.ShapeDtypeStruct((1, width), F32), jax.ShapeDtypeStruct((1, width), F32)],
        scratch_shapes=[pltpu.VMEM((SGU_ROWS, width), F32), pltpu.VMEM((SGU_ROWS, width), F32)],
        compiler_params=_params(("arbitrary",)),
    )(p, p, dmix, norm_g, norm_b, w_s, w_s_t, bias_tile)


SB_DH = 64
SB_SCALE = 1.0 / math.sqrt(SB_DH)


SB_BLOCK = 256
SB_SUB = SB_BLOCK // TILE


def _sum_matrix(kind):
    j = lax.broadcasted_iota(jnp.int32, (TILE, 2 * TILE), 0)
    s = lax.broadcasted_iota(jnp.int32, (TILE, 2 * TILE), 1)
    tri = {"after": j > s, "upto": j <= s, "before": j < s}[kind]
    return jnp.where(jnp.logical_or(s >= TILE, tri), 1.0, 0.0).astype(BF16)


def _strict_mask():
    r = lax.broadcasted_iota(jnp.int32, (SB_BLOCK, SB_BLOCK), 0)
    c = lax.broadcasted_iota(jnp.int32, (SB_BLOCK, SB_BLOCK), 1)
    return c < r


def _head_lanes(h):
    lane = lax.broadcasted_iota(jnp.int32, (1, TILE), 1)
    return (lane >= h * SB_DH) & (lane < (h + 1) * SB_DH)


def _softplus(z):
    return jnp.maximum(z, 0.0) + jnp.log(1.0 + jnp.exp(-jnp.abs(z)))


def _sb_fwd(p, nseq, t_len, gather):
    m = p.shape[0]
    npair = 4
    ng = len(gather)
    last_step = nseq * npair - 1

    def body(q_ref, k_ref, v_ref, *rest):
        o_ref, lt_ref = rest[ng:ng + 2]
        kh_ref, vh_ref = rest[2 * ng + 2:2 * ng + 4]
        step = pl.program_id(0) * npair + pl.program_id(1)
        send, forward, finish = _gather_steps(rest[ng + 2:2 * ng + 2], *rest[2 * ng + 4:])
        pl.when(step == 0)(send)
        pl.when(step == (last_step + 1) // 2)(forward)
        for h in range(2):
            keep = _head_lanes(h)
            kh_ref[h] = jnp.where(keep, k_ref[...], 0).astype(BF16)
            vh_ref[h] = jnp.where(keep, v_ref[...], 0).astype(BF16)
        summat = _sum_matrix("after")
        strict = _strict_mask()

        def one_pass(q, row0, nsub, diag, state):
            rows = pl.ds(row0, nsub * TILE)
            z, sp, pieces = [], [], []
            for h in range(2):
                zh = lax.dot_general(q, kh_ref[h, rows, :], NT_DIMS, preferred_element_type=F32)
                sph = _softplus(zh)
                logkeep = jnp.where(strict, -sph, 0.0) if diag else -sph
                z.append(zh)
                sp.append(sph)
                pieces += [logkeep[:, b * TILE:(b + 1) * TILE] for b in range(nsub)]
            sums = jnp.dot(jnp.concatenate(pieces, axis=0).astype(BF16), summat, preferred_element_type=F32)
            out = []
            for h in range(2):
                carry, acc = state[2 * h], state[2 * h + 1]
                after = [None] * nsub
                for b in reversed(range(nsub)):
                    part = sums[(h * nsub + b) * SB_BLOCK:(h * nsub + b + 1) * SB_BLOCK]
                    after[b] = part[:, :TILE] + carry
                    carry = carry + part[:, TILE:]
                w = jnp.exp(z[h] - sp[h] + jnp.concatenate(after, axis=1))
                if diag:
                    w = jnp.where(strict, w, 0.0)
                out += [carry, acc + jnp.dot(w.astype(BF16), vh_ref[h, rows, :], preferred_element_type=F32)]
            return tuple(out)

        def q_block(i, _):
            r0 = pl.multiple_of(i * SB_BLOCK, SB_BLOCK)
            q = q_ref[pl.ds(r0, SB_BLOCK), :] * SB_SCALE
            zero = jnp.zeros((SB_BLOCK, TILE), F32)
            state = one_pass(q, r0, SB_SUB, True, (zero,) * 4)
            state = lax.fori_loop(
                0, i // 2,
                lambda jj, st: one_pass(q, pl.multiple_of((i - 2 - 2 * jj) * SB_BLOCK, SB_BLOCK), 2 * SB_SUB, False, st),
                state)
            state = lax.cond(i % 2 == 1, lambda st: one_pass(q, 0, SB_SUB, False, st), lambda st: st, state)
            o_ref[pl.ds(r0, SB_BLOCK), :] = (state[1] + state[3]).astype(BF16)
            lt_ref[pl.ds(r0, SB_BLOCK), :] = jnp.where(_head_lanes(0), state[0], state[2])
            return 0

        lax.fori_loop(0, t_len // SB_BLOCK, q_block, 0)
        pl.when(step == last_step)(finish)

    def col(k):
        return pl.BlockSpec((t_len, TILE), lambda s, hp: (s, k * npair + hp))

    out = pl.BlockSpec((t_len, TILE), lambda s, hp: (s, hp))
    res = pl.pallas_call(
        body, name="stickbreak_fwd", grid=(nseq, npair), in_specs=[col(2), col(3), col(4)] + [ANY] * ng,
        out_specs=[out, out] + [ANY] * ng,
        out_shape=[jax.ShapeDtypeStruct((m, npair * TILE), BF16), jax.ShapeDtypeStruct((m, npair * TILE), F32)]
        + [jax.ShapeDtypeStruct(b.shape, b.dtype) for b in gather],
        input_output_aliases={3 + a: 2 + a for a in range(ng)},
        scratch_shapes=[pltpu.VMEM((2, t_len, TILE), BF16), pltpu.VMEM((2, t_len, TILE), BF16)] + _gather_sems(ng),
        compiler_params=pltpu.CompilerParams(dimension_semantics=("arbitrary", "arbitrary"),
                                             vmem_limit_bytes=VMEM_LIMIT_BYTES, has_side_effects=True),
    )(p, p, p, *gather)
    return res[0], res[1], res[2:]


def _sb_bwd(p, dmix, ltot, nseq, t_len, exchange):
    m = p.shape[0]
    npair = 4
    ne = len(exchange)
    last_step = nseq * npair - 1

    def body(q_ref, k_ref, v_ref, do_ref, lt_ref, *rest):
        dq_ref, dk_ref, dv_ref = rest[ne:ne + 3]
        kh_ref, vh_ref, dk_acc, dv_acc = rest[2 * ne + 3:2 * ne + 7]
        step = pl.program_id(0) * npair + pl.program_id(1)
        send, finish = _exchange_steps(rest[:ne], rest[ne + 3:2 * ne + 3], *rest[2 * ne + 7:])
        pl.when(step == 0)(send)
        for h in range(2):
            keep = _head_lanes(h)
            kh_ref[h] = jnp.where(keep, k_ref[...], 0).astype(BF16)
            vh_ref[h] = jnp.where(keep, v_ref[...], 0).astype(BF16)
        dk_acc[...] = jnp.zeros_like(dk_acc)
        dv_acc[...] = jnp.zeros_like(dv_acc)
        sum_upto = _sum_matrix("upto")
        sum_before = _sum_matrix("before")
        strict = _strict_mask()
        lane = lax.broadcasted_iota(jnp.int32, (SB_BLOCK, TILE), 1)

        def running(x, matrix, start, nsub):
            pieces = [x[h][:, b * TILE:(b + 1) * TILE] for h in range(2) for b in range(nsub)]
            sums = jnp.dot(jnp.concatenate(pieces, axis=0).astype(BF16), matrix, preferred_element_type=F32)
            wide, ends = [], []
            for h in range(2):
                total, cols = start[h], []
                for b in range(nsub):
                    part = sums[(h * nsub + b) * SB_BLOCK:(h * nsub + b + 1) * SB_BLOCK]
                    cols.append(part[:, :TILE] + total)
                    total = total + part[:, TILE:]
                wide.append(jnp.concatenate(cols, axis=1))
                ends.append(total)
            return wide, ends

        def one_pass(q, do, qh, doh, ltot, row0, nsub, diag, state):
            rows = pl.ds(row0, nsub * TILE)
            z, sp, logkeep = [], [], []
            for h in range(2):
                zh = lax.dot_general(q, kh_ref[h, rows, :], NT_DIMS, preferred_element_type=F32)
                sph = _softplus(zh)
                z.append(zh)
                sp.append(sph)
                logkeep.append(jnp.where(strict, -sph, 0.0) if diag else -sph)
            upto, sum_l = running(logkeep, sum_upto, [state[0], state[3]], nsub)
            w, g = [], []
            for h in range(2):
                wh = jnp.exp(z[h] - sp[h] + (ltot[h] - upto[h]))
                if diag:
                    wh = jnp.where(strict, wh, 0.0)
                w.append(wh)
                g.append(wh * lax.dot_general(do, vh_ref[h, rows, :], NT_DIMS, preferred_element_type=F32))
            g_before, sum_g = running(g, sum_before, [state[1], state[4]], nsub)
            out, dk_new, dv_new = [], 0.0, 0.0
            for h in range(2):
                dz = g[h] - jnp.exp(z[h] - sp[h]) * (g[h] + g_before[h])
                if diag:
                    dz = jnp.where(strict, dz, 0.0)
                dzb = dz.astype(BF16)
                dq = state[3 * h + 2] + jnp.dot(dzb, kh_ref[h, rows, :], preferred_element_type=F32)
                dk_new = dk_new + lax.dot_general(dzb, qh[h], TN_DIMS, preferred_element_type=F32)
                dv_new = dv_new + lax.dot_general(w[h].astype(BF16), doh[h], TN_DIMS, preferred_element_type=F32)
                out += [sum_l[h], sum_g[h], dq]
            dk_acc[rows, :] += dk_new
            dv_acc[rows, :] += dv_new
            return tuple(out)

        def q_block(i, _):
            r0 = pl.multiple_of(i * SB_BLOCK, SB_BLOCK)
            q = q_ref[pl.ds(r0, SB_BLOCK), :] * SB_SCALE
            do = do_ref[pl.ds(r0, SB_BLOCK), :]
            lt = lt_ref[pl.ds(r0, SB_BLOCK), :]
            qh, doh, ltot = [], [], []
            for h in range(2):
                keep = _head_lanes(h)
                qh.append(jnp.where(keep, q, 0).astype(BF16))
                doh.append(jnp.where(keep, do, 0).astype(BF16))
                ltot.append(jnp.sum(jnp.where(lane == h * SB_DH, lt, 0.0), axis=1, keepdims=True))
            zero = jnp.zeros((SB_BLOCK, TILE), F32)
            state = lax.fori_loop(
                0, i // 2,
                lambda jj, st: one_pass(q, do, qh, doh, ltot, pl.multiple_of(2 * jj * SB_BLOCK, SB_BLOCK),
                                        2 * SB_SUB, False, st),
                (zero,) * 6)
            state = lax.cond(
                i % 2 == 1,
                lambda st: one_pass(q, do, qh, doh, ltot, pl.multiple_of((i - 1) * SB_BLOCK, SB_BLOCK), SB_SUB, False, st),
                lambda st: st, state)
            state = one_pass(q, do, qh, doh, ltot, r0, SB_SUB, True, state)
            dq_ref[pl.ds(r0, SB_BLOCK), :] = ((state[2] + state[5]) * SB_SCALE).astype(BF16)
            return 0

        lax.fori_loop(0, t_len // SB_BLOCK, q_block, 0)
        dk_ref[...] = dk_acc[...].astype(BF16)
        dv_ref[...] = dv_acc[...].astype(BF16)
        pl.when(step == last_step)(finish)

    def col(k):
        return pl.BlockSpec((t_len, TILE), lambda s, hp: (s, k * npair + hp))

    out = pl.BlockSpec((t_len, TILE), lambda s, hp: (s, hp))
    width = npair * TILE
    res = pl.pallas_call(
        body, name="stickbreak_bwd", grid=(nseq, npair),
        in_specs=[col(2), col(3), col(4), col(1), out] + [ANY] * ne, out_specs=[out, out, out] + [ANY] * ne,
        out_shape=[jax.ShapeDtypeStruct((m, width), BF16)] * 3 + _exchange_shapes(exchange),
        scratch_shapes=[pltpu.VMEM((2, t_len, TILE), BF16), pltpu.VMEM((2, t_len, TILE), BF16),
                        pltpu.VMEM((t_len, TILE), F32), pltpu.VMEM((t_len, TILE), F32)] + _exchange_sems(ne),
        compiler_params=pltpu.CompilerParams(dimension_semantics=("arbitrary", "arbitrary"),
                                             vmem_limit_bytes=VMEM_LIMIT_BYTES, has_side_effects=True),
    )(p, p, p, dmix, ltot, *exchange)
    return res[0], res[1], res[2], res[3:]


def _adam_math(w, g, m, v):
    m = ADAM_B1 * m + (1.0 - ADAM_B1) * g
    v = ADAM_B2 * v + (1.0 - ADAM_B2) * (g * g)
    m_hat = m / (1.0 - ADAM_B1 ** ADAM_STEP)
    v_hat = v / (1.0 - ADAM_B2 ** ADAM_STEP)
    delta = -ADAM_LR * (m_hat / (jnp.sqrt(v_hat) + ADAM_EPS) + ADAM_WD * w)
    return delta, m, v


def _cast_place(w, layer, pos, *, name):
    _, r, c = w.shape
    tr = min(r, 256)

    def body(pos_ref, w_ref, o_ref):
        o_ref[...] = w_ref[...].astype(BF16)

    grid_spec = pltpu.PrefetchScalarGridSpec(
        num_scalar_prefetch=1, grid=(r // tr,),
        in_specs=[pl.BlockSpec((None, tr, c), lambda i, pos_ref: (layer, i, 0))],
        out_specs=pl.BlockSpec((None, None, tr, c), lambda i, pos_ref: (0, pos_ref[0], i, 0)))
    return pl.pallas_call(
        body, name=name, grid_spec=grid_spec, out_shape=jax.ShapeDtypeStruct((1, N_CHIP, r, c), BF16),
        compiler_params=_params(("parallel",)),
    )(pos, w)


def _pair_sum(mine, got, pos, *, name):
    l_dim, s_dim, h, c = got.shape
    th = min(h, 512)
    nt = h // th

    def body(pos_ref, a_ref, b_ref, o_ref):
        o_ref[...] = (a_ref[...].astype(F32) + b_ref[...].astype(F32)).astype(BF16)

    spec = pl.BlockSpec((None, None, th, c), lambda l, s, i, pos_ref: (l, s, i, 0))
    grid_spec = pltpu.PrefetchScalarGridSpec(
        num_scalar_prefetch=1, grid=(l_dim, s_dim, nt),
        in_specs=[pl.BlockSpec((None, None, th, c), lambda l, s, i, pos_ref: (l, s, pos_ref[1] * nt + i, 0)), spec],
        out_specs=spec)
    return pl.pallas_call(
        body, name=name, grid_spec=grid_spec, out_shape=jax.ShapeDtypeStruct(got.shape, BF16),
        compiler_params=_params(("parallel",) * 3),
    )(pos, mine, got)


def _chip_sum(sums, landed, pos, *, name):
    l_dim, _, h, c = sums.shape
    th = min(h, 512)
    nt = h // th

    def body(pos_ref, own, r0, r1, r2, o_ref):
        o_ref[...] = ((own[...].astype(F32) + r0[...].astype(F32)) + r1[...].astype(F32)) + r2[...].astype(F32)

    def piece(k):
        return pl.BlockSpec((None, None, th, c), lambda l, i, pos_ref: (l, k, i, 0))

    grid_spec = pltpu.PrefetchScalarGridSpec(
        num_scalar_prefetch=1, grid=(l_dim, nt),
        in_specs=[pl.BlockSpec((None, None, th, c), lambda l, i, pos_ref: (l, pos_ref[0], i, 0)),
                  piece(0), piece(1), piece(2)],
        out_specs=pl.BlockSpec((None, th, c), lambda l, i, pos_ref: (l, pos_ref[1] * nt + i, 0)))
    return pl.pallas_call(
        body, name=name, grid_spec=grid_spec, out_shape=jax.ShapeDtypeStruct((l_dim, 2 * h, c), F32),
        compiler_params=_params(("parallel",) * 2),
    )(pos, sums, landed, landed, landed)


def _adam_big(w, m, v, grads, *, name):
    l_dim, r, c = w.shape
    assert len(grads) == l_dim
    tr = min(r, 256)

    def body(*refs):
        w_ref, m_ref, v_ref = refs[:3]
        g_refs = refs[3:3 + l_dim]
        go_ref, d_ref, mo_ref, vo_ref = refs[3 + l_dim:]
        g = g_refs[0][...]
        for l in range(1, l_dim):
            g = jnp.where(pl.program_id(0) == l, g_refs[l][...], g)
        delta, m_new, v_new = _adam_math(w_ref[...], g, m_ref[...], v_ref[...])
        go_ref[...] = g
        d_ref[...] = delta
        mo_ref[...] = m_new
        vo_ref[...] = v_new

    spec = pl.BlockSpec((None, tr, c), lambda l, i: (l, i, 0))
    gspec = pl.BlockSpec((None, tr, c), lambda l, i: (0, i, 0))
    return pl.pallas_call(
        body, name=name, grid=(l_dim, r // tr), in_specs=[spec] * 3 + [gspec] * l_dim, out_specs=[spec] * 4,
        out_shape=[jax.ShapeDtypeStruct(w.shape, F32)] * 4, compiler_params=_params(("parallel",) * 2),
    )(w, m, v, *grads)


def _position():
    return lax.axis_index("x"), lax.axis_index("y"), lax.axis_index("c")


def _other_chips(x, y):
    return [(1 - x, y), (x, 1 - y), (1 - x, 1 - y)]


def _remote(src, dst, send_sem, recv_sem, device):
    return pltpu.make_async_remote_copy(src_ref=src, dst_ref=dst, send_sem=send_sem, recv_sem=recv_sem,
                                        device_id=device, device_id_type=MESH)


ANY = pl.BlockSpec(memory_space=pl.ANY)


def _gather_sems(n):
    return [pltpu.SemaphoreType.DMA((3 * n,))] * 4


def _gather_steps(outs, send_sems, recv_sems, fwd_send, fwd_recv):
    n = len(outs)
    x, y, c = _position()
    chips = _other_chips(x, y)
    sibling = (x, y, 1 - c)

    def half(a, chip, core):
        h = outs[a].shape[2] // 2
        return outs[a].at[:, 2 * chip[0] + chip[1], pl.ds(core * h, h), :]

    def over_ici(a, k, chip):
        block = half(a, chip, c)
        return _remote(block, block, send_sems.at[3 * a + k], recv_sems.at[3 * a + k], (*chips[k], c))

    def over_d2d(a, k, core):
        block = half(a, chips[k], core)
        return _remote(block, block, fwd_send.at[3 * a + k], fwd_recv.at[3 * a + k], sibling)

    def send():
        for a in range(n):
            for k in range(3):
                over_ici(a, k, (x, y)).start()

    def forward():
        for k in range(3):
            for a in range(n):
                over_ici(a, k, chips[k]).wait_recv()
                over_d2d(a, k, c).start()

    def finish():
        for k in range(3):
            for a in range(n):
                over_d2d(a, k, 1 - c).wait_recv()
        for a in range(n):
            for k in range(3):
                over_ici(a, k, (x, y)).wait_send()
                over_d2d(a, k, c).wait_send()

    return send, forward, finish


def _swap_halves(grads, *, name):
    n = len(grads)

    def body(*refs):
        send, finish = _swap_steps(refs[:n], refs[n:2 * n], *refs[2 * n:])
        send()
        finish()

    sem = pltpu.SemaphoreType.DMA((n,))
    return pl.pallas_call(
        body, name=name, in_specs=[ANY] * n, out_specs=[ANY] * n, out_shape=_swap_shapes(grads),
        scratch_shapes=[sem, sem], compiler_params=pltpu.CompilerParams(has_side_effects=True),
    )(*grads)


def _swap_shapes(grads):
    return [jax.ShapeDtypeStruct(g.shape[:2] + (g.shape[2] // 2, g.shape[3]), g.dtype) for g in grads]


def _swap_steps(ins, outs, send_sems, recv_sems):
    x, y, c = _position()

    def copy(a):
        h = ins[a].shape[2] // 2
        return _remote(ins[a].at[:, :, pl.ds((1 - c) * h, h), :], outs[a], send_sems.at[a], recv_sems.at[a],
                       (x, y, 1 - c))

    def send():
        for a in range(len(ins)):
            copy(a).start()

    def finish():
        for a in range(len(ins)):
            copy(a).wait()

    return send, finish


def _exchange_shapes(sums):
    return [jax.ShapeDtypeStruct((s.shape[0], 3) + s.shape[2:], s.dtype) for s in sums]


def _exchange_sems(n):
    return [pltpu.SemaphoreType.DMA((3 * n,))] * 2


def _exchange_steps(ins, outs, send_sems, recv_sems):
    n = len(ins)
    x, y, c = _position()
    chips = _other_chips(x, y)

    def copy(a, k):
        chip = chips[k]
        return _remote(ins[a].at[:, 2 * chip[0] + chip[1]], outs[a].at[:, k],
                       send_sems.at[3 * a + k], recv_sems.at[3 * a + k], (*chip, c))

    def send():
        for a in range(n):
            for k in range(3):
                copy(a, k).start()

    def finish():
        for a in range(n):
            for k in range(3):
                copy(a, k).wait()

    return send, finish


def _join_halves(bufs, *, name):
    n = len(bufs)

    def body(*refs):
        send, finish = _join_steps(refs[n:2 * n], *refs[2 * n:])
        send()
        finish()

    sem = pltpu.SemaphoreType.DMA((n,))
    return pl.pallas_call(
        body, name=name, in_specs=[ANY] * n, out_specs=[ANY] * n,
        out_shape=[jax.ShapeDtypeStruct(b.shape, b.dtype) for b in bufs],
        input_output_aliases={a: a for a in range(n)},
        scratch_shapes=[sem, sem], compiler_params=pltpu.CompilerParams(has_side_effects=True),
    )(*bufs)


def _join_steps(outs, send_sems, recv_sems):
    x, y, c = _position()

    def copy(a, core):
        h = outs[a].shape[1] // 2
        half = outs[a].at[:, pl.ds(core * h, h), :]
        return _remote(half, half, send_sems.at[a], recv_sems.at[a], (x, y, 1 - c))

    def send():
        for a in range(len(outs)):
            copy(a, c).start()

    def finish():
        for a in range(len(outs)):
            copy(a, c).wait_send()
            copy(a, 1 - c).wait_recv()

    return send, finish


def _allgather_steps(ins, outs, send_sems, recv_sems, local_sems):
    n = len(ins)
    x, y, c = _position()
    me, sibling = (x, y, c), (x, y, 1 - c)
    chips = _other_chips(x, y)

    def slot(a, dev):
        return outs[a].at[4 * dev[0] + 2 * dev[1] + dev[2]]

    def copy(a, k, block, to, own=False):
        return _remote(ins[a] if own else slot(a, block), slot(a, block),
                       send_sems.at[7 * a + k], recv_sems.at[7 * a + k], to)

    def first(a):
        return [copy(a, 0, me, sibling, own=True)] + [copy(a, 1 + k, me, (*chips[k], c), own=True) for k in range(3)]

    def local(a):
        return pltpu.make_async_copy(ins[a], slot(a, me), local_sems.at[a])

    def send():
        for a in range(n):
            local(a).start()
            for cp in first(a):
                cp.start()

    def forward():
        for a in range(n):
            for k in range(3):
                copy(a, 1 + k, (*chips[k], c), me).wait_recv()
                copy(a, 4 + k, (*chips[k], c), sibling).start()

    def finish():
        for a in range(n):
            copy(a, 0, sibling, me).wait_recv()
            for k in range(3):
                copy(a, 4 + k, (*chips[k], 1 - c), me).wait_recv()
        for a in range(n):
            for cp in first(a) + [copy(a, 4 + k, (*chips[k], c), sibling) for k in range(3)]:
                cp.wait_send()
            local(a).wait()

    return send, forward, finish


def _allreduce_small(packs):
    n = len(packs)

    def body(*refs):
        ins, outs, gath = refs[:n], refs[n:2 * n], refs[2 * n:3 * n]
        send_sems, recv_sems = refs[3 * n:]
        x, y, c = _position()
        me, sibling = (x, y, c), (x, y, 1 - c)
        chips = _other_chips(x, y)

        def slot(a, dev):
            return gath[a].at[4 * dev[0] + 2 * dev[1] + dev[2]]

        def copy(a, k, block, to, src=None):
            return _remote(slot(a, block) if src is None else src, slot(a, block),
                           send_sems.at[7 * a + k], recv_sems.at[7 * a + k], to)

        started = []
        for a in range(n):
            slot(a, me)[...] = ins[a][...]
            first = [copy(a, 0, me, sibling, src=ins[a])]
            first += [copy(a, 1 + k, me, (*chip, c), src=ins[a]) for k, chip in enumerate(chips)]
            for cp in first:
                cp.start()
            started += first
        for a in range(n):
            for k, chip in enumerate(chips):
                copy(a, 1 + k, (*chip, c), me).wait_recv()
                cp = copy(a, 4 + k, (*chip, c), sibling)
                cp.start()
                started.append(cp)
        for a in range(n):
            copy(a, 0, sibling, me).wait_recv()
            for k, chip in enumerate(chips):
                copy(a, 4 + k, (*chip, 1 - c), me).wait_recv()
        for cp in started:
            cp.wait_send()
        for a in range(n):
            total = gath[a][0]
            for d in range(1, N_DEV):
                total = total + gath[a][d]
            outs[a][...] = total

    vmem = pl.BlockSpec(memory_space=pltpu.VMEM)
    sem = pltpu.SemaphoreType.DMA((7 * n,))
    return pl.pallas_call(
        body, name="allreduce_small", in_specs=[vmem] * n, out_specs=[vmem] * n,
        out_shape=[jax.ShapeDtypeStruct(p.shape, p.dtype) for p in packs],
        scratch_shapes=[pltpu.VMEM((N_DEV,) + p.shape, p.dtype) for p in packs] + [sem, sem],
        compiler_params=pltpu.CompilerParams(has_side_effects=True, vmem_limit_bytes=VMEM_LIMIT_BYTES),
    )(*packs)


LOSS_ROW = 1040


def _pad_rows(a, rows=8):
    return jnp.concatenate([a, jnp.zeros((rows - a.shape[0], a.shape[1]), a.dtype)], axis=0)

def _adam_small(wide, mid, narrow, late, params):
    names = ["mix_norm_g", "mlp_norm_g", "final_norm_g", "conv_b", "conv_w", "sgu_norm_g", "sgu_norm_b",
             "pool_w", "pool_scale", "sgu_w", "sgu_b"]
    n = len(names)

    def body(*refs):
        wmv = refs[4:4 + 3 * n]
        outs = refs[4 + 3 * n:]
        x, y, _ = _position()
        q = 2 * x + y

        def total(ref):
            t = ref[0]
            for dev in range(1, N_DEV):
                t = t + ref[dev]
            return t

        wide_sum, mid_sum, narrow_sum = total(refs[0]), total(refs[1]), total(refs[2])
        late_ref = refs[3]

        def my_quarter(rows):
            parts = [rows[:, s * TILE:(s + 1) * TILE] for s in range(N_CHIP)]
            return jnp.where(q == 0, parts[0], jnp.where(q == 1, parts[1], jnp.where(q == 2, parts[2], parts[3])))

        def tiles(first_row):
            return [((0, g), narrow_sum[first_row + g * TILE:first_row + (g + 1) * TILE, :]) for g in range(4)]

        grads = {
            "mix_norm_g": [((), wide_sum[0:2, :] + late_ref[0:2, :])],
            "mlp_norm_g": [((), wide_sum[8:10, :])],
            "final_norm_g": [((), wide_sum[16:17, :])],
            "conv_b": [((), mid_sum[0:1, :])],
            "conv_w": [((0,), my_quarter(mid_sum[8:11, :]))],
            "sgu_norm_g": [((), my_quarter(mid_sum[16:17, :]))],
            "sgu_norm_b": [((), my_quarter(mid_sum[24:25, :]))],
            "pool_w": tiles(0),
            "sgu_w": tiles(512),
            "pool_scale": [((0,), narrow_sum[1024:1028, :])],
            "sgu_b": [((0,), narrow_sum[1032:1036, :])],
        }
        outs[4 * n][...] = narrow_sum[LOSS_ROW:LOSS_ROW + 8, :]
        for i, name in enumerate(names):
            w_ref, m_ref, v_ref = wmv[3 * i:3 * i + 3]
            for lead, g in grads[name]:
                idx = lead + (slice(None), slice(None))
                delta, m_new, v_new = _adam_math(w_ref[idx], g, m_ref[idx], v_ref[idx])
                outs[4 * i][idx] = g
                outs[4 * i + 1][idx] = delta
                outs[4 * i + 2][idx] = m_new
                outs[4 * i + 3][idx] = v_new

    vmem = pl.BlockSpec(memory_space=pltpu.VMEM)
    args, out_shape = [wide, mid, narrow, late], []
    for name in names:
        w, m, v = params[name]
        args += [w, m, v]
        out_shape += [jax.ShapeDtypeStruct(w.shape, F32)] * 4
    out_shape.append(jax.ShapeDtypeStruct((8, TILE), F32))
    res = pl.pallas_call(
        body, name="adam_small", in_specs=[vmem] * len(args), out_specs=[vmem] * len(out_shape),
        out_shape=out_shape, compiler_params=pltpu.CompilerParams(vmem_limit_bytes=VMEM_LIMIT_BYTES),
    )(*args)
    return {name: res[4 * i:4 * i + 4] for i, name in enumerate(names)}, res[4 * n]


def _pair_sums(grads, got, pos, tag):
    return [_pair_sum(a, b, pos, name=f"pair_sum_{tag}{i}") for i, (a, b) in enumerate(zip(grads, got))]


def _chip_sums(sums, landed, pos, tag):
    return [_chip_sum(s, r, pos, name=f"chip_sum_{tag}{i}") for i, (s, r) in enumerate(zip(sums, landed))]


def kernel(x, mix_norm_g, mlp_norm_g, ab_w_in, pool_w, pool_scale, conv_w, conv_b, ab_w_out, cd_w_in, sgu_norm_g, sgu_norm_b, sgu_w, sgu_b, cd_w_out, mlp_w1, mlp_w2, final_norm_g, loss_target, m_mix_norm_g, m_mlp_norm_g, m_ab_w_in, m_pool_w, m_pool_scale, m_conv_w, m_conv_b, m_ab_w_out, m_cd_w_in, m_sgu_norm_g, m_sgu_norm_b, m_sgu_w, m_sgu_b, m_cd_w_out, m_mlp_w1, m_mlp_w2, m_final_norm_g, v_mix_norm_g, v_mlp_norm_g, v_ab_w_in, v_pool_w, v_pool_scale, v_conv_w, v_conv_b, v_ab_w_out, v_cd_w_in, v_sgu_norm_g, v_sgu_norm_b, v_sgu_w, v_sgu_b, v_cd_w_out, v_mlp_w1, v_mlp_w2, v_final_norm_g):
    nseq, t_len, d = x.shape
    m_tok = nseq * t_len
    h0 = x.reshape(m_tok, d)
    target = loss_target.reshape(m_tok, d)

    x_idx, y_idx = lax.axis_index("x"), lax.axis_index("y")
    q_idx = 2 * x_idx + y_idx
    pos = jnp.stack([q_idx, lax.axis_index("c")]).astype(jnp.int32)
    def shard_buffer(w, layer, tag):
        return _cast_place(w, layer, pos, name=f"cast_place_{tag}")

    def row_block(w):
        return w.reshape(1, 1, -1, w.shape[-1])

    later_weights = [shard_buffer(cd_w_out, 0, "cd_out"), shard_buffer(mlp_w1, 1, "w1_1"),
                     shard_buffer(mlp_w2, 1, "w2_1")]

    pool_w3, pool_scale3 = pool_w[0], pool_scale[0].reshape(4, 1, TILE)
    sgu_w3 = sgu_w[0]
    sgu_w3_t = jnp.swapaxes(sgu_w3, 1, 2)
    sgu_bias_tile = jnp.broadcast_to(sgu_b[0][:, :, None], (4, TILE, TILE))
    conv_w2, conv_b2 = conv_w[0], conv_b
    def place_quarter(v):
        return lax.dynamic_update_slice(jnp.zeros((v.shape[0], 4 * TILE), F32), v, (0, q_idx * TILE))

    sharded_small = jnp.concatenate(
        [place_quarter(conv_w[0]), place_quarter(sgu_norm_g), place_quarter(sgu_norm_b),
         jnp.zeros((3, 4 * TILE), F32)], axis=0)
    sharded_small, = _allreduce_small([sharded_small])
    sharded_small = sharded_small * 0.5
    conv_w_full = sharded_small[0:3]
    sgu_g_full = sharded_small[3:4]
    sgu_b_full = sharded_small[4:5]

    xn0, ((w_ab_in,),) = _rms_fwd(h0, mix_norm_g[0:1], name="rms_fwd_mix0",
                                  rider=[("gather", [shard_buffer(ab_w_in, 0, "ab_in")])])
    p_ab, ((w_1_0,),) = _mm_nn(xn0, w_ab_in, 0, out_dtype=BF16, name="ab_in_proj",
                               rider=[("gather", [shard_buffer(mlp_w1, 0, "w1_0")])])
    mix0, ((w_ab_out,),) = _ab_fwd(p_ab, pool_w3, pool_scale3, conv_w_full, conv_b2, nseq, t_len,
                                   rider=[("gather", [shard_buffer(ab_w_out, 0, "ab_out")])])
    w_ab_out = row_block(w_ab_out)
    h1, hn0 = _mm_nn(mix0, w_ab_out, 0, out_dtype=F32, name="ab_out_proj", epilogue="residual", extra=h0,
                     norm_g=mlp_norm_g[0:1])
    (act0, relu0), ((w_2_0,),) = _mm_nn(hn0, w_1_0, 0, out_dtype=BF16, name="mlp0_up", epilogue="relu2",
                                        rider=[("gather", [shard_buffer(mlp_w2, 0, "w2_0")])])
    w_2_0 = row_block(w_2_0)
    (h2, xn1), ((w_cd_in,),) = _mm_nn(act0, w_2_0, 0, out_dtype=F32, name="mlp0_down", epilogue="residual", extra=h1,
                                      norm_g=mix_norm_g[1:2],
                                      rider=[("gather", [shard_buffer(cd_w_in, 0, "cd_in")])])

    p_cd = _mm_nn(xn1, w_cd_in, 0, out_dtype=BF16, name="cd_in_proj")
    c_out = _sgu_fwd(p_cd, sgu_g_full, sgu_b_full, sgu_w3, sgu_bias_tile)
    d_out, ltot, (w_cd_out, w_1_1, w_2_1) = _sb_fwd(p_cd, nseq, t_len, later_weights)
    w_cd_out, w_2_1 = row_block(w_cd_out), row_block(w_2_1)
    mix1 = jnp.concatenate([c_out, d_out], axis=1)
    h3, hn1 = _mm_nn(mix1, w_cd_out, 0, out_dtype=F32, name="cd_out_proj", epilogue="residual", extra=h2,
                     norm_g=mlp_norm_g[1:2])
    act1, relu1 = _mm_nn(hn1, w_1_1, 0, out_dtype=BF16, name="mlp1_up", epilogue="relu2")

    dh4, dh4_bf, dg_final, loss_tile = _mlp_down_loss(act1, w_2_1, h3, final_norm_g.reshape(1, d), target)

    def as_pieces(g):
        return g.reshape(1, N_CHIP, -1, g.shape[-1]) if g.shape[1] == 1 else g

    dz1 = _mm_nt(dh4_bf, w_2_1, 0, out_dtype=BF16, name="mlp1_down_bwd", epilogue="relu2_bwd", extra=relu1)
    g_w2_1 = as_pieces(_mm_tn(act1, dh4_bf, 1, name="mlp1_down_wgrad"))
    g_w1_1 = _mm_tn(hn1, dz1, N_CHIP, name="mlp1_up_wgrad")
    (dh3, dh3_bf, dg_mlp1), (got_a,) = _mm_nt(
        dz1, w_1_1, 0, out_dtype=F32, name="mlp1_up_bwd", epilogue="rms_bwd",
        extra=(h3, mlp_norm_g[1:2], dh4), rider=[("swap", [g_w1_1, g_w2_1])])

    g_cd_out = as_pieces(_mm_tn(mix1, dh3_bf, 1, name="cd_out_wgrad"))
    dmix1, (got_cd_out,) = _mm_nt(dh3_bf, w_cd_out, 0, out_dtype=BF16, name="cd_out_bwd",
                                  rider=[("swap", [g_cd_out])])
    sums_a = _pair_sums([g_w1_1, g_w2_1, g_cd_out], got_a + got_cd_out, pos, "a")
    du, dv, dsgu_w, dsgu_bs, dsgu_g, dsgu_b = _sgu_bwd(p_cd, dmix1, sgu_g_full, sgu_b_full, sgu_w3, sgu_w3_t,
                                                      sgu_bias_tile)
    dq, dk, dvv, landed_a = _sb_bwd(p_cd, dmix1, ltot, nseq, t_len, sums_a)
    halves_a = _chip_sums(sums_a, landed_a, pos, "a")
    dp_cd = jnp.concatenate([du, dv, dq, dk, dvv], axis=1)
    g_cd_in, ((r_w1_1, r_w2_1, r_cd_out),) = _mm_tn(xn1, dp_cd, N_CHIP, name="cd_in_wgrad",
                                                    rider=[("join", halves_a)])
    (dh2, dh2_bf, dg_mix1), (got_c,) = _mm_nt(
        dp_cd, w_cd_in, 0, out_dtype=F32, name="cd_in_bwd", epilogue="rms_bwd",
        extra=(h2, mix_norm_g[1:2], dh3), rider=[("swap", [g_cd_in])])

    sums_c = _pair_sums([g_cd_in], got_c, pos, "c")
    dz0, (landed_c,) = _mm_nt(dh2_bf, w_2_0, 0, out_dtype=BF16, name="mlp0_down_bwd", epilogue="relu2_bwd",
                              extra=relu0, rider=[("exchange", sums_c)])
    halves_c = _chip_sums(sums_c, landed_c, pos, "c")
    g_w2_0, ((r_cd_in,),) = _mm_tn(act0, dh2_bf, 1, name="mlp0_down_wgrad", rider=[("join", halves_c)])
    g_w2_0 = as_pieces(g_w2_0)
    g_w1_0, (got_d,) = _mm_tn(hn0, dz0, N_CHIP, name="mlp0_up_wgrad", rider=[("swap", [g_w2_0])])
    sums_d = _pair_sums([g_w2_0], got_d, pos, "d")
    (dh1, dh1_bf, dg_mlp0), (landed_d, got_e) = _mm_nt(
        dz0, w_1_0, 0, out_dtype=F32, name="mlp0_up_bwd", epilogue="rms_bwd",
        extra=(h1, mlp_norm_g[0:1], dh2), rider=[("exchange", sums_d), ("swap", [g_w1_0])])
    halves_d = _chip_sums(sums_d, landed_d, pos, "d")
    sums_e = _pair_sums([g_w1_0], got_e, pos, "e")

    dmix0, ((r_w2_0,),) = _mm_nt(dh1_bf, w_ab_out, 0, out_dtype=BF16, name="ab_out_bwd", rider=[("join", halves_d)])
    g_ab_out = as_pieces(_mm_tn(mix0, dh1_bf, 1, name="ab_out_wgrad"))
    (da, dxb, dgb, dgc, dpool_w, dpool_scale, dconv_w, dconv_b), (landed_e, got_f) = _ab_bwd(
        p_ab, dmix0, pool_w3, pool_scale3, conv_w_full, conv_b2, nseq, t_len,
        rider=[("exchange", sums_e), ("swap", [g_ab_out])])
    halves_e = _chip_sums(sums_e, landed_e, pos, "e")
    sums_f = _pair_sums([g_ab_out], got_f, pos, "f")
    dp_ab = jnp.concatenate([da, dxb, dgb, dgc], axis=1)
    wide = jnp.concatenate([_pad_rows(jnp.concatenate([jnp.zeros_like(dg_mix1), dg_mix1], axis=0)),
                            _pad_rows(jnp.concatenate([dg_mlp0, dg_mlp1], axis=0)), _pad_rows(dg_final)], axis=0)
    mid = jnp.concatenate([_pad_rows(dconv_b), _pad_rows(dconv_w), _pad_rows(dsgu_g), _pad_rows(dsgu_b)], axis=0)
    narrow = jnp.concatenate(
        [dpool_w.reshape(4 * TILE, TILE), dsgu_w.reshape(4 * TILE, TILE), _pad_rows(dpool_scale.reshape(4, TILE)),
         _pad_rows(dsgu_bs[:, :, 0]), loss_tile], axis=0)
    g_ab_in, (landed_f, (r_w1_0,), (wide, mid, narrow)) = _mm_tn(
        xn0, dp_ab, N_CHIP, name="ab_in_wgrad",
        rider=[("exchange", sums_f), ("join", halves_e), ("allgather", [wide, mid, narrow])])
    halves_f = _chip_sums(sums_f, landed_f, pos, "f")
    sums_g = _pair_sums([g_ab_in], _swap_halves([g_ab_in], name="swap_halves_g"), pos, "g")
    (grad_x, _, dg_mix0), (landed_g, (r_ab_out,)) = _mm_nt(
        dp_ab, w_ab_in, 0, out_dtype=F32, name="ab_in_bwd", epilogue="rms_bwd",
        extra=(h0, mix_norm_g[0:1], dh1), rider=[("exchange", sums_g), ("join", halves_f)])
    r_ab_in, = _join_halves(_chip_sums(sums_g, landed_g, pos, "g"), name="join_halves_g")

    big_out = {
        "ab_w_in": _adam_big(ab_w_in, m_ab_w_in, v_ab_w_in, [r_ab_in], name="adam_ab_w_in"),
        "ab_w_out": _adam_big(ab_w_out, m_ab_w_out, v_ab_w_out, [r_ab_out], name="adam_ab_w_out"),
        "cd_w_in": _adam_big(cd_w_in, m_cd_w_in, v_cd_w_in, [r_cd_in], name="adam_cd_w_in"),
        "cd_w_out": _adam_big(cd_w_out, m_cd_w_out, v_cd_w_out, [r_cd_out], name="adam_cd_w_out"),
        "mlp_w1": _adam_big(mlp_w1, m_mlp_w1, v_mlp_w1, [r_w1_0, r_w1_1], name="adam_mlp_w1"),
        "mlp_w2": _adam_big(mlp_w2, m_mlp_w2, v_mlp_w2, [r_w2_0, r_w2_1], name="adam_mlp_w2"),
    }

    late, = _allreduce_small([_pad_rows(dg_mix0)])
    small_out, loss_sum = _adam_small(wide, mid, narrow, late, {
        "mix_norm_g": (mix_norm_g, m_mix_norm_g, v_mix_norm_g),
        "mlp_norm_g": (mlp_norm_g, m_mlp_norm_g, v_mlp_norm_g),
        "final_norm_g": tuple(a.reshape(1, d) for a in (final_norm_g, m_final_norm_g, v_final_norm_g)),
        "conv_b": (conv_b, m_conv_b, v_conv_b),
        "conv_w": (conv_w, m_conv_w, v_conv_w),
        "sgu_norm_g": (sgu_norm_g, m_sgu_norm_g, v_sgu_norm_g),
        "sgu_norm_b": (sgu_norm_b, m_sgu_norm_b, v_sgu_norm_b),
        "pool_w": (pool_w, m_pool_w, v_pool_w),
        "pool_scale": (pool_scale, m_pool_scale, v_pool_scale),
        "sgu_w": (sgu_w, m_sgu_w, v_sgu_w),
        "sgu_b": (sgu_b, m_sgu_b, v_sgu_b),
    })
    small_out["final_norm_g"] = [a.reshape(d) for a in small_out["final_norm_g"]]

    order = ["mix_norm_g", "mlp_norm_g", "ab_w_in", "pool_w", "pool_scale", "conv_w", "conv_b", "ab_w_out",
             "cd_w_in", "sgu_norm_g", "sgu_norm_b", "sgu_w", "sgu_b", "cd_w_out", "mlp_w1", "mlp_w2",
             "final_norm_g"]
    both = {**big_out, **small_out}
    loss = loss_sum[0, 0]
    outs = [loss, grad_x.reshape(nseq, t_len, d)]
    for kind in range(4):
        outs += [both[name][kind] for name in order]
    return tuple(outs)
```

```python
import math

import jax
import jax.numpy as jnp
from jax import lax
from jax.experimental import pallas as pl
from jax.experimental.pallas import tpu as pltpu

F32 = jnp.float32
BF16 = jnp.bfloat16
MESH = pl.DeviceIdType.MESH

EPS = 1e-6
TILE = 128
N_CHIP = 4
N_DEV = 8
VMEM_LIMIT_BYTES = 56 * 1024 * 1024

ADAM_LR = 0.001
ADAM_B1 = 0.9
ADAM_B2 = 0.999
ADAM_EPS = 1e-08
ADAM_WD = 0.01
ADAM_STEP = 10

NT_DIMS = (((1,), (1,)), ((), ()))
TN_DIMS = (((0,), (0,)), ((), ()))


def _params(sem=None):
    return pltpu.CompilerParams(dimension_semantics=sem, vmem_limit_bytes=VMEM_LIMIT_BYTES)


def _call(body, *, name, grid, in_specs, out_specs, out_shape, scratch_shapes, semantics, args, rider=None):
    if not rider:
        res = pl.pallas_call(body, name=name, grid=grid, in_specs=in_specs, out_specs=out_specs, out_shape=out_shape,
                             scratch_shapes=scratch_shapes, compiler_params=_params(semantics))(*args)
        return list(res), []
    plans = [_rider_plan(kind, arrays) for kind, arrays in rider]
    arrays = [a for _, group in rider for a in group]
    nr, n_in, n_out, n_scr = len(arrays), len(in_specs), len(out_specs), len(scratch_shapes)
    first_out, first_scr = n_in + nr, n_in + nr + n_out + nr
    last_step = math.prod(grid) - 1

    def riding(*refs):
        step = 0
        for axis, size in enumerate(grid):
            step = step * size + pl.program_id(axis)
        steps, at, sem_at = [], 0, first_scr + n_scr
        for (kind, group), (_, sems, _) in zip(rider, plans):
            k = len(group)
            steps.append(_rider_steps(kind, refs[n_in + at:n_in + at + k],
                                      refs[first_out + n_out + at:first_out + n_out + at + k],
                                      refs[sem_at:sem_at + len(sems)]))
            at, sem_at = at + k, sem_at + len(sems)
        for send, _, _ in steps:
            pl.when(step == 0)(send)
        for _, forward, _ in steps:
            if forward is not None:
                pl.when(step == last_step)(forward)
        body(*refs[:n_in], *refs[first_out:first_out + n_out], *refs[first_scr:first_scr + n_scr])
        for _, _, finish in steps:
            pl.when(step == last_step)(finish)

    aliases, at = {}, 0
    for (_, group), (_, _, aliased) in zip(rider, plans):
        if aliased:
            aliases.update({n_in + at + a: n_out + at + a for a in range(len(group))})
        at += len(group)
    res = pl.pallas_call(
        riding, name=name, grid=grid, in_specs=list(in_specs) + [ANY] * nr, out_specs=list(out_specs) + [ANY] * nr,
        out_shape=list(out_shape) + [s for shapes, _, _ in plans for s in shapes],
        scratch_shapes=list(scratch_shapes) + [s for _, sems, _ in plans for s in sems],
        input_output_aliases=aliases,
        compiler_params=pltpu.CompilerParams(dimension_semantics=("arbitrary",) * len(grid),
                                             vmem_limit_bytes=VMEM_LIMIT_BYTES, has_side_effects=True),
    )(*args, *arrays)
    rode, at = [], n_out
    for _, group in rider:
        rode.append(list(res[at:at + len(group)]))
        at += len(group)
    return list(res[:n_out]), rode


def _rider_plan(kind, arrays):
    n = len(arrays)
    same = [jax.ShapeDtypeStruct(a.shape, a.dtype) for a in arrays]
    pair = [pltpu.SemaphoreType.DMA((n,))] * 2
    if kind == "gather":
        return same, _gather_sems(n), True
    if kind == "exchange":
        return _exchange_shapes(arrays), _exchange_sems(n), False
    if kind == "swap":
        return _swap_shapes(arrays), pair, False
    if kind == "allgather":
        return ([jax.ShapeDtypeStruct((N_DEV,) + a.shape, a.dtype) for a in arrays],
                [pltpu.SemaphoreType.DMA((7 * n,))] * 2 + [pltpu.SemaphoreType.DMA((n,))], False)
    assert kind == "join"
    return same, pair, True


def _rider_steps(kind, ins, outs, sems):
    if kind == "gather":
        return _gather_steps(outs, *sems)
    if kind == "allgather":
        return _allgather_steps(ins, outs, *sems)
    if kind == "exchange":
        send, finish = _exchange_steps(ins, outs, *sems)
    elif kind == "swap":
        send, finish = _swap_steps(ins, outs, *sems)
    else:
        send, finish = _join_steps(outs, *sems)
    return send, None, finish


def _row_tile(k_dim):
    return 1024 if k_dim <= 1024 else 512


def _mm_nn(a, b4, layer, *, out_dtype, name, epilogue=None, extra=None, norm_g=None, rider=None):
    m, k_dim = a.shape
    _, s_dim, kb, n = b4.shape
    assert kb == k_dim
    tm, tn = _row_tile(k_dim), min(n, 1024)
    assert m % tm == 0 and n % tn == 0
    npb = n // tn
    grid = (m // tm, s_dim * npb)
    n_in = 2 + (extra is not None) + (norm_g is not None)
    two_outputs = norm_g is not None or epilogue == "relu2"
    assert norm_g is None or (tn == s_dim * n and epilogue != "relu2")

    def body(*refs):
        a_ref, b_ref = refs[:2]
        e_ref = refs[2] if extra is not None else None
        g_ref = refs[n_in - 1] if norm_g is not None else None
        o_ref = refs[n_in]
        acc = jnp.dot(a_ref[...], b_ref[...], preferred_element_type=F32)
        if epilogue == "relu2":
            r = jnp.maximum(acc, 0.0)
            refs[n_in + 1][...] = r.astype(BF16)
            acc = r * r
        elif epilogue == "residual":
            acc = acc + e_ref[...]
        o_ref[...] = acc.astype(out_dtype)
        if norm_g is not None:
            rstd = lax.rsqrt(jnp.mean(acc * acc, axis=-1, keepdims=True) + EPS)
            refs[n_in + 1][...] = (acc * rstd * g_ref[...]).astype(BF16)

    in_specs = [
        pl.BlockSpec((tm, k_dim), lambda i, j: (i, 0)),
        pl.BlockSpec((None, None, k_dim, tn), lambda i, j: (layer, j // npb, 0, j % npb)),
    ]
    args = [a, b4]
    if extra is not None:
        in_specs.append(pl.BlockSpec((tm, tn), lambda i, j: (i, j)))
        args.append(extra)
    out_block = pl.BlockSpec((tm, tn), lambda i, j: (i, j))
    out_specs, out_shape = [out_block], [jax.ShapeDtypeStruct((m, s_dim * n), out_dtype)]
    if norm_g is not None:
        in_specs.append(pl.BlockSpec((1, tn), lambda i, j: (0, j)))
        args.append(norm_g)
    if two_outputs:
        out_specs.append(out_block)
        out_shape.append(jax.ShapeDtypeStruct((m, s_dim * n), BF16))
    res, rode = _call(
        body, name=name, grid=grid, in_specs=in_specs, out_specs=out_specs, out_shape=out_shape,
        scratch_shapes=[], semantics=("parallel", "parallel"), args=args, rider=rider)
    res = res if two_outputs else res[0]
    return res if rider is None else (res, rode)


def _mm_nt(a, b4, layer, *, out_dtype, name, epilogue=None, extra=None, rider=None):
    m, k_dim = a.shape
    _, s_dim, n_out, n = b4.shape
    assert k_dim == s_dim * n
    tm, tn = _row_tile(k_dim), min(n_out, 1024)
    assert m % tm == 0 and n_out % tn == 0
    grid = (m // tm, n_out // tn)
    rms = epilogue == "rms_bwd"
    assert not rms or tn == n_out
    extras = [] if extra is None else (list(extra) if rms else [extra])
    n_in = 2 + len(extras)

    def body(*refs):
        a_ref, b_ref = refs[:2]
        e_refs = refs[2:n_in]
        o_ref = refs[n_in]
        acc = lax.dot_general(a_ref[:, 0:n], b_ref[0], NT_DIMS, preferred_element_type=F32)
        for s in range(1, s_dim):
            acc = acc + lax.dot_general(a_ref[:, s * n:(s + 1) * n], b_ref[s], NT_DIMS, preferred_element_type=F32)
        if epilogue == "relu2_bwd":
            acc = acc * (2.0 * e_refs[0][...].astype(F32))
        if not rms:
            o_ref[...] = acc.astype(out_dtype)
        else:
            h_ref, g_ref, dres_ref = e_refs
            dhb_ref, dg_ref = refs[n_in + 1:n_in + 3]
            hv = h_ref[...]
            rstd = lax.rsqrt(jnp.mean(hv * hv, axis=-1, keepdims=True) + EPS)
            xhat = hv * rstd
            dxhat = acc * g_ref[...]
            dh = dres_ref[...] + rstd * (dxhat - xhat * jnp.mean(dxhat * xhat, axis=-1, keepdims=True))
            o_ref[...] = dh
            dhb_ref[...] = dh.astype(BF16)
            dg_part = jnp.sum(acc * xhat, axis=0, keepdims=True)
            first = pl.program_id(0) == 0

            @pl.when(first)
            def _():
                dg_ref[...] = dg_part

            @pl.when(jnp.logical_not(first))
            def _():
                dg_ref[...] += dg_part

    in_specs = [
        pl.BlockSpec((tm, k_dim), lambda i, j: (i, 0)),
        pl.BlockSpec((None, s_dim, tn, n), lambda i, j: (layer, 0, j, 0)),
    ]
    args = [a, b4] + extras
    block = pl.BlockSpec((tm, tn), lambda i, j: (i, j))
    vec = pl.BlockSpec((1, tn), lambda i, j: (0, j))
    if rms:
        in_specs += [block, vec, block]
        out_specs = [block, block, vec]
        out_shape = [jax.ShapeDtypeStruct((m, n_out), F32), jax.ShapeDtypeStruct((m, n_out), BF16),
                     jax.ShapeDtypeStruct((1, n_out), F32)]
    else:
        in_specs += [block] * len(extras)
        out_specs, out_shape = [block], [jax.ShapeDtypeStruct((m, n_out), out_dtype)]
    res, rode = _call(
        body, name=name, grid=grid, in_specs=in_specs, out_specs=out_specs, out_shape=out_shape,
        scratch_shapes=[], semantics=("arbitrary",) * 2 if rms else ("parallel", "parallel"), args=args, rider=rider)
    res = res if rms else res[0]
    return res if rider is None else (res, rode)


def _mm_tn(a, b, s_dim, *, name, rider=None):
    m, k1 = a.shape
    mb, n_all = b.shape
    assert mb == m and n_all % s_dim == 0
    n = n_all // s_dim
    tn, t1 = min(n, 1024), _row_tile(m)
    assert k1 % t1 == 0 and n % tn == 0
    npb = n // tn
    grid = (k1 // t1, s_dim * npb)

    def body(a_ref, b_ref, o_ref):
        o_ref[...] = lax.dot_general(a_ref[...], b_ref[...], TN_DIMS, preferred_element_type=F32).astype(BF16)

    res, rode = _call(
        body, name=name, grid=grid,
        in_specs=[pl.BlockSpec((m, t1), lambda i, j: (0, i)), pl.BlockSpec((m, tn), lambda i, j: (0, j))],
        out_specs=[pl.BlockSpec((None, None, t1, tn), lambda i, j: (0, j // npb, i, j % npb))],
        out_shape=[jax.ShapeDtypeStruct((1, s_dim, k1, n), BF16)],
        scratch_shapes=[], semantics=("parallel", "parallel"), args=[a, b], rider=rider)
    return res[0] if rider is None else (res[0], rode)


ROW_TILE = 512


def _rms_fwd(h, g, *, name, rider=None):
    m, d = h.shape

    def body(h_ref, g_ref, o_ref):
        hv = h_ref[...]
        rstd = lax.rsqrt(jnp.mean(hv * hv, axis=-1, keepdims=True) + EPS)
        o_ref[...] = (hv * rstd * g_ref[...]).astype(BF16)

    res, rode = _call(
        body, name=name, grid=(m // ROW_TILE,),
        in_specs=[pl.BlockSpec((ROW_TILE, d), lambda i: (i, 0)), pl.BlockSpec((1, d), lambda i: (0, 0))],
        out_specs=[pl.BlockSpec((ROW_TILE, d), lambda i: (i, 0))], out_shape=[jax.ShapeDtypeStruct((m, d), BF16)],
        scratch_shapes=[], semantics=("parallel",), args=[h, g], rider=rider)
    return res[0] if rider is None else (res[0], rode)


def _mlp_down_loss(act, w_2, h_res, g, target):
    m, k_dim = act.shape
    d = w_2.shape[-1]
    tm = _row_tile(k_dim)

    def body(a_ref, b_ref, r_ref, g_ref, t_ref, dh_ref, dhb_ref, dg_ref, loss_ref):
        hv = jnp.dot(a_ref[...], b_ref[...], preferred_element_type=F32) + r_ref[...]
        gv = g_ref[...]
        rstd = lax.rsqrt(jnp.mean(hv * hv, axis=-1, keepdims=True) + EPS)
        xhat = hv * rstd
        err = xhat * gv - t_ref[...]
        dy = err * (1.0 / d)
        dxhat = dy * gv
        dh = rstd * (dxhat - xhat * jnp.mean(dxhat * xhat, axis=-1, keepdims=True))
        dh_ref[...] = dh
        dhb_ref[...] = dh.astype(BF16)
        dg_part = jnp.sum(dy * xhat, axis=0, keepdims=True)
        sq = jnp.sum(jnp.sum(err * err, axis=1, keepdims=True), axis=0, keepdims=True) * (0.5 / d)
        loss_part = jnp.broadcast_to(sq, (8, TILE))

        @pl.when(pl.program_id(0) == 0)
        def _():
            dg_ref[...] = dg_part
            loss_ref[...] = loss_part

        @pl.when(pl.program_id(0) > 0)
        def _():
            dg_ref[...] += dg_part
            loss_ref[...] += loss_part

    row = pl.BlockSpec((tm, d), lambda i: (i, 0))
    vec = pl.BlockSpec((1, d), lambda i: (0, 0))
    return pl.pallas_call(
        body, name="mlp1_down_loss", grid=(m // tm,),
        in_specs=[pl.BlockSpec((tm, k_dim), lambda i: (i, 0)),
                  pl.BlockSpec((None, None, k_dim, d), lambda i: (0, 0, 0, 0)), row, vec, row],
        out_specs=[row, row, vec, pl.BlockSpec((8, TILE), lambda i: (0, 0))],
        out_shape=[jax.ShapeDtypeStruct((m, d), F32), jax.ShapeDtypeStruct((m, d), BF16),
                   jax.ShapeDtypeStruct((1, d), F32), jax.ShapeDtypeStruct((8, TILE), F32)],
        compiler_params=_params(("arbitrary",)),
    )(act, w_2, h_res, g, target)


def _shift_down(x, s, t_idx):
    return jnp.where(t_idx >= s, pltpu.roll(x, s, 0), 0.0)


def _shift_up(x, s, t_idx, t_len):
    return jnp.where(t_idx < t_len - s, pltpu.roll(x, t_len - s, 0), 0.0)


def _pool_select(group, s2, s4, s8, s16):
    return jnp.where(group == 0, s2, jnp.where(group == 1, s4, jnp.where(group == 2, s8, s16)))


def _pool_count(group, t_idx):
    win = jnp.left_shift(2, group)
    return jnp.minimum(t_idx + 1, win).astype(F32)


def _pool_fwd_math(a, group, t_idx):
    s2 = a + _shift_down(a, 1, t_idx)
    s4 = s2 + _shift_down(s2, 2, t_idx)
    s8 = s4 + _shift_down(s4, 4, t_idx)
    s16 = s8 + _shift_down(s8, 8, t_idx)
    return _pool_select(group, s2, s4, s8, s16) / _pool_count(group, t_idx) - a


def _pool_bwd_math(dpooled, group, t_idx, t_len):
    e = dpooled / _pool_count(group, t_idx)
    s2 = e + _shift_up(e, 1, t_idx, t_len)
    s4 = s2 + _shift_up(s2, 2, t_idx, t_len)
    s8 = s4 + _shift_up(s4, 4, t_idx, t_len)
    s16 = s8 + _shift_up(s8, 8, t_idx, t_len)
    return _pool_select(group, s2, s4, s8, s16) - dpooled


def _conv_fwd_math(c, w_ref, b_ref, t_idx):
    return (w_ref[0:1, :] * _shift_down(c, 2, t_idx) + w_ref[1:2, :] * _shift_down(c, 1, t_idx)
            + w_ref[2:3, :] * c + b_ref[...])


def _ab_fwd(p, pool_w, pool_scale, conv_w, conv_b, nseq, t_len, rider=None):
    m = p.shape[0]
    ng = 4

    def body(a_ref, xb_ref, gb_ref, gc_ref, pw_ref, ps_ref, cw_ref, cb_ref, o_ref):
        j = pl.program_id(1)
        t_idx = lax.broadcasted_iota(jnp.int32, (t_len, TILE), 0)

        @pl.when(j < ng)
        def _():
            pooled = _pool_fwd_math(a_ref[...].astype(F32), j, t_idx)
            mixed = jnp.dot(pooled.astype(BF16), pw_ref[...].astype(BF16), preferred_element_type=F32)
            o_ref[...] = (mixed * ps_ref[...]).astype(BF16)

        @pl.when(j >= ng)
        def _():
            c = gc_ref[...].astype(F32) * xb_ref[...].astype(F32)
            y = _conv_fwd_math(c, cw_ref, cb_ref, t_idx)
            o_ref[...] = (gb_ref[...].astype(F32) * y).astype(BF16)

    def pool_j(j):
        return jnp.minimum(j, ng - 1)

    def conv_j(j):
        return jnp.maximum(j - ng, 0)

    in_specs = [
        pl.BlockSpec((t_len, TILE), lambda s, j: (s, pool_j(j))),
        pl.BlockSpec((t_len, TILE), lambda s, j: (s, ng + conv_j(j))),
        pl.BlockSpec((t_len, TILE), lambda s, j: (s, 2 * ng + conv_j(j))),
        pl.BlockSpec((t_len, TILE), lambda s, j: (s, 3 * ng + conv_j(j))),
        pl.BlockSpec((None, TILE, TILE), lambda s, j: (pool_j(j), 0, 0)),
        pl.BlockSpec((None, 1, TILE), lambda s, j: (pool_j(j), 0, 0)),
        pl.BlockSpec((3, TILE), lambda s, j: (0, conv_j(j))),
        pl.BlockSpec((1, TILE), lambda s, j: (0, conv_j(j))),
    ]
    res, rode = _call(
        body, name="ab_mixer_fwd", grid=(nseq, 2 * ng), in_specs=in_specs,
        out_specs=[pl.BlockSpec((t_len, TILE), lambda s, j: (s, j))],
        out_shape=[jax.ShapeDtypeStruct((m, 2 * ng * TILE), BF16)], scratch_shapes=[],
        semantics=("parallel", "arbitrary"), args=[p, p, p, p, pool_w, pool_scale, conv_w, conv_b], rider=rider)
    return res[0] if rider is None else (res[0], rode)


def _ab_bwd(p, dmix, pool_w, pool_scale, conv_w, conv_b, nseq, t_len, rider=None):
    m = p.shape[0]
    ng = 4

    def body(a_ref, xb_ref, gb_ref, gc_ref, dma_ref, dmb_ref, pw_ref, ps_ref, cw_ref, cb_ref,
             da_ref, dxb_ref, dgb_ref, dgc_ref, dpw_ref, dps_ref, dcw_ref, dcb_ref):
        j = pl.program_id(0)
        first = pl.program_id(1) == 0
        t_idx = lax.broadcasted_iota(jnp.int32, (t_len, TILE), 0)

        pooled = _pool_fwd_math(a_ref[...].astype(F32), j, t_idx).astype(BF16)
        w_bf = pw_ref[...].astype(BF16)
        mixed = jnp.dot(pooled, w_bf, preferred_element_type=F32)
        dm = dma_ref[...].astype(F32)
        dps = jnp.sum(dm * mixed, axis=0, keepdims=True)
        dmixed = (dm * ps_ref[...]).astype(BF16)
        dpw = lax.dot_general(pooled, dmixed, TN_DIMS, preferred_element_type=F32)
        dpooled = lax.dot_general(dmixed, w_bf, NT_DIMS, preferred_element_type=F32)
        da_ref[...] = _pool_bwd_math(dpooled, j, t_idx, t_len).astype(BF16)

        xb = xb_ref[...].astype(F32)
        gb = gb_ref[...].astype(F32)
        gc = gc_ref[...].astype(F32)
        d = dmb_ref[...].astype(F32)
        c = gc * xb
        c1 = _shift_down(c, 1, t_idx)
        c2 = _shift_down(c, 2, t_idx)
        y = cw_ref[0:1, :] * c2 + cw_ref[1:2, :] * c1 + cw_ref[2:3, :] * c + cb_ref[...]
        dgb_ref[...] = (d * y).astype(BF16)
        dy = d * gb
        dc = (cw_ref[2:3, :] * dy + cw_ref[1:2, :] * _shift_up(dy, 1, t_idx, t_len)
              + cw_ref[0:1, :] * _shift_up(dy, 2, t_idx, t_len))
        dgc_ref[...] = (dc * xb).astype(BF16)
        dxb_ref[...] = (dc * gc).astype(BF16)
        dcw = jnp.concatenate([jnp.sum(dy * c2, axis=0, keepdims=True),
                               jnp.sum(dy * c1, axis=0, keepdims=True),
                               jnp.sum(dy * c, axis=0, keepdims=True)], axis=0)
        dcb = jnp.sum(dy, axis=0, keepdims=True)

        @pl.when(first)
        def _():
            dpw_ref[...] = dpw
            dps_ref[...] = dps
            dcw_ref[...] = dcw
            dcb_ref[...] = dcb

        @pl.when(jnp.logical_not(first))
        def _():
            dpw_ref[...] += dpw
            dps_ref[...] += dps
            dcw_ref[...] += dcw
            dcb_ref[...] += dcb

    def col(k):
        return pl.BlockSpec((t_len, TILE), lambda j, s: (s, k * ng + j))

    in_specs = [
        col(0), col(1), col(2), col(3), col(0), col(1),
        pl.BlockSpec((None, TILE, TILE), lambda j, s: (j, 0, 0)),
        pl.BlockSpec((None, 1, TILE), lambda j, s: (j, 0, 0)),
        pl.BlockSpec((3, TILE), lambda j, s: (0, j)),
        pl.BlockSpec((1, TILE), lambda j, s: (0, j)),
    ]
    piece = pl.BlockSpec((t_len, TILE), lambda j, s: (s, j))
    out_specs = [
        piece, piece, piece, piece,
        pl.BlockSpec((None, TILE, TILE), lambda j, s: (j, 0, 0)),
        pl.BlockSpec((None, 1, TILE), lambda j, s: (j, 0, 0)),
        pl.BlockSpec((3, TILE), lambda j, s: (0, j)),
        pl.BlockSpec((1, TILE), lambda j, s: (0, j)),
    ]
    w = ng * TILE
    out_shape = [jax.ShapeDtypeStruct((m, w), BF16)] * 4 + [
        jax.ShapeDtypeStruct((ng, TILE, TILE), F32), jax.ShapeDtypeStruct((ng, 1, TILE), F32),
        jax.ShapeDtypeStruct((3, w), F32), jax.ShapeDtypeStruct((1, w), F32)]
    res, rode = _call(
        body, name="ab_mixer_bwd", grid=(ng, nseq), in_specs=in_specs, out_specs=out_specs, out_shape=out_shape,
        scratch_shapes=[], semantics=("parallel", "arbitrary"),
        args=[p, p, p, p, dmix, dmix, pool_w, pool_scale, conv_w, conv_b], rider=rider)
    return res if rider is None else (res, rode)


SGU_ROWS = 512
INV_SQRT2 = 1.0 / math.sqrt(2.0)
INV_SQRT_2PI = 1.0 / math.sqrt(2.0 * math.pi)


def _gelu(x):
    return 0.5 * x * (1.0 + lax.erf(x * INV_SQRT2))


def _gelu_grad(x):
    return 0.5 * (1.0 + lax.erf(x * INV_SQRT2)) + x * (INV_SQRT_2PI * jnp.exp(-0.5 * x * x))


def _causal_tile(transposed=False):
    r = lax.broadcasted_iota(jnp.int32, (TILE, TILE), 0)
    c = lax.broadcasted_iota(jnp.int32, (TILE, TILE), 1)
    return r <= c if transposed else c <= r


def _sgu_norm(v, g_ref, b_ref):
    mu = jnp.mean(v, axis=-1, keepdims=True)
    xc = v - mu
    rstd = lax.rsqrt(jnp.mean(xc * xc, axis=-1, keepdims=True) + EPS)
    xhat = xc * rstd
    return xhat, rstd, xhat * g_ref[...] + b_ref[...]


def _sgu_fwd(p, norm_g, norm_b, w_s, bias_tile):
    m = p.shape[0]
    ng = 4
    width = ng * TILE

    def body(u_ref, v_ref, g_ref, b_ref, w_ref, bias_ref, o_ref):
        u = _gelu(u_ref[...].astype(F32))
        _, _, vln = _sgu_norm(_gelu(v_ref[...].astype(F32)), g_ref, b_ref)
        vln = vln.astype(BF16)
        causal = _causal_tile()
        for g in range(ng):
            cols = slice(g * TILE, (g + 1) * TILE)
            wg = jnp.where(causal, w_ref[g], 0.0).astype(BF16)
            for n in range(SGU_ROWS // TILE):
                rows = slice(n * TILE, (n + 1) * TILE)
                s = jnp.dot(wg, vln[rows, cols], preferred_element_type=F32) + bias_ref[g]
                o_ref[rows, cols] = (u[rows, cols] * s).astype(BF16)

    vec = pl.BlockSpec((1, width), lambda i: (0, 0))
    tiles = pl.BlockSpec((ng, TILE, TILE), lambda i: (0, 0, 0))
    return pl.pallas_call(
        body, name="sgu_fwd", grid=(m // SGU_ROWS,),
        in_specs=[pl.BlockSpec((SGU_ROWS, width), lambda i: (i, 0)),
                  pl.BlockSpec((SGU_ROWS, width), lambda i: (i, 1)), vec, vec, tiles, tiles],
        out_specs=pl.BlockSpec((SGU_ROWS, width), lambda i: (i, 0)),
        out_shape=jax.ShapeDtypeStruct((m, width), BF16),
        compiler_params=_params(("parallel",)),
    )(p, p, norm_g, norm_b, w_s, bias_tile)


def _sgu_bwd(p, dmix, norm_g, norm_b, w_s, w_s_t, bias_tile):
    m = p.shape[0]
    ng = 4
    width = ng * TILE

    def body(u_ref, v_ref, dc_ref, g_ref, b_ref, w_ref, wt_ref, bias_ref,
             du_ref, dv_ref, dw_ref, dbs_ref, dg_ref, db_ref, ds_scr, dvln_scr):
        u_pre = u_ref[...].astype(F32)
        v_pre = v_ref[...].astype(F32)
        u = _gelu(u_pre)
        xhat, rstd, vln = _sgu_norm(_gelu(v_pre), g_ref, b_ref)
        vln = vln.astype(BF16)
        dc = dc_ref[...].astype(F32)
        causal = _causal_tile()
        ones = jnp.ones((TILE, TILE), BF16)
        first = pl.program_id(0) == 0
        for g in range(ng):
            cols = slice(g * TILE, (g + 1) * TILE)
            wg = jnp.where(causal, w_ref[g], 0.0).astype(BF16)
            wgt = jnp.where(_causal_tile(transposed=True), wt_ref[g], 0.0).astype(BF16)
            dw_acc = jnp.zeros((TILE, TILE), F32)
            dbs_acc = jnp.zeros((TILE, TILE), F32)
            for n in range(SGU_ROWS // TILE):
                rows = slice(n * TILE, (n + 1) * TILE)
                vt = vln[rows, cols]
                s = jnp.dot(wg, vt, preferred_element_type=F32) + bias_ref[g]
                ds_scr[rows, cols] = dc[rows, cols] * s
                ds = (dc[rows, cols] * u[rows, cols]).astype(BF16)
                dw_acc += lax.dot_general(ds, vt, NT_DIMS, preferred_element_type=F32)
                dbs_acc += jnp.dot(ds, ones, preferred_element_type=F32)
                dvln_scr[rows, cols] = jnp.dot(wgt, ds, preferred_element_type=F32)
            dw_g = jnp.where(causal, dw_acc, 0.0)

            @pl.when(first)
            def _():
                dw_ref[g] = dw_g
                dbs_ref[g] = dbs_acc

            @pl.when(jnp.logical_not(first))
            def _():
                dw_ref[g] += dw_g
                dbs_ref[g] += dbs_acc

        du_ref[...] = (ds_scr[...] * _gelu_grad(u_pre)).astype(BF16)
        dvln = dvln_scr[...]
        dxhat = dvln * g_ref[...]
        dv = rstd * (dxhat - jnp.mean(dxhat, axis=-1, keepdims=True)
                     - xhat * jnp.mean(dxhat * xhat, axis=-1, keepdims=True))
        dv_ref[...] = (dv * _gelu_grad(v_pre)).astype(BF16)
        dg_part = jnp.sum(dvln * xhat, axis=0, keepdims=True)
        db_part = jnp.sum(dvln, axis=0, keepdims=True)

        @pl.when(first)
        def _():
            dg_ref[...] = dg_part
            db_ref[...] = db_part

        @pl.when(jnp.logical_not(first))
        def _():
            dg_ref[...] += dg_part
            db_ref[...] += db_part

    vec = pl.BlockSpec((1, width), lambda i: (0, 0))
    tiles = pl.BlockSpec((ng, TILE, TILE), lambda i: (0, 0, 0))
    rows0 = pl.BlockSpec((SGU_ROWS, width), lambda i: (i, 0))
    rows1 = pl.BlockSpec((SGU_ROWS, width), lambda i: (i, 1))
    return pl.pallas_call(
        body, name="sgu_bwd", grid=(m // SGU_ROWS,),
        in_specs=[rows0, rows1, rows0, vec, vec, tiles, tiles, tiles],
        out_specs=[rows0, rows0, tiles, tiles, vec, vec],
        out_shape=[jax.ShapeDtypeStruct((m, width), BF16), jax.ShapeDtypeStruct((m, width), BF16),
                   jax.ShapeDtypeStruct((ng, TILE, TILE), F32), jax.ShapeDtypeStruct((ng, TILE, TILE), F32),
                   jax.ShapeDtypeStruct((1, width), F32), jax.ShapeDtypeStruct((1, width), F32)],
        scratch_shapes=[pltpu.VMEM((SGU_ROWS, width), F32), pltpu.VMEM((SGU_ROWS, width), F32)],
        compiler_params=_params(("arbitrary",)),
    )(p, p, dmix, norm_g, norm_b, w_s, w_s_t, bias_tile)


SB_DH = 64
SB_SCALE = 1.0 / math.sqrt(SB_DH)


SB_BLOCK = 256
SB_SUB = SB_BLOCK // TILE
SB_PASS = 4


def _split_passes(i):
    rem = i % SB_PASS
    return i // SB_PASS, rem >= 2, rem % 2 == 1


def _sum_matrix(kind):
    j = lax.broadcasted_iota(jnp.int32, (TILE, 2 * TILE), 0)
    s = lax.broadcasted_iota(jnp.int32, (TILE, 2 * TILE), 1)
    tri = {"after": j > s, "upto": j <= s, "before": j < s}[kind]
    return jnp.where(jnp.logical_or(s >= TILE, tri), 1.0, 0.0).astype(BF16)


def _strict_mask():
    r = lax.broadcasted_iota(jnp.int32, (SB_BLOCK, SB_BLOCK), 0)
    c = lax.broadcasted_iota(jnp.int32, (SB_BLOCK, SB_BLOCK), 1)
    return c < r


def _head_lanes(h):
    lane = lax.broadcasted_iota(jnp.int32, (1, TILE), 1)
    return (lane >= h * SB_DH) & (lane < (h + 1) * SB_DH)


def _softplus(z):
    return jnp.maximum(z, 0.0) + jnp.log(1.0 + jnp.exp(-jnp.abs(z)))


def _sb_fwd(p, nseq, t_len, gather):
    m = p.shape[0]
    npair = 4
    ng = len(gather)
    last_step = nseq * npair - 1

    def body(q_ref, k_ref, v_ref, *rest):
        o_ref, lt_ref = rest[ng:ng + 2]
        kh_ref, vh_ref = rest[2 * ng + 2:2 * ng + 4]
        step = pl.program_id(0) * npair + pl.program_id(1)
        send, forward, finish = _gather_steps(rest[ng + 2:2 * ng + 2], *rest[2 * ng + 4:])
        pl.when(step == 0)(send)
        pl.when(step == (last_step + 1) // 2)(forward)
        for h in range(2):
            keep = _head_lanes(h)
            kh_ref[h] = jnp.where(keep, k_ref[...], 0).astype(BF16)
            vh_ref[h] = jnp.where(keep, v_ref[...], 0).astype(BF16)
        summat = _sum_matrix("after")
        strict = _strict_mask()

        def one_pass(q, row0, nsub, diag, state):
            rows = pl.ds(row0, nsub * TILE)
            z, sp, pieces = [], [], []
            for h in range(2):
                zh = lax.dot_general(q, kh_ref[h, rows, :], NT_DIMS, preferred_element_type=F32)
                sph = _softplus(zh)
                logkeep = jnp.where(strict, -sph, 0.0) if diag else -sph
                z.append(zh)
                sp.append(sph)
                pieces += [logkeep[:, b * TILE:(b + 1) * TILE] for b in range(nsub)]
            sums = jnp.dot(jnp.concatenate(pieces, axis=0).astype(BF16), summat, preferred_element_type=F32)
            out = []
            for h in range(2):
                carry, acc = state[2 * h], state[2 * h + 1]
                after = [None] * nsub
                for b in reversed(range(nsub)):
                    part = sums[(h * nsub + b) * SB_BLOCK:(h * nsub + b + 1) * SB_BLOCK]
                    after[b] = part[:, :TILE] + carry
                    carry = carry + part[:, TILE:]
                w = jnp.exp(z[h] - sp[h] + jnp.concatenate(after, axis=1))
                if diag:
                    w = jnp.where(strict, w, 0.0)
                out += [carry, acc + jnp.dot(w.astype(BF16), vh_ref[h, rows, :], preferred_element_type=F32)]
            return tuple(out)

        def q_block(i, _):
            r0 = pl.multiple_of(i * SB_BLOCK, SB_BLOCK)
            q = q_ref[pl.ds(r0, SB_BLOCK), :] * SB_SCALE
            zero = jnp.zeros((SB_BLOCK, TILE), F32)
            state = one_pass(q, r0, SB_SUB, True, (zero,) * 4)
            full, two, one = _split_passes(i)
            state = lax.fori_loop(
                0, full,
                lambda jj, st: one_pass(q, pl.multiple_of((i - SB_PASS * (jj + 1)) * SB_BLOCK, SB_BLOCK),
                                        SB_PASS * SB_SUB, False, st),
                state)
            state = lax.cond(
                two, lambda st: one_pass(q, pl.multiple_of((i % 2) * SB_BLOCK, SB_BLOCK), 2 * SB_SUB, False, st),
                lambda st: st, state)
            state = lax.cond(one, lambda st: one_pass(q, 0, SB_SUB, False, st), lambda st: st, state)
            o_ref[pl.ds(r0, SB_BLOCK), :] = (state[1] + state[3]).astype(BF16)
            lt_ref[pl.ds(r0, SB_BLOCK), :] = jnp.where(_head_lanes(0), state[0], state[2])
            return 0

        lax.fori_loop(0, t_len // SB_BLOCK, q_block, 0)
        pl.when(step == last_step)(finish)

    def col(k):
        return pl.BlockSpec((t_len, TILE), lambda s, hp: (s, k * npair + hp))

    out = pl.BlockSpec((t_len, TILE), lambda s, hp: (s, hp))
    res = pl.pallas_call(
        body, name="stickbreak_fwd", grid=(nseq, npair), in_specs=[col(2), col(3), col(4)] + [ANY] * ng,
        out_specs=[out, out] + [ANY] * ng,
        out_shape=[jax.ShapeDtypeStruct((m, npair * TILE), BF16), jax.ShapeDtypeStruct((m, npair * TILE), F32)]
        + [jax.ShapeDtypeStruct(b.shape, b.dtype) for b in gather],
        input_output_aliases={3 + a: 2 + a for a in range(ng)},
        scratch_shapes=[pltpu.VMEM((2, t_len, TILE), BF16), pltpu.VMEM((2, t_len, TILE), BF16)] + _gather_sems(ng),
        compiler_params=pltpu.CompilerParams(dimension_semantics=("arbitrary", "arbitrary"),
                                             vmem_limit_bytes=VMEM_LIMIT_BYTES, has_side_effects=True),
    )(p, p, p, *gather)
    return res[0], res[1], res[2:]


def _sb_bwd(p, dmix, ltot, nseq, t_len, exchange):
    m = p.shape[0]
    npair = 4
    ne = len(exchange)
    last_step = nseq * npair - 1

    def body(q_ref, k_ref, v_ref, do_ref, lt_ref, *rest):
        dq_ref, dk_ref, dv_ref = rest[ne:ne + 3]
        kh_ref, vh_ref, dk_acc, dv_acc = rest[2 * ne + 3:2 * ne + 7]
        step = pl.program_id(0) * npair + pl.program_id(1)
        send, finish = _exchange_steps(rest[:ne], rest[ne + 3:2 * ne + 3], *rest[2 * ne + 7:])
        pl.when(step == 0)(send)
        for h in range(2):
            keep = _head_lanes(h)
            kh_ref[h] = jnp.where(keep, k_ref[...], 0).astype(BF16)
            vh_ref[h] = jnp.where(keep, v_ref[...], 0).astype(BF16)
        dk_acc[...] = jnp.zeros_like(dk_acc)
        dv_acc[...] = jnp.zeros_like(dv_acc)
        sum_upto = _sum_matrix("upto")
        sum_before = _sum_matrix("before")
        strict = _strict_mask()
        lane = lax.broadcasted_iota(jnp.int32, (SB_BLOCK, TILE), 1)

        def running(x, matrix, start, nsub):
            pieces = [x[h][:, b * TILE:(b + 1) * TILE] for h in range(2) for b in range(nsub)]
            sums = jnp.dot(jnp.concatenate(pieces, axis=0).astype(BF16), matrix, preferred_element_type=F32)
            wide, ends = [], []
            for h in range(2):
                total, cols = start[h], []
                for b in range(nsub):
                    part = sums[(h * nsub + b) * SB_BLOCK:(h * nsub + b + 1) * SB_BLOCK]
                    cols.append(part[:, :TILE] + total)
                    total = total + part[:, TILE:]
                wide.append(jnp.concatenate(cols, axis=1))
                ends.append(total)
            return wide, ends

        def one_pass(q, do, qh, doh, ltot, row0, nsub, diag, state):
            rows = pl.ds(row0, nsub * TILE)
            z, sp, logkeep = [], [], []
            for h in range(2):
                zh = lax.dot_general(q, kh_ref[h, rows, :], NT_DIMS, preferred_element_type=F32)
                sph = _softplus(zh)
                z.append(zh)
                sp.append(sph)
                logkeep.append(jnp.where(strict, -sph, 0.0) if diag else -sph)
            upto, sum_l = running(logkeep, sum_upto, [state[0], state[3]], nsub)
            w, g = [], []
            for h in range(2):
                wh = jnp.exp(z[h] - sp[h] + (ltot[h] - upto[h]))
                if diag:
                    wh = jnp.where(strict, wh, 0.0)
                w.append(wh)
                g.append(wh * lax.dot_general(do, vh_ref[h, rows, :], NT_DIMS, preferred_element_type=F32))
            g_before, sum_g = running(g, sum_before, [state[1], state[4]], nsub)
            out, dk_new, dv_new = [], 0.0, 0.0
            for h in range(2):
                dz = g[h] - jnp.exp(z[h] - sp[h]) * (g[h] + g_before[h])
                if diag:
                    dz = jnp.where(strict, dz, 0.0)
                dzb = dz.astype(BF16)
                dq = state[3 * h + 2] + jnp.dot(dzb, kh_ref[h, rows, :], preferred_element_type=F32)
                dk_new = dk_new + lax.dot_general(dzb, qh[h], TN_DIMS, preferred_element_type=F32)
                dv_new = dv_new + lax.dot_general(w[h].astype(BF16), doh[h], TN_DIMS, preferred_element_type=F32)
                out += [sum_l[h], sum_g[h], dq]
            dk_acc[rows, :] += dk_new
            dv_acc[rows, :] += dv_new
            return tuple(out)

        def q_block(i, _):
            r0 = pl.multiple_of(i * SB_BLOCK, SB_BLOCK)
            q = q_ref[pl.ds(r0, SB_BLOCK), :] * SB_SCALE
            do = do_ref[pl.ds(r0, SB_BLOCK), :]
            lt = lt_ref[pl.ds(r0, SB_BLOCK), :]
            qh, doh, ltot = [], [], []
            for h in range(2):
                keep = _head_lanes(h)
                qh.append(jnp.where(keep, q, 0).astype(BF16))
                doh.append(jnp.where(keep, do, 0).astype(BF16))
                ltot.append(jnp.sum(jnp.where(lane == h * SB_DH, lt, 0.0), axis=1, keepdims=True))
            zero = jnp.zeros((SB_BLOCK, TILE), F32)
            full, two, one = _split_passes(i)
            state = lax.fori_loop(
                0, full,
                lambda jj, st: one_pass(q, do, qh, doh, ltot, pl.multiple_of(SB_PASS * jj * SB_BLOCK, SB_BLOCK),
                                        SB_PASS * SB_SUB, False, st),
                (zero,) * 6)
            state = lax.cond(
                two,
                lambda st: one_pass(q, do, qh, doh, ltot, pl.multiple_of(SB_PASS * full * SB_BLOCK, SB_BLOCK),
                                    2 * SB_SUB, False, st),
                lambda st: st, state)
            state = lax.cond(
                one,
                lambda st: one_pass(q, do, qh, doh, ltot, pl.multiple_of((i - 1) * SB_BLOCK, SB_BLOCK), SB_SUB, False, st),
                lambda st: st, state)
            state = one_pass(q, do, qh, doh, ltot, r0, SB_SUB, True, state)
            dq_ref[pl.ds(r0, SB_BLOCK), :] = ((state[2] + state[5]) * SB_SCALE).astype(BF16)
            return 0

        lax.fori_loop(0, t_len // SB_BLOCK, q_block, 0)
        dk_ref[...] = dk_acc[...].astype(BF16)
        dv_ref[...] = dv_acc[...].astype(BF16)
        pl.when(step == last_step)(finish)

    def col(k):
        return pl.BlockSpec((t_len, TILE), lambda s, hp: (s, k * npair + hp))

    out = pl.BlockSpec((t_len, TILE), lambda s, hp: (s, hp))
    width = npair * TILE
    res = pl.pallas_call(
        body, name="stickbreak_bwd", grid=(nseq, npair),
        in_specs=[col(2), col(3), col(4), col(1), out] + [ANY] * ne, out_specs=[out, out, out] + [ANY] * ne,
        out_shape=[jax.ShapeDtypeStruct((m, width), BF16)] * 3 + _exchange_shapes(exchange),
        scratch_shapes=[pltpu.VMEM((2, t_len, TILE), BF16), pltpu.VMEM((2, t_len, TILE), BF16),
                        pltpu.VMEM((t_len, TILE), F32), pltpu.VMEM((t_len, TILE), F32)] + _exchange_sems(ne),
        compiler_params=pltpu.CompilerParams(dimension_semantics=("arbitrary", "arbitrary"),
                                             vmem_limit_bytes=VMEM_LIMIT_BYTES, has_side_effects=True),
    )(p, p, p, dmix, ltot, *exchange)
    return res[0], res[1], res[2], res[3:]


def _adam_math(w, g, m, v):
    m = ADAM_B1 * m + (1.0 - ADAM_B1) * g
    v = ADAM_B2 * v + (1.0 - ADAM_B2) * (g * g)
    m_hat = m / (1.0 - ADAM_B1 ** ADAM_STEP)
    v_hat = v / (1.0 - ADAM_B2 ** ADAM_STEP)
    delta = -ADAM_LR * (m_hat / (jnp.sqrt(v_hat) + ADAM_EPS) + ADAM_WD * w)
    return delta, m, v


def _cast_place(w, layer, pos, *, name):
    _, r, c = w.shape
    tr = min(r, 256)

    def body(pos_ref, w_ref, o_ref):
        o_ref[...] = w_ref[...].astype(BF16)

    grid_spec = pltpu.PrefetchScalarGridSpec(
        num_scalar_prefetch=1, grid=(r // tr,),
        in_specs=[pl.BlockSpec((None, tr, c), lambda i, pos_ref: (layer, i, 0))],
        out_specs=pl.BlockSpec((None, None, tr, c), lambda i, pos_ref: (0, pos_ref[0], i, 0)))
    return pl.pallas_call(
        body, name=name, grid_spec=grid_spec, out_shape=jax.ShapeDtypeStruct((1, N_CHIP, r, c), BF16),
        compiler_params=_params(("parallel",)),
    )(pos, w)


def _pair_sum(mine, got, pos, *, name):
    l_dim, s_dim, h, c = got.shape
    th = min(h, 512)
    nt = h // th

    def body(pos_ref, a_ref, b_ref, o_ref):
        o_ref[...] = (a_ref[...].astype(F32) + b_ref[...].astype(F32)).astype(BF16)

    spec = pl.BlockSpec((None, None, th, c), lambda l, s, i, pos_ref: (l, s, i, 0))
    grid_spec = pltpu.PrefetchScalarGridSpec(
        num_scalar_prefetch=1, grid=(l_dim, s_dim, nt),
        in_specs=[pl.BlockSpec((None, None, th, c), lambda l, s, i, pos_ref: (l, s, pos_ref[1] * nt + i, 0)), spec],
        out_specs=spec)
    return pl.pallas_call(
        body, name=name, grid_spec=grid_spec, out_shape=jax.ShapeDtypeStruct(got.shape, BF16),
        compiler_params=_params(("parallel",) * 3),
    )(pos, mine, got)


def _chip_sum(sums, landed, pos, *, name):
    l_dim, _, h, c = sums.shape
    th = min(h, 512)
    nt = h // th

    def body(pos_ref, own, r0, r1, r2, o_ref):
        o_ref[...] = ((own[...].astype(F32) + r0[...].astype(F32)) + r1[...].astype(F32)) + r2[...].astype(F32)

    def piece(k):
        return pl.BlockSpec((None, None, th, c), lambda l, i, pos_ref: (l, k, i, 0))

    grid_spec = pltpu.PrefetchScalarGridSpec(
        num_scalar_prefetch=1, grid=(l_dim, nt),
        in_specs=[pl.BlockSpec((None, None, th, c), lambda l, i, pos_ref: (l, pos_ref[0], i, 0)),
                  piece(0), piece(1), piece(2)],
        out_specs=pl.BlockSpec((None, th, c), lambda l, i, pos_ref: (l, pos_ref[1] * nt + i, 0)))
    return pl.pallas_call(
        body, name=name, grid_spec=grid_spec, out_shape=jax.ShapeDtypeStruct((l_dim, 2 * h, c), F32),
        compiler_params=_params(("parallel",) * 2),
    )(pos, sums, landed, landed, landed)


def _adam_big(w, m, v, grads, *, name):
    l_dim, r, c = w.shape
    assert len(grads) == l_dim
    tr = min(r, 256)

    def body(*refs):
        w_ref, m_ref, v_ref = refs[:3]
        g_refs = refs[3:3 + l_dim]
        go_ref, d_ref, mo_ref, vo_ref = refs[3 + l_dim:]
        g = g_refs[0][...]
        for l in range(1, l_dim):
            g = jnp.where(pl.program_id(0) == l, g_refs[l][...], g)
        delta, m_new, v_new = _adam_math(w_ref[...], g, m_ref[...], v_ref[...])
        go_ref[...] = g
        d_ref[...] = delta
        mo_ref[...] = m_new
        vo_ref[...] = v_new

    spec = pl.BlockSpec((None, tr, c), lambda l, i: (l, i, 0))
    gspec = pl.BlockSpec((None, tr, c), lambda l, i: (0, i, 0))
    return pl.pallas_call(
        body, name=name, grid=(l_dim, r // tr), in_specs=[spec] * 3 + [gspec] * l_dim, out_specs=[spec] * 4,
        out_shape=[jax.ShapeDtypeStruct(w.shape, F32)] * 4, compiler_params=_params(("parallel",) * 2),
    )(w, m, v, *grads)


def _position():
    return lax.axis_index("x"), lax.axis_index("y"), lax.axis_index("c")


def _other_chips(x, y):
    return [(1 - x, y), (x, 1 - y), (1 - x, 1 - y)]


def _remote(src, dst, send_sem, recv_sem, device):
    return pltpu.make_async_remote_copy(src_ref=src, dst_ref=dst, send_sem=send_sem, recv_sem=recv_sem,
                                        device_id=device, device_id_type=MESH)


ANY = pl.BlockSpec(memory_space=pl.ANY)


def _gather_sems(n):
    return [pltpu.SemaphoreType.DMA((3 * n,))] * 4


def _gather_steps(outs, send_sems, recv_sems, fwd_send, fwd_recv):
    n = len(outs)
    x, y, c = _position()
    chips = _other_chips(x, y)
    sibling = (x, y, 1 - c)

    def half(a, chip, core):
        h = outs[a].shape[2] // 2
        return outs[a].at[:, 2 * chip[0] + chip[1], pl.ds(core * h, h), :]

    def over_ici(a, k, chip):
        block = half(a, chip, c)
        return _remote(block, block, send_sems.at[3 * a + k], recv_sems.at[3 * a + k], (*chips[k], c))

    def over_d2d(a, k, core):
        block = half(a, chips[k], core)
        return _remote(block, block, fwd_send.at[3 * a + k], fwd_recv.at[3 * a + k], sibling)

    def send():
        for a in range(n):
            for k in range(3):
                over_ici(a, k, (x, y)).start()

    def forward():
        for k in range(3):
            for a in range(n):
                over_ici(a, k, chips[k]).wait_recv()
                over_d2d(a, k, c).start()

    def finish():
        for k in range(3):
            for a in range(n):
                over_d2d(a, k, 1 - c).wait_recv()
        for a in range(n):
            for k in range(3):
                over_ici(a, k, (x, y)).wait_send()
                over_d2d(a, k, c).wait_send()

    return send, forward, finish


def _swap_halves(grads, *, name):
    n = len(grads)

    def body(*refs):
        send, finish = _swap_steps(refs[:n], refs[n:2 * n], *refs[2 * n:])
        send()
        finish()

    sem = pltpu.SemaphoreType.DMA((n,))
    return pl.pallas_call(
        body, name=name, in_specs=[ANY] * n, out_specs=[ANY] * n, out_shape=_swap_shapes(grads),
        scratch_shapes=[sem, sem], compiler_params=pltpu.CompilerParams(has_side_effects=True),
    )(*grads)


def _swap_shapes(grads):
    return [jax.ShapeDtypeStruct(g.shape[:2] + (g.shape[2] // 2, g.shape[3]), g.dtype) for g in grads]


def _swap_steps(ins, outs, send_sems, recv_sems):
    x, y, c = _position()

    def copy(a):
        h = ins[a].shape[2] // 2
        return _remote(ins[a].at[:, :, pl.ds((1 - c) * h, h), :], outs[a], send_sems.at[a], recv_sems.at[a],
                       (x, y, 1 - c))

    def send():
        for a in range(len(ins)):
            copy(a).start()

    def finish():
        for a in range(len(ins)):
            copy(a).wait()

    return send, finish


def _exchange_shapes(sums):
    return [jax.ShapeDtypeStruct((s.shape[0], 3) + s.shape[2:], s.dtype) for s in sums]


def _exchange_sems(n):
    return [pltpu.SemaphoreType.DMA((3 * n,))] * 2


def _exchange_steps(ins, outs, send_sems, recv_sems):
    n = len(ins)
    x, y, c = _position()
    chips = _other_chips(x, y)

    def copy(a, k):
        chip = chips[k]
        return _remote(ins[a].at[:, 2 * chip[0] + chip[1]], outs[a].at[:, k],
                       send_sems.at[3 * a + k], recv_sems.at[3 * a + k], (*chip, c))

    def send():
        for a in range(n):
            for k in range(3):
                copy(a, k).start()

    def finish():
        for a in range(n):
            for k in range(3):
                copy(a, k).wait()

    return send, finish


def _join_halves(bufs, *, name):
    n = len(bufs)

    def body(*refs):
        send, finish = _join_steps(refs[n:2 * n], *refs[2 * n:])
        send()
        finish()

    sem = pltpu.SemaphoreType.DMA((n,))
    return pl.pallas_call(
        body, name=name, in_specs=[ANY] * n, out_specs=[ANY] * n,
        out_shape=[jax.ShapeDtypeStruct(b.shape, b.dtype) for b in bufs],
        input_output_aliases={a: a for a in range(n)},
        scratch_shapes=[sem, sem], compiler_params=pltpu.CompilerParams(has_side_effects=True),
    )(*bufs)


def _join_steps(outs, send_sems, recv_sems):
    x, y, c = _position()

    def copy(a, core):
        h = outs[a].shape[1] // 2
        half = outs[a].at[:, pl.ds(core * h, h), :]
        return _remote(half, half, send_sems.at[a], recv_sems.at[a], (x, y, 1 - c))

    def send():
        for a in range(len(outs)):
            copy(a, c).start()

    def finish():
        for a in range(len(outs)):
            copy(a, c).wait_send()
            copy(a, 1 - c).wait_recv()

    return send, finish


def _allgather_steps(ins, outs, send_sems, recv_sems, local_sems):
    n = len(ins)
    x, y, c = _position()
    me, sibling = (x, y, c), (x, y, 1 - c)
    chips = _other_chips(x, y)

    def slot(a, dev):
        return outs[a].at[4 * dev[0] + 2 * dev[1] + dev[2]]

    def copy(a, k, block, to, own=False):
        return _remote(ins[a] if own else slot(a, block), slot(a, block),
                       send_sems.at[7 * a + k], recv_sems.at[7 * a + k], to)

    def first(a):
        return [copy(a, 0, me, sibling, own=True)] + [copy(a, 1 + k, me, (*chips[k], c), own=True) for k in range(3)]

    def local(a):
        return pltpu.make_async_copy(ins[a], slot(a, me), local_sems.at[a])

    def send():
        for a in range(n):
            local(a).start()
            for cp in first(a):
                cp.start()

    def forward():
        for a in range(n):
            for k in range(3):
                copy(a, 1 + k, (*chips[k], c), me).wait_recv()
                copy(a, 4 + k, (*chips[k], c), sibling).start()

    def finish():
        for a in range(n):
            copy(a, 0, sibling, me).wait_recv()
            for k in range(3):
                copy(a, 4 + k, (*chips[k], 1 - c), me).wait_recv()
        for a in range(n):
            for cp in first(a) + [copy(a, 4 + k, (*chips[k], c), sibling) for k in range(3)]:
                cp.wait_send()
            local(a).wait()

    return send, forward, finish


def _allreduce_small(packs):
    n = len(packs)

    def body(*refs):
        ins, outs, gath = refs[:n], refs[n:2 * n], refs[2 * n:3 * n]
        send_sems, recv_sems = refs[3 * n:]
        x, y, c = _position()
        me, sibling = (x, y, c), (x, y, 1 - c)
        chips = _other_chips(x, y)

        def slot(a, dev):
            return gath[a].at[4 * dev[0] + 2 * dev[1] + dev[2]]

        def copy(a, k, block, to, src=None):
            return _remote(slot(a, block) if src is None else src, slot(a, block),
                           send_sems.at[7 * a + k], recv_sems.at[7 * a + k], to)

        started = []
        for a in range(n):
            slot(a, me)[...] = ins[a][...]
            first = [copy(a, 0, me, sibling, src=ins[a])]
            first += [copy(a, 1 + k, me, (*chip, c), src=ins[a]) for k, chip in enumerate(chips)]
            for cp in first:
                cp.start()
            started += first
        for a in range(n):
            for k, chip in enumerate(chips):
                copy(a, 1 + k, (*chip, c), me).wait_recv()
                cp = copy(a, 4 + k, (*chip, c), sibling)
                cp.start()
                started.append(cp)
        for a in range(n):
            copy(a, 0, sibling, me).wait_recv()
            for k, chip in enumerate(chips):
                copy(a, 4 + k, (*chip, 1 - c), me).wait_recv()
        for cp in started:
            cp.wait_send()
        for a in range(n):
            total = gath[a][0]
            for d in range(1, N_DEV):
                total = total + gath[a][d]
            outs[a][...] = total

    vmem = pl.BlockSpec(memory_space=pltpu.VMEM)
    sem = pltpu.SemaphoreType.DMA((7 * n,))
    return pl.pallas_call(
        body, name="allreduce_small", in_specs=[vmem] * n, out_specs=[vmem] * n,
        out_shape=[jax.ShapeDtypeStruct(p.shape, p.dtype) for p in packs],
        scratch_shapes=[pltpu.VMEM((N_DEV,) + p.shape, p.dtype) for p in packs] + [sem, sem],
        compiler_params=pltpu.CompilerParams(has_side_effects=True, vmem_limit_bytes=VMEM_LIMIT_BYTES),
    )(*packs)


LOSS_ROW = 1040


def _pad_rows(a, rows=8):
    return jnp.concatenate([a, jnp.zeros((rows - a.shape[0], a.shape[1]), a.dtype)], axis=0)

def _adam_small(wide, mid, narrow, late, params):
    names = ["mix_norm_g", "mlp_norm_g", "final_norm_g", "conv_b", "conv_w", "sgu_norm_g", "sgu_norm_b",
             "pool_w", "pool_scale", "sgu_w", "sgu_b"]
    n = len(names)

    def body(*refs):
        wmv = refs[4:4 + 3 * n]
        outs = refs[4 + 3 * n:]
        x, y, _ = _position()
        q = 2 * x + y

        def total(ref):
            t = ref[0]
            for dev in range(1, N_DEV):
                t = t + ref[dev]
            return t

        wide_sum, mid_sum, narrow_sum = total(refs[0]), total(refs[1]), total(refs[2])
        late_ref = refs[3]

        def my_quarter(rows):
            parts = [rows[:, s * TILE:(s + 1) * TILE] for s in range(N_CHIP)]
            return jnp.where(q == 0, parts[0], jnp.where(q == 1, parts[1], jnp.where(q == 2, parts[2], parts[3])))

        def tiles(first_row):
            return [((0, g), narrow_sum[first_row + g * TILE:first_row + (g + 1) * TILE, :]) for g in range(4)]

        grads = {
            "mix_norm_g": [((), wide_sum[0:2, :] + late_ref[0:2, :])],
            "mlp_norm_g": [((), wide_sum[8:10, :])],
            "final_norm_g": [((), wide_sum[16:17, :])],
            "conv_b": [((), mid_sum[0:1, :])],
            "conv_w": [((0,), my_quarter(mid_sum[8:11, :]))],
            "sgu_norm_g": [((), my_quarter(mid_sum[16:17, :]))],
            "sgu_norm_b": [((), my_quarter(mid_sum[24:25, :]))],
            "pool_w": tiles(0),
            "sgu_w": tiles(512),
            "pool_scale": [((0,), narrow_sum[1024:1028, :])],
            "sgu_b": [((0,), narrow_sum[1032:1036, :])],
        }
        outs[4 * n][...] = narrow_sum[LOSS_ROW:LOSS_ROW + 8, :]
        for i, name in enumerate(names):
            w_ref, m_ref, v_ref = wmv[3 * i:3 * i + 3]
            for lead, g in grads[name]:
                idx = lead + (slice(None), slice(None))
                delta, m_new, v_new = _adam_math(w_ref[idx], g, m_ref[idx], v_ref[idx])
                outs[4 * i][idx] = g
                outs[4 * i + 1][idx] = delta
                outs[4 * i + 2][idx] = m_new
                outs[4 * i + 3][idx] = v_new

    vmem = pl.BlockSpec(memory_space=pltpu.VMEM)
    args, out_shape = [wide, mid, narrow, late], []
    for name in names:
        w, m, v = params[name]
        args += [w, m, v]
        out_shape += [jax.ShapeDtypeStruct(w.shape, F32)] * 4
    out_shape.append(jax.ShapeDtypeStruct((8, TILE), F32))
    res = pl.pallas_call(
        body, name="adam_small", in_specs=[vmem] * len(args), out_specs=[vmem] * len(out_shape),
        out_shape=out_shape, compiler_params=pltpu.CompilerParams(vmem_limit_bytes=VMEM_LIMIT_BYTES),
    )(*args)
    return {name: res[4 * i:4 * i + 4] for i, name in enumerate(names)}, res[4 * n]


def _pair_sums(grads, got, pos, tag):
    return [_pair_sum(a, b, pos, name=f"pair_sum_{tag}{i}") for i, (a, b) in enumerate(zip(grads, got))]


def _chip_sums(sums, landed, pos, tag):
    return [_chip_sum(s, r, pos, name=f"chip_sum_{tag}{i}") for i, (s, r) in enumerate(zip(sums, landed))]


def kernel(x, mix_norm_g, mlp_norm_g, ab_w_in, pool_w, pool_scale, conv_w, conv_b, ab_w_out, cd_w_in, sgu_norm_g, sgu_norm_b, sgu_w, sgu_b, cd_w_out, mlp_w1, mlp_w2, final_norm_g, loss_target, m_mix_norm_g, m_mlp_norm_g, m_ab_w_in, m_pool_w, m_pool_scale, m_conv_w, m_conv_b, m_ab_w_out, m_cd_w_in, m_sgu_norm_g, m_sgu_norm_b, m_sgu_w, m_sgu_b, m_cd_w_out, m_mlp_w1, m_mlp_w2, m_final_norm_g, v_mix_norm_g, v_mlp_norm_g, v_ab_w_in, v_pool_w, v_pool_scale, v_conv_w, v_conv_b, v_ab_w_out, v_cd_w_in, v_sgu_norm_g, v_sgu_norm_b, v_sgu_w, v_sgu_b, v_cd_w_out, v_mlp_w1, v_mlp_w2, v_final_norm_g):
    nseq, t_len, d = x.shape
    m_tok = nseq * t_len
    h0 = x.reshape(m_tok, d)
    target = loss_target.reshape(m_tok, d)

    x_idx, y_idx = lax.axis_index("x"), lax.axis_index("y")
    q_idx = 2 * x_idx + y_idx
    pos = jnp.stack([q_idx, lax.axis_index("c")]).astype(jnp.int32)

    def shard_buffer(w, layer, tag):
        return _cast_place(w, layer, pos, name=f"cast_place_{tag}")

    def row_block(w):
        return w.reshape(1, 1, -1, w.shape[-1])

    later_weights = [shard_buffer(cd_w_out, 0, "cd_out"), shard_buffer(mlp_w1, 1, "w1_1"),
                     shard_buffer(mlp_w2, 1, "w2_1")]

    pool_w3, pool_scale3 = pool_w[0], pool_scale[0].reshape(4, 1, TILE)
    sgu_w3 = sgu_w[0]
    sgu_w3_t = jnp.swapaxes(sgu_w3, 1, 2)
    sgu_bias_tile = jnp.broadcast_to(sgu_b[0][:, :, None], (4, TILE, TILE))
    conv_b2 = conv_b

    def place_quarter(v):
        return lax.dynamic_update_slice(jnp.zeros((v.shape[0], 4 * TILE), F32), v, (0, q_idx * TILE))

    sharded_small = jnp.concatenate(
        [place_quarter(conv_w[0]), place_quarter(sgu_norm_g), place_quarter(sgu_norm_b),
         jnp.zeros((3, 4 * TILE), F32)], axis=0)
    sharded_small, = _allreduce_small([sharded_small])
    sharded_small = sharded_small * 0.5
    conv_w_full = sharded_small[0:3]
    sgu_g_full = sharded_small[3:4]
    sgu_b_full = sharded_small[4:5]

    xn0, ((w_ab_in,),) = _rms_fwd(h0, mix_norm_g[0:1], name="rms_fwd_mix0",
                                  rider=[("gather", [shard_buffer(ab_w_in, 0, "ab_in")])])
    p_ab, ((w_1_0,),) = _mm_nn(xn0, w_ab_in, 0, out_dtype=BF16, name="ab_in_proj",
                               rider=[("gather", [shard_buffer(mlp_w1, 0, "w1_0")])])
    mix0, ((w_ab_out,),) = _ab_fwd(p_ab, pool_w3, pool_scale3, conv_w_full, conv_b2, nseq, t_len,
                                   rider=[("gather", [shard_buffer(ab_w_out, 0, "ab_out")])])
    w_ab_out = row_block(w_ab_out)
    h1, hn0 = _mm_nn(mix0, w_ab_out, 0, out_dtype=F32, name="ab_out_proj", epilogue="residual", extra=h0,
                     norm_g=mlp_norm_g[0:1])
    (act0, relu0), ((w_2_0,),) = _mm_nn(hn0, w_1_0, 0, out_dtype=BF16, name="mlp0_up", epilogue="relu2",
                                        rider=[("gather", [shard_buffer(mlp_w2, 0, "w2_0")])])
    w_2_0 = row_block(w_2_0)
    (h2, xn1), ((w_cd_in,),) = _mm_nn(act0, w_2_0, 0, out_dtype=F32, name="mlp0_down", epilogue="residual", extra=h1,
                                      norm_g=mix_norm_g[1:2],
                                      rider=[("gather", [shard_buffer(cd_w_in, 0, "cd_in")])])

    p_cd = _mm_nn(xn1, w_cd_in, 0, out_dtype=BF16, name="cd_in_proj")
    c_out = _sgu_fwd(p_cd, sgu_g_full, sgu_b_full, sgu_w3, sgu_bias_tile)
    d_out, ltot, (w_cd_out, w_1_1, w_2_1) = _sb_fwd(p_cd, nseq, t_len, later_weights)
    w_cd_out, w_2_1 = row_block(w_cd_out), row_block(w_2_1)
    mix1 = jnp.concatenate([c_out, d_out], axis=1)
    h3, hn1 = _mm_nn(mix1, w_cd_out, 0, out_dtype=F32, name="cd_out_proj", epilogue="residual", extra=h2,
                     norm_g=mlp_norm_g[1:2])
    act1, relu1 = _mm_nn(hn1, w_1_1, 0, out_dtype=BF16, name="mlp1_up", epilogue="relu2")

    dh4, dh4_bf, dg_final, loss_tile = _mlp_down_loss(act1, w_2_1, h3, final_norm_g.reshape(1, d), target)

    def as_pieces(g):
        return g.reshape(1, N_CHIP, -1, g.shape[-1]) if g.shape[1] == 1 else g

    dz1 = _mm_nt(dh4_bf, w_2_1, 0, out_dtype=BF16, name="mlp1_down_bwd", epilogue="relu2_bwd", extra=relu1)
    g_w2_1 = as_pieces(_mm_tn(act1, dh4_bf, 1, name="mlp1_down_wgrad"))
    g_w1_1 = _mm_tn(hn1, dz1, N_CHIP, name="mlp1_up_wgrad")
    (dh3, dh3_bf, dg_mlp1), (got_a,) = _mm_nt(
        dz1, w_1_1, 0, out_dtype=F32, name="mlp1_up_bwd", epilogue="rms_bwd",
        extra=(h3, mlp_norm_g[1:2], dh4), rider=[("swap", [g_w1_1, g_w2_1])])

    g_cd_out = as_pieces(_mm_tn(mix1, dh3_bf, 1, name="cd_out_wgrad"))
    dmix1, (got_cd_out,) = _mm_nt(dh3_bf, w_cd_out, 0, out_dtype=BF16, name="cd_out_bwd",
                                  rider=[("swap", [g_cd_out])])
    sums_a = _pair_sums([g_w1_1, g_w2_1, g_cd_out], got_a + got_cd_out, pos, "a")
    du, dv, dsgu_w, dsgu_bs, dsgu_g, dsgu_b = _sgu_bwd(p_cd, dmix1, sgu_g_full, sgu_b_full, sgu_w3, sgu_w3_t,
                                                      sgu_bias_tile)
    dq, dk, dvv, landed_a = _sb_bwd(p_cd, dmix1, ltot, nseq, t_len, sums_a)
    halves_a = _chip_sums(sums_a, landed_a, pos, "a")
    dp_cd = jnp.concatenate([du, dv, dq, dk, dvv], axis=1)
    g_cd_in, ((r_w1_1, r_w2_1, r_cd_out),) = _mm_tn(xn1, dp_cd, N_CHIP, name="cd_in_wgrad",
                                                    rider=[("join", halves_a)])
    (dh2, dh2_bf, dg_mix1), (got_c,) = _mm_nt(
        dp_cd, w_cd_in, 0, out_dtype=F32, name="cd_in_bwd", epilogue="rms_bwd",
        extra=(h2, mix_norm_g[1:2], dh3), rider=[("swap", [g_cd_in])])

    sums_c = _pair_sums([g_cd_in], got_c, pos, "c")
    dz0, (landed_c,) = _mm_nt(dh2_bf, w_2_0, 0, out_dtype=BF16, name="mlp0_down_bwd", epilogue="relu2_bwd",
                              extra=relu0, rider=[("exchange", sums_c)])
    halves_c = _chip_sums(sums_c, landed_c, pos, "c")
    g_w2_0, ((r_cd_in,),) = _mm_tn(act0, dh2_bf, 1, name="mlp0_down_wgrad", rider=[("join", halves_c)])
    g_w2_0 = as_pieces(g_w2_0)
    g_w1_0, (got_d,) = _mm_tn(hn0, dz0, N_CHIP, name="mlp0_up_wgrad", rider=[("swap", [g_w2_0])])
    sums_d = _pair_sums([g_w2_0], got_d, pos, "d")
    (dh1, dh1_bf, dg_mlp0), (landed_d, got_e) = _mm_nt(
        dz0, w_1_0, 0, out_dtype=F32, name="mlp0_up_bwd", epilogue="rms_bwd",
        extra=(h1, mlp_norm_g[0:1], dh2), rider=[("exchange", sums_d), ("swap", [g_w1_0])])
    halves_d = _chip_sums(sums_d, landed_d, pos, "d")
    sums_e = _pair_sums([g_w1_0], got_e, pos, "e")

    dmix0, ((r_w2_0,),) = _mm_nt(dh1_bf, w_ab_out, 0, out_dtype=BF16, name="ab_out_bwd", rider=[("join", halves_d)])
    g_ab_out = as_pieces(_mm_tn(mix0, dh1_bf, 1, name="ab_out_wgrad"))
    (da, dxb, dgb, dgc, dpool_w, dpool_scale, dconv_w, dconv_b), (landed_e, got_f) = _ab_bwd(
        p_ab, dmix0, pool_w3, pool_scale3, conv_w_full, conv_b2, nseq, t_len,
        rider=[("exchange", sums_e), ("swap", [g_ab_out])])
    halves_e = _chip_sums(sums_e, landed_e, pos, "e")
    sums_f = _pair_sums([g_ab_out], got_f, pos, "f")
    dp_ab = jnp.concatenate([da, dxb, dgb, dgc], axis=1)
    wide = jnp.concatenate([_pad_rows(jnp.concatenate([jnp.zeros_like(dg_mix1), dg_mix1], axis=0)),
                            _pad_rows(jnp.concatenate([dg_mlp0, dg_mlp1], axis=0)), _pad_rows(dg_final)], axis=0)
    mid = jnp.concatenate([_pad_rows(dconv_b), _pad_rows(dconv_w), _pad_rows(dsgu_g), _pad_rows(dsgu_b)], axis=0)
    narrow = jnp.concatenate(
        [dpool_w.reshape(4 * TILE, TILE), dsgu_w.reshape(4 * TILE, TILE), _pad_rows(dpool_scale.reshape(4, TILE)),
         _pad_rows(dsgu_bs[:, :, 0]), loss_tile], axis=0)
    g_ab_in, (landed_f, (r_w1_0,), (wide, mid, narrow)) = _mm_tn(
        xn0, dp_ab, N_CHIP, name="ab_in_wgrad",
        rider=[("exchange", sums_f), ("join", halves_e), ("allgather", [wide, mid, narrow])])
    halves_f = _chip_sums(sums_f, landed_f, pos, "f")
    sums_g = _pair_sums([g_ab_in], _swap_halves([g_ab_in], name="swap_halves_g"), pos, "g")
    (grad_x, _, dg_mix0), (landed_g, (r_ab_out,)) = _mm_nt(
        dp_ab, w_ab_in, 0, out_dtype=F32, name="ab_in_bwd", epilogue="rms_bwd",
        extra=(h0, mix_norm_g[0:1], dh1), rider=[("exchange", sums_g), ("join", halves_f)])
    r_ab_in, = _join_halves(_chip_sums(sums_g, landed_g, pos, "g"), name="join_halves_g")

    big_out = {
        "ab_w_in": _adam_big(ab_w_in, m_ab_w_in, v_ab_w_in, [r_ab_in], name="adam_ab_w_in"),
        "ab_w_out": _adam_big(ab_w_out, m_ab_w_out, v_ab_w_out, [r_ab_out], name="adam_ab_w_out"),
        "cd_w_in": _adam_big(cd_w_in, m_cd_w_in, v_cd_w_in, [r_cd_in], name="adam_cd_w_in"),
        "cd_w_out": _adam_big(cd_w_out, m_cd_w_out, v_cd_w_out, [r_cd_out], name="adam_cd_w_out"),
        "mlp_w1": _adam_big(mlp_w1, m_mlp_w1, v_mlp_w1, [r_w1_0, r_w1_1], name="adam_mlp_w1"),
        "mlp_w2": _adam_big(mlp_w2, m_mlp_w2, v_mlp_w2, [r_w2_0, r_w2_1], name="adam_mlp_w2"),
    }

    late, = _allreduce_small([_pad_rows(dg_mix0)])
    small_out, loss_sum = _adam_small(wide, mid, narrow, late, {
        "mix_norm_g": (mix_norm_g, m_mix_norm_g, v_mix_norm_g),
        "mlp_norm_g": (mlp_norm_g, m_mlp_norm_g, v_mlp_norm_g),
        "final_norm_g": tuple(a.reshape(1, d) for a in (final_norm_g, m_final_norm_g, v_final_norm_g)),
        "conv_b": (conv_b, m_conv_b, v_conv_b),
        "conv_w": (conv_w, m_conv_w, v_conv_w),
        "sgu_norm_g": (sgu_norm_g, m_sgu_norm_g, v_sgu_norm_g),
        "sgu_norm_b": (sgu_norm_b, m_sgu_norm_b, v_sgu_norm_b),
        "pool_w": (pool_w, m_pool_w, v_pool_w),
        "pool_scale": (pool_scale, m_pool_scale, v_pool_scale),
        "sgu_w": (sgu_w, m_sgu_w, v_sgu_w),
        "sgu_b": (sgu_b, m_sgu_b, v_sgu_b),
    })
    small_out["final_norm_g"] = [a.reshape(d) for a in small_out["final_norm_g"]]

    order = ["mix_norm_g", "mlp_norm_g", "ab_w_in", "pool_w", "pool_scale", "conv_w", "conv_b", "ab_w_out",
             "cd_w_in", "sgu_norm_g", "sgu_norm_b", "sgu_w", "sgu_b", "cd_w_out", "mlp_w1", "mlp_w2",
             "final_norm_g"]
    both = {**big_out, **small_out}
    loss = loss_sum[0, 0]
    outs = [loss, grad_x.reshape(nseq, t_len, d)]
    for kind in range(4):
        outs += [both[name][kind] for name in order]
    return tuple(outs)
```

```python
import math

import jax
import jax.numpy as jnp
from jax import lax
from jax.experimental import pallas as pl
from jax.experimental.pallas import tpu as pltpu

F32 = jnp.float32
BF16 = jnp.bfloat16
MESH = pl.DeviceIdType.MESH

EPS = 1e-6
TILE = 128
N_CHIP = 4
N_DEV = 8
VMEM_LIMIT_BYTES = 56 * 1024 * 1024

ADAM_LR = 0.001
ADAM_B1 = 0.9
ADAM_B2 = 0.999
ADAM_EPS = 1e-08
ADAM_WD = 0.01
ADAM_STEP = 10

NT_DIMS = (((1,), (1,)), ((), ()))
TN_DIMS = (((0,), (0,)), ((), ()))


def _params(sem=None):
    return pltpu.CompilerParams(dimension_semantics=sem, vmem_limit_bytes=VMEM_LIMIT_BYTES)


def _call(body, *, name, grid, in_specs, out_specs, out_shape, scratch_shapes, semantics, args, rider=None):
    if not rider:
        res = pl.pallas_call(body, name=name, grid=grid, in_specs=in_specs, out_specs=out_specs, out_shape=out_shape,
                             scratch_shapes=scratch_shapes, compiler_params=_params(semantics))(*args)
        return list(res), []
    plans = [_rider_plan(kind, arrays) for kind, arrays in rider]
    arrays = [a for _, group in rider for a in group]
    nr, n_in, n_out, n_scr = len(arrays), len(in_specs), len(out_specs), len(scratch_shapes)
    first_out, first_scr = n_in + nr, n_in + nr + n_out + nr
    last_step = math.prod(grid) - 1

    def riding(*refs):
        step = 0
        for axis, size in enumerate(grid):
            step = step * size + pl.program_id(axis)
        steps, at, sem_at = [], 0, first_scr + n_scr
        for (kind, group), (_, sems, _) in zip(rider, plans):
            k = len(group)
            steps.append(_rider_steps(kind, refs[n_in + at:n_in + at + k],
                                      refs[first_out + n_out + at:first_out + n_out + at + k],
                                      refs[sem_at:sem_at + len(sems)]))
            at, sem_at = at + k, sem_at + len(sems)
        for send, _, _ in steps:
            pl.when(step == 0)(send)
        for _, forward, _ in steps:
            if forward is not None:
                pl.when(step == last_step)(forward)
        body(*refs[:n_in], *refs[first_out:first_out + n_out], *refs[first_scr:first_scr + n_scr])
        for _, _, finish in steps:
            pl.when(step == last_step)(finish)

    aliases, at = {}, 0
    for (_, group), (_, _, aliased) in zip(rider, plans):
        if aliased:
            aliases.update({n_in + at + a: n_out + at + a for a in range(len(group))})
        at += len(group)
    res = pl.pallas_call(
        riding, name=name, grid=grid, in_specs=list(in_specs) + [ANY] * nr, out_specs=list(out_specs) + [ANY] * nr,
        out_shape=list(out_shape) + [s for shapes, _, _ in plans for s in shapes],
        scratch_shapes=list(scratch_shapes) + [s for _, sems, _ in plans for s in sems],
        input_output_aliases=aliases,
        compiler_params=pltpu.CompilerParams(dimension_semantics=("arbitrary",) * len(grid),
                                             vmem_limit_bytes=VMEM_LIMIT_BYTES, has_side_effects=True),
    )(*args, *arrays)
    rode, at = [], n_out
    for _, group in rider:
        rode.append(list(res[at:at + len(group)]))
        at += len(group)
    return list(res[:n_out]), rode


def _rider_plan(kind, arrays):
    n = len(arrays)
    same = [jax.ShapeDtypeStruct(a.shape, a.dtype) for a in arrays]
    pair = [pltpu.SemaphoreType.DMA((n,))] * 2
    if kind == "gather":
        return same, _gather_sems(n), True
    if kind == "exchange":
        return _exchange_shapes(arrays), _exchange_sems(n), False
    if kind == "swap":
        return _swap_shapes(arrays), pair, False
    if kind == "allgather":
        return ([jax.ShapeDtypeStruct((N_DEV,) + a.shape, a.dtype) for a in arrays],
                [pltpu.SemaphoreType.DMA((7 * n,))] * 2 + [pltpu.SemaphoreType.DMA((n,))], False)
    assert kind == "join"
    return same, pair, True


def _rider_steps(kind, ins, outs, sems):
    if kind == "gather":
        return _gather_steps(outs, *sems)
    if kind == "allgather":
        return _allgather_steps(ins, outs, *sems)
    if kind == "exchange":
        send, finish = _exchange_steps(ins, outs, *sems)
    elif kind == "swap":
        send, finish = _swap_steps(ins, outs, *sems)
    else:
        send, finish = _join_steps(outs, *sems)
    return send, None, finish


def _row_tile(k_dim, f32_rows=True):
    if k_dim > 1024:
        return 512
    return 1024 if f32_rows else 2048


def _mm_nn(a, b4, layer, *, out_dtype, name, epilogue=None, extra=None, norm_g=None, rider=None):
    m, k_dim = a.shape
    _, s_dim, kb, n = b4.shape
    assert kb == k_dim
    tm, tn = _row_tile(k_dim, f32_rows=epilogue == "residual" or norm_g is not None), min(n, 1024)
    assert m % tm == 0 and n % tn == 0
    npb = n // tn
    grid = (m // tm, s_dim * npb)
    n_in = 2 + (extra is not None) + (norm_g is not None)
    two_outputs = norm_g is not None or epilogue == "relu2"
    assert norm_g is None or (tn == s_dim * n and epilogue != "relu2")

    def body(*refs):
        a_ref, b_ref = refs[:2]
        e_ref = refs[2] if extra is not None else None
        g_ref = refs[n_in - 1] if norm_g is not None else None
        o_ref = refs[n_in]
        acc = jnp.dot(a_ref[...], b_ref[...], preferred_element_type=F32)
        if epilogue == "relu2":
            r = jnp.maximum(acc, 0.0)
            refs[n_in + 1][...] = r.astype(BF16)
            acc = r * r
        elif epilogue == "residual":
            acc = acc + e_ref[...]
        o_ref[...] = acc.astype(out_dtype)
        if norm_g is not None:
            rstd = lax.rsqrt(jnp.mean(acc * acc, axis=-1, keepdims=True) + EPS)
            refs[n_in + 1][...] = (acc * rstd * g_ref[...]).astype(BF16)

    in_specs = [
        pl.BlockSpec((tm, k_dim), lambda i, j: (i, 0)),
        pl.BlockSpec((None, None, k_dim, tn), lambda i, j: (layer, j // npb, 0, j % npb)),
    ]
    args = [a, b4]
    if extra is not None:
        in_specs.append(pl.BlockSpec((tm, tn), lambda i, j: (i, j)))
        args.append(extra)
    out_block = pl.BlockSpec((tm, tn), lambda i, j: (i, j))
    out_specs, out_shape = [out_block], [jax.ShapeDtypeStruct((m, s_dim * n), out_dtype)]
    if norm_g is not None:
        in_specs.append(pl.BlockSpec((1, tn), lambda i, j: (0, j)))
        args.append(norm_g)
    if two_outputs:
        out_specs.append(out_block)
        out_shape.append(jax.ShapeDtypeStruct((m, s_dim * n), BF16))
    res, rode = _call(
        body, name=name, grid=grid, in_specs=in_specs, out_specs=out_specs, out_shape=out_shape,
        scratch_shapes=[], semantics=("parallel", "parallel"), args=args, rider=rider)
    res = res if two_outputs else res[0]
    return res if rider is None else (res, rode)


def _mm_nt(a, b4, layer, *, out_dtype, name, epilogue=None, extra=None, rider=None):
    m, k_dim = a.shape
    _, s_dim, n_out, n = b4.shape
    assert k_dim == s_dim * n
    rms = epilogue == "rms_bwd"
    tm, tn = _row_tile(k_dim, f32_rows=rms or out_dtype == F32), min(n_out, 1024)
    assert m % tm == 0 and n_out % tn == 0
    grid = (m // tm, n_out // tn)
    assert not rms or tn == n_out
    extras = [] if extra is None else (list(extra) if rms else [extra])
    n_in = 2 + len(extras)

    def body(*refs):
        a_ref, b_ref = refs[:2]
        e_refs = refs[2:n_in]
        o_ref = refs[n_in]
        acc = lax.dot_general(a_ref[:, 0:n], b_ref[0], NT_DIMS, preferred_element_type=F32)
        for s in range(1, s_dim):
            acc = acc + lax.dot_general(a_ref[:, s * n:(s + 1) * n], b_ref[s], NT_DIMS, preferred_element_type=F32)
        if epilogue == "relu2_bwd":
            acc = acc * (2.0 * e_refs[0][...].astype(F32))
        if not rms:
            o_ref[...] = acc.astype(out_dtype)
        else:
            h_ref, g_ref, dres_ref = e_refs
            dhb_ref, dg_ref = refs[n_in + 1:n_in + 3]
            hv = h_ref[...]
            rstd = lax.rsqrt(jnp.mean(hv * hv, axis=-1, keepdims=True) + EPS)
            xhat = hv * rstd
            dxhat = acc * g_ref[...]
            dh = dres_ref[...] + rstd * (dxhat - xhat * jnp.mean(dxhat * xhat, axis=-1, keepdims=True))
            o_ref[...] = dh
            dhb_ref[...] = dh.astype(BF16)
            dg_part = jnp.sum(acc * xhat, axis=0, keepdims=True)
            first = pl.program_id(0) == 0

            @pl.when(first)
            def _():
                dg_ref[...] = dg_part

            @pl.when(jnp.logical_not(first))
            def _():
                dg_ref[...] += dg_part

    in_specs = [
        pl.BlockSpec((tm, k_dim), lambda i, j: (i, 0)),
        pl.BlockSpec((None, s_dim, tn, n), lambda i, j: (layer, 0, j, 0)),
    ]
    args = [a, b4] + extras
    block = pl.BlockSpec((tm, tn), lambda i, j: (i, j))
    vec = pl.BlockSpec((1, tn), lambda i, j: (0, j))
    if rms:
        in_specs += [block, vec, block]
        out_specs = [block, block, vec]
        out_shape = [jax.ShapeDtypeStruct((m, n_out), F32), jax.ShapeDtypeStruct((m, n_out), BF16),
                     jax.ShapeDtypeStruct((1, n_out), F32)]
    else:
        in_specs += [block] * len(extras)
        out_specs, out_shape = [block], [jax.ShapeDtypeStruct((m, n_out), out_dtype)]
    res, rode = _call(
        body, name=name, grid=grid, in_specs=in_specs, out_specs=out_specs, out_shape=out_shape,
        scratch_shapes=[], semantics=("arbitrary",) * 2 if rms else ("parallel", "parallel"), args=args, rider=rider)
    res = res if rms else res[0]
    return res if rider is None else (res, rode)


def _mm_tn(a, b, s_dim, *, name, rider=None):
    m, k1 = a.shape
    mb, n_all = b.shape
    assert mb == m and n_all % s_dim == 0
    n = n_all // s_dim
    tn, t1 = min(n, 1024), min(k1, 1024)
    assert k1 % t1 == 0 and n % tn == 0
    npb = n // tn
    grid = (k1 // t1, s_dim * npb)

    def body(a_ref, b_ref, o_ref):
        o_ref[...] = lax.dot_general(a_ref[...], b_ref[...], TN_DIMS, preferred_element_type=F32).astype(BF16)

    res, rode = _call(
        body, name=name, grid=grid,
        in_specs=[pl.BlockSpec((m, t1), lambda i, j: (0, i)), pl.BlockSpec((m, tn), lambda i, j: (0, j))],
        out_specs=[pl.BlockSpec((None, None, t1, tn), lambda i, j: (0, j // npb, i, j % npb))],
        out_shape=[jax.ShapeDtypeStruct((1, s_dim, k1, n), BF16)],
        scratch_shapes=[], semantics=("parallel", "parallel"), args=[a, b], rider=rider)
    return res[0] if rider is None else (res[0], rode)


ROW_TILE = 512


def _rms_fwd(h, g, *, name, rider=None):
    m, d = h.shape

    def body(h_ref, g_ref, o_ref):
        hv = h_ref[...]
        rstd = lax.rsqrt(jnp.mean(hv * hv, axis=-1, keepdims=True) + EPS)
        o_ref[...] = (hv * rstd * g_ref[...]).astype(BF16)

    res, rode = _call(
        body, name=name, grid=(m // ROW_TILE,),
        in_specs=[pl.BlockSpec((ROW_TILE, d), lambda i: (i, 0)), pl.BlockSpec((1, d), lambda i: (0, 0))],
        out_specs=[pl.BlockSpec((ROW_TILE, d), lambda i: (i, 0))], out_shape=[jax.ShapeDtypeStruct((m, d), BF16)],
        scratch_shapes=[], semantics=("parallel",), args=[h, g], rider=rider)
    return res[0] if rider is None else (res[0], rode)


def _mlp_down_loss(act, w_2, h_res, g, target):
    m, k_dim = act.shape
    d = w_2.shape[-1]
    tm = _row_tile(k_dim)

    def body(a_ref, b_ref, r_ref, g_ref, t_ref, dh_ref, dhb_ref, dg_ref, loss_ref):
        hv = jnp.dot(a_ref[...], b_ref[...], preferred_element_type=F32) + r_ref[...]
        gv = g_ref[...]
        rstd = lax.rsqrt(jnp.mean(hv * hv, axis=-1, keepdims=True) + EPS)
        xhat = hv * rstd
        err = xhat * gv - t_ref[...]
        dy = err * (1.0 / d)
        dxhat = dy * gv
        dh = rstd * (dxhat - xhat * jnp.mean(dxhat * xhat, axis=-1, keepdims=True))
        dh_ref[...] = dh
        dhb_ref[...] = dh.astype(BF16)
        dg_part = jnp.sum(dy * xhat, axis=0, keepdims=True)
        sq = jnp.sum(jnp.sum(err * err, axis=1, keepdims=True), axis=0, keepdims=True) * (0.5 / d)
        loss_part = jnp.broadcast_to(sq, (8, TILE))

        @pl.when(pl.program_id(0) == 0)
        def _():
            dg_ref[...] = dg_part
            loss_ref[...] = loss_part

        @pl.when(pl.program_id(0) > 0)
        def _():
            dg_ref[...] += dg_part
            loss_ref[...] += loss_part

    row = pl.BlockSpec((tm, d), lambda i: (i, 0))
    vec = pl.BlockSpec((1, d), lambda i: (0, 0))
    return pl.pallas_call(
        body, name="mlp1_down_loss", grid=(m // tm,),
        in_specs=[pl.BlockSpec((tm, k_dim), lambda i: (i, 0)),
                  pl.BlockSpec((None, None, k_dim, d), lambda i: (0, 0, 0, 0)), row, vec, row],
        out_specs=[row, row, vec, pl.BlockSpec((8, TILE), lambda i: (0, 0))],
        out_shape=[jax.ShapeDtypeStruct((m, d), F32), jax.ShapeDtypeStruct((m, d), BF16),
                   jax.ShapeDtypeStruct((1, d), F32), jax.ShapeDtypeStruct((8, TILE), F32)],
        compiler_params=_params(("arbitrary",)),
    )(act, w_2, h_res, g, target)


def _shift_down(x, s, t_idx):
    return jnp.where(t_idx >= s, pltpu.roll(x, s, 0), 0.0)


def _shift_up(x, s, t_idx, t_len):
    return jnp.where(t_idx < t_len - s, pltpu.roll(x, t_len - s, 0), 0.0)


def _pool_select(group, s2, s4, s8, s16):
    return jnp.where(group == 0, s2, jnp.where(group == 1, s4, jnp.where(group == 2, s8, s16)))


def _pool_count(group, t_idx):
    win = jnp.left_shift(2, group)
    return jnp.minimum(t_idx + 1, win).astype(F32)


def _pool_fwd_math(a, group, t_idx):
    s2 = a + _shift_down(a, 1, t_idx)
    s4 = s2 + _shift_down(s2, 2, t_idx)
    s8 = s4 + _shift_down(s4, 4, t_idx)
    s16 = s8 + _shift_down(s8, 8, t_idx)
    return _pool_select(group, s2, s4, s8, s16) / _pool_count(group, t_idx) - a


def _pool_bwd_math(dpooled, group, t_idx, t_len):
    e = dpooled / _pool_count(group, t_idx)
    s2 = e + _shift_up(e, 1, t_idx, t_len)
    s4 = s2 + _shift_up(s2, 2, t_idx, t_len)
    s8 = s4 + _shift_up(s4, 4, t_idx, t_len)
    s16 = s8 + _shift_up(s8, 8, t_idx, t_len)
    return _pool_select(group, s2, s4, s8, s16) - dpooled


def _conv_fwd_math(c, w_ref, b_ref, t_idx):
    return (w_ref[0:1, :] * _shift_down(c, 2, t_idx) + w_ref[1:2, :] * _shift_down(c, 1, t_idx)
            + w_ref[2:3, :] * c + b_ref[...])


def _ab_fwd(p, pool_w, pool_scale, conv_w, conv_b, nseq, t_len, rider=None):
    m = p.shape[0]
    ng = 4

    def body(a_ref, xb_ref, gb_ref, gc_ref, pw_ref, ps_ref, cw_ref, cb_ref, o_ref):
        j = pl.program_id(1)
        t_idx = lax.broadcasted_iota(jnp.int32, (t_len, TILE), 0)

        @pl.when(j < ng)
        def _():
            pooled = _pool_fwd_math(a_ref[...].astype(F32), j, t_idx)
            mixed = jnp.dot(pooled.astype(BF16), pw_ref[...].astype(BF16), preferred_element_type=F32)
            o_ref[...] = (mixed * ps_ref[...]).astype(BF16)

        @pl.when(j >= ng)
        def _():
            c = gc_ref[...].astype(F32) * xb_ref[...].astype(F32)
            y = _conv_fwd_math(c, cw_ref, cb_ref, t_idx)
            o_ref[...] = (gb_ref[...].astype(F32) * y).astype(BF16)

    def pool_j(j):
        return jnp.minimum(j, ng - 1)

    def conv_j(j):
        return jnp.maximum(j - ng, 0)

    in_specs = [
        pl.BlockSpec((t_len, TILE), lambda s, j: (s, pool_j(j))),
        pl.BlockSpec((t_len, TILE), lambda s, j: (s, ng + conv_j(j))),
        pl.BlockSpec((t_len, TILE), lambda s, j: (s, 2 * ng + conv_j(j))),
        pl.BlockSpec((t_len, TILE), lambda s, j: (s, 3 * ng + conv_j(j))),
        pl.BlockSpec((None, TILE, TILE), lambda s, j: (pool_j(j), 0, 0)),
        pl.BlockSpec((None, 1, TILE), lambda s, j: (pool_j(j), 0, 0)),
        pl.BlockSpec((3, TILE), lambda s, j: (0, conv_j(j))),
        pl.BlockSpec((1, TILE), lambda s, j: (0, conv_j(j))),
    ]
    res, rode = _call(
        body, name="ab_mixer_fwd", grid=(nseq, 2 * ng), in_specs=in_specs,
        out_specs=[pl.BlockSpec((t_len, TILE), lambda s, j: (s, j))],
        out_shape=[jax.ShapeDtypeStruct((m, 2 * ng * TILE), BF16)], scratch_shapes=[],
        semantics=("parallel", "arbitrary"), args=[p, p, p, p, pool_w, pool_scale, conv_w, conv_b], rider=rider)
    return res[0] if rider is None else (res[0], rode)


def _ab_bwd(p, dmix, pool_w, pool_scale, conv_w, conv_b, nseq, t_len, rider=None):
    m = p.shape[0]
    ng = 4

    def body(a_ref, xb_ref, gb_ref, gc_ref, dma_ref, dmb_ref, pw_ref, ps_ref, cw_ref, cb_ref,
             da_ref, dxb_ref, dgb_ref, dgc_ref, dpw_ref, dps_ref, dcw_ref, dcb_ref):
        j = pl.program_id(0)
        first = pl.program_id(1) == 0
        t_idx = lax.broadcasted_iota(jnp.int32, (t_len, TILE), 0)

        pooled = _pool_fwd_math(a_ref[...].astype(F32), j, t_idx).astype(BF16)
        w_bf = pw_ref[...].astype(BF16)
        mixed = jnp.dot(pooled, w_bf, preferred_element_type=F32)
        dm = dma_ref[...].astype(F32)
        dps = jnp.sum(dm * mixed, axis=0, keepdims=True)
        dmixed = (dm * ps_ref[...]).astype(BF16)
        dpw = lax.dot_general(pooled, dmixed, TN_DIMS, preferred_element_type=F32)
        dpooled = lax.dot_general(dmixed, w_bf, NT_DIMS, preferred_element_type=F32)
        da_ref[...] = _pool_bwd_math(dpooled, j, t_idx, t_len).astype(BF16)

        xb = xb_ref[...].astype(F32)
        gb = gb_ref[...].astype(F32)
        gc = gc_ref[...].astype(F32)
        d = dmb_ref[...].astype(F32)
        c = gc * xb
        c1 = _shift_down(c, 1, t_idx)
        c2 = _shift_down(c, 2, t_idx)
        y = cw_ref[0:1, :] * c2 + cw_ref[1:2, :] * c1 + cw_ref[2:3, :] * c + cb_ref[...]
        dgb_ref[...] = (d * y).astype(BF16)
        dy = d * gb
        dc = (cw_ref[2:3, :] * dy + cw_ref[1:2, :] * _shift_up(dy, 1, t_idx, t_len)
              + cw_ref[0:1, :] * _shift_up(dy, 2, t_idx, t_len))
        dgc_ref[...] = (dc * xb).astype(BF16)
        dxb_ref[...] = (dc * gc).astype(BF16)
        dcw = jnp.concatenate([jnp.sum(dy * c2, axis=0, keepdims=True),
                               jnp.sum(dy * c1, axis=0, keepdims=True),
                               jnp.sum(dy * c, axis=0, keepdims=True)], axis=0)
        dcb = jnp.sum(dy, axis=0, keepdims=True)

        @pl.when(first)
        def _():
            dpw_ref[...] = dpw
            dps_ref[...] = dps
            dcw_ref[...] = dcw
            dcb_ref[...] = dcb

        @pl.when(jnp.logical_not(first))
        def _():
            dpw_ref[...] += dpw
            dps_ref[...] += dps
            dcw_ref[...] += dcw
            dcb_ref[...] += dcb

    def col(k):
        return pl.BlockSpec((t_len, TILE), lambda j, s: (s, k * ng + j))

    in_specs = [
        col(0), col(1), col(2), col(3), col(0), col(1),
        pl.BlockSpec((None, TILE, TILE), lambda j, s: (j, 0, 0)),
        pl.BlockSpec((None, 1, TILE), lambda j, s: (j, 0, 0)),
        pl.BlockSpec((3, TILE), lambda j, s: (0, j)),
        pl.BlockSpec((1, TILE), lambda j, s: (0, j)),
    ]
    piece = pl.BlockSpec((t_len, TILE), lambda j, s: (s, j))
    out_specs = [
        piece, piece, piece, piece,
        pl.BlockSpec((None, TILE, TILE), lambda j, s: (j, 0, 0)),
        pl.BlockSpec((None, 1, TILE), lambda j, s: (j, 0, 0)),
        pl.BlockSpec((3, TILE), lambda j, s: (0, j)),
        pl.BlockSpec((1, TILE), lambda j, s: (0, j)),
    ]
    w = ng * TILE
    out_shape = [jax.ShapeDtypeStruct((m, w), BF16)] * 4 + [
        jax.ShapeDtypeStruct((ng, TILE, TILE), F32), jax.ShapeDtypeStruct((ng, 1, TILE), F32),
        jax.ShapeDtypeStruct((3, w), F32), jax.ShapeDtypeStruct((1, w), F32)]
    res, rode = _call(
        body, name="ab_mixer_bwd", grid=(ng, nseq), in_specs=in_specs, out_specs=out_specs, out_shape=out_shape,
        scratch_shapes=[], semantics=("parallel", "arbitrary"),
        args=[p, p, p, p, dmix, dmix, pool_w, pool_scale, conv_w, conv_b], rider=rider)
    return res if rider is None else (res, rode)


SGU_ROWS = 512
INV_SQRT2 = 1.0 / math.sqrt(2.0)
INV_SQRT_2PI = 1.0 / math.sqrt(2.0 * math.pi)


def _gelu(x):
    return 0.5 * x * (1.0 + lax.erf(x * INV_SQRT2))


def _gelu_grad(x):
    return 0.5 * (1.0 + lax.erf(x * INV_SQRT2)) + x * (INV_SQRT_2PI * jnp.exp(-0.5 * x * x))


def _causal_tile(transposed=False):
    r = lax.broadcasted_iota(jnp.int32, (TILE, TILE), 0)
    c = lax.broadcasted_iota(jnp.int32, (TILE, TILE), 1)
    return r <= c if transposed else c <= r


def _sgu_norm(v, g_ref, b_ref):
    mu = jnp.mean(v, axis=-1, keepdims=True)
    xc = v - mu
    rstd = lax.rsqrt(jnp.mean(xc * xc, axis=-1, keepdims=True) + EPS)
    xhat = xc * rstd
    return xhat, rstd, xhat * g_ref[...] + b_ref[...]


def _sgu_fwd(p, norm_g, norm_b, w_s, bias_tile):
    m = p.shape[0]
    ng = 4
    width = ng * TILE

    def body(u_ref, v_ref, g_ref, b_ref, w_ref, bias_ref, o_ref):
        u = _gelu(u_ref[...].astype(F32))
        _, _, vln = _sgu_norm(_gelu(v_ref[...].astype(F32)), g_ref, b_ref)
        vln = vln.astype(BF16)
        causal = _causal_tile()
        for g in range(ng):
            cols = slice(g * TILE, (g + 1) * TILE)
            wg = jnp.where(causal, w_ref[g], 0.0).astype(BF16)
            for n in range(SGU_ROWS // TILE):
                rows = slice(n * TILE, (n + 1) * TILE)
                s = jnp.dot(wg, vln[rows, cols], preferred_element_type=F32) + bias_ref[g]
                o_ref[rows, cols] = (u[rows, cols] * s).astype(BF16)

    vec = pl.BlockSpec((1, width), lambda i: (0, 0))
    tiles = pl.BlockSpec((ng, TILE, TILE), lambda i: (0, 0, 0))
    return pl.pallas_call(
        body, name="sgu_fwd", grid=(m // SGU_ROWS,),
        in_specs=[pl.BlockSpec((SGU_ROWS, width), lambda i: (i, 0)),
                  pl.BlockSpec((SGU_ROWS, width), lambda i: (i, 1)), vec, vec, tiles, tiles],
        out_specs=pl.BlockSpec((SGU_ROWS, width), lambda i: (i, 0)),
        out_shape=jax.ShapeDtypeStruct((m, width), BF16),
        compiler_params=_params(("parallel",)),
    )(p, p, norm_g, norm_b, w_s, bias_tile)


def _sgu_bwd(p, dmix, norm_g, norm_b, w_s, w_s_t, bias_tile):
    m = p.shape[0]
    ng = 4
    width = ng * TILE

    def body(u_ref, v_ref, dc_ref, g_ref, b_ref, w_ref, wt_ref, bias_ref,
             du_ref, dv_ref, dw_ref, dbs_ref, dg_ref, db_ref, ds_scr, dvln_scr):
        u_pre = u_ref[...].astype(F32)
        v_pre = v_ref[...].astype(F32)
        u = _gelu(u_pre)
        xhat, rstd, vln = _sgu_norm(_gelu(v_pre), g_ref, b_ref)
        vln = vln.astype(BF16)
        dc = dc_ref[...].astype(F32)
        causal = _causal_tile()
        ones = jnp.ones((TILE, TILE), BF16)
        first = pl.program_id(0) == 0
        for g in range(ng):
            cols = slice(g * TILE, (g + 1) * TILE)
            wg = jnp.where(causal, w_ref[g], 0.0).astype(BF16)
            wgt = jnp.where(_causal_tile(transposed=True), wt_ref[g], 0.0).astype(BF16)
            dw_acc = jnp.zeros((TILE, TILE), F32)
            dbs_acc = jnp.zeros((TILE, TILE), F32)
            for n in range(SGU_ROWS // TILE):
                rows = slice(n * TILE, (n + 1) * TILE)
                vt = vln[rows, cols]
                s = jnp.dot(wg, vt, preferred_element_type=F32) + bias_ref[g]
                ds_scr[rows, cols] = dc[rows, cols] * s
                ds = (dc[rows, cols] * u[rows, cols]).astype(BF16)
                dw_acc += lax.dot_general(ds, vt, NT_DIMS, preferred_element_type=F32)
                dbs_acc += jnp.dot(ds, ones, preferred_element_type=F32)
                dvln_scr[rows, cols] = jnp.dot(wgt, ds, preferred_element_type=F32)
            dw_g = jnp.where(causal, dw_acc, 0.0)

            @pl.when(first)
            def _():
                dw_ref[g] = dw_g
                dbs_ref[g] = dbs_acc

            @pl.when(jnp.logical_not(first))
            def _():
                dw_ref[g] += dw_g
                dbs_ref[g] += dbs_acc

        du_ref[...] = (ds_scr[...] * _gelu_grad(u_pre)).astype(BF16)
        dvln = dvln_scr[...]
        dxhat = dvln * g_ref[...]
        dv = rstd * (dxhat - jnp.mean(dxhat, axis=-1, keepdims=True)
                     - xhat * jnp.mean(dxhat * xhat, axis=-1, keepdims=True))
        dv_ref[...] = (dv * _gelu_grad(v_pre)).astype(BF16)
        dg_part = jnp.sum(dvln * xhat, axis=0, keepdims=True)
        db_part = jnp.sum(dvln, axis=0, keepdims=True)

        @pl.when(first)
        def _():
            dg_ref[...] = dg_part
            db_ref[...] = db_part

        @pl.when(jnp.logical_not(first))
        def _():
            dg_ref[...] += dg_part
            db_ref[...] += db_part

    vec = pl.BlockSpec((1, width), lambda i: (0, 0))
    tiles = pl.BlockSpec((ng, TILE, TILE), lambda i: (0, 0, 0))
    rows0 = pl.BlockSpec((SGU_ROWS, width), lambda i: (i, 0))
    rows1 = pl.BlockSpec((SGU_ROWS, width), lambda i: (i, 1))
    return pl.pallas_call(
        body, name="sgu_bwd", grid=(m // SGU_ROWS,),
        in_specs=[rows0, rows1, rows0, vec, vec, tiles, tiles, tiles],
        out_specs=[rows0, rows0, tiles, tiles, vec, vec],
        out_shape=[jax.ShapeDtypeStruct((m, width), BF16), jax.ShapeDtypeStruct((m, width), BF16),
                   jax.ShapeDtypeStruct((ng, TILE, TILE), F32), jax.ShapeDtypeStruct((ng, TILE, TILE), F32),
                   jax.ShapeDtypeStruct((1, width), F32), jax.ShapeDtypeStruct((1, width), F32)],
        scratch_shapes=[pltpu.VMEM((SGU_ROWS, width), F32), pltpu.VMEM((SGU_ROWS, width), F32)],
        compiler_params=_params(("arbitrary",)),
    )(p, p, dmix, norm_g, norm_b, w_s, w_s_t, bias_tile)


SB_DH = 64
SB_SCALE = 1.0 / math.sqrt(SB_DH)


SB_BLOCK = 256
SB_SUB = SB_BLOCK // TILE
SB_PASS = 4


def _split_passes(i):
    rem = i % SB_PASS
    return i // SB_PASS, rem >= 2, rem % 2 == 1


def _sum_matrix(kind):
    j = lax.broadcasted_iota(jnp.int32, (TILE, 2 * TILE), 0)
    s = lax.broadcasted_iota(jnp.int32, (TILE, 2 * TILE), 1)
    tri = {"after": j > s, "upto": j <= s, "before": j < s}[kind]
    return jnp.where(jnp.logical_or(s >= TILE, tri), 1.0, 0.0).astype(BF16)


def _strict_mask():
    r = lax.broadcasted_iota(jnp.int32, (SB_BLOCK, SB_BLOCK), 0)
    c = lax.broadcasted_iota(jnp.int32, (SB_BLOCK, SB_BLOCK), 1)
    return c < r


def _head_lanes(h):
    lane = lax.broadcasted_iota(jnp.int32, (1, TILE), 1)
    return (lane >= h * SB_DH) & (lane < (h + 1) * SB_DH)


def _softplus(z):
    return jnp.maximum(z, 0.0) + jnp.log(1.0 + jnp.exp(-jnp.abs(z)))


def _sb_fwd(p, nseq, t_len, gather):
    m = p.shape[0]
    npair = 4
    ng = len(gather)
    last_step = nseq * npair - 1

    def body(q_ref, k_ref, v_ref, *rest):
        o_ref, lt_ref = rest[ng:ng + 2]
        kh_ref, vh_ref = rest[2 * ng + 2:2 * ng + 4]
        step = pl.program_id(0) * npair + pl.program_id(1)
        send, forward, finish = _gather_steps(rest[ng + 2:2 * ng + 2], *rest[2 * ng + 4:])
        pl.when(step == 0)(send)
        pl.when(step == (last_step + 1) // 2)(forward)
        for h in range(2):
            keep = _head_lanes(h)
            kh_ref[h] = jnp.where(keep, k_ref[...], 0).astype(BF16)
            vh_ref[h] = jnp.where(keep, v_ref[...], 0).astype(BF16)
        summat = _sum_matrix("after")
        strict = _strict_mask()

        def one_pass(q, row0, nsub, diag, state):
            rows = pl.ds(row0, nsub * TILE)
            z, sp, pieces = [], [], []
            for h in range(2):
                zh = lax.dot_general(q, kh_ref[h, rows, :], NT_DIMS, preferred_element_type=F32)
                sph = _softplus(zh)
                logkeep = jnp.where(strict, -sph, 0.0) if diag else -sph
                z.append(zh)
                sp.append(sph)
                pieces += [logkeep[:, b * TILE:(b + 1) * TILE] for b in range(nsub)]
            sums = jnp.dot(jnp.concatenate(pieces, axis=0).astype(BF16), summat, preferred_element_type=F32)
            out = []
            for h in range(2):
                carry, acc = state[2 * h], state[2 * h + 1]
                after = [None] * nsub
                for b in reversed(range(nsub)):
                    part = sums[(h * nsub + b) * SB_BLOCK:(h * nsub + b + 1) * SB_BLOCK]
                    after[b] = part[:, :TILE] + carry
                    carry = carry + part[:, TILE:]
                w = jnp.exp(z[h] - sp[h] + jnp.concatenate(after, axis=1))
                if diag:
                    w = jnp.where(strict, w, 0.0)
                out += [carry, acc + jnp.dot(w.astype(BF16), vh_ref[h, rows, :], preferred_element_type=F32)]
            return tuple(out)

        def q_block(i, _):
            r0 = pl.multiple_of(i * SB_BLOCK, SB_BLOCK)
            q = q_ref[pl.ds(r0, SB_BLOCK), :] * SB_SCALE
            zero = jnp.zeros((SB_BLOCK, TILE), F32)
            state = one_pass(q, r0, SB_SUB, True, (zero,) * 4)
            full, two, one = _split_passes(i)
            state = lax.fori_loop(
                0, full,
                lambda jj, st: one_pass(q, pl.multiple_of((i - SB_PASS * (jj + 1)) * SB_BLOCK, SB_BLOCK),
                                        SB_PASS * SB_SUB, False, st),
                state)
            state = lax.cond(
                two, lambda st: one_pass(q, pl.multiple_of((i % 2) * SB_BLOCK, SB_BLOCK), 2 * SB_SUB, False, st),
                lambda st: st, state)
            state = lax.cond(one, lambda st: one_pass(q, 0, SB_SUB, False, st), lambda st: st, state)
            o_ref[pl.ds(r0, SB_BLOCK), :] = (state[1] + state[3]).astype(BF16)
            lt_ref[pl.ds(r0, SB_BLOCK), :] = jnp.where(_head_lanes(0), state[0], state[2])
            return 0

        lax.fori_loop(0, t_len // SB_BLOCK, q_block, 0)
        pl.when(step == last_step)(finish)

    def col(k):
        return pl.BlockSpec((t_len, TILE), lambda s, hp: (s, k * npair + hp))

    out = pl.BlockSpec((t_len, TILE), lambda s, hp: (s, hp))
    res = pl.pallas_call(
        body, name="stickbreak_fwd", grid=(nseq, npair), in_specs=[col(2), col(3), col(4)] + [ANY] * ng,
        out_specs=[out, out] + [ANY] * ng,
        out_shape=[jax.ShapeDtypeStruct((m, npair * TILE), BF16), jax.ShapeDtypeStruct((m, npair * TILE), F32)]
        + [jax.ShapeDtypeStruct(b.shape, b.dtype) for b in gather],
        input_output_aliases={3 + a: 2 + a for a in range(ng)},
        scratch_shapes=[pltpu.VMEM((2, t_len, TILE), BF16), pltpu.VMEM((2, t_len, TILE), BF16)] + _gather_sems(ng),
        compiler_params=pltpu.CompilerParams(dimension_semantics=("arbitrary", "arbitrary"),
                                             vmem_limit_bytes=VMEM_LIMIT_BYTES, has_side_effects=True),
    )(p, p, p, *gather)
    return res[0], res[1], res[2:]


def _sb_bwd(p, dmix, ltot, nseq, t_len, exchange):
    m = p.shape[0]
    npair = 4
    ne = len(exchange)
    last_step = nseq * npair - 1

    def body(q_ref, k_ref, v_ref, do_ref, lt_ref, *rest):
        dq_ref, dk_ref, dv_ref = rest[ne:ne + 3]
        kh_ref, vh_ref, dk_acc, dv_acc = rest[2 * ne + 3:2 * ne + 7]
        step = pl.program_id(0) * npair + pl.program_id(1)
        send, finish = _exchange_steps(rest[:ne], rest[ne + 3:2 * ne + 3], *rest[2 * ne + 7:])
        pl.when(step == 0)(send)
        for h in range(2):
            keep = _head_lanes(h)
            kh_ref[h] = jnp.where(keep, k_ref[...], 0).astype(BF16)
            vh_ref[h] = jnp.where(keep, v_ref[...], 0).astype(BF16)
        dk_acc[...] = jnp.zeros_like(dk_acc)
        dv_acc[...] = jnp.zeros_like(dv_acc)
        sum_upto = _sum_matrix("upto")
        sum_before = _sum_matrix("before")
        strict = _strict_mask()
        lane = lax.broadcasted_iota(jnp.int32, (SB_BLOCK, TILE), 1)

        def running(x, matrix, start, nsub):
            pieces = [x[h][:, b * TILE:(b + 1) * TILE] for h in range(2) for b in range(nsub)]
            sums = jnp.dot(jnp.concatenate(pieces, axis=0).astype(BF16), matrix, preferred_element_type=F32)
            wide, ends = [], []
            for h in range(2):
                total, cols = start[h], []
                for b in range(nsub):
                    part = sums[(h * nsub + b) * SB_BLOCK:(h * nsub + b + 1) * SB_BLOCK]
                    cols.append(part[:, :TILE] + total)
                    total = total + part[:, TILE:]
                wide.append(jnp.concatenate(cols, axis=1))
                ends.append(total)
            return wide, ends

        def one_pass(q, do, qh, doh, ltot, row0, nsub, diag, state):
            rows = pl.ds(row0, nsub * TILE)
            z, sp, logkeep = [], [], []
            for h in range(2):
                zh = lax.dot_general(q, kh_ref[h, rows, :], NT_DIMS, preferred_element_type=F32)
                sph = _softplus(zh)
                z.append(zh)
                sp.append(sph)
                logkeep.append(jnp.where(strict, -sph, 0.0) if diag else -sph)
            upto, sum_l = running(logkeep, sum_upto, [state[0], state[3]], nsub)
            w, g = [], []
            for h in range(2):
                wh = jnp.exp(z[h] - sp[h] + (ltot[h] - upto[h]))
                if diag:
                    wh = jnp.where(strict, wh, 0.0)
                w.append(wh)
                g.append(wh * lax.dot_general(do, vh_ref[h, rows, :], NT_DIMS, preferred_element_type=F32))
            g_before, sum_g = running(g, sum_before, [state[1], state[4]], nsub)
            out, dk_new, dv_new = [], 0.0, 0.0
            for h in range(2):
                dz = g[h] - jnp.exp(z[h] - sp[h]) * (g[h] + g_before[h])
                if diag:
                    dz = jnp.where(strict, dz, 0.0)
                dzb = dz.astype(BF16)
                dq = state[3 * h + 2] + jnp.dot(dzb, kh_ref[h, rows, :], preferred_element_type=F32)
                dk_new = dk_new + lax.dot_general(dzb, qh[h], TN_DIMS, preferred_element_type=F32)
                dv_new = dv_new + lax.dot_general(w[h].astype(BF16), doh[h], TN_DIMS, preferred_element_type=F32)
                out += [sum_l[h], sum_g[h], dq]
            dk_acc[rows, :] += dk_new
            dv_acc[rows, :] += dv_new
            return tuple(out)

        def q_block(i, _):
            r0 = pl.multiple_of(i * SB_BLOCK, SB_BLOCK)
            q = q_ref[pl.ds(r0, SB_BLOCK), :] * SB_SCALE
            do = do_ref[pl.ds(r0, SB_BLOCK), :]
            lt = lt_ref[pl.ds(r0, SB_BLOCK), :]
            qh, doh, ltot = [], [], []
            for h in range(2):
                keep = _head_lanes(h)
                qh.append(jnp.where(keep, q, 0).astype(BF16))
                doh.append(jnp.where(keep, do, 0).astype(BF16))
                ltot.append(jnp.sum(jnp.where(lane == h * SB_DH, lt, 0.0), axis=1, keepdims=True))
            zero = jnp.zeros((SB_BLOCK, TILE), F32)
            full, two, one = _split_passes(i)
            state = lax.fori_loop(
                0, full,
                lambda jj, st: one_pass(q, do, qh, doh, ltot, pl.multiple_of(SB_PASS * jj * SB_BLOCK, SB_BLOCK),
                                        SB_PASS * SB_SUB, False, st),
                (zero,) * 6)
            state = lax.cond(
                two,
                lambda st: one_pass(q, do, qh, doh, ltot, pl.multiple_of(SB_PASS * full * SB_BLOCK, SB_BLOCK),
                                    2 * SB_SUB, False, st),
                lambda st: st, state)
            state = lax.cond(
                one,
                lambda st: one_pass(q, do, qh, doh, ltot, pl.multiple_of((i - 1) * SB_BLOCK, SB_BLOCK), SB_SUB, False, st),
                lambda st: st, state)
            state = one_pass(q, do, qh, doh, ltot, r0, SB_SUB, True, state)
            dq_ref[pl.ds(r0, SB_BLOCK), :] = ((state[2] + state[5]) * SB_SCALE).astype(BF16)
            return 0

        lax.fori_loop(0, t_len // SB_BLOCK, q_block, 0)
        dk_ref[...] = dk_acc[...].astype(BF16)
        dv_ref[...] = dv_acc[...].astype(BF16)
        pl.when(step == last_step)(finish)

    def col(k):
        return pl.BlockSpec((t_len, TILE), lambda s, hp: (s, k * npair + hp))

    out = pl.BlockSpec((t_len, TILE), lambda s, hp: (s, hp))
    width = npair * TILE
    res = pl.pallas_call(
        body, name="stickbreak_bwd", grid=(nseq, npair),
        in_specs=[col(2), col(3), col(4), col(1), out] + [ANY] * ne, out_specs=[out, out, out] + [ANY] * ne,
        out_shape=[jax.ShapeDtypeStruct((m, width), BF16)] * 3 + _exchange_shapes(exchange),
        scratch_shapes=[pltpu.VMEM((2, t_len, TILE), BF16), pltpu.VMEM((2, t_len, TILE), BF16),
                        pltpu.VMEM((t_len, TILE), F32), pltpu.VMEM((t_len, TILE), F32)] + _exchange_sems(ne),
        compiler_params=pltpu.CompilerParams(dimension_semantics=("arbitrary", "arbitrary"),
                                             vmem_limit_bytes=VMEM_LIMIT_BYTES, has_side_effects=True),
    )(p, p, p, dmix, ltot, *exchange)
    return res[0], res[1], res[2], res[3:]


def _adam_math(w, g, m, v):
    m = ADAM_B1 * m + (1.0 - ADAM_B1) * g
    v = ADAM_B2 * v + (1.0 - ADAM_B2) * (g * g)
    m_hat = m / (1.0 - ADAM_B1 ** ADAM_STEP)
    v_hat = v / (1.0 - ADAM_B2 ** ADAM_STEP)
    delta = -ADAM_LR * (m_hat / (jnp.sqrt(v_hat) + ADAM_EPS) + ADAM_WD * w)
    return delta, m, v


def _cast_place(w, layer, pos, *, name):
    _, r, c = w.shape
    tr = min(r, 256)

    def body(pos_ref, w_ref, o_ref):
        o_ref[...] = w_ref[...].astype(BF16)

    grid_spec = pltpu.PrefetchScalarGridSpec(
        num_scalar_prefetch=1, grid=(r // tr,),
        in_specs=[pl.BlockSpec((None, tr, c), lambda i, pos_ref: (layer, i, 0))],
        out_specs=pl.BlockSpec((None, None, tr, c), lambda i, pos_ref: (0, pos_ref[0], i, 0)))
    return pl.pallas_call(
        body, name=name, grid_spec=grid_spec, out_shape=jax.ShapeDtypeStruct((1, N_CHIP, r, c), BF16),
        compiler_params=_params(("parallel",)),
    )(pos, w)


def _pair_sum(mine, got, pos, *, name):
    l_dim, s_dim, h, c = got.shape
    th = min(h, 512)
    nt = h // th

    def body(pos_ref, a_ref, b_ref, o_ref):
        o_ref[...] = (a_ref[...].astype(F32) + b_ref[...].astype(F32)).astype(BF16)

    spec = pl.BlockSpec((None, None, th, c), lambda l, s, i, pos_ref: (l, s, i, 0))
    grid_spec = pltpu.PrefetchScalarGridSpec(
        num_scalar_prefetch=1, grid=(l_dim, s_dim, nt),
        in_specs=[pl.BlockSpec((None, None, th, c), lambda l, s, i, pos_ref: (l, s, pos_ref[1] * nt + i, 0)), spec],
        out_specs=spec)
    return pl.pallas_call(
        body, name=name, grid_spec=grid_spec, out_shape=jax.ShapeDtypeStruct(got.shape, BF16),
        compiler_params=_params(("parallel",) * 3),
    )(pos, mine, got)


def _chip_sum(sums, landed, pos, *, name):
    l_dim, _, h, c = sums.shape
    th = min(h, 512)
    nt = h // th

    def body(pos_ref, own, r0, r1, r2, o_ref):
        o_ref[...] = ((own[...].astype(F32) + r0[...].astype(F32)) + r1[...].astype(F32)) + r2[...].astype(F32)

    def piece(k):
        return pl.BlockSpec((None, None, th, c), lambda l, i, pos_ref: (l, k, i, 0))

    grid_spec = pltpu.PrefetchScalarGridSpec(
        num_scalar_prefetch=1, grid=(l_dim, nt),
        in_specs=[pl.BlockSpec((None, None, th, c), lambda l, i, pos_ref: (l, pos_ref[0], i, 0)),
                  piece(0), piece(1), piece(2)],
        out_specs=pl.BlockSpec((None, th, c), lambda l, i, pos_ref: (l, pos_ref[1] * nt + i, 0)))
    return pl.pallas_call(
        body, name=name, grid_spec=grid_spec, out_shape=jax.ShapeDtypeStruct((l_dim, 2 * h, c), F32),
        compiler_params=_params(("parallel",) * 2),
    )(pos, sums, landed, landed, landed)


def _adam_big(w, m, v, grads, *, name):
    l_dim, r, c = w.shape
    assert len(grads) == l_dim
    tr = min(r, 256)

    def body(*refs):
        w_ref, m_ref, v_ref = refs[:3]
        g_refs = refs[3:3 + l_dim]
        go_ref, d_ref, mo_ref, vo_ref = refs[3 + l_dim:]
        g = g_refs[0][...]
        for l in range(1, l_dim):
            g = jnp.where(pl.program_id(0) == l, g_refs[l][...], g)
        delta, m_new, v_new = _adam_math(w_ref[...], g, m_ref[...], v_ref[...])
        go_ref[...] = g
        d_ref[...] = delta
        mo_ref[...] = m_new
        vo_ref[...] = v_new

    spec = pl.BlockSpec((None, tr, c), lambda l, i: (l, i, 0))
    gspec = pl.BlockSpec((None, tr, c), lambda l, i: (0, i, 0))
    return pl.pallas_call(
        body, name=name, grid=(l_dim, r // tr), in_specs=[spec] * 3 + [gspec] * l_dim, out_specs=[spec] * 4,
        out_shape=[jax.ShapeDtypeStruct(w.shape, F32)] * 4, compiler_params=_params(("parallel",) * 2),
    )(w, m, v, *grads)


def _position():
    return lax.axis_index("x"), lax.axis_index("y"), lax.axis_index("c")


def _other_chips(x, y):
    return [(1 - x, y), (x, 1 - y), (1 - x, 1 - y)]


def _remote(src, dst, send_sem, recv_sem, device):
    return pltpu.make_async_remote_copy(src_ref=src, dst_ref=dst, send_sem=send_sem, recv_sem=recv_sem,
                                        device_id=device, device_id_type=MESH)


ANY = pl.BlockSpec(memory_space=pl.ANY)


def _gather_sems(n):
    return [pltpu.SemaphoreType.DMA((3 * n,))] * 4


def _gather_steps(outs, send_sems, recv_sems, fwd_send, fwd_recv):
    n = len(outs)
    x, y, c = _position()
    chips = _other_chips(x, y)
    sibling = (x, y, 1 - c)

    def half(a, chip, core):
        h = outs[a].shape[2] // 2
        return outs[a].at[:, 2 * chip[0] + chip[1], pl.ds(core * h, h), :]

    def over_ici(a, k, chip):
        block = half(a, chip, c)
        return _remote(block, block, send_sems.at[3 * a + k], recv_sems.at[3 * a + k], (*chips[k], c))

    def over_d2d(a, k, core):
        block = half(a, chips[k], core)
        return _remote(block, block, fwd_send.at[3 * a + k], fwd_recv.at[3 * a + k], sibling)

    def send():
        for a in range(n):
            for k in range(3):
                over_ici(a, k, (x, y)).start()

    def forward():
        for k in range(3):
            for a in range(n):
                over_ici(a, k, chips[k]).wait_recv()
                over_d2d(a, k, c).start()

    def finish():
        for k in range(3):
            for a in range(n):
                over_d2d(a, k, 1 - c).wait_recv()
        for a in range(n):
            for k in range(3):
                over_ici(a, k, (x, y)).wait_send()
                over_d2d(a, k, c).wait_send()

    return send, forward, finish


def _swap_halves(grads, *, name):
    n = len(grads)

    def body(*refs):
        send, finish = _swap_steps(refs[:n], refs[n:2 * n], *refs[2 * n:])
        send()
        finish()

    sem = pltpu.SemaphoreType.DMA((n,))
    return pl.pallas_call(
        body, name=name, in_specs=[ANY] * n, out_specs=[ANY] * n, out_shape=_swap_shapes(grads),
        scratch_shapes=[sem, sem], compiler_params=pltpu.CompilerParams(has_side_effects=True),
    )(*grads)


def _swap_shapes(grads):
    return [jax.ShapeDtypeStruct(g.shape[:2] + (g.shape[2] // 2, g.shape[3]), g.dtype) for g in grads]


def _swap_steps(ins, outs, send_sems, recv_sems):
    x, y, c = _position()

    def copy(a):
        h = ins[a].shape[2] // 2
        return _remote(ins[a].at[:, :, pl.ds((1 - c) * h, h), :], outs[a], send_sems.at[a], recv_sems.at[a],
                       (x, y, 1 - c))

    def send():
        for a in range(len(ins)):
            copy(a).start()

    def finish():
        for a in range(len(ins)):
            copy(a).wait()

    return send, finish


def _exchange_shapes(sums):
    return [jax.ShapeDtypeStruct((s.shape[0], 3) + s.shape[2:], s.dtype) for s in sums]


def _exchange_sems(n):
    return [pltpu.SemaphoreType.DMA((3 * n,))] * 2


def _exchange_steps(ins, outs, send_sems, recv_sems):
    n = len(ins)
    x, y, c = _position()
    chips = _other_chips(x, y)

    def copy(a, k):
        chip = chips[k]
        return _remote(ins[a].at[:, 2 * chip[0] + chip[1]], outs[a].at[:, k],
                       send_sems.at[3 * a + k], recv_sems.at[3 * a + k], (*chip, c))

    def send():
        for a in range(n):
            for k in range(3):
                copy(a, k).start()

    def finish():
        for a in range(n):
            for k in range(3):
                copy(a, k).wait()

    return send, finish


def _join_halves(bufs, *, name):
    n = len(bufs)

    def body(*refs):
        send, finish = _join_steps(refs[n:2 * n], *refs[2 * n:])
        send()
        finish()

    sem = pltpu.SemaphoreType.DMA((n,))
    return pl.pallas_call(
        body, name=name, in_specs=[ANY] * n, out_specs=[ANY] * n,
        out_shape=[jax.ShapeDtypeStruct(b.shape, b.dtype) for b in bufs],
        input_output_aliases={a: a for a in range(n)},
        scratch_shapes=[sem, sem], compiler_params=pltpu.CompilerParams(has_side_effects=True),
    )(*bufs)


def _join_steps(outs, send_sems, recv_sems):
    x, y, c = _position()

    def copy(a, core):
        h = outs[a].shape[1] // 2
        half = outs[a].at[:, pl.ds(core * h, h), :]
        return _remote(half, half, send_sems.at[a], recv_sems.at[a], (x, y, 1 - c))

    def send():
        for a in range(len(outs)):
            copy(a, c).start()

    def finish():
        for a in range(len(outs)):
            copy(a, c).wait_send()
            copy(a, 1 - c).wait_recv()

    return send, finish


def _allgather_steps(ins, outs, send_sems, recv_sems, local_sems):
    n = len(ins)
    x, y, c = _position()
    me, sibling = (x, y, c), (x, y, 1 - c)
    chips = _other_chips(x, y)

    def slot(a, dev):
        return outs[a].at[4 * dev[0] + 2 * dev[1] + dev[2]]

    def copy(a, k, block, to, own=False):
        return _remote(ins[a] if own else slot(a, block), slot(a, block),
                       send_sems.at[7 * a + k], recv_sems.at[7 * a + k], to)

    def first(a):
        return [copy(a, 0, me, sibling, own=True)] + [copy(a, 1 + k, me, (*chips[k], c), own=True) for k in range(3)]

    def local(a):
        return pltpu.make_async_copy(ins[a], slot(a, me), local_sems.at[a])

    def send():
        for a in range(n):
            local(a).start()
            for cp in first(a):
                cp.start()

    def forward():
        for a in range(n):
            for k in range(3):
                copy(a, 1 + k, (*chips[k], c), me).wait_recv()
                copy(a, 4 + k, (*chips[k], c), sibling).start()

    def finish():
        for a in range(n):
            copy(a, 0, sibling, me).wait_recv()
            for k in range(3):
                copy(a, 4 + k, (*chips[k], 1 - c), me).wait_recv()
        for a in range(n):
            for cp in first(a) + [copy(a, 4 + k, (*chips[k], c), sibling) for k in range(3)]:
                cp.wait_send()
            local(a).wait()

    return send, forward, finish


def _allreduce_small(packs):
    n = len(packs)

    def body(*refs):
        ins, outs, gath = refs[:n], refs[n:2 * n], refs[2 * n:3 * n]
        send_sems, recv_sems = refs[3 * n:]
        x, y, c = _position()
        me, sibling = (x, y, c), (x, y, 1 - c)
        chips = _other_chips(x, y)

        def slot(a, dev):
            return gath[a].at[4 * dev[0] + 2 * dev[1] + dev[2]]

        def copy(a, k, block, to, src=None):
            return _remote(slot(a, block) if src is None else src, slot(a, block),
                           send_sems.at[7 * a + k], recv_sems.at[7 * a + k], to)

        started = []
        for a in range(n):
            slot(a, me)[...] = ins[a][...]
            first = [copy(a, 0, me, sibling, src=ins[a])]
            first += [copy(a, 1 + k, me, (*chip, c), src=ins[a]) for k, chip in enumerate(chips)]
            for cp in first:
                cp.start()
            started += first
        for a in range(n):
            for k, chip in enumerate(chips):
                copy(a, 1 + k, (*chip, c), me).wait_recv()
                cp = copy(a, 4 + k, (*chip, c), sibling)
                cp.start()
                started.append(cp)
        for a in range(n):
            copy(a, 0, sibling, me).wait_recv()
            for k, chip in enumerate(chips):
                copy(a, 4 + k, (*chip, 1 - c), me).wait_recv()
        for cp in started:
            cp.wait_send()
        for a in range(n):
            total = gath[a][0]
            for d in range(1, N_DEV):
                total = total + gath[a][d]
            outs[a][...] = total

    vmem = pl.BlockSpec(memory_space=pltpu.VMEM)
    sem = pltpu.SemaphoreType.DMA((7 * n,))
    return pl.pallas_call(
        body, name="allreduce_small", in_specs=[vmem] * n, out_specs=[vmem] * n,
        out_shape=[jax.ShapeDtypeStruct(p.shape, p.dtype) for p in packs],
        scratch_shapes=[pltpu.VMEM((N_DEV,) + p.shape, p.dtype) for p in packs] + [sem, sem],
        compiler_params=pltpu.CompilerParams(has_side_effects=True, vmem_limit_bytes=VMEM_LIMIT_BYTES),
    )(*packs)


LOSS_ROW = 1040


def _pad_rows(a, rows=8):
    return jnp.concatenate([a, jnp.zeros((rows - a.shape[0], a.shape[1]), a.dtype)], axis=0)

def _adam_small(wide, mid, narrow, late, params):
    names = ["mix_norm_g", "mlp_norm_g", "final_norm_g", "conv_b", "conv_w", "sgu_norm_g", "sgu_norm_b",
             "pool_w", "pool_scale", "sgu_w", "sgu_b"]
    n = len(names)

    def body(*refs):
        wmv = refs[4:4 + 3 * n]
        outs = refs[4 + 3 * n:]
        x, y, _ = _position()
        q = 2 * x + y

        def total(ref):
            t = ref[0]
            for dev in range(1, N_DEV):
                t = t + ref[dev]
            return t

        wide_sum, mid_sum, narrow_sum = total(refs[0]), total(refs[1]), total(refs[2])
        late_ref = refs[3]

        def my_quarter(rows):
            parts = [rows[:, s * TILE:(s + 1) * TILE] for s in range(N_CHIP)]
            return jnp.where(q == 0, parts[0], jnp.where(q == 1, parts[1], jnp.where(q == 2, parts[2], parts[3])))

        def tiles(first_row):
            return [((0, g), narrow_sum[first_row + g * TILE:first_row + (g + 1) * TILE, :]) for g in range(4)]

        grads = {
            "mix_norm_g": [((), wide_sum[0:2, :] + late_ref[0:2, :])],
            "mlp_norm_g": [((), wide_sum[8:10, :])],
            "final_norm_g": [((), wide_sum[16:17, :])],
            "conv_b": [((), mid_sum[0:1, :])],
            "conv_w": [((0,), my_quarter(mid_sum[8:11, :]))],
            "sgu_norm_g": [((), my_quarter(mid_sum[16:17, :]))],
            "sgu_norm_b": [((), my_quarter(mid_sum[24:25, :]))],
            "pool_w": tiles(0),
            "sgu_w": tiles(512),
            "pool_scale": [((0,), narrow_sum[1024:1028, :])],
            "sgu_b": [((0,), narrow_sum[1032:1036, :])],
        }
        outs[4 * n][...] = narrow_sum[LOSS_ROW:LOSS_ROW + 8, :]
        for i, name in enumerate(names):
            w_ref, m_ref, v_ref = wmv[3 * i:3 * i + 3]
            for lead, g in grads[name]:
                idx = lead + (slice(None), slice(None))
                delta, m_new, v_new = _adam_math(w_ref[idx], g, m_ref[idx], v_ref[idx])
                outs[4 * i][idx] = g
                outs[4 * i + 1][idx] = delta
                outs[4 * i + 2][idx] = m_new
                outs[4 * i + 3][idx] = v_new

    vmem = pl.BlockSpec(memory_space=pltpu.VMEM)
    args, out_shape = [wide, mid, narrow, late], []
    for name in names:
        w, m, v = params[name]
        args += [w, m, v]
        out_shape += [jax.ShapeDtypeStruct(w.shape, F32)] * 4
    out_shape.append(jax.ShapeDtypeStruct((8, TILE), F32))
    res = pl.pallas_call(
        body, name="adam_small", in_specs=[vmem] * len(args), out_specs=[vmem] * len(out_shape),
        out_shape=out_shape, compiler_params=pltpu.CompilerParams(vmem_limit_bytes=VMEM_LIMIT_BYTES),
    )(*args)
    return {name: res[4 * i:4 * i + 4] for i, name in enumerate(names)}, res[4 * n]


def _pair_sums(grads, got, pos, tag):
    return [_pair_sum(a, b, pos, name=f"pair_sum_{tag}{i}") for i, (a, b) in enumerate(zip(grads, got))]


def _chip_sums(sums, landed, pos, tag):
    return [_chip_sum(s, r, pos, name=f"chip_sum_{tag}{i}") for i, (s, r) in enumerate(zip(sums, landed))]


def kernel(x, mix_norm_g, mlp_norm_g, ab_w_in, pool_w, pool_scale, conv_w, conv_b, ab_w_out, cd_w_in, sgu_norm_g, sgu_norm_b, sgu_w, sgu_b, cd_w_out, mlp_w1, mlp_w2, final_norm_g, loss_target, m_mix_norm_g, m_mlp_norm_g, m_ab_w_in, m_pool_w, m_pool_scale, m_conv_w, m_conv_b, m_ab_w_out, m_cd_w_in, m_sgu_norm_g, m_sgu_norm_b, m_sgu_w, m_sgu_b, m_cd_w_out, m_mlp_w1, m_mlp_w2, m_final_norm_g, v_mix_norm_g, v_mlp_norm_g, v_ab_w_in, v_pool_w, v_pool_scale, v_conv_w, v_conv_b, v_ab_w_out, v_cd_w_in, v_sgu_norm_g, v_sgu_norm_b, v_sgu_w, v_sgu_b, v_cd_w_out, v_mlp_w1, v_mlp_w2, v_final_norm_g):
    nseq, t_len, d = x.shape
    m_tok = nseq * t_len
    h0 = x.reshape(m_tok, d)
    target = loss_target.reshape(m_tok, d)

    x_idx, y_idx = lax.axis_index("x"), lax.axis_index("y")
    q_idx = 2 * x_idx + y_idx
    pos = jnp.stack([q_idx, lax.axis_index("c")]).astype(jnp.int32)

    def shard_buffer(w, layer, tag):
        return _cast_place(w, layer, pos, name=f"cast_place_{tag}")

    def row_block(w):
        return w.reshape(1, 1, -1, w.shape[-1])

    later_weights = [shard_buffer(cd_w_out, 0, "cd_out"), shard_buffer(mlp_w1, 1, "w1_1"),
                     shard_buffer(mlp_w2, 1, "w2_1")]

    pool_w3, pool_scale3 = pool_w[0], pool_scale[0].reshape(4, 1, TILE)
    sgu_w3 = sgu_w[0]
    sgu_w3_t = jnp.swapaxes(sgu_w3, 1, 2)
    sgu_bias_tile = jnp.broadcast_to(sgu_b[0][:, :, None], (4, TILE, TILE))
    conv_b2 = conv_b

    def place_quarter(v):
        return lax.dynamic_update_slice(jnp.zeros((v.shape[0], 4 * TILE), F32), v, (0, q_idx * TILE))

    sharded_small = jnp.concatenate(
        [place_quarter(conv_w[0]), place_quarter(sgu_norm_g), place_quarter(sgu_norm_b),
         jnp.zeros((3, 4 * TILE), F32)], axis=0)
    sharded_small, = _allreduce_small([sharded_small])
    sharded_small = sharded_small * 0.5
    conv_w_full = sharded_small[0:3]
    sgu_g_full = sharded_small[3:4]
    sgu_b_full = sharded_small[4:5]

    xn0, ((w_ab_in,),) = _rms_fwd(h0, mix_norm_g[0:1], name="rms_fwd_mix0",
                                  rider=[("gather", [shard_buffer(ab_w_in, 0, "ab_in")])])
    p_ab, ((w_1_0,),) = _mm_nn(xn0, w_ab_in, 0, out_dtype=BF16, name="ab_in_proj",
                               rider=[("gather", [shard_buffer(mlp_w1, 0, "w1_0")])])
    mix0, ((w_ab_out,),) = _ab_fwd(p_ab, pool_w3, pool_scale3, conv_w_full, conv_b2, nseq, t_len,
                                   rider=[("gather", [shard_buffer(ab_w_out, 0, "ab_out")])])
    w_ab_out = row_block(w_ab_out)
    h1, hn0 = _mm_nn(mix0, w_ab_out, 0, out_dtype=F32, name="ab_out_proj", epilogue="residual", extra=h0,
                     norm_g=mlp_norm_g[0:1])
    (act0, relu0), ((w_2_0,),) = _mm_nn(hn0, w_1_0, 0, out_dtype=BF16, name="mlp0_up", epilogue="relu2",
                                        rider=[("gather", [shard_buffer(mlp_w2, 0, "w2_0")])])
    w_2_0 = row_block(w_2_0)
    (h2, xn1), ((w_cd_in,),) = _mm_nn(act0, w_2_0, 0, out_dtype=F32, name="mlp0_down", epilogue="residual", extra=h1,
                                      norm_g=mix_norm_g[1:2],
                                      rider=[("gather", [shard_buffer(cd_w_in, 0, "cd_in")])])

    p_cd = _mm_nn(xn1, w_cd_in, 0, out_dtype=BF16, name="cd_in_proj")
    c_out = _sgu_fwd(p_cd, sgu_g_full, sgu_b_full, sgu_w3, sgu_bias_tile)
    d_out, ltot, (w_cd_out, w_1_1, w_2_1) = _sb_fwd(p_cd, nseq, t_len, later_weights)
    w_cd_out, w_2_1 = row_block(w_cd_out), row_block(w_2_1)
    mix1 = jnp.concatenate([c_out, d_out], axis=1)
    h3, hn1 = _mm_nn(mix1, w_cd_out, 0, out_dtype=F32, name="cd_out_proj", epilogue="residual", extra=h2,
                     norm_g=mlp_norm_g[1:2])
    act1, relu1 = _mm_nn(hn1, w_1_1, 0, out_dtype=BF16, name="mlp1_up", epilogue="relu2")

    dh4, dh4_bf, dg_final, loss_tile = _mlp_down_loss(act1, w_2_1, h3, final_norm_g.reshape(1, d), target)

    def as_pieces(g):
        return g.reshape(1, N_CHIP, -1, g.shape[-1]) if g.shape[1] == 1 else g

    dz1 = _mm_nt(dh4_bf, w_2_1, 0, out_dtype=BF16, name="mlp1_down_bwd", epilogue="relu2_bwd", extra=relu1)
    g_w2_1 = as_pieces(_mm_tn(act1, dh4_bf, 1, name="mlp1_down_wgrad"))
    g_w1_1 = _mm_tn(hn1, dz1, N_CHIP, name="mlp1_up_wgrad")
    (dh3, dh3_bf, dg_mlp1), (got_a,) = _mm_nt(
        dz1, w_1_1, 0, out_dtype=F32, name="mlp1_up_bwd", epilogue="rms_bwd",
        extra=(h3, mlp_norm_g[1:2], dh4), rider=[("swap", [g_w1_1, g_w2_1])])

    g_cd_out = as_pieces(_mm_tn(mix1, dh3_bf, 1, name="cd_out_wgrad"))
    dmix1, (got_cd_out,) = _mm_nt(dh3_bf, w_cd_out, 0, out_dtype=BF16, name="cd_out_bwd",
                                  rider=[("swap", [g_cd_out])])
    sums_a = _pair_sums([g_w1_1, g_w2_1, g_cd_out], got_a + got_cd_out, pos, "a")
    du, dv, dsgu_w, dsgu_bs, dsgu_g, dsgu_b = _sgu_bwd(p_cd, dmix1, sgu_g_full, sgu_b_full, sgu_w3, sgu_w3_t,
                                                      sgu_bias_tile)
    dq, dk, dvv, landed_a = _sb_bwd(p_cd, dmix1, ltot, nseq, t_len, sums_a)
    halves_a = _chip_sums(sums_a, landed_a, pos, "a")
    dp_cd = jnp.concatenate([du, dv, dq, dk, dvv], axis=1)
    g_cd_in, ((r_w1_1, r_w2_1, r_cd_out),) = _mm_tn(xn1, dp_cd, N_CHIP, name="cd_in_wgrad",
                                                    rider=[("join", halves_a)])
    (dh2, dh2_bf, dg_mix1), (got_c,) = _mm_nt(
        dp_cd, w_cd_in, 0, out_dtype=F32, name="cd_in_bwd", epilogue="rms_bwd",
        extra=(h2, mix_norm_g[1:2], dh3), rider=[("swap", [g_cd_in])])

    sums_c = _pair_sums([g_cd_in], got_c, pos, "c")
    dz0, (landed_c,) = _mm_nt(dh2_bf, w_2_0, 0, out_dtype=BF16, name="mlp0_down_bwd", epilogue="relu2_bwd",
                              extra=relu0, rider=[("exchange", sums_c)])
    halves_c = _chip_sums(sums_c, landed_c, pos, "c")
    g_w2_0, ((r_cd_in,),) = _mm_tn(act0, dh2_bf, 1, name="mlp0_down_wgrad", rider=[("join", halves_c)])
    g_w2_0 = as_pieces(g_w2_0)
    g_w1_0, (got_d,) = _mm_tn(hn0, dz0, N_CHIP, name="mlp0_up_wgrad", rider=[("swap", [g_w2_0])])
    sums_d = _pair_sums([g_w2_0], got_d, pos, "d")
    (dh1, dh1_bf, dg_mlp0), (landed_d, got_e) = _mm_nt(
        dz0, w_1_0, 0, out_dtype=F32, name="mlp0_up_bwd", epilogue="rms_bwd",
        extra=(h1, mlp_norm_g[0:1], dh2), rider=[("exchange", sums_d), ("swap", [g_w1_0])])
    halves_d = _chip_sums(sums_d, landed_d, pos, "d")
    sums_e = _pair_sums([g_w1_0], got_e, pos, "e")

    dmix0, ((r_w2_0,),) = _mm_nt(dh1_bf, w_ab_out, 0, out_dtype=BF16, name="ab_out_bwd", rider=[("join", halves_d)])
    g_ab_out = as_pieces(_mm_tn(mix0, dh1_bf, 1, name="ab_out_wgrad"))
    (da, dxb, dgb, dgc, dpool_w, dpool_scale, dconv_w, dconv_b), (landed_e, got_f) = _ab_bwd(
        p_ab, dmix0, pool_w3, pool_scale3, conv_w_full, conv_b2, nseq, t_len,
        rider=[("exchange", sums_e), ("swap", [g_ab_out])])
    halves_e = _chip_sums(sums_e, landed_e, pos, "e")
    sums_f = _pair_sums([g_ab_out], got_f, pos, "f")
    dp_ab = jnp.concatenate([da, dxb, dgb, dgc], axis=1)
    wide = jnp.concatenate([_pad_rows(jnp.concatenate([jnp.zeros_like(dg_mix1), dg_mix1], axis=0)),
                            _pad_rows(jnp.concatenate([dg_mlp0, dg_mlp1], axis=0)), _pad_rows(dg_final)], axis=0)
    mid = jnp.concatenate([_pad_rows(dconv_b), _pad_rows(dconv_w), _pad_rows(dsgu_g), _pad_rows(dsgu_b)], axis=0)
    narrow = jnp.concatenate(
        [dpool_w.reshape(4 * TILE, TILE), dsgu_w.reshape(4 * TILE, TILE), _pad_rows(dpool_scale.reshape(4, TILE)),
         _pad_rows(dsgu_bs[:, :, 0]), loss_tile], axis=0)
    g_ab_in, (landed_f, (r_w1_0,), (wide, mid, narrow)) = _mm_tn(
        xn0, dp_ab, N_CHIP, name="ab_in_wgrad",
        rider=[("exchange", sums_f), ("join", halves_e), ("allgather", [wide, mid, narrow])])
    halves_f = _chip_sums(sums_f, landed_f, pos, "f")
    sums_g = _pair_sums([g_ab_in], _swap_halves([g_ab_in], name="swap_halves_g"), pos, "g")
    (grad_x, _, dg_mix0), (landed_g, (r_ab_out,)) = _mm_nt(
        dp_ab, w_ab_in, 0, out_dtype=F32, name="ab_in_bwd", epilogue="rms_bwd",
        extra=(h0, mix_norm_g[0:1], dh1), rider=[("exchange", sums_g), ("join", halves_f)])
    r_ab_in, = _join_halves(_chip_sums(sums_g, landed_g, pos, "g"), name="join_halves_g")

    big_out = {
        "ab_w_in": _adam_big(ab_w_in, m_ab_w_in, v_ab_w_in, [r_ab_in], name="adam_ab_w_in"),
        "ab_w_out": _adam_big(ab_w_out, m_ab_w_out, v_ab_w_out, [r_ab_out], name="adam_ab_w_out"),
        "cd_w_in": _adam_big(cd_w_in, m_cd_w_in, v_cd_w_in, [r_cd_in], name="adam_cd_w_in"),
        "cd_w_out": _adam_big(cd_w_out, m_cd_w_out, v_cd_w_out, [r_cd_out], name="adam_cd_w_out"),
        "mlp_w1": _adam_big(mlp_w1, m_mlp_w1, v_mlp_w1, [r_w1_0, r_w1_1], name="adam_mlp_w1"),
        "mlp_w2": _adam_big(mlp_w2, m_mlp_w2, v_mlp_w2, [r_w2_0, r_w2_1], name="adam_mlp_w2"),
    }

    late, = _allreduce_small([_pad_rows(dg_mix0)])
    small_out, loss_sum = _adam_small(wide, mid, narrow, late, {
        "mix_norm_g": (mix_norm_g, m_mix_norm_g, v_mix_norm_g),
        "mlp_norm_g": (mlp_norm_g, m_mlp_norm_g, v_mlp_norm_g),
        "final_norm_g": tuple(a.reshape(1, d) for a in (final_norm_g, m_final_norm_g, v_final_norm_g)),
        "conv_b": (conv_b, m_conv_b, v_conv_b),
        "conv_w": (conv_w, m_conv_w, v_conv_w),
        "sgu_norm_g": (sgu_norm_g, m_sgu_norm_g, v_sgu_norm_g),
        "sgu_norm_b": (sgu_norm_b, m_sgu_norm_b, v_sgu_norm_b),
        "pool_w": (pool_w, m_pool_w, v_pool_w),
        "pool_scale": (pool_scale, m_pool_scale, v_pool_scale),
        "sgu_w": (sgu_w, m_sgu_w, v_sgu_w),
        "sgu_b": (sgu_b, m_sgu_b, v_sgu_b),
    })
    small_out["final_norm_g"] = [a.reshape(d) for a in small_out["final_norm_g"]]

    order = ["mix_norm_g", "mlp_norm_g", "ab_w_in", "pool_w", "pool_scale", "conv_w", "conv_b", "ab_w_out",
             "cd_w_in", "sgu_norm_g", "sgu_norm_b", "sgu_w", "sgu_b", "cd_w_out", "mlp_w1", "mlp_w2",
             "final_norm_g"]
    both = {**big_out, **small_out}
    loss = loss_sum[0, 0]
    outs = [loss, grad_x.reshape(nseq, t_len, d)]
    for kind in range(4):
        outs += [both[name][kind] for name in order]
    return tuple(outs)
```

```python
import math

import jax
import jax.numpy as jnp
from jax import lax
from jax.experimental import pallas as pl
from jax.experimental.pallas import tpu as pltpu

F32 = jnp.float32
BF16 = jnp.bfloat16
MESH = pl.DeviceIdType.MESH

EPS = 1e-6
TILE = 128
N_CHIP = 4
N_DEV = 8
VMEM_LIMIT_BYTES = 56 * 1024 * 1024

ADAM_LR = 0.001
ADAM_B1 = 0.9
ADAM_B2 = 0.999
ADAM_EPS = 1e-08
ADAM_WD = 0.01
ADAM_STEP = 10

NT_DIMS = (((1,), (1,)), ((), ()))
TN_DIMS = (((0,), (0,)), ((), ()))


def _params(sem=None):
    return pltpu.CompilerParams(dimension_semantics=sem, vmem_limit_bytes=VMEM_LIMIT_BYTES)


def _call(body, *, name, grid, in_specs, out_specs, out_shape, scratch_shapes, semantics, args, rider=None,
          prefetch=None):
    npre = 0 if prefetch is None else 1

    def launch(kernel, in_specs, out_specs, out_shape, scratch_shapes, operands, aliases, params):
        if prefetch is None:
            return pl.pallas_call(kernel, name=name, grid=grid, in_specs=in_specs, out_specs=out_specs,
                                  out_shape=out_shape, scratch_shapes=scratch_shapes, input_output_aliases=aliases,
                                  compiler_params=params)(*operands)
        spec = pltpu.PrefetchScalarGridSpec(num_scalar_prefetch=1, grid=grid, in_specs=in_specs, out_specs=out_specs,
                                            scratch_shapes=scratch_shapes)
        return pl.pallas_call(kernel, name=name, grid_spec=spec, out_shape=out_shape,
                              input_output_aliases={k + 1: v for k, v in aliases.items()},
                              compiler_params=params)(prefetch, *operands)

    if not rider:
        res = launch(body, list(in_specs), list(out_specs), list(out_shape), list(scratch_shapes), args, {},
                     _params(semantics))
        return list(res), []
    plans = [_rider_plan(kind, arrays) for kind, arrays in rider]
    arrays = [a for _, group in rider for a in group]
    nr, n_in, n_out, n_scr = len(arrays), len(in_specs), len(out_specs), len(scratch_shapes)
    first_out, first_scr = n_in + nr, n_in + nr + n_out + nr
    last_step = math.prod(grid) - 1

    def riding(*refs):
        pre, refs = refs[:npre], refs[npre:]
        step = 0
        for axis, size in enumerate(grid):
            step = step * size + pl.program_id(axis)
        steps, at, sem_at = [], 0, first_scr + n_scr
        for (kind, group), (_, sems, _) in zip(rider, plans):
            k = len(group)
            steps.append(_rider_steps(kind, refs[n_in + at:n_in + at + k],
                                      refs[first_out + n_out + at:first_out + n_out + at + k],
                                      refs[sem_at:sem_at + len(sems)]))
            at, sem_at = at + k, sem_at + len(sems)
        for send, _, _ in steps:
            pl.when(step == 0)(send)
        for _, forward, _ in steps:
            if forward is not None:
                pl.when(step == last_step)(forward)
        body(*pre, *refs[:n_in], *refs[first_out:first_out + n_out], *refs[first_scr:first_scr + n_scr])
        for _, _, finish in steps:
            pl.when(step == last_step)(finish)

    aliases, at = {}, 0
    for (_, group), (_, _, aliased) in zip(rider, plans):
        if aliased:
            aliases.update({n_in + at + a: n_out + at + a for a in range(len(group))})
        at += len(group)
    res = launch(
        riding, list(in_specs) + [ANY] * nr, list(out_specs) + [ANY] * nr,
        list(out_shape) + [s for shapes, _, _ in plans for s in shapes],
        list(scratch_shapes) + [s for _, sems, _ in plans for s in sems], [*args, *arrays], aliases,
        pltpu.CompilerParams(dimension_semantics=("arbitrary",) * len(grid), vmem_limit_bytes=VMEM_LIMIT_BYTES,
                             has_side_effects=True))
    rode, at = [], n_out
    for _, group in rider:
        rode.append(list(res[at:at + len(group)]))
        at += len(group)
    return list(res[:n_out]), rode


def _rider_plan(kind, arrays):
    n = len(arrays)
    same = [jax.ShapeDtypeStruct(a.shape, a.dtype) for a in arrays]
    pair = [pltpu.SemaphoreType.DMA((n,))] * 2
    if kind == "gather":
        return same, _gather_sems(n), True
    if kind == "exchange":
        return _exchange_shapes(arrays), _exchange_sems(n), False
    if kind == "swap":
        return _swap_shapes(arrays), pair, False
    if kind == "allgather":
        return ([jax.ShapeDtypeStruct((N_DEV,) + a.shape, a.dtype) for a in arrays],
                [pltpu.SemaphoreType.DMA((7 * n,))] * 2 + [pltpu.SemaphoreType.DMA((n,))], False)
    assert kind == "join"
    return same, pair, True


def _rider_steps(kind, ins, outs, sems):
    if kind == "gather":
        return _gather_steps(outs, *sems)
    if kind == "allgather":
        return _allgather_steps(ins, outs, *sems)
    if kind == "exchange":
        send, finish = _exchange_steps(ins, outs, *sems)
    elif kind == "swap":
        send, finish = _swap_steps(ins, outs, *sems)
    else:
        send, finish = _join_steps(outs, *sems)
    return send, None, finish


def _row_tile(k_dim, roomy=False):
    if k_dim > 1024:
        return 512
    return 2048 if roomy else 1024


def _mm_nn(a, b4, layer, *, out_dtype, name, epilogue=None, extra=None, norm_g=None, rider=None):
    m, k_dim = a.shape
    _, s_dim, kb, n = b4.shape
    assert kb == k_dim
    tm = min(m, _row_tile(k_dim, roomy=epilogue != "residual" and norm_g is None and not rider))
    tn = min(n, 1024)
    assert m % tm == 0 and n % tn == 0
    npb = n // tn
    grid = (m // tm, s_dim * npb)
    n_in = 2 + (extra is not None) + (norm_g is not None)
    two_outputs = norm_g is not None or epilogue == "relu2"
    assert norm_g is None or (tn == s_dim * n and epilogue != "relu2")

    def body(*refs):
        a_ref, b_ref = refs[:2]
        e_ref = refs[2] if extra is not None else None
        g_ref = refs[n_in - 1] if norm_g is not None else None
        o_ref = refs[n_in]
        acc = jnp.dot(a_ref[...], b_ref[...], preferred_element_type=F32)
        if epilogue == "relu2":
            r = jnp.maximum(acc, 0.0)
            refs[n_in + 1][...] = r.astype(BF16)
            acc = r * r
        elif epilogue == "residual":
            acc = acc + e_ref[...]
        o_ref[...] = acc.astype(out_dtype)
        if norm_g is not None:
            rstd = lax.rsqrt(jnp.mean(acc * acc, axis=-1, keepdims=True) + EPS)
            refs[n_in + 1][...] = (acc * rstd * g_ref[...]).astype(BF16)

    in_specs = [
        pl.BlockSpec((tm, k_dim), lambda i, j: (i, 0)),
        pl.BlockSpec((None, None, k_dim, tn), lambda i, j: (layer, j // npb, 0, j % npb)),
    ]
    args = [a, b4]
    if extra is not None:
        in_specs.append(pl.BlockSpec((tm, tn), lambda i, j: (i, j)))
        args.append(extra)
    out_block = pl.BlockSpec((tm, tn), lambda i, j: (i, j))
    out_specs, out_shape = [out_block], [jax.ShapeDtypeStruct((m, s_dim * n), out_dtype)]
    if norm_g is not None:
        in_specs.append(pl.BlockSpec((1, tn), lambda i, j: (0, j)))
        args.append(norm_g)
    if two_outputs:
        out_specs.append(out_block)
        out_shape.append(jax.ShapeDtypeStruct((m, s_dim * n), BF16))
    res, rode = _call(
        body, name=name, grid=grid, in_specs=in_specs, out_specs=out_specs, out_shape=out_shape,
        scratch_shapes=[], semantics=("parallel", "parallel"), args=args, rider=rider)
    res = res if two_outputs else res[0]
    return res if rider is None else (res, rode)


def _mm_nt(a, b4, layer, *, out_dtype, name, epilogue=None, extra=None, rider=None):
    m, k_dim = a.shape
    _, s_dim, n_out, n = b4.shape
    assert k_dim == s_dim * n
    rms = epilogue == "rms_bwd"
    tm, tn = min(m, _row_tile(k_dim, roomy=not rms and out_dtype != F32 and not rider)), min(n_out, 1024)
    assert m % tm == 0 and n_out % tn == 0
    grid = (m // tm, n_out // tn)
    assert not rms or tn == n_out
    extras = [] if extra is None else (list(extra) if rms else [extra])
    n_in = 2 + len(extras)

    def body(*refs):
        a_ref, b_ref = refs[:2]
        e_refs = refs[2:n_in]
        o_ref = refs[n_in]
        acc = lax.dot_general(a_ref[:, 0:n], b_ref[0], NT_DIMS, preferred_element_type=F32)
        for s in range(1, s_dim):
            acc = acc + lax.dot_general(a_ref[:, s * n:(s + 1) * n], b_ref[s], NT_DIMS, preferred_element_type=F32)
        if epilogue == "relu2_bwd":
            acc = acc * (2.0 * e_refs[0][...].astype(F32))
        if not rms:
            o_ref[...] = acc.astype(out_dtype)
        else:
            h_ref, g_ref, dres_ref = e_refs
            dhb_ref, dg_ref = refs[n_in + 1:n_in + 3]
            hv = h_ref[...]
            rstd = lax.rsqrt(jnp.mean(hv * hv, axis=-1, keepdims=True) + EPS)
            xhat = hv * rstd
            dxhat = acc * g_ref[...]
            dh = dres_ref[...] + rstd * (dxhat - xhat * jnp.mean(dxhat * xhat, axis=-1, keepdims=True))
            o_ref[...] = dh
            dhb_ref[...] = dh.astype(BF16)
            dg_part = jnp.sum(acc * xhat, axis=0, keepdims=True)
            first = pl.program_id(0) == 0

            @pl.when(first)
            def _():
                dg_ref[...] = dg_part

            @pl.when(jnp.logical_not(first))
            def _():
                dg_ref[...] += dg_part

    in_specs = [
        pl.BlockSpec((tm, k_dim), lambda i, j: (i, 0)),
        pl.BlockSpec((None, s_dim, tn, n), lambda i, j: (layer, 0, j, 0)),
    ]
    args = [a, b4] + extras
    block = pl.BlockSpec((tm, tn), lambda i, j: (i, j))
    vec = pl.BlockSpec((1, tn), lambda i, j: (0, j))
    if rms:
        in_specs += [block, vec, block]
        out_specs = [block, block, vec]
        out_shape = [jax.ShapeDtypeStruct((m, n_out), F32), jax.ShapeDtypeStruct((m, n_out), BF16),
                     jax.ShapeDtypeStruct((1, n_out), F32)]
    else:
        in_specs += [block] * len(extras)
        out_specs, out_shape = [block], [jax.ShapeDtypeStruct((m, n_out), out_dtype)]
    res, rode = _call(
        body, name=name, grid=grid, in_specs=in_specs, out_specs=out_specs, out_shape=out_shape,
        scratch_shapes=[], semantics=("arbitrary",) * 2 if rms else ("parallel", "parallel"), args=args, rider=rider)
    res = res if rms else res[0]
    return res if rider is None else (res, rode)


def _mm_tn(a, b, s_dim, *, name, rider=None):
    m, k1 = a.shape
    mb, n_all = b.shape
    assert mb == m and n_all % s_dim == 0
    n = n_all // s_dim
    tn, t1 = min(n, 1024), min(k1, 512 if rider else 1024)
    assert k1 % t1 == 0 and n % tn == 0
    npb = n // tn
    grid = (k1 // t1, s_dim * npb)

    def body(a_ref, b_ref, o_ref):
        o_ref[...] = lax.dot_general(a_ref[...], b_ref[...], TN_DIMS, preferred_element_type=F32).astype(BF16)

    res, rode = _call(
        body, name=name, grid=grid,
        in_specs=[pl.BlockSpec((m, t1), lambda i, j: (0, i)), pl.BlockSpec((m, tn), lambda i, j: (0, j))],
        out_specs=[pl.BlockSpec((None, None, t1, tn), lambda i, j: (0, j // npb, i, j % npb))],
        out_shape=[jax.ShapeDtypeStruct((1, s_dim, k1, n), BF16)],
        scratch_shapes=[], semantics=("parallel", "parallel"), args=[a, b], rider=rider)
    return res[0] if rider is None else (res[0], rode)


ROW_TILE = 512


def _rms_fwd(h, g, *, name, rider=None):
    m, d = h.shape

    def body(h_ref, g_ref, o_ref):
        hv = h_ref[...]
        rstd = lax.rsqrt(jnp.mean(hv * hv, axis=-1, keepdims=True) + EPS)
        o_ref[...] = (hv * rstd * g_ref[...]).astype(BF16)

    res, rode = _call(
        body, name=name, grid=(m // ROW_TILE,),
        in_specs=[pl.BlockSpec((ROW_TILE, d), lambda i: (i, 0)), pl.BlockSpec((1, d), lambda i: (0, 0))],
        out_specs=[pl.BlockSpec((ROW_TILE, d), lambda i: (i, 0))], out_shape=[jax.ShapeDtypeStruct((m, d), BF16)],
        scratch_shapes=[], semantics=("parallel",), args=[h, g], rider=rider)
    return res[0] if rider is None else (res[0], rode)


def _mlp_down_loss(act, w_2, h_res, g, target):
    m, k_dim = act.shape
    d = w_2.shape[-1]
    tm = _row_tile(k_dim)

    def body(a_ref, b_ref, r_ref, g_ref, t_ref, dh_ref, dhb_ref, dg_ref, loss_ref):
        hv = jnp.dot(a_ref[...], b_ref[...], preferred_element_type=F32) + r_ref[...]
        gv = g_ref[...]
        rstd = lax.rsqrt(jnp.mean(hv * hv, axis=-1, keepdims=True) + EPS)
        xhat = hv * rstd
        err = xhat * gv - t_ref[...]
        dy = err * (1.0 / d)
        dxhat = dy * gv
        dh = rstd * (dxhat - xhat * jnp.mean(dxhat * xhat, axis=-1, keepdims=True))
        dh_ref[...] = dh
        dhb_ref[...] = dh.astype(BF16)
        dg_part = jnp.sum(dy * xhat, axis=0, keepdims=True)
        sq = jnp.sum(jnp.sum(err * err, axis=1, keepdims=True), axis=0, keepdims=True) * (0.5 / d)
        loss_part = jnp.broadcast_to(sq, (8, TILE))

        @pl.when(pl.program_id(0) == 0)
        def _():
            dg_ref[...] = dg_part
            loss_ref[...] = loss_part

        @pl.when(pl.program_id(0) > 0)
        def _():
            dg_ref[...] += dg_part
            loss_ref[...] += loss_part

    row = pl.BlockSpec((tm, d), lambda i: (i, 0))
    vec = pl.BlockSpec((1, d), lambda i: (0, 0))
    return pl.pallas_call(
        body, name="mlp1_down_loss", grid=(m // tm,),
        in_specs=[pl.BlockSpec((tm, k_dim), lambda i: (i, 0)),
                  pl.BlockSpec((None, None, k_dim, d), lambda i: (0, 0, 0, 0)), row, vec, row],
        out_specs=[row, row, vec, pl.BlockSpec((8, TILE), lambda i: (0, 0))],
        out_shape=[jax.ShapeDtypeStruct((m, d), F32), jax.ShapeDtypeStruct((m, d), BF16),
                   jax.ShapeDtypeStruct((1, d), F32), jax.ShapeDtypeStruct((8, TILE), F32)],
        compiler_params=_params(("arbitrary",)),
    )(act, w_2, h_res, g, target)


def _shift_down(x, s, t_idx):
    return jnp.where(t_idx >= s, pltpu.roll(x, s, 0), 0.0)


def _shift_up(x, s, t_idx, t_len):
    return jnp.where(t_idx < t_len - s, pltpu.roll(x, t_len - s, 0), 0.0)


def _pool_select(group, s2, s4, s8, s16):
    return jnp.where(group == 0, s2, jnp.where(group == 1, s4, jnp.where(group == 2, s8, s16)))


def _pool_count(group, t_idx):
    win = jnp.left_shift(2, group)
    return jnp.minimum(t_idx + 1, win).astype(F32)


def _pool_fwd_math(a, group, t_idx):
    s2 = a + _shift_down(a, 1, t_idx)
    s4 = s2 + _shift_down(s2, 2, t_idx)
    s8 = s4 + _shift_down(s4, 4, t_idx)
    s16 = s8 + _shift_down(s8, 8, t_idx)
    return _pool_select(group, s2, s4, s8, s16) / _pool_count(group, t_idx) - a


def _pool_bwd_math(dpooled, group, t_idx, t_len):
    e = dpooled / _pool_count(group, t_idx)
    s2 = e + _shift_up(e, 1, t_idx, t_len)
    s4 = s2 + _shift_up(s2, 2, t_idx, t_len)
    s8 = s4 + _shift_up(s4, 4, t_idx, t_len)
    s16 = s8 + _shift_up(s8, 8, t_idx, t_len)
    return _pool_select(group, s2, s4, s8, s16) - dpooled


def _conv_fwd_math(c, w_ref, b_ref, t_idx):
    return (w_ref[0:1, :] * _shift_down(c, 2, t_idx) + w_ref[1:2, :] * _shift_down(c, 1, t_idx)
            + w_ref[2:3, :] * c + b_ref[...])


def _ab_fwd(p, pool_w, pool_scale, conv_w, conv_b, nseq, t_len, rider=None):
    m = p.shape[0]
    ng = 4

    def body(a_ref, xb_ref, gb_ref, gc_ref, pw_ref, ps_ref, cw_ref, cb_ref, o_ref):
        j = pl.program_id(1)
        t_idx = lax.broadcasted_iota(jnp.int32, (t_len, TILE), 0)

        @pl.when(j < ng)
        def _():
            pooled = _pool_fwd_math(a_ref[...].astype(F32), j, t_idx)
            mixed = jnp.dot(pooled.astype(BF16), pw_ref[...].astype(BF16), preferred_element_type=F32)
            o_ref[...] = (mixed * ps_ref[...]).astype(BF16)

        @pl.when(j >= ng)
        def _():
            c = gc_ref[...].astype(F32) * xb_ref[...].astype(F32)
            y = _conv_fwd_math(c, cw_ref, cb_ref, t_idx)
            o_ref[...] = (gb_ref[...].astype(F32) * y).astype(BF16)

    def pool_j(j):
        return jnp.minimum(j, ng - 1)

    def conv_j(j):
        return jnp.maximum(j - ng, 0)

    in_specs = [
        pl.BlockSpec((t_len, TILE), lambda s, j: (s, pool_j(j))),
        pl.BlockSpec((t_len, TILE), lambda s, j: (s, ng + conv_j(j))),
        pl.BlockSpec((t_len, TILE), lambda s, j: (s, 2 * ng + conv_j(j))),
        pl.BlockSpec((t_len, TILE), lambda s, j: (s, 3 * ng + conv_j(j))),
        pl.BlockSpec((None, TILE, TILE), lambda s, j: (pool_j(j), 0, 0)),
        pl.BlockSpec((None, 1, TILE), lambda s, j: (pool_j(j), 0, 0)),
        pl.BlockSpec((3, TILE), lambda s, j: (0, conv_j(j))),
        pl.BlockSpec((1, TILE), lambda s, j: (0, conv_j(j))),
    ]
    res, rode = _call(
        body, name="ab_mixer_fwd", grid=(nseq, 2 * ng), in_specs=in_specs,
        out_specs=[pl.BlockSpec((t_len, TILE), lambda s, j: (s, j))],
        out_shape=[jax.ShapeDtypeStruct((m, 2 * ng * TILE), BF16)], scratch_shapes=[],
        semantics=("parallel", "arbitrary"), args=[p, p, p, p, pool_w, pool_scale, conv_w, conv_b], rider=rider)
    return res[0] if rider is None else (res[0], rode)


def _ab_bwd(p, dmix, pool_w, pool_scale, conv_w, conv_b, nseq, t_len, rider=None):
    m = p.shape[0]
    ng = 4

    def body(a_ref, xb_ref, gb_ref, gc_ref, dma_ref, dmb_ref, pw_ref, ps_ref, cw_ref, cb_ref,
             da_ref, dxb_ref, dgb_ref, dgc_ref, dpw_ref, dps_ref, dcw_ref, dcb_ref):
        j = pl.program_id(0)
        first = pl.program_id(1) == 0
        t_idx = lax.broadcasted_iota(jnp.int32, (t_len, TILE), 0)

        pooled = _pool_fwd_math(a_ref[...].astype(F32), j, t_idx).astype(BF16)
        w_bf = pw_ref[...].astype(BF16)
        mixed = jnp.dot(pooled, w_bf, preferred_element_type=F32)
        dm = dma_ref[...].astype(F32)
        dps = jnp.sum(dm * mixed, axis=0, keepdims=True)
        dmixed = (dm * ps_ref[...]).astype(BF16)
        dpw = lax.dot_general(pooled, dmixed, TN_DIMS, preferred_element_type=F32)
        dpooled = lax.dot_general(dmixed, w_bf, NT_DIMS, preferred_element_type=F32)
        da_ref[...] = _pool_bwd_math(dpooled, j, t_idx, t_len).astype(BF16)

        xb = xb_ref[...].astype(F32)
        gb = gb_ref[...].astype(F32)
        gc = gc_ref[...].astype(F32)
        d = dmb_ref[...].astype(F32)
        c = gc * xb
        c1 = _shift_down(c, 1, t_idx)
        c2 = _shift_down(c, 2, t_idx)
        y = cw_ref[0:1, :] * c2 + cw_ref[1:2, :] * c1 + cw_ref[2:3, :] * c + cb_ref[...]
        dgb_ref[...] = (d * y).astype(BF16)
        dy = d * gb
        dc = (cw_ref[2:3, :] * dy + cw_ref[1:2, :] * _shift_up(dy, 1, t_idx, t_len)
              + cw_ref[0:1, :] * _shift_up(dy, 2, t_idx, t_len))
        dgc_ref[...] = (dc * xb).astype(BF16)
        dxb_ref[...] = (dc * gc).astype(BF16)
        dcw = jnp.concatenate([jnp.sum(dy * c2, axis=0, keepdims=True),
                               jnp.sum(dy * c1, axis=0, keepdims=True),
                               jnp.sum(dy * c, axis=0, keepdims=True)], axis=0)
        dcb = jnp.sum(dy, axis=0, keepdims=True)

        @pl.when(first)
        def _():
            dpw_ref[...] = dpw
            dps_ref[...] = dps
            dcw_ref[...] = dcw
            dcb_ref[...] = dcb

        @pl.when(jnp.logical_not(first))
        def _():
            dpw_ref[...] += dpw
            dps_ref[...] += dps
            dcw_ref[...] += dcw
            dcb_ref[...] += dcb

    def col(k):
        return pl.BlockSpec((t_len, TILE), lambda j, s: (s, k * ng + j))

    in_specs = [
        col(0), col(1), col(2), col(3), col(0), col(1),
        pl.BlockSpec((None, TILE, TILE), lambda j, s: (j, 0, 0)),
        pl.BlockSpec((None, 1, TILE), lambda j, s: (j, 0, 0)),
        pl.BlockSpec((3, TILE), lambda j, s: (0, j)),
        pl.BlockSpec((1, TILE), lambda j, s: (0, j)),
    ]
    piece = pl.BlockSpec((t_len, TILE), lambda j, s: (s, j))
    out_specs = [
        piece, piece, piece, piece,
        pl.BlockSpec((None, TILE, TILE), lambda j, s: (j, 0, 0)),
        pl.BlockSpec((None, 1, TILE), lambda j, s: (j, 0, 0)),
        pl.BlockSpec((3, TILE), lambda j, s: (0, j)),
        pl.BlockSpec((1, TILE), lambda j, s: (0, j)),
    ]
    w = ng * TILE
    out_shape = [jax.ShapeDtypeStruct((m, w), BF16)] * 4 + [
        jax.ShapeDtypeStruct((ng, TILE, TILE), F32), jax.ShapeDtypeStruct((ng, 1, TILE), F32),
        jax.ShapeDtypeStruct((3, w), F32), jax.ShapeDtypeStruct((1, w), F32)]
    res, rode = _call(
        body, name="ab_mixer_bwd", grid=(ng, nseq), in_specs=in_specs, out_specs=out_specs, out_shape=out_shape,
        scratch_shapes=[], semantics=("parallel", "arbitrary"),
        args=[p, p, p, p, dmix, dmix, pool_w, pool_scale, conv_w, conv_b], rider=rider)
    return res if rider is None else (res, rode)


SGU_ROWS = 512
INV_SQRT2 = 1.0 / math.sqrt(2.0)
INV_SQRT_2PI = 1.0 / math.sqrt(2.0 * math.pi)


def _gelu(x):
    return 0.5 * x * (1.0 + lax.erf(x * INV_SQRT2))


def _gelu_grad(x):
    return 0.5 * (1.0 + lax.erf(x * INV_SQRT2)) + x * (INV_SQRT_2PI * jnp.exp(-0.5 * x * x))


def _causal_tile(transposed=False):
    r = lax.broadcasted_iota(jnp.int32, (TILE, TILE), 0)
    c = lax.broadcasted_iota(jnp.int32, (TILE, TILE), 1)
    return r <= c if transposed else c <= r


def _sgu_norm(v, g_ref, b_ref):
    mu = jnp.mean(v, axis=-1, keepdims=True)
    xc = v - mu
    rstd = lax.rsqrt(jnp.mean(xc * xc, axis=-1, keepdims=True) + EPS)
    xhat = xc * rstd
    return xhat, rstd, xhat * g_ref[...] + b_ref[...]


def _sgu_fwd(p, norm_g, norm_b, w_s, bias_tile):
    m = p.shape[0]
    ng = 4
    width = ng * TILE

    def body(u_ref, v_ref, g_ref, b_ref, w_ref, bias_ref, o_ref):
        u = _gelu(u_ref[...].astype(F32))
        _, _, vln = _sgu_norm(_gelu(v_ref[...].astype(F32)), g_ref, b_ref)
        vln = vln.astype(BF16)
        causal = _causal_tile()
        for g in range(ng):
            cols = slice(g * TILE, (g + 1) * TILE)
            wg = jnp.where(causal, w_ref[g], 0.0).astype(BF16)
            for n in range(SGU_ROWS // TILE):
                rows = slice(n * TILE, (n + 1) * TILE)
                s = jnp.dot(wg, vln[rows, cols], preferred_element_type=F32) + bias_ref[g]
                o_ref[rows, cols] = (u[rows, cols] * s).astype(BF16)

    vec = pl.BlockSpec((1, width), lambda i: (0, 0))
    tiles = pl.BlockSpec((ng, TILE, TILE), lambda i: (0, 0, 0))
    return pl.pallas_call(
        body, name="sgu_fwd", grid=(m // SGU_ROWS,),
        in_specs=[pl.BlockSpec((SGU_ROWS, width), lambda i: (i, 0)),
                  pl.BlockSpec((SGU_ROWS, width), lambda i: (i, 1)), vec, vec, tiles, tiles],
        out_specs=pl.BlockSpec((SGU_ROWS, width), lambda i: (i, 0)),
        out_shape=jax.ShapeDtypeStruct((m, width), BF16),
        compiler_params=_params(("parallel",)),
    )(p, p, norm_g, norm_b, w_s, bias_tile)


def _sgu_bwd(p, dmix, norm_g, norm_b, w_s, w_s_t, bias_tile):
    m = p.shape[0]
    ng = 4
    width = ng * TILE

    def body(u_ref, v_ref, dc_ref, g_ref, b_ref, w_ref, wt_ref, bias_ref,
             du_ref, dv_ref, dw_ref, dbs_ref, dg_ref, db_ref, ds_scr, dvln_scr):
        u_pre = u_ref[...].astype(F32)
        v_pre = v_ref[...].astype(F32)
        u = _gelu(u_pre)
        xhat, rstd, vln = _sgu_norm(_gelu(v_pre), g_ref, b_ref)
        vln = vln.astype(BF16)
        dc = dc_ref[...].astype(F32)
        causal = _causal_tile()
        ones = jnp.ones((TILE, TILE), BF16)
        first = pl.program_id(0) == 0
        for g in range(ng):
            cols = slice(g * TILE, (g + 1) * TILE)
            wg = jnp.where(causal, w_ref[g], 0.0).astype(BF16)
            wgt = jnp.where(_causal_tile(transposed=True), wt_ref[g], 0.0).astype(BF16)
            dw_acc = jnp.zeros((TILE, TILE), F32)
            dbs_acc = jnp.zeros((TILE, TILE), F32)
            for n in range(SGU_ROWS // TILE):
                rows = slice(n * TILE, (n + 1) * TILE)
                vt = vln[rows, cols]
                s = jnp.dot(wg, vt, preferred_element_type=F32) + bias_ref[g]
                ds_scr[rows, cols] = dc[rows, cols] * s
                ds = (dc[rows, cols] * u[rows, cols]).astype(BF16)
                dw_acc += lax.dot_general(ds, vt, NT_DIMS, preferred_element_type=F32)
                dbs_acc += jnp.dot(ds, ones, preferred_element_type=F32)
                dvln_scr[rows, cols] = jnp.dot(wgt, ds, preferred_element_type=F32)
            dw_g = jnp.where(causal, dw_acc, 0.0)

            @pl.when(first)
            def _():
                dw_ref[g] = dw_g
                dbs_ref[g] = dbs_acc

            @pl.when(jnp.logical_not(first))
            def _():
                dw_ref[g] += dw_g
                dbs_ref[g] += dbs_acc

        du_ref[...] = (ds_scr[...] * _gelu_grad(u_pre)).astype(BF16)
        dvln = dvln_scr[...]
        dxhat = dvln * g_ref[...]
        dv = rstd * (dxhat - jnp.mean(dxhat, axis=-1, keepdims=True)
                     - xhat * jnp.mean(dxhat * xhat, axis=-1, keepdims=True))
        dv_ref[...] = (dv * _gelu_grad(v_pre)).astype(BF16)
        dg_part = jnp.sum(dvln * xhat, axis=0, keepdims=True)
        db_part = jnp.sum(dvln, axis=0, keepdims=True)

        @pl.when(first)
        def _():
            dg_ref[...] = dg_part
            db_ref[...] = db_part

        @pl.when(jnp.logical_not(first))
        def _():
            dg_ref[...] += dg_part
            db_ref[...] += db_part

    vec = pl.BlockSpec((1, width), lambda i: (0, 0))
    tiles = pl.BlockSpec((ng, TILE, TILE), lambda i: (0, 0, 0))
    rows0 = pl.BlockSpec((SGU_ROWS, width), lambda i: (i, 0))
    rows1 = pl.BlockSpec((SGU_ROWS, width), lambda i: (i, 1))
    return pl.pallas_call(
        body, name="sgu_bwd", grid=(m // SGU_ROWS,),
        in_specs=[rows0, rows1, rows0, vec, vec, tiles, tiles, tiles],
        out_specs=[rows0, rows0, tiles, tiles, vec, vec],
        out_shape=[jax.ShapeDtypeStruct((m, width), BF16), jax.ShapeDtypeStruct((m, width), BF16),
                   jax.ShapeDtypeStruct((ng, TILE, TILE), F32), jax.ShapeDtypeStruct((ng, TILE, TILE), F32),
                   jax.ShapeDtypeStruct((1, width), F32), jax.ShapeDtypeStruct((1, width), F32)],
        scratch_shapes=[pltpu.VMEM((SGU_ROWS, width), F32), pltpu.VMEM((SGU_ROWS, width), F32)],
        compiler_params=_params(("arbitrary",)),
    )(p, p, dmix, norm_g, norm_b, w_s, w_s_t, bias_tile)


SB_DH = 64
SB_SCALE = 1.0 / math.sqrt(SB_DH)


SB_BLOCK = 256
SB_SUB = SB_BLOCK // TILE
SB_PASS = 4


def _split_passes(i):
    rem = i % SB_PASS
    return i // SB_PASS, rem >= 2, rem % 2 == 1


def _sum_matrix(kind):
    j = lax.broadcasted_iota(jnp.int32, (TILE, 2 * TILE), 0)
    s = lax.broadcasted_iota(jnp.int32, (TILE, 2 * TILE), 1)
    tri = {"after": j > s, "upto": j <= s, "before": j < s}[kind]
    return jnp.where(jnp.logical_or(s >= TILE, tri), 1.0, 0.0).astype(BF16)


def _strict_mask():
    r = lax.broadcasted_iota(jnp.int32, (SB_BLOCK, SB_BLOCK), 0)
    c = lax.broadcasted_iota(jnp.int32, (SB_BLOCK, SB_BLOCK), 1)
    return c < r


def _head_lanes(h):
    lane = lax.broadcasted_iota(jnp.int32, (1, TILE), 1)
    return (lane >= h * SB_DH) & (lane < (h + 1) * SB_DH)


def _softplus(z):
    return jnp.maximum(z, 0.0) + jnp.log(1.0 + jnp.exp(-jnp.abs(z)))


def _sb_fwd(p, nseq, t_len, gather):
    m = p.shape[0]
    npair = 4
    ng = len(gather)
    last_step = nseq * npair - 1

    def body(q_ref, k_ref, v_ref, *rest):
        o_ref, lt_ref = rest[ng:ng + 2]
        kh_ref, vh_ref = rest[2 * ng + 2:2 * ng + 4]
        step = pl.program_id(0) * npair + pl.program_id(1)
        send, forward, finish = _gather_steps(rest[ng + 2:2 * ng + 2], *rest[2 * ng + 4:])
        pl.when(step == 0)(send)
        pl.when(step == (last_step + 1) // 2)(forward)
        for h in range(2):
            keep = _head_lanes(h)
            kh_ref[h] = jnp.where(keep, k_ref[...], 0).astype(BF16)
            vh_ref[h] = jnp.where(keep, v_ref[...], 0).astype(BF16)
        summat = _sum_matrix("after")
        strict = _strict_mask()

        def one_pass(q, row0, nsub, diag, state):
            rows = pl.ds(row0, nsub * TILE)
            z, sp, pieces = [], [], []
            for h in range(2):
                zh = lax.dot_general(q, kh_ref[h, rows, :], NT_DIMS, preferred_element_type=F32)
                sph = _softplus(zh)
                logkeep = jnp.where(strict, -sph, 0.0) if diag else -sph
                z.append(zh)
                sp.append(sph)
                pieces += [logkeep[:, b * TILE:(b + 1) * TILE] for b in range(nsub)]
            sums = jnp.dot(jnp.concatenate(pieces, axis=0).astype(BF16), summat, preferred_element_type=F32)
            out = []
            for h in range(2):
                carry, acc = state[2 * h], state[2 * h + 1]
                after = [None] * nsub
                for b in reversed(range(nsub)):
                    part = sums[(h * nsub + b) * SB_BLOCK:(h * nsub + b + 1) * SB_BLOCK]
                    after[b] = part[:, :TILE] + carry
                    carry = carry + part[:, TILE:]
                w = jnp.exp(z[h] - sp[h] + jnp.concatenate(after, axis=1))
                if diag:
                    w = jnp.where(strict, w, 0.0)
                out += [carry, acc + jnp.dot(w.astype(BF16), vh_ref[h, rows, :], preferred_element_type=F32)]
            return tuple(out)

        def q_block(i, _):
            r0 = pl.multiple_of(i * SB_BLOCK, SB_BLOCK)
            q = q_ref[pl.ds(r0, SB_BLOCK), :] * SB_SCALE
            zero = jnp.zeros((SB_BLOCK, TILE), F32)
            state = one_pass(q, r0, SB_SUB, True, (zero,) * 4)
            full, two, one = _split_passes(i)
            state = lax.fori_loop(
                0, full,
                lambda jj, st: one_pass(q, pl.multiple_of((i - SB_PASS * (jj + 1)) * SB_BLOCK, SB_BLOCK),
                                        SB_PASS * SB_SUB, False, st),
                state)
            state = lax.cond(
                two, lambda st: one_pass(q, pl.multiple_of((i % 2) * SB_BLOCK, SB_BLOCK), 2 * SB_SUB, False, st),
                lambda st: st, state)
            state = lax.cond(one, lambda st: one_pass(q, 0, SB_SUB, False, st), lambda st: st, state)
            o_ref[pl.ds(r0, SB_BLOCK), :] = (state[1] + state[3]).astype(BF16)
            lt_ref[pl.ds(r0, SB_BLOCK), :] = jnp.where(_head_lanes(0), state[0], state[2])
            return 0

        lax.fori_loop(0, t_len // SB_BLOCK, q_block, 0)
        pl.when(step == last_step)(finish)

    def col(k):
        return pl.BlockSpec((t_len, TILE), lambda s, hp: (s, k * npair + hp))

    out = pl.BlockSpec((t_len, TILE), lambda s, hp: (s, hp))
    res = pl.pallas_call(
        body, name="stickbreak_fwd", grid=(nseq, npair), in_specs=[col(2), col(3), col(4)] + [ANY] * ng,
        out_specs=[out, out] + [ANY] * ng,
        out_shape=[jax.ShapeDtypeStruct((m, npair * TILE), BF16), jax.ShapeDtypeStruct((m, npair * TILE), F32)]
        + [jax.ShapeDtypeStruct(b.shape, b.dtype) for b in gather],
        input_output_aliases={3 + a: 2 + a for a in range(ng)},
        scratch_shapes=[pltpu.VMEM((2, t_len, TILE), BF16), pltpu.VMEM((2, t_len, TILE), BF16)] + _gather_sems(ng),
        compiler_params=pltpu.CompilerParams(dimension_semantics=("arbitrary", "arbitrary"),
                                             vmem_limit_bytes=VMEM_LIMIT_BYTES, has_side_effects=True),
    )(p, p, p, *gather)
    return res[0], res[1], res[2:]


def _sb_bwd(p, dmix, ltot, nseq, t_len, exchange):
    m = p.shape[0]
    npair = 4
    ne = len(exchange)
    last_step = nseq * npair - 1

    def body(q_ref, k_ref, v_ref, do_ref, lt_ref, *rest):
        dq_ref, dk_ref, dv_ref = rest[ne:ne + 3]
        kh_ref, vh_ref, dk_acc, dv_acc = rest[2 * ne + 3:2 * ne + 7]
        step = pl.program_id(0) * npair + pl.program_id(1)
        send, finish = _exchange_steps(rest[:ne], rest[ne + 3:2 * ne + 3], *rest[2 * ne + 7:])
        pl.when(step == 0)(send)
        for h in range(2):
            keep = _head_lanes(h)
            kh_ref[h] = jnp.where(keep, k_ref[...], 0).astype(BF16)
            vh_ref[h] = jnp.where(keep, v_ref[...], 0).astype(BF16)
        dk_acc[...] = jnp.zeros_like(dk_acc)
        dv_acc[...] = jnp.zeros_like(dv_acc)
        sum_upto = _sum_matrix("upto")
        sum_before = _sum_matrix("before")
        strict = _strict_mask()
        lane = lax.broadcasted_iota(jnp.int32, (SB_BLOCK, TILE), 1)

        def running(x, matrix, start, nsub):
            pieces = [x[h][:, b * TILE:(b + 1) * TILE] for h in range(2) for b in range(nsub)]
            sums = jnp.dot(jnp.concatenate(pieces, axis=0).astype(BF16), matrix, preferred_element_type=F32)
            wide, ends = [], []
            for h in range(2):
                total, cols = start[h], []
                for b in range(nsub):
                    part = sums[(h * nsub + b) * SB_BLOCK:(h * nsub + b + 1) * SB_BLOCK]
                    cols.append(part[:, :TILE] + total)
                    total = total + part[:, TILE:]
                wide.append(jnp.concatenate(cols, axis=1))
                ends.append(total)
            return wide, ends

        def one_pass(q, do, qh, doh, ltot, row0, nsub, diag, state):
            rows = pl.ds(row0, nsub * TILE)
            z, sp, logkeep = [], [], []
            for h in range(2):
                zh = lax.dot_general(q, kh_ref[h, rows, :], NT_DIMS, preferred_element_type=F32)
                sph = _softplus(zh)
                z.append(zh)
                sp.append(sph)
                logkeep.append(jnp.where(strict, -sph, 0.0) if diag else -sph)
            upto, sum_l = running(logkeep, sum_upto, [state[0], state[3]], nsub)
            w, g = [], []
            for h in range(2):
                wh = jnp.exp(z[h] - sp[h] + (ltot[h] - upto[h]))
                if diag:
                    wh = jnp.where(strict, wh, 0.0)
                w.append(wh)
                g.append(wh * lax.dot_general(do, vh_ref[h, rows, :], NT_DIMS, preferred_element_type=F32))
            g_before, sum_g = running(g, sum_before, [state[1], state[4]], nsub)
            out, dk_new, dv_new = [], 0.0, 0.0
            for h in range(2):
                dz = g[h] - jnp.exp(z[h] - sp[h]) * (g[h] + g_before[h])
                if diag:
                    dz = jnp.where(strict, dz, 0.0)
                dzb = dz.astype(BF16)
                dq = state[3 * h + 2] + jnp.dot(dzb, kh_ref[h, rows, :], preferred_element_type=F32)
                dk_new = dk_new + lax.dot_general(dzb, qh[h], TN_DIMS, preferred_element_type=F32)
                dv_new = dv_new + lax.dot_general(w[h].astype(BF16), doh[h], TN_DIMS, preferred_element_type=F32)
                out += [sum_l[h], sum_g[h], dq]
            dk_acc[rows, :] += dk_new
            dv_acc[rows, :] += dv_new
            return tuple(out)

        def q_block(i, _):
            r0 = pl.multiple_of(i * SB_BLOCK, SB_BLOCK)
            q = q_ref[pl.ds(r0, SB_BLOCK), :] * SB_SCALE
            do = do_ref[pl.ds(r0, SB_BLOCK), :]
            lt = lt_ref[pl.ds(r0, SB_BLOCK), :]
            qh, doh, ltot = [], [], []
            for h in range(2):
                keep = _head_lanes(h)
                qh.append(jnp.where(keep, q, 0).astype(BF16))
                doh.append(jnp.where(keep, do, 0).astype(BF16))
                ltot.append(jnp.sum(jnp.where(lane == h * SB_DH, lt, 0.0), axis=1, keepdims=True))
            zero = jnp.zeros((SB_BLOCK, TILE), F32)
            full, two, one = _split_passes(i)
            state = lax.fori_loop(
                0, full,
                lambda jj, st: one_pass(q, do, qh, doh, ltot, pl.multiple_of(SB_PASS * jj * SB_BLOCK, SB_BLOCK),
                                        SB_PASS * SB_SUB, False, st),
                (zero,) * 6)
            state = lax.cond(
                two,
                lambda st: one_pass(q, do, qh, doh, ltot, pl.multiple_of(SB_PASS * full * SB_BLOCK, SB_BLOCK),
                                    2 * SB_SUB, False, st),
                lambda st: st, state)
            state = lax.cond(
                one,
                lambda st: one_pass(q, do, qh, doh, ltot, pl.multiple_of((i - 1) * SB_BLOCK, SB_BLOCK), SB_SUB, False, st),
                lambda st: st, state)
            state = one_pass(q, do, qh, doh, ltot, r0, SB_SUB, True, state)
            dq_ref[pl.ds(r0, SB_BLOCK), :] = ((state[2] + state[5]) * SB_SCALE).astype(BF16)
            return 0

        lax.fori_loop(0, t_len // SB_BLOCK, q_block, 0)
        dk_ref[...] = dk_acc[...].astype(BF16)
        dv_ref[...] = dv_acc[...].astype(BF16)
        pl.when(step == last_step)(finish)

    def col(k):
        return pl.BlockSpec((t_len, TILE), lambda s, hp: (s, k * npair + hp))

    out = pl.BlockSpec((t_len, TILE), lambda s, hp: (s, hp))
    width = npair * TILE
    res = pl.pallas_call(
        body, name="stickbreak_bwd", grid=(nseq, npair),
        in_specs=[col(2), col(3), col(4), col(1), out] + [ANY] * ne, out_specs=[out, out, out] + [ANY] * ne,
        out_shape=[jax.ShapeDtypeStruct((m, width), BF16)] * 3 + _exchange_shapes(exchange),
        scratch_shapes=[pltpu.VMEM((2, t_len, TILE), BF16), pltpu.VMEM((2, t_len, TILE), BF16),
                        pltpu.VMEM((t_len, TILE), F32), pltpu.VMEM((t_len, TILE), F32)] + _exchange_sems(ne),
        compiler_params=pltpu.CompilerParams(dimension_semantics=("arbitrary", "arbitrary"),
                                             vmem_limit_bytes=VMEM_LIMIT_BYTES, has_side_effects=True),
    )(p, p, p, dmix, ltot, *exchange)
    return res[0], res[1], res[2], res[3:]


def _adam_math(w, g, m, v):
    m = ADAM_B1 * m + (1.0 - ADAM_B1) * g
    v = ADAM_B2 * v + (1.0 - ADAM_B2) * (g * g)
    m_hat = m / (1.0 - ADAM_B1 ** ADAM_STEP)
    v_hat = v / (1.0 - ADAM_B2 ** ADAM_STEP)
    delta = -ADAM_LR * (m_hat / (jnp.sqrt(v_hat) + ADAM_EPS) + ADAM_WD * w)
    return delta, m, v


def _cast_place(w, layer, pos, *, name):
    _, r, c = w.shape
    tr = min(r, 256)

    def body(pos_ref, w_ref, o_ref):
        o_ref[...] = w_ref[...].astype(BF16)

    grid_spec = pltpu.PrefetchScalarGridSpec(
        num_scalar_prefetch=1, grid=(r // tr,),
        in_specs=[pl.BlockSpec((None, tr, c), lambda i, pos_ref: (layer, i, 0))],
        out_specs=pl.BlockSpec((None, None, tr, c), lambda i, pos_ref: (0, pos_ref[0], i, 0)))
    return pl.pallas_call(
        body, name=name, grid_spec=grid_spec, out_shape=jax.ShapeDtypeStruct((1, N_CHIP, r, c), BF16),
        compiler_params=_params(("parallel",)),
    )(pos, w)


def _cast_place_all(items, pos, *, name, rider=None):
    tiles = [min(w.shape[1], 256) for w, _ in items]
    counts = [w.shape[1] // t for (w, _), t in zip(items, tiles)]
    starts = [sum(counts[:a]) for a in range(len(items))]
    n = len(items)

    def body(pos_ref, *refs):
        i = pl.program_id(0)
        for a in range(n):
            @pl.when((i >= starts[a]) & (i < starts[a] + counts[a]))
            def _():
                refs[n + a][...] = refs[a][...].astype(BF16)

    def block(a):
        return lambda i: jnp.clip(i - starts[a], 0, counts[a] - 1)

    in_specs, out_specs, out_shape = [], [], []
    for a, ((w, layer), t) in enumerate(zip(items, tiles)):
        _, r, c = w.shape
        in_specs.append(pl.BlockSpec((None, t, c), lambda i, pos_ref, a=a, layer=layer: (layer, block(a)(i), 0)))
        out_specs.append(pl.BlockSpec((None, None, t, c), lambda i, pos_ref, a=a: (0, pos_ref[0], block(a)(i), 0)))
        out_shape.append(jax.ShapeDtypeStruct((1, N_CHIP, r, c), BF16))
    res, rode = _call(body, name=name, grid=(sum(counts),), in_specs=in_specs, out_specs=out_specs,
                      out_shape=out_shape, scratch_shapes=[], semantics=("arbitrary",),
                      args=[w for w, _ in items], rider=rider, prefetch=pos)
    return res if rider is None else (res, rode)


def _pair_sum(mine, got, pos, *, name):
    l_dim, s_dim, h, c = got.shape
    th = min(h, 512)
    nt = h // th

    def body(pos_ref, a_ref, b_ref, o_ref):
        o_ref[...] = (a_ref[...].astype(F32) + b_ref[...].astype(F32)).astype(BF16)

    spec = pl.BlockSpec((None, None, th, c), lambda l, s, i, pos_ref: (l, s, i, 0))
    grid_spec = pltpu.PrefetchScalarGridSpec(
        num_scalar_prefetch=1, grid=(l_dim, s_dim, nt),
        in_specs=[pl.BlockSpec((None, None, th, c), lambda l, s, i, pos_ref: (l, s, pos_ref[1] * nt + i, 0)), spec],
        out_specs=spec)
    return pl.pallas_call(
        body, name=name, grid_spec=grid_spec, out_shape=jax.ShapeDtypeStruct(got.shape, BF16),
        compiler_params=_params(("parallel",) * 3),
    )(pos, mine, got)


def _chip_sum(sums, landed, pos, *, name):
    l_dim, _, h, c = sums.shape
    th = min(h, 512)
    nt = h // th

    def body(pos_ref, own, r0, r1, r2, o_ref):
        o_ref[...] = ((own[...].astype(F32) + r0[...].astype(F32)) + r1[...].astype(F32)) + r2[...].astype(F32)

    def piece(k):
        return pl.BlockSpec((None, None, th, c), lambda l, i, pos_ref: (l, k, i, 0))

    grid_spec = pltpu.PrefetchScalarGridSpec(
        num_scalar_prefetch=1, grid=(l_dim, nt),
        in_specs=[pl.BlockSpec((None, None, th, c), lambda l, i, pos_ref: (l, pos_ref[0], i, 0)),
                  piece(0), piece(1), piece(2)],
        out_specs=pl.BlockSpec((None, th, c), lambda l, i, pos_ref: (l, pos_ref[1] * nt + i, 0)))
    return pl.pallas_call(
        body, name=name, grid_spec=grid_spec, out_shape=jax.ShapeDtypeStruct((l_dim, 2 * h, c), F32),
        compiler_params=_params(("parallel",) * 2),
    )(pos, sums, landed, landed, landed)


def _adam_big(w, m, v, grads, *, name):
    l_dim, r, c = w.shape
    assert len(grads) == l_dim
    tr = min(r, 256)

    def body(*refs):
        w_ref, m_ref, v_ref = refs[:3]
        g_refs = refs[3:3 + l_dim]
        go_ref, d_ref, mo_ref, vo_ref = refs[3 + l_dim:]
        g = g_refs[0][...]
        for l in range(1, l_dim):
            g = jnp.where(pl.program_id(0) == l, g_refs[l][...], g)
        delta, m_new, v_new = _adam_math(w_ref[...], g, m_ref[...], v_ref[...])
        go_ref[...] = g
        d_ref[...] = delta
        mo_ref[...] = m_new
        vo_ref[...] = v_new

    spec = pl.BlockSpec((None, tr, c), lambda l, i: (l, i, 0))
    gspec = pl.BlockSpec((None, tr, c), lambda l, i: (0, i, 0))
    return pl.pallas_call(
        body, name=name, grid=(l_dim, r // tr), in_specs=[spec] * 3 + [gspec] * l_dim, out_specs=[spec] * 4,
        out_shape=[jax.ShapeDtypeStruct(w.shape, F32)] * 4, compiler_params=_params(("parallel",) * 2),
    )(w, m, v, *grads)


def _position():
    return lax.axis_index("x"), lax.axis_index("y"), lax.axis_index("c")


def _other_chips(x, y):
    return [(1 - x, y), (x, 1 - y), (1 - x, 1 - y)]


def _remote(src, dst, send_sem, recv_sem, device):
    return pltpu.make_async_remote_copy(src_ref=src, dst_ref=dst, send_sem=send_sem, recv_sem=recv_sem,
                                        device_id=device, device_id_type=MESH)


ANY = pl.BlockSpec(memory_space=pl.ANY)


def _gather_sems(n):
    return [pltpu.SemaphoreType.DMA((3 * n,))] * 4


def _gather_steps(outs, send_sems, recv_sems, fwd_send, fwd_recv):
    n = len(outs)
    x, y, c = _position()
    chips = _other_chips(x, y)
    sibling = (x, y, 1 - c)

    def half(a, chip, core):
        h = outs[a].shape[2] // 2
        return outs[a].at[:, 2 * chip[0] + chip[1], pl.ds(core * h, h), :]

    def over_ici(a, k, chip):
        block = half(a, chip, c)
        return _remote(block, block, send_sems.at[3 * a + k], recv_sems.at[3 * a + k], (*chips[k], c))

    def over_d2d(a, k, core):
        block = half(a, chips[k], core)
        return _remote(block, block, fwd_send.at[3 * a + k], fwd_recv.at[3 * a + k], sibling)

    def send():
        for a in range(n):
            for k in range(3):
                over_ici(a, k, (x, y)).start()

    def forward():
        for k in range(3):
            for a in range(n):
                over_ici(a, k, chips[k]).wait_recv()
                over_d2d(a, k, c).start()

    def finish():
        for k in range(3):
            for a in range(n):
                over_d2d(a, k, 1 - c).wait_recv()
        for a in range(n):
            for k in range(3):
                over_ici(a, k, (x, y)).wait_send()
                over_d2d(a, k, c).wait_send()

    return send, forward, finish


def _swap_halves(grads, *, name):
    n = len(grads)

    def body(*refs):
        send, finish = _swap_steps(refs[:n], refs[n:2 * n], *refs[2 * n:])
        send()
        finish()

    sem = pltpu.SemaphoreType.DMA((n,))
    return pl.pallas_call(
        body, name=name, in_specs=[ANY] * n, out_specs=[ANY] * n, out_shape=_swap_shapes(grads),
        scratch_shapes=[sem, sem], compiler_params=pltpu.CompilerParams(has_side_effects=True),
    )(*grads)


def _swap_shapes(grads):
    return [jax.ShapeDtypeStruct(g.shape[:2] + (g.shape[2] // 2, g.shape[3]), g.dtype) for g in grads]


def _swap_steps(ins, outs, send_sems, recv_sems):
    x, y, c = _position()

    def copy(a):
        h = ins[a].shape[2] // 2
        return _remote(ins[a].at[:, :, pl.ds((1 - c) * h, h), :], outs[a], send_sems.at[a], recv_sems.at[a],
                       (x, y, 1 - c))

    def send():
        for a in range(len(ins)):
            copy(a).start()

    def finish():
        for a in range(len(ins)):
            copy(a).wait()

    return send, finish


def _exchange_shapes(sums):
    return [jax.ShapeDtypeStruct((s.shape[0], 3) + s.shape[2:], s.dtype) for s in sums]


def _exchange_sems(n):
    return [pltpu.SemaphoreType.DMA((3 * n,))] * 2


def _exchange_steps(ins, outs, send_sems, recv_sems):
    n = len(ins)
    x, y, c = _position()
    chips = _other_chips(x, y)

    def copy(a, k):
        chip = chips[k]
        return _remote(ins[a].at[:, 2 * chip[0] + chip[1]], outs[a].at[:, k],
                       send_sems.at[3 * a + k], recv_sems.at[3 * a + k], (*chip, c))

    def send():
        for a in range(n):
            for k in range(3):
                copy(a, k).start()

    def finish():
        for a in range(n):
            for k in range(3):
                copy(a, k).wait()

    return send, finish


def _join_halves(bufs, *, name):
    n = len(bufs)

    def body(*refs):
        send, finish = _join_steps(refs[n:2 * n], *refs[2 * n:])
        send()
        finish()

    sem = pltpu.SemaphoreType.DMA((n,))
    return pl.pallas_call(
        body, name=name, in_specs=[ANY] * n, out_specs=[ANY] * n,
        out_shape=[jax.ShapeDtypeStruct(b.shape, b.dtype) for b in bufs],
        input_output_aliases={a: a for a in range(n)},
        scratch_shapes=[sem, sem], compiler_params=pltpu.CompilerParams(has_side_effects=True),
    )(*bufs)


def _join_steps(outs, send_sems, recv_sems):
    x, y, c = _position()

    def copy(a, core):
        h = outs[a].shape[1] // 2
        half = outs[a].at[:, pl.ds(core * h, h), :]
        return _remote(half, half, send_sems.at[a], recv_sems.at[a], (x, y, 1 - c))

    def send():
        for a in range(len(outs)):
            copy(a, c).start()

    def finish():
        for a in range(len(outs)):
            copy(a, c).wait_send()
            copy(a, 1 - c).wait_recv()

    return send, finish


def _allgather_steps(ins, outs, send_sems, recv_sems, local_sems):
    n = len(ins)
    x, y, c = _position()
    me, sibling = (x, y, c), (x, y, 1 - c)
    chips = _other_chips(x, y)

    def slot(a, dev):
        return outs[a].at[4 * dev[0] + 2 * dev[1] + dev[2]]

    def copy(a, k, block, to, own=False):
        return _remote(ins[a] if own else slot(a, block), slot(a, block),
                       send_sems.at[7 * a + k], recv_sems.at[7 * a + k], to)

    def first(a):
        return [copy(a, 0, me, sibling, own=True)] + [copy(a, 1 + k, me, (*chips[k], c), own=True) for k in range(3)]

    def local(a):
        return pltpu.make_async_copy(ins[a], slot(a, me), local_sems.at[a])

    def send():
        for a in range(n):
            local(a).start()
            for cp in first(a):
                cp.start()

    def forward():
        for a in range(n):
            for k in range(3):
                copy(a, 1 + k, (*chips[k], c), me).wait_recv()
                copy(a, 4 + k, (*chips[k], c), sibling).start()

    def finish():
        for a in range(n):
            copy(a, 0, sibling, me).wait_recv()
            for k in range(3):
                copy(a, 4 + k, (*chips[k], 1 - c), me).wait_recv()
        for a in range(n):
            for cp in first(a) + [copy(a, 4 + k, (*chips[k], c), sibling) for k in range(3)]:
                cp.wait_send()
            local(a).wait()

    return send, forward, finish


def _allreduce_small(packs):
    n = len(packs)

    def body(*refs):
        ins, outs, gath = refs[:n], refs[n:2 * n], refs[2 * n:3 * n]
        send_sems, recv_sems = refs[3 * n:]
        x, y, c = _position()
        me, sibling = (x, y, c), (x, y, 1 - c)
        chips = _other_chips(x, y)

        def slot(a, dev):
            return gath[a].at[4 * dev[0] + 2 * dev[1] + dev[2]]

        def copy(a, k, block, to, src=None):
            return _remote(slot(a, block) if src is None else src, slot(a, block),
                           send_sems.at[7 * a + k], recv_sems.at[7 * a + k], to)

        started = []
        for a in range(n):
            slot(a, me)[...] = ins[a][...]
            first = [copy(a, 0, me, sibling, src=ins[a])]
            first += [copy(a, 1 + k, me, (*chip, c), src=ins[a]) for k, chip in enumerate(chips)]
            for cp in first:
                cp.start()
            started += first
        for a in range(n):
            for k, chip in enumerate(chips):
                copy(a, 1 + k, (*chip, c), me).wait_recv()
                cp = copy(a, 4 + k, (*chip, c), sibling)
                cp.start()
                started.append(cp)
        for a in range(n):
            copy(a, 0, sibling, me).wait_recv()
            for k, chip in enumerate(chips):
                copy(a, 4 + k, (*chip, 1 - c), me).wait_recv()
        for cp in started:
            cp.wait_send()
        for a in range(n):
            total = gath[a][0]
            for d in range(1, N_DEV):
                total = total + gath[a][d]
            outs[a][...] = total

    vmem = pl.BlockSpec(memory_space=pltpu.VMEM)
    sem = pltpu.SemaphoreType.DMA((7 * n,))
    return pl.pallas_call(
        body, name="allreduce_small", in_specs=[vmem] * n, out_specs=[vmem] * n,
        out_shape=[jax.ShapeDtypeStruct(p.shape, p.dtype) for p in packs],
        scratch_shapes=[pltpu.VMEM((N_DEV,) + p.shape, p.dtype) for p in packs] + [sem, sem],
        compiler_params=pltpu.CompilerParams(has_side_effects=True, vmem_limit_bytes=VMEM_LIMIT_BYTES),
    )(*packs)


LOSS_ROW = 1040


def _pad_rows(a, rows=8):
    return jnp.concatenate([a, jnp.zeros((rows - a.shape[0], a.shape[1]), a.dtype)], axis=0)

def _adam_small(wide, mid, narrow, late, params):
    names = ["mix_norm_g", "mlp_norm_g", "final_norm_g", "conv_b", "conv_w", "sgu_norm_g", "sgu_norm_b",
             "pool_w", "pool_scale", "sgu_w", "sgu_b"]
    n = len(names)

    def body(*refs):
        wmv = refs[4:4 + 3 * n]
        outs = refs[4 + 3 * n:]
        x, y, _ = _position()
        q = 2 * x + y

        def total(ref):
            t = ref[0]
            for dev in range(1, N_DEV):
                t = t + ref[dev]
            return t

        wide_sum, mid_sum, narrow_sum = total(refs[0]), total(refs[1]), total(refs[2])
        late_ref = refs[3]

        def my_quarter(rows):
            parts = [rows[:, s * TILE:(s + 1) * TILE] for s in range(N_CHIP)]
            return jnp.where(q == 0, parts[0], jnp.where(q == 1, parts[1], jnp.where(q == 2, parts[2], parts[3])))

        def tiles(first_row):
            return [((0, g), narrow_sum[first_row + g * TILE:first_row + (g + 1) * TILE, :]) for g in range(4)]

        grads = {
            "mix_norm_g": [((), wide_sum[0:2, :] + late_ref[0:2, :])],
            "mlp_norm_g": [((), wide_sum[8:10, :])],
            "final_norm_g": [((), wide_sum[16:17, :])],
            "conv_b": [((), mid_sum[0:1, :])],
            "conv_w": [((0,), my_quarter(mid_sum[8:11, :]))],
            "sgu_norm_g": [((), my_quarter(mid_sum[16:17, :]))],
            "sgu_norm_b": [((), my_quarter(mid_sum[24:25, :]))],
            "pool_w": tiles(0),
            "sgu_w": tiles(512),
            "pool_scale": [((0,), narrow_sum[1024:1028, :])],
            "sgu_b": [((0,), narrow_sum[1032:1036, :])],
        }
        outs[4 * n][...] = narrow_sum[LOSS_ROW:LOSS_ROW + 8, :]
        for i, name in enumerate(names):
            w_ref, m_ref, v_ref = wmv[3 * i:3 * i + 3]
            for lead, g in grads[name]:
                idx = lead + (slice(None), slice(None))
                delta, m_new, v_new = _adam_math(w_ref[idx], g, m_ref[idx], v_ref[idx])
                outs[4 * i][idx] = g
                outs[4 * i + 1][idx] = delta
                outs[4 * i + 2][idx] = m_new
                outs[4 * i + 3][idx] = v_new

    vmem = pl.BlockSpec(memory_space=pltpu.VMEM)
    args, out_shape = [wide, mid, narrow, late], []
    for name in names:
        w, m, v = params[name]
        args += [w, m, v]
        out_shape += [jax.ShapeDtypeStruct(w.shape, F32)] * 4
    out_shape.append(jax.ShapeDtypeStruct((8, TILE), F32))
    res = pl.pallas_call(
        body, name="adam_small", in_specs=[vmem] * len(args), out_specs=[vmem] * len(out_shape),
        out_shape=out_shape, compiler_params=pltpu.CompilerParams(vmem_limit_bytes=VMEM_LIMIT_BYTES),
    )(*args)
    return {name: res[4 * i:4 * i + 4] for i, name in enumerate(names)}, res[4 * n]


def _pair_sums(grads, got, pos, tag):
    return [_pair_sum(a, b, pos, name=f"pair_sum_{tag}{i}") for i, (a, b) in enumerate(zip(grads, got))]


def _chip_sums(sums, landed, pos, tag):
    return [_chip_sum(s, r, pos, name=f"chip_sum_{tag}{i}") for i, (s, r) in enumerate(zip(sums, landed))]


def kernel(x, mix_norm_g, mlp_norm_g, ab_w_in, pool_w, pool_scale, conv_w, conv_b, ab_w_out, cd_w_in, sgu_norm_g, sgu_norm_b, sgu_w, sgu_b, cd_w_out, mlp_w1, mlp_w2, final_norm_g, loss_target, m_mix_norm_g, m_mlp_norm_g, m_ab_w_in, m_pool_w, m_pool_scale, m_conv_w, m_conv_b, m_ab_w_out, m_cd_w_in, m_sgu_norm_g, m_sgu_norm_b, m_sgu_w, m_sgu_b, m_cd_w_out, m_mlp_w1, m_mlp_w2, m_final_norm_g, v_mix_norm_g, v_mlp_norm_g, v_ab_w_in, v_pool_w, v_pool_scale, v_conv_w, v_conv_b, v_ab_w_out, v_cd_w_in, v_sgu_norm_g, v_sgu_norm_b, v_sgu_w, v_sgu_b, v_cd_w_out, v_mlp_w1, v_mlp_w2, v_final_norm_g):
    nseq, t_len, d = x.shape
    m_tok = nseq * t_len
    h0 = x.reshape(m_tok, d)
    target = loss_target.reshape(m_tok, d)

    x_idx, y_idx = lax.axis_index("x"), lax.axis_index("y")
    q_idx = 2 * x_idx + y_idx
    pos = jnp.stack([q_idx, lax.axis_index("c")]).astype(jnp.int32)

    def shard_buffer(w, layer, tag):
        return _cast_place(w, layer, pos, name=f"cast_place_{tag}")

    def row_block(w):
        return w.reshape(1, 1, -1, w.shape[-1])

    (buf_ab_out, buf_w1_0, buf_w2_0, buf_cd_in, *later_weights), ((w_ab_in,),) = _cast_place_all(
        [(ab_w_out, 0), (mlp_w1, 0), (mlp_w2, 0), (cd_w_in, 0), (cd_w_out, 0), (mlp_w1, 1), (mlp_w2, 1)], pos,
        name="cast_place_rest", rider=[("gather", [shard_buffer(ab_w_in, 0, "ab_in")])])

    pool_w3, pool_scale3 = pool_w[0], pool_scale[0].reshape(4, 1, TILE)
    sgu_w3 = sgu_w[0]
    sgu_w3_t = jnp.swapaxes(sgu_w3, 1, 2)
    sgu_bias_tile = jnp.broadcast_to(sgu_b[0][:, :, None], (4, TILE, TILE))
    conv_b2 = conv_b

    def place_quarter(v):
        return lax.dynamic_update_slice(jnp.zeros((v.shape[0], 4 * TILE), F32), v, (0, q_idx * TILE))

    sharded_small = jnp.concatenate(
        [place_quarter(conv_w[0]), place_quarter(sgu_norm_g), place_quarter(sgu_norm_b),
         jnp.zeros((3, 4 * TILE), F32)], axis=0)
    sharded_small, = _allreduce_small([sharded_small])
    sharded_small = sharded_small * 0.5
    conv_w_full = sharded_small[0:3]
    sgu_g_full = sharded_small[3:4]
    sgu_b_full = sharded_small[4:5]

    xn0 = _rms_fwd(h0, mix_norm_g[0:1], name="rms_fwd_mix0")
    p_ab, ((w_1_0,),) = _mm_nn(xn0, w_ab_in, 0, out_dtype=BF16, name="ab_in_proj",
                               rider=[("gather", [buf_w1_0])])
    mix0, ((w_ab_out,),) = _ab_fwd(p_ab, pool_w3, pool_scale3, conv_w_full, conv_b2, nseq, t_len,
                                   rider=[("gather", [buf_ab_out])])
    w_ab_out = row_block(w_ab_out)
    h1, hn0 = _mm_nn(mix0, w_ab_out, 0, out_dtype=F32, name="ab_out_proj", epilogue="residual", extra=h0,
                     norm_g=mlp_norm_g[0:1])
    (act0, relu0), ((w_2_0,),) = _mm_nn(hn0, w_1_0, 0, out_dtype=BF16, name="mlp0_up", epilogue="relu2",
                                        rider=[("gather", [buf_w2_0])])
    w_2_0 = row_block(w_2_0)
    (h2, xn1), ((w_cd_in,),) = _mm_nn(act0, w_2_0, 0, out_dtype=F32, name="mlp0_down", epilogue="residual", extra=h1,
                                      norm_g=mix_norm_g[1:2],
                                      rider=[("gather", [buf_cd_in])])

    p_cd = _mm_nn(xn1, w_cd_in, 0, out_dtype=BF16, name="cd_in_proj")
    c_out = _sgu_fwd(p_cd, sgu_g_full, sgu_b_full, sgu_w3, sgu_bias_tile)
    d_out, ltot, (w_cd_out, w_1_1, w_2_1) = _sb_fwd(p_cd, nseq, t_len, later_weights)
    w_cd_out, w_2_1 = row_block(w_cd_out), row_block(w_2_1)
    mix1 = jnp.concatenate([c_out, d_out], axis=1)
    h3, hn1 = _mm_nn(mix1, w_cd_out, 0, out_dtype=F32, name="cd_out_proj", epilogue="residual", extra=h2,
                     norm_g=mlp_norm_g[1:2])
    act1, relu1 = _mm_nn(hn1, w_1_1, 0, out_dtype=BF16, name="mlp1_up", epilogue="relu2")

    dh4, dh4_bf, dg_final, loss_tile = _mlp_down_loss(act1, w_2_1, h3, final_norm_g.reshape(1, d), target)

    def as_pieces(g):
        return g.reshape(1, N_CHIP, -1, g.shape[-1]) if g.shape[1] == 1 else g

    dz1 = _mm_nt(dh4_bf, w_2_1, 0, out_dtype=BF16, name="mlp1_down_bwd", epilogue="relu2_bwd", extra=relu1)
    g_w2_1 = as_pieces(_mm_tn(act1, dh4_bf, 1, name="mlp1_down_wgrad"))
    g_w1_1 = _mm_tn(hn1, dz1, N_CHIP, name="mlp1_up_wgrad")
    (dh3, dh3_bf, dg_mlp1), (got_a,) = _mm_nt(
        dz1, w_1_1, 0, out_dtype=F32, name="mlp1_up_bwd", epilogue="rms_bwd",
        extra=(h3, mlp_norm_g[1:2], dh4), rider=[("swap", [g_w1_1, g_w2_1])])

    g_cd_out = as_pieces(_mm_tn(mix1, dh3_bf, 1, name="cd_out_wgrad"))
    dmix1, (got_cd_out,) = _mm_nt(dh3_bf, w_cd_out, 0, out_dtype=BF16, name="cd_out_bwd",
                                  rider=[("swap", [g_cd_out])])
    sums_a = _pair_sums([g_w1_1, g_w2_1, g_cd_out], got_a + got_cd_out, pos, "a")
    du, dv, dsgu_w, dsgu_bs, dsgu_g, dsgu_b = _sgu_bwd(p_cd, dmix1, sgu_g_full, sgu_b_full, sgu_w3, sgu_w3_t,
                                                      sgu_bias_tile)
    dq, dk, dvv, landed_a = _sb_bwd(p_cd, dmix1, ltot, nseq, t_len, sums_a)
    halves_a = _chip_sums(sums_a, landed_a, pos, "a")
    dp_cd = jnp.concatenate([du, dv, dq, dk, dvv], axis=1)
    g_cd_in, ((r_w1_1, r_w2_1, r_cd_out),) = _mm_tn(xn1, dp_cd, N_CHIP, name="cd_in_wgrad",
                                                    rider=[("join", halves_a)])
    (dh2, dh2_bf, dg_mix1), (got_c,) = _mm_nt(
        dp_cd, w_cd_in, 0, out_dtype=F32, name="cd_in_bwd", epilogue="rms_bwd",
        extra=(h2, mix_norm_g[1:2], dh3), rider=[("swap", [g_cd_in])])

    sums_c = _pair_sums([g_cd_in], got_c, pos, "c")
    dz0, (landed_c,) = _mm_nt(dh2_bf, w_2_0, 0, out_dtype=BF16, name="mlp0_down_bwd", epilogue="relu2_bwd",
                              extra=relu0, rider=[("exchange", sums_c)])
    halves_c = _chip_sums(sums_c, landed_c, pos, "c")
    g_w2_0, ((r_cd_in,),) = _mm_tn(act0, dh2_bf, 1, name="mlp0_down_wgrad", rider=[("join", halves_c)])
    g_w2_0 = as_pieces(g_w2_0)
    g_w1_0, (got_d,) = _mm_tn(hn0, dz0, N_CHIP, name="mlp0_up_wgrad", rider=[("swap", [g_w2_0])])
    sums_d = _pair_sums([g_w2_0], got_d, pos, "d")
    (dh1, dh1_bf, dg_mlp0), (landed_d, got_e) = _mm_nt(
        dz0, w_1_0, 0, out_dtype=F32, name="mlp0_up_bwd", epilogue="rms_bwd",
        extra=(h1, mlp_norm_g[0:1], dh2), rider=[("exchange", sums_d), ("swap", [g_w1_0])])
    halves_d = _chip_sums(sums_d, landed_d, pos, "d")
    sums_e = _pair_sums([g_w1_0], got_e, pos, "e")

    dmix0, ((r_w2_0,),) = _mm_nt(dh1_bf, w_ab_out, 0, out_dtype=BF16, name="ab_out_bwd", rider=[("join", halves_d)])
    g_ab_out = as_pieces(_mm_tn(mix0, dh1_bf, 1, name="ab_out_wgrad"))
    (da, dxb, dgb, dgc, dpool_w, dpool_scale, dconv_w, dconv_b), (landed_e, got_f) = _ab_bwd(
        p_ab, dmix0, pool_w3, pool_scale3, conv_w_full, conv_b2, nseq, t_len,
        rider=[("exchange", sums_e), ("swap", [g_ab_out])])
    halves_e = _chip_sums(sums_e, landed_e, pos, "e")
    sums_f = _pair_sums([g_ab_out], got_f, pos, "f")
    dp_ab = jnp.concatenate([da, dxb, dgb, dgc], axis=1)
    wide = jnp.concatenate([_pad_rows(jnp.concatenate([jnp.zeros_like(dg_mix1), dg_mix1], axis=0)),
                            _pad_rows(jnp.concatenate([dg_mlp0, dg_mlp1], axis=0)), _pad_rows(dg_final)], axis=0)
    mid = jnp.concatenate([_pad_rows(dconv_b), _pad_rows(dconv_w), _pad_rows(dsgu_g), _pad_rows(dsgu_b)], axis=0)
    narrow = jnp.concatenate(
        [dpool_w.reshape(4 * TILE, TILE), dsgu_w.reshape(4 * TILE, TILE), _pad_rows(dpool_scale.reshape(4, TILE)),
         _pad_rows(dsgu_bs[:, :, 0]), loss_tile], axis=0)
    g_ab_in, (landed_f, (r_w1_0,), (wide, mid, narrow)) = _mm_tn(
        xn0, dp_ab, N_CHIP, name="ab_in_wgrad",
        rider=[("exchange", sums_f), ("join", halves_e), ("allgather", [wide, mid, narrow])])
    halves_f = _chip_sums(sums_f, landed_f, pos, "f")
    sums_g = _pair_sums([g_ab_in], _swap_halves([g_ab_in], name="swap_halves_g"), pos, "g")
    (grad_x, _, dg_mix0), (landed_g, (r_ab_out,)) = _mm_nt(
        dp_ab, w_ab_in, 0, out_dtype=F32, name="ab_in_bwd", epilogue="rms_bwd",
        extra=(h0, mix_norm_g[0:1], dh1), rider=[("exchange", sums_g), ("join", halves_f)])
    r_ab_in, = _join_halves(_chip_sums(sums_g, landed_g, pos, "g"), name="join_halves_g")

    big_out = {
        "ab_w_in": _adam_big(ab_w_in, m_ab_w_in, v_ab_w_in, [r_ab_in], name="adam_ab_w_in"),
        "ab_w_out": _adam_big(ab_w_out, m_ab_w_out, v_ab_w_out, [r_ab_out], name="adam_ab_w_out"),
        "cd_w_in": _adam_big(cd_w_in, m_cd_w_in, v_cd_w_in, [r_cd_in], name="adam_cd_w_in"),
        "cd_w_out": _adam_big(cd_w_out, m_cd_w_out, v_cd_w_out, [r_cd_out], name="adam_cd_w_out"),
        "mlp_w1": _adam_big(mlp_w1, m_mlp_w1, v_mlp_w1, [r_w1_0, r_w1_1], name="adam_mlp_w1"),
        "mlp_w2": _adam_big(mlp_w2, m_mlp_w2, v_mlp_w2, [r_w2_0, r_w2_1], name="adam_mlp_w2"),
    }

    late, = _allreduce_small([_pad_rows(dg_mix0)])
    small_out, loss_sum = _adam_small(wide, mid, narrow, late, {
        "mix_norm_g": (mix_norm_g, m_mix_norm_g, v_mix_norm_g),
        "mlp_norm_g": (mlp_norm_g, m_mlp_norm_g, v_mlp_norm_g),
        "final_norm_g": tuple(a.reshape(1, d) for a in (final_norm_g, m_final_norm_g, v_final_norm_g)),
        "conv_b": (conv_b, m_conv_b, v_conv_b),
        "conv_w": (conv_w, m_conv_w, v_conv_w),
        "sgu_norm_g": (sgu_norm_g, m_sgu_norm_g, v_sgu_norm_g),
        "sgu_norm_b": (sgu_norm_b, m_sgu_norm_b, v_sgu_norm_b),
        "pool_w": (pool_w, m_pool_w, v_pool_w),
        "pool_scale": (pool_scale, m_pool_scale, v_pool_scale),
        "sgu_w": (sgu_w, m_sgu_w, v_sgu_w),
        "sgu_b": (sgu_b, m_sgu_b, v_sgu_b),
    })
    small_out["final_norm_g"] = [a.reshape(d) for a in small_out["final_norm_g"]]

    order = ["mix_norm_g", "mlp_norm_g", "ab_w_in", "pool_w", "pool_scale", "conv_w", "conv_b", "ab_w_out",
             "cd_w_in", "sgu_norm_g", "sgu_norm_b", "sgu_w", "sgu_b", "cd_w_out", "mlp_w1", "mlp_w2",
             "final_norm_g"]
    both = {**big_out, **small_out}
    loss = loss_sum[0, 0]
    outs = [loss, grad_x.reshape(nseq, t_len, d)]
    for kind in range(4):
        outs += [both[name][kind] for name in order]
    return tuple(outs)
```

```python
import math

import jax
import jax.numpy as jnp
from jax import lax
from jax.experimental import pallas as pl
from jax.experimental.pallas import tpu as pltpu

F32 = jnp.float32
BF16 = jnp.bfloat16
MESH = pl.DeviceIdType.MESH

EPS = 1e-6
TILE = 128
N_CHIP = 4
N_DEV = 8
VMEM_LIMIT_BYTES = 56 * 1024 * 1024

ADAM_LR = 0.001
ADAM_B1 = 0.9
ADAM_B2 = 0.999
ADAM_EPS = 1e-08
ADAM_WD = 0.01
ADAM_STEP = 10

NT_DIMS = (((1,), (1,)), ((), ()))
TN_DIMS = (((0,), (0,)), ((), ()))


def _params(sem=None):
    return pltpu.CompilerParams(dimension_semantics=sem, vmem_limit_bytes=VMEM_LIMIT_BYTES)


def _call(body, *, name, grid, in_specs, out_specs, out_shape, scratch_shapes, semantics, args, rider=None,
          prefetch=None):
    npre = 0 if prefetch is None else 1

    def launch(kernel, in_specs, out_specs, out_shape, scratch_shapes, operands, aliases, params):
        if prefetch is None:
            return pl.pallas_call(kernel, name=name, grid=grid, in_specs=in_specs, out_specs=out_specs,
                                  out_shape=out_shape, scratch_shapes=scratch_shapes, input_output_aliases=aliases,
                                  compiler_params=params)(*operands)
        spec = pltpu.PrefetchScalarGridSpec(num_scalar_prefetch=1, grid=grid, in_specs=in_specs, out_specs=out_specs,
                                            scratch_shapes=scratch_shapes)
        return pl.pallas_call(kernel, name=name, grid_spec=spec, out_shape=out_shape,
                              input_output_aliases={k + 1: v for k, v in aliases.items()},
                              compiler_params=params)(prefetch, *operands)

    if not rider:
        res = launch(body, list(in_specs), list(out_specs), list(out_shape), list(scratch_shapes), args, {},
                     _params(semantics))
        return list(res), []
    plans = [_rider_plan(kind, arrays) for kind, arrays in rider]
    arrays = [a for _, group in rider for a in group]
    nr, n_in, n_out, n_scr = len(arrays), len(in_specs), len(out_specs), len(scratch_shapes)
    first_out, first_scr = n_in + nr, n_in + nr + n_out + nr
    last_step = math.prod(grid) - 1

    def riding(*refs):
        pre, refs = refs[:npre], refs[npre:]
        step = 0
        for axis, size in enumerate(grid):
            step = step * size + pl.program_id(axis)
        steps, at, sem_at = [], 0, first_scr + n_scr
        for (kind, group), (_, sems, _) in zip(rider, plans):
            k = len(group)
            steps.append(_rider_steps(kind, refs[n_in + at:n_in + at + k],
                                      refs[first_out + n_out + at:first_out + n_out + at + k],
                                      refs[sem_at:sem_at + len(sems)]))
            at, sem_at = at + k, sem_at + len(sems)
        for send, _, _ in steps:
            pl.when(step == 0)(send)
        for _, forward, _ in steps:
            if forward is not None:
                pl.when(step == last_step)(forward)
        body(*pre, *refs[:n_in], *refs[first_out:first_out + n_out], *refs[first_scr:first_scr + n_scr])
        for _, _, finish in steps:
            pl.when(step == last_step)(finish)

    aliases, at = {}, 0
    for (_, group), (_, _, aliased) in zip(rider, plans):
        if aliased:
            aliases.update({n_in + at + a: n_out + at + a for a in range(len(group))})
        at += len(group)
    res = launch(
        riding, list(in_specs) + [ANY] * nr, list(out_specs) + [ANY] * nr,
        list(out_shape) + [s for shapes, _, _ in plans for s in shapes],
        list(scratch_shapes) + [s for _, sems, _ in plans for s in sems], [*args, *arrays], aliases,
        pltpu.CompilerParams(dimension_semantics=("arbitrary",) * len(grid), vmem_limit_bytes=VMEM_LIMIT_BYTES,
                             has_side_effects=True))
    rode, at = [], n_out
    for _, group in rider:
        rode.append(list(res[at:at + len(group)]))
        at += len(group)
    return list(res[:n_out]), rode


def _rider_plan(kind, arrays):
    n = len(arrays)
    same = [jax.ShapeDtypeStruct(a.shape, a.dtype) for a in arrays]
    pair = [pltpu.SemaphoreType.DMA((n,))] * 2
    if kind == "gather":
        return same, _gather_sems(n), True
    if kind == "exchange":
        return _exchange_shapes(arrays), _exchange_sems(n), False
    if kind == "swap":
        return _swap_shapes(arrays), pair, False
    if kind == "allgather":
        return ([jax.ShapeDtypeStruct((N_DEV,) + a.shape, a.dtype) for a in arrays],
                [pltpu.SemaphoreType.DMA((7 * n,))] * 2 + [pltpu.SemaphoreType.DMA((n,))], False)
    assert kind == "join"
    return same, pair, True


def _rider_steps(kind, ins, outs, sems):
    if kind == "gather":
        return _gather_steps(outs, *sems)
    if kind == "allgather":
        return _allgather_steps(ins, outs, *sems)
    if kind == "exchange":
        send, finish = _exchange_steps(ins, outs, *sems)
    elif kind == "swap":
        send, finish = _swap_steps(ins, outs, *sems)
    else:
        send, finish = _join_steps(outs, *sems)
    return send, None, finish


def _gathers(rider):
    return any(kind in ("gather", "allgather") for kind, _ in rider or ())


def _row_tile(k_dim, roomy=False):
    if k_dim > 1024:
        return 512
    return 2048 if roomy else 1024


def _mm_nn(a, b4, layer, *, out_dtype, name, epilogue=None, extra=None, norm_g=None, rider=None):
    m, k_dim = a.shape
    _, s_dim, kb, n = b4.shape
    assert kb == k_dim
    tm = min(m, _row_tile(k_dim, roomy=epilogue != "residual" and norm_g is None and not _gathers(rider)))
    tn = min(n, 1024)
    assert m % tm == 0 and n % tn == 0
    npb = n // tn
    grid = (m // tm, s_dim * npb)
    n_in = 2 + (extra is not None) + (norm_g is not None)
    two_outputs = norm_g is not None or epilogue == "relu2"
    assert norm_g is None or (tn == s_dim * n and epilogue != "relu2")

    def body(*refs):
        a_ref, b_ref = refs[:2]
        e_ref = refs[2] if extra is not None else None
        g_ref = refs[n_in - 1] if norm_g is not None else None
        o_ref = refs[n_in]
        acc = jnp.dot(a_ref[...], b_ref[...], preferred_element_type=F32)
        if epilogue == "relu2":
            r = jnp.maximum(acc, 0.0)
            refs[n_in + 1][...] = r.astype(BF16)
            acc = r * r
        elif epilogue == "residual":
            acc = acc + e_ref[...]
        o_ref[...] = acc.astype(out_dtype)
        if norm_g is not None:
            rstd = lax.rsqrt(jnp.mean(acc * acc, axis=-1, keepdims=True) + EPS)
            refs[n_in + 1][...] = (acc * rstd * g_ref[...]).astype(BF16)

    in_specs = [
        pl.BlockSpec((tm, k_dim), lambda i, j: (i, 0)),
        pl.BlockSpec((None, None, k_dim, tn), lambda i, j: (layer, j // npb, 0, j % npb)),
    ]
    args = [a, b4]
    if extra is not None:
        in_specs.append(pl.BlockSpec((tm, tn), lambda i, j: (i, j)))
        args.append(extra)
    out_block = pl.BlockSpec((tm, tn), lambda i, j: (i, j))
    out_specs, out_shape = [out_block], [jax.ShapeDtypeStruct((m, s_dim * n), out_dtype)]
    if norm_g is not None:
        in_specs.append(pl.BlockSpec((1, tn), lambda i, j: (0, j)))
        args.append(norm_g)
    if two_outputs:
        out_specs.append(out_block)
        out_shape.append(jax.ShapeDtypeStruct((m, s_dim * n), BF16))
    res, rode = _call(
        body, name=name, grid=grid, in_specs=in_specs, out_specs=out_specs, out_shape=out_shape,
        scratch_shapes=[], semantics=("parallel", "parallel"), args=args, rider=rider)
    res = res if two_outputs else res[0]
    return res if rider is None else (res, rode)


def _mm_nt(a, b4, layer, *, out_dtype, name, epilogue=None, extra=None, rider=None):
    m, k_dim = a.shape
    _, s_dim, n_out, n = b4.shape
    assert k_dim == s_dim * n
    rms = epilogue == "rms_bwd"
    roomy = not rms and out_dtype != F32 and not _gathers(rider)
    tm, tn = min(m, _row_tile(k_dim, roomy=roomy)), min(n_out, 1024)
    assert m % tm == 0 and n_out % tn == 0
    grid = (m // tm, n_out // tn)
    assert not rms or tn == n_out
    extras = [] if extra is None else (list(extra) if rms else [extra])
    n_in = 2 + len(extras)

    def body(*refs):
        a_ref, b_ref = refs[:2]
        e_refs = refs[2:n_in]
        o_ref = refs[n_in]
        acc = lax.dot_general(a_ref[:, 0:n], b_ref[0], NT_DIMS, preferred_element_type=F32)
        for s in range(1, s_dim):
            acc = acc + lax.dot_general(a_ref[:, s * n:(s + 1) * n], b_ref[s], NT_DIMS, preferred_element_type=F32)
        if epilogue == "relu2_bwd":
            acc = acc * (2.0 * e_refs[0][...].astype(F32))
        if not rms:
            o_ref[...] = acc.astype(out_dtype)
        else:
            h_ref, g_ref, dres_ref = e_refs
            dhb_ref, dg_ref = refs[n_in + 1:n_in + 3]
            hv = h_ref[...]
            rstd = lax.rsqrt(jnp.mean(hv * hv, axis=-1, keepdims=True) + EPS)
            xhat = hv * rstd
            dxhat = acc * g_ref[...]
            dh = dres_ref[...] + rstd * (dxhat - xhat * jnp.mean(dxhat * xhat, axis=-1, keepdims=True))
            o_ref[...] = dh
            dhb_ref[...] = dh.astype(BF16)
            dg_part = jnp.sum(acc * xhat, axis=0, keepdims=True)
            first = pl.program_id(0) == 0

            @pl.when(first)
            def _():
                dg_ref[...] = dg_part

            @pl.when(jnp.logical_not(first))
            def _():
                dg_ref[...] += dg_part

    in_specs = [
        pl.BlockSpec((tm, k_dim), lambda i, j: (i, 0)),
        pl.BlockSpec((None, s_dim, tn, n), lambda i, j: (layer, 0, j, 0)),
    ]
    args = [a, b4] + extras
    block = pl.BlockSpec((tm, tn), lambda i, j: (i, j))
    vec = pl.BlockSpec((1, tn), lambda i, j: (0, j))
    if rms:
        in_specs += [block, vec, block]
        out_specs = [block, block, vec]
        out_shape = [jax.ShapeDtypeStruct((m, n_out), F32), jax.ShapeDtypeStruct((m, n_out), BF16),
                     jax.ShapeDtypeStruct((1, n_out), F32)]
    else:
        in_specs += [block] * len(extras)
        out_specs, out_shape = [block], [jax.ShapeDtypeStruct((m, n_out), out_dtype)]
    res, rode = _call(
        body, name=name, grid=grid, in_specs=in_specs, out_specs=out_specs, out_shape=out_shape,
        scratch_shapes=[], semantics=("arbitrary",) * 2 if rms else ("parallel", "parallel"), args=args, rider=rider)
    res = res if rms else res[0]
    return res if rider is None else (res, rode)


def _mm_tn(a, b, s_dim, *, name, rider=None):
    m, k1 = a.shape
    mb, n_all = b.shape
    assert mb == m and n_all % s_dim == 0
    n = n_all // s_dim
    tn, t1 = min(n, 1024), min(k1, 512 if _gathers(rider) else 1024)
    assert k1 % t1 == 0 and n % tn == 0
    npb = n // tn
    grid = (k1 // t1, s_dim * npb)

    def body(a_ref, b_ref, o_ref):
        o_ref[...] = lax.dot_general(a_ref[...], b_ref[...], TN_DIMS, preferred_element_type=F32).astype(BF16)

    res, rode = _call(
        body, name=name, grid=grid,
        in_specs=[pl.BlockSpec((m, t1), lambda i, j: (0, i)), pl.BlockSpec((m, tn), lambda i, j: (0, j))],
        out_specs=[pl.BlockSpec((None, None, t1, tn), lambda i, j: (0, j // npb, i, j % npb))],
        out_shape=[jax.ShapeDtypeStruct((1, s_dim, k1, n), BF16)],
        scratch_shapes=[], semantics=("parallel", "parallel"), args=[a, b], rider=rider)
    return res[0] if rider is None else (res[0], rode)


ROW_TILE = 512


def _rms_fwd(h, g, *, name, rider=None):
    m, d = h.shape

    def body(h_ref, g_ref, o_ref):
        hv = h_ref[...]
        rstd = lax.rsqrt(jnp.mean(hv * hv, axis=-1, keepdims=True) + EPS)
        o_ref[...] = (hv * rstd * g_ref[...]).astype(BF16)

    res, rode = _call(
        body, name=name, grid=(m // ROW_TILE,),
        in_specs=[pl.BlockSpec((ROW_TILE, d), lambda i: (i, 0)), pl.BlockSpec((1, d), lambda i: (0, 0))],
        out_specs=[pl.BlockSpec((ROW_TILE, d), lambda i: (i, 0))], out_shape=[jax.ShapeDtypeStruct((m, d), BF16)],
        scratch_shapes=[], semantics=("parallel",), args=[h, g], rider=rider)
    return res[0] if rider is None else (res[0], rode)


def _mlp_down_loss(act, w_2, h_res, g, target):
    m, k_dim = act.shape
    d = w_2.shape[-1]
    tm = _row_tile(k_dim)

    def body(a_ref, b_ref, r_ref, g_ref, t_ref, dh_ref, dhb_ref, dg_ref, loss_ref):
        hv = jnp.dot(a_ref[...], b_ref[...], preferred_element_type=F32) + r_ref[...]
        gv = g_ref[...]
        rstd = lax.rsqrt(jnp.mean(hv * hv, axis=-1, keepdims=True) + EPS)
        xhat = hv * rstd
        err = xhat * gv - t_ref[...]
        dy = err * (1.0 / d)
        dxhat = dy * gv
        dh = rstd * (dxhat - xhat * jnp.mean(dxhat * xhat, axis=-1, keepdims=True))
        dh_ref[...] = dh
        dhb_ref[...] = dh.astype(BF16)
        dg_part = jnp.sum(dy * xhat, axis=0, keepdims=True)
        sq = jnp.sum(jnp.sum(err * err, axis=1, keepdims=True), axis=0, keepdims=True) * (0.5 / d)
        loss_part = jnp.broadcast_to(sq, (8, TILE))

        @pl.when(pl.program_id(0) == 0)
        def _():
            dg_ref[...] = dg_part
            loss_ref[...] = loss_part

        @pl.when(pl.program_id(0) > 0)
        def _():
            dg_ref[...] += dg_part
            loss_ref[...] += loss_part

    row = pl.BlockSpec((tm, d), lambda i: (i, 0))
    vec = pl.BlockSpec((1, d), lambda i: (0, 0))
    return pl.pallas_call(
        body, name="mlp1_down_loss", grid=(m // tm,),
        in_specs=[pl.BlockSpec((tm, k_dim), lambda i: (i, 0)),
                  pl.BlockSpec((None, None, k_dim, d), lambda i: (0, 0, 0, 0)), row, vec, row],
        out_specs=[row, row, vec, pl.BlockSpec((8, TILE), lambda i: (0, 0))],
        out_shape=[jax.ShapeDtypeStruct((m, d), F32), jax.ShapeDtypeStruct((m, d), BF16),
                   jax.ShapeDtypeStruct((1, d), F32), jax.ShapeDtypeStruct((8, TILE), F32)],
        compiler_params=_params(("arbitrary",)),
    )(act, w_2, h_res, g, target)


def _shift_down(x, s, t_idx):
    return jnp.where(t_idx >= s, pltpu.roll(x, s, 0), 0.0)


def _shift_up(x, s, t_idx, t_len):
    return jnp.where(t_idx < t_len - s, pltpu.roll(x, t_len - s, 0), 0.0)


def _pool_select(group, s2, s4, s8, s16):
    return jnp.where(group == 0, s2, jnp.where(group == 1, s4, jnp.where(group == 2, s8, s16)))


def _pool_count(group, t_idx):
    win = jnp.left_shift(2, group)
    return jnp.minimum(t_idx + 1, win).astype(F32)


def _pool_fwd_math(a, group, t_idx):
    s2 = a + _shift_down(a, 1, t_idx)
    s4 = s2 + _shift_down(s2, 2, t_idx)
    s8 = s4 + _shift_down(s4, 4, t_idx)
    s16 = s8 + _shift_down(s8, 8, t_idx)
    return _pool_select(group, s2, s4, s8, s16) / _pool_count(group, t_idx) - a


def _pool_bwd_math(dpooled, group, t_idx, t_len):
    e = dpooled / _pool_count(group, t_idx)
    s2 = e + _shift_up(e, 1, t_idx, t_len)
    s4 = s2 + _shift_up(s2, 2, t_idx, t_len)
    s8 = s4 + _shift_up(s4, 4, t_idx, t_len)
    s16 = s8 + _shift_up(s8, 8, t_idx, t_len)
    return _pool_select(group, s2, s4, s8, s16) - dpooled


def _conv_fwd_math(c, w_ref, b_ref, t_idx):
    return (w_ref[0:1, :] * _shift_down(c, 2, t_idx) + w_ref[1:2, :] * _shift_down(c, 1, t_idx)
            + w_ref[2:3, :] * c + b_ref[...])


def _ab_fwd(p, pool_w, pool_scale, conv_w, conv_b, nseq, t_len, rider=None):
    m = p.shape[0]
    ng = 4

    def body(a_ref, xb_ref, gb_ref, gc_ref, pw_ref, ps_ref, cw_ref, cb_ref, o_ref):
        j = pl.program_id(1)
        t_idx = lax.broadcasted_iota(jnp.int32, (t_len, TILE), 0)

        @pl.when(j < ng)
        def _():
            pooled = _pool_fwd_math(a_ref[...].astype(F32), j, t_idx)
            mixed = jnp.dot(pooled.astype(BF16), pw_ref[...].astype(BF16), preferred_element_type=F32)
            o_ref[...] = (mixed * ps_ref[...]).astype(BF16)

        @pl.when(j >= ng)
        def _():
            c = gc_ref[...].astype(F32) * xb_ref[...].astype(F32)
            y = _conv_fwd_math(c, cw_ref, cb_ref, t_idx)
            o_ref[...] = (gb_ref[...].astype(F32) * y).astype(BF16)

    def pool_j(j):
        return jnp.minimum(j, ng - 1)

    def conv_j(j):
        return jnp.maximum(j - ng, 0)

    in_specs = [
        pl.BlockSpec((t_len, TILE), lambda s, j: (s, pool_j(j))),
        pl.BlockSpec((t_len, TILE), lambda s, j: (s, ng + conv_j(j))),
        pl.BlockSpec((t_len, TILE), lambda s, j: (s, 2 * ng + conv_j(j))),
        pl.BlockSpec((t_len, TILE), lambda s, j: (s, 3 * ng + conv_j(j))),
        pl.BlockSpec((None, TILE, TILE), lambda s, j: (pool_j(j), 0, 0)),
        pl.BlockSpec((None, 1, TILE), lambda s, j: (pool_j(j), 0, 0)),
        pl.BlockSpec((3, TILE), lambda s, j: (0, conv_j(j))),
        pl.BlockSpec((1, TILE), lambda s, j: (0, conv_j(j))),
    ]
    res, rode = _call(
        body, name="ab_mixer_fwd", grid=(nseq, 2 * ng), in_specs=in_specs,
        out_specs=[pl.BlockSpec((t_len, TILE), lambda s, j: (s, j))],
        out_shape=[jax.ShapeDtypeStruct((m, 2 * ng * TILE), BF16)], scratch_shapes=[],
        semantics=("parallel", "arbitrary"), args=[p, p, p, p, pool_w, pool_scale, conv_w, conv_b], rider=rider)
    return res[0] if rider is None else (res[0], rode)


def _ab_bwd(p, dmix, pool_w, pool_scale, conv_w, conv_b, nseq, t_len, rider=None):
    m = p.shape[0]
    ng = 4

    def body(a_ref, xb_ref, gb_ref, gc_ref, dma_ref, dmb_ref, pw_ref, ps_ref, cw_ref, cb_ref,
             da_ref, dxb_ref, dgb_ref, dgc_ref, dpw_ref, dps_ref, dcw_ref, dcb_ref):
        j = pl.program_id(0)
        first = pl.program_id(1) == 0
        t_idx = lax.broadcasted_iota(jnp.int32, (t_len, TILE), 0)

        pooled = _pool_fwd_math(a_ref[...].astype(F32), j, t_idx).astype(BF16)
        w_bf = pw_ref[...].astype(BF16)
        mixed = jnp.dot(pooled, w_bf, preferred_element_type=F32)
        dm = dma_ref[...].astype(F32)
        dps = jnp.sum(dm * mixed, axis=0, keepdims=True)
        dmixed = (dm * ps_ref[...]).astype(BF16)
        dpw = lax.dot_general(pooled, dmixed, TN_DIMS, preferred_element_type=F32)
        dpooled = lax.dot_general(dmixed, w_bf, NT_DIMS, preferred_element_type=F32)
        da_ref[...] = _pool_bwd_math(dpooled, j, t_idx, t_len).astype(BF16)

        xb = xb_ref[...].astype(F32)
        gb = gb_ref[...].astype(F32)
        gc = gc_ref[...].astype(F32)
        d = dmb_ref[...].astype(F32)
        c = gc * xb
        c1 = _shift_down(c, 1, t_idx)
        c2 = _shift_down(c, 2, t_idx)
        y = cw_ref[0:1, :] * c2 + cw_ref[1:2, :] * c1 + cw_ref[2:3, :] * c + cb_ref[...]
        dgb_ref[...] = (d * y).astype(BF16)
        dy = d * gb
        dc = (cw_ref[2:3, :] * dy + cw_ref[1:2, :] * _shift_up(dy, 1, t_idx, t_len)
              + cw_ref[0:1, :] * _shift_up(dy, 2, t_idx, t_len))
        dgc_ref[...] = (dc * xb).astype(BF16)
        dxb_ref[...] = (dc * gc).astype(BF16)
        dcw = jnp.concatenate([jnp.sum(dy * c2, axis=0, keepdims=True),
                               jnp.sum(dy * c1, axis=0, keepdims=True),
                               jnp.sum(dy * c, axis=0, keepdims=True)], axis=0)
        dcb = jnp.sum(dy, axis=0, keepdims=True)

        @pl.when(first)
        def _():
            dpw_ref[...] = dpw
            dps_ref[...] = dps
            dcw_ref[...] = dcw
            dcb_ref[...] = dcb

        @pl.when(jnp.logical_not(first))
        def _():
            dpw_ref[...] += dpw
            dps_ref[...] += dps
            dcw_ref[...] += dcw
            dcb_ref[...] += dcb

    def col(k):
        return pl.BlockSpec((t_len, TILE), lambda j, s: (s, k * ng + j))

    in_specs = [
        col(0), col(1), col(2), col(3), col(0), col(1),
        pl.BlockSpec((None, TILE, TILE), lambda j, s: (j, 0, 0)),
        pl.BlockSpec((None, 1, TILE), lambda j, s: (j, 0, 0)),
        pl.BlockSpec((3, TILE), lambda j, s: (0, j)),
        pl.BlockSpec((1, TILE), lambda j, s: (0, j)),
    ]
    piece = pl.BlockSpec((t_len, TILE), lambda j, s: (s, j))
    out_specs = [
        piece, piece, piece, piece,
        pl.BlockSpec((None, TILE, TILE), lambda j, s: (j, 0, 0)),
        pl.BlockSpec((None, 1, TILE), lambda j, s: (j, 0, 0)),
        pl.BlockSpec((3, TILE), lambda j, s: (0, j)),
        pl.BlockSpec((1, TILE), lambda j, s: (0, j)),
    ]
    w = ng * TILE
    out_shape = [jax.ShapeDtypeStruct((m, w), BF16)] * 4 + [
        jax.ShapeDtypeStruct((ng, TILE, TILE), F32), jax.ShapeDtypeStruct((ng, 1, TILE), F32),
        jax.ShapeDtypeStruct((3, w), F32), jax.ShapeDtypeStruct((1, w), F32)]
    res, rode = _call(
        body, name="ab_mixer_bwd", grid=(ng, nseq), in_specs=in_specs, out_specs=out_specs, out_shape=out_shape,
        scratch_shapes=[], semantics=("parallel", "arbitrary"),
        args=[p, p, p, p, dmix, dmix, pool_w, pool_scale, conv_w, conv_b], rider=rider)
    return res if rider is None else (res, rode)


SGU_ROWS = 512
INV_SQRT2 = 1.0 / math.sqrt(2.0)
INV_SQRT_2PI = 1.0 / math.sqrt(2.0 * math.pi)


def _gelu(x):
    return 0.5 * x * (1.0 + lax.erf(x * INV_SQRT2))


def _gelu_grad(x):
    return 0.5 * (1.0 + lax.erf(x * INV_SQRT2)) + x * (INV_SQRT_2PI * jnp.exp(-0.5 * x * x))


def _causal_tile(transposed=False):
    r = lax.broadcasted_iota(jnp.int32, (TILE, TILE), 0)
    c = lax.broadcasted_iota(jnp.int32, (TILE, TILE), 1)
    return r <= c if transposed else c <= r


def _sgu_norm(v, g_ref, b_ref):
    mu = jnp.mean(v, axis=-1, keepdims=True)
    xc = v - mu
    rstd = lax.rsqrt(jnp.mean(xc * xc, axis=-1, keepdims=True) + EPS)
    xhat = xc * rstd
    return xhat, rstd, xhat * g_ref[...] + b_ref[...]


def _sgu_fwd(p, norm_g, norm_b, w_s, bias_tile):
    m = p.shape[0]
    ng = 4
    width = ng * TILE

    def body(u_ref, v_ref, g_ref, b_ref, w_ref, bias_ref, o_ref):
        u = _gelu(u_ref[...].astype(F32))
        _, _, vln = _sgu_norm(_gelu(v_ref[...].astype(F32)), g_ref, b_ref)
        vln = vln.astype(BF16)
        causal = _causal_tile()
        for g in range(ng):
            cols = slice(g * TILE, (g + 1) * TILE)
            wg = jnp.where(causal, w_ref[g], 0.0).astype(BF16)
            for n in range(SGU_ROWS // TILE):
                rows = slice(n * TILE, (n + 1) * TILE)
                s = jnp.dot(wg, vln[rows, cols], preferred_element_type=F32) + bias_ref[g]
                o_ref[rows, cols] = (u[rows, cols] * s).astype(BF16)

    vec = pl.BlockSpec((1, width), lambda i: (0, 0))
    tiles = pl.BlockSpec((ng, TILE, TILE), lambda i: (0, 0, 0))
    return pl.pallas_call(
        body, name="sgu_fwd", grid=(m // SGU_ROWS,),
        in_specs=[pl.BlockSpec((SGU_ROWS, width), lambda i: (i, 0)),
                  pl.BlockSpec((SGU_ROWS, width), lambda i: (i, 1)), vec, vec, tiles, tiles],
        out_specs=pl.BlockSpec((SGU_ROWS, width), lambda i: (i, 0)),
        out_shape=jax.ShapeDtypeStruct((m, width), BF16),
        compiler_params=_params(("parallel",)),
    )(p, p, norm_g, norm_b, w_s, bias_tile)


def _sgu_bwd(p, dmix, norm_g, norm_b, w_s, w_s_t, bias_tile):
    m = p.shape[0]
    ng = 4
    width = ng * TILE

    def body(u_ref, v_ref, dc_ref, g_ref, b_ref, w_ref, wt_ref, bias_ref,
             du_ref, dv_ref, dw_ref, dbs_ref, dg_ref, db_ref, ds_scr, dvln_scr):
        u_pre = u_ref[...].astype(F32)
        v_pre = v_ref[...].astype(F32)
        u = _gelu(u_pre)
        xhat, rstd, vln = _sgu_norm(_gelu(v_pre), g_ref, b_ref)
        vln = vln.astype(BF16)
        dc = dc_ref[...].astype(F32)
        causal = _causal_tile()
        ones = jnp.ones((TILE, TILE), BF16)
        first = pl.program_id(0) == 0
        for g in range(ng):
            cols = slice(g * TILE, (g + 1) * TILE)
            wg = jnp.where(causal, w_ref[g], 0.0).astype(BF16)
            wgt = jnp.where(_causal_tile(transposed=True), wt_ref[g], 0.0).astype(BF16)
            dw_acc = jnp.zeros((TILE, TILE), F32)
            dbs_acc = jnp.zeros((TILE, TILE), F32)
            for n in range(SGU_ROWS // TILE):
                rows = slice(n * TILE, (n + 1) * TILE)
                vt = vln[rows, cols]
                s = jnp.dot(wg, vt, preferred_element_type=F32) + bias_ref[g]
                ds_scr[rows, cols] = dc[rows, cols] * s
                ds = (dc[rows, cols] * u[rows, cols]).astype(BF16)
                dw_acc += lax.dot_general(ds, vt, NT_DIMS, preferred_element_type=F32)
                dbs_acc += jnp.dot(ds, ones, preferred_element_type=F32)
                dvln_scr[rows, cols] = jnp.dot(wgt, ds, preferred_element_type=F32)
            dw_g = jnp.where(causal, dw_acc, 0.0)

            @pl.when(first)
            def _():
                dw_ref[g] = dw_g
                dbs_ref[g] = dbs_acc

            @pl.when(jnp.logical_not(first))
            def _():
                dw_ref[g] += dw_g
                dbs_ref[g] += dbs_acc

        du_ref[...] = (ds_scr[...] * _gelu_grad(u_pre)).astype(BF16)
        dvln = dvln_scr[...]
        dxhat = dvln * g_ref[...]
        dv = rstd * (dxhat - jnp.mean(dxhat, axis=-1, keepdims=True)
                     - xhat * jnp.mean(dxhat * xhat, axis=-1, keepdims=True))
        dv_ref[...] = (dv * _gelu_grad(v_pre)).astype(BF16)
        dg_part = jnp.sum(dvln * xhat, axis=0, keepdims=True)
        db_part = jnp.sum(dvln, axis=0, keepdims=True)

        @pl.when(first)
        def _():
            dg_ref[...] = dg_part
            db_ref[...] = db_part

        @pl.when(jnp.logical_not(first))
        def _():
            dg_ref[...] += dg_part
            db_ref[...] += db_part

    vec = pl.BlockSpec((1, width), lambda i: (0, 0))
    tiles = pl.BlockSpec((ng, TILE, TILE), lambda i: (0, 0, 0))
    rows0 = pl.BlockSpec((SGU_ROWS, width), lambda i: (i, 0))
    rows1 = pl.BlockSpec((SGU_ROWS, width), lambda i: (i, 1))
    return pl.pallas_call(
        body, name="sgu_bwd", grid=(m // SGU_ROWS,),
        in_specs=[rows0, rows1, rows0, vec, vec, tiles, tiles, tiles],
        out_specs=[rows0, rows0, tiles, tiles, vec, vec],
        out_shape=[jax.ShapeDtypeStruct((m, width), BF16), jax.ShapeDtypeStruct((m, width), BF16),
                   jax.ShapeDtypeStruct((ng, TILE, TILE), F32), jax.ShapeDtypeStruct((ng, TILE, TILE), F32),
                   jax.ShapeDtypeStruct((1, width), F32), jax.ShapeDtypeStruct((1, width), F32)],
        scratch_shapes=[pltpu.VMEM((SGU_ROWS, width), F32), pltpu.VMEM((SGU_ROWS, width), F32)],
        compiler_params=_params(("arbitrary",)),
    )(p, p, dmix, norm_g, norm_b, w_s, w_s_t, bias_tile)


SB_DH = 64
SB_SCALE = 1.0 / math.sqrt(SB_DH)


SB_BLOCK = 256
SB_SUB = SB_BLOCK // TILE
SB_PASS = 4


def _split_passes(i):
    rem = i % SB_PASS
    return i // SB_PASS, rem >= 2, rem % 2 == 1


def _sum_matrix(kind):
    j = lax.broadcasted_iota(jnp.int32, (TILE, 2 * TILE), 0)
    s = lax.broadcasted_iota(jnp.int32, (TILE, 2 * TILE), 1)
    tri = {"after": j > s, "upto": j <= s, "before": j < s}[kind]
    return jnp.where(jnp.logical_or(s >= TILE, tri), 1.0, 0.0).astype(BF16)


def _strict_mask():
    r = lax.broadcasted_iota(jnp.int32, (SB_BLOCK, SB_BLOCK), 0)
    c = lax.broadcasted_iota(jnp.int32, (SB_BLOCK, SB_BLOCK), 1)
    return c < r


def _head_lanes(h):
    lane = lax.broadcasted_iota(jnp.int32, (1, TILE), 1)
    return (lane >= h * SB_DH) & (lane < (h + 1) * SB_DH)


def _softplus(z):
    return jnp.maximum(z, 0.0) + jnp.log(1.0 + jnp.exp(-jnp.abs(z)))


def _sb_fwd(p, nseq, t_len, gather):
    m = p.shape[0]
    npair = 4
    ng = len(gather)
    last_step = nseq * npair - 1

    def body(q_ref, k_ref, v_ref, *rest):
        o_ref, lt_ref = rest[ng:ng + 2]
        kh_ref, vh_ref = rest[2 * ng + 2:2 * ng + 4]
        step = pl.program_id(0) * npair + pl.program_id(1)
        send, forward, finish = _gather_steps(rest[ng + 2:2 * ng + 2], *rest[2 * ng + 4:])
        pl.when(step == 0)(send)
        pl.when(step == (last_step + 1) // 2)(forward)
        for h in range(2):
            keep = _head_lanes(h)
            kh_ref[h] = jnp.where(keep, k_ref[...], 0).astype(BF16)
            vh_ref[h] = jnp.where(keep, v_ref[...], 0).astype(BF16)
        summat = _sum_matrix("after")
        strict = _strict_mask()

        def one_pass(q, row0, nsub, diag, state):
            rows = pl.ds(row0, nsub * TILE)
            z, sp, pieces = [], [], []
            for h in range(2):
                zh = lax.dot_general(q, kh_ref[h, rows, :], NT_DIMS, preferred_element_type=F32)
                sph = _softplus(zh)
                logkeep = jnp.where(strict, -sph, 0.0) if diag else -sph
                z.append(zh)
                sp.append(sph)
                pieces += [logkeep[:, b * TILE:(b + 1) * TILE] for b in range(nsub)]
            sums = jnp.dot(jnp.concatenate(pieces, axis=0).astype(BF16), summat, preferred_element_type=F32)
            out = []
            for h in range(2):
                carry, acc = state[2 * h], state[2 * h + 1]
                after = [None] * nsub
                for b in reversed(range(nsub)):
                    part = sums[(h * nsub + b) * SB_BLOCK:(h * nsub + b + 1) * SB_BLOCK]
                    after[b] = part[:, :TILE] + carry
                    carry = carry + part[:, TILE:]
                w = jnp.exp(z[h] - sp[h] + jnp.concatenate(after, axis=1))
                if diag:
                    w = jnp.where(strict, w, 0.0)
                out += [carry, acc + jnp.dot(w.astype(BF16), vh_ref[h, rows, :], preferred_element_type=F32)]
            return tuple(out)

        def q_block(i, _):
            r0 = pl.multiple_of(i * SB_BLOCK, SB_BLOCK)
            q = q_ref[pl.ds(r0, SB_BLOCK), :] * SB_SCALE
            zero = jnp.zeros((SB_BLOCK, TILE), F32)
            state = one_pass(q, r0, SB_SUB, True, (zero,) * 4)
            full, two, one = _split_passes(i)
            state = lax.fori_loop(
                0, full,
                lambda jj, st: one_pass(q, pl.multiple_of((i - SB_PASS * (jj + 1)) * SB_BLOCK, SB_BLOCK),
                                        SB_PASS * SB_SUB, False, st),
                state)
            state = lax.cond(
                two, lambda st: one_pass(q, pl.multiple_of((i % 2) * SB_BLOCK, SB_BLOCK), 2 * SB_SUB, False, st),
                lambda st: st, state)
            state = lax.cond(one, lambda st: one_pass(q, 0, SB_SUB, False, st), lambda st: st, state)
            o_ref[pl.ds(r0, SB_BLOCK), :] = (state[1] + state[3]).astype(BF16)
            lt_ref[pl.ds(r0, SB_BLOCK), :] = jnp.where(_head_lanes(0), state[0], state[2])
            return 0

        lax.fori_loop(0, t_len // SB_BLOCK, q_block, 0)
        pl.when(step == last_step)(finish)

    def col(k):
        return pl.BlockSpec((t_len, TILE), lambda s, hp: (s, k * npair + hp))

    out = pl.BlockSpec((t_len, TILE), lambda s, hp: (s, hp))
    res = pl.pallas_call(
        body, name="stickbreak_fwd", grid=(nseq, npair), in_specs=[col(2), col(3), col(4)] + [ANY] * ng,
        out_specs=[out, out] + [ANY] * ng,
        out_shape=[jax.ShapeDtypeStruct((m, npair * TILE), BF16), jax.ShapeDtypeStruct((m, npair * TILE), F32)]
        + [jax.ShapeDtypeStruct(b.shape, b.dtype) for b in gather],
        input_output_aliases={3 + a: 2 + a for a in range(ng)},
        scratch_shapes=[pltpu.VMEM((2, t_len, TILE), BF16), pltpu.VMEM((2, t_len, TILE), BF16)] + _gather_sems(ng),
        compiler_params=pltpu.CompilerParams(dimension_semantics=("arbitrary", "arbitrary"),
                                             vmem_limit_bytes=VMEM_LIMIT_BYTES, has_side_effects=True),
    )(p, p, p, *gather)
    return res[0], res[1], res[2:]


def _sb_bwd(p, dmix, ltot, nseq, t_len, exchange):
    m = p.shape[0]
    npair = 4
    ne = len(exchange)
    last_step = nseq * npair - 1

    def body(q_ref, k_ref, v_ref, do_ref, lt_ref, *rest):
        dq_ref, dk_ref, dv_ref = rest[ne:ne + 3]
        kh_ref, vh_ref, dk_acc, dv_acc = rest[2 * ne + 3:2 * ne + 7]
        step = pl.program_id(0) * npair + pl.program_id(1)
        send, finish = _exchange_steps(rest[:ne], rest[ne + 3:2 * ne + 3], *rest[2 * ne + 7:])
        pl.when(step == 0)(send)
        for h in range(2):
            keep = _head_lanes(h)
            kh_ref[h] = jnp.where(keep, k_ref[...], 0).astype(BF16)
            vh_ref[h] = jnp.where(keep, v_ref[...], 0).astype(BF16)
        dk_acc[...] = jnp.zeros_like(dk_acc)
        dv_acc[...] = jnp.zeros_like(dv_acc)
        sum_upto = _sum_matrix("upto")
        sum_before = _sum_matrix("before")
        strict = _strict_mask()
        lane = lax.broadcasted_iota(jnp.int32, (SB_BLOCK, TILE), 1)

        def running(x, matrix, start, nsub):
            pieces = [x[h][:, b * TILE:(b + 1) * TILE] for h in range(2) for b in range(nsub)]
            sums = jnp.dot(jnp.concatenate(pieces, axis=0).astype(BF16), matrix, preferred_element_type=F32)
            wide, ends = [], []
            for h in range(2):
                total, cols = start[h], []
                for b in range(nsub):
                    part = sums[(h * nsub + b) * SB_BLOCK:(h * nsub + b + 1) * SB_BLOCK]
                    cols.append(part[:, :TILE] + total)
                    total = total + part[:, TILE:]
                wide.append(jnp.concatenate(cols, axis=1))
                ends.append(total)
            return wide, ends

        def one_pass(q, do, qh, doh, ltot, row0, nsub, diag, state):
            rows = pl.ds(row0, nsub * TILE)
            z, sp, logkeep = [], [], []
            for h in range(2):
                zh = lax.dot_general(q, kh_ref[h, rows, :], NT_DIMS, preferred_element_type=F32)
                sph = _softplus(zh)
                z.append(zh)
                sp.append(sph)
                logkeep.append(jnp.where(strict, -sph, 0.0) if diag else -sph)
            upto, sum_l = running(logkeep, sum_upto, [state[0], state[3]], nsub)
            w, g = [], []
            for h in range(2):
                wh = jnp.exp(z[h] - sp[h] + (ltot[h] - upto[h]))
                if diag:
                    wh = jnp.where(strict, wh, 0.0)
                w.append(wh)
                g.append(wh * lax.dot_general(do, vh_ref[h, rows, :], NT_DIMS, preferred_element_type=F32))
            g_before, sum_g = running(g, sum_before, [state[1], state[4]], nsub)
            out, dk_new, dv_new = [], 0.0, 0.0
            for h in range(2):
                dz = g[h] - jnp.exp(z[h] - sp[h]) * (g[h] + g_before[h])
                if diag:
                    dz = jnp.where(strict, dz, 0.0)
                dzb = dz.astype(BF16)
                dq = state[3 * h + 2] + jnp.dot(dzb, kh_ref[h, rows, :], preferred_element_type=F32)
                dk_new = dk_new + lax.dot_general(dzb, qh[h], TN_DIMS, preferred_element_type=F32)
                dv_new = dv_new + lax.dot_general(w[h].astype(BF16), doh[h], TN_DIMS, preferred_element_type=F32)
                out += [sum_l[h], sum_g[h], dq]
            dk_acc[rows, :] += dk_new
            dv_acc[rows, :] += dv_new
            return tuple(out)

        def q_block(i, _):
            r0 = pl.multiple_of(i * SB_BLOCK, SB_BLOCK)
            q = q_ref[pl.ds(r0, SB_BLOCK), :] * SB_SCALE
            do = do_ref[pl.ds(r0, SB_BLOCK), :]
            lt = lt_ref[pl.ds(r0, SB_BLOCK), :]
            qh, doh, ltot = [], [], []
            for h in range(2):
                keep = _head_lanes(h)
                qh.append(jnp.where(keep, q, 0).astype(BF16))
                doh.append(jnp.where(keep, do, 0).astype(BF16))
                ltot.append(jnp.sum(jnp.where(lane == h * SB_DH, lt, 0.0), axis=1, keepdims=True))
            zero = jnp.zeros((SB_BLOCK, TILE), F32)
            full, two, one = _split_passes(i)
            state = lax.fori_loop(
                0, full,
                lambda jj, st: one_pass(q, do, qh, doh, ltot, pl.multiple_of(SB_PASS * jj * SB_BLOCK, SB_BLOCK),
                                        SB_PASS * SB_SUB, False, st),
                (zero,) * 6)
            state = lax.cond(
                two,
                lambda st: one_pass(q, do, qh, doh, ltot, pl.multiple_of(SB_PASS * full * SB_BLOCK, SB_BLOCK),
                                    2 * SB_SUB, False, st),
                lambda st: st, state)
            state = lax.cond(
                one,
                lambda st: one_pass(q, do, qh, doh, ltot, pl.multiple_of((i - 1) * SB_BLOCK, SB_BLOCK), SB_SUB, False, st),
                lambda st: st, state)
            state = one_pass(q, do, qh, doh, ltot, r0, SB_SUB, True, state)
            dq_ref[pl.ds(r0, SB_BLOCK), :] = ((state[2] + state[5]) * SB_SCALE).astype(BF16)
            return 0

        lax.fori_loop(0, t_len // SB_BLOCK, q_block, 0)
        dk_ref[...] = dk_acc[...].astype(BF16)
        dv_ref[...] = dv_acc[...].astype(BF16)
        pl.when(step == last_step)(finish)

    def col(k):
        return pl.BlockSpec((t_len, TILE), lambda s, hp: (s, k * npair + hp))

    out = pl.BlockSpec((t_len, TILE), lambda s, hp: (s, hp))
    width = npair * TILE
    res = pl.pallas_call(
        body, name="stickbreak_bwd", grid=(nseq, npair),
        in_specs=[col(2), col(3), col(4), col(1), out] + [ANY] * ne, out_specs=[out, out, out] + [ANY] * ne,
        out_shape=[jax.ShapeDtypeStruct((m, width), BF16)] * 3 + _exchange_shapes(exchange),
        scratch_shapes=[pltpu.VMEM((2, t_len, TILE), BF16), pltpu.VMEM((2, t_len, TILE), BF16),
                        pltpu.VMEM((t_len, TILE), F32), pltpu.VMEM((t_len, TILE), F32)] + _exchange_sems(ne),
        compiler_params=pltpu.CompilerParams(dimension_semantics=("arbitrary", "arbitrary"),
                                             vmem_limit_bytes=VMEM_LIMIT_BYTES, has_side_effects=True),
    )(p, p, p, dmix, ltot, *exchange)
    return res[0], res[1], res[2], res[3:]


def _adam_math(w, g, m, v):
    m = ADAM_B1 * m + (1.0 - ADAM_B1) * g
    v = ADAM_B2 * v + (1.0 - ADAM_B2) * (g * g)
    m_hat = m / (1.0 - ADAM_B1 ** ADAM_STEP)
    v_hat = v / (1.0 - ADAM_B2 ** ADAM_STEP)
    delta = -ADAM_LR * (m_hat / (jnp.sqrt(v_hat) + ADAM_EPS) + ADAM_WD * w)
    return delta, m, v


def _cast_place(w, layer, pos, *, name):
    _, r, c = w.shape
    tr = min(r, 256)

    def body(pos_ref, w_ref, o_ref):
        o_ref[...] = w_ref[...].astype(BF16)

    grid_spec = pltpu.PrefetchScalarGridSpec(
        num_scalar_prefetch=1, grid=(r // tr,),
        in_specs=[pl.BlockSpec((None, tr, c), lambda i, pos_ref: (layer, i, 0))],
        out_specs=pl.BlockSpec((None, None, tr, c), lambda i, pos_ref: (0, pos_ref[0], i, 0)))
    return pl.pallas_call(
        body, name=name, grid_spec=grid_spec, out_shape=jax.ShapeDtypeStruct((1, N_CHIP, r, c), BF16),
        compiler_params=_params(("parallel",)),
    )(pos, w)


def _cast_place_all(items, pos, *, name, rider=None):
    tiles = [min(w.shape[1], 256) for w, _ in items]
    counts = [w.shape[1] // t for (w, _), t in zip(items, tiles)]
    starts = [sum(counts[:a]) for a in range(len(items))]
    n = len(items)

    def body(pos_ref, *refs):
        i = pl.program_id(0)
        for a in range(n):
            @pl.when((i >= starts[a]) & (i < starts[a] + counts[a]))
            def _():
                refs[n + a][...] = refs[a][...].astype(BF16)

    def block(a):
        return lambda i: jnp.clip(i - starts[a], 0, counts[a] - 1)

    in_specs, out_specs, out_shape = [], [], []
    for a, ((w, layer), t) in enumerate(zip(items, tiles)):
        _, r, c = w.shape
        in_specs.append(pl.BlockSpec((None, t, c), lambda i, pos_ref, a=a, layer=layer: (layer, block(a)(i), 0)))
        out_specs.append(pl.BlockSpec((None, None, t, c), lambda i, pos_ref, a=a: (0, pos_ref[0], block(a)(i), 0)))
        out_shape.append(jax.ShapeDtypeStruct((1, N_CHIP, r, c), BF16))
    res, rode = _call(body, name=name, grid=(sum(counts),), in_specs=in_specs, out_specs=out_specs,
                      out_shape=out_shape, scratch_shapes=[], semantics=("arbitrary",),
                      args=[w for w, _ in items], rider=rider, prefetch=pos)
    return res if rider is None else (res, rode)


def _pair_sum(mine, got, pos, *, name):
    l_dim, s_dim, h, c = got.shape
    th = min(h, 512)
    nt = h // th

    def body(pos_ref, a_ref, b_ref, o_ref):
        o_ref[...] = (a_ref[...].astype(F32) + b_ref[...].astype(F32)).astype(BF16)

    spec = pl.BlockSpec((None, None, th, c), lambda l, s, i, pos_ref: (l, s, i, 0))
    grid_spec = pltpu.PrefetchScalarGridSpec(
        num_scalar_prefetch=1, grid=(l_dim, s_dim, nt),
        in_specs=[pl.BlockSpec((None, None, th, c), lambda l, s, i, pos_ref: (l, s, pos_ref[1] * nt + i, 0)), spec],
        out_specs=spec)
    return pl.pallas_call(
        body, name=name, grid_spec=grid_spec, out_shape=jax.ShapeDtypeStruct(got.shape, BF16),
        compiler_params=_params(("parallel",) * 3),
    )(pos, mine, got)


def _chip_sum(sums, landed, pos, *, name):
    l_dim, _, h, c = sums.shape
    th = min(h, 512)
    nt = h // th

    def body(pos_ref, own, r0, r1, r2, o_ref):
        o_ref[...] = ((own[...].astype(F32) + r0[...].astype(F32)) + r1[...].astype(F32)) + r2[...].astype(F32)

    def piece(k):
        return pl.BlockSpec((None, None, th, c), lambda l, i, pos_ref: (l, k, i, 0))

    grid_spec = pltpu.PrefetchScalarGridSpec(
        num_scalar_prefetch=1, grid=(l_dim, nt),
        in_specs=[pl.BlockSpec((None, None, th, c), lambda l, i, pos_ref: (l, pos_ref[0], i, 0)),
                  piece(0), piece(1), piece(2)],
        out_specs=pl.BlockSpec((None, th, c), lambda l, i, pos_ref: (l, pos_ref[1] * nt + i, 0)))
    return pl.pallas_call(
        body, name=name, grid_spec=grid_spec, out_shape=jax.ShapeDtypeStruct((l_dim, 2 * h, c), F32),
        compiler_params=_params(("parallel",) * 2),
    )(pos, sums, landed, landed, landed)


def _adam_big(w, m, v, grads, *, name):
    l_dim, r, c = w.shape
    assert len(grads) == l_dim
    tr = min(r, 512)

    def body(*refs):
        w_ref, m_ref, v_ref = refs[:3]
        g_refs = refs[3:3 + l_dim]
        go_ref, d_ref, mo_ref, vo_ref = refs[3 + l_dim:]
        g = g_refs[0][...]
        for l in range(1, l_dim):
            g = jnp.where(pl.program_id(0) == l, g_refs[l][...], g)
        delta, m_new, v_new = _adam_math(w_ref[...], g, m_ref[...], v_ref[...])
        go_ref[...] = g
        d_ref[...] = delta
        mo_ref[...] = m_new
        vo_ref[...] = v_new

    spec = pl.BlockSpec((None, tr, c), lambda l, i: (l, i, 0))
    gspec = pl.BlockSpec((None, tr, c), lambda l, i: (0, i, 0))
    return pl.pallas_call(
        body, name=name, grid=(l_dim, r // tr), in_specs=[spec] * 3 + [gspec] * l_dim, out_specs=[spec] * 4,
        out_shape=[jax.ShapeDtypeStruct(w.shape, F32)] * 4, compiler_params=_params(("parallel",) * 2),
    )(w, m, v, *grads)


def _position():
    return lax.axis_index("x"), lax.axis_index("y"), lax.axis_index("c")


def _other_chips(x, y):
    return [(1 - x, y), (x, 1 - y), (1 - x, 1 - y)]


def _remote(src, dst, send_sem, recv_sem, device):
    return pltpu.make_async_remote_copy(src_ref=src, dst_ref=dst, send_sem=send_sem, recv_sem=recv_sem,
                                        device_id=device, device_id_type=MESH)


ANY = pl.BlockSpec(memory_space=pl.ANY)


def _gather_sems(n):
    return [pltpu.SemaphoreType.DMA((3 * n,))] * 4


def _gather_steps(outs, send_sems, recv_sems, fwd_send, fwd_recv):
    n = len(outs)
    x, y, c = _position()
    chips = _other_chips(x, y)
    sibling = (x, y, 1 - c)

    def half(a, chip, core):
        h = outs[a].shape[2] // 2
        return outs[a].at[:, 2 * chip[0] + chip[1], pl.ds(core * h, h), :]

    def over_ici(a, k, chip):
        block = half(a, chip, c)
        return _remote(block, block, send_sems.at[3 * a + k], recv_sems.at[3 * a + k], (*chips[k], c))

    def over_d2d(a, k, core):
        block = half(a, chips[k], core)
        return _remote(block, block, fwd_send.at[3 * a + k], fwd_recv.at[3 * a + k], sibling)

    def send():
        for a in range(n):
            for k in range(3):
                over_ici(a, k, (x, y)).start()

    def forward():
        for k in range(3):
            for a in range(n):
                over_ici(a, k, chips[k]).wait_recv()
                over_d2d(a, k, c).start()

    def finish():
        for k in range(3):
            for a in range(n):
                over_d2d(a, k, 1 - c).wait_recv()
        for a in range(n):
            for k in range(3):
                over_ici(a, k, (x, y)).wait_send()
                over_d2d(a, k, c).wait_send()

    return send, forward, finish


def _swap_halves(grads, *, name):
    n = len(grads)

    def body(*refs):
        send, finish = _swap_steps(refs[:n], refs[n:2 * n], *refs[2 * n:])
        send()
        finish()

    sem = pltpu.SemaphoreType.DMA((n,))
    return pl.pallas_call(
        body, name=name, in_specs=[ANY] * n, out_specs=[ANY] * n, out_shape=_swap_shapes(grads),
        scratch_shapes=[sem, sem], compiler_params=pltpu.CompilerParams(has_side_effects=True),
    )(*grads)


def _swap_shapes(grads):
    return [jax.ShapeDtypeStruct(g.shape[:2] + (g.shape[2] // 2, g.shape[3]), g.dtype) for g in grads]


def _swap_steps(ins, outs, send_sems, recv_sems):
    x, y, c = _position()

    def copy(a):
        h = ins[a].shape[2] // 2
        return _remote(ins[a].at[:, :, pl.ds((1 - c) * h, h), :], outs[a], send_sems.at[a], recv_sems.at[a],
                       (x, y, 1 - c))

    def send():
        for a in range(len(ins)):
            copy(a).start()

    def finish():
        for a in range(len(ins)):
            copy(a).wait()

    return send, finish


def _exchange_shapes(sums):
    return [jax.ShapeDtypeStruct((s.shape[0], 3) + s.shape[2:], s.dtype) for s in sums]


def _exchange_sems(n):
    return [pltpu.SemaphoreType.DMA((3 * n,))] * 2


def _exchange_steps(ins, outs, send_sems, recv_sems):
    n = len(ins)
    x, y, c = _position()
    chips = _other_chips(x, y)

    def copy(a, k):
        chip = chips[k]
        return _remote(ins[a].at[:, 2 * chip[0] + chip[1]], outs[a].at[:, k],
                       send_sems.at[3 * a + k], recv_sems.at[3 * a + k], (*chip, c))

    def send():
        for a in range(n):
            for k in range(3):
                copy(a, k).start()

    def finish():
        for a in range(n):
            for k in range(3):
                copy(a, k).wait()

    return send, finish


def _join_halves(bufs, *, name):
    n = len(bufs)

    def body(*refs):
        send, finish = _join_steps(refs[n:2 * n], *refs[2 * n:])
        send()
        finish()

    sem = pltpu.SemaphoreType.DMA((n,))
    return pl.pallas_call(
        body, name=name, in_specs=[ANY] * n, out_specs=[ANY] * n,
        out_shape=[jax.ShapeDtypeStruct(b.shape, b.dtype) for b in bufs],
        input_output_aliases={a: a for a in range(n)},
        scratch_shapes=[sem, sem], compiler_params=pltpu.CompilerParams(has_side_effects=True),
    )(*bufs)


def _join_steps(outs, send_sems, recv_sems):
    x, y, c = _position()

    def copy(a, core):
        h = outs[a].shape[1] // 2
        half = outs[a].at[:, pl.ds(core * h, h), :]
        return _remote(half, half, send_sems.at[a], recv_sems.at[a], (x, y, 1 - c))

    def send():
        for a in range(len(outs)):
            copy(a, c).start()

    def finish():
        for a in range(len(outs)):
            copy(a, c).wait_send()
            copy(a, 1 - c).wait_recv()

    return send, finish


def _allgather_steps(ins, outs, send_sems, recv_sems, local_sems):
    n = len(ins)
    x, y, c = _position()
    me, sibling = (x, y, c), (x, y, 1 - c)
    chips = _other_chips(x, y)

    def slot(a, dev):
        return outs[a].at[4 * dev[0] + 2 * dev[1] + dev[2]]

    def copy(a, k, block, to, own=False):
        return _remote(ins[a] if own else slot(a, block), slot(a, block),
                       send_sems.at[7 * a + k], recv_sems.at[7 * a + k], to)

    def first(a):
        return [copy(a, 0, me, sibling, own=True)] + [copy(a, 1 + k, me, (*chips[k], c), own=True) for k in range(3)]

    def local(a):
        return pltpu.make_async_copy(ins[a], slot(a, me), local_sems.at[a])

    def send():
        for a in range(n):
            local(a).start()
            for cp in first(a):
                cp.start()

    def forward():
        for a in range(n):
            for k in range(3):
                copy(a, 1 + k, (*chips[k], c), me).wait_recv()
                copy(a, 4 + k, (*chips[k], c), sibling).start()

    def finish():
        for a in range(n):
            copy(a, 0, sibling, me).wait_recv()
            for k in range(3):
                copy(a, 4 + k, (*chips[k], 1 - c), me).wait_recv()
        for a in range(n):
            for cp in first(a) + [copy(a, 4 + k, (*chips[k], c), sibling) for k in range(3)]:
                cp.wait_send()
            local(a).wait()

    return send, forward, finish


def _allreduce_small(packs):
    n = len(packs)

    def body(*refs):
        ins, outs, gath = refs[:n], refs[n:2 * n], refs[2 * n:3 * n]
        send_sems, recv_sems = refs[3 * n:]
        x, y, c = _position()
        me, sibling = (x, y, c), (x, y, 1 - c)
        chips = _other_chips(x, y)

        def slot(a, dev):
            return gath[a].at[4 * dev[0] + 2 * dev[1] + dev[2]]

        def copy(a, k, block, to, src=None):
            return _remote(slot(a, block) if src is None else src, slot(a, block),
                           send_sems.at[7 * a + k], recv_sems.at[7 * a + k], to)

        started = []
        for a in range(n):
            slot(a, me)[...] = ins[a][...]
            first = [copy(a, 0, me, sibling, src=ins[a])]
            first += [copy(a, 1 + k, me, (*chip, c), src=ins[a]) for k, chip in enumerate(chips)]
            for cp in first:
                cp.start()
            started += first
        for a in range(n):
            for k, chip in enumerate(chips):
                copy(a, 1 + k, (*chip, c), me).wait_recv()
                cp = copy(a, 4 + k, (*chip, c), sibling)
                cp.start()
                started.append(cp)
        for a in range(n):
            copy(a, 0, sibling, me).wait_recv()
            for k, chip in enumerate(chips):
                copy(a, 4 + k, (*chip, 1 - c), me).wait_recv()
        for cp in started:
            cp.wait_send()
        for a in range(n):
            total = gath[a][0]
            for d in range(1, N_DEV):
                total = total + gath[a][d]
            outs[a][...] = total

    vmem = pl.BlockSpec(memory_space=pltpu.VMEM)
    sem = pltpu.SemaphoreType.DMA((7 * n,))
    return pl.pallas_call(
        body, name="allreduce_small", in_specs=[vmem] * n, out_specs=[vmem] * n,
        out_shape=[jax.ShapeDtypeStruct(p.shape, p.dtype) for p in packs],
        scratch_shapes=[pltpu.VMEM((N_DEV,) + p.shape, p.dtype) for p in packs] + [sem, sem],
        compiler_params=pltpu.CompilerParams(has_side_effects=True, vmem_limit_bytes=VMEM_LIMIT_BYTES),
    )(*packs)


LOSS_ROW = 1040


def _pad_rows(a, rows=8):
    return jnp.concatenate([a, jnp.zeros((rows - a.shape[0], a.shape[1]), a.dtype)], axis=0)

def _adam_small(wide, mid, narrow, late, params):
    names = ["mix_norm_g", "mlp_norm_g", "final_norm_g", "conv_b", "conv_w", "sgu_norm_g", "sgu_norm_b",
             "pool_w", "pool_scale", "sgu_w", "sgu_b"]
    n = len(names)

    def body(*refs):
        wmv = refs[4:4 + 3 * n]
        outs = refs[4 + 3 * n:]
        x, y, _ = _position()
        q = 2 * x + y

        def total(ref):
            t = ref[0]
            for dev in range(1, N_DEV):
                t = t + ref[dev]
            return t

        wide_sum, mid_sum, narrow_sum = total(refs[0]), total(refs[1]), total(refs[2])
        late_ref = refs[3]

        def my_quarter(rows):
            parts = [rows[:, s * TILE:(s + 1) * TILE] for s in range(N_CHIP)]
            return jnp.where(q == 0, parts[0], jnp.where(q == 1, parts[1], jnp.where(q == 2, parts[2], parts[3])))

        def tiles(first_row):
            return [((0, g), narrow_sum[first_row + g * TILE:first_row + (g + 1) * TILE, :]) for g in range(4)]

        grads = {
            "mix_norm_g": [((), wide_sum[0:2, :] + late_ref[0:2, :])],
            "mlp_norm_g": [((), wide_sum[8:10, :])],
            "final_norm_g": [((), wide_sum[16:17, :])],
            "conv_b": [((), mid_sum[0:1, :])],
            "conv_w": [((0,), my_quarter(mid_sum[8:11, :]))],
            "sgu_norm_g": [((), my_quarter(mid_sum[16:17, :]))],
            "sgu_norm_b": [((), my_quarter(mid_sum[24:25, :]))],
            "pool_w": tiles(0),
            "sgu_w": tiles(512),
            "pool_scale": [((0,), narrow_sum[1024:1028, :])],
            "sgu_b": [((0,), narrow_sum[1032:1036, :])],
        }
        outs[4 * n][...] = narrow_sum[LOSS_ROW:LOSS_ROW + 8, :]
        for i, name in enumerate(names):
            w_ref, m_ref, v_ref = wmv[3 * i:3 * i + 3]
            for lead, g in grads[name]:
                idx = lead + (slice(None), slice(None))
                delta, m_new, v_new = _adam_math(w_ref[idx], g, m_ref[idx], v_ref[idx])
                outs[4 * i][idx] = g
                outs[4 * i + 1][idx] = delta
                outs[4 * i + 2][idx] = m_new
                outs[4 * i + 3][idx] = v_new

    vmem = pl.BlockSpec(memory_space=pltpu.VMEM)
    args, out_shape = [wide, mid, narrow, late], []
    for name in names:
        w, m, v = params[name]
        args += [w, m, v]
        out_shape += [jax.ShapeDtypeStruct(w.shape, F32)] * 4
    out_shape.append(jax.ShapeDtypeStruct((8, TILE), F32))
    res = pl.pallas_call(
        body, name="adam_small", in_specs=[vmem] * len(args), out_specs=[vmem] * len(out_shape),
        out_shape=out_shape, compiler_params=pltpu.CompilerParams(vmem_limit_bytes=VMEM_LIMIT_BYTES),
    )(*args)
    return {name: res[4 * i:4 * i + 4] for i, name in enumerate(names)}, res[4 * n]


def _pair_sums(grads, got, pos, tag):
    return [_pair_sum(a, b, pos, name=f"pair_sum_{tag}{i}") for i, (a, b) in enumerate(zip(grads, got))]


def _chip_sums(sums, landed, pos, tag):
    return [_chip_sum(s, r, pos, name=f"chip_sum_{tag}{i}") for i, (s, r) in enumerate(zip(sums, landed))]


def kernel(x, mix_norm_g, mlp_norm_g, ab_w_in, pool_w, pool_scale, conv_w, conv_b, ab_w_out, cd_w_in, sgu_norm_g, sgu_norm_b, sgu_w, sgu_b, cd_w_out, mlp_w1, mlp_w2, final_norm_g, loss_target, m_mix_norm_g, m_mlp_norm_g, m_ab_w_in, m_pool_w, m_pool_scale, m_conv_w, m_conv_b, m_ab_w_out, m_cd_w_in, m_sgu_norm_g, m_sgu_norm_b, m_sgu_w, m_sgu_b, m_cd_w_out, m_mlp_w1, m_mlp_w2, m_final_norm_g, v_mix_norm_g, v_mlp_norm_g, v_ab_w_in, v_pool_w, v_pool_scale, v_conv_w, v_conv_b, v_ab_w_out, v_cd_w_in, v_sgu_norm_g, v_sgu_norm_b, v_sgu_w, v_sgu_b, v_cd_w_out, v_mlp_w1, v_mlp_w2, v_final_norm_g):
    nseq, t_len, d = x.shape
    m_tok = nseq * t_len
    h0 = x.reshape(m_tok, d)
    target = loss_target.reshape(m_tok, d)

    x_idx, y_idx = lax.axis_index("x"), lax.axis_index("y")
    q_idx = 2 * x_idx + y_idx
    pos = jnp.stack([q_idx, lax.axis_index("c")]).astype(jnp.int32)

    def shard_buffer(w, layer, tag):
        return _cast_place(w, layer, pos, name=f"cast_place_{tag}")

    def row_block(w):
        return w.reshape(1, 1, -1, w.shape[-1])

    (buf_ab_out, buf_w1_0, buf_w2_0, buf_cd_in, *later_weights), ((w_ab_in,),) = _cast_place_all(
        [(ab_w_out, 0), (mlp_w1, 0), (mlp_w2, 0), (cd_w_in, 0), (cd_w_out, 0), (mlp_w1, 1), (mlp_w2, 1)], pos,
        name="cast_place_rest", rider=[("gather", [shard_buffer(ab_w_in, 0, "ab_in")])])

    pool_w3, pool_scale3 = pool_w[0], pool_scale[0].reshape(4, 1, TILE)
    sgu_w3 = sgu_w[0]
    sgu_w3_t = jnp.swapaxes(sgu_w3, 1, 2)
    sgu_bias_tile = jnp.broadcast_to(sgu_b[0][:, :, None], (4, TILE, TILE))
    conv_b2 = conv_b

    def place_quarter(v):
        return lax.dynamic_update_slice(jnp.zeros((v.shape[0], 4 * TILE), F32), v, (0, q_idx * TILE))

    sharded_small = jnp.concatenate(
        [place_quarter(conv_w[0]), place_quarter(sgu_norm_g), place_quarter(sgu_norm_b),
         jnp.zeros((3, 4 * TILE), F32)], axis=0)
    sharded_small, = _allreduce_small([sharded_small])
    sharded_small = sharded_small * 0.5
    conv_w_full = sharded_small[0:3]
    sgu_g_full = sharded_small[3:4]
    sgu_b_full = sharded_small[4:5]

    xn0 = _rms_fwd(h0, mix_norm_g[0:1], name="rms_fwd_mix0")
    p_ab, ((w_1_0,),) = _mm_nn(xn0, w_ab_in, 0, out_dtype=BF16, name="ab_in_proj",
                               rider=[("gather", [buf_w1_0])])
    mix0, ((w_ab_out,),) = _ab_fwd(p_ab, pool_w3, pool_scale3, conv_w_full, conv_b2, nseq, t_len,
                                   rider=[("gather", [buf_ab_out])])
    w_ab_out = row_block(w_ab_out)
    h1, hn0 = _mm_nn(mix0, w_ab_out, 0, out_dtype=F32, name="ab_out_proj", epilogue="residual", extra=h0,
                     norm_g=mlp_norm_g[0:1])
    (act0, relu0), ((w_2_0,),) = _mm_nn(hn0, w_1_0, 0, out_dtype=BF16, name="mlp0_up", epilogue="relu2",
                                        rider=[("gather", [buf_w2_0])])
    w_2_0 = row_block(w_2_0)
    (h2, xn1), ((w_cd_in,),) = _mm_nn(act0, w_2_0, 0, out_dtype=F32, name="mlp0_down", epilogue="residual", extra=h1,
                                      norm_g=mix_norm_g[1:2],
                                      rider=[("gather", [buf_cd_in])])

    p_cd = _mm_nn(xn1, w_cd_in, 0, out_dtype=BF16, name="cd_in_proj")
    c_out = _sgu_fwd(p_cd, sgu_g_full, sgu_b_full, sgu_w3, sgu_bias_tile)
    d_out, ltot, (w_cd_out, w_1_1, w_2_1) = _sb_fwd(p_cd, nseq, t_len, later_weights)
    w_cd_out, w_2_1 = row_block(w_cd_out), row_block(w_2_1)
    mix1 = jnp.concatenate([c_out, d_out], axis=1)
    h3, hn1 = _mm_nn(mix1, w_cd_out, 0, out_dtype=F32, name="cd_out_proj", epilogue="residual", extra=h2,
                     norm_g=mlp_norm_g[1:2])
    act1, relu1 = _mm_nn(hn1, w_1_1, 0, out_dtype=BF16, name="mlp1_up", epilogue="relu2")

    dh4, dh4_bf, dg_final, loss_tile = _mlp_down_loss(act1, w_2_1, h3, final_norm_g.reshape(1, d), target)

    def as_pieces(g):
        return g.reshape(1, N_CHIP, -1, g.shape[-1]) if g.shape[1] == 1 else g

    dz1 = _mm_nt(dh4_bf, w_2_1, 0, out_dtype=BF16, name="mlp1_down_bwd", epilogue="relu2_bwd", extra=relu1)
    g_w2_1 = as_pieces(_mm_tn(act1, dh4_bf, 1, name="mlp1_down_wgrad"))
    g_w1_1 = _mm_tn(hn1, dz1, N_CHIP, name="mlp1_up_wgrad")
    (dh3, dh3_bf, dg_mlp1), (got_a,) = _mm_nt(
        dz1, w_1_1, 0, out_dtype=F32, name="mlp1_up_bwd", epilogue="rms_bwd",
        extra=(h3, mlp_norm_g[1:2], dh4), rider=[("swap", [g_w1_1, g_w2_1])])

    g_cd_out = as_pieces(_mm_tn(mix1, dh3_bf, 1, name="cd_out_wgrad"))
    dmix1, (got_cd_out,) = _mm_nt(dh3_bf, w_cd_out, 0, out_dtype=BF16, name="cd_out_bwd",
                                  rider=[("swap", [g_cd_out])])
    sums_a = _pair_sums([g_w1_1, g_w2_1, g_cd_out], got_a + got_cd_out, pos, "a")
    du, dv, dsgu_w, dsgu_bs, dsgu_g, dsgu_b = _sgu_bwd(p_cd, dmix1, sgu_g_full, sgu_b_full, sgu_w3, sgu_w3_t,
                                                      sgu_bias_tile)
    dq, dk, dvv, landed_a = _sb_bwd(p_cd, dmix1, ltot, nseq, t_len, sums_a)
    halves_a = _chip_sums(sums_a, landed_a, pos, "a")
    dp_cd = jnp.concatenate([du, dv, dq, dk, dvv], axis=1)
    g_cd_in, ((r_w1_1, r_w2_1, r_cd_out),) = _mm_tn(xn1, dp_cd, N_CHIP, name="cd_in_wgrad",
                                                    rider=[("join", halves_a)])
    (dh2, dh2_bf, dg_mix1), (got_c,) = _mm_nt(
        dp_cd, w_cd_in, 0, out_dtype=F32, name="cd_in_bwd", epilogue="rms_bwd",
        extra=(h2, mix_norm_g[1:2], dh3), rider=[("swap", [g_cd_in])])

    sums_c = _pair_sums([g_cd_in], got_c, pos, "c")
    dz0, (landed_c,) = _mm_nt(dh2_bf, w_2_0, 0, out_dtype=BF16, name="mlp0_down_bwd", epilogue="relu2_bwd",
                              extra=relu0, rider=[("exchange", sums_c)])
    halves_c = _chip_sums(sums_c, landed_c, pos, "c")
    g_w2_0, ((r_cd_in,),) = _mm_tn(act0, dh2_bf, 1, name="mlp0_down_wgrad", rider=[("join", halves_c)])
    g_w2_0 = as_pieces(g_w2_0)
    g_w1_0, (got_d,) = _mm_tn(hn0, dz0, N_CHIP, name="mlp0_up_wgrad", rider=[("swap", [g_w2_0])])
    sums_d = _pair_sums([g_w2_0], got_d, pos, "d")
    (dh1, dh1_bf, dg_mlp0), (landed_d, got_e) = _mm_nt(
        dz0, w_1_0, 0, out_dtype=F32, name="mlp0_up_bwd", epilogue="rms_bwd",
        extra=(h1, mlp_norm_g[0:1], dh2), rider=[("exchange", sums_d), ("swap", [g_w1_0])])
    halves_d = _chip_sums(sums_d, landed_d, pos, "d")
    sums_e = _pair_sums([g_w1_0], got_e, pos, "e")

    dmix0, ((r_w2_0,),) = _mm_nt(dh1_bf, w_ab_out, 0, out_dtype=BF16, name="ab_out_bwd", rider=[("join", halves_d)])
    g_ab_out = as_pieces(_mm_tn(mix0, dh1_bf, 1, name="ab_out_wgrad"))
    (da, dxb, dgb, dgc, dpool_w, dpool_scale, dconv_w, dconv_b), (landed_e, got_f) = _ab_bwd(
        p_ab, dmix0, pool_w3, pool_scale3, conv_w_full, conv_b2, nseq, t_len,
        rider=[("exchange", sums_e), ("swap", [g_ab_out])])
    halves_e = _chip_sums(sums_e, landed_e, pos, "e")
    sums_f = _pair_sums([g_ab_out], got_f, pos, "f")
    dp_ab = jnp.concatenate([da, dxb, dgb, dgc], axis=1)
    wide = jnp.concatenate([_pad_rows(jnp.concatenate([jnp.zeros_like(dg_mix1), dg_mix1], axis=0)),
                            _pad_rows(jnp.concatenate([dg_mlp0, dg_mlp1], axis=0)), _pad_rows(dg_final)], axis=0)
    mid = jnp.concatenate([_pad_rows(dconv_b), _pad_rows(dconv_w), _pad_rows(dsgu_g), _pad_rows(dsgu_b)], axis=0)
    narrow = jnp.concatenate(
        [dpool_w.reshape(4 * TILE, TILE), dsgu_w.reshape(4 * TILE, TILE), _pad_rows(dpool_scale.reshape(4, TILE)),
         _pad_rows(dsgu_bs[:, :, 0]), loss_tile], axis=0)
    g_ab_in, (landed_f, (r_w1_0,), (wide, mid, narrow)) = _mm_tn(
        xn0, dp_ab, N_CHIP, name="ab_in_wgrad",
        rider=[("exchange", sums_f), ("join", halves_e), ("allgather", [wide, mid, narrow])])
    halves_f = _chip_sums(sums_f, landed_f, pos, "f")
    sums_g = _pair_sums([g_ab_in], _swap_halves([g_ab_in], name="swap_halves_g"), pos, "g")
    (grad_x, _, dg_mix0), (landed_g, (r_ab_out,)) = _mm_nt(
        dp_ab, w_ab_in, 0, out_dtype=F32, name="ab_in_bwd", epilogue="rms_bwd",
        extra=(h0, mix_norm_g[0:1], dh1), rider=[("exchange", sums_g), ("join", halves_f)])
    r_ab_in, = _join_halves(_chip_sums(sums_g, landed_g, pos, "g"), name="join_halves_g")

    big_out = {
        "ab_w_in": _adam_big(ab_w_in, m_ab_w_in, v_ab_w_in, [r_ab_in], name="adam_ab_w_in"),
        "ab_w_out": _adam_big(ab_w_out, m_ab_w_out, v_ab_w_out, [r_ab_out], name="adam_ab_w_out"),
        "cd_w_in": _adam_big(cd_w_in, m_cd_w_in, v_cd_w_in, [r_cd_in], name="adam_cd_w_in"),
        "cd_w_out": _adam_big(cd_w_out, m_cd_w_out, v_cd_w_out, [r_cd_out], name="adam_cd_w_out"),
        "mlp_w1": _adam_big(mlp_w1, m_mlp_w1, v_mlp_w1, [r_w1_0, r_w1_1], name="adam_mlp_w1"),
        "mlp_w2": _adam_big(mlp_w2, m_mlp_w2, v_mlp_w2, [r_w2_0, r_w2_1], name="adam_mlp_w2"),
    }

    late, = _allreduce_small([_pad_rows(dg_mix0)])
    small_out, loss_sum = _adam_small(wide, mid, narrow, late, {
        "mix_norm_g": (mix_norm_g, m_mix_norm_g, v_mix_norm_g),
        "mlp_norm_g": (mlp_norm_g, m_mlp_norm_g, v_mlp_norm_g),
        "final_norm_g": tuple(a.reshape(1, d) for a in (final_norm_g, m_final_norm_g, v_final_norm_g)),
        "conv_b": (conv_b, m_conv_b, v_conv_b),
        "conv_w": (conv_w, m_conv_w, v_conv_w),
        "sgu_norm_g": (sgu_norm_g, m_sgu_norm_g, v_sgu_norm_g),
        "sgu_norm_b": (sgu_norm_b, m_sgu_norm_b, v_sgu_norm_b),
        "pool_w": (pool_w, m_pool_w, v_pool_w),
        "pool_scale": (pool_scale, m_pool_scale, v_pool_scale),
        "sgu_w": (sgu_w, m_sgu_w, v_sgu_w),
        "sgu_b": (sgu_b, m_sgu_b, v_sgu_b),
    })
    small_out["final_norm_g"] = [a.reshape(d) for a in small_out["final_norm_g"]]

    order = ["mix_norm_g", "mlp_norm_g", "ab_w_in", "pool_w", "pool_scale", "conv_w", "conv_b", "ab_w_out",
             "cd_w_in", "sgu_norm_g", "sgu_norm_b", "sgu_w", "sgu_b", "cd_w_out", "mlp_w1", "mlp_w2",
             "final_norm_g"]
    both = {**big_out, **small_out}
    loss = loss_sum[0, 0]
    outs = [loss, grad_x.reshape(nseq, t_len, d)]
    for kind in range(4):
        outs += [both[name][kind] for name in order]
    return tuple(outs)
```

```python
import math

import jax
import jax.numpy as jnp
from jax import lax
from jax.experimental import pallas as pl
from jax.experimental.pallas import tpu as pltpu

F32 = jnp.float32
BF16 = jnp.bfloat16
MESH = pl.DeviceIdType.MESH

EPS = 1e-6
TILE = 128
N_CHIP = 4
N_DEV = 8
VMEM_LIMIT_BYTES = 56 * 1024 * 1024

ADAM_LR = 0.001
ADAM_B1 = 0.9
ADAM_B2 = 0.999
ADAM_EPS = 1e-08
ADAM_WD = 0.01
ADAM_STEP = 10

NT_DIMS = (((1,), (1,)), ((), ()))
TN_DIMS = (((0,), (0,)), ((), ()))


def _params(sem=None):
    return pltpu.CompilerParams(dimension_semantics=sem, vmem_limit_bytes=VMEM_LIMIT_BYTES)


def _call(body, *, name, grid, in_specs, out_specs, out_shape, scratch_shapes, semantics, args, rider=None,
          prefetch=None):
    npre = 0 if prefetch is None else 1

    def launch(kernel, in_specs, out_specs, out_shape, scratch_shapes, operands, aliases, params):
        if prefetch is None:
            return pl.pallas_call(kernel, name=name, grid=grid, in_specs=in_specs, out_specs=out_specs,
                                  out_shape=out_shape, scratch_shapes=scratch_shapes, input_output_aliases=aliases,
                                  compiler_params=params)(*operands)
        spec = pltpu.PrefetchScalarGridSpec(num_scalar_prefetch=1, grid=grid, in_specs=in_specs, out_specs=out_specs,
                                            scratch_shapes=scratch_shapes)
        return pl.pallas_call(kernel, name=name, grid_spec=spec, out_shape=out_shape,
                              input_output_aliases={k + 1: v for k, v in aliases.items()},
                              compiler_params=params)(prefetch, *operands)

    if not rider:
        res = launch(body, list(in_specs), list(out_specs), list(out_shape), list(scratch_shapes), args, {},
                     _params(semantics))
        return list(res), []
    plans = [_rider_plan(kind, arrays) for kind, arrays in rider]
    arrays = [a for _, group in rider for a in group]
    nr, n_in, n_out, n_scr = len(arrays), len(in_specs), len(out_specs), len(scratch_shapes)
    first_out, first_scr = n_in + nr, n_in + nr + n_out + nr
    last_step = math.prod(grid) - 1

    def riding(*refs):
        pre, refs = refs[:npre], refs[npre:]
        step = 0
        for axis, size in enumerate(grid):
            step = step * size + pl.program_id(axis)
        steps, at, sem_at = [], 0, first_scr + n_scr
        for (kind, group), (_, sems, _) in zip(rider, plans):
            k = len(group)
            steps.append(_rider_steps(kind, refs[n_in + at:n_in + at + k],
                                      refs[first_out + n_out + at:first_out + n_out + at + k],
                                      refs[sem_at:sem_at + len(sems)]))
            at, sem_at = at + k, sem_at + len(sems)
        for send, _, _ in steps:
            pl.when(step == 0)(send)
        for _, forward, _ in steps:
            if forward is not None:
                pl.when(step == last_step)(forward)
        body(*pre, *refs[:n_in], *refs[first_out:first_out + n_out], *refs[first_scr:first_scr + n_scr])
        for _, _, finish in steps:
            pl.when(step == last_step)(finish)

    aliases, at = {}, 0
    for (_, group), (_, _, aliased) in zip(rider, plans):
        if aliased:
            aliases.update({n_in + at + a: n_out + at + a for a in range(len(group))})
        at += len(group)
    res = launch(
        riding, list(in_specs) + [ANY] * nr, list(out_specs) + [ANY] * nr,
        list(out_shape) + [s for shapes, _, _ in plans for s in shapes],
        list(scratch_shapes) + [s for _, sems, _ in plans for s in sems], [*args, *arrays], aliases,
        pltpu.CompilerParams(dimension_semantics=("arbitrary",) * len(grid), vmem_limit_bytes=VMEM_LIMIT_BYTES,
                             has_side_effects=True))
    rode, at = [], n_out
    for _, group in rider:
        rode.append(list(res[at:at + len(group)]))
        at += len(group)
    return list(res[:n_out]), rode


def _rider_plan(kind, arrays):
    n = len(arrays)
    same = [jax.ShapeDtypeStruct(a.shape, a.dtype) for a in arrays]
    pair = [pltpu.SemaphoreType.DMA((n,))] * 2
    if kind == "gather":
        return same, _gather_sems(n), True
    if kind == "exchange":
        return _exchange_shapes(arrays), _exchange_sems(n), False
    if kind == "swap":
        return _swap_shapes(arrays), pair, False
    if kind == "allgather":
        return ([jax.ShapeDtypeStruct((N_DEV,) + a.shape, a.dtype) for a in arrays],
                [pltpu.SemaphoreType.DMA((7 * n,))] * 2 + [pltpu.SemaphoreType.DMA((n,))], False)
    assert kind == "join"
    return same, pair, True


def _rider_steps(kind, ins, outs, sems):
    if kind == "gather":
        return _gather_steps(outs, *sems)
    if kind == "allgather":
        return _allgather_steps(ins, outs, *sems)
    if kind == "exchange":
        send, finish = _exchange_steps(ins, outs, *sems)
    elif kind == "swap":
        send, finish = _swap_steps(ins, outs, *sems)
    else:
        send, finish = _join_steps(outs, *sems)
    return send, None, finish


def _gathers(rider):
    return any(kind in ("gather", "allgather") for kind, _ in rider or ())


def _row_tile(k_dim, roomy=False):
    if k_dim > 1024:
        return 512
    return 2048 if roomy else 1024


def _mm_nn(a, b4, layer, *, out_dtype, name, epilogue=None, extra=None, norm_g=None, rider=None):
    m, k_dim = a.shape
    _, s_dim, kb, n = b4.shape
    assert kb == k_dim
    tm = min(m, _row_tile(k_dim, roomy=epilogue != "residual" and norm_g is None and not _gathers(rider)))
    tn = min(n, 1024)
    assert m % tm == 0 and n % tn == 0
    npb = n // tn
    grid = (m // tm, s_dim * npb)
    n_in = 2 + (extra is not None) + (norm_g is not None)
    two_outputs = norm_g is not None or epilogue == "relu2"
    assert norm_g is None or (tn == s_dim * n and epilogue != "relu2")

    def body(*refs):
        a_ref, b_ref = refs[:2]
        e_ref = refs[2] if extra is not None else None
        g_ref = refs[n_in - 1] if norm_g is not None else None
        o_ref = refs[n_in]
        acc = jnp.dot(a_ref[...], b_ref[...], preferred_element_type=F32)
        if epilogue == "relu2":
            r = jnp.maximum(acc, 0.0)
            refs[n_in + 1][...] = r.astype(BF16)
            acc = r * r
        elif epilogue == "residual":
            acc = acc + e_ref[...]
        o_ref[...] = acc.astype(out_dtype)
        if norm_g is not None:
            rstd = lax.rsqrt(jnp.mean(acc * acc, axis=-1, keepdims=True) + EPS)
            refs[n_in + 1][...] = (acc * rstd * g_ref[...]).astype(BF16)

    in_specs = [
        pl.BlockSpec((tm, k_dim), lambda i, j: (i, 0)),
        pl.BlockSpec((None, None, k_dim, tn), lambda i, j: (layer, j // npb, 0, j % npb)),
    ]
    args = [a, b4]
    if extra is not None:
        in_specs.append(pl.BlockSpec((tm, tn), lambda i, j: (i, j)))
        args.append(extra)
    out_block = pl.BlockSpec((tm, tn), lambda i, j: (i, j))
    out_specs, out_shape = [out_block], [jax.ShapeDtypeStruct((m, s_dim * n), out_dtype)]
    if norm_g is not None:
        in_specs.append(pl.BlockSpec((1, tn), lambda i, j: (0, j)))
        args.append(norm_g)
    if two_outputs:
        out_specs.append(out_block)
        out_shape.append(jax.ShapeDtypeStruct((m, s_dim * n), BF16))
    res, rode = _call(
        body, name=name, grid=grid, in_specs=in_specs, out_specs=out_specs, out_shape=out_shape,
        scratch_shapes=[], semantics=("parallel", "parallel"), args=args, rider=rider)
    res = res if two_outputs else res[0]
    return res if rider is None else (res, rode)


def _mm_nt(a, b4, layer, *, out_dtype, name, epilogue=None, extra=None, rider=None):
    m, k_dim = a.shape
    _, s_dim, n_out, n = b4.shape
    assert k_dim == s_dim * n
    rms = epilogue == "rms_bwd"
    roomy = not rms and out_dtype != F32 and not _gathers(rider)
    tm, tn = min(m, _row_tile(k_dim, roomy=roomy)), min(n_out, 1024)
    assert m % tm == 0 and n_out % tn == 0
    grid = (m // tm, n_out // tn)
    assert not rms or tn == n_out
    extras = [] if extra is None else (list(extra) if rms else [extra])
    n_in = 2 + len(extras)

    def body(*refs):
        a_ref, b_ref = refs[:2]
        e_refs = refs[2:n_in]
        o_ref = refs[n_in]
        acc = lax.dot_general(a_ref[:, 0:n], b_ref[0], NT_DIMS, preferred_element_type=F32)
        for s in range(1, s_dim):
            acc = acc + lax.dot_general(a_ref[:, s * n:(s + 1) * n], b_ref[s], NT_DIMS, preferred_element_type=F32)
        if epilogue == "relu2_bwd":
            acc = acc * (2.0 * e_refs[0][...].astype(F32))
        if not rms:
            o_ref[...] = acc.astype(out_dtype)
        else:
            h_ref, g_ref, dres_ref = e_refs
            dhb_ref, dg_ref = refs[n_in + 1:n_in + 3]
            hv = h_ref[...]
            rstd = lax.rsqrt(jnp.mean(hv * hv, axis=-1, keepdims=True) + EPS)
            xhat = hv * rstd
            dxhat = acc * g_ref[...]
            dh = dres_ref[...] + rstd * (dxhat - xhat * jnp.mean(dxhat * xhat, axis=-1, keepdims=True))
            o_ref[...] = dh
            dhb_ref[...] = dh.astype(BF16)
            dg_part = jnp.sum(acc * xhat, axis=0, keepdims=True)
            first = pl.program_id(0) == 0

            @pl.when(first)
            def _():
                dg_ref[...] = dg_part

            @pl.when(jnp.logical_not(first))
            def _():
                dg_ref[...] += dg_part

    in_specs = [
        pl.BlockSpec((tm, k_dim), lambda i, j: (i, 0)),
        pl.BlockSpec((None, s_dim, tn, n), lambda i, j: (layer, 0, j, 0)),
    ]
    args = [a, b4] + extras
    block = pl.BlockSpec((tm, tn), lambda i, j: (i, j))
    vec = pl.BlockSpec((1, tn), lambda i, j: (0, j))
    if rms:
        in_specs += [block, vec, block]
        out_specs = [block, block, vec]
        out_shape = [jax.ShapeDtypeStruct((m, n_out), F32), jax.ShapeDtypeStruct((m, n_out), BF16),
                     jax.ShapeDtypeStruct((1, n_out), F32)]
    else:
        in_specs += [block] * len(extras)
        out_specs, out_shape = [block], [jax.ShapeDtypeStruct((m, n_out), out_dtype)]
    res, rode = _call(
        body, name=name, grid=grid, in_specs=in_specs, out_specs=out_specs, out_shape=out_shape,
        scratch_shapes=[], semantics=("arbitrary",) * 2 if rms else ("parallel", "parallel"), args=args, rider=rider)
    res = res if rms else res[0]
    return res if rider is None else (res, rode)


def _mm_tn(a, b, s_dim, *, name, rider=None):
    m, k1 = a.shape
    mb, n_all = b.shape
    assert mb == m and n_all % s_dim == 0
    n = n_all // s_dim
    tn, t1 = min(n, 1024), min(k1, 512 if _gathers(rider) else 1024)
    assert k1 % t1 == 0 and n % tn == 0
    npb = n // tn
    grid = (k1 // t1, s_dim * npb)

    def body(a_ref, b_ref, o_ref):
        o_ref[...] = lax.dot_general(a_ref[...], b_ref[...], TN_DIMS, preferred_element_type=F32).astype(BF16)

    res, rode = _call(
        body, name=name, grid=grid,
        in_specs=[pl.BlockSpec((m, t1), lambda i, j: (0, i)), pl.BlockSpec((m, tn), lambda i, j: (0, j))],
        out_specs=[pl.BlockSpec((None, None, t1, tn), lambda i, j: (0, j // npb, i, j % npb))],
        out_shape=[jax.ShapeDtypeStruct((1, s_dim, k1, n), BF16)],
        scratch_shapes=[], semantics=("parallel", "parallel"), args=[a, b], rider=rider)
    return res[0] if rider is None else (res[0], rode)


ROW_TILE = 512


def _rms_fwd(h, g, *, name, rider=None):
    m, d = h.shape

    def body(h_ref, g_ref, o_ref):
        hv = h_ref[...]
        rstd = lax.rsqrt(jnp.mean(hv * hv, axis=-1, keepdims=True) + EPS)
        o_ref[...] = (hv * rstd * g_ref[...]).astype(BF16)

    res, rode = _call(
        body, name=name, grid=(m // ROW_TILE,),
        in_specs=[pl.BlockSpec((ROW_TILE, d), lambda i: (i, 0)), pl.BlockSpec((1, d), lambda i: (0, 0))],
        out_specs=[pl.BlockSpec((ROW_TILE, d), lambda i: (i, 0))], out_shape=[jax.ShapeDtypeStruct((m, d), BF16)],
        scratch_shapes=[], semantics=("parallel",), args=[h, g], rider=rider)
    return res[0] if rider is None else (res[0], rode)


def _mlp_down_loss(act, w_2, h_res, g, target):
    m, k_dim = act.shape
    d = w_2.shape[-1]
    tm = _row_tile(k_dim)

    def body(a_ref, b_ref, r_ref, g_ref, t_ref, dh_ref, dhb_ref, dg_ref, loss_ref):
        hv = jnp.dot(a_ref[...], b_ref[...], preferred_element_type=F32) + r_ref[...]
        gv = g_ref[...]
        rstd = lax.rsqrt(jnp.mean(hv * hv, axis=-1, keepdims=True) + EPS)
        xhat = hv * rstd
        err = xhat * gv - t_ref[...]
        dy = err * (1.0 / d)
        dxhat = dy * gv
        dh = rstd * (dxhat - xhat * jnp.mean(dxhat * xhat, axis=-1, keepdims=True))
        dh_ref[...] = dh
        dhb_ref[...] = dh.astype(BF16)
        dg_part = jnp.sum(dy * xhat, axis=0, keepdims=True)
        sq = jnp.sum(jnp.sum(err * err, axis=1, keepdims=True), axis=0, keepdims=True) * (0.5 / d)
        loss_part = jnp.broadcast_to(sq, (8, TILE))

        @pl.when(pl.program_id(0) == 0)
        def _():
            dg_ref[...] = dg_part
            loss_ref[...] = loss_part

        @pl.when(pl.program_id(0) > 0)
        def _():
            dg_ref[...] += dg_part
            loss_ref[...] += loss_part

    row = pl.BlockSpec((tm, d), lambda i: (i, 0))
    vec = pl.BlockSpec((1, d), lambda i: (0, 0))
    return pl.pallas_call(
        body, name="mlp1_down_loss", grid=(m // tm,),
        in_specs=[pl.BlockSpec((tm, k_dim), lambda i: (i, 0)),
                  pl.BlockSpec((None, None, k_dim, d), lambda i: (0, 0, 0, 0)), row, vec, row],
        out_specs=[row, row, vec, pl.BlockSpec((8, TILE), lambda i: (0, 0))],
        out_shape=[jax.ShapeDtypeStruct((m, d), F32), jax.ShapeDtypeStruct((m, d), BF16),
                   jax.ShapeDtypeStruct((1, d), F32), jax.ShapeDtypeStruct((8, TILE), F32)],
        compiler_params=_params(("arbitrary",)),
    )(act, w_2, h_res, g, target)


def _shift_down(x, s, t_idx):
    return jnp.where(t_idx >= s, pltpu.roll(x, s, 0), 0.0)


def _shift_up(x, s, t_idx, t_len):
    return jnp.where(t_idx < t_len - s, pltpu.roll(x, t_len - s, 0), 0.0)


def _pool_select(group, s2, s4, s8, s16):
    return jnp.where(group == 0, s2, jnp.where(group == 1, s4, jnp.where(group == 2, s8, s16)))


def _pool_count(group, t_idx):
    win = jnp.left_shift(2, group)
    return jnp.minimum(t_idx + 1, win).astype(F32)


def _pool_fwd_math(a, group, t_idx):
    s2 = a + _shift_down(a, 1, t_idx)
    s4 = s2 + _shift_down(s2, 2, t_idx)
    s8 = s4 + _shift_down(s4, 4, t_idx)
    s16 = s8 + _shift_down(s8, 8, t_idx)
    return _pool_select(group, s2, s4, s8, s16) / _pool_count(group, t_idx) - a


def _pool_bwd_math(dpooled, group, t_idx, t_len):
    e = dpooled / _pool_count(group, t_idx)
    s2 = e + _shift_up(e, 1, t_idx, t_len)
    s4 = s2 + _shift_up(s2, 2, t_idx, t_len)
    s8 = s4 + _shift_up(s4, 4, t_idx, t_len)
    s16 = s8 + _shift_up(s8, 8, t_idx, t_len)
    return _pool_select(group, s2, s4, s8, s16) - dpooled


def _conv_fwd_math(c, w_ref, b_ref, t_idx):
    return (w_ref[0:1, :] * _shift_down(c, 2, t_idx) + w_ref[1:2, :] * _shift_down(c, 1, t_idx)
            + w_ref[2:3, :] * c + b_ref[...])


def _ab_fwd(p, pool_w, pool_scale, conv_w, conv_b, nseq, t_len, rider=None):
    m = p.shape[0]
    ng = 4

    def body(a_ref, xb_ref, gb_ref, gc_ref, pw_ref, ps_ref, cw_ref, cb_ref, o_ref):
        j = pl.program_id(1)
        t_idx = lax.broadcasted_iota(jnp.int32, (t_len, TILE), 0)

        @pl.when(j < ng)
        def _():
            pooled = _pool_fwd_math(a_ref[...].astype(F32), j, t_idx)
            mixed = jnp.dot(pooled.astype(BF16), pw_ref[...].astype(BF16), preferred_element_type=F32)
            o_ref[...] = (mixed * ps_ref[...]).astype(BF16)

        @pl.when(j >= ng)
        def _():
            c = gc_ref[...].astype(F32) * xb_ref[...].astype(F32)
            y = _conv_fwd_math(c, cw_ref, cb_ref, t_idx)
            o_ref[...] = (gb_ref[...].astype(F32) * y).astype(BF16)

    def pool_j(j):
        return jnp.minimum(j, ng - 1)

    def conv_j(j):
        return jnp.maximum(j - ng, 0)

    in_specs = [
        pl.BlockSpec((t_len, TILE), lambda s, j: (s, pool_j(j))),
        pl.BlockSpec((t_len, TILE), lambda s, j: (s, ng + conv_j(j))),
        pl.BlockSpec((t_len, TILE), lambda s, j: (s, 2 * ng + conv_j(j))),
        pl.BlockSpec((t_len, TILE), lambda s, j: (s, 3 * ng + conv_j(j))),
        pl.BlockSpec((None, TILE, TILE), lambda s, j: (pool_j(j), 0, 0)),
        pl.BlockSpec((None, 1, TILE), lambda s, j: (pool_j(j), 0, 0)),
        pl.BlockSpec((3, TILE), lambda s, j: (0, conv_j(j))),
        pl.BlockSpec((1, TILE), lambda s, j: (0, conv_j(j))),
    ]
    res, rode = _call(
        body, name="ab_mixer_fwd", grid=(nseq, 2 * ng), in_specs=in_specs,
        out_specs=[pl.BlockSpec((t_len, TILE), lambda s, j: (s, j))],
        out_shape=[jax.ShapeDtypeStruct((m, 2 * ng * TILE), BF16)], scratch_shapes=[],
        semantics=("parallel", "arbitrary"), args=[p, p, p, p, pool_w, pool_scale, conv_w, conv_b], rider=rider)
    return res[0] if rider is None else (res[0], rode)


def _ab_bwd(p, dmix, pool_w, pool_scale, conv_w, conv_b, nseq, t_len, rider=None):
    m = p.shape[0]
    ng = 4

    def body(a_ref, xb_ref, gb_ref, gc_ref, dma_ref, dmb_ref, pw_ref, ps_ref, cw_ref, cb_ref,
             da_ref, dxb_ref, dgb_ref, dgc_ref, dpw_ref, dps_ref, dcw_ref, dcb_ref):
        j = pl.program_id(0)
        first = pl.program_id(1) == 0
        t_idx = lax.broadcasted_iota(jnp.int32, (t_len, TILE), 0)

        pooled = _pool_fwd_math(a_ref[...].astype(F32), j, t_idx).astype(BF16)
        w_bf = pw_ref[...].astype(BF16)
        mixed = jnp.dot(pooled, w_bf, preferred_element_type=F32)
        dm = dma_ref[...].astype(F32)
        dps = jnp.sum(dm * mixed, axis=0, keepdims=True)
        dmixed = (dm * ps_ref[...]).astype(BF16)
        dpw = lax.dot_general(pooled, dmixed, TN_DIMS, preferred_element_type=F32)
        dpooled = lax.dot_general(dmixed, w_bf, NT_DIMS, preferred_element_type=F32)
        da_ref[...] = _pool_bwd_math(dpooled, j, t_idx, t_len).astype(BF16)

        xb = xb_ref[...].astype(F32)
        gb = gb_ref[...].astype(F32)
        gc = gc_ref[...].astype(F32)
        d = dmb_ref[...].astype(F32)
        c = gc * xb
        c1 = _shift_down(c, 1, t_idx)
        c2 = _shift_down(c, 2, t_idx)
        y = cw_ref[0:1, :] * c2 + cw_ref[1:2, :] * c1 + cw_ref[2:3, :] * c + cb_ref[...]
        dgb_ref[...] = (d * y).astype(BF16)
        dy = d * gb
        dc = (cw_ref[2:3, :] * dy + cw_ref[1:2, :] * _shift_up(dy, 1, t_idx, t_len)
              + cw_ref[0:1, :] * _shift_up(dy, 2, t_idx, t_len))
        dgc_ref[...] = (dc * xb).astype(BF16)
        dxb_ref[...] = (dc * gc).astype(BF16)
        dcw = jnp.concatenate([jnp.sum(dy * c2, axis=0, keepdims=True),
                               jnp.sum(dy * c1, axis=0, keepdims=True),
                               jnp.sum(dy * c, axis=0, keepdims=True)], axis=0)
        dcb = jnp.sum(dy, axis=0, keepdims=True)

        @pl.when(first)
        def _():
            dpw_ref[...] = dpw
            dps_ref[...] = dps
            dcw_ref[...] = dcw
            dcb_ref[...] = dcb

        @pl.when(jnp.logical_not(first))
        def _():
            dpw_ref[...] += dpw
            dps_ref[...] += dps
            dcw_ref[...] += dcw
            dcb_ref[...] += dcb

    def col(k):
        return pl.BlockSpec((t_len, TILE), lambda j, s: (s, k * ng + j))

    in_specs = [
        col(0), col(1), col(2), col(3), col(0), col(1),
        pl.BlockSpec((None, TILE, TILE), lambda j, s: (j, 0, 0)),
        pl.BlockSpec((None, 1, TILE), lambda j, s: (j, 0, 0)),
        pl.BlockSpec((3, TILE), lambda j, s: (0, j)),
        pl.BlockSpec((1, TILE), lambda j, s: (0, j)),
    ]
    piece = pl.BlockSpec((t_len, TILE), lambda j, s: (s, j))
    out_specs = [
        piece, piece, piece, piece,
        pl.BlockSpec((None, TILE, TILE), lambda j, s: (j, 0, 0)),
        pl.BlockSpec((None, 1, TILE), lambda j, s: (j, 0, 0)),
        pl.BlockSpec((3, TILE), lambda j, s: (0, j)),
        pl.BlockSpec((1, TILE), lambda j, s: (0, j)),
    ]
    w = ng * TILE
    out_shape = [jax.ShapeDtypeStruct((m, w), BF16)] * 4 + [
        jax.ShapeDtypeStruct((ng, TILE, TILE), F32), jax.ShapeDtypeStruct((ng, 1, TILE), F32),
        jax.ShapeDtypeStruct((3, w), F32), jax.ShapeDtypeStruct((1, w), F32)]
    res, rode = _call(
        body, name="ab_mixer_bwd", grid=(ng, nseq), in_specs=in_specs, out_specs=out_specs, out_shape=out_shape,
        scratch_shapes=[], semantics=("parallel", "arbitrary"),
        args=[p, p, p, p, dmix, dmix, pool_w, pool_scale, conv_w, conv_b], rider=rider)
    return res if rider is None else (res, rode)


SGU_ROWS = 512
INV_SQRT2 = 1.0 / math.sqrt(2.0)
INV_SQRT_2PI = 1.0 / math.sqrt(2.0 * math.pi)


def _gelu(x):
    return 0.5 * x * (1.0 + lax.erf(x * INV_SQRT2))


def _gelu_grad(x):
    return 0.5 * (1.0 + lax.erf(x * INV_SQRT2)) + x * (INV_SQRT_2PI * jnp.exp(-0.5 * x * x))


def _causal_tile(transposed=False):
    r = lax.broadcasted_iota(jnp.int32, (TILE, TILE), 0)
    c = lax.broadcasted_iota(jnp.int32, (TILE, TILE), 1)
    return r <= c if transposed else c <= r


def _sgu_norm(v, g_ref, b_ref):
    mu = jnp.mean(v, axis=-1, keepdims=True)
    xc = v - mu
    rstd = lax.rsqrt(jnp.mean(xc * xc, axis=-1, keepdims=True) + EPS)
    xhat = xc * rstd
    return xhat, rstd, xhat * g_ref[...] + b_ref[...]


def _sgu_fwd(p, norm_g, norm_b, w_s, bias_tile):
    m = p.shape[0]
    ng = 4
    width = ng * TILE

    def body(u_ref, v_ref, g_ref, b_ref, w_ref, bias_ref, o_ref):
        u = _gelu(u_ref[...].astype(F32))
        _, _, vln = _sgu_norm(_gelu(v_ref[...].astype(F32)), g_ref, b_ref)
        vln = vln.astype(BF16)
        causal = _causal_tile()
        for g in range(ng):
            cols = slice(g * TILE, (g + 1) * TILE)
            wg = jnp.where(causal, w_ref[g], 0.0).astype(BF16)
            for n in range(SGU_ROWS // TILE):
                rows = slice(n * TILE, (n + 1) * TILE)
                s = jnp.dot(wg, vln[rows, cols], preferred_element_type=F32) + bias_ref[g]
                o_ref[rows, cols] = (u[rows, cols] * s).astype(BF16)

    vec = pl.BlockSpec((1, width), lambda i: (0, 0))
    tiles = pl.BlockSpec((ng, TILE, TILE), lambda i: (0, 0, 0))
    return pl.pallas_call(
        body, name="sgu_fwd", grid=(m // SGU_ROWS,),
        in_specs=[pl.BlockSpec((SGU_ROWS, width), lambda i: (i, 0)),
                  pl.BlockSpec((SGU_ROWS, width), lambda i: (i, 1)), vec, vec, tiles, tiles],
        out_specs=pl.BlockSpec((SGU_ROWS, width), lambda i: (i, 0)),
        out_shape=jax.ShapeDtypeStruct((m, 2 * width), BF16),
        compiler_params=_params(("parallel",)),
    )(p, p, norm_g, norm_b, w_s, bias_tile)


def _sgu_bwd(p, dmix, norm_g, norm_b, w_s, w_s_t, bias_tile):
    m = p.shape[0]
    ng = 4
    width = ng * TILE

    def body(u_ref, v_ref, dc_ref, g_ref, b_ref, w_ref, wt_ref, bias_ref,
             du_ref, dv_ref, dw_ref, dbs_ref, dg_ref, db_ref, ds_scr, dvln_scr):
        u_pre = u_ref[...].astype(F32)
        v_pre = v_ref[...].astype(F32)
        u = _gelu(u_pre)
        xhat, rstd, vln = _sgu_norm(_gelu(v_pre), g_ref, b_ref)
        vln = vln.astype(BF16)
        dc = dc_ref[...].astype(F32)
        causal = _causal_tile()
        ones = jnp.ones((TILE, TILE), BF16)
        first = pl.program_id(0) == 0
        for g in range(ng):
            cols = slice(g * TILE, (g + 1) * TILE)
            wg = jnp.where(causal, w_ref[g], 0.0).astype(BF16)
            wgt = jnp.where(_causal_tile(transposed=True), wt_ref[g], 0.0).astype(BF16)
            dw_acc = jnp.zeros((TILE, TILE), F32)
            dbs_acc = jnp.zeros((TILE, TILE), F32)
            for n in range(SGU_ROWS // TILE):
                rows = slice(n * TILE, (n + 1) * TILE)
                vt = vln[rows, cols]
                s = jnp.dot(wg, vt, preferred_element_type=F32) + bias_ref[g]
                ds_scr[rows, cols] = dc[rows, cols] * s
                ds = (dc[rows, cols] * u[rows, cols]).astype(BF16)
                dw_acc += lax.dot_general(ds, vt, NT_DIMS, preferred_element_type=F32)
                dbs_acc += jnp.dot(ds, ones, preferred_element_type=F32)
                dvln_scr[rows, cols] = jnp.dot(wgt, ds, preferred_element_type=F32)
            dw_g = jnp.where(causal, dw_acc, 0.0)

            @pl.when(first)
            def _():
                dw_ref[g] = dw_g
                dbs_ref[g] = dbs_acc

            @pl.when(jnp.logical_not(first))
            def _():
                dw_ref[g] += dw_g
                dbs_ref[g] += dbs_acc

        du_ref[...] = (ds_scr[...] * _gelu_grad(u_pre)).astype(BF16)
        dvln = dvln_scr[...]
        dxhat = dvln * g_ref[...]
        dv = rstd * (dxhat - jnp.mean(dxhat, axis=-1, keepdims=True)
                     - xhat * jnp.mean(dxhat * xhat, axis=-1, keepdims=True))
        dv_ref[...] = (dv * _gelu_grad(v_pre)).astype(BF16)
        dg_part = jnp.sum(dvln * xhat, axis=0, keepdims=True)
        db_part = jnp.sum(dvln, axis=0, keepdims=True)

        @pl.when(first)
        def _():
            dg_ref[...] = dg_part
            db_ref[...] = db_part

        @pl.when(jnp.logical_not(first))
        def _():
            dg_ref[...] += dg_part
            db_ref[...] += db_part

    vec = pl.BlockSpec((1, width), lambda i: (0, 0))
    tiles = pl.BlockSpec((ng, TILE, TILE), lambda i: (0, 0, 0))
    rows0 = pl.BlockSpec((SGU_ROWS, width), lambda i: (i, 0))
    rows1 = pl.BlockSpec((SGU_ROWS, width), lambda i: (i, 1))
    return pl.pallas_call(
        body, name="sgu_bwd", grid=(m // SGU_ROWS,),
        in_specs=[rows0, rows1, rows0, vec, vec, tiles, tiles, tiles],
        out_specs=[rows0, rows0, tiles, tiles, vec, vec],
        out_shape=[jax.ShapeDtypeStruct((m, width), BF16), jax.ShapeDtypeStruct((m, width), BF16),
                   jax.ShapeDtypeStruct((ng, TILE, TILE), F32), jax.ShapeDtypeStruct((ng, TILE, TILE), F32),
                   jax.ShapeDtypeStruct((1, width), F32), jax.ShapeDtypeStruct((1, width), F32)],
        scratch_shapes=[pltpu.VMEM((SGU_ROWS, width), F32), pltpu.VMEM((SGU_ROWS, width), F32)],
        compiler_params=_params(("arbitrary",)),
    )(p, p, dmix, norm_g, norm_b, w_s, w_s_t, bias_tile)


SB_DH = 64
SB_SCALE = 1.0 / math.sqrt(SB_DH)


SB_BLOCK = 256
SB_SUB = SB_BLOCK // TILE
SB_PASS = 4


def _split_passes(i):
    rem = i % SB_PASS
    return i // SB_PASS, rem >= 2, rem % 2 == 1


def _sum_matrix(kind):
    j = lax.broadcasted_iota(jnp.int32, (TILE, 2 * TILE), 0)
    s = lax.broadcasted_iota(jnp.int32, (TILE, 2 * TILE), 1)
    tri = {"after": j > s, "upto": j <= s, "before": j < s}[kind]
    return jnp.where(jnp.logical_or(s >= TILE, tri), 1.0, 0.0).astype(BF16)


def _strict_mask():
    r = lax.broadcasted_iota(jnp.int32, (SB_BLOCK, SB_BLOCK), 0)
    c = lax.broadcasted_iota(jnp.int32, (SB_BLOCK, SB_BLOCK), 1)
    return c < r


def _head_lanes(h):
    lane = lax.broadcasted_iota(jnp.int32, (1, TILE), 1)
    return (lane >= h * SB_DH) & (lane < (h + 1) * SB_DH)


def _softplus(z):
    return jnp.maximum(z, 0.0) + jnp.log(1.0 + jnp.exp(-jnp.abs(z)))


def _sb_fwd(p, mix, nseq, t_len, gather):
    m = p.shape[0]
    npair = 4
    ng = len(gather)
    last_step = nseq * npair - 1

    def body(q_ref, k_ref, v_ref, *rest):
        o_ref, lt_ref = rest[ng + 1:ng + 3]
        kh_ref, vh_ref = rest[2 * ng + 3:2 * ng + 5]
        step = pl.program_id(0) * npair + pl.program_id(1)
        send, forward, finish = _gather_steps(rest[ng + 3:2 * ng + 3], *rest[2 * ng + 5:])
        pl.when(step == 0)(send)
        pl.when(step == (last_step + 1) // 2)(forward)
        for h in range(2):
            keep = _head_lanes(h)
            kh_ref[h] = jnp.where(keep, k_ref[...], 0).astype(BF16)
            vh_ref[h] = jnp.where(keep, v_ref[...], 0).astype(BF16)
        summat = _sum_matrix("after")
        strict = _strict_mask()

        def one_pass(q, row0, nsub, diag, state):
            rows = pl.ds(row0, nsub * TILE)
            z, sp, pieces = [], [], []
            for h in range(2):
                zh = lax.dot_general(q, kh_ref[h, rows, :], NT_DIMS, preferred_element_type=F32)
                sph = _softplus(zh)
                logkeep = jnp.where(strict, -sph, 0.0) if diag else -sph
                z.append(zh)
                sp.append(sph)
                pieces += [logkeep[:, b * TILE:(b + 1) * TILE] for b in range(nsub)]
            sums = jnp.dot(jnp.concatenate(pieces, axis=0).astype(BF16), summat, preferred_element_type=F32)
            out = []
            for h in range(2):
                carry, acc = state[2 * h], state[2 * h + 1]
                after = [None] * nsub
                for b in reversed(range(nsub)):
                    part = sums[(h * nsub + b) * SB_BLOCK:(h * nsub + b + 1) * SB_BLOCK]
                    after[b] = part[:, :TILE] + carry
                    carry = carry + part[:, TILE:]
                w = jnp.exp(z[h] - sp[h] + jnp.concatenate(after, axis=1))
                if diag:
                    w = jnp.where(strict, w, 0.0)
                out += [carry, acc + jnp.dot(w.astype(BF16), vh_ref[h, rows, :], preferred_element_type=F32)]
            return tuple(out)

        def q_block(i, _):
            r0 = pl.multiple_of(i * SB_BLOCK, SB_BLOCK)
            q = q_ref[pl.ds(r0, SB_BLOCK), :] * SB_SCALE
            zero = jnp.zeros((SB_BLOCK, TILE), F32)
            state = one_pass(q, r0, SB_SUB, True, (zero,) * 4)
            full, two, one = _split_passes(i)
            state = lax.fori_loop(
                0, full,
                lambda jj, st: one_pass(q, pl.multiple_of((i - SB_PASS * (jj + 1)) * SB_BLOCK, SB_BLOCK),
                                        SB_PASS * SB_SUB, False, st),
                state)
            state = lax.cond(
                two, lambda st: one_pass(q, pl.multiple_of((i % 2) * SB_BLOCK, SB_BLOCK), 2 * SB_SUB, False, st),
                lambda st: st, state)
            state = lax.cond(one, lambda st: one_pass(q, 0, SB_SUB, False, st), lambda st: st, state)
            o_ref[pl.ds(r0, SB_BLOCK), :] = (state[1] + state[3]).astype(BF16)
            lt_ref[pl.ds(r0, SB_BLOCK), :] = jnp.where(_head_lanes(0), state[0], state[2])
            return 0

        lax.fori_loop(0, t_len // SB_BLOCK, q_block, 0)
        pl.when(step == last_step)(finish)

    def col(k):
        return pl.BlockSpec((t_len, TILE), lambda s, hp: (s, k * npair + hp))

    out = pl.BlockSpec((t_len, TILE), lambda s, hp: (s, hp))
    res = pl.pallas_call(
        body, name="stickbreak_fwd", grid=(nseq, npair), in_specs=[col(2), col(3), col(4)] + [ANY] * (ng + 1),
        out_specs=[pl.BlockSpec((t_len, TILE), lambda s, hp: (s, npair + hp)), out] + [ANY] * ng,
        out_shape=[jax.ShapeDtypeStruct(mix.shape, BF16), jax.ShapeDtypeStruct((m, npair * TILE), F32)]
        + [jax.ShapeDtypeStruct(b.shape, b.dtype) for b in gather],
        input_output_aliases={**{3 + a: 2 + a for a in range(ng)}, 3 + ng: 0},
        scratch_shapes=[pltpu.VMEM((2, t_len, TILE), BF16), pltpu.VMEM((2, t_len, TILE), BF16)] + _gather_sems(ng),
        compiler_params=pltpu.CompilerParams(dimension_semantics=("arbitrary", "arbitrary"),
                                             vmem_limit_bytes=VMEM_LIMIT_BYTES, has_side_effects=True),
    )(p, p, p, *gather, mix)
    return res[0], res[1], res[2:]


def _sb_bwd(p, dmix, ltot, nseq, t_len, exchange):
    m = p.shape[0]
    npair = 4
    ne = len(exchange)
    last_step = nseq * npair - 1

    def body(q_ref, k_ref, v_ref, do_ref, lt_ref, *rest):
        dq_ref, dk_ref, dv_ref = rest[ne:ne + 3]
        kh_ref, vh_ref, dk_acc, dv_acc = rest[2 * ne + 3:2 * ne + 7]
        step = pl.program_id(0) * npair + pl.program_id(1)
        send, finish = _exchange_steps(rest[:ne], rest[ne + 3:2 * ne + 3], *rest[2 * ne + 7:])
        pl.when(step == 0)(send)
        for h in range(2):
            keep = _head_lanes(h)
            kh_ref[h] = jnp.where(keep, k_ref[...], 0).astype(BF16)
            vh_ref[h] = jnp.where(keep, v_ref[...], 0).astype(BF16)
        dk_acc[...] = jnp.zeros_like(dk_acc)
        dv_acc[...] = jnp.zeros_like(dv_acc)
        sum_upto = _sum_matrix("upto")
        sum_before = _sum_matrix("before")
        strict = _strict_mask()
        lane = lax.broadcasted_iota(jnp.int32, (SB_BLOCK, TILE), 1)

        def running(x, matrix, start, nsub):
            pieces = [x[h][:, b * TILE:(b + 1) * TILE] for h in range(2) for b in range(nsub)]
            sums = jnp.dot(jnp.concatenate(pieces, axis=0).astype(BF16), matrix, preferred_element_type=F32)
            wide, ends = [], []
            for h in range(2):
                total, cols = start[h], []
                for b in range(nsub):
                    part = sums[(h * nsub + b) * SB_BLOCK:(h * nsub + b + 1) * SB_BLOCK]
                    cols.append(part[:, :TILE] + total)
                    total = total + part[:, TILE:]
                wide.append(jnp.concatenate(cols, axis=1))
                ends.append(total)
            return wide, ends

        def one_pass(q, do, qh, doh, ltot, row0, nsub, diag, state):
            rows = pl.ds(row0, nsub * TILE)
            z, sp, logkeep = [], [], []
            for h in range(2):
                zh = lax.dot_general(q, kh_ref[h, rows, :], NT_DIMS, preferred_element_type=F32)
                sph = _softplus(zh)
                z.append(zh)
                sp.append(sph)
                logkeep.append(jnp.where(strict, -sph, 0.0) if diag else -sph)
            upto, sum_l = running(logkeep, sum_upto, [state[0], state[3]], nsub)
            w, g = [], []
            for h in range(2):
                wh = jnp.exp(z[h] - sp[h] + (ltot[h] - upto[h]))
                if diag:
                    wh = jnp.where(strict, wh, 0.0)
                w.append(wh)
                g.append(wh * lax.dot_general(do, vh_ref[h, rows, :], NT_DIMS, preferred_element_type=F32))
            g_before, sum_g = running(g, sum_before, [state[1], state[4]], nsub)
            out, dk_new, dv_new = [], 0.0, 0.0
            for h in range(2):
                dz = g[h] - jnp.exp(z[h] - sp[h]) * (g[h] + g_before[h])
                if diag:
                    dz = jnp.where(strict, dz, 0.0)
                dzb = dz.astype(BF16)
                dq = state[3 * h + 2] + jnp.dot(dzb, kh_ref[h, rows, :], preferred_element_type=F32)
                dk_new = dk_new + lax.dot_general(dzb, qh[h], TN_DIMS, preferred_element_type=F32)
                dv_new = dv_new + lax.dot_general(w[h].astype(BF16), doh[h], TN_DIMS, preferred_element_type=F32)
                out += [sum_l[h], sum_g[h], dq]
            dk_acc[rows, :] += dk_new
            dv_acc[rows, :] += dv_new
            return tuple(out)

        def q_block(i, _):
            r0 = pl.multiple_of(i * SB_BLOCK, SB_BLOCK)
            q = q_ref[pl.ds(r0, SB_BLOCK), :] * SB_SCALE
            do = do_ref[pl.ds(r0, SB_BLOCK), :]
            lt = lt_ref[pl.ds(r0, SB_BLOCK), :]
            qh, doh, ltot = [], [], []
            for h in range(2):
                keep = _head_lanes(h)
                qh.append(jnp.where(keep, q, 0).astype(BF16))
                doh.append(jnp.where(keep, do, 0).astype(BF16))
                ltot.append(jnp.sum(jnp.where(lane == h * SB_DH, lt, 0.0), axis=1, keepdims=True))
            zero = jnp.zeros((SB_BLOCK, TILE), F32)
            full, two, one = _split_passes(i)
            state = lax.fori_loop(
                0, full,
                lambda jj, st: one_pass(q, do, qh, doh, ltot, pl.multiple_of(SB_PASS * jj * SB_BLOCK, SB_BLOCK),
                                        SB_PASS * SB_SUB, False, st),
                (zero,) * 6)
            state = lax.cond(
                two,
                lambda st: one_pass(q, do, qh, doh, ltot, pl.multiple_of(SB_PASS * full * SB_BLOCK, SB_BLOCK),
                                    2 * SB_SUB, False, st),
                lambda st: st, state)
            state = lax.cond(
                one,
                lambda st: one_pass(q, do, qh, doh, ltot, pl.multiple_of((i - 1) * SB_BLOCK, SB_BLOCK), SB_SUB, False, st),
                lambda st: st, state)
            state = one_pass(q, do, qh, doh, ltot, r0, SB_SUB, True, state)
            dq_ref[pl.ds(r0, SB_BLOCK), :] = ((state[2] + state[5]) * SB_SCALE).astype(BF16)
            return 0

        lax.fori_loop(0, t_len // SB_BLOCK, q_block, 0)
        dk_ref[...] = dk_acc[...].astype(BF16)
        dv_ref[...] = dv_acc[...].astype(BF16)
        pl.when(step == last_step)(finish)

    def col(k):
        return pl.BlockSpec((t_len, TILE), lambda s, hp: (s, k * npair + hp))

    out = pl.BlockSpec((t_len, TILE), lambda s, hp: (s, hp))
    width = npair * TILE
    res = pl.pallas_call(
        body, name="stickbreak_bwd", grid=(nseq, npair),
        in_specs=[col(2), col(3), col(4), col(1), out] + [ANY] * ne, out_specs=[out, out, out] + [ANY] * ne,
        out_shape=[jax.ShapeDtypeStruct((m, width), BF16)] * 3 + _exchange_shapes(exchange),
        scratch_shapes=[pltpu.VMEM((2, t_len, TILE), BF16), pltpu.VMEM((2, t_len, TILE), BF16),
                        pltpu.VMEM((t_len, TILE), F32), pltpu.VMEM((t_len, TILE), F32)] + _exchange_sems(ne),
        compiler_params=pltpu.CompilerParams(dimension_semantics=("arbitrary", "arbitrary"),
                                             vmem_limit_bytes=VMEM_LIMIT_BYTES, has_side_effects=True),
    )(p, p, p, dmix, ltot, *exchange)
    return res[0], res[1], res[2], res[3:]


def _adam_math(w, g, m, v):
    m = ADAM_B1 * m + (1.0 - ADAM_B1) * g
    v = ADAM_B2 * v + (1.0 - ADAM_B2) * (g * g)
    m_hat = m / (1.0 - ADAM_B1 ** ADAM_STEP)
    v_hat = v / (1.0 - ADAM_B2 ** ADAM_STEP)
    delta = -ADAM_LR * (m_hat / (jnp.sqrt(v_hat) + ADAM_EPS) + ADAM_WD * w)
    return delta, m, v


def _cast_place(w, layer, pos, *, name):
    _, r, c = w.shape
    tr = min(r, 256)

    def body(pos_ref, w_ref, o_ref):
        o_ref[...] = w_ref[...].astype(BF16)

    grid_spec = pltpu.PrefetchScalarGridSpec(
        num_scalar_prefetch=1, grid=(r // tr,),
        in_specs=[pl.BlockSpec((None, tr, c), lambda i, pos_ref: (layer, i, 0))],
        out_specs=pl.BlockSpec((None, None, tr, c), lambda i, pos_ref: (0, pos_ref[0], i, 0)))
    return pl.pallas_call(
        body, name=name, grid_spec=grid_spec, out_shape=jax.ShapeDtypeStruct((1, N_CHIP, r, c), BF16),
        compiler_params=_params(("parallel",)),
    )(pos, w)


def _cast_place_all(items, pos, *, name, rider=None):
    tiles = [min(w.shape[1], 256) for w, _ in items]
    counts = [w.shape[1] // t for (w, _), t in zip(items, tiles)]
    starts = [sum(counts[:a]) for a in range(len(items))]
    n = len(items)

    def body(pos_ref, *refs):
        i = pl.program_id(0)
        for a in range(n):
            @pl.when((i >= starts[a]) & (i < starts[a] + counts[a]))
            def _():
                refs[n + a][...] = refs[a][...].astype(BF16)

    def block(a):
        return lambda i: jnp.clip(i - starts[a], 0, counts[a] - 1)

    in_specs, out_specs, out_shape = [], [], []
    for a, ((w, layer), t) in enumerate(zip(items, tiles)):
        _, r, c = w.shape
        in_specs.append(pl.BlockSpec((None, t, c), lambda i, pos_ref, a=a, layer=layer: (layer, block(a)(i), 0)))
        out_specs.append(pl.BlockSpec((None, None, t, c), lambda i, pos_ref, a=a: (0, pos_ref[0], block(a)(i), 0)))
        out_shape.append(jax.ShapeDtypeStruct((1, N_CHIP, r, c), BF16))
    res, rode = _call(body, name=name, grid=(sum(counts),), in_specs=in_specs, out_specs=out_specs,
                      out_shape=out_shape, scratch_shapes=[], semantics=("arbitrary",),
                      args=[w for w, _ in items], rider=rider, prefetch=pos)
    return res if rider is None else (res, rode)


def _pair_sum(mine, got, pos, *, name):
    l_dim, s_dim, h, c = got.shape
    th = min(h, 512)
    nt = h // th

    def body(pos_ref, a_ref, b_ref, o_ref):
        o_ref[...] = (a_ref[...].astype(F32) + b_ref[...].astype(F32)).astype(BF16)

    spec = pl.BlockSpec((None, None, th, c), lambda l, s, i, pos_ref: (l, s, i, 0))
    grid_spec = pltpu.PrefetchScalarGridSpec(
        num_scalar_prefetch=1, grid=(l_dim, s_dim, nt),
        in_specs=[pl.BlockSpec((None, None, th, c), lambda l, s, i, pos_ref: (l, s, pos_ref[1] * nt + i, 0)), spec],
        out_specs=spec)
    return pl.pallas_call(
        body, name=name, grid_spec=grid_spec, out_shape=jax.ShapeDtypeStruct(got.shape, BF16),
        compiler_params=_params(("parallel",) * 3),
    )(pos, mine, got)


def _chip_sum(sums, landed, pos, *, name):
    l_dim, _, h, c = sums.shape
    th = min(h, 512)
    nt = h // th

    def body(pos_ref, own, r0, r1, r2, o_ref):
        o_ref[...] = ((own[...].astype(F32) + r0[...].astype(F32)) + r1[...].astype(F32)) + r2[...].astype(F32)

    def piece(k):
        return pl.BlockSpec((None, None, th, c), lambda l, i, pos_ref: (l, k, i, 0))

    grid_spec = pltpu.PrefetchScalarGridSpec(
        num_scalar_prefetch=1, grid=(l_dim, nt),
        in_specs=[pl.BlockSpec((None, None, th, c), lambda l, i, pos_ref: (l, pos_ref[0], i, 0)),
                  piece(0), piece(1), piece(2)],
        out_specs=pl.BlockSpec((None, th, c), lambda l, i, pos_ref: (l, pos_ref[1] * nt + i, 0)))
    return pl.pallas_call(
        body, name=name, grid_spec=grid_spec, out_shape=jax.ShapeDtypeStruct((l_dim, 2 * h, c), F32),
        compiler_params=_params(("parallel",) * 2),
    )(pos, sums, landed, landed, landed)


def _adam_big(w, m, v, grads, *, name):
    l_dim, r, c = w.shape
    assert len(grads) == l_dim
    tr = min(r, 512)

    def body(*refs):
        w_ref, m_ref, v_ref = refs[:3]
        g_refs = refs[3:3 + l_dim]
        go_ref, d_ref, mo_ref, vo_ref = refs[3 + l_dim:]
        g = g_refs[0][...]
        for l in range(1, l_dim):
            g = jnp.where(pl.program_id(0) == l, g_refs[l][...], g)
        delta, m_new, v_new = _adam_math(w_ref[...], g, m_ref[...], v_ref[...])
        go_ref[...] = g
        d_ref[...] = delta
        mo_ref[...] = m_new
        vo_ref[...] = v_new

    spec = pl.BlockSpec((None, tr, c), lambda l, i: (l, i, 0))
    gspec = pl.BlockSpec((None, tr, c), lambda l, i: (0, i, 0))
    return pl.pallas_call(
        body, name=name, grid=(l_dim, r // tr), in_specs=[spec] * 3 + [gspec] * l_dim, out_specs=[spec] * 4,
        out_shape=[jax.ShapeDtypeStruct(w.shape, F32)] * 4, compiler_params=_params(("parallel",) * 2),
    )(w, m, v, *grads)


def _position():
    return lax.axis_index("x"), lax.axis_index("y"), lax.axis_index("c")


def _other_chips(x, y):
    return [(1 - x, y), (x, 1 - y), (1 - x, 1 - y)]


def _remote(src, dst, send_sem, recv_sem, device):
    return pltpu.make_async_remote_copy(src_ref=src, dst_ref=dst, send_sem=send_sem, recv_sem=recv_sem,
                                        device_id=device, device_id_type=MESH)


ANY = pl.BlockSpec(memory_space=pl.ANY)


def _gather_sems(n):
    return [pltpu.SemaphoreType.DMA((3 * n,))] * 4


def _gather_steps(outs, send_sems, recv_sems, fwd_send, fwd_recv):
    n = len(outs)
    x, y, c = _position()
    chips = _other_chips(x, y)
    sibling = (x, y, 1 - c)

    def half(a, chip, core):
        h = outs[a].shape[2] // 2
        return outs[a].at[:, 2 * chip[0] + chip[1], pl.ds(core * h, h), :]

    def over_ici(a, k, chip):
        block = half(a, chip, c)
        return _remote(block, block, send_sems.at[3 * a + k], recv_sems.at[3 * a + k], (*chips[k], c))

    def over_d2d(a, k, core):
        block = half(a, chips[k], core)
        return _remote(block, block, fwd_send.at[3 * a + k], fwd_recv.at[3 * a + k], sibling)

    def send():
        for a in range(n):
            for k in range(3):
                over_ici(a, k, (x, y)).start()

    def forward():
        for k in range(3):
            for a in range(n):
                over_ici(a, k, chips[k]).wait_recv()
                over_d2d(a, k, c).start()

    def finish():
        for k in range(3):
            for a in range(n):
                over_d2d(a, k, 1 - c).wait_recv()
        for a in range(n):
            for k in range(3):
                over_ici(a, k, (x, y)).wait_send()
                over_d2d(a, k, c).wait_send()

    return send, forward, finish


def _swap_halves(grads, *, name):
    n = len(grads)

    def body(*refs):
        send, finish = _swap_steps(refs[:n], refs[n:2 * n], *refs[2 * n:])
        send()
        finish()

    sem = pltpu.SemaphoreType.DMA((n,))
    return pl.pallas_call(
        body, name=name, in_specs=[ANY] * n, out_specs=[ANY] * n, out_shape=_swap_shapes(grads),
        scratch_shapes=[sem, sem], compiler_params=pltpu.CompilerParams(has_side_effects=True),
    )(*grads)


def _swap_shapes(grads):
    return [jax.ShapeDtypeStruct(g.shape[:2] + (g.shape[2] // 2, g.shape[3]), g.dtype) for g in grads]


def _swap_steps(ins, outs, send_sems, recv_sems):
    x, y, c = _position()

    def copy(a):
        h = ins[a].shape[2] // 2
        return _remote(ins[a].at[:, :, pl.ds((1 - c) * h, h), :], outs[a], send_sems.at[a], recv_sems.at[a],
                       (x, y, 1 - c))

    def send():
        for a in range(len(ins)):
            copy(a).start()

    def finish():
        for a in range(len(ins)):
            copy(a).wait()

    return send, finish


def _exchange_shapes(sums):
    return [jax.ShapeDtypeStruct((s.shape[0], 3) + s.shape[2:], s.dtype) for s in sums]


def _exchange_sems(n):
    return [pltpu.SemaphoreType.DMA((3 * n,))] * 2


def _exchange_steps(ins, outs, send_sems, recv_sems):
    n = len(ins)
    x, y, c = _position()
    chips = _other_chips(x, y)

    def copy(a, k):
        chip = chips[k]
        return _remote(ins[a].at[:, 2 * chip[0] + chip[1]], outs[a].at[:, k],
                       send_sems.at[3 * a + k], recv_sems.at[3 * a + k], (*chip, c))

    def send():
        for a in range(n):
            for k in range(3):
                copy(a, k).start()

    def finish():
        for a in range(n):
            for k in range(3):
                copy(a, k).wait()

    return send, finish


def _join_halves(bufs, *, name):
    n = len(bufs)

    def body(*refs):
        send, finish = _join_steps(refs[n:2 * n], *refs[2 * n:])
        send()
        finish()

    sem = pltpu.SemaphoreType.DMA((n,))
    return pl.pallas_call(
        body, name=name, in_specs=[ANY] * n, out_specs=[ANY] * n,
        out_shape=[jax.ShapeDtypeStruct(b.shape, b.dtype) for b in bufs],
        input_output_aliases={a: a for a in range(n)},
        scratch_shapes=[sem, sem], compiler_params=pltpu.CompilerParams(has_side_effects=True),
    )(*bufs)


def _join_steps(outs, send_sems, recv_sems):
    x, y, c = _position()

    def copy(a, core):
        h = outs[a].shape[1] // 2
        half = outs[a].at[:, pl.ds(core * h, h), :]
        return _remote(half, half, send_sems.at[a], recv_sems.at[a], (x, y, 1 - c))

    def send():
        for a in range(len(outs)):
            copy(a, c).start()

    def finish():
        for a in range(len(outs)):
            copy(a, c).wait_send()
            copy(a, 1 - c).wait_recv()

    return send, finish


def _allgather_steps(ins, outs, send_sems, recv_sems, local_sems):
    n = len(ins)
    x, y, c = _position()
    me, sibling = (x, y, c), (x, y, 1 - c)
    chips = _other_chips(x, y)

    def slot(a, dev):
        return outs[a].at[4 * dev[0] + 2 * dev[1] + dev[2]]

    def copy(a, k, block, to, own=False):
        return _remote(ins[a] if own else slot(a, block), slot(a, block),
                       send_sems.at[7 * a + k], recv_sems.at[7 * a + k], to)

    def first(a):
        return [copy(a, 0, me, sibling, own=True)] + [copy(a, 1 + k, me, (*chips[k], c), own=True) for k in range(3)]

    def local(a):
        return pltpu.make_async_copy(ins[a], slot(a, me), local_sems.at[a])

    def send():
        for a in range(n):
            local(a).start()
            for cp in first(a):
                cp.start()

    def forward():
        for a in range(n):
            for k in range(3):
                copy(a, 1 + k, (*chips[k], c), me).wait_recv()
                copy(a, 4 + k, (*chips[k], c), sibling).start()

    def finish():
        for a in range(n):
            copy(a, 0, sibling, me).wait_recv()
            for k in range(3):
                copy(a, 4 + k, (*chips[k], 1 - c), me).wait_recv()
        for a in range(n):
            for cp in first(a) + [copy(a, 4 + k, (*chips[k], c), sibling) for k in range(3)]:
                cp.wait_send()
            local(a).wait()

    return send, forward, finish


def _allreduce_small(packs):
    n = len(packs)

    def body(*refs):
        ins, outs, gath = refs[:n], refs[n:2 * n], refs[2 * n:3 * n]
        send_sems, recv_sems = refs[3 * n:]
        x, y, c = _position()
        me, sibling = (x, y, c), (x, y, 1 - c)
        chips = _other_chips(x, y)

        def slot(a, dev):
            return gath[a].at[4 * dev[0] + 2 * dev[1] + dev[2]]

        def copy(a, k, block, to, src=None):
            return _remote(slot(a, block) if src is None else src, slot(a, block),
                           send_sems.at[7 * a + k], recv_sems.at[7 * a + k], to)

        started = []
        for a in range(n):
            slot(a, me)[...] = ins[a][...]
            first = [copy(a, 0, me, sibling, src=ins[a])]
            first += [copy(a, 1 + k, me, (*chip, c), src=ins[a]) for k, chip in enumerate(chips)]
            for cp in first:
                cp.start()
            started += first
        for a in range(n):
            for k, chip in enumerate(chips):
                copy(a, 1 + k, (*chip, c), me).wait_recv()
                cp = copy(a, 4 + k, (*chip, c), sibling)
                cp.start()
                started.append(cp)
        for a in range(n):
            copy(a, 0, sibling, me).wait_recv()
            for k, chip in enumerate(chips):
                copy(a, 4 + k, (*chip, 1 - c), me).wait_recv()
        for cp in started:
            cp.wait_send()
        for a in range(n):
            total = gath[a][0]
            for d in range(1, N_DEV):
                total = total + gath[a][d]
            outs[a][...] = total

    vmem = pl.BlockSpec(memory_space=pltpu.VMEM)
    sem = pltpu.SemaphoreType.DMA((7 * n,))
    return pl.pallas_call(
        body, name="allreduce_small", in_specs=[vmem] * n, out_specs=[vmem] * n,
        out_shape=[jax.ShapeDtypeStruct(p.shape, p.dtype) for p in packs],
        scratch_shapes=[pltpu.VMEM((N_DEV,) + p.shape, p.dtype) for p in packs] + [sem, sem],
        compiler_params=pltpu.CompilerParams(has_side_effects=True, vmem_limit_bytes=VMEM_LIMIT_BYTES),
    )(*packs)


LOSS_ROW = 1040


def _pad_rows(a, rows=8):
    return jnp.concatenate([a, jnp.zeros((rows - a.shape[0], a.shape[1]), a.dtype)], axis=0)

def _adam_small(wide, mid, narrow, late, params):
    names = ["mix_norm_g", "mlp_norm_g", "final_norm_g", "conv_b", "conv_w", "sgu_norm_g", "sgu_norm_b",
             "pool_w", "pool_scale", "sgu_w", "sgu_b"]
    n = len(names)

    def body(*refs):
        wmv = refs[4:4 + 3 * n]
        outs = refs[4 + 3 * n:]
        x, y, _ = _position()
        q = 2 * x + y

        def total(ref):
            t = ref[0]
            for dev in range(1, N_DEV):
                t = t + ref[dev]
            return t

        wide_sum, mid_sum, narrow_sum = total(refs[0]), total(refs[1]), total(refs[2])
        late_ref = refs[3]

        def my_quarter(rows):
            parts = [rows[:, s * TILE:(s + 1) * TILE] for s in range(N_CHIP)]
            return jnp.where(q == 0, parts[0], jnp.where(q == 1, parts[1], jnp.where(q == 2, parts[2], parts[3])))

        def tiles(first_row):
            return [((0, g), narrow_sum[first_row + g * TILE:first_row + (g + 1) * TILE, :]) for g in range(4)]

        grads = {
            "mix_norm_g": [((), wide_sum[0:2, :] + late_ref[0:2, :])],
            "mlp_norm_g": [((), wide_sum[8:10, :])],
            "final_norm_g": [((), wide_sum[16:17, :])],
            "conv_b": [((), mid_sum[0:1, :])],
            "conv_w": [((0,), my_quarter(mid_sum[8:11, :]))],
            "sgu_norm_g": [((), my_quarter(mid_sum[16:17, :]))],
            "sgu_norm_b": [((), my_quarter(mid_sum[24:25, :]))],
            "pool_w": tiles(0),
            "sgu_w": tiles(512),
            "pool_scale": [((0,), narrow_sum[1024:1028, :])],
            "sgu_b": [((0,), narrow_sum[1032:1036, :])],
        }
        outs[4 * n][...] = narrow_sum[LOSS_ROW:LOSS_ROW + 8, :]
        for i, name in enumerate(names):
            w_ref, m_ref, v_ref = wmv[3 * i:3 * i + 3]
            for lead, g in grads[name]:
                idx = lead + (slice(None), slice(None))
                delta, m_new, v_new = _adam_math(w_ref[idx], g, m_ref[idx], v_ref[idx])
                outs[4 * i][idx] = g
                outs[4 * i + 1][idx] = delta
                outs[4 * i + 2][idx] = m_new
                outs[4 * i + 3][idx] = v_new

    vmem = pl.BlockSpec(memory_space=pltpu.VMEM)
    args, out_shape = [wide, mid, narrow, late], []
    for name in names:
        w, m, v = params[name]
        args += [w, m, v]
        out_shape += [jax.ShapeDtypeStruct(w.shape, F32)] * 4
    out_shape.append(jax.ShapeDtypeStruct((8, TILE), F32))
    res = pl.pallas_call(
        body, name="adam_small", in_specs=[vmem] * len(args), out_specs=[vmem] * len(out_shape),
        out_shape=out_shape, compiler_params=pltpu.CompilerParams(vmem_limit_bytes=VMEM_LIMIT_BYTES),
    )(*args)
    return {name: res[4 * i:4 * i + 4] for i, name in enumerate(names)}, res[4 * n]


def _pair_sums(grads, got, pos, tag):
    return [_pair_sum(a, b, pos, name=f"pair_sum_{tag}{i}") for i, (a, b) in enumerate(zip(grads, got))]


def _chip_sums(sums, landed, pos, tag):
    return [_chip_sum(s, r, pos, name=f"chip_sum_{tag}{i}") for i, (s, r) in enumerate(zip(sums, landed))]


def kernel(x, mix_norm_g, mlp_norm_g, ab_w_in, pool_w, pool_scale, conv_w, conv_b, ab_w_out, cd_w_in, sgu_norm_g, sgu_norm_b, sgu_w, sgu_b, cd_w_out, mlp_w1, mlp_w2, final_norm_g, loss_target, m_mix_norm_g, m_mlp_norm_g, m_ab_w_in, m_pool_w, m_pool_scale, m_conv_w, m_conv_b, m_ab_w_out, m_cd_w_in, m_sgu_norm_g, m_sgu_norm_b, m_sgu_w, m_sgu_b, m_cd_w_out, m_mlp_w1, m_mlp_w2, m_final_norm_g, v_mix_norm_g, v_mlp_norm_g, v_ab_w_in, v_pool_w, v_pool_scale, v_conv_w, v_conv_b, v_ab_w_out, v_cd_w_in, v_sgu_norm_g, v_sgu_norm_b, v_sgu_w, v_sgu_b, v_cd_w_out, v_mlp_w1, v_mlp_w2, v_final_norm_g):
    nseq, t_len, d = x.shape
    m_tok = nseq * t_len
    h0 = x.reshape(m_tok, d)
    target = loss_target.reshape(m_tok, d)

    x_idx, y_idx = lax.axis_index("x"), lax.axis_index("y")
    q_idx = 2 * x_idx + y_idx
    pos = jnp.stack([q_idx, lax.axis_index("c")]).astype(jnp.int32)

    def shard_buffer(w, layer, tag):
        return _cast_place(w, layer, pos, name=f"cast_place_{tag}")

    def row_block(w):
        return w.reshape(1, 1, -1, w.shape[-1])

    (buf_ab_out, buf_w1_0, buf_w2_0, buf_cd_in, *later_weights), ((w_ab_in,),) = _cast_place_all(
        [(ab_w_out, 0), (mlp_w1, 0), (mlp_w2, 0), (cd_w_in, 0), (cd_w_out, 0), (mlp_w1, 1), (mlp_w2, 1)], pos,
        name="cast_place_rest", rider=[("gather", [shard_buffer(ab_w_in, 0, "ab_in")])])

    pool_w3, pool_scale3 = pool_w[0], pool_scale[0].reshape(4, 1, TILE)
    sgu_w3 = sgu_w[0]
    sgu_w3_t = jnp.swapaxes(sgu_w3, 1, 2)
    sgu_bias_tile = jnp.broadcast_to(sgu_b[0][:, :, None], (4, TILE, TILE))
    conv_b2 = conv_b

    def place_quarter(v):
        return lax.dynamic_update_slice(jnp.zeros((v.shape[0], 4 * TILE), F32), v, (0, q_idx * TILE))

    sharded_small = jnp.concatenate(
        [place_quarter(conv_w[0]), place_quarter(sgu_norm_g), place_quarter(sgu_norm_b),
         jnp.zeros((3, 4 * TILE), F32)], axis=0)
    sharded_small, = _allreduce_small([sharded_small])
    sharded_small = sharded_small * 0.5
    conv_w_full = sharded_small[0:3]
    sgu_g_full = sharded_small[3:4]
    sgu_b_full = sharded_small[4:5]

    xn0 = _rms_fwd(h0, mix_norm_g[0:1], name="rms_fwd_mix0")
    p_ab, ((w_1_0,),) = _mm_nn(xn0, w_ab_in, 0, out_dtype=BF16, name="ab_in_proj",
                               rider=[("gather", [buf_w1_0])])
    mix0, ((w_ab_out,),) = _ab_fwd(p_ab, pool_w3, pool_scale3, conv_w_full, conv_b2, nseq, t_len,
                                   rider=[("gather", [buf_ab_out])])
    w_ab_out = row_block(w_ab_out)
    h1, hn0 = _mm_nn(mix0, w_ab_out, 0, out_dtype=F32, name="ab_out_proj", epilogue="residual", extra=h0,
                     norm_g=mlp_norm_g[0:1])
    (act0, relu0), ((w_2_0,),) = _mm_nn(hn0, w_1_0, 0, out_dtype=BF16, name="mlp0_up", epilogue="relu2",
                                        rider=[("gather", [buf_w2_0])])
    w_2_0 = row_block(w_2_0)
    (h2, xn1), ((w_cd_in,),) = _mm_nn(act0, w_2_0, 0, out_dtype=F32, name="mlp0_down", epilogue="residual", extra=h1,
                                      norm_g=mix_norm_g[1:2],
                                      rider=[("gather", [buf_cd_in])])

    p_cd = _mm_nn(xn1, w_cd_in, 0, out_dtype=BF16, name="cd_in_proj")
    mix1 = _sgu_fwd(p_cd, sgu_g_full, sgu_b_full, sgu_w3, sgu_bias_tile)
    mix1, ltot, (w_cd_out, w_1_1, w_2_1) = _sb_fwd(p_cd, mix1, nseq, t_len, later_weights)
    w_cd_out, w_2_1 = row_block(w_cd_out), row_block(w_2_1)
    h3, hn1 = _mm_nn(mix1, w_cd_out, 0, out_dtype=F32, name="cd_out_proj", epilogue="residual", extra=h2,
                     norm_g=mlp_norm_g[1:2])
    act1, relu1 = _mm_nn(hn1, w_1_1, 0, out_dtype=BF16, name="mlp1_up", epilogue="relu2")

    dh4, dh4_bf, dg_final, loss_tile = _mlp_down_loss(act1, w_2_1, h3, final_norm_g.reshape(1, d), target)

    def as_pieces(g):
        return g.reshape(1, N_CHIP, -1, g.shape[-1]) if g.shape[1] == 1 else g

    dz1 = _mm_nt(dh4_bf, w_2_1, 0, out_dtype=BF16, name="mlp1_down_bwd", epilogue="relu2_bwd", extra=relu1)
    g_w2_1 = as_pieces(_mm_tn(act1, dh4_bf, 1, name="mlp1_down_wgrad"))
    g_w1_1 = _mm_tn(hn1, dz1, N_CHIP, name="mlp1_up_wgrad")
    (dh3, dh3_bf, dg_mlp1), (got_a,) = _mm_nt(
        dz1, w_1_1, 0, out_dtype=F32, name="mlp1_up_bwd", epilogue="rms_bwd",
        extra=(h3, mlp_norm_g[1:2], dh4), rider=[("swap", [g_w1_1, g_w2_1])])

    g_cd_out = as_pieces(_mm_tn(mix1, dh3_bf, 1, name="cd_out_wgrad"))
    dmix1, (got_cd_out,) = _mm_nt(dh3_bf, w_cd_out, 0, out_dtype=BF16, name="cd_out_bwd",
                                  rider=[("swap", [g_cd_out])])
    sums_a = _pair_sums([g_w1_1, g_w2_1, g_cd_out], got_a + got_cd_out, pos, "a")
    du, dv, dsgu_w, dsgu_bs, dsgu_g, dsgu_b = _sgu_bwd(p_cd, dmix1, sgu_g_full, sgu_b_full, sgu_w3, sgu_w3_t,
                                                      sgu_bias_tile)
    dq, dk, dvv, landed_a = _sb_bwd(p_cd, dmix1, ltot, nseq, t_len, sums_a)
    halves_a = _chip_sums(sums_a, landed_a, pos, "a")
    dp_cd = jnp.concatenate([du, dv, dq, dk, dvv], axis=1)
    g_cd_in, ((r_w1_1, r_w2_1, r_cd_out),) = _mm_tn(xn1, dp_cd, N_CHIP, name="cd_in_wgrad",
                                                    rider=[("join", halves_a)])
    (dh2, dh2_bf, dg_mix1), (got_c,) = _mm_nt(
        dp_cd, w_cd_in, 0, out_dtype=F32, name="cd_in_bwd", epilogue="rms_bwd",
        extra=(h2, mix_norm_g[1:2], dh3), rider=[("swap", [g_cd_in])])

    sums_c = _pair_sums([g_cd_in], got_c, pos, "c")
    dz0, (landed_c,) = _mm_nt(dh2_bf, w_2_0, 0, out_dtype=BF16, name="mlp0_down_bwd", epilogue="relu2_bwd",
                              extra=relu0, rider=[("exchange", sums_c)])
    halves_c = _chip_sums(sums_c, landed_c, pos, "c")
    g_w2_0, ((r_cd_in,),) = _mm_tn(act0, dh2_bf, 1, name="mlp0_down_wgrad", rider=[("join", halves_c)])
    g_w2_0 = as_pieces(g_w2_0)
    g_w1_0, (got_d,) = _mm_tn(hn0, dz0, N_CHIP, name="mlp0_up_wgrad", rider=[("swap", [g_w2_0])])
    sums_d = _pair_sums([g_w2_0], got_d, pos, "d")
    (dh1, dh1_bf, dg_mlp0), (landed_d, got_e) = _mm_nt(
        dz0, w_1_0, 0, out_dtype=F32, name="mlp0_up_bwd", epilogue="rms_bwd",
        extra=(h1, mlp_norm_g[0:1], dh2), rider=[("exchange", sums_d), ("swap", [g_w1_0])])
    halves_d = _chip_sums(sums_d, landed_d, pos, "d")
    sums_e = _pair_sums([g_w1_0], got_e, pos, "e")

    dmix0, ((r_w2_0,),) = _mm_nt(dh1_bf, w_ab_out, 0, out_dtype=BF16, name="ab_out_bwd", rider=[("join", halves_d)])
    g_ab_out = as_pieces(_mm_tn(mix0, dh1_bf, 1, name="ab_out_wgrad"))
    (da, dxb, dgb, dgc, dpool_w, dpool_scale, dconv_w, dconv_b), (landed_e, got_f) = _ab_bwd(
        p_ab, dmix0, pool_w3, pool_scale3, conv_w_full, conv_b2, nseq, t_len,
        rider=[("exchange", sums_e), ("swap", [g_ab_out])])
    halves_e = _chip_sums(sums_e, landed_e, pos, "e")
    sums_f = _pair_sums([g_ab_out], got_f, pos, "f")
    dp_ab = jnp.concatenate([da, dxb, dgb, dgc], axis=1)
    wide = jnp.concatenate([_pad_rows(jnp.concatenate([jnp.zeros_like(dg_mix1), dg_mix1], axis=0)),
                            _pad_rows(jnp.concatenate([dg_mlp0, dg_mlp1], axis=0)), _pad_rows(dg_final)], axis=0)
    mid = jnp.concatenate([_pad_rows(dconv_b), _pad_rows(dconv_w), _pad_rows(dsgu_g), _pad_rows(dsgu_b)], axis=0)
    narrow = jnp.concatenate(
        [dpool_w.reshape(4 * TILE, TILE), dsgu_w.reshape(4 * TILE, TILE), _pad_rows(dpool_scale.reshape(4, TILE)),
         _pad_rows(dsgu_bs[:, :, 0]), loss_tile], axis=0)
    g_ab_in, (landed_f, (r_w1_0,), (wide, mid, narrow)) = _mm_tn(
        xn0, dp_ab, N_CHIP, name="ab_in_wgrad",
        rider=[("exchange", sums_f), ("join", halves_e), ("allgather", [wide, mid, narrow])])
    halves_f = _chip_sums(sums_f, landed_f, pos, "f")
    sums_g = _pair_sums([g_ab_in], _swap_halves([g_ab_in], name="swap_halves_g"), pos, "g")
    (grad_x, _, dg_mix0), (landed_g, (r_ab_out,)) = _mm_nt(
        dp_ab, w_ab_in, 0, out_dtype=F32, name="ab_in_bwd", epilogue="rms_bwd",
        extra=(h0, mix_norm_g[0:1], dh1), rider=[("exchange", sums_g), ("join", halves_f)])
    r_ab_in, = _join_halves(_chip_sums(sums_g, landed_g, pos, "g"), name="join_halves_g")

    big_out = {
        "ab_w_in": _adam_big(ab_w_in, m_ab_w_in, v_ab_w_in, [r_ab_in], name="adam_ab_w_in"),
        "ab_w_out": _adam_big(ab_w_out, m_ab_w_out, v_ab_w_out, [r_ab_out], name="adam_ab_w_out"),
        "cd_w_in": _adam_big(cd_w_in, m_cd_w_in, v_cd_w_in, [r_cd_in], name="adam_cd_w_in"),
        "cd_w_out": _adam_big(cd_w_out, m_cd_w_out, v_cd_w_out, [r_cd_out], name="adam_cd_w_out"),
        "mlp_w1": _adam_big(mlp_w1, m_mlp_w1, v_mlp_w1, [r_w1_0, r_w1_1], name="adam_mlp_w1"),
        "mlp_w2": _adam_big(mlp_w2, m_mlp_w2, v_mlp_w2, [r_w2_0, r_w2_1], name="adam_mlp_w2"),
    }

    late, = _allreduce_small([_pad_rows(dg_mix0)])
    small_out, loss_sum = _adam_small(wide, mid, narrow, late, {
        "mix_norm_g": (mix_norm_g, m_mix_norm_g, v_mix_norm_g),
        "mlp_norm_g": (mlp_norm_g, m_mlp_norm_g, v_mlp_norm_g),
        "final_norm_g": tuple(a.reshape(1, d) for a in (final_norm_g, m_final_norm_g, v_final_norm_g)),
        "conv_b": (conv_b, m_conv_b, v_conv_b),
        "conv_w": (conv_w, m_conv_w, v_conv_w),
        "sgu_norm_g": (sgu_norm_g, m_sgu_norm_g, v_sgu_norm_g),
        "sgu_norm_b": (sgu_norm_b, m_sgu_norm_b, v_sgu_norm_b),
        "pool_w": (pool_w, m_pool_w, v_pool_w),
        "pool_scale": (pool_scale, m_pool_scale, v_pool_scale),
        "sgu_w": (sgu_w, m_sgu_w, v_sgu_w),
        "sgu_b": (sgu_b, m_sgu_b, v_sgu_b),
    })
    small_out["final_norm_g"] = [a.reshape(d) for a in small_out["final_norm_g"]]

    order = ["mix_norm_g", "mlp_norm_g", "ab_w_in", "pool_w", "pool_scale", "conv_w", "conv_b", "ab_w_out",
             "cd_w_in", "sgu_norm_g", "sgu_norm_b", "sgu_w", "sgu_b", "cd_w_out", "mlp_w1", "mlp_w2",
             "final_norm_g"]
    both = {**big_out, **small_out}
    loss = loss_sum[0, 0]
    outs = [loss, grad_x.reshape(nseq, t_len, d)]
    for kind in range(4):
        outs += [both[name][kind] for name in order]
    return tuple(outs)
```

```python
import math

import jax
import jax.numpy as jnp
from jax import lax
from jax.experimental import pallas as pl
from jax.experimental.pallas import tpu as pltpu

F32 = jnp.float32
BF16 = jnp.bfloat16
MESH = pl.DeviceIdType.MESH

EPS = 1e-6
TILE = 128
N_CHIP = 4
N_DEV = 8
VMEM_LIMIT_BYTES = 56 * 1024 * 1024

ADAM_LR = 0.001
ADAM_B1 = 0.9
ADAM_B2 = 0.999
ADAM_EPS = 1e-08
ADAM_WD = 0.01
ADAM_STEP = 10

NT_DIMS = (((1,), (1,)), ((), ()))
TN_DIMS = (((0,), (0,)), ((), ()))


def _params(sem=None):
    return pltpu.CompilerParams(dimension_semantics=sem, vmem_limit_bytes=VMEM_LIMIT_BYTES)


def _call(body, *, name, grid, in_specs, out_specs, out_shape, scratch_shapes, semantics, args, rider=None,
          prefetch=None):
    npre = 0 if prefetch is None else 1

    def launch(kernel, in_specs, out_specs, out_shape, scratch_shapes, operands, aliases, params):
        if prefetch is None:
            return pl.pallas_call(kernel, name=name, grid=grid, in_specs=in_specs, out_specs=out_specs,
                                  out_shape=out_shape, scratch_shapes=scratch_shapes, input_output_aliases=aliases,
                                  compiler_params=params)(*operands)
        spec = pltpu.PrefetchScalarGridSpec(num_scalar_prefetch=1, grid=grid, in_specs=in_specs, out_specs=out_specs,
                                            scratch_shapes=scratch_shapes)
        return pl.pallas_call(kernel, name=name, grid_spec=spec, out_shape=out_shape,
                              input_output_aliases={k + 1: v for k, v in aliases.items()},
                              compiler_params=params)(prefetch, *operands)

    if not rider:
        res = launch(body, list(in_specs), list(out_specs), list(out_shape), list(scratch_shapes), args, {},
                     _params(semantics))
        return list(res), []
    plans = [_rider_plan(kind, arrays) for kind, arrays in rider]
    arrays = [a for _, group in rider for a in group]
    nr, n_in, n_out, n_scr = len(arrays), len(in_specs), len(out_specs), len(scratch_shapes)
    first_out, first_scr = n_in + nr, n_in + nr + n_out + nr
    last_step = math.prod(grid) - 1

    def riding(*refs):
        pre, refs = refs[:npre], refs[npre:]
        step = 0
        for axis, size in enumerate(grid):
            step = step * size + pl.program_id(axis)
        steps, at, sem_at = [], 0, first_scr + n_scr
        for (kind, group), (_, sems, _) in zip(rider, plans):
            k = len(group)
            steps.append(_rider_steps(kind, refs[n_in + at:n_in + at + k],
                                      refs[first_out + n_out + at:first_out + n_out + at + k],
                                      refs[sem_at:sem_at + len(sems)]))
            at, sem_at = at + k, sem_at + len(sems)
        for send, _, _ in steps:
            pl.when(step == 0)(send)
        for _, forward, _ in steps:
            if forward is not None:
                pl.when(step == last_step)(forward)
        body(*pre, *refs[:n_in], *refs[first_out:first_out + n_out], *refs[first_scr:first_scr + n_scr])
        for _, _, finish in steps:
            pl.when(step == last_step)(finish)

    aliases, at = {}, 0
    for (_, group), (_, _, aliased) in zip(rider, plans):
        if aliased:
            aliases.update({n_in + at + a: n_out + at + a for a in range(len(group))})
        at += len(group)
    res = launch(
        riding, list(in_specs) + [ANY] * nr, list(out_specs) + [ANY] * nr,
        list(out_shape) + [s for shapes, _, _ in plans for s in shapes],
        list(scratch_shapes) + [s for _, sems, _ in plans for s in sems], [*args, *arrays], aliases,
        pltpu.CompilerParams(dimension_semantics=("arbitrary",) * len(grid), vmem_limit_bytes=VMEM_LIMIT_BYTES,
                             has_side_effects=True))
    rode, at = [], n_out
    for _, group in rider:
        rode.append(list(res[at:at + len(group)]))
        at += len(group)
    return list(res[:n_out]), rode


def _rider_plan(kind, arrays):
    n = len(arrays)
    same = [jax.ShapeDtypeStruct(a.shape, a.dtype) for a in arrays]
    pair = [pltpu.SemaphoreType.DMA((n,))] * 2
    if kind == "gather":
        return same, _gather_sems(n), True
    if kind == "exchange":
        return _exchange_shapes(arrays), _exchange_sems(n), False
    if kind == "swap":
        return _swap_shapes(arrays), pair, False
    if kind == "allgather":
        return ([jax.ShapeDtypeStruct((N_DEV,) + a.shape, a.dtype) for a in arrays],
                [pltpu.SemaphoreType.DMA((7 * n,))] * 2 + [pltpu.SemaphoreType.DMA((n,))], False)
    assert kind == "join"
    return same, pair, True


def _rider_steps(kind, ins, outs, sems):
    if kind == "gather":
        return _gather_steps(outs, *sems)
    if kind == "allgather":
        return _allgather_steps(ins, outs, *sems)
    if kind == "exchange":
        send, finish = _exchange_steps(ins, outs, *sems)
    elif kind == "swap":
        send, finish = _swap_steps(ins, outs, *sems)
    else:
        send, finish = _join_steps(outs, *sems)
    return send, None, finish


def _gathers(rider):
    return any(kind in ("gather", "allgather") for kind, _ in rider or ())


def _row_tile(k_dim, roomy=False):
    if k_dim > 1024:
        return 512
    return 2048 if roomy else 1024


def _mm_nn(a, b4, layer, *, out_dtype, name, epilogue=None, extra=None, norm_g=None, rider=None):
    m, k_dim = a.shape
    _, s_dim, kb, n = b4.shape
    assert kb == k_dim
    tm = min(m, _row_tile(k_dim, roomy=epilogue != "residual" and norm_g is None and not _gathers(rider)))
    tn = min(n, 1024)
    assert m % tm == 0 and n % tn == 0
    npb = n // tn
    grid = (m // tm, s_dim * npb)
    n_in = 2 + (extra is not None) + (norm_g is not None)
    two_outputs = norm_g is not None or epilogue == "relu2"
    assert norm_g is None or (tn == s_dim * n and epilogue != "relu2")

    def body(*refs):
        a_ref, b_ref = refs[:2]
        e_ref = refs[2] if extra is not None else None
        g_ref = refs[n_in - 1] if norm_g is not None else None
        o_ref = refs[n_in]
        acc = jnp.dot(a_ref[...], b_ref[...], preferred_element_type=F32)
        if epilogue == "relu2":
            r = jnp.maximum(acc, 0.0)
            refs[n_in + 1][...] = r.astype(BF16)
            acc = r * r
        elif epilogue == "residual":
            acc = acc + e_ref[...]
        o_ref[...] = acc.astype(out_dtype)
        if norm_g is not None:
            rstd = lax.rsqrt(jnp.mean(acc * acc, axis=-1, keepdims=True) + EPS)
            refs[n_in + 1][...] = (acc * rstd * g_ref[...]).astype(BF16)

    in_specs = [
        pl.BlockSpec((tm, k_dim), lambda i, j: (i, 0)),
        pl.BlockSpec((None, None, k_dim, tn), lambda i, j: (layer, j // npb, 0, j % npb)),
    ]
    args = [a, b4]
    if extra is not None:
        in_specs.append(pl.BlockSpec((tm, tn), lambda i, j: (i, j)))
        args.append(extra)
    out_block = pl.BlockSpec((tm, tn), lambda i, j: (i, j))
    out_specs, out_shape = [out_block], [jax.ShapeDtypeStruct((m, s_dim * n), out_dtype)]
    if norm_g is not None:
        in_specs.append(pl.BlockSpec((1, tn), lambda i, j: (0, j)))
        args.append(norm_g)
    if two_outputs:
        out_specs.append(out_block)
        out_shape.append(jax.ShapeDtypeStruct((m, s_dim * n), BF16))
    res, rode = _call(
        body, name=name, grid=grid, in_specs=in_specs, out_specs=out_specs, out_shape=out_shape,
        scratch_shapes=[], semantics=("parallel", "parallel"), args=args, rider=rider)
    res = res if two_outputs else res[0]
    return res if rider is None else (res, rode)


def _mm_nt(a, b4, layer, *, out_dtype, name, epilogue=None, extra=None, rider=None):
    m, k_dim = a.shape
    _, s_dim, n_out, n = b4.shape
    assert k_dim == s_dim * n
    rms = epilogue == "rms_bwd"
    roomy = not rms and out_dtype != F32 and not _gathers(rider)
    tm, tn = min(m, _row_tile(k_dim, roomy=roomy)), min(n_out, 1024)
    assert m % tm == 0 and n_out % tn == 0
    grid = (m // tm, n_out // tn)
    assert not rms or tn == n_out
    extras = [] if extra is None else (list(extra) if rms else [extra])
    n_in = 2 + len(extras)

    def body(*refs):
        a_ref, b_ref = refs[:2]
        e_refs = refs[2:n_in]
        o_ref = refs[n_in]
        acc = lax.dot_general(a_ref[:, 0:n], b_ref[0], NT_DIMS, preferred_element_type=F32)
        for s in range(1, s_dim):
            acc = acc + lax.dot_general(a_ref[:, s * n:(s + 1) * n], b_ref[s], NT_DIMS, preferred_element_type=F32)
        if epilogue == "relu2_bwd":
            acc = acc * (2.0 * e_refs[0][...].astype(F32))
        if not rms:
            o_ref[...] = acc.astype(out_dtype)
        else:
            h_ref, g_ref, dres_ref = e_refs
            dhb_ref, dg_ref = refs[n_in + 1:n_in + 3]
            hv = h_ref[...]
            rstd = lax.rsqrt(jnp.mean(hv * hv, axis=-1, keepdims=True) + EPS)
            xhat = hv * rstd
            dxhat = acc * g_ref[...]
            dh = dres_ref[...] + rstd * (dxhat - xhat * jnp.mean(dxhat * xhat, axis=-1, keepdims=True))
            o_ref[...] = dh
            dhb_ref[...] = dh.astype(BF16)
            dg_part = jnp.sum(acc * xhat, axis=0, keepdims=True)
            first = pl.program_id(0) == 0

            @pl.when(first)
            def _():
                dg_ref[...] = dg_part

            @pl.when(jnp.logical_not(first))
            def _():
                dg_ref[...] += dg_part

    in_specs = [
        pl.BlockSpec((tm, k_dim), lambda i, j: (i, 0)),
        pl.BlockSpec((None, s_dim, tn, n), lambda i, j: (layer, 0, j, 0)),
    ]
    args = [a, b4] + extras
    block = pl.BlockSpec((tm, tn), lambda i, j: (i, j))
    vec = pl.BlockSpec((1, tn), lambda i, j: (0, j))
    if rms:
        in_specs += [block, vec, block]
        out_specs = [block, block, vec]
        out_shape = [jax.ShapeDtypeStruct((m, n_out), F32), jax.ShapeDtypeStruct((m, n_out), BF16),
                     jax.ShapeDtypeStruct((1, n_out), F32)]
    else:
        in_specs += [block] * len(extras)
        out_specs, out_shape = [block], [jax.ShapeDtypeStruct((m, n_out), out_dtype)]
    res, rode = _call(
        body, name=name, grid=grid, in_specs=in_specs, out_specs=out_specs, out_shape=out_shape,
        scratch_shapes=[], semantics=("arbitrary",) * 2 if rms else ("parallel", "parallel"), args=args, rider=rider)
    res = res if rms else res[0]
    return res if rider is None else (res, rode)


def _mm_tn(a, b, s_dim, *, name, rider=None):
    m, k1 = a.shape
    mb, n_all = b.shape
    assert mb == m and n_all % s_dim == 0
    n = n_all // s_dim
    tn, t1 = min(n, 1024), min(k1, 512 if _gathers(rider) else 1024)
    assert k1 % t1 == 0 and n % tn == 0
    npb = n // tn
    grid = (k1 // t1, s_dim * npb)

    def body(a_ref, b_ref, o_ref):
        o_ref[...] = lax.dot_general(a_ref[...], b_ref[...], TN_DIMS, preferred_element_type=F32).astype(BF16)

    res, rode = _call(
        body, name=name, grid=grid,
        in_specs=[pl.BlockSpec((m, t1), lambda i, j: (0, i)), pl.BlockSpec((m, tn), lambda i, j: (0, j))],
        out_specs=[pl.BlockSpec((None, None, t1, tn), lambda i, j: (0, j // npb, i, j % npb))],
        out_shape=[jax.ShapeDtypeStruct((1, s_dim, k1, n), BF16)],
        scratch_shapes=[], semantics=("parallel", "parallel"), args=[a, b], rider=rider)
    return res[0] if rider is None else (res[0], rode)


ROW_TILE = 512


def _rms_fwd(h, g, *, name, rider=None):
    m, d = h.shape

    def body(h_ref, g_ref, o_ref):
        hv = h_ref[...]
        rstd = lax.rsqrt(jnp.mean(hv * hv, axis=-1, keepdims=True) + EPS)
        o_ref[...] = (hv * rstd * g_ref[...]).astype(BF16)

    res, rode = _call(
        body, name=name, grid=(m // ROW_TILE,),
        in_specs=[pl.BlockSpec((ROW_TILE, d), lambda i: (i, 0)), pl.BlockSpec((1, d), lambda i: (0, 0))],
        out_specs=[pl.BlockSpec((ROW_TILE, d), lambda i: (i, 0))], out_shape=[jax.ShapeDtypeStruct((m, d), BF16)],
        scratch_shapes=[], semantics=("parallel",), args=[h, g], rider=rider)
    return res[0] if rider is None else (res[0], rode)


def _mlp_down_loss(act, w_2, h_res, g, target):
    m, k_dim = act.shape
    d = w_2.shape[-1]
    tm = _row_tile(k_dim)

    def body(a_ref, b_ref, r_ref, g_ref, t_ref, dh_ref, dhb_ref, dg_ref, loss_ref):
        hv = jnp.dot(a_ref[...], b_ref[...], preferred_element_type=F32) + r_ref[...]
        gv = g_ref[...]
        rstd = lax.rsqrt(jnp.mean(hv * hv, axis=-1, keepdims=True) + EPS)
        xhat = hv * rstd
        err = xhat * gv - t_ref[...]
        dy = err * (1.0 / d)
        dxhat = dy * gv
        dh = rstd * (dxhat - xhat * jnp.mean(dxhat * xhat, axis=-1, keepdims=True))
        dh_ref[...] = dh
        dhb_ref[...] = dh.astype(BF16)
        dg_part = jnp.sum(dy * xhat, axis=0, keepdims=True)
        sq = jnp.sum(jnp.sum(err * err, axis=1, keepdims=True), axis=0, keepdims=True) * (0.5 / d)
        loss_part = jnp.broadcast_to(sq, (8, TILE))

        @pl.when(pl.program_id(0) == 0)
        def _():
            dg_ref[...] = dg_part
            loss_ref[...] = loss_part

        @pl.when(pl.program_id(0) > 0)
        def _():
            dg_ref[...] += dg_part
            loss_ref[...] += loss_part

    row = pl.BlockSpec((tm, d), lambda i: (i, 0))
    vec = pl.BlockSpec((1, d), lambda i: (0, 0))
    return pl.pallas_call(
        body, name="mlp1_down_loss", grid=(m // tm,),
        in_specs=[pl.BlockSpec((tm, k_dim), lambda i: (i, 0)),
                  pl.BlockSpec((None, None, k_dim, d), lambda i: (0, 0, 0, 0)), row, vec, row],
        out_specs=[row, row, vec, pl.BlockSpec((8, TILE), lambda i: (0, 0))],
        out_shape=[jax.ShapeDtypeStruct((m, d), F32), jax.ShapeDtypeStruct((m, d), BF16),
                   jax.ShapeDtypeStruct((1, d), F32), jax.ShapeDtypeStruct((8, TILE), F32)],
        compiler_params=_params(("arbitrary",)),
    )(act, w_2, h_res, g, target)


def _shift_down(x, s, t_idx):
    return jnp.where(t_idx >= s, pltpu.roll(x, s, 0), 0.0)


def _shift_up(x, s, t_idx, t_len):
    return jnp.where(t_idx < t_len - s, pltpu.roll(x, t_len - s, 0), 0.0)


def _pool_select(group, s2, s4, s8, s16):
    return jnp.where(group == 0, s2, jnp.where(group == 1, s4, jnp.where(group == 2, s8, s16)))


def _pool_count(group, t_idx):
    win = jnp.left_shift(2, group)
    return jnp.minimum(t_idx + 1, win).astype(F32)


def _pool_fwd_math(a, group, t_idx):
    s2 = a + _shift_down(a, 1, t_idx)
    s4 = s2 + _shift_down(s2, 2, t_idx)
    s8 = s4 + _shift_down(s4, 4, t_idx)
    s16 = s8 + _shift_down(s8, 8, t_idx)
    return _pool_select(group, s2, s4, s8, s16) / _pool_count(group, t_idx) - a


def _pool_bwd_math(dpooled, group, t_idx, t_len):
    e = dpooled / _pool_count(group, t_idx)
    s2 = e + _shift_up(e, 1, t_idx, t_len)
    s4 = s2 + _shift_up(s2, 2, t_idx, t_len)
    s8 = s4 + _shift_up(s4, 4, t_idx, t_len)
    s16 = s8 + _shift_up(s8, 8, t_idx, t_len)
    return _pool_select(group, s2, s4, s8, s16) - dpooled


def _conv_fwd_math(c, w_ref, b_ref, t_idx):
    return (w_ref[0:1, :] * _shift_down(c, 2, t_idx) + w_ref[1:2, :] * _shift_down(c, 1, t_idx)
            + w_ref[2:3, :] * c + b_ref[...])


def _ab_fwd(p, pool_w, pool_scale, conv_w, conv_b, nseq, t_len, rider=None):
    m = p.shape[0]
    ng = 4

    def body(a_ref, xb_ref, gb_ref, gc_ref, pw_ref, ps_ref, cw_ref, cb_ref, o_ref):
        j = pl.program_id(1)
        t_idx = lax.broadcasted_iota(jnp.int32, (t_len, TILE), 0)

        @pl.when(j < ng)
        def _():
            pooled = _pool_fwd_math(a_ref[...].astype(F32), j, t_idx)
            mixed = jnp.dot(pooled.astype(BF16), pw_ref[...].astype(BF16), preferred_element_type=F32)
            o_ref[...] = (mixed * ps_ref[...]).astype(BF16)

        @pl.when(j >= ng)
        def _():
            c = gc_ref[...].astype(F32) * xb_ref[...].astype(F32)
            y = _conv_fwd_math(c, cw_ref, cb_ref, t_idx)
            o_ref[...] = (gb_ref[...].astype(F32) * y).astype(BF16)

    def pool_j(j):
        return jnp.minimum(j, ng - 1)

    def conv_j(j):
        return jnp.maximum(j - ng, 0)

    in_specs = [
        pl.BlockSpec((t_len, TILE), lambda s, j: (s, pool_j(j))),
        pl.BlockSpec((t_len, TILE), lambda s, j: (s, ng + conv_j(j))),
        pl.BlockSpec((t_len, TILE), lambda s, j: (s, 2 * ng + conv_j(j))),
        pl.BlockSpec((t_len, TILE), lambda s, j: (s, 3 * ng + conv_j(j))),
        pl.BlockSpec((None, TILE, TILE), lambda s, j: (pool_j(j), 0, 0)),
        pl.BlockSpec((None, 1, TILE), lambda s, j: (pool_j(j), 0, 0)),
        pl.BlockSpec((3, TILE), lambda s, j: (0, conv_j(j))),
        pl.BlockSpec((1, TILE), lambda s, j: (0, conv_j(j))),
    ]
    res, rode = _call(
        body, name="ab_mixer_fwd", grid=(nseq, 2 * ng), in_specs=in_specs,
        out_specs=[pl.BlockSpec((t_len, TILE), lambda s, j: (s, j))],
        out_shape=[jax.ShapeDtypeStruct((m, 2 * ng * TILE), BF16)], scratch_shapes=[],
        semantics=("parallel", "arbitrary"), args=[p, p, p, p, pool_w, pool_scale, conv_w, conv_b], rider=rider)
    return res[0] if rider is None else (res[0], rode)


def _ab_bwd(p, dmix, pool_w, pool_scale, conv_w, conv_b, nseq, t_len, rider=None):
    m = p.shape[0]
    ng = 4

    def body(a_ref, xb_ref, gb_ref, gc_ref, dma_ref, dmb_ref, pw_ref, ps_ref, cw_ref, cb_ref,
             da_ref, dxb_ref, dgb_ref, dgc_ref, dpw_ref, dps_ref, dcw_ref, dcb_ref):
        j = pl.program_id(0)
        first = pl.program_id(1) == 0
        t_idx = lax.broadcasted_iota(jnp.int32, (t_len, TILE), 0)

        pooled = _pool_fwd_math(a_ref[...].astype(F32), j, t_idx).astype(BF16)
        w_bf = pw_ref[...].astype(BF16)
        mixed = jnp.dot(pooled, w_bf, preferred_element_type=F32)
        dm = dma_ref[...].astype(F32)
        dps = jnp.sum(dm * mixed, axis=0, keepdims=True)
        dmixed = (dm * ps_ref[...]).astype(BF16)
        dpw = lax.dot_general(pooled, dmixed, TN_DIMS, preferred_element_type=F32)
        dpooled = lax.dot_general(dmixed, w_bf, NT_DIMS, preferred_element_type=F32)
        da_ref[...] = _pool_bwd_math(dpooled, j, t_idx, t_len).astype(BF16)

        xb = xb_ref[...].astype(F32)
        gb = gb_ref[...].astype(F32)
        gc = gc_ref[...].astype(F32)
        d = dmb_ref[...].astype(F32)
        c = gc * xb
        c1 = _shift_down(c, 1, t_idx)
        c2 = _shift_down(c, 2, t_idx)
        y = cw_ref[0:1, :] * c2 + cw_ref[1:2, :] * c1 + cw_ref[2:3, :] * c + cb_ref[...]
        dgb_ref[...] = (d * y).astype(BF16)
        dy = d * gb
        dc = (cw_ref[2:3, :] * dy + cw_ref[1:2, :] * _shift_up(dy, 1, t_idx, t_len)
              + cw_ref[0:1, :] * _shift_up(dy, 2, t_idx, t_len))
        dgc_ref[...] = (dc * xb).astype(BF16)
        dxb_ref[...] = (dc * gc).astype(BF16)
        dcw = jnp.concatenate([jnp.sum(dy * c2, axis=0, keepdims=True),
                               jnp.sum(dy * c1, axis=0, keepdims=True),
                               jnp.sum(dy * c, axis=0, keepdims=True)], axis=0)
        dcb = jnp.sum(dy, axis=0, keepdims=True)

        @pl.when(first)
        def _():
            dpw_ref[...] = dpw
            dps_ref[...] = dps
            dcw_ref[...] = dcw
            dcb_ref[...] = dcb

        @pl.when(jnp.logical_not(first))
        def _():
            dpw_ref[...] += dpw
            dps_ref[...] += dps
            dcw_ref[...] += dcw
            dcb_ref[...] += dcb

    def col(k):
        return pl.BlockSpec((t_len, TILE), lambda j, s: (s, k * ng + j))

    in_specs = [
        col(0), col(1), col(2), col(3), col(0), col(1),
        pl.BlockSpec((None, TILE, TILE), lambda j, s: (j, 0, 0)),
        pl.BlockSpec((None, 1, TILE), lambda j, s: (j, 0, 0)),
        pl.BlockSpec((3, TILE), lambda j, s: (0, j)),
        pl.BlockSpec((1, TILE), lambda j, s: (0, j)),
    ]
    piece = pl.BlockSpec((t_len, TILE), lambda j, s: (s, j))
    out_specs = [
        piece, piece, piece, piece,
        pl.BlockSpec((None, TILE, TILE), lambda j, s: (j, 0, 0)),
        pl.BlockSpec((None, 1, TILE), lambda j, s: (j, 0, 0)),
        pl.BlockSpec((3, TILE), lambda j, s: (0, j)),
        pl.BlockSpec((1, TILE), lambda j, s: (0, j)),
    ]
    w = ng * TILE
    out_shape = [jax.ShapeDtypeStruct((m, w), BF16)] * 4 + [
        jax.ShapeDtypeStruct((ng, TILE, TILE), F32), jax.ShapeDtypeStruct((ng, 1, TILE), F32),
        jax.ShapeDtypeStruct((3, w), F32), jax.ShapeDtypeStruct((1, w), F32)]
    res, rode = _call(
        body, name="ab_mixer_bwd", grid=(ng, nseq), in_specs=in_specs, out_specs=out_specs, out_shape=out_shape,
        scratch_shapes=[], semantics=("parallel", "arbitrary"),
        args=[p, p, p, p, dmix, dmix, pool_w, pool_scale, conv_w, conv_b], rider=rider)
    return res if rider is None else (res, rode)


SGU_ROWS = 512
INV_SQRT2 = 1.0 / math.sqrt(2.0)
INV_SQRT_2PI = 1.0 / math.sqrt(2.0 * math.pi)


def _gelu(x):
    return 0.5 * x * (1.0 + lax.erf(x * INV_SQRT2))


def _gelu_grad(x):
    return 0.5 * (1.0 + lax.erf(x * INV_SQRT2)) + x * (INV_SQRT_2PI * jnp.exp(-0.5 * x * x))


def _causal_tile(transposed=False):
    r = lax.broadcasted_iota(jnp.int32, (TILE, TILE), 0)
    c = lax.broadcasted_iota(jnp.int32, (TILE, TILE), 1)
    return r <= c if transposed else c <= r


def _sgu_norm(v, g_ref, b_ref):
    mu = jnp.mean(v, axis=-1, keepdims=True)
    xc = v - mu
    rstd = lax.rsqrt(jnp.mean(xc * xc, axis=-1, keepdims=True) + EPS)
    xhat = xc * rstd
    return xhat, rstd, xhat * g_ref[...] + b_ref[...]


def _sgu_fwd(p, norm_g, norm_b, w_s, bias_tile):
    m = p.shape[0]
    ng = 4
    width = ng * TILE

    def body(u_ref, v_ref, g_ref, b_ref, w_ref, bias_ref, o_ref):
        u = _gelu(u_ref[...].astype(F32))
        _, _, vln = _sgu_norm(_gelu(v_ref[...].astype(F32)), g_ref, b_ref)
        vln = vln.astype(BF16)
        causal = _causal_tile()
        for g in range(ng):
            cols = slice(g * TILE, (g + 1) * TILE)
            wg = jnp.where(causal, w_ref[g], 0.0).astype(BF16)
            for n in range(SGU_ROWS // TILE):
                rows = slice(n * TILE, (n + 1) * TILE)
                s = jnp.dot(wg, vln[rows, cols], preferred_element_type=F32) + bias_ref[g]
                o_ref[rows, cols] = (u[rows, cols] * s).astype(BF16)

    vec = pl.BlockSpec((1, width), lambda i: (0, 0))
    tiles = pl.BlockSpec((ng, TILE, TILE), lambda i: (0, 0, 0))
    return pl.pallas_call(
        body, name="sgu_fwd", grid=(m // SGU_ROWS,),
        in_specs=[pl.BlockSpec((SGU_ROWS, width), lambda i: (i, 0)),
                  pl.BlockSpec((SGU_ROWS, width), lambda i: (i, 1)), vec, vec, tiles, tiles],
        out_specs=pl.BlockSpec((SGU_ROWS, width), lambda i: (i, 0)),
        out_shape=jax.ShapeDtypeStruct((m, 2 * width), BF16),
        compiler_params=_params(("parallel",)),
    )(p, p, norm_g, norm_b, w_s, bias_tile)


def _sgu_bwd(p, dmix, norm_g, norm_b, w_s, w_s_t, bias_tile):
    m = p.shape[0]
    ng = 4
    width = ng * TILE

    def body(u_ref, v_ref, dc_ref, g_ref, b_ref, w_ref, wt_ref, bias_ref,
             du_ref, dv_ref, dw_ref, dbs_ref, dg_ref, db_ref, ds_scr, dvln_scr):
        u_pre = u_ref[...].astype(F32)
        v_pre = v_ref[...].astype(F32)
        u = _gelu(u_pre)
        xhat, rstd, vln = _sgu_norm(_gelu(v_pre), g_ref, b_ref)
        vln = vln.astype(BF16)
        dc = dc_ref[...].astype(F32)
        causal = _causal_tile()
        ones = jnp.ones((TILE, TILE), BF16)
        first = pl.program_id(0) == 0
        for g in range(ng):
            cols = slice(g * TILE, (g + 1) * TILE)
            wg = jnp.where(causal, w_ref[g], 0.0).astype(BF16)
            wgt = jnp.where(_causal_tile(transposed=True), wt_ref[g], 0.0).astype(BF16)
            dw_acc = jnp.zeros((TILE, TILE), F32)
            dbs_acc = jnp.zeros((TILE, TILE), F32)
            for n in range(SGU_ROWS // TILE):
                rows = slice(n * TILE, (n + 1) * TILE)
                vt = vln[rows, cols]
                s = jnp.dot(wg, vt, preferred_element_type=F32) + bias_ref[g]
                ds_scr[rows, cols] = dc[rows, cols] * s
                ds = (dc[rows, cols] * u[rows, cols]).astype(BF16)
                dw_acc += lax.dot_general(ds, vt, NT_DIMS, preferred_element_type=F32)
                dbs_acc += jnp.dot(ds, ones, preferred_element_type=F32)
                dvln_scr[rows, cols] = jnp.dot(wgt, ds, preferred_element_type=F32)
            dw_g = jnp.where(causal, dw_acc, 0.0)

            @pl.when(first)
            def _():
                dw_ref[g] = dw_g
                dbs_ref[g] = dbs_acc

            @pl.when(jnp.logical_not(first))
            def _():
                dw_ref[g] += dw_g
                dbs_ref[g] += dbs_acc

        du_ref[...] = (ds_scr[...] * _gelu_grad(u_pre)).astype(BF16)
        dvln = dvln_scr[...]
        dxhat = dvln * g_ref[...]
        dv = rstd * (dxhat - jnp.mean(dxhat, axis=-1, keepdims=True)
                     - xhat * jnp.mean(dxhat * xhat, axis=-1, keepdims=True))
        dv_ref[...] = (dv * _gelu_grad(v_pre)).astype(BF16)
        dg_part = jnp.sum(dvln * xhat, axis=0, keepdims=True)
        db_part = jnp.sum(dvln, axis=0, keepdims=True)

        @pl.when(first)
        def _():
            dg_ref[...] = dg_part
            db_ref[...] = db_part

        @pl.when(jnp.logical_not(first))
        def _():
            dg_ref[...] += dg_part
            db_ref[...] += db_part

    vec = pl.BlockSpec((1, width), lambda i: (0, 0))
    tiles = pl.BlockSpec((ng, TILE, TILE), lambda i: (0, 0, 0))
    rows0 = pl.BlockSpec((SGU_ROWS, width), lambda i: (i, 0))
    rows1 = pl.BlockSpec((SGU_ROWS, width), lambda i: (i, 1))
    return pl.pallas_call(
        body, name="sgu_bwd", grid=(m // SGU_ROWS,),
        in_specs=[rows0, rows1, rows0, vec, vec, tiles, tiles, tiles],
        out_specs=[rows0, rows0, tiles, tiles, vec, vec],
        out_shape=[jax.ShapeDtypeStruct((m, width), BF16), jax.ShapeDtypeStruct((m, width), BF16),
                   jax.ShapeDtypeStruct((ng, TILE, TILE), F32), jax.ShapeDtypeStruct((ng, TILE, TILE), F32),
                   jax.ShapeDtypeStruct((1, width), F32), jax.ShapeDtypeStruct((1, width), F32)],
        scratch_shapes=[pltpu.VMEM((SGU_ROWS, width), F32), pltpu.VMEM((SGU_ROWS, width), F32)],
        compiler_params=_params(("arbitrary",)),
    )(p, p, dmix, norm_g, norm_b, w_s, w_s_t, bias_tile)


SB_DH = 64
SB_SCALE = 1.0 / math.sqrt(SB_DH)


SB_BLOCK = 256
SB_SUB = SB_BLOCK // TILE
SB_PASS = 4


def _split_passes(i):
    rem = i % SB_PASS
    return i // SB_PASS, rem >= 2, rem % 2 == 1


def _sum_matrix(kind):
    j = lax.broadcasted_iota(jnp.int32, (TILE, 2 * TILE), 0)
    s = lax.broadcasted_iota(jnp.int32, (TILE, 2 * TILE), 1)
    tri = {"after": j > s, "upto": j <= s, "before": j < s}[kind]
    return jnp.where(jnp.logical_or(s >= TILE, tri), 1.0, 0.0).astype(BF16)


def _strict_mask():
    r = lax.broadcasted_iota(jnp.int32, (SB_BLOCK, SB_BLOCK), 0)
    c = lax.broadcasted_iota(jnp.int32, (SB_BLOCK, SB_BLOCK), 1)
    return c < r


def _head_lanes(h):
    lane = lax.broadcasted_iota(jnp.int32, (1, TILE), 1)
    return (lane >= h * SB_DH) & (lane < (h + 1) * SB_DH)


def _log_gates(z):
    log_sig = jnp.minimum(z, 0.0) - jnp.log(1.0 + jnp.exp(-jnp.abs(z)))
    return log_sig, log_sig - z


def _sb_fwd(p, mix, nseq, t_len, gather):
    m = p.shape[0]
    npair = 4
    ng = len(gather)
    last_step = nseq * npair - 1

    def body(q_ref, k_ref, v_ref, *rest):
        o_ref, lt_ref = rest[ng + 1:ng + 3]
        kh_ref, vh_ref = rest[2 * ng + 3:2 * ng + 5]
        step = pl.program_id(0) * npair + pl.program_id(1)
        send, forward, finish = _gather_steps(rest[ng + 3:2 * ng + 3], *rest[2 * ng + 5:])
        pl.when(step == 0)(send)
        pl.when(step == (last_step + 1) // 2)(forward)
        for h in range(2):
            keep = _head_lanes(h)
            kh_ref[h] = jnp.where(keep, k_ref[...], 0).astype(BF16)
            vh_ref[h] = jnp.where(keep, v_ref[...], 0).astype(BF16)
        summat = _sum_matrix("after")
        strict = _strict_mask()

        def one_pass(q, row0, nsub, diag, state):
            rows = pl.ds(row0, nsub * TILE)
            log_sig, pieces = [], []
            for h in range(2):
                zh = lax.dot_general(q, kh_ref[h, rows, :], NT_DIMS, preferred_element_type=F32)
                log_sig_h, logkeep = _log_gates(zh)
                if diag:
                    logkeep = jnp.where(strict, logkeep, 0.0)
                log_sig.append(log_sig_h)
                pieces += [logkeep[:, b * TILE:(b + 1) * TILE] for b in range(nsub)]
            sums = jnp.dot(jnp.concatenate(pieces, axis=0).astype(BF16), summat, preferred_element_type=F32)
            out = []
            for h in range(2):
                carry, acc = state[2 * h], state[2 * h + 1]
                after = [None] * nsub
                for b in reversed(range(nsub)):
                    part = sums[(h * nsub + b) * SB_BLOCK:(h * nsub + b + 1) * SB_BLOCK]
                    after[b] = part[:, :TILE] + carry
                    carry = carry + part[:, TILE:]
                w = jnp.exp(log_sig[h] + jnp.concatenate(after, axis=1))
                if diag:
                    w = jnp.where(strict, w, 0.0)
                out += [carry, acc + jnp.dot(w.astype(BF16), vh_ref[h, rows, :], preferred_element_type=F32)]
            return tuple(out)

        def q_block(i, _):
            r0 = pl.multiple_of(i * SB_BLOCK, SB_BLOCK)
            q = q_ref[pl.ds(r0, SB_BLOCK), :] * SB_SCALE
            zero = jnp.zeros((SB_BLOCK, TILE), F32)
            state = one_pass(q, r0, SB_SUB, True, (zero,) * 4)
            full, two, one = _split_passes(i)
            state = lax.fori_loop(
                0, full,
                lambda jj, st: one_pass(q, pl.multiple_of((i - SB_PASS * (jj + 1)) * SB_BLOCK, SB_BLOCK),
                                        SB_PASS * SB_SUB, False, st),
                state)
            state = lax.cond(
                two, lambda st: one_pass(q, pl.multiple_of((i % 2) * SB_BLOCK, SB_BLOCK), 2 * SB_SUB, False, st),
                lambda st: st, state)
            state = lax.cond(one, lambda st: one_pass(q, 0, SB_SUB, False, st), lambda st: st, state)
            o_ref[pl.ds(r0, SB_BLOCK), :] = (state[1] + state[3]).astype(BF16)
            lt_ref[pl.ds(r0, SB_BLOCK), :] = jnp.where(_head_lanes(0), state[0], state[2])
            return 0

        lax.fori_loop(0, t_len // SB_BLOCK, q_block, 0)
        pl.when(step == last_step)(finish)

    def col(k):
        return pl.BlockSpec((t_len, TILE), lambda s, hp: (s, k * npair + hp))

    out = pl.BlockSpec((t_len, TILE), lambda s, hp: (s, hp))
    res = pl.pallas_call(
        body, name="stickbreak_fwd", grid=(nseq, npair), in_specs=[col(2), col(3), col(4)] + [ANY] * (ng + 1),
        out_specs=[pl.BlockSpec((t_len, TILE), lambda s, hp: (s, npair + hp)), out] + [ANY] * ng,
        out_shape=[jax.ShapeDtypeStruct(mix.shape, BF16), jax.ShapeDtypeStruct((m, npair * TILE), F32)]
        + [jax.ShapeDtypeStruct(b.shape, b.dtype) for b in gather],
        input_output_aliases={**{3 + a: 2 + a for a in range(ng)}, 3 + ng: 0},
        scratch_shapes=[pltpu.VMEM((2, t_len, TILE), BF16), pltpu.VMEM((2, t_len, TILE), BF16)] + _gather_sems(ng),
        compiler_params=pltpu.CompilerParams(dimension_semantics=("arbitrary", "arbitrary"),
                                             vmem_limit_bytes=VMEM_LIMIT_BYTES, has_side_effects=True),
    )(p, p, p, *gather, mix)
    return res[0], res[1], res[2:]


def _sb_bwd(p, dmix, ltot, nseq, t_len, exchange):
    m = p.shape[0]
    npair = 4
    ne = len(exchange)
    last_step = nseq * npair - 1

    def body(q_ref, k_ref, v_ref, do_ref, lt_ref, *rest):
        dq_ref, dk_ref, dv_ref = rest[ne:ne + 3]
        kh_ref, vh_ref, dk_acc, dv_acc = rest[2 * ne + 3:2 * ne + 7]
        step = pl.program_id(0) * npair + pl.program_id(1)
        send, finish = _exchange_steps(rest[:ne], rest[ne + 3:2 * ne + 3], *rest[2 * ne + 7:])
        pl.when(step == 0)(send)
        for h in range(2):
            keep = _head_lanes(h)
            kh_ref[h] = jnp.where(keep, k_ref[...], 0).astype(BF16)
            vh_ref[h] = jnp.where(keep, v_ref[...], 0).astype(BF16)
        dk_acc[...] = jnp.zeros_like(dk_acc)
        dv_acc[...] = jnp.zeros_like(dv_acc)
        sum_upto = _sum_matrix("upto")
        sum_before = _sum_matrix("before")
        strict = _strict_mask()
        lane = lax.broadcasted_iota(jnp.int32, (SB_BLOCK, TILE), 1)

        def running(x, matrix, start, nsub):
            pieces = [x[h][:, b * TILE:(b + 1) * TILE] for h in range(2) for b in range(nsub)]
            sums = jnp.dot(jnp.concatenate(pieces, axis=0).astype(BF16), matrix, preferred_element_type=F32)
            wide, ends = [], []
            for h in range(2):
                total, cols = start[h], []
                for b in range(nsub):
                    part = sums[(h * nsub + b) * SB_BLOCK:(h * nsub + b + 1) * SB_BLOCK]
                    cols.append(part[:, :TILE] + total)
                    total = total + part[:, TILE:]
                wide.append(jnp.concatenate(cols, axis=1))
                ends.append(total)
            return wide, ends

        def one_pass(q, do, qh, doh, ltot, row0, nsub, diag, state):
            rows = pl.ds(row0, nsub * TILE)
            log_sig, logkeep = [], []
            for h in range(2):
                zh = lax.dot_general(q, kh_ref[h, rows, :], NT_DIMS, preferred_element_type=F32)
                log_sig_h, logkeep_h = _log_gates(zh)
                log_sig.append(log_sig_h)
                logkeep.append(jnp.where(strict, logkeep_h, 0.0) if diag else logkeep_h)
            upto, sum_l = running(logkeep, sum_upto, [state[0], state[3]], nsub)
            w, g = [], []
            for h in range(2):
                wh = jnp.exp(log_sig[h] + (ltot[h] - upto[h]))
                if diag:
                    wh = jnp.where(strict, wh, 0.0)
                w.append(wh)
                g.append(wh * lax.dot_general(do, vh_ref[h, rows, :], NT_DIMS, preferred_element_type=F32))
            g_before, sum_g = running(g, sum_before, [state[1], state[4]], nsub)
            out, dk_new, dv_new = [], 0.0, 0.0
            for h in range(2):
                dz = g[h] - jnp.exp(log_sig[h]) * (g[h] + g_before[h])
                if diag:
                    dz = jnp.where(strict, dz, 0.0)
                dzb = dz.astype(BF16)
                dq = state[3 * h + 2] + jnp.dot(dzb, kh_ref[h, rows, :], preferred_element_type=F32)
                dk_new = dk_new + lax.dot_general(dzb, qh[h], TN_DIMS, preferred_element_type=F32)
                dv_new = dv_new + lax.dot_general(w[h].astype(BF16), doh[h], TN_DIMS, preferred_element_type=F32)
                out += [sum_l[h], sum_g[h], dq]
            dk_acc[rows, :] += dk_new
            dv_acc[rows, :] += dv_new
            return tuple(out)

        def q_block(i, _):
            r0 = pl.multiple_of(i * SB_BLOCK, SB_BLOCK)
            q = q_ref[pl.ds(r0, SB_BLOCK), :] * SB_SCALE
            do = do_ref[pl.ds(r0, SB_BLOCK), :]
            lt = lt_ref[pl.ds(r0, SB_BLOCK), :]
            qh, doh, ltot = [], [], []
            for h in range(2):
                keep = _head_lanes(h)
                qh.append(jnp.where(keep, q, 0).astype(BF16))
                doh.append(jnp.where(keep, do, 0).astype(BF16))
                ltot.append(jnp.sum(jnp.where(lane == h * SB_DH, lt, 0.0), axis=1, keepdims=True))
            zero = jnp.zeros((SB_BLOCK, TILE), F32)
            full, two, one = _split_passes(i)
            state = lax.fori_loop(
                0, full,
                lambda jj, st: one_pass(q, do, qh, doh, ltot, pl.multiple_of(SB_PASS * jj * SB_BLOCK, SB_BLOCK),
                                        SB_PASS * SB_SUB, False, st),
                (zero,) * 6)
            state = lax.cond(
                two,
                lambda st: one_pass(q, do, qh, doh, ltot, pl.multiple_of(SB_PASS * full * SB_BLOCK, SB_BLOCK),
                                    2 * SB_SUB, False, st),
                lambda st: st, state)
            state = lax.cond(
                one,
                lambda st: one_pass(q, do, qh, doh, ltot, pl.multiple_of((i - 1) * SB_BLOCK, SB_BLOCK), SB_SUB, False, st),
                lambda st: st, state)
            state = one_pass(q, do, qh, doh, ltot, r0, SB_SUB, True, state)
            dq_ref[pl.ds(r0, SB_BLOCK), :] = ((state[2] + state[5]) * SB_SCALE).astype(BF16)
            return 0

        lax.fori_loop(0, t_len // SB_BLOCK, q_block, 0)
        dk_ref[...] = dk_acc[...].astype(BF16)
        dv_ref[...] = dv_acc[...].astype(BF16)
        pl.when(step == last_step)(finish)

    def col(k):
        return pl.BlockSpec((t_len, TILE), lambda s, hp: (s, k * npair + hp))

    out = pl.BlockSpec((t_len, TILE), lambda s, hp: (s, hp))
    width = npair * TILE
    res = pl.pallas_call(
        body, name="stickbreak_bwd", grid=(nseq, npair),
        in_specs=[col(2), col(3), col(4), col(1), out] + [ANY] * ne, out_specs=[out, out, out] + [ANY] * ne,
        out_shape=[jax.ShapeDtypeStruct((m, width), BF16)] * 3 + _exchange_shapes(exchange),
        scratch_shapes=[pltpu.VMEM((2, t_len, TILE), BF16), pltpu.VMEM((2, t_len, TILE), BF16),
                        pltpu.VMEM((t_len, TILE), F32), pltpu.VMEM((t_len, TILE), F32)] + _exchange_sems(ne),
        compiler_params=pltpu.CompilerParams(dimension_semantics=("arbitrary", "arbitrary"),
                                             vmem_limit_bytes=VMEM_LIMIT_BYTES, has_side_effects=True),
    )(p, p, p, dmix, ltot, *exchange)
    return res[0], res[1], res[2], res[3:]


def _adam_math(w, g, m, v):
    m = ADAM_B1 * m + (1.0 - ADAM_B1) * g
    v = ADAM_B2 * v + (1.0 - ADAM_B2) * (g * g)
    m_hat = m / (1.0 - ADAM_B1 ** ADAM_STEP)
    v_hat = v / (1.0 - ADAM_B2 ** ADAM_STEP)
    delta = -ADAM_LR * (m_hat / (jnp.sqrt(v_hat) + ADAM_EPS) + ADAM_WD * w)
    return delta, m, v


def _cast_place(w, layer, pos, *, name):
    _, r, c = w.shape
    tr = min(r, 256)

    def body(pos_ref, w_ref, o_ref):
        o_ref[...] = w_ref[...].astype(BF16)

    grid_spec = pltpu.PrefetchScalarGridSpec(
        num_scalar_prefetch=1, grid=(r // tr,),
        in_specs=[pl.BlockSpec((None, tr, c), lambda i, pos_ref: (layer, i, 0))],
        out_specs=pl.BlockSpec((None, None, tr, c), lambda i, pos_ref: (0, pos_ref[0], i, 0)))
    return pl.pallas_call(
        body, name=name, grid_spec=grid_spec, out_shape=jax.ShapeDtypeStruct((1, N_CHIP, r, c), BF16),
        compiler_params=_params(("parallel",)),
    )(pos, w)


def _cast_place_all(items, pos, *, name, rider=None):
    tiles = [min(w.shape[1], 256) for w, _ in items]
    counts = [w.shape[1] // t for (w, _), t in zip(items, tiles)]
    starts = [sum(counts[:a]) for a in range(len(items))]
    n = len(items)

    def body(pos_ref, *refs):
        i = pl.program_id(0)
        for a in range(n):
            @pl.when((i >= starts[a]) & (i < starts[a] + counts[a]))
            def _():
                refs[n + a][...] = refs[a][...].astype(BF16)

    def block(a):
        return lambda i: jnp.clip(i - starts[a], 0, counts[a] - 1)

    in_specs, out_specs, out_shape = [], [], []
    for a, ((w, layer), t) in enumerate(zip(items, tiles)):
        _, r, c = w.shape
        in_specs.append(pl.BlockSpec((None, t, c), lambda i, pos_ref, a=a, layer=layer: (layer, block(a)(i), 0)))
        out_specs.append(pl.BlockSpec((None, None, t, c), lambda i, pos_ref, a=a: (0, pos_ref[0], block(a)(i), 0)))
        out_shape.append(jax.ShapeDtypeStruct((1, N_CHIP, r, c), BF16))
    res, rode = _call(body, name=name, grid=(sum(counts),), in_specs=in_specs, out_specs=out_specs,
                      out_shape=out_shape, scratch_shapes=[], semantics=("arbitrary",),
                      args=[w for w, _ in items], rider=rider, prefetch=pos)
    return res if rider is None else (res, rode)


def _pair_sum(mine, got, pos, *, name):
    l_dim, s_dim, h, c = got.shape
    th = min(h, 512)
    nt = h // th

    def body(pos_ref, a_ref, b_ref, o_ref):
        o_ref[...] = (a_ref[...].astype(F32) + b_ref[...].astype(F32)).astype(BF16)

    spec = pl.BlockSpec((None, None, th, c), lambda l, s, i, pos_ref: (l, s, i, 0))
    grid_spec = pltpu.PrefetchScalarGridSpec(
        num_scalar_prefetch=1, grid=(l_dim, s_dim, nt),
        in_specs=[pl.BlockSpec((None, None, th, c), lambda l, s, i, pos_ref: (l, s, pos_ref[1] * nt + i, 0)), spec],
        out_specs=spec)
    return pl.pallas_call(
        body, name=name, grid_spec=grid_spec, out_shape=jax.ShapeDtypeStruct(got.shape, BF16),
        compiler_params=_params(("parallel",) * 3),
    )(pos, mine, got)


def _chip_sum(sums, landed, pos, *, name):
    l_dim, _, h, c = sums.shape
    th = min(h, 512)
    nt = h // th

    def body(pos_ref, own, r0, r1, r2, o_ref):
        o_ref[...] = ((own[...].astype(F32) + r0[...].astype(F32)) + r1[...].astype(F32)) + r2[...].astype(F32)

    def piece(k):
        return pl.BlockSpec((None, None, th, c), lambda l, i, pos_ref: (l, k, i, 0))

    grid_spec = pltpu.PrefetchScalarGridSpec(
        num_scalar_prefetch=1, grid=(l_dim, nt),
        in_specs=[pl.BlockSpec((None, None, th, c), lambda l, i, pos_ref: (l, pos_ref[0], i, 0)),
                  piece(0), piece(1), piece(2)],
        out_specs=pl.BlockSpec((None, th, c), lambda l, i, pos_ref: (l, pos_ref[1] * nt + i, 0)))
    return pl.pallas_call(
        body, name=name, grid_spec=grid_spec, out_shape=jax.ShapeDtypeStruct((l_dim, 2 * h, c), F32),
        compiler_params=_params(("parallel",) * 2),
    )(pos, sums, landed, landed, landed)


def _adam_big(w, m, v, grads, *, name):
    l_dim, r, c = w.shape
    assert len(grads) == l_dim
    tr = min(r, 512)

    def body(*refs):
        w_ref, m_ref, v_ref = refs[:3]
        g_refs = refs[3:3 + l_dim]
        go_ref, d_ref, mo_ref, vo_ref = refs[3 + l_dim:]
        g = g_refs[0][...]
        for l in range(1, l_dim):
            g = jnp.where(pl.program_id(0) == l, g_refs[l][...], g)
        delta, m_new, v_new = _adam_math(w_ref[...], g, m_ref[...], v_ref[...])
        go_ref[...] = g
        d_ref[...] = delta
        mo_ref[...] = m_new
        vo_ref[...] = v_new

    spec = pl.BlockSpec((None, tr, c), lambda l, i: (l, i, 0))
    gspec = pl.BlockSpec((None, tr, c), lambda l, i: (0, i, 0))
    return pl.pallas_call(
        body, name=name, grid=(l_dim, r // tr), in_specs=[spec] * 3 + [gspec] * l_dim, out_specs=[spec] * 4,
        out_shape=[jax.ShapeDtypeStruct(w.shape, F32)] * 4, compiler_params=_params(("parallel",) * 2),
    )(w, m, v, *grads)


def _position():
    return lax.axis_index("x"), lax.axis_index("y"), lax.axis_index("c")


def _other_chips(x, y):
    return [(1 - x, y), (x, 1 - y), (1 - x, 1 - y)]


def _remote(src, dst, send_sem, recv_sem, device):
    return pltpu.make_async_remote_copy(src_ref=src, dst_ref=dst, send_sem=send_sem, recv_sem=recv_sem,
                                        device_id=device, device_id_type=MESH)


ANY = pl.BlockSpec(memory_space=pl.ANY)


def _gather_sems(n):
    return [pltpu.SemaphoreType.DMA((3 * n,))] * 4


def _gather_steps(outs, send_sems, recv_sems, fwd_send, fwd_recv):
    n = len(outs)
    x, y, c = _position()
    chips = _other_chips(x, y)
    sibling = (x, y, 1 - c)

    def half(a, chip, core):
        h = outs[a].shape[2] // 2
        return outs[a].at[:, 2 * chip[0] + chip[1], pl.ds(core * h, h), :]

    def over_ici(a, k, chip):
        block = half(a, chip, c)
        return _remote(block, block, send_sems.at[3 * a + k], recv_sems.at[3 * a + k], (*chips[k], c))

    def over_d2d(a, k, core):
        block = half(a, chips[k], core)
        return _remote(block, block, fwd_send.at[3 * a + k], fwd_recv.at[3 * a + k], sibling)

    def send():
        for a in range(n):
            for k in range(3):
                over_ici(a, k, (x, y)).start()

    def forward():
        for k in range(3):
            for a in range(n):
                over_ici(a, k, chips[k]).wait_recv()
                over_d2d(a, k, c).start()

    def finish():
        for k in range(3):
            for a in range(n):
                over_d2d(a, k, 1 - c).wait_recv()
        for a in range(n):
            for k in range(3):
                over_ici(a, k, (x, y)).wait_send()
                over_d2d(a, k, c).wait_send()

    return send, forward, finish


def _swap_halves(grads, *, name):
    n = len(grads)

    def body(*refs):
        send, finish = _swap_steps(refs[:n], refs[n:2 * n], *refs[2 * n:])
        send()
        finish()

    sem = pltpu.SemaphoreType.DMA((n,))
    return pl.pallas_call(
        body, name=name, in_specs=[ANY] * n, out_specs=[ANY] * n, out_shape=_swap_shapes(grads),
        scratch_shapes=[sem, sem], compiler_params=pltpu.CompilerParams(has_side_effects=True),
    )(*grads)


def _swap_shapes(grads):
    return [jax.ShapeDtypeStruct(g.shape[:2] + (g.shape[2] // 2, g.shape[3]), g.dtype) for g in grads]


def _swap_steps(ins, outs, send_sems, recv_sems):
    x, y, c = _position()

    def copy(a):
        h = ins[a].shape[2] // 2
        return _remote(ins[a].at[:, :, pl.ds((1 - c) * h, h), :], outs[a], send_sems.at[a], recv_sems.at[a],
                       (x, y, 1 - c))

    def send():
        for a in range(len(ins)):
            copy(a).start()

    def finish():
        for a in range(len(ins)):
            copy(a).wait()

    return send, finish


def _exchange_shapes(sums):
    return [jax.ShapeDtypeStruct((s.shape[0], 3) + s.shape[2:], s.dtype) for s in sums]


def _exchange_sems(n):
    return [pltpu.SemaphoreType.DMA((3 * n,))] * 2


def _exchange_steps(ins, outs, send_sems, recv_sems):
    n = len(ins)
    x, y, c = _position()
    chips = _other_chips(x, y)

    def copy(a, k):
        chip = chips[k]
        return _remote(ins[a].at[:, 2 * chip[0] + chip[1]], outs[a].at[:, k],
                       send_sems.at[3 * a + k], recv_sems.at[3 * a + k], (*chip, c))

    def send():
        for a in range(n):
            for k in range(3):
                copy(a, k).start()

    def finish():
        for a in range(n):
            for k in range(3):
                copy(a, k).wait()

    return send, finish


def _join_halves(bufs, *, name):
    n = len(bufs)

    def body(*refs):
        send, finish = _join_steps(refs[n:2 * n], *refs[2 * n:])
        send()
        finish()

    sem = pltpu.SemaphoreType.DMA((n,))
    return pl.pallas_call(
        body, name=name, in_specs=[ANY] * n, out_specs=[ANY] * n,
        out_shape=[jax.ShapeDtypeStruct(b.shape, b.dtype) for b in bufs],
        input_output_aliases={a: a for a in range(n)},
        scratch_shapes=[sem, sem], compiler_params=pltpu.CompilerParams(has_side_effects=True),
    )(*bufs)


def _join_steps(outs, send_sems, recv_sems):
    x, y, c = _position()

    def copy(a, core):
        h = outs[a].shape[1] // 2
        half = outs[a].at[:, pl.ds(core * h, h), :]
        return _remote(half, half, send_sems.at[a], recv_sems.at[a], (x, y, 1 - c))

    def send():
        for a in range(len(outs)):
            copy(a, c).start()

    def finish():
        for a in range(len(outs)):
            copy(a, c).wait_send()
            copy(a, 1 - c).wait_recv()

    return send, finish


def _allgather_steps(ins, outs, send_sems, recv_sems, local_sems):
    n = len(ins)
    x, y, c = _position()
    me, sibling = (x, y, c), (x, y, 1 - c)
    chips = _other_chips(x, y)

    def slot(a, dev):
        return outs[a].at[4 * dev[0] + 2 * dev[1] + dev[2]]

    def copy(a, k, block, to, own=False):
        return _remote(ins[a] if own else slot(a, block), slot(a, block),
                       send_sems.at[7 * a + k], recv_sems.at[7 * a + k], to)

    def first(a):
        return [copy(a, 0, me, sibling, own=True)] + [copy(a, 1 + k, me, (*chips[k], c), own=True) for k in range(3)]

    def local(a):
        return pltpu.make_async_copy(ins[a], slot(a, me), local_sems.at[a])

    def send():
        for a in range(n):
            local(a).start()
            for cp in first(a):
                cp.start()

    def forward():
        for a in range(n):
            for k in range(3):
                copy(a, 1 + k, (*chips[k], c), me).wait_recv()
                copy(a, 4 + k, (*chips[k], c), sibling).start()

    def finish():
        for a in range(n):
            copy(a, 0, sibling, me).wait_recv()
            for k in range(3):
                copy(a, 4 + k, (*chips[k], 1 - c), me).wait_recv()
        for a in range(n):
            for cp in first(a) + [copy(a, 4 + k, (*chips[k], c), sibling) for k in range(3)]:
                cp.wait_send()
            local(a).wait()

    return send, forward, finish


def _allreduce_small(packs):
    n = len(packs)

    def body(*refs):
        ins, outs, gath = refs[:n], refs[n:2 * n], refs[2 * n:3 * n]
        send_sems, recv_sems = refs[3 * n:]
        x, y, c = _position()
        me, sibling = (x, y, c), (x, y, 1 - c)
        chips = _other_chips(x, y)

        def slot(a, dev):
            return gath[a].at[4 * dev[0] + 2 * dev[1] + dev[2]]

        def copy(a, k, block, to, src=None):
            return _remote(slot(a, block) if src is None else src, slot(a, block),
                           send_sems.at[7 * a + k], recv_sems.at[7 * a + k], to)

        started = []
        for a in range(n):
            slot(a, me)[...] = ins[a][...]
            first = [copy(a, 0, me, sibling, src=ins[a])]
            first += [copy(a, 1 + k, me, (*chip, c), src=ins[a]) for k, chip in enumerate(chips)]
            for cp in first:
                cp.start()
            started += first
        for a in range(n):
            for k, chip in enumerate(chips):
                copy(a, 1 + k, (*chip, c), me).wait_recv()
                cp = copy(a, 4 + k, (*chip, c), sibling)
                cp.start()
                started.append(cp)
        for a in range(n):
            copy(a, 0, sibling, me).wait_recv()
            for k, chip in enumerate(chips):
                copy(a, 4 + k, (*chip, 1 - c), me).wait_recv()
        for cp in started:
            cp.wait_send()
        for a in range(n):
            total = gath[a][0]
            for d in range(1, N_DEV):
                total = total + gath[a][d]
            outs[a][...] = total

    vmem = pl.BlockSpec(memory_space=pltpu.VMEM)
    sem = pltpu.SemaphoreType.DMA((7 * n,))
    return pl.pallas_call(
        body, name="allreduce_small", in_specs=[vmem] * n, out_specs=[vmem] * n,
        out_shape=[jax.ShapeDtypeStruct(p.shape, p.dtype) for p in packs],
        scratch_shapes=[pltpu.VMEM((N_DEV,) + p.shape, p.dtype) for p in packs] + [sem, sem],
        compiler_params=pltpu.CompilerParams(has_side_effects=True, vmem_limit_bytes=VMEM_LIMIT_BYTES),
    )(*packs)


LOSS_ROW = 1040


def _pad_rows(a, rows=8):
    return jnp.concatenate([a, jnp.zeros((rows - a.shape[0], a.shape[1]), a.dtype)], axis=0)

def _adam_small(wide, mid, narrow, late, params):
    names = ["mix_norm_g", "mlp_norm_g", "final_norm_g", "conv_b", "conv_w", "sgu_norm_g", "sgu_norm_b",
             "pool_w", "pool_scale", "sgu_w", "sgu_b"]
    n = len(names)

    def body(*refs):
        wmv = refs[4:4 + 3 * n]
        outs = refs[4 + 3 * n:]
        x, y, _ = _position()
        q = 2 * x + y

        def total(ref):
            t = ref[0]
            for dev in range(1, N_DEV):
                t = t + ref[dev]
            return t

        wide_sum, mid_sum, narrow_sum = total(refs[0]), total(refs[1]), total(refs[2])
        late_ref = refs[3]

        def my_quarter(rows):
            parts = [rows[:, s * TILE:(s + 1) * TILE] for s in range(N_CHIP)]
            return jnp.where(q == 0, parts[0], jnp.where(q == 1, parts[1], jnp.where(q == 2, parts[2], parts[3])))

        def tiles(first_row):
            return [((0, g), narrow_sum[first_row + g * TILE:first_row + (g + 1) * TILE, :]) for g in range(4)]

        grads = {
            "mix_norm_g": [((), wide_sum[0:2, :] + late_ref[0:2, :])],
            "mlp_norm_g": [((), wide_sum[8:10, :])],
            "final_norm_g": [((), wide_sum[16:17, :])],
            "conv_b": [((), mid_sum[0:1, :])],
            "conv_w": [((0,), my_quarter(mid_sum[8:11, :]))],
            "sgu_norm_g": [((), my_quarter(mid_sum[16:17, :]))],
            "sgu_norm_b": [((), my_quarter(mid_sum[24:25, :]))],
            "pool_w": tiles(0),
            "sgu_w": tiles(512),
            "pool_scale": [((0,), narrow_sum[1024:1028, :])],
            "sgu_b": [((0,), narrow_sum[1032:1036, :])],
        }
        outs[4 * n][...] = narrow_sum[LOSS_ROW:LOSS_ROW + 8, :]
        for i, name in enumerate(names):
            w_ref, m_ref, v_ref = wmv[3 * i:3 * i + 3]
            for lead, g in grads[name]:
                idx = lead + (slice(None), slice(None))
                delta, m_new, v_new = _adam_math(w_ref[idx], g, m_ref[idx], v_ref[idx])
                outs[4 * i][idx] = g
                outs[4 * i + 1][idx] = delta
                outs[4 * i + 2][idx] = m_new
                outs[4 * i + 3][idx] = v_new

    vmem = pl.BlockSpec(memory_space=pltpu.VMEM)
    args, out_shape = [wide, mid, narrow, late], []
    for name in names:
        w, m, v = params[name]
        args += [w, m, v]
        out_shape += [jax.ShapeDtypeStruct(w.shape, F32)] * 4
    out_shape.append(jax.ShapeDtypeStruct((8, TILE), F32))
    res = pl.pallas_call(
        body, name="adam_small", in_specs=[vmem] * len(args), out_specs=[vmem] * len(out_shape),
        out_shape=out_shape, compiler_params=pltpu.CompilerParams(vmem_limit_bytes=VMEM_LIMIT_BYTES),
    )(*args)
    return {name: res[4 * i:4 * i + 4] for i, name in enumerate(names)}, res[4 * n]


def _pair_sums(grads, got, pos, tag):
    return [_pair_sum(a, b, pos, name=f"pair_sum_{tag}{i}") for i, (a, b) in enumerate(zip(grads, got))]


def _chip_sums(sums, landed, pos, tag):
    return [_chip_sum(s, r, pos, name=f"chip_sum_{tag}{i}") for i, (s, r) in enumerate(zip(sums, landed))]


def kernel(x, mix_norm_g, mlp_norm_g, ab_w_in, pool_w, pool_scale, conv_w, conv_b, ab_w_out, cd_w_in, sgu_norm_g, sgu_norm_b, sgu_w, sgu_b, cd_w_out, mlp_w1, mlp_w2, final_norm_g, loss_target, m_mix_norm_g, m_mlp_norm_g, m_ab_w_in, m_pool_w, m_pool_scale, m_conv_w, m_conv_b, m_ab_w_out, m_cd_w_in, m_sgu_norm_g, m_sgu_norm_b, m_sgu_w, m_sgu_b, m_cd_w_out, m_mlp_w1, m_mlp_w2, m_final_norm_g, v_mix_norm_g, v_mlp_norm_g, v_ab_w_in, v_pool_w, v_pool_scale, v_conv_w, v_conv_b, v_ab_w_out, v_cd_w_in, v_sgu_norm_g, v_sgu_norm_b, v_sgu_w, v_sgu_b, v_cd_w_out, v_mlp_w1, v_mlp_w2, v_final_norm_g):
    nseq, t_len, d = x.shape
    m_tok = nseq * t_len
    h0 = x.reshape(m_tok, d)
    target = loss_target.reshape(m_tok, d)

    x_idx, y_idx = lax.axis_index("x"), lax.axis_index("y")
    q_idx = 2 * x_idx + y_idx
    pos = jnp.stack([q_idx, lax.axis_index("c")]).astype(jnp.int32)

    def shard_buffer(w, layer, tag):
        return _cast_place(w, layer, pos, name=f"cast_place_{tag}")

    def row_block(w):
        return w.reshape(1, 1, -1, w.shape[-1])

    (buf_ab_out, buf_w1_0, buf_w2_0, buf_cd_in, *later_weights), ((w_ab_in,),) = _cast_place_all(
        [(ab_w_out, 0), (mlp_w1, 0), (mlp_w2, 0), (cd_w_in, 0), (cd_w_out, 0), (mlp_w1, 1), (mlp_w2, 1)], pos,
        name="cast_place_rest", rider=[("gather", [shard_buffer(ab_w_in, 0, "ab_in")])])

    pool_w3, pool_scale3 = pool_w[0], pool_scale[0].reshape(4, 1, TILE)
    sgu_w3 = sgu_w[0]
    sgu_w3_t = jnp.swapaxes(sgu_w3, 1, 2)
    sgu_bias_tile = jnp.broadcast_to(sgu_b[0][:, :, None], (4, TILE, TILE))
    conv_b2 = conv_b

    def place_quarter(v):
        return lax.dynamic_update_slice(jnp.zeros((v.shape[0], 4 * TILE), F32), v, (0, q_idx * TILE))

    sharded_small = jnp.concatenate(
        [place_quarter(conv_w[0]), place_quarter(sgu_norm_g), place_quarter(sgu_norm_b),
         jnp.zeros((3, 4 * TILE), F32)], axis=0)
    sharded_small, = _allreduce_small([sharded_small])
    sharded_small = sharded_small * 0.5
    conv_w_full = sharded_small[0:3]
    sgu_g_full = sharded_small[3:4]
    sgu_b_full = sharded_small[4:5]

    xn0 = _rms_fwd(h0, mix_norm_g[0:1], name="rms_fwd_mix0")
    p_ab, ((w_1_0,),) = _mm_nn(xn0, w_ab_in, 0, out_dtype=BF16, name="ab_in_proj",
                               rider=[("gather", [buf_w1_0])])
    mix0, ((w_ab_out,),) = _ab_fwd(p_ab, pool_w3, pool_scale3, conv_w_full, conv_b2, nseq, t_len,
                                   rider=[("gather", [buf_ab_out])])
    w_ab_out = row_block(w_ab_out)
    h1, hn0 = _mm_nn(mix0, w_ab_out, 0, out_dtype=F32, name="ab_out_proj", epilogue="residual", extra=h0,
                     norm_g=mlp_norm_g[0:1])
    (act0, relu0), ((w_2_0,),) = _mm_nn(hn0, w_1_0, 0, out_dtype=BF16, name="mlp0_up", epilogue="relu2",
                                        rider=[("gather", [buf_w2_0])])
    w_2_0 = row_block(w_2_0)
    (h2, xn1), ((w_cd_in,),) = _mm_nn(act0, w_2_0, 0, out_dtype=F32, name="mlp0_down", epilogue="residual", extra=h1,
                                      norm_g=mix_norm_g[1:2],
                                      rider=[("gather", [buf_cd_in])])

    p_cd = _mm_nn(xn1, w_cd_in, 0, out_dtype=BF16, name="cd_in_proj")
    mix1 = _sgu_fwd(p_cd, sgu_g_full, sgu_b_full, sgu_w3, sgu_bias_tile)
    mix1, ltot, (w_cd_out, w_1_1, w_2_1) = _sb_fwd(p_cd, mix1, nseq, t_len, later_weights)
    w_cd_out, w_2_1 = row_block(w_cd_out), row_block(w_2_1)
    h3, hn1 = _mm_nn(mix1, w_cd_out, 0, out_dtype=F32, name="cd_out_proj", epilogue="residual", extra=h2,
                     norm_g=mlp_norm_g[1:2])
    act1, relu1 = _mm_nn(hn1, w_1_1, 0, out_dtype=BF16, name="mlp1_up", epilogue="relu2")

    dh4, dh4_bf, dg_final, loss_tile = _mlp_down_loss(act1, w_2_1, h3, final_norm_g.reshape(1, d), target)

    def as_pieces(g):
        return g.reshape(1, N_CHIP, -1, g.shape[-1]) if g.shape[1] == 1 else g

    dz1 = _mm_nt(dh4_bf, w_2_1, 0, out_dtype=BF16, name="mlp1_down_bwd", epilogue="relu2_bwd", extra=relu1)
    g_w2_1 = as_pieces(_mm_tn(act1, dh4_bf, 1, name="mlp1_down_wgrad"))
    g_w1_1 = _mm_tn(hn1, dz1, N_CHIP, name="mlp1_up_wgrad")
    (dh3, dh3_bf, dg_mlp1), (got_a,) = _mm_nt(
        dz1, w_1_1, 0, out_dtype=F32, name="mlp1_up_bwd", epilogue="rms_bwd",
        extra=(h3, mlp_norm_g[1:2], dh4), rider=[("swap", [g_w1_1, g_w2_1])])

    g_cd_out = as_pieces(_mm_tn(mix1, dh3_bf, 1, name="cd_out_wgrad"))
    dmix1, (got_cd_out,) = _mm_nt(dh3_bf, w_cd_out, 0, out_dtype=BF16, name="cd_out_bwd",
                                  rider=[("swap", [g_cd_out])])
    sums_a = _pair_sums([g_w1_1, g_w2_1, g_cd_out], got_a + got_cd_out, pos, "a")
    du, dv, dsgu_w, dsgu_bs, dsgu_g, dsgu_b = _sgu_bwd(p_cd, dmix1, sgu_g_full, sgu_b_full, sgu_w3, sgu_w3_t,
                                                      sgu_bias_tile)
    dq, dk, dvv, landed_a = _sb_bwd(p_cd, dmix1, ltot, nseq, t_len, sums_a)
    halves_a = _chip_sums(sums_a, landed_a, pos, "a")
    dp_cd = jnp.concatenate([du, dv, dq, dk, dvv], axis=1)
    g_cd_in, ((r_w1_1, r_w2_1, r_cd_out),) = _mm_tn(xn1, dp_cd, N_CHIP, name="cd_in_wgrad",
                                                    rider=[("join", halves_a)])
    (dh2, dh2_bf, dg_mix1), (got_c,) = _mm_nt(
        dp_cd, w_cd_in, 0, out_dtype=F32, name="cd_in_bwd", epilogue="rms_bwd",
        extra=(h2, mix_norm_g[1:2], dh3), rider=[("swap", [g_cd_in])])

    sums_c = _pair_sums([g_cd_in], got_c, pos, "c")
    dz0, (landed_c,) = _mm_nt(dh2_bf, w_2_0, 0, out_dtype=BF16, name="mlp0_down_bwd", epilogue="relu2_bwd",
                              extra=relu0, rider=[("exchange", sums_c)])
    halves_c = _chip_sums(sums_c, landed_c, pos, "c")
    g_w2_0, ((r_cd_in,),) = _mm_tn(act0, dh2_bf, 1, name="mlp0_down_wgrad", rider=[("join", halves_c)])
    g_w2_0 = as_pieces(g_w2_0)
    g_w1_0, (got_d,) = _mm_tn(hn0, dz0, N_CHIP, name="mlp0_up_wgrad", rider=[("swap", [g_w2_0])])
    sums_d = _pair_sums([g_w2_0], got_d, pos, "d")
    (dh1, dh1_bf, dg_mlp0), (landed_d, got_e) = _mm_nt(
        dz0, w_1_0, 0, out_dtype=F32, name="mlp0_up_bwd", epilogue="rms_bwd",
        extra=(h1, mlp_norm_g[0:1], dh2), rider=[("exchange", sums_d), ("swap", [g_w1_0])])
    halves_d = _chip_sums(sums_d, landed_d, pos, "d")
    sums_e = _pair_sums([g_w1_0], got_e, pos, "e")

    dmix0, ((r_w2_0,),) = _mm_nt(dh1_bf, w_ab_out, 0, out_dtype=BF16, name="ab_out_bwd", rider=[("join", halves_d)])
    g_ab_out = as_pieces(_mm_tn(mix0, dh1_bf, 1, name="ab_out_wgrad"))
    (da, dxb, dgb, dgc, dpool_w, dpool_scale, dconv_w, dconv_b), (landed_e, got_f) = _ab_bwd(
        p_ab, dmix0, pool_w3, pool_scale3, conv_w_full, conv_b2, nseq, t_len,
        rider=[("exchange", sums_e), ("swap", [g_ab_out])])
    halves_e = _chip_sums(sums_e, landed_e, pos, "e")
    sums_f = _pair_sums([g_ab_out], got_f, pos, "f")
    dp_ab = jnp.concatenate([da, dxb, dgb, dgc], axis=1)
    wide = jnp.concatenate([_pad_rows(jnp.concatenate([jnp.zeros_like(dg_mix1), dg_mix1], axis=0)),
                            _pad_rows(jnp.concatenate([dg_mlp0, dg_mlp1], axis=0)), _pad_rows(dg_final)], axis=0)
    mid = jnp.concatenate([_pad_rows(dconv_b), _pad_rows(dconv_w), _pad_rows(dsgu_g), _pad_rows(dsgu_b)], axis=0)
    narrow = jnp.concatenate(
        [dpool_w.reshape(4 * TILE, TILE), dsgu_w.reshape(4 * TILE, TILE), _pad_rows(dpool_scale.reshape(4, TILE)),
         _pad_rows(dsgu_bs[:, :, 0]), loss_tile], axis=0)
    g_ab_in, (landed_f, (r_w1_0,), (wide, mid, narrow)) = _mm_tn(
        xn0, dp_ab, N_CHIP, name="ab_in_wgrad",
        rider=[("exchange", sums_f), ("join", halves_e), ("allgather", [wide, mid, narrow])])
    halves_f = _chip_sums(sums_f, landed_f, pos, "f")
    sums_g = _pair_sums([g_ab_in], _swap_halves([g_ab_in], name="swap_halves_g"), pos, "g")
    (grad_x, _, dg_mix0), (landed_g, (r_ab_out,)) = _mm_nt(
        dp_ab, w_ab_in, 0, out_dtype=F32, name="ab_in_bwd", epilogue="rms_bwd",
        extra=(h0, mix_norm_g[0:1], dh1), rider=[("exchange", sums_g), ("join", halves_f)])
    r_ab_in, = _join_halves(_chip_sums(sums_g, landed_g, pos, "g"), name="join_halves_g")

    big_out = {
        "ab_w_in": _adam_big(ab_w_in, m_ab_w_in, v_ab_w_in, [r_ab_in], name="adam_ab_w_in"),
        "ab_w_out": _adam_big(ab_w_out, m_ab_w_out, v_ab_w_out, [r_ab_out], name="adam_ab_w_out"),
        "cd_w_in": _adam_big(cd_w_in, m_cd_w_in, v_cd_w_in, [r_cd_in], name="adam_cd_w_in"),
        "cd_w_out": _adam_big(cd_w_out, m_cd_w_out, v_cd_w_out, [r_cd_out], name="adam_cd_w_out"),
        "mlp_w1": _adam_big(mlp_w1, m_mlp_w1, v_mlp_w1, [r_w1_0, r_w1_1], name="adam_mlp_w1"),
        "mlp_w2": _adam_big(mlp_w2, m_mlp_w2, v_mlp_w2, [r_w2_0, r_w2_1], name="adam_mlp_w2"),
    }

    late, = _allreduce_small([_pad_rows(dg_mix0)])
    small_out, loss_sum = _adam_small(wide, mid, narrow, late, {
        "mix_norm_g": (mix_norm_g, m_mix_norm_g, v_mix_norm_g),
        "mlp_norm_g": (mlp_norm_g, m_mlp_norm_g, v_mlp_norm_g),
        "final_norm_g": tuple(a.reshape(1, d) for a in (final_norm_g, m_final_norm_g, v_final_norm_g)),
        "conv_b": (conv_b, m_conv_b, v_conv_b),
        "conv_w": (conv_w, m_conv_w, v_conv_w),
        "sgu_norm_g": (sgu_norm_g, m_sgu_norm_g, v_sgu_norm_g),
        "sgu_norm_b": (sgu_norm_b, m_sgu_norm_b, v_sgu_norm_b),
        "pool_w": (pool_w, m_pool_w, v_pool_w),
        "pool_scale": (pool_scale, m_pool_scale, v_pool_scale),
        "sgu_w": (sgu_w, m_sgu_w, v_sgu_w),
        "sgu_b": (sgu_b, m_sgu_b, v_sgu_b),
    })
    small_out["final_norm_g"] = [a.reshape(d) for a in small_out["final_norm_g"]]

    order = ["mix_norm_g", "mlp_norm_g", "ab_w_in", "pool_w", "pool_scale", "conv_w", "conv_b", "ab_w_out",
             "cd_w_in", "sgu_norm_g", "sgu_norm_b", "sgu_w", "sgu_b", "cd_w_out", "mlp_w1", "mlp_w2",
             "final_norm_g"]
    both = {**big_out, **small_out}
    loss = loss_sum[0, 0]
    outs = [loss, grad_x.reshape(nseq, t_len, d)]
    for kind in range(4):
        outs += [both[name][kind] for name in order]
    return tuple(outs)
```

```python
import math

import jax
import jax.numpy as jnp
from jax import lax
from jax.experimental import pallas as pl
from jax.experimental.pallas import tpu as pltpu

F32 = jnp.float32
BF16 = jnp.bfloat16
MESH = pl.DeviceIdType.MESH

EPS = 1e-6
TILE = 128
N_CHIP = 4
N_DEV = 8
VMEM_LIMIT_BYTES = 56 * 1024 * 1024

ADAM_LR = 0.001
ADAM_B1 = 0.9
ADAM_B2 = 0.999
ADAM_EPS = 1e-08
ADAM_WD = 0.01
ADAM_STEP = 10

NT_DIMS = (((1,), (1,)), ((), ()))
TN_DIMS = (((0,), (0,)), ((), ()))


def _params(sem=None):
    return pltpu.CompilerParams(dimension_semantics=sem, vmem_limit_bytes=VMEM_LIMIT_BYTES)


def _call(body, *, name, grid, in_specs, out_specs, out_shape, scratch_shapes, semantics, args, rider=None,
          prefetch=None):
    npre = 0 if prefetch is None else 1

    def launch(kernel, in_specs, out_specs, out_shape, scratch_shapes, operands, aliases, params):
        if prefetch is None:
            return pl.pallas_call(kernel, name=name, grid=grid, in_specs=in_specs, out_specs=out_specs,
                                  out_shape=out_shape, scratch_shapes=scratch_shapes, input_output_aliases=aliases,
                                  compiler_params=params)(*operands)
        spec = pltpu.PrefetchScalarGridSpec(num_scalar_prefetch=1, grid=grid, in_specs=in_specs, out_specs=out_specs,
                                            scratch_shapes=scratch_shapes)
        return pl.pallas_call(kernel, name=name, grid_spec=spec, out_shape=out_shape,
                              input_output_aliases={k + 1: v for k, v in aliases.items()},
                              compiler_params=params)(prefetch, *operands)

    if not rider:
        res = launch(body, list(in_specs), list(out_specs), list(out_shape), list(scratch_shapes), args, {},
                     _params(semantics))
        return list(res), []
    plans = [_rider_plan(kind, arrays) for kind, arrays in rider]
    arrays = [a for _, group in rider for a in group]
    nr, n_in, n_out, n_scr = len(arrays), len(in_specs), len(out_specs), len(scratch_shapes)
    first_out, first_scr = n_in + nr, n_in + nr + n_out + nr
    last_step = math.prod(grid) - 1

    def riding(*refs):
        pre, refs = refs[:npre], refs[npre:]
        step = 0
        for axis, size in enumerate(grid):
            step = step * size + pl.program_id(axis)
        steps, at, sem_at = [], 0, first_scr + n_scr
        for (kind, group), (_, sems, _) in zip(rider, plans):
            k = len(group)
            steps.append(_rider_steps(kind, refs[n_in + at:n_in + at + k],
                                      refs[first_out + n_out + at:first_out + n_out + at + k],
                                      refs[sem_at:sem_at + len(sems)]))
            at, sem_at = at + k, sem_at + len(sems)
        for send, _, _ in steps:
            pl.when(step == 0)(send)
        for _, forward, _ in steps:
            if forward is not None:
                pl.when(step == last_step)(forward)
        body(*pre, *refs[:n_in], *refs[first_out:first_out + n_out], *refs[first_scr:first_scr + n_scr])
        for _, _, finish in steps:
            pl.when(step == last_step)(finish)

    aliases, at = {}, 0
    for (_, group), (_, _, aliased) in zip(rider, plans):
        if aliased:
            aliases.update({n_in + at + a: n_out + at + a for a in range(len(group))})
        at += len(group)
    res = launch(
        riding, list(in_specs) + [ANY] * nr, list(out_specs) + [ANY] * nr,
        list(out_shape) + [s for shapes, _, _ in plans for s in shapes],
        list(scratch_shapes) + [s for _, sems, _ in plans for s in sems], [*args, *arrays], aliases,
        pltpu.CompilerParams(dimension_semantics=("arbitrary",) * len(grid), vmem_limit_bytes=VMEM_LIMIT_BYTES,
                             has_side_effects=True))
    rode, at = [], n_out
    for _, group in rider:
        rode.append(list(res[at:at + len(group)]))
        at += len(group)
    return list(res[:n_out]), rode


def _rider_plan(kind, arrays):
    n = len(arrays)
    same = [jax.ShapeDtypeStruct(a.shape, a.dtype) for a in arrays]
    pair = [pltpu.SemaphoreType.DMA((n,))] * 2
    if kind == "gather":
        return same, _gather_sems(n), True
    if kind == "exchange":
        return _exchange_shapes(arrays), _exchange_sems(n), False
    if kind == "swap":
        return _swap_shapes(arrays), pair, False
    if kind == "allgather":
        return ([jax.ShapeDtypeStruct((N_DEV,) + a.shape, a.dtype) for a in arrays],
                [pltpu.SemaphoreType.DMA((7 * n,))] * 2 + [pltpu.SemaphoreType.DMA((n,))], False)
    assert kind == "join"
    return same, pair, True


def _rider_steps(kind, ins, outs, sems):
    if kind == "gather":
        return _gather_steps(outs, *sems)
    if kind == "allgather":
        return _allgather_steps(ins, outs, *sems)
    if kind == "exchange":
        send, finish = _exchange_steps(ins, outs, *sems)
    elif kind == "swap":
        send, finish = _swap_steps(ins, outs, *sems)
    else:
        send, finish = _join_steps(outs, *sems)
    return send, None, finish


def _gathers(rider):
    return any(kind in ("gather", "allgather") for kind, _ in rider or ())


def _row_tile(k_dim, roomy=False):
    if k_dim > 1024:
        return 512
    return 2048 if roomy else 1024


def _mm_nn(a, b4, layer, *, out_dtype, name, epilogue=None, extra=None, norm_g=None, rider=None):
    m, k_dim = a.shape
    _, s_dim, kb, n = b4.shape
    assert kb == k_dim
    tm = min(m, _row_tile(k_dim, roomy=epilogue != "residual" and norm_g is None and not _gathers(rider)))
    tn = min(n, 1024)
    assert m % tm == 0 and n % tn == 0
    npb = n // tn
    grid = (m // tm, s_dim * npb)
    n_in = 2 + (extra is not None) + (norm_g is not None)
    two_outputs = norm_g is not None or epilogue == "relu2"
    assert norm_g is None or (tn == s_dim * n and epilogue != "relu2")

    def body(*refs):
        a_ref, b_ref = refs[:2]
        e_ref = refs[2] if extra is not None else None
        g_ref = refs[n_in - 1] if norm_g is not None else None
        o_ref = refs[n_in]
        acc = jnp.dot(a_ref[...], b_ref[...], preferred_element_type=F32)
        if epilogue == "relu2":
            r = jnp.maximum(acc, 0.0)
            refs[n_in + 1][...] = r.astype(BF16)
            acc = r * r
        elif epilogue == "residual":
            acc = acc + e_ref[...]
        o_ref[...] = acc.astype(out_dtype)
        if norm_g is not None:
            rstd = lax.rsqrt(jnp.mean(acc * acc, axis=-1, keepdims=True) + EPS)
            refs[n_in + 1][...] = (acc * rstd * g_ref[...]).astype(BF16)

    in_specs = [
        pl.BlockSpec((tm, k_dim), lambda i, j: (i, 0)),
        pl.BlockSpec((None, None, k_dim, tn), lambda i, j: (layer, j // npb, 0, j % npb)),
    ]
    args = [a, b4]
    if extra is not None:
        in_specs.append(pl.BlockSpec((tm, tn), lambda i, j: (i, j)))
        args.append(extra)
    out_block = pl.BlockSpec((tm, tn), lambda i, j: (i, j))
    out_specs, out_shape = [out_block], [jax.ShapeDtypeStruct((m, s_dim * n), out_dtype)]
    if norm_g is not None:
        in_specs.append(pl.BlockSpec((1, tn), lambda i, j: (0, j)))
        args.append(norm_g)
    if two_outputs:
        out_specs.append(out_block)
        out_shape.append(jax.ShapeDtypeStruct((m, s_dim * n), BF16))
    res, rode = _call(
        body, name=name, grid=grid, in_specs=in_specs, out_specs=out_specs, out_shape=out_shape,
        scratch_shapes=[], semantics=("parallel", "parallel"), args=args, rider=rider)
    res = res if two_outputs else res[0]
    return res if rider is None else (res, rode)


def _mm_nt(a, b4, layer, *, out_dtype, name, epilogue=None, extra=None, rider=None):
    m, k_dim = a.shape
    _, s_dim, n_out, n = b4.shape
    assert k_dim == s_dim * n
    rms = epilogue == "rms_bwd"
    roomy = not rms and out_dtype != F32 and not _gathers(rider)
    tm, tn = min(m, _row_tile(k_dim, roomy=roomy)), min(n_out, 1024)
    assert m % tm == 0 and n_out % tn == 0
    grid = (m // tm, n_out // tn)
    assert not rms or tn == n_out
    extras = [] if extra is None else (list(extra) if rms else [extra])
    n_in = 2 + len(extras)

    def body(*refs):
        a_ref, b_ref = refs[:2]
        e_refs = refs[2:n_in]
        o_ref = refs[n_in]
        acc = lax.dot_general(a_ref[:, 0:n], b_ref[0], NT_DIMS, preferred_element_type=F32)
        for s in range(1, s_dim):
            acc = acc + lax.dot_general(a_ref[:, s * n:(s + 1) * n], b_ref[s], NT_DIMS, preferred_element_type=F32)
        if epilogue == "relu2_bwd":
            acc = acc * (2.0 * e_refs[0][...].astype(F32))
        if not rms:
            o_ref[...] = acc.astype(out_dtype)
        else:
            h_ref, g_ref, dres_ref = e_refs
            dhb_ref, dg_ref = refs[n_in + 1:n_in + 3]
            hv = h_ref[...]
            rstd = lax.rsqrt(jnp.mean(hv * hv, axis=-1, keepdims=True) + EPS)
            xhat = hv * rstd
            dxhat = acc * g_ref[...]
            dh = dres_ref[...] + rstd * (dxhat - xhat * jnp.mean(dxhat * xhat, axis=-1, keepdims=True))
            o_ref[...] = dh
            dhb_ref[...] = dh.astype(BF16)
            dg_part = jnp.sum(acc * xhat, axis=0, keepdims=True)
            first = pl.program_id(0) == 0

            @pl.when(first)
            def _():
                dg_ref[...] = dg_part

            @pl.when(jnp.logical_not(first))
            def _():
                dg_ref[...] += dg_part

    in_specs = [
        pl.BlockSpec((tm, k_dim), lambda i, j: (i, 0)),
        pl.BlockSpec((None, s_dim, tn, n), lambda i, j: (layer, 0, j, 0)),
    ]
    args = [a, b4] + extras
    block = pl.BlockSpec((tm, tn), lambda i, j: (i, j))
    vec = pl.BlockSpec((1, tn), lambda i, j: (0, j))
    if rms:
        in_specs += [block, vec, block]
        out_specs = [block, block, vec]
        out_shape = [jax.ShapeDtypeStruct((m, n_out), F32), jax.ShapeDtypeStruct((m, n_out), BF16),
                     jax.ShapeDtypeStruct((1, n_out), F32)]
    else:
        in_specs += [block] * len(extras)
        out_specs, out_shape = [block], [jax.ShapeDtypeStruct((m, n_out), out_dtype)]
    res, rode = _call(
        body, name=name, grid=grid, in_specs=in_specs, out_specs=out_specs, out_shape=out_shape,
        scratch_shapes=[], semantics=("arbitrary",) * 2 if rms else ("parallel", "parallel"), args=args, rider=rider)
    res = res if rms else res[0]
    return res if rider is None else (res, rode)


def _mm_tn(a, b, s_dim, *, name, rider=None):
    m, k1 = a.shape
    mb, n_all = b.shape
    assert mb == m and n_all % s_dim == 0
    n = n_all // s_dim
    tn, t1 = min(n, 1024), min(k1, 512 if _gathers(rider) else 1024)
    assert k1 % t1 == 0 and n % tn == 0
    npb = n // tn
    grid = (k1 // t1, s_dim * npb)

    def body(a_ref, b_ref, o_ref):
        o_ref[...] = lax.dot_general(a_ref[...], b_ref[...], TN_DIMS, preferred_element_type=F32).astype(BF16)

    res, rode = _call(
        body, name=name, grid=grid,
        in_specs=[pl.BlockSpec((m, t1), lambda i, j: (0, i)), pl.BlockSpec((m, tn), lambda i, j: (0, j))],
        out_specs=[pl.BlockSpec((None, None, t1, tn), lambda i, j: (0, j // npb, i, j % npb))],
        out_shape=[jax.ShapeDtypeStruct((1, s_dim, k1, n), BF16)],
        scratch_shapes=[], semantics=("parallel", "parallel"), args=[a, b], rider=rider)
    return res[0] if rider is None else (res[0], rode)


ROW_TILE = 512


def _rms_fwd(h, g, *, name, rider=None):
    m, d = h.shape

    def body(h_ref, g_ref, o_ref):
        hv = h_ref[...]
        rstd = lax.rsqrt(jnp.mean(hv * hv, axis=-1, keepdims=True) + EPS)
        o_ref[...] = (hv * rstd * g_ref[...]).astype(BF16)

    res, rode = _call(
        body, name=name, grid=(m // ROW_TILE,),
        in_specs=[pl.BlockSpec((ROW_TILE, d), lambda i: (i, 0)), pl.BlockSpec((1, d), lambda i: (0, 0))],
        out_specs=[pl.BlockSpec((ROW_TILE, d), lambda i: (i, 0))], out_shape=[jax.ShapeDtypeStruct((m, d), BF16)],
        scratch_shapes=[], semantics=("parallel",), args=[h, g], rider=rider)
    return res[0] if rider is None else (res[0], rode)


def _mlp_down_loss(act, w_2, h_res, g, target):
    m, k_dim = act.shape
    d = w_2.shape[-1]
    tm = _row_tile(k_dim)

    def body(a_ref, b_ref, r_ref, g_ref, t_ref, dh_ref, dhb_ref, dg_ref, loss_ref):
        hv = jnp.dot(a_ref[...], b_ref[...], preferred_element_type=F32) + r_ref[...]
        gv = g_ref[...]
        rstd = lax.rsqrt(jnp.mean(hv * hv, axis=-1, keepdims=True) + EPS)
        xhat = hv * rstd
        err = xhat * gv - t_ref[...]
        dy = err * (1.0 / d)
        dxhat = dy * gv
        dh = rstd * (dxhat - xhat * jnp.mean(dxhat * xhat, axis=-1, keepdims=True))
        dh_ref[...] = dh
        dhb_ref[...] = dh.astype(BF16)
        dg_part = jnp.sum(dy * xhat, axis=0, keepdims=True)
        sq = jnp.sum(jnp.sum(err * err, axis=1, keepdims=True), axis=0, keepdims=True) * (0.5 / d)
        loss_part = jnp.broadcast_to(sq, (8, TILE))

        @pl.when(pl.program_id(0) == 0)
        def _():
            dg_ref[...] = dg_part
            loss_ref[...] = loss_part

        @pl.when(pl.program_id(0) > 0)
        def _():
            dg_ref[...] += dg_part
            loss_ref[...] += loss_part

    row = pl.BlockSpec((tm, d), lambda i: (i, 0))
    vec = pl.BlockSpec((1, d), lambda i: (0, 0))
    return pl.pallas_call(
        body, name="mlp1_down_loss", grid=(m // tm,),
        in_specs=[pl.BlockSpec((tm, k_dim), lambda i: (i, 0)),
                  pl.BlockSpec((None, None, k_dim, d), lambda i: (0, 0, 0, 0)), row, vec, row],
        out_specs=[row, row, vec, pl.BlockSpec((8, TILE), lambda i: (0, 0))],
        out_shape=[jax.ShapeDtypeStruct((m, d), F32), jax.ShapeDtypeStruct((m, d), BF16),
                   jax.ShapeDtypeStruct((1, d), F32), jax.ShapeDtypeStruct((8, TILE), F32)],
        compiler_params=_params(("arbitrary",)),
    )(act, w_2, h_res, g, target)


def _shift_down(x, s, t_idx):
    return jnp.where(t_idx >= s, pltpu.roll(x, s, 0), 0.0)


def _shift_up(x, s, t_idx, t_len):
    return jnp.where(t_idx < t_len - s, pltpu.roll(x, t_len - s, 0), 0.0)


def _pool_select(group, s2, s4, s8, s16):
    return jnp.where(group == 0, s2, jnp.where(group == 1, s4, jnp.where(group == 2, s8, s16)))


def _pool_count(group, t_idx):
    win = jnp.left_shift(2, group)
    return jnp.minimum(t_idx + 1, win).astype(F32)


def _pool_fwd_math(a, group, t_idx):
    s2 = a + _shift_down(a, 1, t_idx)
    s4 = s2 + _shift_down(s2, 2, t_idx)
    s8 = s4 + _shift_down(s4, 4, t_idx)
    s16 = s8 + _shift_down(s8, 8, t_idx)
    return _pool_select(group, s2, s4, s8, s16) / _pool_count(group, t_idx) - a


def _pool_bwd_math(dpooled, group, t_idx, t_len):
    e = dpooled / _pool_count(group, t_idx)
    s2 = e + _shift_up(e, 1, t_idx, t_len)
    s4 = s2 + _shift_up(s2, 2, t_idx, t_len)
    s8 = s4 + _shift_up(s4, 4, t_idx, t_len)
    s16 = s8 + _shift_up(s8, 8, t_idx, t_len)
    return _pool_select(group, s2, s4, s8, s16) - dpooled


def _conv_fwd_math(c, w_ref, b_ref, t_idx):
    return (w_ref[0:1, :] * _shift_down(c, 2, t_idx) + w_ref[1:2, :] * _shift_down(c, 1, t_idx)
            + w_ref[2:3, :] * c + b_ref[...])


def _ab_fwd(p, pool_w, pool_scale, conv_w, conv_b, nseq, t_len, rider=None):
    m = p.shape[0]
    ng = 4

    def body(a_ref, xb_ref, gb_ref, gc_ref, pw_ref, ps_ref, cw_ref, cb_ref, o_ref):
        j = pl.program_id(1)
        t_idx = lax.broadcasted_iota(jnp.int32, (t_len, TILE), 0)

        @pl.when(j < ng)
        def _():
            pooled = _pool_fwd_math(a_ref[...].astype(F32), j, t_idx)
            mixed = jnp.dot(pooled.astype(BF16), pw_ref[...].astype(BF16), preferred_element_type=F32)
            o_ref[...] = (mixed * ps_ref[...]).astype(BF16)

        @pl.when(j >= ng)
        def _():
            c = gc_ref[...].astype(F32) * xb_ref[...].astype(F32)
            y = _conv_fwd_math(c, cw_ref, cb_ref, t_idx)
            o_ref[...] = (gb_ref[...].astype(F32) * y).astype(BF16)

    def pool_j(j):
        return jnp.minimum(j, ng - 1)

    def conv_j(j):
        return jnp.maximum(j - ng, 0)

    in_specs = [
        pl.BlockSpec((t_len, TILE), lambda s, j: (s, pool_j(j))),
        pl.BlockSpec((t_len, TILE), lambda s, j: (s, ng + conv_j(j))),
        pl.BlockSpec((t_len, TILE), lambda s, j: (s, 2 * ng + conv_j(j))),
        pl.BlockSpec((t_len, TILE), lambda s, j: (s, 3 * ng + conv_j(j))),
        pl.BlockSpec((None, TILE, TILE), lambda s, j: (pool_j(j), 0, 0)),
        pl.BlockSpec((None, 1, TILE), lambda s, j: (pool_j(j), 0, 0)),
        pl.BlockSpec((3, TILE), lambda s, j: (0, conv_j(j))),
        pl.BlockSpec((1, TILE), lambda s, j: (0, conv_j(j))),
    ]
    res, rode = _call(
        body, name="ab_mixer_fwd", grid=(nseq, 2 * ng), in_specs=in_specs,
        out_specs=[pl.BlockSpec((t_len, TILE), lambda s, j: (s, j))],
        out_shape=[jax.ShapeDtypeStruct((m, 2 * ng * TILE), BF16)], scratch_shapes=[],
        semantics=("parallel", "arbitrary"), args=[p, p, p, p, pool_w, pool_scale, conv_w, conv_b], rider=rider)
    return res[0] if rider is None else (res[0], rode)


def _ab_bwd(p, dmix, pool_w, pool_scale, conv_w, conv_b, nseq, t_len, rider=None):
    m = p.shape[0]
    ng = 4

    def body(a_ref, xb_ref, gb_ref, gc_ref, dma_ref, dmb_ref, pw_ref, ps_ref, cw_ref, cb_ref,
             da_ref, dxb_ref, dgb_ref, dgc_ref, dpw_ref, dps_ref, dcw_ref, dcb_ref):
        j = pl.program_id(0)
        first = pl.program_id(1) == 0
        t_idx = lax.broadcasted_iota(jnp.int32, (t_len, TILE), 0)

        pooled = _pool_fwd_math(a_ref[...].astype(F32), j, t_idx).astype(BF16)
        w_bf = pw_ref[...].astype(BF16)
        mixed = jnp.dot(pooled, w_bf, preferred_element_type=F32)
        dm = dma_ref[...].astype(F32)
        dps = jnp.sum(dm * mixed, axis=0, keepdims=True)
        dmixed = (dm * ps_ref[...]).astype(BF16)
        dpw = lax.dot_general(pooled, dmixed, TN_DIMS, preferred_element_type=F32)
        dpooled = lax.dot_general(dmixed, w_bf, NT_DIMS, preferred_element_type=F32)
        da_ref[...] = _pool_bwd_math(dpooled, j, t_idx, t_len).astype(BF16)

        xb = xb_ref[...].astype(F32)
        gb = gb_ref[...].astype(F32)
        gc = gc_ref[...].astype(F32)
        d = dmb_ref[...].astype(F32)
        c = gc * xb
        c1 = _shift_down(c, 1, t_idx)
        c2 = _shift_down(c, 2, t_idx)
        y = cw_ref[0:1, :] * c2 + cw_ref[1:2, :] * c1 + cw_ref[2:3, :] * c + cb_ref[...]
        dgb_ref[...] = (d * y).astype(BF16)
        dy = d * gb
        dc = (cw_ref[2:3, :] * dy + cw_ref[1:2, :] * _shift_up(dy, 1, t_idx, t_len)
              + cw_ref[0:1, :] * _shift_up(dy, 2, t_idx, t_len))
        dgc_ref[...] = (dc * xb).astype(BF16)
        dxb_ref[...] = (dc * gc).astype(BF16)
        dcw = jnp.concatenate([jnp.sum(dy * c2, axis=0, keepdims=True),
                               jnp.sum(dy * c1, axis=0, keepdims=True),
                               jnp.sum(dy * c, axis=0, keepdims=True)], axis=0)
        dcb = jnp.sum(dy, axis=0, keepdims=True)

        @pl.when(first)
        def _():
            dpw_ref[...] = dpw
            dps_ref[...] = dps
            dcw_ref[...] = dcw
            dcb_ref[...] = dcb

        @pl.when(jnp.logical_not(first))
        def _():
            dpw_ref[...] += dpw
            dps_ref[...] += dps
            dcw_ref[...] += dcw
            dcb_ref[...] += dcb

    def col(k):
        return pl.BlockSpec((t_len, TILE), lambda j, s: (s, k * ng + j))

    in_specs = [
        col(0), col(1), col(2), col(3), col(0), col(1),
        pl.BlockSpec((None, TILE, TILE), lambda j, s: (j, 0, 0)),
        pl.BlockSpec((None, 1, TILE), lambda j, s: (j, 0, 0)),
        pl.BlockSpec((3, TILE), lambda j, s: (0, j)),
        pl.BlockSpec((1, TILE), lambda j, s: (0, j)),
    ]
    piece = pl.BlockSpec((t_len, TILE), lambda j, s: (s, j))
    out_specs = [
        piece, piece, piece, piece,
        pl.BlockSpec((None, TILE, TILE), lambda j, s: (j, 0, 0)),
        pl.BlockSpec((None, 1, TILE), lambda j, s: (j, 0, 0)),
        pl.BlockSpec((3, TILE), lambda j, s: (0, j)),
        pl.BlockSpec((1, TILE), lambda j, s: (0, j)),
    ]
    w = ng * TILE
    out_shape = [jax.ShapeDtypeStruct((m, w), BF16)] * 4 + [
        jax.ShapeDtypeStruct((ng, TILE, TILE), F32), jax.ShapeDtypeStruct((ng, 1, TILE), F32),
        jax.ShapeDtypeStruct((3, w), F32), jax.ShapeDtypeStruct((1, w), F32)]
    res, rode = _call(
        body, name="ab_mixer_bwd", grid=(ng, nseq), in_specs=in_specs, out_specs=out_specs, out_shape=out_shape,
        scratch_shapes=[], semantics=("parallel", "arbitrary"),
        args=[p, p, p, p, dmix, dmix, pool_w, pool_scale, conv_w, conv_b], rider=rider)
    return res if rider is None else (res, rode)


SGU_ROWS = 512
INV_SQRT2 = 1.0 / math.sqrt(2.0)
INV_SQRT_2PI = 1.0 / math.sqrt(2.0 * math.pi)


def _gelu(x):
    return 0.5 * x * (1.0 + lax.erf(x * INV_SQRT2))


def _gelu_grad(x):
    return 0.5 * (1.0 + lax.erf(x * INV_SQRT2)) + x * (INV_SQRT_2PI * jnp.exp(-0.5 * x * x))


def _causal_tile(transposed=False):
    r = lax.broadcasted_iota(jnp.int32, (TILE, TILE), 0)
    c = lax.broadcasted_iota(jnp.int32, (TILE, TILE), 1)
    return r <= c if transposed else c <= r


def _sgu_norm(v, g_ref, b_ref):
    mu = jnp.mean(v, axis=-1, keepdims=True)
    xc = v - mu
    rstd = lax.rsqrt(jnp.mean(xc * xc, axis=-1, keepdims=True) + EPS)
    xhat = xc * rstd
    return xhat, rstd, xhat * g_ref[...] + b_ref[...]


def _sgu_fwd(p, norm_g, norm_b, w_s, bias_tile):
    m = p.shape[0]
    ng = 4
    width = ng * TILE

    def body(u_ref, v_ref, g_ref, b_ref, w_ref, bias_ref, o_ref):
        u = _gelu(u_ref[...].astype(F32))
        _, _, vln = _sgu_norm(_gelu(v_ref[...].astype(F32)), g_ref, b_ref)
        vln = vln.astype(BF16)
        causal = _causal_tile()
        for g in range(ng):
            cols = slice(g * TILE, (g + 1) * TILE)
            wg = jnp.where(causal, w_ref[g], 0.0).astype(BF16)
            for n in range(SGU_ROWS // TILE):
                rows = slice(n * TILE, (n + 1) * TILE)
                s = jnp.dot(wg, vln[rows, cols], preferred_element_type=F32) + bias_ref[g]
                o_ref[rows, cols] = (u[rows, cols] * s).astype(BF16)

    vec = pl.BlockSpec((1, width), lambda i: (0, 0))
    tiles = pl.BlockSpec((ng, TILE, TILE), lambda i: (0, 0, 0))
    return pl.pallas_call(
        body, name="sgu_fwd", grid=(m // SGU_ROWS,),
        in_specs=[pl.BlockSpec((SGU_ROWS, width), lambda i: (i, 0)),
                  pl.BlockSpec((SGU_ROWS, width), lambda i: (i, 1)), vec, vec, tiles, tiles],
        out_specs=pl.BlockSpec((SGU_ROWS, width), lambda i: (i, 0)),
        out_shape=jax.ShapeDtypeStruct((m, 2 * width), BF16),
        compiler_params=_params(("parallel",)),
    )(p, p, norm_g, norm_b, w_s, bias_tile)


def _sgu_bwd(p, dmix, norm_g, norm_b, w_s, w_s_t, bias_tile):
    m = p.shape[0]
    ng = 4
    width = ng * TILE

    def body(u_ref, v_ref, dc_ref, g_ref, b_ref, w_ref, wt_ref, bias_ref,
             du_ref, dv_ref, dw_ref, dbs_ref, dg_ref, db_ref, ds_scr, dvln_scr):
        u_pre = u_ref[...].astype(F32)
        v_pre = v_ref[...].astype(F32)
        u = _gelu(u_pre)
        xhat, rstd, vln = _sgu_norm(_gelu(v_pre), g_ref, b_ref)
        vln = vln.astype(BF16)
        dc = dc_ref[...].astype(F32)
        causal = _causal_tile()
        ones = jnp.ones((TILE, TILE), BF16)
        first = pl.program_id(0) == 0
        for g in range(ng):
            cols = slice(g * TILE, (g + 1) * TILE)
            wg = jnp.where(causal, w_ref[g], 0.0).astype(BF16)
            wgt = jnp.where(_causal_tile(transposed=True), wt_ref[g], 0.0).astype(BF16)
            dw_acc = jnp.zeros((TILE, TILE), F32)
            dbs_acc = jnp.zeros((TILE, TILE), F32)
            for n in range(SGU_ROWS // TILE):
                rows = slice(n * TILE, (n + 1) * TILE)
                vt = vln[rows, cols]
                s = jnp.dot(wg, vt, preferred_element_type=F32) + bias_ref[g]
                ds_scr[rows, cols] = dc[rows, cols] * s
                ds = (dc[rows, cols] * u[rows, cols]).astype(BF16)
                dw_acc += lax.dot_general(ds, vt, NT_DIMS, preferred_element_type=F32)
                dbs_acc += jnp.dot(ds, ones, preferred_element_type=F32)
                dvln_scr[rows, cols] = jnp.dot(wgt, ds, preferred_element_type=F32)
            dw_g = jnp.where(causal, dw_acc, 0.0)

            @pl.when(first)
            def _():
                dw_ref[g] = dw_g
                dbs_ref[g] = dbs_acc

            @pl.when(jnp.logical_not(first))
            def _():
                dw_ref[g] += dw_g
                dbs_ref[g] += dbs_acc

        du_ref[...] = (ds_scr[...] * _gelu_grad(u_pre)).astype(BF16)
        dvln = dvln_scr[...]
        dxhat = dvln * g_ref[...]
        dv = rstd * (dxhat - jnp.mean(dxhat, axis=-1, keepdims=True)
                     - xhat * jnp.mean(dxhat * xhat, axis=-1, keepdims=True))
        dv_ref[...] = (dv * _gelu_grad(v_pre)).astype(BF16)
        dg_part = jnp.sum(dvln * xhat, axis=0, keepdims=True)
        db_part = jnp.sum(dvln, axis=0, keepdims=True)

        @pl.when(first)
        def _():
            dg_ref[...] = dg_part
            db_ref[...] = db_part

        @pl.when(jnp.logical_not(first))
        def _():
            dg_ref[...] += dg_part
            db_ref[...] += db_part

    vec = pl.BlockSpec((1, width), lambda i: (0, 0))
    tiles = pl.BlockSpec((ng, TILE, TILE), lambda i: (0, 0, 0))
    rows0 = pl.BlockSpec((SGU_ROWS, width), lambda i: (i, 0))
    rows1 = pl.BlockSpec((SGU_ROWS, width), lambda i: (i, 1))
    return pl.pallas_call(
        body, name="sgu_bwd", grid=(m // SGU_ROWS,),
        in_specs=[rows0, rows1, rows0, vec, vec, tiles, tiles, tiles],
        out_specs=[rows0, rows0, tiles, tiles, vec, vec],
        out_shape=[jax.ShapeDtypeStruct((m, width), BF16), jax.ShapeDtypeStruct((m, width), BF16),
                   jax.ShapeDtypeStruct((ng, TILE, TILE), F32), jax.ShapeDtypeStruct((ng, TILE, TILE), F32),
                   jax.ShapeDtypeStruct((1, width), F32), jax.ShapeDtypeStruct((1, width), F32)],
        scratch_shapes=[pltpu.VMEM((SGU_ROWS, width), F32), pltpu.VMEM((SGU_ROWS, width), F32)],
        compiler_params=_params(("arbitrary",)),
    )(p, p, dmix, norm_g, norm_b, w_s, w_s_t, bias_tile)


SB_DH = 64
SB_SCALE = 1.0 / math.sqrt(SB_DH)


SB_BLOCK = 256
SB_SUB = SB_BLOCK // TILE
SB_PASS = 4


def _split_passes(i):
    rem = i % SB_PASS
    return i // SB_PASS, rem >= 2, rem % 2 == 1


def _sum_matrix(kind):
    j = lax.broadcasted_iota(jnp.int32, (TILE, 2 * TILE), 0)
    s = lax.broadcasted_iota(jnp.int32, (TILE, 2 * TILE), 1)
    tri = {"after": j > s, "upto": j <= s, "before": j < s}[kind]
    return jnp.where(jnp.logical_or(s >= TILE, tri), 1.0, 0.0).astype(BF16)


def _strict_mask():
    r = lax.broadcasted_iota(jnp.int32, (SB_BLOCK, SB_BLOCK), 0)
    c = lax.broadcasted_iota(jnp.int32, (SB_BLOCK, SB_BLOCK), 1)
    return c < r


def _head_lanes(h):
    lane = lax.broadcasted_iota(jnp.int32, (1, TILE), 1)
    return (lane >= h * SB_DH) & (lane < (h + 1) * SB_DH)


def _log_gates(z):
    log_sig = jnp.minimum(z, 0.0) - jnp.log(1.0 + jnp.exp(-jnp.abs(z)))
    return log_sig, log_sig - z


def _sb_fwd(p, mix, nseq, t_len, gather):
    m = p.shape[0]
    npair = 4
    ng = len(gather)
    last_step = nseq * npair - 1

    def body(q_ref, k_ref, v_ref, *rest):
        o_ref, lt_ref = rest[ng + 1:ng + 3]
        kh_ref, vh_ref = rest[2 * ng + 3:2 * ng + 5]
        step = pl.program_id(0) * npair + pl.program_id(1)
        send, forward, finish = _gather_steps(rest[ng + 3:2 * ng + 3], *rest[2 * ng + 5:])
        pl.when(step == 0)(send)
        pl.when(step == (last_step + 1) // 2)(forward)
        for h in range(2):
            keep = _head_lanes(h)
            kh_ref[h] = jnp.where(keep, k_ref[...], 0).astype(BF16)
            vh_ref[h] = jnp.where(keep, v_ref[...], 0).astype(BF16)
        summat = _sum_matrix("after")
        strict = _strict_mask()

        def one_pass(q, row0, nsub, diag, state):
            rows = pl.ds(row0, nsub * TILE)
            log_sig, pieces = [], []
            for h in range(2):
                zh = lax.dot_general(q, kh_ref[h, rows, :], NT_DIMS, preferred_element_type=F32)
                log_sig_h, logkeep = _log_gates(zh)
                if diag:
                    logkeep = jnp.where(strict, logkeep, 0.0)
                log_sig.append(log_sig_h)
                pieces += [logkeep[:, b * TILE:(b + 1) * TILE] for b in range(nsub)]
            sums = jnp.dot(jnp.concatenate(pieces, axis=0).astype(BF16), summat, preferred_element_type=F32)
            out = []
            for h in range(2):
                carry, acc = state[2 * h], state[2 * h + 1]
                after = [None] * nsub
                for b in reversed(range(nsub)):
                    part = sums[(h * nsub + b) * SB_BLOCK:(h * nsub + b + 1) * SB_BLOCK]
                    after[b] = part[:, :TILE] + carry
                    carry = carry + part[:, TILE:]
                w = jnp.exp(log_sig[h] + jnp.concatenate(after, axis=1))
                if diag:
                    w = jnp.where(strict, w, 0.0)
                out += [carry, acc + jnp.dot(w.astype(BF16), vh_ref[h, rows, :], preferred_element_type=F32)]
            return tuple(out)

        def q_block(i, _):
            r0 = pl.multiple_of(i * SB_BLOCK, SB_BLOCK)
            q = q_ref[pl.ds(r0, SB_BLOCK), :] * SB_SCALE
            zero = jnp.zeros((SB_BLOCK, TILE), F32)
            state = one_pass(q, r0, SB_SUB, True, (zero,) * 4)
            full, two, one = _split_passes(i)
            state = lax.fori_loop(
                0, full,
                lambda jj, st: one_pass(q, pl.multiple_of((i - SB_PASS * (jj + 1)) * SB_BLOCK, SB_BLOCK),
                                        SB_PASS * SB_SUB, False, st),
                state)
            state = lax.cond(
                two, lambda st: one_pass(q, pl.multiple_of((i % 2) * SB_BLOCK, SB_BLOCK), 2 * SB_SUB, False, st),
                lambda st: st, state)
            state = lax.cond(one, lambda st: one_pass(q, 0, SB_SUB, False, st), lambda st: st, state)
            o_ref[pl.ds(r0, SB_BLOCK), :] = (state[1] + state[3]).astype(BF16)
            lt_ref[pl.ds(r0, SB_BLOCK), :] = jnp.where(_head_lanes(0), state[0], state[2])
            return 0

        lax.fori_loop(0, t_len // SB_BLOCK, q_block, 0)
        pl.when(step == last_step)(finish)

    def col(k):
        return pl.BlockSpec((t_len, TILE), lambda s, hp: (s, k * npair + hp))

    out = pl.BlockSpec((t_len, TILE), lambda s, hp: (s, hp))
    res = pl.pallas_call(
        body, name="stickbreak_fwd", grid=(nseq, npair), in_specs=[col(2), col(3), col(4)] + [ANY] * (ng + 1),
        out_specs=[pl.BlockSpec((t_len, TILE), lambda s, hp: (s, npair + hp)), out] + [ANY] * ng,
        out_shape=[jax.ShapeDtypeStruct(mix.shape, BF16), jax.ShapeDtypeStruct((m, npair * TILE), F32)]
        + [jax.ShapeDtypeStruct(b.shape, b.dtype) for b in gather],
        input_output_aliases={**{3 + a: 2 + a for a in range(ng)}, 3 + ng: 0},
        scratch_shapes=[pltpu.VMEM((2, t_len, TILE), BF16), pltpu.VMEM((2, t_len, TILE), BF16)] + _gather_sems(ng),
        compiler_params=pltpu.CompilerParams(dimension_semantics=("arbitrary", "arbitrary"),
                                             vmem_limit_bytes=VMEM_LIMIT_BYTES, has_side_effects=True),
    )(p, p, p, *gather, mix)
    return res[0], res[1], res[2:]


def _sb_bwd(p, dmix, ltot, nseq, t_len, exchange):
    m = p.shape[0]
    npair = 4
    ne = len(exchange)
    last_step = nseq * npair - 1

    def body(q_ref, k_ref, v_ref, do_ref, lt_ref, *rest):
        dq_ref, dk_ref, dv_ref = rest[ne:ne + 3]
        kh_ref, vh_ref, dk_acc, dv_acc = rest[2 * ne + 3:2 * ne + 7]
        step = pl.program_id(0) * npair + pl.program_id(1)
        send, finish = _exchange_steps(rest[:ne], rest[ne + 3:2 * ne + 3], *rest[2 * ne + 7:])
        pl.when(step == 0)(send)
        for h in range(2):
            keep = _head_lanes(h)
            kh_ref[h] = jnp.where(keep, k_ref[...], 0).astype(BF16)
            vh_ref[h] = jnp.where(keep, v_ref[...], 0).astype(BF16)
        dk_acc[...] = jnp.zeros_like(dk_acc)
        dv_acc[...] = jnp.zeros_like(dv_acc)
        sum_upto = _sum_matrix("upto")
        sum_before = _sum_matrix("before")
        strict = _strict_mask()
        lane = lax.broadcasted_iota(jnp.int32, (SB_BLOCK, TILE), 1)

        def running(x, matrix, start, nsub):
            pieces = [x[h][:, b * TILE:(b + 1) * TILE] for h in range(2) for b in range(nsub)]
            sums = jnp.dot(jnp.concatenate(pieces, axis=0).astype(BF16), matrix, preferred_element_type=F32)
            wide, ends = [], []
            for h in range(2):
                total, cols = start[h], []
                for b in range(nsub):
                    part = sums[(h * nsub + b) * SB_BLOCK:(h * nsub + b + 1) * SB_BLOCK]
                    cols.append(part[:, :TILE] + total)
                    total = total + part[:, TILE:]
                wide.append(jnp.concatenate(cols, axis=1))
                ends.append(total)
            return wide, ends

        def one_pass(q, do, qh, doh, ltot, row0, nsub, diag, state):
            rows = pl.ds(row0, nsub * TILE)
            log_sig, logkeep = [], []
            for h in range(2):
                zh = lax.dot_general(q, kh_ref[h, rows, :], NT_DIMS, preferred_element_type=F32)
                log_sig_h, logkeep_h = _log_gates(zh)
                log_sig.append(log_sig_h)
                logkeep.append(jnp.where(strict, logkeep_h, 0.0) if diag else logkeep_h)
            upto, sum_l = running(logkeep, sum_upto, [state[0], state[3]], nsub)
            w, g = [], []
            for h in range(2):
                wh = jnp.exp(log_sig[h] + (ltot[h] - upto[h]))
                if diag:
                    wh = jnp.where(strict, wh, 0.0)
                w.append(wh)
                g.append(wh * lax.dot_general(do, vh_ref[h, rows, :], NT_DIMS, preferred_element_type=F32))
            g_before, sum_g = running(g, sum_before, [state[1], state[4]], nsub)
            out, dk_new, dv_new = [], 0.0, 0.0
            for h in range(2):
                dz = g[h] - jnp.exp(log_sig[h]) * (g[h] + g_before[h])
                if diag:
                    dz = jnp.where(strict, dz, 0.0)
                dzb = dz.astype(BF16)
                dq = state[3 * h + 2] + jnp.dot(dzb, kh_ref[h, rows, :], preferred_element_type=F32)
                dk_new = dk_new + lax.dot_general(dzb, qh[h], TN_DIMS, preferred_element_type=F32)
                dv_new = dv_new + lax.dot_general(w[h].astype(BF16), doh[h], TN_DIMS, preferred_element_type=F32)
                out += [sum_l[h], sum_g[h], dq]
            dk_acc[rows, :] += dk_new
            dv_acc[rows, :] += dv_new
            return tuple(out)

        def q_block(i, _):
            r0 = pl.multiple_of(i * SB_BLOCK, SB_BLOCK)
            q = q_ref[pl.ds(r0, SB_BLOCK), :] * SB_SCALE
            do = do_ref[pl.ds(r0, SB_BLOCK), :]
            lt = lt_ref[pl.ds(r0, SB_BLOCK), :]
            qh, doh, ltot = [], [], []
            for h in range(2):
                keep = _head_lanes(h)
                qh.append(jnp.where(keep, q, 0).astype(BF16))
                doh.append(jnp.where(keep, do, 0).astype(BF16))
                ltot.append(jnp.sum(jnp.where(lane == h * SB_DH, lt, 0.0), axis=1, keepdims=True))
            zero = jnp.zeros((SB_BLOCK, TILE), F32)
            full, two, one = _split_passes(i)
            state = lax.fori_loop(
                0, full,
                lambda jj, st: one_pass(q, do, qh, doh, ltot, pl.multiple_of(SB_PASS * jj * SB_BLOCK, SB_BLOCK),
                                        SB_PASS * SB_SUB, False, st),
                (zero,) * 6)
            state = lax.cond(
                two,
                lambda st: one_pass(q, do, qh, doh, ltot, pl.multiple_of(SB_PASS * full * SB_BLOCK, SB_BLOCK),
                                    2 * SB_SUB, False, st),
                lambda st: st, state)
            state = lax.cond(
                one,
                lambda st: one_pass(q, do, qh, doh, ltot, pl.multiple_of((i - 1) * SB_BLOCK, SB_BLOCK), SB_SUB, False, st),
                lambda st: st, state)
            state = one_pass(q, do, qh, doh, ltot, r0, SB_SUB, True, state)
            dq_ref[pl.ds(r0, SB_BLOCK), :] = ((state[2] + state[5]) * SB_SCALE).astype(BF16)
            return 0

        lax.fori_loop(0, t_len // SB_BLOCK, q_block, 0)
        dk_ref[...] = dk_acc[...].astype(BF16)
        dv_ref[...] = dv_acc[...].astype(BF16)
        pl.when(step == last_step)(finish)

    def col(k):
        return pl.BlockSpec((t_len, TILE), lambda s, hp: (s, k * npair + hp))

    out = pl.BlockSpec((t_len, TILE), lambda s, hp: (s, hp))
    width = npair * TILE
    res = pl.pallas_call(
        body, name="stickbreak_bwd", grid=(nseq, npair),
        in_specs=[col(2), col(3), col(4), col(1), out] + [ANY] * ne, out_specs=[out, out, out] + [ANY] * ne,
        out_shape=[jax.ShapeDtypeStruct((m, width), BF16)] * 3 + _exchange_shapes(exchange),
        scratch_shapes=[pltpu.VMEM((2, t_len, TILE), BF16), pltpu.VMEM((2, t_len, TILE), BF16),
                        pltpu.VMEM((t_len, TILE), F32), pltpu.VMEM((t_len, TILE), F32)] + _exchange_sems(ne),
        compiler_params=pltpu.CompilerParams(dimension_semantics=("arbitrary", "arbitrary"),
                                             vmem_limit_bytes=VMEM_LIMIT_BYTES, has_side_effects=True),
    )(p, p, p, dmix, ltot, *exchange)
    return res[0], res[1], res[2], res[3:]


def _adam_math(w, g, m, v):
    m = ADAM_B1 * m + (1.0 - ADAM_B1) * g
    v = ADAM_B2 * v + (1.0 - ADAM_B2) * (g * g)
    m_hat = m / (1.0 - ADAM_B1 ** ADAM_STEP)
    v_hat = v / (1.0 - ADAM_B2 ** ADAM_STEP)
    delta = -ADAM_LR * (m_hat / (jnp.sqrt(v_hat) + ADAM_EPS) + ADAM_WD * w)
    return delta, m, v


def _cast_place(w, layer, pos, *, name):
    _, r, c = w.shape
    tr = min(r, 256)

    def body(pos_ref, w_ref, o_ref):
        o_ref[...] = w_ref[...].astype(BF16)

    grid_spec = pltpu.PrefetchScalarGridSpec(
        num_scalar_prefetch=1, grid=(r // tr,),
        in_specs=[pl.BlockSpec((None, tr, c), lambda i, pos_ref: (layer, i, 0))],
        out_specs=pl.BlockSpec((None, None, tr, c), lambda i, pos_ref: (0, pos_ref[0], i, 0)))
    return pl.pallas_call(
        body, name=name, grid_spec=grid_spec, out_shape=jax.ShapeDtypeStruct((1, N_CHIP, r, c), BF16),
        compiler_params=_params(("parallel",)),
    )(pos, w)


def _cast_place_all(items, pos, *, name, rider=None):
    tiles = [min(w.shape[1], 256) for w, _ in items]
    counts = [w.shape[1] // t for (w, _), t in zip(items, tiles)]
    starts = [sum(counts[:a]) for a in range(len(items))]
    n = len(items)

    def body(pos_ref, *refs):
        i = pl.program_id(0)
        for a in range(n):
            @pl.when((i >= starts[a]) & (i < starts[a] + counts[a]))
            def _():
                refs[n + a][...] = refs[a][...].astype(BF16)

    def block(a):
        return lambda i: jnp.clip(i - starts[a], 0, counts[a] - 1)

    in_specs, out_specs, out_shape = [], [], []
    for a, ((w, layer), t) in enumerate(zip(items, tiles)):
        _, r, c = w.shape
        in_specs.append(pl.BlockSpec((None, t, c), lambda i, pos_ref, a=a, layer=layer: (layer, block(a)(i), 0)))
        out_specs.append(pl.BlockSpec((None, None, t, c), lambda i, pos_ref, a=a: (0, pos_ref[0], block(a)(i), 0)))
        out_shape.append(jax.ShapeDtypeStruct((1, N_CHIP, r, c), BF16))
    res, rode = _call(body, name=name, grid=(sum(counts),), in_specs=in_specs, out_specs=out_specs,
                      out_shape=out_shape, scratch_shapes=[], semantics=("arbitrary",),
                      args=[w for w, _ in items], rider=rider, prefetch=pos)
    return res if rider is None else (res, rode)


def _pair_sum(mine, got, pos, *, name):
    l_dim, s_dim, h, c = got.shape
    th = min(h, 512)
    nt = h // th

    def body(pos_ref, a_ref, b_ref, o_ref):
        o_ref[...] = (a_ref[...].astype(F32) + b_ref[...].astype(F32)).astype(BF16)

    spec = pl.BlockSpec((None, None, th, c), lambda l, s, i, pos_ref: (l, s, i, 0))
    grid_spec = pltpu.PrefetchScalarGridSpec(
        num_scalar_prefetch=1, grid=(l_dim, s_dim, nt),
        in_specs=[pl.BlockSpec((None, None, th, c), lambda l, s, i, pos_ref: (l, s, pos_ref[1] * nt + i, 0)), spec],
        out_specs=spec)
    return pl.pallas_call(
        body, name=name, grid_spec=grid_spec, out_shape=jax.ShapeDtypeStruct(got.shape, BF16),
        compiler_params=_params(("parallel",) * 3),
    )(pos, mine, got)


def _chip_sum(sums, landed, pos, *, name):
    l_dim, _, h, c = sums.shape
    th = min(h, 512)
    nt = h // th

    def body(pos_ref, own, r0, r1, r2, o_ref):
        o_ref[...] = ((own[...].astype(F32) + r0[...].astype(F32)) + r1[...].astype(F32)) + r2[...].astype(F32)

    def piece(k):
        return pl.BlockSpec((None, None, th, c), lambda l, i, pos_ref: (l, k, i, 0))

    grid_spec = pltpu.PrefetchScalarGridSpec(
        num_scalar_prefetch=1, grid=(l_dim, nt),
        in_specs=[pl.BlockSpec((None, None, th, c), lambda l, i, pos_ref: (l, pos_ref[0], i, 0)),
                  piece(0), piece(1), piece(2)],
        out_specs=pl.BlockSpec((None, th, c), lambda l, i, pos_ref: (l, pos_ref[1] * nt + i, 0)))
    return pl.pallas_call(
        body, name=name, grid_spec=grid_spec, out_shape=jax.ShapeDtypeStruct((l_dim, 2 * h, c), F32),
        compiler_params=_params(("parallel",) * 2),
    )(pos, sums, landed, landed, landed)


def _adam_big(w, m, v, grads, *, name):
    l_dim, r, c = w.shape
    assert len(grads) == l_dim
    tr = min(r, 512)

    def body(*refs):
        w_ref, m_ref, v_ref = refs[:3]
        g_refs = refs[3:3 + l_dim]
        go_ref, d_ref, mo_ref, vo_ref = refs[3 + l_dim:]
        g = g_refs[0][...]
        for l in range(1, l_dim):
            g = jnp.where(pl.program_id(0) == l, g_refs[l][...], g)
        delta, m_new, v_new = _adam_math(w_ref[...], g, m_ref[...], v_ref[...])
        go_ref[...] = g
        d_ref[...] = delta
        mo_ref[...] = m_new
        vo_ref[...] = v_new

    spec = pl.BlockSpec((None, tr, c), lambda l, i: (l, i, 0))
    gspec = pl.BlockSpec((None, tr, c), lambda l, i: (0, i, 0))
    return pl.pallas_call(
        body, name=name, grid=(l_dim, r // tr), in_specs=[spec] * 3 + [gspec] * l_dim, out_specs=[spec] * 4,
        out_shape=[jax.ShapeDtypeStruct(w.shape, F32)] * 4, compiler_params=_params(("parallel",) * 2),
    )(w, m, v, *grads)


def _position():
    return lax.axis_index("x"), lax.axis_index("y"), lax.axis_index("c")


def _other_chips(x, y):
    return [(1 - x, y), (x, 1 - y), (1 - x, 1 - y)]


def _remote(src, dst, send_sem, recv_sem, device):
    return pltpu.make_async_remote_copy(src_ref=src, dst_ref=dst, send_sem=send_sem, recv_sem=recv_sem,
                                        device_id=device, device_id_type=MESH)


ANY = pl.BlockSpec(memory_space=pl.ANY)


def _gather_sems(n):
    return [pltpu.SemaphoreType.DMA((3 * n,))] * 4


def _gather_steps(outs, send_sems, recv_sems, fwd_send, fwd_recv):
    n = len(outs)
    x, y, c = _position()
    chips = _other_chips(x, y)
    sibling = (x, y, 1 - c)

    def half(a, chip, core):
        h = outs[a].shape[2] // 2
        return outs[a].at[:, 2 * chip[0] + chip[1], pl.ds(core * h, h), :]

    def over_ici(a, k, chip):
        block = half(a, chip, c)
        return _remote(block, block, send_sems.at[3 * a + k], recv_sems.at[3 * a + k], (*chips[k], c))

    def over_d2d(a, k, core):
        block = half(a, chips[k], core)
        return _remote(block, block, fwd_send.at[3 * a + k], fwd_recv.at[3 * a + k], sibling)

    def send():
        for a in range(n):
            for k in range(3):
                over_ici(a, k, (x, y)).start()

    def forward():
        for k in range(3):
            for a in range(n):
                over_ici(a, k, chips[k]).wait_recv()
                over_d2d(a, k, c).start()

    def finish():
        for k in range(3):
            for a in range(n):
                over_d2d(a, k, 1 - c).wait_recv()
        for a in range(n):
            for k in range(3):
                over_ici(a, k, (x, y)).wait_send()
                over_d2d(a, k, c).wait_send()

    return send, forward, finish


def _swap_halves(grads, *, name):
    n = len(grads)

    def body(*refs):
        send, finish = _swap_steps(refs[:n], refs[n:2 * n], *refs[2 * n:])
        send()
        finish()

    sem = pltpu.SemaphoreType.DMA((n,))
    return pl.pallas_call(
        body, name=name, in_specs=[ANY] * n, out_specs=[ANY] * n, out_shape=_swap_shapes(grads),
        scratch_shapes=[sem, sem], compiler_params=pltpu.CompilerParams(has_side_effects=True),
    )(*grads)


def _swap_shapes(grads):
    return [jax.ShapeDtypeStruct(g.shape[:2] + (g.shape[2] // 2, g.shape[3]), g.dtype) for g in grads]


def _swap_steps(ins, outs, send_sems, recv_sems):
    x, y, c = _position()

    def copy(a):
        h = ins[a].shape[2] // 2
        return _remote(ins[a].at[:, :, pl.ds((1 - c) * h, h), :], outs[a], send_sems.at[a], recv_sems.at[a],
                       (x, y, 1 - c))

    def send():
        for a in range(len(ins)):
            copy(a).start()

    def finish():
        for a in range(len(ins)):
            copy(a).wait()

    return send, finish


def _exchange_shapes(sums):
    return [jax.ShapeDtypeStruct((s.shape[0], 3) + s.shape[2:], s.dtype) for s in sums]


def _exchange_sems(n):
    return [pltpu.SemaphoreType.DMA((3 * n,))] * 2


def _exchange_steps(ins, outs, send_sems, recv_sems):
    n = len(ins)
    x, y, c = _position()
    chips = _other_chips(x, y)

    def copy(a, k):
        chip = chips[k]
        return _remote(ins[a].at[:, 2 * chip[0] + chip[1]], outs[a].at[:, k],
                       send_sems.at[3 * a + k], recv_sems.at[3 * a + k], (*chip, c))

    def send():
        for a in range(n):
            for k in range(3):
                copy(a, k).start()

    def finish():
        for a in range(n):
            for k in range(3):
                copy(a, k).wait()

    return send, finish


def _join_halves(bufs, *, name):
    n = len(bufs)

    def body(*refs):
        send, finish = _join_steps(refs[n:2 * n], *refs[2 * n:])
        send()
        finish()

    sem = pltpu.SemaphoreType.DMA((n,))
    return pl.pallas_call(
        body, name=name, in_specs=[ANY] * n, out_specs=[ANY] * n,
        out_shape=[jax.ShapeDtypeStruct(b.shape, b.dtype) for b in bufs],
        input_output_aliases={a: a for a in range(n)},
        scratch_shapes=[sem, sem], compiler_params=pltpu.CompilerParams(has_side_effects=True),
    )(*bufs)


def _join_steps(outs, send_sems, recv_sems):
    x, y, c = _position()

    def copy(a, core):
        h = outs[a].shape[1] // 2
        half = outs[a].at[:, pl.ds(core * h, h), :]
        return _remote(half, half, send_sems.at[a], recv_sems.at[a], (x, y, 1 - c))

    def send():
        for a in range(len(outs)):
            copy(a, c).start()

    def finish():
        for a in range(len(outs)):
            copy(a, c).wait_send()
            copy(a, 1 - c).wait_recv()

    return send, finish


def _allgather_steps(ins, outs, send_sems, recv_sems, local_sems):
    n = len(ins)
    x, y, c = _position()
    me, sibling = (x, y, c), (x, y, 1 - c)
    chips = _other_chips(x, y)

    def slot(a, dev):
        return outs[a].at[4 * dev[0] + 2 * dev[1] + dev[2]]

    def copy(a, k, block, to, own=False):
        return _remote(ins[a] if own else slot(a, block), slot(a, block),
                       send_sems.at[7 * a + k], recv_sems.at[7 * a + k], to)

    def first(a):
        return [copy(a, 0, me, sibling, own=True)] + [copy(a, 1 + k, me, (*chips[k], c), own=True) for k in range(3)]

    def local(a):
        return pltpu.make_async_copy(ins[a], slot(a, me), local_sems.at[a])

    def send():
        for a in range(n):
            local(a).start()
            for cp in first(a):
                cp.start()

    def forward():
        for a in range(n):
            for k in range(3):
                copy(a, 1 + k, (*chips[k], c), me).wait_recv()
                copy(a, 4 + k, (*chips[k], c), sibling).start()

    def finish():
        for a in range(n):
            copy(a, 0, sibling, me).wait_recv()
            for k in range(3):
                copy(a, 4 + k, (*chips[k], 1 - c), me).wait_recv()
        for a in range(n):
            for cp in first(a) + [copy(a, 4 + k, (*chips[k], c), sibling) for k in range(3)]:
                cp.wait_send()
            local(a).wait()

    return send, forward, finish


def _allreduce_small(packs):
    n = len(packs)

    def body(*refs):
        ins, outs, gath = refs[:n], refs[n:2 * n], refs[2 * n:3 * n]
        send_sems, recv_sems = refs[3 * n:]
        x, y, c = _position()
        me, sibling = (x, y, c), (x, y, 1 - c)
        chips = _other_chips(x, y)

        def slot(a, dev):
            return gath[a].at[4 * dev[0] + 2 * dev[1] + dev[2]]

        def copy(a, k, block, to, src=None):
            return _remote(slot(a, block) if src is None else src, slot(a, block),
                           send_sems.at[7 * a + k], recv_sems.at[7 * a + k], to)

        started = []
        for a in range(n):
            slot(a, me)[...] = ins[a][...]
            first = [copy(a, 0, me, sibling, src=ins[a])]
            first += [copy(a, 1 + k, me, (*chip, c), src=ins[a]) for k, chip in enumerate(chips)]
            for cp in first:
                cp.start()
            started += first
        for a in range(n):
            for k, chip in enumerate(chips):
                copy(a, 1 + k, (*chip, c), me).wait_recv()
                cp = copy(a, 4 + k, (*chip, c), sibling)
                cp.start()
                started.append(cp)
        for a in range(n):
            copy(a, 0, sibling, me).wait_recv()
            for k, chip in enumerate(chips):
                copy(a, 4 + k, (*chip, 1 - c), me).wait_recv()
        for cp in started:
            cp.wait_send()
        for a in range(n):
            total = gath[a][0]
            for d in range(1, N_DEV):
                total = total + gath[a][d]
            outs[a][...] = total

    vmem = pl.BlockSpec(memory_space=pltpu.VMEM)
    sem = pltpu.SemaphoreType.DMA((7 * n,))
    return pl.pallas_call(
        body, name="allreduce_small", in_specs=[vmem] * n, out_specs=[vmem] * n,
        out_shape=[jax.ShapeDtypeStruct(p.shape, p.dtype) for p in packs],
        scratch_shapes=[pltpu.VMEM((N_DEV,) + p.shape, p.dtype) for p in packs] + [sem, sem],
        compiler_params=pltpu.CompilerParams(has_side_effects=True, vmem_limit_bytes=VMEM_LIMIT_BYTES),
    )(*packs)


LOSS_ROW = 520


def _pad_rows(a, rows=8):
    return jnp.concatenate([a, jnp.zeros((rows - a.shape[0], a.shape[1]), a.dtype)], axis=0)

def _adam_small(wide, mid, sgu, pool, late, params):
    names = ["mix_norm_g", "mlp_norm_g", "final_norm_g", "conv_b", "conv_w", "sgu_norm_g", "sgu_norm_b",
             "pool_w", "pool_scale", "sgu_w", "sgu_b"]
    n = len(names)

    def body(*refs):
        wmv = refs[5:5 + 3 * n]
        outs = refs[5 + 3 * n:]
        x, y, _ = _position()
        q = 2 * x + y

        def total(ref):
            t = ref[0]
            for dev in range(1, N_DEV):
                t = t + ref[dev]
            return t

        wide_sum, mid_sum, sgu_sum, pool_sum = total(refs[0]), total(refs[1]), total(refs[2]), total(refs[3])
        late_ref = refs[4]

        def my_quarter(rows):
            parts = [rows[:, s * TILE:(s + 1) * TILE] for s in range(N_CHIP)]
            return jnp.where(q == 0, parts[0], jnp.where(q == 1, parts[1], jnp.where(q == 2, parts[2], parts[3])))

        def tiles(pack):
            return [((0, g), pack[g * TILE:(g + 1) * TILE, :]) for g in range(4)]

        grads = {
            "mix_norm_g": [((), wide_sum[0:2, :] + late_ref[0:2, :])],
            "mlp_norm_g": [((), wide_sum[8:10, :])],
            "final_norm_g": [((), wide_sum[16:17, :])],
            "conv_b": [((), mid_sum[0:1, :])],
            "conv_w": [((0,), my_quarter(mid_sum[8:11, :]))],
            "sgu_norm_g": [((), my_quarter(mid_sum[16:17, :]))],
            "sgu_norm_b": [((), my_quarter(mid_sum[24:25, :]))],
            "pool_w": tiles(pool_sum),
            "sgu_w": tiles(sgu_sum),
            "pool_scale": [((0,), pool_sum[512:516, :])],
            "sgu_b": [((0,), sgu_sum[512:516, :])],
        }
        outs[4 * n][...] = sgu_sum[LOSS_ROW:LOSS_ROW + 8, :]
        for i, name in enumerate(names):
            w_ref, m_ref, v_ref = wmv[3 * i:3 * i + 3]
            for lead, g in grads[name]:
                idx = lead + (slice(None), slice(None))
                delta, m_new, v_new = _adam_math(w_ref[idx], g, m_ref[idx], v_ref[idx])
                outs[4 * i][idx] = g
                outs[4 * i + 1][idx] = delta
                outs[4 * i + 2][idx] = m_new
                outs[4 * i + 3][idx] = v_new

    vmem = pl.BlockSpec(memory_space=pltpu.VMEM)
    args, out_shape = [wide, mid, sgu, pool, late], []
    for name in names:
        w, m, v = params[name]
        args += [w, m, v]
        out_shape += [jax.ShapeDtypeStruct(w.shape, F32)] * 4
    out_shape.append(jax.ShapeDtypeStruct((8, TILE), F32))
    res = pl.pallas_call(
        body, name="adam_small", in_specs=[vmem] * len(args), out_specs=[vmem] * len(out_shape),
        out_shape=out_shape, compiler_params=pltpu.CompilerParams(vmem_limit_bytes=VMEM_LIMIT_BYTES),
    )(*args)
    return {name: res[4 * i:4 * i + 4] for i, name in enumerate(names)}, res[4 * n]


def _pair_sums(grads, got, pos, tag):
    return [_pair_sum(a, b, pos, name=f"pair_sum_{tag}{i}") for i, (a, b) in enumerate(zip(grads, got))]


def _chip_sums(sums, landed, pos, tag):
    return [_chip_sum(s, r, pos, name=f"chip_sum_{tag}{i}") for i, (s, r) in enumerate(zip(sums, landed))]


def kernel(x, mix_norm_g, mlp_norm_g, ab_w_in, pool_w, pool_scale, conv_w, conv_b, ab_w_out, cd_w_in, sgu_norm_g, sgu_norm_b, sgu_w, sgu_b, cd_w_out, mlp_w1, mlp_w2, final_norm_g, loss_target, m_mix_norm_g, m_mlp_norm_g, m_ab_w_in, m_pool_w, m_pool_scale, m_conv_w, m_conv_b, m_ab_w_out, m_cd_w_in, m_sgu_norm_g, m_sgu_norm_b, m_sgu_w, m_sgu_b, m_cd_w_out, m_mlp_w1, m_mlp_w2, m_final_norm_g, v_mix_norm_g, v_mlp_norm_g, v_ab_w_in, v_pool_w, v_pool_scale, v_conv_w, v_conv_b, v_ab_w_out, v_cd_w_in, v_sgu_norm_g, v_sgu_norm_b, v_sgu_w, v_sgu_b, v_cd_w_out, v_mlp_w1, v_mlp_w2, v_final_norm_g):
    nseq, t_len, d = x.shape
    m_tok = nseq * t_len
    h0 = x.reshape(m_tok, d)
    target = loss_target.reshape(m_tok, d)

    x_idx, y_idx = lax.axis_index("x"), lax.axis_index("y")
    q_idx = 2 * x_idx + y_idx
    pos = jnp.stack([q_idx, lax.axis_index("c")]).astype(jnp.int32)

    def shard_buffer(w, layer, tag):
        return _cast_place(w, layer, pos, name=f"cast_place_{tag}")

    def row_block(w):
        return w.reshape(1, 1, -1, w.shape[-1])

    (buf_ab_out, buf_w1_0, buf_w2_0, buf_cd_in, *later_weights), ((w_ab_in,),) = _cast_place_all(
        [(ab_w_out, 0), (mlp_w1, 0), (mlp_w2, 0), (cd_w_in, 0), (cd_w_out, 0), (mlp_w1, 1), (mlp_w2, 1)], pos,
        name="cast_place_rest", rider=[("gather", [shard_buffer(ab_w_in, 0, "ab_in")])])

    pool_w3, pool_scale3 = pool_w[0], pool_scale[0].reshape(4, 1, TILE)
    sgu_w3 = sgu_w[0]
    sgu_w3_t = jnp.swapaxes(sgu_w3, 1, 2)
    sgu_bias_tile = jnp.broadcast_to(sgu_b[0][:, :, None], (4, TILE, TILE))
    conv_b2 = conv_b

    def place_quarter(v):
        return lax.dynamic_update_slice(jnp.zeros((v.shape[0], 4 * TILE), F32), v, (0, q_idx * TILE))

    sharded_small = jnp.concatenate(
        [place_quarter(conv_w[0]), place_quarter(sgu_norm_g), place_quarter(sgu_norm_b),
         jnp.zeros((3, 4 * TILE), F32)], axis=0)
    sharded_small, = _allreduce_small([sharded_small])
    sharded_small = sharded_small * 0.5
    conv_w_full = sharded_small[0:3]
    sgu_g_full = sharded_small[3:4]
    sgu_b_full = sharded_small[4:5]

    xn0 = _rms_fwd(h0, mix_norm_g[0:1], name="rms_fwd_mix0")
    p_ab, ((w_1_0,),) = _mm_nn(xn0, w_ab_in, 0, out_dtype=BF16, name="ab_in_proj",
                               rider=[("gather", [buf_w1_0])])
    mix0, ((w_ab_out,),) = _ab_fwd(p_ab, pool_w3, pool_scale3, conv_w_full, conv_b2, nseq, t_len,
                                   rider=[("gather", [buf_ab_out])])
    w_ab_out = row_block(w_ab_out)
    h1, hn0 = _mm_nn(mix0, w_ab_out, 0, out_dtype=F32, name="ab_out_proj", epilogue="residual", extra=h0,
                     norm_g=mlp_norm_g[0:1])
    (act0, relu0), ((w_2_0,),) = _mm_nn(hn0, w_1_0, 0, out_dtype=BF16, name="mlp0_up", epilogue="relu2",
                                        rider=[("gather", [buf_w2_0])])
    w_2_0 = row_block(w_2_0)
    (h2, xn1), ((w_cd_in,),) = _mm_nn(act0, w_2_0, 0, out_dtype=F32, name="mlp0_down", epilogue="residual", extra=h1,
                                      norm_g=mix_norm_g[1:2],
                                      rider=[("gather", [buf_cd_in])])

    p_cd = _mm_nn(xn1, w_cd_in, 0, out_dtype=BF16, name="cd_in_proj")
    mix1 = _sgu_fwd(p_cd, sgu_g_full, sgu_b_full, sgu_w3, sgu_bias_tile)
    mix1, ltot, (w_cd_out, w_1_1, w_2_1) = _sb_fwd(p_cd, mix1, nseq, t_len, later_weights)
    w_cd_out, w_2_1 = row_block(w_cd_out), row_block(w_2_1)
    h3, hn1 = _mm_nn(mix1, w_cd_out, 0, out_dtype=F32, name="cd_out_proj", epilogue="residual", extra=h2,
                     norm_g=mlp_norm_g[1:2])
    act1, relu1 = _mm_nn(hn1, w_1_1, 0, out_dtype=BF16, name="mlp1_up", epilogue="relu2")

    dh4, dh4_bf, dg_final, loss_tile = _mlp_down_loss(act1, w_2_1, h3, final_norm_g.reshape(1, d), target)

    def as_pieces(g):
        return g.reshape(1, N_CHIP, -1, g.shape[-1]) if g.shape[1] == 1 else g

    dz1 = _mm_nt(dh4_bf, w_2_1, 0, out_dtype=BF16, name="mlp1_down_bwd", epilogue="relu2_bwd", extra=relu1)
    g_w2_1 = as_pieces(_mm_tn(act1, dh4_bf, 1, name="mlp1_down_wgrad"))
    g_w1_1 = _mm_tn(hn1, dz1, N_CHIP, name="mlp1_up_wgrad")
    (dh3, dh3_bf, dg_mlp1), (got_a,) = _mm_nt(
        dz1, w_1_1, 0, out_dtype=F32, name="mlp1_up_bwd", epilogue="rms_bwd",
        extra=(h3, mlp_norm_g[1:2], dh4), rider=[("swap", [g_w1_1, g_w2_1])])

    g_cd_out = as_pieces(_mm_tn(mix1, dh3_bf, 1, name="cd_out_wgrad"))
    dmix1, (got_cd_out,) = _mm_nt(dh3_bf, w_cd_out, 0, out_dtype=BF16, name="cd_out_bwd",
                                  rider=[("swap", [g_cd_out])])
    sums_a = _pair_sums([g_w1_1, g_w2_1, g_cd_out], got_a + got_cd_out, pos, "a")
    du, dv, dsgu_w, dsgu_bs, dsgu_g, dsgu_b = _sgu_bwd(p_cd, dmix1, sgu_g_full, sgu_b_full, sgu_w3, sgu_w3_t,
                                                      sgu_bias_tile)
    dq, dk, dvv, landed_a = _sb_bwd(p_cd, dmix1, ltot, nseq, t_len, sums_a)
    halves_a = _chip_sums(sums_a, landed_a, pos, "a")
    dp_cd = jnp.concatenate([du, dv, dq, dk, dvv], axis=1)
    g_cd_in, ((r_w1_1, r_w2_1, r_cd_out),) = _mm_tn(xn1, dp_cd, N_CHIP, name="cd_in_wgrad",
                                                    rider=[("join", halves_a)])
    (dh2, dh2_bf, dg_mix1), (got_c,) = _mm_nt(
        dp_cd, w_cd_in, 0, out_dtype=F32, name="cd_in_bwd", epilogue="rms_bwd",
        extra=(h2, mix_norm_g[1:2], dh3), rider=[("swap", [g_cd_in])])

    sums_c = _pair_sums([g_cd_in], got_c, pos, "c")
    dz0, (landed_c,) = _mm_nt(dh2_bf, w_2_0, 0, out_dtype=BF16, name="mlp0_down_bwd", epilogue="relu2_bwd",
                              extra=relu0, rider=[("exchange", sums_c)])
    halves_c = _chip_sums(sums_c, landed_c, pos, "c")
    sgu_pack = jnp.concatenate([dsgu_w.reshape(4 * TILE, TILE), _pad_rows(dsgu_bs[:, :, 0]), loss_tile], axis=0)
    g_w2_0, ((r_cd_in,), (sgu_pack,)) = _mm_tn(act0, dh2_bf, 1, name="mlp0_down_wgrad",
                                               rider=[("join", halves_c), ("allgather", [sgu_pack])])
    g_w2_0 = as_pieces(g_w2_0)
    g_w1_0, (got_d,) = _mm_tn(hn0, dz0, N_CHIP, name="mlp0_up_wgrad", rider=[("swap", [g_w2_0])])
    sums_d = _pair_sums([g_w2_0], got_d, pos, "d")
    (dh1, dh1_bf, dg_mlp0), (landed_d, got_e) = _mm_nt(
        dz0, w_1_0, 0, out_dtype=F32, name="mlp0_up_bwd", epilogue="rms_bwd",
        extra=(h1, mlp_norm_g[0:1], dh2), rider=[("exchange", sums_d), ("swap", [g_w1_0])])
    halves_d = _chip_sums(sums_d, landed_d, pos, "d")
    sums_e = _pair_sums([g_w1_0], got_e, pos, "e")

    dmix0, ((r_w2_0,),) = _mm_nt(dh1_bf, w_ab_out, 0, out_dtype=BF16, name="ab_out_bwd", rider=[("join", halves_d)])
    g_ab_out = as_pieces(_mm_tn(mix0, dh1_bf, 1, name="ab_out_wgrad"))
    (da, dxb, dgb, dgc, dpool_w, dpool_scale, dconv_w, dconv_b), (landed_e, got_f) = _ab_bwd(
        p_ab, dmix0, pool_w3, pool_scale3, conv_w_full, conv_b2, nseq, t_len,
        rider=[("exchange", sums_e), ("swap", [g_ab_out])])
    halves_e = _chip_sums(sums_e, landed_e, pos, "e")
    sums_f = _pair_sums([g_ab_out], got_f, pos, "f")
    dp_ab = jnp.concatenate([da, dxb, dgb, dgc], axis=1)
    wide = jnp.concatenate([_pad_rows(jnp.concatenate([jnp.zeros_like(dg_mix1), dg_mix1], axis=0)),
                            _pad_rows(jnp.concatenate([dg_mlp0, dg_mlp1], axis=0)), _pad_rows(dg_final)], axis=0)
    mid = jnp.concatenate([_pad_rows(dconv_b), _pad_rows(dconv_w), _pad_rows(dsgu_g), _pad_rows(dsgu_b)], axis=0)
    pool_pack = jnp.concatenate([dpool_w.reshape(4 * TILE, TILE), _pad_rows(dpool_scale.reshape(4, TILE))], axis=0)
    g_ab_in, (landed_f, (r_w1_0,), (wide, mid, pool_pack)) = _mm_tn(
        xn0, dp_ab, N_CHIP, name="ab_in_wgrad",
        rider=[("exchange", sums_f), ("join", halves_e), ("allgather", [wide, mid, pool_pack])])
    halves_f = _chip_sums(sums_f, landed_f, pos, "f")
    sums_g = _pair_sums([g_ab_in], _swap_halves([g_ab_in], name="swap_halves_g"), pos, "g")
    (grad_x, _, dg_mix0), (landed_g, (r_ab_out,)) = _mm_nt(
        dp_ab, w_ab_in, 0, out_dtype=F32, name="ab_in_bwd", epilogue="rms_bwd",
        extra=(h0, mix_norm_g[0:1], dh1), rider=[("exchange", sums_g), ("join", halves_f)])
    r_ab_in, = _join_halves(_chip_sums(sums_g, landed_g, pos, "g"), name="join_halves_g")

    big_out = {
        "ab_w_in": _adam_big(ab_w_in, m_ab_w_in, v_ab_w_in, [r_ab_in], name="adam_ab_w_in"),
        "ab_w_out": _adam_big(ab_w_out, m_ab_w_out, v_ab_w_out, [r_ab_out], name="adam_ab_w_out"),
        "cd_w_in": _adam_big(cd_w_in, m_cd_w_in, v_cd_w_in, [r_cd_in], name="adam_cd_w_in"),
        "cd_w_out": _adam_big(cd_w_out, m_cd_w_out, v_cd_w_out, [r_cd_out], name="adam_cd_w_out"),
        "mlp_w1": _adam_big(mlp_w1, m_mlp_w1, v_mlp_w1, [r_w1_0, r_w1_1], name="adam_mlp_w1"),
        "mlp_w2": _adam_big(mlp_w2, m_mlp_w2, v_mlp_w2, [r_w2_0, r_w2_1], name="adam_mlp_w2"),
    }

    late, = _allreduce_small([_pad_rows(dg_mix0)])
    small_out, loss_sum = _adam_small(wide, mid, sgu_pack, pool_pack, late, {
        "mix_norm_g": (mix_norm_g, m_mix_norm_g, v_mix_norm_g),
        "mlp_norm_g": (mlp_norm_g, m_mlp_norm_g, v_mlp_norm_g),
        "final_norm_g": tuple(a.reshape(1, d) for a in (final_norm_g, m_final_norm_g, v_final_norm_g)),
        "conv_b": (conv_b, m_conv_b, v_conv_b),
        "conv_w": (conv_w, m_conv_w, v_conv_w),
        "sgu_norm_g": (sgu_norm_g, m_sgu_norm_g, v_sgu_norm_g),
        "sgu_norm_b": (sgu_norm_b, m_sgu_norm_b, v_sgu_norm_b),
        "pool_w": (pool_w, m_pool_w, v_pool_w),
        "pool_scale": (pool_scale, m_pool_scale, v_pool_scale),
        "sgu_w": (sgu_w, m_sgu_w, v_sgu_w),
        "sgu_b": (sgu_b, m_sgu_b, v_sgu_b),
    })
    small_out["final_norm_g"] = [a.reshape(d) for a in small_out["final_norm_g"]]

    order = ["mix_norm_g", "mlp_norm_g", "ab_w_in", "pool_w", "pool_scale", "conv_w", "conv_b", "ab_w_out",
             "cd_w_in", "sgu_norm_g", "sgu_norm_b", "sgu_w", "sgu_b", "cd_w_out", "mlp_w1", "mlp_w2",
             "final_norm_g"]
    both = {**big_out, **small_out}
    loss = loss_sum[0, 0]
    outs = [loss, grad_x.reshape(nseq, t_len, d)]
    for kind in range(4):
        outs += [both[name][kind] for name in order]
    return tuple(outs)
```

```python
import math

import jax
import jax.numpy as jnp
from jax import lax
from jax.experimental import pallas as pl
from jax.experimental.pallas import tpu as pltpu

F32 = jnp.float32
BF16 = jnp.bfloat16
MESH = pl.DeviceIdType.MESH

EPS = 1e-6
TILE = 128
N_CHIP = 4
N_DEV = 8
VMEM_LIMIT_BYTES = 56 * 1024 * 1024

ADAM_LR = 0.001
ADAM_B1 = 0.9
ADAM_B2 = 0.999
ADAM_EPS = 1e-08
ADAM_WD = 0.01
ADAM_STEP = 10

NT_DIMS = (((1,), (1,)), ((), ()))
TN_DIMS = (((0,), (0,)), ((), ()))


def _params(sem=None):
    return pltpu.CompilerParams(dimension_semantics=sem, vmem_limit_bytes=VMEM_LIMIT_BYTES)


def _call(body, *, name, grid, in_specs, out_specs, out_shape, scratch_shapes, semantics, args, rider=None,
          prefetch=None):
    npre = 0 if prefetch is None else 1

    def launch(kernel, in_specs, out_specs, out_shape, scratch_shapes, operands, aliases, params):
        if prefetch is None:
            return pl.pallas_call(kernel, name=name, grid=grid, in_specs=in_specs, out_specs=out_specs,
                                  out_shape=out_shape, scratch_shapes=scratch_shapes, input_output_aliases=aliases,
                                  compiler_params=params)(*operands)
        spec = pltpu.PrefetchScalarGridSpec(num_scalar_prefetch=1, grid=grid, in_specs=in_specs, out_specs=out_specs,
                                            scratch_shapes=scratch_shapes)
        return pl.pallas_call(kernel, name=name, grid_spec=spec, out_shape=out_shape,
                              input_output_aliases={k + 1: v for k, v in aliases.items()},
                              compiler_params=params)(prefetch, *operands)

    if not rider:
        res = launch(body, list(in_specs), list(out_specs), list(out_shape), list(scratch_shapes), args, {},
                     _params(semantics))
        return list(res), []
    plans = [_rider_plan(kind, arrays) for kind, arrays in rider]
    arrays = [a for _, group in rider for a in group]
    nr, n_in, n_out, n_scr = len(arrays), len(in_specs), len(out_specs), len(scratch_shapes)
    first_out, first_scr = n_in + nr, n_in + nr + n_out + nr
    last_step = math.prod(grid) - 1

    def riding(*refs):
        pre, refs = refs[:npre], refs[npre:]
        step = 0
        for axis, size in enumerate(grid):
            step = step * size + pl.program_id(axis)
        steps, at, sem_at = [], 0, first_scr + n_scr
        for (kind, group), (_, sems, _) in zip(rider, plans):
            k = len(group)
            steps.append(_rider_steps(kind, refs[n_in + at:n_in + at + k],
                                      refs[first_out + n_out + at:first_out + n_out + at + k],
                                      refs[sem_at:sem_at + len(sems)]))
            at, sem_at = at + k, sem_at + len(sems)
        for send, _, _ in steps:
            pl.when(step == 0)(send)
        for _, forward, _ in steps:
            if forward is not None:
                pl.when(step == last_step)(forward)
        body(*pre, *refs[:n_in], *refs[first_out:first_out + n_out], *refs[first_scr:first_scr + n_scr])
        for _, _, finish in steps:
            pl.when(step == last_step)(finish)

    aliases, at = {}, 0
    for (_, group), (_, _, aliased) in zip(rider, plans):
        if aliased:
            aliases.update({n_in + at + a: n_out + at + a for a in range(len(group))})
        at += len(group)
    res = launch(
        riding, list(in_specs) + [ANY] * nr, list(out_specs) + [ANY] * nr,
        list(out_shape) + [s for shapes, _, _ in plans for s in shapes],
        list(scratch_shapes) + [s for _, sems, _ in plans for s in sems], [*args, *arrays], aliases,
        pltpu.CompilerParams(dimension_semantics=("arbitrary",) * len(grid), vmem_limit_bytes=VMEM_LIMIT_BYTES,
                             has_side_effects=True))
    rode, at = [], n_out
    for _, group in rider:
        rode.append(list(res[at:at + len(group)]))
        at += len(group)
    return list(res[:n_out]), rode


def _rider_plan(kind, arrays):
    n = len(arrays)
    same = [jax.ShapeDtypeStruct(a.shape, a.dtype) for a in arrays]
    pair = [pltpu.SemaphoreType.DMA((n,))] * 2
    if kind == "gather":
        return same, _gather_sems(n), True
    if kind == "exchange":
        return _exchange_shapes(arrays), _exchange_sems(n), False
    if kind == "swap":
        return _swap_shapes(arrays), pair, False
    if kind == "allgather":
        return ([jax.ShapeDtypeStruct((N_DEV,) + a.shape, a.dtype) for a in arrays],
                [pltpu.SemaphoreType.DMA((7 * n,))] * 2 + [pltpu.SemaphoreType.DMA((n,))], False)
    assert kind == "join"
    return same, pair, True


def _rider_steps(kind, ins, outs, sems):
    if kind == "gather":
        return _gather_steps(outs, *sems)
    if kind == "allgather":
        return _allgather_steps(ins, outs, *sems)
    if kind == "exchange":
        send, finish = _exchange_steps(ins, outs, *sems)
    elif kind == "swap":
        send, finish = _swap_steps(ins, outs, *sems)
    else:
        send, finish = _join_steps(outs, *sems)
    return send, None, finish


def _gathers(rider):
    return any(kind in ("gather", "allgather") for kind, _ in rider or ())


def _row_tile(k_dim, roomy=False):
    if k_dim > 1024:
        return 512
    return 2048 if roomy else 1024


def _mm_nn(a, b4, layer, *, out_dtype, name, epilogue=None, extra=None, norm_g=None, rider=None):
    m, k_dim = a.shape
    _, s_dim, kb, n = b4.shape
    assert kb == k_dim
    tm = min(m, _row_tile(k_dim, roomy=epilogue != "residual" and norm_g is None and not _gathers(rider)))
    tn = min(n, 1024)
    assert m % tm == 0 and n % tn == 0
    npb = n // tn
    grid = (m // tm, s_dim * npb)
    n_in = 2 + (extra is not None) + (norm_g is not None)
    two_outputs = norm_g is not None or epilogue == "relu2"
    assert norm_g is None or (tn == s_dim * n and epilogue != "relu2")

    def body(*refs):
        a_ref, b_ref = refs[:2]
        e_ref = refs[2] if extra is not None else None
        g_ref = refs[n_in - 1] if norm_g is not None else None
        o_ref = refs[n_in]
        acc = jnp.dot(a_ref[...], b_ref[...], preferred_element_type=F32)
        if epilogue == "relu2":
            r = jnp.maximum(acc, 0.0)
            refs[n_in + 1][...] = r.astype(BF16)
            acc = r * r
        elif epilogue == "residual":
            acc = acc + e_ref[...]
        o_ref[...] = acc.astype(out_dtype)
        if norm_g is not None:
            rstd = lax.rsqrt(jnp.mean(acc * acc, axis=-1, keepdims=True) + EPS)
            refs[n_in + 1][...] = (acc * rstd * g_ref[...]).astype(BF16)

    in_specs = [
        pl.BlockSpec((tm, k_dim), lambda i, j: (i, 0)),
        pl.BlockSpec((None, None, k_dim, tn), lambda i, j: (layer, j // npb, 0, j % npb)),
    ]
    args = [a, b4]
    if extra is not None:
        in_specs.append(pl.BlockSpec((tm, tn), lambda i, j: (i, j)))
        args.append(extra)
    out_block = pl.BlockSpec((tm, tn), lambda i, j: (i, j))
    out_specs, out_shape = [out_block], [jax.ShapeDtypeStruct((m, s_dim * n), out_dtype)]
    if norm_g is not None:
        in_specs.append(pl.BlockSpec((1, tn), lambda i, j: (0, j)))
        args.append(norm_g)
    if two_outputs:
        out_specs.append(out_block)
        out_shape.append(jax.ShapeDtypeStruct((m, s_dim * n), BF16))
    res, rode = _call(
        body, name=name, grid=grid, in_specs=in_specs, out_specs=out_specs, out_shape=out_shape,
        scratch_shapes=[], semantics=("parallel", "parallel"), args=args, rider=rider)
    res = res if two_outputs else res[0]
    return res if rider is None else (res, rode)


def _mm_nt(a, b4, layer, *, out_dtype, name, epilogue=None, extra=None, rider=None):
    m, k_dim = a.shape
    _, s_dim, n_out, n = b4.shape
    assert k_dim == s_dim * n
    rms = epilogue == "rms_bwd"
    roomy = not rms and out_dtype != F32 and not _gathers(rider)
    tm, tn = min(m, _row_tile(k_dim, roomy=roomy)), min(n_out, 1024)
    assert m % tm == 0 and n_out % tn == 0
    grid = (m // tm, n_out // tn)
    assert not rms or tn == n_out
    extras = [] if extra is None else (list(extra) if rms else [extra])
    n_in = 2 + len(extras)

    def body(*refs):
        a_ref, b_ref = refs[:2]
        e_refs = refs[2:n_in]
        o_ref = refs[n_in]
        acc = lax.dot_general(a_ref[:, 0:n], b_ref[0], NT_DIMS, preferred_element_type=F32)
        for s in range(1, s_dim):
            acc = acc + lax.dot_general(a_ref[:, s * n:(s + 1) * n], b_ref[s], NT_DIMS, preferred_element_type=F32)
        if epilogue == "relu2_bwd":
            acc = acc * (2.0 * e_refs[0][...].astype(F32))
        if not rms:
            o_ref[...] = acc.astype(out_dtype)
        else:
            h_ref, g_ref, dres_ref = e_refs
            dhb_ref, dg_ref = refs[n_in + 1:n_in + 3]
            hv = h_ref[...]
            rstd = lax.rsqrt(jnp.mean(hv * hv, axis=-1, keepdims=True) + EPS)
            xhat = hv * rstd
            dxhat = acc * g_ref[...]
            dh = dres_ref[...] + rstd * (dxhat - xhat * jnp.mean(dxhat * xhat, axis=-1, keepdims=True))
            o_ref[...] = dh
            dhb_ref[...] = dh.astype(BF16)
            dg_part = jnp.sum(acc * xhat, axis=0, keepdims=True)
            first = pl.program_id(0) == 0

            @pl.when(first)
            def _():
                dg_ref[...] = dg_part

            @pl.when(jnp.logical_not(first))
            def _():
                dg_ref[...] += dg_part

    in_specs = [
        pl.BlockSpec((tm, k_dim), lambda i, j: (i, 0)),
        pl.BlockSpec((None, s_dim, tn, n), lambda i, j: (layer, 0, j, 0)),
    ]
    args = [a, b4] + extras
    block = pl.BlockSpec((tm, tn), lambda i, j: (i, j))
    vec = pl.BlockSpec((1, tn), lambda i, j: (0, j))
    if rms:
        in_specs += [block, vec, block]
        out_specs = [block, block, vec]
        out_shape = [jax.ShapeDtypeStruct((m, n_out), F32), jax.ShapeDtypeStruct((m, n_out), BF16),
                     jax.ShapeDtypeStruct((1, n_out), F32)]
    else:
        in_specs += [block] * len(extras)
        out_specs, out_shape = [block], [jax.ShapeDtypeStruct((m, n_out), out_dtype)]
    res, rode = _call(
        body, name=name, grid=grid, in_specs=in_specs, out_specs=out_specs, out_shape=out_shape,
        scratch_shapes=[], semantics=("arbitrary",) * 2 if rms else ("parallel", "parallel"), args=args, rider=rider)
    res = res if rms else res[0]
    return res if rider is None else (res, rode)


def _mm_tn(a, b, s_dim, *, name, rider=None):
    m, k1 = a.shape
    mb, n_all = b.shape
    assert mb == m and n_all % s_dim == 0
    n = n_all // s_dim
    tn, t1 = min(n, 1024), min(k1, 512 if _gathers(rider) else 1024)
    assert k1 % t1 == 0 and n % tn == 0
    npb = n // tn
    grid = (k1 // t1, s_dim * npb)

    def body(a_ref, b_ref, o_ref):
        o_ref[...] = lax.dot_general(a_ref[...], b_ref[...], TN_DIMS, preferred_element_type=F32).astype(BF16)

    res, rode = _call(
        body, name=name, grid=grid,
        in_specs=[pl.BlockSpec((m, t1), lambda i, j: (0, i)), pl.BlockSpec((m, tn), lambda i, j: (0, j))],
        out_specs=[pl.BlockSpec((None, None, t1, tn), lambda i, j: (0, j // npb, i, j % npb))],
        out_shape=[jax.ShapeDtypeStruct((1, s_dim, k1, n), BF16)],
        scratch_shapes=[], semantics=("parallel", "parallel"), args=[a, b], rider=rider)
    return res[0] if rider is None else (res[0], rode)


ROW_TILE = 512


def _rms_fwd(h, g, *, name, rider=None):
    m, d = h.shape

    def body(h_ref, g_ref, o_ref):
        hv = h_ref[...]
        rstd = lax.rsqrt(jnp.mean(hv * hv, axis=-1, keepdims=True) + EPS)
        o_ref[...] = (hv * rstd * g_ref[...]).astype(BF16)

    res, rode = _call(
        body, name=name, grid=(m // ROW_TILE,),
        in_specs=[pl.BlockSpec((ROW_TILE, d), lambda i: (i, 0)), pl.BlockSpec((1, d), lambda i: (0, 0))],
        out_specs=[pl.BlockSpec((ROW_TILE, d), lambda i: (i, 0))], out_shape=[jax.ShapeDtypeStruct((m, d), BF16)],
        scratch_shapes=[], semantics=("parallel",), args=[h, g], rider=rider)
    return res[0] if rider is None else (res[0], rode)


def _mlp_down_loss(act, w_2, h_res, g, target):
    m, k_dim = act.shape
    d = w_2.shape[-1]
    tm = _row_tile(k_dim)

    def body(a_ref, b_ref, r_ref, g_ref, t_ref, dh_ref, dhb_ref, dg_ref, loss_ref):
        hv = jnp.dot(a_ref[...], b_ref[...], preferred_element_type=F32) + r_ref[...]
        gv = g_ref[...]
        rstd = lax.rsqrt(jnp.mean(hv * hv, axis=-1, keepdims=True) + EPS)
        xhat = hv * rstd
        err = xhat * gv - t_ref[...]
        dy = err * (1.0 / d)
        dxhat = dy * gv
        dh = rstd * (dxhat - xhat * jnp.mean(dxhat * xhat, axis=-1, keepdims=True))
        dh_ref[...] = dh
        dhb_ref[...] = dh.astype(BF16)
        dg_part = jnp.sum(dy * xhat, axis=0, keepdims=True)
        sq = jnp.sum(jnp.sum(err * err, axis=1, keepdims=True), axis=0, keepdims=True) * (0.5 / d)
        loss_part = jnp.broadcast_to(sq, (8, TILE))

        @pl.when(pl.program_id(0) == 0)
        def _():
            dg_ref[...] = dg_part
            loss_ref[...] = loss_part

        @pl.when(pl.program_id(0) > 0)
        def _():
            dg_ref[...] += dg_part
            loss_ref[...] += loss_part

    row = pl.BlockSpec((tm, d), lambda i: (i, 0))
    vec = pl.BlockSpec((1, d), lambda i: (0, 0))
    return pl.pallas_call(
        body, name="mlp1_down_loss", grid=(m // tm,),
        in_specs=[pl.BlockSpec((tm, k_dim), lambda i: (i, 0)),
                  pl.BlockSpec((None, None, k_dim, d), lambda i: (0, 0, 0, 0)), row, vec, row],
        out_specs=[row, row, vec, pl.BlockSpec((8, TILE), lambda i: (0, 0))],
        out_shape=[jax.ShapeDtypeStruct((m, d), F32), jax.ShapeDtypeStruct((m, d), BF16),
                   jax.ShapeDtypeStruct((1, d), F32), jax.ShapeDtypeStruct((8, TILE), F32)],
        compiler_params=_params(("arbitrary",)),
    )(act, w_2, h_res, g, target)


def _shift_down(x, s, t_idx):
    return jnp.where(t_idx >= s, pltpu.roll(x, s, 0), 0.0)


def _shift_up(x, s, t_idx, t_len):
    return jnp.where(t_idx < t_len - s, pltpu.roll(x, t_len - s, 0), 0.0)


def _pool_select(group, s2, s4, s8, s16):
    return jnp.where(group == 0, s2, jnp.where(group == 1, s4, jnp.where(group == 2, s8, s16)))


def _pool_count(group, t_idx):
    win = jnp.left_shift(2, group)
    return jnp.minimum(t_idx + 1, win).astype(F32)


def _pool_fwd_math(a, group, t_idx):
    s2 = a + _shift_down(a, 1, t_idx)
    s4 = s2 + _shift_down(s2, 2, t_idx)
    s8 = s4 + _shift_down(s4, 4, t_idx)
    s16 = s8 + _shift_down(s8, 8, t_idx)
    return _pool_select(group, s2, s4, s8, s16) / _pool_count(group, t_idx) - a


def _pool_bwd_math(dpooled, group, t_idx, t_len):
    e = dpooled / _pool_count(group, t_idx)
    s2 = e + _shift_up(e, 1, t_idx, t_len)
    s4 = s2 + _shift_up(s2, 2, t_idx, t_len)
    s8 = s4 + _shift_up(s4, 4, t_idx, t_len)
    s16 = s8 + _shift_up(s8, 8, t_idx, t_len)
    return _pool_select(group, s2, s4, s8, s16) - dpooled


def _conv_fwd_math(c, w_ref, b_ref, t_idx):
    return (w_ref[0:1, :] * _shift_down(c, 2, t_idx) + w_ref[1:2, :] * _shift_down(c, 1, t_idx)
            + w_ref[2:3, :] * c + b_ref[...])


def _ab_fwd(p, pool_w, pool_scale, conv_w, conv_b, nseq, t_len, rider=None):
    m = p.shape[0]
    ng = 4

    def body(a_ref, xb_ref, gb_ref, gc_ref, pw_ref, ps_ref, cw_ref, cb_ref, o_ref):
        j = pl.program_id(1)
        t_idx = lax.broadcasted_iota(jnp.int32, (t_len, TILE), 0)

        @pl.when(j < ng)
        def _():
            pooled = _pool_fwd_math(a_ref[...].astype(F32), j, t_idx)
            mixed = jnp.dot(pooled.astype(BF16), pw_ref[...].astype(BF16), preferred_element_type=F32)
            o_ref[...] = (mixed * ps_ref[...]).astype(BF16)

        @pl.when(j >= ng)
        def _():
            c = gc_ref[...].astype(F32) * xb_ref[...].astype(F32)
            y = _conv_fwd_math(c, cw_ref, cb_ref, t_idx)
            o_ref[...] = (gb_ref[...].astype(F32) * y).astype(BF16)

    def pool_j(j):
        return jnp.minimum(j, ng - 1)

    def conv_j(j):
        return jnp.maximum(j - ng, 0)

    in_specs = [
        pl.BlockSpec((t_len, TILE), lambda s, j: (s, pool_j(j))),
        pl.BlockSpec((t_len, TILE), lambda s, j: (s, ng + conv_j(j))),
        pl.BlockSpec((t_len, TILE), lambda s, j: (s, 2 * ng + conv_j(j))),
        pl.BlockSpec((t_len, TILE), lambda s, j: (s, 3 * ng + conv_j(j))),
        pl.BlockSpec((None, TILE, TILE), lambda s, j: (pool_j(j), 0, 0)),
        pl.BlockSpec((None, 1, TILE), lambda s, j: (pool_j(j), 0, 0)),
        pl.BlockSpec((3, TILE), lambda s, j: (0, conv_j(j))),
        pl.BlockSpec((1, TILE), lambda s, j: (0, conv_j(j))),
    ]
    res, rode = _call(
        body, name="ab_mixer_fwd", grid=(nseq, 2 * ng), in_specs=in_specs,
        out_specs=[pl.BlockSpec((t_len, TILE), lambda s, j: (s, j))],
        out_shape=[jax.ShapeDtypeStruct((m, 2 * ng * TILE), BF16)], scratch_shapes=[],
        semantics=("parallel", "arbitrary"), args=[p, p, p, p, pool_w, pool_scale, conv_w, conv_b], rider=rider)
    return res[0] if rider is None else (res[0], rode)


def _ab_bwd(p, dmix, pool_w, pool_scale, conv_w, conv_b, nseq, t_len, rider=None):
    m = p.shape[0]
    ng = 4

    def body(a_ref, xb_ref, gb_ref, gc_ref, dma_ref, dmb_ref, pw_ref, ps_ref, cw_ref, cb_ref,
             da_ref, dxb_ref, dgb_ref, dgc_ref, dpw_ref, dps_ref, dcw_ref, dcb_ref):
        j = pl.program_id(0)
        first = pl.program_id(1) == 0
        t_idx = lax.broadcasted_iota(jnp.int32, (t_len, TILE), 0)

        pooled = _pool_fwd_math(a_ref[...].astype(F32), j, t_idx).astype(BF16)
        w_bf = pw_ref[...].astype(BF16)
        mixed = jnp.dot(pooled, w_bf, preferred_element_type=F32)
        dm = dma_ref[...].astype(F32)
        dps = jnp.sum(dm * mixed, axis=0, keepdims=True)
        dmixed = (dm * ps_ref[...]).astype(BF16)
        dpw = lax.dot_general(pooled, dmixed, TN_DIMS, preferred_element_type=F32)
        dpooled = lax.dot_general(dmixed, w_bf, NT_DIMS, preferred_element_type=F32)
        da_ref[...] = _pool_bwd_math(dpooled, j, t_idx, t_len).astype(BF16)

        xb = xb_ref[...].astype(F32)
        gb = gb_ref[...].astype(F32)
        gc = gc_ref[...].astype(F32)
        d = dmb_ref[...].astype(F32)
        c = gc * xb
        c1 = _shift_down(c, 1, t_idx)
        c2 = _shift_down(c, 2, t_idx)
        y = cw_ref[0:1, :] * c2 + cw_ref[1:2, :] * c1 + cw_ref[2:3, :] * c + cb_ref[...]
        dgb_ref[...] = (d * y).astype(BF16)
        dy = d * gb
        dc = (cw_ref[2:3, :] * dy + cw_ref[1:2, :] * _shift_up(dy, 1, t_idx, t_len)
              + cw_ref[0:1, :] * _shift_up(dy, 2, t_idx, t_len))
        dgc_ref[...] = (dc * xb).astype(BF16)
        dxb_ref[...] = (dc * gc).astype(BF16)
        dcw = jnp.concatenate([jnp.sum(dy * c2, axis=0, keepdims=True),
                               jnp.sum(dy * c1, axis=0, keepdims=True),
                               jnp.sum(dy * c, axis=0, keepdims=True)], axis=0)
        dcb = jnp.sum(dy, axis=0, keepdims=True)

        @pl.when(first)
        def _():
            dpw_ref[...] = dpw
            dps_ref[...] = dps
            dcw_ref[...] = dcw
            dcb_ref[...] = dcb

        @pl.when(jnp.logical_not(first))
        def _():
            dpw_ref[...] += dpw
            dps_ref[...] += dps
            dcw_ref[...] += dcw
            dcb_ref[...] += dcb

    def col(k):
        return pl.BlockSpec((t_len, TILE), lambda j, s: (s, k * ng + j))

    in_specs = [
        col(0), col(1), col(2), col(3), col(0), col(1),
        pl.BlockSpec((None, TILE, TILE), lambda j, s: (j, 0, 0)),
        pl.BlockSpec((None, 1, TILE), lambda j, s: (j, 0, 0)),
        pl.BlockSpec((3, TILE), lambda j, s: (0, j)),
        pl.BlockSpec((1, TILE), lambda j, s: (0, j)),
    ]
    piece = pl.BlockSpec((t_len, TILE), lambda j, s: (s, j))
    out_specs = [
        piece, piece, piece, piece,
        pl.BlockSpec((None, TILE, TILE), lambda j, s: (j, 0, 0)),
        pl.BlockSpec((None, 1, TILE), lambda j, s: (j, 0, 0)),
        pl.BlockSpec((3, TILE), lambda j, s: (0, j)),
        pl.BlockSpec((1, TILE), lambda j, s: (0, j)),
    ]
    w = ng * TILE
    out_shape = [jax.ShapeDtypeStruct((m, w), BF16)] * 4 + [
        jax.ShapeDtypeStruct((ng, TILE, TILE), F32), jax.ShapeDtypeStruct((ng, 1, TILE), F32),
        jax.ShapeDtypeStruct((3, w), F32), jax.ShapeDtypeStruct((1, w), F32)]
    res, rode = _call(
        body, name="ab_mixer_bwd", grid=(ng, nseq), in_specs=in_specs, out_specs=out_specs, out_shape=out_shape,
        scratch_shapes=[], semantics=("parallel", "arbitrary"),
        args=[p, p, p, p, dmix, dmix, pool_w, pool_scale, conv_w, conv_b], rider=rider)
    return res if rider is None else (res, rode)


SGU_ROWS = 512
INV_SQRT2 = 1.0 / math.sqrt(2.0)
INV_SQRT_2PI = 1.0 / math.sqrt(2.0 * math.pi)


def _gelu(x):
    return 0.5 * x * (1.0 + lax.erf(x * INV_SQRT2))


def _gelu_grad(x):
    return 0.5 * (1.0 + lax.erf(x * INV_SQRT2)) + x * (INV_SQRT_2PI * jnp.exp(-0.5 * x * x))


def _causal_tile(transposed=False):
    r = lax.broadcasted_iota(jnp.int32, (TILE, TILE), 0)
    c = lax.broadcasted_iota(jnp.int32, (TILE, TILE), 1)
    return r <= c if transposed else c <= r


def _sgu_norm(v, g_ref, b_ref):
    mu = jnp.mean(v, axis=-1, keepdims=True)
    xc = v - mu
    rstd = lax.rsqrt(jnp.mean(xc * xc, axis=-1, keepdims=True) + EPS)
    xhat = xc * rstd
    return xhat, rstd, xhat * g_ref[...] + b_ref[...]


def _sgu_fwd(p, norm_g, norm_b, w_s, bias_tile):
    m = p.shape[0]
    ng = 4
    width = ng * TILE

    def body(u_ref, v_ref, g_ref, b_ref, w_ref, bias_ref, o_ref):
        u = _gelu(u_ref[...].astype(F32))
        _, _, vln = _sgu_norm(_gelu(v_ref[...].astype(F32)), g_ref, b_ref)
        vln = vln.astype(BF16)
        causal = _causal_tile()
        for g in range(ng):
            cols = slice(g * TILE, (g + 1) * TILE)
            wg = jnp.where(causal, w_ref[g], 0.0).astype(BF16)
            for n in range(SGU_ROWS // TILE):
                rows = slice(n * TILE, (n + 1) * TILE)
                s = jnp.dot(wg, vln[rows, cols], preferred_element_type=F32) + bias_ref[g]
                o_ref[rows, cols] = (u[rows, cols] * s).astype(BF16)

    vec = pl.BlockSpec((1, width), lambda i: (0, 0))
    tiles = pl.BlockSpec((ng, TILE, TILE), lambda i: (0, 0, 0))
    return pl.pallas_call(
        body, name="sgu_fwd", grid=(m // SGU_ROWS,),
        in_specs=[pl.BlockSpec((SGU_ROWS, width), lambda i: (i, 0)),
                  pl.BlockSpec((SGU_ROWS, width), lambda i: (i, 1)), vec, vec, tiles, tiles],
        out_specs=pl.BlockSpec((SGU_ROWS, width), lambda i: (i, 0)),
        out_shape=jax.ShapeDtypeStruct((m, 2 * width), BF16),
        compiler_params=_params(("parallel",)),
    )(p, p, norm_g, norm_b, w_s, bias_tile)


def _sgu_bwd(p, dmix, norm_g, norm_b, w_s, w_s_t, bias_tile):
    m = p.shape[0]
    ng = 4
    width = ng * TILE

    def body(u_ref, v_ref, dc_ref, g_ref, b_ref, w_ref, wt_ref, bias_ref,
             du_ref, dv_ref, dw_ref, dbs_ref, dg_ref, db_ref, ds_scr, dvln_scr):
        u_pre = u_ref[...].astype(F32)
        v_pre = v_ref[...].astype(F32)
        u = _gelu(u_pre)
        xhat, rstd, vln = _sgu_norm(_gelu(v_pre), g_ref, b_ref)
        vln = vln.astype(BF16)
        dc = dc_ref[...].astype(F32)
        causal = _causal_tile()
        ones = jnp.ones((TILE, TILE), BF16)
        first = pl.program_id(0) == 0
        for g in range(ng):
            cols = slice(g * TILE, (g + 1) * TILE)
            wg = jnp.where(causal, w_ref[g], 0.0).astype(BF16)
            wgt = jnp.where(_causal_tile(transposed=True), wt_ref[g], 0.0).astype(BF16)
            dw_acc = jnp.zeros((TILE, TILE), F32)
            dbs_acc = jnp.zeros((TILE, TILE), F32)
            for n in range(SGU_ROWS // TILE):
                rows = slice(n * TILE, (n + 1) * TILE)
                vt = vln[rows, cols]
                s = jnp.dot(wg, vt, preferred_element_type=F32) + bias_ref[g]
                ds_scr[rows, cols] = dc[rows, cols] * s
                ds = (dc[rows, cols] * u[rows, cols]).astype(BF16)
                dw_acc += lax.dot_general(ds, vt, NT_DIMS, preferred_element_type=F32)
                dbs_acc += jnp.dot(ds, ones, preferred_element_type=F32)
                dvln_scr[rows, cols] = jnp.dot(wgt, ds, preferred_element_type=F32)
            dw_g = jnp.where(causal, dw_acc, 0.0)

            @pl.when(first)
            def _():
                dw_ref[g] = dw_g
                dbs_ref[g] = dbs_acc

            @pl.when(jnp.logical_not(first))
            def _():
                dw_ref[g] += dw_g
                dbs_ref[g] += dbs_acc

        du_ref[...] = (ds_scr[...] * _gelu_grad(u_pre)).astype(BF16)
        dvln = dvln_scr[...]
        dxhat = dvln * g_ref[...]
        dv = rstd * (dxhat - jnp.mean(dxhat, axis=-1, keepdims=True)
                     - xhat * jnp.mean(dxhat * xhat, axis=-1, keepdims=True))
        dv_ref[...] = (dv * _gelu_grad(v_pre)).astype(BF16)
        dg_part = jnp.sum(dvln * xhat, axis=0, keepdims=True)
        db_part = jnp.sum(dvln, axis=0, keepdims=True)

        @pl.when(first)
        def _():
            dg_ref[...] = dg_part
            db_ref[...] = db_part

        @pl.when(jnp.logical_not(first))
        def _():
            dg_ref[...] += dg_part
            db_ref[...] += db_part

    vec = pl.BlockSpec((1, width), lambda i: (0, 0))
    tiles = pl.BlockSpec((ng, TILE, TILE), lambda i: (0, 0, 0))
    rows0 = pl.BlockSpec((SGU_ROWS, width), lambda i: (i, 0))
    rows1 = pl.BlockSpec((SGU_ROWS, width), lambda i: (i, 1))
    return pl.pallas_call(
        body, name="sgu_bwd", grid=(m // SGU_ROWS,),
        in_specs=[rows0, rows1, rows0, vec, vec, tiles, tiles, tiles],
        out_specs=[rows0, rows0, tiles, tiles, vec, vec],
        out_shape=[jax.ShapeDtypeStruct((m, width), BF16), jax.ShapeDtypeStruct((m, width), BF16),
                   jax.ShapeDtypeStruct((ng, TILE, TILE), F32), jax.ShapeDtypeStruct((ng, TILE, TILE), F32),
                   jax.ShapeDtypeStruct((1, width), F32), jax.ShapeDtypeStruct((1, width), F32)],
        scratch_shapes=[pltpu.VMEM((SGU_ROWS, width), F32), pltpu.VMEM((SGU_ROWS, width), F32)],
        compiler_params=_params(("arbitrary",)),
    )(p, p, dmix, norm_g, norm_b, w_s, w_s_t, bias_tile)


SB_DH = 64
SB_SCALE = 1.0 / math.sqrt(SB_DH)


SB_BLOCK = 256
SB_SUB = SB_BLOCK // TILE
SB_PASS = 4


def _split_passes(i):
    rem = i % SB_PASS
    return i // SB_PASS, rem >= 2, rem % 2 == 1


def _sum_matrix(kind):
    j = lax.broadcasted_iota(jnp.int32, (TILE, 2 * TILE), 0)
    s = lax.broadcasted_iota(jnp.int32, (TILE, 2 * TILE), 1)
    tri = {"after": j > s, "upto": j <= s, "before": j < s}[kind]
    return jnp.where(jnp.logical_or(s >= TILE, tri), 1.0, 0.0).astype(BF16)


def _strict_mask():
    r = lax.broadcasted_iota(jnp.int32, (SB_BLOCK, SB_BLOCK), 0)
    c = lax.broadcasted_iota(jnp.int32, (SB_BLOCK, SB_BLOCK), 1)
    return c < r


def _head_lanes(h):
    lane = lax.broadcasted_iota(jnp.int32, (1, TILE), 1)
    return (lane >= h * SB_DH) & (lane < (h + 1) * SB_DH)


def _log_gates(z):
    log_sig = jnp.minimum(z, 0.0) - jnp.log(1.0 + jnp.exp(-jnp.abs(z)))
    return log_sig, log_sig - z


def _sb_fwd(p, mix, nseq, t_len, gather):
    m = p.shape[0]
    npair = 4
    ng = len(gather)
    last_step = nseq * npair - 1

    def body(q_ref, k_ref, v_ref, *rest):
        o_ref, lt_ref = rest[ng + 1:ng + 3]
        kh_ref, vh_ref = rest[2 * ng + 3:2 * ng + 5]
        step = pl.program_id(0) * npair + pl.program_id(1)
        send, forward, finish = _gather_steps(rest[ng + 3:2 * ng + 3], *rest[2 * ng + 5:])
        pl.when(step == 0)(send)
        pl.when(step == (last_step + 1) // 2)(forward)
        for h in range(2):
            keep = _head_lanes(h)
            kh_ref[h] = jnp.where(keep, k_ref[...], 0).astype(BF16)
            vh_ref[h] = jnp.where(keep, v_ref[...], 0).astype(BF16)
        summat = _sum_matrix("after")
        strict = _strict_mask()

        def one_pass(q, row0, nsub, diag, state):
            rows = pl.ds(row0, nsub * TILE)
            log_sig, pieces = [], []
            for h in range(2):
                zh = lax.dot_general(q, kh_ref[h, rows, :], NT_DIMS, preferred_element_type=F32)
                log_sig_h, logkeep = _log_gates(zh)
                if diag:
                    logkeep = jnp.where(strict, logkeep, 0.0)
                log_sig.append(log_sig_h)
                pieces += [logkeep[:, b * TILE:(b + 1) * TILE] for b in range(nsub)]
            sums = jnp.dot(jnp.concatenate(pieces, axis=0).astype(BF16), summat, preferred_element_type=F32)
            out = []
            for h in range(2):
                carry, acc = state[2 * h], state[2 * h + 1]
                after = [None] * nsub
                for b in reversed(range(nsub)):
                    part = sums[(h * nsub + b) * SB_BLOCK:(h * nsub + b + 1) * SB_BLOCK]
                    after[b] = part[:, :TILE] + carry
                    carry = carry + part[:, TILE:]
                w = jnp.exp(log_sig[h] + jnp.concatenate(after, axis=1))
                if diag:
                    w = jnp.where(strict, w, 0.0)
                out += [carry, acc + jnp.dot(w.astype(BF16), vh_ref[h, rows, :], preferred_element_type=F32)]
            return tuple(out)

        def q_block(i, _):
            r0 = pl.multiple_of(i * SB_BLOCK, SB_BLOCK)
            q = q_ref[pl.ds(r0, SB_BLOCK), :] * SB_SCALE
            zero = jnp.zeros((SB_BLOCK, TILE), F32)
            state = one_pass(q, r0, SB_SUB, True, (zero,) * 4)
            full, two, one = _split_passes(i)
            state = lax.fori_loop(
                0, full,
                lambda jj, st: one_pass(q, pl.multiple_of((i - SB_PASS * (jj + 1)) * SB_BLOCK, SB_BLOCK),
                                        SB_PASS * SB_SUB, False, st),
                state)
            state = lax.cond(
                two, lambda st: one_pass(q, pl.multiple_of((i % 2) * SB_BLOCK, SB_BLOCK), 2 * SB_SUB, False, st),
                lambda st: st, state)
            state = lax.cond(one, lambda st: one_pass(q, 0, SB_SUB, False, st), lambda st: st, state)
            o_ref[pl.ds(r0, SB_BLOCK), :] = (state[1] + state[3]).astype(BF16)
            lt_ref[pl.ds(r0, SB_BLOCK), :] = jnp.where(_head_lanes(0), state[0], state[2])
            return 0

        lax.fori_loop(0, t_len // SB_BLOCK, q_block, 0)
        pl.when(step == last_step)(finish)

    def col(k):
        return pl.BlockSpec((t_len, TILE), lambda s, hp: (s, k * npair + hp))

    out = pl.BlockSpec((t_len, TILE), lambda s, hp: (s, hp))
    res = pl.pallas_call(
        body, name="stickbreak_fwd", grid=(nseq, npair), in_specs=[col(2), col(3), col(4)] + [ANY] * (ng + 1),
        out_specs=[pl.BlockSpec((t_len, TILE), lambda s, hp: (s, npair + hp)), out] + [ANY] * ng,
        out_shape=[jax.ShapeDtypeStruct(mix.shape, BF16), jax.ShapeDtypeStruct((m, npair * TILE), F32)]
        + [jax.ShapeDtypeStruct(b.shape, b.dtype) for b in gather],
        input_output_aliases={**{3 + a: 2 + a for a in range(ng)}, 3 + ng: 0},
        scratch_shapes=[pltpu.VMEM((2, t_len, TILE), BF16), pltpu.VMEM((2, t_len, TILE), BF16)] + _gather_sems(ng),
        compiler_params=pltpu.CompilerParams(dimension_semantics=("arbitrary", "arbitrary"),
                                             vmem_limit_bytes=VMEM_LIMIT_BYTES, has_side_effects=True),
    )(p, p, p, *gather, mix)
    return res[0], res[1], res[2:]


def _sb_bwd(p, dmix, ltot, nseq, t_len, exchange):
    m = p.shape[0]
    npair = 4
    ne = len(exchange)
    last_step = nseq * npair - 1

    def body(q_ref, k_ref, v_ref, do_ref, lt_ref, *rest):
        dq_ref, dk_ref, dv_ref = rest[ne:ne + 3]
        kh_ref, vh_ref, dk_acc, dv_acc = rest[2 * ne + 3:2 * ne + 7]
        step = pl.program_id(0) * npair + pl.program_id(1)
        send, finish = _exchange_steps(rest[:ne], rest[ne + 3:2 * ne + 3], *rest[2 * ne + 7:])
        pl.when(step == 0)(send)
        for h in range(2):
            keep = _head_lanes(h)
            kh_ref[h] = jnp.where(keep, k_ref[...], 0).astype(BF16)
            vh_ref[h] = jnp.where(keep, v_ref[...], 0).astype(BF16)
        dk_acc[...] = jnp.zeros_like(dk_acc)
        dv_acc[...] = jnp.zeros_like(dv_acc)
        sum_upto = _sum_matrix("upto")
        sum_before = _sum_matrix("before")
        strict = _strict_mask()
        lane = lax.broadcasted_iota(jnp.int32, (SB_BLOCK, TILE), 1)

        def running(x, matrix, start, nsub):
            pieces = [x[h][:, b * TILE:(b + 1) * TILE] for h in range(2) for b in range(nsub)]
            sums = jnp.dot(jnp.concatenate(pieces, axis=0).astype(BF16), matrix, preferred_element_type=F32)
            wide, ends = [], []
            for h in range(2):
                total, cols = start[h], []
                for b in range(nsub):
                    part = sums[(h * nsub + b) * SB_BLOCK:(h * nsub + b + 1) * SB_BLOCK]
                    cols.append(part[:, :TILE] + total)
                    total = total + part[:, TILE:]
                wide.append(jnp.concatenate(cols, axis=1))
                ends.append(total)
            return wide, ends

        def one_pass(q, do, qh, doh, ltot, row0, nsub, diag, state):
            rows = pl.ds(row0, nsub * TILE)
            log_sig, logkeep = [], []
            for h in range(2):
                zh = lax.dot_general(q, kh_ref[h, rows, :], NT_DIMS, preferred_element_type=F32)
                log_sig_h, logkeep_h = _log_gates(zh)
                log_sig.append(log_sig_h)
                logkeep.append(jnp.where(strict, logkeep_h, 0.0) if diag else logkeep_h)
            upto, sum_l = running(logkeep, sum_upto, [state[0], state[3]], nsub)
            w, g = [], []
            for h in range(2):
                wh = jnp.exp(log_sig[h] + (ltot[h] - upto[h]))
                if diag:
                    wh = jnp.where(strict, wh, 0.0)
                w.append(wh)
                g.append(wh * lax.dot_general(do, vh_ref[h, rows, :], NT_DIMS, preferred_element_type=F32))
            g_before, sum_g = running(g, sum_before, [state[1], state[4]], nsub)
            out, dk_new, dv_new = [], 0.0, 0.0
            for h in range(2):
                dz = g[h] - jnp.exp(log_sig[h]) * (g[h] + g_before[h])
                if diag:
                    dz = jnp.where(strict, dz, 0.0)
                dzb = dz.astype(BF16)
                dq = state[3 * h + 2] + jnp.dot(dzb, kh_ref[h, rows, :], preferred_element_type=F32)
                dk_new = dk_new + lax.dot_general(dzb, qh[h], TN_DIMS, preferred_element_type=F32)
                dv_new = dv_new + lax.dot_general(w[h].astype(BF16), doh[h], TN_DIMS, preferred_element_type=F32)
                out += [sum_l[h], sum_g[h], dq]
            dk_acc[rows, :] += dk_new
            dv_acc[rows, :] += dv_new
            return tuple(out)

        def q_block(i, _):
            r0 = pl.multiple_of(i * SB_BLOCK, SB_BLOCK)
            q = q_ref[pl.ds(r0, SB_BLOCK), :] * SB_SCALE
            do = do_ref[pl.ds(r0, SB_BLOCK), :]
            lt = lt_ref[pl.ds(r0, SB_BLOCK), :]
            qh, doh, ltot = [], [], []
            for h in range(2):
                keep = _head_lanes(h)
                qh.append(jnp.where(keep, q, 0).astype(BF16))
                doh.append(jnp.where(keep, do, 0).astype(BF16))
                ltot.append(jnp.sum(jnp.where(lane == h * SB_DH, lt, 0.0), axis=1, keepdims=True))
            zero = jnp.zeros((SB_BLOCK, TILE), F32)
            full, two, one = _split_passes(i)
            state = lax.fori_loop(
                0, full,
                lambda jj, st: one_pass(q, do, qh, doh, ltot, pl.multiple_of(SB_PASS * jj * SB_BLOCK, SB_BLOCK),
                                        SB_PASS * SB_SUB, False, st),
                (zero,) * 6)
            state = lax.cond(
                two,
                lambda st: one_pass(q, do, qh, doh, ltot, pl.multiple_of(SB_PASS * full * SB_BLOCK, SB_BLOCK),
                                    2 * SB_SUB, False, st),
                lambda st: st, state)
            state = lax.cond(
                one,
                lambda st: one_pass(q, do, qh, doh, ltot, pl.multiple_of((i - 1) * SB_BLOCK, SB_BLOCK), SB_SUB, False, st),
                lambda st: st, state)
            state = one_pass(q, do, qh, doh, ltot, r0, SB_SUB, True, state)
            dq_ref[pl.ds(r0, SB_BLOCK), :] = ((state[2] + state[5]) * SB_SCALE).astype(BF16)
            return 0

        lax.fori_loop(0, t_len // SB_BLOCK, q_block, 0)
        dk_ref[...] = dk_acc[...].astype(BF16)
        dv_ref[...] = dv_acc[...].astype(BF16)
        pl.when(step == last_step)(finish)

    def col(k):
        return pl.BlockSpec((t_len, TILE), lambda s, hp: (s, k * npair + hp))

    out = pl.BlockSpec((t_len, TILE), lambda s, hp: (s, hp))
    width = npair * TILE
    res = pl.pallas_call(
        body, name="stickbreak_bwd", grid=(nseq, npair),
        in_specs=[col(2), col(3), col(4), col(1), out] + [ANY] * ne, out_specs=[out, out, out] + [ANY] * ne,
        out_shape=[jax.ShapeDtypeStruct((m, width), BF16)] * 3 + _exchange_shapes(exchange),
        scratch_shapes=[pltpu.VMEM((2, t_len, TILE), BF16), pltpu.VMEM((2, t_len, TILE), BF16),
                        pltpu.VMEM((t_len, TILE), F32), pltpu.VMEM((t_len, TILE), F32)] + _exchange_sems(ne),
        compiler_params=pltpu.CompilerParams(dimension_semantics=("arbitrary", "arbitrary"),
                                             vmem_limit_bytes=VMEM_LIMIT_BYTES, has_side_effects=True),
    )(p, p, p, dmix, ltot, *exchange)
    return res[0], res[1], res[2], res[3:]


def _adam_math(w, g, m, v):
    m = ADAM_B1 * m + (1.0 - ADAM_B1) * g
    v = ADAM_B2 * v + (1.0 - ADAM_B2) * (g * g)
    m_hat = m / (1.0 - ADAM_B1 ** ADAM_STEP)
    v_hat = v / (1.0 - ADAM_B2 ** ADAM_STEP)
    delta = -ADAM_LR * (m_hat / (jnp.sqrt(v_hat) + ADAM_EPS) + ADAM_WD * w)
    return delta, m, v


def _cast_place(w, layer, pos, *, name):
    _, r, c = w.shape
    tr = min(r, 256)

    def body(pos_ref, w_ref, o_ref):
        o_ref[...] = w_ref[...].astype(BF16)

    grid_spec = pltpu.PrefetchScalarGridSpec(
        num_scalar_prefetch=1, grid=(r // tr,),
        in_specs=[pl.BlockSpec((None, tr, c), lambda i, pos_ref: (layer, i, 0))],
        out_specs=pl.BlockSpec((None, None, tr, c), lambda i, pos_ref: (0, pos_ref[0], i, 0)))
    return pl.pallas_call(
        body, name=name, grid_spec=grid_spec, out_shape=jax.ShapeDtypeStruct((1, N_CHIP, r, c), BF16),
        compiler_params=_params(("parallel",)),
    )(pos, w)


def _cast_place_all(items, pos, *, name, rider=None):
    tiles = [min(w.shape[1], 256) for w, _ in items]
    counts = [w.shape[1] // t for (w, _), t in zip(items, tiles)]
    starts = [sum(counts[:a]) for a in range(len(items))]
    n = len(items)

    def body(pos_ref, *refs):
        i = pl.program_id(0)
        for a in range(n):
            @pl.when((i >= starts[a]) & (i < starts[a] + counts[a]))
            def _():
                refs[n + a][...] = refs[a][...].astype(BF16)

    def block(a):
        return lambda i: jnp.clip(i - starts[a], 0, counts[a] - 1)

    in_specs, out_specs, out_shape = [], [], []
    for a, ((w, layer), t) in enumerate(zip(items, tiles)):
        _, r, c = w.shape
        in_specs.append(pl.BlockSpec((None, t, c), lambda i, pos_ref, a=a, layer=layer: (layer, block(a)(i), 0)))
        out_specs.append(pl.BlockSpec((None, None, t, c), lambda i, pos_ref, a=a: (0, pos_ref[0], block(a)(i), 0)))
        out_shape.append(jax.ShapeDtypeStruct((1, N_CHIP, r, c), BF16))
    res, rode = _call(body, name=name, grid=(sum(counts),), in_specs=in_specs, out_specs=out_specs,
                      out_shape=out_shape, scratch_shapes=[], semantics=("arbitrary",),
                      args=[w for w, _ in items], rider=rider, prefetch=pos)
    return res if rider is None else (res, rode)


def _pair_sum(mine, got, pos, *, name):
    l_dim, s_dim, h, c = got.shape
    th = min(h, 512)
    nt = h // th

    def body(pos_ref, a_ref, b_ref, o_ref):
        o_ref[...] = (a_ref[...].astype(F32) + b_ref[...].astype(F32)).astype(BF16)

    pieces = 2
    spec = pl.BlockSpec((None, pieces, th, c), lambda l, s, i, pos_ref: (l, s, i, 0))
    grid_spec = pltpu.PrefetchScalarGridSpec(
        num_scalar_prefetch=1, grid=(l_dim, s_dim // pieces, nt),
        in_specs=[pl.BlockSpec((None, pieces, th, c), lambda l, s, i, pos_ref: (l, s, pos_ref[1] * nt + i, 0)), spec],
        out_specs=spec)
    return pl.pallas_call(
        body, name=name, grid_spec=grid_spec, out_shape=jax.ShapeDtypeStruct(got.shape, BF16),
        compiler_params=_params(("parallel",) * 3),
    )(pos, mine, got)


def _chip_sum(sums, landed, pos, *, name):
    l_dim, _, h, c = sums.shape
    th = min(h, 512)
    nt = h // th

    def body(pos_ref, own, r0, r1, r2, o_ref):
        o_ref[...] = ((own[...].astype(F32) + r0[...].astype(F32)) + r1[...].astype(F32)) + r2[...].astype(F32)

    def piece(k):
        return pl.BlockSpec((None, None, th, c), lambda l, i, pos_ref: (l, k, i, 0))

    grid_spec = pltpu.PrefetchScalarGridSpec(
        num_scalar_prefetch=1, grid=(l_dim, nt),
        in_specs=[pl.BlockSpec((None, None, th, c), lambda l, i, pos_ref: (l, pos_ref[0], i, 0)),
                  piece(0), piece(1), piece(2)],
        out_specs=pl.BlockSpec((None, th, c), lambda l, i, pos_ref: (l, pos_ref[1] * nt + i, 0)))
    return pl.pallas_call(
        body, name=name, grid_spec=grid_spec, out_shape=jax.ShapeDtypeStruct((l_dim, 2 * h, c), F32),
        compiler_params=_params(("parallel",) * 2),
    )(pos, sums, landed, landed, landed)


def _adam_big(w, m, v, grads, *, name):
    l_dim, r, c = w.shape
    assert len(grads) == l_dim
    tr = min(r, 512)

    def body(*refs):
        w_ref, m_ref, v_ref = refs[:3]
        g_refs = refs[3:3 + l_dim]
        go_ref, d_ref, mo_ref, vo_ref = refs[3 + l_dim:]
        g = g_refs[0][...]
        for l in range(1, l_dim):
            g = jnp.where(pl.program_id(0) == l, g_refs[l][...], g)
        delta, m_new, v_new = _adam_math(w_ref[...], g, m_ref[...], v_ref[...])
        go_ref[...] = g
        d_ref[...] = delta
        mo_ref[...] = m_new
        vo_ref[...] = v_new

    spec = pl.BlockSpec((None, tr, c), lambda l, i: (l, i, 0))
    gspec = pl.BlockSpec((None, tr, c), lambda l, i: (0, i, 0))
    return pl.pallas_call(
        body, name=name, grid=(l_dim, r // tr), in_specs=[spec] * 3 + [gspec] * l_dim, out_specs=[spec] * 4,
        out_shape=[jax.ShapeDtypeStruct(w.shape, F32)] * 4, compiler_params=_params(("parallel",) * 2),
    )(w, m, v, *grads)


def _position():
    return lax.axis_index("x"), lax.axis_index("y"), lax.axis_index("c")


def _other_chips(x, y):
    return [(1 - x, y), (x, 1 - y), (1 - x, 1 - y)]


def _remote(src, dst, send_sem, recv_sem, device):
    return pltpu.make_async_remote_copy(src_ref=src, dst_ref=dst, send_sem=send_sem, recv_sem=recv_sem,
                                        device_id=device, device_id_type=MESH)


ANY = pl.BlockSpec(memory_space=pl.ANY)


def _gather_sems(n):
    return [pltpu.SemaphoreType.DMA((3 * n,))] * 4


def _gather_steps(outs, send_sems, recv_sems, fwd_send, fwd_recv):
    n = len(outs)
    x, y, c = _position()
    chips = _other_chips(x, y)
    sibling = (x, y, 1 - c)

    def half(a, chip, core):
        h = outs[a].shape[2] // 2
        return outs[a].at[:, 2 * chip[0] + chip[1], pl.ds(core * h, h), :]

    def over_ici(a, k, chip):
        block = half(a, chip, c)
        return _remote(block, block, send_sems.at[3 * a + k], recv_sems.at[3 * a + k], (*chips[k], c))

    def over_d2d(a, k, core):
        block = half(a, chips[k], core)
        return _remote(block, block, fwd_send.at[3 * a + k], fwd_recv.at[3 * a + k], sibling)

    def send():
        for a in range(n):
            for k in range(3):
                over_ici(a, k, (x, y)).start()

    def forward():
        for k in range(3):
            for a in range(n):
                over_ici(a, k, chips[k]).wait_recv()
                over_d2d(a, k, c).start()

    def finish():
        for k in range(3):
            for a in range(n):
                over_d2d(a, k, 1 - c).wait_recv()
        for a in range(n):
            for k in range(3):
                over_ici(a, k, (x, y)).wait_send()
                over_d2d(a, k, c).wait_send()

    return send, forward, finish


def _swap_halves(grads, *, name):
    n = len(grads)

    def body(*refs):
        send, finish = _swap_steps(refs[:n], refs[n:2 * n], *refs[2 * n:])
        send()
        finish()

    sem = pltpu.SemaphoreType.DMA((n,))
    return pl.pallas_call(
        body, name=name, in_specs=[ANY] * n, out_specs=[ANY] * n, out_shape=_swap_shapes(grads),
        scratch_shapes=[sem, sem], compiler_params=pltpu.CompilerParams(has_side_effects=True),
    )(*grads)


def _swap_shapes(grads):
    return [jax.ShapeDtypeStruct(g.shape[:2] + (g.shape[2] // 2, g.shape[3]), g.dtype) for g in grads]


def _swap_steps(ins, outs, send_sems, recv_sems):
    x, y, c = _position()

    def copy(a):
        h = ins[a].shape[2] // 2
        return _remote(ins[a].at[:, :, pl.ds((1 - c) * h, h), :], outs[a], send_sems.at[a], recv_sems.at[a],
                       (x, y, 1 - c))

    def send():
        for a in range(len(ins)):
            copy(a).start()

    def finish():
        for a in range(len(ins)):
            copy(a).wait()

    return send, finish


def _exchange_shapes(sums):
    return [jax.ShapeDtypeStruct((s.shape[0], 3) + s.shape[2:], s.dtype) for s in sums]


def _exchange_sems(n):
    return [pltpu.SemaphoreType.DMA((3 * n,))] * 2


def _exchange_steps(ins, outs, send_sems, recv_sems):
    n = len(ins)
    x, y, c = _position()
    chips = _other_chips(x, y)

    def copy(a, k):
        chip = chips[k]
        return _remote(ins[a].at[:, 2 * chip[0] + chip[1]], outs[a].at[:, k],
                       send_sems.at[3 * a + k], recv_sems.at[3 * a + k], (*chip, c))

    def send():
        for a in range(n):
            for k in range(3):
                copy(a, k).start()

    def finish():
        for a in range(n):
            for k in range(3):
                copy(a, k).wait()

    return send, finish


def _join_halves(bufs, *, name):
    n = len(bufs)

    def body(*refs):
        send, finish = _join_steps(refs[n:2 * n], *refs[2 * n:])
        send()
        finish()

    sem = pltpu.SemaphoreType.DMA((n,))
    return pl.pallas_call(
        body, name=name, in_specs=[ANY] * n, out_specs=[ANY] * n,
        out_shape=[jax.ShapeDtypeStruct(b.shape, b.dtype) for b in bufs],
        input_output_aliases={a: a for a in range(n)},
        scratch_shapes=[sem, sem], compiler_params=pltpu.CompilerParams(has_side_effects=True),
    )(*bufs)


def _join_steps(outs, send_sems, recv_sems):
    x, y, c = _position()

    def copy(a, core):
        h = outs[a].shape[1] // 2
        half = outs[a].at[:, pl.ds(core * h, h), :]
        return _remote(half, half, send_sems.at[a], recv_sems.at[a], (x, y, 1 - c))

    def send():
        for a in range(len(outs)):
            copy(a, c).start()

    def finish():
        for a in range(len(outs)):
            copy(a, c).wait_send()
            copy(a, 1 - c).wait_recv()

    return send, finish


def _allgather_steps(ins, outs, send_sems, recv_sems, local_sems):
    n = len(ins)
    x, y, c = _position()
    me, sibling = (x, y, c), (x, y, 1 - c)
    chips = _other_chips(x, y)

    def slot(a, dev):
        return outs[a].at[4 * dev[0] + 2 * dev[1] + dev[2]]

    def copy(a, k, block, to, own=False):
        return _remote(ins[a] if own else slot(a, block), slot(a, block),
                       send_sems.at[7 * a + k], recv_sems.at[7 * a + k], to)

    def first(a):
        return [copy(a, 0, me, sibling, own=True)] + [copy(a, 1 + k, me, (*chips[k], c), own=True) for k in range(3)]

    def local(a):
        return pltpu.make_async_copy(ins[a], slot(a, me), local_sems.at[a])

    def send():
        for a in range(n):
            local(a).start()
            for cp in first(a):
                cp.start()

    def forward():
        for a in range(n):
            for k in range(3):
                copy(a, 1 + k, (*chips[k], c), me).wait_recv()
                copy(a, 4 + k, (*chips[k], c), sibling).start()

    def finish():
        for a in range(n):
            copy(a, 0, sibling, me).wait_recv()
            for k in range(3):
                copy(a, 4 + k, (*chips[k], 1 - c), me).wait_recv()
        for a in range(n):
            for cp in first(a) + [copy(a, 4 + k, (*chips[k], c), sibling) for k in range(3)]:
                cp.wait_send()
            local(a).wait()

    return send, forward, finish


def _allreduce_small(packs):
    n = len(packs)

    def body(*refs):
        ins, outs, gath = refs[:n], refs[n:2 * n], refs[2 * n:3 * n]
        send_sems, recv_sems = refs[3 * n:]
        x, y, c = _position()
        me, sibling = (x, y, c), (x, y, 1 - c)
        chips = _other_chips(x, y)

        def slot(a, dev):
            return gath[a].at[4 * dev[0] + 2 * dev[1] + dev[2]]

        def copy(a, k, block, to, src=None):
            return _remote(slot(a, block) if src is None else src, slot(a, block),
                           send_sems.at[7 * a + k], recv_sems.at[7 * a + k], to)

        started = []
        for a in range(n):
            slot(a, me)[...] = ins[a][...]
            first = [copy(a, 0, me, sibling, src=ins[a])]
            first += [copy(a, 1 + k, me, (*chip, c), src=ins[a]) for k, chip in enumerate(chips)]
            for cp in first:
                cp.start()
            started += first
        for a in range(n):
            for k, chip in enumerate(chips):
                copy(a, 1 + k, (*chip, c), me).wait_recv()
                cp = copy(a, 4 + k, (*chip, c), sibling)
                cp.start()
                started.append(cp)
        for a in range(n):
            copy(a, 0, sibling, me).wait_recv()
            for k, chip in enumerate(chips):
                copy(a, 4 + k, (*chip, 1 - c), me).wait_recv()
        for cp in started:
            cp.wait_send()
        for a in range(n):
            total = gath[a][0]
            for d in range(1, N_DEV):
                total = total + gath[a][d]
            outs[a][...] = total

    vmem = pl.BlockSpec(memory_space=pltpu.VMEM)
    sem = pltpu.SemaphoreType.DMA((7 * n,))
    return pl.pallas_call(
        body, name="allreduce_small", in_specs=[vmem] * n, out_specs=[vmem] * n,
        out_shape=[jax.ShapeDtypeStruct(p.shape, p.dtype) for p in packs],
        scratch_shapes=[pltpu.VMEM((N_DEV,) + p.shape, p.dtype) for p in packs] + [sem, sem],
        compiler_params=pltpu.CompilerParams(has_side_effects=True, vmem_limit_bytes=VMEM_LIMIT_BYTES),
    )(*packs)


LOSS_ROW = 520


def _pad_rows(a, rows=8):
    return jnp.concatenate([a, jnp.zeros((rows - a.shape[0], a.shape[1]), a.dtype)], axis=0)

def _adam_small(wide, mid, sgu, pool, late, params):
    names = ["mix_norm_g", "mlp_norm_g", "final_norm_g", "conv_b", "conv_w", "sgu_norm_g", "sgu_norm_b",
             "pool_w", "pool_scale", "sgu_w", "sgu_b"]
    n = len(names)

    def body(*refs):
        wmv = refs[5:5 + 3 * n]
        outs = refs[5 + 3 * n:]
        x, y, _ = _position()
        q = 2 * x + y

        def total(ref):
            t = ref[0]
            for dev in range(1, N_DEV):
                t = t + ref[dev]
            return t

        wide_sum, mid_sum, sgu_sum, pool_sum = total(refs[0]), total(refs[1]), total(refs[2]), total(refs[3])
        late_ref = refs[4]

        def my_quarter(rows):
            parts = [rows[:, s * TILE:(s + 1) * TILE] for s in range(N_CHIP)]
            return jnp.where(q == 0, parts[0], jnp.where(q == 1, parts[1], jnp.where(q == 2, parts[2], parts[3])))

        def tiles(pack):
            return [((0, g), pack[g * TILE:(g + 1) * TILE, :]) for g in range(4)]

        grads = {
            "mix_norm_g": [((), wide_sum[0:2, :] + late_ref[0:2, :])],
            "mlp_norm_g": [((), wide_sum[8:10, :])],
            "final_norm_g": [((), wide_sum[16:17, :])],
            "conv_b": [((), mid_sum[0:1, :])],
            "conv_w": [((0,), my_quarter(mid_sum[8:11, :]))],
            "sgu_norm_g": [((), my_quarter(mid_sum[16:17, :]))],
            "sgu_norm_b": [((), my_quarter(mid_sum[24:25, :]))],
            "pool_w": tiles(pool_sum),
            "sgu_w": tiles(sgu_sum),
            "pool_scale": [((0,), pool_sum[512:516, :])],
            "sgu_b": [((0,), sgu_sum[512:516, :])],
        }
        outs[4 * n][...] = sgu_sum[LOSS_ROW:LOSS_ROW + 8, :]
        for i, name in enumerate(names):
            w_ref, m_ref, v_ref = wmv[3 * i:3 * i + 3]
            for lead, g in grads[name]:
                idx = lead + (slice(None), slice(None))
                delta, m_new, v_new = _adam_math(w_ref[idx], g, m_ref[idx], v_ref[idx])
                outs[4 * i][idx] = g
                outs[4 * i + 1][idx] = delta
                outs[4 * i + 2][idx] = m_new
                outs[4 * i + 3][idx] = v_new

    vmem = pl.BlockSpec(memory_space=pltpu.VMEM)
    args, out_shape = [wide, mid, sgu, pool, late], []
    for name in names:
        w, m, v = params[name]
        args += [w, m, v]
        out_shape += [jax.ShapeDtypeStruct(w.shape, F32)] * 4
    out_shape.append(jax.ShapeDtypeStruct((8, TILE), F32))
    res = pl.pallas_call(
        body, name="adam_small", in_specs=[vmem] * len(args), out_specs=[vmem] * len(out_shape),
        out_shape=out_shape, compiler_params=pltpu.CompilerParams(vmem_limit_bytes=VMEM_LIMIT_BYTES),
    )(*args)
    return {name: res[4 * i:4 * i + 4] for i, name in enumerate(names)}, res[4 * n]


def _pair_sums(grads, got, pos, tag):
    return [_pair_sum(a, b, pos, name=f"pair_sum_{tag}{i}") for i, (a, b) in enumerate(zip(grads, got))]


def _chip_sums(sums, landed, pos, tag):
    return [_chip_sum(s, r, pos, name=f"chip_sum_{tag}{i}") for i, (s, r) in enumerate(zip(sums, landed))]


def kernel(x, mix_norm_g, mlp_norm_g, ab_w_in, pool_w, pool_scale, conv_w, conv_b, ab_w_out, cd_w_in, sgu_norm_g, sgu_norm_b, sgu_w, sgu_b, cd_w_out, mlp_w1, mlp_w2, final_norm_g, loss_target, m_mix_norm_g, m_mlp_norm_g, m_ab_w_in, m_pool_w, m_pool_scale, m_conv_w, m_conv_b, m_ab_w_out, m_cd_w_in, m_sgu_norm_g, m_sgu_norm_b, m_sgu_w, m_sgu_b, m_cd_w_out, m_mlp_w1, m_mlp_w2, m_final_norm_g, v_mix_norm_g, v_mlp_norm_g, v_ab_w_in, v_pool_w, v_pool_scale, v_conv_w, v_conv_b, v_ab_w_out, v_cd_w_in, v_sgu_norm_g, v_sgu_norm_b, v_sgu_w, v_sgu_b, v_cd_w_out, v_mlp_w1, v_mlp_w2, v_final_norm_g):
    nseq, t_len, d = x.shape
    m_tok = nseq * t_len
    h0 = x.reshape(m_tok, d)
    target = loss_target.reshape(m_tok, d)

    x_idx, y_idx = lax.axis_index("x"), lax.axis_index("y")
    q_idx = 2 * x_idx + y_idx
    pos = jnp.stack([q_idx, lax.axis_index("c")]).astype(jnp.int32)

    def shard_buffer(w, layer, tag):
        return _cast_place(w, layer, pos, name=f"cast_place_{tag}")

    def row_block(w):
        return w.reshape(1, 1, -1, w.shape[-1])

    (buf_ab_out, buf_w1_0, buf_w2_0, buf_cd_in, *later_weights), ((w_ab_in,),) = _cast_place_all(
        [(ab_w_out, 0), (mlp_w1, 0), (mlp_w2, 0), (cd_w_in, 0), (cd_w_out, 0), (mlp_w1, 1), (mlp_w2, 1)], pos,
        name="cast_place_rest", rider=[("gather", [shard_buffer(ab_w_in, 0, "ab_in")])])

    pool_w3, pool_scale3 = pool_w[0], pool_scale[0].reshape(4, 1, TILE)
    sgu_w3 = sgu_w[0]
    sgu_w3_t = jnp.swapaxes(sgu_w3, 1, 2)
    sgu_bias_tile = jnp.broadcast_to(sgu_b[0][:, :, None], (4, TILE, TILE))
    conv_b2 = conv_b

    def place_quarter(v):
        return lax.dynamic_update_slice(jnp.zeros((v.shape[0], 4 * TILE), F32), v, (0, q_idx * TILE))

    sharded_small = jnp.concatenate(
        [place_quarter(conv_w[0]), place_quarter(sgu_norm_g), place_quarter(sgu_norm_b),
         jnp.zeros((3, 4 * TILE), F32)], axis=0)
    sharded_small, = _allreduce_small([sharded_small])
    sharded_small = sharded_small * 0.5
    conv_w_full = sharded_small[0:3]
    sgu_g_full = sharded_small[3:4]
    sgu_b_full = sharded_small[4:5]

    xn0 = _rms_fwd(h0, mix_norm_g[0:1], name="rms_fwd_mix0")
    p_ab, ((w_1_0,),) = _mm_nn(xn0, w_ab_in, 0, out_dtype=BF16, name="ab_in_proj",
                               rider=[("gather", [buf_w1_0])])
    mix0, ((w_ab_out,),) = _ab_fwd(p_ab, pool_w3, pool_scale3, conv_w_full, conv_b2, nseq, t_len,
                                   rider=[("gather", [buf_ab_out])])
    w_ab_out = row_block(w_ab_out)
    h1, hn0 = _mm_nn(mix0, w_ab_out, 0, out_dtype=F32, name="ab_out_proj", epilogue="residual", extra=h0,
                     norm_g=mlp_norm_g[0:1])
    (act0, relu0), ((w_2_0,),) = _mm_nn(hn0, w_1_0, 0, out_dtype=BF16, name="mlp0_up", epilogue="relu2",
                                        rider=[("gather", [buf_w2_0])])
    w_2_0 = row_block(w_2_0)
    (h2, xn1), ((w_cd_in,),) = _mm_nn(act0, w_2_0, 0, out_dtype=F32, name="mlp0_down", epilogue="residual", extra=h1,
                                      norm_g=mix_norm_g[1:2],
                                      rider=[("gather", [buf_cd_in])])

    p_cd = _mm_nn(xn1, w_cd_in, 0, out_dtype=BF16, name="cd_in_proj")
    mix1 = _sgu_fwd(p_cd, sgu_g_full, sgu_b_full, sgu_w3, sgu_bias_tile)
    mix1, ltot, (w_cd_out, w_1_1, w_2_1) = _sb_fwd(p_cd, mix1, nseq, t_len, later_weights)
    w_cd_out, w_2_1 = row_block(w_cd_out), row_block(w_2_1)
    h3, hn1 = _mm_nn(mix1, w_cd_out, 0, out_dtype=F32, name="cd_out_proj", epilogue="residual", extra=h2,
                     norm_g=mlp_norm_g[1:2])
    act1, relu1 = _mm_nn(hn1, w_1_1, 0, out_dtype=BF16, name="mlp1_up", epilogue="relu2")

    dh4, dh4_bf, dg_final, loss_tile = _mlp_down_loss(act1, w_2_1, h3, final_norm_g.reshape(1, d), target)

    def as_pieces(g):
        return g.reshape(1, N_CHIP, -1, g.shape[-1]) if g.shape[1] == 1 else g

    dz1 = _mm_nt(dh4_bf, w_2_1, 0, out_dtype=BF16, name="mlp1_down_bwd", epilogue="relu2_bwd", extra=relu1)
    g_w2_1 = as_pieces(_mm_tn(act1, dh4_bf, 1, name="mlp1_down_wgrad"))
    g_w1_1 = _mm_tn(hn1, dz1, N_CHIP, name="mlp1_up_wgrad")
    (dh3, dh3_bf, dg_mlp1), (got_a,) = _mm_nt(
        dz1, w_1_1, 0, out_dtype=F32, name="mlp1_up_bwd", epilogue="rms_bwd",
        extra=(h3, mlp_norm_g[1:2], dh4), rider=[("swap", [g_w1_1, g_w2_1])])

    g_cd_out = as_pieces(_mm_tn(mix1, dh3_bf, 1, name="cd_out_wgrad"))
    dmix1, (got_cd_out,) = _mm_nt(dh3_bf, w_cd_out, 0, out_dtype=BF16, name="cd_out_bwd",
                                  rider=[("swap", [g_cd_out])])
    sums_a = _pair_sums([g_w1_1, g_w2_1, g_cd_out], got_a + got_cd_out, pos, "a")
    du, dv, dsgu_w, dsgu_bs, dsgu_g, dsgu_b = _sgu_bwd(p_cd, dmix1, sgu_g_full, sgu_b_full, sgu_w3, sgu_w3_t,
                                                      sgu_bias_tile)
    dq, dk, dvv, landed_a = _sb_bwd(p_cd, dmix1, ltot, nseq, t_len, sums_a)
    halves_a = _chip_sums(sums_a, landed_a, pos, "a")
    dp_cd = jnp.concatenate([du, dv, dq, dk, dvv], axis=1)
    g_cd_in, ((r_w1_1, r_w2_1, r_cd_out),) = _mm_tn(xn1, dp_cd, N_CHIP, name="cd_in_wgrad",
                                                    rider=[("join", halves_a)])
    (dh2, dh2_bf, dg_mix1), (got_c,) = _mm_nt(
        dp_cd, w_cd_in, 0, out_dtype=F32, name="cd_in_bwd", epilogue="rms_bwd",
        extra=(h2, mix_norm_g[1:2], dh3), rider=[("swap", [g_cd_in])])

    sums_c = _pair_sums([g_cd_in], got_c, pos, "c")
    dz0, (landed_c,) = _mm_nt(dh2_bf, w_2_0, 0, out_dtype=BF16, name="mlp0_down_bwd", epilogue="relu2_bwd",
                              extra=relu0, rider=[("exchange", sums_c)])
    halves_c = _chip_sums(sums_c, landed_c, pos, "c")
    sgu_pack = jnp.concatenate([dsgu_w.reshape(4 * TILE, TILE), _pad_rows(dsgu_bs[:, :, 0]), loss_tile], axis=0)
    g_w2_0, ((r_cd_in,), (sgu_pack,)) = _mm_tn(act0, dh2_bf, 1, name="mlp0_down_wgrad",
                                               rider=[("join", halves_c), ("allgather", [sgu_pack])])
    g_w2_0 = as_pieces(g_w2_0)
    g_w1_0, (got_d,) = _mm_tn(hn0, dz0, N_CHIP, name="mlp0_up_wgrad", rider=[("swap", [g_w2_0])])
    sums_d = _pair_sums([g_w2_0], got_d, pos, "d")
    (dh1, dh1_bf, dg_mlp0), (landed_d, got_e) = _mm_nt(
        dz0, w_1_0, 0, out_dtype=F32, name="mlp0_up_bwd", epilogue="rms_bwd",
        extra=(h1, mlp_norm_g[0:1], dh2), rider=[("exchange", sums_d), ("swap", [g_w1_0])])
    halves_d = _chip_sums(sums_d, landed_d, pos, "d")
    sums_e = _pair_sums([g_w1_0], got_e, pos, "e")

    dmix0, ((r_w2_0,),) = _mm_nt(dh1_bf, w_ab_out, 0, out_dtype=BF16, name="ab_out_bwd", rider=[("join", halves_d)])
    g_ab_out = as_pieces(_mm_tn(mix0, dh1_bf, 1, name="ab_out_wgrad"))
    (da, dxb, dgb, dgc, dpool_w, dpool_scale, dconv_w, dconv_b), (landed_e, got_f) = _ab_bwd(
        p_ab, dmix0, pool_w3, pool_scale3, conv_w_full, conv_b2, nseq, t_len,
        rider=[("exchange", sums_e), ("swap", [g_ab_out])])
    halves_e = _chip_sums(sums_e, landed_e, pos, "e")
    sums_f = _pair_sums([g_ab_out], got_f, pos, "f")
    dp_ab = jnp.concatenate([da, dxb, dgb, dgc], axis=1)
    wide = jnp.concatenate([_pad_rows(jnp.concatenate([jnp.zeros_like(dg_mix1), dg_mix1], axis=0)),
                            _pad_rows(jnp.concatenate([dg_mlp0, dg_mlp1], axis=0)), _pad_rows(dg_final)], axis=0)
    mid = jnp.concatenate([_pad_rows(dconv_b), _pad_rows(dconv_w), _pad_rows(dsgu_g), _pad_rows(dsgu_b)], axis=0)
    pool_pack = jnp.concatenate([dpool_w.reshape(4 * TILE, TILE), _pad_rows(dpool_scale.reshape(4, TILE))], axis=0)
    g_ab_in, (landed_f, (r_w1_0,), (wide, mid, pool_pack)) = _mm_tn(
        xn0, dp_ab, N_CHIP, name="ab_in_wgrad",
        rider=[("exchange", sums_f), ("join", halves_e), ("allgather", [wide, mid, pool_pack])])
    halves_f = _chip_sums(sums_f, landed_f, pos, "f")
    sums_g = _pair_sums([g_ab_in], _swap_halves([g_ab_in], name="swap_halves_g"), pos, "g")
    (grad_x, _, dg_mix0), (landed_g, (r_ab_out,)) = _mm_nt(
        dp_ab, w_ab_in, 0, out_dtype=F32, name="ab_in_bwd", epilogue="rms_bwd",
        extra=(h0, mix_norm_g[0:1], dh1), rider=[("exchange", sums_g), ("join", halves_f)])
    r_ab_in, = _join_halves(_chip_sums(sums_g, landed_g, pos, "g"), name="join_halves_g")

    big_out = {
        "ab_w_in": _adam_big(ab_w_in, m_ab_w_in, v_ab_w_in, [r_ab_in], name="adam_ab_w_in"),
        "ab_w_out": _adam_big(ab_w_out, m_ab_w_out, v_ab_w_out, [r_ab_out], name="adam_ab_w_out"),
        "cd_w_in": _adam_big(cd_w_in, m_cd_w_in, v_cd_w_in, [r_cd_in], name="adam_cd_w_in"),
        "cd_w_out": _adam_big(cd_w_out, m_cd_w_out, v_cd_w_out, [r_cd_out], name="adam_cd_w_out"),
        "mlp_w1": _adam_big(mlp_w1, m_mlp_w1, v_mlp_w1, [r_w1_0, r_w1_1], name="adam_mlp_w1"),
        "mlp_w2": _adam_big(mlp_w2, m_mlp_w2, v_mlp_w2, [r_w2_0, r_w2_1], name="adam_mlp_w2"),
    }

    late, = _allreduce_small([_pad_rows(dg_mix0)])
    small_out, loss_sum = _adam_small(wide, mid, sgu_pack, pool_pack, late, {
        "mix_norm_g": (mix_norm_g, m_mix_norm_g, v_mix_norm_g),
        "mlp_norm_g": (mlp_norm_g, m_mlp_norm_g, v_mlp_norm_g),
        "final_norm_g": tuple(a.reshape(1, d) for a in (final_norm_g, m_final_norm_g, v_final_norm_g)),
        "conv_b": (conv_b, m_conv_b, v_conv_b),
        "conv_w": (conv_w, m_conv_w, v_conv_w),
        "sgu_norm_g": (sgu_norm_g, m_sgu_norm_g, v_sgu_norm_g),
        "sgu_norm_b": (sgu_norm_b, m_sgu_norm_b, v_sgu_norm_b),
        "pool_w": (pool_w, m_pool_w, v_pool_w),
        "pool_scale": (pool_scale, m_pool_scale, v_pool_scale),
        "sgu_w": (sgu_w, m_sgu_w, v_sgu_w),
        "sgu_b": (sgu_b, m_sgu_b, v_sgu_b),
    })
    small_out["final_norm_g"] = [a.reshape(d) for a in small_out["final_norm_g"]]

    order = ["mix_norm_g", "mlp_norm_g", "ab_w_in", "pool_w", "pool_scale", "conv_w", "conv_b", "ab_w_out",
             "cd_w_in", "sgu_norm_g", "sgu_norm_b", "sgu_w", "sgu_b", "cd_w_out", "mlp_w1", "mlp_w2",
             "final_norm_g"]
    both = {**big_out, **small_out}
    loss = loss_sum[0, 0]
    outs = [loss, grad_x.reshape(nseq, t_len, d)]
    for kind in range(4):
        outs += [both[name][kind] for name in order]
    return tuple(outs)
```

```python
import math

import jax
import jax.numpy as jnp
from jax import lax
from jax.experimental import pallas as pl
from jax.experimental.pallas import tpu as pltpu

F32 = jnp.float32
BF16 = jnp.bfloat16
MESH = pl.DeviceIdType.MESH

EPS = 1e-6
TILE = 128
N_CHIP = 4
N_DEV = 8
VMEM_LIMIT_BYTES = 56 * 1024 * 1024

ADAM_LR = 0.001
ADAM_B1 = 0.9
ADAM_B2 = 0.999
ADAM_EPS = 1e-08
ADAM_WD = 0.01
ADAM_STEP = 10

NT_DIMS = (((1,), (1,)), ((), ()))
TN_DIMS = (((0,), (0,)), ((), ()))


def _params(sem=None):
    return pltpu.CompilerParams(dimension_semantics=sem, vmem_limit_bytes=VMEM_LIMIT_BYTES)


def _call(body, *, name, grid, in_specs, out_specs, out_shape, scratch_shapes, semantics, args, rider=None,
          prefetch=None):
    npre = 0 if prefetch is None else 1

    def launch(kernel, in_specs, out_specs, out_shape, scratch_shapes, operands, aliases, params):
        if prefetch is None:
            return pl.pallas_call(kernel, name=name, grid=grid, in_specs=in_specs, out_specs=out_specs,
                                  out_shape=out_shape, scratch_shapes=scratch_shapes, input_output_aliases=aliases,
                                  compiler_params=params)(*operands)
        spec = pltpu.PrefetchScalarGridSpec(num_scalar_prefetch=1, grid=grid, in_specs=in_specs, out_specs=out_specs,
                                            scratch_shapes=scratch_shapes)
        return pl.pallas_call(kernel, name=name, grid_spec=spec, out_shape=out_shape,
                              input_output_aliases={k + 1: v for k, v in aliases.items()},
                              compiler_params=params)(prefetch, *operands)

    if not rider:
        res = launch(body, list(in_specs), list(out_specs), list(out_shape), list(scratch_shapes), args, {},
                     _params(semantics))
        return list(res), []
    plans = [_rider_plan(kind, arrays) for kind, arrays in rider]
    arrays = [a for _, group in rider for a in group]
    nr, n_in, n_out, n_scr = len(arrays), len(in_specs), len(out_specs), len(scratch_shapes)
    first_out, first_scr = n_in + nr, n_in + nr + n_out + nr
    last_step = math.prod(grid) - 1

    def riding(*refs):
        pre, refs = refs[:npre], refs[npre:]
        step = 0
        for axis, size in enumerate(grid):
            step = step * size + pl.program_id(axis)
        steps, at, sem_at = [], 0, first_scr + n_scr
        for (kind, group), (_, sems, _) in zip(rider, plans):
            k = len(group)
            steps.append(_rider_steps(kind, refs[n_in + at:n_in + at + k],
                                      refs[first_out + n_out + at:first_out + n_out + at + k],
                                      refs[sem_at:sem_at + len(sems)]))
            at, sem_at = at + k, sem_at + len(sems)
        for send, _, _ in steps:
            pl.when(step == 0)(send)
        for _, forward, _ in steps:
            if forward is not None:
                pl.when(step == last_step)(forward)
        body(*pre, *refs[:n_in], *refs[first_out:first_out + n_out], *refs[first_scr:first_scr + n_scr])
        for _, _, finish in steps:
            pl.when(step == last_step)(finish)

    aliases, at = {}, 0
    for (_, group), (_, _, aliased) in zip(rider, plans):
        if aliased:
            aliases.update({n_in + at + a: n_out + at + a for a in range(len(group))})
        at += len(group)
    res = launch(
        riding, list(in_specs) + [ANY] * nr, list(out_specs) + [ANY] * nr,
        list(out_shape) + [s for shapes, _, _ in plans for s in shapes],
        list(scratch_shapes) + [s for _, sems, _ in plans for s in sems], [*args, *arrays], aliases,
        pltpu.CompilerParams(dimension_semantics=("arbitrary",) * len(grid), vmem_limit_bytes=VMEM_LIMIT_BYTES,
                             has_side_effects=True))
    rode, at = [], n_out
    for _, group in rider:
        rode.append(list(res[at:at + len(group)]))
        at += len(group)
    return list(res[:n_out]), rode


def _rider_plan(kind, arrays):
    n = len(arrays)
    same = [jax.ShapeDtypeStruct(a.shape, a.dtype) for a in arrays]
    pair = [pltpu.SemaphoreType.DMA((n,))] * 2
    if kind == "gather":
        return same, _gather_sems(n), True
    if kind == "exchange":
        return _exchange_shapes(arrays), _exchange_sems(n), False
    if kind == "swap":
        return _swap_shapes(arrays), pair, False
    if kind == "allgather":
        return ([jax.ShapeDtypeStruct((N_DEV,) + a.shape, a.dtype) for a in arrays],
                [pltpu.SemaphoreType.DMA((7 * n,))] * 2 + [pltpu.SemaphoreType.DMA((n,))], False)
    assert kind == "join"
    return same, pair, True


def _rider_steps(kind, ins, outs, sems):
    if kind == "gather":
        return _gather_steps(outs, *sems)
    if kind == "allgather":
        return _allgather_steps(ins, outs, *sems)
    if kind == "exchange":
        send, finish = _exchange_steps(ins, outs, *sems)
    elif kind == "swap":
        send, finish = _swap_steps(ins, outs, *sems)
    else:
        send, finish = _join_steps(outs, *sems)
    return send, None, finish


def _gathers(rider):
    return any(kind in ("gather", "allgather") for kind, _ in rider or ())


def _row_tile(k_dim, roomy=False):
    if k_dim > 1024:
        return 512
    return 2048 if roomy else 1024


def _mm_nn(a, b4, layer, *, out_dtype, name, epilogue=None, extra=None, norm_g=None, rider=None):
    m, k_dim = a.shape
    _, s_dim, kb, n = b4.shape
    assert kb == k_dim
    tm = min(m, _row_tile(k_dim, roomy=epilogue != "residual" and norm_g is None and not _gathers(rider)))
    tn = min(n, 1024)
    assert m % tm == 0 and n % tn == 0
    npb = n // tn
    grid = (m // tm, s_dim * npb)
    n_in = 2 + (extra is not None) + (norm_g is not None)
    two_outputs = norm_g is not None or epilogue == "relu2"
    assert norm_g is None or (tn == s_dim * n and epilogue != "relu2")

    def body(*refs):
        a_ref, b_ref = refs[:2]
        e_ref = refs[2] if extra is not None else None
        g_ref = refs[n_in - 1] if norm_g is not None else None
        o_ref = refs[n_in]
        acc = jnp.dot(a_ref[...], b_ref[...], preferred_element_type=F32)
        if epilogue == "relu2":
            r = jnp.maximum(acc, 0.0)
            refs[n_in + 1][...] = r.astype(BF16)
            acc = r * r
        elif epilogue == "residual":
            acc = acc + e_ref[...]
        o_ref[...] = acc.astype(out_dtype)
        if norm_g is not None:
            rstd = lax.rsqrt(jnp.mean(acc * acc, axis=-1, keepdims=True) + EPS)
            refs[n_in + 1][...] = (acc * rstd * g_ref[...]).astype(BF16)

    in_specs = [
        pl.BlockSpec((tm, k_dim), lambda i, j: (i, 0)),
        pl.BlockSpec((None, None, k_dim, tn), lambda i, j: (layer, j // npb, 0, j % npb)),
    ]
    args = [a, b4]
    if extra is not None:
        in_specs.append(pl.BlockSpec((tm, tn), lambda i, j: (i, j)))
        args.append(extra)
    out_block = pl.BlockSpec((tm, tn), lambda i, j: (i, j))
    out_specs, out_shape = [out_block], [jax.ShapeDtypeStruct((m, s_dim * n), out_dtype)]
    if norm_g is not None:
        in_specs.append(pl.BlockSpec((1, tn), lambda i, j: (0, j)))
        args.append(norm_g)
    if two_outputs:
        out_specs.append(out_block)
        out_shape.append(jax.ShapeDtypeStruct((m, s_dim * n), BF16))
    res, rode = _call(
        body, name=name, grid=grid, in_specs=in_specs, out_specs=out_specs, out_shape=out_shape,
        scratch_shapes=[], semantics=("parallel", "parallel"), args=args, rider=rider)
    res = res if two_outputs else res[0]
    return res if rider is None else (res, rode)


def _mm_nt(a, b4, layer, *, out_dtype, name, epilogue=None, extra=None, rider=None):
    m, k_dim = a.shape
    _, s_dim, n_out, n = b4.shape
    assert k_dim == s_dim * n
    rms = epilogue == "rms_bwd"
    roomy = not rms and out_dtype != F32 and not _gathers(rider)
    tm, tn = min(m, _row_tile(k_dim, roomy=roomy)), min(n_out, 1024)
    assert m % tm == 0 and n_out % tn == 0
    grid = (m // tm, n_out // tn)
    assert not rms or tn == n_out
    extras = [] if extra is None else (list(extra) if rms else [extra])
    n_in = 2 + len(extras)

    def body(*refs):
        a_ref, b_ref = refs[:2]
        e_refs = refs[2:n_in]
        o_ref = refs[n_in]
        acc = lax.dot_general(a_ref[:, 0:n], b_ref[0], NT_DIMS, preferred_element_type=F32)
        for s in range(1, s_dim):
            acc = acc + lax.dot_general(a_ref[:, s * n:(s + 1) * n], b_ref[s], NT_DIMS, preferred_element_type=F32)
        if epilogue == "relu2_bwd":
            acc = acc * (2.0 * e_refs[0][...].astype(F32))
        if not rms:
            o_ref[...] = acc.astype(out_dtype)
        else:
            h_ref, g_ref, dres_ref = e_refs
            dhb_ref, dg_ref = refs[n_in + 1:n_in + 3]
            hv = h_ref[...]
            rstd = lax.rsqrt(jnp.mean(hv * hv, axis=-1, keepdims=True) + EPS)
            xhat = hv * rstd
            dxhat = acc * g_ref[...]
            dh = dres_ref[...] + rstd * (dxhat - xhat * jnp.mean(dxhat * xhat, axis=-1, keepdims=True))
            o_ref[...] = dh
            dhb_ref[...] = dh.astype(BF16)
            dg_part = jnp.sum(acc * xhat, axis=0, keepdims=True)
            first = pl.program_id(0) == 0

            @pl.when(first)
            def _():
                dg_ref[...] = dg_part

            @pl.when(jnp.logical_not(first))
            def _():
                dg_ref[...] += dg_part

    in_specs = [
        pl.BlockSpec((tm, k_dim), lambda i, j: (i, 0)),
        pl.BlockSpec((None, s_dim, tn, n), lambda i, j: (layer, 0, j, 0)),
    ]
    args = [a, b4] + extras
    block = pl.BlockSpec((tm, tn), lambda i, j: (i, j))
    vec = pl.BlockSpec((1, tn), lambda i, j: (0, j))
    if rms:
        in_specs += [block, vec, block]
        out_specs = [block, block, vec]
        out_shape = [jax.ShapeDtypeStruct((m, n_out), F32), jax.ShapeDtypeStruct((m, n_out), BF16),
                     jax.ShapeDtypeStruct((1, n_out), F32)]
    else:
        in_specs += [block] * len(extras)
        out_specs, out_shape = [block], [jax.ShapeDtypeStruct((m, n_out), out_dtype)]
    res, rode = _call(
        body, name=name, grid=grid, in_specs=in_specs, out_specs=out_specs, out_shape=out_shape,
        scratch_shapes=[], semantics=("arbitrary",) * 2 if rms else ("parallel", "parallel"), args=args, rider=rider)
    res = res if rms else res[0]
    return res if rider is None else (res, rode)


def _mm_tn(a, b, s_dim, *, name, rider=None):
    m, k1 = a.shape
    mb, n_all = b.shape
    assert mb == m and n_all % s_dim == 0
    n = n_all // s_dim
    tn, t1 = min(n, 1024), min(k1, 512 if _gathers(rider) else 1024)
    assert k1 % t1 == 0 and n % tn == 0
    npb = n // tn
    grid = (k1 // t1, s_dim * npb)

    def body(a_ref, b_ref, o_ref):
        o_ref[...] = lax.dot_general(a_ref[...], b_ref[...], TN_DIMS, preferred_element_type=F32).astype(BF16)

    res, rode = _call(
        body, name=name, grid=grid,
        in_specs=[pl.BlockSpec((m, t1), lambda i, j: (0, i)), pl.BlockSpec((m, tn), lambda i, j: (0, j))],
        out_specs=[pl.BlockSpec((None, None, t1, tn), lambda i, j: (0, j // npb, i, j % npb))],
        out_shape=[jax.ShapeDtypeStruct((1, s_dim, k1, n), BF16)],
        scratch_shapes=[], semantics=("parallel", "parallel"), args=[a, b], rider=rider)
    return res[0] if rider is None else (res[0], rode)


ROW_TILE = 512


def _rms_fwd(h, g, *, name, rider=None):
    m, d = h.shape

    def body(h_ref, g_ref, o_ref):
        hv = h_ref[...]
        rstd = lax.rsqrt(jnp.mean(hv * hv, axis=-1, keepdims=True) + EPS)
        o_ref[...] = (hv * rstd * g_ref[...]).astype(BF16)

    res, rode = _call(
        body, name=name, grid=(m // ROW_TILE,),
        in_specs=[pl.BlockSpec((ROW_TILE, d), lambda i: (i, 0)), pl.BlockSpec((1, d), lambda i: (0, 0))],
        out_specs=[pl.BlockSpec((ROW_TILE, d), lambda i: (i, 0))], out_shape=[jax.ShapeDtypeStruct((m, d), BF16)],
        scratch_shapes=[], semantics=("parallel",), args=[h, g], rider=rider)
    return res[0] if rider is None else (res[0], rode)


def _mlp_down_loss(act, w_2, h_res, g, target):
    m, k_dim = act.shape
    d = w_2.shape[-1]
    tm = _row_tile(k_dim)

    def body(a_ref, b_ref, r_ref, g_ref, t_ref, dh_ref, dhb_ref, dg_ref, loss_ref):
        hv = jnp.dot(a_ref[...], b_ref[...], preferred_element_type=F32) + r_ref[...]
        gv = g_ref[...]
        rstd = lax.rsqrt(jnp.mean(hv * hv, axis=-1, keepdims=True) + EPS)
        xhat = hv * rstd
        err = xhat * gv - t_ref[...]
        dy = err * (1.0 / d)
        dxhat = dy * gv
        dh = rstd * (dxhat - xhat * jnp.mean(dxhat * xhat, axis=-1, keepdims=True))
        dh_ref[...] = dh
        dhb_ref[...] = dh.astype(BF16)
        dg_part = jnp.sum(dy * xhat, axis=0, keepdims=True)
        sq = jnp.sum(jnp.sum(err * err, axis=1, keepdims=True), axis=0, keepdims=True) * (0.5 / d)
        loss_part = jnp.broadcast_to(sq, (8, TILE))

        @pl.when(pl.program_id(0) == 0)
        def _():
            dg_ref[...] = dg_part
            loss_ref[...] = loss_part

        @pl.when(pl.program_id(0) > 0)
        def _():
            dg_ref[...] += dg_part
            loss_ref[...] += loss_part

    row = pl.BlockSpec((tm, d), lambda i: (i, 0))
    vec = pl.BlockSpec((1, d), lambda i: (0, 0))
    return pl.pallas_call(
        body, name="mlp1_down_loss", grid=(m // tm,),
        in_specs=[pl.BlockSpec((tm, k_dim), lambda i: (i, 0)),
                  pl.BlockSpec((None, None, k_dim, d), lambda i: (0, 0, 0, 0)), row, vec, row],
        out_specs=[row, row, vec, pl.BlockSpec((8, TILE), lambda i: (0, 0))],
        out_shape=[jax.ShapeDtypeStruct((m, d), F32), jax.ShapeDtypeStruct((m, d), BF16),
                   jax.ShapeDtypeStruct((1, d), F32), jax.ShapeDtypeStruct((8, TILE), F32)],
        compiler_params=_params(("arbitrary",)),
    )(act, w_2, h_res, g, target)


def _shift_down(x, s, t_idx):
    return jnp.where(t_idx >= s, pltpu.roll(x, s, 0), 0.0)


def _shift_up(x, s, t_idx, t_len):
    return jnp.where(t_idx < t_len - s, pltpu.roll(x, t_len - s, 0), 0.0)


def _pool_select(group, s2, s4, s8, s16):
    return jnp.where(group == 0, s2, jnp.where(group == 1, s4, jnp.where(group == 2, s8, s16)))


def _pool_count(group, t_idx):
    win = jnp.left_shift(2, group)
    return jnp.minimum(t_idx + 1, win).astype(F32)


def _pool_fwd_math(a, group, t_idx):
    s2 = a + _shift_down(a, 1, t_idx)
    s4 = s2 + _shift_down(s2, 2, t_idx)
    s8 = s4 + _shift_down(s4, 4, t_idx)
    s16 = s8 + _shift_down(s8, 8, t_idx)
    return _pool_select(group, s2, s4, s8, s16) / _pool_count(group, t_idx) - a


def _pool_bwd_math(dpooled, group, t_idx, t_len):
    e = dpooled / _pool_count(group, t_idx)
    s2 = e + _shift_up(e, 1, t_idx, t_len)
    s4 = s2 + _shift_up(s2, 2, t_idx, t_len)
    s8 = s4 + _shift_up(s4, 4, t_idx, t_len)
    s16 = s8 + _shift_up(s8, 8, t_idx, t_len)
    return _pool_select(group, s2, s4, s8, s16) - dpooled


def _conv_fwd_math(c, w_ref, b_ref, t_idx):
    return (w_ref[0:1, :] * _shift_down(c, 2, t_idx) + w_ref[1:2, :] * _shift_down(c, 1, t_idx)
            + w_ref[2:3, :] * c + b_ref[...])


def _ab_fwd(p, pool_w, pool_scale, conv_w, conv_b, nseq, t_len, rider=None):
    m = p.shape[0]
    ng = 4

    def body(a_ref, xb_ref, gb_ref, gc_ref, pw_ref, ps_ref, cw_ref, cb_ref, o_ref):
        j = pl.program_id(1)
        t_idx = lax.broadcasted_iota(jnp.int32, (t_len, TILE), 0)

        @pl.when(j < ng)
        def _():
            pooled = _pool_fwd_math(a_ref[...].astype(F32), j, t_idx)
            mixed = jnp.dot(pooled.astype(BF16), pw_ref[...].astype(BF16), preferred_element_type=F32)
            o_ref[...] = (mixed * ps_ref[...]).astype(BF16)

        @pl.when(j >= ng)
        def _():
            c = gc_ref[...].astype(F32) * xb_ref[...].astype(F32)
            y = _conv_fwd_math(c, cw_ref, cb_ref, t_idx)
            o_ref[...] = (gb_ref[...].astype(F32) * y).astype(BF16)

    def pool_j(j):
        return jnp.minimum(j, ng - 1)

    def conv_j(j):
        return jnp.maximum(j - ng, 0)

    in_specs = [
        pl.BlockSpec((t_len, TILE), lambda s, j: (s, pool_j(j))),
        pl.BlockSpec((t_len, TILE), lambda s, j: (s, ng + conv_j(j))),
        pl.BlockSpec((t_len, TILE), lambda s, j: (s, 2 * ng + conv_j(j))),
        pl.BlockSpec((t_len, TILE), lambda s, j: (s, 3 * ng + conv_j(j))),
        pl.BlockSpec((None, TILE, TILE), lambda s, j: (pool_j(j), 0, 0)),
        pl.BlockSpec((None, 1, TILE), lambda s, j: (pool_j(j), 0, 0)),
        pl.BlockSpec((3, TILE), lambda s, j: (0, conv_j(j))),
        pl.BlockSpec((1, TILE), lambda s, j: (0, conv_j(j))),
    ]
    res, rode = _call(
        body, name="ab_mixer_fwd", grid=(nseq, 2 * ng), in_specs=in_specs,
        out_specs=[pl.BlockSpec((t_len, TILE), lambda s, j: (s, j))],
        out_shape=[jax.ShapeDtypeStruct((m, 2 * ng * TILE), BF16)], scratch_shapes=[],
        semantics=("parallel", "arbitrary"), args=[p, p, p, p, pool_w, pool_scale, conv_w, conv_b], rider=rider)
    return res[0] if rider is None else (res[0], rode)


def _ab_bwd(p, dmix, pool_w, pool_scale, conv_w, conv_b, nseq, t_len, rider=None):
    m = p.shape[0]
    ng = 4

    def body(a_ref, xb_ref, gb_ref, gc_ref, dma_ref, dmb_ref, pw_ref, ps_ref, cw_ref, cb_ref,
             da_ref, dxb_ref, dgb_ref, dgc_ref, dpw_ref, dps_ref, dcw_ref, dcb_ref):
        j = pl.program_id(0)
        first = pl.program_id(1) == 0
        t_idx = lax.broadcasted_iota(jnp.int32, (t_len, TILE), 0)

        pooled = _pool_fwd_math(a_ref[...].astype(F32), j, t_idx).astype(BF16)
        w_bf = pw_ref[...].astype(BF16)
        mixed = jnp.dot(pooled, w_bf, preferred_element_type=F32)
        dm = dma_ref[...].astype(F32)
        dps = jnp.sum(dm * mixed, axis=0, keepdims=True)
        dmixed = (dm * ps_ref[...]).astype(BF16)
        dpw = lax.dot_general(pooled, dmixed, TN_DIMS, preferred_element_type=F32)
        dpooled = lax.dot_general(dmixed, w_bf, NT_DIMS, preferred_element_type=F32)
        da_ref[...] = _pool_bwd_math(dpooled, j, t_idx, t_len).astype(BF16)

        xb = xb_ref[...].astype(F32)
        gb = gb_ref[...].astype(F32)
        gc = gc_ref[...].astype(F32)
        d = dmb_ref[...].astype(F32)
        c = gc * xb
        c1 = _shift_down(c, 1, t_idx)
        c2 = _shift_down(c, 2, t_idx)
        y = cw_ref[0:1, :] * c2 + cw_ref[1:2, :] * c1 + cw_ref[2:3, :] * c + cb_ref[...]
        dgb_ref[...] = (d * y).astype(BF16)
        dy = d * gb
        dc = (cw_ref[2:3, :] * dy + cw_ref[1:2, :] * _shift_up(dy, 1, t_idx, t_len)
              + cw_ref[0:1, :] * _shift_up(dy, 2, t_idx, t_len))
        dgc_ref[...] = (dc * xb).astype(BF16)
        dxb_ref[...] = (dc * gc).astype(BF16)
        dcw = jnp.concatenate([jnp.sum(dy * c2, axis=0, keepdims=True),
                               jnp.sum(dy * c1, axis=0, keepdims=True),
                               jnp.sum(dy * c, axis=0, keepdims=True)], axis=0)
        dcb = jnp.sum(dy, axis=0, keepdims=True)

        @pl.when(first)
        def _():
            dpw_ref[...] = dpw
            dps_ref[...] = dps
            dcw_ref[...] = dcw
            dcb_ref[...] = dcb

        @pl.when(jnp.logical_not(first))
        def _():
            dpw_ref[...] += dpw
            dps_ref[...] += dps
            dcw_ref[...] += dcw
            dcb_ref[...] += dcb

    def col(k):
        return pl.BlockSpec((t_len, TILE), lambda j, s: (s, k * ng + j))

    in_specs = [
        col(0), col(1), col(2), col(3), col(0), col(1),
        pl.BlockSpec((None, TILE, TILE), lambda j, s: (j, 0, 0)),
        pl.BlockSpec((None, 1, TILE), lambda j, s: (j, 0, 0)),
        pl.BlockSpec((3, TILE), lambda j, s: (0, j)),
        pl.BlockSpec((1, TILE), lambda j, s: (0, j)),
    ]
    piece = pl.BlockSpec((t_len, TILE), lambda j, s: (s, j))
    out_specs = [
        piece, piece, piece, piece,
        pl.BlockSpec((None, TILE, TILE), lambda j, s: (j, 0, 0)),
        pl.BlockSpec((None, 1, TILE), lambda j, s: (j, 0, 0)),
        pl.BlockSpec((3, TILE), lambda j, s: (0, j)),
        pl.BlockSpec((1, TILE), lambda j, s: (0, j)),
    ]
    w = ng * TILE
    out_shape = [jax.ShapeDtypeStruct((m, w), BF16)] * 4 + [
        jax.ShapeDtypeStruct((ng, TILE, TILE), F32), jax.ShapeDtypeStruct((ng, 1, TILE), F32),
        jax.ShapeDtypeStruct((3, w), F32), jax.ShapeDtypeStruct((1, w), F32)]
    res, rode = _call(
        body, name="ab_mixer_bwd", grid=(ng, nseq), in_specs=in_specs, out_specs=out_specs, out_shape=out_shape,
        scratch_shapes=[], semantics=("parallel", "arbitrary"),
        args=[p, p, p, p, dmix, dmix, pool_w, pool_scale, conv_w, conv_b], rider=rider)
    return res if rider is None else (res, rode)


SGU_ROWS = 512
INV_SQRT2 = 1.0 / math.sqrt(2.0)
INV_SQRT_2PI = 1.0 / math.sqrt(2.0 * math.pi)


def _gelu(x):
    return 0.5 * x * (1.0 + lax.erf(x * INV_SQRT2))


def _gelu_grad(x):
    return 0.5 * (1.0 + lax.erf(x * INV_SQRT2)) + x * (INV_SQRT_2PI * jnp.exp(-0.5 * x * x))


def _causal_tile(transposed=False):
    r = lax.broadcasted_iota(jnp.int32, (TILE, TILE), 0)
    c = lax.broadcasted_iota(jnp.int32, (TILE, TILE), 1)
    return r <= c if transposed else c <= r


def _sgu_norm(v, g_ref, b_ref):
    mu = jnp.mean(v, axis=-1, keepdims=True)
    xc = v - mu
    rstd = lax.rsqrt(jnp.mean(xc * xc, axis=-1, keepdims=True) + EPS)
    xhat = xc * rstd
    return xhat, rstd, xhat * g_ref[...] + b_ref[...]


def _sgu_fwd(p, norm_g, norm_b, w_s, bias_tile):
    m = p.shape[0]
    ng = 4
    width = ng * TILE

    def body(u_ref, v_ref, g_ref, b_ref, w_ref, bias_ref, o_ref):
        u = _gelu(u_ref[...].astype(F32))
        _, _, vln = _sgu_norm(_gelu(v_ref[...].astype(F32)), g_ref, b_ref)
        vln = vln.astype(BF16)
        causal = _causal_tile()
        for g in range(ng):
            cols = slice(g * TILE, (g + 1) * TILE)
            wg = jnp.where(causal, w_ref[g], 0.0).astype(BF16)
            for n in range(SGU_ROWS // TILE):
                rows = slice(n * TILE, (n + 1) * TILE)
                s = jnp.dot(wg, vln[rows, cols], preferred_element_type=F32) + bias_ref[g]
                o_ref[rows, cols] = (u[rows, cols] * s).astype(BF16)

    vec = pl.BlockSpec((1, width), lambda i: (0, 0))
    tiles = pl.BlockSpec((ng, TILE, TILE), lambda i: (0, 0, 0))
    return pl.pallas_call(
        body, name="sgu_fwd", grid=(m // SGU_ROWS,),
        in_specs=[pl.BlockSpec((SGU_ROWS, width), lambda i: (i, 0)),
                  pl.BlockSpec((SGU_ROWS, width), lambda i: (i, 1)), vec, vec, tiles, tiles],
        out_specs=pl.BlockSpec((SGU_ROWS, width), lambda i: (i, 0)),
        out_shape=jax.ShapeDtypeStruct((m, 2 * width), BF16),
        compiler_params=_params(("parallel",)),
    )(p, p, norm_g, norm_b, w_s, bias_tile)


def _sgu_bwd(p, dmix, norm_g, norm_b, w_s, w_s_t, bias_tile):
    m = p.shape[0]
    ng = 4
    width = ng * TILE

    def body(u_ref, v_ref, dc_ref, g_ref, b_ref, w_ref, wt_ref, bias_ref,
             du_ref, dv_ref, dw_ref, dbs_ref, dg_ref, db_ref, ds_scr, dvln_scr):
        u_pre = u_ref[...].astype(F32)
        v_pre = v_ref[...].astype(F32)
        u = _gelu(u_pre)
        xhat, rstd, vln = _sgu_norm(_gelu(v_pre), g_ref, b_ref)
        vln = vln.astype(BF16)
        dc = dc_ref[...].astype(F32)
        causal = _causal_tile()
        ones = jnp.ones((TILE, TILE), BF16)
        first = pl.program_id(0) == 0
        for g in range(ng):
            cols = slice(g * TILE, (g + 1) * TILE)
            wg = jnp.where(causal, w_ref[g], 0.0).astype(BF16)
            wgt = jnp.where(_causal_tile(transposed=True), wt_ref[g], 0.0).astype(BF16)
            dw_acc = jnp.zeros((TILE, TILE), F32)
            dbs_acc = jnp.zeros((TILE, TILE), F32)
            for n in range(SGU_ROWS // TILE):
                rows = slice(n * TILE, (n + 1) * TILE)
                vt = vln[rows, cols]
                s = jnp.dot(wg, vt, preferred_element_type=F32) + bias_ref[g]
                ds_scr[rows, cols] = dc[rows, cols] * s
                ds = (dc[rows, cols] * u[rows, cols]).astype(BF16)
                dw_acc += lax.dot_general(ds, vt, NT_DIMS, preferred_element_type=F32)
                dbs_acc += jnp.dot(ds, ones, preferred_element_type=F32)
                dvln_scr[rows, cols] = jnp.dot(wgt, ds, preferred_element_type=F32)
            dw_g = jnp.where(causal, dw_acc, 0.0)

            @pl.when(first)
            def _():
                dw_ref[g] = dw_g
                dbs_ref[g] = dbs_acc

            @pl.when(jnp.logical_not(first))
            def _():
                dw_ref[g] += dw_g
                dbs_ref[g] += dbs_acc

        du_ref[...] = (ds_scr[...] * _gelu_grad(u_pre)).astype(BF16)
        dvln = dvln_scr[...]
        dxhat = dvln * g_ref[...]
        dv = rstd * (dxhat - jnp.mean(dxhat, axis=-1, keepdims=True)
                     - xhat * jnp.mean(dxhat * xhat, axis=-1, keepdims=True))
        dv_ref[...] = (dv * _gelu_grad(v_pre)).astype(BF16)
        dg_part = jnp.sum(dvln * xhat, axis=0, keepdims=True)
        db_part = jnp.sum(dvln, axis=0, keepdims=True)

        @pl.when(first)
        def _():
            dg_ref[...] = dg_part
            db_ref[...] = db_part

        @pl.when(jnp.logical_not(first))
        def _():
            dg_ref[...] += dg_part
            db_ref[...] += db_part

    vec = pl.BlockSpec((1, width), lambda i: (0, 0))
    tiles = pl.BlockSpec((ng, TILE, TILE), lambda i: (0, 0, 0))
    rows0 = pl.BlockSpec((SGU_ROWS, width), lambda i: (i, 0))
    rows1 = pl.BlockSpec((SGU_ROWS, width), lambda i: (i, 1))
    return pl.pallas_call(
        body, name="sgu_bwd", grid=(m // SGU_ROWS,),
        in_specs=[rows0, rows1, rows0, vec, vec, tiles, tiles, tiles],
        out_specs=[rows0, rows0, tiles, tiles, vec, vec],
        out_shape=[jax.ShapeDtypeStruct((m, width), BF16), jax.ShapeDtypeStruct((m, width), BF16),
                   jax.ShapeDtypeStruct((ng, TILE, TILE), F32), jax.ShapeDtypeStruct((ng, TILE, TILE), F32),
                   jax.ShapeDtypeStruct((1, width), F32), jax.ShapeDtypeStruct((1, width), F32)],
        scratch_shapes=[pltpu.VMEM((SGU_ROWS, width), F32), pltpu.VMEM((SGU_ROWS, width), F32)],
        compiler_params=_params(("arbitrary",)),
    )(p, p, dmix, norm_g, norm_b, w_s, w_s_t, bias_tile)


SB_DH = 64
SB_SCALE = 1.0 / math.sqrt(SB_DH)


SB_BLOCK = 256
SB_SUB = SB_BLOCK // TILE
SB_PASS = 4


def _split_passes(i):
    rem = i % SB_PASS
    return i // SB_PASS, rem >= 2, rem % 2 == 1


def _sum_matrix(kind):
    j = lax.broadcasted_iota(jnp.int32, (TILE, 2 * TILE), 0)
    s = lax.broadcasted_iota(jnp.int32, (TILE, 2 * TILE), 1)
    tri = {"after": j > s, "upto": j <= s, "before": j < s}[kind]
    return jnp.where(jnp.logical_or(s >= TILE, tri), 1.0, 0.0).astype(BF16)


def _strict_mask():
    r = lax.broadcasted_iota(jnp.int32, (SB_BLOCK, SB_BLOCK), 0)
    c = lax.broadcasted_iota(jnp.int32, (SB_BLOCK, SB_BLOCK), 1)
    return c < r


def _head_lanes(h):
    lane = lax.broadcasted_iota(jnp.int32, (1, TILE), 1)
    return (lane >= h * SB_DH) & (lane < (h + 1) * SB_DH)


def _log_gates(z):
    log_sig = jnp.minimum(z, 0.0) - jnp.log(1.0 + jnp.exp(-jnp.abs(z)))
    return log_sig, log_sig - z


def _sb_fwd(p, mix, nseq, t_len, gather):
    m = p.shape[0]
    npair = 4
    ng = len(gather)
    last_step = nseq * npair - 1

    def body(q_ref, k_ref, v_ref, *rest):
        o_ref, lt_ref = rest[ng + 1:ng + 3]
        kh_ref, vh_ref = rest[2 * ng + 3:2 * ng + 5]
        step = pl.program_id(0) * npair + pl.program_id(1)
        send, forward, finish = _gather_steps(rest[ng + 3:2 * ng + 3], *rest[2 * ng + 5:])
        pl.when(step == 0)(send)
        pl.when(step == (last_step + 1) // 2)(forward)
        for h in range(2):
            keep = _head_lanes(h)
            kh_ref[h] = jnp.where(keep, k_ref[...], 0).astype(BF16)
            vh_ref[h] = jnp.where(keep, v_ref[...], 0).astype(BF16)
        summat = _sum_matrix("after")
        strict = _strict_mask()

        def one_pass(q, row0, nsub, diag, state):
            rows = pl.ds(row0, nsub * TILE)
            log_sig, pieces = [], []
            for h in range(2):
                zh = lax.dot_general(q, kh_ref[h, rows, :], NT_DIMS, preferred_element_type=F32)
                log_sig_h, logkeep = _log_gates(zh)
                if diag:
                    logkeep = jnp.where(strict, logkeep, 0.0)
                log_sig.append(log_sig_h)
                pieces += [logkeep[:, b * TILE:(b + 1) * TILE] for b in range(nsub)]
            sums = jnp.dot(jnp.concatenate(pieces, axis=0).astype(BF16), summat, preferred_element_type=F32)
            out = []
            for h in range(2):
                carry, acc = state[2 * h], state[2 * h + 1]
                after = [None] * nsub
                for b in reversed(range(nsub)):
                    part = sums[(h * nsub + b) * SB_BLOCK:(h * nsub + b + 1) * SB_BLOCK]
                    after[b] = part[:, :TILE] + carry
                    carry = carry + part[:, TILE:]
                w = jnp.exp(log_sig[h] + jnp.concatenate(after, axis=1))
                if diag:
                    w = jnp.where(strict, w, 0.0)
                out += [carry, acc + jnp.dot(w.astype(BF16), vh_ref[h, rows, :], preferred_element_type=F32)]
            return tuple(out)

        def q_block(i, _):
            r0 = pl.multiple_of(i * SB_BLOCK, SB_BLOCK)
            q = q_ref[pl.ds(r0, SB_BLOCK), :] * SB_SCALE
            zero = jnp.zeros((SB_BLOCK, TILE), F32)
            state = one_pass(q, r0, SB_SUB, True, (zero,) * 4)
            full, two, one = _split_passes(i)
            state = lax.fori_loop(
                0, full,
                lambda jj, st: one_pass(q, pl.multiple_of((i - SB_PASS * (jj + 1)) * SB_BLOCK, SB_BLOCK),
                                        SB_PASS * SB_SUB, False, st),
                state)
            state = lax.cond(
                two, lambda st: one_pass(q, pl.multiple_of((i % 2) * SB_BLOCK, SB_BLOCK), 2 * SB_SUB, False, st),
                lambda st: st, state)
            state = lax.cond(one, lambda st: one_pass(q, 0, SB_SUB, False, st), lambda st: st, state)
            o_ref[pl.ds(r0, SB_BLOCK), :] = (state[1] + state[3]).astype(BF16)
            lt_ref[pl.ds(r0, SB_BLOCK), :] = jnp.where(_head_lanes(0), state[0], state[2])
            return 0

        lax.fori_loop(0, t_len // SB_BLOCK, q_block, 0)
        pl.when(step == last_step)(finish)

    def col(k):
        return pl.BlockSpec((t_len, TILE), lambda s, hp: (s, k * npair + hp))

    out = pl.BlockSpec((t_len, TILE), lambda s, hp: (s, hp))
    res = pl.pallas_call(
        body, name="stickbreak_fwd", grid=(nseq, npair), in_specs=[col(2), col(3), col(4)] + [ANY] * (ng + 1),
        out_specs=[pl.BlockSpec((t_len, TILE), lambda s, hp: (s, npair + hp)), out] + [ANY] * ng,
        out_shape=[jax.ShapeDtypeStruct(mix.shape, BF16), jax.ShapeDtypeStruct((m, npair * TILE), F32)]
        + [jax.ShapeDtypeStruct(b.shape, b.dtype) for b in gather],
        input_output_aliases={**{3 + a: 2 + a for a in range(ng)}, 3 + ng: 0},
        scratch_shapes=[pltpu.VMEM((2, t_len, TILE), BF16), pltpu.VMEM((2, t_len, TILE), BF16)] + _gather_sems(ng),
        compiler_params=pltpu.CompilerParams(dimension_semantics=("arbitrary", "arbitrary"),
                                             vmem_limit_bytes=VMEM_LIMIT_BYTES, has_side_effects=True),
    )(p, p, p, *gather, mix)
    return res[0], res[1], res[2:]


def _sb_bwd(p, dmix, ltot, nseq, t_len, exchange):
    m = p.shape[0]
    npair = 4
    ne = len(exchange)
    last_step = nseq * npair - 1

    def body(q_ref, k_ref, v_ref, do_ref, lt_ref, *rest):
        dq_ref, dk_ref, dv_ref = rest[ne:ne + 3]
        kh_ref, vh_ref, dk_acc, dv_acc = rest[2 * ne + 3:2 * ne + 7]
        step = pl.program_id(0) * npair + pl.program_id(1)
        send, finish = _exchange_steps(rest[:ne], rest[ne + 3:2 * ne + 3], *rest[2 * ne + 7:])
        pl.when(step == 0)(send)
        for h in range(2):
            keep = _head_lanes(h)
            kh_ref[h] = jnp.where(keep, k_ref[...], 0).astype(BF16)
            vh_ref[h] = jnp.where(keep, v_ref[...], 0).astype(BF16)
        dk_acc[...] = jnp.zeros_like(dk_acc)
        dv_acc[...] = jnp.zeros_like(dv_acc)
        sum_upto = _sum_matrix("upto")
        sum_before = _sum_matrix("before")
        strict = _strict_mask()
        lane = lax.broadcasted_iota(jnp.int32, (SB_BLOCK, TILE), 1)

        def running(x, matrix, start, nsub):
            pieces = [x[h][:, b * TILE:(b + 1) * TILE] for h in range(2) for b in range(nsub)]
            sums = jnp.dot(jnp.concatenate(pieces, axis=0).astype(BF16), matrix, preferred_element_type=F32)
            wide, ends = [], []
            for h in range(2):
                total, cols = start[h], []
                for b in range(nsub):
                    part = sums[(h * nsub + b) * SB_BLOCK:(h * nsub + b + 1) * SB_BLOCK]
                    cols.append(part[:, :TILE] + total)
                    total = total + part[:, TILE:]
                wide.append(jnp.concatenate(cols, axis=1))
                ends.append(total)
            return wide, ends

        def one_pass(q, do, qh, doh, ltot, row0, nsub, diag, state):
            rows = pl.ds(row0, nsub * TILE)
            log_sig, logkeep = [], []
            for h in range(2):
                zh = lax.dot_general(q, kh_ref[h, rows, :], NT_DIMS, preferred_element_type=F32)
                log_sig_h, logkeep_h = _log_gates(zh)
                log_sig.append(log_sig_h)
                logkeep.append(jnp.where(strict, logkeep_h, 0.0) if diag else logkeep_h)
            upto, sum_l = running(logkeep, sum_upto, [state[0], state[3]], nsub)
            w, g = [], []
            for h in range(2):
                wh = jnp.exp(log_sig[h] + (ltot[h] - upto[h]))
                if diag:
                    wh = jnp.where(strict, wh, 0.0)
                w.append(wh)
                g.append(wh * lax.dot_general(do, vh_ref[h, rows, :], NT_DIMS, preferred_element_type=F32))
            g_before, sum_g = running(g, sum_before, [state[1], state[4]], nsub)
            out, dk_new, dv_new = [], 0.0, 0.0
            for h in range(2):
                dz = g[h] - jnp.exp(log_sig[h]) * (g[h] + g_before[h])
                if diag:
                    dz = jnp.where(strict, dz, 0.0)
                dzb = dz.astype(BF16)
                dq = state[3 * h + 2] + jnp.dot(dzb, kh_ref[h, rows, :], preferred_element_type=F32)
                dk_new = dk_new + lax.dot_general(dzb, qh[h], TN_DIMS, preferred_element_type=F32)
                dv_new = dv_new + lax.dot_general(w[h].astype(BF16), doh[h], TN_DIMS, preferred_element_type=F32)
                out += [sum_l[h], sum_g[h], dq]
            dk_acc[rows, :] += dk_new
            dv_acc[rows, :] += dv_new
            return tuple(out)

        def q_block(i, _):
            r0 = pl.multiple_of(i * SB_BLOCK, SB_BLOCK)
            q = q_ref[pl.ds(r0, SB_BLOCK), :] * SB_SCALE
            do = do_ref[pl.ds(r0, SB_BLOCK), :]
            lt = lt_ref[pl.ds(r0, SB_BLOCK), :]
            qh, doh, ltot = [], [], []
            for h in range(2):
                keep = _head_lanes(h)
                qh.append(jnp.where(keep, q, 0).astype(BF16))
                doh.append(jnp.where(keep, do, 0).astype(BF16))
                ltot.append(jnp.sum(jnp.where(lane == h * SB_DH, lt, 0.0), axis=1, keepdims=True))
            zero = jnp.zeros((SB_BLOCK, TILE), F32)
            full, two, one = _split_passes(i)
            state = lax.fori_loop(
                0, full,
                lambda jj, st: one_pass(q, do, qh, doh, ltot, pl.multiple_of(SB_PASS * jj * SB_BLOCK, SB_BLOCK),
                                        SB_PASS * SB_SUB, False, st),
                (zero,) * 6)
            state = lax.cond(
                two,
                lambda st: one_pass(q, do, qh, doh, ltot, pl.multiple_of(SB_PASS * full * SB_BLOCK, SB_BLOCK),
                                    2 * SB_SUB, False, st),
                lambda st: st, state)
            state = lax.cond(
                one,
                lambda st: one_pass(q, do, qh, doh, ltot, pl.multiple_of((i - 1) * SB_BLOCK, SB_BLOCK), SB_SUB, False, st),
                lambda st: st, state)
            state = one_pass(q, do, qh, doh, ltot, r0, SB_SUB, True, state)
            dq_ref[pl.ds(r0, SB_BLOCK), :] = ((state[2] + state[5]) * SB_SCALE).astype(BF16)
            return 0

        lax.fori_loop(0, t_len // SB_BLOCK, q_block, 0)
        dk_ref[...] = dk_acc[...].astype(BF16)
        dv_ref[...] = dv_acc[...].astype(BF16)
        pl.when(step == last_step)(finish)

    def col(k):
        return pl.BlockSpec((t_len, TILE), lambda s, hp: (s, k * npair + hp))

    out = pl.BlockSpec((t_len, TILE), lambda s, hp: (s, hp))
    width = npair * TILE
    res = pl.pallas_call(
        body, name="stickbreak_bwd", grid=(nseq, npair),
        in_specs=[col(2), col(3), col(4), col(1), out] + [ANY] * ne, out_specs=[out, out, out] + [ANY] * ne,
        out_shape=[jax.ShapeDtypeStruct((m, width), BF16)] * 3 + _exchange_shapes(exchange),
        scratch_shapes=[pltpu.VMEM((2, t_len, TILE), BF16), pltpu.VMEM((2, t_len, TILE), BF16),
                        pltpu.VMEM((t_len, TILE), F32), pltpu.VMEM((t_len, TILE), F32)] + _exchange_sems(ne),
        compiler_params=pltpu.CompilerParams(dimension_semantics=("arbitrary", "arbitrary"),
                                             vmem_limit_bytes=VMEM_LIMIT_BYTES, has_side_effects=True),
    )(p, p, p, dmix, ltot, *exchange)
    return res[0], res[1], res[2], res[3:]


def _adam_math(w, g, m, v):
    m = ADAM_B1 * m + (1.0 - ADAM_B1) * g
    v = ADAM_B2 * v + (1.0 - ADAM_B2) * (g * g)
    m_hat = m / (1.0 - ADAM_B1 ** ADAM_STEP)
    v_hat = v / (1.0 - ADAM_B2 ** ADAM_STEP)
    delta = -ADAM_LR * (m_hat / (jnp.sqrt(v_hat) + ADAM_EPS) + ADAM_WD * w)
    return delta, m, v


def _cast_place(w, layer, pos, *, name):
    _, r, c = w.shape
    tr = min(r, 256)

    def body(pos_ref, w_ref, o_ref):
        o_ref[...] = w_ref[...].astype(BF16)

    grid_spec = pltpu.PrefetchScalarGridSpec(
        num_scalar_prefetch=1, grid=(r // tr,),
        in_specs=[pl.BlockSpec((None, tr, c), lambda i, pos_ref: (layer, i, 0))],
        out_specs=pl.BlockSpec((None, None, tr, c), lambda i, pos_ref: (0, pos_ref[0], i, 0)))
    return pl.pallas_call(
        body, name=name, grid_spec=grid_spec, out_shape=jax.ShapeDtypeStruct((1, N_CHIP, r, c), BF16),
        compiler_params=_params(("parallel",)),
    )(pos, w)


def _cast_place_all(items, pos, *, name, rider=None):
    tiles = [min(w.shape[1], 256) for w, _ in items]
    counts = [w.shape[1] // t for (w, _), t in zip(items, tiles)]
    starts = [sum(counts[:a]) for a in range(len(items))]
    n = len(items)

    def body(pos_ref, *refs):
        i = pl.program_id(0)
        for a in range(n):
            @pl.when((i >= starts[a]) & (i < starts[a] + counts[a]))
            def _():
                refs[n + a][...] = refs[a][...].astype(BF16)

    def block(a):
        return lambda i: jnp.clip(i - starts[a], 0, counts[a] - 1)

    in_specs, out_specs, out_shape = [], [], []
    for a, ((w, layer), t) in enumerate(zip(items, tiles)):
        _, r, c = w.shape
        in_specs.append(pl.BlockSpec((None, t, c), lambda i, pos_ref, a=a, layer=layer: (layer, block(a)(i), 0)))
        out_specs.append(pl.BlockSpec((None, None, t, c), lambda i, pos_ref, a=a: (0, pos_ref[0], block(a)(i), 0)))
        out_shape.append(jax.ShapeDtypeStruct((1, N_CHIP, r, c), BF16))
    res, rode = _call(body, name=name, grid=(sum(counts),), in_specs=in_specs, out_specs=out_specs,
                      out_shape=out_shape, scratch_shapes=[], semantics=("arbitrary",),
                      args=[w for w, _ in items], rider=rider, prefetch=pos)
    return res if rider is None else (res, rode)


def _pair_sum(mine, got, pos, *, name):
    l_dim, s_dim, h, c = got.shape
    th = min(h, 512)
    nt = h // th

    def body(pos_ref, a_ref, b_ref, o_ref):
        o_ref[...] = (a_ref[...].astype(F32) + b_ref[...].astype(F32)).astype(BF16)

    pieces = 2
    spec = pl.BlockSpec((None, pieces, th, c), lambda l, s, i, pos_ref: (l, s, i, 0))
    grid_spec = pltpu.PrefetchScalarGridSpec(
        num_scalar_prefetch=1, grid=(l_dim, s_dim // pieces, nt),
        in_specs=[pl.BlockSpec((None, pieces, th, c), lambda l, s, i, pos_ref: (l, s, pos_ref[1] * nt + i, 0)), spec],
        out_specs=spec)
    return pl.pallas_call(
        body, name=name, grid_spec=grid_spec, out_shape=jax.ShapeDtypeStruct(got.shape, BF16),
        compiler_params=_params(("parallel",) * 3),
    )(pos, mine, got)


def _chip_sum(sums, landed, pos, *, name):
    l_dim, _, h, c = sums.shape
    th = min(h, 128)
    nt = h // th

    def body(pos_ref, own, r0, r1, r2, o_ref):
        o_ref[...] = ((own[...].astype(F32) + r0[...].astype(F32)) + r1[...].astype(F32)) + r2[...].astype(F32)

    def piece(k):
        return pl.BlockSpec((None, None, th, c), lambda l, i, pos_ref: (l, k, i, 0))

    grid_spec = pltpu.PrefetchScalarGridSpec(
        num_scalar_prefetch=1, grid=(l_dim, nt),
        in_specs=[pl.BlockSpec((None, None, th, c), lambda l, i, pos_ref: (l, pos_ref[0], i, 0)),
                  piece(0), piece(1), piece(2)],
        out_specs=pl.BlockSpec((None, th, c), lambda l, i, pos_ref: (l, pos_ref[1] * nt + i, 0)))
    return pl.pallas_call(
        body, name=name, grid_spec=grid_spec, out_shape=jax.ShapeDtypeStruct((l_dim, 2 * h, c), F32),
        compiler_params=_params(("parallel",) * 2),
    )(pos, sums, landed, landed, landed)


def _adam_big(w, m, v, grads, *, name):
    l_dim, r, c = w.shape
    assert len(grads) == l_dim
    tr = min(r, 512)

    def body(*refs):
        w_ref, m_ref, v_ref = refs[:3]
        g_refs = refs[3:3 + l_dim]
        go_ref, d_ref, mo_ref, vo_ref = refs[3 + l_dim:]
        g = g_refs[0][...]
        for l in range(1, l_dim):
            g = jnp.where(pl.program_id(0) == l, g_refs[l][...], g)
        delta, m_new, v_new = _adam_math(w_ref[...], g, m_ref[...], v_ref[...])
        go_ref[...] = g
        d_ref[...] = delta
        mo_ref[...] = m_new
        vo_ref[...] = v_new

    spec = pl.BlockSpec((None, tr, c), lambda l, i: (l, i, 0))
    gspec = pl.BlockSpec((None, tr, c), lambda l, i: (0, i, 0))
    return pl.pallas_call(
        body, name=name, grid=(l_dim, r // tr), in_specs=[spec] * 3 + [gspec] * l_dim, out_specs=[spec] * 4,
        out_shape=[jax.ShapeDtypeStruct(w.shape, F32)] * 4, compiler_params=_params(("parallel",) * 2),
    )(w, m, v, *grads)


def _position():
    return lax.axis_index("x"), lax.axis_index("y"), lax.axis_index("c")


def _other_chips(x, y):
    return [(1 - x, y), (x, 1 - y), (1 - x, 1 - y)]


def _remote(src, dst, send_sem, recv_sem, device):
    return pltpu.make_async_remote_copy(src_ref=src, dst_ref=dst, send_sem=send_sem, recv_sem=recv_sem,
                                        device_id=device, device_id_type=MESH)


ANY = pl.BlockSpec(memory_space=pl.ANY)


def _gather_sems(n):
    return [pltpu.SemaphoreType.DMA((3 * n,))] * 4


def _gather_steps(outs, send_sems, recv_sems, fwd_send, fwd_recv):
    n = len(outs)
    x, y, c = _position()
    chips = _other_chips(x, y)
    sibling = (x, y, 1 - c)

    def half(a, chip, core):
        h = outs[a].shape[2] // 2
        return outs[a].at[:, 2 * chip[0] + chip[1], pl.ds(core * h, h), :]

    def over_ici(a, k, chip):
        block = half(a, chip, c)
        return _remote(block, block, send_sems.at[3 * a + k], recv_sems.at[3 * a + k], (*chips[k], c))

    def over_d2d(a, k, core):
        block = half(a, chips[k], core)
        return _remote(block, block, fwd_send.at[3 * a + k], fwd_recv.at[3 * a + k], sibling)

    def send():
        for a in range(n):
            for k in range(3):
                over_ici(a, k, (x, y)).start()

    def forward():
        for k in range(3):
            for a in range(n):
                over_ici(a, k, chips[k]).wait_recv()
                over_d2d(a, k, c).start()

    def finish():
        for k in range(3):
            for a in range(n):
                over_d2d(a, k, 1 - c).wait_recv()
        for a in range(n):
            for k in range(3):
                over_ici(a, k, (x, y)).wait_send()
                over_d2d(a, k, c).wait_send()

    return send, forward, finish


def _swap_halves(grads, *, name):
    n = len(grads)

    def body(*refs):
        send, finish = _swap_steps(refs[:n], refs[n:2 * n], *refs[2 * n:])
        send()
        finish()

    sem = pltpu.SemaphoreType.DMA((n,))
    return pl.pallas_call(
        body, name=name, in_specs=[ANY] * n, out_specs=[ANY] * n, out_shape=_swap_shapes(grads),
        scratch_shapes=[sem, sem], compiler_params=pltpu.CompilerParams(has_side_effects=True),
    )(*grads)


def _swap_shapes(grads):
    return [jax.ShapeDtypeStruct(g.shape[:2] + (g.shape[2] // 2, g.shape[3]), g.dtype) for g in grads]


def _swap_steps(ins, outs, send_sems, recv_sems):
    x, y, c = _position()

    def copy(a):
        h = ins[a].shape[2] // 2
        return _remote(ins[a].at[:, :, pl.ds((1 - c) * h, h), :], outs[a], send_sems.at[a], recv_sems.at[a],
                       (x, y, 1 - c))

    def send():
        for a in range(len(ins)):
            copy(a).start()

    def finish():
        for a in range(len(ins)):
            copy(a).wait()

    return send, finish


def _exchange_shapes(sums):
    return [jax.ShapeDtypeStruct((s.shape[0], 3) + s.shape[2:], s.dtype) for s in sums]


def _exchange_sems(n):
    return [pltpu.SemaphoreType.DMA((3 * n,))] * 2


def _exchange_steps(ins, outs, send_sems, recv_sems):
    n = len(ins)
    x, y, c = _position()
    chips = _other_chips(x, y)

    def copy(a, k):
        chip = chips[k]
        return _remote(ins[a].at[:, 2 * chip[0] + chip[1]], outs[a].at[:, k],
                       send_sems.at[3 * a + k], recv_sems.at[3 * a + k], (*chip, c))

    def send():
        for a in range(n):
            for k in range(3):
                copy(a, k).start()

    def finish():
        for a in range(n):
            for k in range(3):
                copy(a, k).wait()

    return send, finish


def _join_halves(bufs, *, name):
    n = len(bufs)

    def body(*refs):
        send, finish = _join_steps(refs[n:2 * n], *refs[2 * n:])
        send()
        finish()

    sem = pltpu.SemaphoreType.DMA((n,))
    return pl.pallas_call(
        body, name=name, in_specs=[ANY] * n, out_specs=[ANY] * n,
        out_shape=[jax.ShapeDtypeStruct(b.shape, b.dtype) for b in bufs],
        input_output_aliases={a: a for a in range(n)},
        scratch_shapes=[sem, sem], compiler_params=pltpu.CompilerParams(has_side_effects=True),
    )(*bufs)


def _join_steps(outs, send_sems, recv_sems):
    x, y, c = _position()

    def copy(a, core):
        h = outs[a].shape[1] // 2
        half = outs[a].at[:, pl.ds(core * h, h), :]
        return _remote(half, half, send_sems.at[a], recv_sems.at[a], (x, y, 1 - c))

    def send():
        for a in range(len(outs)):
            copy(a, c).start()

    def finish():
        for a in range(len(outs)):
            copy(a, c).wait_send()
            copy(a, 1 - c).wait_recv()

    return send, finish


def _allgather_steps(ins, outs, send_sems, recv_sems, local_sems):
    n = len(ins)
    x, y, c = _position()
    me, sibling = (x, y, c), (x, y, 1 - c)
    chips = _other_chips(x, y)

    def slot(a, dev):
        return outs[a].at[4 * dev[0] + 2 * dev[1] + dev[2]]

    def copy(a, k, block, to, own=False):
        return _remote(ins[a] if own else slot(a, block), slot(a, block),
                       send_sems.at[7 * a + k], recv_sems.at[7 * a + k], to)

    def first(a):
        return [copy(a, 0, me, sibling, own=True)] + [copy(a, 1 + k, me, (*chips[k], c), own=True) for k in range(3)]

    def local(a):
        return pltpu.make_async_copy(ins[a], slot(a, me), local_sems.at[a])

    def send():
        for a in range(n):
            local(a).start()
            for cp in first(a):
                cp.start()

    def forward():
        for a in range(n):
            for k in range(3):
                copy(a, 1 + k, (*chips[k], c), me).wait_recv()
                copy(a, 4 + k, (*chips[k], c), sibling).start()

    def finish():
        for a in range(n):
            copy(a, 0, sibling, me).wait_recv()
            for k in range(3):
                copy(a, 4 + k, (*chips[k], 1 - c), me).wait_recv()
        for a in range(n):
            for cp in first(a) + [copy(a, 4 + k, (*chips[k], c), sibling) for k in range(3)]:
                cp.wait_send()
            local(a).wait()

    return send, forward, finish


def _allreduce_small(packs):
    n = len(packs)

    def body(*refs):
        ins, outs, gath = refs[:n], refs[n:2 * n], refs[2 * n:3 * n]
        send_sems, recv_sems = refs[3 * n:]
        x, y, c = _position()
        me, sibling = (x, y, c), (x, y, 1 - c)
        chips = _other_chips(x, y)

        def slot(a, dev):
            return gath[a].at[4 * dev[0] + 2 * dev[1] + dev[2]]

        def copy(a, k, block, to, src=None):
            return _remote(slot(a, block) if src is None else src, slot(a, block),
                           send_sems.at[7 * a + k], recv_sems.at[7 * a + k], to)

        started = []
        for a in range(n):
            slot(a, me)[...] = ins[a][...]
            first = [copy(a, 0, me, sibling, src=ins[a])]
            first += [copy(a, 1 + k, me, (*chip, c), src=ins[a]) for k, chip in enumerate(chips)]
            for cp in first:
                cp.start()
            started += first
        for a in range(n):
            for k, chip in enumerate(chips):
                copy(a, 1 + k, (*chip, c), me).wait_recv()
                cp = copy(a, 4 + k, (*chip, c), sibling)
                cp.start()
                started.append(cp)
        for a in range(n):
            copy(a, 0, sibling, me).wait_recv()
            for k, chip in enumerate(chips):
                copy(a, 4 + k, (*chip, 1 - c), me).wait_recv()
        for cp in started:
            cp.wait_send()
        for a in range(n):
            total = gath[a][0]
            for d in range(1, N_DEV):
                total = total + gath[a][d]
            outs[a][...] = total

    vmem = pl.BlockSpec(memory_space=pltpu.VMEM)
    sem = pltpu.SemaphoreType.DMA((7 * n,))
    return pl.pallas_call(
        body, name="allreduce_small", in_specs=[vmem] * n, out_specs=[vmem] * n,
        out_shape=[jax.ShapeDtypeStruct(p.shape, p.dtype) for p in packs],
        scratch_shapes=[pltpu.VMEM((N_DEV,) + p.shape, p.dtype) for p in packs] + [sem, sem],
        compiler_params=pltpu.CompilerParams(has_side_effects=True, vmem_limit_bytes=VMEM_LIMIT_BYTES),
    )(*packs)


LOSS_ROW = 520


def _pad_rows(a, rows=8):
    return jnp.concatenate([a, jnp.zeros((rows - a.shape[0], a.shape[1]), a.dtype)], axis=0)

def _adam_small(wide, mid, sgu, pool, late, params):
    names = ["mix_norm_g", "mlp_norm_g", "final_norm_g", "conv_b", "conv_w", "sgu_norm_g", "sgu_norm_b",
             "pool_w", "pool_scale", "sgu_w", "sgu_b"]
    n = len(names)

    def body(*refs):
        wmv = refs[5:5 + 3 * n]
        outs = refs[5 + 3 * n:]
        x, y, _ = _position()
        q = 2 * x + y

        def total(ref):
            t = ref[0]
            for dev in range(1, N_DEV):
                t = t + ref[dev]
            return t

        wide_sum, mid_sum, sgu_sum, pool_sum = total(refs[0]), total(refs[1]), total(refs[2]), total(refs[3])
        late_ref = refs[4]

        def my_quarter(rows):
            parts = [rows[:, s * TILE:(s + 1) * TILE] for s in range(N_CHIP)]
            return jnp.where(q == 0, parts[0], jnp.where(q == 1, parts[1], jnp.where(q == 2, parts[2], parts[3])))

        def tiles(pack):
            return [((0, g), pack[g * TILE:(g + 1) * TILE, :]) for g in range(4)]

        grads = {
            "mix_norm_g": [((), wide_sum[0:2, :] + late_ref[0:2, :])],
            "mlp_norm_g": [((), wide_sum[8:10, :])],
            "final_norm_g": [((), wide_sum[16:17, :])],
            "conv_b": [((), mid_sum[0:1, :])],
            "conv_w": [((0,), my_quarter(mid_sum[8:11, :]))],
            "sgu_norm_g": [((), my_quarter(mid_sum[16:17, :]))],
            "sgu_norm_b": [((), my_quarter(mid_sum[24:25, :]))],
            "pool_w": tiles(pool_sum),
            "sgu_w": tiles(sgu_sum),
            "pool_scale": [((0,), pool_sum[512:516, :])],
            "sgu_b": [((0,), sgu_sum[512:516, :])],
        }
        outs[4 * n][...] = sgu_sum[LOSS_ROW:LOSS_ROW + 8, :]
        for i, name in enumerate(names):
            w_ref, m_ref, v_ref = wmv[3 * i:3 * i + 3]
            for lead, g in grads[name]:
                idx = lead + (slice(None), slice(None))
                delta, m_new, v_new = _adam_math(w_ref[idx], g, m_ref[idx], v_ref[idx])
                outs[4 * i][idx] = g
                outs[4 * i + 1][idx] = delta
                outs[4 * i + 2][idx] = m_new
                outs[4 * i + 3][idx] = v_new

    vmem = pl.BlockSpec(memory_space=pltpu.VMEM)
    args, out_shape = [wide, mid, sgu, pool, late], []
    for name in names:
        w, m, v = params[name]
        args += [w, m, v]
        out_shape += [jax.ShapeDtypeStruct(w.shape, F32)] * 4
    out_shape.append(jax.ShapeDtypeStruct((8, TILE), F32))
    res = pl.pallas_call(
        body, name="adam_small", in_specs=[vmem] * len(args), out_specs=[vmem] * len(out_shape),
        out_shape=out_shape, compiler_params=pltpu.CompilerParams(vmem_limit_bytes=VMEM_LIMIT_BYTES),
    )(*args)
    return {name: res[4 * i:4 * i + 4] for i, name in enumerate(names)}, res[4 * n]


def _pair_sums(grads, got, pos, tag):
    return [_pair_sum(a, b, pos, name=f"pair_sum_{tag}{i}") for i, (a, b) in enumerate(zip(grads, got))]


def _chip_sums(sums, landed, pos, tag):
    return [_chip_sum(s, r, pos, name=f"chip_sum_{tag}{i}") for i, (s, r) in enumerate(zip(sums, landed))]


def kernel(x, mix_norm_g, mlp_norm_g, ab_w_in, pool_w, pool_scale, conv_w, conv_b, ab_w_out, cd_w_in, sgu_norm_g, sgu_norm_b, sgu_w, sgu_b, cd_w_out, mlp_w1, mlp_w2, final_norm_g, loss_target, m_mix_norm_g, m_mlp_norm_g, m_ab_w_in, m_pool_w, m_pool_scale, m_conv_w, m_conv_b, m_ab_w_out, m_cd_w_in, m_sgu_norm_g, m_sgu_norm_b, m_sgu_w, m_sgu_b, m_cd_w_out, m_mlp_w1, m_mlp_w2, m_final_norm_g, v_mix_norm_g, v_mlp_norm_g, v_ab_w_in, v_pool_w, v_pool_scale, v_conv_w, v_conv_b, v_ab_w_out, v_cd_w_in, v_sgu_norm_g, v_sgu_norm_b, v_sgu_w, v_sgu_b, v_cd_w_out, v_mlp_w1, v_mlp_w2, v_final_norm_g):
    nseq, t_len, d = x.shape
    m_tok = nseq * t_len
    h0 = x.reshape(m_tok, d)
    target = loss_target.reshape(m_tok, d)

    x_idx, y_idx = lax.axis_index("x"), lax.axis_index("y")
    q_idx = 2 * x_idx + y_idx
    pos = jnp.stack([q_idx, lax.axis_index("c")]).astype(jnp.int32)

    def shard_buffer(w, layer, tag):
        return _cast_place(w, layer, pos, name=f"cast_place_{tag}")

    def row_block(w):
        return w.reshape(1, 1, -1, w.shape[-1])

    (buf_ab_out, buf_w1_0, buf_w2_0, buf_cd_in, *later_weights), ((w_ab_in,),) = _cast_place_all(
        [(ab_w_out, 0), (mlp_w1, 0), (mlp_w2, 0), (cd_w_in, 0), (cd_w_out, 0), (mlp_w1, 1), (mlp_w2, 1)], pos,
        name="cast_place_rest", rider=[("gather", [shard_buffer(ab_w_in, 0, "ab_in")])])

    pool_w3, pool_scale3 = pool_w[0], pool_scale[0].reshape(4, 1, TILE)
    sgu_w3 = sgu_w[0]
    sgu_w3_t = jnp.swapaxes(sgu_w3, 1, 2)
    sgu_bias_tile = jnp.broadcast_to(sgu_b[0][:, :, None], (4, TILE, TILE))
    conv_b2 = conv_b

    def place_quarter(v):
        return lax.dynamic_update_slice(jnp.zeros((v.shape[0], 4 * TILE), F32), v, (0, q_idx * TILE))

    sharded_small = jnp.concatenate(
        [place_quarter(conv_w[0]), place_quarter(sgu_norm_g), place_quarter(sgu_norm_b),
         jnp.zeros((3, 4 * TILE), F32)], axis=0)
    sharded_small, = _allreduce_small([sharded_small])
    sharded_small = sharded_small * 0.5
    conv_w_full = sharded_small[0:3]
    sgu_g_full = sharded_small[3:4]
    sgu_b_full = sharded_small[4:5]

    xn0 = _rms_fwd(h0, mix_norm_g[0:1], name="rms_fwd_mix0")
    p_ab, ((w_1_0,),) = _mm_nn(xn0, w_ab_in, 0, out_dtype=BF16, name="ab_in_proj",
                               rider=[("gather", [buf_w1_0])])
    mix0, ((w_ab_out,),) = _ab_fwd(p_ab, pool_w3, pool_scale3, conv_w_full, conv_b2, nseq, t_len,
                                   rider=[("gather", [buf_ab_out])])
    w_ab_out = row_block(w_ab_out)
    h1, hn0 = _mm_nn(mix0, w_ab_out, 0, out_dtype=F32, name="ab_out_proj", epilogue="residual", extra=h0,
                     norm_g=mlp_norm_g[0:1])
    (act0, relu0), ((w_2_0,),) = _mm_nn(hn0, w_1_0, 0, out_dtype=BF16, name="mlp0_up", epilogue="relu2",
                                        rider=[("gather", [buf_w2_0])])
    w_2_0 = row_block(w_2_0)
    (h2, xn1), ((w_cd_in,),) = _mm_nn(act0, w_2_0, 0, out_dtype=F32, name="mlp0_down", epilogue="residual", extra=h1,
                                      norm_g=mix_norm_g[1:2],
                                      rider=[("gather", [buf_cd_in])])

    p_cd = _mm_nn(xn1, w_cd_in, 0, out_dtype=BF16, name="cd_in_proj")
    mix1 = _sgu_fwd(p_cd, sgu_g_full, sgu_b_full, sgu_w3, sgu_bias_tile)
    mix1, ltot, (w_cd_out, w_1_1, w_2_1) = _sb_fwd(p_cd, mix1, nseq, t_len, later_weights)
    w_cd_out, w_2_1 = row_block(w_cd_out), row_block(w_2_1)
    h3, hn1 = _mm_nn(mix1, w_cd_out, 0, out_dtype=F32, name="cd_out_proj", epilogue="residual", extra=h2,
                     norm_g=mlp_norm_g[1:2])
    act1, relu1 = _mm_nn(hn1, w_1_1, 0, out_dtype=BF16, name="mlp1_up", epilogue="relu2")

    dh4, dh4_bf, dg_final, loss_tile = _mlp_down_loss(act1, w_2_1, h3, final_norm_g.reshape(1, d), target)

    def as_pieces(g):
        return g.reshape(1, N_CHIP, -1, g.shape[-1]) if g.shape[1] == 1 else g

    dz1 = _mm_nt(dh4_bf, w_2_1, 0, out_dtype=BF16, name="mlp1_down_bwd", epilogue="relu2_bwd", extra=relu1)
    g_w2_1 = as_pieces(_mm_tn(act1, dh4_bf, 1, name="mlp1_down_wgrad"))
    g_w1_1 = _mm_tn(hn1, dz1, N_CHIP, name="mlp1_up_wgrad")
    (dh3, dh3_bf, dg_mlp1), (got_a,) = _mm_nt(
        dz1, w_1_1, 0, out_dtype=F32, name="mlp1_up_bwd", epilogue="rms_bwd",
        extra=(h3, mlp_norm_g[1:2], dh4), rider=[("swap", [g_w1_1, g_w2_1])])

    g_cd_out = as_pieces(_mm_tn(mix1, dh3_bf, 1, name="cd_out_wgrad"))
    dmix1, (got_cd_out,) = _mm_nt(dh3_bf, w_cd_out, 0, out_dtype=BF16, name="cd_out_bwd",
                                  rider=[("swap", [g_cd_out])])
    sums_a = _pair_sums([g_w1_1, g_w2_1, g_cd_out], got_a + got_cd_out, pos, "a")
    du, dv, dsgu_w, dsgu_bs, dsgu_g, dsgu_b = _sgu_bwd(p_cd, dmix1, sgu_g_full, sgu_b_full, sgu_w3, sgu_w3_t,
                                                      sgu_bias_tile)
    dq, dk, dvv, landed_a = _sb_bwd(p_cd, dmix1, ltot, nseq, t_len, sums_a)
    halves_a = _chip_sums(sums_a, landed_a, pos, "a")
    dp_cd = jnp.concatenate([du, dv, dq, dk, dvv], axis=1)
    g_cd_in, ((r_w1_1, r_w2_1, r_cd_out),) = _mm_tn(xn1, dp_cd, N_CHIP, name="cd_in_wgrad",
                                                    rider=[("join", halves_a)])
    (dh2, dh2_bf, dg_mix1), (got_c,) = _mm_nt(
        dp_cd, w_cd_in, 0, out_dtype=F32, name="cd_in_bwd", epilogue="rms_bwd",
        extra=(h2, mix_norm_g[1:2], dh3), rider=[("swap", [g_cd_in])])

    sums_c = _pair_sums([g_cd_in], got_c, pos, "c")
    dz0, (landed_c,) = _mm_nt(dh2_bf, w_2_0, 0, out_dtype=BF16, name="mlp0_down_bwd", epilogue="relu2_bwd",
                              extra=relu0, rider=[("exchange", sums_c)])
    halves_c = _chip_sums(sums_c, landed_c, pos, "c")
    sgu_pack = jnp.concatenate([dsgu_w.reshape(4 * TILE, TILE), _pad_rows(dsgu_bs[:, :, 0]), loss_tile], axis=0)
    g_w2_0, ((r_cd_in,), (sgu_pack,)) = _mm_tn(act0, dh2_bf, 1, name="mlp0_down_wgrad",
                                               rider=[("join", halves_c), ("allgather", [sgu_pack])])
    g_w2_0 = as_pieces(g_w2_0)
    g_w1_0, (got_d,) = _mm_tn(hn0, dz0, N_CHIP, name="mlp0_up_wgrad", rider=[("swap", [g_w2_0])])
    sums_d = _pair_sums([g_w2_0], got_d, pos, "d")
    (dh1, dh1_bf, dg_mlp0), (landed_d, got_e) = _mm_nt(
        dz0, w_1_0, 0, out_dtype=F32, name="mlp0_up_bwd", epilogue="rms_bwd",
        extra=(h1, mlp_norm_g[0:1], dh2), rider=[("exchange", sums_d), ("swap", [g_w1_0])])
    halves_d = _chip_sums(sums_d, landed_d, pos, "d")
    sums_e = _pair_sums([g_w1_0], got_e, pos, "e")

    dmix0, ((r_w2_0,),) = _mm_nt(dh1_bf, w_ab_out, 0, out_dtype=BF16, name="ab_out_bwd", rider=[("join", halves_d)])
    g_ab_out = as_pieces(_mm_tn(mix0, dh1_bf, 1, name="ab_out_wgrad"))
    (da, dxb, dgb, dgc, dpool_w, dpool_scale, dconv_w, dconv_b), (landed_e, got_f) = _ab_bwd(
        p_ab, dmix0, pool_w3, pool_scale3, conv_w_full, conv_b2, nseq, t_len,
        rider=[("exchange", sums_e), ("swap", [g_ab_out])])
    halves_e = _chip_sums(sums_e, landed_e, pos, "e")
    sums_f = _pair_sums([g_ab_out], got_f, pos, "f")
    dp_ab = jnp.concatenate([da, dxb, dgb, dgc], axis=1)
    wide = jnp.concatenate([_pad_rows(jnp.concatenate([jnp.zeros_like(dg_mix1), dg_mix1], axis=0)),
                            _pad_rows(jnp.concatenate([dg_mlp0, dg_mlp1], axis=0)), _pad_rows(dg_final)], axis=0)
    mid = jnp.concatenate([_pad_rows(dconv_b), _pad_rows(dconv_w), _pad_rows(dsgu_g), _pad_rows(dsgu_b)], axis=0)
    pool_pack = jnp.concatenate([dpool_w.reshape(4 * TILE, TILE), _pad_rows(dpool_scale.reshape(4, TILE))], axis=0)
    g_ab_in, (landed_f, (r_w1_0,), (wide, mid, pool_pack)) = _mm_tn(
        xn0, dp_ab, N_CHIP, name="ab_in_wgrad",
        rider=[("exchange", sums_f), ("join", halves_e), ("allgather", [wide, mid, pool_pack])])
    halves_f = _chip_sums(sums_f, landed_f, pos, "f")
    sums_g = _pair_sums([g_ab_in], _swap_halves([g_ab_in], name="swap_halves_g"), pos, "g")
    (grad_x, _, dg_mix0), (landed_g, (r_ab_out,)) = _mm_nt(
        dp_ab, w_ab_in, 0, out_dtype=F32, name="ab_in_bwd", epilogue="rms_bwd",
        extra=(h0, mix_norm_g[0:1], dh1), rider=[("exchange", sums_g), ("join", halves_f)])
    r_ab_in, = _join_halves(_chip_sums(sums_g, landed_g, pos, "g"), name="join_halves_g")

    big_out = {
        "ab_w_in": _adam_big(ab_w_in, m_ab_w_in, v_ab_w_in, [r_ab_in], name="adam_ab_w_in"),
        "ab_w_out": _adam_big(ab_w_out, m_ab_w_out, v_ab_w_out, [r_ab_out], name="adam_ab_w_out"),
        "cd_w_in": _adam_big(cd_w_in, m_cd_w_in, v_cd_w_in, [r_cd_in], name="adam_cd_w_in"),
        "cd_w_out": _adam_big(cd_w_out, m_cd_w_out, v_cd_w_out, [r_cd_out], name="adam_cd_w_out"),
        "mlp_w1": _adam_big(mlp_w1, m_mlp_w1, v_mlp_w1, [r_w1_0, r_w1_1], name="adam_mlp_w1"),
        "mlp_w2": _adam_big(mlp_w2, m_mlp_w2, v_mlp_w2, [r_w2_0, r_w2_1], name="adam_mlp_w2"),
    }

    late, = _allreduce_small([_pad_rows(dg_mix0)])
    small_out, loss_sum = _adam_small(wide, mid, sgu_pack, pool_pack, late, {
        "mix_norm_g": (mix_norm_g, m_mix_norm_g, v_mix_norm_g),
        "mlp_norm_g": (mlp_norm_g, m_mlp_norm_g, v_mlp_norm_g),
        "final_norm_g": tuple(a.reshape(1, d) for a in (final_norm_g, m_final_norm_g, v_final_norm_g)),
        "conv_b": (conv_b, m_conv_b, v_conv_b),
        "conv_w": (conv_w, m_conv_w, v_conv_w),
        "sgu_norm_g": (sgu_norm_g, m_sgu_norm_g, v_sgu_norm_g),
        "sgu_norm_b": (sgu_norm_b, m_sgu_norm_b, v_sgu_norm_b),
        "pool_w": (pool_w, m_pool_w, v_pool_w),
        "pool_scale": (pool_scale, m_pool_scale, v_pool_scale),
        "sgu_w": (sgu_w, m_sgu_w, v_sgu_w),
        "sgu_b": (sgu_b, m_sgu_b, v_sgu_b),
    })
    small_out["final_norm_g"] = [a.reshape(d) for a in small_out["final_norm_g"]]

    order = ["mix_norm_g", "mlp_norm_g", "ab_w_in", "pool_w", "pool_scale", "conv_w", "conv_b", "ab_w_out",
             "cd_w_in", "sgu_norm_g", "sgu_norm_b", "sgu_w", "sgu_b", "cd_w_out", "mlp_w1", "mlp_w2",
             "final_norm_g"]
    both = {**big_out, **small_out}
    loss = loss_sum[0, 0]
    outs = [loss, grad_x.reshape(nseq, t_len, d)]
    for kind in range(4):
        outs += [both[name][kind] for name in order]
    return tuple(outs)
```

```python
import math

import jax
import jax.numpy as jnp
from jax import lax
from jax.experimental import pallas as pl
from jax.experimental.pallas import tpu as pltpu

F32 = jnp.float32
BF16 = jnp.bfloat16
MESH = pl.DeviceIdType.MESH

EPS = 1e-6
TILE = 128
N_CHIP = 4
N_DEV = 8
VMEM_LIMIT_BYTES = 56 * 1024 * 1024

ADAM_LR = 0.001
ADAM_B1 = 0.9
ADAM_B2 = 0.999
ADAM_EPS = 1e-08
ADAM_WD = 0.01
ADAM_STEP = 10

NT_DIMS = (((1,), (1,)), ((), ()))
TN_DIMS = (((0,), (0,)), ((), ()))


def _params(sem=None):
    return pltpu.CompilerParams(dimension_semantics=sem, vmem_limit_bytes=VMEM_LIMIT_BYTES)


def _call(body, *, name, grid, in_specs, out_specs, out_shape, scratch_shapes, semantics, args, rider=None,
          prefetch=None):
    npre = 0 if prefetch is None else 1

    def launch(kernel, in_specs, out_specs, out_shape, scratch_shapes, operands, aliases, params):
        if prefetch is None:
            return pl.pallas_call(kernel, name=name, grid=grid, in_specs=in_specs, out_specs=out_specs,
                                  out_shape=out_shape, scratch_shapes=scratch_shapes, input_output_aliases=aliases,
                                  compiler_params=params)(*operands)
        spec = pltpu.PrefetchScalarGridSpec(num_scalar_prefetch=1, grid=grid, in_specs=in_specs, out_specs=out_specs,
                                            scratch_shapes=scratch_shapes)
        return pl.pallas_call(kernel, name=name, grid_spec=spec, out_shape=out_shape,
                              input_output_aliases={k + 1: v for k, v in aliases.items()},
                              compiler_params=params)(prefetch, *operands)

    if not rider:
        res = launch(body, list(in_specs), list(out_specs), list(out_shape), list(scratch_shapes), args, {},
                     _params(semantics))
        return list(res), []
    plans = [_rider_plan(kind, arrays) for kind, arrays in rider]
    arrays = [a for _, group in rider for a in group]
    nr, n_in, n_out, n_scr = len(arrays), len(in_specs), len(out_specs), len(scratch_shapes)
    first_out, first_scr = n_in + nr, n_in + nr + n_out + nr
    last_step = math.prod(grid) - 1

    def riding(*refs):
        pre, refs = refs[:npre], refs[npre:]
        step = 0
        for axis, size in enumerate(grid):
            step = step * size + pl.program_id(axis)
        steps, at, sem_at = [], 0, first_scr + n_scr
        for (kind, group), (_, sems, _) in zip(rider, plans):
            k = len(group)
            steps.append(_rider_steps(kind, refs[n_in + at:n_in + at + k],
                                      refs[first_out + n_out + at:first_out + n_out + at + k],
                                      refs[sem_at:sem_at + len(sems)]))
            at, sem_at = at + k, sem_at + len(sems)
        for send, _, _ in steps:
            pl.when(step == 0)(send)
        for _, forward, _ in steps:
            if forward is not None:
                pl.when(step == last_step)(forward)
        body(*pre, *refs[:n_in], *refs[first_out:first_out + n_out], *refs[first_scr:first_scr + n_scr])
        for _, _, finish in steps:
            pl.when(step == last_step)(finish)

    aliases, at = {}, 0
    for (_, group), (_, _, aliased) in zip(rider, plans):
        if aliased:
            aliases.update({n_in + at + a: n_out + at + a for a in range(len(group))})
        at += len(group)
    res = launch(
        riding, list(in_specs) + [ANY] * nr, list(out_specs) + [ANY] * nr,
        list(out_shape) + [s for shapes, _, _ in plans for s in shapes],
        list(scratch_shapes) + [s for _, sems, _ in plans for s in sems], [*args, *arrays], aliases,
        pltpu.CompilerParams(dimension_semantics=("arbitrary",) * len(grid), vmem_limit_bytes=VMEM_LIMIT_BYTES,
                             has_side_effects=True))
    rode, at = [], n_out
    for _, group in rider:
        rode.append(list(res[at:at + len(group)]))
        at += len(group)
    return list(res[:n_out]), rode


def _rider_plan(kind, arrays):
    n = len(arrays)
    same = [jax.ShapeDtypeStruct(a.shape, a.dtype) for a in arrays]
    pair = [pltpu.SemaphoreType.DMA((n,))] * 2
    if kind == "gather":
        return same, _gather_sems(n), True
    if kind == "exchange":
        return _exchange_shapes(arrays), _exchange_sems(n), False
    if kind == "swap":
        return _swap_shapes(arrays), pair, False
    if kind == "allgather":
        return ([jax.ShapeDtypeStruct((N_DEV,) + a.shape, a.dtype) for a in arrays],
                [pltpu.SemaphoreType.DMA((7 * n,))] * 2 + [pltpu.SemaphoreType.DMA((n,))], False)
    assert kind == "join"
    return same, pair, True


def _rider_steps(kind, ins, outs, sems):
    if kind == "gather":
        return _gather_steps(outs, *sems)
    if kind == "allgather":
        return _allgather_steps(ins, outs, *sems)
    if kind == "exchange":
        send, finish = _exchange_steps(ins, outs, *sems)
    elif kind == "swap":
        send, finish = _swap_steps(ins, outs, *sems)
    else:
        send, finish = _join_steps(outs, *sems)
    return send, None, finish


def _gathers(rider):
    return any(kind in ("gather", "allgather") for kind, _ in rider or ())


def _row_tile(k_dim, roomy=False):
    if k_dim > 1024:
        return 512
    return 2048 if roomy else 1024


def _mm_nn(a, b4, layer, *, out_dtype, name, epilogue=None, extra=None, norm_g=None, rider=None):
    m, k_dim = a.shape
    _, s_dim, kb, n = b4.shape
    assert kb == k_dim
    tm = min(m, _row_tile(k_dim, roomy=epilogue != "residual" and norm_g is None and not _gathers(rider)))
    tn = min(n, 1024)
    assert m % tm == 0 and n % tn == 0
    npb = n // tn
    grid = (m // tm, s_dim * npb)
    n_in = 2 + (extra is not None) + (norm_g is not None)
    two_outputs = norm_g is not None or epilogue == "relu2"
    assert norm_g is None or (tn == s_dim * n and epilogue != "relu2")

    def body(*refs):
        a_ref, b_ref = refs[:2]
        e_ref = refs[2] if extra is not None else None
        g_ref = refs[n_in - 1] if norm_g is not None else None
        o_ref = refs[n_in]
        acc = jnp.dot(a_ref[...], b_ref[...], preferred_element_type=F32)
        if epilogue == "relu2":
            r = jnp.maximum(acc, 0.0)
            refs[n_in + 1][...] = r.astype(BF16)
            acc = r * r
        elif epilogue == "residual":
            acc = acc + e_ref[...]
        o_ref[...] = acc.astype(out_dtype)
        if norm_g is not None:
            rstd = lax.rsqrt(jnp.mean(acc * acc, axis=-1, keepdims=True) + EPS)
            refs[n_in + 1][...] = (acc * rstd * g_ref[...]).astype(BF16)

    in_specs = [
        pl.BlockSpec((tm, k_dim), lambda i, j: (i, 0)),
        pl.BlockSpec((None, None, k_dim, tn), lambda i, j: (layer, j // npb, 0, j % npb)),
    ]
    args = [a, b4]
    if extra is not None:
        in_specs.append(pl.BlockSpec((tm, tn), lambda i, j: (i, j)))
        args.append(extra)
    out_block = pl.BlockSpec((tm, tn), lambda i, j: (i, j))
    out_specs, out_shape = [out_block], [jax.ShapeDtypeStruct((m, s_dim * n), out_dtype)]
    if norm_g is not None:
        in_specs.append(pl.BlockSpec((1, tn), lambda i, j: (0, j)))
        args.append(norm_g)
    if two_outputs:
        out_specs.append(out_block)
        out_shape.append(jax.ShapeDtypeStruct((m, s_dim * n), BF16))
    res, rode = _call(
        body, name=name, grid=grid, in_specs=in_specs, out_specs=out_specs, out_shape=out_shape,
        scratch_shapes=[], semantics=("parallel", "parallel"), args=args, rider=rider)
    res = res if two_outputs else res[0]
    return res if rider is None else (res, rode)


def _mm_nt(a, b4, layer, *, out_dtype, name, epilogue=None, extra=None, rider=None):
    m, k_dim = a.shape
    _, s_dim, n_out, n = b4.shape
    assert k_dim == s_dim * n
    rms = epilogue == "rms_bwd"
    roomy = not rms and out_dtype != F32 and not _gathers(rider)
    tm, tn = min(m, _row_tile(k_dim, roomy=roomy)), min(n_out, 1024)
    assert m % tm == 0 and n_out % tn == 0
    grid = (m // tm, n_out // tn)
    assert not rms or tn == n_out
    extras = [] if extra is None else (list(extra) if rms else [extra])
    n_in = 2 + len(extras)

    def body(*refs):
        a_ref, b_ref = refs[:2]
        e_refs = refs[2:n_in]
        o_ref = refs[n_in]
        acc = lax.dot_general(a_ref[:, 0:n], b_ref[0], NT_DIMS, preferred_element_type=F32)
        for s in range(1, s_dim):
            acc = acc + lax.dot_general(a_ref[:, s * n:(s + 1) * n], b_ref[s], NT_DIMS, preferred_element_type=F32)
        if epilogue == "relu2_bwd":
            acc = acc * (2.0 * e_refs[0][...].astype(F32))
        if not rms:
            o_ref[...] = acc.astype(out_dtype)
        else:
            h_ref, g_ref, dres_ref = e_refs
            dhb_ref, dg_ref = refs[n_in + 1:n_in + 3]
            hv = h_ref[...]
            rstd = lax.rsqrt(jnp.mean(hv * hv, axis=-1, keepdims=True) + EPS)
            xhat = hv * rstd
            dxhat = acc * g_ref[...]
            dh = dres_ref[...] + rstd * (dxhat - xhat * jnp.mean(dxhat * xhat, axis=-1, keepdims=True))
            o_ref[...] = dh
            dhb_ref[...] = dh.astype(BF16)
            dg_part = jnp.sum(acc * xhat, axis=0, keepdims=True)
            first = pl.program_id(0) == 0

            @pl.when(first)
            def _():
                dg_ref[...] = dg_part

            @pl.when(jnp.logical_not(first))
            def _():
                dg_ref[...] += dg_part

    in_specs = [
        pl.BlockSpec((tm, k_dim), lambda i, j: (i, 0)),
        pl.BlockSpec((None, s_dim, tn, n), lambda i, j: (layer, 0, j, 0)),
    ]
    args = [a, b4] + extras
    block = pl.BlockSpec((tm, tn), lambda i, j: (i, j))
    vec = pl.BlockSpec((1, tn), lambda i, j: (0, j))
    if rms:
        in_specs += [block, vec, block]
        out_specs = [block, block, vec]
        out_shape = [jax.ShapeDtypeStruct((m, n_out), F32), jax.ShapeDtypeStruct((m, n_out), BF16),
                     jax.ShapeDtypeStruct((1, n_out), F32)]
    else:
        in_specs += [block] * len(extras)
        out_specs, out_shape = [block], [jax.ShapeDtypeStruct((m, n_out), out_dtype)]
    res, rode = _call(
        body, name=name, grid=grid, in_specs=in_specs, out_specs=out_specs, out_shape=out_shape,
        scratch_shapes=[], semantics=("arbitrary",) * 2 if rms else ("parallel", "parallel"), args=args, rider=rider)
    res = res if rms else res[0]
    return res if rider is None else (res, rode)


def _mm_tn(a, b, s_dim, *, name, rider=None):
    m, k1 = a.shape
    mb, n_all = b.shape
    assert mb == m and n_all % s_dim == 0
    n = n_all // s_dim
    tn, t1 = min(n, 1024), min(k1, 512 if _gathers(rider) else 1024)
    assert k1 % t1 == 0 and n % tn == 0
    npb = n // tn
    grid = (k1 // t1, s_dim * npb)

    def body(a_ref, b_ref, o_ref):
        o_ref[...] = lax.dot_general(a_ref[...], b_ref[...], TN_DIMS, preferred_element_type=F32).astype(BF16)

    res, rode = _call(
        body, name=name, grid=grid,
        in_specs=[pl.BlockSpec((m, t1), lambda i, j: (0, i)), pl.BlockSpec((m, tn), lambda i, j: (0, j))],
        out_specs=[pl.BlockSpec((None, None, t1, tn), lambda i, j: (0, j // npb, i, j % npb))],
        out_shape=[jax.ShapeDtypeStruct((1, s_dim, k1, n), BF16)],
        scratch_shapes=[], semantics=("parallel", "parallel"), args=[a, b], rider=rider)
    return res[0] if rider is None else (res[0], rode)


ROW_TILE = 512


def _rms_fwd(h, g, *, name, rider=None):
    m, d = h.shape

    def body(h_ref, g_ref, o_ref):
        hv = h_ref[...]
        rstd = lax.rsqrt(jnp.mean(hv * hv, axis=-1, keepdims=True) + EPS)
        o_ref[...] = (hv * rstd * g_ref[...]).astype(BF16)

    res, rode = _call(
        body, name=name, grid=(m // ROW_TILE,),
        in_specs=[pl.BlockSpec((ROW_TILE, d), lambda i: (i, 0)), pl.BlockSpec((1, d), lambda i: (0, 0))],
        out_specs=[pl.BlockSpec((ROW_TILE, d), lambda i: (i, 0))], out_shape=[jax.ShapeDtypeStruct((m, d), BF16)],
        scratch_shapes=[], semantics=("parallel",), args=[h, g], rider=rider)
    return res[0] if rider is None else (res[0], rode)


def _mlp_down_loss(act, w_2, h_res, g, target):
    m, k_dim = act.shape
    d = w_2.shape[-1]
    tm = _row_tile(k_dim)

    def body(a_ref, b_ref, r_ref, g_ref, t_ref, dh_ref, dhb_ref, dg_ref, loss_ref):
        hv = jnp.dot(a_ref[...], b_ref[...], preferred_element_type=F32) + r_ref[...]
        gv = g_ref[...]
        rstd = lax.rsqrt(jnp.mean(hv * hv, axis=-1, keepdims=True) + EPS)
        xhat = hv * rstd
        err = xhat * gv - t_ref[...]
        dy = err * (1.0 / d)
        dxhat = dy * gv
        dh = rstd * (dxhat - xhat * jnp.mean(dxhat * xhat, axis=-1, keepdims=True))
        dh_ref[...] = dh
        dhb_ref[...] = dh.astype(BF16)
        dg_part = jnp.sum(dy * xhat, axis=0, keepdims=True)
        sq = jnp.sum(jnp.sum(err * err, axis=1, keepdims=True), axis=0, keepdims=True) * (0.5 / d)
        loss_part = jnp.broadcast_to(sq, (8, TILE))

        @pl.when(pl.program_id(0) == 0)
        def _():
            dg_ref[...] = dg_part
            loss_ref[...] = loss_part

        @pl.when(pl.program_id(0) > 0)
        def _():
            dg_ref[...] += dg_part
            loss_ref[...] += loss_part

    row = pl.BlockSpec((tm, d), lambda i: (i, 0))
    vec = pl.BlockSpec((1, d), lambda i: (0, 0))
    return pl.pallas_call(
        body, name="mlp1_down_loss", grid=(m // tm,),
        in_specs=[pl.BlockSpec((tm, k_dim), lambda i: (i, 0)),
                  pl.BlockSpec((None, None, k_dim, d), lambda i: (0, 0, 0, 0)), row, vec, row],
        out_specs=[row, row, vec, pl.BlockSpec((8, TILE), lambda i: (0, 0))],
        out_shape=[jax.ShapeDtypeStruct((m, d), F32), jax.ShapeDtypeStruct((m, d), BF16),
                   jax.ShapeDtypeStruct((1, d), F32), jax.ShapeDtypeStruct((8, TILE), F32)],
        compiler_params=_params(("arbitrary",)),
    )(act, w_2, h_res, g, target)


def _shift_down(x, s, t_idx):
    return jnp.where(t_idx >= s, pltpu.roll(x, s, 0), 0.0)


def _shift_up(x, s, t_idx, t_len):
    return jnp.where(t_idx < t_len - s, pltpu.roll(x, t_len - s, 0), 0.0)


def _pool_select(group, s2, s4, s8, s16):
    return jnp.where(group == 0, s2, jnp.where(group == 1, s4, jnp.where(group == 2, s8, s16)))


def _pool_count(group, t_idx):
    win = jnp.left_shift(2, group)
    return jnp.minimum(t_idx + 1, win).astype(F32)


def _pool_fwd_math(a, group, t_idx):
    s2 = a + _shift_down(a, 1, t_idx)
    s4 = s2 + _shift_down(s2, 2, t_idx)
    s8 = s4 + _shift_down(s4, 4, t_idx)
    s16 = s8 + _shift_down(s8, 8, t_idx)
    return _pool_select(group, s2, s4, s8, s16) / _pool_count(group, t_idx) - a


def _pool_bwd_math(dpooled, group, t_idx, t_len):
    e = dpooled / _pool_count(group, t_idx)
    s2 = e + _shift_up(e, 1, t_idx, t_len)
    s4 = s2 + _shift_up(s2, 2, t_idx, t_len)
    s8 = s4 + _shift_up(s4, 4, t_idx, t_len)
    s16 = s8 + _shift_up(s8, 8, t_idx, t_len)
    return _pool_select(group, s2, s4, s8, s16) - dpooled


def _conv_fwd_math(c, w_ref, b_ref, t_idx):
    return (w_ref[0:1, :] * _shift_down(c, 2, t_idx) + w_ref[1:2, :] * _shift_down(c, 1, t_idx)
            + w_ref[2:3, :] * c + b_ref[...])


def _ab_fwd(p, pool_w, pool_scale, conv_w, conv_b, nseq, t_len, rider=None):
    m = p.shape[0]
    ng = 4

    def body(a_ref, xb_ref, gb_ref, gc_ref, pw_ref, ps_ref, cw_ref, cb_ref, o_ref):
        j = pl.program_id(1)
        t_idx = lax.broadcasted_iota(jnp.int32, (t_len, TILE), 0)

        @pl.when(j < ng)
        def _():
            pooled = _pool_fwd_math(a_ref[...].astype(F32), j, t_idx)
            mixed = jnp.dot(pooled.astype(BF16), pw_ref[...].astype(BF16), preferred_element_type=F32)
            o_ref[...] = (mixed * ps_ref[...]).astype(BF16)

        @pl.when(j >= ng)
        def _():
            c = gc_ref[...].astype(F32) * xb_ref[...].astype(F32)
            y = _conv_fwd_math(c, cw_ref, cb_ref, t_idx)
            o_ref[...] = (gb_ref[...].astype(F32) * y).astype(BF16)

    def pool_j(j):
        return jnp.minimum(j, ng - 1)

    def conv_j(j):
        return jnp.maximum(j - ng, 0)

    in_specs = [
        pl.BlockSpec((t_len, TILE), lambda s, j: (s, pool_j(j))),
        pl.BlockSpec((t_len, TILE), lambda s, j: (s, ng + conv_j(j))),
        pl.BlockSpec((t_len, TILE), lambda s, j: (s, 2 * ng + conv_j(j))),
        pl.BlockSpec((t_len, TILE), lambda s, j: (s, 3 * ng + conv_j(j))),
        pl.BlockSpec((None, TILE, TILE), lambda s, j: (pool_j(j), 0, 0)),
        pl.BlockSpec((None, 1, TILE), lambda s, j: (pool_j(j), 0, 0)),
        pl.BlockSpec((3, TILE), lambda s, j: (0, conv_j(j))),
        pl.BlockSpec((1, TILE), lambda s, j: (0, conv_j(j))),
    ]
    res, rode = _call(
        body, name="ab_mixer_fwd", grid=(nseq, 2 * ng), in_specs=in_specs,
        out_specs=[pl.BlockSpec((t_len, TILE), lambda s, j: (s, j))],
        out_shape=[jax.ShapeDtypeStruct((m, 2 * ng * TILE), BF16)], scratch_shapes=[],
        semantics=("parallel", "arbitrary"), args=[p, p, p, p, pool_w, pool_scale, conv_w, conv_b], rider=rider)
    return res[0] if rider is None else (res[0], rode)


def _ab_bwd(p, dmix, pool_w, pool_scale, conv_w, conv_b, nseq, t_len, rider=None):
    m = p.shape[0]
    ng = 4

    def body(a_ref, xb_ref, gb_ref, gc_ref, dma_ref, dmb_ref, pw_ref, ps_ref, cw_ref, cb_ref,
             da_ref, dxb_ref, dgb_ref, dgc_ref, dpw_ref, dps_ref, dcw_ref, dcb_ref):
        j = pl.program_id(0)
        first = pl.program_id(1) == 0
        t_idx = lax.broadcasted_iota(jnp.int32, (t_len, TILE), 0)

        pooled = _pool_fwd_math(a_ref[...].astype(F32), j, t_idx).astype(BF16)
        w_bf = pw_ref[...].astype(BF16)
        mixed = jnp.dot(pooled, w_bf, preferred_element_type=F32)
        dm = dma_ref[...].astype(F32)
        dps = jnp.sum(dm * mixed, axis=0, keepdims=True)
        dmixed = (dm * ps_ref[...]).astype(BF16)
        dpw = lax.dot_general(pooled, dmixed, TN_DIMS, preferred_element_type=F32)
        dpooled = lax.dot_general(dmixed, w_bf, NT_DIMS, preferred_element_type=F32)
        da_ref[...] = _pool_bwd_math(dpooled, j, t_idx, t_len).astype(BF16)

        xb = xb_ref[...].astype(F32)
        gb = gb_ref[...].astype(F32)
        gc = gc_ref[...].astype(F32)
        d = dmb_ref[...].astype(F32)
        c = gc * xb
        c1 = _shift_down(c, 1, t_idx)
        c2 = _shift_down(c, 2, t_idx)
        y = cw_ref[0:1, :] * c2 + cw_ref[1:2, :] * c1 + cw_ref[2:3, :] * c + cb_ref[...]
        dgb_ref[...] = (d * y).astype(BF16)
        dy = d * gb
        dc = (cw_ref[2:3, :] * dy + cw_ref[1:2, :] * _shift_up(dy, 1, t_idx, t_len)
              + cw_ref[0:1, :] * _shift_up(dy, 2, t_idx, t_len))
        dgc_ref[...] = (dc * xb).astype(BF16)
        dxb_ref[...] = (dc * gc).astype(BF16)
        dcw = jnp.concatenate([jnp.sum(dy * c2, axis=0, keepdims=True),
                               jnp.sum(dy * c1, axis=0, keepdims=True),
                               jnp.sum(dy * c, axis=0, keepdims=True)], axis=0)
        dcb = jnp.sum(dy, axis=0, keepdims=True)

        @pl.when(first)
        def _():
            dpw_ref[...] = dpw
            dps_ref[...] = dps
            dcw_ref[...] = dcw
            dcb_ref[...] = dcb

        @pl.when(jnp.logical_not(first))
        def _():
            dpw_ref[...] += dpw
            dps_ref[...] += dps
            dcw_ref[...] += dcw
            dcb_ref[...] += dcb

    def col(k):
        return pl.BlockSpec((t_len, TILE), lambda j, s: (s, k * ng + j))

    in_specs = [
        col(0), col(1), col(2), col(3), col(0), col(1),
        pl.BlockSpec((None, TILE, TILE), lambda j, s: (j, 0, 0)),
        pl.BlockSpec((None, 1, TILE), lambda j, s: (j, 0, 0)),
        pl.BlockSpec((3, TILE), lambda j, s: (0, j)),
        pl.BlockSpec((1, TILE), lambda j, s: (0, j)),
    ]
    piece = pl.BlockSpec((t_len, TILE), lambda j, s: (s, j))
    out_specs = [
        piece, piece, piece, piece,
        pl.BlockSpec((None, TILE, TILE), lambda j, s: (j, 0, 0)),
        pl.BlockSpec((None, 1, TILE), lambda j, s: (j, 0, 0)),
        pl.BlockSpec((3, TILE), lambda j, s: (0, j)),
        pl.BlockSpec((1, TILE), lambda j, s: (0, j)),
    ]
    w = ng * TILE
    out_shape = [jax.ShapeDtypeStruct((m, w), BF16)] * 4 + [
        jax.ShapeDtypeStruct((ng, TILE, TILE), F32), jax.ShapeDtypeStruct((ng, 1, TILE), F32),
        jax.ShapeDtypeStruct((3, w), F32), jax.ShapeDtypeStruct((1, w), F32)]
    res, rode = _call(
        body, name="ab_mixer_bwd", grid=(ng, nseq), in_specs=in_specs, out_specs=out_specs, out_shape=out_shape,
        scratch_shapes=[], semantics=("parallel", "arbitrary"),
        args=[p, p, p, p, dmix, dmix, pool_w, pool_scale, conv_w, conv_b], rider=rider)
    return res if rider is None else (res, rode)


SGU_ROWS = 512
INV_SQRT2 = 1.0 / math.sqrt(2.0)
INV_SQRT_2PI = 1.0 / math.sqrt(2.0 * math.pi)


def _gelu(x):
    return 0.5 * x * (1.0 + lax.erf(x * INV_SQRT2))


def _gelu_grad(x):
    return 0.5 * (1.0 + lax.erf(x * INV_SQRT2)) + x * (INV_SQRT_2PI * jnp.exp(-0.5 * x * x))


def _causal_tile(transposed=False):
    r = lax.broadcasted_iota(jnp.int32, (TILE, TILE), 0)
    c = lax.broadcasted_iota(jnp.int32, (TILE, TILE), 1)
    return r <= c if transposed else c <= r


def _sgu_norm(v, g_ref, b_ref):
    mu = jnp.mean(v, axis=-1, keepdims=True)
    xc = v - mu
    rstd = lax.rsqrt(jnp.mean(xc * xc, axis=-1, keepdims=True) + EPS)
    xhat = xc * rstd
    return xhat, rstd, xhat * g_ref[...] + b_ref[...]


def _sgu_fwd(p, norm_g, norm_b, w_s, bias_tile):
    m = p.shape[0]
    ng = 4
    width = ng * TILE

    def body(u_ref, v_ref, g_ref, b_ref, w_ref, bias_ref, o_ref):
        u = _gelu(u_ref[...].astype(F32))
        _, _, vln = _sgu_norm(_gelu(v_ref[...].astype(F32)), g_ref, b_ref)
        vln = vln.astype(BF16)
        causal = _causal_tile()
        for g in range(ng):
            cols = slice(g * TILE, (g + 1) * TILE)
            wg = jnp.where(causal, w_ref[g], 0.0).astype(BF16)
            for n in range(SGU_ROWS // TILE):
                rows = slice(n * TILE, (n + 1) * TILE)
                s = jnp.dot(wg, vln[rows, cols], preferred_element_type=F32) + bias_ref[g]
                o_ref[rows, cols] = (u[rows, cols] * s).astype(BF16)

    vec = pl.BlockSpec((1, width), lambda i: (0, 0))
    tiles = pl.BlockSpec((ng, TILE, TILE), lambda i: (0, 0, 0))
    return pl.pallas_call(
        body, name="sgu_fwd", grid=(m // SGU_ROWS,),
        in_specs=[pl.BlockSpec((SGU_ROWS, width), lambda i: (i, 0)),
                  pl.BlockSpec((SGU_ROWS, width), lambda i: (i, 1)), vec, vec, tiles, tiles],
        out_specs=pl.BlockSpec((SGU_ROWS, width), lambda i: (i, 0)),
        out_shape=jax.ShapeDtypeStruct((m, 2 * width), BF16),
        compiler_params=_params(("parallel",)),
    )(p, p, norm_g, norm_b, w_s, bias_tile)


def _sgu_bwd(p, dmix, norm_g, norm_b, w_s, w_s_t, bias_tile):
    m = p.shape[0]
    ng = 4
    width = ng * TILE

    def body(u_ref, v_ref, dc_ref, g_ref, b_ref, w_ref, wt_ref, bias_ref,
             du_ref, dv_ref, dw_ref, dbs_ref, dg_ref, db_ref, ds_scr, dvln_scr):
        u_pre = u_ref[...].astype(F32)
        v_pre = v_ref[...].astype(F32)
        u = _gelu(u_pre)
        xhat, rstd, vln = _sgu_norm(_gelu(v_pre), g_ref, b_ref)
        vln = vln.astype(BF16)
        dc = dc_ref[...].astype(F32)
        causal = _causal_tile()
        ones = jnp.ones((TILE, TILE), BF16)
        first = pl.program_id(0) == 0
        for g in range(ng):
            cols = slice(g * TILE, (g + 1) * TILE)
            wg = jnp.where(causal, w_ref[g], 0.0).astype(BF16)
            wgt = jnp.where(_causal_tile(transposed=True), wt_ref[g], 0.0).astype(BF16)
            dw_acc = jnp.zeros((TILE, TILE), F32)
            dbs_acc = jnp.zeros((TILE, TILE), F32)
            for n in range(SGU_ROWS // TILE):
                rows = slice(n * TILE, (n + 1) * TILE)
                vt = vln[rows, cols]
                s = jnp.dot(wg, vt, preferred_element_type=F32) + bias_ref[g]
                ds_scr[rows, cols] = dc[rows, cols] * s
                ds = (dc[rows, cols] * u[rows, cols]).astype(BF16)
                dw_acc += lax.dot_general(ds, vt, NT_DIMS, preferred_element_type=F32)
                dbs_acc += jnp.dot(ds, ones, preferred_element_type=F32)
                dvln_scr[rows, cols] = jnp.dot(wgt, ds, preferred_element_type=F32)
            dw_g = jnp.where(causal, dw_acc, 0.0)

            @pl.when(first)
            def _():
                dw_ref[g] = dw_g
                dbs_ref[g] = dbs_acc

            @pl.when(jnp.logical_not(first))
            def _():
                dw_ref[g] += dw_g
                dbs_ref[g] += dbs_acc

        du_ref[...] = (ds_scr[...] * _gelu_grad(u_pre)).astype(BF16)
        dvln = dvln_scr[...]
        dxhat = dvln * g_ref[...]
        dv = rstd * (dxhat - jnp.mean(dxhat, axis=-1, keepdims=True)
                     - xhat * jnp.mean(dxhat * xhat, axis=-1, keepdims=True))
        dv_ref[...] = (dv * _gelu_grad(v_pre)).astype(BF16)
        dg_part = jnp.sum(dvln * xhat, axis=0, keepdims=True)
        db_part = jnp.sum(dvln, axis=0, keepdims=True)

        @pl.when(first)
        def _():
            dg_ref[...] = dg_part
            db_ref[...] = db_part

        @pl.when(jnp.logical_not(first))
        def _():
            dg_ref[...] += dg_part
            db_ref[...] += db_part

    vec = pl.BlockSpec((1, width), lambda i: (0, 0))
    tiles = pl.BlockSpec((ng, TILE, TILE), lambda i: (0, 0, 0))
    rows0 = pl.BlockSpec((SGU_ROWS, width), lambda i: (i, 0))
    rows1 = pl.BlockSpec((SGU_ROWS, width), lambda i: (i, 1))
    return pl.pallas_call(
        body, name="sgu_bwd", grid=(m // SGU_ROWS,),
        in_specs=[rows0, rows1, rows0, vec, vec, tiles, tiles, tiles],
        out_specs=[rows0, rows0, tiles, tiles, vec, vec],
        out_shape=[jax.ShapeDtypeStruct((m, width), BF16), jax.ShapeDtypeStruct((m, width), BF16),
                   jax.ShapeDtypeStruct((ng, TILE, TILE), F32), jax.ShapeDtypeStruct((ng, TILE, TILE), F32),
                   jax.ShapeDtypeStruct((1, width), F32), jax.ShapeDtypeStruct((1, width), F32)],
        scratch_shapes=[pltpu.VMEM((SGU_ROWS, width), F32), pltpu.VMEM((SGU_ROWS, width), F32)],
        compiler_params=_params(("arbitrary",)),
    )(p, p, dmix, norm_g, norm_b, w_s, w_s_t, bias_tile)


SB_DH = 64
SB_SCALE = 1.0 / math.sqrt(SB_DH)


SB_BLOCK = 256
SB_SUB = SB_BLOCK // TILE
SB_PASS = 4


def _split_passes(i):
    rem = i % SB_PASS
    return i // SB_PASS, rem >= 2, rem % 2 == 1


def _sum_matrix(kind):
    j = lax.broadcasted_iota(jnp.int32, (TILE, 2 * TILE), 0)
    s = lax.broadcasted_iota(jnp.int32, (TILE, 2 * TILE), 1)
    tri = {"after": j > s, "upto": j <= s, "before": j < s}[kind]
    return jnp.where(jnp.logical_or(s >= TILE, tri), 1.0, 0.0).astype(BF16)


def _strict_mask():
    r = lax.broadcasted_iota(jnp.int32, (SB_BLOCK, SB_BLOCK), 0)
    c = lax.broadcasted_iota(jnp.int32, (SB_BLOCK, SB_BLOCK), 1)
    return c < r


def _head_lanes(h):
    lane = lax.broadcasted_iota(jnp.int32, (1, TILE), 1)
    return (lane >= h * SB_DH) & (lane < (h + 1) * SB_DH)


def _log_gates(z):
    log_sig = jnp.minimum(z, 0.0) - jnp.log(1.0 + jnp.exp(-jnp.abs(z)))
    return log_sig, log_sig - z


def _sb_fwd(p, mix, nseq, t_len, gather):
    m = p.shape[0]
    npair = 4
    ng = len(gather)
    last_step = nseq * npair - 1

    def body(q_ref, k_ref, v_ref, *rest):
        o_ref, lt_ref = rest[ng + 1:ng + 3]
        kh_ref, vh_ref = rest[2 * ng + 3:2 * ng + 5]
        step = pl.program_id(0) * npair + pl.program_id(1)
        send, forward, finish = _gather_steps(rest[ng + 3:2 * ng + 3], *rest[2 * ng + 5:])
        pl.when(step == 0)(send)
        pl.when(step == (last_step + 1) // 2)(forward)
        for h in range(2):
            keep = _head_lanes(h)
            kh_ref[h] = jnp.where(keep, k_ref[...], 0).astype(BF16)
            vh_ref[h] = jnp.where(keep, v_ref[...], 0).astype(BF16)
        summat = _sum_matrix("after")
        strict = _strict_mask()

        def one_pass(q, row0, nsub, diag, state):
            rows = pl.ds(row0, nsub * TILE)
            log_sig, pieces = [], []
            for h in range(2):
                zh = lax.dot_general(q, kh_ref[h, rows, :], NT_DIMS, preferred_element_type=F32)
                log_sig_h, logkeep = _log_gates(zh)
                if diag:
                    logkeep = jnp.where(strict, logkeep, 0.0)
                log_sig.append(log_sig_h)
                pieces += [logkeep[:, b * TILE:(b + 1) * TILE] for b in range(nsub)]
            sums = jnp.dot(jnp.concatenate(pieces, axis=0).astype(BF16), summat, preferred_element_type=F32)
            out = []
            for h in range(2):
                carry, acc = state[2 * h], state[2 * h + 1]
                after = [None] * nsub
                for b in reversed(range(nsub)):
                    part = sums[(h * nsub + b) * SB_BLOCK:(h * nsub + b + 1) * SB_BLOCK]
                    after[b] = part[:, :TILE] + carry
                    carry = carry + part[:, TILE:]
                w = jnp.exp(log_sig[h] + jnp.concatenate(after, axis=1))
                if diag:
                    w = jnp.where(strict, w, 0.0)
                out += [carry, acc + jnp.dot(w.astype(BF16), vh_ref[h, rows, :], preferred_element_type=F32)]
            return tuple(out)

        def q_block(i, _):
            r0 = pl.multiple_of(i * SB_BLOCK, SB_BLOCK)
            q = q_ref[pl.ds(r0, SB_BLOCK), :] * SB_SCALE
            zero = jnp.zeros((SB_BLOCK, TILE), F32)
            state = one_pass(q, r0, SB_SUB, True, (zero,) * 4)
            full, two, one = _split_passes(i)
            state = lax.fori_loop(
                0, full,
                lambda jj, st: one_pass(q, pl.multiple_of((i - SB_PASS * (jj + 1)) * SB_BLOCK, SB_BLOCK),
                                        SB_PASS * SB_SUB, False, st),
                state)
            state = lax.cond(
                two, lambda st: one_pass(q, pl.multiple_of((i % 2) * SB_BLOCK, SB_BLOCK), 2 * SB_SUB, False, st),
                lambda st: st, state)
            state = lax.cond(one, lambda st: one_pass(q, 0, SB_SUB, False, st), lambda st: st, state)
            o_ref[pl.ds(r0, SB_BLOCK), :] = (state[1] + state[3]).astype(BF16)
            lt_ref[pl.ds(r0, SB_BLOCK), :] = jnp.where(_head_lanes(0), state[0], state[2])
            return 0

        lax.fori_loop(0, t_len // SB_BLOCK, q_block, 0)
        pl.when(step == last_step)(finish)

    def col(k):
        return pl.BlockSpec((t_len, TILE), lambda s, hp: (s, k * npair + hp))

    out = pl.BlockSpec((t_len, TILE), lambda s, hp: (s, hp))
    res = pl.pallas_call(
        body, name="stickbreak_fwd", grid=(nseq, npair), in_specs=[col(2), col(3), col(4)] + [ANY] * (ng + 1),
        out_specs=[pl.BlockSpec((t_len, TILE), lambda s, hp: (s, npair + hp)), out] + [ANY] * ng,
        out_shape=[jax.ShapeDtypeStruct(mix.shape, BF16), jax.ShapeDtypeStruct((m, npair * TILE), F32)]
        + [jax.ShapeDtypeStruct(b.shape, b.dtype) for b in gather],
        input_output_aliases={**{3 + a: 2 + a for a in range(ng)}, 3 + ng: 0},
        scratch_shapes=[pltpu.VMEM((2, t_len, TILE), BF16), pltpu.VMEM((2, t_len, TILE), BF16)] + _gather_sems(ng),
        compiler_params=pltpu.CompilerParams(dimension_semantics=("arbitrary", "arbitrary"),
                                             vmem_limit_bytes=VMEM_LIMIT_BYTES, has_side_effects=True),
    )(p, p, p, *gather, mix)
    return res[0], res[1], res[2:]


def _sb_bwd(p, dmix, ltot, nseq, t_len, exchange):
    m = p.shape[0]
    npair = 4
    ne = len(exchange)
    last_step = nseq * npair - 1

    def body(q_ref, k_ref, v_ref, do_ref, lt_ref, *rest):
        dq_ref, dk_ref, dv_ref = rest[ne:ne + 3]
        kh_ref, vh_ref, dk_acc, dv_acc = rest[2 * ne + 3:2 * ne + 7]
        step = pl.program_id(0) * npair + pl.program_id(1)
        send, finish = _exchange_steps(rest[:ne], rest[ne + 3:2 * ne + 3], *rest[2 * ne + 7:])
        pl.when(step == 0)(send)
        for h in range(2):
            keep = _head_lanes(h)
            kh_ref[h] = jnp.where(keep, k_ref[...], 0).astype(BF16)
            vh_ref[h] = jnp.where(keep, v_ref[...], 0).astype(BF16)
        dk_acc[...] = jnp.zeros_like(dk_acc)
        dv_acc[...] = jnp.zeros_like(dv_acc)
        sum_upto = _sum_matrix("upto")
        sum_before = _sum_matrix("before")
        strict = _strict_mask()
        lane = lax.broadcasted_iota(jnp.int32, (SB_BLOCK, TILE), 1)

        def running(x, matrix, start, nsub):
            pieces = [x[h][:, b * TILE:(b + 1) * TILE] for h in range(2) for b in range(nsub)]
            sums = jnp.dot(jnp.concatenate(pieces, axis=0).astype(BF16), matrix, preferred_element_type=F32)
            wide, ends = [], []
            for h in range(2):
                total, cols = start[h], []
                for b in range(nsub):
                    part = sums[(h * nsub + b) * SB_BLOCK:(h * nsub + b + 1) * SB_BLOCK]
                    cols.append(part[:, :TILE] + total)
                    total = total + part[:, TILE:]
                wide.append(jnp.concatenate(cols, axis=1))
                ends.append(total)
            return wide, ends

        def one_pass(q, do, qh, doh, ltot, row0, nsub, diag, state):
            rows = pl.ds(row0, nsub * TILE)
            log_sig, logkeep = [], []
            for h in range(2):
                zh = lax.dot_general(q, kh_ref[h, rows, :], NT_DIMS, preferred_element_type=F32)
                log_sig_h, logkeep_h = _log_gates(zh)
                log_sig.append(log_sig_h)
                logkeep.append(jnp.where(strict, logkeep_h, 0.0) if diag else logkeep_h)
            upto, sum_l = running(logkeep, sum_upto, [state[0], state[3]], nsub)
            w, g = [], []
            for h in range(2):
                wh = jnp.exp(log_sig[h] + (ltot[h] - upto[h]))
                if diag:
                    wh = jnp.where(strict, wh, 0.0)
                w.append(wh)
                g.append(wh * lax.dot_general(do, vh_ref[h, rows, :], NT_DIMS, preferred_element_type=F32))
            g_before, sum_g = running(g, sum_before, [state[1], state[4]], nsub)
            out, dk_new, dv_new = [], 0.0, 0.0
            for h in range(2):
                dz = g[h] - jnp.exp(log_sig[h]) * (g[h] + g_before[h])
                if diag:
                    dz = jnp.where(strict, dz, 0.0)
                dzb = dz.astype(BF16)
                dq = state[3 * h + 2] + jnp.dot(dzb, kh_ref[h, rows, :], preferred_element_type=F32)
                dk_new = dk_new + lax.dot_general(dzb, qh[h], TN_DIMS, preferred_element_type=F32)
                dv_new = dv_new + lax.dot_general(w[h].astype(BF16), doh[h], TN_DIMS, preferred_element_type=F32)
                out += [sum_l[h], sum_g[h], dq]
            dk_acc[rows, :] += dk_new
            dv_acc[rows, :] += dv_new
            return tuple(out)

        def q_block(i, _):
            r0 = pl.multiple_of(i * SB_BLOCK, SB_BLOCK)
            q = q_ref[pl.ds(r0, SB_BLOCK), :] * SB_SCALE
            do = do_ref[pl.ds(r0, SB_BLOCK), :]
            lt = lt_ref[pl.ds(r0, SB_BLOCK), :]
            qh, doh, ltot = [], [], []
            for h in range(2):
                keep = _head_lanes(h)
                qh.append(jnp.where(keep, q, 0).astype(BF16))
                doh.append(jnp.where(keep, do, 0).astype(BF16))
                ltot.append(jnp.sum(jnp.where(lane == h * SB_DH, lt, 0.0), axis=1, keepdims=True))
            zero = jnp.zeros((SB_BLOCK, TILE), F32)
            full, two, one = _split_passes(i)
            state = lax.fori_loop(
                0, full,
                lambda jj, st: one_pass(q, do, qh, doh, ltot, pl.multiple_of(SB_PASS * jj * SB_BLOCK, SB_BLOCK),
                                        SB_PASS * SB_SUB, False, st),
                (zero,) * 6)
            state = lax.cond(
                two,
                lambda st: one_pass(q, do, qh, doh, ltot, pl.multiple_of(SB_PASS * full * SB_BLOCK, SB_BLOCK),
                                    2 * SB_SUB, False, st),
                lambda st: st, state)
            state = lax.cond(
                one,
                lambda st: one_pass(q, do, qh, doh, ltot, pl.multiple_of((i - 1) * SB_BLOCK, SB_BLOCK), SB_SUB, False, st),
                lambda st: st, state)
            state = one_pass(q, do, qh, doh, ltot, r0, SB_SUB, True, state)
            dq_ref[pl.ds(r0, SB_BLOCK), :] = ((state[2] + state[5]) * SB_SCALE).astype(BF16)
            return 0

        lax.fori_loop(0, t_len // SB_BLOCK, q_block, 0)
        dk_ref[...] = dk_acc[...].astype(BF16)
        dv_ref[...] = dv_acc[...].astype(BF16)
        pl.when(step == last_step)(finish)

    def col(k):
        return pl.BlockSpec((t_len, TILE), lambda s, hp: (s, k * npair + hp))

    out = pl.BlockSpec((t_len, TILE), lambda s, hp: (s, hp))
    width = npair * TILE
    res = pl.pallas_call(
        body, name="stickbreak_bwd", grid=(nseq, npair),
        in_specs=[col(2), col(3), col(4), col(1), out] + [ANY] * ne, out_specs=[out, out, out] + [ANY] * ne,
        out_shape=[jax.ShapeDtypeStruct((m, width), BF16)] * 3 + _exchange_shapes(exchange),
        scratch_shapes=[pltpu.VMEM((2, t_len, TILE), BF16), pltpu.VMEM((2, t_len, TILE), BF16),
                        pltpu.VMEM((t_len, TILE), F32), pltpu.VMEM((t_len, TILE), F32)] + _exchange_sems(ne),
        compiler_params=pltpu.CompilerParams(dimension_semantics=("arbitrary", "arbitrary"),
                                             vmem_limit_bytes=VMEM_LIMIT_BYTES, has_side_effects=True),
    )(p, p, p, dmix, ltot, *exchange)
    return res[0], res[1], res[2], res[3:]


def _adam_math(w, g, m, v):
    m = ADAM_B1 * m + (1.0 - ADAM_B1) * g
    v = ADAM_B2 * v + (1.0 - ADAM_B2) * (g * g)
    m_hat = m / (1.0 - ADAM_B1 ** ADAM_STEP)
    v_hat = v / (1.0 - ADAM_B2 ** ADAM_STEP)
    delta = -ADAM_LR * (m_hat / (jnp.sqrt(v_hat) + ADAM_EPS) + ADAM_WD * w)
    return delta, m, v


def _cast_place(w, layer, pos, *, name):
    _, r, c = w.shape
    tr = min(r, 256)

    def body(pos_ref, w_ref, o_ref):
        o_ref[...] = w_ref[...].astype(BF16)

    grid_spec = pltpu.PrefetchScalarGridSpec(
        num_scalar_prefetch=1, grid=(r // tr,),
        in_specs=[pl.BlockSpec((None, tr, c), lambda i, pos_ref: (layer, i, 0))],
        out_specs=pl.BlockSpec((None, None, tr, c), lambda i, pos_ref: (0, pos_ref[0], i, 0)))
    return pl.pallas_call(
        body, name=name, grid_spec=grid_spec, out_shape=jax.ShapeDtypeStruct((1, N_CHIP, r, c), BF16),
        compiler_params=_params(("parallel",)),
    )(pos, w)


def _cast_place_all(items, pos, *, name, rider=None):
    tiles = [min(w.shape[1], 256) for w, _ in items]
    counts = [w.shape[1] // t for (w, _), t in zip(items, tiles)]
    starts = [sum(counts[:a]) for a in range(len(items))]
    n = len(items)

    def body(pos_ref, *refs):
        i = pl.program_id(0)
        for a in range(n):
            @pl.when((i >= starts[a]) & (i < starts[a] + counts[a]))
            def _():
                refs[n + a][...] = refs[a][...].astype(BF16)

    def block(a):
        return lambda i: jnp.clip(i - starts[a], 0, counts[a] - 1)

    in_specs, out_specs, out_shape = [], [], []
    for a, ((w, layer), t) in enumerate(zip(items, tiles)):
        _, r, c = w.shape
        in_specs.append(pl.BlockSpec((None, t, c), lambda i, pos_ref, a=a, layer=layer: (layer, block(a)(i), 0)))
        out_specs.append(pl.BlockSpec((None, None, t, c), lambda i, pos_ref, a=a: (0, pos_ref[0], block(a)(i), 0)))
        out_shape.append(jax.ShapeDtypeStruct((1, N_CHIP, r, c), BF16))
    res, rode = _call(body, name=name, grid=(sum(counts),), in_specs=in_specs, out_specs=out_specs,
                      out_shape=out_shape, scratch_shapes=[], semantics=("arbitrary",),
                      args=[w for w, _ in items], rider=rider, prefetch=pos)
    return res if rider is None else (res, rode)


def _pair_sum(mine, got, pos, *, name):
    l_dim, s_dim, h, c = got.shape
    th = min(h, 512)
    nt = h // th

    def body(pos_ref, a_ref, b_ref, o_ref):
        o_ref[...] = (a_ref[...].astype(F32) + b_ref[...].astype(F32)).astype(BF16)

    pieces = 2
    spec = pl.BlockSpec((None, pieces, th, c), lambda l, s, i, pos_ref: (l, s, i, 0))
    grid_spec = pltpu.PrefetchScalarGridSpec(
        num_scalar_prefetch=1, grid=(l_dim, s_dim // pieces, nt),
        in_specs=[pl.BlockSpec((None, pieces, th, c), lambda l, s, i, pos_ref: (l, s, pos_ref[1] * nt + i, 0)), spec],
        out_specs=spec)
    return pl.pallas_call(
        body, name=name, grid_spec=grid_spec, out_shape=jax.ShapeDtypeStruct(got.shape, BF16),
        compiler_params=_params(("parallel",) * 3),
    )(pos, mine, got)


def _chip_sum(sums, landed, pos, *, name):
    l_dim, _, h, c = sums.shape
    th = min(h, 256)
    nt = h // th

    def body(pos_ref, own, r0, r1, r2, o_ref):
        o_ref[...] = ((own[...].astype(F32) + r0[...].astype(F32)) + r1[...].astype(F32)) + r2[...].astype(F32)

    def piece(k):
        return pl.BlockSpec((None, None, th, c), lambda l, i, pos_ref: (l, k, i, 0))

    grid_spec = pltpu.PrefetchScalarGridSpec(
        num_scalar_prefetch=1, grid=(l_dim, nt),
        in_specs=[pl.BlockSpec((None, None, th, c), lambda l, i, pos_ref: (l, pos_ref[0], i, 0)),
                  piece(0), piece(1), piece(2)],
        out_specs=pl.BlockSpec((None, th, c), lambda l, i, pos_ref: (l, pos_ref[1] * nt + i, 0)))
    return pl.pallas_call(
        body, name=name, grid_spec=grid_spec, out_shape=jax.ShapeDtypeStruct((l_dim, 2 * h, c), F32),
        compiler_params=_params(("parallel",) * 2),
    )(pos, sums, landed, landed, landed)


def _adam_big(w, m, v, grads, *, name):
    l_dim, r, c = w.shape
    assert len(grads) == l_dim
    tr = min(r, 512)

    def body(*refs):
        w_ref, m_ref, v_ref = refs[:3]
        g_refs = refs[3:3 + l_dim]
        go_ref, d_ref, mo_ref, vo_ref = refs[3 + l_dim:]
        g = g_refs[0][...]
        for l in range(1, l_dim):
            g = jnp.where(pl.program_id(0) == l, g_refs[l][...], g)
        delta, m_new, v_new = _adam_math(w_ref[...], g, m_ref[...], v_ref[...])
        go_ref[...] = g
        d_ref[...] = delta
        mo_ref[...] = m_new
        vo_ref[...] = v_new

    spec = pl.BlockSpec((None, tr, c), lambda l, i: (l, i, 0))
    gspec = pl.BlockSpec((None, tr, c), lambda l, i: (0, i, 0))
    return pl.pallas_call(
        body, name=name, grid=(l_dim, r // tr), in_specs=[spec] * 3 + [gspec] * l_dim, out_specs=[spec] * 4,
        out_shape=[jax.ShapeDtypeStruct(w.shape, F32)] * 4, compiler_params=_params(("parallel",) * 2),
    )(w, m, v, *grads)


def _position():
    return lax.axis_index("x"), lax.axis_index("y"), lax.axis_index("c")


def _other_chips(x, y):
    return [(1 - x, y), (x, 1 - y), (1 - x, 1 - y)]


def _remote(src, dst, send_sem, recv_sem, device):
    return pltpu.make_async_remote_copy(src_ref=src, dst_ref=dst, send_sem=send_sem, recv_sem=recv_sem,
                                        device_id=device, device_id_type=MESH)


ANY = pl.BlockSpec(memory_space=pl.ANY)


def _gather_sems(n):
    return [pltpu.SemaphoreType.DMA((3 * n,))] * 4


def _gather_steps(outs, send_sems, recv_sems, fwd_send, fwd_recv):
    n = len(outs)
    x, y, c = _position()
    chips = _other_chips(x, y)
    sibling = (x, y, 1 - c)

    def half(a, chip, core):
        h = outs[a].shape[2] // 2
        return outs[a].at[:, 2 * chip[0] + chip[1], pl.ds(core * h, h), :]

    def over_ici(a, k, chip):
        block = half(a, chip, c)
        return _remote(block, block, send_sems.at[3 * a + k], recv_sems.at[3 * a + k], (*chips[k], c))

    def over_d2d(a, k, core):
        block = half(a, chips[k], core)
        return _remote(block, block, fwd_send.at[3 * a + k], fwd_recv.at[3 * a + k], sibling)

    def send():
        for a in range(n):
            for k in range(3):
                over_ici(a, k, (x, y)).start()

    def forward():
        for k in range(3):
            for a in range(n):
                over_ici(a, k, chips[k]).wait_recv()
                over_d2d(a, k, c).start()

    def finish():
        for k in range(3):
            for a in range(n):
                over_d2d(a, k, 1 - c).wait_recv()
        for a in range(n):
            for k in range(3):
                over_ici(a, k, (x, y)).wait_send()
                over_d2d(a, k, c).wait_send()

    return send, forward, finish


def _swap_halves(grads, *, name):
    n = len(grads)

    def body(*refs):
        send, finish = _swap_steps(refs[:n], refs[n:2 * n], *refs[2 * n:])
        send()
        finish()

    sem = pltpu.SemaphoreType.DMA((n,))
    return pl.pallas_call(
        body, name=name, in_specs=[ANY] * n, out_specs=[ANY] * n, out_shape=_swap_shapes(grads),
        scratch_shapes=[sem, sem], compiler_params=pltpu.CompilerParams(has_side_effects=True),
    )(*grads)


def _swap_shapes(grads):
    return [jax.ShapeDtypeStruct(g.shape[:2] + (g.shape[2] // 2, g.shape[3]), g.dtype) for g in grads]


def _swap_steps(ins, outs, send_sems, recv_sems):
    x, y, c = _position()

    def copy(a):
        h = ins[a].shape[2] // 2
        return _remote(ins[a].at[:, :, pl.ds((1 - c) * h, h), :], outs[a], send_sems.at[a], recv_sems.at[a],
                       (x, y, 1 - c))

    def send():
        for a in range(len(ins)):
            copy(a).start()

    def finish():
        for a in range(len(ins)):
            copy(a).wait()

    return send, finish


def _exchange_shapes(sums):
    return [jax.ShapeDtypeStruct((s.shape[0], 3) + s.shape[2:], s.dtype) for s in sums]


def _exchange_sems(n):
    return [pltpu.SemaphoreType.DMA((3 * n,))] * 2


def _exchange_steps(ins, outs, send_sems, recv_sems):
    n = len(ins)
    x, y, c = _position()
    chips = _other_chips(x, y)

    def copy(a, k):
        chip = chips[k]
        return _remote(ins[a].at[:, 2 * chip[0] + chip[1]], outs[a].at[:, k],
                       send_sems.at[3 * a + k], recv_sems.at[3 * a + k], (*chip, c))

    def send():
        for a in range(n):
            for k in range(3):
                copy(a, k).start()

    def finish():
        for a in range(n):
            for k in range(3):
                copy(a, k).wait()

    return send, finish


def _join_halves(bufs, *, name):
    n = len(bufs)

    def body(*refs):
        send, finish = _join_steps(refs[n:2 * n], *refs[2 * n:])
        send()
        finish()

    sem = pltpu.SemaphoreType.DMA((n,))
    return pl.pallas_call(
        body, name=name, in_specs=[ANY] * n, out_specs=[ANY] * n,
        out_shape=[jax.ShapeDtypeStruct(b.shape, b.dtype) for b in bufs],
        input_output_aliases={a: a for a in range(n)},
        scratch_shapes=[sem, sem], compiler_params=pltpu.CompilerParams(has_side_effects=True),
    )(*bufs)


def _join_steps(outs, send_sems, recv_sems):
    x, y, c = _position()

    def copy(a, core):
        h = outs[a].shape[1] // 2
        half = outs[a].at[:, pl.ds(core * h, h), :]
        return _remote(half, half, send_sems.at[a], recv_sems.at[a], (x, y, 1 - c))

    def send():
        for a in range(len(outs)):
            copy(a, c).start()

    def finish():
        for a in range(len(outs)):
            copy(a, c).wait_send()
            copy(a, 1 - c).wait_recv()

    return send, finish


def _allgather_steps(ins, outs, send_sems, recv_sems, local_sems):
    n = len(ins)
    x, y, c = _position()
    me, sibling = (x, y, c), (x, y, 1 - c)
    chips = _other_chips(x, y)

    def slot(a, dev):
        return outs[a].at[4 * dev[0] + 2 * dev[1] + dev[2]]

    def copy(a, k, block, to, own=False):
        return _remote(ins[a] if own else slot(a, block), slot(a, block),
                       send_sems.at[7 * a + k], recv_sems.at[7 * a + k], to)

    def first(a):
        return [copy(a, 0, me, sibling, own=True)] + [copy(a, 1 + k, me, (*chips[k], c), own=True) for k in range(3)]

    def local(a):
        return pltpu.make_async_copy(ins[a], slot(a, me), local_sems.at[a])

    def send():
        for a in range(n):
            local(a).start()
            for cp in first(a):
                cp.start()

    def forward():
        for a in range(n):
            for k in range(3):
                copy(a, 1 + k, (*chips[k], c), me).wait_recv()
                copy(a, 4 + k, (*chips[k], c), sibling).start()

    def finish():
        for a in range(n):
            copy(a, 0, sibling, me).wait_recv()
            for k in range(3):
                copy(a, 4 + k, (*chips[k], 1 - c), me).wait_recv()
        for a in range(n):
            for cp in first(a) + [copy(a, 4 + k, (*chips[k], c), sibling) for k in range(3)]:
                cp.wait_send()
            local(a).wait()

    return send, forward, finish


def _allreduce_small(packs):
    n = len(packs)

    def body(*refs):
        ins, outs, gath = refs[:n], refs[n:2 * n], refs[2 * n:3 * n]
        send_sems, recv_sems = refs[3 * n:]
        x, y, c = _position()
        me, sibling = (x, y, c), (x, y, 1 - c)
        chips = _other_chips(x, y)

        def slot(a, dev):
            return gath[a].at[4 * dev[0] + 2 * dev[1] + dev[2]]

        def copy(a, k, block, to, src=None):
            return _remote(slot(a, block) if src is None else src, slot(a, block),
                           send_sems.at[7 * a + k], recv_sems.at[7 * a + k], to)

        started = []
        for a in range(n):
            slot(a, me)[...] = ins[a][...]
            first = [copy(a, 0, me, sibling, src=ins[a])]
            first += [copy(a, 1 + k, me, (*chip, c), src=ins[a]) for k, chip in enumerate(chips)]
            for cp in first:
                cp.start()
            started += first
        for a in range(n):
            for k, chip in enumerate(chips):
                copy(a, 1 + k, (*chip, c), me).wait_recv()
                cp = copy(a, 4 + k, (*chip, c), sibling)
                cp.start()
                started.append(cp)
        for a in range(n):
            copy(a, 0, sibling, me).wait_recv()
            for k, chip in enumerate(chips):
                copy(a, 4 + k, (*chip, 1 - c), me).wait_recv()
        for cp in started:
            cp.wait_send()
        for a in range(n):
            total = gath[a][0]
            for d in range(1, N_DEV):
                total = total + gath[a][d]
            outs[a][...] = total

    vmem = pl.BlockSpec(memory_space=pltpu.VMEM)
    sem = pltpu.SemaphoreType.DMA((7 * n,))
    return pl.pallas_call(
        body, name="allreduce_small", in_specs=[vmem] * n, out_specs=[vmem] * n,
        out_shape=[jax.ShapeDtypeStruct(p.shape, p.dtype) for p in packs],
        scratch_shapes=[pltpu.VMEM((N_DEV,) + p.shape, p.dtype) for p in packs] + [sem, sem],
        compiler_params=pltpu.CompilerParams(has_side_effects=True, vmem_limit_bytes=VMEM_LIMIT_BYTES),
    )(*packs)


LOSS_ROW = 520


def _pad_rows(a, rows=8):
    return jnp.concatenate([a, jnp.zeros((rows - a.shape[0], a.shape[1]), a.dtype)], axis=0)

def _adam_small(wide, mid, sgu, pool, late, params):
    names = ["mix_norm_g", "mlp_norm_g", "final_norm_g", "conv_b", "conv_w", "sgu_norm_g", "sgu_norm_b",
             "pool_w", "pool_scale", "sgu_w", "sgu_b"]
    n = len(names)

    def body(*refs):
        wmv = refs[5:5 + 3 * n]
        outs = refs[5 + 3 * n:]
        x, y, _ = _position()
        q = 2 * x + y

        def total(ref):
            t = ref[0]
            for dev in range(1, N_DEV):
                t = t + ref[dev]
            return t

        wide_sum, mid_sum, sgu_sum, pool_sum = total(refs[0]), total(refs[1]), total(refs[2]), total(refs[3])
        late_ref = refs[4]

        def my_quarter(rows):
            parts = [rows[:, s * TILE:(s + 1) * TILE] for s in range(N_CHIP)]
            return jnp.where(q == 0, parts[0], jnp.where(q == 1, parts[1], jnp.where(q == 2, parts[2], parts[3])))

        def tiles(pack):
            return [((0, g), pack[g * TILE:(g + 1) * TILE, :]) for g in range(4)]

        grads = {
            "mix_norm_g": [((), wide_sum[0:2, :] + late_ref[0:2, :])],
            "mlp_norm_g": [((), wide_sum[8:10, :])],
            "final_norm_g": [((), wide_sum[16:17, :])],
            "conv_b": [((), mid_sum[0:1, :])],
            "conv_w": [((0,), my_quarter(mid_sum[8:11, :]))],
            "sgu_norm_g": [((), my_quarter(mid_sum[16:17, :]))],
            "sgu_norm_b": [((), my_quarter(mid_sum[24:25, :]))],
            "pool_w": tiles(pool_sum),
            "sgu_w": tiles(sgu_sum),
            "pool_scale": [((0,), pool_sum[512:516, :])],
            "sgu_b": [((0,), sgu_sum[512:516, :])],
        }
        outs[4 * n][...] = sgu_sum[LOSS_ROW:LOSS_ROW + 8, :]
        for i, name in enumerate(names):
            w_ref, m_ref, v_ref = wmv[3 * i:3 * i + 3]
            for lead, g in grads[name]:
                idx = lead + (slice(None), slice(None))
                delta, m_new, v_new = _adam_math(w_ref[idx], g, m_ref[idx], v_ref[idx])
                outs[4 * i][idx] = g
                outs[4 * i + 1][idx] = delta
                outs[4 * i + 2][idx] = m_new
                outs[4 * i + 3][idx] = v_new

    vmem = pl.BlockSpec(memory_space=pltpu.VMEM)
    args, out_shape = [wide, mid, sgu, pool, late], []
    for name in names:
        w, m, v = params[name]
        args += [w, m, v]
        out_shape += [jax.ShapeDtypeStruct(w.shape, F32)] * 4
    out_shape.append(jax.ShapeDtypeStruct((8, TILE), F32))
    res = pl.pallas_call(
        body, name="adam_small", in_specs=[vmem] * len(args), out_specs=[vmem] * len(out_shape),
        out_shape=out_shape, compiler_params=pltpu.CompilerParams(vmem_limit_bytes=VMEM_LIMIT_BYTES),
    )(*args)
    return {name: res[4 * i:4 * i + 4] for i, name in enumerate(names)}, res[4 * n]


def _pair_sums(grads, got, pos, tag):
    return [_pair_sum(a, b, pos, name=f"pair_sum_{tag}{i}") for i, (a, b) in enumerate(zip(grads, got))]


def _chip_sums(sums, landed, pos, tag):
    return [_chip_sum(s, r, pos, name=f"chip_sum_{tag}{i}") for i, (s, r) in enumerate(zip(sums, landed))]


def kernel(x, mix_norm_g, mlp_norm_g, ab_w_in, pool_w, pool_scale, conv_w, conv_b, ab_w_out, cd_w_in, sgu_norm_g, sgu_norm_b, sgu_w, sgu_b, cd_w_out, mlp_w1, mlp_w2, final_norm_g, loss_target, m_mix_norm_g, m_mlp_norm_g, m_ab_w_in, m_pool_w, m_pool_scale, m_conv_w, m_conv_b, m_ab_w_out, m_cd_w_in, m_sgu_norm_g, m_sgu_norm_b, m_sgu_w, m_sgu_b, m_cd_w_out, m_mlp_w1, m_mlp_w2, m_final_norm_g, v_mix_norm_g, v_mlp_norm_g, v_ab_w_in, v_pool_w, v_pool_scale, v_conv_w, v_conv_b, v_ab_w_out, v_cd_w_in, v_sgu_norm_g, v_sgu_norm_b, v_sgu_w, v_sgu_b, v_cd_w_out, v_mlp_w1, v_mlp_w2, v_final_norm_g):
    nseq, t_len, d = x.shape
    m_tok = nseq * t_len
    h0 = x.reshape(m_tok, d)
    target = loss_target.reshape(m_tok, d)

    x_idx, y_idx = lax.axis_index("x"), lax.axis_index("y")
    q_idx = 2 * x_idx + y_idx
    pos = jnp.stack([q_idx, lax.axis_index("c")]).astype(jnp.int32)

    def shard_buffer(w, layer, tag):
        return _cast_place(w, layer, pos, name=f"cast_place_{tag}")

    def row_block(w):
        return w.reshape(1, 1, -1, w.shape[-1])

    (buf_ab_out, buf_w1_0, buf_w2_0, buf_cd_in, *later_weights), ((w_ab_in,),) = _cast_place_all(
        [(ab_w_out, 0), (mlp_w1, 0), (mlp_w2, 0), (cd_w_in, 0), (cd_w_out, 0), (mlp_w1, 1), (mlp_w2, 1)], pos,
        name="cast_place_rest", rider=[("gather", [shard_buffer(ab_w_in, 0, "ab_in")])])

    pool_w3, pool_scale3 = pool_w[0], pool_scale[0].reshape(4, 1, TILE)
    sgu_w3 = sgu_w[0]
    sgu_w3_t = jnp.swapaxes(sgu_w3, 1, 2)
    sgu_bias_tile = jnp.broadcast_to(sgu_b[0][:, :, None], (4, TILE, TILE))
    conv_b2 = conv_b

    def place_quarter(v):
        return lax.dynamic_update_slice(jnp.zeros((v.shape[0], 4 * TILE), F32), v, (0, q_idx * TILE))

    sharded_small = jnp.concatenate(
        [place_quarter(conv_w[0]), place_quarter(sgu_norm_g), place_quarter(sgu_norm_b),
         jnp.zeros((3, 4 * TILE), F32)], axis=0)
    sharded_small, = _allreduce_small([sharded_small])
    sharded_small = sharded_small * 0.5
    conv_w_full = sharded_small[0:3]
    sgu_g_full = sharded_small[3:4]
    sgu_b_full = sharded_small[4:5]

    xn0 = _rms_fwd(h0, mix_norm_g[0:1], name="rms_fwd_mix0")
    p_ab, ((w_1_0,),) = _mm_nn(xn0, w_ab_in, 0, out_dtype=BF16, name="ab_in_proj",
                               rider=[("gather", [buf_w1_0])])
    mix0, ((w_ab_out,),) = _ab_fwd(p_ab, pool_w3, pool_scale3, conv_w_full, conv_b2, nseq, t_len,
                                   rider=[("gather", [buf_ab_out])])
    w_ab_out = row_block(w_ab_out)
    h1, hn0 = _mm_nn(mix0, w_ab_out, 0, out_dtype=F32, name="ab_out_proj", epilogue="residual", extra=h0,
                     norm_g=mlp_norm_g[0:1])
    (act0, relu0), ((w_2_0,),) = _mm_nn(hn0, w_1_0, 0, out_dtype=BF16, name="mlp0_up", epilogue="relu2",
                                        rider=[("gather", [buf_w2_0])])
    w_2_0 = row_block(w_2_0)
    (h2, xn1), ((w_cd_in,),) = _mm_nn(act0, w_2_0, 0, out_dtype=F32, name="mlp0_down", epilogue="residual", extra=h1,
                                      norm_g=mix_norm_g[1:2],
                                      rider=[("gather", [buf_cd_in])])

    p_cd = _mm_nn(xn1, w_cd_in, 0, out_dtype=BF16, name="cd_in_proj")
    mix1 = _sgu_fwd(p_cd, sgu_g_full, sgu_b_full, sgu_w3, sgu_bias_tile)
    mix1, ltot, (w_cd_out, w_1_1, w_2_1) = _sb_fwd(p_cd, mix1, nseq, t_len, later_weights)
    w_cd_out, w_2_1 = row_block(w_cd_out), row_block(w_2_1)
    h3, hn1 = _mm_nn(mix1, w_cd_out, 0, out_dtype=F32, name="cd_out_proj", epilogue="residual", extra=h2,
                     norm_g=mlp_norm_g[1:2])
    act1, relu1 = _mm_nn(hn1, w_1_1, 0, out_dtype=BF16, name="mlp1_up", epilogue="relu2")

    dh4, dh4_bf, dg_final, loss_tile = _mlp_down_loss(act1, w_2_1, h3, final_norm_g.reshape(1, d), target)

    def as_pieces(g):
        return g.reshape(1, N_CHIP, -1, g.shape[-1]) if g.shape[1] == 1 else g

    dz1 = _mm_nt(dh4_bf, w_2_1, 0, out_dtype=BF16, name="mlp1_down_bwd", epilogue="relu2_bwd", extra=relu1)
    g_w2_1 = as_pieces(_mm_tn(act1, dh4_bf, 1, name="mlp1_down_wgrad"))
    g_w1_1 = _mm_tn(hn1, dz1, N_CHIP, name="mlp1_up_wgrad")
    (dh3, dh3_bf, dg_mlp1), (got_a,) = _mm_nt(
        dz1, w_1_1, 0, out_dtype=F32, name="mlp1_up_bwd", epilogue="rms_bwd",
        extra=(h3, mlp_norm_g[1:2], dh4), rider=[("swap", [g_w1_1, g_w2_1])])

    g_cd_out = as_pieces(_mm_tn(mix1, dh3_bf, 1, name="cd_out_wgrad"))
    dmix1, (got_cd_out,) = _mm_nt(dh3_bf, w_cd_out, 0, out_dtype=BF16, name="cd_out_bwd",
                                  rider=[("swap", [g_cd_out])])
    sums_a = _pair_sums([g_w1_1, g_w2_1, g_cd_out], got_a + got_cd_out, pos, "a")
    du, dv, dsgu_w, dsgu_bs, dsgu_g, dsgu_b = _sgu_bwd(p_cd, dmix1, sgu_g_full, sgu_b_full, sgu_w3, sgu_w3_t,
                                                      sgu_bias_tile)
    dq, dk, dvv, landed_a = _sb_bwd(p_cd, dmix1, ltot, nseq, t_len, sums_a)
    halves_a = _chip_sums(sums_a, landed_a, pos, "a")
    dp_cd = jnp.concatenate([du, dv, dq, dk, dvv], axis=1)
    g_cd_in, ((r_w1_1, r_w2_1, r_cd_out),) = _mm_tn(xn1, dp_cd, N_CHIP, name="cd_in_wgrad",
                                                    rider=[("join", halves_a)])
    (dh2, dh2_bf, dg_mix1), (got_c,) = _mm_nt(
        dp_cd, w_cd_in, 0, out_dtype=F32, name="cd_in_bwd", epilogue="rms_bwd",
        extra=(h2, mix_norm_g[1:2], dh3), rider=[("swap", [g_cd_in])])

    sums_c = _pair_sums([g_cd_in], got_c, pos, "c")
    dz0, (landed_c,) = _mm_nt(dh2_bf, w_2_0, 0, out_dtype=BF16, name="mlp0_down_bwd", epilogue="relu2_bwd",
                              extra=relu0, rider=[("exchange", sums_c)])
    halves_c = _chip_sums(sums_c, landed_c, pos, "c")
    sgu_pack = jnp.concatenate([dsgu_w.reshape(4 * TILE, TILE), _pad_rows(dsgu_bs[:, :, 0]), loss_tile], axis=0)
    g_w2_0, ((r_cd_in,), (sgu_pack,)) = _mm_tn(act0, dh2_bf, 1, name="mlp0_down_wgrad",
                                               rider=[("join", halves_c), ("allgather", [sgu_pack])])
    g_w2_0 = as_pieces(g_w2_0)
    g_w1_0, (got_d,) = _mm_tn(hn0, dz0, N_CHIP, name="mlp0_up_wgrad", rider=[("swap", [g_w2_0])])
    sums_d = _pair_sums([g_w2_0], got_d, pos, "d")
    (dh1, dh1_bf, dg_mlp0), (landed_d, got_e) = _mm_nt(
        dz0, w_1_0, 0, out_dtype=F32, name="mlp0_up_bwd", epilogue="rms_bwd",
        extra=(h1, mlp_norm_g[0:1], dh2), rider=[("exchange", sums_d), ("swap", [g_w1_0])])
    halves_d = _chip_sums(sums_d, landed_d, pos, "d")
    sums_e = _pair_sums([g_w1_0], got_e, pos, "e")

    dmix0, ((r_w2_0,),) = _mm_nt(dh1_bf, w_ab_out, 0, out_dtype=BF16, name="ab_out_bwd", rider=[("join", halves_d)])
    g_ab_out = as_pieces(_mm_tn(mix0, dh1_bf, 1, name="ab_out_wgrad"))
    (da, dxb, dgb, dgc, dpool_w, dpool_scale, dconv_w, dconv_b), (landed_e, got_f) = _ab_bwd(
        p_ab, dmix0, pool_w3, pool_scale3, conv_w_full, conv_b2, nseq, t_len,
        rider=[("exchange", sums_e), ("swap", [g_ab_out])])
    halves_e = _chip_sums(sums_e, landed_e, pos, "e")
    sums_f = _pair_sums([g_ab_out], got_f, pos, "f")
    dp_ab = jnp.concatenate([da, dxb, dgb, dgc], axis=1)
    wide = jnp.concatenate([_pad_rows(jnp.concatenate([jnp.zeros_like(dg_mix1), dg_mix1], axis=0)),
                            _pad_rows(jnp.concatenate([dg_mlp0, dg_mlp1], axis=0)), _pad_rows(dg_final)], axis=0)
    mid = jnp.concatenate([_pad_rows(dconv_b), _pad_rows(dconv_w), _pad_rows(dsgu_g), _pad_rows(dsgu_b)], axis=0)
    pool_pack = jnp.concatenate([dpool_w.reshape(4 * TILE, TILE), _pad_rows(dpool_scale.reshape(4, TILE))], axis=0)
    g_ab_in, (landed_f, (r_w1_0,), (wide, mid, pool_pack)) = _mm_tn(
        xn0, dp_ab, N_CHIP, name="ab_in_wgrad",
        rider=[("exchange", sums_f), ("join", halves_e), ("allgather", [wide, mid, pool_pack])])
    halves_f = _chip_sums(sums_f, landed_f, pos, "f")
    sums_g = _pair_sums([g_ab_in], _swap_halves([g_ab_in], name="swap_halves_g"), pos, "g")
    (grad_x, _, dg_mix0), (landed_g, (r_ab_out,)) = _mm_nt(
        dp_ab, w_ab_in, 0, out_dtype=F32, name="ab_in_bwd", epilogue="rms_bwd",
        extra=(h0, mix_norm_g[0:1], dh1), rider=[("exchange", sums_g), ("join", halves_f)])
    r_ab_in, = _join_halves(_chip_sums(sums_g, landed_g, pos, "g"), name="join_halves_g")

    big_out = {
        "ab_w_in": _adam_big(ab_w_in, m_ab_w_in, v_ab_w_in, [r_ab_in], name="adam_ab_w_in"),
        "ab_w_out": _adam_big(ab_w_out, m_ab_w_out, v_ab_w_out, [r_ab_out], name="adam_ab_w_out"),
        "cd_w_in": _adam_big(cd_w_in, m_cd_w_in, v_cd_w_in, [r_cd_in], name="adam_cd_w_in"),
        "cd_w_out": _adam_big(cd_w_out, m_cd_w_out, v_cd_w_out, [r_cd_out], name="adam_cd_w_out"),
        "mlp_w1": _adam_big(mlp_w1, m_mlp_w1, v_mlp_w1, [r_w1_0, r_w1_1], name="adam_mlp_w1"),
        "mlp_w2": _adam_big(mlp_w2, m_mlp_w2, v_mlp_w2, [r_w2_0, r_w2_1], name="adam_mlp_w2"),
    }

    late, = _allreduce_small([_pad_rows(dg_mix0)])
    small_out, loss_sum = _adam_small(wide, mid, sgu_pack, pool_pack, late, {
        "mix_norm_g": (mix_norm_g, m_mix_norm_g, v_mix_norm_g),
        "mlp_norm_g": (mlp_norm_g, m_mlp_norm_g, v_mlp_norm_g),
        "final_norm_g": tuple(a.reshape(1, d) for a in (final_norm_g, m_final_norm_g, v_final_norm_g)),
        "conv_b": (conv_b, m_conv_b, v_conv_b),
        "conv_w": (conv_w, m_conv_w, v_conv_w),
        "sgu_norm_g": (sgu_norm_g, m_sgu_norm_g, v_sgu_norm_g),
        "sgu_norm_b": (sgu_norm_b, m_sgu_norm_b, v_sgu_norm_b),
        "pool_w": (pool_w, m_pool_w, v_pool_w),
        "pool_scale": (pool_scale, m_pool_scale, v_pool_scale),
        "sgu_w": (sgu_w, m_sgu_w, v_sgu_w),
        "sgu_b": (sgu_b, m_sgu_b, v_sgu_b),
    })
    small_out["final_norm_g"] = [a.reshape(d) for a in small_out["final_norm_g"]]

    order = ["mix_norm_g", "mlp_norm_g", "ab_w_in", "pool_w", "pool_scale", "conv_w", "conv_b", "ab_w_out",
             "cd_w_in", "sgu_norm_g", "sgu_norm_b", "sgu_w", "sgu_b", "cd_w_out", "mlp_w1", "mlp_w2",
             "final_norm_g"]
    both = {**big_out, **small_out}
    loss = loss_sum[0, 0]
    outs = [loss, grad_x.reshape(nseq, t_len, d)]
    for kind in range(4):
        outs += [both[name][kind] for name in order]
    return tuple(outs)
```
